```python
import jax, jax.numpy as jnp
from jax import lax
import numpy as np

D_MODEL = 1024
BATCH = 8
SEQ = 8192
DEPTH = 1

GLA_HEADS = 4
GLA_DK = D_MODEL // 16
GLA_DV = D_MODEL // 8
GLA_WIDTH = GLA_HEADS * GLA_DV
GLA_GATE_RANK = 16
GLA_GATE_TAU = 16.0
GLA_CHUNK = 64
SWA_Q_HEADS = 8
SWA_KV_HEADS = 2
SWA_HEAD_DIM = D_MODEL // 16
SWA_WIDTH = SWA_Q_HEADS * SWA_HEAD_DIM
WINDOW = 128
SWA_BLOCK = 128
ROPE_THETA = 10000.0
RMS_EPS = 1e-6
D_MIX = GLA_WIDTH + SWA_WIDTH
COL_SIZES = (
    GLA_HEADS * GLA_DK,
    GLA_HEADS * GLA_DK,
    GLA_WIDTH,
    GLA_GATE_RANK,
    GLA_WIDTH,
    SWA_WIDTH,
    SWA_KV_HEADS * SWA_HEAD_DIM,
    SWA_KV_HEADS * SWA_HEAD_DIM,
    SWA_WIDTH,
)
D_IN = (4 * GLA_HEADS * GLA_DK // 2) + 2 * GLA_WIDTH + GLA_GATE_RANK + 2 * SWA_WIDTH + 2 * SWA_KV_HEADS * SWA_HEAD_DIM

kernel_name = "hymba_gla_swa_sink_adaln"


def rmsnorm(x, g):
    xf = x.astype(jnp.float32)
    y = xf * lax.rsqrt(jnp.mean(xf * xf, axis=-1, keepdims=True) + RMS_EPS)
    return (y * g.astype(jnp.float32)).astype(x.dtype)


def split_cols(t, sizes):
    outs = []
    start = 0
    for s in sizes:
        outs.append(t[..., start:start + s])
        start += s
    return outs


def rope(t, positions):
    hd = t.shape[-1]
    inv_freq = 1.0 / (ROPE_THETA ** (jnp.arange(0, hd, 2, dtype=jnp.float32) / hd))
    ang = positions.astype(jnp.float32)[..., None] * inv_freq
    cos = jnp.cos(ang)[:, :, None, :]
    sin = jnp.sin(ang)[:, :, None, :]
    tf = t.astype(jnp.float32)
    t1, t2 = tf[..., : hd // 2], tf[..., hd // 2:]
    return jnp.concatenate([t1 * cos - t2 * sin, t2 * cos + t1 * sin], axis=-1).astype(t.dtype)


def gla_chunked(q, k, v, log_a):
    B, S, H, dk = q.shape
    dv = v.shape[-1]
    C = GLA_CHUNK
    N = S // C

    def chunks(t):
        return t.reshape(B, N, C, H, t.shape[-1]).transpose(0, 3, 1, 2, 4).astype(jnp.float32)

    qc = chunks(q) * (dk ** -0.5)
    kc = chunks(k)
    vc = chunks(v)
    b = jnp.cumsum(chunks(log_a), axis=3)
    b_last = b[:, :, :, -1:, :]
    q_d = qc * jnp.exp(b)
    k_d = kc * jnp.exp(-b)
    k_tail = kc * jnp.exp(b_last - b)
    causal = jnp.tril(jnp.ones((C, C), dtype=bool))
    scores = jnp.einsum('bhnid,bhnjd->bhnij', q_d, k_d)
    scores = jnp.where(causal, scores, 0.0)
    o_intra = jnp.einsum('bhnij,bhnjv->bhniv', scores, vc)
    u = jnp.einsum('bhncd,bhncv->bhndv', k_tail, vc)
    decay = jnp.exp(b_last[:, :, :, 0, :])

    def step(state, inp):
        dec, un = inp
        return dec[..., None] * state + un, state

    state0 = jnp.zeros((B, H, dk, dv), jnp.float32)
    _, s_prev = lax.scan(step, state0, (decay.transpose(2, 0, 1, 3), u.transpose(2, 0, 1, 3, 4)))
    s_prev = s_prev.transpose(1, 2, 0, 3, 4)
    o = o_intra + jnp.einsum('bhncd,bhndv->bhncv', q_d, s_prev)
    return o.transpose(0, 2, 3, 1, 4).reshape(B, S, H, dv)


def sliding_window_sink_attention(q, k, v, sinks):
    B, S, Hq, hd = q.shape
    Hkv = k.shape[2]
    G = Hq // Hkv
    L = SWA_BLOCK
    N = S // L
    qb = q.reshape(B, N, L, Hkv, G, hd)

    def band(t):
        tb = t.reshape(B, N, L, Hkv, hd)
        prev = jnp.concatenate([jnp.zeros_like(tb[:, :1]), tb[:, :-1]], axis=1)
        return jnp.concatenate([prev, tb], axis=2)

    kb = band(k)
    vb = band(v)
    scores = jnp.einsum('bnqhgd,bnkhd->bhgnqk', qb, kb).astype(jnp.float32) * (hd ** -0.5)
    qi = jnp.arange(L)[None, :, None]
    kj = jnp.arange(2 * L)[None, None, :]
    blk = jnp.arange(N)[:, None, None]
    dist = qi + L - kj
    valid = (dist >= 0) & (dist < WINDOW) & (blk * L + kj - L >= 0)
    scores = jnp.where(valid, scores, -jnp.inf)
    sink = jnp.broadcast_to(sinks.astype(jnp.float32).reshape(Hkv, G, 1, 1, 1), scores.shape[:-1] + (1,))
    probs = jax.nn.softmax(jnp.concatenate([scores, sink], axis=-1), axis=-1)[..., :-1]
    o = jnp.einsum('bhgnqk,bnkhd->bnqhgd', probs.astype(v.dtype), vb)
    return o.reshape(B, S, Hq * hd)


def _fwd_setup_inputs(seed: int = 0) -> dict:
    key = jax.random.key(seed)
    ks = jax.random.split(key, 16)
    f32 = jnp.float32
    x = jax.random.normal(ks[0], (BATCH, SEQ, D_MODEL), f32)
    c = jax.random.normal(ks[1], (BATCH, D_MODEL), f32)
    offsets = jax.random.randint(ks[2], (BATCH, 1), 0, 4096, dtype=jnp.int32)
    positions = jnp.arange(SEQ, dtype=jnp.int32)[None, :] + offsets
    w_ada = jax.random.normal(ks[3], (DEPTH, D_MODEL, 3 * D_MODEL), f32) * (0.5 * D_MODEL ** -0.5)
    gate_bias = jnp.concatenate([jnp.zeros((2 * D_MODEL,), f32), jnp.ones((D_MODEL,), f32)])
    b_ada = gate_bias[None, :] + 0.02 * jax.random.normal(ks[4], (DEPTH, 3 * D_MODEL), f32)
    g_norm = 1.0 + 0.02 * jax.random.normal(ks[5], (DEPTH, D_MODEL), f32)
    w_in = jax.random.normal(ks[6], (DEPTH, D_MODEL, D_IN), f32) * (D_MODEL ** -0.5)
    w_decay = jax.random.normal(ks[7], (DEPTH, GLA_GATE_RANK, GLA_HEADS * GLA_DK), f32) * (GLA_GATE_RANK ** -0.5)
    b_decay = 0.1 * jax.random.normal(ks[8], (DEPTH, GLA_HEADS * GLA_DK), f32)
    g_gla_head = 1.0 + 0.02 * jax.random.normal(ks[9], (DEPTH, GLA_WIDTH), f32)
    sinks = 0.5 * jax.random.normal(ks[10], (DEPTH, SWA_Q_HEADS), f32)
    w_out = jax.random.normal(ks[11], (DEPTH, D_MIX, D_MODEL), f32) * (D_MIX ** -0.5)
    g_final = 1.0 + 0.02 * jax.random.normal(ks[12], (D_MODEL,), f32)
    return {"x": x, "c": c, "positions": positions, "w_ada": w_ada, "b_ada": b_ada,
            "g_norm": g_norm, "w_in": w_in, "w_decay": w_decay, "b_decay": b_decay,
            "g_gla_head": g_gla_head, "sinks": sinks, "w_out": w_out, "g_final": g_final}


def _fwd_reference(x, c, positions, w_ada, b_ada, g_norm, w_in, w_decay, b_decay, g_gla_head, sinks, w_out, g_final):
    B, S, _ = x.shape
    for l in range(DEPTH):
        mod = jnp.dot(jax.nn.silu(c.astype(jnp.float32)), w_ada[l].astype(jnp.float32)) + b_ada[l].astype(jnp.float32)
        shift, scale, gate = jnp.split(mod, 3, axis=-1)
        h = (rmsnorm(x, g_norm[l]).astype(jnp.float32) * (1.0 + scale[:, None, :]) + shift[:, None, :]).astype(x.dtype)
        proj = jnp.einsum('bsd,de->bse', h, w_in[l])
        gq, gk, gv, ga, gz, sq, sk, sv, sz = split_cols(proj, COL_SIZES)
        z = jnp.einsum('bsr,rk->bsk', ga, w_decay[l]) + b_decay[l]
        log_a = jax.nn.log_sigmoid(z.astype(jnp.float32)) / GLA_GATE_TAU
        o_gla = gla_chunked(gq.reshape(B, S, GLA_HEADS, GLA_DK), gk.reshape(B, S, GLA_HEADS, GLA_DK),
                            gv.reshape(B, S, GLA_HEADS, GLA_DV), log_a.reshape(B, S, GLA_HEADS, GLA_DK))
        o_gla = rmsnorm(o_gla, g_gla_head[l].reshape(GLA_HEADS, GLA_DV)).reshape(B, S, GLA_WIDTH)
        o_gla = (o_gla * jax.nn.silu(gz.astype(jnp.float32))).astype(x.dtype)
        q = rope(sq.reshape(B, S, SWA_Q_HEADS, SWA_HEAD_DIM), positions)
        k = rope(sk.reshape(B, S, SWA_KV_HEADS, SWA_HEAD_DIM), positions)
        v = sv.reshape(B, S, SWA_KV_HEADS, SWA_HEAD_DIM)
        o_swa = sliding_window_sink_attention(q, k, v, sinks[l])
        o_swa = (o_swa.astype(jnp.float32) * jax.nn.silu(sz.astype(jnp.float32))).astype(x.dtype)
        y = jnp.einsum('bse,ed->bsd', jnp.concatenate([o_gla, o_swa], axis=-1), w_out[l])
        x = (x.astype(jnp.float32) + gate[:, None, :] * y.astype(jnp.float32)).astype(x.dtype)
    return rmsnorm(x, g_final)


import jax as _jax
import jax.numpy as _jnp

TWIN_FORMAT = 'train_step'
FWD_PARAMS = ['x', 'c', 'positions', 'w_ada', 'b_ada', 'g_norm', 'w_in', 'w_decay', 'b_decay', 'g_gla_head', 'sinks', 'w_out', 'g_final']
TWIN_WEIGHTS = ['w_ada', 'b_ada', 'g_norm', 'w_in', 'w_decay', 'b_decay', 'g_gla_head', 'sinks', 'w_out', 'g_final']
TWIN_DIFF_INPUT = 'x'
TWIN_INPUTS = ['x', 'c', 'positions', 'w_ada', 'b_ada', 'g_norm', 'w_in', 'w_decay', 'b_decay', 'g_gla_head', 'sinks', 'w_out', 'g_final', 'loss_target', 'm_w_ada', 'm_b_ada', 'm_g_norm', 'm_w_in', 'm_w_decay', 'm_b_decay', 'm_g_gla_head', 'm_sinks', 'm_w_out', 'm_g_final', 'v_w_ada', 'v_b_ada', 'v_g_norm', 'v_w_in', 'v_w_decay', 'v_b_decay', 'v_g_gla_head', 'v_sinks', 'v_w_out', 'v_g_final']
TWIN_OUTPUTS = ['loss', 'grad_x', 'grad_w_ada', 'grad_b_ada', 'grad_g_norm', 'grad_w_in', 'grad_w_decay', 'grad_b_decay', 'grad_g_gla_head', 'grad_sinks', 'grad_w_out', 'grad_g_final', 'delta_w_ada', 'delta_b_ada', 'delta_g_norm', 'delta_w_in', 'delta_w_decay', 'delta_b_decay', 'delta_g_gla_head', 'delta_sinks', 'delta_w_out', 'delta_g_final', 'new_m_w_ada', 'new_m_b_ada', 'new_m_g_norm', 'new_m_w_in', 'new_m_w_decay', 'new_m_b_decay', 'new_m_g_gla_head', 'new_m_sinks', 'new_m_w_out', 'new_m_g_final', 'new_v_w_ada', 'new_v_b_ada', 'new_v_g_norm', 'new_v_w_in', 'new_v_w_decay', 'new_v_b_decay', 'new_v_g_gla_head', 'new_v_sinks', 'new_v_w_out', 'new_v_g_final']
TWIN_LEAF_KINDS = {'loss': 'loss', 'grad_x': 'grad_x', 'grad_w_ada': 'grad_w', 'grad_b_ada': 'grad_w', 'grad_g_norm': 'grad_w', 'grad_w_in': 'grad_w', 'grad_w_decay': 'grad_w', 'grad_b_decay': 'grad_w', 'grad_g_gla_head': 'grad_w', 'grad_sinks': 'grad_w', 'grad_w_out': 'grad_w', 'grad_g_final': 'grad_w', 'delta_w_ada': 'delta_w', 'delta_b_ada': 'delta_w', 'delta_g_norm': 'delta_w', 'delta_w_in': 'delta_w', 'delta_w_decay': 'delta_w', 'delta_b_decay': 'delta_w', 'delta_g_gla_head': 'delta_w', 'delta_sinks': 'delta_w', 'delta_w_out': 'delta_w', 'delta_g_final': 'delta_w', 'new_m_w_ada': 'new_m', 'new_m_b_ada': 'new_m', 'new_m_g_norm': 'new_m', 'new_m_w_in': 'new_m', 'new_m_w_decay': 'new_m', 'new_m_b_decay': 'new_m', 'new_m_g_gla_head': 'new_m', 'new_m_sinks': 'new_m', 'new_m_w_out': 'new_m', 'new_m_g_final': 'new_m', 'new_v_w_ada': 'new_v', 'new_v_b_ada': 'new_v', 'new_v_g_norm': 'new_v', 'new_v_w_in': 'new_v', 'new_v_w_decay': 'new_v', 'new_v_b_decay': 'new_v', 'new_v_g_gla_head': 'new_v', 'new_v_sinks': 'new_v', 'new_v_w_out': 'new_v', 'new_v_g_final': 'new_v'}


def _forward(args):
    return _fwd_reference(*[args[k] for k in FWD_PARAMS])


def _output_shape():
    def fwd():
        inp = _fwd_setup_inputs(0)
        return _fwd_reference(*[inp[k] for k in FWD_PARAMS])
    out = _jax.eval_shape(fwd)
    return out.shape, out.dtype

N_MICROBATCH = 1
ADAM_LR = 0.001
ADAM_B1 = 0.9
ADAM_B2 = 0.999
ADAM_EPS = 1e-08
ADAM_WD = 0.01
ADAM_STEP = 10
PER_EXAMPLE_BATCH_AXIS = {'x': 0, 'c': 0, 'positions': 0, 'loss_target': 0}
SHARED_INPUTS = []
_WEIGHT_DTYPES = {'w_ada': _jnp.float32, 'b_ada': _jnp.float32, 'g_norm': _jnp.float32, 'w_in': _jnp.float32, 'w_decay': _jnp.float32, 'b_decay': _jnp.float32, 'g_gla_head': _jnp.float32, 'sinks': _jnp.float32, 'w_out': _jnp.float32, 'g_final': _jnp.float32}
MOMENT_SCALE = {'w_ada': 1.455117e-01, 'b_ada': 2.824459e-01, 'g_norm': 2.271169e-01, 'w_in': 1.418056e-01, 'w_decay': 4.020214e-02, 'b_decay': 1.074067e-01, 'g_gla_head': 1.573503e-01, 'sinks': 3.067082e-02, 'w_out': 1.154412e-01, 'g_final': 6.413386e+01}


def _to_microbatches(a, axis):
    t = _jnp.moveaxis(a, axis, 0)
    t = t.reshape((N_MICROBATCH, t.shape[0] // N_MICROBATCH) + t.shape[1:])
    return _jnp.moveaxis(t, 1, axis + 1)


def setup_inputs(seed: int = 0) -> dict:
    inp = _fwd_setup_inputs(seed)
    key = _jax.random.fold_in(_jax.random.key(seed), 7919)
    shape, _ = _output_shape()
    out = dict(inp)
    out["loss_target"] = _jax.random.normal(_jax.random.fold_in(key, 0), shape, _jnp.float32)
    for i, name in enumerate(TWIN_WEIGHTS):
        w = inp[name].astype(_jnp.float32)
        if MOMENT_SCALE is None:
            s = _jnp.sqrt(_jnp.mean(_jnp.square(w)) + 1e-30)
        else:
            s = MOMENT_SCALE[name]
        km, kv = _jax.random.split(_jax.random.fold_in(key, i + 1))
        out[name] = w
        out["m_" + name] = s * _jax.random.normal(km, w.shape, _jnp.float32)
        out["v_" + name] = (s * s) * _jax.random.uniform(kv, w.shape, _jnp.float32, 0.5, 1.5)
    if N_MICROBATCH > 1:
        for name, axis in PER_EXAMPLE_BATCH_AXIS.items():
            out[name] = _to_microbatches(out[name], axis)
    return {'x': out['x'], 'c': out['c'], 'positions': out['positions'], 'w_ada': out['w_ada'], 'b_ada': out['b_ada'], 'g_norm': out['g_norm'], 'w_in': out['w_in'], 'w_decay': out['w_decay'], 'b_decay': out['b_decay'], 'g_gla_head': out['g_gla_head'], 'sinks': out['sinks'], 'w_out': out['w_out'], 'g_final': out['g_final'], 'loss_target': out['loss_target'], 'm_w_ada': out['m_w_ada'], 'm_b_ada': out['m_b_ada'], 'm_g_norm': out['m_g_norm'], 'm_w_in': out['m_w_in'], 'm_w_decay': out['m_w_decay'], 'm_b_decay': out['m_b_decay'], 'm_g_gla_head': out['m_g_gla_head'], 'm_sinks': out['m_sinks'], 'm_w_out': out['m_w_out'], 'm_g_final': out['m_g_final'], 'v_w_ada': out['v_w_ada'], 'v_b_ada': out['v_b_ada'], 'v_g_norm': out['v_g_norm'], 'v_w_in': out['v_w_in'], 'v_w_decay': out['v_w_decay'], 'v_b_decay': out['v_b_decay'], 'v_g_gla_head': out['v_g_gla_head'], 'v_sinks': out['v_sinks'], 'v_w_out': out['v_w_out'], 'v_g_final': out['v_g_final']}


def _loss(weights, diff, rest, loss_target):
    with _jax.named_scope("forward"):
        args = {**rest, TWIN_DIFF_INPUT: diff, **{k: w.astype(_WEIGHT_DTYPES[k]) for k, w in weights.items()}}
        y = _forward(args)
    with _jax.named_scope("loss_head"):
        err = _jnp.square(y.astype(_jnp.float32) - loss_target)
        return 0.5 * _jnp.sum(_jnp.mean(err, axis=-1)) if err.ndim else 0.5 * err


def _adamw(w, g, m, v):
    m = ADAM_B1 * m + (1.0 - ADAM_B1) * g
    v = ADAM_B2 * v + (1.0 - ADAM_B2) * _jnp.square(g)
    m_hat = m / (1.0 - ADAM_B1 ** ADAM_STEP)
    v_hat = v / (1.0 - ADAM_B2 ** ADAM_STEP)
    delta = -ADAM_LR * (m_hat / (_jnp.sqrt(v_hat) + ADAM_EPS) + ADAM_WD * w)
    return delta, m, v


def reference(x, c, positions, w_ada, b_ada, g_norm, w_in, w_decay, b_decay, g_gla_head, sinks, w_out, g_final, loss_target, m_w_ada, m_b_ada, m_g_norm, m_w_in, m_w_decay, m_b_decay, m_g_gla_head, m_sinks, m_w_out, m_g_final, v_w_ada, v_b_ada, v_g_norm, v_w_in, v_w_decay, v_b_decay, v_g_gla_head, v_sinks, v_w_out, v_g_final):
    given = dict(x=x, c=c, positions=positions, w_ada=w_ada, b_ada=b_ada, g_norm=g_norm, w_in=w_in, w_decay=w_decay, b_decay=b_decay, g_gla_head=g_gla_head, sinks=sinks, w_out=w_out, g_final=g_final, loss_target=loss_target, m_w_ada=m_w_ada, m_b_ada=m_b_ada, m_g_norm=m_g_norm, m_w_in=m_w_in, m_w_decay=m_w_decay, m_b_decay=m_b_decay, m_g_gla_head=m_g_gla_head, m_sinks=m_sinks, m_w_out=m_w_out, m_g_final=m_g_final, v_w_ada=v_w_ada, v_b_ada=v_b_ada, v_g_norm=v_g_norm, v_w_in=v_w_in, v_w_decay=v_w_decay, v_b_decay=v_b_decay, v_g_gla_head=v_g_gla_head, v_sinks=v_sinks, v_w_out=v_w_out, v_g_final=v_g_final)
    weights = {n: given[n] for n in TWIN_WEIGHTS}
    shared = {n: given[n] for n in SHARED_INPUTS}
    per_example = {n: given[n] for n in ['x', 'c', 'positions']}
    grad_fn = _jax.value_and_grad(_loss, argnums=(0, 1))

    def one_microbatch(ex, loss_target):
        ex = dict(ex)
        diff = ex.pop(TWIN_DIFF_INPUT)
        return grad_fn(weights, diff, {**shared, **ex}, loss_target)

    if N_MICROBATCH == 1:
        loss, (grad_w, grad_x) = one_microbatch(per_example, given["loss_target"])
    else:
        def body(carry, xs):
            loss_sum, grad_sum = carry
            l_k, (gw_k, gx_k) = one_microbatch(xs[0], xs[1])
            with _jax.named_scope("update"):
                return (loss_sum + l_k, _jax.tree.map(_jnp.add, grad_sum, gw_k)), gx_k

        init = (_jnp.zeros((), _jnp.float32), _jax.tree.map(_jnp.zeros_like, weights))
        (loss, grad_w), grad_x = _jax.lax.scan(body, init, (per_example, given["loss_target"]))
    with _jax.named_scope("update"):
        delta_w, new_m, new_v = {}, {}, {}
        for n in TWIN_WEIGHTS:
            delta_w[n], new_m[n], new_v[n] = _adamw(weights[n], grad_w[n], given["m_" + n], given["v_" + n])
    return (loss, grad_x, *[grad_w[n] for n in TWIN_WEIGHTS], *[delta_w[n] for n in TWIN_WEIGHTS],
            *[new_m[n] for n in TWIN_WEIGHTS], *[new_v[n] for n in TWIN_WEIGHTS])
```

```python
import functools

import jax
import jax.numpy as jnp
from jax import lax
from jax.experimental import pallas as pl
from jax.experimental.pallas import tpu as pltpu

F32 = jnp.float32
BF = jnp.bfloat16

D_MODEL = 1024
GLA_HEADS = 4
GLA_DK = 64
GLA_DV = 128
GLA_CHUNK = 64
GLA_RANK = 16
GLA_TAU = 16.0
SWA_HEADS = 8
SWA_BLOCK = 128
RMS_EPS = 1e-6
ROPE_THETA = 10000.0
D_IN = 2832

OFF_QK, OFF_V, OFF_GZ, OFF_SQ, OFF_SZ, OFF_SK, OFF_SV, OFF_GA = 0, 512, 1024, 1536, 2048, 2560, 2688, 2816
D_PAD = 2944
LANES = 128
VMEM_LIMIT = 56 * 1024 * 1024

ADAM_LR, ADAM_B1, ADAM_B2, ADAM_EPS, ADAM_WD, ADAM_STEP = 0.001, 0.9, 0.999, 1e-08, 0.01, 10

NT = (((1,), (1,)), ((), ()))
TN = (((0,), (0,)), ((), ()))
MESH = pl.DeviceIdType.MESH


def _dot(a, b, dims=None):
    if dims is None:
        return jnp.dot(a, b, preferred_element_type=F32)
    return lax.dot_general(a, b, dims, preferred_element_type=F32)


def _sigmoid(x):
    return 1.0 / (1.0 + jnp.exp(-x))


def _params(sem=None):
    return pltpu.CompilerParams(dimension_semantics=sem, vmem_limit_bytes=VMEM_LIMIT)


def _full(shape):
    return pl.BlockSpec(shape, lambda i: (0,) * len(shape))


def _all_gather(block, name):
    m_per, n = block.shape

    def body(x_ref, out_ref, send_sems, recv_sems, local_sem):
        x, y, c = lax.axis_index("x"), lax.axis_index("y"), lax.axis_index("c")
        me, sibling = (x, y, c), (x, y, 1 - c)
        chips = [(1 - x, y), (x, 1 - y), (1 - x, 1 - y)]

        def rows(px, py, pc):
            return out_ref.at[pl.ds((4 * px + 2 * py + pc) * m_per, m_per), :]

        def copy(k, blk, to, src=None):
            return pltpu.make_async_remote_copy(
                src_ref=rows(*blk) if src is None else src, dst_ref=rows(*blk),
                send_sem=send_sems.at[k], recv_sem=recv_sems.at[k], device_id=to, device_id_type=MESH)

        mine = pltpu.make_async_copy(x_ref, rows(*me), local_sem)
        mine.start()
        first = [copy(0, me, sibling, src=x_ref)]
        first += [copy(1 + j, me, (*chip, c), src=x_ref) for j, chip in enumerate(chips)]
        for cp in first:
            cp.start()
        passed = [copy(4 + j, (*chip, c), sibling) for j, chip in enumerate(chips)]
        for j, chip in enumerate(chips):
            copy(1 + j, (*chip, c), me).wait_recv()
            passed[j].start()
        copy(0, sibling, me).wait_recv()
        for j, chip in enumerate(chips):
            copy(4 + j, (*chip, 1 - c), me).wait_recv()
        for cp in first + passed:
            cp.wait_send()
        mine.wait()

    return pl.pallas_call(
        body, name=name,
        out_shape=jax.ShapeDtypeStruct((8 * m_per, n), block.dtype),
        in_specs=[pl.BlockSpec(memory_space=pltpu.VMEM)],
        out_specs=pl.BlockSpec(memory_space=pltpu.VMEM),
        scratch_shapes=[pltpu.SemaphoreType.DMA((7,)), pltpu.SemaphoreType.DMA((7,)), pltpu.SemaphoreType.DMA],
        compiler_params=pltpu.CompilerParams(vmem_limit_bytes=VMEM_LIMIT),
    )(block)


def _reduce_scatter(parts, name):
    _, rr, cc = parts.shape
    r2 = rr // 2

    def body(p_ref, out_ref, acc_ref, land_ref, send_sems, recv_sems):
        x, y, c = lax.axis_index("x"), lax.axis_index("y"), lax.axis_index("c")
        sibling = (x, y, 1 - c)
        chips = [(1 - x, y), (x, 1 - y), (1 - x, 1 - y)]
        mine = pl.ds(pl.multiple_of(c * r2, r2), r2)
        other = pl.ds(pl.multiple_of((1 - c) * r2, r2), r2)

        swap = pltpu.make_async_remote_copy(
            src_ref=p_ref.at[:, other, :], dst_ref=acc_ref, send_sem=send_sems.at[0], recv_sem=recv_sems.at[0],
            device_id=sibling, device_id_type=MESH)
        swap.start()
        swap.wait()
        for j in range(4):
            acc_ref[j] = acc_ref[j] + p_ref[j, mine, :]

        sends = []
        for k, (tx, ty) in enumerate(chips):
            cp = pltpu.make_async_remote_copy(
                src_ref=acc_ref.at[2 * tx + ty], dst_ref=land_ref.at[k], send_sem=send_sems.at[1 + k],
                recv_sem=recv_sems.at[1 + k], device_id=(tx, ty, c), device_id_type=MESH)
            cp.start()
            sends.append(cp)
        for cp in sends:
            cp.wait_recv()
        total = acc_ref[2 * x + y]
        for k in range(3):
            total = total + land_ref[k]
        out_ref[mine, :] = total
        for cp in sends:
            cp.wait_send()

        share = pltpu.make_async_remote_copy(
            src_ref=out_ref.at[mine, :], dst_ref=out_ref.at[mine, :], send_sem=send_sems.at[4],
            recv_sem=recv_sems.at[4], device_id=sibling, device_id_type=MESH)
        share.start()
        share.wait()

    return pl.pallas_call(
        body, name=name,
        out_shape=jax.ShapeDtypeStruct((rr, cc), F32),
        in_specs=[pl.BlockSpec(memory_space=pltpu.VMEM)],
        out_specs=pl.BlockSpec(memory_space=pltpu.VMEM),
        scratch_shapes=[pltpu.VMEM((4, r2, cc), F32), pltpu.VMEM((3, r2, cc), F32),
                        pltpu.SemaphoreType.DMA((5,)), pltpu.SemaphoreType.DMA((5,))],
        compiler_params=pltpu.CompilerParams(vmem_limit_bytes=VMEM_LIMIT),
    )(parts)


def _ada_fwd(c_all, w_ada, b_shard):
    def body(c_ref, w_ref, b_ref, o_ref):
        cv = c_ref[...]
        sc = (cv * _sigmoid(cv)).astype(BF)
        o_ref[...] = _dot(sc, w_ref[...].astype(BF)) + b_ref[...]

    return pl.pallas_call(
        body, name="ada_fwd", out_shape=jax.ShapeDtypeStruct((8, w_ada.shape[1]), F32),
        compiler_params=_params(),
    )(c_all, w_ada, b_shard)


def _rope_tables(pos_col, inv_freq):
    s = pos_col.shape[0]
    tm = min(1024, s)

    def body(p_ref, f_ref, cos_ref, sin_ref):
        ang = p_ref[...].astype(F32) * f_ref[...]
        lane = lax.broadcasted_iota(jnp.int32, ang.shape, 1)
        cos_ref[...] = jnp.cos(ang)
        sn = jnp.sin(ang)
        sin_ref[...] = jnp.where((lane % 64) < 32, -sn, sn)

    return pl.pallas_call(
        body, name="rope_tables", grid=(s // tm,),
        in_specs=[pl.BlockSpec((tm, 1), lambda i: (i, 0)), _full((1, LANES))],
        out_specs=[pl.BlockSpec((tm, LANES), lambda i: (i, 0))] * 2,
        out_shape=[jax.ShapeDtypeStruct((s, LANES), F32)] * 2,
        compiler_params=_params(("arbitrary",)),
    )(pos_col, inv_freq)


def _rope(t, cosb, sinb, first_half):
    partner = jnp.where(first_half, pltpu.roll(t, 96, 1), pltpu.roll(t, 32, 1))
    return t * cosb + partner * sinb


def _rope_t(g, cosb, sinb, first_half):
    gs = g * sinb
    partner = jnp.where(first_half, pltpu.roll(gs, 96, 1), pltpu.roll(gs, 32, 1))
    return g * cosb + partner


def _modnorm(x, g, sc1p, shift):
    r = lax.rsqrt(jnp.mean(x * x, axis=-1, keepdims=True) + RMS_EPS)
    xn = x * r
    return xn, r, (xn * g) * sc1p + shift


def _inproj_fwd(x2d, shift, sc1p, g_norm, wpad):
    s = x2d.shape[0]
    tm = min(512, s)

    def body(x_ref, sh_ref, sc_ref, g_ref, w_ref, o_ref):
        _, _, h = _modnorm(x_ref[...], g_ref[...], sc_ref[...], sh_ref[...])
        o_ref[...] = _dot(h.astype(BF), w_ref[...])

    vec = _full((1, D_MODEL))
    return pl.pallas_call(
        body, name="inproj_fwd", grid=(s // tm,),
        in_specs=[pl.BlockSpec((tm, D_MODEL), lambda i: (i, 0)), vec, vec, vec, _full((D_MODEL, D_PAD))],
        out_specs=pl.BlockSpec((tm, D_PAD), lambda i: (i, 0)),
        out_shape=jax.ShapeDtypeStruct((s, D_PAD), F32),
        compiler_params=_params(("arbitrary",)),
    )(x2d, shift, sc1p, g_norm, wpad)


def _split3(a):
    hi = a.astype(BF)
    r1 = a - hi.astype(F32)
    mid = r1.astype(BF)
    lo = (r1 - mid.astype(F32)).astype(BF)
    return hi, mid, lo


def _tri_matmul(tri, a):
    hi, mid, lo = _split3(a)
    return _dot(tri, hi) + _dot(tri, mid) + _dot(tri, lo)


def _gla_chunk_common(qk, ga, wd, bd, tril_b, sgn, qsc):
    z2 = _dot(ga.astype(BF), wd) + bd
    la2 = (jnp.minimum(z2, 0.0) - jnp.log1p(jnp.exp(-jnp.abs(z2)))) * (1.0 / GLA_TAU)
    b2 = _tri_matmul(tril_b, la2)
    bl = b2[GLA_CHUNK - 1:GLA_CHUNK, :]
    e = jnp.exp(b2 * sgn)
    f = jnp.exp(bl - b2)
    qkd = qk * e * qsc
    kt = qk * f
    dec = jnp.exp(bl)
    return z2, e, f, qkd, kt, dec


def _gla_masks():
    lane = lax.broadcasted_iota(jnp.int32, (GLA_CHUNK, 512), 1)
    lo = (lane % LANES) < GLA_DK
    sgn = jnp.where(lo, 1.0, -1.0).astype(F32)
    qsc = jnp.where(lo, GLA_DK ** -0.5, 1.0).astype(F32)
    row = lax.broadcasted_iota(jnp.int32, (GLA_CHUNK, GLA_CHUNK), 0)
    col = lax.broadcasted_iota(jnp.int32, (GLA_CHUNK, GLA_CHUNK), 1)
    tril = col <= row
    lo_h = lax.broadcasted_iota(jnp.int32, (GLA_CHUNK, LANES), 1) < GLA_DK
    return lo_h, sgn, qsc, tril


def _gla_fwd(proj, wdecp, bdecp, ggla):
    s = proj.shape[0]
    tb = min(256, s)
    nch = tb // GLA_CHUNK

    def body(qk_ref, v_ref, gz_ref, ga_ref, wd_ref, bd_ref, gg_ref, og_ref, opre_ref, sprev_ref, st_ref):
        @pl.when(pl.program_id(0) == 0)
        def _():
            st_ref[...] = jnp.zeros_like(st_ref)

        lo_h, sgn, qsc, tril = _gla_masks()
        tril_b = tril.astype(BF)
        wd, bd, gg = wd_ref[...], bd_ref[...], gg_ref[...]
        for c in range(nch):
            rs = slice(c * GLA_CHUNK, (c + 1) * GLA_CHUNK)
            qk = qk_ref[rs, :]
            _, _, _, qkd, kt, dec = _gla_chunk_common(qk, ga_ref[rs, :], wd, bd, tril_b, sgn, qsc)
            v = v_ref[rs, :]
            gz = gz_ref[rs, :]
            for h in range(GLA_HEADS):
                ls = slice(h * LANES, (h + 1) * LANES)
                a = jnp.where(lo_h, qkd[:, ls], 0.0).astype(BF)
                bm = jnp.where(lo_h, pltpu.roll(qkd[:, ls], 64, 1), 0.0).astype(BF)
                ktl = jnp.where(lo_h, pltpu.roll(kt[:, ls], 64, 1), 0.0).astype(BF)
                vh = v[:, ls].astype(BF)
                p = jnp.where(tril, _dot(a, bm, NT), 0.0)
                st = st_ref[h]
                sprev_ref[c, h] = st
                o = _dot(p.astype(BF), vh) + _dot(a, st.astype(BF), NT)
                st_ref[h] = st * dec[:, ls] + _dot(vh, ktl, TN)
                r = lax.rsqrt(jnp.mean(o * o, axis=-1, keepdims=True) + RMS_EPS)
                gzh = gz[:, ls]
                opre_ref[rs, ls] = o
                og_ref[rs, ls] = (((o * r) * gg[:, ls]) * (gzh * _sigmoid(gzh))).astype(og_ref.dtype)

    def col(width, off):
        return pl.BlockSpec((tb, width), lambda i: (i, off // width))

    return pl.pallas_call(
        body, name="gla_fwd", grid=(s // tb,),
        in_specs=[col(512, OFF_QK), col(512, OFF_V), col(512, OFF_GZ), col(LANES, OFF_GA),
                  _full((LANES, 512)), _full((1, 512)), _full((1, 512))],
        out_specs=[pl.BlockSpec((tb, 512), lambda i: (i, 0)), pl.BlockSpec((tb, 512), lambda i: (i, 0)),
                   pl.BlockSpec((nch, GLA_HEADS, LANES, LANES), lambda i: (i, 0, 0, 0))],
        out_shape=[jax.ShapeDtypeStruct((s, 512), BF), jax.ShapeDtypeStruct((s, 512), F32),
                   jax.ShapeDtypeStruct((s // GLA_CHUNK, GLA_HEADS, LANES, LANES), F32)],
        scratch_shapes=[pltpu.VMEM((GLA_HEADS, LANES, LANES), F32)],
        compiler_params=_params(("arbitrary",)),
    )(proj, proj, proj, proj, wdecp, bdecp, ggla)


def _gla_bwd(proj, dog, opre, sprev, wdecp, bdecp, ggla):
    s = proj.shape[0]
    tb = min(256, s)
    nch = tb // GLA_CHUNK
    nb = s // tb

    def body(qk_ref, v_ref, gz_ref, ga_ref, dog_ref, opre_ref, sprev_ref, wd_ref, bd_ref, gg_ref,
             dqk_ref, dv_ref, dgz_ref, dga_ref, dwd_ref, dbd_ref, dgg_ref, dst_ref):
        @pl.when(pl.program_id(0) == 0)
        def _():
            dst_ref[...] = jnp.zeros_like(dst_ref)
            dwd_ref[...] = jnp.zeros_like(dwd_ref)
            dbd_ref[...] = jnp.zeros_like(dbd_ref)
            dgg_ref[...] = jnp.zeros_like(dgg_ref)

        lo_h, sgn, qsc, tril = _gla_masks()
        tril_b = tril.astype(BF)
        triu_b = (lax.broadcasted_iota(jnp.int32, (GLA_CHUNK, GLA_CHUNK), 1)
                  >= lax.broadcasted_iota(jnp.int32, (GLA_CHUNK, GLA_CHUNK), 0)).astype(BF)
        last_row = lax.broadcasted_iota(jnp.int32, (GLA_CHUNK, LANES), 0) == GLA_CHUNK - 1
        wd, bd, gg = wd_ref[...], bd_ref[...], gg_ref[...]
        for c in reversed(range(nch)):
            rs = slice(c * GLA_CHUNK, (c + 1) * GLA_CHUNK)
            qk = qk_ref[rs, :]
            ga = ga_ref[rs, :]
            z2, e, f, qkd, kt, dec = _gla_chunk_common(qk, ga, wd, bd, tril_b, sgn, qsc)
            v = v_ref[rs, :]
            gz = gz_ref[rs, :]
            dog_c = dog_ref[rs, :]
            o_c = opre_ref[rs, :]
            db_parts = []
            for h in range(GLA_HEADS):
                ls = slice(h * LANES, (h + 1) * LANES)
                e_h, f_h = e[:, ls], f[:, ls]
                a32 = jnp.where(lo_h, qkd[:, ls], 0.0)
                bm32 = jnp.where(lo_h, pltpu.roll(qkd[:, ls], 64, 1), 0.0)
                kt32 = jnp.where(lo_h, pltpu.roll(kt[:, ls], 64, 1), 0.0)
                a, bm, ktl = a32.astype(BF), bm32.astype(BF), kt32.astype(BF)
                vh = v[:, ls].astype(BF)
                p = jnp.where(tril, _dot(a, bm, NT), 0.0).astype(BF)
                st = sprev_ref[c, h]
                stb = st.astype(BF)
                dstn = dst_ref[h]
                dstn_b = dstn.astype(BF)

                o = o_c[:, ls]
                gzh = gz[:, ls]
                dogh = dog_c[:, ls]
                r = lax.rsqrt(jnp.mean(o * o, axis=-1, keepdims=True) + RMS_EPS)
                ohat = o * r
                sg = _sigmoid(gzh)
                sil = gzh * sg
                g_h = gg[:, ls]
                dgz_ref[rs, ls] = (dogh * (ohat * g_h) * (sg * (1.0 + gzh * (1.0 - sg)))).astype(dgz_ref.dtype)
                dn = dogh * sil * g_h
                dgg_ref[:, ls] += jnp.sum(dogh * sil * ohat, axis=0, keepdims=True)
                do32 = r * (dn - ohat * jnp.mean(dn * ohat, axis=-1, keepdims=True))
                do = do32.astype(BF)

                dp = jnp.where(tril, _dot(do, vh, NT), 0.0).astype(BF)
                dv_ref[rs, ls] = (_dot(p, do, TN) + _dot(ktl, dstn_b, NT)).astype(dv_ref.dtype)
                dqd = _dot(dp, bm) + _dot(do, stb)
                dkd = _dot(dp, a, TN)
                dkt = _dot(vh, dstn_b)
                dst_ref[h] = dstn * dec[:, ls] + _dot(do, a, TN)
                ddec = jnp.sum(dstn * st, axis=0, keepdims=True)

                dq = dqd * e_h * (GLA_DK ** -0.5)
                dk = dkd * pltpu.roll(e_h, 64, 1) + dkt * f_h
                dqk_ref[rs, ls] = jnp.where(lo_h, dq, pltpu.roll(jnp.where(lo_h, dk, 0.0), 64, 1)).astype(dqk_ref.dtype)
                dkt_kt = dkt * kt32
                db = dqd * a32 - dkd * bm32 - dkt_kt
                dbl = jnp.sum(dkt_kt, axis=0, keepdims=True) + ddec * dec[:, ls]
                db = jnp.where(last_row, db + dbl, db)
                db_parts.append(jnp.where(lo_h, db, 0.0))
            db2 = jnp.concatenate(db_parts, axis=1)
            dla = _tri_matmul(triu_b, db2)
            dz32 = dla * (1.0 / GLA_TAU) * _sigmoid(-z2)
            dz = dz32.astype(BF)
            dga_ref[rs, :] = _dot(dz, wd, NT).astype(dga_ref.dtype)
            dwd_ref[...] += _dot(ga.astype(BF), dz, TN)
            dbd_ref[...] += jnp.sum(dz32, axis=0, keepdims=True)

    def col(width, off):
        return pl.BlockSpec((tb, width), lambda i: (nb - 1 - i, off // width))

    def rev(width):
        return pl.BlockSpec((tb, width), lambda i: (nb - 1 - i, 0))

    return pl.pallas_call(
        body, name="gla_bwd", grid=(nb,),
        in_specs=[col(512, OFF_QK), col(512, OFF_V), col(512, OFF_GZ), col(LANES, OFF_GA), rev(512), rev(512),
                  pl.BlockSpec((nch, GLA_HEADS, LANES, LANES), lambda i: (nb - 1 - i, 0, 0, 0)),
                  _full((LANES, 512)), _full((1, 512)), _full((1, 512))],
        out_specs=[rev(512), rev(512), rev(512), rev(LANES), _full((LANES, 512)), _full((1, 512)), _full((1, 512))],
        out_shape=[jax.ShapeDtypeStruct((s, 512), BF), jax.ShapeDtypeStruct((s, 512), BF),
                   jax.ShapeDtypeStruct((s, 512), BF), jax.ShapeDtypeStruct((s, LANES), BF),
                   jax.ShapeDtypeStruct((LANES, 512), F32), jax.ShapeDtypeStruct((1, 512), F32),
                   jax.ShapeDtypeStruct((1, 512), F32)],
        scratch_shapes=[pltpu.VMEM((GLA_HEADS, LANES, LANES), F32)],
        compiler_params=_params(("arbitrary",)),
    )(proj, proj, proj, proj, dog, opre, sprev, wdecp, bdecp, ggla)


def _swa_masks(n):
    lane = lax.broadcasted_iota(jnp.int32, (2 * SWA_BLOCK, LANES), 1)
    lo2 = lane < 64
    lane1 = lax.broadcasted_iota(jnp.int32, (SWA_BLOCK, LANES), 1)
    first_half = (lane1 % 64) < 32
    qi = lax.broadcasted_iota(jnp.int32, (SWA_BLOCK, 2 * SWA_BLOCK), 0)
    kj = lax.broadcasted_iota(jnp.int32, (SWA_BLOCK, 2 * SWA_BLOCK), 1)
    no_prev = jnp.where(n > 0, 0, 4 * SWA_BLOCK)
    valid = jnp.where(kj < SWA_BLOCK, kj - qi - no_prev, qi - kj + SWA_BLOCK + 1) > 0
    return lo2, lane1 < 64, first_half, valid


def _kv_variants(t, lo2):
    tr = pltpu.roll(t, 64, 1)
    lo_v = [jnp.where(lo2, t, 0.0).astype(BF), jnp.where(lo2, tr, 0.0).astype(BF)]
    hi_v = [jnp.where(lo2, 0.0, tr).astype(BF), jnp.where(lo2, 0.0, t).astype(BF)]
    return lo_v, hi_v


def _swa_probs(q, kg, valid, sink):
    sc = jnp.where(valid, _dot(q, kg, NT), -1e30)
    m = jnp.maximum(jnp.max(sc, axis=-1, keepdims=True), sink)
    ex = jnp.exp(sc - m)
    es = jnp.exp(sink - m)
    den = jnp.sum(ex, axis=-1, keepdims=True) + es
    return ex / den, es / den


def _swa_fwd(proj, cos, sin, sinks):
    s = proj.shape[0]
    nb = s // SWA_BLOCK

    def body(sq_ref, sz_ref, sk_ref, sv_ref, cos_ref, sin_ref, sinks_ref, os_ref, opre_ref, kprev, vprev):
        n = pl.program_id(0)

        @pl.when(n == 0)
        def _():
            kprev[...] = jnp.zeros_like(kprev)
            vprev[...] = jnp.zeros_like(vprev)

        lo2, _, first_half, valid = _swa_masks(n)
        cosb, sinb = cos_ref[...], sin_ref[...]
        kc = _rope(sk_ref[...], cosb, sinb, first_half)
        vc = sv_ref[...]
        k_lo, k_hi = _kv_variants(jnp.concatenate([kprev[...], kc], axis=0), lo2)
        v_lo, v_hi = _kv_variants(jnp.concatenate([vprev[...], vc], axis=0), lo2)
        for p in range(4):
            g = p // 2
            ls = slice(p * LANES, (p + 1) * LANES)
            q = (_rope(sq_ref[:, ls], cosb, sinb, first_half) * 0.125).astype(BF)
            p_lo, _ = _swa_probs(q, k_lo[g], valid, sinks_ref[0, 2 * p])
            p_hi, _ = _swa_probs(q, k_hi[g], valid, sinks_ref[0, 2 * p + 1])
            o = _dot(p_lo.astype(BF), v_lo[g]) + _dot(p_hi.astype(BF), v_hi[g])
            sz = sz_ref[:, ls]
            opre_ref[:, ls] = o
            os_ref[:, ls] = (o * (sz * _sigmoid(sz))).astype(os_ref.dtype)
        kprev[...] = kc
        vprev[...] = vc

    def col(width, off):
        return pl.BlockSpec((SWA_BLOCK, width), lambda i: (i, off // width))

    row = pl.BlockSpec((SWA_BLOCK, LANES), lambda i: (i, 0))
    return pl.pallas_call(
        body, name="swa_fwd", grid=(nb,),
        in_specs=[col(512, OFF_SQ), col(512, OFF_SZ), col(LANES, OFF_SK), col(LANES, OFF_SV), row, row,
                  pl.BlockSpec(memory_space=pltpu.SMEM)],
        out_specs=[pl.BlockSpec((SWA_BLOCK, 512), lambda i: (i, 0))] * 2,
        out_shape=[jax.ShapeDtypeStruct((s, 512), BF), jax.ShapeDtypeStruct((s, 512), F32)],
        scratch_shapes=[pltpu.VMEM((SWA_BLOCK, LANES), F32)] * 2,
        compiler_params=_params(("arbitrary",)),
    )(proj, proj, proj, proj, cos, sin, sinks)


def _swa_bwd(proj, dos, opre, cos, sin, sinks):
    s = proj.shape[0]
    nb = s // SWA_BLOCK

    def body(sq_ref, sz_ref, sk_ref, sv_ref, dos_ref, opre_ref, cos_ref, sin_ref, sinks_ref,
             dsq_ref, dsz_ref, dsk_ref, dsv_ref, dsink_ref, kprev, vprev, cprev, sprev):
        n = pl.program_id(0)

        @pl.when(n == 0)
        def _():
            kprev[...] = jnp.zeros_like(kprev)
            vprev[...] = jnp.zeros_like(vprev)
            cprev[...] = jnp.zeros_like(cprev)
            sprev[...] = jnp.zeros_like(sprev)
            for hd in range(SWA_HEADS):
                dsink_ref[0, hd] = 0.0

        lo2, lo1, first_half, valid = _swa_masks(n)
        cosb, sinb = cos_ref[...], sin_ref[...]
        kc = _rope(sk_ref[...], cosb, sinb, first_half)
        vc = sv_ref[...]
        k_lo, k_hi = _kv_variants(jnp.concatenate([kprev[...], kc], axis=0), lo2)
        v_lo, v_hi = _kv_variants(jnp.concatenate([vprev[...], vc], axis=0), lo2)
        dk_g = [None, None]
        dv_g = [None, None]
        for p in range(4):
            g = p // 2
            ls = slice(p * LANES, (p + 1) * LANES)
            q32 = _rope(sq_ref[:, ls], cosb, sinb, first_half) * 0.125
            q = q32.astype(BF)
            sz = sz_ref[:, ls]
            sg = _sigmoid(sz)
            dos_p = dos_ref[:, ls]
            dsz_ref[:, ls] = (dos_p * opre_ref[:, ls] * (sg * (1.0 + sz * (1.0 - sg)))).astype(dsz_ref.dtype)
            do32 = dos_p * (sz * sg)
            do = do32.astype(BF)
            ds_pair, pr_pair = [], []
            dq = None
            for half, (kg, vg) in enumerate(((k_lo[g], v_lo[g]), (k_hi[g], v_hi[g]))):
                pr, ps = _swa_probs(q, kg, valid, sinks_ref[0, 2 * p + half])
                dpr = _dot(do, vg, NT)
                rd = jnp.sum(pr * dpr, axis=-1, keepdims=True)
                ds = (pr * (dpr - rd)).astype(BF)
                dsink_ref[0, 2 * p + half] += -jnp.sum(ps * rd)
                part = _dot(ds, kg)
                dq = part if dq is None else dq + part
                ds_pair.append(ds)
                pr_pair.append(pr.astype(BF))
            dsq_ref[:, ls] = _rope_t(dq * 0.125, cosb, sinb, first_half).astype(dsq_ref.dtype)
            q_split = jnp.concatenate([jnp.where(lo1, q32, 0.0), jnp.where(lo1, 0.0, q32)], axis=0).astype(BF)
            do_split = jnp.concatenate([jnp.where(lo1, do32, 0.0), jnp.where(lo1, 0.0, do32)], axis=0).astype(BF)
            dkp = _dot(jnp.concatenate(ds_pair, axis=0), q_split, TN)
            dvp = _dot(jnp.concatenate(pr_pair, axis=0), do_split, TN)
            dk_g[g] = dkp if dk_g[g] is None else dk_g[g] + dkp
            dv_g[g] = dvp if dv_g[g] is None else dv_g[g] + dvp

        def home(m0, m1):
            t0 = m0 + pltpu.roll(m0, 64, 1)
            t1 = m1 + pltpu.roll(m1, 64, 1)
            return jnp.where(lo2, t0, t1)

        dk = home(dk_g[0], dk_g[1])
        dv = home(dv_g[0], dv_g[1])
        cur = pl.ds(pl.multiple_of(n * SWA_BLOCK, SWA_BLOCK), SWA_BLOCK)
        dsk_ref[cur, :] = _rope_t(dk[SWA_BLOCK:], cosb, sinb, first_half)
        dsv_ref[cur, :] = dv[SWA_BLOCK:]

        @pl.when(n > 0)
        def _():
            prv = pl.ds(pl.multiple_of((n - 1) * SWA_BLOCK, SWA_BLOCK), SWA_BLOCK)
            dsk_ref[prv, :] += _rope_t(dk[:SWA_BLOCK], cprev[...], sprev[...], first_half)
            dsv_ref[prv, :] += dv[:SWA_BLOCK]

        kprev[...] = kc
        vprev[...] = vc
        cprev[...] = cosb
        sprev[...] = sinb

    def col(width, off):
        return pl.BlockSpec((SWA_BLOCK, width), lambda i: (i, off // width))

    row = pl.BlockSpec((SWA_BLOCK, LANES), lambda i: (i, 0))
    wide = pl.BlockSpec((SWA_BLOCK, 512), lambda i: (i, 0))
    return pl.pallas_call(
        body, name="swa_bwd", grid=(nb,),
        in_specs=[col(512, OFF_SQ), col(512, OFF_SZ), col(LANES, OFF_SK), col(LANES, OFF_SV), wide, wide, row, row,
                  pl.BlockSpec(memory_space=pltpu.SMEM)],
        out_specs=[wide, wide, _full((s, LANES)), _full((s, LANES)), pl.BlockSpec(memory_space=pltpu.SMEM)],
        out_shape=[jax.ShapeDtypeStruct((s, 512), BF), jax.ShapeDtypeStruct((s, 512), BF),
                   jax.ShapeDtypeStruct((s, LANES), F32), jax.ShapeDtypeStruct((s, LANES), F32),
                   jax.ShapeDtypeStruct((1, SWA_HEADS), F32)],
        scratch_shapes=[pltpu.VMEM((SWA_BLOCK, LANES), F32)] * 4,
        compiler_params=_params(("arbitrary",)),
    )(proj, proj, proj, proj, dos, opre, cos, sin, sinks)


def _outproj(og, osw, w_out, x2d, target, gate, g_final):
    s = x2d.shape[0]
    tm = min(512, s)

    def body(og_ref, os_ref, w_ref, x_ref, t_ref, gate_ref, gf_ref,
             dx2_ref, dog_ref, dos_ref, dw_ref, loss_ref, dgf_ref, dgate_ref):
        @pl.when(pl.program_id(0) == 0)
        def _():
            dw_ref[...] = jnp.zeros_like(dw_ref)
            loss_ref[...] = jnp.zeros_like(loss_ref)
            dgf_ref[...] = jnp.zeros_like(dgf_ref)
            dgate_ref[...] = jnp.zeros_like(dgate_ref)

        ogv, osv, w = og_ref[...], os_ref[...], w_ref[...]
        gate, gf = gate_ref[...], gf_ref[...]
        y = _dot(ogv, w[:512]) + _dot(osv, w[512:])
        x2 = x_ref[...] + gate * y
        r = lax.rsqrt(jnp.mean(x2 * x2, axis=-1, keepdims=True) + RMS_EPS)
        xn = x2 * r
        err = xn * gf - t_ref[...]
        loss_ref[...] += 0.5 * jnp.sum(jnp.mean(err * err, axis=-1, keepdims=True), axis=0, keepdims=True)
        dyf = err * (1.0 / D_MODEL)
        dgf_ref[...] += jnp.sum(dyf * xn, axis=0, keepdims=True)
        t = dyf * gf
        dx2 = r * (t - xn * jnp.mean(t * xn, axis=-1, keepdims=True))
        dx2_ref[...] = dx2
        dgate_ref[...] += jnp.sum(dx2 * y, axis=0, keepdims=True)
        dy = (dx2 * gate).astype(BF)
        dmix = _dot(dy, w, NT)
        dog_ref[...] = dmix[:, :512]
        dos_ref[...] = dmix[:, 512:]
        dw_ref[:512, :] += _dot(ogv, dy, TN)
        dw_ref[512:, :] += _dot(osv, dy, TN)

    half = pl.BlockSpec((tm, 512), lambda i: (i, 0))
    rowb = pl.BlockSpec((tm, D_MODEL), lambda i: (i, 0))
    vec = _full((1, D_MODEL))
    return pl.pallas_call(
        body, name="outproj", grid=(s // tm,),
        in_specs=[half, half, _full((D_MODEL, D_MODEL)), rowb, rowb, vec, vec],
        out_specs=[rowb, half, half, _full((D_MODEL, D_MODEL)), _full((1, 1)), vec, vec],
        out_shape=[jax.ShapeDtypeStruct((s, D_MODEL), F32), jax.ShapeDtypeStruct((s, 512), F32),
                   jax.ShapeDtypeStruct((s, 512), F32), jax.ShapeDtypeStruct((D_MODEL, D_MODEL), F32),
                   jax.ShapeDtypeStruct((1, 1), F32), jax.ShapeDtypeStruct((1, D_MODEL), F32),
                   jax.ShapeDtypeStruct((1, D_MODEL), F32)],
        compiler_params=_params(("arbitrary",)),
    )(og, osw, w_out, x2d, target, gate, g_final)


_PIECES = ((OFF_QK, 512), (OFF_V, 512), (OFF_GZ, 512), (OFF_SQ, 512), (OFF_SZ, 512),
           (OFF_SK, LANES), (OFF_SV, LANES), (OFF_GA, LANES))


def _inproj_bwd(x2d, shift, sc1p, g_norm, wpad, dx2, pieces):
    s = x2d.shape[0]
    tm = min(256, s)
    nsteps = s // tm

    def body(x_ref, sh_ref, sc_ref, g_ref, w_hbm, dx2_ref, *rest):
        piece_refs = rest[:len(_PIECES)]
        gx_ref, dw_hbm, dsh_ref, dsc_ref, dg_ref, w_vm, dw_vm, sem = rest[len(_PIECES):]
        i = pl.program_id(0)

        @pl.when(i == 0)
        def _():
            cp = pltpu.make_async_copy(w_hbm, w_vm, sem)
            cp.start()
            dw_vm[...] = jnp.zeros_like(dw_vm)
            dsh_ref[...] = jnp.zeros_like(dsh_ref)
            dsc_ref[...] = jnp.zeros_like(dsc_ref)
            dg_ref[...] = jnp.zeros_like(dg_ref)
            cp.wait()

        g, sc1p_v = g_ref[...], sc_ref[...]
        xn, r, h = _modnorm(x_ref[...], g, sc1p_v, sh_ref[...])
        hb = h.astype(BF)
        dh = None
        for (off, width), pr in zip(_PIECES, piece_refs):
            dp = pr[...].astype(BF)
            part = _dot(dp, w_vm[:, off:off + width], NT)
            dh = part if dh is None else dh + part
            dw_vm[:, off:off + width] += _dot(hb, dp, TN)
        dsh_ref[...] += jnp.sum(dh, axis=0, keepdims=True)
        dsc_ref[...] += jnp.sum(dh * (xn * g), axis=0, keepdims=True)
        dg_ref[...] += jnp.sum(dh * xn * sc1p_v, axis=0, keepdims=True)
        dxn = dh * g * sc1p_v
        gx_ref[...] = dx2_ref[...] + r * (dxn - xn * jnp.mean(dxn * xn, axis=-1, keepdims=True))

        @pl.when(i == nsteps - 1)
        def _():
            cp = pltpu.make_async_copy(dw_vm, dw_hbm, sem)
            cp.start()
            cp.wait()

    rowb = pl.BlockSpec((tm, D_MODEL), lambda i: (i, 0))
    vec = _full((1, D_MODEL))
    anyspec = pl.BlockSpec(memory_space=pl.ANY)
    piece_specs = [pl.BlockSpec((tm, width), lambda i: (i, 0)) for _, width in _PIECES]
    return pl.pallas_call(
        body, name="inproj_bwd", grid=(nsteps,),
        in_specs=[rowb, vec, vec, vec, anyspec, rowb] + piece_specs,
        out_specs=[rowb, anyspec, vec, vec, vec],
        out_shape=[jax.ShapeDtypeStruct((s, D_MODEL), F32), jax.ShapeDtypeStruct((D_MODEL, D_PAD), F32),
                   jax.ShapeDtypeStruct((1, D_MODEL), F32), jax.ShapeDtypeStruct((1, D_MODEL), F32),
                   jax.ShapeDtypeStruct((1, D_MODEL), F32)],
        scratch_shapes=[pltpu.VMEM((D_MODEL, D_PAD), BF), pltpu.VMEM((D_MODEL, D_PAD), F32), pltpu.SemaphoreType.DMA],
        compiler_params=_params(("arbitrary",)),
    )(x2d, shift, sc1p, g_norm, wpad, dx2, *pieces)


def _adam(w, g, m, v):
    m2 = ADAM_B1 * m + (1.0 - ADAM_B1) * g
    v2 = ADAM_B2 * v + (1.0 - ADAM_B2) * (g * g)
    m_hat = m2 / (1.0 - ADAM_B1 ** ADAM_STEP)
    v_hat = v2 / (1.0 - ADAM_B2 ** ADAM_STEP)
    delta = -ADAM_LR * (m_hat / (jnp.sqrt(v_hat) + ADAM_EPS) + ADAM_WD * w)
    return delta, m2, v2


def _adamw(w, g, m, v, name):
    rr, cc = w.shape
    tr = min(256, rr)

    def body(w_ref, g_ref, m_ref, v_ref, d_ref, m2_ref, v2_ref):
        d_ref[...], m2_ref[...], v2_ref[...] = _adam(w_ref[...], g_ref[...], m_ref[...], v_ref[...])

    blk = pl.BlockSpec((tr, cc), lambda i: (i, 0))
    return pl.pallas_call(
        body, name=name, grid=(rr // tr,), in_specs=[blk] * 4, out_specs=[blk] * 3,
        out_shape=[jax.ShapeDtypeStruct((rr, cc), F32)] * 3,
        compiler_params=_params(("arbitrary",)),
    )(w, g, m, v)


def _ada_update(c_all, dmod_cols, w, m, v):
    rr, cc = w.shape
    tr = min(256, rr)
    c_all = jnp.pad(c_all, ((0, 8), (0, 0)))
    dmod_cols = jnp.pad(dmod_cols, ((0, 8), (0, 0)))

    def body(c_ref, dm_ref, w_ref, m_ref, v_ref, g_ref, d_ref, m2_ref, v2_ref):
        cv = c_ref[...]
        sc = (cv * _sigmoid(cv)).astype(BF)
        g = _dot(sc, dm_ref[...].astype(BF), TN)
        g_ref[...] = g
        d_ref[...], m2_ref[...], v2_ref[...] = _adam(w_ref[...], g, m_ref[...], v_ref[...])

    blk = pl.BlockSpec((tr, cc), lambda i: (i, 0))
    return pl.pallas_call(
        body, name="ada_update", grid=(rr // tr,),
        in_specs=[pl.BlockSpec((16, tr), lambda i: (0, i)), _full((16, cc)), blk, blk, blk],
        out_specs=[blk] * 4, out_shape=[jax.ShapeDtypeStruct((rr, cc), F32)] * 4,
        compiler_params=_params(("arbitrary",)),
    )(c_all, dmod_cols, w, m, v)


def _small_update(parts, weights, moms, vels):
    n = len(weights)

    def body(*refs):
        p_refs, w_refs, m_refs, v_refs = refs[:n + 1], refs[n + 1:2 * n + 1], refs[2 * n + 1:3 * n + 1], refs[3 * n + 1:4 * n + 1]
        outs = refs[4 * n + 1:]
        for i in range(n):
            g = p_refs[i][0]
            for d in range(1, 8):
                g = g + p_refs[i][d]
            delta, m2, v2 = _adam(w_refs[i][...], g, m_refs[i][...], v_refs[i][...])
            outs[4 * i][...] = g
            outs[4 * i + 1][...] = delta
            outs[4 * i + 2][...] = m2
            outs[4 * i + 3][...] = v2
        tot = p_refs[n][0]
        for d in range(1, 8):
            tot = tot + p_refs[n][d]
        outs[4 * n][...] = tot

    out_shape = []
    for w in weights:
        out_shape += [jax.ShapeDtypeStruct(w.shape, F32)] * 4
    out_shape.append(jax.ShapeDtypeStruct(parts[n].shape[1:], F32))
    return pl.pallas_call(body, name="small_update", out_shape=out_shape, compiler_params=_params())(
        *parts, *weights, *moms, *vels)


def _pad_w_in(w):
    gq, gk = w[:, 0:256], w[:, 256:512]
    qk = []
    for h in range(GLA_HEADS):
        qk += [gq[:, 64 * h:64 * h + 64], gk[:, 64 * h:64 * h + 64]]
    gv, ga, gz = w[:, 512:1024], w[:, 1024:1040], w[:, 1040:1552]
    sq, sk, sv, sz = w[:, 1552:2064], w[:, 2064:2192], w[:, 2192:2320], w[:, 2320:2832]
    pad = jnp.zeros((w.shape[0], LANES - GLA_RANK), w.dtype)
    return jnp.concatenate(qk + [gv, gz, sq, sz, sk, sv, ga, pad], axis=1)


def _unpad_dw_in(d):
    qk = d[:, 0:512]
    gq = jnp.concatenate([qk[:, 128 * h:128 * h + 64] for h in range(GLA_HEADS)], axis=1)
    gk = jnp.concatenate([qk[:, 128 * h + 64:128 * h + 128] for h in range(GLA_HEADS)], axis=1)
    return jnp.concatenate([gq, gk, d[:, OFF_V:OFF_V + 512], d[:, OFF_GA:OFF_GA + GLA_RANK], d[:, OFF_GZ:OFF_GZ + 512],
                            d[:, OFF_SQ:OFF_SQ + 512], d[:, OFF_SK:OFF_SK + 128], d[:, OFF_SV:OFF_SV + 128],
                            d[:, OFF_SZ:OFF_SZ + 512]], axis=1)


def _dup_heads(t):
    parts = []
    for h in range(GLA_HEADS):
        parts += [t[..., 64 * h:64 * h + 64]] * 2
    return jnp.concatenate(parts, axis=-1)


def _rows8(a):
    flat = a.reshape(-1)
    rows = -(-flat.shape[0] // LANES)
    rows8 = -(-rows // 8) * 8
    flat = jnp.pad(flat, (0, rows8 * LANES - flat.shape[0]))
    return flat.reshape(rows8, LANES)


def kernel(x, c, positions, w_ada, b_ada, g_norm, w_in, w_decay, b_decay, g_gla_head, sinks, w_out, g_final, loss_target, m_w_ada, m_b_ada, m_g_norm, m_w_in, m_w_decay, m_b_decay, m_g_gla_head, m_sinks, m_w_out, m_g_final, v_w_ada, v_b_ada, v_g_norm, v_w_in, v_w_decay, v_b_decay, v_g_gla_head, v_sinks, v_w_out, v_g_final):
    ax, ay, ac = lax.axis_index("x"), lax.axis_index("y"), lax.axis_index("c")
    chip = 2 * ax + ay
    dev = 2 * chip + ac
    s = x.shape[1]
    x2d = x[0]
    target = loss_target[0]
    w_ada2, w_in2, w_out2, w_dec2 = w_ada[0], w_in[0], w_out[0], w_decay[0]
    ada_cols = w_ada2.shape[1]
    in_cols = w_in2.shape[1]
    out_rows = w_out2.shape[0]

    first = _all_gather(jnp.concatenate([c.reshape(8, LANES), w_dec2.reshape(8, LANES)], axis=0), "gather_c")
    first = first.reshape(8, 2, 8, LANES)
    c_all = first[:, 0].reshape(8, D_MODEL)
    w_dec_full = first[0::2, 1].reshape(4, GLA_RANK, 64).transpose(1, 0, 2).reshape(GLA_RANK, 256)
    b_shard = lax.dynamic_slice(b_ada, (0, chip * ada_cols), (1, ada_cols))
    mod_all = _all_gather(_ada_fwd(c_all, w_ada2, b_shard), "gather_mod")
    mod = mod_all.reshape(4, 2, 8, ada_cols)[:, 0]
    mod = lax.dynamic_slice(mod, (0, dev, 0), (4, 1, ada_cols)).reshape(1, 4 * ada_cols)
    shift, sc1p, gate = mod[:, :D_MODEL], 1.0 + mod[:, D_MODEL:2 * D_MODEL], mod[:, 2 * D_MODEL:]

    half_in = lax.dynamic_slice(w_in2, (ac * (D_MODEL // 2), 0), (D_MODEL // 2, in_cols)).astype(BF)
    w_in_all = _all_gather(half_in, "gather_w_in").reshape(4, D_MODEL, in_cols)
    wpad = _pad_w_in(w_in_all.transpose(1, 0, 2).reshape(D_MODEL, 4 * in_cols))
    half_out = lax.dynamic_slice(w_out2, (ac * (out_rows // 2), 0), (out_rows // 2, D_MODEL)).astype(BF)
    w_out_all = _all_gather(half_out, "gather_w_out")

    wdecp = jnp.pad(_dup_heads(w_dec_full), ((0, LANES - GLA_RANK), (0, 0))).astype(BF)
    bdecp = _dup_heads(b_decay)
    inv_freq = 1.0 / (ROPE_THETA ** (jnp.arange(0, 64, 2, dtype=F32) / 64))
    cos, sin = _rope_tables(positions.reshape(s, 1), jnp.tile(inv_freq, 4).reshape(1, LANES))

    proj = _inproj_fwd(x2d, shift, sc1p, g_norm, wpad)
    og, o_gla, sprev = _gla_fwd(proj, wdecp, bdecp, g_gla_head)
    osw, o_swa = _swa_fwd(proj, cos, sin, sinks)
    dx2, dog, dos, dw_out, loss_p, dgf, dgate = _outproj(og, osw, w_out_all, x2d, target, gate, g_final.reshape(1, D_MODEL))
    dsq, dsz, dsk, dsv, dsinks = _swa_bwd(proj, dos, o_swa, cos, sin, sinks)
    dqk, dv, dgz, dga, dwdp, dbdp, dgg = _gla_bwd(proj, dog, o_gla, sprev, wdecp, bdecp, g_gla_head)
    pieces = (dqk, dv, dgz, dsq, dsz, dsk, dsv, dga)
    gx, dwpad, dshift, dscale, dgn = _inproj_bwd(x2d, shift, sc1p, g_norm, wpad, dx2, pieces)

    dw_in_full = _unpad_dw_in(dwpad)
    g_w_in = _reduce_scatter(dw_in_full.reshape(D_MODEL, 4, in_cols).transpose(1, 0, 2), "reduce_w_in")
    g_w_out = _reduce_scatter(dw_out.reshape(4, out_rows, D_MODEL), "reduce_w_out")

    dwd = jnp.concatenate([dwdp[:GLA_RANK, 128 * h:128 * h + 64] for h in range(GLA_HEADS)], axis=1)
    dbd = jnp.concatenate([dbdp[:, 128 * h:128 * h + 64] for h in range(GLA_HEADS)], axis=1)
    segs = [jnp.concatenate([dshift, dscale, dgate], axis=1), dgn, dgf, dwd, dbd, dgg, dsinks, loss_p]
    packed = [_rows8(a) for a in segs]
    offs = [0]
    for a in packed:
        offs.append(offs[-1] + a.shape[0])
    small = _all_gather(jnp.concatenate(packed, axis=0), "gather_small").reshape(8, offs[-1], LANES)

    def seg(i, size):
        return small[:, offs[i]:offs[i + 1]].reshape(8, -1)[:, :size]

    dmod_all = seg(0, 3 * D_MODEL)
    dwd_all = lax.dynamic_slice(seg(3, GLA_RANK * 256).reshape(8, GLA_RANK, 256), (0, 0, chip * 64), (8, GLA_RANK, 64))
    parts = [dmod_all.reshape(8, 1, 3 * D_MODEL), seg(1, D_MODEL).reshape(8, 1, D_MODEL), dwd_all,
             seg(4, 256).reshape(8, 1, 256), seg(5, 512).reshape(8, 1, 512), seg(6, SWA_HEADS).reshape(8, 1, SWA_HEADS),
             seg(2, D_MODEL).reshape(8, 1, D_MODEL), seg(7, LANES).reshape(8, 1, LANES)]
    smalls = _small_update(
        parts,
        [b_ada, g_norm, w_dec2, b_decay, g_gla_head, sinks, g_final.reshape(1, D_MODEL)],
        [m_b_ada, m_g_norm, m_w_decay[0], m_b_decay, m_g_gla_head, m_sinks, m_g_final.reshape(1, D_MODEL)],
        [v_b_ada, v_g_norm, v_w_decay[0], v_b_decay, v_g_gla_head, v_sinks, v_g_final.reshape(1, D_MODEL)])
    (g_b_ada, d_b_ada, nm_b_ada, nv_b_ada, g_gn, d_gn, nm_gn, nv_gn, g_wd, d_wd, nm_wd, nv_wd,
     g_bd, d_bd, nm_bd, nv_bd, g_gg, d_gg, nm_gg, nv_gg, g_sk, d_sk, nm_sk, nv_sk,
     g_gf, d_gf, nm_gf, nv_gf, loss_row) = smalls
    loss = loss_row[0, 0]

    dmod_cols = lax.dynamic_slice(dmod_all, (0, chip * ada_cols), (8, ada_cols))
    g_w_ada, d_w_ada, nm_w_ada, nv_w_ada = _ada_update(c_all, dmod_cols, w_ada2, m_w_ada[0], v_w_ada[0])
    d_w_in, nm_w_in, nv_w_in = _adamw(w_in2, g_w_in, m_w_in[0], v_w_in[0], "adamw_w_in")
    d_w_out, nm_w_out, nv_w_out = _adamw(w_out2, g_w_out, m_w_out[0], v_w_out[0], "adamw_w_out")

    flat = lambda a: a.reshape(D_MODEL)
    grads = [g_w_ada[None], g_b_ada, g_gn, g_w_in[None], g_wd[None], g_bd, g_gg, g_sk, g_w_out[None], flat(g_gf)]
    deltas = [d_w_ada[None], d_b_ada, d_gn, d_w_in[None], d_wd[None], d_bd, d_gg, d_sk, d_w_out[None], flat(d_gf)]
    new_m = [nm_w_ada[None], nm_b_ada, nm_gn, nm_w_in[None], nm_wd[None], nm_bd, nm_gg, nm_sk, nm_w_out[None], flat(nm_gf)]
    new_v = [nv_w_ada[None], nv_b_ada, nv_gn, nv_w_in[None], nv_wd[None], nv_bd, nv_gg, nv_sk, nv_w_out[None], flat(nv_gf)]
    return (loss, gx[None], *grads, *deltas, *new_m, *new_v)
```

```python
import jax
import jax.numpy as jnp
from jax import lax
from jax.experimental import pallas as pl
from jax.experimental.pallas import tpu as pltpu

F32 = jnp.float32
BF = jnp.bfloat16

D_MODEL = 1024
GLA_HEADS = 4
GLA_DK = 64
GLA_CHUNK = 64
GLA_RANK = 16
GLA_TAU = 16.0
GLA_ROWS = 256
SWA_HEADS = 8
SWA_BLOCK = 128
SWA_QBLOCKS = 2
RMS_EPS = 1e-6
ROPE_THETA = 10000.0

OFF_QK, OFF_V, OFF_GZ, OFF_SQ, OFF_SZ, OFF_SK, OFF_SV, OFF_GA = 0, 512, 1024, 1536, 2048, 2560, 2688, 2816
D_PAD = 2944
LANES = 128
VMEM_LIMIT = 56 * 1024 * 1024

ADAM_LR, ADAM_B1, ADAM_B2, ADAM_EPS, ADAM_WD, ADAM_STEP = 0.001, 0.9, 0.999, 1e-08, 0.01, 10

NT = (((1,), (1,)), ((), ()))
TN = (((0,), (0,)), ((), ()))
MESH = pl.DeviceIdType.MESH


def _dot(a, b, dims=None):
    if dims is None:
        return jnp.dot(a, b, preferred_element_type=F32)
    return lax.dot_general(a, b, dims, preferred_element_type=F32)


def _sigmoid(x):
    return 1.0 / (1.0 + jnp.exp(-x))


def _params(sem=None):
    return pltpu.CompilerParams(dimension_semantics=sem, vmem_limit_bytes=VMEM_LIMIT)


def _full(shape):
    return pl.BlockSpec(shape, lambda i: (0,) * len(shape))


def _all_gather(block, name):
    m_per, n = block.shape

    def body(x_ref, out_ref, send_sems, recv_sems, local_sem):
        x, y, c = lax.axis_index("x"), lax.axis_index("y"), lax.axis_index("c")
        me, sibling = (x, y, c), (x, y, 1 - c)
        chips = [(1 - x, y), (x, 1 - y), (1 - x, 1 - y)]

        def rows(px, py, pc):
            return out_ref.at[pl.ds((4 * px + 2 * py + pc) * m_per, m_per), :]

        def copy(k, blk, to, src=None):
            return pltpu.make_async_remote_copy(
                src_ref=rows(*blk) if src is None else src, dst_ref=rows(*blk),
                send_sem=send_sems.at[k], recv_sem=recv_sems.at[k], device_id=to, device_id_type=MESH)

        mine = pltpu.make_async_copy(x_ref, rows(*me), local_sem)
        mine.start()
        first = [copy(0, me, sibling, src=x_ref)]
        first += [copy(1 + j, me, (*chip, c), src=x_ref) for j, chip in enumerate(chips)]
        for cp in first:
            cp.start()
        passed = [copy(4 + j, (*chip, c), sibling) for j, chip in enumerate(chips)]
        for j, chip in enumerate(chips):
            copy(1 + j, (*chip, c), me).wait_recv()
            passed[j].start()
        copy(0, sibling, me).wait_recv()
        for j, chip in enumerate(chips):
            copy(4 + j, (*chip, 1 - c), me).wait_recv()
        for cp in first + passed:
            cp.wait_send()
        mine.wait()

    return pl.pallas_call(
        body, name=name,
        out_shape=jax.ShapeDtypeStruct((8 * m_per, n), block.dtype),
        in_specs=[pl.BlockSpec(memory_space=pltpu.VMEM)],
        out_specs=pl.BlockSpec(memory_space=pltpu.VMEM),
        scratch_shapes=[pltpu.SemaphoreType.DMA((7,)), pltpu.SemaphoreType.DMA((7,)), pltpu.SemaphoreType.DMA],
        compiler_params=pltpu.CompilerParams(vmem_limit_bytes=VMEM_LIMIT),
    )(block)


def _reduce_scatter(parts, name):
    _, rr, cc = parts.shape
    r2 = rr // 2

    def body(p_ref, out_ref, acc_ref, land_ref, send_sems, recv_sems):
        x, y, c = lax.axis_index("x"), lax.axis_index("y"), lax.axis_index("c")
        sibling = (x, y, 1 - c)
        chips = [(1 - x, y), (x, 1 - y), (1 - x, 1 - y)]
        mine = pl.ds(pl.multiple_of(c * r2, r2), r2)
        other = pl.ds(pl.multiple_of((1 - c) * r2, r2), r2)

        swap = pltpu.make_async_remote_copy(
            src_ref=p_ref.at[:, other, :], dst_ref=acc_ref, send_sem=send_sems.at[0], recv_sem=recv_sems.at[0],
            device_id=sibling, device_id_type=MESH)
        swap.start()
        swap.wait()
        for j in range(4):
            acc_ref[j] = acc_ref[j] + p_ref[j, mine, :]

        sends = []
        for k, (tx, ty) in enumerate(chips):
            cp = pltpu.make_async_remote_copy(
                src_ref=acc_ref.at[2 * tx + ty], dst_ref=land_ref.at[k], send_sem=send_sems.at[1 + k],
                recv_sem=recv_sems.at[1 + k], device_id=(tx, ty, c), device_id_type=MESH)
            cp.start()
            sends.append(cp)
        for cp in sends:
            cp.wait_recv()
        total = acc_ref[2 * x + y]
        for k in range(3):
            total = total + land_ref[k]
        out_ref[mine, :] = total
        for cp in sends:
            cp.wait_send()

        share = pltpu.make_async_remote_copy(
            src_ref=out_ref.at[mine, :], dst_ref=out_ref.at[mine, :], send_sem=send_sems.at[4],
            recv_sem=recv_sems.at[4], device_id=sibling, device_id_type=MESH)
        share.start()
        share.wait()

    return pl.pallas_call(
        body, name=name,
        out_shape=jax.ShapeDtypeStruct((rr, cc), F32),
        in_specs=[pl.BlockSpec(memory_space=pltpu.VMEM)],
        out_specs=pl.BlockSpec(memory_space=pltpu.VMEM),
        scratch_shapes=[pltpu.VMEM((4, r2, cc), F32), pltpu.VMEM((3, r2, cc), F32),
                        pltpu.SemaphoreType.DMA((5,)), pltpu.SemaphoreType.DMA((5,))],
        compiler_params=pltpu.CompilerParams(vmem_limit_bytes=VMEM_LIMIT),
    )(parts)


def _ada_fwd(c_all, w_ada, b_shard):
    def body(c_ref, w_ref, b_ref, o_ref):
        cv = c_ref[...]
        sc = (cv * _sigmoid(cv)).astype(BF)
        o_ref[...] = _dot(sc, w_ref[...].astype(BF)) + b_ref[...]

    return pl.pallas_call(
        body, name="ada_fwd", out_shape=jax.ShapeDtypeStruct((8, w_ada.shape[1]), F32),
        compiler_params=_params(),
    )(c_all, w_ada, b_shard)


def _rope_tables(pos_col, inv_freq):
    s = pos_col.shape[0]
    tm = min(1024, s)

    def body(p_ref, f_ref, cos_ref, sin_ref):
        ang = p_ref[...].astype(F32) * f_ref[...]
        lane = lax.broadcasted_iota(jnp.int32, ang.shape, 1)
        cos_ref[...] = jnp.cos(ang)
        sn = jnp.sin(ang)
        sin_ref[...] = jnp.where((lane % 64) < 32, -sn, sn)

    return pl.pallas_call(
        body, name="rope_tables", grid=(s // tm,),
        in_specs=[pl.BlockSpec((tm, 1), lambda i: (i, 0)), _full((1, LANES))],
        out_specs=[pl.BlockSpec((tm, LANES), lambda i: (i, 0))] * 2,
        out_shape=[jax.ShapeDtypeStruct((s, LANES), F32)] * 2,
        compiler_params=_params(("arbitrary",)),
    )(pos_col, inv_freq)


def _rope(t, cosb, sinb, first_half):
    partner = jnp.where(first_half, pltpu.roll(t, 96, 1), pltpu.roll(t, 32, 1))
    return t * cosb + partner * sinb


def _rope_t(g, cosb, sinb, first_half):
    gs = g * sinb
    partner = jnp.where(first_half, pltpu.roll(gs, 96, 1), pltpu.roll(gs, 32, 1))
    return g * cosb + partner


def _modnorm(x, g, sc1p, shift):
    r = lax.rsqrt(jnp.mean(x * x, axis=-1, keepdims=True) + RMS_EPS)
    xn = x * r
    return xn, r, (xn * g) * sc1p + shift


def _inproj_fwd(x2d, shift, sc1p, g_norm, wpad):
    s = x2d.shape[0]
    tm = min(512, s)

    def body(x_ref, sh_ref, sc_ref, g_ref, w_ref, o_ref):
        _, _, h = _modnorm(x_ref[...], g_ref[...], sc_ref[...], sh_ref[...])
        o_ref[...] = _dot(h.astype(BF), w_ref[...])

    vec = _full((1, D_MODEL))
    return pl.pallas_call(
        body, name="inproj_fwd", grid=(s // tm,),
        in_specs=[pl.BlockSpec((tm, D_MODEL), lambda i: (i, 0)), vec, vec, vec, _full((D_MODEL, D_PAD))],
        out_specs=pl.BlockSpec((tm, D_PAD), lambda i: (i, 0)),
        out_shape=jax.ShapeDtypeStruct((s, D_PAD), F32),
        compiler_params=_params(("arbitrary",)),
    )(x2d, shift, sc1p, g_norm, wpad)


def _split3(a):
    hi = a.astype(BF)
    r1 = a - hi.astype(F32)
    mid = r1.astype(BF)
    lo = (r1 - mid.astype(F32)).astype(BF)
    return hi, mid, lo


def _tri_matmul(tri, a):
    hi, mid, lo = _split3(a)
    return _dot(tri, hi) + _dot(tri, mid) + _dot(tri, lo)


def _chunks(tb):
    return [slice(c * GLA_CHUNK, (c + 1) * GLA_CHUNK) for c in range(tb // GLA_CHUNK)]


def _per_chunk_rows(rows, width):
    return jnp.concatenate([jnp.broadcast_to(r, (GLA_CHUNK, width)) for r in rows], axis=0)


def _gla_masks(tb):
    lane = lax.broadcasted_iota(jnp.int32, (1, 512), 1)
    lo512 = (lane % LANES) < GLA_DK
    sgn = jnp.where(lo512, 1.0, -1.0).astype(F32)
    qsc = jnp.where(lo512, GLA_DK ** -0.5, 1.0).astype(F32)
    lo_h = lax.broadcasted_iota(jnp.int32, (tb, LANES), 1) < GLA_DK
    row = lax.broadcasted_iota(jnp.int32, (tb, tb), 0)
    col = lax.broadcasted_iota(jnp.int32, (tb, tb), 1)
    same = (row // GLA_CHUNK) == (col // GLA_CHUNK)
    tril = jnp.where(same, row - col, -1) >= 0
    triu = jnp.where(same, col - row, -1) >= 0
    return lo_h, sgn, qsc, tril, triu


def _gla_block_common(qk, ga, wd, bd, tril_b, sgn, qsc):
    tb = qk.shape[0]
    z2 = _dot(ga.astype(BF), wd) + bd
    la2 = (jnp.minimum(z2, 0.0) - jnp.log1p(jnp.exp(-jnp.abs(z2)))) * (1.0 / GLA_TAU)
    b2 = _tri_matmul(tril_b, la2)
    bls = [b2[rs.stop - 1:rs.stop, :] for rs in _chunks(tb)]
    e = jnp.exp(b2 * sgn)
    f = jnp.exp(_per_chunk_rows(bls, 512) - b2)
    qkd = qk * e * qsc
    kt = qk * f
    decs = [jnp.exp(bl) for bl in bls]
    return z2, e, f, qkd, kt, decs


def _gla_fwd(proj, wdecp, bdecp, ggla):
    s = proj.shape[0]
    tb = min(GLA_ROWS, s)
    nch = tb // GLA_CHUNK

    def body(qk_ref, v_ref, gz_ref, ga_ref, wd_ref, bd_ref, gg_ref, og_ref, opre_ref, sprev_ref, st_ref):
        @pl.when(pl.program_id(0) == 0)
        def _():
            st_ref[...] = jnp.zeros_like(st_ref)

        lo_h, sgn, qsc, tril, _ = _gla_masks(tb)
        tril_b = jnp.where(tril, 1.0, 0.0).astype(BF)
        gg = gg_ref[...]
        _, _, _, qkd, kt, decs = _gla_block_common(qk_ref[...], ga_ref[...], wd_ref[...], bd_ref[...], tril_b, sgn, qsc)
        for h in range(GLA_HEADS):
            ls = slice(h * LANES, (h + 1) * LANES)
            a = jnp.where(lo_h, qkd[:, ls], 0.0).astype(BF)
            bm = jnp.where(lo_h, pltpu.roll(qkd[:, ls], 64, 1), 0.0).astype(BF)
            ktl = jnp.where(lo_h, pltpu.roll(kt[:, ls], 64, 1), 0.0).astype(BF)
            vh = v_ref[:, ls].astype(BF)
            p = jnp.where(tril, _dot(a, bm, NT), 0.0).astype(BF)
            o = _dot(p, vh)
            st = st_ref[h]
            inter = []
            for c, rs in enumerate(_chunks(tb)):
                sprev_ref[c, h] = st
                inter.append(_dot(a[rs], st.astype(BF), NT))
                st = st * decs[c][:, ls] + _dot(vh[rs], ktl[rs], TN)
            st_ref[h] = st
            o = o + jnp.concatenate(inter, axis=0)
            r = lax.rsqrt(jnp.mean(o * o, axis=-1, keepdims=True) + RMS_EPS)
            gzh = gz_ref[:, ls]
            opre_ref[:, ls] = o
            og_ref[:, ls] = (((o * r) * gg[:, ls]) * (gzh * _sigmoid(gzh))).astype(og_ref.dtype)

    def col(width, off):
        return pl.BlockSpec((tb, width), lambda i: (i, off // width))

    return pl.pallas_call(
        body, name="gla_fwd", grid=(s // tb,),
        in_specs=[col(512, OFF_QK), col(512, OFF_V), col(512, OFF_GZ), col(LANES, OFF_GA),
                  _full((LANES, 512)), _full((1, 512)), _full((1, 512))],
        out_specs=[pl.BlockSpec((tb, 512), lambda i: (i, 0)), pl.BlockSpec((tb, 512), lambda i: (i, 0)),
                   pl.BlockSpec((nch, GLA_HEADS, LANES, LANES), lambda i: (i, 0, 0, 0))],
        out_shape=[jax.ShapeDtypeStruct((s, 512), BF), jax.ShapeDtypeStruct((s, 512), F32),
                   jax.ShapeDtypeStruct((s // GLA_CHUNK, GLA_HEADS, LANES, LANES), F32)],
        scratch_shapes=[pltpu.VMEM((GLA_HEADS, LANES, LANES), F32)],
        compiler_params=_params(("arbitrary",)),
    )(proj, proj, proj, proj, wdecp, bdecp, ggla)


def _gla_bwd(proj, dog, opre, sprev, wdecp, bdecp, ggla):
    s = proj.shape[0]
    tb = min(GLA_ROWS, s)
    nch = tb // GLA_CHUNK
    nb = s // tb

    def body(qk_ref, v_ref, gz_ref, ga_ref, dog_ref, opre_ref, sprev_ref, wd_ref, bd_ref, gg_ref,
             dqk_ref, dv_ref, dgz_ref, dga_ref, dwd_ref, dbd_ref, dgg_ref, dst_ref):
        @pl.when(pl.program_id(0) == 0)
        def _():
            dst_ref[...] = jnp.zeros_like(dst_ref)
            dwd_ref[...] = jnp.zeros_like(dwd_ref)
            dbd_ref[...] = jnp.zeros_like(dbd_ref)
            dgg_ref[...] = jnp.zeros_like(dgg_ref)

        lo_h, sgn, qsc, tril, triu = _gla_masks(tb)
        tril_b = jnp.where(tril, 1.0, 0.0).astype(BF)
        triu_b = jnp.where(triu, 1.0, 0.0).astype(BF)
        last_row = (lax.broadcasted_iota(jnp.int32, (tb, LANES), 0) % GLA_CHUNK) == GLA_CHUNK - 1
        wd, gg = wd_ref[...], gg_ref[...]
        ga = ga_ref[...]
        z2, e, f, qkd, kt, decs = _gla_block_common(qk_ref[...], ga, wd, bd_ref[...], tril_b, sgn, qsc)
        chunks = _chunks(tb)
        db_parts = []
        for h in range(GLA_HEADS):
            ls = slice(h * LANES, (h + 1) * LANES)
            e_h, f_h = e[:, ls], f[:, ls]
            a32 = jnp.where(lo_h, qkd[:, ls], 0.0)
            bm32 = jnp.where(lo_h, pltpu.roll(qkd[:, ls], 64, 1), 0.0)
            kt32 = jnp.where(lo_h, pltpu.roll(kt[:, ls], 64, 1), 0.0)
            a, bm, ktl = a32.astype(BF), bm32.astype(BF), kt32.astype(BF)
            vh = v_ref[:, ls].astype(BF)
            p = jnp.where(tril, _dot(a, bm, NT), 0.0).astype(BF)

            o = opre_ref[:, ls]
            gzh = gz_ref[:, ls]
            dogh = dog_ref[:, ls]
            r = lax.rsqrt(jnp.mean(o * o, axis=-1, keepdims=True) + RMS_EPS)
            ohat = o * r
            sg = _sigmoid(gzh)
            sil = gzh * sg
            g_h = gg[:, ls]
            dgz_ref[:, ls] = (dogh * (ohat * g_h) * (sg * (1.0 + gzh * (1.0 - sg)))).astype(dgz_ref.dtype)
            dn = dogh * sil * g_h
            dgg_ref[:, ls] += jnp.sum(dogh * sil * ohat, axis=0, keepdims=True)
            do = (r * (dn - ohat * jnp.mean(dn * ohat, axis=-1, keepdims=True))).astype(BF)

            dp = jnp.where(tril, _dot(do, vh, NT), 0.0).astype(BF)
            dv = _dot(p, do, TN)
            dqd = _dot(dp, bm)
            dkd = _dot(dp, a, TN)
            d = dst_ref[h]
            dv_s, dqd_s, dkt_s, ddec = [None] * nch, [None] * nch, [None] * nch, [None] * nch
            for c in reversed(range(nch)):
                rs = chunks[c]
                st = sprev_ref[c, h]
                d_b = d.astype(BF)
                dv_s[c] = _dot(ktl[rs], d_b, NT)
                dqd_s[c] = _dot(do[rs], st.astype(BF))
                dkt_s[c] = _dot(vh[rs], d_b)
                ddec[c] = jnp.sum(d * st, axis=0, keepdims=True)
                d = d * decs[c][:, ls] + _dot(do[rs], a[rs], TN)
            dst_ref[h] = d
            dv_ref[:, ls] = (dv + jnp.concatenate(dv_s, axis=0)).astype(dv_ref.dtype)
            dqd = dqd + jnp.concatenate(dqd_s, axis=0)
            dkt = jnp.concatenate(dkt_s, axis=0)

            dq = dqd * e_h * (GLA_DK ** -0.5)
            dk = dkd * pltpu.roll(e_h, 64, 1) + dkt * f_h
            dqk_ref[:, ls] = jnp.where(lo_h, dq, pltpu.roll(jnp.where(lo_h, dk, 0.0), 64, 1)).astype(dqk_ref.dtype)
            dkt_kt = dkt * kt32
            db = dqd * a32 - dkd * bm32 - dkt_kt
            dbl = [jnp.sum(dkt_kt[rs], axis=0, keepdims=True) + ddec[c] * decs[c][:, ls] for c, rs in enumerate(chunks)]
            db = jnp.where(last_row, db + _per_chunk_rows(dbl, LANES), db)
            db_parts.append(jnp.where(lo_h, db, 0.0))
        db2 = jnp.concatenate(db_parts, axis=1)
        dla = _tri_matmul(triu_b, db2)
        dz32 = dla * (1.0 / GLA_TAU) * _sigmoid(-z2)
        dz = dz32.astype(BF)
        dga_ref[...] = _dot(dz, wd, NT).astype(dga_ref.dtype)
        dwd_ref[...] += _dot(ga.astype(BF), dz, TN)
        dbd_ref[...] += jnp.sum(dz32, axis=0, keepdims=True)

    def col(width, off):
        return pl.BlockSpec((tb, width), lambda i: (nb - 1 - i, off // width))

    def rev(width):
        return pl.BlockSpec((tb, width), lambda i: (nb - 1 - i, 0))

    return pl.pallas_call(
        body, name="gla_bwd", grid=(nb,),
        in_specs=[col(512, OFF_QK), col(512, OFF_V), col(512, OFF_GZ), col(LANES, OFF_GA), rev(512), rev(512),
                  pl.BlockSpec((nch, GLA_HEADS, LANES, LANES), lambda i: (nb - 1 - i, 0, 0, 0)),
                  _full((LANES, 512)), _full((1, 512)), _full((1, 512))],
        out_specs=[rev(512), rev(512), rev(512), rev(LANES), _full((LANES, 512)), _full((1, 512)), _full((1, 512))],
        out_shape=[jax.ShapeDtypeStruct((s, 512), BF), jax.ShapeDtypeStruct((s, 512), BF),
                   jax.ShapeDtypeStruct((s, 512), BF), jax.ShapeDtypeStruct((s, LANES), BF),
                   jax.ShapeDtypeStruct((LANES, 512), F32), jax.ShapeDtypeStruct((1, 512), F32),
                   jax.ShapeDtypeStruct((1, 512), F32)],
        scratch_shapes=[pltpu.VMEM((GLA_HEADS, LANES, LANES), F32)],
        compiler_params=_params(("arbitrary",)),
    )(proj, proj, proj, proj, dog, opre, sprev, wdecp, bdecp, ggla)


_SWA_ROW_HEADS = (0, 2, 1, 3, 4, 6, 5, 7)


def _swa_masks():
    lo2 = lax.broadcasted_iota(jnp.int32, (2 * SWA_BLOCK, LANES), 1) < 64
    lane1 = lax.broadcasted_iota(jnp.int32, (SWA_BLOCK, LANES), 1)
    first_half = (lane1 % 64) < 32
    return lo2, lane1 < 64, first_half


def _swa_valid(block_index):
    shape = (SWA_HEADS * SWA_BLOCK, 2 * SWA_BLOCK)
    qi = lax.broadcasted_iota(jnp.int32, shape, 0) % SWA_BLOCK
    kj = lax.broadcasted_iota(jnp.int32, shape, 1)
    no_prev = jnp.where(block_index > 0, 0, 4 * SWA_BLOCK)
    return jnp.where(kj < SWA_BLOCK, kj - qi - no_prev, qi - kj + SWA_BLOCK + 1) > 0


def _kv_variants(t, lo2):
    tr = pltpu.roll(t, 64, 1)
    lo_v = [jnp.where(lo2, t, 0.0).astype(BF), jnp.where(lo2, tr, 0.0).astype(BF)]
    hi_v = [jnp.where(lo2, 0.0, tr).astype(BF), jnp.where(lo2, 0.0, t).astype(BF)]
    return lo_v, hi_v


def _swa_probs(qg, k_lo, k_hi, valid, sinks_ref):
    sc = jnp.concatenate([_dot(qg[0], k_lo[0], NT), _dot(qg[0], k_hi[0], NT),
                          _dot(qg[1], k_lo[1], NT), _dot(qg[1], k_hi[1], NT)], axis=0)
    sink = jnp.concatenate([jnp.full((SWA_BLOCK, 1), sinks_ref[0, hd], F32) for hd in _SWA_ROW_HEADS], axis=0)
    sc = jnp.where(valid, sc, -1e30)
    m = jnp.maximum(jnp.max(sc, axis=-1, keepdims=True), sink)
    ex = jnp.exp(sc - m)
    es = jnp.exp(sink - m)
    inv = 1.0 / (jnp.sum(ex, axis=-1, keepdims=True) + es)
    return ex * inv, es * inv


def _swa_queries(sq_ref, rows, cosb, sinb, first_half):
    qs = [_rope(sq_ref[rows, p * LANES:(p + 1) * LANES], cosb, sinb, first_half) * 0.125 for p in range(4)]
    return [jnp.concatenate(qs[0:2], axis=0), jnp.concatenate(qs[2:4], axis=0)]


def _swa_fwd(proj, cos, sin, sinks):
    s = proj.shape[0]
    nq = min(SWA_QBLOCKS, s // SWA_BLOCK)
    tq = nq * SWA_BLOCK

    def body(sq_ref, sz_ref, sk_ref, sv_ref, cos_ref, sin_ref, sinks_ref, os_ref, opre_ref, kprev, vprev):
        n = pl.program_id(0)

        @pl.when(n == 0)
        def _():
            kprev[...] = jnp.zeros_like(kprev)
            vprev[...] = jnp.zeros_like(vprev)

        lo2, _, first_half = _swa_masks()
        kp, vp = kprev[...], vprev[...]
        for j in range(nq):
            rows = slice(j * SWA_BLOCK, (j + 1) * SWA_BLOCK)
            cosb, sinb = cos_ref[rows, :], sin_ref[rows, :]
            kc = _rope(sk_ref[rows, :], cosb, sinb, first_half)
            vc = sv_ref[rows, :]
            k_lo, k_hi = _kv_variants(jnp.concatenate([kp, kc], axis=0), lo2)
            v_lo, v_hi = _kv_variants(jnp.concatenate([vp, vc], axis=0), lo2)
            qg = [q.astype(BF) for q in _swa_queries(sq_ref, rows, cosb, sinb, first_half)]
            pr, _ = _swa_probs(qg, k_lo, k_hi, _swa_valid(n * nq + j), sinks_ref)
            pr = pr.astype(BF)
            for g in range(2):
                og = _dot(pr[512 * g:512 * g + 256], v_lo[g]) + _dot(pr[512 * g + 256:512 * g + 512], v_hi[g])
                for i in range(2):
                    ls = slice((2 * g + i) * LANES, (2 * g + i + 1) * LANES)
                    o = og[i * SWA_BLOCK:(i + 1) * SWA_BLOCK]
                    sz = sz_ref[rows, ls]
                    opre_ref[rows, ls] = o
                    os_ref[rows, ls] = (o * (sz * _sigmoid(sz))).astype(os_ref.dtype)
            kp, vp = kc, vc
        kprev[...] = kp
        vprev[...] = vp

    def col(width, off):
        return pl.BlockSpec((tq, width), lambda i: (i, off // width))

    row = pl.BlockSpec((tq, LANES), lambda i: (i, 0))
    return pl.pallas_call(
        body, name="swa_fwd", grid=(s // tq,),
        in_specs=[col(512, OFF_SQ), col(512, OFF_SZ), col(LANES, OFF_SK), col(LANES, OFF_SV), row, row,
                  pl.BlockSpec(memory_space=pltpu.SMEM)],
        out_specs=[pl.BlockSpec((tq, 512), lambda i: (i, 0))] * 2,
        out_shape=[jax.ShapeDtypeStruct((s, 512), BF), jax.ShapeDtypeStruct((s, 512), F32)],
        scratch_shapes=[pltpu.VMEM((SWA_BLOCK, LANES), F32)] * 2,
        compiler_params=_params(("arbitrary",)),
    )(proj, proj, proj, proj, cos, sin, sinks)


def _swa_bwd(proj, dos, opre, cos, sin, sinks):
    s = proj.shape[0]
    nq = min(SWA_QBLOCKS, s // SWA_BLOCK)
    tq = nq * SWA_BLOCK

    def body(sq_ref, sz_ref, sk_ref, sv_ref, dos_ref, opre_ref, cos_ref, sin_ref, sinks_ref,
             dsq_ref, dsz_ref, dsk_ref, dsv_ref, dsink_ref, kprev, vprev, cprev, sprev):
        n = pl.program_id(0)

        @pl.when(n == 0)
        def _():
            kprev[...] = jnp.zeros_like(kprev)
            vprev[...] = jnp.zeros_like(vprev)
            cprev[...] = jnp.zeros_like(cprev)
            sprev[...] = jnp.zeros_like(sprev)
            for hd in range(SWA_HEADS):
                dsink_ref[0, hd] = 0.0

        lo2, lo1, first_half = _swa_masks()
        lo1s = jnp.concatenate([lo1, lo1], axis=0)

        def home(m0, m1):
            t0 = m0 + pltpu.roll(m0, 64, 1)
            t1 = m1 + pltpu.roll(m1, 64, 1)
            return jnp.where(lo2, t0, t1)

        kp, vp, cp_, sp_ = kprev[...], vprev[...], cprev[...], sprev[...]
        for j in range(nq):
            rows = slice(j * SWA_BLOCK, (j + 1) * SWA_BLOCK)
            blk = n * nq + j
            cosb, sinb = cos_ref[rows, :], sin_ref[rows, :]
            kc = _rope(sk_ref[rows, :], cosb, sinb, first_half)
            vc = sv_ref[rows, :]
            k_lo, k_hi = _kv_variants(jnp.concatenate([kp, kc], axis=0), lo2)
            v_lo, v_hi = _kv_variants(jnp.concatenate([vp, vc], axis=0), lo2)
            qg32 = _swa_queries(sq_ref, rows, cosb, sinb, first_half)
            qg = [q.astype(BF) for q in qg32]
            pr, ps = _swa_probs(qg, k_lo, k_hi, _swa_valid(blk), sinks_ref)

            dog32 = []
            for g in range(2):
                parts = []
                for i in range(2):
                    ls = slice((2 * g + i) * LANES, (2 * g + i + 1) * LANES)
                    sz = sz_ref[rows, ls]
                    sg = _sigmoid(sz)
                    dos_p = dos_ref[rows, ls]
                    dsz_ref[rows, ls] = (dos_p * opre_ref[rows, ls] * (sg * (1.0 + sz * (1.0 - sg)))).astype(dsz_ref.dtype)
                    parts.append(dos_p * (sz * sg))
                dog32.append(jnp.concatenate(parts, axis=0))
            dog = [t.astype(BF) for t in dog32]
            dpr = jnp.concatenate([_dot(dog[0], v_lo[0], NT), _dot(dog[0], v_hi[0], NT),
                                   _dot(dog[1], v_lo[1], NT), _dot(dog[1], v_hi[1], NT)], axis=0)
            rd = jnp.sum(pr * dpr, axis=-1, keepdims=True)
            ds = (pr * (dpr - rd)).astype(BF)
            prb = pr.astype(BF)
            sink_term = ps * rd
            for r, hd in enumerate(_SWA_ROW_HEADS):
                dsink_ref[0, hd] += -jnp.sum(sink_term[r * SWA_BLOCK:(r + 1) * SWA_BLOCK])

            dk_g, dv_g = [], []
            for g in range(2):
                ds_lo, ds_hi = ds[512 * g:512 * g + 256], ds[512 * g + 256:512 * g + 512]
                dq = _dot(ds_lo, k_lo[g]) + _dot(ds_hi, k_hi[g])
                for i in range(2):
                    ls = slice((2 * g + i) * LANES, (2 * g + i + 1) * LANES)
                    dsq_ref[rows, ls] = _rope_t(dq[i * SWA_BLOCK:(i + 1) * SWA_BLOCK] * 0.125, cosb, sinb,
                                                first_half).astype(dsq_ref.dtype)
                q_split = jnp.concatenate([jnp.where(lo1s, qg32[g], 0.0), jnp.where(lo1s, 0.0, qg32[g])], axis=0).astype(BF)
                do_split = jnp.concatenate([jnp.where(lo1s, dog32[g], 0.0), jnp.where(lo1s, 0.0, dog32[g])], axis=0).astype(BF)
                dk_g.append(_dot(ds[512 * g:512 * g + 512], q_split, TN))
                dv_g.append(_dot(prb[512 * g:512 * g + 512], do_split, TN))
            dk = home(dk_g[0], dk_g[1])
            dv = home(dv_g[0], dv_g[1])
            cur = pl.ds(pl.multiple_of(blk * SWA_BLOCK, SWA_BLOCK), SWA_BLOCK)
            dsk_ref[cur, :] = _rope_t(dk[SWA_BLOCK:], cosb, sinb, first_half)
            dsv_ref[cur, :] = dv[SWA_BLOCK:]
            dk_prev = _rope_t(dk[:SWA_BLOCK], cp_, sp_, first_half)
            dv_prev = dv[:SWA_BLOCK]
            if j == 0:
                @pl.when(n > 0)
                def _():
                    prv = pl.ds(pl.multiple_of((blk - 1) * SWA_BLOCK, SWA_BLOCK), SWA_BLOCK)
                    dsk_ref[prv, :] += dk_prev
                    dsv_ref[prv, :] += dv_prev
            else:
                prv = pl.ds(pl.multiple_of((blk - 1) * SWA_BLOCK, SWA_BLOCK), SWA_BLOCK)
                dsk_ref[prv, :] += dk_prev
                dsv_ref[prv, :] += dv_prev
            kp, vp, cp_, sp_ = kc, vc, cosb, sinb
        kprev[...] = kp
        vprev[...] = vp
        cprev[...] = cp_
        sprev[...] = sp_

    def col(width, off):
        return pl.BlockSpec((tq, width), lambda i: (i, off // width))

    row = pl.BlockSpec((tq, LANES), lambda i: (i, 0))
    wide = pl.BlockSpec((tq, 512), lambda i: (i, 0))
    return pl.pallas_call(
        body, name="swa_bwd", grid=(s // tq,),
        in_specs=[col(512, OFF_SQ), col(512, OFF_SZ), col(LANES, OFF_SK), col(LANES, OFF_SV), wide, wide, row, row,
                  pl.BlockSpec(memory_space=pltpu.SMEM)],
        out_specs=[wide, wide, _full((s, LANES)), _full((s, LANES)), pl.BlockSpec(memory_space=pltpu.SMEM)],
        out_shape=[jax.ShapeDtypeStruct((s, 512), BF), jax.ShapeDtypeStruct((s, 512), BF),
                   jax.ShapeDtypeStruct((s, LANES), F32), jax.ShapeDtypeStruct((s, LANES), F32),
                   jax.ShapeDtypeStruct((1, SWA_HEADS), F32)],
        scratch_shapes=[pltpu.VMEM((SWA_BLOCK, LANES), F32)] * 4,
        compiler_params=_params(("arbitrary",)),
    )(proj, proj, proj, proj, dos, opre, cos, sin, sinks)


def _outproj(og, osw, w_out, x2d, target, gate, g_final):
    s = x2d.shape[0]
    tm = min(512, s)

    def body(og_ref, os_ref, w_ref, x_ref, t_ref, gate_ref, gf_ref,
             dx2_ref, dog_ref, dos_ref, dw_ref, loss_ref, dgf_ref, dgate_ref):
        @pl.when(pl.program_id(0) == 0)
        def _():
            dw_ref[...] = jnp.zeros_like(dw_ref)
            loss_ref[...] = jnp.zeros_like(loss_ref)
            dgf_ref[...] = jnp.zeros_like(dgf_ref)
            dgate_ref[...] = jnp.zeros_like(dgate_ref)

        ogv, osv, w = og_ref[...], os_ref[...], w_ref[...]
        gate, gf = gate_ref[...], gf_ref[...]
        y = _dot(ogv, w[:512]) + _dot(osv, w[512:])
        x2 = x_ref[...] + gate * y
        r = lax.rsqrt(jnp.mean(x2 * x2, axis=-1, keepdims=True) + RMS_EPS)
        xn = x2 * r
        err = xn * gf - t_ref[...]
        loss_ref[...] += 0.5 * jnp.sum(jnp.mean(err * err, axis=-1, keepdims=True), axis=0, keepdims=True)
        dyf = err * (1.0 / D_MODEL)
        dgf_ref[...] += jnp.sum(dyf * xn, axis=0, keepdims=True)
        t = dyf * gf
        dx2 = r * (t - xn * jnp.mean(t * xn, axis=-1, keepdims=True))
        dx2_ref[...] = dx2
        dgate_ref[...] += jnp.sum(dx2 * y, axis=0, keepdims=True)
        dy = (dx2 * gate).astype(BF)
        dmix = _dot(dy, w, NT)
        dog_ref[...] = dmix[:, :512]
        dos_ref[...] = dmix[:, 512:]
        dw_ref[:512, :] += _dot(ogv, dy, TN)
        dw_ref[512:, :] += _dot(osv, dy, TN)

    half = pl.BlockSpec((tm, 512), lambda i: (i, 0))
    rowb = pl.BlockSpec((tm, D_MODEL), lambda i: (i, 0))
    vec = _full((1, D_MODEL))
    return pl.pallas_call(
        body, name="outproj", grid=(s // tm,),
        in_specs=[half, half, _full((D_MODEL, D_MODEL)), rowb, rowb, vec, vec],
        out_specs=[rowb, half, half, _full((D_MODEL, D_MODEL)), _full((1, 1)), vec, vec],
        out_shape=[jax.ShapeDtypeStruct((s, D_MODEL), F32), jax.ShapeDtypeStruct((s, 512), F32),
                   jax.ShapeDtypeStruct((s, 512), F32), jax.ShapeDtypeStruct((D_MODEL, D_MODEL), F32),
                   jax.ShapeDtypeStruct((1, 1), F32), jax.ShapeDtypeStruct((1, D_MODEL), F32),
                   jax.ShapeDtypeStruct((1, D_MODEL), F32)],
        compiler_params=_params(("arbitrary",)),
    )(og, osw, w_out, x2d, target, gate, g_final)


_PIECES = ((OFF_QK, 512), (OFF_V, 512), (OFF_GZ, 512), (OFF_SQ, 512), (OFF_SZ, 512),
           (OFF_SK, LANES), (OFF_SV, LANES), (OFF_GA, LANES))


def _inproj_bwd(x2d, shift, sc1p, g_norm, wpad, dx2, pieces):
    s = x2d.shape[0]
    tm = min(256, s)
    nsteps = s // tm

    def body(x_ref, sh_ref, sc_ref, g_ref, w_hbm, dx2_ref, *rest):
        piece_refs = rest[:len(_PIECES)]
        gx_ref, dw_hbm, dsh_ref, dsc_ref, dg_ref, w_vm, dw_vm, sem = rest[len(_PIECES):]
        i = pl.program_id(0)

        @pl.when(i == 0)
        def _():
            cp = pltpu.make_async_copy(w_hbm, w_vm, sem)
            cp.start()
            dw_vm[...] = jnp.zeros_like(dw_vm)
            dsh_ref[...] = jnp.zeros_like(dsh_ref)
            dsc_ref[...] = jnp.zeros_like(dsc_ref)
            dg_ref[...] = jnp.zeros_like(dg_ref)
            cp.wait()

        g, sc1p_v = g_ref[...], sc_ref[...]
        xn, r, h = _modnorm(x_ref[...], g, sc1p_v, sh_ref[...])
        hb = h.astype(BF)
        dh = None
        for (off, width), pr in zip(_PIECES, piece_refs):
            dp = pr[...].astype(BF)
            part = _dot(dp, w_vm[:, off:off + width], NT)
            dh = part if dh is None else dh + part
            dw_vm[:, off:off + width] += _dot(hb, dp, TN)
        dsh_ref[...] += jnp.sum(dh, axis=0, keepdims=True)
        dsc_ref[...] += jnp.sum(dh * (xn * g), axis=0, keepdims=True)
        dg_ref[...] += jnp.sum(dh * xn * sc1p_v, axis=0, keepdims=True)
        dxn = dh * g * sc1p_v
        gx_ref[...] = dx2_ref[...] + r * (dxn - xn * jnp.mean(dxn * xn, axis=-1, keepdims=True))

        @pl.when(i == nsteps - 1)
        def _():
            cp = pltpu.make_async_copy(dw_vm, dw_hbm, sem)
            cp.start()
            cp.wait()

    rowb = pl.BlockSpec((tm, D_MODEL), lambda i: (i, 0))
    vec = _full((1, D_MODEL))
    anyspec = pl.BlockSpec(memory_space=pl.ANY)
    piece_specs = [pl.BlockSpec((tm, width), lambda i: (i, 0)) for _, width in _PIECES]
    return pl.pallas_call(
        body, name="inproj_bwd", grid=(nsteps,),
        in_specs=[rowb, vec, vec, vec, anyspec, rowb] + piece_specs,
        out_specs=[rowb, anyspec, vec, vec, vec],
        out_shape=[jax.ShapeDtypeStruct((s, D_MODEL), F32), jax.ShapeDtypeStruct((D_MODEL, D_PAD), F32),
                   jax.ShapeDtypeStruct((1, D_MODEL), F32), jax.ShapeDtypeStruct((1, D_MODEL), F32),
                   jax.ShapeDtypeStruct((1, D_MODEL), F32)],
        scratch_shapes=[pltpu.VMEM((D_MODEL, D_PAD), BF), pltpu.VMEM((D_MODEL, D_PAD), F32), pltpu.SemaphoreType.DMA],
        compiler_params=_params(("arbitrary",)),
    )(x2d, shift, sc1p, g_norm, wpad, dx2, *pieces)


def _adam(w, g, m, v):
    m2 = ADAM_B1 * m + (1.0 - ADAM_B1) * g
    v2 = ADAM_B2 * v + (1.0 - ADAM_B2) * (g * g)
    m_hat = m2 / (1.0 - ADAM_B1 ** ADAM_STEP)
    v_hat = v2 / (1.0 - ADAM_B2 ** ADAM_STEP)
    delta = -ADAM_LR * (m_hat / (jnp.sqrt(v_hat) + ADAM_EPS) + ADAM_WD * w)
    return delta, m2, v2


def _adamw(w, g, m, v, name):
    rr, cc = w.shape
    tr = min(256, rr)

    def body(w_ref, g_ref, m_ref, v_ref, d_ref, m2_ref, v2_ref):
        d_ref[...], m2_ref[...], v2_ref[...] = _adam(w_ref[...], g_ref[...], m_ref[...], v_ref[...])

    blk = pl.BlockSpec((tr, cc), lambda i: (i, 0))
    return pl.pallas_call(
        body, name=name, grid=(rr // tr,), in_specs=[blk] * 4, out_specs=[blk] * 3,
        out_shape=[jax.ShapeDtypeStruct((rr, cc), F32)] * 3,
        compiler_params=_params(("arbitrary",)),
    )(w, g, m, v)


def _ada_update(c_all, dmod_cols, w, m, v):
    rr, cc = w.shape
    tr = min(256, rr)
    c_all = jnp.pad(c_all, ((0, 8), (0, 0)))
    dmod_cols = jnp.pad(dmod_cols, ((0, 8), (0, 0)))

    def body(c_ref, dm_ref, w_ref, m_ref, v_ref, g_ref, d_ref, m2_ref, v2_ref):
        cv = c_ref[...]
        sc = (cv * _sigmoid(cv)).astype(BF)
        g = _dot(sc, dm_ref[...].astype(BF), TN)
        g_ref[...] = g
        d_ref[...], m2_ref[...], v2_ref[...] = _adam(w_ref[...], g, m_ref[...], v_ref[...])

    blk = pl.BlockSpec((tr, cc), lambda i: (i, 0))
    return pl.pallas_call(
        body, name="ada_update", grid=(rr // tr,),
        in_specs=[pl.BlockSpec((16, tr), lambda i: (0, i)), _full((16, cc)), blk, blk, blk],
        out_specs=[blk] * 4, out_shape=[jax.ShapeDtypeStruct((rr, cc), F32)] * 4,
        compiler_params=_params(("arbitrary",)),
    )(c_all, dmod_cols, w, m, v)


def _small_update(parts, weights, moms, vels):
    n = len(weights)

    def body(*refs):
        p_refs, w_refs, m_refs, v_refs = refs[:n + 1], refs[n + 1:2 * n + 1], refs[2 * n + 1:3 * n + 1], refs[3 * n + 1:4 * n + 1]
        outs = refs[4 * n + 1:]
        for i in range(n):
            g = p_refs[i][0]
            for d in range(1, 8):
                g = g + p_refs[i][d]
            delta, m2, v2 = _adam(w_refs[i][...], g, m_refs[i][...], v_refs[i][...])
            outs[4 * i][...] = g
            outs[4 * i + 1][...] = delta
            outs[4 * i + 2][...] = m2
            outs[4 * i + 3][...] = v2
        tot = p_refs[n][0]
        for d in range(1, 8):
            tot = tot + p_refs[n][d]
        outs[4 * n][...] = tot

    out_shape = []
    for w in weights:
        out_shape += [jax.ShapeDtypeStruct(w.shape, F32)] * 4
    out_shape.append(jax.ShapeDtypeStruct(parts[n].shape[1:], F32))
    return pl.pallas_call(body, name="small_update", out_shape=out_shape, compiler_params=_params())(
        *parts, *weights, *moms, *vels)


def _pad_w_in(w):
    gq, gk = w[:, 0:256], w[:, 256:512]
    qk = []
    for h in range(GLA_HEADS):
        qk += [gq[:, 64 * h:64 * h + 64], gk[:, 64 * h:64 * h + 64]]
    gv, ga, gz = w[:, 512:1024], w[:, 1024:1040], w[:, 1040:1552]
    sq, sk, sv, sz = w[:, 1552:2064], w[:, 2064:2192], w[:, 2192:2320], w[:, 2320:2832]
    pad = jnp.zeros((w.shape[0], LANES - GLA_RANK), w.dtype)
    return jnp.concatenate(qk + [gv, gz, sq, sz, sk, sv, ga, pad], axis=1)


def _unpad_dw_in(d):
    qk = d[:, 0:512]
    gq = jnp.concatenate([qk[:, 128 * h:128 * h + 64] for h in range(GLA_HEADS)], axis=1)
    gk = jnp.concatenate([qk[:, 128 * h + 64:128 * h + 128] for h in range(GLA_HEADS)], axis=1)
    return jnp.concatenate([gq, gk, d[:, OFF_V:OFF_V + 512], d[:, OFF_GA:OFF_GA + GLA_RANK], d[:, OFF_GZ:OFF_GZ + 512],
                            d[:, OFF_SQ:OFF_SQ + 512], d[:, OFF_SK:OFF_SK + 128], d[:, OFF_SV:OFF_SV + 128],
                            d[:, OFF_SZ:OFF_SZ + 512]], axis=1)


def _dup_heads(t):
    parts = []
    for h in range(GLA_HEADS):
        parts += [t[..., 64 * h:64 * h + 64]] * 2
    return jnp.concatenate(parts, axis=-1)


def _rows8(a):
    flat = a.reshape(-1)
    rows = -(-flat.shape[0] // LANES)
    rows8 = -(-rows // 8) * 8
    flat = jnp.pad(flat, (0, rows8 * LANES - flat.shape[0]))
    return flat.reshape(rows8, LANES)


def kernel(x, c, positions, w_ada, b_ada, g_norm, w_in, w_decay, b_decay, g_gla_head, sinks, w_out, g_final, loss_target, m_w_ada, m_b_ada, m_g_norm, m_w_in, m_w_decay, m_b_decay, m_g_gla_head, m_sinks, m_w_out, m_g_final, v_w_ada, v_b_ada, v_g_norm, v_w_in, v_w_decay, v_b_decay, v_g_gla_head, v_sinks, v_w_out, v_g_final):
    ax, ay, ac = lax.axis_index("x"), lax.axis_index("y"), lax.axis_index("c")
    chip = 2 * ax + ay
    dev = 2 * chip + ac
    s = x.shape[1]
    x2d = x[0]
    target = loss_target[0]
    w_ada2, w_in2, w_out2, w_dec2 = w_ada[0], w_in[0], w_out[0], w_decay[0]
    ada_cols = w_ada2.shape[1]
    in_cols = w_in2.shape[1]
    out_rows = w_out2.shape[0]

    first = _all_gather(jnp.concatenate([c.reshape(8, LANES), w_dec2.reshape(8, LANES)], axis=0), "gather_c")
    first = first.reshape(8, 2, 8, LANES)
    c_all = first[:, 0].reshape(8, D_MODEL)
    w_dec_full = first[0::2, 1].reshape(4, GLA_RANK, 64).transpose(1, 0, 2).reshape(GLA_RANK, 256)
    b_shard = lax.dynamic_slice(b_ada, (0, chip * ada_cols), (1, ada_cols))
    mod_all = _all_gather(_ada_fwd(c_all, w_ada2, b_shard), "gather_mod")
    mod = mod_all.reshape(4, 2, 8, ada_cols)[:, 0]
    mod = lax.dynamic_slice(mod, (0, dev, 0), (4, 1, ada_cols)).reshape(1, 4 * ada_cols)
    shift, sc1p, gate = mod[:, :D_MODEL], 1.0 + mod[:, D_MODEL:2 * D_MODEL], mod[:, 2 * D_MODEL:]

    half_in = lax.dynamic_slice(w_in2, (ac * (D_MODEL // 2), 0), (D_MODEL // 2, in_cols)).astype(BF)
    w_in_all = _all_gather(half_in, "gather_w_in").reshape(4, D_MODEL, in_cols)
    wpad = _pad_w_in(w_in_all.transpose(1, 0, 2).reshape(D_MODEL, 4 * in_cols))
    half_out = lax.dynamic_slice(w_out2, (ac * (out_rows // 2), 0), (out_rows // 2, D_MODEL)).astype(BF)
    w_out_all = _all_gather(half_out, "gather_w_out")

    wdecp = jnp.pad(_dup_heads(w_dec_full), ((0, LANES - GLA_RANK), (0, 0))).astype(BF)
    bdecp = _dup_heads(b_decay)
    inv_freq = 1.0 / (ROPE_THETA ** (jnp.arange(0, 64, 2, dtype=F32) / 64))
    cos, sin = _rope_tables(positions.reshape(s, 1), jnp.tile(inv_freq, 4).reshape(1, LANES))

    proj = _inproj_fwd(x2d, shift, sc1p, g_norm, wpad)
    og, o_gla, sprev = _gla_fwd(proj, wdecp, bdecp, g_gla_head)
    osw, o_swa = _swa_fwd(proj, cos, sin, sinks)
    dx2, dog, dos, dw_out, loss_p, dgf, dgate = _outproj(og, osw, w_out_all, x2d, target, gate, g_final.reshape(1, D_MODEL))
    dsq, dsz, dsk, dsv, dsinks = _swa_bwd(proj, dos, o_swa, cos, sin, sinks)
    dqk, dv, dgz, dga, dwdp, dbdp, dgg = _gla_bwd(proj, dog, o_gla, sprev, wdecp, bdecp, g_gla_head)
    pieces = (dqk, dv, dgz, dsq, dsz, dsk, dsv, dga)
    gx, dwpad, dshift, dscale, dgn = _inproj_bwd(x2d, shift, sc1p, g_norm, wpad, dx2, pieces)

    dw_in_full = _unpad_dw_in(dwpad)
    g_w_in = _reduce_scatter(dw_in_full.reshape(D_MODEL, 4, in_cols).transpose(1, 0, 2), "reduce_w_in")
    g_w_out = _reduce_scatter(dw_out.reshape(4, out_rows, D_MODEL), "reduce_w_out")

    dwd = jnp.concatenate([dwdp[:GLA_RANK, 128 * h:128 * h + 64] for h in range(GLA_HEADS)], axis=1)
    dbd = jnp.concatenate([dbdp[:, 128 * h:128 * h + 64] for h in range(GLA_HEADS)], axis=1)
    segs = [jnp.concatenate([dshift, dscale, dgate], axis=1), dgn, dgf, dwd, dbd, dgg, dsinks, loss_p]
    packed = [_rows8(a) for a in segs]
    offs = [0]
    for a in packed:
        offs.append(offs[-1] + a.shape[0])
    small = _all_gather(jnp.concatenate(packed, axis=0), "gather_small").reshape(8, offs[-1], LANES)

    def seg(i, size):
        return small[:, offs[i]:offs[i + 1]].reshape(8, -1)[:, :size]

    dmod_all = seg(0, 3 * D_MODEL)
    dwd_all = lax.dynamic_slice(seg(3, GLA_RANK * 256).reshape(8, GLA_RANK, 256), (0, 0, chip * 64), (8, GLA_RANK, 64))
    parts = [dmod_all.reshape(8, 1, 3 * D_MODEL), seg(1, D_MODEL).reshape(8, 1, D_MODEL), dwd_all,
             seg(4, 256).reshape(8, 1, 256), seg(5, 512).reshape(8, 1, 512), seg(6, SWA_HEADS).reshape(8, 1, SWA_HEADS),
             seg(2, D_MODEL).reshape(8, 1, D_MODEL), seg(7, LANES).reshape(8, 1, LANES)]
    smalls = _small_update(
        parts,
        [b_ada, g_norm, w_dec2, b_decay, g_gla_head, sinks, g_final.reshape(1, D_MODEL)],
        [m_b_ada, m_g_norm, m_w_decay[0], m_b_decay, m_g_gla_head, m_sinks, m_g_final.reshape(1, D_MODEL)],
        [v_b_ada, v_g_norm, v_w_decay[0], v_b_decay, v_g_gla_head, v_sinks, v_g_final.reshape(1, D_MODEL)])
    (g_b_ada, d_b_ada, nm_b_ada, nv_b_ada, g_gn, d_gn, nm_gn, nv_gn, g_wd, d_wd, nm_wd, nv_wd,
     g_bd, d_bd, nm_bd, nv_bd, g_gg, d_gg, nm_gg, nv_gg, g_sk, d_sk, nm_sk, nv_sk,
     g_gf, d_gf, nm_gf, nv_gf, loss_row) = smalls
    loss = loss_row[0, 0]

    dmod_cols = lax.dynamic_slice(dmod_all, (0, chip * ada_cols), (8, ada_cols))
    g_w_ada, d_w_ada, nm_w_ada, nv_w_ada = _ada_update(c_all, dmod_cols, w_ada2, m_w_ada[0], v_w_ada[0])
    d_w_in, nm_w_in, nv_w_in = _adamw(w_in2, g_w_in, m_w_in[0], v_w_in[0], "adamw_w_in")
    d_w_out, nm_w_out, nv_w_out = _adamw(w_out2, g_w_out, m_w_out[0], v_w_out[0], "adamw_w_out")

    flat = lambda a: a.reshape(D_MODEL)
    grads = [g_w_ada[None], g_b_ada, g_gn, g_w_in[None], g_wd[None], g_bd, g_gg, g_sk, g_w_out[None], flat(g_gf)]
    deltas = [d_w_ada[None], d_b_ada, d_gn, d_w_in[None], d_wd[None], d_bd, d_gg, d_sk, d_w_out[None], flat(d_gf)]
    new_m = [nm_w_ada[None], nm_b_ada, nm_gn, nm_w_in[None], nm_wd[None], nm_bd, nm_gg, nm_sk, nm_w_out[None], flat(nm_gf)]
    new_v = [nv_w_ada[None], nv_b_ada, nv_gn, nv_w_in[None], nv_wd[None], nv_bd, nv_gg, nv_sk, nv_w_out[None], flat(nv_gf)]
    return (loss, gx[None], *grads, *deltas, *new_m, *new_v)
```

```python
import jax
import jax.numpy as jnp
from jax import lax
from jax.experimental import pallas as pl
from jax.experimental.pallas import tpu as pltpu

F32 = jnp.float32
BF = jnp.bfloat16

D_MODEL = 1024
GLA_HEADS = 4
GLA_DK = 64
GLA_CHUNK = 64
GLA_RANK = 16
GLA_TAU = 16.0
GLA_ROWS = 256
SWA_HEADS = 8
SWA_BLOCK = 128
SWA_QBLOCKS = 2
RMS_EPS = 1e-6
ROPE_THETA = 10000.0

OFF_QK, OFF_V, OFF_GZ, OFF_SQ, OFF_SZ, OFF_SK, OFF_SV, OFF_GA = 0, 512, 1024, 1536, 2048, 2560, 2688, 2816
D_PAD = 2944
LANES = 128
VMEM_LIMIT = 56 * 1024 * 1024

ADAM_LR, ADAM_B1, ADAM_B2, ADAM_EPS, ADAM_WD, ADAM_STEP = 0.001, 0.9, 0.999, 1e-08, 0.01, 10

NT = (((1,), (1,)), ((), ()))
TN = (((0,), (0,)), ((), ()))
MESH = pl.DeviceIdType.MESH


def _dot(a, b, dims=None):
    if dims is None:
        return jnp.dot(a, b, preferred_element_type=F32)
    return lax.dot_general(a, b, dims, preferred_element_type=F32)


def _sigmoid(x):
    return 1.0 / (1.0 + jnp.exp(-x))


def _params(sem=None):
    return pltpu.CompilerParams(dimension_semantics=sem, vmem_limit_bytes=VMEM_LIMIT)


def _full(shape):
    return pl.BlockSpec(shape, lambda i: (0,) * len(shape))


def _all_gather(block, name):
    def body(x_ref, out_ref, send_sems, recv_sems, local_sem):
        x, y, c = lax.axis_index("x"), lax.axis_index("y"), lax.axis_index("c")
        me, sibling = (x, y, c), (x, y, 1 - c)
        chips = [(1 - x, y), (x, 1 - y), (1 - x, 1 - y)]

        def rows(px, py, pc):
            return out_ref.at[4 * px + 2 * py + pc]

        def copy(k, blk, to, src=None):
            return pltpu.make_async_remote_copy(
                src_ref=rows(*blk) if src is None else src, dst_ref=rows(*blk),
                send_sem=send_sems.at[k], recv_sem=recv_sems.at[k], device_id=to, device_id_type=MESH)

        mine = pltpu.make_async_copy(x_ref, rows(*me), local_sem)
        mine.start()
        first = [copy(0, me, sibling, src=x_ref)]
        first += [copy(1 + j, me, (*chip, c), src=x_ref) for j, chip in enumerate(chips)]
        for cp in first:
            cp.start()
        passed = [copy(4 + j, (*chip, c), sibling) for j, chip in enumerate(chips)]
        for j, chip in enumerate(chips):
            copy(1 + j, (*chip, c), me).wait_recv()
            passed[j].start()
        copy(0, sibling, me).wait_recv()
        for j, chip in enumerate(chips):
            copy(4 + j, (*chip, 1 - c), me).wait_recv()
        for cp in first + passed:
            cp.wait_send()
        mine.wait()

    return pl.pallas_call(
        body, name=name,
        out_shape=jax.ShapeDtypeStruct((8,) + block.shape, block.dtype),
        in_specs=[pl.BlockSpec(memory_space=pltpu.VMEM)],
        out_specs=pl.BlockSpec(memory_space=pltpu.VMEM),
        scratch_shapes=[pltpu.SemaphoreType.DMA((7,)), pltpu.SemaphoreType.DMA((7,)), pltpu.SemaphoreType.DMA],
        compiler_params=pltpu.CompilerParams(vmem_limit_bytes=VMEM_LIMIT),
    )(block)


def _reduce_scatter(parts, name):
    _, rr, cc = parts.shape
    c2 = cc // 2

    def body(p_hbm, out_ref, acc_ref, own_ref, send_ref, land_ref, res_ref, send_sems, recv_sems, local_sems):
        x, y, c = lax.axis_index("x"), lax.axis_index("y"), lax.axis_index("c")
        sibling = (x, y, 1 - c)
        chips = [(1 - x, y), (x, 1 - y), (1 - x, 1 - y)]
        mine = pl.ds(pl.multiple_of(c * c2, c2), c2)
        other = pl.ds(pl.multiple_of((1 - c) * c2, c2), c2)

        own = pltpu.make_async_copy(p_hbm.at[:, :, mine], own_ref, local_sems.at[0])
        own.start()
        swap = pltpu.make_async_remote_copy(
            src_ref=p_hbm.at[:, :, other], dst_ref=acc_ref, send_sem=send_sems.at[0], recv_sem=recv_sems.at[0],
            device_id=sibling, device_id_type=MESH)
        swap.start()
        own.wait()
        swap.wait()
        for j in range(4):
            acc_ref[j] = acc_ref[j] + own_ref[j]

        sends = []
        for k, (tx, ty) in enumerate(chips):
            send_ref[k] = acc_ref[2 * tx + ty].astype(send_ref.dtype)
            cp = pltpu.make_async_remote_copy(
                src_ref=send_ref.at[k], dst_ref=land_ref.at[k], send_sem=send_sems.at[1 + k],
                recv_sem=recv_sems.at[1 + k], device_id=(tx, ty, c), device_id_type=MESH)
            cp.start()
            sends.append(cp)
        for cp in sends:
            cp.wait_recv()
        total = acc_ref[2 * x + y]
        for k in range(3):
            total = total + land_ref[k].astype(F32)
        res_ref[...] = total
        for cp in sends:
            cp.wait_send()

        put = pltpu.make_async_copy(res_ref, out_ref.at[:, mine], local_sems.at[1])
        put.start()
        share = pltpu.make_async_remote_copy(
            src_ref=res_ref, dst_ref=out_ref.at[:, mine], send_sem=send_sems.at[4],
            recv_sem=recv_sems.at[4], device_id=sibling, device_id_type=MESH)
        share.start()
        put.wait()
        share.wait()

    return pl.pallas_call(
        body, name=name,
        out_shape=jax.ShapeDtypeStruct((rr, cc), F32),
        in_specs=[pl.BlockSpec(memory_space=pl.ANY)],
        out_specs=pl.BlockSpec(memory_space=pltpu.VMEM),
        scratch_shapes=[pltpu.VMEM((4, rr, c2), F32), pltpu.VMEM((4, rr, c2), F32), pltpu.VMEM((3, rr, c2), BF),
                        pltpu.VMEM((3, rr, c2), BF), pltpu.VMEM((rr, c2), F32),
                        pltpu.SemaphoreType.DMA((5,)), pltpu.SemaphoreType.DMA((5,)), pltpu.SemaphoreType.DMA((2,))],
        compiler_params=pltpu.CompilerParams(vmem_limit_bytes=VMEM_LIMIT),
    )(parts)


def _ada_fwd(c_all, w_ada, b_shard):
    def body(c_ref, w_ref, b_ref, o_ref):
        cv = c_ref[...]
        sc = (cv * _sigmoid(cv)).astype(BF)
        o_ref[...] = _dot(sc, w_ref[...].astype(BF)) + b_ref[...]

    return pl.pallas_call(
        body, name="ada_fwd", out_shape=jax.ShapeDtypeStruct((8, w_ada.shape[1]), F32),
        compiler_params=_params(),
    )(c_all, w_ada, b_shard)


def _rope_tables(pos_col, inv_freq):
    s = pos_col.shape[0]
    tm = min(1024, s)

    def body(p_ref, f_ref, cos_ref, sin_ref):
        ang = p_ref[...].astype(F32) * f_ref[...]
        lane = lax.broadcasted_iota(jnp.int32, ang.shape, 1)
        cos_ref[...] = jnp.cos(ang)
        sn = jnp.sin(ang)
        sin_ref[...] = jnp.where((lane % 64) < 32, -sn, sn)

    return pl.pallas_call(
        body, name="rope_tables", grid=(s // tm,),
        in_specs=[pl.BlockSpec((tm, 1), lambda i: (i, 0)), _full((1, LANES))],
        out_specs=[pl.BlockSpec((tm, LANES), lambda i: (i, 0))] * 2,
        out_shape=[jax.ShapeDtypeStruct((s, LANES), F32)] * 2,
        compiler_params=_params(("arbitrary",)),
    )(pos_col, inv_freq)


def _rope(t, cosb, sinb, first_half):
    partner = jnp.where(first_half, pltpu.roll(t, 96, 1), pltpu.roll(t, 32, 1))
    return t * cosb + partner * sinb


def _rope_t(g, cosb, sinb, first_half):
    gs = g * sinb
    partner = jnp.where(first_half, pltpu.roll(gs, 96, 1), pltpu.roll(gs, 32, 1))
    return g * cosb + partner


def _modnorm(x, g, sc1p, shift):
    r = lax.rsqrt(jnp.mean(x * x, axis=-1, keepdims=True) + RMS_EPS)
    xn = x * r
    return xn, r, (xn * g) * sc1p + shift


def _inproj_fwd(x2d, shift, sc1p, g_norm, wpad_t):
    s = x2d.shape[0]
    tm = min(512, s)

    def body(x_ref, sh_ref, sc_ref, g_ref, w_ref, o_ref):
        _, _, h = _modnorm(x_ref[...], g_ref[...], sc_ref[...], sh_ref[...])
        o_ref[...] = _dot(h.astype(BF), w_ref[...], NT)

    vec = _full((1, D_MODEL))
    return pl.pallas_call(
        body, name="inproj_fwd", grid=(s // tm,),
        in_specs=[pl.BlockSpec((tm, D_MODEL), lambda i: (i, 0)), vec, vec, vec, _full((D_PAD, D_MODEL))],
        out_specs=pl.BlockSpec((tm, D_PAD), lambda i: (i, 0)),
        out_shape=jax.ShapeDtypeStruct((s, D_PAD), F32),
        compiler_params=_params(("arbitrary",)),
    )(x2d, shift, sc1p, g_norm, wpad_t)


def _split3(a):
    hi = a.astype(BF)
    r1 = a - hi.astype(F32)
    mid = r1.astype(BF)
    lo = (r1 - mid.astype(F32)).astype(BF)
    return hi, mid, lo


def _tri_matmul(tri, a):
    hi, mid, lo = _split3(a)
    return _dot(tri, hi) + _dot(tri, mid) + _dot(tri, lo)


def _chunks(tb):
    return [slice(c * GLA_CHUNK, (c + 1) * GLA_CHUNK) for c in range(tb // GLA_CHUNK)]


def _per_chunk_rows(rows, width):
    return jnp.concatenate([jnp.broadcast_to(r, (GLA_CHUNK, width)) for r in rows], axis=0)


def _gla_masks(tb):
    lane = lax.broadcasted_iota(jnp.int32, (1, 512), 1)
    lo512 = (lane % LANES) < GLA_DK
    sgn = jnp.where(lo512, 1.0, -1.0).astype(F32)
    qsc = jnp.where(lo512, GLA_DK ** -0.5, 1.0).astype(F32)
    lo_h = lax.broadcasted_iota(jnp.int32, (tb, LANES), 1) < GLA_DK
    row = lax.broadcasted_iota(jnp.int32, (tb, tb), 0)
    col = lax.broadcasted_iota(jnp.int32, (tb, tb), 1)
    same = (row // GLA_CHUNK) == (col // GLA_CHUNK)
    tril = jnp.where(same, row - col, -1) >= 0
    triu = jnp.where(same, col - row, -1) >= 0
    return lo_h, sgn, qsc, tril, triu


def _gla_block_common(qk, ga, wd, bd, tril_b, sgn, qsc):
    tb = qk.shape[0]
    z2 = _dot(ga.astype(BF), wd) + bd
    la2 = (jnp.minimum(z2, 0.0) - jnp.log1p(jnp.exp(-jnp.abs(z2)))) * (1.0 / GLA_TAU)
    b2 = _tri_matmul(tril_b, la2)
    bls = [b2[rs.stop - 1:rs.stop, :] for rs in _chunks(tb)]
    e = jnp.exp(b2 * sgn)
    f = jnp.exp(_per_chunk_rows(bls, 512) - b2)
    qkd = qk * e * qsc
    kt = qk * f
    decs = [jnp.exp(bl) for bl in bls]
    return z2, e, f, qkd, kt, decs


def _gla_fwd(proj, wdecp, bdecp, ggla):
    s = proj.shape[0]
    tb = min(GLA_ROWS, s)
    nch = tb // GLA_CHUNK

    def body(qk_ref, v_ref, gz_ref, ga_ref, wd_ref, bd_ref, gg_ref, og_ref, opre_ref, sprev_ref, st_ref):
        @pl.when(pl.program_id(0) == 0)
        def _():
            st_ref[...] = jnp.zeros_like(st_ref)

        lo_h, sgn, qsc, tril, _ = _gla_masks(tb)
        tril_b = jnp.where(tril, 1.0, 0.0).astype(BF)
        gg = gg_ref[...]
        _, _, _, qkd, kt, decs = _gla_block_common(qk_ref[...], ga_ref[...], wd_ref[...], bd_ref[...], tril_b, sgn, qsc)
        for h in range(GLA_HEADS):
            ls = slice(h * LANES, (h + 1) * LANES)
            a = jnp.where(lo_h, qkd[:, ls], 0.0).astype(BF)
            bm = jnp.where(lo_h, pltpu.roll(qkd[:, ls], 64, 1), 0.0).astype(BF)
            ktl = jnp.where(lo_h, pltpu.roll(kt[:, ls], 64, 1), 0.0).astype(BF)
            vh = v_ref[:, ls].astype(BF)
            p = jnp.where(tril, _dot(a, bm, NT), 0.0).astype(BF)
            o = _dot(p, vh)
            st = st_ref[h]
            inter = []
            for c, rs in enumerate(_chunks(tb)):
                sprev_ref[c, h] = st
                inter.append(_dot(a[rs], st.astype(BF), NT))
                st = st * decs[c][:, ls] + _dot(vh[rs], ktl[rs], TN)
            st_ref[h] = st
            o = o + jnp.concatenate(inter, axis=0)
            r = lax.rsqrt(jnp.mean(o * o, axis=-1, keepdims=True) + RMS_EPS)
            gzh = gz_ref[:, ls]
            opre_ref[:, ls] = o
            og_ref[:, ls] = (((o * r) * gg[:, ls]) * (gzh * _sigmoid(gzh))).astype(og_ref.dtype)

    def col(width, off):
        return pl.BlockSpec((tb, width), lambda i: (i, off // width))

    return pl.pallas_call(
        body, name="gla_fwd", grid=(s // tb,),
        in_specs=[col(512, OFF_QK), col(512, OFF_V), col(512, OFF_GZ), col(LANES, OFF_GA),
                  _full((LANES, 512)), _full((1, 512)), _full((1, 512))],
        out_specs=[pl.BlockSpec((tb, 512), lambda i: (i, 0)), pl.BlockSpec((tb, 512), lambda i: (i, 0)),
                   pl.BlockSpec((nch, GLA_HEADS, LANES, LANES), lambda i: (i, 0, 0, 0))],
        out_shape=[jax.ShapeDtypeStruct((s, 512), BF), jax.ShapeDtypeStruct((s, 512), F32),
                   jax.ShapeDtypeStruct((s // GLA_CHUNK, GLA_HEADS, LANES, LANES), F32)],
        scratch_shapes=[pltpu.VMEM((GLA_HEADS, LANES, LANES), F32)],
        compiler_params=_params(("arbitrary",)),
    )(proj, proj, proj, proj, wdecp, bdecp, ggla)


def _gla_bwd(proj, dog, opre, sprev, wdecp, bdecp, ggla):
    s = proj.shape[0]
    tb = min(GLA_ROWS, s)
    nch = tb // GLA_CHUNK
    nb = s // tb

    def body(qk_ref, v_ref, gz_ref, ga_ref, dog_ref, opre_ref, sprev_ref, wd_ref, bd_ref, gg_ref,
             dqk_ref, dv_ref, dgz_ref, dga_ref, dwd_ref, dbd_ref, dgg_ref, dst_ref):
        @pl.when(pl.program_id(0) == 0)
        def _():
            dst_ref[...] = jnp.zeros_like(dst_ref)
            dwd_ref[...] = jnp.zeros_like(dwd_ref)
            dbd_ref[...] = jnp.zeros_like(dbd_ref)
            dgg_ref[...] = jnp.zeros_like(dgg_ref)

        lo_h, sgn, qsc, tril, triu = _gla_masks(tb)
        tril_b = jnp.where(tril, 1.0, 0.0).astype(BF)
        triu_b = jnp.where(triu, 1.0, 0.0).astype(BF)
        last_row = (lax.broadcasted_iota(jnp.int32, (tb, LANES), 0) % GLA_CHUNK) == GLA_CHUNK - 1
        wd, gg = wd_ref[...], gg_ref[...]
        ga = ga_ref[...]
        z2, e, f, qkd, kt, decs = _gla_block_common(qk_ref[...], ga, wd, bd_ref[...], tril_b, sgn, qsc)
        chunks = _chunks(tb)
        db_parts = []
        for h in range(GLA_HEADS):
            ls = slice(h * LANES, (h + 1) * LANES)
            e_h, f_h = e[:, ls], f[:, ls]
            a32 = jnp.where(lo_h, qkd[:, ls], 0.0)
            bm32 = jnp.where(lo_h, pltpu.roll(qkd[:, ls], 64, 1), 0.0)
            kt32 = jnp.where(lo_h, pltpu.roll(kt[:, ls], 64, 1), 0.0)
            a, bm, ktl = a32.astype(BF), bm32.astype(BF), kt32.astype(BF)
            vh = v_ref[:, ls].astype(BF)
            p = jnp.where(tril, _dot(a, bm, NT), 0.0).astype(BF)

            o = opre_ref[:, ls]
            gzh = gz_ref[:, ls]
            dogh = dog_ref[:, ls]
            r = lax.rsqrt(jnp.mean(o * o, axis=-1, keepdims=True) + RMS_EPS)
            ohat = o * r
            sg = _sigmoid(gzh)
            sil = gzh * sg
            g_h = gg[:, ls]
            dgz_ref[:, ls] = (dogh * (ohat * g_h) * (sg * (1.0 + gzh * (1.0 - sg)))).astype(dgz_ref.dtype)
            dn = dogh * sil * g_h
            dgg_ref[:, ls] += jnp.sum(dogh * sil * ohat, axis=0, keepdims=True)
            do = (r * (dn - ohat * jnp.mean(dn * ohat, axis=-1, keepdims=True))).astype(BF)

            dp = jnp.where(tril, _dot(do, vh, NT), 0.0).astype(BF)
            dv = _dot(p, do, TN)
            dqd = _dot(dp, bm)
            dkd = _dot(dp, a, TN)
            d = dst_ref[h]
            dv_s, dqd_s, dkt_s, ddec = [None] * nch, [None] * nch, [None] * nch, [None] * nch
            for c in reversed(range(nch)):
                rs = chunks[c]
                st = sprev_ref[c, h]
                d_b = d.astype(BF)
                dv_s[c] = _dot(ktl[rs], d_b, NT)
                dqd_s[c] = _dot(do[rs], st.astype(BF))
                dkt_s[c] = _dot(vh[rs], d_b)
                ddec[c] = jnp.sum(d * st, axis=0, keepdims=True)
                d = d * decs[c][:, ls] + _dot(do[rs], a[rs], TN)
            dst_ref[h] = d
            dv_ref[:, ls] = (dv + jnp.concatenate(dv_s, axis=0)).astype(dv_ref.dtype)
            dqd = dqd + jnp.concatenate(dqd_s, axis=0)
            dkt = jnp.concatenate(dkt_s, axis=0)

            dq = dqd * e_h * (GLA_DK ** -0.5)
            dk = dkd * pltpu.roll(e_h, 64, 1) + dkt * f_h
            dqk_ref[:, ls] = jnp.where(lo_h, dq, pltpu.roll(jnp.where(lo_h, dk, 0.0), 64, 1)).astype(dqk_ref.dtype)
            dkt_kt = dkt * kt32
            db = dqd * a32 - dkd * bm32 - dkt_kt
            dbl = [jnp.sum(dkt_kt[rs], axis=0, keepdims=True) + ddec[c] * decs[c][:, ls] for c, rs in enumerate(chunks)]
            db = jnp.where(last_row, db + _per_chunk_rows(dbl, LANES), db)
            db_parts.append(jnp.where(lo_h, db, 0.0))
        db2 = jnp.concatenate(db_parts, axis=1)
        dla = _tri_matmul(triu_b, db2)
        dz32 = dla * (1.0 / GLA_TAU) * _sigmoid(-z2)
        dz = dz32.astype(BF)
        dga_ref[...] = _dot(dz, wd, NT).astype(dga_ref.dtype)
        dwd_ref[...] += _dot(ga.astype(BF), dz, TN)
        dbd_ref[...] += jnp.sum(dz32, axis=0, keepdims=True)

    def col(width, off):
        return pl.BlockSpec((tb, width), lambda i: (nb - 1 - i, off // width))

    def rev(width):
        return pl.BlockSpec((tb, width), lambda i: (nb - 1 - i, 0))

    return pl.pallas_call(
        body, name="gla_bwd", grid=(nb,),
        in_specs=[col(512, OFF_QK), col(512, OFF_V), col(512, OFF_GZ), col(LANES, OFF_GA), rev(512), rev(512),
                  pl.BlockSpec((nch, GLA_HEADS, LANES, LANES), lambda i: (nb - 1 - i, 0, 0, 0)),
                  _full((LANES, 512)), _full((1, 512)), _full((1, 512))],
        out_specs=[rev(512), rev(512), rev(512), rev(LANES), _full((LANES, 512)), _full((1, 512)), _full((1, 512))],
        out_shape=[jax.ShapeDtypeStruct((s, 512), BF), jax.ShapeDtypeStruct((s, 512), BF),
                   jax.ShapeDtypeStruct((s, 512), BF), jax.ShapeDtypeStruct((s, LANES), BF),
                   jax.ShapeDtypeStruct((LANES, 512), F32), jax.ShapeDtypeStruct((1, 512), F32),
                   jax.ShapeDtypeStruct((1, 512), F32)],
        scratch_shapes=[pltpu.VMEM((GLA_HEADS, LANES, LANES), F32)],
        compiler_params=_params(("arbitrary",)),
    )(proj, proj, proj, proj, dog, opre, sprev, wdecp, bdecp, ggla)


_SWA_ROW_HEADS = (0, 2, 1, 3, 4, 6, 5, 7)


def _swa_masks():
    lo2 = lax.broadcasted_iota(jnp.int32, (2 * SWA_BLOCK, LANES), 1) < 64
    lane1 = lax.broadcasted_iota(jnp.int32, (SWA_BLOCK, LANES), 1)
    first_half = (lane1 % 64) < 32
    return lo2, lane1 < 64, first_half


def _swa_valid(block_index):
    shape = (SWA_HEADS * SWA_BLOCK, 2 * SWA_BLOCK)
    qi = lax.broadcasted_iota(jnp.int32, shape, 0) % SWA_BLOCK
    kj = lax.broadcasted_iota(jnp.int32, shape, 1)
    no_prev = jnp.where(block_index > 0, 0, 4 * SWA_BLOCK)
    return jnp.where(kj < SWA_BLOCK, kj - qi - no_prev, qi - kj + SWA_BLOCK + 1) > 0


def _kv_variants(t, lo2):
    tr = pltpu.roll(t, 64, 1)
    lo_v = [jnp.where(lo2, t, 0.0).astype(BF), jnp.where(lo2, tr, 0.0).astype(BF)]
    hi_v = [jnp.where(lo2, 0.0, tr).astype(BF), jnp.where(lo2, 0.0, t).astype(BF)]
    return lo_v, hi_v


def _swa_probs(qg, k_lo, k_hi, valid, sinks_ref):
    sc = jnp.concatenate([_dot(qg[0], k_lo[0], NT), _dot(qg[0], k_hi[0], NT),
                          _dot(qg[1], k_lo[1], NT), _dot(qg[1], k_hi[1], NT)], axis=0)
    sink = jnp.concatenate([jnp.full((SWA_BLOCK, 1), sinks_ref[0, hd], F32) for hd in _SWA_ROW_HEADS], axis=0)
    sc = jnp.where(valid, sc, -1e30)
    m = jnp.maximum(jnp.max(sc, axis=-1, keepdims=True), sink)
    ex = jnp.exp(sc - m)
    es = jnp.exp(sink - m)
    inv = 1.0 / (jnp.sum(ex, axis=-1, keepdims=True) + es)
    return ex * inv, es * inv


def _swa_queries(sq_ref, rows, cosb, sinb, first_half):
    qs = [_rope(sq_ref[rows, p * LANES:(p + 1) * LANES], cosb, sinb, first_half) * 0.125 for p in range(4)]
    return [jnp.concatenate(qs[0:2], axis=0), jnp.concatenate(qs[2:4], axis=0)]


def _swa_fwd(proj, cos, sin, sinks):
    s = proj.shape[0]
    nq = min(SWA_QBLOCKS, s // SWA_BLOCK)
    tq = nq * SWA_BLOCK

    def body(sq_ref, sz_ref, sk_ref, sv_ref, cos_ref, sin_ref, sinks_ref, os_ref, opre_ref, kprev, vprev):
        n = pl.program_id(0)

        @pl.when(n == 0)
        def _():
            kprev[...] = jnp.zeros_like(kprev)
            vprev[...] = jnp.zeros_like(vprev)

        lo2, _, first_half = _swa_masks()
        kp, vp = kprev[...], vprev[...]
        for j in range(nq):
            rows = slice(j * SWA_BLOCK, (j + 1) * SWA_BLOCK)
            cosb, sinb = cos_ref[rows, :], sin_ref[rows, :]
            kc = _rope(sk_ref[rows, :], cosb, sinb, first_half)
            vc = sv_ref[rows, :]
            k_lo, k_hi = _kv_variants(jnp.concatenate([kp, kc], axis=0), lo2)
            v_lo, v_hi = _kv_variants(jnp.concatenate([vp, vc], axis=0), lo2)
            qg = [q.astype(BF) for q in _swa_queries(sq_ref, rows, cosb, sinb, first_half)]
            pr, _ = _swa_probs(qg, k_lo, k_hi, _swa_valid(n * nq + j), sinks_ref)
            pr = pr.astype(BF)
            for g in range(2):
                og = _dot(pr[512 * g:512 * g + 256], v_lo[g]) + _dot(pr[512 * g + 256:512 * g + 512], v_hi[g])
                for i in range(2):
                    ls = slice((2 * g + i) * LANES, (2 * g + i + 1) * LANES)
                    o = og[i * SWA_BLOCK:(i + 1) * SWA_BLOCK]
                    sz = sz_ref[rows, ls]
                    opre_ref[rows, ls] = o
                    os_ref[rows, ls] = (o * (sz * _sigmoid(sz))).astype(os_ref.dtype)
            kp, vp = kc, vc
        kprev[...] = kp
        vprev[...] = vp

    def col(width, off):
        return pl.BlockSpec((tq, width), lambda i: (i, off // width))

    row = pl.BlockSpec((tq, LANES), lambda i: (i, 0))
    return pl.pallas_call(
        body, name="swa_fwd", grid=(s // tq,),
        in_specs=[col(512, OFF_SQ), col(512, OFF_SZ), col(LANES, OFF_SK), col(LANES, OFF_SV), row, row,
                  pl.BlockSpec(memory_space=pltpu.SMEM)],
        out_specs=[pl.BlockSpec((tq, 512), lambda i: (i, 0))] * 2,
        out_shape=[jax.ShapeDtypeStruct((s, 512), BF), jax.ShapeDtypeStruct((s, 512), F32)],
        scratch_shapes=[pltpu.VMEM((SWA_BLOCK, LANES), F32)] * 2,
        compiler_params=_params(("arbitrary",)),
    )(proj, proj, proj, proj, cos, sin, sinks)


def _swa_bwd(proj, dos, opre, cos, sin, sinks):
    s = proj.shape[0]
    nq = min(SWA_QBLOCKS, s // SWA_BLOCK)
    tq = nq * SWA_BLOCK

    def body(sq_ref, sz_ref, sk_ref, sv_ref, dos_ref, opre_ref, cos_ref, sin_ref, sinks_ref,
             dsq_ref, dsz_ref, dsk_ref, dsv_ref, dsink_ref, kprev, vprev, cprev, sprev):
        n = pl.program_id(0)

        @pl.when(n == 0)
        def _():
            kprev[...] = jnp.zeros_like(kprev)
            vprev[...] = jnp.zeros_like(vprev)
            cprev[...] = jnp.zeros_like(cprev)
            sprev[...] = jnp.zeros_like(sprev)
            for hd in range(SWA_HEADS):
                dsink_ref[0, hd] = 0.0

        lo2, lo1, first_half = _swa_masks()
        lo1s = jnp.concatenate([lo1, lo1], axis=0)

        def home(m0, m1):
            t0 = m0 + pltpu.roll(m0, 64, 1)
            t1 = m1 + pltpu.roll(m1, 64, 1)
            return jnp.where(lo2, t0, t1)

        kp, vp, cp_, sp_ = kprev[...], vprev[...], cprev[...], sprev[...]
        for j in range(nq):
            rows = slice(j * SWA_BLOCK, (j + 1) * SWA_BLOCK)
            blk = n * nq + j
            cosb, sinb = cos_ref[rows, :], sin_ref[rows, :]
            kc = _rope(sk_ref[rows, :], cosb, sinb, first_half)
            vc = sv_ref[rows, :]
            k_lo, k_hi = _kv_variants(jnp.concatenate([kp, kc], axis=0), lo2)
            v_lo, v_hi = _kv_variants(jnp.concatenate([vp, vc], axis=0), lo2)
            qg32 = _swa_queries(sq_ref, rows, cosb, sinb, first_half)
            qg = [q.astype(BF) for q in qg32]
            pr, ps = _swa_probs(qg, k_lo, k_hi, _swa_valid(blk), sinks_ref)

            dog32 = []
            for g in range(2):
                parts = []
                for i in range(2):
                    ls = slice((2 * g + i) * LANES, (2 * g + i + 1) * LANES)
                    sz = sz_ref[rows, ls]
                    sg = _sigmoid(sz)
                    dos_p = dos_ref[rows, ls]
                    dsz_ref[rows, ls] = (dos_p * opre_ref[rows, ls] * (sg * (1.0 + sz * (1.0 - sg)))).astype(dsz_ref.dtype)
                    parts.append(dos_p * (sz * sg))
                dog32.append(jnp.concatenate(parts, axis=0))
            dog = [t.astype(BF) for t in dog32]
            dpr = jnp.concatenate([_dot(dog[0], v_lo[0], NT), _dot(dog[0], v_hi[0], NT),
                                   _dot(dog[1], v_lo[1], NT), _dot(dog[1], v_hi[1], NT)], axis=0)
            rd = jnp.sum(pr * dpr, axis=-1, keepdims=True)
            ds = (pr * (dpr - rd)).astype(BF)
            prb = pr.astype(BF)
            sink_term = ps * rd
            for r, hd in enumerate(_SWA_ROW_HEADS):
                dsink_ref[0, hd] += -jnp.sum(sink_term[r * SWA_BLOCK:(r + 1) * SWA_BLOCK])

            dk_g, dv_g = [], []
            for g in range(2):
                ds_lo, ds_hi = ds[512 * g:512 * g + 256], ds[512 * g + 256:512 * g + 512]
                dq = _dot(ds_lo, k_lo[g]) + _dot(ds_hi, k_hi[g])
                for i in range(2):
                    ls = slice((2 * g + i) * LANES, (2 * g + i + 1) * LANES)
                    dsq_ref[rows, ls] = _rope_t(dq[i * SWA_BLOCK:(i + 1) * SWA_BLOCK] * 0.125, cosb, sinb,
                                                first_half).astype(dsq_ref.dtype)
                q_split = jnp.concatenate([jnp.where(lo1s, qg32[g], 0.0), jnp.where(lo1s, 0.0, qg32[g])], axis=0).astype(BF)
                do_split = jnp.concatenate([jnp.where(lo1s, dog32[g], 0.0), jnp.where(lo1s, 0.0, dog32[g])], axis=0).astype(BF)
                dk_g.append(_dot(ds[512 * g:512 * g + 512], q_split, TN))
                dv_g.append(_dot(prb[512 * g:512 * g + 512], do_split, TN))
            dk = home(dk_g[0], dk_g[1])
            dv = home(dv_g[0], dv_g[1])
            cur = pl.ds(pl.multiple_of(blk * SWA_BLOCK, SWA_BLOCK), SWA_BLOCK)
            dsk_ref[cur, :] = _rope_t(dk[SWA_BLOCK:], cosb, sinb, first_half)
            dsv_ref[cur, :] = dv[SWA_BLOCK:]
            dk_prev = _rope_t(dk[:SWA_BLOCK], cp_, sp_, first_half)
            dv_prev = dv[:SWA_BLOCK]
            if j == 0:
                @pl.when(n > 0)
                def _():
                    prv = pl.ds(pl.multiple_of((blk - 1) * SWA_BLOCK, SWA_BLOCK), SWA_BLOCK)
                    dsk_ref[prv, :] += dk_prev
                    dsv_ref[prv, :] += dv_prev
            else:
                prv = pl.ds(pl.multiple_of((blk - 1) * SWA_BLOCK, SWA_BLOCK), SWA_BLOCK)
                dsk_ref[prv, :] += dk_prev
                dsv_ref[prv, :] += dv_prev
            kp, vp, cp_, sp_ = kc, vc, cosb, sinb
        kprev[...] = kp
        vprev[...] = vp
        cprev[...] = cp_
        sprev[...] = sp_

    def col(width, off):
        return pl.BlockSpec((tq, width), lambda i: (i, off // width))

    row = pl.BlockSpec((tq, LANES), lambda i: (i, 0))
    wide = pl.BlockSpec((tq, 512), lambda i: (i, 0))
    return pl.pallas_call(
        body, name="swa_bwd", grid=(s // tq,),
        in_specs=[col(512, OFF_SQ), col(512, OFF_SZ), col(LANES, OFF_SK), col(LANES, OFF_SV), wide, wide, row, row,
                  pl.BlockSpec(memory_space=pltpu.SMEM)],
        out_specs=[wide, wide, _full((s, LANES)), _full((s, LANES)), pl.BlockSpec(memory_space=pltpu.SMEM)],
        out_shape=[jax.ShapeDtypeStruct((s, 512), BF), jax.ShapeDtypeStruct((s, 512), BF),
                   jax.ShapeDtypeStruct((s, LANES), F32), jax.ShapeDtypeStruct((s, LANES), F32),
                   jax.ShapeDtypeStruct((1, SWA_HEADS), F32)],
        scratch_shapes=[pltpu.VMEM((SWA_BLOCK, LANES), F32)] * 4,
        compiler_params=_params(("arbitrary",)),
    )(proj, proj, proj, proj, dos, opre, cos, sin, sinks)


def _outproj(og, osw, w_out, x2d, target, gate, g_final):
    s = x2d.shape[0]
    tm = min(512, s)

    def body(og_ref, os_ref, w_ref, x_ref, t_ref, gate_ref, gf_ref,
             dx2_ref, dog_ref, dos_ref, dw_ref, loss_ref, dgf_ref, dgate_ref):
        @pl.when(pl.program_id(0) == 0)
        def _():
            dw_ref[...] = jnp.zeros_like(dw_ref)
            loss_ref[...] = jnp.zeros_like(loss_ref)
            dgf_ref[...] = jnp.zeros_like(dgf_ref)
            dgate_ref[...] = jnp.zeros_like(dgate_ref)

        ogv, osv, w = og_ref[...], os_ref[...], w_ref[...]
        gate, gf = gate_ref[...], gf_ref[...]
        y = _dot(ogv, w[:512]) + _dot(osv, w[512:])
        x2 = x_ref[...] + gate * y
        r = lax.rsqrt(jnp.mean(x2 * x2, axis=-1, keepdims=True) + RMS_EPS)
        xn = x2 * r
        err = xn * gf - t_ref[...]
        loss_ref[...] += 0.5 * jnp.sum(jnp.mean(err * err, axis=-1, keepdims=True), axis=0, keepdims=True)
        dyf = err * (1.0 / D_MODEL)
        dgf_ref[...] += jnp.sum(dyf * xn, axis=0, keepdims=True)
        t = dyf * gf
        dx2 = r * (t - xn * jnp.mean(t * xn, axis=-1, keepdims=True))
        dx2_ref[...] = dx2
        dgate_ref[...] += jnp.sum(dx2 * y, axis=0, keepdims=True)
        dy = (dx2 * gate).astype(BF)
        dmix = _dot(dy, w, NT)
        dog_ref[...] = dmix[:, :512]
        dos_ref[...] = dmix[:, 512:]
        dw_ref[:512, :] += _dot(ogv, dy, TN)
        dw_ref[512:, :] += _dot(osv, dy, TN)

    half = pl.BlockSpec((tm, 512), lambda i: (i, 0))
    rowb = pl.BlockSpec((tm, D_MODEL), lambda i: (i, 0))
    vec = _full((1, D_MODEL))
    return pl.pallas_call(
        body, name="outproj", grid=(s // tm,),
        in_specs=[half, half, _full((D_MODEL, D_MODEL)), rowb, rowb, vec, vec],
        out_specs=[rowb, half, half, _full((D_MODEL, D_MODEL)), _full((1, 1)), vec, vec],
        out_shape=[jax.ShapeDtypeStruct((s, D_MODEL), F32), jax.ShapeDtypeStruct((s, 512), F32),
                   jax.ShapeDtypeStruct((s, 512), F32), jax.ShapeDtypeStruct((D_MODEL, D_MODEL), F32),
                   jax.ShapeDtypeStruct((1, 1), F32), jax.ShapeDtypeStruct((1, D_MODEL), F32),
                   jax.ShapeDtypeStruct((1, D_MODEL), F32)],
        compiler_params=_params(("arbitrary",)),
    )(og, osw, w_out, x2d, target, gate, g_final)


_PIECES = ((OFF_QK, 512), (OFF_V, 512), (OFF_GZ, 512), (OFF_SQ, 512), (OFF_SZ, 512),
           (OFF_SK, LANES), (OFF_SV, LANES), (OFF_GA, LANES))


def _inproj_bwd(x2d, shift, sc1p, g_norm, wpad_t, dx2, pieces):
    s = x2d.shape[0]
    tm = min(256, s)
    nsteps = s // tm

    def body(x_ref, sh_ref, sc_ref, g_ref, w_hbm, dx2_ref, *rest):
        piece_refs = rest[:len(_PIECES)]
        gx_ref, dw_hbm, dsh_ref, dsc_ref, dg_ref, w_vm, dw_vm, sem = rest[len(_PIECES):]
        i = pl.program_id(0)

        @pl.when(i == 0)
        def _():
            cp = pltpu.make_async_copy(w_hbm, w_vm, sem)
            cp.start()
            dw_vm[...] = jnp.zeros_like(dw_vm)
            dsh_ref[...] = jnp.zeros_like(dsh_ref)
            dsc_ref[...] = jnp.zeros_like(dsc_ref)
            dg_ref[...] = jnp.zeros_like(dg_ref)
            cp.wait()

        g, sc1p_v = g_ref[...], sc_ref[...]
        xn, r, h = _modnorm(x_ref[...], g, sc1p_v, sh_ref[...])
        hb = h.astype(BF)
        dh = None
        for (off, width), pr in zip(_PIECES, piece_refs):
            dp = pr[...].astype(BF)
            part = _dot(dp, w_vm[off:off + width, :])
            dh = part if dh is None else dh + part
            dw_vm[off:off + width, :] += _dot(dp, hb, TN)
        dsh_ref[...] += jnp.sum(dh, axis=0, keepdims=True)
        dsc_ref[...] += jnp.sum(dh * (xn * g), axis=0, keepdims=True)
        dg_ref[...] += jnp.sum(dh * xn * sc1p_v, axis=0, keepdims=True)
        dxn = dh * g * sc1p_v
        gx_ref[...] = dx2_ref[...] + r * (dxn - xn * jnp.mean(dxn * xn, axis=-1, keepdims=True))

        @pl.when(i == nsteps - 1)
        def _():
            cp = pltpu.make_async_copy(dw_vm, dw_hbm, sem)
            cp.start()
            cp.wait()

    rowb = pl.BlockSpec((tm, D_MODEL), lambda i: (i, 0))
    vec = _full((1, D_MODEL))
    anyspec = pl.BlockSpec(memory_space=pl.ANY)
    piece_specs = [pl.BlockSpec((tm, width), lambda i: (i, 0)) for _, width in _PIECES]
    return pl.pallas_call(
        body, name="inproj_bwd", grid=(nsteps,),
        in_specs=[rowb, vec, vec, vec, anyspec, rowb] + piece_specs,
        out_specs=[rowb, anyspec, vec, vec, vec],
        out_shape=[jax.ShapeDtypeStruct((s, D_MODEL), F32), jax.ShapeDtypeStruct((D_PAD, D_MODEL), F32),
                   jax.ShapeDtypeStruct((1, D_MODEL), F32), jax.ShapeDtypeStruct((1, D_MODEL), F32),
                   jax.ShapeDtypeStruct((1, D_MODEL), F32)],
        scratch_shapes=[pltpu.VMEM((D_PAD, D_MODEL), BF), pltpu.VMEM((D_PAD, D_MODEL), F32), pltpu.SemaphoreType.DMA],
        compiler_params=_params(("arbitrary",)),
    )(x2d, shift, sc1p, g_norm, wpad_t, dx2, *pieces)


def _adam(w, g, m, v):
    m2 = ADAM_B1 * m + (1.0 - ADAM_B1) * g
    v2 = ADAM_B2 * v + (1.0 - ADAM_B2) * (g * g)
    m_hat = m2 / (1.0 - ADAM_B1 ** ADAM_STEP)
    v_hat = v2 / (1.0 - ADAM_B2 ** ADAM_STEP)
    delta = -ADAM_LR * (m_hat / (jnp.sqrt(v_hat) + ADAM_EPS) + ADAM_WD * w)
    return delta, m2, v2


def _adamw(w, g, m, v, name):
    rr, cc = w.shape
    tc = min(256, cc)

    def body(w_ref, g_ref, m_ref, v_ref, d_ref, m2_ref, v2_ref):
        d_ref[...], m2_ref[...], v2_ref[...] = _adam(w_ref[...], g_ref[...], m_ref[...], v_ref[...])

    blk = pl.BlockSpec((rr, tc), lambda i: (0, i))
    return pl.pallas_call(
        body, name=name, grid=(cc // tc,), in_specs=[blk] * 4, out_specs=[blk] * 3,
        out_shape=[jax.ShapeDtypeStruct((rr, cc), F32)] * 3,
        compiler_params=_params(("arbitrary",)),
    )(w, g, m, v)


def _ada_update(c_all, dmod_cols, w, m, v):
    rr, cc = w.shape
    tr = min(256, rr)
    c_all = jnp.pad(c_all, ((0, 8), (0, 0)))
    dmod_cols = jnp.pad(dmod_cols, ((0, 8), (0, 0)))

    def body(c_ref, dm_ref, w_ref, m_ref, v_ref, g_ref, d_ref, m2_ref, v2_ref):
        cv = c_ref[...]
        sc = (cv * _sigmoid(cv)).astype(BF)
        g = _dot(sc, dm_ref[...].astype(BF), TN)
        g_ref[...] = g
        d_ref[...], m2_ref[...], v2_ref[...] = _adam(w_ref[...], g, m_ref[...], v_ref[...])

    blk = pl.BlockSpec((tr, cc), lambda i: (i, 0))
    return pl.pallas_call(
        body, name="ada_update", grid=(rr // tr,),
        in_specs=[pl.BlockSpec((16, tr), lambda i: (0, i)), _full((16, cc)), blk, blk, blk],
        out_specs=[blk] * 4, out_shape=[jax.ShapeDtypeStruct((rr, cc), F32)] * 4,
        compiler_params=_params(("arbitrary",)),
    )(c_all, dmod_cols, w, m, v)


def _small_update(parts, weights, moms, vels):
    n = len(weights)

    def body(*refs):
        p_refs, w_refs, m_refs, v_refs = refs[:n + 1], refs[n + 1:2 * n + 1], refs[2 * n + 1:3 * n + 1], refs[3 * n + 1:4 * n + 1]
        outs = refs[4 * n + 1:]
        for i in range(n):
            g = p_refs[i][0]
            for d in range(1, 8):
                g = g + p_refs[i][d]
            delta, m2, v2 = _adam(w_refs[i][...], g, m_refs[i][...], v_refs[i][...])
            outs[4 * i][...] = g
            outs[4 * i + 1][...] = delta
            outs[4 * i + 2][...] = m2
            outs[4 * i + 3][...] = v2
        tot = p_refs[n][0]
        for d in range(1, 8):
            tot = tot + p_refs[n][d]
        outs[4 * n][...] = tot

    out_shape = []
    for w in weights:
        out_shape += [jax.ShapeDtypeStruct(w.shape, F32)] * 4
    out_shape.append(jax.ShapeDtypeStruct(parts[n].shape[1:], F32))
    return pl.pallas_call(body, name="small_update", out_shape=out_shape, compiler_params=_params())(
        *parts, *weights, *moms, *vels)


def _pad_w_in_t(w):
    gq, gk = w[0:256], w[256:512]
    qk = []
    for h in range(GLA_HEADS):
        qk += [gq[64 * h:64 * h + 64], gk[64 * h:64 * h + 64]]
    gv, ga, gz = w[512:1024], w[1024:1040], w[1040:1552]
    sq, sk, sv, sz = w[1552:2064], w[2064:2192], w[2192:2320], w[2320:2832]
    pad = jnp.zeros((LANES - GLA_RANK, w.shape[1]), w.dtype)
    return jnp.concatenate(qk + [gv, gz, sq, sz, sk, sv, ga, pad], axis=0)


def _unpad_dw_in_t(d):
    qk = d[0:512]
    gq = jnp.concatenate([qk[128 * h:128 * h + 64] for h in range(GLA_HEADS)], axis=0)
    gk = jnp.concatenate([qk[128 * h + 64:128 * h + 128] for h in range(GLA_HEADS)], axis=0)
    return jnp.concatenate([gq, gk, d[OFF_V:OFF_V + 512], d[OFF_GA:OFF_GA + GLA_RANK], d[OFF_GZ:OFF_GZ + 512],
                            d[OFF_SQ:OFF_SQ + 512], d[OFF_SK:OFF_SK + 128], d[OFF_SV:OFF_SV + 128],
                            d[OFF_SZ:OFF_SZ + 512]], axis=0)


def _dup_heads(t):
    parts = []
    for h in range(GLA_HEADS):
        parts += [t[..., 64 * h:64 * h + 64]] * 2
    return jnp.concatenate(parts, axis=-1)


def _rows8(a):
    flat = a.reshape(-1)
    rows = -(-flat.shape[0] // LANES)
    rows8 = -(-rows // 8) * 8
    flat = jnp.pad(flat, (0, rows8 * LANES - flat.shape[0]))
    return flat.reshape(rows8, LANES)


def kernel(x, c, positions, w_ada, b_ada, g_norm, w_in, w_decay, b_decay, g_gla_head, sinks, w_out, g_final, loss_target, m_w_ada, m_b_ada, m_g_norm, m_w_in, m_w_decay, m_b_decay, m_g_gla_head, m_sinks, m_w_out, m_g_final, v_w_ada, v_b_ada, v_g_norm, v_w_in, v_w_decay, v_b_decay, v_g_gla_head, v_sinks, v_w_out, v_g_final):
    ax, ay, ac = lax.axis_index("x"), lax.axis_index("y"), lax.axis_index("c")
    chip = 2 * ax + ay
    dev = 2 * chip + ac
    s = x.shape[1]
    x2d = x[0]
    target = loss_target[0]
    w_ada2, w_out2, w_dec2 = w_ada[0], w_out[0], w_decay[0]
    w_in_t, m_w_in_t, v_w_in_t = w_in[0].T, m_w_in[0].T, v_w_in[0].T
    ada_cols = w_ada2.shape[1]
    in_cols = w_in_t.shape[0]
    out_rows = w_out2.shape[0]
    half = D_MODEL // 2

    first = _all_gather(jnp.concatenate([c.reshape(8, LANES), w_dec2.reshape(8, LANES)], axis=0), "gather_c")
    first = first.reshape(8, 2, 8, LANES)
    c_all = first[:, 0].reshape(8, D_MODEL)
    w_dec_full = first[0::2, 1].reshape(4, GLA_RANK, 64).transpose(1, 0, 2).reshape(GLA_RANK, 256)
    b_shard = lax.dynamic_slice(b_ada, (0, chip * ada_cols), (1, ada_cols))
    mod_all = _all_gather(_ada_fwd(c_all, w_ada2, b_shard), "gather_mod")
    mod = mod_all.reshape(4, 2, 8, ada_cols)[:, 0]
    mod = lax.dynamic_slice(mod, (0, dev, 0), (4, 1, ada_cols)).reshape(1, 4 * ada_cols)
    shift, sc1p, gate = mod[:, :D_MODEL], 1.0 + mod[:, D_MODEL:2 * D_MODEL], mod[:, 2 * D_MODEL:]

    half_in = lax.dynamic_slice(w_in_t, (0, ac * half), (in_cols, half)).astype(BF)
    w_in_all = _all_gather(half_in, "gather_w_in").reshape(4, 2, in_cols, half)
    wpad_t = _pad_w_in_t(w_in_all.transpose(0, 2, 1, 3).reshape(4 * in_cols, D_MODEL))
    half_out = lax.dynamic_slice(w_out2, (ac * (out_rows // 2), 0), (out_rows // 2, D_MODEL)).astype(BF)
    w_out_all = _all_gather(half_out, "gather_w_out").reshape(D_MODEL, D_MODEL)

    wdecp = jnp.pad(_dup_heads(w_dec_full), ((0, LANES - GLA_RANK), (0, 0))).astype(BF)
    bdecp = _dup_heads(b_decay)
    inv_freq = 1.0 / (ROPE_THETA ** (jnp.arange(0, 64, 2, dtype=F32) / 64))
    cos, sin = _rope_tables(positions.reshape(s, 1), jnp.tile(inv_freq, 4).reshape(1, LANES))

    proj = _inproj_fwd(x2d, shift, sc1p, g_norm, wpad_t)
    og, o_gla, sprev = _gla_fwd(proj, wdecp, bdecp, g_gla_head)
    osw, o_swa = _swa_fwd(proj, cos, sin, sinks)
    dx2, dog, dos, dw_out, loss_p, dgf, dgate = _outproj(og, osw, w_out_all, x2d, target, gate, g_final.reshape(1, D_MODEL))
    dsq, dsz, dsk, dsv, dsinks = _swa_bwd(proj, dos, o_swa, cos, sin, sinks)
    dqk, dv, dgz, dga, dwdp, dbdp, dgg = _gla_bwd(proj, dog, o_gla, sprev, wdecp, bdecp, g_gla_head)
    pieces = (dqk, dv, dgz, dsq, dsz, dsk, dsv, dga)
    gx, dwpad_t, dshift, dscale, dgn = _inproj_bwd(x2d, shift, sc1p, g_norm, wpad_t, dx2, pieces)

    g_w_in_t = _reduce_scatter(_unpad_dw_in_t(dwpad_t).reshape(4, in_cols, D_MODEL), "reduce_w_in")
    g_w_out = _reduce_scatter(dw_out.reshape(4, out_rows, D_MODEL), "reduce_w_out")

    dwd = jnp.concatenate([dwdp[:GLA_RANK, 128 * h:128 * h + 64] for h in range(GLA_HEADS)], axis=1)
    dbd = jnp.concatenate([dbdp[:, 128 * h:128 * h + 64] for h in range(GLA_HEADS)], axis=1)
    segs = [jnp.concatenate([dshift, dscale, dgate], axis=1), dgn, dgf, dwd, dbd, dgg, dsinks, loss_p]
    packed = [_rows8(a) for a in segs]
    offs = [0]
    for a in packed:
        offs.append(offs[-1] + a.shape[0])
    small = _all_gather(jnp.concatenate(packed, axis=0), "gather_small")

    def seg(i, size):
        return small[:, offs[i]:offs[i + 1]].reshape(8, -1)[:, :size]

    dmod_all = seg(0, 3 * D_MODEL)
    dwd_all = lax.dynamic_slice(seg(3, GLA_RANK * 256).reshape(8, GLA_RANK, 256), (0, 0, chip * 64), (8, GLA_RANK, 64))
    parts = [dmod_all.reshape(8, 1, 3 * D_MODEL), seg(1, D_MODEL).reshape(8, 1, D_MODEL), dwd_all,
             seg(4, 256).reshape(8, 1, 256), seg(5, 512).reshape(8, 1, 512), seg(6, SWA_HEADS).reshape(8, 1, SWA_HEADS),
             seg(2, D_MODEL).reshape(8, 1, D_MODEL), seg(7, LANES).reshape(8, 1, LANES)]
    smalls = _small_update(
        parts,
        [b_ada, g_norm, w_dec2, b_decay, g_gla_head, sinks, g_final.reshape(1, D_MODEL)],
        [m_b_ada, m_g_norm, m_w_decay[0], m_b_decay, m_g_gla_head, m_sinks, m_g_final.reshape(1, D_MODEL)],
        [v_b_ada, v_g_norm, v_w_decay[0], v_b_decay, v_g_gla_head, v_sinks, v_g_final.reshape(1, D_MODEL)])
    (g_b_ada, d_b_ada, nm_b_ada, nv_b_ada, g_gn, d_gn, nm_gn, nv_gn, g_wd, d_wd, nm_wd, nv_wd,
     g_bd, d_bd, nm_bd, nv_bd, g_gg, d_gg, nm_gg, nv_gg, g_sk, d_sk, nm_sk, nv_sk,
     g_gf, d_gf, nm_gf, nv_gf, loss_row) = smalls
    loss = loss_row[0, 0]

    dmod_cols = lax.dynamic_slice(dmod_all, (0, chip * ada_cols), (8, ada_cols))
    g_w_ada, d_w_ada, nm_w_ada, nv_w_ada = _ada_update(c_all, dmod_cols, w_ada2, m_w_ada[0], v_w_ada[0])
    d_w_in_t, nm_w_in_t, nv_w_in_t = _adamw(w_in_t, g_w_in_t, m_w_in_t, v_w_in_t, "adamw_w_in")
    g_w_in, d_w_in, nm_w_in, nv_w_in = g_w_in_t.T, d_w_in_t.T, nm_w_in_t.T, nv_w_in_t.T
    d_w_out, nm_w_out, nv_w_out = _adamw(w_out2, g_w_out, m_w_out[0], v_w_out[0], "adamw_w_out")

    flat = lambda a: a.reshape(D_MODEL)
    grads = [g_w_ada[None], g_b_ada, g_gn, g_w_in[None], g_wd[None], g_bd, g_gg, g_sk, g_w_out[None], flat(g_gf)]
    deltas = [d_w_ada[None], d_b_ada, d_gn, d_w_in[None], d_wd[None], d_bd, d_gg, d_sk, d_w_out[None], flat(d_gf)]
    new_m = [nm_w_ada[None], nm_b_ada, nm_gn, nm_w_in[None], nm_wd[None], nm_bd, nm_gg, nm_sk, nm_w_out[None], flat(nm_gf)]
    new_v = [nv_w_ada[None], nv_b_ada, nv_gn, nv_w_in[None], nv_wd[None], nv_bd, nv_gg, nv_sk, nv_w_out[None], flat(nv_gf)]
    return (loss, gx[None], *grads, *deltas, *new_m, *new_v)
```

```python
import jax
import jax.numpy as jnp
from jax import lax
from jax.experimental import pallas as pl
from jax.experimental.pallas import tpu as pltpu

F32 = jnp.float32
BF = jnp.bfloat16

D_MODEL = 1024
GLA_HEADS = 4
GLA_DK = 64
GLA_CHUNK = 64
GLA_RANK = 16
GLA_TAU = 16.0
GLA_ROWS = 256
SWA_HEADS = 8
SWA_BLOCK = 128
SWA_QBLOCKS = 2
RMS_EPS = 1e-6
ROPE_THETA = 10000.0

OFF_QK, OFF_V, OFF_GZ, OFF_SQ, OFF_SZ, OFF_SK, OFF_SV, OFF_GA = 0, 512, 1024, 1536, 2048, 2560, 2688, 2816
D_PAD = 2944
LANES = 128
VMEM_LIMIT = 56 * 1024 * 1024

ADAM_LR, ADAM_B1, ADAM_B2, ADAM_EPS, ADAM_WD, ADAM_STEP = 0.001, 0.9, 0.999, 1e-08, 0.01, 10

NT = (((1,), (1,)), ((), ()))
TN = (((0,), (0,)), ((), ()))
MESH = pl.DeviceIdType.MESH


def _dot(a, b, dims=None):
    if dims is None:
        return jnp.dot(a, b, preferred_element_type=F32)
    return lax.dot_general(a, b, dims, preferred_element_type=F32)


def _sigmoid(x):
    return 1.0 / (1.0 + jnp.exp(-x))


def _params(sem=None):
    return pltpu.CompilerParams(dimension_semantics=sem, vmem_limit_bytes=VMEM_LIMIT)


def _full(shape):
    return pl.BlockSpec(shape, lambda i: (0,) * len(shape))


def _all_gather(block, name):
    def body(x_ref, out_ref, send_sems, recv_sems, local_sem):
        x, y, c = lax.axis_index("x"), lax.axis_index("y"), lax.axis_index("c")
        me, sibling = (x, y, c), (x, y, 1 - c)
        chips = [(1 - x, y), (x, 1 - y), (1 - x, 1 - y)]

        def rows(px, py, pc):
            return out_ref.at[4 * px + 2 * py + pc]

        def copy(k, blk, to, src=None):
            return pltpu.make_async_remote_copy(
                src_ref=rows(*blk) if src is None else src, dst_ref=rows(*blk),
                send_sem=send_sems.at[k], recv_sem=recv_sems.at[k], device_id=to, device_id_type=MESH)

        mine = pltpu.make_async_copy(x_ref, rows(*me), local_sem)
        mine.start()
        first = [copy(0, me, sibling, src=x_ref)]
        first += [copy(1 + j, me, (*chip, c), src=x_ref) for j, chip in enumerate(chips)]
        for cp in first:
            cp.start()
        passed = [copy(4 + j, (*chip, c), sibling) for j, chip in enumerate(chips)]
        for j, chip in enumerate(chips):
            copy(1 + j, (*chip, c), me).wait_recv()
            passed[j].start()
        copy(0, sibling, me).wait_recv()
        for j, chip in enumerate(chips):
            copy(4 + j, (*chip, 1 - c), me).wait_recv()
        for cp in first + passed:
            cp.wait_send()
        mine.wait()

    return pl.pallas_call(
        body, name=name,
        out_shape=jax.ShapeDtypeStruct((8,) + block.shape, block.dtype),
        in_specs=[pl.BlockSpec(memory_space=pltpu.VMEM)],
        out_specs=pl.BlockSpec(memory_space=pltpu.VMEM),
        scratch_shapes=[pltpu.SemaphoreType.DMA((7,)), pltpu.SemaphoreType.DMA((7,)), pltpu.SemaphoreType.DMA],
        compiler_params=pltpu.CompilerParams(vmem_limit_bytes=VMEM_LIMIT),
    )(block)


def _reduce_scatter(parts, name):
    _, rr, cc = parts.shape
    c2 = cc // 2

    def body(p_hbm, out_ref, acc_ref, own_ref, send_ref, land_ref, res_ref, send_sems, recv_sems, local_sems):
        x, y, c = lax.axis_index("x"), lax.axis_index("y"), lax.axis_index("c")
        sibling = (x, y, 1 - c)
        chips = [(1 - x, y), (x, 1 - y), (1 - x, 1 - y)]
        mine = pl.ds(pl.multiple_of(c * c2, c2), c2)
        other = pl.ds(pl.multiple_of((1 - c) * c2, c2), c2)

        own = pltpu.make_async_copy(p_hbm.at[:, :, mine], own_ref, local_sems.at[0])
        own.start()
        swap = pltpu.make_async_remote_copy(
            src_ref=p_hbm.at[:, :, other], dst_ref=acc_ref, send_sem=send_sems.at[0], recv_sem=recv_sems.at[0],
            device_id=sibling, device_id_type=MESH)
        swap.start()
        own.wait()
        swap.wait()
        for j in range(4):
            acc_ref[j] = acc_ref[j] + own_ref[j]

        sends = []
        for k, (tx, ty) in enumerate(chips):
            send_ref[k] = acc_ref[2 * tx + ty].astype(send_ref.dtype)
            cp = pltpu.make_async_remote_copy(
                src_ref=send_ref.at[k], dst_ref=land_ref.at[k], send_sem=send_sems.at[1 + k],
                recv_sem=recv_sems.at[1 + k], device_id=(tx, ty, c), device_id_type=MESH)
            cp.start()
            sends.append(cp)
        for cp in sends:
            cp.wait_recv()
        total = acc_ref[2 * x + y]
        for k in range(3):
            total = total + land_ref[k].astype(F32)
        res_ref[...] = total
        for cp in sends:
            cp.wait_send()

        put = pltpu.make_async_copy(res_ref, out_ref.at[:, mine], local_sems.at[1])
        put.start()
        share = pltpu.make_async_remote_copy(
            src_ref=res_ref, dst_ref=out_ref.at[:, mine], send_sem=send_sems.at[4],
            recv_sem=recv_sems.at[4], device_id=sibling, device_id_type=MESH)
        share.start()
        put.wait()
        share.wait()

    return pl.pallas_call(
        body, name=name,
        out_shape=jax.ShapeDtypeStruct((rr, cc), F32),
        in_specs=[pl.BlockSpec(memory_space=pl.ANY)],
        out_specs=pl.BlockSpec(memory_space=pltpu.VMEM),
        scratch_shapes=[pltpu.VMEM((4, rr, c2), F32), pltpu.VMEM((4, rr, c2), F32), pltpu.VMEM((3, rr, c2), BF),
                        pltpu.VMEM((3, rr, c2), BF), pltpu.VMEM((rr, c2), F32),
                        pltpu.SemaphoreType.DMA((5,)), pltpu.SemaphoreType.DMA((5,)), pltpu.SemaphoreType.DMA((2,))],
        compiler_params=pltpu.CompilerParams(vmem_limit_bytes=VMEM_LIMIT),
    )(parts)


def _ada_fwd(c_all, w_ada, b_shard):
    def body(c_ref, w_ref, b_ref, o_ref):
        cv = c_ref[...]
        sc = (cv * _sigmoid(cv)).astype(BF)
        o_ref[...] = _dot(sc, w_ref[...].astype(BF)) + b_ref[...]

    return pl.pallas_call(
        body, name="ada_fwd", out_shape=jax.ShapeDtypeStruct((8, w_ada.shape[1]), F32),
        compiler_params=_params(),
    )(c_all, w_ada, b_shard)


def _rope_tables(pos_col, inv_freq):
    s = pos_col.shape[0]
    tm = min(1024, s)

    def body(p_ref, f_ref, cos_ref, sin_ref):
        ang = p_ref[...].astype(F32) * f_ref[...]
        lane = lax.broadcasted_iota(jnp.int32, ang.shape, 1)
        cos_ref[...] = jnp.cos(ang)
        sn = jnp.sin(ang)
        sin_ref[...] = jnp.where((lane % 64) < 32, -sn, sn)

    return pl.pallas_call(
        body, name="rope_tables", grid=(s // tm,),
        in_specs=[pl.BlockSpec((tm, 1), lambda i: (i, 0)), _full((1, LANES))],
        out_specs=[pl.BlockSpec((tm, LANES), lambda i: (i, 0))] * 2,
        out_shape=[jax.ShapeDtypeStruct((s, LANES), F32)] * 2,
        compiler_params=_params(("arbitrary",)),
    )(pos_col, inv_freq)


def _rope(t, cosb, sinb, first_half):
    partner = jnp.where(first_half, pltpu.roll(t, 96, 1), pltpu.roll(t, 32, 1))
    return t * cosb + partner * sinb


def _rope_t(g, cosb, sinb, first_half):
    gs = g * sinb
    partner = jnp.where(first_half, pltpu.roll(gs, 96, 1), pltpu.roll(gs, 32, 1))
    return g * cosb + partner


def _modnorm(x, g, sc1p, shift):
    r = lax.rsqrt(jnp.mean(x * x, axis=-1, keepdims=True) + RMS_EPS)
    xn = x * r
    return xn, r, (xn * g) * sc1p + shift


def _inproj_fwd(x2d, shift, sc1p, g_norm, wpad_t):
    s = x2d.shape[0]
    tm = min(512, s)

    def body(x_ref, sh_ref, sc_ref, g_ref, w_ref, o_ref):
        _, _, h = _modnorm(x_ref[...], g_ref[...], sc_ref[...], sh_ref[...])
        o_ref[...] = _dot(h.astype(BF), w_ref[...], NT)

    vec = _full((1, D_MODEL))
    return pl.pallas_call(
        body, name="inproj_fwd", grid=(s // tm,),
        in_specs=[pl.BlockSpec((tm, D_MODEL), lambda i: (i, 0)), vec, vec, vec, _full((D_PAD, D_MODEL))],
        out_specs=pl.BlockSpec((tm, D_PAD), lambda i: (i, 0)),
        out_shape=jax.ShapeDtypeStruct((s, D_PAD), F32),
        compiler_params=_params(("arbitrary",)),
    )(x2d, shift, sc1p, g_norm, wpad_t)


def _split3(a):
    hi = a.astype(BF)
    r1 = a - hi.astype(F32)
    mid = r1.astype(BF)
    lo = (r1 - mid.astype(F32)).astype(BF)
    return hi, mid, lo


def _tri_matmul(tri, a):
    hi, mid, lo = _split3(a)
    return _dot(tri, hi) + _dot(tri, mid) + _dot(tri, lo)


def _chunks(tb):
    return [slice(c * GLA_CHUNK, (c + 1) * GLA_CHUNK) for c in range(tb // GLA_CHUNK)]


def _per_chunk_rows(rows, width):
    return jnp.concatenate([jnp.broadcast_to(r, (GLA_CHUNK, width)) for r in rows], axis=0)


def _gla_masks(tb):
    lane = lax.broadcasted_iota(jnp.int32, (1, 512), 1)
    lo512 = (lane % LANES) < GLA_DK
    sgn = jnp.where(lo512, 1.0, -1.0).astype(F32)
    qsc = jnp.where(lo512, GLA_DK ** -0.5, 1.0).astype(F32)
    lo_h = lax.broadcasted_iota(jnp.int32, (tb, LANES), 1) < GLA_DK
    row = lax.broadcasted_iota(jnp.int32, (tb, tb), 0)
    col = lax.broadcasted_iota(jnp.int32, (tb, tb), 1)
    same = (row // GLA_CHUNK) == (col // GLA_CHUNK)
    tril = jnp.where(same, row - col, -1) >= 0
    triu = jnp.where(same, col - row, -1) >= 0
    return lo_h, sgn, qsc, tril, triu


def _gla_block_common(qk, ga, wd, bd, tril_b, sgn, qsc):
    tb = qk.shape[0]
    z2 = _dot(ga.astype(BF), wd) + bd
    la2 = (jnp.minimum(z2, 0.0) - jnp.log1p(jnp.exp(-jnp.abs(z2)))) * (1.0 / GLA_TAU)
    b2 = _tri_matmul(tril_b, la2)
    bls = [b2[rs.stop - 1:rs.stop, :] for rs in _chunks(tb)]
    e = jnp.exp(b2 * sgn)
    f = jnp.exp(_per_chunk_rows(bls, 512) - b2)
    qkd = qk * e * qsc
    kt = qk * f
    decs = [jnp.exp(bl) for bl in bls]
    return z2, e, f, qkd, kt, decs


def _gla_fwd(proj, wdecp, bdecp, ggla):
    s = proj.shape[0]
    tb = min(GLA_ROWS, s)
    nch = tb // GLA_CHUNK

    def body(qk_ref, v_ref, gz_ref, ga_ref, wd_ref, bd_ref, gg_ref, og_ref, opre_ref, sprev_ref, st_ref):
        @pl.when(pl.program_id(0) == 0)
        def _():
            st_ref[...] = jnp.zeros_like(st_ref)

        lo_h, sgn, qsc, tril, _ = _gla_masks(tb)
        tril_b = jnp.where(tril, 1.0, 0.0).astype(BF)
        gg = gg_ref[...]
        _, _, _, qkd, kt, decs = _gla_block_common(qk_ref[...], ga_ref[...], wd_ref[...], bd_ref[...], tril_b, sgn, qsc)
        for h in range(GLA_HEADS):
            ls = slice(h * LANES, (h + 1) * LANES)
            a = jnp.where(lo_h, qkd[:, ls], 0.0).astype(BF)
            bm = jnp.where(lo_h, pltpu.roll(qkd[:, ls], 64, 1), 0.0).astype(BF)
            ktl = jnp.where(lo_h, pltpu.roll(kt[:, ls], 64, 1), 0.0).astype(BF)
            vh = v_ref[:, ls].astype(BF)
            p = jnp.where(tril, _dot(a, bm, NT), 0.0).astype(BF)
            o = _dot(p, vh)
            st = st_ref[h]
            inter = []
            for c, rs in enumerate(_chunks(tb)):
                sprev_ref[c, h] = st
                inter.append(_dot(a[rs], st.astype(BF), NT))
                st = st * decs[c][:, ls] + _dot(vh[rs], ktl[rs], TN)
            st_ref[h] = st
            o = o + jnp.concatenate(inter, axis=0)
            r = lax.rsqrt(jnp.mean(o * o, axis=-1, keepdims=True) + RMS_EPS)
            gzh = gz_ref[:, ls]
            opre_ref[:, ls] = o
            og_ref[:, ls] = (((o * r) * gg[:, ls]) * (gzh * _sigmoid(gzh))).astype(og_ref.dtype)

    def col(width, off):
        return pl.BlockSpec((tb, width), lambda i: (i, off // width))

    return pl.pallas_call(
        body, name="gla_fwd", grid=(s // tb,),
        in_specs=[col(512, OFF_QK), col(512, OFF_V), col(512, OFF_GZ), col(LANES, OFF_GA),
                  _full((LANES, 512)), _full((1, 512)), _full((1, 512))],
        out_specs=[pl.BlockSpec((tb, 512), lambda i: (i, 0)), pl.BlockSpec((tb, 512), lambda i: (i, 0)),
                   pl.BlockSpec((nch, GLA_HEADS, LANES, LANES), lambda i: (i, 0, 0, 0))],
        out_shape=[jax.ShapeDtypeStruct((s, 512), BF), jax.ShapeDtypeStruct((s, 512), F32),
                   jax.ShapeDtypeStruct((s // GLA_CHUNK, GLA_HEADS, LANES, LANES), F32)],
        scratch_shapes=[pltpu.VMEM((GLA_HEADS, LANES, LANES), F32)],
        compiler_params=_params(("arbitrary",)),
    )(proj, proj, proj, proj, wdecp, bdecp, ggla)


def _gla_bwd(proj, dog, opre, sprev, wdecp, bdecp, ggla):
    s = proj.shape[0]
    tb = min(GLA_ROWS, s)
    nch = tb // GLA_CHUNK
    nb = s // tb

    def body(qk_ref, v_ref, gz_ref, ga_ref, dog_ref, opre_ref, sprev_ref, wd_ref, bd_ref, gg_ref,
             dqk_ref, dv_ref, dgz_ref, dga_ref, dwd_ref, dbd_ref, dgg_ref, dst_ref):
        @pl.when(pl.program_id(0) == 0)
        def _():
            dst_ref[...] = jnp.zeros_like(dst_ref)
            dwd_ref[...] = jnp.zeros_like(dwd_ref)
            dbd_ref[...] = jnp.zeros_like(dbd_ref)
            dgg_ref[...] = jnp.zeros_like(dgg_ref)

        lo_h, sgn, qsc, tril, triu = _gla_masks(tb)
        tril_b = jnp.where(tril, 1.0, 0.0).astype(BF)
        triu_b = jnp.where(triu, 1.0, 0.0).astype(BF)
        last_row = (lax.broadcasted_iota(jnp.int32, (tb, LANES), 0) % GLA_CHUNK) == GLA_CHUNK - 1
        wd, gg = wd_ref[...], gg_ref[...]
        ga = ga_ref[...]
        z2, e, f, qkd, kt, decs = _gla_block_common(qk_ref[...], ga, wd, bd_ref[...], tril_b, sgn, qsc)
        chunks = _chunks(tb)
        db_parts = []
        for h in range(GLA_HEADS):
            ls = slice(h * LANES, (h + 1) * LANES)
            e_h, f_h = e[:, ls], f[:, ls]
            a32 = jnp.where(lo_h, qkd[:, ls], 0.0)
            bm32 = jnp.where(lo_h, pltpu.roll(qkd[:, ls], 64, 1), 0.0)
            kt32 = jnp.where(lo_h, pltpu.roll(kt[:, ls], 64, 1), 0.0)
            a, bm, ktl = a32.astype(BF), bm32.astype(BF), kt32.astype(BF)
            vh = v_ref[:, ls].astype(BF)
            p = jnp.where(tril, _dot(a, bm, NT), 0.0).astype(BF)

            o = opre_ref[:, ls]
            gzh = gz_ref[:, ls]
            dogh = dog_ref[:, ls]
            r = lax.rsqrt(jnp.mean(o * o, axis=-1, keepdims=True) + RMS_EPS)
            ohat = o * r
            sg = _sigmoid(gzh)
            sil = gzh * sg
            g_h = gg[:, ls]
            dgz_ref[:, ls] = (dogh * (ohat * g_h) * (sg * (1.0 + gzh * (1.0 - sg)))).astype(dgz_ref.dtype)
            dn = dogh * sil * g_h
            dgg_ref[:, ls] += jnp.sum(dogh * sil * ohat, axis=0, keepdims=True)
            do = (r * (dn - ohat * jnp.mean(dn * ohat, axis=-1, keepdims=True))).astype(BF)

            dp = jnp.where(tril, _dot(do, vh, NT), 0.0).astype(BF)
            dv = _dot(p, do, TN)
            dqd = _dot(dp, bm)
            dkd = _dot(dp, a, TN)
            d = dst_ref[h]
            dv_s, dqd_s, dkt_s, ddec = [None] * nch, [None] * nch, [None] * nch, [None] * nch
            for c in reversed(range(nch)):
                rs = chunks[c]
                st = sprev_ref[c, h]
                d_b = d.astype(BF)
                dv_s[c] = _dot(ktl[rs], d_b, NT)
                dqd_s[c] = _dot(do[rs], st.astype(BF))
                dkt_s[c] = _dot(vh[rs], d_b)
                ddec[c] = jnp.sum(d * st, axis=0, keepdims=True)
                d = d * decs[c][:, ls] + _dot(do[rs], a[rs], TN)
            dst_ref[h] = d
            dv_ref[:, ls] = (dv + jnp.concatenate(dv_s, axis=0)).astype(dv_ref.dtype)
            dqd = dqd + jnp.concatenate(dqd_s, axis=0)
            dkt = jnp.concatenate(dkt_s, axis=0)

            dq = dqd * e_h * (GLA_DK ** -0.5)
            dk = dkd * pltpu.roll(e_h, 64, 1) + dkt * f_h
            dqk_ref[:, ls] = jnp.where(lo_h, dq, pltpu.roll(jnp.where(lo_h, dk, 0.0), 64, 1)).astype(dqk_ref.dtype)
            dkt_kt = dkt * kt32
            db = dqd * a32 - dkd * bm32 - dkt_kt
            dbl = [jnp.sum(dkt_kt[rs], axis=0, keepdims=True) + ddec[c] * decs[c][:, ls] for c, rs in enumerate(chunks)]
            db = jnp.where(last_row, db + _per_chunk_rows(dbl, LANES), db)
            db_parts.append(jnp.where(lo_h, db, 0.0))
        db2 = jnp.concatenate(db_parts, axis=1)
        dla = _tri_matmul(triu_b, db2)
        dz32 = dla * (1.0 / GLA_TAU) * _sigmoid(-z2)
        dz = dz32.astype(BF)
        dga_ref[...] = _dot(dz, wd, NT).astype(dga_ref.dtype)
        dwd_ref[...] += _dot(ga.astype(BF), dz, TN)
        dbd_ref[...] += jnp.sum(dz32, axis=0, keepdims=True)

    def col(width, off):
        return pl.BlockSpec((tb, width), lambda i: (nb - 1 - i, off // width))

    def rev(width):
        return pl.BlockSpec((tb, width), lambda i: (nb - 1 - i, 0))

    return pl.pallas_call(
        body, name="gla_bwd", grid=(nb,),
        in_specs=[col(512, OFF_QK), col(512, OFF_V), col(512, OFF_GZ), col(LANES, OFF_GA), rev(512), rev(512),
                  pl.BlockSpec((nch, GLA_HEADS, LANES, LANES), lambda i: (nb - 1 - i, 0, 0, 0)),
                  _full((LANES, 512)), _full((1, 512)), _full((1, 512))],
        out_specs=[rev(512), rev(512), rev(512), rev(LANES), _full((LANES, 512)), _full((1, 512)), _full((1, 512))],
        out_shape=[jax.ShapeDtypeStruct((s, 512), BF), jax.ShapeDtypeStruct((s, 512), BF),
                   jax.ShapeDtypeStruct((s, 512), BF), jax.ShapeDtypeStruct((s, LANES), BF),
                   jax.ShapeDtypeStruct((LANES, 512), F32), jax.ShapeDtypeStruct((1, 512), F32),
                   jax.ShapeDtypeStruct((1, 512), F32)],
        scratch_shapes=[pltpu.VMEM((GLA_HEADS, LANES, LANES), F32)],
        compiler_params=_params(("arbitrary",)),
    )(proj, proj, proj, proj, dog, opre, sprev, wdecp, bdecp, ggla)


_SWA_COL_HEADS = (0, 2, 1, 3, 4, 6, 5, 7)
_SWA_COLS = SWA_HEADS * SWA_BLOCK


def _swa_masks():
    lo2 = lax.broadcasted_iota(jnp.int32, (2 * SWA_BLOCK, LANES), 1) < 64
    lane1 = lax.broadcasted_iota(jnp.int32, (SWA_BLOCK, LANES), 1)
    first_half = (lane1 % 64) < 32
    key = lax.broadcasted_iota(jnp.int32, (SWA_BLOCK, _SWA_COLS), 0)
    query = lax.broadcasted_iota(jnp.int32, (SWA_BLOCK, _SWA_COLS), 1) % SWA_BLOCK
    return lo2, lane1 < 64, first_half, key > query


def _merge_band(t, prev_mask, prev_bias=None):
    prev = t[:SWA_BLOCK] if prev_bias is None else t[:SWA_BLOCK] + prev_bias
    return jnp.where(prev_mask, prev, t[SWA_BLOCK:])


def _split_band(t, prev_mask_b):
    prev = t * prev_mask_b
    return jnp.concatenate([prev, t - prev], axis=0)


def _kv_variants(t, lo2):
    tr = pltpu.roll(t, 64, 1)
    lo_v = [jnp.where(lo2, t, 0.0).astype(BF), jnp.where(lo2, tr, 0.0).astype(BF)]
    hi_v = [jnp.where(lo2, 0.0, tr).astype(BF), jnp.where(lo2, 0.0, t).astype(BF)]
    return lo_v, hi_v


def _kv_variants_t(t):
    tt = t.T
    sw = jnp.concatenate([tt[64:], tt[:64]], axis=0)
    top = lax.broadcasted_iota(jnp.int32, tt.shape, 0) < 64
    lo_v = [jnp.where(top, tt, 0.0).astype(BF), jnp.where(top, sw, 0.0).astype(BF)]
    hi_v = [jnp.where(top, 0.0, sw).astype(BF), jnp.where(top, 0.0, tt).astype(BF)]
    return lo_v, hi_v


def _swa_softmax(qg, k_lo, k_hi, prev_mask, prev_bias, sinks_ref):
    st = jnp.concatenate([_dot(k_lo[0], qg[0], NT), _dot(k_hi[0], qg[0], NT),
                          _dot(k_lo[1], qg[1], NT), _dot(k_hi[1], qg[1], NT)], axis=1)
    st = _merge_band(st, prev_mask, prev_bias)
    sink = jnp.concatenate([jnp.full((1, SWA_BLOCK), sinks_ref[0, hd], F32) for hd in _SWA_COL_HEADS], axis=1)
    m = jnp.maximum(jnp.max(st, axis=0, keepdims=True), sink)
    ex = jnp.exp(st - m)
    es = jnp.exp(sink - m)
    inv = 1.0 / (jnp.sum(ex, axis=0, keepdims=True) + es)
    return ex, es, inv


def _no_prev_bias(block_index):
    return jnp.where(block_index > 0, 0.0, -1e30).astype(F32)


def _swa_queries(sq_ref, rows, cosb, sinb, first_half):
    qs = [_rope(sq_ref[rows, p * LANES:(p + 1) * LANES], cosb, sinb, first_half) * 0.125 for p in range(4)]
    return [jnp.concatenate(qs[0:2], axis=0), jnp.concatenate(qs[2:4], axis=0)]


def _swa_fwd(proj, cos, sin, sinks):
    s = proj.shape[0]
    nq = min(SWA_QBLOCKS, s // SWA_BLOCK)
    tq = nq * SWA_BLOCK

    def body(sq_ref, sz_ref, sk_ref, sv_ref, cos_ref, sin_ref, sinks_ref, os_ref, opre_ref, kprev, vprev):
        n = pl.program_id(0)

        @pl.when(n == 0)
        def _():
            kprev[...] = jnp.zeros_like(kprev)
            vprev[...] = jnp.zeros_like(vprev)

        lo2, _, first_half, prev_mask = _swa_masks()
        prev_mask_b = jnp.where(prev_mask, 1.0, 0.0).astype(BF)
        kp, vp = kprev[...], vprev[...]
        for j in range(nq):
            rows = slice(j * SWA_BLOCK, (j + 1) * SWA_BLOCK)
            cosb, sinb = cos_ref[rows, :], sin_ref[rows, :]
            kc = _rope(sk_ref[rows, :], cosb, sinb, first_half)
            vc = sv_ref[rows, :]
            k_lo, k_hi = _kv_variants(jnp.concatenate([kp, kc], axis=0), lo2)
            vt_lo, vt_hi = _kv_variants_t(jnp.concatenate([vp, vc], axis=0))
            qg = [q.astype(BF) for q in _swa_queries(sq_ref, rows, cosb, sinb, first_half)]
            ex, _, inv = _swa_softmax(qg, k_lo, k_hi, prev_mask, _no_prev_bias(n) if j == 0 else None, sinks_ref)
            pt = _split_band(ex.astype(BF), prev_mask_b)
            for g in range(2):
                c0, c1, c2 = 512 * g, 512 * g + 256, 512 * g + 512
                ot = _dot(vt_lo[g], pt[:, c0:c1]) * inv[:, c0:c1] + _dot(vt_hi[g], pt[:, c1:c2]) * inv[:, c1:c2]
                og = ot.T
                for i in range(2):
                    ls = slice((2 * g + i) * LANES, (2 * g + i + 1) * LANES)
                    o = og[i * SWA_BLOCK:(i + 1) * SWA_BLOCK]
                    sz = sz_ref[rows, ls]
                    opre_ref[rows, ls] = o
                    os_ref[rows, ls] = (o * (sz * _sigmoid(sz))).astype(os_ref.dtype)
            kp, vp = kc, vc
        kprev[...] = kp
        vprev[...] = vp

    def col(width, off):
        return pl.BlockSpec((tq, width), lambda i: (i, off // width))

    row = pl.BlockSpec((tq, LANES), lambda i: (i, 0))
    return pl.pallas_call(
        body, name="swa_fwd", grid=(s // tq,),
        in_specs=[col(512, OFF_SQ), col(512, OFF_SZ), col(LANES, OFF_SK), col(LANES, OFF_SV), row, row,
                  pl.BlockSpec(memory_space=pltpu.SMEM)],
        out_specs=[pl.BlockSpec((tq, 512), lambda i: (i, 0))] * 2,
        out_shape=[jax.ShapeDtypeStruct((s, 512), BF), jax.ShapeDtypeStruct((s, 512), F32)],
        scratch_shapes=[pltpu.VMEM((SWA_BLOCK, LANES), F32)] * 2,
        compiler_params=_params(("arbitrary",)),
    )(proj, proj, proj, proj, cos, sin, sinks)


def _swa_bwd(proj, dos, opre, cos, sin, sinks):
    s = proj.shape[0]
    nq = min(SWA_QBLOCKS, s // SWA_BLOCK)
    tq = nq * SWA_BLOCK

    def body(sq_ref, sz_ref, sk_ref, sv_ref, dos_ref, opre_ref, cos_ref, sin_ref, sinks_ref,
             dsq_ref, dsz_ref, dsk_ref, dsv_ref, dsink_ref, kprev, vprev, cprev, sprev):
        n = pl.program_id(0)

        @pl.when(n == 0)
        def _():
            kprev[...] = jnp.zeros_like(kprev)
            vprev[...] = jnp.zeros_like(vprev)
            cprev[...] = jnp.zeros_like(cprev)
            sprev[...] = jnp.zeros_like(sprev)
            for hd in range(SWA_HEADS):
                dsink_ref[0, hd] = 0.0

        lo2, lo1, first_half, prev_mask = _swa_masks()
        prev_mask_b = jnp.where(prev_mask, 1.0, 0.0).astype(BF)
        lo1s = jnp.concatenate([lo1, lo1], axis=0)

        def home(m0, m1):
            t0 = m0 + pltpu.roll(m0, 64, 1)
            t1 = m1 + pltpu.roll(m1, 64, 1)
            return jnp.where(lo2, t0, t1)

        kp, vp, cp_, sp_ = kprev[...], vprev[...], cprev[...], sprev[...]
        for j in range(nq):
            rows = slice(j * SWA_BLOCK, (j + 1) * SWA_BLOCK)
            blk = n * nq + j
            cosb, sinb = cos_ref[rows, :], sin_ref[rows, :]
            kc = _rope(sk_ref[rows, :], cosb, sinb, first_half)
            vc = sv_ref[rows, :]
            kcat = jnp.concatenate([kp, kc], axis=0)
            k_lo, k_hi = _kv_variants(kcat, lo2)
            kt_lo, kt_hi = _kv_variants_t(kcat)
            v_lo, v_hi = _kv_variants(jnp.concatenate([vp, vc], axis=0), lo2)
            qg32 = _swa_queries(sq_ref, rows, cosb, sinb, first_half)
            qg = [q.astype(BF) for q in qg32]
            ex, es, inv = _swa_softmax(qg, k_lo, k_hi, prev_mask, _no_prev_bias(n) if j == 0 else None, sinks_ref)
            pr, ps = ex * inv, es * inv

            dog32 = []
            for g in range(2):
                parts = []
                for i in range(2):
                    ls = slice((2 * g + i) * LANES, (2 * g + i + 1) * LANES)
                    sz = sz_ref[rows, ls]
                    sg = _sigmoid(sz)
                    dos_p = dos_ref[rows, ls]
                    dsz_ref[rows, ls] = (dos_p * opre_ref[rows, ls] * (sg * (1.0 + sz * (1.0 - sg)))).astype(dsz_ref.dtype)
                    parts.append(dos_p * (sz * sg))
                dog32.append(jnp.concatenate(parts, axis=0))
            dog = [t.astype(BF) for t in dog32]
            dpr = _merge_band(jnp.concatenate([_dot(v_lo[0], dog[0], NT), _dot(v_hi[0], dog[0], NT),
                                               _dot(v_lo[1], dog[1], NT), _dot(v_hi[1], dog[1], NT)], axis=1), prev_mask)
            rd = jnp.sum(pr * dpr, axis=0, keepdims=True)
            ds = _split_band((pr * (dpr - rd)).astype(BF), prev_mask_b)
            prb = _split_band(pr.astype(BF), prev_mask_b)
            sink_term = ps * rd
            for r, hd in enumerate(_SWA_COL_HEADS):
                dsink_ref[0, hd] += -jnp.sum(sink_term[:, r * SWA_BLOCK:(r + 1) * SWA_BLOCK])

            dk_g, dv_g = [], []
            for g in range(2):
                c0, c1, c2 = 512 * g, 512 * g + 256, 512 * g + 512
                dq = (_dot(kt_lo[g], ds[:, c0:c1]) + _dot(kt_hi[g], ds[:, c1:c2])).T
                for i in range(2):
                    ls = slice((2 * g + i) * LANES, (2 * g + i + 1) * LANES)
                    dsq_ref[rows, ls] = _rope_t(dq[i * SWA_BLOCK:(i + 1) * SWA_BLOCK] * 0.125, cosb, sinb,
                                                first_half).astype(dsq_ref.dtype)
                q_split = jnp.concatenate([jnp.where(lo1s, qg32[g], 0.0), jnp.where(lo1s, 0.0, qg32[g])], axis=0).astype(BF)
                do_split = jnp.concatenate([jnp.where(lo1s, dog32[g], 0.0), jnp.where(lo1s, 0.0, dog32[g])], axis=0).astype(BF)
                dk_g.append(_dot(ds[:, c0:c2], q_split))
                dv_g.append(_dot(prb[:, c0:c2], do_split))
            dk = home(dk_g[0], dk_g[1])
            dv = home(dv_g[0], dv_g[1])
            cur = pl.ds(pl.multiple_of(blk * SWA_BLOCK, SWA_BLOCK), SWA_BLOCK)
            dsk_ref[cur, :] = _rope_t(dk[SWA_BLOCK:], cosb, sinb, first_half)
            dsv_ref[cur, :] = dv[SWA_BLOCK:]
            dk_prev = _rope_t(dk[:SWA_BLOCK], cp_, sp_, first_half)
            dv_prev = dv[:SWA_BLOCK]
            if j == 0:
                @pl.when(n > 0)
                def _():
                    prv = pl.ds(pl.multiple_of((blk - 1) * SWA_BLOCK, SWA_BLOCK), SWA_BLOCK)
                    dsk_ref[prv, :] += dk_prev
                    dsv_ref[prv, :] += dv_prev
            else:
                prv = pl.ds(pl.multiple_of((blk - 1) * SWA_BLOCK, SWA_BLOCK), SWA_BLOCK)
                dsk_ref[prv, :] += dk_prev
                dsv_ref[prv, :] += dv_prev
            kp, vp, cp_, sp_ = kc, vc, cosb, sinb
        kprev[...] = kp
        vprev[...] = vp
        cprev[...] = cp_
        sprev[...] = sp_

    def col(width, off):
        return pl.BlockSpec((tq, width), lambda i: (i, off // width))

    row = pl.BlockSpec((tq, LANES), lambda i: (i, 0))
    wide = pl.BlockSpec((tq, 512), lambda i: (i, 0))
    return pl.pallas_call(
        body, name="swa_bwd", grid=(s // tq,),
        in_specs=[col(512, OFF_SQ), col(512, OFF_SZ), col(LANES, OFF_SK), col(LANES, OFF_SV), wide, wide, row, row,
                  pl.BlockSpec(memory_space=pltpu.SMEM)],
        out_specs=[wide, wide, _full((s, LANES)), _full((s, LANES)), pl.BlockSpec(memory_space=pltpu.SMEM)],
        out_shape=[jax.ShapeDtypeStruct((s, 512), BF), jax.ShapeDtypeStruct((s, 512), BF),
                   jax.ShapeDtypeStruct((s, LANES), F32), jax.ShapeDtypeStruct((s, LANES), F32),
                   jax.ShapeDtypeStruct((1, SWA_HEADS), F32)],
        scratch_shapes=[pltpu.VMEM((SWA_BLOCK, LANES), F32)] * 4,
        compiler_params=_params(("arbitrary",)),
    )(proj, proj, proj, proj, dos, opre, cos, sin, sinks)


def _outproj(og, osw, w_out, x2d, target, gate, g_final):
    s = x2d.shape[0]
    tm = min(512, s)

    def body(og_ref, os_ref, w_ref, x_ref, t_ref, gate_ref, gf_ref,
             dx2_ref, dog_ref, dos_ref, dw_ref, loss_ref, dgf_ref, dgate_ref):
        @pl.when(pl.program_id(0) == 0)
        def _():
            dw_ref[...] = jnp.zeros_like(dw_ref)
            loss_ref[...] = jnp.zeros_like(loss_ref)
            dgf_ref[...] = jnp.zeros_like(dgf_ref)
            dgate_ref[...] = jnp.zeros_like(dgate_ref)

        ogv, osv, w = og_ref[...], os_ref[...], w_ref[...]
        gate, gf = gate_ref[...], gf_ref[...]
        y = _dot(ogv, w[:512]) + _dot(osv, w[512:])
        x2 = x_ref[...] + gate * y
        r = lax.rsqrt(jnp.mean(x2 * x2, axis=-1, keepdims=True) + RMS_EPS)
        xn = x2 * r
        err = xn * gf - t_ref[...]
        loss_ref[...] += 0.5 * jnp.sum(jnp.mean(err * err, axis=-1, keepdims=True), axis=0, keepdims=True)
        dyf = err * (1.0 / D_MODEL)
        dgf_ref[...] += jnp.sum(dyf * xn, axis=0, keepdims=True)
        t = dyf * gf
        dx2 = r * (t - xn * jnp.mean(t * xn, axis=-1, keepdims=True))
        dx2_ref[...] = dx2
        dgate_ref[...] += jnp.sum(dx2 * y, axis=0, keepdims=True)
        dy = (dx2 * gate).astype(BF)
        dmix = _dot(dy, w, NT)
        dog_ref[...] = dmix[:, :512]
        dos_ref[...] = dmix[:, 512:]
        dw_ref[:512, :] += _dot(ogv, dy, TN)
        dw_ref[512:, :] += _dot(osv, dy, TN)

    half = pl.BlockSpec((tm, 512), lambda i: (i, 0))
    rowb = pl.BlockSpec((tm, D_MODEL), lambda i: (i, 0))
    vec = _full((1, D_MODEL))
    return pl.pallas_call(
        body, name="outproj", grid=(s // tm,),
        in_specs=[half, half, _full((D_MODEL, D_MODEL)), rowb, rowb, vec, vec],
        out_specs=[rowb, half, half, _full((D_MODEL, D_MODEL)), _full((1, 1)), vec, vec],
        out_shape=[jax.ShapeDtypeStruct((s, D_MODEL), F32), jax.ShapeDtypeStruct((s, 512), F32),
                   jax.ShapeDtypeStruct((s, 512), F32), jax.ShapeDtypeStruct((D_MODEL, D_MODEL), F32),
                   jax.ShapeDtypeStruct((1, 1), F32), jax.ShapeDtypeStruct((1, D_MODEL), F32),
                   jax.ShapeDtypeStruct((1, D_MODEL), F32)],
        compiler_params=_params(("arbitrary",)),
    )(og, osw, w_out, x2d, target, gate, g_final)


_PIECES = ((OFF_QK, 512), (OFF_V, 512), (OFF_GZ, 512), (OFF_SQ, 512), (OFF_SZ, 512),
           (OFF_SK, LANES), (OFF_SV, LANES), (OFF_GA, LANES))


def _inproj_bwd(x2d, shift, sc1p, g_norm, wpad_t, dx2, pieces):
    s = x2d.shape[0]
    tm = min(256, s)
    nsteps = s // tm

    def body(x_ref, sh_ref, sc_ref, g_ref, w_hbm, dx2_ref, *rest):
        piece_refs = rest[:len(_PIECES)]
        gx_ref, dw_hbm, dsh_ref, dsc_ref, dg_ref, w_vm, dw_vm, sem = rest[len(_PIECES):]
        i = pl.program_id(0)

        @pl.when(i == 0)
        def _():
            cp = pltpu.make_async_copy(w_hbm, w_vm, sem)
            cp.start()
            dw_vm[...] = jnp.zeros_like(dw_vm)
            dsh_ref[...] = jnp.zeros_like(dsh_ref)
            dsc_ref[...] = jnp.zeros_like(dsc_ref)
            dg_ref[...] = jnp.zeros_like(dg_ref)
            cp.wait()

        g, sc1p_v = g_ref[...], sc_ref[...]
        xn, r, h = _modnorm(x_ref[...], g, sc1p_v, sh_ref[...])
        hb = h.astype(BF)
        dh = None
        for (off, width), pr in zip(_PIECES, piece_refs):
            dp = pr[...].astype(BF)
            part = _dot(dp, w_vm[off:off + width, :])
            dh = part if dh is None else dh + part
            dw_vm[off:off + width, :] += _dot(dp, hb, TN)
        dsh_ref[...] += jnp.sum(dh, axis=0, keepdims=True)
        dsc_ref[...] += jnp.sum(dh * (xn * g), axis=0, keepdims=True)
        dg_ref[...] += jnp.sum(dh * xn * sc1p_v, axis=0, keepdims=True)
        dxn = dh * g * sc1p_v
        gx_ref[...] = dx2_ref[...] + r * (dxn - xn * jnp.mean(dxn * xn, axis=-1, keepdims=True))

        @pl.when(i == nsteps - 1)
        def _():
            cp = pltpu.make_async_copy(dw_vm, dw_hbm, sem)
            cp.start()
            cp.wait()

    rowb = pl.BlockSpec((tm, D_MODEL), lambda i: (i, 0))
    vec = _full((1, D_MODEL))
    anyspec = pl.BlockSpec(memory_space=pl.ANY)
    piece_specs = [pl.BlockSpec((tm, width), lambda i: (i, 0)) for _, width in _PIECES]
    return pl.pallas_call(
        body, name="inproj_bwd", grid=(nsteps,),
        in_specs=[rowb, vec, vec, vec, anyspec, rowb] + piece_specs,
        out_specs=[rowb, anyspec, vec, vec, vec],
        out_shape=[jax.ShapeDtypeStruct((s, D_MODEL), F32), jax.ShapeDtypeStruct((D_PAD, D_MODEL), F32),
                   jax.ShapeDtypeStruct((1, D_MODEL), F32), jax.ShapeDtypeStruct((1, D_MODEL), F32),
                   jax.ShapeDtypeStruct((1, D_MODEL), F32)],
        scratch_shapes=[pltpu.VMEM((D_PAD, D_MODEL), BF), pltpu.VMEM((D_PAD, D_MODEL), F32), pltpu.SemaphoreType.DMA],
        compiler_params=_params(("arbitrary",)),
    )(x2d, shift, sc1p, g_norm, wpad_t, dx2, *pieces)


def _adam(w, g, m, v):
    m2 = ADAM_B1 * m + (1.0 - ADAM_B1) * g
    v2 = ADAM_B2 * v + (1.0 - ADAM_B2) * (g * g)
    m_hat = m2 / (1.0 - ADAM_B1 ** ADAM_STEP)
    v_hat = v2 / (1.0 - ADAM_B2 ** ADAM_STEP)
    delta = -ADAM_LR * (m_hat / (jnp.sqrt(v_hat) + ADAM_EPS) + ADAM_WD * w)
    return delta, m2, v2


def _adamw(w, g, m, v, name):
    rr, cc = w.shape
    tc = min(256, cc)

    def body(w_ref, g_ref, m_ref, v_ref, d_ref, m2_ref, v2_ref):
        d_ref[...], m2_ref[...], v2_ref[...] = _adam(w_ref[...], g_ref[...], m_ref[...], v_ref[...])

    blk = pl.BlockSpec((rr, tc), lambda i: (0, i))
    return pl.pallas_call(
        body, name=name, grid=(cc // tc,), in_specs=[blk] * 4, out_specs=[blk] * 3,
        out_shape=[jax.ShapeDtypeStruct((rr, cc), F32)] * 3,
        compiler_params=_params(("arbitrary",)),
    )(w, g, m, v)


def _ada_update(c_all, dmod_cols, w, m, v):
    rr, cc = w.shape
    tr = min(256, rr)
    c_all = jnp.pad(c_all, ((0, 8), (0, 0)))
    dmod_cols = jnp.pad(dmod_cols, ((0, 8), (0, 0)))

    def body(c_ref, dm_ref, w_ref, m_ref, v_ref, g_ref, d_ref, m2_ref, v2_ref):
        cv = c_ref[...]
        sc = (cv * _sigmoid(cv)).astype(BF)
        g = _dot(sc, dm_ref[...].astype(BF), TN)
        g_ref[...] = g
        d_ref[...], m2_ref[...], v2_ref[...] = _adam(w_ref[...], g, m_ref[...], v_ref[...])

    blk = pl.BlockSpec((tr, cc), lambda i: (i, 0))
    return pl.pallas_call(
        body, name="ada_update", grid=(rr // tr,),
        in_specs=[pl.BlockSpec((16, tr), lambda i: (0, i)), _full((16, cc)), blk, blk, blk],
        out_specs=[blk] * 4, out_shape=[jax.ShapeDtypeStruct((rr, cc), F32)] * 4,
        compiler_params=_params(("arbitrary",)),
    )(c_all, dmod_cols, w, m, v)


def _small_update(parts, weights, moms, vels):
    n = len(weights)

    def body(*refs):
        p_refs, w_refs, m_refs, v_refs = refs[:n + 1], refs[n + 1:2 * n + 1], refs[2 * n + 1:3 * n + 1], refs[3 * n + 1:4 * n + 1]
        outs = refs[4 * n + 1:]
        for i in range(n):
            g = p_refs[i][0]
            for d in range(1, 8):
                g = g + p_refs[i][d]
            delta, m2, v2 = _adam(w_refs[i][...], g, m_refs[i][...], v_refs[i][...])
            outs[4 * i][...] = g
            outs[4 * i + 1][...] = delta
            outs[4 * i + 2][...] = m2
            outs[4 * i + 3][...] = v2
        tot = p_refs[n][0]
        for d in range(1, 8):
            tot = tot + p_refs[n][d]
        outs[4 * n][...] = tot

    out_shape = []
    for w in weights:
        out_shape += [jax.ShapeDtypeStruct(w.shape, F32)] * 4
    out_shape.append(jax.ShapeDtypeStruct(parts[n].shape[1:], F32))
    return pl.pallas_call(body, name="small_update", out_shape=out_shape, compiler_params=_params())(
        *parts, *weights, *moms, *vels)


def _pad_w_in_t(w):
    gq, gk = w[0:256], w[256:512]
    qk = []
    for h in range(GLA_HEADS):
        qk += [gq[64 * h:64 * h + 64], gk[64 * h:64 * h + 64]]
    gv, ga, gz = w[512:1024], w[1024:1040], w[1040:1552]
    sq, sk, sv, sz = w[1552:2064], w[2064:2192], w[2192:2320], w[2320:2832]
    pad = jnp.zeros((LANES - GLA_RANK, w.shape[1]), w.dtype)
    return jnp.concatenate(qk + [gv, gz, sq, sz, sk, sv, ga, pad], axis=0)


def _unpad_dw_in_t(d):
    qk = d[0:512]
    gq = jnp.concatenate([qk[128 * h:128 * h + 64] for h in range(GLA_HEADS)], axis=0)
    gk = jnp.concatenate([qk[128 * h + 64:128 * h + 128] for h in range(GLA_HEADS)], axis=0)
    return jnp.concatenate([gq, gk, d[OFF_V:OFF_V + 512], d[OFF_GA:OFF_GA + GLA_RANK], d[OFF_GZ:OFF_GZ + 512],
                            d[OFF_SQ:OFF_SQ + 512], d[OFF_SK:OFF_SK + 128], d[OFF_SV:OFF_SV + 128],
                            d[OFF_SZ:OFF_SZ + 512]], axis=0)


def _dup_heads(t):
    parts = []
    for h in range(GLA_HEADS):
        parts += [t[..., 64 * h:64 * h + 64]] * 2
    return jnp.concatenate(parts, axis=-1)


def _rows8(a):
    flat = a.reshape(-1)
    rows = -(-flat.shape[0] // LANES)
    rows8 = -(-rows // 8) * 8
    flat = jnp.pad(flat, (0, rows8 * LANES - flat.shape[0]))
    return flat.reshape(rows8, LANES)


def kernel(x, c, positions, w_ada, b_ada, g_norm, w_in, w_decay, b_decay, g_gla_head, sinks, w_out, g_final, loss_target, m_w_ada, m_b_ada, m_g_norm, m_w_in, m_w_decay, m_b_decay, m_g_gla_head, m_sinks, m_w_out, m_g_final, v_w_ada, v_b_ada, v_g_norm, v_w_in, v_w_decay, v_b_decay, v_g_gla_head, v_sinks, v_w_out, v_g_final):
    ax, ay, ac = lax.axis_index("x"), lax.axis_index("y"), lax.axis_index("c")
    chip = 2 * ax + ay
    dev = 2 * chip + ac
    s = x.shape[1]
    x2d = x[0]
    target = loss_target[0]
    w_ada2, w_out2, w_dec2 = w_ada[0], w_out[0], w_decay[0]
    w_in_t, m_w_in_t, v_w_in_t = w_in[0].T, m_w_in[0].T, v_w_in[0].T
    ada_cols = w_ada2.shape[1]
    in_cols = w_in_t.shape[0]
    out_rows = w_out2.shape[0]
    half = D_MODEL // 2

    first = _all_gather(jnp.concatenate([c.reshape(8, LANES), w_dec2.reshape(8, LANES)], axis=0), "gather_c")
    first = first.reshape(8, 2, 8, LANES)
    c_all = first[:, 0].reshape(8, D_MODEL)
    w_dec_full = first[0::2, 1].reshape(4, GLA_RANK, 64).transpose(1, 0, 2).reshape(GLA_RANK, 256)
    b_shard = lax.dynamic_slice(b_ada, (0, chip * ada_cols), (1, ada_cols))
    mod_all = _all_gather(_ada_fwd(c_all, w_ada2, b_shard), "gather_mod")
    mod = mod_all.reshape(4, 2, 8, ada_cols)[:, 0]
    mod = lax.dynamic_slice(mod, (0, dev, 0), (4, 1, ada_cols)).reshape(1, 4 * ada_cols)
    shift, sc1p, gate = mod[:, :D_MODEL], 1.0 + mod[:, D_MODEL:2 * D_MODEL], mod[:, 2 * D_MODEL:]

    half_in = lax.dynamic_slice(w_in_t, (0, ac * half), (in_cols, half)).astype(BF)
    w_in_all = _all_gather(half_in, "gather_w_in").reshape(4, 2, in_cols, half)
    wpad_t = _pad_w_in_t(w_in_all.transpose(0, 2, 1, 3).reshape(4 * in_cols, D_MODEL))
    half_out = lax.dynamic_slice(w_out2, (ac * (out_rows // 2), 0), (out_rows // 2, D_MODEL)).astype(BF)
    w_out_all = _all_gather(half_out, "gather_w_out").reshape(D_MODEL, D_MODEL)

    wdecp = jnp.pad(_dup_heads(w_dec_full), ((0, LANES - GLA_RANK), (0, 0))).astype(BF)
    bdecp = _dup_heads(b_decay)
    inv_freq = 1.0 / (ROPE_THETA ** (jnp.arange(0, 64, 2, dtype=F32) / 64))
    cos, sin = _rope_tables(positions.reshape(s, 1), jnp.tile(inv_freq, 4).reshape(1, LANES))

    proj = _inproj_fwd(x2d, shift, sc1p, g_norm, wpad_t)
    og, o_gla, sprev = _gla_fwd(proj, wdecp, bdecp, g_gla_head)
    osw, o_swa = _swa_fwd(proj, cos, sin, sinks)
    dx2, dog, dos, dw_out, loss_p, dgf, dgate = _outproj(og, osw, w_out_all, x2d, target, gate, g_final.reshape(1, D_MODEL))
    dsq, dsz, dsk, dsv, dsinks = _swa_bwd(proj, dos, o_swa, cos, sin, sinks)
    dqk, dv, dgz, dga, dwdp, dbdp, dgg = _gla_bwd(proj, dog, o_gla, sprev, wdecp, bdecp, g_gla_head)
    pieces = (dqk, dv, dgz, dsq, dsz, dsk, dsv, dga)
    gx, dwpad_t, dshift, dscale, dgn = _inproj_bwd(x2d, shift, sc1p, g_norm, wpad_t, dx2, pieces)

    g_w_in_t = _reduce_scatter(_unpad_dw_in_t(dwpad_t).reshape(4, in_cols, D_MODEL), "reduce_w_in")
    g_w_out = _reduce_scatter(dw_out.reshape(4, out_rows, D_MODEL), "reduce_w_out")

    dwd = jnp.concatenate([dwdp[:GLA_RANK, 128 * h:128 * h + 64] for h in range(GLA_HEADS)], axis=1)
    dbd = jnp.concatenate([dbdp[:, 128 * h:128 * h + 64] for h in range(GLA_HEADS)], axis=1)
    segs = [jnp.concatenate([dshift, dscale, dgate], axis=1), dgn, dgf, dwd, dbd, dgg, dsinks, loss_p]
    packed = [_rows8(a) for a in segs]
    offs = [0]
    for a in packed:
        offs.append(offs[-1] + a.shape[0])
    small = _all_gather(jnp.concatenate(packed, axis=0), "gather_small")

    def seg(i, size):
        return small[:, offs[i]:offs[i + 1]].reshape(8, -1)[:, :size]

    dmod_all = seg(0, 3 * D_MODEL)
    dwd_all = lax.dynamic_slice(seg(3, GLA_RANK * 256).reshape(8, GLA_RANK, 256), (0, 0, chip * 64), (8, GLA_RANK, 64))
    parts = [dmod_all.reshape(8, 1, 3 * D_MODEL), seg(1, D_MODEL).reshape(8, 1, D_MODEL), dwd_all,
             seg(4, 256).reshape(8, 1, 256), seg(5, 512).reshape(8, 1, 512), seg(6, SWA_HEADS).reshape(8, 1, SWA_HEADS),
             seg(2, D_MODEL).reshape(8, 1, D_MODEL), seg(7, LANES).reshape(8, 1, LANES)]
    smalls = _small_update(
        parts,
        [b_ada, g_norm, w_dec2, b_decay, g_gla_head, sinks, g_final.reshape(1, D_MODEL)],
        [m_b_ada, m_g_norm, m_w_decay[0], m_b_decay, m_g_gla_head, m_sinks, m_g_final.reshape(1, D_MODEL)],
        [v_b_ada, v_g_norm, v_w_decay[0], v_b_decay, v_g_gla_head, v_sinks, v_g_final.reshape(1, D_MODEL)])
    (g_b_ada, d_b_ada, nm_b_ada, nv_b_ada, g_gn, d_gn, nm_gn, nv_gn, g_wd, d_wd, nm_wd, nv_wd,
     g_bd, d_bd, nm_bd, nv_bd, g_gg, d_gg, nm_gg, nv_gg, g_sk, d_sk, nm_sk, nv_sk,
     g_gf, d_gf, nm_gf, nv_gf, loss_row) = smalls
    loss = loss_row[0, 0]

    dmod_cols = lax.dynamic_slice(dmod_all, (0, chip * ada_cols), (8, ada_cols))
    g_w_ada, d_w_ada, nm_w_ada, nv_w_ada = _ada_update(c_all, dmod_cols, w_ada2, m_w_ada[0], v_w_ada[0])
    d_w_in_t, nm_w_in_t, nv_w_in_t = _adamw(w_in_t, g_w_in_t, m_w_in_t, v_w_in_t, "adamw_w_in")
    g_w_in, d_w_in, nm_w_in, nv_w_in = g_w_in_t.T, d_w_in_t.T, nm_w_in_t.T, nv_w_in_t.T
    d_w_out, nm_w_out, nv_w_out = _adamw(w_out2, g_w_out, m_w_out[0], v_w_out[0], "adamw_w_out")

    flat = lambda a: a.reshape(D_MODEL)
    grads = [g_w_ada[None], g_b_ada, g_gn, g_w_in[None], g_wd[None], g_bd, g_gg, g_sk, g_w_out[None], flat(g_gf)]
    deltas = [d_w_ada[None], d_b_ada, d_gn, d_w_in[None], d_wd[None], d_bd, d_gg, d_sk, d_w_out[None], flat(d_gf)]
    new_m = [nm_w_ada[None], nm_b_ada, nm_gn, nm_w_in[None], nm_wd[None], nm_bd, nm_gg, nm_sk, nm_w_out[None], flat(nm_gf)]
    new_v = [nv_w_ada[None], nv_b_ada, nv_gn, nv_w_in[None], nv_wd[None], nv_bd, nv_gg, nv_sk, nv_w_out[None], flat(nv_gf)]
    return (loss, gx[None], *grads, *deltas, *new_m, *new_v)
```

```python
import jax
import jax.numpy as jnp
from jax import lax
from jax.experimental import pallas as pl
from jax.experimental.pallas import tpu as pltpu

F32 = jnp.float32
BF = jnp.bfloat16

D_MODEL = 1024
GLA_HEADS = 4
GLA_DK = 64
GLA_CHUNK = 64
GLA_RANK = 16
GLA_TAU = 16.0
GLA_ROWS = 256
SWA_HEADS = 8
SWA_BLOCK = 128
SWA_QBLOCKS = 2
RMS_EPS = 1e-6
ROPE_THETA = 10000.0

OFF_QK, OFF_V, OFF_GZ, OFF_SQ, OFF_SZ, OFF_SK, OFF_SV, OFF_GA = 0, 512, 1024, 1536, 2048, 2560, 2688, 2816
D_PAD = 2944
D_IN = 2832
LANES = 128
VMEM_LIMIT = 56 * 1024 * 1024

ADAM_LR, ADAM_B1, ADAM_B2, ADAM_EPS, ADAM_WD, ADAM_STEP = 0.001, 0.9, 0.999, 1e-08, 0.01, 10

NT = (((1,), (1,)), ((), ()))
TN = (((0,), (0,)), ((), ()))
MESH = pl.DeviceIdType.MESH


def _dot(a, b, dims=None):
    if dims is None:
        return jnp.dot(a, b, preferred_element_type=F32)
    return lax.dot_general(a, b, dims, preferred_element_type=F32)


def _sigmoid(x):
    return 1.0 / (1.0 + jnp.exp(-x))


def _params(sem=None):
    return pltpu.CompilerParams(dimension_semantics=sem, vmem_limit_bytes=VMEM_LIMIT)


def _full(shape):
    return pl.BlockSpec(shape, lambda i: (0,) * len(shape))


_GATHER_SEMS = [pltpu.SemaphoreType.DMA((7,)), pltpu.SemaphoreType.DMA((7,)), pltpu.SemaphoreType.DMA]


class _Gather:
    def __init__(self, x_ref, out_ref, send_sems, recv_sems, local_sem):
        x, y, c = lax.axis_index("x"), lax.axis_index("y"), lax.axis_index("c")
        self.me, self.sibling, self.c = (x, y, c), (x, y, 1 - c), c
        self.chips = [(1 - x, y), (x, 1 - y), (1 - x, 1 - y)]
        self.x_ref, self.out_ref, self.send_sems, self.recv_sems = x_ref, out_ref, send_sems, recv_sems
        self.mine = pltpu.make_async_copy(x_ref, self._slab(*self.me), local_sem)

    def _slab(self, px, py, pc):
        return self.out_ref.at[4 * px + 2 * py + pc]

    def _copy(self, k, blk, to, src=None):
        return pltpu.make_async_remote_copy(
            src_ref=self._slab(*blk) if src is None else src, dst_ref=self._slab(*blk),
            send_sem=self.send_sems.at[k], recv_sem=self.recv_sems.at[k], device_id=to, device_id_type=MESH)

    def start(self):
        self.mine.start()
        self.sent = [self._copy(0, self.me, self.sibling, src=self.x_ref)]
        self.sent += [self._copy(1 + j, self.me, (*chip, self.c), src=self.x_ref) for j, chip in enumerate(self.chips)]
        for cp in self.sent:
            cp.start()

    def relay(self):
        for j, chip in enumerate(self.chips):
            self._copy(1 + j, (*chip, self.c), self.me).wait_recv()
            cp = self._copy(4 + j, (*chip, self.c), self.sibling)
            cp.start()
            self.sent.append(cp)

    def finish(self):
        self._copy(0, self.sibling, self.me).wait_recv()
        for j, chip in enumerate(self.chips):
            self._copy(4 + j, (*chip, 1 - self.c), self.me).wait_recv()
        for cp in self.sent:
            cp.wait_send()
        self.mine.wait()


def _prologue(cw, w_ada, b_shard, half_in, half_out, pos_col, inv_freq):
    s = pos_col.shape[0]
    rt = min(512, s)

    def body(cw_ref, wada_ref, b_ref, hin_ref, hout_ref, pos_ref, f_ref,
             first_ref, mod_ref, win_ref, wout_ref, cos_ref, sin_ref, mod_blk, *sems):
        g_c = _Gather(cw_ref, first_ref, *sems[0:3])
        g_in = _Gather(hin_ref, win_ref, *sems[3:6])
        g_out = _Gather(hout_ref, wout_ref, *sems[6:9])
        g_mod = _Gather(mod_blk, mod_ref, *sems[9:12])
        g_c.start()
        g_in.start()
        g_out.start()
        g_c.relay()
        g_c.finish()
        c_rows = [jnp.concatenate([first_ref[d, r:r + 1, :] for r in range(8)], axis=1) for d in range(8)]
        c_all = jnp.concatenate(c_rows, axis=0)
        sc = (c_all * _sigmoid(c_all)).astype(BF)
        mod_blk[...] = _dot(sc, wada_ref[...].astype(BF)) + b_ref[...]
        g_mod.start()

        def rope_rows(i, carry):
            rows = pl.ds(pl.multiple_of(i * rt, rt), rt)
            ang = pos_ref[rows, :].astype(F32) * f_ref[...]
            lane = lax.broadcasted_iota(jnp.int32, ang.shape, 1)
            cos_ref[rows, :] = jnp.cos(ang)
            sn = jnp.sin(ang)
            sin_ref[rows, :] = jnp.where((lane % 64) < 32, -sn, sn)
            return carry

        lax.fori_loop(0, s // rt, rope_rows, 0)
        g_mod.relay()
        g_out.relay()
        g_in.relay()
        g_mod.finish()
        g_out.finish()
        g_in.finish()

    vm = pl.BlockSpec(memory_space=pltpu.VMEM)
    return pl.pallas_call(
        body, name="prologue",
        out_shape=[jax.ShapeDtypeStruct((8,) + cw.shape, F32), jax.ShapeDtypeStruct((8, 8, w_ada.shape[1]), F32),
                   jax.ShapeDtypeStruct((8,) + half_in.shape, half_in.dtype),
                   jax.ShapeDtypeStruct((8,) + half_out.shape, half_out.dtype),
                   jax.ShapeDtypeStruct((s, LANES), F32), jax.ShapeDtypeStruct((s, LANES), F32)],
        in_specs=[vm] * 7, out_specs=[vm] * 6,
        scratch_shapes=[pltpu.VMEM((8, w_ada.shape[1]), F32)] + _GATHER_SEMS * 4,
        compiler_params=pltpu.CompilerParams(vmem_limit_bytes=VMEM_LIMIT),
    )(cw, w_ada, b_shard, half_in, half_out, pos_col, inv_freq)


def _reduce_scratch(rr, cc):
    c2 = cc // 2
    return [pltpu.VMEM((4, rr, c2), F32), pltpu.VMEM((4, rr, c2), F32), pltpu.VMEM((3, rr, c2), BF),
            pltpu.VMEM((3, rr, c2), BF), pltpu.VMEM((rr, c2), F32),
            pltpu.SemaphoreType.DMA((5,)), pltpu.SemaphoreType.DMA((5,)), pltpu.SemaphoreType.DMA((2,))]


class _Reduce:
    def __init__(self, p_hbm, out_ref, acc_ref, own_ref, send_ref, land_ref, res_ref, send_sems, recv_sems, local_sems):
        x, y, c = lax.axis_index("x"), lax.axis_index("y"), lax.axis_index("c")
        c2 = out_ref.shape[1] // 2
        self.c, self.my_chip, sibling = c, 2 * x + y, (x, y, 1 - c)
        self.chips = [(1 - x, y), (x, 1 - y), (1 - x, 1 - y)]
        mine = pl.ds(pl.multiple_of(c * c2, c2), c2)
        other = pl.ds(pl.multiple_of((1 - c) * c2, c2), c2)
        self.acc_ref, self.own_ref, self.send_ref, self.land_ref, self.res_ref = acc_ref, own_ref, send_ref, land_ref, res_ref
        self.send_sems, self.recv_sems = send_sems, recv_sems
        self.own = pltpu.make_async_copy(p_hbm.at[:, :, mine], own_ref, local_sems.at[0])
        self.swap = pltpu.make_async_remote_copy(
            src_ref=p_hbm.at[:, :, other], dst_ref=acc_ref, send_sem=send_sems.at[0], recv_sem=recv_sems.at[0],
            device_id=sibling, device_id_type=MESH)
        self.put = pltpu.make_async_copy(res_ref, out_ref.at[:, mine], local_sems.at[1])
        self.share = pltpu.make_async_remote_copy(
            src_ref=res_ref, dst_ref=out_ref.at[:, mine], send_sem=send_sems.at[4],
            recv_sem=recv_sems.at[4], device_id=sibling, device_id_type=MESH)

    def start(self):
        self.own.start()
        self.swap.start()

    def combine_and_send(self):
        self.own.wait()
        self.swap.wait()
        for j in range(4):
            self.acc_ref[j] = self.acc_ref[j] + self.own_ref[j]
        self.sends = []
        for k, (tx, ty) in enumerate(self.chips):
            self.send_ref[k] = self.acc_ref[2 * tx + ty].astype(self.send_ref.dtype)
            cp = pltpu.make_async_remote_copy(
                src_ref=self.send_ref.at[k], dst_ref=self.land_ref.at[k], send_sem=self.send_sems.at[1 + k],
                recv_sem=self.recv_sems.at[1 + k], device_id=(tx, ty, self.c), device_id_type=MESH)
            cp.start()
            self.sends.append(cp)

    def total_and_share(self):
        for cp in self.sends:
            cp.wait_recv()
        total = self.acc_ref[self.my_chip]
        for k in range(3):
            total = total + self.land_ref[k].astype(F32)
        self.res_ref[...] = total
        for cp in self.sends:
            cp.wait_send()
        self.put.start()
        self.share.start()

    def finish(self):
        self.put.wait()
        self.share.wait()


def _epilogue(dw_in_parts, dw_out_parts, small):
    _, r_in, cc = dw_in_parts.shape
    _, r_out, _ = dw_out_parts.shape
    n_red = len(_reduce_scratch(r_in, cc))

    def body(pin_hbm, pout_hbm, small_ref, gin_ref, gout_ref, small_all_ref, *scratch):
        red_in = _Reduce(pin_hbm, gin_ref, *scratch[0:n_red])
        red_out = _Reduce(pout_hbm, gout_ref, *scratch[n_red:2 * n_red])
        gat = _Gather(small_ref, small_all_ref, *scratch[2 * n_red:])
        red_out.start()
        red_in.start()
        gat.start()
        red_out.combine_and_send()
        red_in.combine_and_send()
        gat.relay()
        red_out.total_and_share()
        red_in.total_and_share()
        gat.finish()
        red_out.finish()
        red_in.finish()

    vm = pl.BlockSpec(memory_space=pltpu.VMEM)
    anyspec = pl.BlockSpec(memory_space=pl.ANY)
    return pl.pallas_call(
        body, name="epilogue",
        out_shape=[jax.ShapeDtypeStruct((r_in, cc), F32), jax.ShapeDtypeStruct((r_out, cc), F32),
                   jax.ShapeDtypeStruct((8,) + small.shape, F32)],
        in_specs=[anyspec, anyspec, vm], out_specs=[vm, vm, vm],
        scratch_shapes=_reduce_scratch(r_in, cc) + _reduce_scratch(r_out, cc) + _GATHER_SEMS,
        compiler_params=pltpu.CompilerParams(vmem_limit_bytes=VMEM_LIMIT),
    )(dw_in_parts, dw_out_parts, small)


def _rope(t, cosb, sinb, first_half):
    partner = jnp.where(first_half, pltpu.roll(t, 96, 1), pltpu.roll(t, 32, 1))
    return t * cosb + partner * sinb


def _rope_t(g, cosb, sinb, first_half):
    gs = g * sinb
    partner = jnp.where(first_half, pltpu.roll(gs, 96, 1), pltpu.roll(gs, 32, 1))
    return g * cosb + partner


def _modnorm(x, g, sc1p, shift):
    r = lax.rsqrt(jnp.mean(x * x, axis=-1, keepdims=True) + RMS_EPS)
    xn = x * r
    return xn, r, (xn * g) * sc1p + shift


def _inproj_fwd(x2d, shift, sc1p, g_norm, wpad_t):
    s = x2d.shape[0]
    tm = min(512, s)

    def body(x_ref, sh_ref, sc_ref, g_ref, w_ref, o_ref):
        _, _, h = _modnorm(x_ref[...], g_ref[...], sc_ref[...], sh_ref[...])
        o_ref[...] = _dot(h.astype(BF), w_ref[...], NT)

    vec = _full((1, D_MODEL))
    return pl.pallas_call(
        body, name="inproj_fwd", grid=(s // tm,),
        in_specs=[pl.BlockSpec((tm, D_MODEL), lambda i: (i, 0)), vec, vec, vec, _full((D_PAD, D_MODEL))],
        out_specs=pl.BlockSpec((tm, D_PAD), lambda i: (i, 0)),
        out_shape=jax.ShapeDtypeStruct((s, D_PAD), F32),
        compiler_params=_params(("arbitrary",)),
    )(x2d, shift, sc1p, g_norm, wpad_t)


def _split3(a):
    hi = a.astype(BF)
    r1 = a - hi.astype(F32)
    mid = r1.astype(BF)
    lo = (r1 - mid.astype(F32)).astype(BF)
    return hi, mid, lo


def _tri_matmul(tri, a):
    hi, mid, lo = _split3(a)
    return _dot(tri, hi) + _dot(tri, mid) + _dot(tri, lo)


def _chunks(tb):
    return [slice(c * GLA_CHUNK, (c + 1) * GLA_CHUNK) for c in range(tb // GLA_CHUNK)]


def _per_chunk_rows(rows, width):
    return jnp.concatenate([jnp.broadcast_to(r, (GLA_CHUNK, width)) for r in rows], axis=0)


def _gla_masks(tb):
    lane = lax.broadcasted_iota(jnp.int32, (1, 512), 1)
    lo512 = (lane % LANES) < GLA_DK
    sgn = jnp.where(lo512, 1.0, -1.0).astype(F32)
    qsc = jnp.where(lo512, GLA_DK ** -0.5, 1.0).astype(F32)
    lo_h = lax.broadcasted_iota(jnp.int32, (tb, LANES), 1) < GLA_DK
    row = lax.broadcasted_iota(jnp.int32, (tb, tb), 0)
    col = lax.broadcasted_iota(jnp.int32, (tb, tb), 1)
    same = (row // GLA_CHUNK) == (col // GLA_CHUNK)
    tril = jnp.where(same, row - col, -1) >= 0
    triu = jnp.where(same, col - row, -1) >= 0
    return lo_h, sgn, qsc, tril, triu


def _gla_block_common(qk, ga, wd, bd, tril_b, sgn, qsc):
    tb = qk.shape[0]
    z2 = _dot(ga.astype(BF), wd) + bd
    la2 = (jnp.minimum(z2, 0.0) - jnp.log1p(jnp.exp(-jnp.abs(z2)))) * (1.0 / GLA_TAU)
    b2 = _tri_matmul(tril_b, la2)
    bls = [b2[rs.stop - 1:rs.stop, :] for rs in _chunks(tb)]
    e = jnp.exp(b2 * sgn)
    f = jnp.exp(_per_chunk_rows(bls, 512) - b2)
    qkd = qk * e * qsc
    kt = qk * f
    decs = [jnp.exp(bl) for bl in bls]
    return z2, e, f, qkd, kt, decs


def _gla_fwd(proj, wdecp, bdecp, ggla):
    s = proj.shape[0]
    tb = min(GLA_ROWS, s)
    nch = tb // GLA_CHUNK

    def body(qk_ref, v_ref, gz_ref, ga_ref, wd_ref, bd_ref, gg_ref, og_ref, opre_ref, sprev_ref, st_ref):
        @pl.when(pl.program_id(0) == 0)
        def _():
            st_ref[...] = jnp.zeros_like(st_ref)

        lo_h, sgn, qsc, tril, _ = _gla_masks(tb)
        tril_b = jnp.where(tril, 1.0, 0.0).astype(BF)
        gg = gg_ref[...]
        _, _, _, qkd, kt, decs = _gla_block_common(qk_ref[...], ga_ref[...], wd_ref[...], bd_ref[...], tril_b, sgn, qsc)
        for h in range(GLA_HEADS):
            ls = slice(h * LANES, (h + 1) * LANES)
            a = jnp.where(lo_h, qkd[:, ls], 0.0).astype(BF)
            bm = jnp.where(lo_h, pltpu.roll(qkd[:, ls], 64, 1), 0.0).astype(BF)
            ktl = jnp.where(lo_h, pltpu.roll(kt[:, ls], 64, 1), 0.0).astype(BF)
            vh = v_ref[:, ls].astype(BF)
            p = jnp.where(tril, _dot(a, bm, NT), 0.0).astype(BF)
            o = _dot(p, vh)
            st = st_ref[h]
            inter = []
            for c, rs in enumerate(_chunks(tb)):
                sprev_ref[c, h] = st
                inter.append(_dot(a[rs], st.astype(BF), NT))
                st = st * decs[c][:, ls] + _dot(vh[rs], ktl[rs], TN)
            st_ref[h] = st
            o = o + jnp.concatenate(inter, axis=0)
            r = lax.rsqrt(jnp.mean(o * o, axis=-1, keepdims=True) + RMS_EPS)
            gzh = gz_ref[:, ls]
            opre_ref[:, ls] = o
            og_ref[:, ls] = (((o * r) * gg[:, ls]) * (gzh * _sigmoid(gzh))).astype(og_ref.dtype)

    def col(width, off):
        return pl.BlockSpec((tb, width), lambda i: (i, off // width))

    return pl.pallas_call(
        body, name="gla_fwd", grid=(s // tb,),
        in_specs=[col(512, OFF_QK), col(512, OFF_V), col(512, OFF_GZ), col(LANES, OFF_GA),
                  _full((LANES, 512)), _full((1, 512)), _full((1, 512))],
        out_specs=[pl.BlockSpec((tb, 512), lambda i: (i, 0)), pl.BlockSpec((tb, 512), lambda i: (i, 0)),
                   pl.BlockSpec((nch, GLA_HEADS, LANES, LANES), lambda i: (i, 0, 0, 0))],
        out_shape=[jax.ShapeDtypeStruct((s, 512), BF), jax.ShapeDtypeStruct((s, 512), F32),
                   jax.ShapeDtypeStruct((s // GLA_CHUNK, GLA_HEADS, LANES, LANES), F32)],
        scratch_shapes=[pltpu.VMEM((GLA_HEADS, LANES, LANES), F32)],
        compiler_params=_params(("arbitrary",)),
    )(proj, proj, proj, proj, wdecp, bdecp, ggla)


def _gla_bwd(proj, dog, opre, sprev, wdecp, bdecp, ggla):
    s = proj.shape[0]
    tb = min(GLA_ROWS, s)
    nch = tb // GLA_CHUNK
    nb = s // tb

    def body(qk_ref, v_ref, gz_ref, ga_ref, dog_ref, opre_ref, sprev_ref, wd_ref, bd_ref, gg_ref,
             dqk_ref, dv_ref, dgz_ref, dga_ref, dwd_ref, dbd_ref, dgg_ref, dst_ref):
        @pl.when(pl.program_id(0) == 0)
        def _():
            dst_ref[...] = jnp.zeros_like(dst_ref)
            dwd_ref[...] = jnp.zeros_like(dwd_ref)
            dbd_ref[...] = jnp.zeros_like(dbd_ref)
            dgg_ref[...] = jnp.zeros_like(dgg_ref)

        lo_h, sgn, qsc, tril, triu = _gla_masks(tb)
        tril_b = jnp.where(tril, 1.0, 0.0).astype(BF)
        triu_b = jnp.where(triu, 1.0, 0.0).astype(BF)
        last_row = (lax.broadcasted_iota(jnp.int32, (tb, LANES), 0) % GLA_CHUNK) == GLA_CHUNK - 1
        wd, gg = wd_ref[...], gg_ref[...]
        ga = ga_ref[...]
        z2, e, f, qkd, kt, decs = _gla_block_common(qk_ref[...], ga, wd, bd_ref[...], tril_b, sgn, qsc)
        chunks = _chunks(tb)
        db_parts = []
        for h in range(GLA_HEADS):
            ls = slice(h * LANES, (h + 1) * LANES)
            e_h, f_h = e[:, ls], f[:, ls]
            a32 = jnp.where(lo_h, qkd[:, ls], 0.0)
            bm32 = jnp.where(lo_h, pltpu.roll(qkd[:, ls], 64, 1), 0.0)
            kt32 = jnp.where(lo_h, pltpu.roll(kt[:, ls], 64, 1), 0.0)
            a, bm, ktl = a32.astype(BF), bm32.astype(BF), kt32.astype(BF)
            vh = v_ref[:, ls].astype(BF)
            p = jnp.where(tril, _dot(a, bm, NT), 0.0).astype(BF)

            o = opre_ref[:, ls]
            gzh = gz_ref[:, ls]
            dogh = dog_ref[:, ls]
            r = lax.rsqrt(jnp.mean(o * o, axis=-1, keepdims=True) + RMS_EPS)
            ohat = o * r
            sg = _sigmoid(gzh)
            sil = gzh * sg
            g_h = gg[:, ls]
            dgz_ref[:, ls] = (dogh * (ohat * g_h) * (sg * (1.0 + gzh * (1.0 - sg)))).astype(dgz_ref.dtype)
            dn = dogh * sil * g_h
            dgg_ref[:, ls] += jnp.sum(dogh * sil * ohat, axis=0, keepdims=True)
            do = (r * (dn - ohat * jnp.mean(dn * ohat, axis=-1, keepdims=True))).astype(BF)

            dp = jnp.where(tril, _dot(do, vh, NT), 0.0).astype(BF)
            dv = _dot(p, do, TN)
            dqd = _dot(dp, bm)
            dkd = _dot(dp, a, TN)
            d = dst_ref[h]
            dv_s, dqd_s, dkt_s, ddec = [None] * nch, [None] * nch, [None] * nch, [None] * nch
            for c in reversed(range(nch)):
                rs = chunks[c]
                st = sprev_ref[c, h]
                d_b = d.astype(BF)
                dv_s[c] = _dot(ktl[rs], d_b, NT)
                dqd_s[c] = _dot(do[rs], st.astype(BF))
                dkt_s[c] = _dot(vh[rs], d_b)
                ddec[c] = jnp.sum(d * st, axis=0, keepdims=True)
                d = d * decs[c][:, ls] + _dot(do[rs], a[rs], TN)
            dst_ref[h] = d
            dv_ref[:, ls] = (dv + jnp.concatenate(dv_s, axis=0)).astype(dv_ref.dtype)
            dqd = dqd + jnp.concatenate(dqd_s, axis=0)
            dkt = jnp.concatenate(dkt_s, axis=0)

            dq = dqd * e_h * (GLA_DK ** -0.5)
            dk = dkd * pltpu.roll(e_h, 64, 1) + dkt * f_h
            dqk_ref[:, ls] = jnp.where(lo_h, dq, pltpu.roll(jnp.where(lo_h, dk, 0.0), 64, 1)).astype(dqk_ref.dtype)
            dkt_kt = dkt * kt32
            db = dqd * a32 - dkd * bm32 - dkt_kt
            dbl = [jnp.sum(dkt_kt[rs], axis=0, keepdims=True) + ddec[c] * decs[c][:, ls] for c, rs in enumerate(chunks)]
            db = jnp.where(last_row, db + _per_chunk_rows(dbl, LANES), db)
            db_parts.append(jnp.where(lo_h, db, 0.0))
        db2 = jnp.concatenate(db_parts, axis=1)
        dla = _tri_matmul(triu_b, db2)
        dz32 = dla * (1.0 / GLA_TAU) * _sigmoid(-z2)
        dz = dz32.astype(BF)
        dga_ref[...] = _dot(dz, wd, NT).astype(dga_ref.dtype)
        dwd_ref[...] += _dot(ga.astype(BF), dz, TN)
        dbd_ref[...] += jnp.sum(dz32, axis=0, keepdims=True)

    def col(width, off):
        return pl.BlockSpec((tb, width), lambda i: (nb - 1 - i, off // width))

    def rev(width):
        return pl.BlockSpec((tb, width), lambda i: (nb - 1 - i, 0))

    return pl.pallas_call(
        body, name="gla_bwd", grid=(nb,),
        in_specs=[col(512, OFF_QK), col(512, OFF_V), col(512, OFF_GZ), col(LANES, OFF_GA), rev(512), rev(512),
                  pl.BlockSpec((nch, GLA_HEADS, LANES, LANES), lambda i: (nb - 1 - i, 0, 0, 0)),
                  _full((LANES, 512)), _full((1, 512)), _full((1, 512))],
        out_specs=[rev(512), rev(512), rev(512), rev(LANES), _full((LANES, 512)), _full((1, 512)), _full((1, 512))],
        out_shape=[jax.ShapeDtypeStruct((s, 512), BF), jax.ShapeDtypeStruct((s, 512), BF),
                   jax.ShapeDtypeStruct((s, 512), BF), jax.ShapeDtypeStruct((s, LANES), BF),
                   jax.ShapeDtypeStruct((LANES, 512), F32), jax.ShapeDtypeStruct((1, 512), F32),
                   jax.ShapeDtypeStruct((1, 512), F32)],
        scratch_shapes=[pltpu.VMEM((GLA_HEADS, LANES, LANES), F32)],
        compiler_params=_params(("arbitrary",)),
    )(proj, proj, proj, proj, dog, opre, sprev, wdecp, bdecp, ggla)


_SWA_COL_HEADS = (0, 2, 1, 3, 4, 6, 5, 7)
_SWA_COLS = SWA_HEADS * SWA_BLOCK


def _swa_masks():
    lo2 = lax.broadcasted_iota(jnp.int32, (2 * SWA_BLOCK, LANES), 1) < 64
    lane1 = lax.broadcasted_iota(jnp.int32, (SWA_BLOCK, LANES), 1)
    first_half = (lane1 % 64) < 32
    key = lax.broadcasted_iota(jnp.int32, (SWA_BLOCK, _SWA_COLS), 0)
    query = lax.broadcasted_iota(jnp.int32, (SWA_BLOCK, _SWA_COLS), 1) % SWA_BLOCK
    return lo2, lane1 < 64, first_half, key > query


def _merge_band(t, prev_mask, prev_bias=None):
    prev = t[:SWA_BLOCK] if prev_bias is None else t[:SWA_BLOCK] + prev_bias
    return jnp.where(prev_mask, prev, t[SWA_BLOCK:])


def _split_band(t, prev_mask_b):
    prev = t * prev_mask_b
    return jnp.concatenate([prev, t - prev], axis=0)


def _kv_variants(t, lo2):
    tr = pltpu.roll(t, 64, 1)
    lo_v = [jnp.where(lo2, t, 0.0).astype(BF), jnp.where(lo2, tr, 0.0).astype(BF)]
    hi_v = [jnp.where(lo2, 0.0, tr).astype(BF), jnp.where(lo2, 0.0, t).astype(BF)]
    return lo_v, hi_v


def _kv_variants_t(t):
    tt = t.T
    sw = jnp.concatenate([tt[64:], tt[:64]], axis=0)
    top = lax.broadcasted_iota(jnp.int32, tt.shape, 0) < 64
    lo_v = [jnp.where(top, tt, 0.0).astype(BF), jnp.where(top, sw, 0.0).astype(BF)]
    hi_v = [jnp.where(top, 0.0, sw).astype(BF), jnp.where(top, 0.0, tt).astype(BF)]
    return lo_v, hi_v


def _swa_softmax(qg, k_lo, k_hi, prev_mask, prev_bias, sinks_ref):
    st = jnp.concatenate([_dot(k_lo[0], qg[0], NT), _dot(k_hi[0], qg[0], NT),
                          _dot(k_lo[1], qg[1], NT), _dot(k_hi[1], qg[1], NT)], axis=1)
    st = _merge_band(st, prev_mask, prev_bias)
    sink = jnp.concatenate([jnp.full((1, SWA_BLOCK), sinks_ref[0, hd], F32) for hd in _SWA_COL_HEADS], axis=1)
    m = jnp.maximum(jnp.max(st, axis=0, keepdims=True), sink)
    ex = jnp.exp(st - m)
    es = jnp.exp(sink - m)
    inv = 1.0 / (jnp.sum(ex, axis=0, keepdims=True) + es)
    return ex, es, inv


def _no_prev_bias(block_index):
    return jnp.where(block_index > 0, 0.0, -1e30).astype(F32)


def _swa_queries(sq_ref, rows, cosb, sinb, first_half):
    qs = [_rope(sq_ref[rows, p * LANES:(p + 1) * LANES], cosb, sinb, first_half) * 0.125 for p in range(4)]
    return [jnp.concatenate(qs[0:2], axis=0), jnp.concatenate(qs[2:4], axis=0)]


def _swa_fwd(proj, cos, sin, sinks):
    s = proj.shape[0]
    nq = min(SWA_QBLOCKS, s // SWA_BLOCK)
    tq = nq * SWA_BLOCK

    def body(sq_ref, sz_ref, sk_ref, sv_ref, cos_ref, sin_ref, sinks_ref, os_ref, opre_ref, kprev, vprev):
        n = pl.program_id(0)

        @pl.when(n == 0)
        def _():
            kprev[...] = jnp.zeros_like(kprev)
            vprev[...] = jnp.zeros_like(vprev)

        lo2, _, first_half, prev_mask = _swa_masks()
        prev_mask_b = jnp.where(prev_mask, 1.0, 0.0).astype(BF)
        kp, vp = kprev[...], vprev[...]
        for j in range(nq):
            rows = slice(j * SWA_BLOCK, (j + 1) * SWA_BLOCK)
            cosb, sinb = cos_ref[rows, :], sin_ref[rows, :]
            kc = _rope(sk_ref[rows, :], cosb, sinb, first_half)
            vc = sv_ref[rows, :]
            k_lo, k_hi = _kv_variants(jnp.concatenate([kp, kc], axis=0), lo2)
            vt_lo, vt_hi = _kv_variants_t(jnp.concatenate([vp, vc], axis=0))
            qg = [q.astype(BF) for q in _swa_queries(sq_ref, rows, cosb, sinb, first_half)]
            ex, _, inv = _swa_softmax(qg, k_lo, k_hi, prev_mask, _no_prev_bias(n) if j == 0 else None, sinks_ref)
            pt = _split_band(ex.astype(BF), prev_mask_b)
            for g in range(2):
                c0, c1, c2 = 512 * g, 512 * g + 256, 512 * g + 512
                ot = _dot(vt_lo[g], pt[:, c0:c1]) * inv[:, c0:c1] + _dot(vt_hi[g], pt[:, c1:c2]) * inv[:, c1:c2]
                og = ot.T
                for i in range(2):
                    ls = slice((2 * g + i) * LANES, (2 * g + i + 1) * LANES)
                    o = og[i * SWA_BLOCK:(i + 1) * SWA_BLOCK]
                    sz = sz_ref[rows, ls]
                    opre_ref[rows, ls] = o
                    os_ref[rows, ls] = (o * (sz * _sigmoid(sz))).astype(os_ref.dtype)
            kp, vp = kc, vc
        kprev[...] = kp
        vprev[...] = vp

    def col(width, off):
        return pl.BlockSpec((tq, width), lambda i: (i, off // width))

    row = pl.BlockSpec((tq, LANES), lambda i: (i, 0))
    return pl.pallas_call(
        body, name="swa_fwd", grid=(s // tq,),
        in_specs=[col(512, OFF_SQ), col(512, OFF_SZ), col(LANES, OFF_SK), col(LANES, OFF_SV), row, row,
                  pl.BlockSpec(memory_space=pltpu.SMEM)],
        out_specs=[pl.BlockSpec((tq, 512), lambda i: (i, 0))] * 2,
        out_shape=[jax.ShapeDtypeStruct((s, 512), BF), jax.ShapeDtypeStruct((s, 512), F32)],
        scratch_shapes=[pltpu.VMEM((SWA_BLOCK, LANES), F32)] * 2,
        compiler_params=_params(("arbitrary",)),
    )(proj, proj, proj, proj, cos, sin, sinks)


def _swa_bwd(proj, dos, opre, cos, sin, sinks):
    s = proj.shape[0]
    nq = min(SWA_QBLOCKS, s // SWA_BLOCK)
    tq = nq * SWA_BLOCK

    def body(sq_ref, sz_ref, sk_ref, sv_ref, dos_ref, opre_ref, cos_ref, sin_ref, sinks_ref,
             dsq_ref, dsz_ref, dsk_ref, dsv_ref, dsink_ref, kprev, vprev, cprev, sprev):
        n = pl.program_id(0)

        @pl.when(n == 0)
        def _():
            kprev[...] = jnp.zeros_like(kprev)
            vprev[...] = jnp.zeros_like(vprev)
            cprev[...] = jnp.zeros_like(cprev)
            sprev[...] = jnp.zeros_like(sprev)
            for hd in range(SWA_HEADS):
                dsink_ref[0, hd] = 0.0

        lo2, lo1, first_half, prev_mask = _swa_masks()
        prev_mask_b = jnp.where(prev_mask, 1.0, 0.0).astype(BF)
        lo1s = jnp.concatenate([lo1, lo1], axis=0)

        def home(m0, m1):
            t0 = m0 + pltpu.roll(m0, 64, 1)
            t1 = m1 + pltpu.roll(m1, 64, 1)
            return jnp.where(lo2, t0, t1)

        kp, vp, cp_, sp_ = kprev[...], vprev[...], cprev[...], sprev[...]
        for j in range(nq):
            rows = slice(j * SWA_BLOCK, (j + 1) * SWA_BLOCK)
            blk = n * nq + j
            cosb, sinb = cos_ref[rows, :], sin_ref[rows, :]
            kc = _rope(sk_ref[rows, :], cosb, sinb, first_half)
            vc = sv_ref[rows, :]
            kcat = jnp.concatenate([kp, kc], axis=0)
            k_lo, k_hi = _kv_variants(kcat, lo2)
            kt_lo, kt_hi = _kv_variants_t(kcat)
            v_lo, v_hi = _kv_variants(jnp.concatenate([vp, vc], axis=0), lo2)
            qg32 = _swa_queries(sq_ref, rows, cosb, sinb, first_half)
            qg = [q.astype(BF) for q in qg32]
            ex, es, inv = _swa_softmax(qg, k_lo, k_hi, prev_mask, _no_prev_bias(n) if j == 0 else None, sinks_ref)
            pr, ps = ex * inv, es * inv

            dog32 = []
            for g in range(2):
                parts = []
                for i in range(2):
                    ls = slice((2 * g + i) * LANES, (2 * g + i + 1) * LANES)
                    sz = sz_ref[rows, ls]
                    sg = _sigmoid(sz)
                    dos_p = dos_ref[rows, ls]
                    dsz_ref[rows, ls] = (dos_p * opre_ref[rows, ls] * (sg * (1.0 + sz * (1.0 - sg)))).astype(dsz_ref.dtype)
                    parts.append(dos_p * (sz * sg))
                dog32.append(jnp.concatenate(parts, axis=0))
            dog = [t.astype(BF) for t in dog32]
            dpr = _merge_band(jnp.concatenate([_dot(v_lo[0], dog[0], NT), _dot(v_hi[0], dog[0], NT),
                                               _dot(v_lo[1], dog[1], NT), _dot(v_hi[1], dog[1], NT)], axis=1), prev_mask)
            rd = jnp.sum(pr * dpr, axis=0, keepdims=True)
            ds = _split_band((pr * (dpr - rd)).astype(BF), prev_mask_b)
            prb = _split_band(pr.astype(BF), prev_mask_b)
            sink_term = ps * rd
            for r, hd in enumerate(_SWA_COL_HEADS):
                dsink_ref[0, hd] += -jnp.sum(sink_term[:, r * SWA_BLOCK:(r + 1) * SWA_BLOCK])

            dk_g, dv_g = [], []
            for g in range(2):
                c0, c1, c2 = 512 * g, 512 * g + 256, 512 * g + 512
                dq = (_dot(kt_lo[g], ds[:, c0:c1]) + _dot(kt_hi[g], ds[:, c1:c2])).T
                for i in range(2):
                    ls = slice((2 * g + i) * LANES, (2 * g + i + 1) * LANES)
                    dsq_ref[rows, ls] = _rope_t(dq[i * SWA_BLOCK:(i + 1) * SWA_BLOCK] * 0.125, cosb, sinb,
                                                first_half).astype(dsq_ref.dtype)
                q_split = jnp.concatenate([jnp.where(lo1s, qg32[g], 0.0), jnp.where(lo1s, 0.0, qg32[g])], axis=0).astype(BF)
                do_split = jnp.concatenate([jnp.where(lo1s, dog32[g], 0.0), jnp.where(lo1s, 0.0, dog32[g])], axis=0).astype(BF)
                dk_g.append(_dot(ds[:, c0:c2], q_split))
                dv_g.append(_dot(prb[:, c0:c2], do_split))
            dk = home(dk_g[0], dk_g[1])
            dv = home(dv_g[0], dv_g[1])
            cur = pl.ds(pl.multiple_of(blk * SWA_BLOCK, SWA_BLOCK), SWA_BLOCK)
            dsk_ref[cur, :] = _rope_t(dk[SWA_BLOCK:], cosb, sinb, first_half)
            dsv_ref[cur, :] = dv[SWA_BLOCK:]
            dk_prev = _rope_t(dk[:SWA_BLOCK], cp_, sp_, first_half)
            dv_prev = dv[:SWA_BLOCK]
            if j == 0:
                @pl.when(n > 0)
                def _():
                    prv = pl.ds(pl.multiple_of((blk - 1) * SWA_BLOCK, SWA_BLOCK), SWA_BLOCK)
                    dsk_ref[prv, :] += dk_prev
                    dsv_ref[prv, :] += dv_prev
            else:
                prv = pl.ds(pl.multiple_of((blk - 1) * SWA_BLOCK, SWA_BLOCK), SWA_BLOCK)
                dsk_ref[prv, :] += dk_prev
                dsv_ref[prv, :] += dv_prev
            kp, vp, cp_, sp_ = kc, vc, cosb, sinb
        kprev[...] = kp
        vprev[...] = vp
        cprev[...] = cp_
        sprev[...] = sp_

    def col(width, off):
        return pl.BlockSpec((tq, width), lambda i: (i, off // width))

    row = pl.BlockSpec((tq, LANES), lambda i: (i, 0))
    wide = pl.BlockSpec((tq, 512), lambda i: (i, 0))
    return pl.pallas_call(
        body, name="swa_bwd", grid=(s // tq,),
        in_specs=[col(512, OFF_SQ), col(512, OFF_SZ), col(LANES, OFF_SK), col(LANES, OFF_SV), wide, wide, row, row,
                  pl.BlockSpec(memory_space=pltpu.SMEM)],
        out_specs=[wide, wide, _full((s, LANES)), _full((s, LANES)), pl.BlockSpec(memory_space=pltpu.SMEM)],
        out_shape=[jax.ShapeDtypeStruct((s, 512), BF), jax.ShapeDtypeStruct((s, 512), BF),
                   jax.ShapeDtypeStruct((s, LANES), F32), jax.ShapeDtypeStruct((s, LANES), F32),
                   jax.ShapeDtypeStruct((1, SWA_HEADS), F32)],
        scratch_shapes=[pltpu.VMEM((SWA_BLOCK, LANES), F32)] * 4,
        compiler_params=_params(("arbitrary",)),
    )(proj, proj, proj, proj, dos, opre, cos, sin, sinks)


def _outproj(og, osw, w_out, x2d, target, gate, g_final):
    s = x2d.shape[0]
    tm = min(512, s)

    def body(og_ref, os_ref, w_ref, x_ref, t_ref, gate_ref, gf_ref,
             dx2_ref, dog_ref, dos_ref, dw_ref, loss_ref, dgf_ref, dgate_ref):
        @pl.when(pl.program_id(0) == 0)
        def _():
            dw_ref[...] = jnp.zeros_like(dw_ref)
            loss_ref[...] = jnp.zeros_like(loss_ref)
            dgf_ref[...] = jnp.zeros_like(dgf_ref)
            dgate_ref[...] = jnp.zeros_like(dgate_ref)

        ogv, osv, w = og_ref[...], os_ref[...], w_ref[...]
        gate, gf = gate_ref[...], gf_ref[...]
        y = _dot(ogv, w[:512]) + _dot(osv, w[512:])
        x2 = x_ref[...] + gate * y
        r = lax.rsqrt(jnp.mean(x2 * x2, axis=-1, keepdims=True) + RMS_EPS)
        xn = x2 * r
        err = xn * gf - t_ref[...]
        loss_ref[...] += 0.5 * jnp.sum(jnp.mean(err * err, axis=-1, keepdims=True), axis=0, keepdims=True)
        dyf = err * (1.0 / D_MODEL)
        dgf_ref[...] += jnp.sum(dyf * xn, axis=0, keepdims=True)
        t = dyf * gf
        dx2 = r * (t - xn * jnp.mean(t * xn, axis=-1, keepdims=True))
        dx2_ref[...] = dx2
        dgate_ref[...] += jnp.sum(dx2 * y, axis=0, keepdims=True)
        dy = (dx2 * gate).astype(BF)
        dmix = _dot(dy, w, NT)
        dog_ref[...] = dmix[:, :512]
        dos_ref[...] = dmix[:, 512:]
        dw_ref[:512, :] += _dot(ogv, dy, TN)
        dw_ref[512:, :] += _dot(osv, dy, TN)

    half = pl.BlockSpec((tm, 512), lambda i: (i, 0))
    rowb = pl.BlockSpec((tm, D_MODEL), lambda i: (i, 0))
    vec = _full((1, D_MODEL))
    return pl.pallas_call(
        body, name="outproj", grid=(s // tm,),
        in_specs=[half, half, _full((D_MODEL, D_MODEL)), rowb, rowb, vec, vec],
        out_specs=[rowb, half, half, _full((D_MODEL, D_MODEL)), _full((1, 1)), vec, vec],
        out_shape=[jax.ShapeDtypeStruct((s, D_MODEL), F32), jax.ShapeDtypeStruct((s, 512), F32),
                   jax.ShapeDtypeStruct((s, 512), F32), jax.ShapeDtypeStruct((D_MODEL, D_MODEL), F32),
                   jax.ShapeDtypeStruct((1, 1), F32), jax.ShapeDtypeStruct((1, D_MODEL), F32),
                   jax.ShapeDtypeStruct((1, D_MODEL), F32)],
        compiler_params=_params(("arbitrary",)),
    )(og, osw, w_out, x2d, target, gate, g_final)


_PIECES = ((OFF_QK, 512), (OFF_V, 512), (OFF_GZ, 512), (OFF_SQ, 512), (OFF_SZ, 512),
           (OFF_SK, LANES), (OFF_SV, LANES), (OFF_GA, LANES))

_UNPAD_ROWS = tuple(
    [(OFF_QK + 128 * h, 64 * h, 64) for h in range(GLA_HEADS)]
    + [(OFF_QK + 128 * h + 64, 256 + 64 * h, 64) for h in range(GLA_HEADS)]
    + [(OFF_V, 512, 512), (OFF_GA, 1024, GLA_RANK), (OFF_GZ, 1040, 512), (OFF_SQ, 1552, 512),
       (OFF_SK, 2064, 128), (OFF_SV, 2192, 128), (OFF_SZ, 2320, 512)])


def _inproj_bwd(x2d, shift, sc1p, g_norm, wpad_t, dx2, pieces):
    s = x2d.shape[0]
    tm = min(512, s)
    nsteps = s // tm

    def body(x_ref, sh_ref, sc_ref, g_ref, w_hbm, dx2_ref, *rest):
        piece_refs = rest[:len(_PIECES)]
        gx_ref, dw_hbm, dsh_ref, dsc_ref, dg_ref, w_vm, dw_vm, sem, out_sems = rest[len(_PIECES):]
        i = pl.program_id(0)

        @pl.when(i == 0)
        def _():
            cp = pltpu.make_async_copy(w_hbm, w_vm, sem)
            cp.start()
            dw_vm[...] = jnp.zeros_like(dw_vm)
            dsh_ref[...] = jnp.zeros_like(dsh_ref)
            dsc_ref[...] = jnp.zeros_like(dsc_ref)
            dg_ref[...] = jnp.zeros_like(dg_ref)
            cp.wait()

        g, sc1p_v = g_ref[...], sc_ref[...]
        xn, r, h = _modnorm(x_ref[...], g, sc1p_v, sh_ref[...])
        hb = h.astype(BF)
        dh = None
        for (off, width), pr in zip(_PIECES, piece_refs):
            dp = pr[...].astype(BF)
            part = _dot(dp, w_vm[off:off + width, :])
            dh = part if dh is None else dh + part
            dw_vm[off:off + width, :] += _dot(dp, hb, TN)
        dsh_ref[...] += jnp.sum(dh, axis=0, keepdims=True)
        dsc_ref[...] += jnp.sum(dh * (xn * g), axis=0, keepdims=True)
        dg_ref[...] += jnp.sum(dh * xn * sc1p_v, axis=0, keepdims=True)
        dxn = dh * g * sc1p_v
        gx_ref[...] = dx2_ref[...] + r * (dxn - xn * jnp.mean(dxn * xn, axis=-1, keepdims=True))

        @pl.when(i == nsteps - 1)
        def _():
            copies = [pltpu.make_async_copy(dw_vm.at[src:src + n], dw_hbm.at[dst:dst + n], out_sems.at[k])
                      for k, (src, dst, n) in enumerate(_UNPAD_ROWS)]
            for cp in copies:
                cp.start()
            for cp in copies:
                cp.wait()

    rowb = pl.BlockSpec((tm, D_MODEL), lambda i: (i, 0))
    vec = _full((1, D_MODEL))
    anyspec = pl.BlockSpec(memory_space=pl.ANY)
    piece_specs = [pl.BlockSpec((tm, width), lambda i: (i, 0)) for _, width in _PIECES]
    return pl.pallas_call(
        body, name="inproj_bwd", grid=(nsteps,),
        in_specs=[rowb, vec, vec, vec, anyspec, rowb] + piece_specs,
        out_specs=[rowb, anyspec, vec, vec, vec],
        out_shape=[jax.ShapeDtypeStruct((s, D_MODEL), F32), jax.ShapeDtypeStruct((D_IN, D_MODEL), F32),
                   jax.ShapeDtypeStruct((1, D_MODEL), F32), jax.ShapeDtypeStruct((1, D_MODEL), F32),
                   jax.ShapeDtypeStruct((1, D_MODEL), F32)],
        scratch_shapes=[pltpu.VMEM((D_PAD, D_MODEL), BF), pltpu.VMEM((D_PAD, D_MODEL), F32), pltpu.SemaphoreType.DMA,
                        pltpu.SemaphoreType.DMA((len(_UNPAD_ROWS),))],
        compiler_params=_params(("arbitrary",)),
    )(x2d, shift, sc1p, g_norm, wpad_t, dx2, *pieces)


def _adam(w, g, m, v):
    m2 = ADAM_B1 * m + (1.0 - ADAM_B1) * g
    v2 = ADAM_B2 * v + (1.0 - ADAM_B2) * (g * g)
    m_hat = m2 / (1.0 - ADAM_B1 ** ADAM_STEP)
    v_hat = v2 / (1.0 - ADAM_B2 ** ADAM_STEP)
    delta = -ADAM_LR * (m_hat / (jnp.sqrt(v_hat) + ADAM_EPS) + ADAM_WD * w)
    return delta, m2, v2


def _adamw(w, g, m, v, name):
    rr, cc = w.shape
    tc = min(256, cc)

    def body(w_ref, g_ref, m_ref, v_ref, d_ref, m2_ref, v2_ref):
        d_ref[...], m2_ref[...], v2_ref[...] = _adam(w_ref[...], g_ref[...], m_ref[...], v_ref[...])

    blk = pl.BlockSpec((rr, tc), lambda i: (0, i))
    return pl.pallas_call(
        body, name=name, grid=(cc // tc,), in_specs=[blk] * 4, out_specs=[blk] * 3,
        out_shape=[jax.ShapeDtypeStruct((rr, cc), F32)] * 3,
        compiler_params=_params(("arbitrary",)),
    )(w, g, m, v)


def _ada_update(c_all, dmod_cols, w, m, v):
    rr, cc = w.shape
    tr = min(256, rr)
    c_all = jnp.pad(c_all, ((0, 8), (0, 0)))
    dmod_cols = jnp.pad(dmod_cols, ((0, 8), (0, 0)))

    def body(c_ref, dm_ref, w_ref, m_ref, v_ref, g_ref, d_ref, m2_ref, v2_ref):
        cv = c_ref[...]
        sc = (cv * _sigmoid(cv)).astype(BF)
        g = _dot(sc, dm_ref[...].astype(BF), TN)
        g_ref[...] = g
        d_ref[...], m2_ref[...], v2_ref[...] = _adam(w_ref[...], g, m_ref[...], v_ref[...])

    blk = pl.BlockSpec((tr, cc), lambda i: (i, 0))
    return pl.pallas_call(
        body, name="ada_update", grid=(rr // tr,),
        in_specs=[pl.BlockSpec((16, tr), lambda i: (0, i)), _full((16, cc)), blk, blk, blk],
        out_specs=[blk] * 4, out_shape=[jax.ShapeDtypeStruct((rr, cc), F32)] * 4,
        compiler_params=_params(("arbitrary",)),
    )(c_all, dmod_cols, w, m, v)


def _small_update(parts, weights, moms, vels):
    n = len(weights)

    def body(*refs):
        p_refs, w_refs, m_refs, v_refs = refs[:n + 1], refs[n + 1:2 * n + 1], refs[2 * n + 1:3 * n + 1], refs[3 * n + 1:4 * n + 1]
        outs = refs[4 * n + 1:]
        for i in range(n):
            g = p_refs[i][0]
            for d in range(1, 8):
                g = g + p_refs[i][d]
            delta, m2, v2 = _adam(w_refs[i][...], g, m_refs[i][...], v_refs[i][...])
            outs[4 * i][...] = g
            outs[4 * i + 1][...] = delta
            outs[4 * i + 2][...] = m2
            outs[4 * i + 3][...] = v2
        tot = p_refs[n][0]
        for d in range(1, 8):
            tot = tot + p_refs[n][d]
        outs[4 * n][...] = tot

    out_shape = []
    for w in weights:
        out_shape += [jax.ShapeDtypeStruct(w.shape, F32)] * 4
    out_shape.append(jax.ShapeDtypeStruct(parts[n].shape[1:], F32))
    return pl.pallas_call(body, name="small_update", out_shape=out_shape, compiler_params=_params())(
        *parts, *weights, *moms, *vels)


def _pad_w_in_t(w):
    gq, gk = w[0:256], w[256:512]
    qk = []
    for h in range(GLA_HEADS):
        qk += [gq[64 * h:64 * h + 64], gk[64 * h:64 * h + 64]]
    gv, ga, gz = w[512:1024], w[1024:1040], w[1040:1552]
    sq, sk, sv, sz = w[1552:2064], w[2064:2192], w[2192:2320], w[2320:2832]
    pad = jnp.zeros((LANES - GLA_RANK, w.shape[1]), w.dtype)
    return jnp.concatenate(qk + [gv, gz, sq, sz, sk, sv, ga, pad], axis=0)


def _dup_heads(t):
    parts = []
    for h in range(GLA_HEADS):
        parts += [t[..., 64 * h:64 * h + 64]] * 2
    return jnp.concatenate(parts, axis=-1)


def _rows8(a):
    flat = a.reshape(-1)
    rows = -(-flat.shape[0] // LANES)
    rows8 = -(-rows // 8) * 8
    flat = jnp.pad(flat, (0, rows8 * LANES - flat.shape[0]))
    return flat.reshape(rows8, LANES)


def kernel(x, c, positions, w_ada, b_ada, g_norm, w_in, w_decay, b_decay, g_gla_head, sinks, w_out, g_final, loss_target, m_w_ada, m_b_ada, m_g_norm, m_w_in, m_w_decay, m_b_decay, m_g_gla_head, m_sinks, m_w_out, m_g_final, v_w_ada, v_b_ada, v_g_norm, v_w_in, v_w_decay, v_b_decay, v_g_gla_head, v_sinks, v_w_out, v_g_final):
    ax, ay, ac = lax.axis_index("x"), lax.axis_index("y"), lax.axis_index("c")
    chip = 2 * ax + ay
    dev = 2 * chip + ac
    s = x.shape[1]
    x2d = x[0]
    target = loss_target[0]
    w_ada2, w_out2, w_dec2 = w_ada[0], w_out[0], w_decay[0]
    w_in_t, m_w_in_t, v_w_in_t = w_in[0].T, m_w_in[0].T, v_w_in[0].T
    ada_cols = w_ada2.shape[1]
    in_cols = w_in_t.shape[0]
    out_rows = w_out2.shape[0]
    half = D_MODEL // 2

    cw = jnp.concatenate([c.reshape(8, LANES), w_dec2.reshape(8, LANES)], axis=0)
    b_shard = lax.dynamic_slice(b_ada, (0, chip * ada_cols), (1, ada_cols))
    half_in = lax.dynamic_slice(w_in_t, (0, ac * half), (in_cols, half)).astype(BF)
    half_out = lax.dynamic_slice(w_out2, (ac * (out_rows // 2), 0), (out_rows // 2, D_MODEL)).astype(BF)
    inv_freq = 1.0 / (ROPE_THETA ** (jnp.arange(0, 64, 2, dtype=F32) / 64))
    first, mod_all, w_in_all, w_out_all, cos, sin = _prologue(
        cw, w_ada2, b_shard, half_in, half_out, positions.reshape(s, 1), jnp.tile(inv_freq, 4).reshape(1, LANES))

    first = first.reshape(8, 2, 8, LANES)
    c_all = first[:, 0].reshape(8, D_MODEL)
    w_dec_full = first[0::2, 1].reshape(4, GLA_RANK, 64).transpose(1, 0, 2).reshape(GLA_RANK, 256)
    mod = mod_all.reshape(4, 2, 8, ada_cols)[:, 0]
    mod = lax.dynamic_slice(mod, (0, dev, 0), (4, 1, ada_cols)).reshape(1, 4 * ada_cols)
    shift, sc1p, gate = mod[:, :D_MODEL], 1.0 + mod[:, D_MODEL:2 * D_MODEL], mod[:, 2 * D_MODEL:]
    w_in_all = w_in_all.reshape(4, 2, in_cols, half)
    wpad_t = _pad_w_in_t(w_in_all.transpose(0, 2, 1, 3).reshape(4 * in_cols, D_MODEL))
    w_out_all = w_out_all.reshape(D_MODEL, D_MODEL)

    wdecp = jnp.pad(_dup_heads(w_dec_full), ((0, LANES - GLA_RANK), (0, 0))).astype(BF)
    bdecp = _dup_heads(b_decay)

    proj = _inproj_fwd(x2d, shift, sc1p, g_norm, wpad_t)
    og, o_gla, sprev = _gla_fwd(proj, wdecp, bdecp, g_gla_head)
    osw, o_swa = _swa_fwd(proj, cos, sin, sinks)
    dx2, dog, dos, dw_out, loss_p, dgf, dgate = _outproj(og, osw, w_out_all, x2d, target, gate, g_final.reshape(1, D_MODEL))
    dsq, dsz, dsk, dsv, dsinks = _swa_bwd(proj, dos, o_swa, cos, sin, sinks)
    dqk, dv, dgz, dga, dwdp, dbdp, dgg = _gla_bwd(proj, dog, o_gla, sprev, wdecp, bdecp, g_gla_head)
    pieces = (dqk, dv, dgz, dsq, dsz, dsk, dsv, dga)
    gx, dw_in_t, dshift, dscale, dgn = _inproj_bwd(x2d, shift, sc1p, g_norm, wpad_t, dx2, pieces)

    dwd = jnp.concatenate([dwdp[:GLA_RANK, 128 * h:128 * h + 64] for h in range(GLA_HEADS)], axis=1)
    dbd = jnp.concatenate([dbdp[:, 128 * h:128 * h + 64] for h in range(GLA_HEADS)], axis=1)
    segs = [jnp.concatenate([dshift, dscale, dgate], axis=1), dgn, dgf, dwd, dbd, dgg, dsinks, loss_p]
    packed = [_rows8(a) for a in segs]
    offs = [0]
    for a in packed:
        offs.append(offs[-1] + a.shape[0])
    g_w_in_t, g_w_out, small = _epilogue(dw_in_t.reshape(4, in_cols, D_MODEL), dw_out.reshape(4, out_rows, D_MODEL),
                                         jnp.concatenate(packed, axis=0))

    def seg(i, size):
        return small[:, offs[i]:offs[i + 1]].reshape(8, -1)[:, :size]

    dmod_all = seg(0, 3 * D_MODEL)
    dwd_all = lax.dynamic_slice(seg(3, GLA_RANK * 256).reshape(8, GLA_RANK, 256), (0, 0, chip * 64), (8, GLA_RANK, 64))
    parts = [dmod_all.reshape(8, 1, 3 * D_MODEL), seg(1, D_MODEL).reshape(8, 1, D_MODEL), dwd_all,
             seg(4, 256).reshape(8, 1, 256), seg(5, 512).reshape(8, 1, 512), seg(6, SWA_HEADS).reshape(8, 1, SWA_HEADS),
             seg(2, D_MODEL).reshape(8, 1, D_MODEL), seg(7, LANES).reshape(8, 1, LANES)]
    smalls = _small_update(
        parts,
        [b_ada, g_norm, w_dec2, b_decay, g_gla_head, sinks, g_final.reshape(1, D_MODEL)],
        [m_b_ada, m_g_norm, m_w_decay[0], m_b_decay, m_g_gla_head, m_sinks, m_g_final.reshape(1, D_MODEL)],
        [v_b_ada, v_g_norm, v_w_decay[0], v_b_decay, v_g_gla_head, v_sinks, v_g_final.reshape(1, D_MODEL)])
    (g_b_ada, d_b_ada, nm_b_ada, nv_b_ada, g_gn, d_gn, nm_gn, nv_gn, g_wd, d_wd, nm_wd, nv_wd,
     g_bd, d_bd, nm_bd, nv_bd, g_gg, d_gg, nm_gg, nv_gg, g_sk, d_sk, nm_sk, nv_sk,
     g_gf, d_gf, nm_gf, nv_gf, loss_row) = smalls
    loss = loss_row[0, 0]

    dmod_cols = lax.dynamic_slice(dmod_all, (0, chip * ada_cols), (8, ada_cols))
    g_w_ada, d_w_ada, nm_w_ada, nv_w_ada = _ada_update(c_all, dmod_cols, w_ada2, m_w_ada[0], v_w_ada[0])
    d_w_in_t, nm_w_in_t, nv_w_in_t = _adamw(w_in_t, g_w_in_t, m_w_in_t, v_w_in_t, "adamw_w_in")
    g_w_in, d_w_in, nm_w_in, nv_w_in = g_w_in_t.T, d_w_in_t.T, nm_w_in_t.T, nv_w_in_t.T
    d_w_out, nm_w_out, nv_w_out = _adamw(w_out2, g_w_out, m_w_out[0], v_w_out[0], "adamw_w_out")

    flat = lambda a: a.reshape(D_MODEL)
    grads = [g_w_ada[None], g_b_ada, g_gn, g_w_in[None], g_wd[None], g_bd, g_gg, g_sk, g_w_out[None], flat(g_gf)]
    deltas = [d_w_ada[None], d_b_ada, d_gn, d_w_in[None], d_wd[None], d_bd, d_gg, d_sk, d_w_out[None], flat(d_gf)]
    new_m = [nm_w_ada[None], nm_b_ada, nm_gn, nm_w_in[None], nm_wd[None], nm_bd, nm_gg, nm_sk, nm_w_out[None], flat(nm_gf)]
    new_v = [nv_w_ada[None], nv_b_ada, nv_gn, nv_w_in[None], nv_wd[None], nv_bd, nv_gg, nv_sk, nv_w_out[None], flat(nv_gf)]
    return (loss, gx[None], *grads, *deltas, *new_m, *new_v)
```

```python
import jax
import jax.numpy as jnp
from jax import lax
from jax.experimental import pallas as pl
from jax.experimental.pallas import tpu as pltpu

F32 = jnp.float32
BF = jnp.bfloat16

D_MODEL = 1024
GLA_HEADS = 4
GLA_DK = 64
GLA_CHUNK = 64
GLA_RANK = 16
GLA_TAU = 16.0
GLA_ROWS = 256
SWA_HEADS = 8
SWA_BLOCK = 128
SWA_QBLOCKS = 8
RMS_EPS = 1e-6
ROPE_THETA = 10000.0

OFF_QK, OFF_V, OFF_GZ, OFF_SQ, OFF_SZ, OFF_SK, OFF_SV, OFF_GA = 0, 512, 1024, 1536, 2048, 2560, 2688, 2816
D_PAD = 2944
D_IN = 2832
LANES = 128
VMEM_LIMIT = 56 * 1024 * 1024

ADAM_LR, ADAM_B1, ADAM_B2, ADAM_EPS, ADAM_WD, ADAM_STEP = 0.001, 0.9, 0.999, 1e-08, 0.01, 10

NT = (((1,), (1,)), ((), ()))
TN = (((0,), (0,)), ((), ()))
MESH = pl.DeviceIdType.MESH


def _dot(a, b, dims=None):
    if dims is None:
        return jnp.dot(a, b, preferred_element_type=F32)
    return lax.dot_general(a, b, dims, preferred_element_type=F32)


def _sigmoid(x):
    return 1.0 / (1.0 + jnp.exp(-x))


def _params(sem=None):
    return pltpu.CompilerParams(dimension_semantics=sem, vmem_limit_bytes=VMEM_LIMIT)


def _full(shape):
    return pl.BlockSpec(shape, lambda i: (0,) * len(shape))


_GATHER_SEMS = [pltpu.SemaphoreType.DMA((7,)), pltpu.SemaphoreType.DMA((7,)), pltpu.SemaphoreType.DMA]


class _Gather:
    def __init__(self, x_ref, out_ref, send_sems, recv_sems, local_sem):
        x, y, c = lax.axis_index("x"), lax.axis_index("y"), lax.axis_index("c")
        self.me, self.sibling, self.c = (x, y, c), (x, y, 1 - c), c
        self.chips = [(1 - x, y), (x, 1 - y), (1 - x, 1 - y)]
        self.x_ref, self.out_ref, self.send_sems, self.recv_sems = x_ref, out_ref, send_sems, recv_sems
        self.mine = pltpu.make_async_copy(x_ref, self._slab(*self.me), local_sem)

    def _slab(self, px, py, pc):
        return self.out_ref.at[4 * px + 2 * py + pc]

    def _copy(self, k, blk, to, src=None):
        return pltpu.make_async_remote_copy(
            src_ref=self._slab(*blk) if src is None else src, dst_ref=self._slab(*blk),
            send_sem=self.send_sems.at[k], recv_sem=self.recv_sems.at[k], device_id=to, device_id_type=MESH)

    def start(self):
        self.mine.start()
        self.sent = [self._copy(0, self.me, self.sibling, src=self.x_ref)]
        self.sent += [self._copy(1 + j, self.me, (*chip, self.c), src=self.x_ref) for j, chip in enumerate(self.chips)]
        for cp in self.sent:
            cp.start()

    def relay(self):
        for j, chip in enumerate(self.chips):
            self._copy(1 + j, (*chip, self.c), self.me).wait_recv()
            cp = self._copy(4 + j, (*chip, self.c), self.sibling)
            cp.start()
            self.sent.append(cp)

    def finish(self):
        self._copy(0, self.sibling, self.me).wait_recv()
        for j, chip in enumerate(self.chips):
            self._copy(4 + j, (*chip, 1 - self.c), self.me).wait_recv()
        for cp in self.sent:
            cp.wait_send()
        self.mine.wait()


def _prologue(cw, w_ada, b_shard, half_in, half_out, pos_col, inv_freq):
    s = pos_col.shape[0]
    rt = min(512, s)

    def body(cw_ref, wada_ref, b_ref, hin_ref, hout_ref, pos_ref, f_ref,
             first_ref, mod_ref, win_ref, wout_ref, cos_ref, sin_ref, mod_blk, *sems):
        g_c = _Gather(cw_ref, first_ref, *sems[0:3])
        g_in = _Gather(hin_ref, win_ref, *sems[3:6])
        g_out = _Gather(hout_ref, wout_ref, *sems[6:9])
        g_mod = _Gather(mod_blk, mod_ref, *sems[9:12])
        g_c.start()
        g_in.start()
        g_out.start()
        g_c.relay()
        g_c.finish()
        c_rows = [jnp.concatenate([first_ref[d, r:r + 1, :] for r in range(8)], axis=1) for d in range(8)]
        c_all = jnp.concatenate(c_rows, axis=0)
        sc = (c_all * _sigmoid(c_all)).astype(BF)
        mod_blk[...] = _dot(sc, wada_ref[...].astype(BF)) + b_ref[...]
        g_mod.start()

        def rope_rows(i, carry):
            rows = pl.ds(pl.multiple_of(i * rt, rt), rt)
            ang = pos_ref[rows, :].astype(F32) * f_ref[...]
            lane = lax.broadcasted_iota(jnp.int32, ang.shape, 1)
            cos_ref[rows, :] = jnp.cos(ang)
            sn = jnp.sin(ang)
            sin_ref[rows, :] = jnp.where((lane % 64) < 32, -sn, sn)
            return carry

        lax.fori_loop(0, s // rt, rope_rows, 0)
        g_mod.relay()
        g_out.relay()
        g_in.relay()
        g_mod.finish()
        g_out.finish()
        g_in.finish()

    vm = pl.BlockSpec(memory_space=pltpu.VMEM)
    return pl.pallas_call(
        body, name="prologue",
        out_shape=[jax.ShapeDtypeStruct((8,) + cw.shape, F32), jax.ShapeDtypeStruct((8, 8, w_ada.shape[1]), F32),
                   jax.ShapeDtypeStruct((8,) + half_in.shape, half_in.dtype),
                   jax.ShapeDtypeStruct((8,) + half_out.shape, half_out.dtype),
                   jax.ShapeDtypeStruct((s, LANES), F32), jax.ShapeDtypeStruct((s, LANES), F32)],
        in_specs=[vm] * 7, out_specs=[vm] * 6,
        scratch_shapes=[pltpu.VMEM((8, w_ada.shape[1]), F32)] + _GATHER_SEMS * 4,
        compiler_params=pltpu.CompilerParams(vmem_limit_bytes=VMEM_LIMIT),
    )(cw, w_ada, b_shard, half_in, half_out, pos_col, inv_freq)


def _reduce_scratch(rr, cc):
    c2 = cc // 2
    return [pltpu.VMEM((4, rr, c2), F32), pltpu.VMEM((4, rr, c2), F32), pltpu.VMEM((3, rr, c2), BF),
            pltpu.VMEM((3, rr, c2), BF), pltpu.VMEM((rr, c2), F32),
            pltpu.SemaphoreType.DMA((5,)), pltpu.SemaphoreType.DMA((5,)), pltpu.SemaphoreType.DMA((2,))]


class _Reduce:
    def __init__(self, p_hbm, out_ref, acc_ref, own_ref, send_ref, land_ref, res_ref, send_sems, recv_sems, local_sems):
        x, y, c = lax.axis_index("x"), lax.axis_index("y"), lax.axis_index("c")
        c2 = out_ref.shape[1] // 2
        self.c, self.my_chip, sibling = c, 2 * x + y, (x, y, 1 - c)
        self.chips = [(1 - x, y), (x, 1 - y), (1 - x, 1 - y)]
        mine = pl.ds(pl.multiple_of(c * c2, c2), c2)
        other = pl.ds(pl.multiple_of((1 - c) * c2, c2), c2)
        self.acc_ref, self.own_ref, self.send_ref, self.land_ref, self.res_ref = acc_ref, own_ref, send_ref, land_ref, res_ref
        self.send_sems, self.recv_sems = send_sems, recv_sems
        self.own = pltpu.make_async_copy(p_hbm.at[:, :, mine], own_ref, local_sems.at[0])
        self.swap = pltpu.make_async_remote_copy(
            src_ref=p_hbm.at[:, :, other], dst_ref=acc_ref, send_sem=send_sems.at[0], recv_sem=recv_sems.at[0],
            device_id=sibling, device_id_type=MESH)
        self.put = pltpu.make_async_copy(res_ref, out_ref.at[:, mine], local_sems.at[1])
        self.share = pltpu.make_async_remote_copy(
            src_ref=res_ref, dst_ref=out_ref.at[:, mine], send_sem=send_sems.at[4],
            recv_sem=recv_sems.at[4], device_id=sibling, device_id_type=MESH)

    def start(self):
        self.own.start()
        self.swap.start()

    def combine_and_send(self):
        self.own.wait()
        self.swap.wait()
        for j in range(4):
            self.acc_ref[j] = self.acc_ref[j] + self.own_ref[j]
        self.sends = []
        for k, (tx, ty) in enumerate(self.chips):
            self.send_ref[k] = self.acc_ref[2 * tx + ty].astype(self.send_ref.dtype)
            cp = pltpu.make_async_remote_copy(
                src_ref=self.send_ref.at[k], dst_ref=self.land_ref.at[k], send_sem=self.send_sems.at[1 + k],
                recv_sem=self.recv_sems.at[1 + k], device_id=(tx, ty, self.c), device_id_type=MESH)
            cp.start()
            self.sends.append(cp)

    def total_and_share(self):
        for cp in self.sends:
            cp.wait_recv()
        total = self.acc_ref[self.my_chip]
        for k in range(3):
            total = total + self.land_ref[k].astype(F32)
        self.res_ref[...] = total
        for cp in self.sends:
            cp.wait_send()
        self.put.start()
        self.share.start()

    def finish(self):
        self.put.wait()
        self.share.wait()


def _epilogue(dw_in_parts, dw_out_parts, small):
    _, r_in, cc = dw_in_parts.shape
    _, r_out, _ = dw_out_parts.shape
    n_red = len(_reduce_scratch(r_in, cc))

    def body(pin_hbm, pout_hbm, small_ref, gin_ref, gout_ref, small_all_ref, *scratch):
        red_in = _Reduce(pin_hbm, gin_ref, *scratch[0:n_red])
        red_out = _Reduce(pout_hbm, gout_ref, *scratch[n_red:2 * n_red])
        gat = _Gather(small_ref, small_all_ref, *scratch[2 * n_red:])
        red_out.start()
        red_in.start()
        gat.start()
        red_out.combine_and_send()
        red_in.combine_and_send()
        gat.relay()
        red_out.total_and_share()
        red_in.total_and_share()
        gat.finish()
        red_out.finish()
        red_in.finish()

    vm = pl.BlockSpec(memory_space=pltpu.VMEM)
    anyspec = pl.BlockSpec(memory_space=pl.ANY)
    return pl.pallas_call(
        body, name="epilogue",
        out_shape=[jax.ShapeDtypeStruct((r_in, cc), F32), jax.ShapeDtypeStruct((r_out, cc), F32),
                   jax.ShapeDtypeStruct((8,) + small.shape, F32)],
        in_specs=[anyspec, anyspec, vm], out_specs=[vm, vm, vm],
        scratch_shapes=_reduce_scratch(r_in, cc) + _reduce_scratch(r_out, cc) + _GATHER_SEMS,
        compiler_params=pltpu.CompilerParams(vmem_limit_bytes=VMEM_LIMIT),
    )(dw_in_parts, dw_out_parts, small)


def _rope(t, cosb, sinb, first_half):
    partner = jnp.where(first_half, pltpu.roll(t, 96, 1), pltpu.roll(t, 32, 1))
    return t * cosb + partner * sinb


def _rope_t(g, cosb, sinb, first_half):
    gs = g * sinb
    partner = jnp.where(first_half, pltpu.roll(gs, 96, 1), pltpu.roll(gs, 32, 1))
    return g * cosb + partner


def _modnorm(x, g, sc1p, shift):
    r = lax.rsqrt(jnp.mean(x * x, axis=-1, keepdims=True) + RMS_EPS)
    xn = x * r
    return xn, r, (xn * g) * sc1p + shift


def _inproj_fwd(x2d, shift, sc1p, g_norm, wpad_t):
    s = x2d.shape[0]
    tm = min(512, s)

    def body(x_ref, sh_ref, sc_ref, g_ref, w_ref, o_ref):
        _, _, h = _modnorm(x_ref[...], g_ref[...], sc_ref[...], sh_ref[...])
        o_ref[...] = _dot(h.astype(BF), w_ref[...], NT)

    vec = _full((1, D_MODEL))
    return pl.pallas_call(
        body, name="inproj_fwd", grid=(s // tm,),
        in_specs=[pl.BlockSpec((tm, D_MODEL), lambda i: (i, 0)), vec, vec, vec, _full((D_PAD, D_MODEL))],
        out_specs=pl.BlockSpec((tm, D_PAD), lambda i: (i, 0)),
        out_shape=jax.ShapeDtypeStruct((s, D_PAD), F32),
        compiler_params=_params(("arbitrary",)),
    )(x2d, shift, sc1p, g_norm, wpad_t)


def _split3(a):
    hi = a.astype(BF)
    r1 = a - hi.astype(F32)
    mid = r1.astype(BF)
    lo = (r1 - mid.astype(F32)).astype(BF)
    return hi, mid, lo


def _tri_matmul(tri, a):
    hi, mid, lo = _split3(a)
    return _dot(tri, hi) + _dot(tri, mid) + _dot(tri, lo)


def _chunks(tb):
    return [slice(c * GLA_CHUNK, (c + 1) * GLA_CHUNK) for c in range(tb // GLA_CHUNK)]


def _per_chunk_rows(rows, width):
    return jnp.concatenate([jnp.broadcast_to(r, (GLA_CHUNK, width)) for r in rows], axis=0)


def _gla_masks(tb):
    lo_h = lax.broadcasted_iota(jnp.int32, (tb, LANES), 1) < GLA_DK
    row = lax.broadcasted_iota(jnp.int32, (tb, tb), 0)
    col = lax.broadcasted_iota(jnp.int32, (tb, tb), 1)
    same = (row // GLA_CHUNK) == (col // GLA_CHUNK)
    tril = jnp.where(same, row - col, -1) >= 0
    triu = jnp.where(same, col - row, -1) >= 0
    return lo_h, tril, triu


def _head(t, h, lo_h):
    blk = t[:, LANES * (h // 2):LANES * (h // 2 + 1)]
    return jnp.where(lo_h, blk, 0.0) if h % 2 == 0 else jnp.where(lo_h, 0.0, blk)


def _gla_block_common(qk, ga, wd, bd, tril_b):
    tb = qk.shape[0]
    q, k = qk[:, :256], qk[:, 256:]
    z = _dot(ga.astype(BF), wd) + bd
    la = (jnp.minimum(z, 0.0) - jnp.log(1.0 + jnp.exp(-jnp.abs(z)))) * (1.0 / GLA_TAU)
    b = _tri_matmul(tril_b, la)
    bls = [b[rs.stop - 1:rs.stop, :] for rs in _chunks(tb)]
    eq = jnp.exp(b)
    ek = jnp.exp(-b)
    f = jnp.exp(_per_chunk_rows(bls, 256) - b)
    return z, eq, ek, f, q * (eq * GLA_DK ** -0.5), k * ek, k * f, bls


def _gla_fwd(proj, wdecp, bdec, ggla):
    s = proj.shape[0]
    tb = min(GLA_ROWS, s)
    nch = tb // GLA_CHUNK

    def body(qk_ref, v_ref, gz_ref, ga_ref, wd_ref, bd_ref, gg_ref, og_ref, opre_ref, sprev_ref, st_ref):
        @pl.when(pl.program_id(0) == 0)
        def _():
            st_ref[...] = jnp.zeros_like(st_ref)

        lo_h, tril, _ = _gla_masks(tb)
        tril_b = jnp.where(tril, 1.0, 0.0).astype(BF)
        gg = gg_ref[...]
        _, _, _, _, qd, kd, kt, bls = _gla_block_common(qk_ref[...], ga_ref[...], wd_ref[...], bd_ref[...], tril_b)
        decs = [jnp.exp(bl) for bl in bls]
        for h in range(GLA_HEADS):
            ls = slice(h * LANES, (h + 1) * LANES)
            blk = slice(LANES * (h // 2), LANES * (h // 2 + 1))
            a = _head(qd, h, lo_h).astype(BF)
            bm = _head(kd, h, lo_h).astype(BF)
            ktl = _head(kt, h, lo_h).astype(BF)
            vh = v_ref[:, ls].astype(BF)
            p = jnp.where(tril, _dot(a, bm, NT), 0.0).astype(BF)
            o = _dot(p, vh)
            st = st_ref[h]
            inter = []
            for c, rs in enumerate(_chunks(tb)):
                sprev_ref[c, h] = st
                inter.append(_dot(a[rs], st.astype(BF), NT))
                st = st * decs[c][:, blk] + _dot(vh[rs], ktl[rs], TN)
            st_ref[h] = st
            o = o + jnp.concatenate(inter, axis=0)
            r = lax.rsqrt(jnp.mean(o * o, axis=-1, keepdims=True) + RMS_EPS)
            gzh = gz_ref[:, ls]
            opre_ref[:, ls] = o
            og_ref[:, ls] = (((o * r) * gg[:, ls]) * (gzh * _sigmoid(gzh))).astype(og_ref.dtype)

    def col(width, off):
        return pl.BlockSpec((tb, width), lambda i: (i, off // width))

    return pl.pallas_call(
        body, name="gla_fwd", grid=(s // tb,),
        in_specs=[col(512, OFF_QK), col(512, OFF_V), col(512, OFF_GZ), col(LANES, OFF_GA),
                  _full((LANES, 256)), _full((1, 256)), _full((1, 512))],
        out_specs=[pl.BlockSpec((tb, 512), lambda i: (i, 0)), pl.BlockSpec((tb, 512), lambda i: (i, 0)),
                   pl.BlockSpec((nch, GLA_HEADS, LANES, LANES), lambda i: (i, 0, 0, 0))],
        out_shape=[jax.ShapeDtypeStruct((s, 512), BF), jax.ShapeDtypeStruct((s, 512), F32),
                   jax.ShapeDtypeStruct((s // GLA_CHUNK, GLA_HEADS, LANES, LANES), F32)],
        scratch_shapes=[pltpu.VMEM((GLA_HEADS, LANES, LANES), F32)],
        compiler_params=_params(("arbitrary",)),
    )(proj, proj, proj, proj, wdecp, bdec, ggla)


def _gla_bwd(proj, dog, opre, sprev, wdecp, bdec, ggla):
    s = proj.shape[0]
    tb = min(GLA_ROWS, s)
    nch = tb // GLA_CHUNK
    nb = s // tb

    def body(qk_ref, v_ref, gz_ref, ga_ref, dog_ref, opre_ref, sprev_ref, wd_ref, bd_ref, gg_ref,
             dqk_ref, dv_ref, dgz_ref, dga_ref, dwd_ref, dbd_ref, dgg_ref, dst_ref):
        @pl.when(pl.program_id(0) == 0)
        def _():
            dst_ref[...] = jnp.zeros_like(dst_ref)
            dwd_ref[...] = jnp.zeros_like(dwd_ref)
            dbd_ref[...] = jnp.zeros_like(dbd_ref)
            dgg_ref[...] = jnp.zeros_like(dgg_ref)

        lo_h, tril, triu = _gla_masks(tb)
        tril_b = jnp.where(tril, 1.0, 0.0).astype(BF)
        triu_b = jnp.where(triu, 1.0, 0.0).astype(BF)
        last_row = (lax.broadcasted_iota(jnp.int32, (tb, LANES), 0) % GLA_CHUNK) == GLA_CHUNK - 1
        wd, gg = wd_ref[...], gg_ref[...]
        ga = ga_ref[...]
        z, eq, ek, f, qd, kd, kt, bls = _gla_block_common(qk_ref[...], ga, wd, bd_ref[...], tril_b)
        decs = [jnp.exp(bl) for bl in bls]
        chunks = _chunks(tb)
        db_parts = []
        for pair in range(GLA_HEADS // 2):
            blk = slice(LANES * pair, LANES * (pair + 1))
            dqd_b, dkd_b, dkt_b, ddec_b = None, None, None, [None] * nch
            for h in (2 * pair, 2 * pair + 1):
                ls = slice(h * LANES, (h + 1) * LANES)
                a = _head(qd, h, lo_h).astype(BF)
                bm = _head(kd, h, lo_h).astype(BF)
                ktl = _head(kt, h, lo_h).astype(BF)
                vh = v_ref[:, ls].astype(BF)
                p = jnp.where(tril, _dot(a, bm, NT), 0.0).astype(BF)

                o = opre_ref[:, ls]
                gzh = gz_ref[:, ls]
                dogh = dog_ref[:, ls]
                r = lax.rsqrt(jnp.mean(o * o, axis=-1, keepdims=True) + RMS_EPS)
                ohat = o * r
                sg = _sigmoid(gzh)
                sil = gzh * sg
                g_h = gg[:, ls]
                dgz_ref[:, ls] = (dogh * (ohat * g_h) * (sg * (1.0 + gzh * (1.0 - sg)))).astype(dgz_ref.dtype)
                dn = dogh * sil * g_h
                dgg_ref[:, ls] += jnp.sum(dogh * sil * ohat, axis=0, keepdims=True)
                do = (r * (dn - ohat * jnp.mean(dn * ohat, axis=-1, keepdims=True))).astype(BF)

                dp = jnp.where(tril, _dot(do, vh, NT), 0.0).astype(BF)
                dv = _dot(p, do, TN)
                dqd = _dot(dp, bm)
                dkd = _dot(dp, a, TN)
                d = dst_ref[h]
                dv_s, dqd_s, dkt_s = [None] * nch, [None] * nch, [None] * nch
                for c in reversed(range(nch)):
                    rs = chunks[c]
                    st = sprev_ref[c, h]
                    d_b = d.astype(BF)
                    dv_s[c] = _dot(ktl[rs], d_b, NT)
                    dqd_s[c] = _dot(do[rs], st.astype(BF))
                    dkt_s[c] = _dot(vh[rs], d_b)
                    dd = jnp.sum(d * st, axis=0, keepdims=True)
                    ddec_b[c] = dd if ddec_b[c] is None else ddec_b[c] + dd
                    d = d * decs[c][:, blk] + _dot(do[rs], a[rs], TN)
                dst_ref[h] = d
                dv_ref[:, ls] = (dv + jnp.concatenate(dv_s, axis=0)).astype(dv_ref.dtype)
                dqd = dqd + jnp.concatenate(dqd_s, axis=0)
                dkt = jnp.concatenate(dkt_s, axis=0)
                dqd_b = dqd if dqd_b is None else dqd_b + dqd
                dkd_b = dkd if dkd_b is None else dkd_b + dkd
                dkt_b = dkt if dkt_b is None else dkt_b + dkt

            dqk_ref[:, blk] = (dqd_b * (eq[:, blk] * GLA_DK ** -0.5)).astype(dqk_ref.dtype)
            dqk_ref[:, 256 + LANES * pair:256 + LANES * (pair + 1)] = (dkd_b * ek[:, blk] + dkt_b * f[:, blk]).astype(dqk_ref.dtype)
            dkt_kt = dkt_b * kt[:, blk]
            db = dqd_b * qd[:, blk] - dkd_b * kd[:, blk] - dkt_kt
            dbl = [jnp.sum(dkt_kt[rs], axis=0, keepdims=True) + ddec_b[c] * decs[c][:, blk] for c, rs in enumerate(chunks)]
            db_parts.append(jnp.where(last_row, db + _per_chunk_rows(dbl, LANES), db))
        dla = _tri_matmul(triu_b, jnp.concatenate(db_parts, axis=1))
        dz32 = dla * (1.0 / GLA_TAU) * _sigmoid(-z)
        dz = dz32.astype(BF)
        dga_ref[...] = _dot(dz, wd, NT).astype(dga_ref.dtype)
        dwd_ref[...] += _dot(ga.astype(BF), dz, TN)
        dbd_ref[...] += jnp.sum(dz32, axis=0, keepdims=True)

    def col(width, off):
        return pl.BlockSpec((tb, width), lambda i: (nb - 1 - i, off // width))

    def rev(width):
        return pl.BlockSpec((tb, width), lambda i: (nb - 1 - i, 0))

    return pl.pallas_call(
        body, name="gla_bwd", grid=(nb,),
        in_specs=[col(512, OFF_QK), col(512, OFF_V), col(512, OFF_GZ), col(LANES, OFF_GA), rev(512), rev(512),
                  pl.BlockSpec((nch, GLA_HEADS, LANES, LANES), lambda i: (nb - 1 - i, 0, 0, 0)),
                  _full((LANES, 256)), _full((1, 256)), _full((1, 512))],
        out_specs=[rev(512), rev(512), rev(512), rev(LANES), _full((LANES, 256)), _full((1, 256)), _full((1, 512))],
        out_shape=[jax.ShapeDtypeStruct((s, 512), BF), jax.ShapeDtypeStruct((s, 512), BF),
                   jax.ShapeDtypeStruct((s, 512), BF), jax.ShapeDtypeStruct((s, LANES), BF),
                   jax.ShapeDtypeStruct((LANES, 256), F32), jax.ShapeDtypeStruct((1, 256), F32),
                   jax.ShapeDtypeStruct((1, 512), F32)],
        scratch_shapes=[pltpu.VMEM((GLA_HEADS, LANES, LANES), F32)],
        compiler_params=_params(("arbitrary",)),
    )(proj, proj, proj, proj, dog, opre, sprev, wdecp, bdec, ggla)


_SWA_COL_HEADS = (0, 2, 1, 3, 4, 6, 5, 7)
_SWA_COLS = SWA_HEADS * SWA_BLOCK


def _swa_masks():
    lo2 = lax.broadcasted_iota(jnp.int32, (2 * SWA_BLOCK, LANES), 1) < 64
    lane1 = lax.broadcasted_iota(jnp.int32, (SWA_BLOCK, LANES), 1)
    first_half = (lane1 % 64) < 32
    key = lax.broadcasted_iota(jnp.int32, (SWA_BLOCK, _SWA_COLS), 0)
    query = lax.broadcasted_iota(jnp.int32, (SWA_BLOCK, _SWA_COLS), 1) % SWA_BLOCK
    return lo2, lane1 < 64, first_half, key > query


def _merge_band(t, prev_mask, prev_bias=None):
    prev = t[:SWA_BLOCK] if prev_bias is None else t[:SWA_BLOCK] + prev_bias
    return jnp.where(prev_mask, prev, t[SWA_BLOCK:])


def _split_band(t, prev_mask_b):
    prev = t * prev_mask_b
    return jnp.concatenate([prev, t - prev], axis=0)


def _kv_variants(t, lo2):
    tr = pltpu.roll(t, 64, 1)
    lo_v = [jnp.where(lo2, t, 0.0).astype(BF), jnp.where(lo2, tr, 0.0).astype(BF)]
    hi_v = [jnp.where(lo2, 0.0, tr).astype(BF), jnp.where(lo2, 0.0, t).astype(BF)]
    return lo_v, hi_v


def _kv_variants_t(t):
    tt = t.T
    sw = jnp.concatenate([tt[64:], tt[:64]], axis=0)
    top = lax.broadcasted_iota(jnp.int32, tt.shape, 0) < 64
    lo_v = [jnp.where(top, tt, 0.0).astype(BF), jnp.where(top, sw, 0.0).astype(BF)]
    hi_v = [jnp.where(top, 0.0, sw).astype(BF), jnp.where(top, 0.0, tt).astype(BF)]
    return lo_v, hi_v


def _swa_softmax(qg, k_lo, k_hi, prev_mask, prev_bias, sinks_ref):
    st = jnp.concatenate([_dot(k_lo[0], qg[0], NT), _dot(k_hi[0], qg[0], NT),
                          _dot(k_lo[1], qg[1], NT), _dot(k_hi[1], qg[1], NT)], axis=1)
    st = _merge_band(st, prev_mask, prev_bias)
    sink = jnp.concatenate([jnp.full((1, SWA_BLOCK), sinks_ref[0, hd], F32) for hd in _SWA_COL_HEADS], axis=1)
    m = jnp.maximum(jnp.max(st, axis=0, keepdims=True), sink)
    ex = jnp.exp(st - m)
    es = jnp.exp(sink - m)
    inv = 1.0 / (jnp.sum(ex, axis=0, keepdims=True) + es)
    return ex, es, inv


def _no_prev_bias(block_index):
    return jnp.where(block_index > 0, 0.0, -1e30).astype(F32)


def _swa_queries(sq_ref, rows, cosb, sinb, first_half):
    qs = [_rope(sq_ref[rows, p * LANES:(p + 1) * LANES], cosb, sinb, first_half) * 0.125 for p in range(4)]
    return [jnp.concatenate(qs[0:2], axis=0), jnp.concatenate(qs[2:4], axis=0)]


def _swa_fwd(proj, cos, sin, sinks):
    s = proj.shape[0]
    nq = min(SWA_QBLOCKS, s // SWA_BLOCK)
    tq = nq * SWA_BLOCK

    def body(sq_ref, sz_ref, sk_ref, sv_ref, cos_ref, sin_ref, sinks_ref, os_ref, opre_ref, kprev, vprev):
        n = pl.program_id(0)

        @pl.when(n == 0)
        def _():
            kprev[...] = jnp.zeros_like(kprev)
            vprev[...] = jnp.zeros_like(vprev)

        lo2, _, first_half, prev_mask = _swa_masks()
        prev_mask_b = jnp.where(prev_mask, 1.0, 0.0).astype(BF)
        kp, vp = kprev[...], vprev[...]
        for j in range(nq):
            rows = slice(j * SWA_BLOCK, (j + 1) * SWA_BLOCK)
            cosb, sinb = cos_ref[rows, :], sin_ref[rows, :]
            kc = _rope(sk_ref[rows, :], cosb, sinb, first_half)
            vc = sv_ref[rows, :]
            k_lo, k_hi = _kv_variants(jnp.concatenate([kp, kc], axis=0), lo2)
            vt_lo, vt_hi = _kv_variants_t(jnp.concatenate([vp, vc], axis=0))
            qg = [q.astype(BF) for q in _swa_queries(sq_ref, rows, cosb, sinb, first_half)]
            ex, _, inv = _swa_softmax(qg, k_lo, k_hi, prev_mask, _no_prev_bias(n) if j == 0 else None, sinks_ref)
            pt = _split_band(ex.astype(BF), prev_mask_b)
            for g in range(2):
                c0, c1, c2 = 512 * g, 512 * g + 256, 512 * g + 512
                ot = _dot(vt_lo[g], pt[:, c0:c1]) * inv[:, c0:c1] + _dot(vt_hi[g], pt[:, c1:c2]) * inv[:, c1:c2]
                og = ot.T
                for i in range(2):
                    ls = slice((2 * g + i) * LANES, (2 * g + i + 1) * LANES)
                    o = og[i * SWA_BLOCK:(i + 1) * SWA_BLOCK]
                    sz = sz_ref[rows, ls]
                    opre_ref[rows, ls] = o
                    os_ref[rows, ls] = (o * (sz * _sigmoid(sz))).astype(os_ref.dtype)
            kp, vp = kc, vc
        kprev[...] = kp
        vprev[...] = vp

    def col(width, off):
        return pl.BlockSpec((tq, width), lambda i: (i, off // width))

    row = pl.BlockSpec((tq, LANES), lambda i: (i, 0))
    return pl.pallas_call(
        body, name="swa_fwd", grid=(s // tq,),
        in_specs=[col(512, OFF_SQ), col(512, OFF_SZ), col(LANES, OFF_SK), col(LANES, OFF_SV), row, row,
                  pl.BlockSpec(memory_space=pltpu.SMEM)],
        out_specs=[pl.BlockSpec((tq, 512), lambda i: (i, 0))] * 2,
        out_shape=[jax.ShapeDtypeStruct((s, 512), BF), jax.ShapeDtypeStruct((s, 512), F32)],
        scratch_shapes=[pltpu.VMEM((SWA_BLOCK, LANES), F32)] * 2,
        compiler_params=_params(("arbitrary",)),
    )(proj, proj, proj, proj, cos, sin, sinks)


def _swa_bwd(proj, dos, opre, cos, sin, sinks):
    s = proj.shape[0]
    nq = min(SWA_QBLOCKS, s // SWA_BLOCK)
    tq = nq * SWA_BLOCK

    def body(sq_ref, sz_ref, sk_ref, sv_ref, dos_ref, opre_ref, cos_ref, sin_ref, sinks_ref,
             dsq_ref, dsz_ref, dsk_ref, dsv_ref, dsink_ref, kprev, vprev, cprev, sprev):
        n = pl.program_id(0)

        @pl.when(n == 0)
        def _():
            kprev[...] = jnp.zeros_like(kprev)
            vprev[...] = jnp.zeros_like(vprev)
            cprev[...] = jnp.zeros_like(cprev)
            sprev[...] = jnp.zeros_like(sprev)
            for hd in range(SWA_HEADS):
                dsink_ref[0, hd] = 0.0

        lo2, lo1, first_half, prev_mask = _swa_masks()
        prev_mask_b = jnp.where(prev_mask, 1.0, 0.0).astype(BF)
        lo1s = jnp.concatenate([lo1, lo1], axis=0)

        def home(m0, m1):
            t0 = m0 + pltpu.roll(m0, 64, 1)
            t1 = m1 + pltpu.roll(m1, 64, 1)
            return jnp.where(lo2, t0, t1)

        kp, vp, cp_, sp_ = kprev[...], vprev[...], cprev[...], sprev[...]
        for j in range(nq):
            rows = slice(j * SWA_BLOCK, (j + 1) * SWA_BLOCK)
            blk = n * nq + j
            cosb, sinb = cos_ref[rows, :], sin_ref[rows, :]
            kc = _rope(sk_ref[rows, :], cosb, sinb, first_half)
            vc = sv_ref[rows, :]
            kcat = jnp.concatenate([kp, kc], axis=0)
            k_lo, k_hi = _kv_variants(kcat, lo2)
            kt_lo, kt_hi = _kv_variants_t(kcat)
            v_lo, v_hi = _kv_variants(jnp.concatenate([vp, vc], axis=0), lo2)
            qg32 = _swa_queries(sq_ref, rows, cosb, sinb, first_half)
            qg = [q.astype(BF) for q in qg32]
            ex, es, inv = _swa_softmax(qg, k_lo, k_hi, prev_mask, _no_prev_bias(n) if j == 0 else None, sinks_ref)
            pr, ps = ex * inv, es * inv

            dog32 = []
            for g in range(2):
                parts = []
                for i in range(2):
                    ls = slice((2 * g + i) * LANES, (2 * g + i + 1) * LANES)
                    sz = sz_ref[rows, ls]
                    sg = _sigmoid(sz)
                    dos_p = dos_ref[rows, ls]
                    dsz_ref[rows, ls] = (dos_p * opre_ref[rows, ls] * (sg * (1.0 + sz * (1.0 - sg)))).astype(dsz_ref.dtype)
                    parts.append(dos_p * (sz * sg))
                dog32.append(jnp.concatenate(parts, axis=0))
            dog = [t.astype(BF) for t in dog32]
            dpr = _merge_band(jnp.concatenate([_dot(v_lo[0], dog[0], NT), _dot(v_hi[0], dog[0], NT),
                                               _dot(v_lo[1], dog[1], NT), _dot(v_hi[1], dog[1], NT)], axis=1), prev_mask)
            rd = jnp.sum(pr * dpr, axis=0, keepdims=True)
            ds = _split_band((pr * (dpr - rd)).astype(BF), prev_mask_b)
            prb = _split_band(pr.astype(BF), prev_mask_b)
            sink_term = ps * rd
            for r, hd in enumerate(_SWA_COL_HEADS):
                dsink_ref[0, hd] += -jnp.sum(sink_term[:, r * SWA_BLOCK:(r + 1) * SWA_BLOCK])

            dk_g, dv_g = [], []
            for g in range(2):
                c0, c1, c2 = 512 * g, 512 * g + 256, 512 * g + 512
                dq = (_dot(kt_lo[g], ds[:, c0:c1]) + _dot(kt_hi[g], ds[:, c1:c2])).T
                for i in range(2):
                    ls = slice((2 * g + i) * LANES, (2 * g + i + 1) * LANES)
                    dsq_ref[rows, ls] = _rope_t(dq[i * SWA_BLOCK:(i + 1) * SWA_BLOCK] * 0.125, cosb, sinb,
                                                first_half).astype(dsq_ref.dtype)
                q_split = jnp.concatenate([jnp.where(lo1s, qg32[g], 0.0), jnp.where(lo1s, 0.0, qg32[g])], axis=0).astype(BF)
                do_split = jnp.concatenate([jnp.where(lo1s, dog32[g], 0.0), jnp.where(lo1s, 0.0, dog32[g])], axis=0).astype(BF)
                dk_g.append(_dot(ds[:, c0:c2], q_split))
                dv_g.append(_dot(prb[:, c0:c2], do_split))
            dk = home(dk_g[0], dk_g[1])
            dv = home(dv_g[0], dv_g[1])
            cur = pl.ds(pl.multiple_of(blk * SWA_BLOCK, SWA_BLOCK), SWA_BLOCK)
            dsk_ref[cur, :] = _rope_t(dk[SWA_BLOCK:], cosb, sinb, first_half)
            dsv_ref[cur, :] = dv[SWA_BLOCK:]
            dk_prev = _rope_t(dk[:SWA_BLOCK], cp_, sp_, first_half)
            dv_prev = dv[:SWA_BLOCK]
            if j == 0:
                @pl.when(n > 0)
                def _():
                    prv = pl.ds(pl.multiple_of((blk - 1) * SWA_BLOCK, SWA_BLOCK), SWA_BLOCK)
                    dsk_ref[prv, :] += dk_prev
                    dsv_ref[prv, :] += dv_prev
            else:
                prv = pl.ds(pl.multiple_of((blk - 1) * SWA_BLOCK, SWA_BLOCK), SWA_BLOCK)
                dsk_ref[prv, :] += dk_prev
                dsv_ref[prv, :] += dv_prev
            kp, vp, cp_, sp_ = kc, vc, cosb, sinb
        kprev[...] = kp
        vprev[...] = vp
        cprev[...] = cp_
        sprev[...] = sp_

    def col(width, off):
        return pl.BlockSpec((tq, width), lambda i: (i, off // width))

    row = pl.BlockSpec((tq, LANES), lambda i: (i, 0))
    wide = pl.BlockSpec((tq, 512), lambda i: (i, 0))
    return pl.pallas_call(
        body, name="swa_bwd", grid=(s // tq,),
        in_specs=[col(512, OFF_SQ), col(512, OFF_SZ), col(LANES, OFF_SK), col(LANES, OFF_SV), wide, wide, row, row,
                  pl.BlockSpec(memory_space=pltpu.SMEM)],
        out_specs=[wide, wide, _full((s, LANES)), _full((s, LANES)), pl.BlockSpec(memory_space=pltpu.SMEM)],
        out_shape=[jax.ShapeDtypeStruct((s, 512), BF), jax.ShapeDtypeStruct((s, 512), BF),
                   jax.ShapeDtypeStruct((s, LANES), F32), jax.ShapeDtypeStruct((s, LANES), F32),
                   jax.ShapeDtypeStruct((1, SWA_HEADS), F32)],
        scratch_shapes=[pltpu.VMEM((SWA_BLOCK, LANES), F32)] * 4,
        compiler_params=_params(("arbitrary",)),
    )(proj, proj, proj, proj, dos, opre, cos, sin, sinks)


def _outproj(og, osw, w_out, x2d, target, gate, g_final):
    s = x2d.shape[0]
    tm = min(512, s)

    def body(og_ref, os_ref, w_ref, x_ref, t_ref, gate_ref, gf_ref,
             dx2_ref, dog_ref, dos_ref, dw_ref, loss_ref, dgf_ref, dgate_ref):
        @pl.when(pl.program_id(0) == 0)
        def _():
            dw_ref[...] = jnp.zeros_like(dw_ref)
            loss_ref[...] = jnp.zeros_like(loss_ref)
            dgf_ref[...] = jnp.zeros_like(dgf_ref)
            dgate_ref[...] = jnp.zeros_like(dgate_ref)

        ogv, osv, w = og_ref[...], os_ref[...], w_ref[...]
        gate, gf = gate_ref[...], gf_ref[...]
        y = _dot(ogv, w[:512]) + _dot(osv, w[512:])
        x2 = x_ref[...] + gate * y
        r = lax.rsqrt(jnp.mean(x2 * x2, axis=-1, keepdims=True) + RMS_EPS)
        xn = x2 * r
        err = xn * gf - t_ref[...]
        loss_ref[...] += 0.5 * jnp.sum(jnp.mean(err * err, axis=-1, keepdims=True), axis=0, keepdims=True)
        dyf = err * (1.0 / D_MODEL)
        dgf_ref[...] += jnp.sum(dyf * xn, axis=0, keepdims=True)
        t = dyf * gf
        dx2 = r * (t - xn * jnp.mean(t * xn, axis=-1, keepdims=True))
        dx2_ref[...] = dx2
        dgate_ref[...] += jnp.sum(dx2 * y, axis=0, keepdims=True)
        dy = (dx2 * gate).astype(BF)
        dmix = _dot(dy, w, NT)
        dog_ref[...] = dmix[:, :512]
        dos_ref[...] = dmix[:, 512:]
        dw_ref[:512, :] += _dot(ogv, dy, TN)
        dw_ref[512:, :] += _dot(osv, dy, TN)

    half = pl.BlockSpec((tm, 512), lambda i: (i, 0))
    rowb = pl.BlockSpec((tm, D_MODEL), lambda i: (i, 0))
    vec = _full((1, D_MODEL))
    return pl.pallas_call(
        body, name="outproj", grid=(s // tm,),
        in_specs=[half, half, _full((D_MODEL, D_MODEL)), rowb, rowb, vec, vec],
        out_specs=[rowb, half, half, _full((D_MODEL, D_MODEL)), _full((1, 1)), vec, vec],
        out_shape=[jax.ShapeDtypeStruct((s, D_MODEL), F32), jax.ShapeDtypeStruct((s, 512), F32),
                   jax.ShapeDtypeStruct((s, 512), F32), jax.ShapeDtypeStruct((D_MODEL, D_MODEL), F32),
                   jax.ShapeDtypeStruct((1, 1), F32), jax.ShapeDtypeStruct((1, D_MODEL), F32),
                   jax.ShapeDtypeStruct((1, D_MODEL), F32)],
        compiler_params=_params(("arbitrary",)),
    )(og, osw, w_out, x2d, target, gate, g_final)


_PIECES = ((OFF_QK, 512), (OFF_V, 512), (OFF_GZ, 512), (OFF_SQ, 512), (OFF_SZ, 512),
           (OFF_SK, LANES), (OFF_SV, LANES), (OFF_GA, LANES))

_UNPAD_ROWS = ((OFF_QK, 0, 1024),
               (OFF_GA, 1024, GLA_RANK),
               (OFF_GZ, 1040, 1024),
               (OFF_SK, 2064, 256),
               (OFF_SZ, 2320, 512))


def _inproj_bwd(x2d, shift, sc1p, g_norm, wpad_t, dx2, pieces):
    s = x2d.shape[0]
    tm = min(512, s)
    nsteps = s // tm

    def body(x_ref, sh_ref, sc_ref, g_ref, w_hbm, dx2_ref, *rest):
        piece_refs = rest[:len(_PIECES)]
        gx_ref, dw_hbm, dsh_ref, dsc_ref, dg_ref, w_vm, dw_vm, sem, out_sems = rest[len(_PIECES):]
        i = pl.program_id(0)

        @pl.when(i == 0)
        def _():
            cp = pltpu.make_async_copy(w_hbm, w_vm, sem)
            cp.start()
            dw_vm[...] = jnp.zeros_like(dw_vm)
            dsh_ref[...] = jnp.zeros_like(dsh_ref)
            dsc_ref[...] = jnp.zeros_like(dsc_ref)
            dg_ref[...] = jnp.zeros_like(dg_ref)
            cp.wait()

        g, sc1p_v = g_ref[...], sc_ref[...]
        xn, r, h = _modnorm(x_ref[...], g, sc1p_v, sh_ref[...])
        hb = h.astype(BF)
        dh = None
        for (off, width), pr in zip(_PIECES, piece_refs):
            dp = pr[...].astype(BF)
            part = _dot(dp, w_vm[off:off + width, :])
            dh = part if dh is None else dh + part
            dw_vm[off:off + width, :] += _dot(dp, hb, TN)
        dsh_ref[...] += jnp.sum(dh, axis=0, keepdims=True)
        dsc_ref[...] += jnp.sum(dh * (xn * g), axis=0, keepdims=True)
        dg_ref[...] += jnp.sum(dh * xn * sc1p_v, axis=0, keepdims=True)
        dxn = dh * g * sc1p_v
        gx_ref[...] = dx2_ref[...] + r * (dxn - xn * jnp.mean(dxn * xn, axis=-1, keepdims=True))

        @pl.when(i == nsteps - 1)
        def _():
            copies = [pltpu.make_async_copy(dw_vm.at[src:src + n], dw_hbm.at[dst:dst + n], out_sems.at[k])
                      for k, (src, dst, n) in enumerate(_UNPAD_ROWS)]
            for cp in copies:
                cp.start()
            for cp in copies:
                cp.wait()

    rowb = pl.BlockSpec((tm, D_MODEL), lambda i: (i, 0))
    vec = _full((1, D_MODEL))
    anyspec = pl.BlockSpec(memory_space=pl.ANY)
    piece_specs = [pl.BlockSpec((tm, width), lambda i: (i, 0)) for _, width in _PIECES]
    return pl.pallas_call(
        body, name="inproj_bwd", grid=(nsteps,),
        in_specs=[rowb, vec, vec, vec, anyspec, rowb] + piece_specs,
        out_specs=[rowb, anyspec, vec, vec, vec],
        out_shape=[jax.ShapeDtypeStruct((s, D_MODEL), F32), jax.ShapeDtypeStruct((D_IN, D_MODEL), F32),
                   jax.ShapeDtypeStruct((1, D_MODEL), F32), jax.ShapeDtypeStruct((1, D_MODEL), F32),
                   jax.ShapeDtypeStruct((1, D_MODEL), F32)],
        scratch_shapes=[pltpu.VMEM((D_PAD, D_MODEL), BF), pltpu.VMEM((D_PAD, D_MODEL), F32), pltpu.SemaphoreType.DMA,
                        pltpu.SemaphoreType.DMA((len(_UNPAD_ROWS),))],
        compiler_params=_params(("arbitrary",)),
    )(x2d, shift, sc1p, g_norm, wpad_t, dx2, *pieces)


def _adam(w, g, m, v):
    m2 = ADAM_B1 * m + (1.0 - ADAM_B1) * g
    v2 = ADAM_B2 * v + (1.0 - ADAM_B2) * (g * g)
    m_hat = m2 / (1.0 - ADAM_B1 ** ADAM_STEP)
    v_hat = v2 / (1.0 - ADAM_B2 ** ADAM_STEP)
    delta = -ADAM_LR * (m_hat / (jnp.sqrt(v_hat) + ADAM_EPS) + ADAM_WD * w)
    return delta, m2, v2


def _adamw(w, g, m, v, name):
    rr, cc = w.shape
    tc = min(256, cc)

    def body(w_ref, g_ref, m_ref, v_ref, d_ref, m2_ref, v2_ref):
        d_ref[...], m2_ref[...], v2_ref[...] = _adam(w_ref[...], g_ref[...], m_ref[...], v_ref[...])

    blk = pl.BlockSpec((rr, tc), lambda i: (0, i))
    return pl.pallas_call(
        body, name=name, grid=(cc // tc,), in_specs=[blk] * 4, out_specs=[blk] * 3,
        out_shape=[jax.ShapeDtypeStruct((rr, cc), F32)] * 3,
        compiler_params=_params(("arbitrary",)),
    )(w, g, m, v)


def _ada_update(c_all, dmod_cols, w, m, v):
    rr, cc = w.shape
    tr = min(256, rr)
    c_all = jnp.pad(c_all, ((0, 8), (0, 0)))
    dmod_cols = jnp.pad(dmod_cols, ((0, 8), (0, 0)))

    def body(c_ref, dm_ref, w_ref, m_ref, v_ref, g_ref, d_ref, m2_ref, v2_ref):
        cv = c_ref[...]
        sc = (cv * _sigmoid(cv)).astype(BF)
        g = _dot(sc, dm_ref[...].astype(BF), TN)
        g_ref[...] = g
        d_ref[...], m2_ref[...], v2_ref[...] = _adam(w_ref[...], g, m_ref[...], v_ref[...])

    blk = pl.BlockSpec((tr, cc), lambda i: (i, 0))
    return pl.pallas_call(
        body, name="ada_update", grid=(rr // tr,),
        in_specs=[pl.BlockSpec((16, tr), lambda i: (0, i)), _full((16, cc)), blk, blk, blk],
        out_specs=[blk] * 4, out_shape=[jax.ShapeDtypeStruct((rr, cc), F32)] * 4,
        compiler_params=_params(("arbitrary",)),
    )(c_all, dmod_cols, w, m, v)


def _small_update(parts, weights, moms, vels):
    n = len(weights)

    def body(*refs):
        p_refs, w_refs, m_refs, v_refs = refs[:n + 1], refs[n + 1:2 * n + 1], refs[2 * n + 1:3 * n + 1], refs[3 * n + 1:4 * n + 1]
        outs = refs[4 * n + 1:]
        for i in range(n):
            g = p_refs[i][0]
            for d in range(1, 8):
                g = g + p_refs[i][d]
            delta, m2, v2 = _adam(w_refs[i][...], g, m_refs[i][...], v_refs[i][...])
            outs[4 * i][...] = g
            outs[4 * i + 1][...] = delta
            outs[4 * i + 2][...] = m2
            outs[4 * i + 3][...] = v2
        tot = p_refs[n][0]
        for d in range(1, 8):
            tot = tot + p_refs[n][d]
        outs[4 * n][...] = tot

    out_shape = []
    for w in weights:
        out_shape += [jax.ShapeDtypeStruct(w.shape, F32)] * 4
    out_shape.append(jax.ShapeDtypeStruct(parts[n].shape[1:], F32))
    return pl.pallas_call(body, name="small_update", out_shape=out_shape, compiler_params=_params())(
        *parts, *weights, *moms, *vels)


def _pad_w_in_t(w):
    pad = jnp.zeros((LANES - GLA_RANK, w.shape[1]), w.dtype)
    return jnp.concatenate([w[dst:dst + n] for _, dst, n in sorted(_UNPAD_ROWS)] + [pad], axis=0)


def _rows8(a):
    flat = a.reshape(-1)
    rows = -(-flat.shape[0] // LANES)
    rows8 = -(-rows // 8) * 8
    flat = jnp.pad(flat, (0, rows8 * LANES - flat.shape[0]))
    return flat.reshape(rows8, LANES)


def kernel(x, c, positions, w_ada, b_ada, g_norm, w_in, w_decay, b_decay, g_gla_head, sinks, w_out, g_final, loss_target, m_w_ada, m_b_ada, m_g_norm, m_w_in, m_w_decay, m_b_decay, m_g_gla_head, m_sinks, m_w_out, m_g_final, v_w_ada, v_b_ada, v_g_norm, v_w_in, v_w_decay, v_b_decay, v_g_gla_head, v_sinks, v_w_out, v_g_final):
    ax, ay, ac = lax.axis_index("x"), lax.axis_index("y"), lax.axis_index("c")
    chip = 2 * ax + ay
    dev = 2 * chip + ac
    s = x.shape[1]
    x2d = x[0]
    target = loss_target[0]
    w_ada2, w_out2, w_dec2 = w_ada[0], w_out[0], w_decay[0]
    w_in_t, m_w_in_t, v_w_in_t = w_in[0].T, m_w_in[0].T, v_w_in[0].T
    ada_cols = w_ada2.shape[1]
    in_cols = w_in_t.shape[0]
    out_rows = w_out2.shape[0]
    half = D_MODEL // 2

    cw = jnp.concatenate([c.reshape(8, LANES), w_dec2.reshape(8, LANES)], axis=0)
    b_shard = lax.dynamic_slice(b_ada, (0, chip * ada_cols), (1, ada_cols))
    half_in = lax.dynamic_slice(w_in_t, (0, ac * half), (in_cols, half)).astype(BF)
    half_out = lax.dynamic_slice(w_out2, (ac * (out_rows // 2), 0), (out_rows // 2, D_MODEL)).astype(BF)
    inv_freq = 1.0 / (ROPE_THETA ** (jnp.arange(0, 64, 2, dtype=F32) / 64))
    first, mod_all, w_in_all, w_out_all, cos, sin = _prologue(
        cw, w_ada2, b_shard, half_in, half_out, positions.reshape(s, 1), jnp.tile(inv_freq, 4).reshape(1, LANES))

    first = first.reshape(8, 2, 8, LANES)
    c_all = first[:, 0].reshape(8, D_MODEL)
    w_dec_full = first[0::2, 1].reshape(4, GLA_RANK, 64).transpose(1, 0, 2).reshape(GLA_RANK, 256)
    mod = mod_all.reshape(4, 2, 8, ada_cols)[:, 0]
    mod = lax.dynamic_slice(mod, (0, dev, 0), (4, 1, ada_cols)).reshape(1, 4 * ada_cols)
    shift, sc1p, gate = mod[:, :D_MODEL], 1.0 + mod[:, D_MODEL:2 * D_MODEL], mod[:, 2 * D_MODEL:]
    w_in_all = w_in_all.reshape(4, 2, in_cols, half)
    wpad_t = _pad_w_in_t(w_in_all.transpose(0, 2, 1, 3).reshape(4 * in_cols, D_MODEL))
    w_out_all = w_out_all.reshape(D_MODEL, D_MODEL)

    wdecp = jnp.pad(w_dec_full, ((0, LANES - GLA_RANK), (0, 0))).astype(BF)

    proj = _inproj_fwd(x2d, shift, sc1p, g_norm, wpad_t)
    og, o_gla, sprev = _gla_fwd(proj, wdecp, b_decay, g_gla_head)
    osw, o_swa = _swa_fwd(proj, cos, sin, sinks)
    dx2, dog, dos, dw_out, loss_p, dgf, dgate = _outproj(og, osw, w_out_all, x2d, target, gate, g_final.reshape(1, D_MODEL))
    dsq, dsz, dsk, dsv, dsinks = _swa_bwd(proj, dos, o_swa, cos, sin, sinks)
    dqk, dv, dgz, dga, dwdp, dbd, dgg = _gla_bwd(proj, dog, o_gla, sprev, wdecp, b_decay, g_gla_head)
    pieces = (dqk, dv, dgz, dsq, dsz, dsk, dsv, dga)
    gx, dw_in_t, dshift, dscale, dgn = _inproj_bwd(x2d, shift, sc1p, g_norm, wpad_t, dx2, pieces)

    segs = [jnp.concatenate([dshift, dscale, dgate], axis=1), dgn, dgf, dwdp[:GLA_RANK], dbd, dgg, dsinks, loss_p]
    packed = [_rows8(a) for a in segs]
    offs = [0]
    for a in packed:
        offs.append(offs[-1] + a.shape[0])
    g_w_in_t, g_w_out, small = _epilogue(dw_in_t.reshape(4, in_cols, D_MODEL), dw_out.reshape(4, out_rows, D_MODEL),
                                         jnp.concatenate(packed, axis=0))

    def seg(i, size):
        return small[:, offs[i]:offs[i + 1]].reshape(8, -1)[:, :size]

    dmod_all = seg(0, 3 * D_MODEL)
    dwd_all = lax.dynamic_slice(seg(3, GLA_RANK * 256).reshape(8, GLA_RANK, 256), (0, 0, chip * 64), (8, GLA_RANK, 64))
    parts = [dmod_all.reshape(8, 1, 3 * D_MODEL), seg(1, D_MODEL).reshape(8, 1, D_MODEL), dwd_all,
             seg(4, 256).reshape(8, 1, 256), seg(5, 512).reshape(8, 1, 512), seg(6, SWA_HEADS).reshape(8, 1, SWA_HEADS),
             seg(2, D_MODEL).reshape(8, 1, D_MODEL), seg(7, LANES).reshape(8, 1, LANES)]
    smalls = _small_update(
        parts,
        [b_ada, g_norm, w_dec2, b_decay, g_gla_head, sinks, g_final.reshape(1, D_MODEL)],
        [m_b_ada, m_g_norm, m_w_decay[0], m_b_decay, m_g_gla_head, m_sinks, m_g_final.reshape(1, D_MODEL)],
        [v_b_ada, v_g_norm, v_w_decay[0], v_b_decay, v_g_gla_head, v_sinks, v_g_final.reshape(1, D_MODEL)])
    (g_b_ada, d_b_ada, nm_b_ada, nv_b_ada, g_gn, d_gn, nm_gn, nv_gn, g_wd, d_wd, nm_wd, nv_wd,
     g_bd, d_bd, nm_bd, nv_bd, g_gg, d_gg, nm_gg, nv_gg, g_sk, d_sk, nm_sk, nv_sk,
     g_gf, d_gf, nm_gf, nv_gf, loss_row) = smalls
    loss = loss_row[0, 0]

    dmod_cols = lax.dynamic_slice(dmod_all, (0, chip * ada_cols), (8, ada_cols))
    g_w_ada, d_w_ada, nm_w_ada, nv_w_ada = _ada_update(c_all, dmod_cols, w_ada2, m_w_ada[0], v_w_ada[0])
    d_w_in_t, nm_w_in_t, nv_w_in_t = _adamw(w_in_t, g_w_in_t, m_w_in_t, v_w_in_t, "adamw_w_in")
    g_w_in, d_w_in, nm_w_in, nv_w_in = g_w_in_t.T, d_w_in_t.T, nm_w_in_t.T, nv_w_in_t.T
    d_w_out, nm_w_out, nv_w_out = _adamw(w_out2, g_w_out, m_w_out[0], v_w_out[0], "adamw_w_out")

    flat = lambda a: a.reshape(D_MODEL)
    grads = [g_w_ada[None], g_b_ada, g_gn, g_w_in[None], g_wd[None], g_bd, g_gg, g_sk, g_w_out[None], flat(g_gf)]
    deltas = [d_w_ada[None], d_b_ada, d_gn, d_w_in[None], d_wd[None], d_bd, d_gg, d_sk, d_w_out[None], flat(d_gf)]
    new_m = [nm_w_ada[None], nm_b_ada, nm_gn, nm_w_in[None], nm_wd[None], nm_bd, nm_gg, nm_sk, nm_w_out[None], flat(nm_gf)]
    new_v = [nv_w_ada[None], nv_b_ada, nv_gn, nv_w_in[None], nv_wd[None], nv_bd, nv_gg, nv_sk, nv_w_out[None], flat(nv_gf)]
    return (loss, gx[None], *grads, *deltas, *new_m, *new_v)
```

```python
import jax
import jax.numpy as jnp
from jax import lax
from jax.experimental import pallas as pl
from jax.experimental.pallas import tpu as pltpu

F32 = jnp.float32
BF = jnp.bfloat16

D_MODEL = 1024
GLA_HEADS = 4
GLA_DK = 64
GLA_CHUNK = 64
GLA_RANK = 16
GLA_TAU = 16.0
GLA_ROWS = 256
SWA_HEADS = 8
SWA_BLOCK = 128
SWA_QBLOCKS = 8
RMS_EPS = 1e-6
ROPE_THETA = 10000.0

OFF_QK, OFF_V, OFF_GZ, OFF_SQ, OFF_SZ, OFF_SK, OFF_SV, OFF_GA = 0, 512, 1024, 1536, 2048, 2560, 2688, 2816
D_PAD = 2944
D_IN = 2832
LANES = 128
VMEM_LIMIT = 56 * 1024 * 1024

ADAM_LR, ADAM_B1, ADAM_B2, ADAM_EPS, ADAM_WD, ADAM_STEP = 0.001, 0.9, 0.999, 1e-08, 0.01, 10

NT = (((1,), (1,)), ((), ()))
TN = (((0,), (0,)), ((), ()))
MESH = pl.DeviceIdType.MESH


def _dot(a, b, dims=None):
    if dims is None:
        return jnp.dot(a, b, preferred_element_type=F32)
    return lax.dot_general(a, b, dims, preferred_element_type=F32)


def _sigmoid(x):
    return 1.0 / (1.0 + jnp.exp(-x))


def _params(sem=None):
    return pltpu.CompilerParams(dimension_semantics=sem, vmem_limit_bytes=VMEM_LIMIT)


def _full(shape):
    return pl.BlockSpec(shape, lambda i: (0,) * len(shape))


_GATHER_SEMS = [pltpu.SemaphoreType.DMA((7,)), pltpu.SemaphoreType.DMA((7,)), pltpu.SemaphoreType.DMA]


class _Gather:
    def __init__(self, x_ref, out_ref, send_sems, recv_sems, local_sem):
        x, y, c = lax.axis_index("x"), lax.axis_index("y"), lax.axis_index("c")
        self.me, self.sibling, self.c = (x, y, c), (x, y, 1 - c), c
        self.chips = [(1 - x, y), (x, 1 - y), (1 - x, 1 - y)]
        self.x_ref, self.out_ref, self.send_sems, self.recv_sems = x_ref, out_ref, send_sems, recv_sems
        self.mine = pltpu.make_async_copy(x_ref, self._slab(*self.me), local_sem)

    def _slab(self, px, py, pc):
        return self.out_ref.at[4 * px + 2 * py + pc]

    def _copy(self, k, blk, to, src=None):
        return pltpu.make_async_remote_copy(
            src_ref=self._slab(*blk) if src is None else src, dst_ref=self._slab(*blk),
            send_sem=self.send_sems.at[k], recv_sem=self.recv_sems.at[k], device_id=to, device_id_type=MESH)

    def start(self):
        self.mine.start()
        self.sent = [self._copy(0, self.me, self.sibling, src=self.x_ref)]
        self.sent += [self._copy(1 + j, self.me, (*chip, self.c), src=self.x_ref) for j, chip in enumerate(self.chips)]
        for cp in self.sent:
            cp.start()

    def relay(self):
        for j, chip in enumerate(self.chips):
            self._copy(1 + j, (*chip, self.c), self.me).wait_recv()
            cp = self._copy(4 + j, (*chip, self.c), self.sibling)
            cp.start()
            self.sent.append(cp)

    def finish(self):
        self._copy(0, self.sibling, self.me).wait_recv()
        for j, chip in enumerate(self.chips):
            self._copy(4 + j, (*chip, 1 - self.c), self.me).wait_recv()
        for cp in self.sent:
            cp.wait_send()
        self.mine.wait()


def _prologue(cw, w_ada, b_shard, half_in, half_out, pos_col, inv_freq):
    s = pos_col.shape[0]
    rt = min(512, s)

    def body(cw_ref, wada_ref, b_ref, hin_ref, hout_ref, pos_ref, f_ref,
             first_ref, mod_ref, win_ref, wout_ref, cos_ref, sin_ref, mod_blk, *sems):
        g_c = _Gather(cw_ref, first_ref, *sems[0:3])
        g_in = _Gather(hin_ref, win_ref, *sems[3:6])
        g_out = _Gather(hout_ref, wout_ref, *sems[6:9])
        g_mod = _Gather(mod_blk, mod_ref, *sems[9:12])
        g_c.start()
        g_in.start()
        g_out.start()
        g_c.relay()
        g_c.finish()
        c_rows = [jnp.concatenate([first_ref[d, r:r + 1, :] for r in range(8)], axis=1) for d in range(8)]
        c_all = jnp.concatenate(c_rows, axis=0)
        sc = (c_all * _sigmoid(c_all)).astype(BF)
        mod_blk[...] = _dot(sc, wada_ref[...].astype(BF)) + b_ref[...]
        g_mod.start()

        def rope_rows(i, carry):
            rows = pl.ds(pl.multiple_of(i * rt, rt), rt)
            ang = pos_ref[rows, :].astype(F32) * f_ref[...]
            lane = lax.broadcasted_iota(jnp.int32, ang.shape, 1)
            cos_ref[rows, :] = jnp.cos(ang)
            sn = jnp.sin(ang)
            sin_ref[rows, :] = jnp.where((lane % 64) < 32, -sn, sn)
            return carry

        lax.fori_loop(0, s // rt, rope_rows, 0)
        g_mod.relay()
        g_out.relay()
        g_in.relay()
        g_mod.finish()
        g_out.finish()
        g_in.finish()

    vm = pl.BlockSpec(memory_space=pltpu.VMEM)
    return pl.pallas_call(
        body, name="prologue",
        out_shape=[jax.ShapeDtypeStruct((8,) + cw.shape, F32), jax.ShapeDtypeStruct((8, 8, w_ada.shape[1]), F32),
                   jax.ShapeDtypeStruct((8,) + half_in.shape, half_in.dtype),
                   jax.ShapeDtypeStruct((8,) + half_out.shape, half_out.dtype),
                   jax.ShapeDtypeStruct((s, LANES), F32), jax.ShapeDtypeStruct((s, LANES), F32)],
        in_specs=[vm] * 7, out_specs=[vm] * 6,
        scratch_shapes=[pltpu.VMEM((8, w_ada.shape[1]), F32)] + _GATHER_SEMS * 4,
        compiler_params=pltpu.CompilerParams(vmem_limit_bytes=VMEM_LIMIT),
    )(cw, w_ada, b_shard, half_in, half_out, pos_col, inv_freq)


def _reduce_scratch(rr, cc):
    c2 = cc // 2
    return [pltpu.VMEM((4, rr, c2), F32), pltpu.VMEM((4, rr, c2), F32), pltpu.VMEM((3, rr, c2), BF),
            pltpu.VMEM((3, rr, c2), BF), pltpu.VMEM((rr, c2), F32),
            pltpu.SemaphoreType.DMA((5,)), pltpu.SemaphoreType.DMA((5,)), pltpu.SemaphoreType.DMA((2,))]


class _Reduce:
    def __init__(self, p_hbm, out_ref, acc_ref, own_ref, send_ref, land_ref, res_ref, send_sems, recv_sems, local_sems):
        x, y, c = lax.axis_index("x"), lax.axis_index("y"), lax.axis_index("c")
        c2 = out_ref.shape[1] // 2
        self.c, self.my_chip, sibling = c, 2 * x + y, (x, y, 1 - c)
        self.chips = [(1 - x, y), (x, 1 - y), (1 - x, 1 - y)]
        mine = pl.ds(pl.multiple_of(c * c2, c2), c2)
        other = pl.ds(pl.multiple_of((1 - c) * c2, c2), c2)
        self.acc_ref, self.own_ref, self.send_ref, self.land_ref, self.res_ref = acc_ref, own_ref, send_ref, land_ref, res_ref
        self.send_sems, self.recv_sems = send_sems, recv_sems
        self.own = pltpu.make_async_copy(p_hbm.at[:, :, mine], own_ref, local_sems.at[0])
        self.swap = pltpu.make_async_remote_copy(
            src_ref=p_hbm.at[:, :, other], dst_ref=acc_ref, send_sem=send_sems.at[0], recv_sem=recv_sems.at[0],
            device_id=sibling, device_id_type=MESH)
        self.put = pltpu.make_async_copy(res_ref, out_ref.at[:, mine], local_sems.at[1])
        self.share = pltpu.make_async_remote_copy(
            src_ref=res_ref, dst_ref=out_ref.at[:, mine], send_sem=send_sems.at[4],
            recv_sem=recv_sems.at[4], device_id=sibling, device_id_type=MESH)

    def start(self):
        self.own.start()
        self.swap.start()

    def combine_and_send(self):
        self.own.wait()
        self.swap.wait()
        for j in range(4):
            self.acc_ref[j] = self.acc_ref[j] + self.own_ref[j]
        self.sends = []
        for k, (tx, ty) in enumerate(self.chips):
            self.send_ref[k] = self.acc_ref[2 * tx + ty].astype(self.send_ref.dtype)
            cp = pltpu.make_async_remote_copy(
                src_ref=self.send_ref.at[k], dst_ref=self.land_ref.at[k], send_sem=self.send_sems.at[1 + k],
                recv_sem=self.recv_sems.at[1 + k], device_id=(tx, ty, self.c), device_id_type=MESH)
            cp.start()
            self.sends.append(cp)

    def total_and_share(self):
        for cp in self.sends:
            cp.wait_recv()
        total = self.acc_ref[self.my_chip]
        for k in range(3):
            total = total + self.land_ref[k].astype(F32)
        self.res_ref[...] = total
        for cp in self.sends:
            cp.wait_send()
        self.put.start()
        self.share.start()

    def finish(self):
        self.put.wait()
        self.share.wait()


def _epilogue(dw_in_parts, dw_out_parts, small):
    _, r_in, cc = dw_in_parts.shape
    _, r_out, _ = dw_out_parts.shape
    n_red = len(_reduce_scratch(r_in, cc))

    def body(pin_hbm, pout_hbm, small_ref, gin_ref, gout_ref, small_all_ref, *scratch):
        red_in = _Reduce(pin_hbm, gin_ref, *scratch[0:n_red])
        red_out = _Reduce(pout_hbm, gout_ref, *scratch[n_red:2 * n_red])
        gat = _Gather(small_ref, small_all_ref, *scratch[2 * n_red:])
        red_out.start()
        red_in.start()
        gat.start()
        red_out.combine_and_send()
        red_in.combine_and_send()
        gat.relay()
        red_out.total_and_share()
        red_in.total_and_share()
        gat.finish()
        red_out.finish()
        red_in.finish()

    vm = pl.BlockSpec(memory_space=pltpu.VMEM)
    anyspec = pl.BlockSpec(memory_space=pl.ANY)
    return pl.pallas_call(
        body, name="epilogue",
        out_shape=[jax.ShapeDtypeStruct((r_in, cc), F32), jax.ShapeDtypeStruct((r_out, cc), F32),
                   jax.ShapeDtypeStruct((8,) + small.shape, F32)],
        in_specs=[anyspec, anyspec, vm], out_specs=[vm, vm, vm],
        scratch_shapes=_reduce_scratch(r_in, cc) + _reduce_scratch(r_out, cc) + _GATHER_SEMS,
        compiler_params=pltpu.CompilerParams(vmem_limit_bytes=VMEM_LIMIT),
    )(dw_in_parts, dw_out_parts, small)


def _rope(t, cosb, sinb, first_half):
    partner = jnp.where(first_half, pltpu.roll(t, 96, 1), pltpu.roll(t, 32, 1))
    return t * cosb + partner * sinb


def _rope_t(g, cosb, sinb, first_half):
    gs = g * sinb
    partner = jnp.where(first_half, pltpu.roll(gs, 96, 1), pltpu.roll(gs, 32, 1))
    return g * cosb + partner


def _modnorm(x, g, sc1p, shift):
    r = lax.rsqrt(jnp.mean(x * x, axis=-1, keepdims=True) + RMS_EPS)
    xn = x * r
    return xn, r, (xn * g) * sc1p + shift


def _inproj_fwd(x2d, shift, sc1p, g_norm, wpad_t):
    s = x2d.shape[0]
    tm = min(512, s)

    def body(x_ref, sh_ref, sc_ref, g_ref, w_ref, o_ref):
        _, _, h = _modnorm(x_ref[...], g_ref[...], sc_ref[...], sh_ref[...])
        o_ref[...] = _dot(h.astype(BF), w_ref[...], NT)

    vec = _full((1, D_MODEL))
    return pl.pallas_call(
        body, name="inproj_fwd", grid=(s // tm,),
        in_specs=[pl.BlockSpec((tm, D_MODEL), lambda i: (i, 0)), vec, vec, vec, _full((D_PAD, D_MODEL))],
        out_specs=pl.BlockSpec((tm, D_PAD), lambda i: (i, 0)),
        out_shape=jax.ShapeDtypeStruct((s, D_PAD), F32),
        compiler_params=_params(("arbitrary",)),
    )(x2d, shift, sc1p, g_norm, wpad_t)


def _split3(a):
    hi = a.astype(BF)
    r1 = a - hi.astype(F32)
    mid = r1.astype(BF)
    lo = (r1 - mid.astype(F32)).astype(BF)
    return hi, mid, lo


def _tri_matmul(tri, a):
    hi, mid, lo = _split3(a)
    return _dot(tri, hi) + _dot(tri, mid) + _dot(tri, lo)


def _chunks(tb):
    return [slice(c * GLA_CHUNK, (c + 1) * GLA_CHUNK) for c in range(tb // GLA_CHUNK)]


def _per_chunk_rows(rows, width):
    return jnp.concatenate([jnp.broadcast_to(r, (GLA_CHUNK, width)) for r in rows], axis=0)


def _gla_triangle(tb):
    row = lax.broadcasted_iota(jnp.int32, (tb, tb), 0)
    col = lax.broadcasted_iota(jnp.int32, (tb, tb), 1)
    return (((row // GLA_CHUNK) == (col // GLA_CHUNK)) & (col <= row)).astype(F32)


def _lane_mean(x, ones_b):
    hi = x.astype(BF)
    lo = (x - hi.astype(F32)).astype(BF)
    return (_dot(hi, ones_b) + _dot(lo, ones_b)) * (1.0 / LANES)


def _head(t, h, lo_h):
    blk = t[:, LANES * (h // 2):LANES * (h // 2 + 1)]
    return jnp.where(lo_h, blk, 0.0) if h % 2 == 0 else jnp.where(lo_h, 0.0, blk)


def _gla_block_common(qk, ga, wd, bd, tril_b):
    tb = qk.shape[0]
    q, k = qk[:, :256], qk[:, 256:]
    z = _dot(ga.astype(BF), wd) + bd
    la = (jnp.minimum(z, 0.0) - jnp.log(1.0 + jnp.exp(-jnp.abs(z)))) * (1.0 / GLA_TAU)
    b = _tri_matmul(tril_b, la)
    bls = [b[rs.stop - 1:rs.stop, :] for rs in _chunks(tb)]
    eq = jnp.exp(b)
    ek = jnp.exp(-b)
    f = jnp.exp(_per_chunk_rows(bls, 256) - b)
    return z, eq, ek, f, q * (eq * GLA_DK ** -0.5), k * ek, k * f, bls


def _gla_fwd(proj, wdecp, bdec, ggla):
    s = proj.shape[0]
    tb = min(GLA_ROWS, s)
    nch = tb // GLA_CHUNK

    def body(qk_ref, v_ref, gz_ref, ga_ref, wd_ref, bd_ref, gg_ref, tri_ref, og_ref, opre_ref, sprev_ref, st_ref):
        @pl.when(pl.program_id(0) == 0)
        def _():
            st_ref[...] = jnp.zeros_like(st_ref)

        lo_h = lax.broadcasted_iota(jnp.int32, (tb, LANES), 1) < GLA_DK
        tril = tri_ref[...] > 0.5
        tril_b = tri_ref[...].astype(BF)
        ones_b = jnp.ones((LANES, LANES), BF)
        gg = gg_ref[...]
        _, _, _, _, qd, kd, kt, bls = _gla_block_common(qk_ref[...], ga_ref[...], wd_ref[...], bd_ref[...], tril_b)
        decs = [jnp.exp(bl) for bl in bls]
        heads = range(GLA_HEADS)
        chunks = _chunks(tb)
        lanes = [slice(h * LANES, (h + 1) * LANES) for h in heads]
        a = [_head(qd, h, lo_h).astype(BF) for h in heads]
        bm = [_head(kd, h, lo_h).astype(BF) for h in heads]
        ktl = [_head(kt, h, lo_h).astype(BF) for h in heads]
        vh = [v_ref[:, lanes[h]].astype(BF) for h in heads]
        sc = [_dot(a[h], bm[h], NT) for h in heads]
        upd = [[_dot(vh[h][rs], ktl[h][rs], TN) for rs in chunks] for h in heads]
        p = [jnp.where(tril, sc[h], 0.0).astype(BF) for h in heads]
        o = [_dot(p[h], vh[h]) for h in heads]
        states = []
        for h in heads:
            st = st_ref[h]
            entering = []
            for c in range(nch):
                entering.append(st)
                sprev_ref[c, h] = st
                st = st * decs[c][:, LANES * (h // 2):LANES * (h // 2 + 1)] + upd[h][c]
            st_ref[h] = st
            states.append(entering)
        inter = [[_dot(a[h][rs], states[h][c].astype(BF), NT) for c, rs in enumerate(chunks)] for h in heads]
        o = [o[h] + jnp.concatenate(inter[h], axis=0) for h in heads]
        ms = [_lane_mean(o[h] * o[h], ones_b) for h in heads]
        for h in heads:
            gzh = gz_ref[:, lanes[h]]
            opre_ref[:, lanes[h]] = o[h]
            og_ref[:, lanes[h]] = (((o[h] * lax.rsqrt(ms[h] + RMS_EPS)) * gg[:, lanes[h]])
                                   * (gzh * _sigmoid(gzh))).astype(og_ref.dtype)

    def col(width, off):
        return pl.BlockSpec((tb, width), lambda i: (i, off // width))

    return pl.pallas_call(
        body, name="gla_fwd", grid=(s // tb,),
        in_specs=[col(512, OFF_QK), col(512, OFF_V), col(512, OFF_GZ), col(LANES, OFF_GA),
                  _full((LANES, 256)), _full((1, 256)), _full((1, 512)), _full((tb, tb))],
        out_specs=[pl.BlockSpec((tb, 512), lambda i: (i, 0)), pl.BlockSpec((tb, 512), lambda i: (i, 0)),
                   pl.BlockSpec((nch, GLA_HEADS, LANES, LANES), lambda i: (i, 0, 0, 0))],
        out_shape=[jax.ShapeDtypeStruct((s, 512), BF), jax.ShapeDtypeStruct((s, 512), F32),
                   jax.ShapeDtypeStruct((s // GLA_CHUNK, GLA_HEADS, LANES, LANES), F32)],
        scratch_shapes=[pltpu.VMEM((GLA_HEADS, LANES, LANES), F32)],
        compiler_params=_params(("arbitrary",)),
    )(proj, proj, proj, proj, wdecp, bdec, ggla, _gla_triangle(tb))


def _gla_bwd(proj, dog, opre, sprev, wdecp, bdec, ggla):
    s = proj.shape[0]
    tb = min(GLA_ROWS, s)
    nch = tb // GLA_CHUNK
    nb = s // tb

    def body(qk_ref, v_ref, gz_ref, ga_ref, dog_ref, opre_ref, sprev_ref, wd_ref, bd_ref, gg_ref, tri_ref, triu_ref,
             dqk_ref, dv_ref, dgz_ref, dga_ref, dwd_ref, dbd_ref, dgg_ref, dst_ref):
        @pl.when(pl.program_id(0) == 0)
        def _():
            dst_ref[...] = jnp.zeros_like(dst_ref)
            dwd_ref[...] = jnp.zeros_like(dwd_ref)
            dbd_ref[...] = jnp.zeros_like(dbd_ref)
            dgg_ref[...] = jnp.zeros_like(dgg_ref)

        lo_h = lax.broadcasted_iota(jnp.int32, (tb, LANES), 1) < GLA_DK
        tril = tri_ref[...] > 0.5
        tril_b = tri_ref[...].astype(BF)
        triu_b = triu_ref[...].astype(BF)
        ones_b = jnp.ones((LANES, LANES), BF)
        last_row = (lax.broadcasted_iota(jnp.int32, (tb, LANES), 0) % GLA_CHUNK) == GLA_CHUNK - 1
        wd, gg = wd_ref[...], gg_ref[...]
        ga = ga_ref[...]
        z, eq, ek, f, qd, kd, kt, bls = _gla_block_common(qk_ref[...], ga, wd, bd_ref[...], tril_b)
        decs = [jnp.exp(bl) for bl in bls]
        chunks = _chunks(tb)
        heads = range(GLA_HEADS)
        lanes = [slice(h * LANES, (h + 1) * LANES) for h in heads]
        blks = [slice(LANES * (h // 2), LANES * (h // 2 + 1)) for h in heads]
        a = [_head(qd, h, lo_h).astype(BF) for h in heads]
        bm = [_head(kd, h, lo_h).astype(BF) for h in heads]
        ktl = [_head(kt, h, lo_h).astype(BF) for h in heads]
        vh = [v_ref[:, lanes[h]].astype(BF) for h in heads]
        sc = [_dot(a[h], bm[h], NT) for h in heads]

        o = [opre_ref[:, lanes[h]] for h in heads]
        ms = [_lane_mean(o[h] * o[h], ones_b) for h in heads]
        gz = [gz_ref[:, lanes[h]] for h in heads]
        dog = [dog_ref[:, lanes[h]] for h in heads]
        sg = [_sigmoid(gz[h]) for h in heads]
        r = [lax.rsqrt(ms[h] + RMS_EPS) for h in heads]
        ohat = [o[h] * r[h] for h in heads]
        sil = [gz[h] * sg[h] for h in heads]
        for h in heads:
            g_h = gg[:, lanes[h]]
            dgz_ref[:, lanes[h]] = (dog[h] * (ohat[h] * g_h) * (sg[h] * (1.0 + gz[h] * (1.0 - sg[h])))).astype(dgz_ref.dtype)
            dgg_ref[:, lanes[h]] += jnp.sum(dog[h] * sil[h] * ohat[h], axis=0, keepdims=True)
        dn = [dog[h] * sil[h] * gg[:, lanes[h]] for h in heads]
        mdn = [_lane_mean(dn[h] * ohat[h], ones_b) for h in heads]
        do = [(r[h] * (dn[h] - ohat[h] * mdn[h])).astype(BF) for h in heads]

        p = [jnp.where(tril, sc[h], 0.0).astype(BF) for h in heads]
        dpr = [_dot(do[h], vh[h], NT) for h in heads]
        incr = [[_dot(do[h][rs], a[h][rs], TN) for rs in chunks] for h in heads]
        dv = [_dot(p[h], do[h], TN) for h in heads]
        dp = [jnp.where(tril, dpr[h], 0.0).astype(BF) for h in heads]
        dqd = [_dot(dp[h], bm[h]) for h in heads]
        dkd = [_dot(dp[h], a[h], TN) for h in heads]
        st = [[sprev_ref[c, h] for c in range(nch)] for h in heads]
        leaving = []
        for h in heads:
            d = dst_ref[h]
            out = [None] * nch
            for c in reversed(range(nch)):
                out[c] = d
                d = d * decs[c][:, blks[h]] + incr[h][c]
            dst_ref[h] = d
            leaving.append(out)
        lv_b = [[leaving[h][c].astype(BF) for c in range(nch)] for h in heads]
        dv_s = [[_dot(ktl[h][rs], lv_b[h][c], NT) for c, rs in enumerate(chunks)] for h in heads]
        dqd_s = [[_dot(do[h][rs], st[h][c].astype(BF)) for c, rs in enumerate(chunks)] for h in heads]
        dkt_s = [[_dot(vh[h][rs], lv_b[h][c]) for c, rs in enumerate(chunks)] for h in heads]
        ddec = [[jnp.sum(leaving[h][c] * st[h][c], axis=0, keepdims=True) for c in range(nch)] for h in heads]
        for h in heads:
            dv_ref[:, lanes[h]] = (dv[h] + jnp.concatenate(dv_s[h], axis=0)).astype(dv_ref.dtype)
        dqd = [dqd[h] + jnp.concatenate(dqd_s[h], axis=0) for h in heads]
        dkt = [jnp.concatenate(dkt_s[h], axis=0) for h in heads]

        db_parts = []
        for pair in range(GLA_HEADS // 2):
            blk, h0, h1 = blks[2 * pair], 2 * pair, 2 * pair + 1
            dqd_b, dkd_b, dkt_b = dqd[h0] + dqd[h1], dkd[h0] + dkd[h1], dkt[h0] + dkt[h1]
            dqk_ref[:, blk] = (dqd_b * (eq[:, blk] * GLA_DK ** -0.5)).astype(dqk_ref.dtype)
            dqk_ref[:, 256 + LANES * pair:256 + LANES * (pair + 1)] = (dkd_b * ek[:, blk] + dkt_b * f[:, blk]).astype(dqk_ref.dtype)
            dkt_kt = dkt_b * kt[:, blk]
            db = dqd_b * qd[:, blk] - dkd_b * kd[:, blk] - dkt_kt
            dbl = [jnp.sum(dkt_kt[rs], axis=0, keepdims=True) + (ddec[h0][c] + ddec[h1][c]) * decs[c][:, blk]
                   for c, rs in enumerate(chunks)]
            db_parts.append(jnp.where(last_row, db + _per_chunk_rows(dbl, LANES), db))
        dla = _tri_matmul(triu_b, jnp.concatenate(db_parts, axis=1))
        dz32 = dla * (1.0 / GLA_TAU) * _sigmoid(-z)
        dz = dz32.astype(BF)
        dga_ref[...] = _dot(dz, wd, NT).astype(dga_ref.dtype)
        dwd_ref[...] += _dot(ga.astype(BF), dz, TN)
        dbd_ref[...] += jnp.sum(dz32, axis=0, keepdims=True)

    def col(width, off):
        return pl.BlockSpec((tb, width), lambda i: (nb - 1 - i, off // width))

    def rev(width):
        return pl.BlockSpec((tb, width), lambda i: (nb - 1 - i, 0))

    return pl.pallas_call(
        body, name="gla_bwd", grid=(nb,),
        in_specs=[col(512, OFF_QK), col(512, OFF_V), col(512, OFF_GZ), col(LANES, OFF_GA), rev(512), rev(512),
                  pl.BlockSpec((nch, GLA_HEADS, LANES, LANES), lambda i: (nb - 1 - i, 0, 0, 0)),
                  _full((LANES, 256)), _full((1, 256)), _full((1, 512)), _full((tb, tb)), _full((tb, tb))],
        out_specs=[rev(512), rev(512), rev(512), rev(LANES), _full((LANES, 256)), _full((1, 256)), _full((1, 512))],
        out_shape=[jax.ShapeDtypeStruct((s, 512), BF), jax.ShapeDtypeStruct((s, 512), BF),
                   jax.ShapeDtypeStruct((s, 512), BF), jax.ShapeDtypeStruct((s, LANES), BF),
                   jax.ShapeDtypeStruct((LANES, 256), F32), jax.ShapeDtypeStruct((1, 256), F32),
                   jax.ShapeDtypeStruct((1, 512), F32)],
        scratch_shapes=[pltpu.VMEM((GLA_HEADS, LANES, LANES), F32)],
        compiler_params=_params(("arbitrary",)),
    )(proj, proj, proj, proj, dog, opre, sprev, wdecp, bdec, ggla, _gla_triangle(tb), _gla_triangle(tb).T)


_SWA_COL_HEADS = (0, 2, 1, 3, 4, 6, 5, 7)
_SWA_COLS = SWA_HEADS * SWA_BLOCK


def _swa_masks():
    lo2 = lax.broadcasted_iota(jnp.int32, (2 * SWA_BLOCK, LANES), 1) < 64
    lane1 = lax.broadcasted_iota(jnp.int32, (SWA_BLOCK, LANES), 1)
    first_half = (lane1 % 64) < 32
    key = lax.broadcasted_iota(jnp.int32, (SWA_BLOCK, _SWA_COLS), 0)
    query = lax.broadcasted_iota(jnp.int32, (SWA_BLOCK, _SWA_COLS), 1) % SWA_BLOCK
    return lo2, lane1 < 64, first_half, key > query


def _merge_band(t, prev_mask, prev_bias=None):
    prev = t[:SWA_BLOCK] if prev_bias is None else t[:SWA_BLOCK] + prev_bias
    return jnp.where(prev_mask, prev, t[SWA_BLOCK:])


def _split_band(t, prev_mask_b):
    prev = t * prev_mask_b
    return jnp.concatenate([prev, t - prev], axis=0)


def _kv_variants(t, lo2):
    tr = pltpu.roll(t, 64, 1)
    lo_v = [jnp.where(lo2, t, 0.0).astype(BF), jnp.where(lo2, tr, 0.0).astype(BF)]
    hi_v = [jnp.where(lo2, 0.0, tr).astype(BF), jnp.where(lo2, 0.0, t).astype(BF)]
    return lo_v, hi_v


def _kv_variants_t(t):
    tt = t.T
    sw = jnp.concatenate([tt[64:], tt[:64]], axis=0)
    top = lax.broadcasted_iota(jnp.int32, tt.shape, 0) < 64
    lo_v = [jnp.where(top, tt, 0.0).astype(BF), jnp.where(top, sw, 0.0).astype(BF)]
    hi_v = [jnp.where(top, 0.0, sw).astype(BF), jnp.where(top, 0.0, tt).astype(BF)]
    return lo_v, hi_v


def _swa_softmax(qg, k_lo, k_hi, prev_mask, prev_bias, sinks_ref):
    st = jnp.concatenate([_dot(k_lo[0], qg[0], NT), _dot(k_hi[0], qg[0], NT),
                          _dot(k_lo[1], qg[1], NT), _dot(k_hi[1], qg[1], NT)], axis=1)
    st = _merge_band(st, prev_mask, prev_bias)
    sink = jnp.concatenate([jnp.full((1, SWA_BLOCK), sinks_ref[0, hd], F32) for hd in _SWA_COL_HEADS], axis=1)
    m = jnp.maximum(jnp.max(st, axis=0, keepdims=True), sink)
    ex = jnp.exp(st - m)
    es = jnp.exp(sink - m)
    inv = 1.0 / (jnp.sum(ex, axis=0, keepdims=True) + es)
    return ex, es, inv


def _no_prev_bias(block_index):
    return jnp.where(block_index > 0, 0.0, -1e30).astype(F32)


def _swa_queries(sq_ref, rows, cosb, sinb, first_half):
    qs = [_rope(sq_ref[rows, p * LANES:(p + 1) * LANES], cosb, sinb, first_half) * 0.125 for p in range(4)]
    return [jnp.concatenate(qs[0:2], axis=0), jnp.concatenate(qs[2:4], axis=0)]


def _swa_fwd(proj, cos, sin, sinks):
    s = proj.shape[0]
    nq = min(SWA_QBLOCKS, s // SWA_BLOCK)
    tq = nq * SWA_BLOCK

    def body(sq_ref, sz_ref, sk_ref, sv_ref, cos_ref, sin_ref, sinks_ref, os_ref, opre_ref, kprev, vprev):
        n = pl.program_id(0)

        @pl.when(n == 0)
        def _():
            kprev[...] = jnp.zeros_like(kprev)
            vprev[...] = jnp.zeros_like(vprev)

        lo2, _, first_half, prev_mask = _swa_masks()
        prev_mask_b = jnp.where(prev_mask, 1.0, 0.0).astype(BF)
        kp, vp = kprev[...], vprev[...]
        for j in range(nq):
            rows = slice(j * SWA_BLOCK, (j + 1) * SWA_BLOCK)
            cosb, sinb = cos_ref[rows, :], sin_ref[rows, :]
            kc = _rope(sk_ref[rows, :], cosb, sinb, first_half)
            vc = sv_ref[rows, :]
            k_lo, k_hi = _kv_variants(jnp.concatenate([kp, kc], axis=0), lo2)
            vt_lo, vt_hi = _kv_variants_t(jnp.concatenate([vp, vc], axis=0))
            qg = [q.astype(BF) for q in _swa_queries(sq_ref, rows, cosb, sinb, first_half)]
            ex, _, inv = _swa_softmax(qg, k_lo, k_hi, prev_mask, _no_prev_bias(n) if j == 0 else None, sinks_ref)
            pt = _split_band(ex.astype(BF), prev_mask_b)
            for g in range(2):
                c0, c1, c2 = 512 * g, 512 * g + 256, 512 * g + 512
                ot = _dot(vt_lo[g], pt[:, c0:c1]) * inv[:, c0:c1] + _dot(vt_hi[g], pt[:, c1:c2]) * inv[:, c1:c2]
                og = ot.T
                for i in range(2):
                    ls = slice((2 * g + i) * LANES, (2 * g + i + 1) * LANES)
                    o = og[i * SWA_BLOCK:(i + 1) * SWA_BLOCK]
                    sz = sz_ref[rows, ls]
                    opre_ref[rows, ls] = o
                    os_ref[rows, ls] = (o * (sz * _sigmoid(sz))).astype(os_ref.dtype)
            kp, vp = kc, vc
        kprev[...] = kp
        vprev[...] = vp

    def col(width, off):
        return pl.BlockSpec((tq, width), lambda i: (i, off // width))

    row = pl.BlockSpec((tq, LANES), lambda i: (i, 0))
    return pl.pallas_call(
        body, name="swa_fwd", grid=(s // tq,),
        in_specs=[col(512, OFF_SQ), col(512, OFF_SZ), col(LANES, OFF_SK), col(LANES, OFF_SV), row, row,
                  pl.BlockSpec(memory_space=pltpu.SMEM)],
        out_specs=[pl.BlockSpec((tq, 512), lambda i: (i, 0))] * 2,
        out_shape=[jax.ShapeDtypeStruct((s, 512), BF), jax.ShapeDtypeStruct((s, 512), F32)],
        scratch_shapes=[pltpu.VMEM((SWA_BLOCK, LANES), F32)] * 2,
        compiler_params=_params(("arbitrary",)),
    )(proj, proj, proj, proj, cos, sin, sinks)


def _swa_bwd(proj, dos, opre, cos, sin, sinks):
    s = proj.shape[0]
    nq = min(SWA_QBLOCKS, s // SWA_BLOCK)
    tq = nq * SWA_BLOCK

    def body(sq_ref, sz_ref, sk_ref, sv_ref, dos_ref, opre_ref, cos_ref, sin_ref, sinks_ref,
             dsq_ref, dsz_ref, dsk_ref, dsv_ref, dsink_ref, kprev, vprev, cprev, sprev):
        n = pl.program_id(0)

        @pl.when(n == 0)
        def _():
            kprev[...] = jnp.zeros_like(kprev)
            vprev[...] = jnp.zeros_like(vprev)
            cprev[...] = jnp.zeros_like(cprev)
            sprev[...] = jnp.zeros_like(sprev)
            for hd in range(SWA_HEADS):
                dsink_ref[0, hd] = 0.0

        lo2, lo1, first_half, prev_mask = _swa_masks()
        prev_mask_b = jnp.where(prev_mask, 1.0, 0.0).astype(BF)
        lo1s = jnp.concatenate([lo1, lo1], axis=0)

        def home(m0, m1):
            t0 = m0 + pltpu.roll(m0, 64, 1)
            t1 = m1 + pltpu.roll(m1, 64, 1)
            return jnp.where(lo2, t0, t1)

        kp, vp, cp_, sp_ = kprev[...], vprev[...], cprev[...], sprev[...]
        for j in range(nq):
            rows = slice(j * SWA_BLOCK, (j + 1) * SWA_BLOCK)
            blk = n * nq + j
            cosb, sinb = cos_ref[rows, :], sin_ref[rows, :]
            kc = _rope(sk_ref[rows, :], cosb, sinb, first_half)
            vc = sv_ref[rows, :]
            kcat = jnp.concatenate([kp, kc], axis=0)
            k_lo, k_hi = _kv_variants(kcat, lo2)
            kt_lo, kt_hi = _kv_variants_t(kcat)
            v_lo, v_hi = _kv_variants(jnp.concatenate([vp, vc], axis=0), lo2)
            qg32 = _swa_queries(sq_ref, rows, cosb, sinb, first_half)
            qg = [q.astype(BF) for q in qg32]
            ex, es, inv = _swa_softmax(qg, k_lo, k_hi, prev_mask, _no_prev_bias(n) if j == 0 else None, sinks_ref)
            pr, ps = ex * inv, es * inv

            dog32 = []
            for g in range(2):
                parts = []
                for i in range(2):
                    ls = slice((2 * g + i) * LANES, (2 * g + i + 1) * LANES)
                    sz = sz_ref[rows, ls]
                    sg = _sigmoid(sz)
                    dos_p = dos_ref[rows, ls]
                    dsz_ref[rows, ls] = (dos_p * opre_ref[rows, ls] * (sg * (1.0 + sz * (1.0 - sg)))).astype(dsz_ref.dtype)
                    parts.append(dos_p * (sz * sg))
                dog32.append(jnp.concatenate(parts, axis=0))
            dog = [t.astype(BF) for t in dog32]
            dpr = _merge_band(jnp.concatenate([_dot(v_lo[0], dog[0], NT), _dot(v_hi[0], dog[0], NT),
                                               _dot(v_lo[1], dog[1], NT), _dot(v_hi[1], dog[1], NT)], axis=1), prev_mask)
            rd = jnp.sum(pr * dpr, axis=0, keepdims=True)
            ds = _split_band((pr * (dpr - rd)).astype(BF), prev_mask_b)
            prb = _split_band(pr.astype(BF), prev_mask_b)
            sink_term = ps * rd
            for r, hd in enumerate(_SWA_COL_HEADS):
                dsink_ref[0, hd] += -jnp.sum(sink_term[:, r * SWA_BLOCK:(r + 1) * SWA_BLOCK])

            dk_g, dv_g = [], []
            for g in range(2):
                c0, c1, c2 = 512 * g, 512 * g + 256, 512 * g + 512
                dq = (_dot(kt_lo[g], ds[:, c0:c1]) + _dot(kt_hi[g], ds[:, c1:c2])).T
                for i in range(2):
                    ls = slice((2 * g + i) * LANES, (2 * g + i + 1) * LANES)
                    dsq_ref[rows, ls] = _rope_t(dq[i * SWA_BLOCK:(i + 1) * SWA_BLOCK] * 0.125, cosb, sinb,
                                                first_half).astype(dsq_ref.dtype)
                q_split = jnp.concatenate([jnp.where(lo1s, qg32[g], 0.0), jnp.where(lo1s, 0.0, qg32[g])], axis=0).astype(BF)
                do_split = jnp.concatenate([jnp.where(lo1s, dog32[g], 0.0), jnp.where(lo1s, 0.0, dog32[g])], axis=0).astype(BF)
                dk_g.append(_dot(ds[:, c0:c2], q_split))
                dv_g.append(_dot(prb[:, c0:c2], do_split))
            dk = home(dk_g[0], dk_g[1])
            dv = home(dv_g[0], dv_g[1])
            cur = pl.ds(pl.multiple_of(blk * SWA_BLOCK, SWA_BLOCK), SWA_BLOCK)
            dsk_ref[cur, :] = _rope_t(dk[SWA_BLOCK:], cosb, sinb, first_half)
            dsv_ref[cur, :] = dv[SWA_BLOCK:]
            dk_prev = _rope_t(dk[:SWA_BLOCK], cp_, sp_, first_half)
            dv_prev = dv[:SWA_BLOCK]
            if j == 0:
                @pl.when(n > 0)
                def _():
                    prv = pl.ds(pl.multiple_of((blk - 1) * SWA_BLOCK, SWA_BLOCK), SWA_BLOCK)
                    dsk_ref[prv, :] += dk_prev
                    dsv_ref[prv, :] += dv_prev
            else:
                prv = pl.ds(pl.multiple_of((blk - 1) * SWA_BLOCK, SWA_BLOCK), SWA_BLOCK)
                dsk_ref[prv, :] += dk_prev
                dsv_ref[prv, :] += dv_prev
            kp, vp, cp_, sp_ = kc, vc, cosb, sinb
        kprev[...] = kp
        vprev[...] = vp
        cprev[...] = cp_
        sprev[...] = sp_

    def col(width, off):
        return pl.BlockSpec((tq, width), lambda i: (i, off // width))

    row = pl.BlockSpec((tq, LANES), lambda i: (i, 0))
    wide = pl.BlockSpec((tq, 512), lambda i: (i, 0))
    return pl.pallas_call(
        body, name="swa_bwd", grid=(s // tq,),
        in_specs=[col(512, OFF_SQ), col(512, OFF_SZ), col(LANES, OFF_SK), col(LANES, OFF_SV), wide, wide, row, row,
                  pl.BlockSpec(memory_space=pltpu.SMEM)],
        out_specs=[wide, wide, _full((s, LANES)), _full((s, LANES)), pl.BlockSpec(memory_space=pltpu.SMEM)],
        out_shape=[jax.ShapeDtypeStruct((s, 512), BF), jax.ShapeDtypeStruct((s, 512), BF),
                   jax.ShapeDtypeStruct((s, LANES), F32), jax.ShapeDtypeStruct((s, LANES), F32),
                   jax.ShapeDtypeStruct((1, SWA_HEADS), F32)],
        scratch_shapes=[pltpu.VMEM((SWA_BLOCK, LANES), F32)] * 4,
        compiler_params=_params(("arbitrary",)),
    )(proj, proj, proj, proj, dos, opre, cos, sin, sinks)


def _outproj(og, osw, w_out, x2d, target, gate, g_final):
    s = x2d.shape[0]
    tm = min(512, s)

    def body(og_ref, os_ref, w_ref, x_ref, t_ref, gate_ref, gf_ref,
             dx2_ref, dog_ref, dos_ref, dw_ref, loss_ref, dgf_ref, dgate_ref):
        @pl.when(pl.program_id(0) == 0)
        def _():
            dw_ref[...] = jnp.zeros_like(dw_ref)
            loss_ref[...] = jnp.zeros_like(loss_ref)
            dgf_ref[...] = jnp.zeros_like(dgf_ref)
            dgate_ref[...] = jnp.zeros_like(dgate_ref)

        ogv, osv, w = og_ref[...], os_ref[...], w_ref[...]
        gate, gf = gate_ref[...], gf_ref[...]
        y = _dot(ogv, w[:512]) + _dot(osv, w[512:])
        x2 = x_ref[...] + gate * y
        r = lax.rsqrt(jnp.mean(x2 * x2, axis=-1, keepdims=True) + RMS_EPS)
        xn = x2 * r
        err = xn * gf - t_ref[...]
        loss_ref[...] += 0.5 * jnp.sum(jnp.mean(err * err, axis=-1, keepdims=True), axis=0, keepdims=True)
        dyf = err * (1.0 / D_MODEL)
        dgf_ref[...] += jnp.sum(dyf * xn, axis=0, keepdims=True)
        t = dyf * gf
        dx2 = r * (t - xn * jnp.mean(t * xn, axis=-1, keepdims=True))
        dx2_ref[...] = dx2
        dgate_ref[...] += jnp.sum(dx2 * y, axis=0, keepdims=True)
        dy = (dx2 * gate).astype(BF)
        dmix = _dot(dy, w, NT)
        dog_ref[...] = dmix[:, :512]
        dos_ref[...] = dmix[:, 512:]
        dw_ref[:512, :] += _dot(ogv, dy, TN)
        dw_ref[512:, :] += _dot(osv, dy, TN)

    half = pl.BlockSpec((tm, 512), lambda i: (i, 0))
    rowb = pl.BlockSpec((tm, D_MODEL), lambda i: (i, 0))
    vec = _full((1, D_MODEL))
    return pl.pallas_call(
        body, name="outproj", grid=(s // tm,),
        in_specs=[half, half, _full((D_MODEL, D_MODEL)), rowb, rowb, vec, vec],
        out_specs=[rowb, half, half, _full((D_MODEL, D_MODEL)), _full((1, 1)), vec, vec],
        out_shape=[jax.ShapeDtypeStruct((s, D_MODEL), F32), jax.ShapeDtypeStruct((s, 512), F32),
                   jax.ShapeDtypeStruct((s, 512), F32), jax.ShapeDtypeStruct((D_MODEL, D_MODEL), F32),
                   jax.ShapeDtypeStruct((1, 1), F32), jax.ShapeDtypeStruct((1, D_MODEL), F32),
                   jax.ShapeDtypeStruct((1, D_MODEL), F32)],
        compiler_params=_params(("arbitrary",)),
    )(og, osw, w_out, x2d, target, gate, g_final)


_PIECES = ((OFF_QK, 512), (OFF_V, 512), (OFF_GZ, 512), (OFF_SQ, 512), (OFF_SZ, 512),
           (OFF_SK, LANES), (OFF_SV, LANES), (OFF_GA, LANES))

_UNPAD_ROWS = ((OFF_QK, 0, 1024),
               (OFF_GA, 1024, GLA_RANK),
               (OFF_GZ, 1040, 1024),
               (OFF_SK, 2064, 256),
               (OFF_SZ, 2320, 512))


def _inproj_bwd(x2d, shift, sc1p, g_norm, wpad_t, dx2, pieces):
    s = x2d.shape[0]
    tm = min(512, s)
    nsteps = s // tm

    def body(x_ref, sh_ref, sc_ref, g_ref, w_hbm, dx2_ref, *rest):
        piece_refs = rest[:len(_PIECES)]
        gx_ref, dw_hbm, dsh_ref, dsc_ref, dg_ref, w_vm, dw_vm, sem, out_sems = rest[len(_PIECES):]
        i = pl.program_id(0)

        @pl.when(i == 0)
        def _():
            cp = pltpu.make_async_copy(w_hbm, w_vm, sem)
            cp.start()
            dw_vm[...] = jnp.zeros_like(dw_vm)
            dsh_ref[...] = jnp.zeros_like(dsh_ref)
            dsc_ref[...] = jnp.zeros_like(dsc_ref)
            dg_ref[...] = jnp.zeros_like(dg_ref)
            cp.wait()

        g, sc1p_v = g_ref[...], sc_ref[...]
        xn, r, h = _modnorm(x_ref[...], g, sc1p_v, sh_ref[...])
        hb = h.astype(BF)
        dh = None
        for (off, width), pr in zip(_PIECES, piece_refs):
            dp = pr[...].astype(BF)
            part = _dot(dp, w_vm[off:off + width, :])
            dh = part if dh is None else dh + part
            dw_vm[off:off + width, :] += _dot(dp, hb, TN)
        dsh_ref[...] += jnp.sum(dh, axis=0, keepdims=True)
        dsc_ref[...] += jnp.sum(dh * (xn * g), axis=0, keepdims=True)
        dg_ref[...] += jnp.sum(dh * xn * sc1p_v, axis=0, keepdims=True)
        dxn = dh * g * sc1p_v
        gx_ref[...] = dx2_ref[...] + r * (dxn - xn * jnp.mean(dxn * xn, axis=-1, keepdims=True))

        @pl.when(i == nsteps - 1)
        def _():
            copies = [pltpu.make_async_copy(dw_vm.at[src:src + n], dw_hbm.at[dst:dst + n], out_sems.at[k])
                      for k, (src, dst, n) in enumerate(_UNPAD_ROWS)]
            for cp in copies:
                cp.start()
            for cp in copies:
                cp.wait()

    rowb = pl.BlockSpec((tm, D_MODEL), lambda i: (i, 0))
    vec = _full((1, D_MODEL))
    anyspec = pl.BlockSpec(memory_space=pl.ANY)
    piece_specs = [pl.BlockSpec((tm, width), lambda i: (i, 0)) for _, width in _PIECES]
    return pl.pallas_call(
        body, name="inproj_bwd", grid=(nsteps,),
        in_specs=[rowb, vec, vec, vec, anyspec, rowb] + piece_specs,
        out_specs=[rowb, anyspec, vec, vec, vec],
        out_shape=[jax.ShapeDtypeStruct((s, D_MODEL), F32), jax.ShapeDtypeStruct((D_IN, D_MODEL), F32),
                   jax.ShapeDtypeStruct((1, D_MODEL), F32), jax.ShapeDtypeStruct((1, D_MODEL), F32),
                   jax.ShapeDtypeStruct((1, D_MODEL), F32)],
        scratch_shapes=[pltpu.VMEM((D_PAD, D_MODEL), BF), pltpu.VMEM((D_PAD, D_MODEL), F32), pltpu.SemaphoreType.DMA,
                        pltpu.SemaphoreType.DMA((len(_UNPAD_ROWS),))],
        compiler_params=_params(("arbitrary",)),
    )(x2d, shift, sc1p, g_norm, wpad_t, dx2, *pieces)


def _adam(w, g, m, v):
    m2 = ADAM_B1 * m + (1.0 - ADAM_B1) * g
    v2 = ADAM_B2 * v + (1.0 - ADAM_B2) * (g * g)
    m_hat = m2 / (1.0 - ADAM_B1 ** ADAM_STEP)
    v_hat = v2 / (1.0 - ADAM_B2 ** ADAM_STEP)
    delta = -ADAM_LR * (m_hat / (jnp.sqrt(v_hat) + ADAM_EPS) + ADAM_WD * w)
    return delta, m2, v2


def _adamw(w, g, m, v, name):
    rr, cc = w.shape
    tc = min(256, cc)

    def body(w_ref, g_ref, m_ref, v_ref, d_ref, m2_ref, v2_ref):
        d_ref[...], m2_ref[...], v2_ref[...] = _adam(w_ref[...], g_ref[...], m_ref[...], v_ref[...])

    blk = pl.BlockSpec((rr, tc), lambda i: (0, i))
    return pl.pallas_call(
        body, name=name, grid=(cc // tc,), in_specs=[blk] * 4, out_specs=[blk] * 3,
        out_shape=[jax.ShapeDtypeStruct((rr, cc), F32)] * 3,
        compiler_params=_params(("arbitrary",)),
    )(w, g, m, v)


def _ada_update(c_all, dmod_cols, w, m, v):
    rr, cc = w.shape
    tr = min(256, rr)
    c_all = jnp.pad(c_all, ((0, 8), (0, 0)))
    dmod_cols = jnp.pad(dmod_cols, ((0, 8), (0, 0)))

    def body(c_ref, dm_ref, w_ref, m_ref, v_ref, g_ref, d_ref, m2_ref, v2_ref):
        cv = c_ref[...]
        sc = (cv * _sigmoid(cv)).astype(BF)
        g = _dot(sc, dm_ref[...].astype(BF), TN)
        g_ref[...] = g
        d_ref[...], m2_ref[...], v2_ref[...] = _adam(w_ref[...], g, m_ref[...], v_ref[...])

    blk = pl.BlockSpec((tr, cc), lambda i: (i, 0))
    return pl.pallas_call(
        body, name="ada_update", grid=(rr // tr,),
        in_specs=[pl.BlockSpec((16, tr), lambda i: (0, i)), _full((16, cc)), blk, blk, blk],
        out_specs=[blk] * 4, out_shape=[jax.ShapeDtypeStruct((rr, cc), F32)] * 4,
        compiler_params=_params(("arbitrary",)),
    )(c_all, dmod_cols, w, m, v)


def _small_update(parts, weights, moms, vels):
    n = len(weights)

    def body(*refs):
        p_refs, w_refs, m_refs, v_refs = refs[:n + 1], refs[n + 1:2 * n + 1], refs[2 * n + 1:3 * n + 1], refs[3 * n + 1:4 * n + 1]
        outs = refs[4 * n + 1:]
        for i in range(n):
            g = p_refs[i][0]
            for d in range(1, 8):
                g = g + p_refs[i][d]
            delta, m2, v2 = _adam(w_refs[i][...], g, m_refs[i][...], v_refs[i][...])
            outs[4 * i][...] = g
            outs[4 * i + 1][...] = delta
            outs[4 * i + 2][...] = m2
            outs[4 * i + 3][...] = v2
        tot = p_refs[n][0]
        for d in range(1, 8):
            tot = tot + p_refs[n][d]
        outs[4 * n][...] = tot

    out_shape = []
    for w in weights:
        out_shape += [jax.ShapeDtypeStruct(w.shape, F32)] * 4
    out_shape.append(jax.ShapeDtypeStruct(parts[n].shape[1:], F32))
    return pl.pallas_call(body, name="small_update", out_shape=out_shape, compiler_params=_params())(
        *parts, *weights, *moms, *vels)


def _pad_w_in_t(w):
    pad = jnp.zeros((LANES - GLA_RANK, w.shape[1]), w.dtype)
    return jnp.concatenate([w[dst:dst + n] for _, dst, n in sorted(_UNPAD_ROWS)] + [pad], axis=0)


def _rows8(a):
    flat = a.reshape(-1)
    rows = -(-flat.shape[0] // LANES)
    rows8 = -(-rows // 8) * 8
    flat = jnp.pad(flat, (0, rows8 * LANES - flat.shape[0]))
    return flat.reshape(rows8, LANES)


def kernel(x, c, positions, w_ada, b_ada, g_norm, w_in, w_decay, b_decay, g_gla_head, sinks, w_out, g_final, loss_target, m_w_ada, m_b_ada, m_g_norm, m_w_in, m_w_decay, m_b_decay, m_g_gla_head, m_sinks, m_w_out, m_g_final, v_w_ada, v_b_ada, v_g_norm, v_w_in, v_w_decay, v_b_decay, v_g_gla_head, v_sinks, v_w_out, v_g_final):
    ax, ay, ac = lax.axis_index("x"), lax.axis_index("y"), lax.axis_index("c")
    chip = 2 * ax + ay
    dev = 2 * chip + ac
    s = x.shape[1]
    x2d = x[0]
    target = loss_target[0]
    w_ada2, w_out2, w_dec2 = w_ada[0], w_out[0], w_decay[0]
    w_in_t, m_w_in_t, v_w_in_t = w_in[0].T, m_w_in[0].T, v_w_in[0].T
    ada_cols = w_ada2.shape[1]
    in_cols = w_in_t.shape[0]
    out_rows = w_out2.shape[0]
    half = D_MODEL // 2

    cw = jnp.concatenate([c.reshape(8, LANES), w_dec2.reshape(8, LANES)], axis=0)
    b_shard = lax.dynamic_slice(b_ada, (0, chip * ada_cols), (1, ada_cols))
    half_in = lax.dynamic_slice(w_in_t, (0, ac * half), (in_cols, half)).astype(BF)
    half_out = lax.dynamic_slice(w_out2, (ac * (out_rows // 2), 0), (out_rows // 2, D_MODEL)).astype(BF)
    inv_freq = 1.0 / (ROPE_THETA ** (jnp.arange(0, 64, 2, dtype=F32) / 64))
    first, mod_all, w_in_all, w_out_all, cos, sin = _prologue(
        cw, w_ada2, b_shard, half_in, half_out, positions.reshape(s, 1), jnp.tile(inv_freq, 4).reshape(1, LANES))

    first = first.reshape(8, 2, 8, LANES)
    c_all = first[:, 0].reshape(8, D_MODEL)
    w_dec_full = first[0::2, 1].reshape(4, GLA_RANK, 64).transpose(1, 0, 2).reshape(GLA_RANK, 256)
    mod = mod_all.reshape(4, 2, 8, ada_cols)[:, 0]
    mod = lax.dynamic_slice(mod, (0, dev, 0), (4, 1, ada_cols)).reshape(1, 4 * ada_cols)
    shift, sc1p, gate = mod[:, :D_MODEL], 1.0 + mod[:, D_MODEL:2 * D_MODEL], mod[:, 2 * D_MODEL:]
    w_in_all = w_in_all.reshape(4, 2, in_cols, half)
    wpad_t = _pad_w_in_t(w_in_all.transpose(0, 2, 1, 3).reshape(4 * in_cols, D_MODEL))
    w_out_all = w_out_all.reshape(D_MODEL, D_MODEL)

    wdecp = jnp.pad(w_dec_full, ((0, LANES - GLA_RANK), (0, 0))).astype(BF)

    proj = _inproj_fwd(x2d, shift, sc1p, g_norm, wpad_t)
    og, o_gla, sprev = _gla_fwd(proj, wdecp, b_decay, g_gla_head)
    osw, o_swa = _swa_fwd(proj, cos, sin, sinks)
    dx2, dog, dos, dw_out, loss_p, dgf, dgate = _outproj(og, osw, w_out_all, x2d, target, gate, g_final.reshape(1, D_MODEL))
    dsq, dsz, dsk, dsv, dsinks = _swa_bwd(proj, dos, o_swa, cos, sin, sinks)
    dqk, dv, dgz, dga, dwdp, dbd, dgg = _gla_bwd(proj, dog, o_gla, sprev, wdecp, b_decay, g_gla_head)
    pieces = (dqk, dv, dgz, dsq, dsz, dsk, dsv, dga)
    gx, dw_in_t, dshift, dscale, dgn = _inproj_bwd(x2d, shift, sc1p, g_norm, wpad_t, dx2, pieces)

    segs = [jnp.concatenate([dshift, dscale, dgate], axis=1), dgn, dgf, dwdp[:GLA_RANK], dbd, dgg, dsinks, loss_p]
    packed = [_rows8(a) for a in segs]
    offs = [0]
    for a in packed:
        offs.append(offs[-1] + a.shape[0])
    g_w_in_t, g_w_out, small = _epilogue(dw_in_t.reshape(4, in_cols, D_MODEL), dw_out.reshape(4, out_rows, D_MODEL),
                                         jnp.concatenate(packed, axis=0))

    def seg(i, size):
        return small[:, offs[i]:offs[i + 1]].reshape(8, -1)[:, :size]

    dmod_all = seg(0, 3 * D_MODEL)
    dwd_all = lax.dynamic_slice(seg(3, GLA_RANK * 256).reshape(8, GLA_RANK, 256), (0, 0, chip * 64), (8, GLA_RANK, 64))
    parts = [dmod_all.reshape(8, 1, 3 * D_MODEL), seg(1, D_MODEL).reshape(8, 1, D_MODEL), dwd_all,
             seg(4, 256).reshape(8, 1, 256), seg(5, 512).reshape(8, 1, 512), seg(6, SWA_HEADS).reshape(8, 1, SWA_HEADS),
             seg(2, D_MODEL).reshape(8, 1, D_MODEL), seg(7, LANES).reshape(8, 1, LANES)]
    smalls = _small_update(
        parts,
        [b_ada, g_norm, w_dec2, b_decay, g_gla_head, sinks, g_final.reshape(1, D_MODEL)],
        [m_b_ada, m_g_norm, m_w_decay[0], m_b_decay, m_g_gla_head, m_sinks, m_g_final.reshape(1, D_MODEL)],
        [v_b_ada, v_g_norm, v_w_decay[0], v_b_decay, v_g_gla_head, v_sinks, v_g_final.reshape(1, D_MODEL)])
    (g_b_ada, d_b_ada, nm_b_ada, nv_b_ada, g_gn, d_gn, nm_gn, nv_gn, g_wd, d_wd, nm_wd, nv_wd,
     g_bd, d_bd, nm_bd, nv_bd, g_gg, d_gg, nm_gg, nv_gg, g_sk, d_sk, nm_sk, nv_sk,
     g_gf, d_gf, nm_gf, nv_gf, loss_row) = smalls
    loss = loss_row[0, 0]

    dmod_cols = lax.dynamic_slice(dmod_all, (0, chip * ada_cols), (8, ada_cols))
    g_w_ada, d_w_ada, nm_w_ada, nv_w_ada = _ada_update(c_all, dmod_cols, w_ada2, m_w_ada[0], v_w_ada[0])
    d_w_in_t, nm_w_in_t, nv_w_in_t = _adamw(w_in_t, g_w_in_t, m_w_in_t, v_w_in_t, "adamw_w_in")
    g_w_in, d_w_in, nm_w_in, nv_w_in = g_w_in_t.T, d_w_in_t.T, nm_w_in_t.T, nv_w_in_t.T
    d_w_out, nm_w_out, nv_w_out = _adamw(w_out2, g_w_out, m_w_out[0], v_w_out[0], "adamw_w_out")

    flat = lambda a: a.reshape(D_MODEL)
    grads = [g_w_ada[None], g_b_ada, g_gn, g_w_in[None], g_wd[None], g_bd, g_gg, g_sk, g_w_out[None], flat(g_gf)]
    deltas = [d_w_ada[None], d_b_ada, d_gn, d_w_in[None], d_wd[None], d_bd, d_gg, d_sk, d_w_out[None], flat(d_gf)]
    new_m = [nm_w_ada[None], nm_b_ada, nm_gn, nm_w_in[None], nm_wd[None], nm_bd, nm_gg, nm_sk, nm_w_out[None], flat(nm_gf)]
    new_v = [nv_w_ada[None], nv_b_ada, nv_gn, nv_w_in[None], nv_wd[None], nv_bd, nv_gg, nv_sk, nv_w_out[None], flat(nv_gf)]
    return (loss, gx[None], *grads, *deltas, *new_m, *new_v)
```

```python
import jax
import jax.numpy as jnp
from jax import lax
from jax.experimental import pallas as pl
from jax.experimental.pallas import tpu as pltpu

F32 = jnp.float32
BF = jnp.bfloat16

D_MODEL = 1024
GLA_HEADS = 4
GLA_DK = 64
GLA_CHUNK = 64
GLA_RANK = 16
GLA_TAU = 16.0
GLA_SUB = 256
GLA_ROWS = 512
SWA_HEADS = 8
SWA_BLOCK = 128
SWA_QBLOCKS = 8
RMS_EPS = 1e-6
ROPE_THETA = 10000.0

OFF_QK, OFF_V, OFF_GZ, OFF_SQ, OFF_SZ, OFF_SK, OFF_SV, OFF_GA = 0, 512, 1024, 1536, 2048, 2560, 2688, 2816
D_PAD = 2944
D_IN = 2832
LANES = 128
VMEM_LIMIT = 56 * 1024 * 1024

ADAM_LR, ADAM_B1, ADAM_B2, ADAM_EPS, ADAM_WD, ADAM_STEP = 0.001, 0.9, 0.999, 1e-08, 0.01, 10

NT = (((1,), (1,)), ((), ()))
TN = (((0,), (0,)), ((), ()))
MESH = pl.DeviceIdType.MESH


def _dot(a, b, dims=None):
    if dims is None:
        return jnp.dot(a, b, preferred_element_type=F32)
    return lax.dot_general(a, b, dims, preferred_element_type=F32)


def _sigmoid(x):
    return 1.0 / (1.0 + jnp.exp(-x))


def _params(sem=None):
    return pltpu.CompilerParams(dimension_semantics=sem, vmem_limit_bytes=VMEM_LIMIT)


def _full(shape):
    return pl.BlockSpec(shape, lambda i: (0,) * len(shape))


_GATHER_SEMS = [pltpu.SemaphoreType.DMA((7,)), pltpu.SemaphoreType.DMA((7,)), pltpu.SemaphoreType.DMA]


class _Gather:
    def __init__(self, x_ref, out_ref, send_sems, recv_sems, local_sem):
        x, y, c = lax.axis_index("x"), lax.axis_index("y"), lax.axis_index("c")
        self.me, self.sibling, self.c = (x, y, c), (x, y, 1 - c), c
        self.chips = [(1 - x, y), (x, 1 - y), (1 - x, 1 - y)]
        self.x_ref, self.out_ref, self.send_sems, self.recv_sems = x_ref, out_ref, send_sems, recv_sems
        self.mine = pltpu.make_async_copy(x_ref, self._slab(*self.me), local_sem)

    def _slab(self, px, py, pc):
        return self.out_ref.at[4 * px + 2 * py + pc]

    def _copy(self, k, blk, to, src=None):
        return pltpu.make_async_remote_copy(
            src_ref=self._slab(*blk) if src is None else src, dst_ref=self._slab(*blk),
            send_sem=self.send_sems.at[k], recv_sem=self.recv_sems.at[k], device_id=to, device_id_type=MESH)

    def start(self):
        self.mine.start()
        self.sent = [self._copy(0, self.me, self.sibling, src=self.x_ref)]
        self.sent += [self._copy(1 + j, self.me, (*chip, self.c), src=self.x_ref) for j, chip in enumerate(self.chips)]
        for cp in self.sent:
            cp.start()

    def relay(self):
        for j, chip in enumerate(self.chips):
            self._copy(1 + j, (*chip, self.c), self.me).wait_recv()
            cp = self._copy(4 + j, (*chip, self.c), self.sibling)
            cp.start()
            self.sent.append(cp)

    def finish(self):
        self._copy(0, self.sibling, self.me).wait_recv()
        for j, chip in enumerate(self.chips):
            self._copy(4 + j, (*chip, 1 - self.c), self.me).wait_recv()
        for cp in self.sent:
            cp.wait_send()
        self.mine.wait()


def _prologue(cw, w_ada, b_shard, half_in, half_out, pos_col, inv_freq):
    s = pos_col.shape[0]
    rt = min(512, s)

    def body(cw_ref, wada_ref, b_ref, hin_ref, hout_ref, pos_ref, f_ref,
             first_ref, mod_ref, win_ref, wout_ref, cos_ref, sin_ref, mod_blk, *sems):
        g_c = _Gather(cw_ref, first_ref, *sems[0:3])
        g_in = _Gather(hin_ref, win_ref, *sems[3:6])
        g_out = _Gather(hout_ref, wout_ref, *sems[6:9])
        g_mod = _Gather(mod_blk, mod_ref, *sems[9:12])
        g_c.start()
        g_in.start()
        g_out.start()
        g_c.relay()
        g_c.finish()
        c_rows = [jnp.concatenate([first_ref[d, r:r + 1, :] for r in range(8)], axis=1) for d in range(8)]
        c_all = jnp.concatenate(c_rows, axis=0)
        sc = (c_all * _sigmoid(c_all)).astype(BF)
        mod_blk[...] = _dot(sc, wada_ref[...].astype(BF)) + b_ref[...]
        g_mod.start()

        def rope_rows(i, carry):
            rows = pl.ds(pl.multiple_of(i * rt, rt), rt)
            ang = pos_ref[rows, :].astype(F32) * f_ref[...]
            lane = lax.broadcasted_iota(jnp.int32, ang.shape, 1)
            cos_ref[rows, :] = jnp.cos(ang)
            sn = jnp.sin(ang)
            sin_ref[rows, :] = jnp.where((lane % 64) < 32, -sn, sn)
            return carry

        lax.fori_loop(0, s // rt, rope_rows, 0)
        g_mod.relay()
        g_out.relay()
        g_in.relay()
        g_mod.finish()
        g_out.finish()
        g_in.finish()

    vm = pl.BlockSpec(memory_space=pltpu.VMEM)
    return pl.pallas_call(
        body, name="prologue",
        out_shape=[jax.ShapeDtypeStruct((8,) + cw.shape, F32), jax.ShapeDtypeStruct((8, 8, w_ada.shape[1]), F32),
                   jax.ShapeDtypeStruct((8,) + half_in.shape, half_in.dtype),
                   jax.ShapeDtypeStruct((8,) + half_out.shape, half_out.dtype),
                   jax.ShapeDtypeStruct((s, LANES), F32), jax.ShapeDtypeStruct((s, LANES), F32)],
        in_specs=[vm] * 7, out_specs=[vm] * 6,
        scratch_shapes=[pltpu.VMEM((8, w_ada.shape[1]), F32)] + _GATHER_SEMS * 4,
        compiler_params=pltpu.CompilerParams(vmem_limit_bytes=VMEM_LIMIT),
    )(cw, w_ada, b_shard, half_in, half_out, pos_col, inv_freq)


def _reduce_scratch(rr, cc):
    c2 = cc // 2
    return [pltpu.VMEM((4, rr, c2), F32), pltpu.VMEM((4, rr, c2), F32), pltpu.VMEM((3, rr, c2), BF),
            pltpu.VMEM((3, rr, c2), BF), pltpu.VMEM((rr, c2), F32),
            pltpu.SemaphoreType.DMA((5,)), pltpu.SemaphoreType.DMA((5,)), pltpu.SemaphoreType.DMA((2,))]


class _Reduce:
    def __init__(self, p_hbm, out_ref, acc_ref, own_ref, send_ref, land_ref, res_ref, send_sems, recv_sems, local_sems):
        x, y, c = lax.axis_index("x"), lax.axis_index("y"), lax.axis_index("c")
        c2 = out_ref.shape[1] // 2
        self.c, self.my_chip, sibling = c, 2 * x + y, (x, y, 1 - c)
        self.chips = [(1 - x, y), (x, 1 - y), (1 - x, 1 - y)]
        mine = pl.ds(pl.multiple_of(c * c2, c2), c2)
        other = pl.ds(pl.multiple_of((1 - c) * c2, c2), c2)
        self.acc_ref, self.own_ref, self.send_ref, self.land_ref, self.res_ref = acc_ref, own_ref, send_ref, land_ref, res_ref
        self.send_sems, self.recv_sems = send_sems, recv_sems
        self.own = pltpu.make_async_copy(p_hbm.at[:, :, mine], own_ref, local_sems.at[0])
        self.swap = pltpu.make_async_remote_copy(
            src_ref=p_hbm.at[:, :, other], dst_ref=acc_ref, send_sem=send_sems.at[0], recv_sem=recv_sems.at[0],
            device_id=sibling, device_id_type=MESH)
        self.put = pltpu.make_async_copy(res_ref, out_ref.at[:, mine], local_sems.at[1])
        self.share = pltpu.make_async_remote_copy(
            src_ref=res_ref, dst_ref=out_ref.at[:, mine], send_sem=send_sems.at[4],
            recv_sem=recv_sems.at[4], device_id=sibling, device_id_type=MESH)

    def start(self):
        self.own.start()
        self.swap.start()

    def combine_and_send(self):
        self.own.wait()
        self.swap.wait()
        for j in range(4):
            self.acc_ref[j] = self.acc_ref[j] + self.own_ref[j]
        self.sends = []
        for k, (tx, ty) in enumerate(self.chips):
            self.send_ref[k] = self.acc_ref[2 * tx + ty].astype(self.send_ref.dtype)
            cp = pltpu.make_async_remote_copy(
                src_ref=self.send_ref.at[k], dst_ref=self.land_ref.at[k], send_sem=self.send_sems.at[1 + k],
                recv_sem=self.recv_sems.at[1 + k], device_id=(tx, ty, self.c), device_id_type=MESH)
            cp.start()
            self.sends.append(cp)

    def total_and_share(self):
        for cp in self.sends:
            cp.wait_recv()
        total = self.acc_ref[self.my_chip]
        for k in range(3):
            total = total + self.land_ref[k].astype(F32)
        self.res_ref[...] = total
        for cp in self.sends:
            cp.wait_send()
        self.put.start()
        self.share.start()

    def finish(self):
        self.put.wait()
        self.share.wait()


def _epilogue(dw_in_parts, dw_out_parts, small):
    _, r_in, cc = dw_in_parts.shape
    _, r_out, _ = dw_out_parts.shape
    n_red = len(_reduce_scratch(r_in, cc))

    def body(pin_hbm, pout_hbm, small_ref, gin_ref, gout_ref, small_all_ref, *scratch):
        red_in = _Reduce(pin_hbm, gin_ref, *scratch[0:n_red])
        red_out = _Reduce(pout_hbm, gout_ref, *scratch[n_red:2 * n_red])
        gat = _Gather(small_ref, small_all_ref, *scratch[2 * n_red:])
        red_out.start()
        red_in.start()
        gat.start()
        red_out.combine_and_send()
        red_in.combine_and_send()
        gat.relay()
        red_out.total_and_share()
        red_in.total_and_share()
        gat.finish()
        red_out.finish()
        red_in.finish()

    vm = pl.BlockSpec(memory_space=pltpu.VMEM)
    anyspec = pl.BlockSpec(memory_space=pl.ANY)
    return pl.pallas_call(
        body, name="epilogue",
        out_shape=[jax.ShapeDtypeStruct((r_in, cc), F32), jax.ShapeDtypeStruct((r_out, cc), F32),
                   jax.ShapeDtypeStruct((8,) + small.shape, F32)],
        in_specs=[anyspec, anyspec, vm], out_specs=[vm, vm, vm],
        scratch_shapes=_reduce_scratch(r_in, cc) + _reduce_scratch(r_out, cc) + _GATHER_SEMS,
        compiler_params=pltpu.CompilerParams(vmem_limit_bytes=VMEM_LIMIT),
    )(dw_in_parts, dw_out_parts, small)


def _rope(t, cosb, sinb, first_half):
    partner = jnp.where(first_half, pltpu.roll(t, 96, 1), pltpu.roll(t, 32, 1))
    return t * cosb + partner * sinb


def _rope_t(g, cosb, sinb, first_half):
    gs = g * sinb
    partner = jnp.where(first_half, pltpu.roll(gs, 96, 1), pltpu.roll(gs, 32, 1))
    return g * cosb + partner


def _modnorm(x, g, sc1p, shift):
    r = lax.rsqrt(jnp.mean(x * x, axis=-1, keepdims=True) + RMS_EPS)
    xn = x * r
    return xn, r, (xn * g) * sc1p + shift


def _inproj_fwd(x2d, shift, sc1p, g_norm, wpad_t):
    s = x2d.shape[0]
    tm = min(512, s)

    def body(x_ref, sh_ref, sc_ref, g_ref, w_ref, o_ref):
        _, _, h = _modnorm(x_ref[...], g_ref[...], sc_ref[...], sh_ref[...])
        o_ref[...] = _dot(h.astype(BF), w_ref[...], NT)

    vec = _full((1, D_MODEL))
    return pl.pallas_call(
        body, name="inproj_fwd", grid=(s // tm,),
        in_specs=[pl.BlockSpec((tm, D_MODEL), lambda i: (i, 0)), vec, vec, vec, _full((D_PAD, D_MODEL))],
        out_specs=pl.BlockSpec((tm, D_PAD), lambda i: (i, 0)),
        out_shape=jax.ShapeDtypeStruct((s, D_PAD), F32),
        compiler_params=_params(("arbitrary",)),
    )(x2d, shift, sc1p, g_norm, wpad_t)


def _split3(a):
    hi = a.astype(BF)
    r1 = a - hi.astype(F32)
    mid = r1.astype(BF)
    lo = (r1 - mid.astype(F32)).astype(BF)
    return hi, mid, lo


def _tri_matmul(tri, a):
    hi, mid, lo = _split3(a)
    return _dot(tri, hi) + _dot(tri, mid) + _dot(tri, lo)


def _chunks(tb):
    return [slice(c * GLA_CHUNK, (c + 1) * GLA_CHUNK) for c in range(tb // GLA_CHUNK)]


def _per_chunk_rows(rows, width):
    return jnp.concatenate([jnp.broadcast_to(r, (GLA_CHUNK, width)) for r in rows], axis=0)


def _gla_triangle(tb):
    row = lax.broadcasted_iota(jnp.int32, (tb, tb), 0)
    col = lax.broadcasted_iota(jnp.int32, (tb, tb), 1)
    return (((row // GLA_CHUNK) == (col // GLA_CHUNK)) & (col <= row)).astype(F32)


def _lane_mean(x, ones_b):
    hi = x.astype(BF)
    lo = (x - hi.astype(F32)).astype(BF)
    return (_dot(hi, ones_b) + _dot(lo, ones_b)) * (1.0 / LANES)


def _head(t, h, lo_h):
    blk = t[:, LANES * (h // 2):LANES * (h // 2 + 1)]
    return jnp.where(lo_h, blk, 0.0) if h % 2 == 0 else jnp.where(lo_h, 0.0, blk)


def _gla_block_common(qk, ga, wd, bd, tril_b):
    tb = qk.shape[0]
    q, k = qk[:, :256], qk[:, 256:]
    z = _dot(ga.astype(BF), wd) + bd
    la = (jnp.minimum(z, 0.0) - jnp.log(1.0 + jnp.exp(-jnp.abs(z)))) * (1.0 / GLA_TAU)
    b = _tri_matmul(tril_b, la)
    bls = [b[rs.stop - 1:rs.stop, :] for rs in _chunks(tb)]
    eq = jnp.exp(b)
    ek = jnp.exp(-b)
    f = jnp.exp(_per_chunk_rows(bls, 256) - b)
    return z, eq, ek, f, q * (eq * GLA_DK ** -0.5), k * ek, k * f, bls


def _gla_units(s):
    sub = min(GLA_SUB, s)
    tb = min(GLA_ROWS, s)
    subs = [slice(i * sub, (i + 1) * sub) for i in range(tb // sub)]
    units = [(i, h) for i in range(len(subs)) for h in range(GLA_HEADS)]
    return tb, sub, subs, units


def _gla_fwd(proj, wdecp, bdec, ggla):
    s = proj.shape[0]
    tb, sub, subs, units = _gla_units(s)
    nch = sub // GLA_CHUNK

    def body(qk_ref, v_ref, gz_ref, ga_ref, wd_ref, bd_ref, gg_ref, tri_ref, og_ref, opre_ref, sprev_ref, st_ref):
        @pl.when(pl.program_id(0) == 0)
        def _():
            st_ref[...] = jnp.zeros_like(st_ref)

        lo_h = lax.broadcasted_iota(jnp.int32, (sub, LANES), 1) < GLA_DK
        tril = tri_ref[...] > 0.5
        tril_b = tri_ref[...].astype(BF)
        ones_b = jnp.ones((LANES, LANES), BF)
        gg, wd, bd = gg_ref[...], wd_ref[...], bd_ref[...]
        chunks = _chunks(sub)
        lanes = [slice(h * LANES, (h + 1) * LANES) for h in range(GLA_HEADS)]
        com = [_gla_block_common(qk_ref[sl, :], ga_ref[sl, :], wd, bd, tril_b) for sl in subs]
        decs = [[jnp.exp(bl) for bl in cm[7]] for cm in com]
        a = {(i, h): _head(com[i][4], h, lo_h).astype(BF) for i, h in units}
        bm = {(i, h): _head(com[i][5], h, lo_h).astype(BF) for i, h in units}
        ktl = {(i, h): _head(com[i][6], h, lo_h).astype(BF) for i, h in units}
        vh = {(i, h): v_ref[subs[i], lanes[h]].astype(BF) for i, h in units}
        sc = {u: _dot(a[u], bm[u], NT) for u in units}
        upd = {u: [_dot(vh[u][rs], ktl[u][rs], TN) for rs in chunks] for u in units}
        p = {u: jnp.where(tril, sc[u], 0.0).astype(BF) for u in units}
        o = {u: _dot(p[u], vh[u]) for u in units}
        states = {}
        for h in range(GLA_HEADS):
            st = st_ref[h]
            for i in range(len(subs)):
                entering = []
                for c in range(nch):
                    entering.append(st)
                    sprev_ref[i * nch + c, h] = st
                    st = st * decs[i][c][:, LANES * (h // 2):LANES * (h // 2 + 1)] + upd[(i, h)][c]
                states[(i, h)] = entering
            st_ref[h] = st
        inter = {u: [_dot(a[u][rs], states[u][c].astype(BF), NT) for c, rs in enumerate(chunks)] for u in units}
        o = {u: o[u] + jnp.concatenate(inter[u], axis=0) for u in units}
        ms = {u: _lane_mean(o[u] * o[u], ones_b) for u in units}
        for i, h in units:
            gzh = gz_ref[subs[i], lanes[h]]
            opre_ref[subs[i], lanes[h]] = o[(i, h)]
            og_ref[subs[i], lanes[h]] = (((o[(i, h)] * lax.rsqrt(ms[(i, h)] + RMS_EPS)) * gg[:, lanes[h]])
                                         * (gzh * _sigmoid(gzh))).astype(og_ref.dtype)

    def col(width, off):
        return pl.BlockSpec((tb, width), lambda i: (i, off // width))

    return pl.pallas_call(
        body, name="gla_fwd", grid=(s // tb,),
        in_specs=[col(512, OFF_QK), col(512, OFF_V), col(512, OFF_GZ), col(LANES, OFF_GA),
                  _full((LANES, 256)), _full((1, 256)), _full((1, 512)), _full((sub, sub))],
        out_specs=[pl.BlockSpec((tb, 512), lambda i: (i, 0)), pl.BlockSpec((tb, 512), lambda i: (i, 0)),
                   pl.BlockSpec((tb // GLA_CHUNK, GLA_HEADS, LANES, LANES), lambda i: (i, 0, 0, 0))],
        out_shape=[jax.ShapeDtypeStruct((s, 512), BF), jax.ShapeDtypeStruct((s, 512), F32),
                   jax.ShapeDtypeStruct((s // GLA_CHUNK, GLA_HEADS, LANES, LANES), F32)],
        scratch_shapes=[pltpu.VMEM((GLA_HEADS, LANES, LANES), F32)],
        compiler_params=_params(("arbitrary",)),
    )(proj, proj, proj, proj, wdecp, bdec, ggla, _gla_triangle(sub))


def _gla_bwd(proj, dog, opre, sprev, wdecp, bdec, ggla):
    s = proj.shape[0]
    tb, sub, subs, units = _gla_units(s)
    nsub = len(subs)
    nch = sub // GLA_CHUNK
    nb = s // tb

    def body(qk_ref, v_ref, gz_ref, ga_ref, dog_ref, opre_ref, sprev_ref, wd_ref, bd_ref, gg_ref, tri_ref, triu_ref,
             dqk_ref, dv_ref, dgz_ref, dga_ref, dwd_ref, dbd_ref, dgg_ref, dst_ref):
        @pl.when(pl.program_id(0) == 0)
        def _():
            dst_ref[...] = jnp.zeros_like(dst_ref)
            dwd_ref[...] = jnp.zeros_like(dwd_ref)
            dbd_ref[...] = jnp.zeros_like(dbd_ref)
            dgg_ref[...] = jnp.zeros_like(dgg_ref)

        lo_h = lax.broadcasted_iota(jnp.int32, (sub, LANES), 1) < GLA_DK
        tril = tri_ref[...] > 0.5
        tril_b = tri_ref[...].astype(BF)
        triu_b = triu_ref[...].astype(BF)
        ones_b = jnp.ones((LANES, LANES), BF)
        last_row = (lax.broadcasted_iota(jnp.int32, (sub, LANES), 0) % GLA_CHUNK) == GLA_CHUNK - 1
        wd, gg, bd = wd_ref[...], gg_ref[...], bd_ref[...]
        chunks = _chunks(sub)
        lanes = [slice(h * LANES, (h + 1) * LANES) for h in range(GLA_HEADS)]
        blks = [slice(LANES * (h // 2), LANES * (h // 2 + 1)) for h in range(GLA_HEADS)]
        ga = [ga_ref[sl, :] for sl in subs]
        com = [_gla_block_common(qk_ref[sl, :], ga[i], wd, bd, tril_b) for i, sl in enumerate(subs)]
        decs = [[jnp.exp(bl) for bl in cm[7]] for cm in com]
        a = {(i, h): _head(com[i][4], h, lo_h).astype(BF) for i, h in units}
        bm = {(i, h): _head(com[i][5], h, lo_h).astype(BF) for i, h in units}
        ktl = {(i, h): _head(com[i][6], h, lo_h).astype(BF) for i, h in units}
        vh = {(i, h): v_ref[subs[i], lanes[h]].astype(BF) for i, h in units}
        sc = {u: _dot(a[u], bm[u], NT) for u in units}

        o = {(i, h): opre_ref[subs[i], lanes[h]] for i, h in units}
        ms = {u: _lane_mean(o[u] * o[u], ones_b) for u in units}
        gz = {(i, h): gz_ref[subs[i], lanes[h]] for i, h in units}
        dog = {(i, h): dog_ref[subs[i], lanes[h]] for i, h in units}
        sg = {u: _sigmoid(gz[u]) for u in units}
        r = {u: lax.rsqrt(ms[u] + RMS_EPS) for u in units}
        ohat = {u: o[u] * r[u] for u in units}
        sil = {u: gz[u] * sg[u] for u in units}
        for i, h in units:
            u = (i, h)
            dgz_ref[subs[i], lanes[h]] = (dog[u] * (ohat[u] * gg[:, lanes[h]])
                                          * (sg[u] * (1.0 + gz[u] * (1.0 - sg[u])))).astype(dgz_ref.dtype)
            dgg_ref[:, lanes[h]] += jnp.sum(dog[u] * sil[u] * ohat[u], axis=0, keepdims=True)
        dn = {(i, h): dog[(i, h)] * sil[(i, h)] * gg[:, lanes[h]] for i, h in units}
        mdn = {u: _lane_mean(dn[u] * ohat[u], ones_b) for u in units}
        do = {u: (r[u] * (dn[u] - ohat[u] * mdn[u])).astype(BF) for u in units}

        p = {u: jnp.where(tril, sc[u], 0.0).astype(BF) for u in units}
        dpr = {u: _dot(do[u], vh[u], NT) for u in units}
        incr = {u: [_dot(do[u][rs], a[u][rs], TN) for rs in chunks] for u in units}
        dv = {u: _dot(p[u], do[u], TN) for u in units}
        dp = {u: jnp.where(tril, dpr[u], 0.0).astype(BF) for u in units}
        dqd = {u: _dot(dp[u], bm[u]) for u in units}
        dkd = {u: _dot(dp[u], a[u], TN) for u in units}
        st = {(i, h): [sprev_ref[i * nch + c, h] for c in range(nch)] for i, h in units}
        leaving = {}
        for h in range(GLA_HEADS):
            d = dst_ref[h]
            for i in reversed(range(nsub)):
                out = [None] * nch
                for c in reversed(range(nch)):
                    out[c] = d
                    d = d * decs[i][c][:, blks[h]] + incr[(i, h)][c]
                leaving[(i, h)] = out
            dst_ref[h] = d
        lv_b = {u: [leaving[u][c].astype(BF) for c in range(nch)] for u in units}
        dv_s = {u: [_dot(ktl[u][rs], lv_b[u][c], NT) for c, rs in enumerate(chunks)] for u in units}
        dqd_s = {u: [_dot(do[u][rs], st[u][c].astype(BF)) for c, rs in enumerate(chunks)] for u in units}
        dkt_s = {u: [_dot(vh[u][rs], lv_b[u][c]) for c, rs in enumerate(chunks)] for u in units}
        ddec = {u: [jnp.sum(leaving[u][c] * st[u][c], axis=0, keepdims=True) for c in range(nch)] for u in units}
        for i, h in units:
            dv_ref[subs[i], lanes[h]] = (dv[(i, h)] + jnp.concatenate(dv_s[(i, h)], axis=0)).astype(dv_ref.dtype)
        dqd = {u: dqd[u] + jnp.concatenate(dqd_s[u], axis=0) for u in units}
        dkt = {u: jnp.concatenate(dkt_s[u], axis=0) for u in units}

        db = []
        for i, sl in enumerate(subs):
            _, eq, ek, f, qd, kd, kt, _ = com[i]
            parts = []
            for pair in range(GLA_HEADS // 2):
                blk, u0, u1 = blks[2 * pair], (i, 2 * pair), (i, 2 * pair + 1)
                dqd_b, dkd_b, dkt_b = dqd[u0] + dqd[u1], dkd[u0] + dkd[u1], dkt[u0] + dkt[u1]
                dqk_ref[sl, blk] = (dqd_b * (eq[:, blk] * GLA_DK ** -0.5)).astype(dqk_ref.dtype)
                dqk_ref[sl, 256 + LANES * pair:256 + LANES * (pair + 1)] = (dkd_b * ek[:, blk] + dkt_b * f[:, blk]).astype(dqk_ref.dtype)
                dkt_kt = dkt_b * kt[:, blk]
                dbp = dqd_b * qd[:, blk] - dkd_b * kd[:, blk] - dkt_kt
                dbl = [jnp.sum(dkt_kt[rs], axis=0, keepdims=True) + (ddec[u0][c] + ddec[u1][c]) * decs[i][c][:, blk]
                       for c, rs in enumerate(chunks)]
                parts.append(jnp.where(last_row, dbp + _per_chunk_rows(dbl, LANES), dbp))
            db.append(jnp.concatenate(parts, axis=1))
        dla = [_tri_matmul(triu_b, db[i]) for i in range(nsub)]
        dz32 = [dla[i] * (1.0 / GLA_TAU) * _sigmoid(-com[i][0]) for i in range(nsub)]
        dz = [t.astype(BF) for t in dz32]
        for i, sl in enumerate(subs):
            dga_ref[sl, :] = _dot(dz[i], wd, NT).astype(dga_ref.dtype)
            dwd_ref[...] += _dot(ga[i].astype(BF), dz[i], TN)
            dbd_ref[...] += jnp.sum(dz32[i], axis=0, keepdims=True)

    def col(width, off):
        return pl.BlockSpec((tb, width), lambda i: (nb - 1 - i, off // width))

    def rev(width):
        return pl.BlockSpec((tb, width), lambda i: (nb - 1 - i, 0))

    return pl.pallas_call(
        body, name="gla_bwd", grid=(nb,),
        in_specs=[col(512, OFF_QK), col(512, OFF_V), col(512, OFF_GZ), col(LANES, OFF_GA), rev(512), rev(512),
                  pl.BlockSpec((tb // GLA_CHUNK, GLA_HEADS, LANES, LANES), lambda i: (nb - 1 - i, 0, 0, 0)),
                  _full((LANES, 256)), _full((1, 256)), _full((1, 512)), _full((sub, sub)), _full((sub, sub))],
        out_specs=[rev(512), rev(512), rev(512), rev(LANES), _full((LANES, 256)), _full((1, 256)), _full((1, 512))],
        out_shape=[jax.ShapeDtypeStruct((s, 512), BF), jax.ShapeDtypeStruct((s, 512), BF),
                   jax.ShapeDtypeStruct((s, 512), BF), jax.ShapeDtypeStruct((s, LANES), BF),
                   jax.ShapeDtypeStruct((LANES, 256), F32), jax.ShapeDtypeStruct((1, 256), F32),
                   jax.ShapeDtypeStruct((1, 512), F32)],
        scratch_shapes=[pltpu.VMEM((GLA_HEADS, LANES, LANES), F32)],
        compiler_params=_params(("arbitrary",)),
    )(proj, proj, proj, proj, dog, opre, sprev, wdecp, bdec, ggla, _gla_triangle(sub), _gla_triangle(sub).T)


_SWA_COL_HEADS = (0, 2, 1, 3, 4, 6, 5, 7)
_SWA_COLS = SWA_HEADS * SWA_BLOCK


def _swa_masks():
    lo2 = lax.broadcasted_iota(jnp.int32, (2 * SWA_BLOCK, LANES), 1) < 64
    lane1 = lax.broadcasted_iota(jnp.int32, (SWA_BLOCK, LANES), 1)
    first_half = (lane1 % 64) < 32
    key = lax.broadcasted_iota(jnp.int32, (SWA_BLOCK, _SWA_COLS), 0)
    query = lax.broadcasted_iota(jnp.int32, (SWA_BLOCK, _SWA_COLS), 1) % SWA_BLOCK
    return lo2, lane1 < 64, first_half, key > query


def _merge_band(t, prev_mask, prev_bias=None):
    prev = t[:SWA_BLOCK] if prev_bias is None else t[:SWA_BLOCK] + prev_bias
    return jnp.where(prev_mask, prev, t[SWA_BLOCK:])


def _split_band(t, prev_mask_b):
    prev = t * prev_mask_b
    return jnp.concatenate([prev, t - prev], axis=0)


def _kv_variants(t, lo2):
    tr = pltpu.roll(t, 64, 1)
    lo_v = [jnp.where(lo2, t, 0.0).astype(BF), jnp.where(lo2, tr, 0.0).astype(BF)]
    hi_v = [jnp.where(lo2, 0.0, tr).astype(BF), jnp.where(lo2, 0.0, t).astype(BF)]
    return lo_v, hi_v


def _kv_variants_t(t):
    tt = t.T
    sw = jnp.concatenate([tt[64:], tt[:64]], axis=0)
    top = lax.broadcasted_iota(jnp.int32, tt.shape, 0) < 64
    lo_v = [jnp.where(top, tt, 0.0).astype(BF), jnp.where(top, sw, 0.0).astype(BF)]
    hi_v = [jnp.where(top, 0.0, sw).astype(BF), jnp.where(top, 0.0, tt).astype(BF)]
    return lo_v, hi_v


def _swa_softmax(qg, k_lo, k_hi, prev_mask, prev_bias, sinks_ref):
    st = jnp.concatenate([_dot(k_lo[0], qg[0], NT), _dot(k_hi[0], qg[0], NT),
                          _dot(k_lo[1], qg[1], NT), _dot(k_hi[1], qg[1], NT)], axis=1)
    st = _merge_band(st, prev_mask, prev_bias)
    sink = jnp.concatenate([jnp.full((1, SWA_BLOCK), sinks_ref[0, hd], F32) for hd in _SWA_COL_HEADS], axis=1)
    m = jnp.maximum(jnp.max(st, axis=0, keepdims=True), sink)
    ex = jnp.exp(st - m)
    es = jnp.exp(sink - m)
    inv = 1.0 / (jnp.sum(ex, axis=0, keepdims=True) + es)
    return ex, es, inv


def _no_prev_bias(block_index):
    return jnp.where(block_index > 0, 0.0, -1e30).astype(F32)


def _swa_queries(sq_ref, rows, cosb, sinb, first_half):
    qs = [_rope(sq_ref[rows, p * LANES:(p + 1) * LANES], cosb, sinb, first_half) * 0.125 for p in range(4)]
    return [jnp.concatenate(qs[0:2], axis=0), jnp.concatenate(qs[2:4], axis=0)]


def _swa_fwd(proj, cos, sin, sinks):
    s = proj.shape[0]
    nq = min(SWA_QBLOCKS, s // SWA_BLOCK)
    tq = nq * SWA_BLOCK

    def body(sq_ref, sz_ref, sk_ref, sv_ref, cos_ref, sin_ref, sinks_ref, os_ref, opre_ref, kprev, vprev):
        n = pl.program_id(0)

        @pl.when(n == 0)
        def _():
            kprev[...] = jnp.zeros_like(kprev)
            vprev[...] = jnp.zeros_like(vprev)

        lo2, _, first_half, prev_mask = _swa_masks()
        prev_mask_b = jnp.where(prev_mask, 1.0, 0.0).astype(BF)
        kp, vp = kprev[...], vprev[...]
        for j in range(nq):
            rows = slice(j * SWA_BLOCK, (j + 1) * SWA_BLOCK)
            cosb, sinb = cos_ref[rows, :], sin_ref[rows, :]
            kc = _rope(sk_ref[rows, :], cosb, sinb, first_half)
            vc = sv_ref[rows, :]
            k_lo, k_hi = _kv_variants(jnp.concatenate([kp, kc], axis=0), lo2)
            vt_lo, vt_hi = _kv_variants_t(jnp.concatenate([vp, vc], axis=0))
            qg = [q.astype(BF) for q in _swa_queries(sq_ref, rows, cosb, sinb, first_half)]
            ex, _, inv = _swa_softmax(qg, k_lo, k_hi, prev_mask, _no_prev_bias(n) if j == 0 else None, sinks_ref)
            pt = _split_band(ex.astype(BF), prev_mask_b)
            for g in range(2):
                c0, c1, c2 = 512 * g, 512 * g + 256, 512 * g + 512
                ot = _dot(vt_lo[g], pt[:, c0:c1]) * inv[:, c0:c1] + _dot(vt_hi[g], pt[:, c1:c2]) * inv[:, c1:c2]
                og = ot.T
                for i in range(2):
                    ls = slice((2 * g + i) * LANES, (2 * g + i + 1) * LANES)
                    o = og[i * SWA_BLOCK:(i + 1) * SWA_BLOCK]
                    sz = sz_ref[rows, ls]
                    opre_ref[rows, ls] = o
                    os_ref[rows, ls] = (o * (sz * _sigmoid(sz))).astype(os_ref.dtype)
            kp, vp = kc, vc
        kprev[...] = kp
        vprev[...] = vp

    def col(width, off):
        return pl.BlockSpec((tq, width), lambda i: (i, off // width))

    row = pl.BlockSpec((tq, LANES), lambda i: (i, 0))
    return pl.pallas_call(
        body, name="swa_fwd", grid=(s // tq,),
        in_specs=[col(512, OFF_SQ), col(512, OFF_SZ), col(LANES, OFF_SK), col(LANES, OFF_SV), row, row,
                  pl.BlockSpec(memory_space=pltpu.SMEM)],
        out_specs=[pl.BlockSpec((tq, 512), lambda i: (i, 0))] * 2,
        out_shape=[jax.ShapeDtypeStruct((s, 512), BF), jax.ShapeDtypeStruct((s, 512), F32)],
        scratch_shapes=[pltpu.VMEM((SWA_BLOCK, LANES), F32)] * 2,
        compiler_params=_params(("arbitrary",)),
    )(proj, proj, proj, proj, cos, sin, sinks)


def _swa_bwd(proj, dos, opre, cos, sin, sinks):
    s = proj.shape[0]
    nq = min(SWA_QBLOCKS, s // SWA_BLOCK)
    tq = nq * SWA_BLOCK

    def body(sq_ref, sz_ref, sk_ref, sv_ref, dos_ref, opre_ref, cos_ref, sin_ref, sinks_ref,
             dsq_ref, dsz_ref, dsk_ref, dsv_ref, dsink_ref, kprev, vprev, cprev, sprev):
        n = pl.program_id(0)

        @pl.when(n == 0)
        def _():
            kprev[...] = jnp.zeros_like(kprev)
            vprev[...] = jnp.zeros_like(vprev)
            cprev[...] = jnp.zeros_like(cprev)
            sprev[...] = jnp.zeros_like(sprev)
            for hd in range(SWA_HEADS):
                dsink_ref[0, hd] = 0.0

        lo2, lo1, first_half, prev_mask = _swa_masks()
        prev_mask_b = jnp.where(prev_mask, 1.0, 0.0).astype(BF)
        lo1s = jnp.concatenate([lo1, lo1], axis=0)

        def home(m0, m1):
            t0 = m0 + pltpu.roll(m0, 64, 1)
            t1 = m1 + pltpu.roll(m1, 64, 1)
            return jnp.where(lo2, t0, t1)

        kp, vp, cp_, sp_ = kprev[...], vprev[...], cprev[...], sprev[...]
        for j in range(nq):
            rows = slice(j * SWA_BLOCK, (j + 1) * SWA_BLOCK)
            blk = n * nq + j
            cosb, sinb = cos_ref[rows, :], sin_ref[rows, :]
            kc = _rope(sk_ref[rows, :], cosb, sinb, first_half)
            vc = sv_ref[rows, :]
            kcat = jnp.concatenate([kp, kc], axis=0)
            k_lo, k_hi = _kv_variants(kcat, lo2)
            kt_lo, kt_hi = _kv_variants_t(kcat)
            v_lo, v_hi = _kv_variants(jnp.concatenate([vp, vc], axis=0), lo2)
            qg32 = _swa_queries(sq_ref, rows, cosb, sinb, first_half)
            qg = [q.astype(BF) for q in qg32]
            ex, es, inv = _swa_softmax(qg, k_lo, k_hi, prev_mask, _no_prev_bias(n) if j == 0 else None, sinks_ref)
            pr, ps = ex * inv, es * inv

            dog32 = []
            for g in range(2):
                parts = []
                for i in range(2):
                    ls = slice((2 * g + i) * LANES, (2 * g + i + 1) * LANES)
                    sz = sz_ref[rows, ls]
                    sg = _sigmoid(sz)
                    dos_p = dos_ref[rows, ls]
                    dsz_ref[rows, ls] = (dos_p * opre_ref[rows, ls] * (sg * (1.0 + sz * (1.0 - sg)))).astype(dsz_ref.dtype)
                    parts.append(dos_p * (sz * sg))
                dog32.append(jnp.concatenate(parts, axis=0))
            dog = [t.astype(BF) for t in dog32]
            dpr = _merge_band(jnp.concatenate([_dot(v_lo[0], dog[0], NT), _dot(v_hi[0], dog[0], NT),
                                               _dot(v_lo[1], dog[1], NT), _dot(v_hi[1], dog[1], NT)], axis=1), prev_mask)
            rd = jnp.sum(pr * dpr, axis=0, keepdims=True)
            ds = _split_band((pr * (dpr - rd)).astype(BF), prev_mask_b)
            prb = _split_band(pr.astype(BF), prev_mask_b)
            sink_term = ps * rd
            for r, hd in enumerate(_SWA_COL_HEADS):
                dsink_ref[0, hd] += -jnp.sum(sink_term[:, r * SWA_BLOCK:(r + 1) * SWA_BLOCK])

            dk_g, dv_g = [], []
            for g in range(2):
                c0, c1, c2 = 512 * g, 512 * g + 256, 512 * g + 512
                dq = (_dot(kt_lo[g], ds[:, c0:c1]) + _dot(kt_hi[g], ds[:, c1:c2])).T
                for i in range(2):
                    ls = slice((2 * g + i) * LANES, (2 * g + i + 1) * LANES)
                    dsq_ref[rows, ls] = _rope_t(dq[i * SWA_BLOCK:(i + 1) * SWA_BLOCK] * 0.125, cosb, sinb,
                                                first_half).astype(dsq_ref.dtype)
                q_split = jnp.concatenate([jnp.where(lo1s, qg32[g], 0.0), jnp.where(lo1s, 0.0, qg32[g])], axis=0).astype(BF)
                do_split = jnp.concatenate([jnp.where(lo1s, dog32[g], 0.0), jnp.where(lo1s, 0.0, dog32[g])], axis=0).astype(BF)
                dk_g.append(_dot(ds[:, c0:c2], q_split))
                dv_g.append(_dot(prb[:, c0:c2], do_split))
            dk = home(dk_g[0], dk_g[1])
            dv = home(dv_g[0], dv_g[1])
            cur = pl.ds(pl.multiple_of(blk * SWA_BLOCK, SWA_BLOCK), SWA_BLOCK)
            dsk_ref[cur, :] = _rope_t(dk[SWA_BLOCK:], cosb, sinb, first_half)
            dsv_ref[cur, :] = dv[SWA_BLOCK:]
            dk_prev = _rope_t(dk[:SWA_BLOCK], cp_, sp_, first_half)
            dv_prev = dv[:SWA_BLOCK]
            if j == 0:
                @pl.when(n > 0)
                def _():
                    prv = pl.ds(pl.multiple_of((blk - 1) * SWA_BLOCK, SWA_BLOCK), SWA_BLOCK)
                    dsk_ref[prv, :] += dk_prev
                    dsv_ref[prv, :] += dv_prev
            else:
                prv = pl.ds(pl.multiple_of((blk - 1) * SWA_BLOCK, SWA_BLOCK), SWA_BLOCK)
                dsk_ref[prv, :] += dk_prev
                dsv_ref[prv, :] += dv_prev
            kp, vp, cp_, sp_ = kc, vc, cosb, sinb
        kprev[...] = kp
        vprev[...] = vp
        cprev[...] = cp_
        sprev[...] = sp_

    def col(width, off):
        return pl.BlockSpec((tq, width), lambda i: (i, off // width))

    row = pl.BlockSpec((tq, LANES), lambda i: (i, 0))
    wide = pl.BlockSpec((tq, 512), lambda i: (i, 0))
    return pl.pallas_call(
        body, name="swa_bwd", grid=(s // tq,),
        in_specs=[col(512, OFF_SQ), col(512, OFF_SZ), col(LANES, OFF_SK), col(LANES, OFF_SV), wide, wide, row, row,
                  pl.BlockSpec(memory_space=pltpu.SMEM)],
        out_specs=[wide, wide, _full((s, LANES)), _full((s, LANES)), pl.BlockSpec(memory_space=pltpu.SMEM)],
        out_shape=[jax.ShapeDtypeStruct((s, 512), BF), jax.ShapeDtypeStruct((s, 512), BF),
                   jax.ShapeDtypeStruct((s, LANES), F32), jax.ShapeDtypeStruct((s, LANES), F32),
                   jax.ShapeDtypeStruct((1, SWA_HEADS), F32)],
        scratch_shapes=[pltpu.VMEM((SWA_BLOCK, LANES), F32)] * 4,
        compiler_params=_params(("arbitrary",)),
    )(proj, proj, proj, proj, dos, opre, cos, sin, sinks)


def _outproj(og, osw, w_out, x2d, target, gate, g_final):
    s = x2d.shape[0]
    tm = min(512, s)

    def body(og_ref, os_ref, w_ref, x_ref, t_ref, gate_ref, gf_ref,
             dx2_ref, dog_ref, dos_ref, dw_ref, loss_ref, dgf_ref, dgate_ref):
        @pl.when(pl.program_id(0) == 0)
        def _():
            dw_ref[...] = jnp.zeros_like(dw_ref)
            loss_ref[...] = jnp.zeros_like(loss_ref)
            dgf_ref[...] = jnp.zeros_like(dgf_ref)
            dgate_ref[...] = jnp.zeros_like(dgate_ref)

        ogv, osv, w = og_ref[...], os_ref[...], w_ref[...]
        gate, gf = gate_ref[...], gf_ref[...]
        y = _dot(ogv, w[:512]) + _dot(osv, w[512:])
        x2 = x_ref[...] + gate * y
        r = lax.rsqrt(jnp.mean(x2 * x2, axis=-1, keepdims=True) + RMS_EPS)
        xn = x2 * r
        err = xn * gf - t_ref[...]
        loss_ref[...] += 0.5 * jnp.sum(jnp.mean(err * err, axis=-1, keepdims=True), axis=0, keepdims=True)
        dyf = err * (1.0 / D_MODEL)
        dgf_ref[...] += jnp.sum(dyf * xn, axis=0, keepdims=True)
        t = dyf * gf
        dx2 = r * (t - xn * jnp.mean(t * xn, axis=-1, keepdims=True))
        dx2_ref[...] = dx2
        dgate_ref[...] += jnp.sum(dx2 * y, axis=0, keepdims=True)
        dy = (dx2 * gate).astype(BF)
        dmix = _dot(dy, w, NT)
        dog_ref[...] = dmix[:, :512]
        dos_ref[...] = dmix[:, 512:]
        dw_ref[:512, :] += _dot(ogv, dy, TN)
        dw_ref[512:, :] += _dot(osv, dy, TN)

    half = pl.BlockSpec((tm, 512), lambda i: (i, 0))
    rowb = pl.BlockSpec((tm, D_MODEL), lambda i: (i, 0))
    vec = _full((1, D_MODEL))
    return pl.pallas_call(
        body, name="outproj", grid=(s // tm,),
        in_specs=[half, half, _full((D_MODEL, D_MODEL)), rowb, rowb, vec, vec],
        out_specs=[rowb, half, half, _full((D_MODEL, D_MODEL)), _full((1, 1)), vec, vec],
        out_shape=[jax.ShapeDtypeStruct((s, D_MODEL), F32), jax.ShapeDtypeStruct((s, 512), F32),
                   jax.ShapeDtypeStruct((s, 512), F32), jax.ShapeDtypeStruct((D_MODEL, D_MODEL), F32),
                   jax.ShapeDtypeStruct((1, 1), F32), jax.ShapeDtypeStruct((1, D_MODEL), F32),
                   jax.ShapeDtypeStruct((1, D_MODEL), F32)],
        compiler_params=_params(("arbitrary",)),
    )(og, osw, w_out, x2d, target, gate, g_final)


_PIECES = ((OFF_QK, 512), (OFF_V, 512), (OFF_GZ, 512), (OFF_SQ, 512), (OFF_SZ, 512),
           (OFF_SK, LANES), (OFF_SV, LANES), (OFF_GA, LANES))

_UNPAD_ROWS = ((OFF_QK, 0, 1024),
               (OFF_GA, 1024, GLA_RANK),
               (OFF_GZ, 1040, 1024),
               (OFF_SK, 2064, 256),
               (OFF_SZ, 2320, 512))


def _inproj_bwd(x2d, shift, sc1p, g_norm, wpad_t, dx2, pieces):
    s = x2d.shape[0]
    tm = min(512, s)
    nsteps = s // tm

    def body(x_ref, sh_ref, sc_ref, g_ref, w_hbm, dx2_ref, *rest):
        piece_refs = rest[:len(_PIECES)]
        gx_ref, dw_hbm, dsh_ref, dsc_ref, dg_ref, w_vm, dw_vm, sem, out_sems = rest[len(_PIECES):]
        i = pl.program_id(0)

        @pl.when(i == 0)
        def _():
            cp = pltpu.make_async_copy(w_hbm, w_vm, sem)
            cp.start()
            dw_vm[...] = jnp.zeros_like(dw_vm)
            dsh_ref[...] = jnp.zeros_like(dsh_ref)
            dsc_ref[...] = jnp.zeros_like(dsc_ref)
            dg_ref[...] = jnp.zeros_like(dg_ref)
            cp.wait()

        g, sc1p_v = g_ref[...], sc_ref[...]
        xn, r, h = _modnorm(x_ref[...], g, sc1p_v, sh_ref[...])
        hb = h.astype(BF)
        dh = None
        for (off, width), pr in zip(_PIECES, piece_refs):
            dp = pr[...].astype(BF)
            part = _dot(dp, w_vm[off:off + width, :])
            dh = part if dh is None else dh + part
            dw_vm[off:off + width, :] += _dot(dp, hb, TN)
        dsh_ref[...] += jnp.sum(dh, axis=0, keepdims=True)
        dsc_ref[...] += jnp.sum(dh * (xn * g), axis=0, keepdims=True)
        dg_ref[...] += jnp.sum(dh * xn * sc1p_v, axis=0, keepdims=True)
        dxn = dh * g * sc1p_v
        gx_ref[...] = dx2_ref[...] + r * (dxn - xn * jnp.mean(dxn * xn, axis=-1, keepdims=True))

        @pl.when(i == nsteps - 1)
        def _():
            copies = [pltpu.make_async_copy(dw_vm.at[src:src + n], dw_hbm.at[dst:dst + n], out_sems.at[k])
                      for k, (src, dst, n) in enumerate(_UNPAD_ROWS)]
            for cp in copies:
                cp.start()
            for cp in copies:
                cp.wait()

    rowb = pl.BlockSpec((tm, D_MODEL), lambda i: (i, 0))
    vec = _full((1, D_MODEL))
    anyspec = pl.BlockSpec(memory_space=pl.ANY)
    piece_specs = [pl.BlockSpec((tm, width), lambda i: (i, 0)) for _, width in _PIECES]
    return pl.pallas_call(
        body, name="inproj_bwd", grid=(nsteps,),
        in_specs=[rowb, vec, vec, vec, anyspec, rowb] + piece_specs,
        out_specs=[rowb, anyspec, vec, vec, vec],
        out_shape=[jax.ShapeDtypeStruct((s, D_MODEL), F32), jax.ShapeDtypeStruct((D_IN, D_MODEL), F32),
                   jax.ShapeDtypeStruct((1, D_MODEL), F32), jax.ShapeDtypeStruct((1, D_MODEL), F32),
                   jax.ShapeDtypeStruct((1, D_MODEL), F32)],
        scratch_shapes=[pltpu.VMEM((D_PAD, D_MODEL), BF), pltpu.VMEM((D_PAD, D_MODEL), F32), pltpu.SemaphoreType.DMA,
                        pltpu.SemaphoreType.DMA((len(_UNPAD_ROWS),))],
        compiler_params=_params(("arbitrary",)),
    )(x2d, shift, sc1p, g_norm, wpad_t, dx2, *pieces)


def _adam(w, g, m, v):
    m2 = ADAM_B1 * m + (1.0 - ADAM_B1) * g
    v2 = ADAM_B2 * v + (1.0 - ADAM_B2) * (g * g)
    m_hat = m2 / (1.0 - ADAM_B1 ** ADAM_STEP)
    v_hat = v2 / (1.0 - ADAM_B2 ** ADAM_STEP)
    delta = -ADAM_LR * (m_hat / (jnp.sqrt(v_hat) + ADAM_EPS) + ADAM_WD * w)
    return delta, m2, v2


def _adamw(w, g, m, v, name):
    rr, cc = w.shape
    tc = min(256, cc)

    def body(w_ref, g_ref, m_ref, v_ref, d_ref, m2_ref, v2_ref):
        d_ref[...], m2_ref[...], v2_ref[...] = _adam(w_ref[...], g_ref[...], m_ref[...], v_ref[...])

    blk = pl.BlockSpec((rr, tc), lambda i: (0, i))
    return pl.pallas_call(
        body, name=name, grid=(cc // tc,), in_specs=[blk] * 4, out_specs=[blk] * 3,
        out_shape=[jax.ShapeDtypeStruct((rr, cc), F32)] * 3,
        compiler_params=_params(("arbitrary",)),
    )(w, g, m, v)


def _ada_update(c_all, dmod_cols, w, m, v):
    rr, cc = w.shape
    tr = min(256, rr)
    c_all = jnp.pad(c_all, ((0, 8), (0, 0)))
    dmod_cols = jnp.pad(dmod_cols, ((0, 8), (0, 0)))

    def body(c_ref, dm_ref, w_ref, m_ref, v_ref, g_ref, d_ref, m2_ref, v2_ref):
        cv = c_ref[...]
        sc = (cv * _sigmoid(cv)).astype(BF)
        g = _dot(sc, dm_ref[...].astype(BF), TN)
        g_ref[...] = g
        d_ref[...], m2_ref[...], v2_ref[...] = _adam(w_ref[...], g, m_ref[...], v_ref[...])

    blk = pl.BlockSpec((tr, cc), lambda i: (i, 0))
    return pl.pallas_call(
        body, name="ada_update", grid=(rr // tr,),
        in_specs=[pl.BlockSpec((16, tr), lambda i: (0, i)), _full((16, cc)), blk, blk, blk],
        out_specs=[blk] * 4, out_shape=[jax.ShapeDtypeStruct((rr, cc), F32)] * 4,
        compiler_params=_params(("arbitrary",)),
    )(c_all, dmod_cols, w, m, v)


def _small_update(parts, weights, moms, vels):
    n = len(weights)

    def body(*refs):
        p_refs, w_refs, m_refs, v_refs = refs[:n + 1], refs[n + 1:2 * n + 1], refs[2 * n + 1:3 * n + 1], refs[3 * n + 1:4 * n + 1]
        outs = refs[4 * n + 1:]
        for i in range(n):
            g = p_refs[i][0]
            for d in range(1, 8):
                g = g + p_refs[i][d]
            delta, m2, v2 = _adam(w_refs[i][...], g, m_refs[i][...], v_refs[i][...])
            outs[4 * i][...] = g
            outs[4 * i + 1][...] = delta
            outs[4 * i + 2][...] = m2
            outs[4 * i + 3][...] = v2
        tot = p_refs[n][0]
        for d in range(1, 8):
            tot = tot + p_refs[n][d]
        outs[4 * n][...] = tot

    out_shape = []
    for w in weights:
        out_shape += [jax.ShapeDtypeStruct(w.shape, F32)] * 4
    out_shape.append(jax.ShapeDtypeStruct(parts[n].shape[1:], F32))
    return pl.pallas_call(body, name="small_update", out_shape=out_shape, compiler_params=_params())(
        *parts, *weights, *moms, *vels)


def _pad_w_in_t(w):
    pad = jnp.zeros((LANES - GLA_RANK, w.shape[1]), w.dtype)
    return jnp.concatenate([w[dst:dst + n] for _, dst, n in sorted(_UNPAD_ROWS)] + [pad], axis=0)


def _rows8(a):
    flat = a.reshape(-1)
    rows = -(-flat.shape[0] // LANES)
    rows8 = -(-rows // 8) * 8
    flat = jnp.pad(flat, (0, rows8 * LANES - flat.shape[0]))
    return flat.reshape(rows8, LANES)


def kernel(x, c, positions, w_ada, b_ada, g_norm, w_in, w_decay, b_decay, g_gla_head, sinks, w_out, g_final, loss_target, m_w_ada, m_b_ada, m_g_norm, m_w_in, m_w_decay, m_b_decay, m_g_gla_head, m_sinks, m_w_out, m_g_final, v_w_ada, v_b_ada, v_g_norm, v_w_in, v_w_decay, v_b_decay, v_g_gla_head, v_sinks, v_w_out, v_g_final):
    ax, ay, ac = lax.axis_index("x"), lax.axis_index("y"), lax.axis_index("c")
    chip = 2 * ax + ay
    dev = 2 * chip + ac
    s = x.shape[1]
    x2d = x[0]
    target = loss_target[0]
    w_ada2, w_out2, w_dec2 = w_ada[0], w_out[0], w_decay[0]
    w_in_t, m_w_in_t, v_w_in_t = w_in[0].T, m_w_in[0].T, v_w_in[0].T
    ada_cols = w_ada2.shape[1]
    in_cols = w_in_t.shape[0]
    out_rows = w_out2.shape[0]
    half = D_MODEL // 2

    cw = jnp.concatenate([c.reshape(8, LANES), w_dec2.reshape(8, LANES)], axis=0)
    b_shard = lax.dynamic_slice(b_ada, (0, chip * ada_cols), (1, ada_cols))
    half_in = lax.dynamic_slice(w_in_t, (0, ac * half), (in_cols, half)).astype(BF)
    half_out = lax.dynamic_slice(w_out2, (ac * (out_rows // 2), 0), (out_rows // 2, D_MODEL)).astype(BF)
    inv_freq = 1.0 / (ROPE_THETA ** (jnp.arange(0, 64, 2, dtype=F32) / 64))
    first, mod_all, w_in_all, w_out_all, cos, sin = _prologue(
        cw, w_ada2, b_shard, half_in, half_out, positions.reshape(s, 1), jnp.tile(inv_freq, 4).reshape(1, LANES))

    first = first.reshape(8, 2, 8, LANES)
    c_all = first[:, 0].reshape(8, D_MODEL)
    w_dec_full = first[0::2, 1].reshape(4, GLA_RANK, 64).transpose(1, 0, 2).reshape(GLA_RANK, 256)
    mod = mod_all.reshape(4, 2, 8, ada_cols)[:, 0]
    mod = lax.dynamic_slice(mod, (0, dev, 0), (4, 1, ada_cols)).reshape(1, 4 * ada_cols)
    shift, sc1p, gate = mod[:, :D_MODEL], 1.0 + mod[:, D_MODEL:2 * D_MODEL], mod[:, 2 * D_MODEL:]
    w_in_all = w_in_all.reshape(4, 2, in_cols, half)
    wpad_t = _pad_w_in_t(w_in_all.transpose(0, 2, 1, 3).reshape(4 * in_cols, D_MODEL))
    w_out_all = w_out_all.reshape(D_MODEL, D_MODEL)

    wdecp = jnp.pad(w_dec_full, ((0, LANES - GLA_RANK), (0, 0))).astype(BF)

    proj = _inproj_fwd(x2d, shift, sc1p, g_norm, wpad_t)
    og, o_gla, sprev = _gla_fwd(proj, wdecp, b_decay, g_gla_head)
    osw, o_swa = _swa_fwd(proj, cos, sin, sinks)
    dx2, dog, dos, dw_out, loss_p, dgf, dgate = _outproj(og, osw, w_out_all, x2d, target, gate, g_final.reshape(1, D_MODEL))
    dsq, dsz, dsk, dsv, dsinks = _swa_bwd(proj, dos, o_swa, cos, sin, sinks)
    dqk, dv, dgz, dga, dwdp, dbd, dgg = _gla_bwd(proj, dog, o_gla, sprev, wdecp, b_decay, g_gla_head)
    pieces = (dqk, dv, dgz, dsq, dsz, dsk, dsv, dga)
    gx, dw_in_t, dshift, dscale, dgn = _inproj_bwd(x2d, shift, sc1p, g_norm, wpad_t, dx2, pieces)

    segs = [jnp.concatenate([dshift, dscale, dgate], axis=1), dgn, dgf, dwdp[:GLA_RANK], dbd, dgg, dsinks, loss_p]
    packed = [_rows8(a) for a in segs]
    offs = [0]
    for a in packed:
        offs.append(offs[-1] + a.shape[0])
    g_w_in_t, g_w_out, small = _epilogue(dw_in_t.reshape(4, in_cols, D_MODEL), dw_out.reshape(4, out_rows, D_MODEL),
                                         jnp.concatenate(packed, axis=0))

    def seg(i, size):
        return small[:, offs[i]:offs[i + 1]].reshape(8, -1)[:, :size]

    dmod_all = seg(0, 3 * D_MODEL)
    dwd_all = lax.dynamic_slice(seg(3, GLA_RANK * 256).reshape(8, GLA_RANK, 256), (0, 0, chip * 64), (8, GLA_RANK, 64))
    parts = [dmod_all.reshape(8, 1, 3 * D_MODEL), seg(1, D_MODEL).reshape(8, 1, D_MODEL), dwd_all,
             seg(4, 256).reshape(8, 1, 256), seg(5, 512).reshape(8, 1, 512), seg(6, SWA_HEADS).reshape(8, 1, SWA_HEADS),
             seg(2, D_MODEL).reshape(8, 1, D_MODEL), seg(7, LANES).reshape(8, 1, LANES)]
    smalls = _small_update(
        parts,
        [b_ada, g_norm, w_dec2, b_decay, g_gla_head, sinks, g_final.reshape(1, D_MODEL)],
        [m_b_ada, m_g_norm, m_w_decay[0], m_b_decay, m_g_gla_head, m_sinks, m_g_final.reshape(1, D_MODEL)],
        [v_b_ada, v_g_norm, v_w_decay[0], v_b_decay, v_g_gla_head, v_sinks, v_g_final.reshape(1, D_MODEL)])
    (g_b_ada, d_b_ada, nm_b_ada, nv_b_ada, g_gn, d_gn, nm_gn, nv_gn, g_wd, d_wd, nm_wd, nv_wd,
     g_bd, d_bd, nm_bd, nv_bd, g_gg, d_gg, nm_gg, nv_gg, g_sk, d_sk, nm_sk, nv_sk,
     g_gf, d_gf, nm_gf, nv_gf, loss_row) = smalls
    loss = loss_row[0, 0]

    dmod_cols = lax.dynamic_slice(dmod_all, (0, chip * ada_cols), (8, ada_cols))
    g_w_ada, d_w_ada, nm_w_ada, nv_w_ada = _ada_update(c_all, dmod_cols, w_ada2, m_w_ada[0], v_w_ada[0])
    d_w_in_t, nm_w_in_t, nv_w_in_t = _adamw(w_in_t, g_w_in_t, m_w_in_t, v_w_in_t, "adamw_w_in")
    g_w_in, d_w_in, nm_w_in, nv_w_in = g_w_in_t.T, d_w_in_t.T, nm_w_in_t.T, nv_w_in_t.T
    d_w_out, nm_w_out, nv_w_out = _adamw(w_out2, g_w_out, m_w_out[0], v_w_out[0], "adamw_w_out")

    flat = lambda a: a.reshape(D_MODEL)
    grads = [g_w_ada[None], g_b_ada, g_gn, g_w_in[None], g_wd[None], g_bd, g_gg, g_sk, g_w_out[None], flat(g_gf)]
    deltas = [d_w_ada[None], d_b_ada, d_gn, d_w_in[None], d_wd[None], d_bd, d_gg, d_sk, d_w_out[None], flat(d_gf)]
    new_m = [nm_w_ada[None], nm_b_ada, nm_gn, nm_w_in[None], nm_wd[None], nm_bd, nm_gg, nm_sk, nm_w_out[None], flat(nm_gf)]
    new_v = [nv_w_ada[None], nv_b_ada, nv_gn, nv_w_in[None], nv_wd[None], nv_bd, nv_gg, nv_sk, nv_w_out[None], flat(nv_gf)]
    return (loss, gx[None], *grads, *deltas, *new_m, *new_v)
```

```python
import jax
import jax.numpy as jnp
from jax import lax
from jax.experimental import pallas as pl
from jax.experimental.pallas import tpu as pltpu

F32 = jnp.float32
BF = jnp.bfloat16

D_MODEL = 1024
GLA_HEADS = 4
GLA_DK = 64
GLA_CHUNK = 64
GLA_RANK = 16
GLA_TAU = 16.0
GLA_SUB = 256
GLA_ROWS = 512
SWA_HEADS = 8
SWA_BLOCK = 128
SWA_QBLOCKS = 8
RMS_EPS = 1e-6
ROPE_THETA = 10000.0

OFF_QK, OFF_V, OFF_GZ, OFF_SQ, OFF_SZ, OFF_SK, OFF_SV, OFF_GA = 0, 512, 1024, 1536, 2048, 2560, 2688, 2816
D_PAD = 2944
D_IN = 2832
LANES = 128
VMEM_LIMIT = 56 * 1024 * 1024

ADAM_LR, ADAM_B1, ADAM_B2, ADAM_EPS, ADAM_WD, ADAM_STEP = 0.001, 0.9, 0.999, 1e-08, 0.01, 10

NT = (((1,), (1,)), ((), ()))
TN = (((0,), (0,)), ((), ()))
MESH = pl.DeviceIdType.MESH


def _dot(a, b, dims=None):
    if dims is None:
        return jnp.dot(a, b, preferred_element_type=F32)
    return lax.dot_general(a, b, dims, preferred_element_type=F32)


def _sigmoid(x):
    return 1.0 / (1.0 + jnp.exp(-x))


def _params(sem=None):
    return pltpu.CompilerParams(dimension_semantics=sem, vmem_limit_bytes=VMEM_LIMIT)


def _full(shape):
    return pl.BlockSpec(shape, lambda i: (0,) * len(shape))


def _subtiles(rows, size=256):
    size = min(size, rows)
    return [slice(k * size, (k + 1) * size) for k in range(rows // size)]


_GATHER_SEMS = [pltpu.SemaphoreType.DMA((7,)), pltpu.SemaphoreType.DMA((7,)), pltpu.SemaphoreType.DMA]


class _Gather:
    def __init__(self, x_ref, out_ref, send_sems, recv_sems, local_sem):
        x, y, c = lax.axis_index("x"), lax.axis_index("y"), lax.axis_index("c")
        self.me, self.sibling, self.c = (x, y, c), (x, y, 1 - c), c
        self.chips = [(1 - x, y), (x, 1 - y), (1 - x, 1 - y)]
        self.x_ref, self.out_ref, self.send_sems, self.recv_sems = x_ref, out_ref, send_sems, recv_sems
        self.mine = pltpu.make_async_copy(x_ref, self._slab(*self.me), local_sem)

    def _slab(self, px, py, pc):
        return self.out_ref.at[4 * px + 2 * py + pc]

    def _copy(self, k, blk, to, src=None):
        return pltpu.make_async_remote_copy(
            src_ref=self._slab(*blk) if src is None else src, dst_ref=self._slab(*blk),
            send_sem=self.send_sems.at[k], recv_sem=self.recv_sems.at[k], device_id=to, device_id_type=MESH)

    def start(self):
        self.mine.start()
        self.sent = [self._copy(0, self.me, self.sibling, src=self.x_ref)]
        self.sent += [self._copy(1 + j, self.me, (*chip, self.c), src=self.x_ref) for j, chip in enumerate(self.chips)]
        for cp in self.sent:
            cp.start()

    def relay(self):
        for j, chip in enumerate(self.chips):
            self._copy(1 + j, (*chip, self.c), self.me).wait_recv()
            cp = self._copy(4 + j, (*chip, self.c), self.sibling)
            cp.start()
            self.sent.append(cp)

    def finish(self):
        self._copy(0, self.sibling, self.me).wait_recv()
        for j, chip in enumerate(self.chips):
            self._copy(4 + j, (*chip, 1 - self.c), self.me).wait_recv()
        for cp in self.sent:
            cp.wait_send()
        self.mine.wait()


def _prologue(cw, w_ada, b_shard, half_in, half_out, pos_col, inv_freq):
    s = pos_col.shape[0]
    rt = min(512, s)

    def body(cw_ref, wada_ref, b_ref, hin_ref, hout_ref, pos_ref, f_ref,
             first_ref, mod_ref, win_ref, wout_ref, cos_ref, sin_ref, mod_blk, *sems):
        g_c = _Gather(cw_ref, first_ref, *sems[0:3])
        g_in = _Gather(hin_ref, win_ref, *sems[3:6])
        g_out = _Gather(hout_ref, wout_ref, *sems[6:9])
        g_mod = _Gather(mod_blk, mod_ref, *sems[9:12])
        g_c.start()
        g_in.start()
        g_out.start()
        g_c.relay()
        g_c.finish()
        c_rows = [jnp.concatenate([first_ref[d, r:r + 1, :] for r in range(8)], axis=1) for d in range(8)]
        c_all = jnp.concatenate(c_rows, axis=0)
        sc = (c_all * _sigmoid(c_all)).astype(BF)
        mod_blk[...] = _dot(sc, wada_ref[...].astype(BF)) + b_ref[...]
        g_mod.start()

        def rope_rows(i, carry):
            rows = pl.ds(pl.multiple_of(i * rt, rt), rt)
            ang = pos_ref[rows, :].astype(F32) * f_ref[...]
            lane = lax.broadcasted_iota(jnp.int32, ang.shape, 1)
            cos_ref[rows, :] = jnp.cos(ang)
            sn = jnp.sin(ang)
            sin_ref[rows, :] = jnp.where((lane % 64) < 32, -sn, sn)
            return carry

        lax.fori_loop(0, s // rt, rope_rows, 0)
        g_mod.relay()
        g_out.relay()
        g_in.relay()
        g_mod.finish()
        g_out.finish()
        g_in.finish()

    vm = pl.BlockSpec(memory_space=pltpu.VMEM)
    return pl.pallas_call(
        body, name="prologue",
        out_shape=[jax.ShapeDtypeStruct((8,) + cw.shape, F32), jax.ShapeDtypeStruct((8, 8, w_ada.shape[1]), F32),
                   jax.ShapeDtypeStruct((8,) + half_in.shape, half_in.dtype),
                   jax.ShapeDtypeStruct((8,) + half_out.shape, half_out.dtype),
                   jax.ShapeDtypeStruct((s, LANES), F32), jax.ShapeDtypeStruct((s, LANES), F32)],
        in_specs=[vm] * 7, out_specs=[vm] * 6,
        scratch_shapes=[pltpu.VMEM((8, w_ada.shape[1]), F32)] + _GATHER_SEMS * 4,
        compiler_params=pltpu.CompilerParams(vmem_limit_bytes=VMEM_LIMIT),
    )(cw, w_ada, b_shard, half_in, half_out, pos_col, inv_freq)


def _reduce_scratch(rr, cc):
    c2 = cc // 2
    return [pltpu.VMEM((4, rr, c2), F32), pltpu.VMEM((4, rr, c2), F32), pltpu.VMEM((3, rr, c2), BF),
            pltpu.VMEM((3, rr, c2), BF), pltpu.VMEM((rr, c2), F32),
            pltpu.SemaphoreType.DMA((5,)), pltpu.SemaphoreType.DMA((5,)), pltpu.SemaphoreType.DMA((2,))]


class _Reduce:
    def __init__(self, p_hbm, out_ref, acc_ref, own_ref, send_ref, land_ref, res_ref, send_sems, recv_sems, local_sems):
        x, y, c = lax.axis_index("x"), lax.axis_index("y"), lax.axis_index("c")
        c2 = out_ref.shape[1] // 2
        self.c, self.my_chip, sibling = c, 2 * x + y, (x, y, 1 - c)
        self.chips = [(1 - x, y), (x, 1 - y), (1 - x, 1 - y)]
        mine = pl.ds(pl.multiple_of(c * c2, c2), c2)
        other = pl.ds(pl.multiple_of((1 - c) * c2, c2), c2)
        self.acc_ref, self.own_ref, self.send_ref, self.land_ref, self.res_ref = acc_ref, own_ref, send_ref, land_ref, res_ref
        self.send_sems, self.recv_sems = send_sems, recv_sems
        self.own = pltpu.make_async_copy(p_hbm.at[:, :, mine], own_ref, local_sems.at[0])
        self.swap = pltpu.make_async_remote_copy(
            src_ref=p_hbm.at[:, :, other], dst_ref=acc_ref, send_sem=send_sems.at[0], recv_sem=recv_sems.at[0],
            device_id=sibling, device_id_type=MESH)
        self.put = pltpu.make_async_copy(res_ref, out_ref.at[:, mine], local_sems.at[1])
        self.share = pltpu.make_async_remote_copy(
            src_ref=res_ref, dst_ref=out_ref.at[:, mine], send_sem=send_sems.at[4],
            recv_sem=recv_sems.at[4], device_id=sibling, device_id_type=MESH)

    def start(self):
        self.own.start()
        self.swap.start()

    def combine_and_send(self):
        self.own.wait()
        self.swap.wait()
        for j in range(4):
            self.acc_ref[j] = self.acc_ref[j] + self.own_ref[j]
        self.sends = []
        for k, (tx, ty) in enumerate(self.chips):
            self.send_ref[k] = self.acc_ref[2 * tx + ty].astype(self.send_ref.dtype)
            cp = pltpu.make_async_remote_copy(
                src_ref=self.send_ref.at[k], dst_ref=self.land_ref.at[k], send_sem=self.send_sems.at[1 + k],
                recv_sem=self.recv_sems.at[1 + k], device_id=(tx, ty, self.c), device_id_type=MESH)
            cp.start()
            self.sends.append(cp)

    def total_and_share(self):
        for cp in self.sends:
            cp.wait_recv()
        total = self.acc_ref[self.my_chip]
        for k in range(3):
            total = total + self.land_ref[k].astype(F32)
        self.res_ref[...] = total
        for cp in self.sends:
            cp.wait_send()
        self.put.start()
        self.share.start()

    def finish(self):
        self.put.wait()
        self.share.wait()


def _epilogue(dw_in_parts, dw_out_parts, small):
    _, r_in, cc = dw_in_parts.shape
    _, r_out, _ = dw_out_parts.shape
    n_red = len(_reduce_scratch(r_in, cc))

    def body(pin_hbm, pout_hbm, small_ref, gin_ref, gout_ref, small_all_ref, *scratch):
        red_in = _Reduce(pin_hbm, gin_ref, *scratch[0:n_red])
        red_out = _Reduce(pout_hbm, gout_ref, *scratch[n_red:2 * n_red])
        gat = _Gather(small_ref, small_all_ref, *scratch[2 * n_red:])
        red_out.start()
        red_in.start()
        gat.start()
        red_out.combine_and_send()
        red_in.combine_and_send()
        gat.relay()
        red_out.total_and_share()
        red_in.total_and_share()
        gat.finish()
        red_out.finish()
        red_in.finish()

    vm = pl.BlockSpec(memory_space=pltpu.VMEM)
    anyspec = pl.BlockSpec(memory_space=pl.ANY)
    return pl.pallas_call(
        body, name="epilogue",
        out_shape=[jax.ShapeDtypeStruct((r_in, cc), F32), jax.ShapeDtypeStruct((r_out, cc), F32),
                   jax.ShapeDtypeStruct((8,) + small.shape, F32)],
        in_specs=[anyspec, anyspec, vm], out_specs=[vm, vm, vm],
        scratch_shapes=_reduce_scratch(r_in, cc) + _reduce_scratch(r_out, cc) + _GATHER_SEMS,
        compiler_params=pltpu.CompilerParams(vmem_limit_bytes=VMEM_LIMIT),
    )(dw_in_parts, dw_out_parts, small)


def _rope(t, cosb, sinb, first_half):
    partner = jnp.where(first_half, pltpu.roll(t, 96, 1), pltpu.roll(t, 32, 1))
    return t * cosb + partner * sinb


def _rope_t(g, cosb, sinb, first_half):
    gs = g * sinb
    partner = jnp.where(first_half, pltpu.roll(gs, 96, 1), pltpu.roll(gs, 32, 1))
    return g * cosb + partner


def _modnorm(x, g, sc1p, shift):
    r = lax.rsqrt(jnp.mean(x * x, axis=-1, keepdims=True) + RMS_EPS)
    xn = x * r
    return xn, r, (xn * g) * sc1p + shift


def _inproj_fwd(x2d, shift, sc1p, g_norm, wpad_t):
    s = x2d.shape[0]
    tm = min(512, s)

    def body(x_ref, sh_ref, sc_ref, g_ref, w_ref, o_ref):
        subs = _subtiles(tm)
        hs = [_modnorm(x_ref[sl, :], g_ref[...], sc_ref[...], sh_ref[...])[2].astype(BF) for sl in subs]
        for sl, h in zip(subs, hs):
            o_ref[sl, :] = _dot(h, w_ref[...], NT)

    vec = _full((1, D_MODEL))
    return pl.pallas_call(
        body, name="inproj_fwd", grid=(s // tm,),
        in_specs=[pl.BlockSpec((tm, D_MODEL), lambda i: (i, 0)), vec, vec, vec, _full((D_PAD, D_MODEL))],
        out_specs=pl.BlockSpec((tm, D_PAD), lambda i: (i, 0)),
        out_shape=jax.ShapeDtypeStruct((s, D_PAD), F32),
        compiler_params=_params(("arbitrary",)),
    )(x2d, shift, sc1p, g_norm, wpad_t)


def _split3(a):
    hi = a.astype(BF)
    r1 = a - hi.astype(F32)
    mid = r1.astype(BF)
    lo = (r1 - mid.astype(F32)).astype(BF)
    return hi, mid, lo


def _tri_matmul(tri, a):
    hi, mid, lo = _split3(a)
    return _dot(tri, hi) + _dot(tri, mid) + _dot(tri, lo)


def _chunks(tb):
    return [slice(c * GLA_CHUNK, (c + 1) * GLA_CHUNK) for c in range(tb // GLA_CHUNK)]


def _per_chunk_rows(rows, width):
    return jnp.concatenate([jnp.broadcast_to(r, (GLA_CHUNK, width)) for r in rows], axis=0)


def _gla_triangle(tb):
    row = lax.broadcasted_iota(jnp.int32, (tb, tb), 0)
    col = lax.broadcasted_iota(jnp.int32, (tb, tb), 1)
    return (((row // GLA_CHUNK) == (col // GLA_CHUNK)) & (col <= row)).astype(F32)


def _lane_mean(x, ones_b):
    hi = x.astype(BF)
    lo = (x - hi.astype(F32)).astype(BF)
    return (_dot(hi, ones_b) + _dot(lo, ones_b)) * (1.0 / LANES)


def _head(t, h, lo_h):
    blk = t[:, LANES * (h // 2):LANES * (h // 2 + 1)]
    return jnp.where(lo_h, blk, 0.0) if h % 2 == 0 else jnp.where(lo_h, 0.0, blk)


def _gla_block_common(qk, ga, wd, bd, tril_b):
    tb = qk.shape[0]
    q, k = qk[:, :256], qk[:, 256:]
    z = _dot(ga.astype(BF), wd) + bd
    la = (jnp.minimum(z, 0.0) - jnp.log(1.0 + jnp.exp(-jnp.abs(z)))) * (1.0 / GLA_TAU)
    b = _tri_matmul(tril_b, la)
    bls = [b[rs.stop - 1:rs.stop, :] for rs in _chunks(tb)]
    eq = jnp.exp(b)
    ek = jnp.exp(-b)
    f = jnp.exp(_per_chunk_rows(bls, 256) - b)
    return z, eq, ek, f, q * (eq * GLA_DK ** -0.5), k * ek, k * f, bls


def _gla_units(s):
    sub = min(GLA_SUB, s)
    tb = min(GLA_ROWS, s)
    subs = [slice(i * sub, (i + 1) * sub) for i in range(tb // sub)]
    units = [(i, h) for i in range(len(subs)) for h in range(GLA_HEADS)]
    return tb, sub, subs, units


def _gla_fwd(proj, wdecp, bdec, ggla):
    s = proj.shape[0]
    tb, sub, subs, units = _gla_units(s)
    nch = sub // GLA_CHUNK

    def body(qk_ref, v_ref, gz_ref, ga_ref, wd_ref, bd_ref, gg_ref, tri_ref, og_ref, opre_ref, sprev_ref, st_ref):
        @pl.when(pl.program_id(0) == 0)
        def _():
            st_ref[...] = jnp.zeros_like(st_ref)

        lo_h = lax.broadcasted_iota(jnp.int32, (sub, LANES), 1) < GLA_DK
        tril = tri_ref[...] > 0.5
        tril_b = tri_ref[...].astype(BF)
        ones_b = jnp.ones((LANES, LANES), BF)
        gg, wd, bd = gg_ref[...], wd_ref[...], bd_ref[...]
        chunks = _chunks(sub)
        lanes = [slice(h * LANES, (h + 1) * LANES) for h in range(GLA_HEADS)]
        com = [_gla_block_common(qk_ref[sl, :], ga_ref[sl, :], wd, bd, tril_b) for sl in subs]
        decs = [[jnp.exp(bl) for bl in cm[7]] for cm in com]
        a = {(i, h): _head(com[i][4], h, lo_h).astype(BF) for i, h in units}
        bm = {(i, h): _head(com[i][5], h, lo_h).astype(BF) for i, h in units}
        ktl = {(i, h): _head(com[i][6], h, lo_h).astype(BF) for i, h in units}
        vh = {(i, h): v_ref[subs[i], lanes[h]].astype(BF) for i, h in units}
        sc = {u: _dot(a[u], bm[u], NT) for u in units}
        upd = {u: [_dot(vh[u][rs], ktl[u][rs], TN) for rs in chunks] for u in units}
        p = {u: jnp.where(tril, sc[u], 0.0).astype(BF) for u in units}
        o = {u: _dot(p[u], vh[u]) for u in units}
        states = {}
        for h in range(GLA_HEADS):
            st = st_ref[h]
            for i in range(len(subs)):
                entering = []
                for c in range(nch):
                    entering.append(st)
                    sprev_ref[i * nch + c, h] = st
                    st = st * decs[i][c][:, LANES * (h // 2):LANES * (h // 2 + 1)] + upd[(i, h)][c]
                states[(i, h)] = entering
            st_ref[h] = st
        inter = {u: [_dot(a[u][rs], states[u][c].astype(BF), NT) for c, rs in enumerate(chunks)] for u in units}
        o = {u: o[u] + jnp.concatenate(inter[u], axis=0) for u in units}
        ms = {u: _lane_mean(o[u] * o[u], ones_b) for u in units}
        for i, h in units:
            gzh = gz_ref[subs[i], lanes[h]]
            opre_ref[subs[i], lanes[h]] = o[(i, h)]
            og_ref[subs[i], lanes[h]] = (((o[(i, h)] * lax.rsqrt(ms[(i, h)] + RMS_EPS)) * gg[:, lanes[h]])
                                         * (gzh * _sigmoid(gzh))).astype(og_ref.dtype)

    def col(width, off):
        return pl.BlockSpec((tb, width), lambda i: (i, off // width))

    return pl.pallas_call(
        body, name="gla_fwd", grid=(s // tb,),
        in_specs=[col(512, OFF_QK), col(512, OFF_V), col(512, OFF_GZ), col(LANES, OFF_GA),
                  _full((LANES, 256)), _full((1, 256)), _full((1, 512)), _full((sub, sub))],
        out_specs=[pl.BlockSpec((tb, 512), lambda i: (i, 0)), pl.BlockSpec((tb, 512), lambda i: (i, 0)),
                   pl.BlockSpec((tb // GLA_CHUNK, GLA_HEADS, LANES, LANES), lambda i: (i, 0, 0, 0))],
        out_shape=[jax.ShapeDtypeStruct((s, 512), BF), jax.ShapeDtypeStruct((s, 512), F32),
                   jax.ShapeDtypeStruct((s // GLA_CHUNK, GLA_HEADS, LANES, LANES), F32)],
        scratch_shapes=[pltpu.VMEM((GLA_HEADS, LANES, LANES), F32)],
        compiler_params=_params(("arbitrary",)),
    )(proj, proj, proj, proj, wdecp, bdec, ggla, _gla_triangle(sub))


def _gla_bwd(proj, dog, opre, sprev, wdecp, bdec, ggla):
    s = proj.shape[0]
    tb, sub, subs, units = _gla_units(s)
    nsub = len(subs)
    nch = sub // GLA_CHUNK
    nb = s // tb

    def body(qk_ref, v_ref, gz_ref, ga_ref, dog_ref, opre_ref, sprev_ref, wd_ref, bd_ref, gg_ref, tri_ref, triu_ref,
             dqk_ref, dv_ref, dgz_ref, dga_ref, dwd_ref, dbd_ref, dgg_ref, dst_ref):
        @pl.when(pl.program_id(0) == 0)
        def _():
            dst_ref[...] = jnp.zeros_like(dst_ref)
            dwd_ref[...] = jnp.zeros_like(dwd_ref)
            dbd_ref[...] = jnp.zeros_like(dbd_ref)
            dgg_ref[...] = jnp.zeros_like(dgg_ref)

        lo_h = lax.broadcasted_iota(jnp.int32, (sub, LANES), 1) < GLA_DK
        tril = tri_ref[...] > 0.5
        tril_b = tri_ref[...].astype(BF)
        triu_b = triu_ref[...].astype(BF)
        ones_b = jnp.ones((LANES, LANES), BF)
        last_row = (lax.broadcasted_iota(jnp.int32, (sub, LANES), 0) % GLA_CHUNK) == GLA_CHUNK - 1
        wd, gg, bd = wd_ref[...], gg_ref[...], bd_ref[...]
        chunks = _chunks(sub)
        lanes = [slice(h * LANES, (h + 1) * LANES) for h in range(GLA_HEADS)]
        blks = [slice(LANES * (h // 2), LANES * (h // 2 + 1)) for h in range(GLA_HEADS)]
        ga = [ga_ref[sl, :] for sl in subs]
        com = [_gla_block_common(qk_ref[sl, :], ga[i], wd, bd, tril_b) for i, sl in enumerate(subs)]
        decs = [[jnp.exp(bl) for bl in cm[7]] for cm in com]
        a = {(i, h): _head(com[i][4], h, lo_h).astype(BF) for i, h in units}
        bm = {(i, h): _head(com[i][5], h, lo_h).astype(BF) for i, h in units}
        ktl = {(i, h): _head(com[i][6], h, lo_h).astype(BF) for i, h in units}
        vh = {(i, h): v_ref[subs[i], lanes[h]].astype(BF) for i, h in units}
        sc = {u: _dot(a[u], bm[u], NT) for u in units}

        o = {(i, h): opre_ref[subs[i], lanes[h]] for i, h in units}
        ms = {u: _lane_mean(o[u] * o[u], ones_b) for u in units}
        gz = {(i, h): gz_ref[subs[i], lanes[h]] for i, h in units}
        dog = {(i, h): dog_ref[subs[i], lanes[h]] for i, h in units}
        sg = {u: _sigmoid(gz[u]) for u in units}
        r = {u: lax.rsqrt(ms[u] + RMS_EPS) for u in units}
        ohat = {u: o[u] * r[u] for u in units}
        sil = {u: gz[u] * sg[u] for u in units}
        for i, h in units:
            u = (i, h)
            dgz_ref[subs[i], lanes[h]] = (dog[u] * (ohat[u] * gg[:, lanes[h]])
                                          * (sg[u] * (1.0 + gz[u] * (1.0 - sg[u])))).astype(dgz_ref.dtype)
            dgg_ref[:, lanes[h]] += jnp.sum(dog[u] * sil[u] * ohat[u], axis=0, keepdims=True)
        dn = {(i, h): dog[(i, h)] * sil[(i, h)] * gg[:, lanes[h]] for i, h in units}
        mdn = {u: _lane_mean(dn[u] * ohat[u], ones_b) for u in units}
        do = {u: (r[u] * (dn[u] - ohat[u] * mdn[u])).astype(BF) for u in units}

        p = {u: jnp.where(tril, sc[u], 0.0).astype(BF) for u in units}
        dpr = {u: _dot(do[u], vh[u], NT) for u in units}
        incr = {u: [_dot(do[u][rs], a[u][rs], TN) for rs in chunks] for u in units}
        dv = {u: _dot(p[u], do[u], TN) for u in units}
        dp = {u: jnp.where(tril, dpr[u], 0.0).astype(BF) for u in units}
        dqd = {u: _dot(dp[u], bm[u]) for u in units}
        dkd = {u: _dot(dp[u], a[u], TN) for u in units}
        st = {(i, h): [sprev_ref[i * nch + c, h] for c in range(nch)] for i, h in units}
        leaving = {}
        for h in range(GLA_HEADS):
            d = dst_ref[h]
            for i in reversed(range(nsub)):
                out = [None] * nch
                for c in reversed(range(nch)):
                    out[c] = d
                    d = d * decs[i][c][:, blks[h]] + incr[(i, h)][c]
                leaving[(i, h)] = out
            dst_ref[h] = d
        lv_b = {u: [leaving[u][c].astype(BF) for c in range(nch)] for u in units}
        dv_s = {u: [_dot(ktl[u][rs], lv_b[u][c], NT) for c, rs in enumerate(chunks)] for u in units}
        dqd_s = {u: [_dot(do[u][rs], st[u][c].astype(BF)) for c, rs in enumerate(chunks)] for u in units}
        dkt_s = {u: [_dot(vh[u][rs], lv_b[u][c]) for c, rs in enumerate(chunks)] for u in units}
        ddec = {u: [jnp.sum(leaving[u][c] * st[u][c], axis=0, keepdims=True) for c in range(nch)] for u in units}
        for i, h in units:
            dv_ref[subs[i], lanes[h]] = (dv[(i, h)] + jnp.concatenate(dv_s[(i, h)], axis=0)).astype(dv_ref.dtype)
        dqd = {u: dqd[u] + jnp.concatenate(dqd_s[u], axis=0) for u in units}
        dkt = {u: jnp.concatenate(dkt_s[u], axis=0) for u in units}

        db = []
        for i, sl in enumerate(subs):
            _, eq, ek, f, qd, kd, kt, _ = com[i]
            parts = []
            for pair in range(GLA_HEADS // 2):
                blk, u0, u1 = blks[2 * pair], (i, 2 * pair), (i, 2 * pair + 1)
                dqd_b, dkd_b, dkt_b = dqd[u0] + dqd[u1], dkd[u0] + dkd[u1], dkt[u0] + dkt[u1]
                dqk_ref[sl, blk] = (dqd_b * (eq[:, blk] * GLA_DK ** -0.5)).astype(dqk_ref.dtype)
                dqk_ref[sl, 256 + LANES * pair:256 + LANES * (pair + 1)] = (dkd_b * ek[:, blk] + dkt_b * f[:, blk]).astype(dqk_ref.dtype)
                dkt_kt = dkt_b * kt[:, blk]
                dbp = dqd_b * qd[:, blk] - dkd_b * kd[:, blk] - dkt_kt
                dbl = [jnp.sum(dkt_kt[rs], axis=0, keepdims=True) + (ddec[u0][c] + ddec[u1][c]) * decs[i][c][:, blk]
                       for c, rs in enumerate(chunks)]
                parts.append(jnp.where(last_row, dbp + _per_chunk_rows(dbl, LANES), dbp))
            db.append(jnp.concatenate(parts, axis=1))
        dla = [_tri_matmul(triu_b, db[i]) for i in range(nsub)]
        dz32 = [dla[i] * (1.0 / GLA_TAU) * _sigmoid(-com[i][0]) for i in range(nsub)]
        dz = [t.astype(BF) for t in dz32]
        for i, sl in enumerate(subs):
            dga_ref[sl, :] = _dot(dz[i], wd, NT).astype(dga_ref.dtype)
            dwd_ref[...] += _dot(ga[i].astype(BF), dz[i], TN)
            dbd_ref[...] += jnp.sum(dz32[i], axis=0, keepdims=True)

    def col(width, off):
        return pl.BlockSpec((tb, width), lambda i: (nb - 1 - i, off // width))

    def rev(width):
        return pl.BlockSpec((tb, width), lambda i: (nb - 1 - i, 0))

    return pl.pallas_call(
        body, name="gla_bwd", grid=(nb,),
        in_specs=[col(512, OFF_QK), col(512, OFF_V), col(512, OFF_GZ), col(LANES, OFF_GA), rev(512), rev(512),
                  pl.BlockSpec((tb // GLA_CHUNK, GLA_HEADS, LANES, LANES), lambda i: (nb - 1 - i, 0, 0, 0)),
                  _full((LANES, 256)), _full((1, 256)), _full((1, 512)), _full((sub, sub)), _full((sub, sub))],
        out_specs=[rev(512), rev(512), rev(512), rev(LANES), _full((LANES, 256)), _full((1, 256)), _full((1, 512))],
        out_shape=[jax.ShapeDtypeStruct((s, 512), BF), jax.ShapeDtypeStruct((s, 512), BF),
                   jax.ShapeDtypeStruct((s, 512), BF), jax.ShapeDtypeStruct((s, LANES), BF),
                   jax.ShapeDtypeStruct((LANES, 256), F32), jax.ShapeDtypeStruct((1, 256), F32),
                   jax.ShapeDtypeStruct((1, 512), F32)],
        scratch_shapes=[pltpu.VMEM((GLA_HEADS, LANES, LANES), F32)],
        compiler_params=_params(("arbitrary",)),
    )(proj, proj, proj, proj, dog, opre, sprev, wdecp, bdec, ggla, _gla_triangle(sub), _gla_triangle(sub).T)


_SWA_COL_HEADS = (0, 2, 1, 3, 4, 6, 5, 7)
_SWA_COLS = SWA_HEADS * SWA_BLOCK


def _swa_masks():
    lo2 = lax.broadcasted_iota(jnp.int32, (2 * SWA_BLOCK, LANES), 1) < 64
    lane1 = lax.broadcasted_iota(jnp.int32, (SWA_BLOCK, LANES), 1)
    first_half = (lane1 % 64) < 32
    key = lax.broadcasted_iota(jnp.int32, (SWA_BLOCK, _SWA_COLS), 0)
    query = lax.broadcasted_iota(jnp.int32, (SWA_BLOCK, _SWA_COLS), 1) % SWA_BLOCK
    return lo2, lane1 < 64, first_half, key > query


def _merge_band(t, prev_mask, prev_bias=None):
    prev = t[:SWA_BLOCK] if prev_bias is None else t[:SWA_BLOCK] + prev_bias
    return jnp.where(prev_mask, prev, t[SWA_BLOCK:])


def _split_band(t, prev_mask_b):
    prev = t * prev_mask_b
    return jnp.concatenate([prev, t - prev], axis=0)


def _kv_variants(t, lo2):
    tr = pltpu.roll(t, 64, 1)
    lo_v = [jnp.where(lo2, t, 0.0).astype(BF), jnp.where(lo2, tr, 0.0).astype(BF)]
    hi_v = [jnp.where(lo2, 0.0, tr).astype(BF), jnp.where(lo2, 0.0, t).astype(BF)]
    return lo_v, hi_v


def _kv_variants_t(t):
    tt = t.T
    sw = jnp.concatenate([tt[64:], tt[:64]], axis=0)
    top = lax.broadcasted_iota(jnp.int32, tt.shape, 0) < 64
    lo_v = [jnp.where(top, tt, 0.0).astype(BF), jnp.where(top, sw, 0.0).astype(BF)]
    hi_v = [jnp.where(top, 0.0, sw).astype(BF), jnp.where(top, 0.0, tt).astype(BF)]
    return lo_v, hi_v


def _swa_scores(qg, k_lo, k_hi):
    return jnp.concatenate([_dot(k_lo[0], qg[0], NT), _dot(k_hi[0], qg[0], NT),
                            _dot(k_lo[1], qg[1], NT), _dot(k_hi[1], qg[1], NT)], axis=1)


def _sink_row(sinks_ref):
    return jnp.concatenate([jnp.full((1, SWA_BLOCK), sinks_ref[0, hd], F32) for hd in _SWA_COL_HEADS], axis=1)


def _swa_softmax(st, prev_mask, prev_bias, sink):
    st = _merge_band(st, prev_mask, prev_bias)
    m = jnp.maximum(jnp.max(st, axis=0, keepdims=True), sink)
    ex = jnp.exp(st - m)
    es = jnp.exp(sink - m)
    inv = 1.0 / (jnp.sum(ex, axis=0, keepdims=True) + es)
    return ex, es, inv


def _no_prev_bias(block_index):
    return jnp.where(block_index > 0, 0.0, -1e30).astype(F32)


def _swa_queries(sq_ref, rows, cosb, sinb, first_half):
    qs = [_rope(sq_ref[rows, p * LANES:(p + 1) * LANES], cosb, sinb, first_half) * 0.125 for p in range(4)]
    return [jnp.concatenate(qs[0:2], axis=0), jnp.concatenate(qs[2:4], axis=0)]


def _swa_fwd(proj, cos, sin, sinks):
    s = proj.shape[0]
    nq = min(SWA_QBLOCKS, s // SWA_BLOCK)
    tq = nq * SWA_BLOCK

    def body(sq_ref, sz_ref, sk_ref, sv_ref, cos_ref, sin_ref, sinks_ref, os_ref, opre_ref, kprev, vprev):
        n = pl.program_id(0)

        @pl.when(n == 0)
        def _():
            kprev[...] = jnp.zeros_like(kprev)
            vprev[...] = jnp.zeros_like(vprev)

        lo2, _, first_half, prev_mask = _swa_masks()
        prev_mask_b = jnp.where(prev_mask, 1.0, 0.0).astype(BF)
        sink = _sink_row(sinks_ref)
        blocks = range(nq)
        rows = [slice(j * SWA_BLOCK, (j + 1) * SWA_BLOCK) for j in blocks]
        cosb = [cos_ref[rows[j], :] for j in blocks]
        sinb = [sin_ref[rows[j], :] for j in blocks]
        kc = [_rope(sk_ref[rows[j], :], cosb[j], sinb[j], first_half) for j in blocks]
        vc = [sv_ref[rows[j], :] for j in blocks]
        kcat = [jnp.concatenate([kprev[...] if j == 0 else kc[j - 1], kc[j]], axis=0) for j in blocks]
        vcat = [jnp.concatenate([vprev[...] if j == 0 else vc[j - 1], vc[j]], axis=0) for j in blocks]
        kprev[...] = kc[-1]
        vprev[...] = vc[-1]
        kvar = [_kv_variants(kcat[j], lo2) for j in blocks]
        vtvar = [_kv_variants_t(vcat[j]) for j in blocks]
        qg = [[q.astype(BF) for q in _swa_queries(sq_ref, rows[j], cosb[j], sinb[j], first_half)] for j in blocks]
        st = [_swa_scores(qg[j], *kvar[j]) for j in blocks]
        soft = [_swa_softmax(st[j], prev_mask, _no_prev_bias(n) if j == 0 else None, sink) for j in blocks]
        pt = [_split_band(soft[j][0].astype(BF), prev_mask_b) for j in blocks]
        og = {}
        for j in blocks:
            inv = soft[j][2]
            for g in range(2):
                c0, c1, c2 = 512 * g, 512 * g + 256, 512 * g + 512
                ot = (_dot(vtvar[j][0][g], pt[j][:, c0:c1]) * inv[:, c0:c1]
                      + _dot(vtvar[j][1][g], pt[j][:, c1:c2]) * inv[:, c1:c2])
                og[(j, g)] = ot.T
        for j in blocks:
            for g in range(2):
                for i in range(2):
                    ls = slice((2 * g + i) * LANES, (2 * g + i + 1) * LANES)
                    o = og[(j, g)][i * SWA_BLOCK:(i + 1) * SWA_BLOCK]
                    sz = sz_ref[rows[j], ls]
                    opre_ref[rows[j], ls] = o
                    os_ref[rows[j], ls] = (o * (sz * _sigmoid(sz))).astype(os_ref.dtype)

    def col(width, off):
        return pl.BlockSpec((tq, width), lambda i: (i, off // width))

    row = pl.BlockSpec((tq, LANES), lambda i: (i, 0))
    return pl.pallas_call(
        body, name="swa_fwd", grid=(s // tq,),
        in_specs=[col(512, OFF_SQ), col(512, OFF_SZ), col(LANES, OFF_SK), col(LANES, OFF_SV), row, row,
                  pl.BlockSpec(memory_space=pltpu.SMEM)],
        out_specs=[pl.BlockSpec((tq, 512), lambda i: (i, 0))] * 2,
        out_shape=[jax.ShapeDtypeStruct((s, 512), BF), jax.ShapeDtypeStruct((s, 512), F32)],
        scratch_shapes=[pltpu.VMEM((SWA_BLOCK, LANES), F32)] * 2,
        compiler_params=_params(("arbitrary",)),
    )(proj, proj, proj, proj, cos, sin, sinks)


def _swa_bwd(proj, dos, opre, cos, sin, sinks):
    s = proj.shape[0]
    nq = min(SWA_QBLOCKS, s // SWA_BLOCK)
    tq = nq * SWA_BLOCK

    def body(sq_ref, sz_ref, sk_ref, sv_ref, dos_ref, opre_ref, cos_ref, sin_ref, sinks_ref,
             dsq_ref, dsz_ref, dsk_ref, dsv_ref, dsink_ref, kprev, vprev, cprev, sprev):
        n = pl.program_id(0)

        @pl.when(n == 0)
        def _():
            kprev[...] = jnp.zeros_like(kprev)
            vprev[...] = jnp.zeros_like(vprev)
            cprev[...] = jnp.zeros_like(cprev)
            sprev[...] = jnp.zeros_like(sprev)
            for hd in range(SWA_HEADS):
                dsink_ref[0, hd] = 0.0

        lo2, lo1, first_half, prev_mask = _swa_masks()
        prev_mask_b = jnp.where(prev_mask, 1.0, 0.0).astype(BF)
        lo1s = jnp.concatenate([lo1, lo1], axis=0)
        sink = _sink_row(sinks_ref)

        def home(m0, m1):
            t0 = m0 + pltpu.roll(m0, 64, 1)
            t1 = m1 + pltpu.roll(m1, 64, 1)
            return jnp.where(lo2, t0, t1)

        kp, vp, cp_, sp_ = kprev[...], vprev[...], cprev[...], sprev[...]
        for j in range(nq):
            rows = slice(j * SWA_BLOCK, (j + 1) * SWA_BLOCK)
            blk = n * nq + j
            cosb, sinb = cos_ref[rows, :], sin_ref[rows, :]
            kc = _rope(sk_ref[rows, :], cosb, sinb, first_half)
            vc = sv_ref[rows, :]
            kcat = jnp.concatenate([kp, kc], axis=0)
            k_lo, k_hi = _kv_variants(kcat, lo2)
            kt_lo, kt_hi = _kv_variants_t(kcat)
            v_lo, v_hi = _kv_variants(jnp.concatenate([vp, vc], axis=0), lo2)
            qg32 = _swa_queries(sq_ref, rows, cosb, sinb, first_half)
            qg = [q.astype(BF) for q in qg32]
            ex, es, inv = _swa_softmax(_swa_scores(qg, k_lo, k_hi), prev_mask, _no_prev_bias(n) if j == 0 else None, sink)
            pr, ps = ex * inv, es * inv

            dog32 = []
            for g in range(2):
                parts = []
                for i in range(2):
                    ls = slice((2 * g + i) * LANES, (2 * g + i + 1) * LANES)
                    sz = sz_ref[rows, ls]
                    sg = _sigmoid(sz)
                    dos_p = dos_ref[rows, ls]
                    dsz_ref[rows, ls] = (dos_p * opre_ref[rows, ls] * (sg * (1.0 + sz * (1.0 - sg)))).astype(dsz_ref.dtype)
                    parts.append(dos_p * (sz * sg))
                dog32.append(jnp.concatenate(parts, axis=0))
            dog = [t.astype(BF) for t in dog32]
            dpr = _merge_band(jnp.concatenate([_dot(v_lo[0], dog[0], NT), _dot(v_hi[0], dog[0], NT),
                                               _dot(v_lo[1], dog[1], NT), _dot(v_hi[1], dog[1], NT)], axis=1), prev_mask)
            rd = jnp.sum(pr * dpr, axis=0, keepdims=True)
            ds = _split_band((pr * (dpr - rd)).astype(BF), prev_mask_b)
            prb = _split_band(pr.astype(BF), prev_mask_b)
            sink_term = ps * rd
            for r, hd in enumerate(_SWA_COL_HEADS):
                dsink_ref[0, hd] += -jnp.sum(sink_term[:, r * SWA_BLOCK:(r + 1) * SWA_BLOCK])

            dk_g, dv_g = [], []
            for g in range(2):
                c0, c1, c2 = 512 * g, 512 * g + 256, 512 * g + 512
                dq = (_dot(kt_lo[g], ds[:, c0:c1]) + _dot(kt_hi[g], ds[:, c1:c2])).T
                for i in range(2):
                    ls = slice((2 * g + i) * LANES, (2 * g + i + 1) * LANES)
                    dsq_ref[rows, ls] = _rope_t(dq[i * SWA_BLOCK:(i + 1) * SWA_BLOCK] * 0.125, cosb, sinb,
                                                first_half).astype(dsq_ref.dtype)
                q_split = jnp.concatenate([jnp.where(lo1s, qg32[g], 0.0), jnp.where(lo1s, 0.0, qg32[g])], axis=0).astype(BF)
                do_split = jnp.concatenate([jnp.where(lo1s, dog32[g], 0.0), jnp.where(lo1s, 0.0, dog32[g])], axis=0).astype(BF)
                dk_g.append(_dot(ds[:, c0:c2], q_split))
                dv_g.append(_dot(prb[:, c0:c2], do_split))
            dk = home(dk_g[0], dk_g[1])
            dv = home(dv_g[0], dv_g[1])
            cur = pl.ds(pl.multiple_of(blk * SWA_BLOCK, SWA_BLOCK), SWA_BLOCK)
            dsk_ref[cur, :] = _rope_t(dk[SWA_BLOCK:], cosb, sinb, first_half)
            dsv_ref[cur, :] = dv[SWA_BLOCK:]
            dk_prev = _rope_t(dk[:SWA_BLOCK], cp_, sp_, first_half)
            dv_prev = dv[:SWA_BLOCK]
            if j == 0:
                @pl.when(n > 0)
                def _():
                    prv = pl.ds(pl.multiple_of((blk - 1) * SWA_BLOCK, SWA_BLOCK), SWA_BLOCK)
                    dsk_ref[prv, :] += dk_prev
                    dsv_ref[prv, :] += dv_prev
            else:
                prv = pl.ds(pl.multiple_of((blk - 1) * SWA_BLOCK, SWA_BLOCK), SWA_BLOCK)
                dsk_ref[prv, :] += dk_prev
                dsv_ref[prv, :] += dv_prev
            kp, vp, cp_, sp_ = kc, vc, cosb, sinb
        kprev[...] = kp
        vprev[...] = vp
        cprev[...] = cp_
        sprev[...] = sp_

    def col(width, off):
        return pl.BlockSpec((tq, width), lambda i: (i, off // width))

    row = pl.BlockSpec((tq, LANES), lambda i: (i, 0))
    wide = pl.BlockSpec((tq, 512), lambda i: (i, 0))
    return pl.pallas_call(
        body, name="swa_bwd", grid=(s // tq,),
        in_specs=[col(512, OFF_SQ), col(512, OFF_SZ), col(LANES, OFF_SK), col(LANES, OFF_SV), wide, wide, row, row,
                  pl.BlockSpec(memory_space=pltpu.SMEM)],
        out_specs=[wide, wide, _full((s, LANES)), _full((s, LANES)), pl.BlockSpec(memory_space=pltpu.SMEM)],
        out_shape=[jax.ShapeDtypeStruct((s, 512), BF), jax.ShapeDtypeStruct((s, 512), BF),
                   jax.ShapeDtypeStruct((s, LANES), F32), jax.ShapeDtypeStruct((s, LANES), F32),
                   jax.ShapeDtypeStruct((1, SWA_HEADS), F32)],
        scratch_shapes=[pltpu.VMEM((SWA_BLOCK, LANES), F32)] * 4,
        compiler_params=_params(("arbitrary",)),
    )(proj, proj, proj, proj, dos, opre, cos, sin, sinks)


def _outproj(og, osw, w_out, x2d, target, gate, g_final):
    s = x2d.shape[0]
    tm = min(512, s)

    def body(og_ref, os_ref, w_ref, x_ref, t_ref, gate_ref, gf_ref,
             dx2_ref, dog_ref, dos_ref, dw_ref, loss_ref, dgf_ref, dgate_ref):
        @pl.when(pl.program_id(0) == 0)
        def _():
            dw_ref[...] = jnp.zeros_like(dw_ref)
            loss_ref[...] = jnp.zeros_like(loss_ref)
            dgf_ref[...] = jnp.zeros_like(dgf_ref)
            dgate_ref[...] = jnp.zeros_like(dgate_ref)

        w = w_ref[...]
        gate, gf = gate_ref[...], gf_ref[...]
        subs = _subtiles(tm)
        ogv = [og_ref[sl, :] for sl in subs]
        osv = [os_ref[sl, :] for sl in subs]
        y = [_dot(ogv[k], w[:512]) + _dot(osv[k], w[512:]) for k in range(len(subs))]
        dys = []
        for k, sl in enumerate(subs):
            x2 = x_ref[sl, :] + gate * y[k]
            r = lax.rsqrt(jnp.mean(x2 * x2, axis=-1, keepdims=True) + RMS_EPS)
            xn = x2 * r
            err = xn * gf - t_ref[sl, :]
            loss_ref[...] += 0.5 * jnp.sum(jnp.mean(err * err, axis=-1, keepdims=True), axis=0, keepdims=True)
            dyf = err * (1.0 / D_MODEL)
            dgf_ref[...] += jnp.sum(dyf * xn, axis=0, keepdims=True)
            t = dyf * gf
            dx2 = r * (t - xn * jnp.mean(t * xn, axis=-1, keepdims=True))
            dx2_ref[sl, :] = dx2
            dgate_ref[...] += jnp.sum(dx2 * y[k], axis=0, keepdims=True)
            dys.append((dx2 * gate).astype(BF))
            dmix = _dot(dys[k], w, NT)
            dog_ref[sl, :] = dmix[:, :512]
            dos_ref[sl, :] = dmix[:, 512:]
        dy = jnp.concatenate(dys, axis=0)
        dw_ref[:512, :] += _dot(og_ref[...], dy, TN)
        dw_ref[512:, :] += _dot(os_ref[...], dy, TN)

    half = pl.BlockSpec((tm, 512), lambda i: (i, 0))
    rowb = pl.BlockSpec((tm, D_MODEL), lambda i: (i, 0))
    vec = _full((1, D_MODEL))
    return pl.pallas_call(
        body, name="outproj", grid=(s // tm,),
        in_specs=[half, half, _full((D_MODEL, D_MODEL)), rowb, rowb, vec, vec],
        out_specs=[rowb, half, half, _full((D_MODEL, D_MODEL)), _full((1, 1)), vec, vec],
        out_shape=[jax.ShapeDtypeStruct((s, D_MODEL), F32), jax.ShapeDtypeStruct((s, 512), F32),
                   jax.ShapeDtypeStruct((s, 512), F32), jax.ShapeDtypeStruct((D_MODEL, D_MODEL), F32),
                   jax.ShapeDtypeStruct((1, 1), F32), jax.ShapeDtypeStruct((1, D_MODEL), F32),
                   jax.ShapeDtypeStruct((1, D_MODEL), F32)],
        compiler_params=_params(("arbitrary",)),
    )(og, osw, w_out, x2d, target, gate, g_final)


_PIECES = ((OFF_QK, 512), (OFF_V, 512), (OFF_GZ, 512), (OFF_SQ, 512), (OFF_SZ, 512),
           (OFF_SK, LANES), (OFF_SV, LANES), (OFF_GA, LANES))

_UNPAD_ROWS = ((OFF_QK, 0, 1024),
               (OFF_GA, 1024, GLA_RANK),
               (OFF_GZ, 1040, 1024),
               (OFF_SK, 2064, 256),
               (OFF_SZ, 2320, 512))


def _inproj_bwd(x2d, shift, sc1p, g_norm, wpad_t, dx2, pieces):
    s = x2d.shape[0]
    tm = min(512, s)
    nsteps = s // tm

    def body(x_ref, sh_ref, sc_ref, g_ref, w_hbm, dx2_ref, *rest):
        piece_refs = rest[:len(_PIECES)]
        gx_ref, dw_hbm, dsh_ref, dsc_ref, dg_ref, w_vm, dw_vm, sem, out_sems = rest[len(_PIECES):]
        i = pl.program_id(0)

        @pl.when(i == 0)
        def _():
            cp = pltpu.make_async_copy(w_hbm, w_vm, sem)
            cp.start()
            dw_vm[...] = jnp.zeros_like(dw_vm)
            dsh_ref[...] = jnp.zeros_like(dsh_ref)
            dsc_ref[...] = jnp.zeros_like(dsc_ref)
            dg_ref[...] = jnp.zeros_like(dg_ref)
            cp.wait()

        g, sc1p_v, shift_v = g_ref[...], sc_ref[...], sh_ref[...]
        subs = _subtiles(tm)
        norm = [_modnorm(x_ref[sl, :], g, sc1p_v, shift_v) for sl in subs]
        hb = jnp.concatenate([h.astype(BF) for _, _, h in norm], axis=0)
        for (off, width), pr in zip(_PIECES, piece_refs):
            dw_vm[off:off + width, :] += _dot(pr[...].astype(BF), hb, TN)
        for sl, (xn, r, _) in zip(subs, norm):
            dh = None
            for (off, width), pr in zip(_PIECES, piece_refs):
                part = _dot(pr[sl, :].astype(BF), w_vm[off:off + width, :])
                dh = part if dh is None else dh + part
            dsh_ref[...] += jnp.sum(dh, axis=0, keepdims=True)
            dsc_ref[...] += jnp.sum(dh * (xn * g), axis=0, keepdims=True)
            dg_ref[...] += jnp.sum(dh * xn * sc1p_v, axis=0, keepdims=True)
            dxn = dh * g * sc1p_v
            gx_ref[sl, :] = dx2_ref[sl, :] + r * (dxn - xn * jnp.mean(dxn * xn, axis=-1, keepdims=True))

        @pl.when(i == nsteps - 1)
        def _():
            copies = [pltpu.make_async_copy(dw_vm.at[src:src + n], dw_hbm.at[dst:dst + n], out_sems.at[k])
                      for k, (src, dst, n) in enumerate(_UNPAD_ROWS)]
            for cp in copies:
                cp.start()
            for cp in copies:
                cp.wait()

    rowb = pl.BlockSpec((tm, D_MODEL), lambda i: (i, 0))
    vec = _full((1, D_MODEL))
    anyspec = pl.BlockSpec(memory_space=pl.ANY)
    piece_specs = [pl.BlockSpec((tm, width), lambda i: (i, 0)) for _, width in _PIECES]
    return pl.pallas_call(
        body, name="inproj_bwd", grid=(nsteps,),
        in_specs=[rowb, vec, vec, vec, anyspec, rowb] + piece_specs,
        out_specs=[rowb, anyspec, vec, vec, vec],
        out_shape=[jax.ShapeDtypeStruct((s, D_MODEL), F32), jax.ShapeDtypeStruct((D_IN, D_MODEL), F32),
                   jax.ShapeDtypeStruct((1, D_MODEL), F32), jax.ShapeDtypeStruct((1, D_MODEL), F32),
                   jax.ShapeDtypeStruct((1, D_MODEL), F32)],
        scratch_shapes=[pltpu.VMEM((D_PAD, D_MODEL), BF), pltpu.VMEM((D_PAD, D_MODEL), F32), pltpu.SemaphoreType.DMA,
                        pltpu.SemaphoreType.DMA((len(_UNPAD_ROWS),))],
        compiler_params=_params(("arbitrary",)),
    )(x2d, shift, sc1p, g_norm, wpad_t, dx2, *pieces)


def _adam(w, g, m, v):
    m2 = ADAM_B1 * m + (1.0 - ADAM_B1) * g
    v2 = ADAM_B2 * v + (1.0 - ADAM_B2) * (g * g)
    m_hat = m2 / (1.0 - ADAM_B1 ** ADAM_STEP)
    v_hat = v2 / (1.0 - ADAM_B2 ** ADAM_STEP)
    delta = -ADAM_LR * (m_hat / (jnp.sqrt(v_hat) + ADAM_EPS) + ADAM_WD * w)
    return delta, m2, v2


def _adamw(w, g, m, v, name):
    rr, cc = w.shape
    tc = min(256, cc)

    def body(w_ref, g_ref, m_ref, v_ref, d_ref, m2_ref, v2_ref):
        d_ref[...], m2_ref[...], v2_ref[...] = _adam(w_ref[...], g_ref[...], m_ref[...], v_ref[...])

    blk = pl.BlockSpec((rr, tc), lambda i: (0, i))
    return pl.pallas_call(
        body, name=name, grid=(cc // tc,), in_specs=[blk] * 4, out_specs=[blk] * 3,
        out_shape=[jax.ShapeDtypeStruct((rr, cc), F32)] * 3,
        compiler_params=_params(("arbitrary",)),
    )(w, g, m, v)


def _ada_update(c_all, dmod_cols, w, m, v):
    rr, cc = w.shape
    tr = min(256, rr)
    c_all = jnp.pad(c_all, ((0, 8), (0, 0)))
    dmod_cols = jnp.pad(dmod_cols, ((0, 8), (0, 0)))

    def body(c_ref, dm_ref, w_ref, m_ref, v_ref, g_ref, d_ref, m2_ref, v2_ref):
        cv = c_ref[...]
        sc = (cv * _sigmoid(cv)).astype(BF)
        g = _dot(sc, dm_ref[...].astype(BF), TN)
        g_ref[...] = g
        d_ref[...], m2_ref[...], v2_ref[...] = _adam(w_ref[...], g, m_ref[...], v_ref[...])

    blk = pl.BlockSpec((tr, cc), lambda i: (i, 0))
    return pl.pallas_call(
        body, name="ada_update", grid=(rr // tr,),
        in_specs=[pl.BlockSpec((16, tr), lambda i: (0, i)), _full((16, cc)), blk, blk, blk],
        out_specs=[blk] * 4, out_shape=[jax.ShapeDtypeStruct((rr, cc), F32)] * 4,
        compiler_params=_params(("arbitrary",)),
    )(c_all, dmod_cols, w, m, v)


def _small_update(parts, weights, moms, vels):
    n = len(weights)

    def body(*refs):
        p_refs, w_refs, m_refs, v_refs = refs[:n + 1], refs[n + 1:2 * n + 1], refs[2 * n + 1:3 * n + 1], refs[3 * n + 1:4 * n + 1]
        outs = refs[4 * n + 1:]
        for i in range(n):
            g = p_refs[i][0]
            for d in range(1, 8):
                g = g + p_refs[i][d]
            delta, m2, v2 = _adam(w_refs[i][...], g, m_refs[i][...], v_refs[i][...])
            outs[4 * i][...] = g
            outs[4 * i + 1][...] = delta
            outs[4 * i + 2][...] = m2
            outs[4 * i + 3][...] = v2
        tot = p_refs[n][0]
        for d in range(1, 8):
            tot = tot + p_refs[n][d]
        outs[4 * n][...] = tot

    out_shape = []
    for w in weights:
        out_shape += [jax.ShapeDtypeStruct(w.shape, F32)] * 4
    out_shape.append(jax.ShapeDtypeStruct(parts[n].shape[1:], F32))
    return pl.pallas_call(body, name="small_update", out_shape=out_shape, compiler_params=_params())(
        *parts, *weights, *moms, *vels)


def _pad_w_in_t(w):
    pad = jnp.zeros((LANES - GLA_RANK, w.shape[1]), w.dtype)
    return jnp.concatenate([w[dst:dst + n] for _, dst, n in sorted(_UNPAD_ROWS)] + [pad], axis=0)


def _rows8(a):
    flat = a.reshape(-1)
    rows = -(-flat.shape[0] // LANES)
    rows8 = -(-rows // 8) * 8
    flat = jnp.pad(flat, (0, rows8 * LANES - flat.shape[0]))
    return flat.reshape(rows8, LANES)


def kernel(x, c, positions, w_ada, b_ada, g_norm, w_in, w_decay, b_decay, g_gla_head, sinks, w_out, g_final, loss_target, m_w_ada, m_b_ada, m_g_norm, m_w_in, m_w_decay, m_b_decay, m_g_gla_head, m_sinks, m_w_out, m_g_final, v_w_ada, v_b_ada, v_g_norm, v_w_in, v_w_decay, v_b_decay, v_g_gla_head, v_sinks, v_w_out, v_g_final):
    ax, ay, ac = lax.axis_index("x"), lax.axis_index("y"), lax.axis_index("c")
    chip = 2 * ax + ay
    dev = 2 * chip + ac
    s = x.shape[1]
    x2d = x[0]
    target = loss_target[0]
    w_ada2, w_out2, w_dec2 = w_ada[0], w_out[0], w_decay[0]
    w_in_t, m_w_in_t, v_w_in_t = w_in[0].T, m_w_in[0].T, v_w_in[0].T
    ada_cols = w_ada2.shape[1]
    in_cols = w_in_t.shape[0]
    out_rows = w_out2.shape[0]
    half = D_MODEL // 2

    cw = jnp.concatenate([c.reshape(8, LANES), w_dec2.reshape(8, LANES)], axis=0)
    b_shard = lax.dynamic_slice(b_ada, (0, chip * ada_cols), (1, ada_cols))
    half_in = lax.dynamic_slice(w_in_t, (0, ac * half), (in_cols, half)).astype(BF)
    half_out = lax.dynamic_slice(w_out2, (ac * (out_rows // 2), 0), (out_rows // 2, D_MODEL)).astype(BF)
    inv_freq = 1.0 / (ROPE_THETA ** (jnp.arange(0, 64, 2, dtype=F32) / 64))
    first, mod_all, w_in_all, w_out_all, cos, sin = _prologue(
        cw, w_ada2, b_shard, half_in, half_out, positions.reshape(s, 1), jnp.tile(inv_freq, 4).reshape(1, LANES))

    first = first.reshape(8, 2, 8, LANES)
    c_all = first[:, 0].reshape(8, D_MODEL)
    w_dec_full = first[0::2, 1].reshape(4, GLA_RANK, 64).transpose(1, 0, 2).reshape(GLA_RANK, 256)
    mod = mod_all.reshape(4, 2, 8, ada_cols)[:, 0]
    mod = lax.dynamic_slice(mod, (0, dev, 0), (4, 1, ada_cols)).reshape(1, 4 * ada_cols)
    shift, sc1p, gate = mod[:, :D_MODEL], 1.0 + mod[:, D_MODEL:2 * D_MODEL], mod[:, 2 * D_MODEL:]
    w_in_all = w_in_all.reshape(4, 2, in_cols, half)
    wpad_t = _pad_w_in_t(w_in_all.transpose(0, 2, 1, 3).reshape(4 * in_cols, D_MODEL))
    w_out_all = w_out_all.reshape(D_MODEL, D_MODEL)

    wdecp = jnp.pad(w_dec_full, ((0, LANES - GLA_RANK), (0, 0))).astype(BF)

    proj = _inproj_fwd(x2d, shift, sc1p, g_norm, wpad_t)
    og, o_gla, sprev = _gla_fwd(proj, wdecp, b_decay, g_gla_head)
    osw, o_swa = _swa_fwd(proj, cos, sin, sinks)
    dx2, dog, dos, dw_out, loss_p, dgf, dgate = _outproj(og, osw, w_out_all, x2d, target, gate, g_final.reshape(1, D_MODEL))
    dsq, dsz, dsk, dsv, dsinks = _swa_bwd(proj, dos, o_swa, cos, sin, sinks)
    dqk, dv, dgz, dga, dwdp, dbd, dgg = _gla_bwd(proj, dog, o_gla, sprev, wdecp, b_decay, g_gla_head)
    pieces = (dqk, dv, dgz, dsq, dsz, dsk, dsv, dga)
    gx, dw_in_t, dshift, dscale, dgn = _inproj_bwd(x2d, shift, sc1p, g_norm, wpad_t, dx2, pieces)

    segs = [jnp.concatenate([dshift, dscale, dgate], axis=1), dgn, dgf, dwdp[:GLA_RANK], dbd, dgg, dsinks, loss_p]
    packed = [_rows8(a) for a in segs]
    offs = [0]
    for a in packed:
        offs.append(offs[-1] + a.shape[0])
    g_w_in_t, g_w_out, small = _epilogue(dw_in_t.reshape(4, in_cols, D_MODEL), dw_out.reshape(4, out_rows, D_MODEL),
                                         jnp.concatenate(packed, axis=0))

    def seg(i, size):
        return small[:, offs[i]:offs[i + 1]].reshape(8, -1)[:, :size]

    dmod_all = seg(0, 3 * D_MODEL)
    dwd_all = lax.dynamic_slice(seg(3, GLA_RANK * 256).reshape(8, GLA_RANK, 256), (0, 0, chip * 64), (8, GLA_RANK, 64))
    parts = [dmod_all.reshape(8, 1, 3 * D_MODEL), seg(1, D_MODEL).reshape(8, 1, D_MODEL), dwd_all,
             seg(4, 256).reshape(8, 1, 256), seg(5, 512).reshape(8, 1, 512), seg(6, SWA_HEADS).reshape(8, 1, SWA_HEADS),
             seg(2, D_MODEL).reshape(8, 1, D_MODEL), seg(7, LANES).reshape(8, 1, LANES)]
    smalls = _small_update(
        parts,
        [b_ada, g_norm, w_dec2, b_decay, g_gla_head, sinks, g_final.reshape(1, D_MODEL)],
        [m_b_ada, m_g_norm, m_w_decay[0], m_b_decay, m_g_gla_head, m_sinks, m_g_final.reshape(1, D_MODEL)],
        [v_b_ada, v_g_norm, v_w_decay[0], v_b_decay, v_g_gla_head, v_sinks, v_g_final.reshape(1, D_MODEL)])
    (g_b_ada, d_b_ada, nm_b_ada, nv_b_ada, g_gn, d_gn, nm_gn, nv_gn, g_wd, d_wd, nm_wd, nv_wd,
     g_bd, d_bd, nm_bd, nv_bd, g_gg, d_gg, nm_gg, nv_gg, g_sk, d_sk, nm_sk, nv_sk,
     g_gf, d_gf, nm_gf, nv_gf, loss_row) = smalls
    loss = loss_row[0, 0]

    dmod_cols = lax.dynamic_slice(dmod_all, (0, chip * ada_cols), (8, ada_cols))
    g_w_ada, d_w_ada, nm_w_ada, nv_w_ada = _ada_update(c_all, dmod_cols, w_ada2, m_w_ada[0], v_w_ada[0])
    d_w_in_t, nm_w_in_t, nv_w_in_t = _adamw(w_in_t, g_w_in_t, m_w_in_t, v_w_in_t, "adamw_w_in")
    g_w_in, d_w_in, nm_w_in, nv_w_in = g_w_in_t.T, d_w_in_t.T, nm_w_in_t.T, nv_w_in_t.T
    d_w_out, nm_w_out, nv_w_out = _adamw(w_out2, g_w_out, m_w_out[0], v_w_out[0], "adamw_w_out")

    flat = lambda a: a.reshape(D_MODEL)
    grads = [g_w_ada[None], g_b_ada, g_gn, g_w_in[None], g_wd[None], g_bd, g_gg, g_sk, g_w_out[None], flat(g_gf)]
    deltas = [d_w_ada[None], d_b_ada, d_gn, d_w_in[None], d_wd[None], d_bd, d_gg, d_sk, d_w_out[None], flat(d_gf)]
    new_m = [nm_w_ada[None], nm_b_ada, nm_gn, nm_w_in[None], nm_wd[None], nm_bd, nm_gg, nm_sk, nm_w_out[None], flat(nm_gf)]
    new_v = [nv_w_ada[None], nv_b_ada, nv_gn, nv_w_in[None], nv_wd[None], nv_bd, nv_gg, nv_sk, nv_w_out[None], flat(nv_gf)]
    return (loss, gx[None], *grads, *deltas, *new_m, *new_v)
```

```python
import jax
import jax.numpy as jnp
from jax import lax
from jax.experimental import pallas as pl
from jax.experimental.pallas import tpu as pltpu

F32 = jnp.float32
BF = jnp.bfloat16

D_MODEL = 1024
GLA_HEADS = 4
GLA_DK = 64
GLA_CHUNK = 64
GLA_RANK = 16
GLA_TAU = 16.0
GLA_SUB = 256
GLA_ROWS = 512
SWA_HEADS = 8
SWA_BLOCK = 128
SWA_QBLOCKS = 8
RMS_EPS = 1e-6
ROPE_THETA = 10000.0

OFF_QK, OFF_V, OFF_GZ, OFF_SQ, OFF_SZ, OFF_SK, OFF_SV, OFF_GA = 0, 512, 1024, 1536, 2048, 2560, 2688, 2816
D_PAD = 2944
D_IN = 2832
LANES = 128
VMEM_LIMIT = 56 * 1024 * 1024

ADAM_LR, ADAM_B1, ADAM_B2, ADAM_EPS, ADAM_WD, ADAM_STEP = 0.001, 0.9, 0.999, 1e-08, 0.01, 10

NT = (((1,), (1,)), ((), ()))
TN = (((0,), (0,)), ((), ()))
MESH = pl.DeviceIdType.MESH


def _dot(a, b, dims=None):
    if dims is None:
        return jnp.dot(a, b, preferred_element_type=F32)
    return lax.dot_general(a, b, dims, preferred_element_type=F32)


def _sigmoid(x):
    return 1.0 / (1.0 + jnp.exp(-x))


def _params(sem=None):
    return pltpu.CompilerParams(dimension_semantics=sem, vmem_limit_bytes=VMEM_LIMIT)


def _full(shape):
    return pl.BlockSpec(shape, lambda i: (0,) * len(shape))


def _subtiles(rows, size=256):
    size = min(size, rows)
    return [slice(k * size, (k + 1) * size) for k in range(rows // size)]


_GATHER_SEMS = [pltpu.SemaphoreType.DMA((7,)), pltpu.SemaphoreType.DMA((7,)), pltpu.SemaphoreType.DMA]


class _Gather:
    def __init__(self, x_ref, out_ref, send_sems, recv_sems, local_sem):
        x, y, c = lax.axis_index("x"), lax.axis_index("y"), lax.axis_index("c")
        self.me, self.sibling, self.c = (x, y, c), (x, y, 1 - c), c
        self.chips = [(1 - x, y), (x, 1 - y), (1 - x, 1 - y)]
        self.x_ref, self.out_ref, self.send_sems, self.recv_sems = x_ref, out_ref, send_sems, recv_sems
        self.mine = pltpu.make_async_copy(x_ref, self._slab(*self.me), local_sem)

    def _slab(self, px, py, pc):
        return self.out_ref.at[4 * px + 2 * py + pc]

    def _copy(self, k, blk, to, src=None):
        return pltpu.make_async_remote_copy(
            src_ref=self._slab(*blk) if src is None else src, dst_ref=self._slab(*blk),
            send_sem=self.send_sems.at[k], recv_sem=self.recv_sems.at[k], device_id=to, device_id_type=MESH)

    def start(self):
        self.mine.start()
        self.sent = [self._copy(0, self.me, self.sibling, src=self.x_ref)]
        self.sent += [self._copy(1 + j, self.me, (*chip, self.c), src=self.x_ref) for j, chip in enumerate(self.chips)]
        for cp in self.sent:
            cp.start()

    def relay(self):
        for j, chip in enumerate(self.chips):
            self._copy(1 + j, (*chip, self.c), self.me).wait_recv()
            cp = self._copy(4 + j, (*chip, self.c), self.sibling)
            cp.start()
            self.sent.append(cp)

    def finish(self):
        self._copy(0, self.sibling, self.me).wait_recv()
        for j, chip in enumerate(self.chips):
            self._copy(4 + j, (*chip, 1 - self.c), self.me).wait_recv()
        for cp in self.sent:
            cp.wait_send()
        self.mine.wait()


def _prologue(cw, w_ada, b_shard, half_in, half_out, pos_col, inv_freq):
    s = pos_col.shape[0]
    rt = min(512, s)

    def body(cw_ref, wada_ref, b_ref, hin_ref, hout_ref, pos_ref, f_ref,
             first_ref, mod_ref, win_ref, wout_ref, cos_ref, sin_ref, mod_blk, *sems):
        g_c = _Gather(cw_ref, first_ref, *sems[0:3])
        g_in = _Gather(hin_ref, win_ref, *sems[3:6])
        g_out = _Gather(hout_ref, wout_ref, *sems[6:9])
        g_mod = _Gather(mod_blk, mod_ref, *sems[9:12])
        g_c.start()
        g_in.start()
        g_out.start()
        g_c.relay()
        g_c.finish()
        c_rows = [jnp.concatenate([first_ref[d, r:r + 1, :] for r in range(8)], axis=1) for d in range(8)]
        c_all = jnp.concatenate(c_rows, axis=0)
        sc = (c_all * _sigmoid(c_all)).astype(BF)
        mod_blk[...] = _dot(sc, wada_ref[...].astype(BF)) + b_ref[...]
        g_mod.start()

        def rope_rows(i, carry):
            rows = pl.ds(pl.multiple_of(i * rt, rt), rt)
            ang = pos_ref[rows, :].astype(F32) * f_ref[...]
            lane = lax.broadcasted_iota(jnp.int32, ang.shape, 1)
            cos_ref[rows, :] = jnp.cos(ang)
            sn = jnp.sin(ang)
            sin_ref[rows, :] = jnp.where((lane % 64) < 32, -sn, sn)
            return carry

        lax.fori_loop(0, s // rt, rope_rows, 0)
        g_mod.relay()
        g_out.relay()
        g_in.relay()
        g_mod.finish()
        g_out.finish()
        g_in.finish()

    vm = pl.BlockSpec(memory_space=pltpu.VMEM)
    return pl.pallas_call(
        body, name="prologue",
        out_shape=[jax.ShapeDtypeStruct((8,) + cw.shape, F32), jax.ShapeDtypeStruct((8, 8, w_ada.shape[1]), F32),
                   jax.ShapeDtypeStruct((8,) + half_in.shape, half_in.dtype),
                   jax.ShapeDtypeStruct((8,) + half_out.shape, half_out.dtype),
                   jax.ShapeDtypeStruct((s, LANES), F32), jax.ShapeDtypeStruct((s, LANES), F32)],
        in_specs=[vm] * 7, out_specs=[vm] * 6,
        scratch_shapes=[pltpu.VMEM((8, w_ada.shape[1]), F32)] + _GATHER_SEMS * 4,
        compiler_params=pltpu.CompilerParams(vmem_limit_bytes=VMEM_LIMIT),
    )(cw, w_ada, b_shard, half_in, half_out, pos_col, inv_freq)


def _reduce_scratch(rr, cc):
    c2 = cc // 2
    return [pltpu.VMEM((4, rr, c2), F32), pltpu.VMEM((4, rr, c2), F32), pltpu.VMEM((3, rr, c2), BF),
            pltpu.VMEM((3, rr, c2), BF), pltpu.VMEM((rr, c2), F32),
            pltpu.SemaphoreType.DMA((5,)), pltpu.SemaphoreType.DMA((5,)), pltpu.SemaphoreType.DMA((2,))]


class _Reduce:
    def __init__(self, p_hbm, out_ref, acc_ref, own_ref, send_ref, land_ref, res_ref, send_sems, recv_sems, local_sems):
        x, y, c = lax.axis_index("x"), lax.axis_index("y"), lax.axis_index("c")
        c2 = out_ref.shape[1] // 2
        self.c, self.my_chip, sibling = c, 2 * x + y, (x, y, 1 - c)
        self.chips = [(1 - x, y), (x, 1 - y), (1 - x, 1 - y)]
        mine = pl.ds(pl.multiple_of(c * c2, c2), c2)
        other = pl.ds(pl.multiple_of((1 - c) * c2, c2), c2)
        self.acc_ref, self.own_ref, self.send_ref, self.land_ref, self.res_ref = acc_ref, own_ref, send_ref, land_ref, res_ref
        self.send_sems, self.recv_sems = send_sems, recv_sems
        self.own = pltpu.make_async_copy(p_hbm.at[:, :, mine], own_ref, local_sems.at[0])
        self.swap = pltpu.make_async_remote_copy(
            src_ref=p_hbm.at[:, :, other], dst_ref=acc_ref, send_sem=send_sems.at[0], recv_sem=recv_sems.at[0],
            device_id=sibling, device_id_type=MESH)
        self.put = pltpu.make_async_copy(res_ref, out_ref.at[:, mine], local_sems.at[1])
        self.share = pltpu.make_async_remote_copy(
            src_ref=res_ref, dst_ref=out_ref.at[:, mine], send_sem=send_sems.at[4],
            recv_sem=recv_sems.at[4], device_id=sibling, device_id_type=MESH)

    def start(self):
        self.own.start()
        self.swap.start()

    def combine_and_send(self):
        self.own.wait()
        self.swap.wait()
        for j in range(4):
            self.acc_ref[j] = self.acc_ref[j] + self.own_ref[j]
        self.sends = []
        for k, (tx, ty) in enumerate(self.chips):
            self.send_ref[k] = self.acc_ref[2 * tx + ty].astype(self.send_ref.dtype)
            cp = pltpu.make_async_remote_copy(
                src_ref=self.send_ref.at[k], dst_ref=self.land_ref.at[k], send_sem=self.send_sems.at[1 + k],
                recv_sem=self.recv_sems.at[1 + k], device_id=(tx, ty, self.c), device_id_type=MESH)
            cp.start()
            self.sends.append(cp)

    def total_and_share(self):
        for cp in self.sends:
            cp.wait_recv()
        total = self.acc_ref[self.my_chip]
        for k in range(3):
            total = total + self.land_ref[k].astype(F32)
        self.res_ref[...] = total
        for cp in self.sends:
            cp.wait_send()
        self.put.start()
        self.share.start()

    def finish(self):
        self.put.wait()
        self.share.wait()


def _epilogue(dw_in_parts, dw_out_parts, small):
    _, r_in, cc = dw_in_parts.shape
    _, r_out, _ = dw_out_parts.shape
    n_red = len(_reduce_scratch(r_in, cc))

    def body(pin_hbm, pout_hbm, small_ref, gin_ref, gout_ref, small_all_ref, *scratch):
        red_in = _Reduce(pin_hbm, gin_ref, *scratch[0:n_red])
        red_out = _Reduce(pout_hbm, gout_ref, *scratch[n_red:2 * n_red])
        gat = _Gather(small_ref, small_all_ref, *scratch[2 * n_red:])
        red_out.start()
        red_in.start()
        gat.start()
        red_out.combine_and_send()
        red_in.combine_and_send()
        gat.relay()
        red_out.total_and_share()
        red_in.total_and_share()
        gat.finish()
        red_out.finish()
        red_in.finish()

    vm = pl.BlockSpec(memory_space=pltpu.VMEM)
    anyspec = pl.BlockSpec(memory_space=pl.ANY)
    return pl.pallas_call(
        body, name="epilogue",
        out_shape=[jax.ShapeDtypeStruct((r_in, cc), F32), jax.ShapeDtypeStruct((r_out, cc), F32),
                   jax.ShapeDtypeStruct((8,) + small.shape, F32)],
        in_specs=[anyspec, anyspec, vm], out_specs=[vm, vm, vm],
        scratch_shapes=_reduce_scratch(r_in, cc) + _reduce_scratch(r_out, cc) + _GATHER_SEMS,
        compiler_params=pltpu.CompilerParams(vmem_limit_bytes=VMEM_LIMIT),
    )(dw_in_parts, dw_out_parts, small)


def _rope(t, cosb, sinb, first_half):
    partner = jnp.where(first_half, pltpu.roll(t, 96, 1), pltpu.roll(t, 32, 1))
    return t * cosb + partner * sinb


def _rope_t(g, cosb, sinb, first_half):
    gs = g * sinb
    partner = jnp.where(first_half, pltpu.roll(gs, 96, 1), pltpu.roll(gs, 32, 1))
    return g * cosb + partner


def _modnorm(x, g, sc1p, shift):
    r = lax.rsqrt(jnp.mean(x * x, axis=-1, keepdims=True) + RMS_EPS)
    xn = x * r
    return xn, r, (xn * g) * sc1p + shift


def _inproj_fwd(x2d, shift, sc1p, g_norm, wpad_t):
    s = x2d.shape[0]
    tm = min(512, s)

    def body(x_ref, sh_ref, sc_ref, g_ref, w_ref, o_ref):
        subs = _subtiles(tm)
        hs = [_modnorm(x_ref[sl, :], g_ref[...], sc_ref[...], sh_ref[...])[2].astype(BF) for sl in subs]
        for sl, h in zip(subs, hs):
            o_ref[sl, :] = _dot(h, w_ref[...], NT)

    vec = _full((1, D_MODEL))
    return pl.pallas_call(
        body, name="inproj_fwd", grid=(s // tm,),
        in_specs=[pl.BlockSpec((tm, D_MODEL), lambda i: (i, 0)), vec, vec, vec, _full((D_PAD, D_MODEL))],
        out_specs=pl.BlockSpec((tm, D_PAD), lambda i: (i, 0)),
        out_shape=jax.ShapeDtypeStruct((s, D_PAD), F32),
        compiler_params=_params(("arbitrary",)),
    )(x2d, shift, sc1p, g_norm, wpad_t)


def _split3(a):
    hi = a.astype(BF)
    r1 = a - hi.astype(F32)
    mid = r1.astype(BF)
    lo = (r1 - mid.astype(F32)).astype(BF)
    return hi, mid, lo


def _tri_matmul(tri, a):
    hi, mid, lo = _split3(a)
    return _dot(tri, hi) + _dot(tri, mid) + _dot(tri, lo)


def _chunks(tb):
    return [slice(c * GLA_CHUNK, (c + 1) * GLA_CHUNK) for c in range(tb // GLA_CHUNK)]


def _per_chunk_rows(rows, width):
    return jnp.concatenate([jnp.broadcast_to(r, (GLA_CHUNK, width)) for r in rows], axis=0)


def _gla_triangle(tb):
    row = lax.broadcasted_iota(jnp.int32, (tb, tb), 0)
    col = lax.broadcasted_iota(jnp.int32, (tb, tb), 1)
    return (((row // GLA_CHUNK) == (col // GLA_CHUNK)) & (col <= row)).astype(F32)


def _lane_mean(x, ones_b):
    hi = x.astype(BF)
    lo = (x - hi.astype(F32)).astype(BF)
    return (_dot(hi, ones_b) + _dot(lo, ones_b)) * (1.0 / LANES)


def _head(t, h, lo_h):
    blk = t[:, LANES * (h // 2):LANES * (h // 2 + 1)]
    return jnp.where(lo_h, blk, 0.0) if h % 2 == 0 else jnp.where(lo_h, 0.0, blk)


def _gla_block_common(qk, ga, wd, bd, tril_b):
    tb = qk.shape[0]
    q, k = qk[:, :256], qk[:, 256:]
    z = _dot(ga.astype(BF), wd) + bd
    la = (jnp.minimum(z, 0.0) - jnp.log(1.0 + jnp.exp(-jnp.abs(z)))) * (1.0 / GLA_TAU)
    b = _tri_matmul(tril_b, la)
    bls = [b[rs.stop - 1:rs.stop, :] for rs in _chunks(tb)]
    eq = jnp.exp(b)
    ek = jnp.exp(-b)
    f = jnp.exp(_per_chunk_rows(bls, 256) - b)
    return z, eq, ek, f, q * (eq * GLA_DK ** -0.5), k * ek, k * f, bls


def _gla_units(s):
    sub = min(GLA_SUB, s)
    tb = min(GLA_ROWS, s)
    subs = [slice(i * sub, (i + 1) * sub) for i in range(tb // sub)]
    units = [(i, h) for i in range(len(subs)) for h in range(GLA_HEADS)]
    return tb, sub, subs, units


def _gla_fwd(proj, wdecp, bdec, ggla):
    s = proj.shape[0]
    tb, sub, subs, units = _gla_units(s)
    nch = sub // GLA_CHUNK

    def body(qk_ref, v_ref, gz_ref, ga_ref, wd_ref, bd_ref, gg_ref, tri_ref, og_ref, opre_ref, sprev_ref, st_ref):
        @pl.when(pl.program_id(0) == 0)
        def _():
            st_ref[...] = jnp.zeros_like(st_ref)

        lo_h = lax.broadcasted_iota(jnp.int32, (sub, LANES), 1) < GLA_DK
        tril = tri_ref[...] > 0.5
        tril_b = tri_ref[...].astype(BF)
        ones_b = jnp.ones((LANES, LANES), BF)
        gg, wd, bd = gg_ref[...], wd_ref[...], bd_ref[...]
        chunks = _chunks(sub)
        lanes = [slice(h * LANES, (h + 1) * LANES) for h in range(GLA_HEADS)]
        com = [_gla_block_common(qk_ref[sl, :], ga_ref[sl, :], wd, bd, tril_b) for sl in subs]
        decs = [[jnp.exp(bl) for bl in cm[7]] for cm in com]
        a = {(i, h): _head(com[i][4], h, lo_h).astype(BF) for i, h in units}
        bm = {(i, h): _head(com[i][5], h, lo_h).astype(BF) for i, h in units}
        ktl = {(i, h): _head(com[i][6], h, lo_h).astype(BF) for i, h in units}
        vh = {(i, h): v_ref[subs[i], lanes[h]].astype(BF) for i, h in units}
        sc = {u: _dot(a[u], bm[u], NT) for u in units}
        upd = {u: [_dot(vh[u][rs], ktl[u][rs], TN) for rs in chunks] for u in units}
        p = {u: jnp.where(tril, sc[u], 0.0).astype(BF) for u in units}
        o = {u: _dot(p[u], vh[u]) for u in units}
        states = {}
        for h in range(GLA_HEADS):
            st = st_ref[h]
            for i in range(len(subs)):
                entering = []
                for c in range(nch):
                    entering.append(st)
                    sprev_ref[i * nch + c, h] = st
                    st = st * decs[i][c][:, LANES * (h // 2):LANES * (h // 2 + 1)] + upd[(i, h)][c]
                states[(i, h)] = entering
            st_ref[h] = st
        inter = {u: [_dot(a[u][rs], states[u][c].astype(BF), NT) for c, rs in enumerate(chunks)] for u in units}
        o = {u: o[u] + jnp.concatenate(inter[u], axis=0) for u in units}
        ms = {u: _lane_mean(o[u] * o[u], ones_b) for u in units}
        for i, h in units:
            gzh = gz_ref[subs[i], lanes[h]]
            opre_ref[subs[i], lanes[h]] = o[(i, h)]
            og_ref[subs[i], lanes[h]] = (((o[(i, h)] * lax.rsqrt(ms[(i, h)] + RMS_EPS)) * gg[:, lanes[h]])
                                         * (gzh * _sigmoid(gzh))).astype(og_ref.dtype)

    def col(width, off):
        return pl.BlockSpec((tb, width), lambda i: (i, off // width))

    return pl.pallas_call(
        body, name="gla_fwd", grid=(s // tb,),
        in_specs=[col(512, OFF_QK), col(512, OFF_V), col(512, OFF_GZ), col(LANES, OFF_GA),
                  _full((LANES, 256)), _full((1, 256)), _full((1, 512)), _full((sub, sub))],
        out_specs=[pl.BlockSpec((tb, 512), lambda i: (i, 0)), pl.BlockSpec((tb, 512), lambda i: (i, 0)),
                   pl.BlockSpec((tb // GLA_CHUNK, GLA_HEADS, LANES, LANES), lambda i: (i, 0, 0, 0))],
        out_shape=[jax.ShapeDtypeStruct((s, 512), BF), jax.ShapeDtypeStruct((s, 512), F32),
                   jax.ShapeDtypeStruct((s // GLA_CHUNK, GLA_HEADS, LANES, LANES), F32)],
        scratch_shapes=[pltpu.VMEM((GLA_HEADS, LANES, LANES), F32)],
        compiler_params=_params(("arbitrary",)),
    )(proj, proj, proj, proj, wdecp, bdec, ggla, _gla_triangle(sub))


def _gla_bwd(proj, dog, opre, sprev, wdecp, bdec, ggla):
    s = proj.shape[0]
    tb, sub, subs, units = _gla_units(s)
    nsub = len(subs)
    nch = sub // GLA_CHUNK
    nb = s // tb

    def body(qk_ref, v_ref, gz_ref, ga_ref, dog_ref, opre_ref, sprev_ref, wd_ref, bd_ref, gg_ref, tri_ref, triu_ref,
             dqk_ref, dv_ref, dgz_ref, dga_ref, dwd_ref, dbd_ref, dgg_ref, dst_ref):
        @pl.when(pl.program_id(0) == 0)
        def _():
            dst_ref[...] = jnp.zeros_like(dst_ref)
            dwd_ref[...] = jnp.zeros_like(dwd_ref)
            dbd_ref[...] = jnp.zeros_like(dbd_ref)
            dgg_ref[...] = jnp.zeros_like(dgg_ref)

        lo_h = lax.broadcasted_iota(jnp.int32, (sub, LANES), 1) < GLA_DK
        tril = tri_ref[...] > 0.5
        tril_b = tri_ref[...].astype(BF)
        triu_b = triu_ref[...].astype(BF)
        ones_b = jnp.ones((LANES, LANES), BF)
        last_row = (lax.broadcasted_iota(jnp.int32, (sub, LANES), 0) % GLA_CHUNK) == GLA_CHUNK - 1
        wd, gg, bd = wd_ref[...], gg_ref[...], bd_ref[...]
        chunks = _chunks(sub)
        lanes = [slice(h * LANES, (h + 1) * LANES) for h in range(GLA_HEADS)]
        blks = [slice(LANES * (h // 2), LANES * (h // 2 + 1)) for h in range(GLA_HEADS)]
        ga = [ga_ref[sl, :] for sl in subs]
        com = [_gla_block_common(qk_ref[sl, :], ga[i], wd, bd, tril_b) for i, sl in enumerate(subs)]
        decs = [[jnp.exp(bl) for bl in cm[7]] for cm in com]
        a = {(i, h): _head(com[i][4], h, lo_h).astype(BF) for i, h in units}
        bm = {(i, h): _head(com[i][5], h, lo_h).astype(BF) for i, h in units}
        ktl = {(i, h): _head(com[i][6], h, lo_h).astype(BF) for i, h in units}
        vh = {(i, h): v_ref[subs[i], lanes[h]].astype(BF) for i, h in units}
        sc = {u: _dot(a[u], bm[u], NT) for u in units}

        o = {(i, h): opre_ref[subs[i], lanes[h]] for i, h in units}
        ms = {u: _lane_mean(o[u] * o[u], ones_b) for u in units}
        gz = {(i, h): gz_ref[subs[i], lanes[h]] for i, h in units}
        dog = {(i, h): dog_ref[subs[i], lanes[h]] for i, h in units}
        sg = {u: _sigmoid(gz[u]) for u in units}
        r = {u: lax.rsqrt(ms[u] + RMS_EPS) for u in units}
        ohat = {u: o[u] * r[u] for u in units}
        sil = {u: gz[u] * sg[u] for u in units}
        for i, h in units:
            u = (i, h)
            dgz_ref[subs[i], lanes[h]] = (dog[u] * (ohat[u] * gg[:, lanes[h]])
                                          * (sg[u] * (1.0 + gz[u] * (1.0 - sg[u])))).astype(dgz_ref.dtype)
            dgg_ref[:, lanes[h]] += jnp.sum(dog[u] * sil[u] * ohat[u], axis=0, keepdims=True)
        dn = {(i, h): dog[(i, h)] * sil[(i, h)] * gg[:, lanes[h]] for i, h in units}
        mdn = {u: _lane_mean(dn[u] * ohat[u], ones_b) for u in units}
        do = {u: (r[u] * (dn[u] - ohat[u] * mdn[u])).astype(BF) for u in units}

        p = {u: jnp.where(tril, sc[u], 0.0).astype(BF) for u in units}
        dpr = {u: _dot(do[u], vh[u], NT) for u in units}
        incr = {u: [_dot(do[u][rs], a[u][rs], TN) for rs in chunks] for u in units}
        dv = {u: _dot(p[u], do[u], TN) for u in units}
        dp = {u: jnp.where(tril, dpr[u], 0.0).astype(BF) for u in units}
        dqd = {u: _dot(dp[u], bm[u]) for u in units}
        dkd = {u: _dot(dp[u], a[u], TN) for u in units}
        st = {(i, h): [sprev_ref[i * nch + c, h] for c in range(nch)] for i, h in units}
        leaving = {}
        for h in range(GLA_HEADS):
            d = dst_ref[h]
            for i in reversed(range(nsub)):
                out = [None] * nch
                for c in reversed(range(nch)):
                    out[c] = d
                    d = d * decs[i][c][:, blks[h]] + incr[(i, h)][c]
                leaving[(i, h)] = out
            dst_ref[h] = d
        lv_b = {u: [leaving[u][c].astype(BF) for c in range(nch)] for u in units}
        dv_s = {u: [_dot(ktl[u][rs], lv_b[u][c], NT) for c, rs in enumerate(chunks)] for u in units}
        dqd_s = {u: [_dot(do[u][rs], st[u][c].astype(BF)) for c, rs in enumerate(chunks)] for u in units}
        dkt_s = {u: [_dot(vh[u][rs], lv_b[u][c]) for c, rs in enumerate(chunks)] for u in units}
        ddec = {u: [jnp.sum(leaving[u][c] * st[u][c], axis=0, keepdims=True) for c in range(nch)] for u in units}
        for i, h in units:
            dv_ref[subs[i], lanes[h]] = (dv[(i, h)] + jnp.concatenate(dv_s[(i, h)], axis=0)).astype(dv_ref.dtype)
        dqd = {u: dqd[u] + jnp.concatenate(dqd_s[u], axis=0) for u in units}
        dkt = {u: jnp.concatenate(dkt_s[u], axis=0) for u in units}

        db = []
        for i, sl in enumerate(subs):
            _, eq, ek, f, qd, kd, kt, _ = com[i]
            parts = []
            for pair in range(GLA_HEADS // 2):
                blk, u0, u1 = blks[2 * pair], (i, 2 * pair), (i, 2 * pair + 1)
                dqd_b, dkd_b, dkt_b = dqd[u0] + dqd[u1], dkd[u0] + dkd[u1], dkt[u0] + dkt[u1]
                dqk_ref[sl, blk] = (dqd_b * (eq[:, blk] * GLA_DK ** -0.5)).astype(dqk_ref.dtype)
                dqk_ref[sl, 256 + LANES * pair:256 + LANES * (pair + 1)] = (dkd_b * ek[:, blk] + dkt_b * f[:, blk]).astype(dqk_ref.dtype)
                dkt_kt = dkt_b * kt[:, blk]
                dbp = dqd_b * qd[:, blk] - dkd_b * kd[:, blk] - dkt_kt
                dbl = [jnp.sum(dkt_kt[rs], axis=0, keepdims=True) + (ddec[u0][c] + ddec[u1][c]) * decs[i][c][:, blk]
                       for c, rs in enumerate(chunks)]
                parts.append(jnp.where(last_row, dbp + _per_chunk_rows(dbl, LANES), dbp))
            db.append(jnp.concatenate(parts, axis=1))
        dla = [_tri_matmul(triu_b, db[i]) for i in range(nsub)]
        dz32 = [dla[i] * (1.0 / GLA_TAU) * _sigmoid(-com[i][0]) for i in range(nsub)]
        dz = [t.astype(BF) for t in dz32]
        for i, sl in enumerate(subs):
            dga_ref[sl, :] = _dot(dz[i], wd, NT).astype(dga_ref.dtype)
            dwd_ref[...] += _dot(ga[i].astype(BF), dz[i], TN)
            dbd_ref[...] += jnp.sum(dz32[i], axis=0, keepdims=True)

    def col(width, off):
        return pl.BlockSpec((tb, width), lambda i: (nb - 1 - i, off // width))

    def rev(width):
        return pl.BlockSpec((tb, width), lambda i: (nb - 1 - i, 0))

    return pl.pallas_call(
        body, name="gla_bwd", grid=(nb,),
        in_specs=[col(512, OFF_QK), col(512, OFF_V), col(512, OFF_GZ), col(LANES, OFF_GA), rev(512), rev(512),
                  pl.BlockSpec((tb // GLA_CHUNK, GLA_HEADS, LANES, LANES), lambda i: (nb - 1 - i, 0, 0, 0)),
                  _full((LANES, 256)), _full((1, 256)), _full((1, 512)), _full((sub, sub)), _full((sub, sub))],
        out_specs=[rev(512), rev(512), rev(512), rev(LANES), _full((LANES, 256)), _full((1, 256)), _full((1, 512))],
        out_shape=[jax.ShapeDtypeStruct((s, 512), BF), jax.ShapeDtypeStruct((s, 512), BF),
                   jax.ShapeDtypeStruct((s, 512), BF), jax.ShapeDtypeStruct((s, LANES), BF),
                   jax.ShapeDtypeStruct((LANES, 256), F32), jax.ShapeDtypeStruct((1, 256), F32),
                   jax.ShapeDtypeStruct((1, 512), F32)],
        scratch_shapes=[pltpu.VMEM((GLA_HEADS, LANES, LANES), F32)],
        compiler_params=_params(("arbitrary",)),
    )(proj, proj, proj, proj, dog, opre, sprev, wdecp, bdec, ggla, _gla_triangle(sub), _gla_triangle(sub).T)


_SWA_COL_HEADS = (0, 2, 1, 3, 4, 6, 5, 7)
_SWA_COLS = SWA_HEADS * SWA_BLOCK


def _swa_masks():
    lo2 = lax.broadcasted_iota(jnp.int32, (2 * SWA_BLOCK, LANES), 1) < 64
    lane1 = lax.broadcasted_iota(jnp.int32, (SWA_BLOCK, LANES), 1)
    first_half = (lane1 % 64) < 32
    key = lax.broadcasted_iota(jnp.int32, (SWA_BLOCK, _SWA_COLS), 0)
    query = lax.broadcasted_iota(jnp.int32, (SWA_BLOCK, _SWA_COLS), 1) % SWA_BLOCK
    return lo2, lane1 < 64, first_half, key > query


def _merge_band(t, prev_mask, prev_bias=None):
    prev = t[:SWA_BLOCK] if prev_bias is None else t[:SWA_BLOCK] + prev_bias
    return jnp.where(prev_mask, prev, t[SWA_BLOCK:])


def _split_band(t, prev_mask_b):
    prev = t * prev_mask_b
    return jnp.concatenate([prev, t - prev], axis=0)


def _kv_variants(t, lo2):
    tr = pltpu.roll(t, 64, 1)
    lo_v = [jnp.where(lo2, t, 0.0).astype(BF), jnp.where(lo2, tr, 0.0).astype(BF)]
    hi_v = [jnp.where(lo2, 0.0, tr).astype(BF), jnp.where(lo2, 0.0, t).astype(BF)]
    return lo_v, hi_v


def _kv_variants_t(t):
    tt = t.T
    sw = jnp.concatenate([tt[64:], tt[:64]], axis=0)
    top = lax.broadcasted_iota(jnp.int32, tt.shape, 0) < 64
    lo_v = [jnp.where(top, tt, 0.0).astype(BF), jnp.where(top, sw, 0.0).astype(BF)]
    hi_v = [jnp.where(top, 0.0, sw).astype(BF), jnp.where(top, 0.0, tt).astype(BF)]
    return lo_v, hi_v


def _swa_scores(qg, k_lo, k_hi):
    return jnp.concatenate([_dot(k_lo[0], qg[0], NT), _dot(k_hi[0], qg[0], NT),
                            _dot(k_lo[1], qg[1], NT), _dot(k_hi[1], qg[1], NT)], axis=1)


def _sink_row(sinks_ref):
    return jnp.concatenate([jnp.full((1, SWA_BLOCK), sinks_ref[0, hd], F32) for hd in _SWA_COL_HEADS], axis=1)


def _swa_softmax(st, prev_mask, prev_bias, sink):
    st = _merge_band(st, prev_mask, prev_bias)
    m = jnp.maximum(jnp.max(st, axis=0, keepdims=True), sink)
    ex = jnp.exp(st - m)
    es = jnp.exp(sink - m)
    inv = 1.0 / (jnp.sum(ex, axis=0, keepdims=True) + es)
    return ex, es, inv


def _no_prev_bias(block_index):
    return jnp.where(block_index > 0, 0.0, -1e30).astype(F32)


def _swa_queries(sq_ref, rows, cosb, sinb, first_half):
    qs = [_rope(sq_ref[rows, p * LANES:(p + 1) * LANES], cosb, sinb, first_half) * 0.125 for p in range(4)]
    return [jnp.concatenate(qs[0:2], axis=0), jnp.concatenate(qs[2:4], axis=0)]


def _swa_fwd(proj, cos, sin, sinks):
    s = proj.shape[0]
    nq = min(SWA_QBLOCKS, s // SWA_BLOCK)
    tq = nq * SWA_BLOCK

    def body(sq_ref, sz_ref, sk_ref, sv_ref, cos_ref, sin_ref, sinks_ref, os_ref, opre_ref, kprev, vprev):
        n = pl.program_id(0)

        @pl.when(n == 0)
        def _():
            kprev[...] = jnp.zeros_like(kprev)
            vprev[...] = jnp.zeros_like(vprev)

        lo2, _, first_half, prev_mask = _swa_masks()
        prev_mask_b = jnp.where(prev_mask, 1.0, 0.0).astype(BF)
        sink = _sink_row(sinks_ref)
        blocks = range(nq)
        rows = [slice(j * SWA_BLOCK, (j + 1) * SWA_BLOCK) for j in blocks]
        cosb = [cos_ref[rows[j], :] for j in blocks]
        sinb = [sin_ref[rows[j], :] for j in blocks]
        kc = [_rope(sk_ref[rows[j], :], cosb[j], sinb[j], first_half) for j in blocks]
        vc = [sv_ref[rows[j], :] for j in blocks]
        kcat = [jnp.concatenate([kprev[...] if j == 0 else kc[j - 1], kc[j]], axis=0) for j in blocks]
        vcat = [jnp.concatenate([vprev[...] if j == 0 else vc[j - 1], vc[j]], axis=0) for j in blocks]
        kprev[...] = kc[-1]
        vprev[...] = vc[-1]
        kvar = [_kv_variants(kcat[j], lo2) for j in blocks]
        vtvar = [_kv_variants_t(vcat[j]) for j in blocks]
        qg = [[q.astype(BF) for q in _swa_queries(sq_ref, rows[j], cosb[j], sinb[j], first_half)] for j in blocks]
        st = [_swa_scores(qg[j], *kvar[j]) for j in blocks]
        soft = [_swa_softmax(st[j], prev_mask, _no_prev_bias(n) if j == 0 else None, sink) for j in blocks]
        pt = [_split_band(soft[j][0].astype(BF), prev_mask_b) for j in blocks]
        og = {}
        for j in blocks:
            inv = soft[j][2]
            for g in range(2):
                c0, c1, c2 = 512 * g, 512 * g + 256, 512 * g + 512
                ot = (_dot(vtvar[j][0][g], pt[j][:, c0:c1]) * inv[:, c0:c1]
                      + _dot(vtvar[j][1][g], pt[j][:, c1:c2]) * inv[:, c1:c2])
                og[(j, g)] = ot.T
        for j in blocks:
            for g in range(2):
                for i in range(2):
                    ls = slice((2 * g + i) * LANES, (2 * g + i + 1) * LANES)
                    o = og[(j, g)][i * SWA_BLOCK:(i + 1) * SWA_BLOCK]
                    sz = sz_ref[rows[j], ls]
                    opre_ref[rows[j], ls] = o
                    os_ref[rows[j], ls] = (o * (sz * _sigmoid(sz))).astype(os_ref.dtype)

    def col(width, off):
        return pl.BlockSpec((tq, width), lambda i: (i, off // width))

    row = pl.BlockSpec((tq, LANES), lambda i: (i, 0))
    return pl.pallas_call(
        body, name="swa_fwd", grid=(s // tq,),
        in_specs=[col(512, OFF_SQ), col(512, OFF_SZ), col(LANES, OFF_SK), col(LANES, OFF_SV), row, row,
                  pl.BlockSpec(memory_space=pltpu.SMEM)],
        out_specs=[pl.BlockSpec((tq, 512), lambda i: (i, 0))] * 2,
        out_shape=[jax.ShapeDtypeStruct((s, 512), BF), jax.ShapeDtypeStruct((s, 512), F32)],
        scratch_shapes=[pltpu.VMEM((SWA_BLOCK, LANES), F32)] * 2,
        compiler_params=_params(("arbitrary",)),
    )(proj, proj, proj, proj, cos, sin, sinks)


def _swa_bwd(proj, dos, opre, cos, sin, sinks):
    s = proj.shape[0]
    nq = min(SWA_QBLOCKS, s // SWA_BLOCK)
    tq = nq * SWA_BLOCK

    def body(sq_ref, sz_ref, sk_ref, sv_ref, dos_ref, opre_ref, cos_ref, sin_ref, sinks_ref,
             dsq_ref, dsz_ref, dsk_ref, dsv_ref, dsink_ref, kprev, vprev, cprev, sprev):
        n = pl.program_id(0)

        @pl.when(n == 0)
        def _():
            kprev[...] = jnp.zeros_like(kprev)
            vprev[...] = jnp.zeros_like(vprev)
            cprev[...] = jnp.zeros_like(cprev)
            sprev[...] = jnp.zeros_like(sprev)
            for hd in range(SWA_HEADS):
                dsink_ref[0, hd] = 0.0

        lo2, lo1, first_half, prev_mask = _swa_masks()
        prev_mask_b = jnp.where(prev_mask, 1.0, 0.0).astype(BF)
        lo1s = jnp.concatenate([lo1, lo1], axis=0)
        sink = _sink_row(sinks_ref)

        def home(m0, m1):
            t0 = m0 + pltpu.roll(m0, 64, 1)
            t1 = m1 + pltpu.roll(m1, 64, 1)
            return jnp.where(lo2, t0, t1)

        kp, vp, cp_, sp_ = kprev[...], vprev[...], cprev[...], sprev[...]
        for j in range(nq):
            rows = slice(j * SWA_BLOCK, (j + 1) * SWA_BLOCK)
            blk = n * nq + j
            cosb, sinb = cos_ref[rows, :], sin_ref[rows, :]
            kc = _rope(sk_ref[rows, :], cosb, sinb, first_half)
            vc = sv_ref[rows, :]
            kcat = jnp.concatenate([kp, kc], axis=0)
            k_lo, k_hi = _kv_variants(kcat, lo2)
            kt_lo, kt_hi = _kv_variants_t(kcat)
            v_lo, v_hi = _kv_variants(jnp.concatenate([vp, vc], axis=0), lo2)
            qg32 = _swa_queries(sq_ref, rows, cosb, sinb, first_half)
            qg = [q.astype(BF) for q in qg32]
            ex, es, inv = _swa_softmax(_swa_scores(qg, k_lo, k_hi), prev_mask, _no_prev_bias(n) if j == 0 else None, sink)
            pr, ps = ex * inv, es * inv

            dog32 = []
            for g in range(2):
                parts = []
                for i in range(2):
                    ls = slice((2 * g + i) * LANES, (2 * g + i + 1) * LANES)
                    sz = sz_ref[rows, ls]
                    sg = _sigmoid(sz)
                    dos_p = dos_ref[rows, ls]
                    dsz_ref[rows, ls] = (dos_p * opre_ref[rows, ls] * (sg * (1.0 + sz * (1.0 - sg)))).astype(dsz_ref.dtype)
                    parts.append(dos_p * (sz * sg))
                dog32.append(jnp.concatenate(parts, axis=0))
            dog = [t.astype(BF) for t in dog32]
            dpr = _merge_band(jnp.concatenate([_dot(v_lo[0], dog[0], NT), _dot(v_hi[0], dog[0], NT),
                                               _dot(v_lo[1], dog[1], NT), _dot(v_hi[1], dog[1], NT)], axis=1), prev_mask)
            rd = jnp.sum(pr * dpr, axis=0, keepdims=True)
            ds = _split_band((pr * (dpr - rd)).astype(BF), prev_mask_b)
            prb = _split_band(pr.astype(BF), prev_mask_b)
            sink_term = ps * rd
            for r, hd in enumerate(_SWA_COL_HEADS):
                dsink_ref[0, hd] += -jnp.sum(sink_term[:, r * SWA_BLOCK:(r + 1) * SWA_BLOCK])

            dk_g, dv_g = [], []
            for g in range(2):
                c0, c1, c2 = 512 * g, 512 * g + 256, 512 * g + 512
                dq = (_dot(kt_lo[g], ds[:, c0:c1]) + _dot(kt_hi[g], ds[:, c1:c2])).T
                for i in range(2):
                    ls = slice((2 * g + i) * LANES, (2 * g + i + 1) * LANES)
                    dsq_ref[rows, ls] = _rope_t(dq[i * SWA_BLOCK:(i + 1) * SWA_BLOCK] * 0.125, cosb, sinb,
                                                first_half).astype(dsq_ref.dtype)
                q_split = jnp.concatenate([jnp.where(lo1s, qg32[g], 0.0), jnp.where(lo1s, 0.0, qg32[g])], axis=0).astype(BF)
                do_split = jnp.concatenate([jnp.where(lo1s, dog32[g], 0.0), jnp.where(lo1s, 0.0, dog32[g])], axis=0).astype(BF)
                dk_g.append(_dot(ds[:, c0:c2], q_split))
                dv_g.append(_dot(prb[:, c0:c2], do_split))
            dk = home(dk_g[0], dk_g[1])
            dv = home(dv_g[0], dv_g[1])
            cur = pl.ds(pl.multiple_of(blk * SWA_BLOCK, SWA_BLOCK), SWA_BLOCK)
            dsk_ref[cur, :] = _rope_t(dk[SWA_BLOCK:], cosb, sinb, first_half)
            dsv_ref[cur, :] = dv[SWA_BLOCK:]
            dk_prev = _rope_t(dk[:SWA_BLOCK], cp_, sp_, first_half)
            dv_prev = dv[:SWA_BLOCK]
            if j == 0:
                @pl.when(n > 0)
                def _():
                    prv = pl.ds(pl.multiple_of((blk - 1) * SWA_BLOCK, SWA_BLOCK), SWA_BLOCK)
                    dsk_ref[prv, :] += dk_prev
                    dsv_ref[prv, :] += dv_prev
            else:
                prv = pl.ds(pl.multiple_of((blk - 1) * SWA_BLOCK, SWA_BLOCK), SWA_BLOCK)
                dsk_ref[prv, :] += dk_prev
                dsv_ref[prv, :] += dv_prev
            kp, vp, cp_, sp_ = kc, vc, cosb, sinb
        kprev[...] = kp
        vprev[...] = vp
        cprev[...] = cp_
        sprev[...] = sp_

    def col(width, off):
        return pl.BlockSpec((tq, width), lambda i: (i, off // width))

    row = pl.BlockSpec((tq, LANES), lambda i: (i, 0))
    wide = pl.BlockSpec((tq, 512), lambda i: (i, 0))
    return pl.pallas_call(
        body, name="swa_bwd", grid=(s // tq,),
        in_specs=[col(512, OFF_SQ), col(512, OFF_SZ), col(LANES, OFF_SK), col(LANES, OFF_SV), wide, wide, row, row,
                  pl.BlockSpec(memory_space=pltpu.SMEM)],
        out_specs=[wide, wide, _full((s, LANES)), _full((s, LANES)), pl.BlockSpec(memory_space=pltpu.SMEM)],
        out_shape=[jax.ShapeDtypeStruct((s, 512), BF), jax.ShapeDtypeStruct((s, 512), BF),
                   jax.ShapeDtypeStruct((s, LANES), F32), jax.ShapeDtypeStruct((s, LANES), F32),
                   jax.ShapeDtypeStruct((1, SWA_HEADS), F32)],
        scratch_shapes=[pltpu.VMEM((SWA_BLOCK, LANES), F32)] * 4,
        compiler_params=_params(("arbitrary",)),
    )(proj, proj, proj, proj, dos, opre, cos, sin, sinks)


def _outproj(og, osw, w_out, x2d, target, gate, g_final):
    s = x2d.shape[0]
    tm = min(512, s)

    def body(og_ref, os_ref, w_ref, x_ref, t_ref, gate_ref, gf_ref,
             dx2_ref, dog_ref, dos_ref, dw_ref, loss_ref, dgf_ref, dgate_ref):
        @pl.when(pl.program_id(0) == 0)
        def _():
            dw_ref[...] = jnp.zeros_like(dw_ref)
            loss_ref[...] = jnp.zeros_like(loss_ref)
            dgf_ref[...] = jnp.zeros_like(dgf_ref)
            dgate_ref[...] = jnp.zeros_like(dgate_ref)

        w = w_ref[...]
        gate, gf = gate_ref[...], gf_ref[...]
        subs = _subtiles(tm)
        ogv = [og_ref[sl, :] for sl in subs]
        osv = [os_ref[sl, :] for sl in subs]
        y = [_dot(ogv[k], w[:512]) + _dot(osv[k], w[512:]) for k in range(len(subs))]
        dys = []
        for k, sl in enumerate(subs):
            x2 = x_ref[sl, :] + gate * y[k]
            r = lax.rsqrt(jnp.mean(x2 * x2, axis=-1, keepdims=True) + RMS_EPS)
            xn = x2 * r
            err = xn * gf - t_ref[sl, :]
            loss_ref[...] += 0.5 * jnp.sum(jnp.mean(err * err, axis=-1, keepdims=True), axis=0, keepdims=True)
            dyf = err * (1.0 / D_MODEL)
            dgf_ref[...] += jnp.sum(dyf * xn, axis=0, keepdims=True)
            t = dyf * gf
            dx2 = r * (t - xn * jnp.mean(t * xn, axis=-1, keepdims=True))
            dx2_ref[sl, :] = dx2
            dgate_ref[...] += jnp.sum(dx2 * y[k], axis=0, keepdims=True)
            dys.append((dx2 * gate).astype(BF))
            dmix = _dot(dys[k], w, NT)
            dog_ref[sl, :] = dmix[:, :512]
            dos_ref[sl, :] = dmix[:, 512:]
        dy = jnp.concatenate(dys, axis=0)
        dw_ref[:512, :] += _dot(og_ref[...], dy, TN)
        dw_ref[512:, :] += _dot(os_ref[...], dy, TN)

    half = pl.BlockSpec((tm, 512), lambda i: (i, 0))
    rowb = pl.BlockSpec((tm, D_MODEL), lambda i: (i, 0))
    vec = _full((1, D_MODEL))
    return pl.pallas_call(
        body, name="outproj", grid=(s // tm,),
        in_specs=[half, half, _full((D_MODEL, D_MODEL)), rowb, rowb, vec, vec],
        out_specs=[rowb, half, half, _full((D_MODEL, D_MODEL)), _full((1, 1)), vec, vec],
        out_shape=[jax.ShapeDtypeStruct((s, D_MODEL), F32), jax.ShapeDtypeStruct((s, 512), F32),
                   jax.ShapeDtypeStruct((s, 512), F32), jax.ShapeDtypeStruct((D_MODEL, D_MODEL), F32),
                   jax.ShapeDtypeStruct((1, 1), F32), jax.ShapeDtypeStruct((1, D_MODEL), F32),
                   jax.ShapeDtypeStruct((1, D_MODEL), F32)],
        compiler_params=_params(("arbitrary",)),
    )(og, osw, w_out, x2d, target, gate, g_final)


_PIECES = ((OFF_QK, 512), (OFF_V, 512), (OFF_GZ, 512), (OFF_SQ, 512), (OFF_SZ, 512),
           (OFF_SK, LANES), (OFF_SV, LANES), (OFF_GA, LANES))

_UNPAD_ROWS = ((OFF_QK, 0, 1024),
               (OFF_GA, 1024, GLA_RANK),
               (OFF_GZ, 1040, 1024),
               (OFF_SK, 2064, 256),
               (OFF_SZ, 2320, 512))


def _inproj_bwd(x2d, shift, sc1p, g_norm, wpad_t, dx2, pieces):
    s = x2d.shape[0]
    tm = min(512, s)
    nsteps = s // tm

    def body(x_ref, sh_ref, sc_ref, g_ref, w_hbm, dx2_ref, *rest):
        piece_refs = rest[:len(_PIECES)]
        gx_ref, dw_hbm, dsh_ref, dsc_ref, dg_ref, w_vm, dw_vm, sem, out_sems = rest[len(_PIECES):]
        i = pl.program_id(0)

        @pl.when(i == 0)
        def _():
            cp = pltpu.make_async_copy(w_hbm, w_vm, sem)
            cp.start()
            dw_vm[...] = jnp.zeros_like(dw_vm)
            dsh_ref[...] = jnp.zeros_like(dsh_ref)
            dsc_ref[...] = jnp.zeros_like(dsc_ref)
            dg_ref[...] = jnp.zeros_like(dg_ref)
            cp.wait()

        g, sc1p_v, shift_v = g_ref[...], sc_ref[...], sh_ref[...]
        subs = _subtiles(tm)
        dhs = []
        for sl in subs:
            dh = None
            for (off, width), pr in zip(_PIECES, piece_refs):
                part = _dot(pr[sl, :].astype(BF), w_vm[off:off + width, :])
                dh = part if dh is None else dh + part
            dhs.append(dh)
        norm = [_modnorm(x_ref[sl, :], g, sc1p_v, shift_v) for sl in subs]
        hb = jnp.concatenate([h.astype(BF) for _, _, h in norm], axis=0)
        for (off, width), pr in zip(_PIECES, piece_refs):
            dw_vm[off:off + width, :] += _dot(pr[...].astype(BF), hb, TN)
        for sl, (xn, r, _), dh in zip(subs, norm, dhs):
            dsh_ref[...] += jnp.sum(dh, axis=0, keepdims=True)
            dsc_ref[...] += jnp.sum(dh * (xn * g), axis=0, keepdims=True)
            dg_ref[...] += jnp.sum(dh * xn * sc1p_v, axis=0, keepdims=True)
            dxn = dh * g * sc1p_v
            gx_ref[sl, :] = dx2_ref[sl, :] + r * (dxn - xn * jnp.mean(dxn * xn, axis=-1, keepdims=True))

        @pl.when(i == nsteps - 1)
        def _():
            copies = [pltpu.make_async_copy(dw_vm.at[src:src + n], dw_hbm.at[dst:dst + n], out_sems.at[k])
                      for k, (src, dst, n) in enumerate(_UNPAD_ROWS)]
            for cp in copies:
                cp.start()
            for cp in copies:
                cp.wait()

    rowb = pl.BlockSpec((tm, D_MODEL), lambda i: (i, 0))
    vec = _full((1, D_MODEL))
    anyspec = pl.BlockSpec(memory_space=pl.ANY)
    piece_specs = [pl.BlockSpec((tm, width), lambda i: (i, 0)) for _, width in _PIECES]
    return pl.pallas_call(
        body, name="inproj_bwd", grid=(nsteps,),
        in_specs=[rowb, vec, vec, vec, anyspec, rowb] + piece_specs,
        out_specs=[rowb, anyspec, vec, vec, vec],
        out_shape=[jax.ShapeDtypeStruct((s, D_MODEL), F32), jax.ShapeDtypeStruct((D_IN, D_MODEL), F32),
                   jax.ShapeDtypeStruct((1, D_MODEL), F32), jax.ShapeDtypeStruct((1, D_MODEL), F32),
                   jax.ShapeDtypeStruct((1, D_MODEL), F32)],
        scratch_shapes=[pltpu.VMEM((D_PAD, D_MODEL), BF), pltpu.VMEM((D_PAD, D_MODEL), F32), pltpu.SemaphoreType.DMA,
                        pltpu.SemaphoreType.DMA((len(_UNPAD_ROWS),))],
        compiler_params=_params(("arbitrary",)),
    )(x2d, shift, sc1p, g_norm, wpad_t, dx2, *pieces)


def _adam(w, g, m, v):
    m2 = ADAM_B1 * m + (1.0 - ADAM_B1) * g
    v2 = ADAM_B2 * v + (1.0 - ADAM_B2) * (g * g)
    m_hat = m2 / (1.0 - ADAM_B1 ** ADAM_STEP)
    v_hat = v2 / (1.0 - ADAM_B2 ** ADAM_STEP)
    delta = -ADAM_LR * (m_hat / (jnp.sqrt(v_hat) + ADAM_EPS) + ADAM_WD * w)
    return delta, m2, v2


def _adamw(w, g, m, v, name):
    rr, cc = w.shape
    tc = min(256, cc)

    def body(w_ref, g_ref, m_ref, v_ref, d_ref, m2_ref, v2_ref):
        d_ref[...], m2_ref[...], v2_ref[...] = _adam(w_ref[...], g_ref[...], m_ref[...], v_ref[...])

    blk = pl.BlockSpec((rr, tc), lambda i: (0, i))
    return pl.pallas_call(
        body, name=name, grid=(cc // tc,), in_specs=[blk] * 4, out_specs=[blk] * 3,
        out_shape=[jax.ShapeDtypeStruct((rr, cc), F32)] * 3,
        compiler_params=_params(("arbitrary",)),
    )(w, g, m, v)


def _adamw_t(w3, g, m3, v3, name):
    rr, _, cc = w3.shape
    tc = min(256, cc)

    def body(w_ref, g_ref, m_ref, v_ref, d_ref, m2_ref, v2_ref, g3_ref):
        g = g_ref[...]
        d_ref[:, 0, :], m2_ref[:, 0, :], v2_ref[:, 0, :] = _adam(w_ref[:, 0, :], g, m_ref[:, 0, :], v_ref[:, 0, :])
        g3_ref[:, 0, :] = g

    b3 = pl.BlockSpec((rr, 1, tc), lambda i: (0, 0, i))
    return pl.pallas_call(
        body, name=name, grid=(cc // tc,), in_specs=[b3, pl.BlockSpec((rr, tc), lambda i: (0, i)), b3, b3],
        out_specs=[b3] * 4, out_shape=[jax.ShapeDtypeStruct((rr, 1, cc), F32)] * 4,
        compiler_params=_params(("arbitrary",)),
    )(w3, g, m3, v3)


def _ada_update(c_all, dmod_cols, w, m, v):
    rr, cc = w.shape
    tr = min(256, rr)
    c_all = jnp.pad(c_all, ((0, 8), (0, 0)))
    dmod_cols = jnp.pad(dmod_cols, ((0, 8), (0, 0)))

    def body(c_ref, dm_ref, w_ref, m_ref, v_ref, g_ref, d_ref, m2_ref, v2_ref):
        cv = c_ref[...]
        sc = (cv * _sigmoid(cv)).astype(BF)
        g = _dot(sc, dm_ref[...].astype(BF), TN)
        g_ref[...] = g
        d_ref[...], m2_ref[...], v2_ref[...] = _adam(w_ref[...], g, m_ref[...], v_ref[...])

    blk = pl.BlockSpec((tr, cc), lambda i: (i, 0))
    return pl.pallas_call(
        body, name="ada_update", grid=(rr // tr,),
        in_specs=[pl.BlockSpec((16, tr), lambda i: (0, i)), _full((16, cc)), blk, blk, blk],
        out_specs=[blk] * 4, out_shape=[jax.ShapeDtypeStruct((rr, cc), F32)] * 4,
        compiler_params=_params(("arbitrary",)),
    )(c_all, dmod_cols, w, m, v)


def _small_update(parts, weights, moms, vels):
    n = len(weights)

    def body(*refs):
        p_refs, w_refs, m_refs, v_refs = refs[:n + 1], refs[n + 1:2 * n + 1], refs[2 * n + 1:3 * n + 1], refs[3 * n + 1:4 * n + 1]
        outs = refs[4 * n + 1:]
        for i in range(n):
            g = p_refs[i][0]
            for d in range(1, 8):
                g = g + p_refs[i][d]
            delta, m2, v2 = _adam(w_refs[i][...], g, m_refs[i][...], v_refs[i][...])
            outs[4 * i][...] = g
            outs[4 * i + 1][...] = delta
            outs[4 * i + 2][...] = m2
            outs[4 * i + 3][...] = v2
        tot = p_refs[n][0]
        for d in range(1, 8):
            tot = tot + p_refs[n][d]
        outs[4 * n][...] = tot

    out_shape = []
    for w in weights:
        out_shape += [jax.ShapeDtypeStruct(w.shape, F32)] * 4
    out_shape.append(jax.ShapeDtypeStruct(parts[n].shape[1:], F32))
    return pl.pallas_call(body, name="small_update", out_shape=out_shape, compiler_params=_params())(
        *parts, *weights, *moms, *vels)


def _pad_w_in_t(w):
    pad = jnp.zeros((LANES - GLA_RANK, w.shape[1]), w.dtype)
    return jnp.concatenate([w[dst:dst + n] for _, dst, n in sorted(_UNPAD_ROWS)] + [pad], axis=0)


def _rows8(a):
    flat = a.reshape(-1)
    rows = -(-flat.shape[0] // LANES)
    rows8 = -(-rows // 8) * 8
    flat = jnp.pad(flat, (0, rows8 * LANES - flat.shape[0]))
    return flat.reshape(rows8, LANES)


def kernel(x, c, positions, w_ada, b_ada, g_norm, w_in, w_decay, b_decay, g_gla_head, sinks, w_out, g_final, loss_target, m_w_ada, m_b_ada, m_g_norm, m_w_in, m_w_decay, m_b_decay, m_g_gla_head, m_sinks, m_w_out, m_g_final, v_w_ada, v_b_ada, v_g_norm, v_w_in, v_w_decay, v_b_decay, v_g_gla_head, v_sinks, v_w_out, v_g_final):
    ax, ay, ac = lax.axis_index("x"), lax.axis_index("y"), lax.axis_index("c")
    chip = 2 * ax + ay
    dev = 2 * chip + ac
    s = x.shape[1]
    x2d = x[0]
    target = loss_target[0]
    w_ada2, w_out2, w_dec2 = w_ada[0], w_out[0], w_decay[0]
    w_in_t = w_in[0].T
    ada_cols = w_ada2.shape[1]
    in_cols = w_in_t.shape[0]
    out_rows = w_out2.shape[0]
    half = D_MODEL // 2

    cw = jnp.concatenate([c.reshape(8, LANES), w_dec2.reshape(8, LANES)], axis=0)
    b_shard = lax.dynamic_slice(b_ada, (0, chip * ada_cols), (1, ada_cols))
    half_in = lax.dynamic_slice(w_in_t, (0, ac * half), (in_cols, half)).astype(BF)
    half_out = lax.dynamic_slice(w_out2, (ac * (out_rows // 2), 0), (out_rows // 2, D_MODEL)).astype(BF)
    inv_freq = 1.0 / (ROPE_THETA ** (jnp.arange(0, 64, 2, dtype=F32) / 64))
    first, mod_all, w_in_all, w_out_all, cos, sin = _prologue(
        cw, w_ada2, b_shard, half_in, half_out, positions.reshape(s, 1), jnp.tile(inv_freq, 4).reshape(1, LANES))

    first = first.reshape(8, 2, 8, LANES)
    c_all = first[:, 0].reshape(8, D_MODEL)
    w_dec_full = first[0::2, 1].reshape(4, GLA_RANK, 64).transpose(1, 0, 2).reshape(GLA_RANK, 256)
    mod = mod_all.reshape(4, 2, 8, ada_cols)[:, 0]
    mod = lax.dynamic_slice(mod, (0, dev, 0), (4, 1, ada_cols)).reshape(1, 4 * ada_cols)
    shift, sc1p, gate = mod[:, :D_MODEL], 1.0 + mod[:, D_MODEL:2 * D_MODEL], mod[:, 2 * D_MODEL:]
    w_in_all = w_in_all.reshape(4, 2, in_cols, half)
    wpad_t = _pad_w_in_t(w_in_all.transpose(0, 2, 1, 3).reshape(4 * in_cols, D_MODEL))
    w_out_all = w_out_all.reshape(D_MODEL, D_MODEL)

    wdecp = jnp.pad(w_dec_full, ((0, LANES - GLA_RANK), (0, 0))).astype(BF)

    proj = _inproj_fwd(x2d, shift, sc1p, g_norm, wpad_t)
    og, o_gla, sprev = _gla_fwd(proj, wdecp, b_decay, g_gla_head)
    osw, o_swa = _swa_fwd(proj, cos, sin, sinks)
    dx2, dog, dos, dw_out, loss_p, dgf, dgate = _outproj(og, osw, w_out_all, x2d, target, gate, g_final.reshape(1, D_MODEL))
    dsq, dsz, dsk, dsv, dsinks = _swa_bwd(proj, dos, o_swa, cos, sin, sinks)
    dqk, dv, dgz, dga, dwdp, dbd, dgg = _gla_bwd(proj, dog, o_gla, sprev, wdecp, b_decay, g_gla_head)
    pieces = (dqk, dv, dgz, dsq, dsz, dsk, dsv, dga)
    gx, dw_in_t, dshift, dscale, dgn = _inproj_bwd(x2d, shift, sc1p, g_norm, wpad_t, dx2, pieces)

    segs = [jnp.concatenate([dshift, dscale, dgate], axis=1), dgn, dgf, dwdp[:GLA_RANK], dbd, dgg, dsinks, loss_p]
    packed = [_rows8(a) for a in segs]
    offs = [0]
    for a in packed:
        offs.append(offs[-1] + a.shape[0])
    g_w_in_t, g_w_out, small = _epilogue(dw_in_t.reshape(4, in_cols, D_MODEL), dw_out.reshape(4, out_rows, D_MODEL),
                                         jnp.concatenate(packed, axis=0))

    def seg(i, size):
        return small[:, offs[i]:offs[i + 1]].reshape(8, -1)[:, :size]

    dmod_all = seg(0, 3 * D_MODEL)
    dwd_all = lax.dynamic_slice(seg(3, GLA_RANK * 256).reshape(8, GLA_RANK, 256), (0, 0, chip * 64), (8, GLA_RANK, 64))
    parts = [dmod_all.reshape(8, 1, 3 * D_MODEL), seg(1, D_MODEL).reshape(8, 1, D_MODEL), dwd_all,
             seg(4, 256).reshape(8, 1, 256), seg(5, 512).reshape(8, 1, 512), seg(6, SWA_HEADS).reshape(8, 1, SWA_HEADS),
             seg(2, D_MODEL).reshape(8, 1, D_MODEL), seg(7, LANES).reshape(8, 1, LANES)]
    smalls = _small_update(
        parts,
        [b_ada, g_norm, w_dec2, b_decay, g_gla_head, sinks, g_final.reshape(1, D_MODEL)],
        [m_b_ada, m_g_norm, m_w_decay[0], m_b_decay, m_g_gla_head, m_sinks, m_g_final.reshape(1, D_MODEL)],
        [v_b_ada, v_g_norm, v_w_decay[0], v_b_decay, v_g_gla_head, v_sinks, v_g_final.reshape(1, D_MODEL)])
    (g_b_ada, d_b_ada, nm_b_ada, nv_b_ada, g_gn, d_gn, nm_gn, nv_gn, g_wd, d_wd, nm_wd, nv_wd,
     g_bd, d_bd, nm_bd, nv_bd, g_gg, d_gg, nm_gg, nv_gg, g_sk, d_sk, nm_sk, nv_sk,
     g_gf, d_gf, nm_gf, nv_gf, loss_row) = smalls
    loss = loss_row[0, 0]

    dmod_cols = lax.dynamic_slice(dmod_all, (0, chip * ada_cols), (8, ada_cols))
    g_w_ada, d_w_ada, nm_w_ada, nv_w_ada = _ada_update(c_all, dmod_cols, w_ada2, m_w_ada[0], v_w_ada[0])
    to3 = lambda a: jnp.transpose(a, (2, 0, 1))
    from3 = lambda a: jnp.transpose(a, (1, 2, 0))[0]
    d3, nm3, nv3, g3 = _adamw_t(to3(w_in), g_w_in_t, to3(m_w_in), to3(v_w_in), "adamw_w_in")
    g_w_in, d_w_in, nm_w_in, nv_w_in = from3(g3), from3(d3), from3(nm3), from3(nv3)
    d_w_out, nm_w_out, nv_w_out = _adamw(w_out2, g_w_out, m_w_out[0], v_w_out[0], "adamw_w_out")

    flat = lambda a: a.reshape(D_MODEL)
    grads = [g_w_ada[None], g_b_ada, g_gn, g_w_in[None], g_wd[None], g_bd, g_gg, g_sk, g_w_out[None], flat(g_gf)]
    deltas = [d_w_ada[None], d_b_ada, d_gn, d_w_in[None], d_wd[None], d_bd, d_gg, d_sk, d_w_out[None], flat(d_gf)]
    new_m = [nm_w_ada[None], nm_b_ada, nm_gn, nm_w_in[None], nm_wd[None], nm_bd, nm_gg, nm_sk, nm_w_out[None], flat(nm_gf)]
    new_v = [nv_w_ada[None], nv_b_ada, nv_gn, nv_w_in[None], nv_wd[None], nv_bd, nv_gg, nv_sk, nv_w_out[None], flat(nv_gf)]
    return (loss, gx[None], *grads, *deltas, *new_m, *new_v)
```

```python
import jax
import jax.numpy as jnp
from jax import lax
from jax.experimental import pallas as pl
from jax.experimental.pallas import tpu as pltpu

F32 = jnp.float32
BF = jnp.bfloat16

D_MODEL = 1024
GLA_HEADS = 4
GLA_DK = 64
GLA_CHUNK = 64
GLA_RANK = 16
GLA_TAU = 16.0
GLA_SUB = 256
GLA_ROWS = 512
SWA_HEADS = 8
SWA_BLOCK = 128
SWA_QBLOCKS = 8
RMS_EPS = 1e-6
ROPE_THETA = 10000.0

OFF_QK, OFF_V, OFF_GZ, OFF_SQ, OFF_SZ, OFF_SK, OFF_SV, OFF_GA = 0, 512, 1024, 1536, 2048, 2560, 2688, 2816
D_PAD = 2944
D_IN = 2832
LANES = 128
VMEM_LIMIT = 56 * 1024 * 1024

ADAM_LR, ADAM_B1, ADAM_B2, ADAM_EPS, ADAM_WD, ADAM_STEP = 0.001, 0.9, 0.999, 1e-08, 0.01, 10

NT = (((1,), (1,)), ((), ()))
TN = (((0,), (0,)), ((), ()))
MESH = pl.DeviceIdType.MESH


def _dot(a, b, dims=None):
    if dims is None:
        return jnp.dot(a, b, preferred_element_type=F32)
    return lax.dot_general(a, b, dims, preferred_element_type=F32)


def _sigmoid(x):
    return 1.0 / (1.0 + jnp.exp(-x))


def _params(sem=None):
    return pltpu.CompilerParams(dimension_semantics=sem, vmem_limit_bytes=VMEM_LIMIT)


def _full(shape):
    return pl.BlockSpec(shape, lambda i: (0,) * len(shape))


def _subtiles(rows, size=256):
    size = min(size, rows)
    return [slice(k * size, (k + 1) * size) for k in range(rows // size)]


_GATHER_SEMS = [pltpu.SemaphoreType.DMA((7,)), pltpu.SemaphoreType.DMA((7,)), pltpu.SemaphoreType.DMA]


class _Gather:
    def __init__(self, x_ref, out_ref, send_sems, recv_sems, local_sem):
        x, y, c = lax.axis_index("x"), lax.axis_index("y"), lax.axis_index("c")
        self.me, self.sibling, self.c = (x, y, c), (x, y, 1 - c), c
        self.chips = [(1 - x, y), (x, 1 - y), (1 - x, 1 - y)]
        self.x_ref, self.out_ref, self.send_sems, self.recv_sems = x_ref, out_ref, send_sems, recv_sems
        self.mine = pltpu.make_async_copy(x_ref, self._slab(*self.me), local_sem)

    def _slab(self, px, py, pc):
        return self.out_ref.at[4 * px + 2 * py + pc]

    def _copy(self, k, blk, to, src=None):
        return pltpu.make_async_remote_copy(
            src_ref=self._slab(*blk) if src is None else src, dst_ref=self._slab(*blk),
            send_sem=self.send_sems.at[k], recv_sem=self.recv_sems.at[k], device_id=to, device_id_type=MESH)

    def start(self):
        self.mine.start()
        self.sent = [self._copy(0, self.me, self.sibling, src=self.x_ref)]
        self.sent += [self._copy(1 + j, self.me, (*chip, self.c), src=self.x_ref) for j, chip in enumerate(self.chips)]
        for cp in self.sent:
            cp.start()

    def relay(self):
        for j, chip in enumerate(self.chips):
            self._copy(1 + j, (*chip, self.c), self.me).wait_recv()
            cp = self._copy(4 + j, (*chip, self.c), self.sibling)
            cp.start()
            self.sent.append(cp)

    def finish(self):
        self._copy(0, self.sibling, self.me).wait_recv()
        for j, chip in enumerate(self.chips):
            self._copy(4 + j, (*chip, 1 - self.c), self.me).wait_recv()
        for cp in self.sent:
            cp.wait_send()
        self.mine.wait()


def _prologue(cw, w_ada, b_shard, half_in, half_out, pos_col, inv_freq):
    s = pos_col.shape[0]
    rt = min(512, s)

    def body(cw_ref, wada_ref, b_ref, hin_ref, hout_ref, pos_ref, f_ref,
             first_ref, mod_ref, win_ref, wout_ref, cos_ref, sin_ref, mod_blk, *sems):
        g_c = _Gather(cw_ref, first_ref, *sems[0:3])
        g_in = _Gather(hin_ref, win_ref, *sems[3:6])
        g_out = _Gather(hout_ref, wout_ref, *sems[6:9])
        g_mod = _Gather(mod_blk, mod_ref, *sems[9:12])
        g_c.start()
        g_in.start()
        g_out.start()
        g_c.relay()
        g_c.finish()
        c_rows = [jnp.concatenate([first_ref[d, r:r + 1, :] for r in range(8)], axis=1) for d in range(8)]
        c_all = jnp.concatenate(c_rows, axis=0)
        sc = (c_all * _sigmoid(c_all)).astype(BF)
        mod_blk[...] = _dot(sc, wada_ref[...].astype(BF)) + b_ref[...]
        g_mod.start()

        def rope_rows(i, carry):
            rows = pl.ds(pl.multiple_of(i * rt, rt), rt)
            ang = pos_ref[rows, :].astype(F32) * f_ref[...]
            lane = lax.broadcasted_iota(jnp.int32, ang.shape, 1)
            cos_ref[rows, :] = jnp.cos(ang)
            sn = jnp.sin(ang)
            sin_ref[rows, :] = jnp.where((lane % 64) < 32, -sn, sn)
            return carry

        lax.fori_loop(0, s // rt, rope_rows, 0)
        g_in.relay()
        g_out.relay()
        g_mod.relay()
        g_in.finish()
        g_out.finish()
        g_mod.finish()

    vm = pl.BlockSpec(memory_space=pltpu.VMEM)
    return pl.pallas_call(
        body, name="prologue",
        out_shape=[jax.ShapeDtypeStruct((8,) + cw.shape, F32), jax.ShapeDtypeStruct((8, 8, w_ada.shape[1]), F32),
                   jax.ShapeDtypeStruct((8,) + half_in.shape, half_in.dtype),
                   jax.ShapeDtypeStruct((8,) + half_out.shape, half_out.dtype),
                   jax.ShapeDtypeStruct((s, LANES), F32), jax.ShapeDtypeStruct((s, LANES), F32)],
        in_specs=[vm] * 7, out_specs=[vm] * 6,
        scratch_shapes=[pltpu.VMEM((8, w_ada.shape[1]), F32)] + _GATHER_SEMS * 4,
        compiler_params=pltpu.CompilerParams(vmem_limit_bytes=VMEM_LIMIT),
    )(cw, w_ada, b_shard, half_in, half_out, pos_col, inv_freq)


def _reduce_scratch(rr, cc):
    c2 = cc // 2
    return [pltpu.VMEM((4, rr, c2), F32), pltpu.VMEM((4, rr, c2), F32), pltpu.VMEM((3, rr, c2), BF),
            pltpu.VMEM((3, rr, c2), BF), pltpu.VMEM((rr, c2), F32),
            pltpu.SemaphoreType.DMA((8,)), pltpu.SemaphoreType.DMA((8,)), pltpu.SemaphoreType.DMA((5,))]


class _Reduce:
    def __init__(self, p_hbm, out_ref, acc_ref, own_ref, send_ref, land_ref, res_ref, send_sems, recv_sems, local_sems):
        x, y, c = lax.axis_index("x"), lax.axis_index("y"), lax.axis_index("c")
        c2 = out_ref.shape[1] // 2
        self.c, sibling = c, (x, y, 1 - c)
        self.chips = [(1 - x, y), (x, 1 - y), (1 - x, 1 - y)]
        shards = [2 * tx + ty for tx, ty in self.chips] + [2 * x + y]
        mine = pl.ds(pl.multiple_of(c * c2, c2), c2)
        other = pl.ds(pl.multiple_of((1 - c) * c2, c2), c2)
        self.acc_ref, self.own_ref, self.send_ref, self.land_ref, self.res_ref = acc_ref, own_ref, send_ref, land_ref, res_ref
        self.send_sems, self.recv_sems = send_sems, recv_sems
        self.own = [pltpu.make_async_copy(p_hbm.at[j, :, mine], own_ref.at[k], local_sems.at[k])
                    for k, j in enumerate(shards)]
        self.swap = [pltpu.make_async_remote_copy(
            src_ref=p_hbm.at[j, :, other], dst_ref=acc_ref.at[k], send_sem=send_sems.at[k], recv_sem=recv_sems.at[k],
            device_id=sibling, device_id_type=MESH) for k, j in enumerate(shards)]
        self.put = pltpu.make_async_copy(res_ref, out_ref.at[:, mine], local_sems.at[4])
        self.share = pltpu.make_async_remote_copy(
            src_ref=res_ref, dst_ref=out_ref.at[:, mine], send_sem=send_sems.at[7],
            recv_sem=recv_sems.at[7], device_id=sibling, device_id_type=MESH)

    def start(self):
        for k in range(4):
            self.own[k].start()
            self.swap[k].start()

    def _combine(self, k):
        self.own[k].wait()
        self.swap[k].wait()
        self.acc_ref[k] = self.acc_ref[k] + self.own_ref[k]

    def combine_and_send(self):
        self.sends = []
        for k, (tx, ty) in enumerate(self.chips):
            self._combine(k)
            self.send_ref[k] = self.acc_ref[k].astype(self.send_ref.dtype)
            cp = pltpu.make_async_remote_copy(
                src_ref=self.send_ref.at[k], dst_ref=self.land_ref.at[k], send_sem=self.send_sems.at[4 + k],
                recv_sem=self.recv_sems.at[4 + k], device_id=(tx, ty, self.c), device_id_type=MESH)
            cp.start()
            self.sends.append(cp)
        self._combine(3)

    def total_and_share(self):
        for cp in self.sends:
            cp.wait_recv()
        total = self.acc_ref[3]
        for k in range(3):
            total = total + self.land_ref[k].astype(F32)
        self.res_ref[...] = total
        for cp in self.sends:
            cp.wait_send()
        self.put.start()
        self.share.start()

    def finish(self):
        self.put.wait()
        self.share.wait()


def _epilogue(dw_in_parts, dw_out_parts, small):
    _, r_in, cc = dw_in_parts.shape
    _, r_out, _ = dw_out_parts.shape
    n_red = len(_reduce_scratch(r_in, cc))

    def body(pin_hbm, pout_hbm, small_ref, gin_ref, gout_ref, small_all_ref, *scratch):
        red_in = _Reduce(pin_hbm, gin_ref, *scratch[0:n_red])
        red_out = _Reduce(pout_hbm, gout_ref, *scratch[n_red:2 * n_red])
        gat = _Gather(small_ref, small_all_ref, *scratch[2 * n_red:])
        red_out.start()
        red_in.start()
        gat.start()
        red_out.combine_and_send()
        red_in.combine_and_send()
        gat.relay()
        red_out.total_and_share()
        red_in.total_and_share()
        gat.finish()
        red_out.finish()
        red_in.finish()

    vm = pl.BlockSpec(memory_space=pltpu.VMEM)
    anyspec = pl.BlockSpec(memory_space=pl.ANY)
    return pl.pallas_call(
        body, name="epilogue",
        out_shape=[jax.ShapeDtypeStruct((r_in, cc), F32), jax.ShapeDtypeStruct((r_out, cc), F32),
                   jax.ShapeDtypeStruct((8,) + small.shape, F32)],
        in_specs=[anyspec, anyspec, vm], out_specs=[vm, vm, vm],
        scratch_shapes=_reduce_scratch(r_in, cc) + _reduce_scratch(r_out, cc) + _GATHER_SEMS,
        compiler_params=pltpu.CompilerParams(vmem_limit_bytes=VMEM_LIMIT),
    )(dw_in_parts, dw_out_parts, small)


def _rope(t, cosb, sinb, first_half):
    partner = jnp.where(first_half, pltpu.roll(t, 96, 1), pltpu.roll(t, 32, 1))
    return t * cosb + partner * sinb


def _rope_t(g, cosb, sinb, first_half):
    gs = g * sinb
    partner = jnp.where(first_half, pltpu.roll(gs, 96, 1), pltpu.roll(gs, 32, 1))
    return g * cosb + partner


def _modnorm(x, g, sc1p, shift):
    r = lax.rsqrt(jnp.mean(x * x, axis=-1, keepdims=True) + RMS_EPS)
    xn = x * r
    return xn, r, (xn * g) * sc1p + shift


def _inproj_fwd(x2d, shift, sc1p, g_norm, wpad_t):
    s = x2d.shape[0]
    tm = min(512, s)

    def body(x_ref, sh_ref, sc_ref, g_ref, w_ref, o_ref):
        subs = _subtiles(tm)
        hs = [_modnorm(x_ref[sl, :], g_ref[...], sc_ref[...], sh_ref[...])[2].astype(BF) for sl in subs]
        for sl, h in zip(subs, hs):
            o_ref[sl, :] = _dot(h, w_ref[...], NT)

    vec = _full((1, D_MODEL))
    return pl.pallas_call(
        body, name="inproj_fwd", grid=(s // tm,),
        in_specs=[pl.BlockSpec((tm, D_MODEL), lambda i: (i, 0)), vec, vec, vec, _full((D_PAD, D_MODEL))],
        out_specs=pl.BlockSpec((tm, D_PAD), lambda i: (i, 0)),
        out_shape=jax.ShapeDtypeStruct((s, D_PAD), F32),
        compiler_params=_params(("arbitrary",)),
    )(x2d, shift, sc1p, g_norm, wpad_t)


def _split3(a):
    hi = a.astype(BF)
    r1 = a - hi.astype(F32)
    mid = r1.astype(BF)
    lo = (r1 - mid.astype(F32)).astype(BF)
    return hi, mid, lo


def _tri_matmul(tri, a):
    hi, mid, lo = _split3(a)
    return _dot(tri, hi) + _dot(tri, mid) + _dot(tri, lo)


def _chunks(tb):
    return [slice(c * GLA_CHUNK, (c + 1) * GLA_CHUNK) for c in range(tb // GLA_CHUNK)]


def _per_chunk_rows(rows, width):
    return jnp.concatenate([jnp.broadcast_to(r, (GLA_CHUNK, width)) for r in rows], axis=0)


def _gla_triangle(tb):
    row = lax.broadcasted_iota(jnp.int32, (tb, tb), 0)
    col = lax.broadcasted_iota(jnp.int32, (tb, tb), 1)
    return (((row // GLA_CHUNK) == (col // GLA_CHUNK)) & (col <= row)).astype(F32)


def _lane_mean(x, ones_b):
    hi = x.astype(BF)
    lo = (x - hi.astype(F32)).astype(BF)
    return (_dot(hi, ones_b) + _dot(lo, ones_b)) * (1.0 / LANES)


def _head(t, h, lo_h):
    blk = t[:, LANES * (h // 2):LANES * (h // 2 + 1)]
    return jnp.where(lo_h, blk, 0.0) if h % 2 == 0 else jnp.where(lo_h, 0.0, blk)


def _gla_block_common(qk, ga, wd, bd, tril_b):
    tb = qk.shape[0]
    q, k = qk[:, :256], qk[:, 256:]
    z = _dot(ga.astype(BF), wd) + bd
    la = (jnp.minimum(z, 0.0) - jnp.log(1.0 + jnp.exp(-jnp.abs(z)))) * (1.0 / GLA_TAU)
    b = _tri_matmul(tril_b, la)
    bls = [b[rs.stop - 1:rs.stop, :] for rs in _chunks(tb)]
    eq = jnp.exp(b)
    ek = jnp.exp(-b)
    f = jnp.exp(_per_chunk_rows(bls, 256) - b)
    return z, eq, ek, f, q * (eq * GLA_DK ** -0.5), k * ek, k * f, bls


def _gla_units(s):
    sub = min(GLA_SUB, s)
    tb = min(GLA_ROWS, s)
    subs = [slice(i * sub, (i + 1) * sub) for i in range(tb // sub)]
    units = [(i, h) for i in range(len(subs)) for h in range(GLA_HEADS)]
    return tb, sub, subs, units


def _gla_fwd(proj, wdecp, bdec, ggla):
    s = proj.shape[0]
    tb, sub, subs, units = _gla_units(s)
    nch = sub // GLA_CHUNK

    def body(qk_ref, v_ref, gz_ref, ga_ref, wd_ref, bd_ref, gg_ref, tri_ref, og_ref, opre_ref, sprev_ref, st_ref):
        @pl.when(pl.program_id(0) == 0)
        def _():
            st_ref[...] = jnp.zeros_like(st_ref)

        lo_h = lax.broadcasted_iota(jnp.int32, (sub, LANES), 1) < GLA_DK
        tril = tri_ref[...] > 0.5
        tril_b = tri_ref[...].astype(BF)
        ones_b = jnp.ones((LANES, LANES), BF)
        gg, wd, bd = gg_ref[...], wd_ref[...], bd_ref[...]
        chunks = _chunks(sub)
        lanes = [slice(h * LANES, (h + 1) * LANES) for h in range(GLA_HEADS)]
        com = [_gla_block_common(qk_ref[sl, :], ga_ref[sl, :], wd, bd, tril_b) for sl in subs]
        decs = [[jnp.exp(bl) for bl in cm[7]] for cm in com]
        a = {(i, h): _head(com[i][4], h, lo_h).astype(BF) for i, h in units}
        bm = {(i, h): _head(com[i][5], h, lo_h).astype(BF) for i, h in units}
        ktl = {(i, h): _head(com[i][6], h, lo_h).astype(BF) for i, h in units}
        vh = {(i, h): v_ref[subs[i], lanes[h]].astype(BF) for i, h in units}
        sc = {u: _dot(a[u], bm[u], NT) for u in units}
        upd = {u: [_dot(vh[u][rs], ktl[u][rs], TN) for rs in chunks] for u in units}
        p = {u: jnp.where(tril, sc[u], 0.0).astype(BF) for u in units}
        o = {u: _dot(p[u], vh[u]) for u in units}
        states = {}
        for h in range(GLA_HEADS):
            st = st_ref[h]
            for i in range(len(subs)):
                entering = []
                for c in range(nch):
                    entering.append(st)
                    sprev_ref[i * nch + c, h] = st
                    st = st * decs[i][c][:, LANES * (h // 2):LANES * (h // 2 + 1)] + upd[(i, h)][c]
                states[(i, h)] = entering
            st_ref[h] = st
        inter = {u: [_dot(a[u][rs], states[u][c].astype(BF), NT) for c, rs in enumerate(chunks)] for u in units}
        o = {u: o[u] + jnp.concatenate(inter[u], axis=0) for u in units}
        ms = {u: _lane_mean(o[u] * o[u], ones_b) for u in units}
        for i, h in units:
            gzh = gz_ref[subs[i], lanes[h]]
            opre_ref[subs[i], lanes[h]] = o[(i, h)]
            og_ref[subs[i], lanes[h]] = (((o[(i, h)] * lax.rsqrt(ms[(i, h)] + RMS_EPS)) * gg[:, lanes[h]])
                                         * (gzh * _sigmoid(gzh))).astype(og_ref.dtype)

    def col(width, off):
        return pl.BlockSpec((tb, width), lambda i: (i, off // width))

    return pl.pallas_call(
        body, name="gla_fwd", grid=(s // tb,),
        in_specs=[col(512, OFF_QK), col(512, OFF_V), col(512, OFF_GZ), col(LANES, OFF_GA),
                  _full((LANES, 256)), _full((1, 256)), _full((1, 512)), _full((sub, sub))],
        out_specs=[pl.BlockSpec((tb, 512), lambda i: (i, 0)), pl.BlockSpec((tb, 512), lambda i: (i, 0)),
                   pl.BlockSpec((tb // GLA_CHUNK, GLA_HEADS, LANES, LANES), lambda i: (i, 0, 0, 0))],
        out_shape=[jax.ShapeDtypeStruct((s, 512), BF), jax.ShapeDtypeStruct((s, 512), F32),
                   jax.ShapeDtypeStruct((s // GLA_CHUNK, GLA_HEADS, LANES, LANES), F32)],
        scratch_shapes=[pltpu.VMEM((GLA_HEADS, LANES, LANES), F32)],
        compiler_params=_params(("arbitrary",)),
    )(proj, proj, proj, proj, wdecp, bdec, ggla, _gla_triangle(sub))


def _gla_bwd(proj, dog, opre, sprev, wdecp, bdec, ggla):
    s = proj.shape[0]
    tb, sub, subs, units = _gla_units(s)
    nsub = len(subs)
    nch = sub // GLA_CHUNK
    nb = s // tb

    def body(qk_ref, v_ref, gz_ref, ga_ref, dog_ref, opre_ref, sprev_ref, wd_ref, bd_ref, gg_ref, tri_ref, triu_ref,
             dqk_ref, dv_ref, dgz_ref, dga_ref, dwd_ref, dbd_ref, dgg_ref, dst_ref):
        @pl.when(pl.program_id(0) == 0)
        def _():
            dst_ref[...] = jnp.zeros_like(dst_ref)
            dwd_ref[...] = jnp.zeros_like(dwd_ref)
            dbd_ref[...] = jnp.zeros_like(dbd_ref)
            dgg_ref[...] = jnp.zeros_like(dgg_ref)

        lo_h = lax.broadcasted_iota(jnp.int32, (sub, LANES), 1) < GLA_DK
        tril = tri_ref[...] > 0.5
        tril_b = tri_ref[...].astype(BF)
        triu_b = triu_ref[...].astype(BF)
        ones_b = jnp.ones((LANES, LANES), BF)
        last_row = (lax.broadcasted_iota(jnp.int32, (sub, LANES), 0) % GLA_CHUNK) == GLA_CHUNK - 1
        wd, gg, bd = wd_ref[...], gg_ref[...], bd_ref[...]
        chunks = _chunks(sub)
        lanes = [slice(h * LANES, (h + 1) * LANES) for h in range(GLA_HEADS)]
        blks = [slice(LANES * (h // 2), LANES * (h // 2 + 1)) for h in range(GLA_HEADS)]
        ga = [ga_ref[sl, :] for sl in subs]
        com = [_gla_block_common(qk_ref[sl, :], ga[i], wd, bd, tril_b) for i, sl in enumerate(subs)]
        decs = [[jnp.exp(bl) for bl in cm[7]] for cm in com]
        a = {(i, h): _head(com[i][4], h, lo_h).astype(BF) for i, h in units}
        bm = {(i, h): _head(com[i][5], h, lo_h).astype(BF) for i, h in units}
        ktl = {(i, h): _head(com[i][6], h, lo_h).astype(BF) for i, h in units}
        vh = {(i, h): v_ref[subs[i], lanes[h]].astype(BF) for i, h in units}
        sc = {u: _dot(a[u], bm[u], NT) for u in units}

        o = {(i, h): opre_ref[subs[i], lanes[h]] for i, h in units}
        ms = {u: _lane_mean(o[u] * o[u], ones_b) for u in units}
        gz = {(i, h): gz_ref[subs[i], lanes[h]] for i, h in units}
        dog = {(i, h): dog_ref[subs[i], lanes[h]] for i, h in units}
        sg = {u: _sigmoid(gz[u]) for u in units}
        r = {u: lax.rsqrt(ms[u] + RMS_EPS) for u in units}
        ohat = {u: o[u] * r[u] for u in units}
        sil = {u: gz[u] * sg[u] for u in units}
        for i, h in units:
            u = (i, h)
            dgz_ref[subs[i], lanes[h]] = (dog[u] * (ohat[u] * gg[:, lanes[h]])
                                          * (sg[u] * (1.0 + gz[u] * (1.0 - sg[u])))).astype(dgz_ref.dtype)
            dgg_ref[:, lanes[h]] += jnp.sum(dog[u] * sil[u] * ohat[u], axis=0, keepdims=True)
        dn = {(i, h): dog[(i, h)] * sil[(i, h)] * gg[:, lanes[h]] for i, h in units}
        mdn = {u: _lane_mean(dn[u] * ohat[u], ones_b) for u in units}
        do = {u: (r[u] * (dn[u] - ohat[u] * mdn[u])).astype(BF) for u in units}

        p = {u: jnp.where(tril, sc[u], 0.0).astype(BF) for u in units}
        dpr = {u: _dot(do[u], vh[u], NT) for u in units}
        incr = {u: [_dot(do[u][rs], a[u][rs], TN) for rs in chunks] for u in units}
        dv = {u: _dot(p[u], do[u], TN) for u in units}
        dp = {u: jnp.where(tril, dpr[u], 0.0).astype(BF) for u in units}
        dqd = {u: _dot(dp[u], bm[u]) for u in units}
        dkd = {u: _dot(dp[u], a[u], TN) for u in units}
        st = {(i, h): [sprev_ref[i * nch + c, h] for c in range(nch)] for i, h in units}
        leaving = {}
        for h in range(GLA_HEADS):
            d = dst_ref[h]
            for i in reversed(range(nsub)):
                out = [None] * nch
                for c in reversed(range(nch)):
                    out[c] = d
                    d = d * decs[i][c][:, blks[h]] + incr[(i, h)][c]
                leaving[(i, h)] = out
            dst_ref[h] = d
        lv_b = {u: [leaving[u][c].astype(BF) for c in range(nch)] for u in units}
        dv_s = {u: [_dot(ktl[u][rs], lv_b[u][c], NT) for c, rs in enumerate(chunks)] for u in units}
        dqd_s = {u: [_dot(do[u][rs], st[u][c].astype(BF)) for c, rs in enumerate(chunks)] for u in units}
        dkt_s = {u: [_dot(vh[u][rs], lv_b[u][c]) for c, rs in enumerate(chunks)] for u in units}
        ddec = {u: [jnp.sum(leaving[u][c] * st[u][c], axis=0, keepdims=True) for c in range(nch)] for u in units}
        for i, h in units:
            dv_ref[subs[i], lanes[h]] = (dv[(i, h)] + jnp.concatenate(dv_s[(i, h)], axis=0)).astype(dv_ref.dtype)
        dqd = {u: dqd[u] + jnp.concatenate(dqd_s[u], axis=0) for u in units}
        dkt = {u: jnp.concatenate(dkt_s[u], axis=0) for u in units}

        db = []
        for i, sl in enumerate(subs):
            _, eq, ek, f, qd, kd, kt, _ = com[i]
            parts = []
            for pair in range(GLA_HEADS // 2):
                blk, u0, u1 = blks[2 * pair], (i, 2 * pair), (i, 2 * pair + 1)
                dqd_b, dkd_b, dkt_b = dqd[u0] + dqd[u1], dkd[u0] + dkd[u1], dkt[u0] + dkt[u1]
                dqk_ref[sl, blk] = (dqd_b * (eq[:, blk] * GLA_DK ** -0.5)).astype(dqk_ref.dtype)
                dqk_ref[sl, 256 + LANES * pair:256 + LANES * (pair + 1)] = (dkd_b * ek[:, blk] + dkt_b * f[:, blk]).astype(dqk_ref.dtype)
                dkt_kt = dkt_b * kt[:, blk]
                dbp = dqd_b * qd[:, blk] - dkd_b * kd[:, blk] - dkt_kt
                dbl = [jnp.sum(dkt_kt[rs], axis=0, keepdims=True) + (ddec[u0][c] + ddec[u1][c]) * decs[i][c][:, blk]
                       for c, rs in enumerate(chunks)]
                parts.append(jnp.where(last_row, dbp + _per_chunk_rows(dbl, LANES), dbp))
            db.append(jnp.concatenate(parts, axis=1))
        dla = [_tri_matmul(triu_b, db[i]) for i in range(nsub)]
        dz32 = [dla[i] * (1.0 / GLA_TAU) * _sigmoid(-com[i][0]) for i in range(nsub)]
        dz = [t.astype(BF) for t in dz32]
        for i, sl in enumerate(subs):
            dga_ref[sl, :] = _dot(dz[i], wd, NT).astype(dga_ref.dtype)
            dwd_ref[...] += _dot(ga[i].astype(BF), dz[i], TN)
            dbd_ref[...] += jnp.sum(dz32[i], axis=0, keepdims=True)

    def col(width, off):
        return pl.BlockSpec((tb, width), lambda i: (nb - 1 - i, off // width))

    def rev(width):
        return pl.BlockSpec((tb, width), lambda i: (nb - 1 - i, 0))

    return pl.pallas_call(
        body, name="gla_bwd", grid=(nb,),
        in_specs=[col(512, OFF_QK), col(512, OFF_V), col(512, OFF_GZ), col(LANES, OFF_GA), rev(512), rev(512),
                  pl.BlockSpec((tb // GLA_CHUNK, GLA_HEADS, LANES, LANES), lambda i: (nb - 1 - i, 0, 0, 0)),
                  _full((LANES, 256)), _full((1, 256)), _full((1, 512)), _full((sub, sub)), _full((sub, sub))],
        out_specs=[rev(512), rev(512), rev(512), rev(LANES), _full((LANES, 256)), _full((1, 256)), _full((1, 512))],
        out_shape=[jax.ShapeDtypeStruct((s, 512), BF), jax.ShapeDtypeStruct((s, 512), BF),
                   jax.ShapeDtypeStruct((s, 512), BF), jax.ShapeDtypeStruct((s, LANES), BF),
                   jax.ShapeDtypeStruct((LANES, 256), F32), jax.ShapeDtypeStruct((1, 256), F32),
                   jax.ShapeDtypeStruct((1, 512), F32)],
        scratch_shapes=[pltpu.VMEM((GLA_HEADS, LANES, LANES), F32)],
        compiler_params=_params(("arbitrary",)),
    )(proj, proj, proj, proj, dog, opre, sprev, wdecp, bdec, ggla, _gla_triangle(sub), _gla_triangle(sub).T)


_SWA_COL_HEADS = (0, 2, 1, 3, 4, 6, 5, 7)
_SWA_COLS = SWA_HEADS * SWA_BLOCK


def _swa_masks():
    lo2 = lax.broadcasted_iota(jnp.int32, (2 * SWA_BLOCK, LANES), 1) < 64
    lane1 = lax.broadcasted_iota(jnp.int32, (SWA_BLOCK, LANES), 1)
    first_half = (lane1 % 64) < 32
    key = lax.broadcasted_iota(jnp.int32, (SWA_BLOCK, _SWA_COLS), 0)
    query = lax.broadcasted_iota(jnp.int32, (SWA_BLOCK, _SWA_COLS), 1) % SWA_BLOCK
    return lo2, lane1 < 64, first_half, key > query


def _merge_band(t, prev_mask, prev_bias=None):
    prev = t[:SWA_BLOCK] if prev_bias is None else t[:SWA_BLOCK] + prev_bias
    return jnp.where(prev_mask, prev, t[SWA_BLOCK:])


def _split_band(t, prev_mask_b):
    prev = t * prev_mask_b
    return jnp.concatenate([prev, t - prev], axis=0)


def _kv_variants(t, lo2):
    tr = pltpu.roll(t, 64, 1)
    lo_v = [jnp.where(lo2, t, 0.0).astype(BF), jnp.where(lo2, tr, 0.0).astype(BF)]
    hi_v = [jnp.where(lo2, 0.0, tr).astype(BF), jnp.where(lo2, 0.0, t).astype(BF)]
    return lo_v, hi_v


def _kv_variants_t(t):
    tt = t.T
    sw = jnp.concatenate([tt[64:], tt[:64]], axis=0)
    top = lax.broadcasted_iota(jnp.int32, tt.shape, 0) < 64
    lo_v = [jnp.where(top, tt, 0.0).astype(BF), jnp.where(top, sw, 0.0).astype(BF)]
    hi_v = [jnp.where(top, 0.0, sw).astype(BF), jnp.where(top, 0.0, tt).astype(BF)]
    return lo_v, hi_v


def _swa_scores(qg, k_lo, k_hi):
    return jnp.concatenate([_dot(k_lo[0], qg[0], NT), _dot(k_hi[0], qg[0], NT),
                            _dot(k_lo[1], qg[1], NT), _dot(k_hi[1], qg[1], NT)], axis=1)


def _sink_row(sinks_ref):
    return jnp.concatenate([jnp.full((1, SWA_BLOCK), sinks_ref[0, hd], F32) for hd in _SWA_COL_HEADS], axis=1)


def _swa_softmax(st, prev_mask, prev_bias, sink):
    st = _merge_band(st, prev_mask, prev_bias)
    m = jnp.maximum(jnp.max(st, axis=0, keepdims=True), sink)
    ex = jnp.exp(st - m)
    es = jnp.exp(sink - m)
    inv = 1.0 / (jnp.sum(ex, axis=0, keepdims=True) + es)
    return ex, es, inv


def _no_prev_bias(block_index):
    return jnp.where(block_index > 0, 0.0, -1e30).astype(F32)


def _swa_queries(sq_ref, rows, cosb, sinb, first_half):
    qs = [_rope(sq_ref[rows, p * LANES:(p + 1) * LANES], cosb, sinb, first_half) * 0.125 for p in range(4)]
    return [jnp.concatenate(qs[0:2], axis=0), jnp.concatenate(qs[2:4], axis=0)]


def _swa_fwd(proj, cos, sin, sinks):
    s = proj.shape[0]
    nq = min(SWA_QBLOCKS, s // SWA_BLOCK)
    tq = nq * SWA_BLOCK

    def body(sq_ref, sz_ref, sk_ref, sv_ref, cos_ref, sin_ref, sinks_ref, os_ref, opre_ref, kprev, vprev):
        n = pl.program_id(0)

        @pl.when(n == 0)
        def _():
            kprev[...] = jnp.zeros_like(kprev)
            vprev[...] = jnp.zeros_like(vprev)

        lo2, _, first_half, prev_mask = _swa_masks()
        prev_mask_b = jnp.where(prev_mask, 1.0, 0.0).astype(BF)
        sink = _sink_row(sinks_ref)
        blocks = range(nq)
        rows = [slice(j * SWA_BLOCK, (j + 1) * SWA_BLOCK) for j in blocks]
        cosb = [cos_ref[rows[j], :] for j in blocks]
        sinb = [sin_ref[rows[j], :] for j in blocks]
        kc = [_rope(sk_ref[rows[j], :], cosb[j], sinb[j], first_half) for j in blocks]
        vc = [sv_ref[rows[j], :] for j in blocks]
        kcat = [jnp.concatenate([kprev[...] if j == 0 else kc[j - 1], kc[j]], axis=0) for j in blocks]
        vcat = [jnp.concatenate([vprev[...] if j == 0 else vc[j - 1], vc[j]], axis=0) for j in blocks]
        kprev[...] = kc[-1]
        vprev[...] = vc[-1]
        kvar = [_kv_variants(kcat[j], lo2) for j in blocks]
        vtvar = [_kv_variants_t(vcat[j]) for j in blocks]
        qg = [[q.astype(BF) for q in _swa_queries(sq_ref, rows[j], cosb[j], sinb[j], first_half)] for j in blocks]
        st = [_swa_scores(qg[j], *kvar[j]) for j in blocks]
        soft = [_swa_softmax(st[j], prev_mask, _no_prev_bias(n) if j == 0 else None, sink) for j in blocks]
        pt = [_split_band(soft[j][0].astype(BF), prev_mask_b) for j in blocks]
        og = {}
        for j in blocks:
            inv = soft[j][2]
            for g in range(2):
                c0, c1, c2 = 512 * g, 512 * g + 256, 512 * g + 512
                ot = (_dot(vtvar[j][0][g], pt[j][:, c0:c1]) * inv[:, c0:c1]
                      + _dot(vtvar[j][1][g], pt[j][:, c1:c2]) * inv[:, c1:c2])
                og[(j, g)] = ot.T
        for j in blocks:
            for g in range(2):
                for i in range(2):
                    ls = slice((2 * g + i) * LANES, (2 * g + i + 1) * LANES)
                    o = og[(j, g)][i * SWA_BLOCK:(i + 1) * SWA_BLOCK]
                    sz = sz_ref[rows[j], ls]
                    opre_ref[rows[j], ls] = o
                    os_ref[rows[j], ls] = (o * (sz * _sigmoid(sz))).astype(os_ref.dtype)

    def col(width, off):
        return pl.BlockSpec((tq, width), lambda i: (i, off // width))

    row = pl.BlockSpec((tq, LANES), lambda i: (i, 0))
    return pl.pallas_call(
        body, name="swa_fwd", grid=(s // tq,),
        in_specs=[col(512, OFF_SQ), col(512, OFF_SZ), col(LANES, OFF_SK), col(LANES, OFF_SV), row, row,
                  pl.BlockSpec(memory_space=pltpu.SMEM)],
        out_specs=[pl.BlockSpec((tq, 512), lambda i: (i, 0))] * 2,
        out_shape=[jax.ShapeDtypeStruct((s, 512), BF), jax.ShapeDtypeStruct((s, 512), F32)],
        scratch_shapes=[pltpu.VMEM((SWA_BLOCK, LANES), F32)] * 2,
        compiler_params=_params(("arbitrary",)),
    )(proj, proj, proj, proj, cos, sin, sinks)


def _swa_bwd(proj, dos, opre, cos, sin, sinks):
    s = proj.shape[0]
    nq = min(SWA_QBLOCKS, s // SWA_BLOCK)
    tq = nq * SWA_BLOCK

    def body(sq_ref, sz_ref, sk_ref, sv_ref, dos_ref, opre_ref, cos_ref, sin_ref, sinks_ref,
             dsq_ref, dsz_ref, dsk_ref, dsv_ref, dsink_ref, kprev, vprev, cprev, sprev):
        n = pl.program_id(0)

        @pl.when(n == 0)
        def _():
            kprev[...] = jnp.zeros_like(kprev)
            vprev[...] = jnp.zeros_like(vprev)
            cprev[...] = jnp.zeros_like(cprev)
            sprev[...] = jnp.zeros_like(sprev)
            for hd in range(SWA_HEADS):
                dsink_ref[0, hd] = 0.0

        lo2, lo1, first_half, prev_mask = _swa_masks()
        prev_mask_b = jnp.where(prev_mask, 1.0, 0.0).astype(BF)
        lo1s = jnp.concatenate([lo1, lo1], axis=0)
        sink = _sink_row(sinks_ref)

        def home(m0, m1):
            t0 = m0 + pltpu.roll(m0, 64, 1)
            t1 = m1 + pltpu.roll(m1, 64, 1)
            return jnp.where(lo2, t0, t1)

        kp, vp, cp_, sp_ = kprev[...], vprev[...], cprev[...], sprev[...]
        for j in range(nq):
            rows = slice(j * SWA_BLOCK, (j + 1) * SWA_BLOCK)
            blk = n * nq + j
            cosb, sinb = cos_ref[rows, :], sin_ref[rows, :]
            kc = _rope(sk_ref[rows, :], cosb, sinb, first_half)
            vc = sv_ref[rows, :]
            kcat = jnp.concatenate([kp, kc], axis=0)
            k_lo, k_hi = _kv_variants(kcat, lo2)
            kt_lo, kt_hi = _kv_variants_t(kcat)
            v_lo, v_hi = _kv_variants(jnp.concatenate([vp, vc], axis=0), lo2)
            qg32 = _swa_queries(sq_ref, rows, cosb, sinb, first_half)
            qg = [q.astype(BF) for q in qg32]
            ex, es, inv = _swa_softmax(_swa_scores(qg, k_lo, k_hi), prev_mask, _no_prev_bias(n) if j == 0 else None, sink)
            pr, ps = ex * inv, es * inv

            dog32 = []
            for g in range(2):
                parts = []
                for i in range(2):
                    ls = slice((2 * g + i) * LANES, (2 * g + i + 1) * LANES)
                    sz = sz_ref[rows, ls]
                    sg = _sigmoid(sz)
                    dos_p = dos_ref[rows, ls]
                    dsz_ref[rows, ls] = (dos_p * opre_ref[rows, ls] * (sg * (1.0 + sz * (1.0 - sg)))).astype(dsz_ref.dtype)
                    parts.append(dos_p * (sz * sg))
                dog32.append(jnp.concatenate(parts, axis=0))
            dog = [t.astype(BF) for t in dog32]
            dpr = _merge_band(jnp.concatenate([_dot(v_lo[0], dog[0], NT), _dot(v_hi[0], dog[0], NT),
                                               _dot(v_lo[1], dog[1], NT), _dot(v_hi[1], dog[1], NT)], axis=1), prev_mask)
            rd = jnp.sum(pr * dpr, axis=0, keepdims=True)
            ds = _split_band((pr * (dpr - rd)).astype(BF), prev_mask_b)
            prb = _split_band(pr.astype(BF), prev_mask_b)
            sink_term = ps * rd
            for r, hd in enumerate(_SWA_COL_HEADS):
                dsink_ref[0, hd] += -jnp.sum(sink_term[:, r * SWA_BLOCK:(r + 1) * SWA_BLOCK])

            dk_g, dv_g = [], []
            for g in range(2):
                c0, c1, c2 = 512 * g, 512 * g + 256, 512 * g + 512
                dq = (_dot(kt_lo[g], ds[:, c0:c1]) + _dot(kt_hi[g], ds[:, c1:c2])).T
                for i in range(2):
                    ls = slice((2 * g + i) * LANES, (2 * g + i + 1) * LANES)
                    dsq_ref[rows, ls] = _rope_t(dq[i * SWA_BLOCK:(i + 1) * SWA_BLOCK] * 0.125, cosb, sinb,
                                                first_half).astype(dsq_ref.dtype)
                q_split = jnp.concatenate([jnp.where(lo1s, qg32[g], 0.0), jnp.where(lo1s, 0.0, qg32[g])], axis=0).astype(BF)
                do_split = jnp.concatenate([jnp.where(lo1s, dog32[g], 0.0), jnp.where(lo1s, 0.0, dog32[g])], axis=0).astype(BF)
                dk_g.append(_dot(ds[:, c0:c2], q_split))
                dv_g.append(_dot(prb[:, c0:c2], do_split))
            dk = home(dk_g[0], dk_g[1])
            dv = home(dv_g[0], dv_g[1])
            cur = pl.ds(pl.multiple_of(blk * SWA_BLOCK, SWA_BLOCK), SWA_BLOCK)
            dsk_ref[cur, :] = _rope_t(dk[SWA_BLOCK:], cosb, sinb, first_half)
            dsv_ref[cur, :] = dv[SWA_BLOCK:]
            dk_prev = _rope_t(dk[:SWA_BLOCK], cp_, sp_, first_half)
            dv_prev = dv[:SWA_BLOCK]
            if j == 0:
                @pl.when(n > 0)
                def _():
                    prv = pl.ds(pl.multiple_of((blk - 1) * SWA_BLOCK, SWA_BLOCK), SWA_BLOCK)
                    dsk_ref[prv, :] += dk_prev
                    dsv_ref[prv, :] += dv_prev
            else:
                prv = pl.ds(pl.multiple_of((blk - 1) * SWA_BLOCK, SWA_BLOCK), SWA_BLOCK)
                dsk_ref[prv, :] += dk_prev
                dsv_ref[prv, :] += dv_prev
            kp, vp, cp_, sp_ = kc, vc, cosb, sinb
        kprev[...] = kp
        vprev[...] = vp
        cprev[...] = cp_
        sprev[...] = sp_

    def col(width, off):
        return pl.BlockSpec((tq, width), lambda i: (i, off // width))

    row = pl.BlockSpec((tq, LANES), lambda i: (i, 0))
    wide = pl.BlockSpec((tq, 512), lambda i: (i, 0))
    return pl.pallas_call(
        body, name="swa_bwd", grid=(s // tq,),
        in_specs=[col(512, OFF_SQ), col(512, OFF_SZ), col(LANES, OFF_SK), col(LANES, OFF_SV), wide, wide, row, row,
                  pl.BlockSpec(memory_space=pltpu.SMEM)],
        out_specs=[wide, wide, _full((s, LANES)), _full((s, LANES)), pl.BlockSpec(memory_space=pltpu.SMEM)],
        out_shape=[jax.ShapeDtypeStruct((s, 512), BF), jax.ShapeDtypeStruct((s, 512), BF),
                   jax.ShapeDtypeStruct((s, LANES), F32), jax.ShapeDtypeStruct((s, LANES), F32),
                   jax.ShapeDtypeStruct((1, SWA_HEADS), F32)],
        scratch_shapes=[pltpu.VMEM((SWA_BLOCK, LANES), F32)] * 4,
        compiler_params=_params(("arbitrary",)),
    )(proj, proj, proj, proj, dos, opre, cos, sin, sinks)


def _outproj(og, osw, w_out, x2d, target, gate, g_final):
    s = x2d.shape[0]
    tm = min(512, s)

    def body(og_ref, os_ref, w_ref, x_ref, t_ref, gate_ref, gf_ref,
             dx2_ref, dog_ref, dos_ref, dw_ref, loss_ref, dgf_ref, dgate_ref):
        @pl.when(pl.program_id(0) == 0)
        def _():
            dw_ref[...] = jnp.zeros_like(dw_ref)
            loss_ref[...] = jnp.zeros_like(loss_ref)
            dgf_ref[...] = jnp.zeros_like(dgf_ref)
            dgate_ref[...] = jnp.zeros_like(dgate_ref)

        w = w_ref[...]
        gate, gf = gate_ref[...], gf_ref[...]
        subs = _subtiles(tm)
        ogv = [og_ref[sl, :] for sl in subs]
        osv = [os_ref[sl, :] for sl in subs]
        y = [_dot(ogv[k], w[:512]) + _dot(osv[k], w[512:]) for k in range(len(subs))]
        dys = []
        for k, sl in enumerate(subs):
            x2 = x_ref[sl, :] + gate * y[k]
            r = lax.rsqrt(jnp.mean(x2 * x2, axis=-1, keepdims=True) + RMS_EPS)
            xn = x2 * r
            err = xn * gf - t_ref[sl, :]
            loss_ref[...] += 0.5 * jnp.sum(jnp.mean(err * err, axis=-1, keepdims=True), axis=0, keepdims=True)
            dyf = err * (1.0 / D_MODEL)
            dgf_ref[...] += jnp.sum(dyf * xn, axis=0, keepdims=True)
            t = dyf * gf
            dx2 = r * (t - xn * jnp.mean(t * xn, axis=-1, keepdims=True))
            dx2_ref[sl, :] = dx2
            dgate_ref[...] += jnp.sum(dx2 * y[k], axis=0, keepdims=True)
            dys.append((dx2 * gate).astype(BF))
            dmix = _dot(dys[k], w, NT)
            dog_ref[sl, :] = dmix[:, :512]
            dos_ref[sl, :] = dmix[:, 512:]
        dy = jnp.concatenate(dys, axis=0)
        dw_ref[:512, :] += _dot(og_ref[...], dy, TN)
        dw_ref[512:, :] += _dot(os_ref[...], dy, TN)

    half = pl.BlockSpec((tm, 512), lambda i: (i, 0))
    rowb = pl.BlockSpec((tm, D_MODEL), lambda i: (i, 0))
    vec = _full((1, D_MODEL))
    return pl.pallas_call(
        body, name="outproj", grid=(s // tm,),
        in_specs=[half, half, _full((D_MODEL, D_MODEL)), rowb, rowb, vec, vec],
        out_specs=[rowb, half, half, _full((D_MODEL, D_MODEL)), _full((1, 1)), vec, vec],
        out_shape=[jax.ShapeDtypeStruct((s, D_MODEL), F32), jax.ShapeDtypeStruct((s, 512), F32),
                   jax.ShapeDtypeStruct((s, 512), F32), jax.ShapeDtypeStruct((D_MODEL, D_MODEL), F32),
                   jax.ShapeDtypeStruct((1, 1), F32), jax.ShapeDtypeStruct((1, D_MODEL), F32),
                   jax.ShapeDtypeStruct((1, D_MODEL), F32)],
        compiler_params=_params(("arbitrary",)),
    )(og, osw, w_out, x2d, target, gate, g_final)


_PIECES = ((OFF_QK, 512), (OFF_V, 512), (OFF_GZ, 512), (OFF_SQ, 512), (OFF_SZ, 512),
           (OFF_SK, LANES), (OFF_SV, LANES), (OFF_GA, LANES))

_UNPAD_ROWS = ((OFF_QK, 0, 1024),
               (OFF_GA, 1024, GLA_RANK),
               (OFF_GZ, 1040, 1024),
               (OFF_SK, 2064, 256),
               (OFF_SZ, 2320, 512))


def _inproj_bwd(x2d, shift, sc1p, g_norm, wpad_t, dx2, pieces):
    s = x2d.shape[0]
    tm = min(512, s)
    nsteps = s // tm

    def body(x_ref, sh_ref, sc_ref, g_ref, w_hbm, dx2_ref, *rest):
        piece_refs = rest[:len(_PIECES)]
        gx_ref, dw_hbm, dsh_ref, dsc_ref, dg_ref, w_vm, dw_vm, sem, out_sems = rest[len(_PIECES):]
        i = pl.program_id(0)

        @pl.when(i == 0)
        def _():
            cp = pltpu.make_async_copy(w_hbm, w_vm, sem)
            cp.start()
            dw_vm[...] = jnp.zeros_like(dw_vm)
            dsh_ref[...] = jnp.zeros_like(dsh_ref)
            dsc_ref[...] = jnp.zeros_like(dsc_ref)
            dg_ref[...] = jnp.zeros_like(dg_ref)
            cp.wait()

        g, sc1p_v, shift_v = g_ref[...], sc_ref[...], sh_ref[...]
        subs = _subtiles(tm)
        dhs = []
        for sl in subs:
            dh = None
            for (off, width), pr in zip(_PIECES, piece_refs):
                part = _dot(pr[sl, :].astype(BF), w_vm[off:off + width, :])
                dh = part if dh is None else dh + part
            dhs.append(dh)
        norm = [_modnorm(x_ref[sl, :], g, sc1p_v, shift_v) for sl in subs]
        hb = jnp.concatenate([h.astype(BF) for _, _, h in norm], axis=0)
        for (off, width), pr in zip(_PIECES, piece_refs):
            dw_vm[off:off + width, :] += _dot(pr[...].astype(BF), hb, TN)
        for sl, (xn, r, _), dh in zip(subs, norm, dhs):
            dsh_ref[...] += jnp.sum(dh, axis=0, keepdims=True)
            dsc_ref[...] += jnp.sum(dh * (xn * g), axis=0, keepdims=True)
            dg_ref[...] += jnp.sum(dh * xn * sc1p_v, axis=0, keepdims=True)
            dxn = dh * g * sc1p_v
            gx_ref[sl, :] = dx2_ref[sl, :] + r * (dxn - xn * jnp.mean(dxn * xn, axis=-1, keepdims=True))

        @pl.when(i == nsteps - 1)
        def _():
            copies = [pltpu.make_async_copy(dw_vm.at[src:src + n], dw_hbm.at[dst:dst + n], out_sems.at[k])
                      for k, (src, dst, n) in enumerate(_UNPAD_ROWS)]
            for cp in copies:
                cp.start()
            for cp in copies:
                cp.wait()

    rowb = pl.BlockSpec((tm, D_MODEL), lambda i: (i, 0))
    vec = _full((1, D_MODEL))
    anyspec = pl.BlockSpec(memory_space=pl.ANY)
    piece_specs = [pl.BlockSpec((tm, width), lambda i: (i, 0)) for _, width in _PIECES]
    return pl.pallas_call(
        body, name="inproj_bwd", grid=(nsteps,),
        in_specs=[rowb, vec, vec, vec, anyspec, rowb] + piece_specs,
        out_specs=[rowb, anyspec, vec, vec, vec],
        out_shape=[jax.ShapeDtypeStruct((s, D_MODEL), F32), jax.ShapeDtypeStruct((D_IN, D_MODEL), F32),
                   jax.ShapeDtypeStruct((1, D_MODEL), F32), jax.ShapeDtypeStruct((1, D_MODEL), F32),
                   jax.ShapeDtypeStruct((1, D_MODEL), F32)],
        scratch_shapes=[pltpu.VMEM((D_PAD, D_MODEL), BF), pltpu.VMEM((D_PAD, D_MODEL), F32), pltpu.SemaphoreType.DMA,
                        pltpu.SemaphoreType.DMA((len(_UNPAD_ROWS),))],
        compiler_params=_params(("arbitrary",)),
    )(x2d, shift, sc1p, g_norm, wpad_t, dx2, *pieces)


def _adam(w, g, m, v):
    m2 = ADAM_B1 * m + (1.0 - ADAM_B1) * g
    v2 = ADAM_B2 * v + (1.0 - ADAM_B2) * (g * g)
    m_hat = m2 / (1.0 - ADAM_B1 ** ADAM_STEP)
    v_hat = v2 / (1.0 - ADAM_B2 ** ADAM_STEP)
    delta = -ADAM_LR * (m_hat / (jnp.sqrt(v_hat) + ADAM_EPS) + ADAM_WD * w)
    return delta, m2, v2


def _adamw(w, g, m, v, name):
    rr, cc = w.shape
    tc = min(256, cc)

    def body(w_ref, g_ref, m_ref, v_ref, d_ref, m2_ref, v2_ref):
        d_ref[...], m2_ref[...], v2_ref[...] = _adam(w_ref[...], g_ref[...], m_ref[...], v_ref[...])

    blk = pl.BlockSpec((rr, tc), lambda i: (0, i))
    return pl.pallas_call(
        body, name=name, grid=(cc // tc,), in_specs=[blk] * 4, out_specs=[blk] * 3,
        out_shape=[jax.ShapeDtypeStruct((rr, cc), F32)] * 3,
        compiler_params=_params(("arbitrary",)),
    )(w, g, m, v)


def _adamw_t(w3, g, m3, v3, name):
    rr, _, cc = w3.shape
    tc = min(256, cc)

    def body(w_ref, g_ref, m_ref, v_ref, d_ref, m2_ref, v2_ref, g3_ref):
        g = g_ref[...]
        d_ref[:, 0, :], m2_ref[:, 0, :], v2_ref[:, 0, :] = _adam(w_ref[:, 0, :], g, m_ref[:, 0, :], v_ref[:, 0, :])
        g3_ref[:, 0, :] = g

    b3 = pl.BlockSpec((rr, 1, tc), lambda i: (0, 0, i))
    return pl.pallas_call(
        body, name=name, grid=(cc // tc,), in_specs=[b3, pl.BlockSpec((rr, tc), lambda i: (0, i)), b3, b3],
        out_specs=[b3] * 4, out_shape=[jax.ShapeDtypeStruct((rr, 1, cc), F32)] * 4,
        compiler_params=_params(("arbitrary",)),
    )(w3, g, m3, v3)


def _ada_update(c_all, dmod_cols, w, m, v):
    rr, cc = w.shape
    tr = min(256, rr)
    c_all = jnp.pad(c_all, ((0, 8), (0, 0)))
    dmod_cols = jnp.pad(dmod_cols, ((0, 8), (0, 0)))

    def body(c_ref, dm_ref, w_ref, m_ref, v_ref, g_ref, d_ref, m2_ref, v2_ref):
        cv = c_ref[...]
        sc = (cv * _sigmoid(cv)).astype(BF)
        g = _dot(sc, dm_ref[...].astype(BF), TN)
        g_ref[...] = g
        d_ref[...], m2_ref[...], v2_ref[...] = _adam(w_ref[...], g, m_ref[...], v_ref[...])

    blk = pl.BlockSpec((tr, cc), lambda i: (i, 0))
    return pl.pallas_call(
        body, name="ada_update", grid=(rr // tr,),
        in_specs=[pl.BlockSpec((16, tr), lambda i: (0, i)), _full((16, cc)), blk, blk, blk],
        out_specs=[blk] * 4, out_shape=[jax.ShapeDtypeStruct((rr, cc), F32)] * 4,
        compiler_params=_params(("arbitrary",)),
    )(c_all, dmod_cols, w, m, v)


def _small_update(parts, weights, moms, vels):
    n = len(weights)

    def body(*refs):
        p_refs, w_refs, m_refs, v_refs = refs[:n + 1], refs[n + 1:2 * n + 1], refs[2 * n + 1:3 * n + 1], refs[3 * n + 1:4 * n + 1]
        outs = refs[4 * n + 1:]
        for i in range(n):
            g = p_refs[i][0]
            for d in range(1, 8):
                g = g + p_refs[i][d]
            delta, m2, v2 = _adam(w_refs[i][...], g, m_refs[i][...], v_refs[i][...])
            outs[4 * i][...] = g
            outs[4 * i + 1][...] = delta
            outs[4 * i + 2][...] = m2
            outs[4 * i + 3][...] = v2
        tot = p_refs[n][0]
        for d in range(1, 8):
            tot = tot + p_refs[n][d]
        outs[4 * n][...] = tot

    out_shape = []
    for w in weights:
        out_shape += [jax.ShapeDtypeStruct(w.shape, F32)] * 4
    out_shape.append(jax.ShapeDtypeStruct(parts[n].shape[1:], F32))
    return pl.pallas_call(body, name="small_update", out_shape=out_shape, compiler_params=_params())(
        *parts, *weights, *moms, *vels)


def _pad_w_in_t(w):
    pad = jnp.zeros((LANES - GLA_RANK, w.shape[1]), w.dtype)
    return jnp.concatenate([w[dst:dst + n] for _, dst, n in sorted(_UNPAD_ROWS)] + [pad], axis=0)


def _rows8(a):
    flat = a.reshape(-1)
    rows = -(-flat.shape[0] // LANES)
    rows8 = -(-rows // 8) * 8
    flat = jnp.pad(flat, (0, rows8 * LANES - flat.shape[0]))
    return flat.reshape(rows8, LANES)


def kernel(x, c, positions, w_ada, b_ada, g_norm, w_in, w_decay, b_decay, g_gla_head, sinks, w_out, g_final, loss_target, m_w_ada, m_b_ada, m_g_norm, m_w_in, m_w_decay, m_b_decay, m_g_gla_head, m_sinks, m_w_out, m_g_final, v_w_ada, v_b_ada, v_g_norm, v_w_in, v_w_decay, v_b_decay, v_g_gla_head, v_sinks, v_w_out, v_g_final):
    ax, ay, ac = lax.axis_index("x"), lax.axis_index("y"), lax.axis_index("c")
    chip = 2 * ax + ay
    dev = 2 * chip + ac
    s = x.shape[1]
    x2d = x[0]
    target = loss_target[0]
    w_ada2, w_out2, w_dec2 = w_ada[0], w_out[0], w_decay[0]
    w_in_t = w_in[0].T
    ada_cols = w_ada2.shape[1]
    in_cols = w_in_t.shape[0]
    out_rows = w_out2.shape[0]
    half = D_MODEL // 2

    cw = jnp.concatenate([c.reshape(8, LANES), w_dec2.reshape(8, LANES)], axis=0)
    b_shard = lax.dynamic_slice(b_ada, (0, chip * ada_cols), (1, ada_cols))
    half_in = lax.dynamic_slice(w_in_t, (0, ac * half), (in_cols, half)).astype(BF)
    half_out = lax.dynamic_slice(w_out2, (ac * (out_rows // 2), 0), (out_rows // 2, D_MODEL)).astype(BF)
    inv_freq = 1.0 / (ROPE_THETA ** (jnp.arange(0, 64, 2, dtype=F32) / 64))
    first, mod_all, w_in_all, w_out_all, cos, sin = _prologue(
        cw, w_ada2, b_shard, half_in, half_out, positions.reshape(s, 1), jnp.tile(inv_freq, 4).reshape(1, LANES))

    first = first.reshape(8, 2, 8, LANES)
    c_all = first[:, 0].reshape(8, D_MODEL)
    w_dec_full = first[0::2, 1].reshape(4, GLA_RANK, 64).transpose(1, 0, 2).reshape(GLA_RANK, 256)
    mod = mod_all.reshape(4, 2, 8, ada_cols)[:, 0]
    mod = lax.dynamic_slice(mod, (0, dev, 0), (4, 1, ada_cols)).reshape(1, 4 * ada_cols)
    shift, sc1p, gate = mod[:, :D_MODEL], 1.0 + mod[:, D_MODEL:2 * D_MODEL], mod[:, 2 * D_MODEL:]
    w_in_all = w_in_all.reshape(4, 2, in_cols, half)
    wpad_t = _pad_w_in_t(w_in_all.transpose(0, 2, 1, 3).reshape(4 * in_cols, D_MODEL))
    w_out_all = w_out_all.reshape(D_MODEL, D_MODEL)

    wdecp = jnp.pad(w_dec_full, ((0, LANES - GLA_RANK), (0, 0))).astype(BF)

    proj = _inproj_fwd(x2d, shift, sc1p, g_norm, wpad_t)
    og, o_gla, sprev = _gla_fwd(proj, wdecp, b_decay, g_gla_head)
    osw, o_swa = _swa_fwd(proj, cos, sin, sinks)
    dx2, dog, dos, dw_out, loss_p, dgf, dgate = _outproj(og, osw, w_out_all, x2d, target, gate, g_final.reshape(1, D_MODEL))
    dsq, dsz, dsk, dsv, dsinks = _swa_bwd(proj, dos, o_swa, cos, sin, sinks)
    dqk, dv, dgz, dga, dwdp, dbd, dgg = _gla_bwd(proj, dog, o_gla, sprev, wdecp, b_decay, g_gla_head)
    pieces = (dqk, dv, dgz, dsq, dsz, dsk, dsv, dga)
    gx, dw_in_t, dshift, dscale, dgn = _inproj_bwd(x2d, shift, sc1p, g_norm, wpad_t, dx2, pieces)

    segs = [jnp.concatenate([dshift, dscale, dgate], axis=1), dgn, dgf, dwdp[:GLA_RANK], dbd, dgg, dsinks, loss_p]
    packed = [_rows8(a) for a in segs]
    offs = [0]
    for a in packed:
        offs.append(offs[-1] + a.shape[0])
    g_w_in_t, g_w_out, small = _epilogue(dw_in_t.reshape(4, in_cols, D_MODEL), dw_out.reshape(4, out_rows, D_MODEL),
                                         jnp.concatenate(packed, axis=0))

    def seg(i, size):
        return small[:, offs[i]:offs[i + 1]].reshape(8, -1)[:, :size]

    dmod_all = seg(0, 3 * D_MODEL)
    dwd_all = lax.dynamic_slice(seg(3, GLA_RANK * 256).reshape(8, GLA_RANK, 256), (0, 0, chip * 64), (8, GLA_RANK, 64))
    parts = [dmod_all.reshape(8, 1, 3 * D_MODEL), seg(1, D_MODEL).reshape(8, 1, D_MODEL), dwd_all,
             seg(4, 256).reshape(8, 1, 256), seg(5, 512).reshape(8, 1, 512), seg(6, SWA_HEADS).reshape(8, 1, SWA_HEADS),
             seg(2, D_MODEL).reshape(8, 1, D_MODEL), seg(7, LANES).reshape(8, 1, LANES)]
    smalls = _small_update(
        parts,
        [b_ada, g_norm, w_dec2, b_decay, g_gla_head, sinks, g_final.reshape(1, D_MODEL)],
        [m_b_ada, m_g_norm, m_w_decay[0], m_b_decay, m_g_gla_head, m_sinks, m_g_final.reshape(1, D_MODEL)],
        [v_b_ada, v_g_norm, v_w_decay[0], v_b_decay, v_g_gla_head, v_sinks, v_g_final.reshape(1, D_MODEL)])
    (g_b_ada, d_b_ada, nm_b_ada, nv_b_ada, g_gn, d_gn, nm_gn, nv_gn, g_wd, d_wd, nm_wd, nv_wd,
     g_bd, d_bd, nm_bd, nv_bd, g_gg, d_gg, nm_gg, nv_gg, g_sk, d_sk, nm_sk, nv_sk,
     g_gf, d_gf, nm_gf, nv_gf, loss_row) = smalls
    loss = loss_row[0, 0]

    dmod_cols = lax.dynamic_slice(dmod_all, (0, chip * ada_cols), (8, ada_cols))
    g_w_ada, d_w_ada, nm_w_ada, nv_w_ada = _ada_update(c_all, dmod_cols, w_ada2, m_w_ada[0], v_w_ada[0])
    to3 = lambda a: jnp.transpose(a, (2, 0, 1))
    from3 = lambda a: jnp.transpose(a, (1, 2, 0))[0]
    d3, nm3, nv3, g3 = _adamw_t(to3(w_in), g_w_in_t, to3(m_w_in), to3(v_w_in), "adamw_w_in")
    g_w_in, d_w_in, nm_w_in, nv_w_in = from3(g3), from3(d3), from3(nm3), from3(nv3)
    d_w_out, nm_w_out, nv_w_out = _adamw(w_out2, g_w_out, m_w_out[0], v_w_out[0], "adamw_w_out")

    flat = lambda a: a.reshape(D_MODEL)
    grads = [g_w_ada[None], g_b_ada, g_gn, g_w_in[None], g_wd[None], g_bd, g_gg, g_sk, g_w_out[None], flat(g_gf)]
    deltas = [d_w_ada[None], d_b_ada, d_gn, d_w_in[None], d_wd[None], d_bd, d_gg, d_sk, d_w_out[None], flat(d_gf)]
    new_m = [nm_w_ada[None], nm_b_ada, nm_gn, nm_w_in[None], nm_wd[None], nm_bd, nm_gg, nm_sk, nm_w_out[None], flat(nm_gf)]
    new_v = [nv_w_ada[None], nv_b_ada, nv_gn, nv_w_in[None], nv_wd[None], nv_bd, nv_gg, nv_sk, nv_w_out[None], flat(nv_gf)]
    return (loss, gx[None], *grads, *deltas, *new_m, *new_v)
```

```python
import jax
import jax.numpy as jnp
from jax import lax
from jax.experimental import pallas as pl
from jax.experimental.pallas import tpu as pltpu

F32 = jnp.float32
BF = jnp.bfloat16

D_MODEL = 1024
GLA_HEADS = 4
GLA_DK = 64
GLA_CHUNK = 64
GLA_RANK = 16
GLA_TAU = 16.0
GLA_SUB = 256
GLA_ROWS = 512
SWA_HEADS = 8
SWA_BLOCK = 128
SWA_QBLOCKS = 8
RMS_EPS = 1e-6
ROPE_THETA = 10000.0

OFF_QK, OFF_V, OFF_GZ, OFF_SQ, OFF_SZ, OFF_SK, OFF_SV, OFF_GA = 0, 512, 1024, 1536, 2048, 2560, 2688, 2816
D_PAD = 2944
D_IN = 2832
LANES = 128
VMEM_LIMIT = 56 * 1024 * 1024

ADAM_LR, ADAM_B1, ADAM_B2, ADAM_EPS, ADAM_WD, ADAM_STEP = 0.001, 0.9, 0.999, 1e-08, 0.01, 10

NT = (((1,), (1,)), ((), ()))
TN = (((0,), (0,)), ((), ()))
MESH = pl.DeviceIdType.MESH


def _dot(a, b, dims=None):
    if dims is None:
        return jnp.dot(a, b, preferred_element_type=F32)
    return lax.dot_general(a, b, dims, preferred_element_type=F32)


def _sigmoid(x):
    return 1.0 / (1.0 + jnp.exp(-x))


def _params(sem=None):
    return pltpu.CompilerParams(dimension_semantics=sem, vmem_limit_bytes=VMEM_LIMIT)


def _full(shape):
    return pl.BlockSpec(shape, lambda i: (0,) * len(shape))


def _subtiles(rows, size=256):
    size = min(size, rows)
    return [slice(k * size, (k + 1) * size) for k in range(rows // size)]


_GATHER_SEMS = [pltpu.SemaphoreType.DMA((7,)), pltpu.SemaphoreType.DMA((7,)), pltpu.SemaphoreType.DMA]


class _Gather:
    def __init__(self, x_ref, out_ref, send_sems, recv_sems, local_sem):
        x, y, c = lax.axis_index("x"), lax.axis_index("y"), lax.axis_index("c")
        self.me, self.sibling, self.c = (x, y, c), (x, y, 1 - c), c
        self.xn, self.yn, self.dg = (1 - x, y), (x, 1 - y), (1 - x, 1 - y)
        self.pass_from = (lax.rem(x + 1 - c, 2), lax.rem(y + c, 2))
        self.pass_to = (lax.rem(x + c, 2), lax.rem(y + 1 - c, 2))
        self.x_ref, self.out_ref, self.send_sems, self.recv_sems = x_ref, out_ref, send_sems, recv_sems
        self.mine = pltpu.make_async_copy(x_ref, self._slab(*self.me), local_sem)

    def _slab(self, px, py, pc):
        return self.out_ref.at[4 * px + 2 * py + pc]

    def _copy(self, k, blk, to, src=None):
        return pltpu.make_async_remote_copy(
            src_ref=self._slab(*blk) if src is None else src, dst_ref=self._slab(*blk),
            send_sem=self.send_sems.at[k], recv_sem=self.recv_sems.at[k], device_id=to, device_id_type=MESH)

    def _sends(self):
        c = self.c
        return [self._copy(0, self.me, self.sibling, src=self.x_ref),
                self._copy(1, self.me, (*self.xn, c), src=self.x_ref),
                self._copy(2, self.me, (*self.yn, c), src=self.x_ref),
                self._copy(3, (*self.pass_from, c), (*self.pass_to, c)),
                self._copy(4, (*self.xn, c), self.sibling),
                self._copy(5, (*self.yn, c), self.sibling),
                self._copy(6, (*self.dg, c), self.sibling)]

    def start(self):
        self.mine.start()
        for cp in self._sends()[0:3]:
            cp.start()

    def relay(self):
        sends = self._sends()
        self._copy(1, (*self.xn, self.c), self.me).wait_recv()
        self._copy(2, (*self.yn, self.c), self.me).wait_recv()
        for k in (3, 4, 5):
            sends[k].start()
        self._copy(3, (*self.dg, self.c), self.me).wait_recv()
        sends[6].start()

    def finish(self):
        c = self.c
        self._copy(0, self.sibling, self.me).wait_recv()
        for k, chip in ((4, self.xn), (5, self.yn), (6, self.dg)):
            self._copy(k, (*chip, 1 - c), self.me).wait_recv()
        for cp in self._sends():
            cp.wait_send()
        self.mine.wait()


def _prologue(cw, w_ada, b_shard, half_in, half_out, pos_col, inv_freq):
    s = pos_col.shape[0]
    rt = min(512, s)

    def body(cw_ref, wada_ref, b_ref, hin_ref, hout_ref, pos_ref, f_ref,
             first_ref, mod_ref, win_ref, wout_ref, cos_ref, sin_ref, mod_blk, *sems):
        g_c = _Gather(cw_ref, first_ref, *sems[0:3])
        g_in = _Gather(hin_ref, win_ref, *sems[3:6])
        g_out = _Gather(hout_ref, wout_ref, *sems[6:9])
        g_mod = _Gather(mod_blk, mod_ref, *sems[9:12])
        g_c.start()
        g_in.start()
        g_out.start()
        g_c.relay()
        g_c.finish()
        c_rows = [jnp.concatenate([first_ref[d, r:r + 1, :] for r in range(8)], axis=1) for d in range(8)]
        c_all = jnp.concatenate(c_rows, axis=0)
        sc = (c_all * _sigmoid(c_all)).astype(BF)
        mod_blk[...] = _dot(sc, wada_ref[...].astype(BF)) + b_ref[...]
        g_mod.start()

        def rope_rows(i, carry):
            rows = pl.ds(pl.multiple_of(i * rt, rt), rt)
            ang = pos_ref[rows, :].astype(F32) * f_ref[...]
            lane = lax.broadcasted_iota(jnp.int32, ang.shape, 1)
            cos_ref[rows, :] = jnp.cos(ang)
            sn = jnp.sin(ang)
            sin_ref[rows, :] = jnp.where((lane % 64) < 32, -sn, sn)
            return carry

        lax.fori_loop(0, s // rt, rope_rows, 0)
        g_in.relay()
        g_out.relay()
        g_mod.relay()
        g_in.finish()
        g_out.finish()
        g_mod.finish()

    vm = pl.BlockSpec(memory_space=pltpu.VMEM)
    return pl.pallas_call(
        body, name="prologue",
        out_shape=[jax.ShapeDtypeStruct((8,) + cw.shape, F32), jax.ShapeDtypeStruct((8, 8, w_ada.shape[1]), F32),
                   jax.ShapeDtypeStruct((8,) + half_in.shape, half_in.dtype),
                   jax.ShapeDtypeStruct((8,) + half_out.shape, half_out.dtype),
                   jax.ShapeDtypeStruct((s, LANES), F32), jax.ShapeDtypeStruct((s, LANES), F32)],
        in_specs=[vm] * 7, out_specs=[vm] * 6,
        scratch_shapes=[pltpu.VMEM((8, w_ada.shape[1]), F32)] + _GATHER_SEMS * 4,
        compiler_params=pltpu.CompilerParams(vmem_limit_bytes=VMEM_LIMIT),
    )(cw, w_ada, b_shard, half_in, half_out, pos_col, inv_freq)


def _reduce_scratch(rr, cc):
    c2 = cc // 2
    return [pltpu.VMEM((4, rr, c2), F32), pltpu.VMEM((4, rr, c2), F32), pltpu.VMEM((3, rr, c2), BF),
            pltpu.VMEM((2, rr, c2), BF), pltpu.VMEM((rr, c2), BF), pltpu.VMEM((rr, c2), F32),
            pltpu.SemaphoreType.DMA((8,)), pltpu.SemaphoreType.DMA((8,)), pltpu.SemaphoreType.DMA((5,))]


class _Reduce:
    def __init__(self, p_hbm, out_ref, acc_ref, own_ref, send_ref, land_ref, relay_ref, res_ref,
                 send_sems, recv_sems, local_sems):
        x, y, c = lax.axis_index("x"), lax.axis_index("y"), lax.axis_index("c")
        c2 = out_ref.shape[1] // 2
        sibling = (x, y, 1 - c)
        first = (lax.rem(x + 1 - c, 2), lax.rem(y + c, 2))
        second = (lax.rem(x + c, 2), lax.rem(y + 1 - c, 2))
        shards = [2 * first[0] + first[1], 2 * second[0] + second[1], 2 * (1 - x) + (1 - y), 2 * x + y]
        sibling_slot = (1, 0, 2, 3)
        mine = pl.ds(pl.multiple_of(c * c2, c2), c2)
        other = pl.ds(pl.multiple_of((1 - c) * c2, c2), c2)
        self.acc_ref, self.own_ref, self.send_ref, self.land_ref = acc_ref, own_ref, send_ref, land_ref
        self.relay_ref, self.res_ref = relay_ref, res_ref
        self.own = [pltpu.make_async_copy(p_hbm.at[j, :, mine], own_ref.at[k], local_sems.at[k])
                    for k, j in enumerate(shards)]
        self.swap_out = [pltpu.make_async_remote_copy(
            src_ref=p_hbm.at[j, :, other], dst_ref=acc_ref.at[sibling_slot[k]], send_sem=send_sems.at[k],
            recv_sem=recv_sems.at[sibling_slot[k]], device_id=sibling, device_id_type=MESH) for k, j in enumerate(shards)]
        self.swap_in = [pltpu.make_async_remote_copy(
            src_ref=p_hbm.at[j, :, other], dst_ref=acc_ref.at[k], send_sem=send_sems.at[k], recv_sem=recv_sems.at[k],
            device_id=sibling, device_id_type=MESH) for k, j in enumerate(shards)]

        def message(k, src, dst, to):
            return pltpu.make_async_remote_copy(src_ref=src, dst_ref=dst, send_sem=send_sems.at[k], recv_sem=recv_sems.at[k],
                                                device_id=(*to, c), device_id_type=MESH)

        self.direct = message(4, send_ref.at[0], land_ref.at[0], first)
        self.passed = message(5, send_ref.at[1], relay_ref, first)
        self.joint = message(6, send_ref.at[2], land_ref.at[1], second)
        self.put = pltpu.make_async_copy(res_ref, out_ref.at[:, mine], local_sems.at[4])
        self.share = pltpu.make_async_remote_copy(
            src_ref=res_ref, dst_ref=out_ref.at[:, mine], send_sem=send_sems.at[7],
            recv_sem=recv_sems.at[7], device_id=sibling, device_id_type=MESH)

    def start(self):
        for k in (0, 2, 1, 3):
            self.own[k].start()
            self.swap_out[k].start()

    def _combine(self, k):
        self.own[k].wait()
        self.swap_out[k].wait_send()
        self.swap_in[k].wait_recv()
        self.acc_ref[k] = self.acc_ref[k] + self.own_ref[k]

    def combine_and_send(self):
        dt = self.send_ref.dtype
        self._combine(0)
        self.send_ref[0] = self.acc_ref[0].astype(dt)
        self.direct.start()
        self._combine(2)
        self.send_ref[1] = self.acc_ref[2].astype(dt)
        self.passed.start()
        self._combine(1)
        self.passed.wait_recv()
        self.send_ref[2] = (self.acc_ref[1] + self.relay_ref[...].astype(F32)).astype(dt)
        self.joint.start()
        self._combine(3)

    def total_and_share(self):
        self.direct.wait_recv()
        self.joint.wait_recv()
        self.res_ref[...] = self.acc_ref[3] + self.land_ref[0].astype(F32) + self.land_ref[1].astype(F32)
        for cp in (self.direct, self.passed, self.joint):
            cp.wait_send()
        self.put.start()
        self.share.start()

    def finish(self):
        self.put.wait()
        self.share.wait()


def _epilogue(dw_in_parts, dw_out_parts, small):
    _, r_in, cc = dw_in_parts.shape
    _, r_out, _ = dw_out_parts.shape
    n_red = len(_reduce_scratch(r_in, cc))

    def body(pin_hbm, pout_hbm, small_ref, gin_ref, gout_ref, small_all_ref, *scratch):
        red_in = _Reduce(pin_hbm, gin_ref, *scratch[0:n_red])
        red_out = _Reduce(pout_hbm, gout_ref, *scratch[n_red:2 * n_red])
        gat = _Gather(small_ref, small_all_ref, *scratch[2 * n_red:])
        red_out.start()
        red_in.start()
        gat.start()
        red_out.combine_and_send()
        red_in.combine_and_send()
        gat.relay()
        red_out.total_and_share()
        red_in.total_and_share()
        gat.finish()
        red_out.finish()
        red_in.finish()

    vm = pl.BlockSpec(memory_space=pltpu.VMEM)
    anyspec = pl.BlockSpec(memory_space=pl.ANY)
    return pl.pallas_call(
        body, name="epilogue",
        out_shape=[jax.ShapeDtypeStruct((r_in, cc), F32), jax.ShapeDtypeStruct((r_out, cc), F32),
                   jax.ShapeDtypeStruct((8,) + small.shape, F32)],
        in_specs=[anyspec, anyspec, vm], out_specs=[vm, vm, vm],
        scratch_shapes=_reduce_scratch(r_in, cc) + _reduce_scratch(r_out, cc) + _GATHER_SEMS,
        compiler_params=pltpu.CompilerParams(vmem_limit_bytes=VMEM_LIMIT),
    )(dw_in_parts, dw_out_parts, small)


def _rope(t, cosb, sinb, first_half):
    partner = jnp.where(first_half, pltpu.roll(t, 96, 1), pltpu.roll(t, 32, 1))
    return t * cosb + partner * sinb


def _rope_t(g, cosb, sinb, first_half):
    gs = g * sinb
    partner = jnp.where(first_half, pltpu.roll(gs, 96, 1), pltpu.roll(gs, 32, 1))
    return g * cosb + partner


def _modnorm(x, g, sc1p, shift):
    r = lax.rsqrt(jnp.mean(x * x, axis=-1, keepdims=True) + RMS_EPS)
    xn = x * r
    return xn, r, (xn * g) * sc1p + shift


def _inproj_fwd(x2d, shift, sc1p, g_norm, wpad_t):
    s = x2d.shape[0]
    tm = min(512, s)

    def body(x_ref, sh_ref, sc_ref, g_ref, w_ref, o_ref):
        subs = _subtiles(tm)
        hs = [_modnorm(x_ref[sl, :], g_ref[...], sc_ref[...], sh_ref[...])[2].astype(BF) for sl in subs]
        for sl, h in zip(subs, hs):
            o_ref[sl, :] = _dot(h, w_ref[...], NT)

    vec = _full((1, D_MODEL))
    return pl.pallas_call(
        body, name="inproj_fwd", grid=(s // tm,),
        in_specs=[pl.BlockSpec((tm, D_MODEL), lambda i: (i, 0)), vec, vec, vec, _full((D_PAD, D_MODEL))],
        out_specs=pl.BlockSpec((tm, D_PAD), lambda i: (i, 0)),
        out_shape=jax.ShapeDtypeStruct((s, D_PAD), F32),
        compiler_params=_params(("arbitrary",)),
    )(x2d, shift, sc1p, g_norm, wpad_t)


def _split3(a):
    hi = a.astype(BF)
    r1 = a - hi.astype(F32)
    mid = r1.astype(BF)
    lo = (r1 - mid.astype(F32)).astype(BF)
    return hi, mid, lo


def _tri_matmul(tri, a):
    hi, mid, lo = _split3(a)
    return _dot(tri, hi) + _dot(tri, mid) + _dot(tri, lo)


def _chunks(tb):
    return [slice(c * GLA_CHUNK, (c + 1) * GLA_CHUNK) for c in range(tb // GLA_CHUNK)]


def _per_chunk_rows(rows, width):
    return jnp.concatenate([jnp.broadcast_to(r, (GLA_CHUNK, width)) for r in rows], axis=0)


def _gla_triangle(tb):
    row = lax.broadcasted_iota(jnp.int32, (tb, tb), 0)
    col = lax.broadcasted_iota(jnp.int32, (tb, tb), 1)
    return (((row // GLA_CHUNK) == (col // GLA_CHUNK)) & (col <= row)).astype(F32)


def _lane_mean(x, ones_b):
    hi = x.astype(BF)
    lo = (x - hi.astype(F32)).astype(BF)
    return (_dot(hi, ones_b) + _dot(lo, ones_b)) * (1.0 / LANES)


def _head(t, h, lo_h):
    blk = t[:, LANES * (h // 2):LANES * (h // 2 + 1)]
    return jnp.where(lo_h, blk, 0.0) if h % 2 == 0 else jnp.where(lo_h, 0.0, blk)


def _gla_block_common(qk, ga, wd, bd, tril_b):
    tb = qk.shape[0]
    q, k = qk[:, :256], qk[:, 256:]
    z = _dot(ga.astype(BF), wd) + bd
    la = (jnp.minimum(z, 0.0) - jnp.log(1.0 + jnp.exp(-jnp.abs(z)))) * (1.0 / GLA_TAU)
    b = _tri_matmul(tril_b, la)
    bls = [b[rs.stop - 1:rs.stop, :] for rs in _chunks(tb)]
    eq = jnp.exp(b)
    ek = jnp.exp(-b)
    f = jnp.exp(_per_chunk_rows(bls, 256) - b)
    return z, eq, ek, f, q * (eq * GLA_DK ** -0.5), k * ek, k * f, bls


def _gla_units(s):
    sub = min(GLA_SUB, s)
    tb = min(GLA_ROWS, s)
    subs = [slice(i * sub, (i + 1) * sub) for i in range(tb // sub)]
    units = [(i, h) for i in range(len(subs)) for h in range(GLA_HEADS)]
    return tb, sub, subs, units


def _gla_fwd(proj, wdecp, bdec, ggla):
    s = proj.shape[0]
    tb, sub, subs, units = _gla_units(s)
    nch = sub // GLA_CHUNK

    def body(qk_ref, v_ref, gz_ref, ga_ref, wd_ref, bd_ref, gg_ref, tri_ref, og_ref, opre_ref, sprev_ref, st_ref):
        @pl.when(pl.program_id(0) == 0)
        def _():
            st_ref[...] = jnp.zeros_like(st_ref)

        lo_h = lax.broadcasted_iota(jnp.int32, (sub, LANES), 1) < GLA_DK
        tril = tri_ref[...] > 0.5
        tril_b = tri_ref[...].astype(BF)
        ones_b = jnp.ones((LANES, LANES), BF)
        gg, wd, bd = gg_ref[...], wd_ref[...], bd_ref[...]
        chunks = _chunks(sub)
        lanes = [slice(h * LANES, (h + 1) * LANES) for h in range(GLA_HEADS)]
        com = [_gla_block_common(qk_ref[sl, :], ga_ref[sl, :], wd, bd, tril_b) for sl in subs]
        decs = [[jnp.exp(bl) for bl in cm[7]] for cm in com]
        a = {(i, h): _head(com[i][4], h, lo_h).astype(BF) for i, h in units}
        bm = {(i, h): _head(com[i][5], h, lo_h).astype(BF) for i, h in units}
        ktl = {(i, h): _head(com[i][6], h, lo_h).astype(BF) for i, h in units}
        vh = {(i, h): v_ref[subs[i], lanes[h]].astype(BF) for i, h in units}
        sc = {u: _dot(a[u], bm[u], NT) for u in units}
        upd = {u: [_dot(vh[u][rs], ktl[u][rs], TN) for rs in chunks] for u in units}
        p = {u: jnp.where(tril, sc[u], 0.0).astype(BF) for u in units}
        o = {u: _dot(p[u], vh[u]) for u in units}
        states = {}
        for h in range(GLA_HEADS):
            st = st_ref[h]
            for i in range(len(subs)):
                entering = []
                for c in range(nch):
                    entering.append(st)
                    sprev_ref[i * nch + c, h] = st
                    st = st * decs[i][c][:, LANES * (h // 2):LANES * (h // 2 + 1)] + upd[(i, h)][c]
                states[(i, h)] = entering
            st_ref[h] = st
        inter = {u: [_dot(a[u][rs], states[u][c].astype(BF), NT) for c, rs in enumerate(chunks)] for u in units}
        o = {u: o[u] + jnp.concatenate(inter[u], axis=0) for u in units}
        ms = {u: _lane_mean(o[u] * o[u], ones_b) for u in units}
        for i, h in units:
            gzh = gz_ref[subs[i], lanes[h]]
            opre_ref[subs[i], lanes[h]] = o[(i, h)]
            og_ref[subs[i], lanes[h]] = (((o[(i, h)] * lax.rsqrt(ms[(i, h)] + RMS_EPS)) * gg[:, lanes[h]])
                                         * (gzh * _sigmoid(gzh))).astype(og_ref.dtype)

    def col(width, off):
        return pl.BlockSpec((tb, width), lambda i: (i, off // width))

    return pl.pallas_call(
        body, name="gla_fwd", grid=(s // tb,),
        in_specs=[col(512, OFF_QK), col(512, OFF_V), col(512, OFF_GZ), col(LANES, OFF_GA),
                  _full((LANES, 256)), _full((1, 256)), _full((1, 512)), _full((sub, sub))],
        out_specs=[pl.BlockSpec((tb, 512), lambda i: (i, 0)), pl.BlockSpec((tb, 512), lambda i: (i, 0)),
                   pl.BlockSpec((tb // GLA_CHUNK, GLA_HEADS, LANES, LANES), lambda i: (i, 0, 0, 0))],
        out_shape=[jax.ShapeDtypeStruct((s, 512), BF), jax.ShapeDtypeStruct((s, 512), F32),
                   jax.ShapeDtypeStruct((s // GLA_CHUNK, GLA_HEADS, LANES, LANES), F32)],
        scratch_shapes=[pltpu.VMEM((GLA_HEADS, LANES, LANES), F32)],
        compiler_params=_params(("arbitrary",)),
    )(proj, proj, proj, proj, wdecp, bdec, ggla, _gla_triangle(sub))


def _gla_bwd(proj, dog, opre, sprev, wdecp, bdec, ggla):
    s = proj.shape[0]
    tb, sub, subs, units = _gla_units(s)
    nsub = len(subs)
    nch = sub // GLA_CHUNK
    nb = s // tb

    def body(qk_ref, v_ref, gz_ref, ga_ref, dog_ref, opre_ref, sprev_ref, wd_ref, bd_ref, gg_ref, tri_ref, triu_ref,
             dqk_ref, dv_ref, dgz_ref, dga_ref, dwd_ref, dbd_ref, dgg_ref, dst_ref):
        @pl.when(pl.program_id(0) == 0)
        def _():
            dst_ref[...] = jnp.zeros_like(dst_ref)
            dwd_ref[...] = jnp.zeros_like(dwd_ref)
            dbd_ref[...] = jnp.zeros_like(dbd_ref)
            dgg_ref[...] = jnp.zeros_like(dgg_ref)

        lo_h = lax.broadcasted_iota(jnp.int32, (sub, LANES), 1) < GLA_DK
        tril = tri_ref[...] > 0.5
        tril_b = tri_ref[...].astype(BF)
        triu_b = triu_ref[...].astype(BF)
        ones_b = jnp.ones((LANES, LANES), BF)
        last_row = (lax.broadcasted_iota(jnp.int32, (sub, LANES), 0) % GLA_CHUNK) == GLA_CHUNK - 1
        wd, gg, bd = wd_ref[...], gg_ref[...], bd_ref[...]
        chunks = _chunks(sub)
        lanes = [slice(h * LANES, (h + 1) * LANES) for h in range(GLA_HEADS)]
        blks = [slice(LANES * (h // 2), LANES * (h // 2 + 1)) for h in range(GLA_HEADS)]
        ga = [ga_ref[sl, :] for sl in subs]
        com = [_gla_block_common(qk_ref[sl, :], ga[i], wd, bd, tril_b) for i, sl in enumerate(subs)]
        decs = [[jnp.exp(bl) for bl in cm[7]] for cm in com]
        a = {(i, h): _head(com[i][4], h, lo_h).astype(BF) for i, h in units}
        bm = {(i, h): _head(com[i][5], h, lo_h).astype(BF) for i, h in units}
        ktl = {(i, h): _head(com[i][6], h, lo_h).astype(BF) for i, h in units}
        vh = {(i, h): v_ref[subs[i], lanes[h]].astype(BF) for i, h in units}
        sc = {u: _dot(a[u], bm[u], NT) for u in units}

        o = {(i, h): opre_ref[subs[i], lanes[h]] for i, h in units}
        ms = {u: _lane_mean(o[u] * o[u], ones_b) for u in units}
        gz = {(i, h): gz_ref[subs[i], lanes[h]] for i, h in units}
        dog = {(i, h): dog_ref[subs[i], lanes[h]] for i, h in units}
        sg = {u: _sigmoid(gz[u]) for u in units}
        r = {u: lax.rsqrt(ms[u] + RMS_EPS) for u in units}
        ohat = {u: o[u] * r[u] for u in units}
        sil = {u: gz[u] * sg[u] for u in units}
        for i, h in units:
            u = (i, h)
            dgz_ref[subs[i], lanes[h]] = (dog[u] * (ohat[u] * gg[:, lanes[h]])
                                          * (sg[u] * (1.0 + gz[u] * (1.0 - sg[u])))).astype(dgz_ref.dtype)
            dgg_ref[:, lanes[h]] += jnp.sum(dog[u] * sil[u] * ohat[u], axis=0, keepdims=True)
        dn = {(i, h): dog[(i, h)] * sil[(i, h)] * gg[:, lanes[h]] for i, h in units}
        mdn = {u: _lane_mean(dn[u] * ohat[u], ones_b) for u in units}
        do = {u: (r[u] * (dn[u] - ohat[u] * mdn[u])).astype(BF) for u in units}

        p = {u: jnp.where(tril, sc[u], 0.0).astype(BF) for u in units}
        dpr = {u: _dot(do[u], vh[u], NT) for u in units}
        incr = {u: [_dot(do[u][rs], a[u][rs], TN) for rs in chunks] for u in units}
        dv = {u: _dot(p[u], do[u], TN) for u in units}
        dp = {u: jnp.where(tril, dpr[u], 0.0).astype(BF) for u in units}
        dqd = {u: _dot(dp[u], bm[u]) for u in units}
        dkd = {u: _dot(dp[u], a[u], TN) for u in units}
        st = {(i, h): [sprev_ref[i * nch + c, h] for c in range(nch)] for i, h in units}
        leaving = {}
        for h in range(GLA_HEADS):
            d = dst_ref[h]
            for i in reversed(range(nsub)):
                out = [None] * nch
                for c in reversed(range(nch)):
                    out[c] = d
                    d = d * decs[i][c][:, blks[h]] + incr[(i, h)][c]
                leaving[(i, h)] = out
            dst_ref[h] = d
        lv_b = {u: [leaving[u][c].astype(BF) for c in range(nch)] for u in units}
        dv_s = {u: [_dot(ktl[u][rs], lv_b[u][c], NT) for c, rs in enumerate(chunks)] for u in units}
        dqd_s = {u: [_dot(do[u][rs], st[u][c].astype(BF)) for c, rs in enumerate(chunks)] for u in units}
        dkt_s = {u: [_dot(vh[u][rs], lv_b[u][c]) for c, rs in enumerate(chunks)] for u in units}
        ddec = {u: [jnp.sum(leaving[u][c] * st[u][c], axis=0, keepdims=True) for c in range(nch)] for u in units}
        for i, h in units:
            dv_ref[subs[i], lanes[h]] = (dv[(i, h)] + jnp.concatenate(dv_s[(i, h)], axis=0)).astype(dv_ref.dtype)
        dqd = {u: dqd[u] + jnp.concatenate(dqd_s[u], axis=0) for u in units}
        dkt = {u: jnp.concatenate(dkt_s[u], axis=0) for u in units}

        db = []
        for i, sl in enumerate(subs):
            _, eq, ek, f, qd, kd, kt, _ = com[i]
            parts = []
            for pair in range(GLA_HEADS // 2):
                blk, u0, u1 = blks[2 * pair], (i, 2 * pair), (i, 2 * pair + 1)
                dqd_b, dkd_b, dkt_b = dqd[u0] + dqd[u1], dkd[u0] + dkd[u1], dkt[u0] + dkt[u1]
                dqk_ref[sl, blk] = (dqd_b * (eq[:, blk] * GLA_DK ** -0.5)).astype(dqk_ref.dtype)
                dqk_ref[sl, 256 + LANES * pair:256 + LANES * (pair + 1)] = (dkd_b * ek[:, blk] + dkt_b * f[:, blk]).astype(dqk_ref.dtype)
                dkt_kt = dkt_b * kt[:, blk]
                dbp = dqd_b * qd[:, blk] - dkd_b * kd[:, blk] - dkt_kt
                dbl = [jnp.sum(dkt_kt[rs], axis=0, keepdims=True) + (ddec[u0][c] + ddec[u1][c]) * decs[i][c][:, blk]
                       for c, rs in enumerate(chunks)]
                parts.append(jnp.where(last_row, dbp + _per_chunk_rows(dbl, LANES), dbp))
            db.append(jnp.concatenate(parts, axis=1))
        dla = [_tri_matmul(triu_b, db[i]) for i in range(nsub)]
        dz32 = [dla[i] * (1.0 / GLA_TAU) * _sigmoid(-com[i][0]) for i in range(nsub)]
        dz = [t.astype(BF) for t in dz32]
        for i, sl in enumerate(subs):
            dga_ref[sl, :] = _dot(dz[i], wd, NT).astype(dga_ref.dtype)
            dwd_ref[...] += _dot(ga[i].astype(BF), dz[i], TN)
            dbd_ref[...] += jnp.sum(dz32[i], axis=0, keepdims=True)

    def col(width, off):
        return pl.BlockSpec((tb, width), lambda i: (nb - 1 - i, off // width))

    def rev(width):
        return pl.BlockSpec((tb, width), lambda i: (nb - 1 - i, 0))

    return pl.pallas_call(
        body, name="gla_bwd", grid=(nb,),
        in_specs=[col(512, OFF_QK), col(512, OFF_V), col(512, OFF_GZ), col(LANES, OFF_GA), rev(512), rev(512),
                  pl.BlockSpec((tb // GLA_CHUNK, GLA_HEADS, LANES, LANES), lambda i: (nb - 1 - i, 0, 0, 0)),
                  _full((LANES, 256)), _full((1, 256)), _full((1, 512)), _full((sub, sub)), _full((sub, sub))],
        out_specs=[rev(512), rev(512), rev(512), rev(LANES), _full((LANES, 256)), _full((1, 256)), _full((1, 512))],
        out_shape=[jax.ShapeDtypeStruct((s, 512), BF), jax.ShapeDtypeStruct((s, 512), BF),
                   jax.ShapeDtypeStruct((s, 512), BF), jax.ShapeDtypeStruct((s, LANES), BF),
                   jax.ShapeDtypeStruct((LANES, 256), F32), jax.ShapeDtypeStruct((1, 256), F32),
                   jax.ShapeDtypeStruct((1, 512), F32)],
        scratch_shapes=[pltpu.VMEM((GLA_HEADS, LANES, LANES), F32)],
        compiler_params=_params(("arbitrary",)),
    )(proj, proj, proj, proj, dog, opre, sprev, wdecp, bdec, ggla, _gla_triangle(sub), _gla_triangle(sub).T)


_SWA_COL_HEADS = (0, 2, 1, 3, 4, 6, 5, 7)
_SWA_COLS = SWA_HEADS * SWA_BLOCK


def _swa_masks():
    lo2 = lax.broadcasted_iota(jnp.int32, (2 * SWA_BLOCK, LANES), 1) < 64
    lane1 = lax.broadcasted_iota(jnp.int32, (SWA_BLOCK, LANES), 1)
    first_half = (lane1 % 64) < 32
    key = lax.broadcasted_iota(jnp.int32, (SWA_BLOCK, _SWA_COLS), 0)
    query = lax.broadcasted_iota(jnp.int32, (SWA_BLOCK, _SWA_COLS), 1) % SWA_BLOCK
    return lo2, lane1 < 64, first_half, key > query


def _merge_band(t, prev_mask, prev_bias=None):
    prev = t[:SWA_BLOCK] if prev_bias is None else t[:SWA_BLOCK] + prev_bias
    return jnp.where(prev_mask, prev, t[SWA_BLOCK:])


def _split_band(t, prev_mask_b):
    prev = t * prev_mask_b
    return jnp.concatenate([prev, t - prev], axis=0)


def _kv_variants(t, lo2):
    tr = pltpu.roll(t, 64, 1)
    lo_v = [jnp.where(lo2, t, 0.0).astype(BF), jnp.where(lo2, tr, 0.0).astype(BF)]
    hi_v = [jnp.where(lo2, 0.0, tr).astype(BF), jnp.where(lo2, 0.0, t).astype(BF)]
    return lo_v, hi_v


def _kv_variants_t(t):
    tt = t.T
    sw = jnp.concatenate([tt[64:], tt[:64]], axis=0)
    top = lax.broadcasted_iota(jnp.int32, tt.shape, 0) < 64
    lo_v = [jnp.where(top, tt, 0.0).astype(BF), jnp.where(top, sw, 0.0).astype(BF)]
    hi_v = [jnp.where(top, 0.0, sw).astype(BF), jnp.where(top, 0.0, tt).astype(BF)]
    return lo_v, hi_v


def _swa_scores(qg, k_lo, k_hi):
    return jnp.concatenate([_dot(k_lo[0], qg[0], NT), _dot(k_hi[0], qg[0], NT),
                            _dot(k_lo[1], qg[1], NT), _dot(k_hi[1], qg[1], NT)], axis=1)


def _sink_row(sinks_ref):
    return jnp.concatenate([jnp.full((1, SWA_BLOCK), sinks_ref[0, hd], F32) for hd in _SWA_COL_HEADS], axis=1)


def _swa_softmax(st, prev_mask, prev_bias, sink):
    st = _merge_band(st, prev_mask, prev_bias)
    m = jnp.maximum(jnp.max(st, axis=0, keepdims=True), sink)
    ex = jnp.exp(st - m)
    es = jnp.exp(sink - m)
    inv = 1.0 / (jnp.sum(ex, axis=0, keepdims=True) + es)
    return ex, es, inv


def _no_prev_bias(block_index):
    return jnp.where(block_index > 0, 0.0, -1e30).astype(F32)


def _swa_queries(sq_ref, rows, cosb, sinb, first_half):
    qs = [_rope(sq_ref[rows, p * LANES:(p + 1) * LANES], cosb, sinb, first_half) * 0.125 for p in range(4)]
    return [jnp.concatenate(qs[0:2], axis=0), jnp.concatenate(qs[2:4], axis=0)]


def _swa_fwd(proj, cos, sin, sinks):
    s = proj.shape[0]
    nq = min(SWA_QBLOCKS, s // SWA_BLOCK)
    tq = nq * SWA_BLOCK

    def body(sq_ref, sz_ref, sk_ref, sv_ref, cos_ref, sin_ref, sinks_ref, os_ref, opre_ref, kprev, vprev):
        n = pl.program_id(0)

        @pl.when(n == 0)
        def _():
            kprev[...] = jnp.zeros_like(kprev)
            vprev[...] = jnp.zeros_like(vprev)

        lo2, _, first_half, prev_mask = _swa_masks()
        prev_mask_b = jnp.where(prev_mask, 1.0, 0.0).astype(BF)
        sink = _sink_row(sinks_ref)
        blocks = range(nq)
        rows = [slice(j * SWA_BLOCK, (j + 1) * SWA_BLOCK) for j in blocks]
        cosb = [cos_ref[rows[j], :] for j in blocks]
        sinb = [sin_ref[rows[j], :] for j in blocks]
        kc = [_rope(sk_ref[rows[j], :], cosb[j], sinb[j], first_half) for j in blocks]
        vc = [sv_ref[rows[j], :] for j in blocks]
        kcat = [jnp.concatenate([kprev[...] if j == 0 else kc[j - 1], kc[j]], axis=0) for j in blocks]
        vcat = [jnp.concatenate([vprev[...] if j == 0 else vc[j - 1], vc[j]], axis=0) for j in blocks]
        kprev[...] = kc[-1]
        vprev[...] = vc[-1]
        kvar = [_kv_variants(kcat[j], lo2) for j in blocks]
        vtvar = [_kv_variants_t(vcat[j]) for j in blocks]
        qg = [[q.astype(BF) for q in _swa_queries(sq_ref, rows[j], cosb[j], sinb[j], first_half)] for j in blocks]
        st = [_swa_scores(qg[j], *kvar[j]) for j in blocks]
        soft = [_swa_softmax(st[j], prev_mask, _no_prev_bias(n) if j == 0 else None, sink) for j in blocks]
        pt = [_split_band(soft[j][0].astype(BF), prev_mask_b) for j in blocks]
        og = {}
        for j in blocks:
            inv = soft[j][2]
            for g in range(2):
                c0, c1, c2 = 512 * g, 512 * g + 256, 512 * g + 512
                ot = (_dot(vtvar[j][0][g], pt[j][:, c0:c1]) * inv[:, c0:c1]
                      + _dot(vtvar[j][1][g], pt[j][:, c1:c2]) * inv[:, c1:c2])
                og[(j, g)] = ot.T
        for j in blocks:
            for g in range(2):
                for i in range(2):
                    ls = slice((2 * g + i) * LANES, (2 * g + i + 1) * LANES)
                    o = og[(j, g)][i * SWA_BLOCK:(i + 1) * SWA_BLOCK]
                    sz = sz_ref[rows[j], ls]
                    opre_ref[rows[j], ls] = o
                    os_ref[rows[j], ls] = (o * (sz * _sigmoid(sz))).astype(os_ref.dtype)

    def col(width, off):
        return pl.BlockSpec((tq, width), lambda i: (i, off // width))

    row = pl.BlockSpec((tq, LANES), lambda i: (i, 0))
    return pl.pallas_call(
        body, name="swa_fwd", grid=(s // tq,),
        in_specs=[col(512, OFF_SQ), col(512, OFF_SZ), col(LANES, OFF_SK), col(LANES, OFF_SV), row, row,
                  pl.BlockSpec(memory_space=pltpu.SMEM)],
        out_specs=[pl.BlockSpec((tq, 512), lambda i: (i, 0))] * 2,
        out_shape=[jax.ShapeDtypeStruct((s, 512), BF), jax.ShapeDtypeStruct((s, 512), F32)],
        scratch_shapes=[pltpu.VMEM((SWA_BLOCK, LANES), F32)] * 2,
        compiler_params=_params(("arbitrary",)),
    )(proj, proj, proj, proj, cos, sin, sinks)


def _swa_bwd(proj, dos, opre, cos, sin, sinks):
    s = proj.shape[0]
    nq = min(SWA_QBLOCKS, s // SWA_BLOCK)
    tq = nq * SWA_BLOCK

    def body(sq_ref, sz_ref, sk_ref, sv_ref, dos_ref, opre_ref, cos_ref, sin_ref, sinks_ref,
             dsq_ref, dsz_ref, dsk_ref, dsv_ref, dsink_ref, kprev, vprev, cprev, sprev):
        n = pl.program_id(0)

        @pl.when(n == 0)
        def _():
            kprev[...] = jnp.zeros_like(kprev)
            vprev[...] = jnp.zeros_like(vprev)
            cprev[...] = jnp.zeros_like(cprev)
            sprev[...] = jnp.zeros_like(sprev)
            for hd in range(SWA_HEADS):
                dsink_ref[0, hd] = 0.0

        lo2, lo1, first_half, prev_mask = _swa_masks()
        prev_mask_b = jnp.where(prev_mask, 1.0, 0.0).astype(BF)
        lo1s = jnp.concatenate([lo1, lo1], axis=0)
        sink = _sink_row(sinks_ref)

        def home(m0, m1):
            t0 = m0 + pltpu.roll(m0, 64, 1)
            t1 = m1 + pltpu.roll(m1, 64, 1)
            return jnp.where(lo2, t0, t1)

        kp, vp, cp_, sp_ = kprev[...], vprev[...], cprev[...], sprev[...]
        for j in range(nq):
            rows = slice(j * SWA_BLOCK, (j + 1) * SWA_BLOCK)
            blk = n * nq + j
            cosb, sinb = cos_ref[rows, :], sin_ref[rows, :]
            kc = _rope(sk_ref[rows, :], cosb, sinb, first_half)
            vc = sv_ref[rows, :]
            kcat = jnp.concatenate([kp, kc], axis=0)
            k_lo, k_hi = _kv_variants(kcat, lo2)
            kt_lo, kt_hi = _kv_variants_t(kcat)
            v_lo, v_hi = _kv_variants(jnp.concatenate([vp, vc], axis=0), lo2)
            qg32 = _swa_queries(sq_ref, rows, cosb, sinb, first_half)
            qg = [q.astype(BF) for q in qg32]
            ex, es, inv = _swa_softmax(_swa_scores(qg, k_lo, k_hi), prev_mask, _no_prev_bias(n) if j == 0 else None, sink)
            pr, ps = ex * inv, es * inv

            dog32 = []
            for g in range(2):
                parts = []
                for i in range(2):
                    ls = slice((2 * g + i) * LANES, (2 * g + i + 1) * LANES)
                    sz = sz_ref[rows, ls]
                    sg = _sigmoid(sz)
                    dos_p = dos_ref[rows, ls]
                    dsz_ref[rows, ls] = (dos_p * opre_ref[rows, ls] * (sg * (1.0 + sz * (1.0 - sg)))).astype(dsz_ref.dtype)
                    parts.append(dos_p * (sz * sg))
                dog32.append(jnp.concatenate(parts, axis=0))
            dog = [t.astype(BF) for t in dog32]
            dpr = _merge_band(jnp.concatenate([_dot(v_lo[0], dog[0], NT), _dot(v_hi[0], dog[0], NT),
                                               _dot(v_lo[1], dog[1], NT), _dot(v_hi[1], dog[1], NT)], axis=1), prev_mask)
            rd = jnp.sum(pr * dpr, axis=0, keepdims=True)
            ds = _split_band((pr * (dpr - rd)).astype(BF), prev_mask_b)
            prb = _split_band(pr.astype(BF), prev_mask_b)
            sink_term = ps * rd
            for r, hd in enumerate(_SWA_COL_HEADS):
                dsink_ref[0, hd] += -jnp.sum(sink_term[:, r * SWA_BLOCK:(r + 1) * SWA_BLOCK])

            dk_g, dv_g = [], []
            for g in range(2):
                c0, c1, c2 = 512 * g, 512 * g + 256, 512 * g + 512
                dq = (_dot(kt_lo[g], ds[:, c0:c1]) + _dot(kt_hi[g], ds[:, c1:c2])).T
                for i in range(2):
                    ls = slice((2 * g + i) * LANES, (2 * g + i + 1) * LANES)
                    dsq_ref[rows, ls] = _rope_t(dq[i * SWA_BLOCK:(i + 1) * SWA_BLOCK] * 0.125, cosb, sinb,
                                                first_half).astype(dsq_ref.dtype)
                q_split = jnp.concatenate([jnp.where(lo1s, qg32[g], 0.0), jnp.where(lo1s, 0.0, qg32[g])], axis=0).astype(BF)
                do_split = jnp.concatenate([jnp.where(lo1s, dog32[g], 0.0), jnp.where(lo1s, 0.0, dog32[g])], axis=0).astype(BF)
                dk_g.append(_dot(ds[:, c0:c2], q_split))
                dv_g.append(_dot(prb[:, c0:c2], do_split))
            dk = home(dk_g[0], dk_g[1])
            dv = home(dv_g[0], dv_g[1])
            cur = pl.ds(pl.multiple_of(blk * SWA_BLOCK, SWA_BLOCK), SWA_BLOCK)
            dsk_ref[cur, :] = _rope_t(dk[SWA_BLOCK:], cosb, sinb, first_half)
            dsv_ref[cur, :] = dv[SWA_BLOCK:]
            dk_prev = _rope_t(dk[:SWA_BLOCK], cp_, sp_, first_half)
            dv_prev = dv[:SWA_BLOCK]
            if j == 0:
                @pl.when(n > 0)
                def _():
                    prv = pl.ds(pl.multiple_of((blk - 1) * SWA_BLOCK, SWA_BLOCK), SWA_BLOCK)
                    dsk_ref[prv, :] += dk_prev
                    dsv_ref[prv, :] += dv_prev
            else:
                prv = pl.ds(pl.multiple_of((blk - 1) * SWA_BLOCK, SWA_BLOCK), SWA_BLOCK)
                dsk_ref[prv, :] += dk_prev
                dsv_ref[prv, :] += dv_prev
            kp, vp, cp_, sp_ = kc, vc, cosb, sinb
        kprev[...] = kp
        vprev[...] = vp
        cprev[...] = cp_
        sprev[...] = sp_

    def col(width, off):
        return pl.BlockSpec((tq, width), lambda i: (i, off // width))

    row = pl.BlockSpec((tq, LANES), lambda i: (i, 0))
    wide = pl.BlockSpec((tq, 512), lambda i: (i, 0))
    return pl.pallas_call(
        body, name="swa_bwd", grid=(s // tq,),
        in_specs=[col(512, OFF_SQ), col(512, OFF_SZ), col(LANES, OFF_SK), col(LANES, OFF_SV), wide, wide, row, row,
                  pl.BlockSpec(memory_space=pltpu.SMEM)],
        out_specs=[wide, wide, _full((s, LANES)), _full((s, LANES)), pl.BlockSpec(memory_space=pltpu.SMEM)],
        out_shape=[jax.ShapeDtypeStruct((s, 512), BF), jax.ShapeDtypeStruct((s, 512), BF),
                   jax.ShapeDtypeStruct((s, LANES), F32), jax.ShapeDtypeStruct((s, LANES), F32),
                   jax.ShapeDtypeStruct((1, SWA_HEADS), F32)],
        scratch_shapes=[pltpu.VMEM((SWA_BLOCK, LANES), F32)] * 4,
        compiler_params=_params(("arbitrary",)),
    )(proj, proj, proj, proj, dos, opre, cos, sin, sinks)


def _outproj(og, osw, w_out, x2d, target, gate, g_final):
    s = x2d.shape[0]
    tm = min(512, s)

    def body(og_ref, os_ref, w_ref, x_ref, t_ref, gate_ref, gf_ref,
             dx2_ref, dog_ref, dos_ref, dw_ref, loss_ref, dgf_ref, dgate_ref):
        @pl.when(pl.program_id(0) == 0)
        def _():
            dw_ref[...] = jnp.zeros_like(dw_ref)
            loss_ref[...] = jnp.zeros_like(loss_ref)
            dgf_ref[...] = jnp.zeros_like(dgf_ref)
            dgate_ref[...] = jnp.zeros_like(dgate_ref)

        w = w_ref[...]
        gate, gf = gate_ref[...], gf_ref[...]
        subs = _subtiles(tm)
        ogv = [og_ref[sl, :] for sl in subs]
        osv = [os_ref[sl, :] for sl in subs]
        y = [_dot(ogv[k], w[:512]) + _dot(osv[k], w[512:]) for k in range(len(subs))]
        dys = []
        for k, sl in enumerate(subs):
            x2 = x_ref[sl, :] + gate * y[k]
            r = lax.rsqrt(jnp.mean(x2 * x2, axis=-1, keepdims=True) + RMS_EPS)
            xn = x2 * r
            err = xn * gf - t_ref[sl, :]
            loss_ref[...] += 0.5 * jnp.sum(jnp.mean(err * err, axis=-1, keepdims=True), axis=0, keepdims=True)
            dyf = err * (1.0 / D_MODEL)
            dgf_ref[...] += jnp.sum(dyf * xn, axis=0, keepdims=True)
            t = dyf * gf
            dx2 = r * (t - xn * jnp.mean(t * xn, axis=-1, keepdims=True))
            dx2_ref[sl, :] = dx2
            dgate_ref[...] += jnp.sum(dx2 * y[k], axis=0, keepdims=True)
            dys.append((dx2 * gate).astype(BF))
            dmix = _dot(dys[k], w, NT)
            dog_ref[sl, :] = dmix[:, :512]
            dos_ref[sl, :] = dmix[:, 512:]
        dy = jnp.concatenate(dys, axis=0)
        dw_ref[:512, :] += _dot(og_ref[...], dy, TN)
        dw_ref[512:, :] += _dot(os_ref[...], dy, TN)

    half = pl.BlockSpec((tm, 512), lambda i: (i, 0))
    rowb = pl.BlockSpec((tm, D_MODEL), lambda i: (i, 0))
    vec = _full((1, D_MODEL))
    return pl.pallas_call(
        body, name="outproj", grid=(s // tm,),
        in_specs=[half, half, _full((D_MODEL, D_MODEL)), rowb, rowb, vec, vec],
        out_specs=[rowb, half, half, _full((D_MODEL, D_MODEL)), _full((1, 1)), vec, vec],
        out_shape=[jax.ShapeDtypeStruct((s, D_MODEL), F32), jax.ShapeDtypeStruct((s, 512), F32),
                   jax.ShapeDtypeStruct((s, 512), F32), jax.ShapeDtypeStruct((D_MODEL, D_MODEL), F32),
                   jax.ShapeDtypeStruct((1, 1), F32), jax.ShapeDtypeStruct((1, D_MODEL), F32),
                   jax.ShapeDtypeStruct((1, D_MODEL), F32)],
        compiler_params=_params(("arbitrary",)),
    )(og, osw, w_out, x2d, target, gate, g_final)


_PIECES = ((OFF_QK, 512), (OFF_V, 512), (OFF_GZ, 512), (OFF_SQ, 512), (OFF_SZ, 512),
           (OFF_SK, LANES), (OFF_SV, LANES), (OFF_GA, LANES))

_UNPAD_ROWS = ((OFF_QK, 0, 1024),
               (OFF_GA, 1024, GLA_RANK),
               (OFF_GZ, 1040, 1024),
               (OFF_SK, 2064, 256),
               (OFF_SZ, 2320, 512))


def _inproj_bwd(x2d, shift, sc1p, g_norm, wpad_t, dx2, pieces):
    s = x2d.shape[0]
    tm = min(512, s)
    nsteps = s // tm

    def body(x_ref, sh_ref, sc_ref, g_ref, w_hbm, dx2_ref, *rest):
        piece_refs = rest[:len(_PIECES)]
        gx_ref, dw_hbm, dsh_ref, dsc_ref, dg_ref, w_vm, dw_vm, sem, out_sems = rest[len(_PIECES):]
        i = pl.program_id(0)

        @pl.when(i == 0)
        def _():
            cp = pltpu.make_async_copy(w_hbm, w_vm, sem)
            cp.start()
            dw_vm[...] = jnp.zeros_like(dw_vm)
            dsh_ref[...] = jnp.zeros_like(dsh_ref)
            dsc_ref[...] = jnp.zeros_like(dsc_ref)
            dg_ref[...] = jnp.zeros_like(dg_ref)
            cp.wait()

        g, sc1p_v, shift_v = g_ref[...], sc_ref[...], sh_ref[...]
        subs = _subtiles(tm)
        dhs = []
        for sl in subs:
            dh = None
            for (off, width), pr in zip(_PIECES, piece_refs):
                part = _dot(pr[sl, :].astype(BF), w_vm[off:off + width, :])
                dh = part if dh is None else dh + part
            dhs.append(dh)
        norm = [_modnorm(x_ref[sl, :], g, sc1p_v, shift_v) for sl in subs]
        hb = jnp.concatenate([h.astype(BF) for _, _, h in norm], axis=0)
        for (off, width), pr in zip(_PIECES, piece_refs):
            dw_vm[off:off + width, :] += _dot(pr[...].astype(BF), hb, TN)
        for sl, (xn, r, _), dh in zip(subs, norm, dhs):
            dsh_ref[...] += jnp.sum(dh, axis=0, keepdims=True)
            dsc_ref[...] += jnp.sum(dh * (xn * g), axis=0, keepdims=True)
            dg_ref[...] += jnp.sum(dh * xn * sc1p_v, axis=0, keepdims=True)
            dxn = dh * g * sc1p_v
            gx_ref[sl, :] = dx2_ref[sl, :] + r * (dxn - xn * jnp.mean(dxn * xn, axis=-1, keepdims=True))

        @pl.when(i == nsteps - 1)
        def _():
            copies = [pltpu.make_async_copy(dw_vm.at[src:src + n], dw_hbm.at[dst:dst + n], out_sems.at[k])
                      for k, (src, dst, n) in enumerate(_UNPAD_ROWS)]
            for cp in copies:
                cp.start()
            for cp in copies:
                cp.wait()

    rowb = pl.BlockSpec((tm, D_MODEL), lambda i: (i, 0))
    vec = _full((1, D_MODEL))
    anyspec = pl.BlockSpec(memory_space=pl.ANY)
    piece_specs = [pl.BlockSpec((tm, width), lambda i: (i, 0)) for _, width in _PIECES]
    return pl.pallas_call(
        body, name="inproj_bwd", grid=(nsteps,),
        in_specs=[rowb, vec, vec, vec, anyspec, rowb] + piece_specs,
        out_specs=[rowb, anyspec, vec, vec, vec],
        out_shape=[jax.ShapeDtypeStruct((s, D_MODEL), F32), jax.ShapeDtypeStruct((D_IN, D_MODEL), F32),
                   jax.ShapeDtypeStruct((1, D_MODEL), F32), jax.ShapeDtypeStruct((1, D_MODEL), F32),
                   jax.ShapeDtypeStruct((1, D_MODEL), F32)],
        scratch_shapes=[pltpu.VMEM((D_PAD, D_MODEL), BF), pltpu.VMEM((D_PAD, D_MODEL), F32), pltpu.SemaphoreType.DMA,
                        pltpu.SemaphoreType.DMA((len(_UNPAD_ROWS),))],
        compiler_params=_params(("arbitrary",)),
    )(x2d, shift, sc1p, g_norm, wpad_t, dx2, *pieces)


def _adam(w, g, m, v):
    m2 = ADAM_B1 * m + (1.0 - ADAM_B1) * g
    v2 = ADAM_B2 * v + (1.0 - ADAM_B2) * (g * g)
    m_hat = m2 / (1.0 - ADAM_B1 ** ADAM_STEP)
    v_hat = v2 / (1.0 - ADAM_B2 ** ADAM_STEP)
    delta = -ADAM_LR * (m_hat / (jnp.sqrt(v_hat) + ADAM_EPS) + ADAM_WD * w)
    return delta, m2, v2


def _adamw(w, g, m, v, name):
    rr, cc = w.shape
    tc = min(256, cc)

    def body(w_ref, g_ref, m_ref, v_ref, d_ref, m2_ref, v2_ref):
        d_ref[...], m2_ref[...], v2_ref[...] = _adam(w_ref[...], g_ref[...], m_ref[...], v_ref[...])

    blk = pl.BlockSpec((rr, tc), lambda i: (0, i))
    return pl.pallas_call(
        body, name=name, grid=(cc // tc,), in_specs=[blk] * 4, out_specs=[blk] * 3,
        out_shape=[jax.ShapeDtypeStruct((rr, cc), F32)] * 3,
        compiler_params=_params(("arbitrary",)),
    )(w, g, m, v)


def _adamw_t(w3, g, m3, v3, name):
    rr, _, cc = w3.shape
    tc = min(256, cc)

    def body(w_ref, g_ref, m_ref, v_ref, d_ref, m2_ref, v2_ref, g3_ref):
        g = g_ref[...]
        d_ref[:, 0, :], m2_ref[:, 0, :], v2_ref[:, 0, :] = _adam(w_ref[:, 0, :], g, m_ref[:, 0, :], v_ref[:, 0, :])
        g3_ref[:, 0, :] = g

    b3 = pl.BlockSpec((rr, 1, tc), lambda i: (0, 0, i))
    return pl.pallas_call(
        body, name=name, grid=(cc // tc,), in_specs=[b3, pl.BlockSpec((rr, tc), lambda i: (0, i)), b3, b3],
        out_specs=[b3] * 4, out_shape=[jax.ShapeDtypeStruct((rr, 1, cc), F32)] * 4,
        compiler_params=_params(("arbitrary",)),
    )(w3, g, m3, v3)


def _ada_update(c_all, dmod_cols, w, m, v):
    rr, cc = w.shape
    tr = min(256, rr)
    c_all = jnp.pad(c_all, ((0, 8), (0, 0)))
    dmod_cols = jnp.pad(dmod_cols, ((0, 8), (0, 0)))

    def body(c_ref, dm_ref, w_ref, m_ref, v_ref, g_ref, d_ref, m2_ref, v2_ref):
        cv = c_ref[...]
        sc = (cv * _sigmoid(cv)).astype(BF)
        g = _dot(sc, dm_ref[...].astype(BF), TN)
        g_ref[...] = g
        d_ref[...], m2_ref[...], v2_ref[...] = _adam(w_ref[...], g, m_ref[...], v_ref[...])

    blk = pl.BlockSpec((tr, cc), lambda i: (i, 0))
    return pl.pallas_call(
        body, name="ada_update", grid=(rr // tr,),
        in_specs=[pl.BlockSpec((16, tr), lambda i: (0, i)), _full((16, cc)), blk, blk, blk],
        out_specs=[blk] * 4, out_shape=[jax.ShapeDtypeStruct((rr, cc), F32)] * 4,
        compiler_params=_params(("arbitrary",)),
    )(c_all, dmod_cols, w, m, v)


def _small_update(parts, weights, moms, vels):
    n = len(weights)

    def body(*refs):
        p_refs, w_refs, m_refs, v_refs = refs[:n + 1], refs[n + 1:2 * n + 1], refs[2 * n + 1:3 * n + 1], refs[3 * n + 1:4 * n + 1]
        outs = refs[4 * n + 1:]
        for i in range(n):
            g = p_refs[i][0]
            for d in range(1, 8):
                g = g + p_refs[i][d]
            delta, m2, v2 = _adam(w_refs[i][...], g, m_refs[i][...], v_refs[i][...])
            outs[4 * i][...] = g
            outs[4 * i + 1][...] = delta
            outs[4 * i + 2][...] = m2
            outs[4 * i + 3][...] = v2
        tot = p_refs[n][0]
        for d in range(1, 8):
            tot = tot + p_refs[n][d]
        outs[4 * n][...] = tot

    out_shape = []
    for w in weights:
        out_shape += [jax.ShapeDtypeStruct(w.shape, F32)] * 4
    out_shape.append(jax.ShapeDtypeStruct(parts[n].shape[1:], F32))
    return pl.pallas_call(body, name="small_update", out_shape=out_shape, compiler_params=_params())(
        *parts, *weights, *moms, *vels)


def _pad_w_in_t(w):
    pad = jnp.zeros((LANES - GLA_RANK, w.shape[1]), w.dtype)
    return jnp.concatenate([w[dst:dst + n] for _, dst, n in sorted(_UNPAD_ROWS)] + [pad], axis=0)


def _rows8(a):
    flat = a.reshape(-1)
    rows = -(-flat.shape[0] // LANES)
    rows8 = -(-rows // 8) * 8
    flat = jnp.pad(flat, (0, rows8 * LANES - flat.shape[0]))
    return flat.reshape(rows8, LANES)


def kernel(x, c, positions, w_ada, b_ada, g_norm, w_in, w_decay, b_decay, g_gla_head, sinks, w_out, g_final, loss_target, m_w_ada, m_b_ada, m_g_norm, m_w_in, m_w_decay, m_b_decay, m_g_gla_head, m_sinks, m_w_out, m_g_final, v_w_ada, v_b_ada, v_g_norm, v_w_in, v_w_decay, v_b_decay, v_g_gla_head, v_sinks, v_w_out, v_g_final):
    ax, ay, ac = lax.axis_index("x"), lax.axis_index("y"), lax.axis_index("c")
    chip = 2 * ax + ay
    dev = 2 * chip + ac
    s = x.shape[1]
    x2d = x[0]
    target = loss_target[0]
    w_ada2, w_out2, w_dec2 = w_ada[0], w_out[0], w_decay[0]
    w_in_t = w_in[0].T
    ada_cols = w_ada2.shape[1]
    in_cols = w_in_t.shape[0]
    out_rows = w_out2.shape[0]
    half = D_MODEL // 2

    cw = jnp.concatenate([c.reshape(8, LANES), w_dec2.reshape(8, LANES)], axis=0)
    b_shard = lax.dynamic_slice(b_ada, (0, chip * ada_cols), (1, ada_cols))
    half_in = lax.dynamic_slice(w_in_t, (0, ac * half), (in_cols, half)).astype(BF)
    half_out = lax.dynamic_slice(w_out2, (ac * (out_rows // 2), 0), (out_rows // 2, D_MODEL)).astype(BF)
    inv_freq = 1.0 / (ROPE_THETA ** (jnp.arange(0, 64, 2, dtype=F32) / 64))
    first, mod_all, w_in_all, w_out_all, cos, sin = _prologue(
        cw, w_ada2, b_shard, half_in, half_out, positions.reshape(s, 1), jnp.tile(inv_freq, 4).reshape(1, LANES))

    first = first.reshape(8, 2, 8, LANES)
    c_all = first[:, 0].reshape(8, D_MODEL)
    w_dec_full = first[0::2, 1].reshape(4, GLA_RANK, 64).transpose(1, 0, 2).reshape(GLA_RANK, 256)
    mod = mod_all.reshape(4, 2, 8, ada_cols)[:, 0]
    mod = lax.dynamic_slice(mod, (0, dev, 0), (4, 1, ada_cols)).reshape(1, 4 * ada_cols)
    shift, sc1p, gate = mod[:, :D_MODEL], 1.0 + mod[:, D_MODEL:2 * D_MODEL], mod[:, 2 * D_MODEL:]
    w_in_all = w_in_all.reshape(4, 2, in_cols, half)
    wpad_t = _pad_w_in_t(w_in_all.transpose(0, 2, 1, 3).reshape(4 * in_cols, D_MODEL))
    w_out_all = w_out_all.reshape(D_MODEL, D_MODEL)

    wdecp = jnp.pad(w_dec_full, ((0, LANES - GLA_RANK), (0, 0))).astype(BF)

    proj = _inproj_fwd(x2d, shift, sc1p, g_norm, wpad_t)
    og, o_gla, sprev = _gla_fwd(proj, wdecp, b_decay, g_gla_head)
    osw, o_swa = _swa_fwd(proj, cos, sin, sinks)
    dx2, dog, dos, dw_out, loss_p, dgf, dgate = _outproj(og, osw, w_out_all, x2d, target, gate, g_final.reshape(1, D_MODEL))
    dsq, dsz, dsk, dsv, dsinks = _swa_bwd(proj, dos, o_swa, cos, sin, sinks)
    dqk, dv, dgz, dga, dwdp, dbd, dgg = _gla_bwd(proj, dog, o_gla, sprev, wdecp, b_decay, g_gla_head)
    pieces = (dqk, dv, dgz, dsq, dsz, dsk, dsv, dga)
    gx, dw_in_t, dshift, dscale, dgn = _inproj_bwd(x2d, shift, sc1p, g_norm, wpad_t, dx2, pieces)

    segs = [jnp.concatenate([dshift, dscale, dgate], axis=1), dgn, dgf, dwdp[:GLA_RANK], dbd, dgg, dsinks, loss_p]
    packed = [_rows8(a) for a in segs]
    offs = [0]
    for a in packed:
        offs.append(offs[-1] + a.shape[0])
    g_w_in_t, g_w_out, small = _epilogue(dw_in_t.reshape(4, in_cols, D_MODEL), dw_out.reshape(4, out_rows, D_MODEL),
                                         jnp.concatenate(packed, axis=0))

    def seg(i, size):
        return small[:, offs[i]:offs[i + 1]].reshape(8, -1)[:, :size]

    dmod_all = seg(0, 3 * D_MODEL)
    dwd_all = lax.dynamic_slice(seg(3, GLA_RANK * 256).reshape(8, GLA_RANK, 256), (0, 0, chip * 64), (8, GLA_RANK, 64))
    parts = [dmod_all.reshape(8, 1, 3 * D_MODEL), seg(1, D_MODEL).reshape(8, 1, D_MODEL), dwd_all,
             seg(4, 256).reshape(8, 1, 256), seg(5, 512).reshape(8, 1, 512), seg(6, SWA_HEADS).reshape(8, 1, SWA_HEADS),
             seg(2, D_MODEL).reshape(8, 1, D_MODEL), seg(7, LANES).reshape(8, 1, LANES)]
    smalls = _small_update(
        parts,
        [b_ada, g_norm, w_dec2, b_decay, g_gla_head, sinks, g_final.reshape(1, D_MODEL)],
        [m_b_ada, m_g_norm, m_w_decay[0], m_b_decay, m_g_gla_head, m_sinks, m_g_final.reshape(1, D_MODEL)],
        [v_b_ada, v_g_norm, v_w_decay[0], v_b_decay, v_g_gla_head, v_sinks, v_g_final.reshape(1, D_MODEL)])
    (g_b_ada, d_b_ada, nm_b_ada, nv_b_ada, g_gn, d_gn, nm_gn, nv_gn, g_wd, d_wd, nm_wd, nv_wd,
     g_bd, d_bd, nm_bd, nv_bd, g_gg, d_gg, nm_gg, nv_gg, g_sk, d_sk, nm_sk, nv_sk,
     g_gf, d_gf, nm_gf, nv_gf, loss_row) = smalls
    loss = loss_row[0, 0]

    dmod_cols = lax.dynamic_slice(dmod_all, (0, chip * ada_cols), (8, ada_cols))
    g_w_ada, d_w_ada, nm_w_ada, nv_w_ada = _ada_update(c_all, dmod_cols, w_ada2, m_w_ada[0], v_w_ada[0])
    to3 = lambda a: jnp.transpose(a, (2, 0, 1))
    from3 = lambda a: jnp.transpose(a, (1, 2, 0))[0]
    d3, nm3, nv3, g3 = _adamw_t(to3(w_in), g_w_in_t, to3(m_w_in), to3(v_w_in), "adamw_w_in")
    g_w_in, d_w_in, nm_w_in, nv_w_in = from3(g3), from3(d3), from3(nm3), from3(nv3)
    d_w_out, nm_w_out, nv_w_out = _adamw(w_out2, g_w_out, m_w_out[0], v_w_out[0], "adamw_w_out")

    flat = lambda a: a.reshape(D_MODEL)
    grads = [g_w_ada[None], g_b_ada, g_gn, g_w_in[None], g_wd[None], g_bd, g_gg, g_sk, g_w_out[None], flat(g_gf)]
    deltas = [d_w_ada[None], d_b_ada, d_gn, d_w_in[None], d_wd[None], d_bd, d_gg, d_sk, d_w_out[None], flat(d_gf)]
    new_m = [nm_w_ada[None], nm_b_ada, nm_gn, nm_w_in[None], nm_wd[None], nm_bd, nm_gg, nm_sk, nm_w_out[None], flat(nm_gf)]
    new_v = [nv_w_ada[None], nv_b_ada, nv_gn, nv_w_in[None], nv_wd[None], nv_bd, nv_gg, nv_sk, nv_w_out[None], flat(nv_gf)]
    return (loss, gx[None], *grads, *deltas, *new_m, *new_v)
```

```python
import jax
import jax.numpy as jnp
from jax import lax
from jax.experimental import pallas as pl
from jax.experimental.pallas import tpu as pltpu

F32 = jnp.float32
BF = jnp.bfloat16

D_MODEL = 1024
GLA_HEADS = 4
GLA_DK = 64
GLA_CHUNK = 64
GLA_RANK = 16
GLA_TAU = 16.0
GLA_SUB = 256
GLA_ROWS = 512
SWA_HEADS = 8
SWA_BLOCK = 128
SWA_QBLOCKS = 8
RMS_EPS = 1e-6
ROPE_THETA = 10000.0

OFF_QK, OFF_V, OFF_GZ, OFF_SQ, OFF_SZ, OFF_SK, OFF_SV, OFF_GA = 0, 512, 1024, 1536, 2048, 2560, 2688, 2816
D_PAD = 2944
D_IN = 2832
LANES = 128
VMEM_LIMIT = 56 * 1024 * 1024

ADAM_LR, ADAM_B1, ADAM_B2, ADAM_EPS, ADAM_WD, ADAM_STEP = 0.001, 0.9, 0.999, 1e-08, 0.01, 10

NT = (((1,), (1,)), ((), ()))
TN = (((0,), (0,)), ((), ()))
MESH = pl.DeviceIdType.MESH


def _dot(a, b, dims=None):
    if dims is None:
        return jnp.dot(a, b, preferred_element_type=F32)
    return lax.dot_general(a, b, dims, preferred_element_type=F32)


def _sigmoid(x):
    return 1.0 / (1.0 + jnp.exp(-x))


def _params(sem=None):
    return pltpu.CompilerParams(dimension_semantics=sem, vmem_limit_bytes=VMEM_LIMIT)


def _full(shape):
    return pl.BlockSpec(shape, lambda i: (0,) * len(shape))


def _subtiles(rows, size=256):
    size = min(size, rows)
    return [slice(k * size, (k + 1) * size) for k in range(rows // size)]


_GATHER_SEMS = [pltpu.SemaphoreType.DMA((7,)), pltpu.SemaphoreType.DMA((7,)), pltpu.SemaphoreType.DMA]


class _Gather:
    def __init__(self, x_ref, out_ref, send_sems, recv_sems, local_sem):
        x, y, c = lax.axis_index("x"), lax.axis_index("y"), lax.axis_index("c")
        self.me, self.sibling, self.c = (x, y, c), (x, y, 1 - c), c
        self.xn, self.yn, self.dg = (1 - x, y), (x, 1 - y), (1 - x, 1 - y)
        self.pass_from = (lax.rem(x + 1 - c, 2), lax.rem(y + c, 2))
        self.pass_to = (lax.rem(x + c, 2), lax.rem(y + 1 - c, 2))
        self.x_ref, self.out_ref, self.send_sems, self.recv_sems = x_ref, out_ref, send_sems, recv_sems
        self.mine = pltpu.make_async_copy(x_ref, self._slab(*self.me), local_sem)

    def _slab(self, px, py, pc):
        return self.out_ref.at[4 * px + 2 * py + pc]

    def _copy(self, k, blk, to, src=None):
        return pltpu.make_async_remote_copy(
            src_ref=self._slab(*blk) if src is None else src, dst_ref=self._slab(*blk),
            send_sem=self.send_sems.at[k], recv_sem=self.recv_sems.at[k], device_id=to, device_id_type=MESH)

    def _sends(self):
        c = self.c
        return [self._copy(0, self.me, self.sibling, src=self.x_ref),
                self._copy(1, self.me, (*self.xn, c), src=self.x_ref),
                self._copy(2, self.me, (*self.yn, c), src=self.x_ref),
                self._copy(3, (*self.pass_from, c), (*self.pass_to, c)),
                self._copy(4, (*self.xn, c), self.sibling),
                self._copy(5, (*self.yn, c), self.sibling),
                self._copy(6, (*self.dg, c), self.sibling)]

    def start(self):
        self.mine.start()
        for cp in self._sends()[0:3]:
            cp.start()

    def pass_on(self):
        sends = self._sends()
        self._copy(1, (*self.xn, self.c), self.me).wait_recv()
        self._copy(2, (*self.yn, self.c), self.me).wait_recv()
        for k in (3, 4, 5):
            sends[k].start()

    def relay_diagonal(self):
        self._copy(3, (*self.dg, self.c), self.me).wait_recv()
        self._sends()[6].start()

    def relay(self):
        self.pass_on()
        self.relay_diagonal()

    def finish(self):
        c = self.c
        self._copy(0, self.sibling, self.me).wait_recv()
        for k, chip in ((4, self.xn), (5, self.yn), (6, self.dg)):
            self._copy(k, (*chip, 1 - c), self.me).wait_recv()
        for cp in self._sends():
            cp.wait_send()
        self.mine.wait()


def _prologue(cw, w_ada, b_shard, half_in, half_out, pos_col, inv_freq):
    s = pos_col.shape[0]
    rt = min(512, s)

    def body(cw_ref, wada_ref, b_ref, hin_ref, hout_ref, pos_ref, f_ref,
             first_ref, mod_ref, win_ref, wout_ref, cos_ref, sin_ref, mod_blk, *sems):
        g_c = _Gather(cw_ref, first_ref, *sems[0:3])
        g_in = _Gather(hin_ref, win_ref, *sems[3:6])
        g_out = _Gather(hout_ref, wout_ref, *sems[6:9])
        g_mod = _Gather(mod_blk, mod_ref, *sems[9:12])
        g_c.start()
        g_in.start()
        g_out.start()
        g_c.relay()
        g_c.finish()
        c_rows = [jnp.concatenate([first_ref[d, r:r + 1, :] for r in range(8)], axis=1) for d in range(8)]
        c_all = jnp.concatenate(c_rows, axis=0)
        sc = (c_all * _sigmoid(c_all)).astype(BF)
        mod_blk[...] = _dot(sc, wada_ref[...].astype(BF)) + b_ref[...]
        g_mod.start()

        def rope_rows(i, carry):
            rows = pl.ds(pl.multiple_of(i * rt, rt), rt)
            ang = pos_ref[rows, :].astype(F32) * f_ref[...]
            lane = lax.broadcasted_iota(jnp.int32, ang.shape, 1)
            cos_ref[rows, :] = jnp.cos(ang)
            sn = jnp.sin(ang)
            sin_ref[rows, :] = jnp.where((lane % 64) < 32, -sn, sn)
            return carry

        steps = s // rt
        lax.fori_loop(0, steps // 2, rope_rows, 0)
        g_in.pass_on()
        g_out.pass_on()
        lax.fori_loop(steps // 2, steps, rope_rows, 0)
        g_in.relay_diagonal()
        g_out.relay_diagonal()
        g_mod.relay()
        g_in.finish()
        g_out.finish()
        g_mod.finish()

    vm = pl.BlockSpec(memory_space=pltpu.VMEM)
    return pl.pallas_call(
        body, name="prologue",
        out_shape=[jax.ShapeDtypeStruct((8,) + cw.shape, F32), jax.ShapeDtypeStruct((8, 8, w_ada.shape[1]), F32),
                   jax.ShapeDtypeStruct((8,) + half_in.shape, half_in.dtype),
                   jax.ShapeDtypeStruct((8,) + half_out.shape, half_out.dtype),
                   jax.ShapeDtypeStruct((s, LANES), F32), jax.ShapeDtypeStruct((s, LANES), F32)],
        in_specs=[vm] * 7, out_specs=[vm] * 6,
        scratch_shapes=[pltpu.VMEM((8, w_ada.shape[1]), F32)] + _GATHER_SEMS * 4,
        compiler_params=pltpu.CompilerParams(vmem_limit_bytes=VMEM_LIMIT),
    )(cw, w_ada, b_shard, half_in, half_out, pos_col, inv_freq)


def _reduce_scratch(rr, cc):
    c2 = cc // 2
    return [pltpu.VMEM((4, rr, c2), F32), pltpu.VMEM((4, rr, c2), F32), pltpu.VMEM((3, rr, c2), BF),
            pltpu.VMEM((2, rr, c2), BF), pltpu.VMEM((rr, c2), BF), pltpu.VMEM((rr, c2), F32),
            pltpu.SemaphoreType.DMA((8,)), pltpu.SemaphoreType.DMA((8,)), pltpu.SemaphoreType.DMA((5,))]


class _Reduce:
    def __init__(self, p_hbm, out_ref, acc_ref, own_ref, send_ref, land_ref, relay_ref, res_ref,
                 send_sems, recv_sems, local_sems):
        x, y, c = lax.axis_index("x"), lax.axis_index("y"), lax.axis_index("c")
        c2 = out_ref.shape[1] // 2
        sibling = (x, y, 1 - c)
        first = (lax.rem(x + 1 - c, 2), lax.rem(y + c, 2))
        second = (lax.rem(x + c, 2), lax.rem(y + 1 - c, 2))
        shards = [2 * first[0] + first[1], 2 * second[0] + second[1], 2 * (1 - x) + (1 - y), 2 * x + y]
        sibling_slot = (1, 0, 2, 3)
        mine = pl.ds(pl.multiple_of(c * c2, c2), c2)
        other = pl.ds(pl.multiple_of((1 - c) * c2, c2), c2)
        self.acc_ref, self.own_ref, self.send_ref, self.land_ref = acc_ref, own_ref, send_ref, land_ref
        self.relay_ref, self.res_ref = relay_ref, res_ref
        self.own = [pltpu.make_async_copy(p_hbm.at[j, :, mine], own_ref.at[k], local_sems.at[k])
                    for k, j in enumerate(shards)]
        self.swap_out = [pltpu.make_async_remote_copy(
            src_ref=p_hbm.at[j, :, other], dst_ref=acc_ref.at[sibling_slot[k]], send_sem=send_sems.at[k],
            recv_sem=recv_sems.at[sibling_slot[k]], device_id=sibling, device_id_type=MESH) for k, j in enumerate(shards)]
        self.swap_in = [pltpu.make_async_remote_copy(
            src_ref=p_hbm.at[j, :, other], dst_ref=acc_ref.at[k], send_sem=send_sems.at[k], recv_sem=recv_sems.at[k],
            device_id=sibling, device_id_type=MESH) for k, j in enumerate(shards)]

        def message(k, src, dst, to):
            return pltpu.make_async_remote_copy(src_ref=src, dst_ref=dst, send_sem=send_sems.at[k], recv_sem=recv_sems.at[k],
                                                device_id=(*to, c), device_id_type=MESH)

        self.direct = message(4, send_ref.at[0], land_ref.at[0], first)
        self.passed = message(5, send_ref.at[1], relay_ref, first)
        self.joint = message(6, send_ref.at[2], land_ref.at[1], second)
        self.put = pltpu.make_async_copy(res_ref, out_ref.at[:, mine], local_sems.at[4])
        self.share = pltpu.make_async_remote_copy(
            src_ref=res_ref, dst_ref=out_ref.at[:, mine], send_sem=send_sems.at[7],
            recv_sem=recv_sems.at[7], device_id=sibling, device_id_type=MESH)

    def start(self):
        for k in (0, 2, 1, 3):
            self.own[k].start()
            self.swap_out[k].start()

    def _combine(self, k):
        self.own[k].wait()
        self.swap_out[k].wait_send()
        self.swap_in[k].wait_recv()
        self.acc_ref[k] = self.acc_ref[k] + self.own_ref[k]

    def combine_and_send(self):
        dt = self.send_ref.dtype
        self._combine(0)
        self.send_ref[0] = self.acc_ref[0].astype(dt)
        self.direct.start()
        self._combine(2)
        self.send_ref[1] = self.acc_ref[2].astype(dt)
        self.passed.start()
        self._combine(1)
        self.passed.wait_recv()
        self.send_ref[2] = (self.acc_ref[1] + self.relay_ref[...].astype(F32)).astype(dt)
        self.joint.start()
        self._combine(3)

    def total_and_share(self):
        self.direct.wait_recv()
        self.joint.wait_recv()
        self.res_ref[...] = self.acc_ref[3] + self.land_ref[0].astype(F32) + self.land_ref[1].astype(F32)
        for cp in (self.direct, self.passed, self.joint):
            cp.wait_send()
        self.put.start()
        self.share.start()

    def finish(self):
        self.put.wait()
        self.share.wait()


def _epilogue(dw_in_parts, dw_out_parts, small):
    _, r_in, cc = dw_in_parts.shape
    _, r_out, _ = dw_out_parts.shape
    n_red = len(_reduce_scratch(r_in, cc))

    def body(pin_hbm, pout_hbm, small_ref, gin_ref, gout_ref, small_all_ref, *scratch):
        red_in = _Reduce(pin_hbm, gin_ref, *scratch[0:n_red])
        red_out = _Reduce(pout_hbm, gout_ref, *scratch[n_red:2 * n_red])
        gat = _Gather(small_ref, small_all_ref, *scratch[2 * n_red:])
        red_out.start()
        red_in.start()
        gat.start()
        red_out.combine_and_send()
        red_in.combine_and_send()
        gat.relay()
        red_out.total_and_share()
        red_in.total_and_share()
        gat.finish()
        red_out.finish()
        red_in.finish()

    vm = pl.BlockSpec(memory_space=pltpu.VMEM)
    anyspec = pl.BlockSpec(memory_space=pl.ANY)
    return pl.pallas_call(
        body, name="epilogue",
        out_shape=[jax.ShapeDtypeStruct((r_in, cc), F32), jax.ShapeDtypeStruct((r_out, cc), F32),
                   jax.ShapeDtypeStruct((8,) + small.shape, F32)],
        in_specs=[anyspec, anyspec, vm], out_specs=[vm, vm, vm],
        scratch_shapes=_reduce_scratch(r_in, cc) + _reduce_scratch(r_out, cc) + _GATHER_SEMS,
        compiler_params=pltpu.CompilerParams(vmem_limit_bytes=VMEM_LIMIT),
    )(dw_in_parts, dw_out_parts, small)


def _rope(t, cosb, sinb, first_half):
    partner = jnp.where(first_half, pltpu.roll(t, 96, 1), pltpu.roll(t, 32, 1))
    return t * cosb + partner * sinb


def _rope_t(g, cosb, sinb, first_half):
    gs = g * sinb
    partner = jnp.where(first_half, pltpu.roll(gs, 96, 1), pltpu.roll(gs, 32, 1))
    return g * cosb + partner


def _modnorm(x, g, sc1p, shift):
    r = lax.rsqrt(jnp.mean(x * x, axis=-1, keepdims=True) + RMS_EPS)
    xn = x * r
    return xn, r, (xn * g) * sc1p + shift


def _inproj_fwd(x2d, shift, sc1p, g_norm, wpad_t):
    s = x2d.shape[0]
    tm = min(512, s)

    def body(x_ref, sh_ref, sc_ref, g_ref, w_ref, o_ref):
        subs = _subtiles(tm)
        hs = [_modnorm(x_ref[sl, :], g_ref[...], sc_ref[...], sh_ref[...])[2].astype(BF) for sl in subs]
        for sl, h in zip(subs, hs):
            o_ref[sl, :] = _dot(h, w_ref[...], NT)

    vec = _full((1, D_MODEL))
    return pl.pallas_call(
        body, name="inproj_fwd", grid=(s // tm,),
        in_specs=[pl.BlockSpec((tm, D_MODEL), lambda i: (i, 0)), vec, vec, vec, _full((D_PAD, D_MODEL))],
        out_specs=pl.BlockSpec((tm, D_PAD), lambda i: (i, 0)),
        out_shape=jax.ShapeDtypeStruct((s, D_PAD), F32),
        compiler_params=_params(("arbitrary",)),
    )(x2d, shift, sc1p, g_norm, wpad_t)


def _split3(a):
    hi = a.astype(BF)
    r1 = a - hi.astype(F32)
    mid = r1.astype(BF)
    lo = (r1 - mid.astype(F32)).astype(BF)
    return hi, mid, lo


def _tri_matmul(tri, a):
    hi, mid, lo = _split3(a)
    return _dot(tri, hi) + _dot(tri, mid) + _dot(tri, lo)


def _chunks(tb):
    return [slice(c * GLA_CHUNK, (c + 1) * GLA_CHUNK) for c in range(tb // GLA_CHUNK)]


def _per_chunk_rows(rows, width):
    return jnp.concatenate([jnp.broadcast_to(r, (GLA_CHUNK, width)) for r in rows], axis=0)


def _gla_triangle(tb):
    row = lax.broadcasted_iota(jnp.int32, (tb, tb), 0)
    col = lax.broadcasted_iota(jnp.int32, (tb, tb), 1)
    return (((row // GLA_CHUNK) == (col // GLA_CHUNK)) & (col <= row)).astype(F32)


def _lane_mean(x, ones_b):
    hi = x.astype(BF)
    lo = (x - hi.astype(F32)).astype(BF)
    return (_dot(hi, ones_b) + _dot(lo, ones_b)) * (1.0 / LANES)


def _head(t, h, lo_h):
    blk = t[:, LANES * (h // 2):LANES * (h // 2 + 1)]
    return jnp.where(lo_h, blk, 0.0) if h % 2 == 0 else jnp.where(lo_h, 0.0, blk)


def _gla_block_common(qk, ga, wd, bd, tril_b):
    tb = qk.shape[0]
    q, k = qk[:, :256], qk[:, 256:]
    z = _dot(ga.astype(BF), wd) + bd
    la = (jnp.minimum(z, 0.0) - jnp.log(1.0 + jnp.exp(-jnp.abs(z)))) * (1.0 / GLA_TAU)
    b = _tri_matmul(tril_b, la)
    bls = [b[rs.stop - 1:rs.stop, :] for rs in _chunks(tb)]
    eq = jnp.exp(b)
    ek = jnp.exp(-b)
    f = jnp.exp(_per_chunk_rows(bls, 256) - b)
    return z, eq, ek, f, q * (eq * GLA_DK ** -0.5), k * ek, k * f, bls


def _gla_units(s):
    sub = min(GLA_SUB, s)
    tb = min(GLA_ROWS, s)
    subs = [slice(i * sub, (i + 1) * sub) for i in range(tb // sub)]
    units = [(i, h) for i in range(len(subs)) for h in range(GLA_HEADS)]
    return tb, sub, subs, units


def _gla_fwd(proj, wdecp, bdec, ggla):
    s = proj.shape[0]
    tb, sub, subs, units = _gla_units(s)
    nch = sub // GLA_CHUNK

    def body(qk_ref, v_ref, gz_ref, ga_ref, wd_ref, bd_ref, gg_ref, tri_ref, og_ref, opre_ref, sprev_ref, st_ref):
        @pl.when(pl.program_id(0) == 0)
        def _():
            st_ref[...] = jnp.zeros_like(st_ref)

        lo_h = lax.broadcasted_iota(jnp.int32, (sub, LANES), 1) < GLA_DK
        tril = tri_ref[...] > 0.5
        tril_b = tri_ref[...].astype(BF)
        ones_b = jnp.ones((LANES, LANES), BF)
        gg, wd, bd = gg_ref[...], wd_ref[...], bd_ref[...]
        chunks = _chunks(sub)
        lanes = [slice(h * LANES, (h + 1) * LANES) for h in range(GLA_HEADS)]
        com = [_gla_block_common(qk_ref[sl, :], ga_ref[sl, :], wd, bd, tril_b) for sl in subs]
        decs = [[jnp.exp(bl) for bl in cm[7]] for cm in com]
        a = {(i, h): _head(com[i][4], h, lo_h).astype(BF) for i, h in units}
        bm = {(i, h): _head(com[i][5], h, lo_h).astype(BF) for i, h in units}
        ktl = {(i, h): _head(com[i][6], h, lo_h).astype(BF) for i, h in units}
        vh = {(i, h): v_ref[subs[i], lanes[h]].astype(BF) for i, h in units}
        sc = {u: _dot(a[u], bm[u], NT) for u in units}
        upd = {u: [_dot(vh[u][rs], ktl[u][rs], TN) for rs in chunks] for u in units}
        p = {u: jnp.where(tril, sc[u], 0.0).astype(BF) for u in units}
        o = {u: _dot(p[u], vh[u]) for u in units}
        states = {}
        for h in range(GLA_HEADS):
            st = st_ref[h]
            for i in range(len(subs)):
                entering = []
                for c in range(nch):
                    entering.append(st)
                    sprev_ref[i * nch + c, h] = st
                    st = st * decs[i][c][:, LANES * (h // 2):LANES * (h // 2 + 1)] + upd[(i, h)][c]
                states[(i, h)] = entering
            st_ref[h] = st
        inter = {u: [_dot(a[u][rs], states[u][c].astype(BF), NT) for c, rs in enumerate(chunks)] for u in units}
        o = {u: o[u] + jnp.concatenate(inter[u], axis=0) for u in units}
        ms = {u: _lane_mean(o[u] * o[u], ones_b) for u in units}
        for i, h in units:
            gzh = gz_ref[subs[i], lanes[h]]
            opre_ref[subs[i], lanes[h]] = o[(i, h)]
            og_ref[subs[i], lanes[h]] = (((o[(i, h)] * lax.rsqrt(ms[(i, h)] + RMS_EPS)) * gg[:, lanes[h]])
                                         * (gzh * _sigmoid(gzh))).astype(og_ref.dtype)

    def col(width, off):
        return pl.BlockSpec((tb, width), lambda i: (i, off // width))

    return pl.pallas_call(
        body, name="gla_fwd", grid=(s // tb,),
        in_specs=[col(512, OFF_QK), col(512, OFF_V), col(512, OFF_GZ), col(LANES, OFF_GA),
                  _full((LANES, 256)), _full((1, 256)), _full((1, 512)), _full((sub, sub))],
        out_specs=[pl.BlockSpec((tb, 512), lambda i: (i, 0)), pl.BlockSpec((tb, 512), lambda i: (i, 0)),
                   pl.BlockSpec((tb // GLA_CHUNK, GLA_HEADS, LANES, LANES), lambda i: (i, 0, 0, 0))],
        out_shape=[jax.ShapeDtypeStruct((s, 512), BF), jax.ShapeDtypeStruct((s, 512), F32),
                   jax.ShapeDtypeStruct((s // GLA_CHUNK, GLA_HEADS, LANES, LANES), F32)],
        scratch_shapes=[pltpu.VMEM((GLA_HEADS, LANES, LANES), F32)],
        compiler_params=_params(("arbitrary",)),
    )(proj, proj, proj, proj, wdecp, bdec, ggla, _gla_triangle(sub))


def _gla_bwd(proj, dog, opre, sprev, wdecp, bdec, ggla):
    s = proj.shape[0]
    tb, sub, subs, units = _gla_units(s)
    nsub = len(subs)
    nch = sub // GLA_CHUNK
    nb = s // tb

    def body(qk_ref, v_ref, gz_ref, ga_ref, dog_ref, opre_ref, sprev_ref, wd_ref, bd_ref, gg_ref, tri_ref, triu_ref,
             dqk_ref, dv_ref, dgz_ref, dga_ref, dwd_ref, dbd_ref, dgg_ref, dst_ref):
        @pl.when(pl.program_id(0) == 0)
        def _():
            dst_ref[...] = jnp.zeros_like(dst_ref)
            dwd_ref[...] = jnp.zeros_like(dwd_ref)
            dbd_ref[...] = jnp.zeros_like(dbd_ref)
            dgg_ref[...] = jnp.zeros_like(dgg_ref)

        lo_h = lax.broadcasted_iota(jnp.int32, (sub, LANES), 1) < GLA_DK
        tril = tri_ref[...] > 0.5
        tril_b = tri_ref[...].astype(BF)
        triu_b = triu_ref[...].astype(BF)
        ones_b = jnp.ones((LANES, LANES), BF)
        last_row = (lax.broadcasted_iota(jnp.int32, (sub, LANES), 0) % GLA_CHUNK) == GLA_CHUNK - 1
        wd, gg, bd = wd_ref[...], gg_ref[...], bd_ref[...]
        chunks = _chunks(sub)
        lanes = [slice(h * LANES, (h + 1) * LANES) for h in range(GLA_HEADS)]
        blks = [slice(LANES * (h // 2), LANES * (h // 2 + 1)) for h in range(GLA_HEADS)]
        ga = [ga_ref[sl, :] for sl in subs]
        com = [_gla_block_common(qk_ref[sl, :], ga[i], wd, bd, tril_b) for i, sl in enumerate(subs)]
        decs = [[jnp.exp(bl) for bl in cm[7]] for cm in com]
        a = {(i, h): _head(com[i][4], h, lo_h).astype(BF) for i, h in units}
        bm = {(i, h): _head(com[i][5], h, lo_h).astype(BF) for i, h in units}
        ktl = {(i, h): _head(com[i][6], h, lo_h).astype(BF) for i, h in units}
        vh = {(i, h): v_ref[subs[i], lanes[h]].astype(BF) for i, h in units}
        sc = {u: _dot(a[u], bm[u], NT) for u in units}

        o = {(i, h): opre_ref[subs[i], lanes[h]] for i, h in units}
        ms = {u: _lane_mean(o[u] * o[u], ones_b) for u in units}
        gz = {(i, h): gz_ref[subs[i], lanes[h]] for i, h in units}
        dog = {(i, h): dog_ref[subs[i], lanes[h]] for i, h in units}
        sg = {u: _sigmoid(gz[u]) for u in units}
        r = {u: lax.rsqrt(ms[u] + RMS_EPS) for u in units}
        ohat = {u: o[u] * r[u] for u in units}
        sil = {u: gz[u] * sg[u] for u in units}
        for i, h in units:
            u = (i, h)
            dgz_ref[subs[i], lanes[h]] = (dog[u] * (ohat[u] * gg[:, lanes[h]])
                                          * (sg[u] * (1.0 + gz[u] * (1.0 - sg[u])))).astype(dgz_ref.dtype)
            dgg_ref[:, lanes[h]] += jnp.sum(dog[u] * sil[u] * ohat[u], axis=0, keepdims=True)
        dn = {(i, h): dog[(i, h)] * sil[(i, h)] * gg[:, lanes[h]] for i, h in units}
        mdn = {u: _lane_mean(dn[u] * ohat[u], ones_b) for u in units}
        do = {u: (r[u] * (dn[u] - ohat[u] * mdn[u])).astype(BF) for u in units}

        p = {u: jnp.where(tril, sc[u], 0.0).astype(BF) for u in units}
        dpr = {u: _dot(do[u], vh[u], NT) for u in units}
        incr = {u: [_dot(do[u][rs], a[u][rs], TN) for rs in chunks] for u in units}
        dv = {u: _dot(p[u], do[u], TN) for u in units}
        dp = {u: jnp.where(tril, dpr[u], 0.0).astype(BF) for u in units}
        dqd = {u: _dot(dp[u], bm[u]) for u in units}
        dkd = {u: _dot(dp[u], a[u], TN) for u in units}
        st = {(i, h): [sprev_ref[i * nch + c, h] for c in range(nch)] for i, h in units}
        leaving = {}
        for h in range(GLA_HEADS):
            d = dst_ref[h]
            for i in reversed(range(nsub)):
                out = [None] * nch
                for c in reversed(range(nch)):
                    out[c] = d
                    d = d * decs[i][c][:, blks[h]] + incr[(i, h)][c]
                leaving[(i, h)] = out
            dst_ref[h] = d
        lv_b = {u: [leaving[u][c].astype(BF) for c in range(nch)] for u in units}
        dv_s = {u: [_dot(ktl[u][rs], lv_b[u][c], NT) for c, rs in enumerate(chunks)] for u in units}
        dqd_s = {u: [_dot(do[u][rs], st[u][c].astype(BF)) for c, rs in enumerate(chunks)] for u in units}
        dkt_s = {u: [_dot(vh[u][rs], lv_b[u][c]) for c, rs in enumerate(chunks)] for u in units}
        ddec = {u: [jnp.sum(leaving[u][c] * st[u][c], axis=0, keepdims=True) for c in range(nch)] for u in units}
        for i, h in units:
            dv_ref[subs[i], lanes[h]] = (dv[(i, h)] + jnp.concatenate(dv_s[(i, h)], axis=0)).astype(dv_ref.dtype)
        dqd = {u: dqd[u] + jnp.concatenate(dqd_s[u], axis=0) for u in units}
        dkt = {u: jnp.concatenate(dkt_s[u], axis=0) for u in units}

        db = []
        for i, sl in enumerate(subs):
            _, eq, ek, f, qd, kd, kt, _ = com[i]
            parts = []
            for pair in range(GLA_HEADS // 2):
                blk, u0, u1 = blks[2 * pair], (i, 2 * pair), (i, 2 * pair + 1)
                dqd_b, dkd_b, dkt_b = dqd[u0] + dqd[u1], dkd[u0] + dkd[u1], dkt[u0] + dkt[u1]
                dqk_ref[sl, blk] = (dqd_b * (eq[:, blk] * GLA_DK ** -0.5)).astype(dqk_ref.dtype)
                dqk_ref[sl, 256 + LANES * pair:256 + LANES * (pair + 1)] = (dkd_b * ek[:, blk] + dkt_b * f[:, blk]).astype(dqk_ref.dtype)
                dkt_kt = dkt_b * kt[:, blk]
                dbp = dqd_b * qd[:, blk] - dkd_b * kd[:, blk] - dkt_kt
                dbl = [jnp.sum(dkt_kt[rs], axis=0, keepdims=True) + (ddec[u0][c] + ddec[u1][c]) * decs[i][c][:, blk]
                       for c, rs in enumerate(chunks)]
                parts.append(jnp.where(last_row, dbp + _per_chunk_rows(dbl, LANES), dbp))
            db.append(jnp.concatenate(parts, axis=1))
        dla = [_tri_matmul(triu_b, db[i]) for i in range(nsub)]
        dz32 = [dla[i] * (1.0 / GLA_TAU) * _sigmoid(-com[i][0]) for i in range(nsub)]
        dz = [t.astype(BF) for t in dz32]
        for i, sl in enumerate(subs):
            dga_ref[sl, :] = _dot(dz[i], wd, NT).astype(dga_ref.dtype)
            dwd_ref[...] += _dot(ga[i].astype(BF), dz[i], TN)
            dbd_ref[...] += jnp.sum(dz32[i], axis=0, keepdims=True)

    def col(width, off):
        return pl.BlockSpec((tb, width), lambda i: (nb - 1 - i, off // width))

    def rev(width):
        return pl.BlockSpec((tb, width), lambda i: (nb - 1 - i, 0))

    return pl.pallas_call(
        body, name="gla_bwd", grid=(nb,),
        in_specs=[col(512, OFF_QK), col(512, OFF_V), col(512, OFF_GZ), col(LANES, OFF_GA), rev(512), rev(512),
                  pl.BlockSpec((tb // GLA_CHUNK, GLA_HEADS, LANES, LANES), lambda i: (nb - 1 - i, 0, 0, 0)),
                  _full((LANES, 256)), _full((1, 256)), _full((1, 512)), _full((sub, sub)), _full((sub, sub))],
        out_specs=[rev(512), rev(512), rev(512), rev(LANES), _full((LANES, 256)), _full((1, 256)), _full((1, 512))],
        out_shape=[jax.ShapeDtypeStruct((s, 512), BF), jax.ShapeDtypeStruct((s, 512), BF),
                   jax.ShapeDtypeStruct((s, 512), BF), jax.ShapeDtypeStruct((s, LANES), BF),
                   jax.ShapeDtypeStruct((LANES, 256), F32), jax.ShapeDtypeStruct((1, 256), F32),
                   jax.ShapeDtypeStruct((1, 512), F32)],
        scratch_shapes=[pltpu.VMEM((GLA_HEADS, LANES, LANES), F32)],
        compiler_params=_params(("arbitrary",)),
    )(proj, proj, proj, proj, dog, opre, sprev, wdecp, bdec, ggla, _gla_triangle(sub), _gla_triangle(sub).T)


_SWA_COL_HEADS = (0, 2, 1, 3, 4, 6, 5, 7)
_SWA_COLS = SWA_HEADS * SWA_BLOCK


def _swa_masks():
    lo2 = lax.broadcasted_iota(jnp.int32, (2 * SWA_BLOCK, LANES), 1) < 64
    lane1 = lax.broadcasted_iota(jnp.int32, (SWA_BLOCK, LANES), 1)
    first_half = (lane1 % 64) < 32
    key = lax.broadcasted_iota(jnp.int32, (SWA_BLOCK, _SWA_COLS), 0)
    query = lax.broadcasted_iota(jnp.int32, (SWA_BLOCK, _SWA_COLS), 1) % SWA_BLOCK
    return lo2, lane1 < 64, first_half, key > query


def _merge_band(t, prev_mask, prev_bias=None):
    prev = t[:SWA_BLOCK] if prev_bias is None else t[:SWA_BLOCK] + prev_bias
    return jnp.where(prev_mask, prev, t[SWA_BLOCK:])


def _split_band(t, prev_mask_b):
    prev = t * prev_mask_b
    return jnp.concatenate([prev, t - prev], axis=0)


def _kv_variants(t, lo2):
    tr = pltpu.roll(t, 64, 1)
    lo_v = [jnp.where(lo2, t, 0.0).astype(BF), jnp.where(lo2, tr, 0.0).astype(BF)]
    hi_v = [jnp.where(lo2, 0.0, tr).astype(BF), jnp.where(lo2, 0.0, t).astype(BF)]
    return lo_v, hi_v


def _kv_variants_t(t):
    tt = t.T
    sw = jnp.concatenate([tt[64:], tt[:64]], axis=0)
    top = lax.broadcasted_iota(jnp.int32, tt.shape, 0) < 64
    lo_v = [jnp.where(top, tt, 0.0).astype(BF), jnp.where(top, sw, 0.0).astype(BF)]
    hi_v = [jnp.where(top, 0.0, sw).astype(BF), jnp.where(top, 0.0, tt).astype(BF)]
    return lo_v, hi_v


def _swa_scores(qg, k_lo, k_hi):
    return jnp.concatenate([_dot(k_lo[0], qg[0], NT), _dot(k_hi[0], qg[0], NT),
                            _dot(k_lo[1], qg[1], NT), _dot(k_hi[1], qg[1], NT)], axis=1)


def _sink_row(sinks_ref):
    return jnp.concatenate([jnp.full((1, SWA_BLOCK), sinks_ref[0, hd], F32) for hd in _SWA_COL_HEADS], axis=1)


def _swa_softmax(st, prev_mask, prev_bias, sink):
    st = _merge_band(st, prev_mask, prev_bias)
    m = jnp.maximum(jnp.max(st, axis=0, keepdims=True), sink)
    ex = jnp.exp(st - m)
    es = jnp.exp(sink - m)
    inv = 1.0 / (jnp.sum(ex, axis=0, keepdims=True) + es)
    return ex, es, inv


def _no_prev_bias(block_index):
    return jnp.where(block_index > 0, 0.0, -1e30).astype(F32)


def _swa_queries(sq_ref, rows, cosb, sinb, first_half):
    qs = [_rope(sq_ref[rows, p * LANES:(p + 1) * LANES], cosb, sinb, first_half) * 0.125 for p in range(4)]
    return [jnp.concatenate(qs[0:2], axis=0), jnp.concatenate(qs[2:4], axis=0)]


def _swa_fwd(proj, cos, sin, sinks):
    s = proj.shape[0]
    nq = min(SWA_QBLOCKS, s // SWA_BLOCK)
    tq = nq * SWA_BLOCK

    def body(sq_ref, sz_ref, sk_ref, sv_ref, cos_ref, sin_ref, sinks_ref, os_ref, opre_ref, kprev, vprev):
        n = pl.program_id(0)

        @pl.when(n == 0)
        def _():
            kprev[...] = jnp.zeros_like(kprev)
            vprev[...] = jnp.zeros_like(vprev)

        lo2, _, first_half, prev_mask = _swa_masks()
        prev_mask_b = jnp.where(prev_mask, 1.0, 0.0).astype(BF)
        sink = _sink_row(sinks_ref)
        blocks = range(nq)
        rows = [slice(j * SWA_BLOCK, (j + 1) * SWA_BLOCK) for j in blocks]
        cosb = [cos_ref[rows[j], :] for j in blocks]
        sinb = [sin_ref[rows[j], :] for j in blocks]
        kc = [_rope(sk_ref[rows[j], :], cosb[j], sinb[j], first_half) for j in blocks]
        vc = [sv_ref[rows[j], :] for j in blocks]
        kcat = [jnp.concatenate([kprev[...] if j == 0 else kc[j - 1], kc[j]], axis=0) for j in blocks]
        vcat = [jnp.concatenate([vprev[...] if j == 0 else vc[j - 1], vc[j]], axis=0) for j in blocks]
        kprev[...] = kc[-1]
        vprev[...] = vc[-1]
        kvar = [_kv_variants(kcat[j], lo2) for j in blocks]
        vtvar = [_kv_variants_t(vcat[j]) for j in blocks]
        qg = [[q.astype(BF) for q in _swa_queries(sq_ref, rows[j], cosb[j], sinb[j], first_half)] for j in blocks]
        st = [_swa_scores(qg[j], *kvar[j]) for j in blocks]
        soft = [_swa_softmax(st[j], prev_mask, _no_prev_bias(n) if j == 0 else None, sink) for j in blocks]
        pt = [_split_band(soft[j][0].astype(BF), prev_mask_b) for j in blocks]
        og = {}
        for j in blocks:
            inv = soft[j][2]
            for g in range(2):
                c0, c1, c2 = 512 * g, 512 * g + 256, 512 * g + 512
                ot = (_dot(vtvar[j][0][g], pt[j][:, c0:c1]) * inv[:, c0:c1]
                      + _dot(vtvar[j][1][g], pt[j][:, c1:c2]) * inv[:, c1:c2])
                og[(j, g)] = ot.T
        for j in blocks:
            for g in range(2):
                for i in range(2):
                    ls = slice((2 * g + i) * LANES, (2 * g + i + 1) * LANES)
                    o = og[(j, g)][i * SWA_BLOCK:(i + 1) * SWA_BLOCK]
                    sz = sz_ref[rows[j], ls]
                    opre_ref[rows[j], ls] = o
                    os_ref[rows[j], ls] = (o * (sz * _sigmoid(sz))).astype(os_ref.dtype)

    def col(width, off):
        return pl.BlockSpec((tq, width), lambda i: (i, off // width))

    row = pl.BlockSpec((tq, LANES), lambda i: (i, 0))
    return pl.pallas_call(
        body, name="swa_fwd", grid=(s // tq,),
        in_specs=[col(512, OFF_SQ), col(512, OFF_SZ), col(LANES, OFF_SK), col(LANES, OFF_SV), row, row,
                  pl.BlockSpec(memory_space=pltpu.SMEM)],
        out_specs=[pl.BlockSpec((tq, 512), lambda i: (i, 0))] * 2,
        out_shape=[jax.ShapeDtypeStruct((s, 512), BF), jax.ShapeDtypeStruct((s, 512), F32)],
        scratch_shapes=[pltpu.VMEM((SWA_BLOCK, LANES), F32)] * 2,
        compiler_params=_params(("arbitrary",)),
    )(proj, proj, proj, proj, cos, sin, sinks)


def _swa_bwd(proj, dos, opre, cos, sin, sinks):
    s = proj.shape[0]
    nq = min(SWA_QBLOCKS, s // SWA_BLOCK)
    tq = nq * SWA_BLOCK

    def body(sq_ref, sz_ref, sk_ref, sv_ref, dos_ref, opre_ref, cos_ref, sin_ref, sinks_ref,
             dsq_ref, dsz_ref, dsk_ref, dsv_ref, dsink_ref, kprev, vprev, cprev, sprev):
        n = pl.program_id(0)

        @pl.when(n == 0)
        def _():
            kprev[...] = jnp.zeros_like(kprev)
            vprev[...] = jnp.zeros_like(vprev)
            cprev[...] = jnp.zeros_like(cprev)
            sprev[...] = jnp.zeros_like(sprev)
            for hd in range(SWA_HEADS):
                dsink_ref[0, hd] = 0.0

        lo2, lo1, first_half, prev_mask = _swa_masks()
        prev_mask_b = jnp.where(prev_mask, 1.0, 0.0).astype(BF)
        lo1s = jnp.concatenate([lo1, lo1], axis=0)
        sink = _sink_row(sinks_ref)

        def home(m0, m1):
            t0 = m0 + pltpu.roll(m0, 64, 1)
            t1 = m1 + pltpu.roll(m1, 64, 1)
            return jnp.where(lo2, t0, t1)

        kp, vp, cp_, sp_ = kprev[...], vprev[...], cprev[...], sprev[...]
        for j in range(nq):
            rows = slice(j * SWA_BLOCK, (j + 1) * SWA_BLOCK)
            blk = n * nq + j
            cosb, sinb = cos_ref[rows, :], sin_ref[rows, :]
            kc = _rope(sk_ref[rows, :], cosb, sinb, first_half)
            vc = sv_ref[rows, :]
            kcat = jnp.concatenate([kp, kc], axis=0)
            k_lo, k_hi = _kv_variants(kcat, lo2)
            kt_lo, kt_hi = _kv_variants_t(kcat)
            v_lo, v_hi = _kv_variants(jnp.concatenate([vp, vc], axis=0), lo2)
            qg32 = _swa_queries(sq_ref, rows, cosb, sinb, first_half)
            qg = [q.astype(BF) for q in qg32]
            ex, es, inv = _swa_softmax(_swa_scores(qg, k_lo, k_hi), prev_mask, _no_prev_bias(n) if j == 0 else None, sink)
            pr, ps = ex * inv, es * inv

            dog32 = []
            for g in range(2):
                parts = []
                for i in range(2):
                    ls = slice((2 * g + i) * LANES, (2 * g + i + 1) * LANES)
                    sz = sz_ref[rows, ls]
                    sg = _sigmoid(sz)
                    dos_p = dos_ref[rows, ls]
                    dsz_ref[rows, ls] = (dos_p * opre_ref[rows, ls] * (sg * (1.0 + sz * (1.0 - sg)))).astype(dsz_ref.dtype)
                    parts.append(dos_p * (sz * sg))
                dog32.append(jnp.concatenate(parts, axis=0))
            dog = [t.astype(BF) for t in dog32]
            dpr = _merge_band(jnp.concatenate([_dot(v_lo[0], dog[0], NT), _dot(v_hi[0], dog[0], NT),
                                               _dot(v_lo[1], dog[1], NT), _dot(v_hi[1], dog[1], NT)], axis=1), prev_mask)
            rd = jnp.sum(pr * dpr, axis=0, keepdims=True)
            ds = _split_band((pr * (dpr - rd)).astype(BF), prev_mask_b)
            prb = _split_band(pr.astype(BF), prev_mask_b)
            sink_term = ps * rd
            for r, hd in enumerate(_SWA_COL_HEADS):
                dsink_ref[0, hd] += -jnp.sum(sink_term[:, r * SWA_BLOCK:(r + 1) * SWA_BLOCK])

            dk_g, dv_g = [], []
            for g in range(2):
                c0, c1, c2 = 512 * g, 512 * g + 256, 512 * g + 512
                dq = (_dot(kt_lo[g], ds[:, c0:c1]) + _dot(kt_hi[g], ds[:, c1:c2])).T
                for i in range(2):
                    ls = slice((2 * g + i) * LANES, (2 * g + i + 1) * LANES)
                    dsq_ref[rows, ls] = _rope_t(dq[i * SWA_BLOCK:(i + 1) * SWA_BLOCK] * 0.125, cosb, sinb,
                                                first_half).astype(dsq_ref.dtype)
                q_split = jnp.concatenate([jnp.where(lo1s, qg32[g], 0.0), jnp.where(lo1s, 0.0, qg32[g])], axis=0).astype(BF)
                do_split = jnp.concatenate([jnp.where(lo1s, dog32[g], 0.0), jnp.where(lo1s, 0.0, dog32[g])], axis=0).astype(BF)
                dk_g.append(_dot(ds[:, c0:c2], q_split))
                dv_g.append(_dot(prb[:, c0:c2], do_split))
            dk = home(dk_g[0], dk_g[1])
            dv = home(dv_g[0], dv_g[1])
            cur = pl.ds(pl.multiple_of(blk * SWA_BLOCK, SWA_BLOCK), SWA_BLOCK)
            dsk_ref[cur, :] = _rope_t(dk[SWA_BLOCK:], cosb, sinb, first_half)
            dsv_ref[cur, :] = dv[SWA_BLOCK:]
            dk_prev = _rope_t(dk[:SWA_BLOCK], cp_, sp_, first_half)
            dv_prev = dv[:SWA_BLOCK]
            if j == 0:
                @pl.when(n > 0)
                def _():
                    prv = pl.ds(pl.multiple_of((blk - 1) * SWA_BLOCK, SWA_BLOCK), SWA_BLOCK)
                    dsk_ref[prv, :] += dk_prev
                    dsv_ref[prv, :] += dv_prev
            else:
                prv = pl.ds(pl.multiple_of((blk - 1) * SWA_BLOCK, SWA_BLOCK), SWA_BLOCK)
                dsk_ref[prv, :] += dk_prev
                dsv_ref[prv, :] += dv_prev
            kp, vp, cp_, sp_ = kc, vc, cosb, sinb
        kprev[...] = kp
        vprev[...] = vp
        cprev[...] = cp_
        sprev[...] = sp_

    def col(width, off):
        return pl.BlockSpec((tq, width), lambda i: (i, off // width))

    row = pl.BlockSpec((tq, LANES), lambda i: (i, 0))
    wide = pl.BlockSpec((tq, 512), lambda i: (i, 0))
    return pl.pallas_call(
        body, name="swa_bwd", grid=(s // tq,),
        in_specs=[col(512, OFF_SQ), col(512, OFF_SZ), col(LANES, OFF_SK), col(LANES, OFF_SV), wide, wide, row, row,
                  pl.BlockSpec(memory_space=pltpu.SMEM)],
        out_specs=[wide, wide, _full((s, LANES)), _full((s, LANES)), pl.BlockSpec(memory_space=pltpu.SMEM)],
        out_shape=[jax.ShapeDtypeStruct((s, 512), BF), jax.ShapeDtypeStruct((s, 512), BF),
                   jax.ShapeDtypeStruct((s, LANES), F32), jax.ShapeDtypeStruct((s, LANES), F32),
                   jax.ShapeDtypeStruct((1, SWA_HEADS), F32)],
        scratch_shapes=[pltpu.VMEM((SWA_BLOCK, LANES), F32)] * 4,
        compiler_params=_params(("arbitrary",)),
    )(proj, proj, proj, proj, dos, opre, cos, sin, sinks)


def _outproj(og, osw, w_out, x2d, target, gate, g_final):
    s = x2d.shape[0]
    tm = min(512, s)

    def body(og_ref, os_ref, w_ref, x_ref, t_ref, gate_ref, gf_ref,
             dx2_ref, dog_ref, dos_ref, dw_ref, loss_ref, dgf_ref, dgate_ref):
        @pl.when(pl.program_id(0) == 0)
        def _():
            dw_ref[...] = jnp.zeros_like(dw_ref)
            loss_ref[...] = jnp.zeros_like(loss_ref)
            dgf_ref[...] = jnp.zeros_like(dgf_ref)
            dgate_ref[...] = jnp.zeros_like(dgate_ref)

        w = w_ref[...]
        gate, gf = gate_ref[...], gf_ref[...]
        subs = _subtiles(tm)
        ogv = [og_ref[sl, :] for sl in subs]
        osv = [os_ref[sl, :] for sl in subs]
        y = [_dot(ogv[k], w[:512]) + _dot(osv[k], w[512:]) for k in range(len(subs))]
        dys = []
        for k, sl in enumerate(subs):
            x2 = x_ref[sl, :] + gate * y[k]
            r = lax.rsqrt(jnp.mean(x2 * x2, axis=-1, keepdims=True) + RMS_EPS)
            xn = x2 * r
            err = xn * gf - t_ref[sl, :]
            loss_ref[...] += 0.5 * jnp.sum(jnp.mean(err * err, axis=-1, keepdims=True), axis=0, keepdims=True)
            dyf = err * (1.0 / D_MODEL)
            dgf_ref[...] += jnp.sum(dyf * xn, axis=0, keepdims=True)
            t = dyf * gf
            dx2 = r * (t - xn * jnp.mean(t * xn, axis=-1, keepdims=True))
            dx2_ref[sl, :] = dx2
            dgate_ref[...] += jnp.sum(dx2 * y[k], axis=0, keepdims=True)
            dys.append((dx2 * gate).astype(BF))
            dmix = _dot(dys[k], w, NT)
            dog_ref[sl, :] = dmix[:, :512]
            dos_ref[sl, :] = dmix[:, 512:]
        dy = jnp.concatenate(dys, axis=0)
        dw_ref[:512, :] += _dot(og_ref[...], dy, TN)
        dw_ref[512:, :] += _dot(os_ref[...], dy, TN)

    half = pl.BlockSpec((tm, 512), lambda i: (i, 0))
    rowb = pl.BlockSpec((tm, D_MODEL), lambda i: (i, 0))
    vec = _full((1, D_MODEL))
    return pl.pallas_call(
        body, name="outproj", grid=(s // tm,),
        in_specs=[half, half, _full((D_MODEL, D_MODEL)), rowb, rowb, vec, vec],
        out_specs=[rowb, half, half, _full((D_MODEL, D_MODEL)), _full((1, 1)), vec, vec],
        out_shape=[jax.ShapeDtypeStruct((s, D_MODEL), F32), jax.ShapeDtypeStruct((s, 512), F32),
                   jax.ShapeDtypeStruct((s, 512), F32), jax.ShapeDtypeStruct((D_MODEL, D_MODEL), F32),
                   jax.ShapeDtypeStruct((1, 1), F32), jax.ShapeDtypeStruct((1, D_MODEL), F32),
                   jax.ShapeDtypeStruct((1, D_MODEL), F32)],
        compiler_params=_params(("arbitrary",)),
    )(og, osw, w_out, x2d, target, gate, g_final)


_PIECES = ((OFF_QK, 512), (OFF_V, 512), (OFF_GZ, 512), (OFF_SQ, 512), (OFF_SZ, 512),
           (OFF_SK, LANES), (OFF_SV, LANES), (OFF_GA, LANES))

_UNPAD_ROWS = ((OFF_QK, 0, 1024),
               (OFF_GA, 1024, GLA_RANK),
               (OFF_GZ, 1040, 1024),
               (OFF_SK, 2064, 256),
               (OFF_SZ, 2320, 512))


def _inproj_bwd(x2d, shift, sc1p, g_norm, wpad_t, dx2, pieces):
    s = x2d.shape[0]
    tm = min(512, s)
    nsteps = s // tm

    def body(x_ref, sh_ref, sc_ref, g_ref, w_hbm, dx2_ref, *rest):
        piece_refs = rest[:len(_PIECES)]
        gx_ref, dw_hbm, dsh_ref, dsc_ref, dg_ref, w_vm, dw_vm, sem, out_sems = rest[len(_PIECES):]
        i = pl.program_id(0)

        @pl.when(i == 0)
        def _():
            cp = pltpu.make_async_copy(w_hbm, w_vm, sem)
            cp.start()
            dw_vm[...] = jnp.zeros_like(dw_vm)
            dsh_ref[...] = jnp.zeros_like(dsh_ref)
            dsc_ref[...] = jnp.zeros_like(dsc_ref)
            dg_ref[...] = jnp.zeros_like(dg_ref)
            cp.wait()

        g, sc1p_v, shift_v = g_ref[...], sc_ref[...], sh_ref[...]
        subs = _subtiles(tm)
        dhs = []
        for sl in subs:
            dh = None
            for (off, width), pr in zip(_PIECES, piece_refs):
                part = _dot(pr[sl, :].astype(BF), w_vm[off:off + width, :])
                dh = part if dh is None else dh + part
            dhs.append(dh)
        norm = [_modnorm(x_ref[sl, :], g, sc1p_v, shift_v) for sl in subs]
        hb = jnp.concatenate([h.astype(BF) for _, _, h in norm], axis=0)
        for (off, width), pr in zip(_PIECES, piece_refs):
            dw_vm[off:off + width, :] += _dot(pr[...].astype(BF), hb, TN)
        for sl, (xn, r, _), dh in zip(subs, norm, dhs):
            dsh_ref[...] += jnp.sum(dh, axis=0, keepdims=True)
            dsc_ref[...] += jnp.sum(dh * (xn * g), axis=0, keepdims=True)
            dg_ref[...] += jnp.sum(dh * xn * sc1p_v, axis=0, keepdims=True)
            dxn = dh * g * sc1p_v
            gx_ref[sl, :] = dx2_ref[sl, :] + r * (dxn - xn * jnp.mean(dxn * xn, axis=-1, keepdims=True))

        @pl.when(i == nsteps - 1)
        def _():
            copies = [pltpu.make_async_copy(dw_vm.at[src:src + n], dw_hbm.at[dst:dst + n], out_sems.at[k])
                      for k, (src, dst, n) in enumerate(_UNPAD_ROWS)]
            for cp in copies:
                cp.start()
            for cp in copies:
                cp.wait()

    rowb = pl.BlockSpec((tm, D_MODEL), lambda i: (i, 0))
    vec = _full((1, D_MODEL))
    anyspec = pl.BlockSpec(memory_space=pl.ANY)
    piece_specs = [pl.BlockSpec((tm, width), lambda i: (i, 0)) for _, width in _PIECES]
    return pl.pallas_call(
        body, name="inproj_bwd", grid=(nsteps,),
        in_specs=[rowb, vec, vec, vec, anyspec, rowb] + piece_specs,
        out_specs=[rowb, anyspec, vec, vec, vec],
        out_shape=[jax.ShapeDtypeStruct((s, D_MODEL), F32), jax.ShapeDtypeStruct((D_IN, D_MODEL), F32),
                   jax.ShapeDtypeStruct((1, D_MODEL), F32), jax.ShapeDtypeStruct((1, D_MODEL), F32),
                   jax.ShapeDtypeStruct((1, D_MODEL), F32)],
        scratch_shapes=[pltpu.VMEM((D_PAD, D_MODEL), BF), pltpu.VMEM((D_PAD, D_MODEL), F32), pltpu.SemaphoreType.DMA,
                        pltpu.SemaphoreType.DMA((len(_UNPAD_ROWS),))],
        compiler_params=_params(("arbitrary",)),
    )(x2d, shift, sc1p, g_norm, wpad_t, dx2, *pieces)


def _adam(w, g, m, v):
    m2 = ADAM_B1 * m + (1.0 - ADAM_B1) * g
    v2 = ADAM_B2 * v + (1.0 - ADAM_B2) * (g * g)
    m_hat = m2 / (1.0 - ADAM_B1 ** ADAM_STEP)
    v_hat = v2 / (1.0 - ADAM_B2 ** ADAM_STEP)
    delta = -ADAM_LR * (m_hat / (jnp.sqrt(v_hat) + ADAM_EPS) + ADAM_WD * w)
    return delta, m2, v2


def _adamw(w, g, m, v, name):
    rr, cc = w.shape
    tc = min(256, cc)

    def body(w_ref, g_ref, m_ref, v_ref, d_ref, m2_ref, v2_ref):
        d_ref[...], m2_ref[...], v2_ref[...] = _adam(w_ref[...], g_ref[...], m_ref[...], v_ref[...])

    blk = pl.BlockSpec((rr, tc), lambda i: (0, i))
    return pl.pallas_call(
        body, name=name, grid=(cc // tc,), in_specs=[blk] * 4, out_specs=[blk] * 3,
        out_shape=[jax.ShapeDtypeStruct((rr, cc), F32)] * 3,
        compiler_params=_params(("arbitrary",)),
    )(w, g, m, v)


def _adamw_t(w3, g, m3, v3, name):
    rr, _, cc = w3.shape
    tc = min(256, cc)

    def body(w_ref, g_ref, m_ref, v_ref, d_ref, m2_ref, v2_ref, g3_ref):
        g = g_ref[...]
        d_ref[:, 0, :], m2_ref[:, 0, :], v2_ref[:, 0, :] = _adam(w_ref[:, 0, :], g, m_ref[:, 0, :], v_ref[:, 0, :])
        g3_ref[:, 0, :] = g

    b3 = pl.BlockSpec((rr, 1, tc), lambda i: (0, 0, i))
    return pl.pallas_call(
        body, name=name, grid=(cc // tc,), in_specs=[b3, pl.BlockSpec((rr, tc), lambda i: (0, i)), b3, b3],
        out_specs=[b3] * 4, out_shape=[jax.ShapeDtypeStruct((rr, 1, cc), F32)] * 4,
        compiler_params=_params(("arbitrary",)),
    )(w3, g, m3, v3)


def _ada_update(c_all, dmod_cols, w, m, v):
    rr, cc = w.shape
    tr = min(256, rr)
    c_all = jnp.pad(c_all, ((0, 8), (0, 0)))
    dmod_cols = jnp.pad(dmod_cols, ((0, 8), (0, 0)))

    def body(c_ref, dm_ref, w_ref, m_ref, v_ref, g_ref, d_ref, m2_ref, v2_ref):
        cv = c_ref[...]
        sc = (cv * _sigmoid(cv)).astype(BF)
        g = _dot(sc, dm_ref[...].astype(BF), TN)
        g_ref[...] = g
        d_ref[...], m2_ref[...], v2_ref[...] = _adam(w_ref[...], g, m_ref[...], v_ref[...])

    blk = pl.BlockSpec((tr, cc), lambda i: (i, 0))
    return pl.pallas_call(
        body, name="ada_update", grid=(rr // tr,),
        in_specs=[pl.BlockSpec((16, tr), lambda i: (0, i)), _full((16, cc)), blk, blk, blk],
        out_specs=[blk] * 4, out_shape=[jax.ShapeDtypeStruct((rr, cc), F32)] * 4,
        compiler_params=_params(("arbitrary",)),
    )(c_all, dmod_cols, w, m, v)


def _small_update(parts, weights, moms, vels):
    n = len(weights)

    def body(*refs):
        p_refs, w_refs, m_refs, v_refs = refs[:n + 1], refs[n + 1:2 * n + 1], refs[2 * n + 1:3 * n + 1], refs[3 * n + 1:4 * n + 1]
        outs = refs[4 * n + 1:]
        for i in range(n):
            g = p_refs[i][0]
            for d in range(1, 8):
                g = g + p_refs[i][d]
            delta, m2, v2 = _adam(w_refs[i][...], g, m_refs[i][...], v_refs[i][...])
            outs[4 * i][...] = g
            outs[4 * i + 1][...] = delta
            outs[4 * i + 2][...] = m2
            outs[4 * i + 3][...] = v2
        tot = p_refs[n][0]
        for d in range(1, 8):
            tot = tot + p_refs[n][d]
        outs[4 * n][...] = tot

    out_shape = []
    for w in weights:
        out_shape += [jax.ShapeDtypeStruct(w.shape, F32)] * 4
    out_shape.append(jax.ShapeDtypeStruct(parts[n].shape[1:], F32))
    return pl.pallas_call(body, name="small_update", out_shape=out_shape, compiler_params=_params())(
        *parts, *weights, *moms, *vels)


def _pad_w_in_t(w):
    pad = jnp.zeros((LANES - GLA_RANK, w.shape[1]), w.dtype)
    return jnp.concatenate([w[dst:dst + n] for _, dst, n in sorted(_UNPAD_ROWS)] + [pad], axis=0)


def _rows8(a):
    flat = a.reshape(-1)
    rows = -(-flat.shape[0] // LANES)
    rows8 = -(-rows // 8) * 8
    flat = jnp.pad(flat, (0, rows8 * LANES - flat.shape[0]))
    return flat.reshape(rows8, LANES)


def kernel(x, c, positions, w_ada, b_ada, g_norm, w_in, w_decay, b_decay, g_gla_head, sinks, w_out, g_final, loss_target, m_w_ada, m_b_ada, m_g_norm, m_w_in, m_w_decay, m_b_decay, m_g_gla_head, m_sinks, m_w_out, m_g_final, v_w_ada, v_b_ada, v_g_norm, v_w_in, v_w_decay, v_b_decay, v_g_gla_head, v_sinks, v_w_out, v_g_final):
    ax, ay, ac = lax.axis_index("x"), lax.axis_index("y"), lax.axis_index("c")
    chip = 2 * ax + ay
    dev = 2 * chip + ac
    s = x.shape[1]
    x2d = x[0]
    target = loss_target[0]
    w_ada2, w_out2, w_dec2 = w_ada[0], w_out[0], w_decay[0]
    w_in_t = w_in[0].T
    ada_cols = w_ada2.shape[1]
    in_cols = w_in_t.shape[0]
    out_rows = w_out2.shape[0]
    half = D_MODEL // 2

    cw = jnp.concatenate([c.reshape(8, LANES), w_dec2.reshape(8, LANES)], axis=0)
    b_shard = lax.dynamic_slice(b_ada, (0, chip * ada_cols), (1, ada_cols))
    half_in = lax.dynamic_slice(w_in_t, (0, ac * half), (in_cols, half)).astype(BF)
    half_out = lax.dynamic_slice(w_out2, (ac * (out_rows // 2), 0), (out_rows // 2, D_MODEL)).astype(BF)
    inv_freq = 1.0 / (ROPE_THETA ** (jnp.arange(0, 64, 2, dtype=F32) / 64))
    first, mod_all, w_in_all, w_out_all, cos, sin = _prologue(
        cw, w_ada2, b_shard, half_in, half_out, positions.reshape(s, 1), jnp.tile(inv_freq, 4).reshape(1, LANES))

    first = first.reshape(8, 2, 8, LANES)
    c_all = first[:, 0].reshape(8, D_MODEL)
    w_dec_full = first[0::2, 1].reshape(4, GLA_RANK, 64).transpose(1, 0, 2).reshape(GLA_RANK, 256)
    mod = mod_all.reshape(4, 2, 8, ada_cols)[:, 0]
    mod = lax.dynamic_slice(mod, (0, dev, 0), (4, 1, ada_cols)).reshape(1, 4 * ada_cols)
    shift, sc1p, gate = mod[:, :D_MODEL], 1.0 + mod[:, D_MODEL:2 * D_MODEL], mod[:, 2 * D_MODEL:]
    w_in_all = w_in_all.reshape(4, 2, in_cols, half)
    wpad_t = _pad_w_in_t(w_in_all.transpose(0, 2, 1, 3).reshape(4 * in_cols, D_MODEL))
    w_out_all = w_out_all.reshape(D_MODEL, D_MODEL)

    wdecp = jnp.pad(w_dec_full, ((0, LANES - GLA_RANK), (0, 0))).astype(BF)

    proj = _inproj_fwd(x2d, shift, sc1p, g_norm, wpad_t)
    og, o_gla, sprev = _gla_fwd(proj, wdecp, b_decay, g_gla_head)
    osw, o_swa = _swa_fwd(proj, cos, sin, sinks)
    dx2, dog, dos, dw_out, loss_p, dgf, dgate = _outproj(og, osw, w_out_all, x2d, target, gate, g_final.reshape(1, D_MODEL))
    dsq, dsz, dsk, dsv, dsinks = _swa_bwd(proj, dos, o_swa, cos, sin, sinks)
    dqk, dv, dgz, dga, dwdp, dbd, dgg = _gla_bwd(proj, dog, o_gla, sprev, wdecp, b_decay, g_gla_head)
    pieces = (dqk, dv, dgz, dsq, dsz, dsk, dsv, dga)
    gx, dw_in_t, dshift, dscale, dgn = _inproj_bwd(x2d, shift, sc1p, g_norm, wpad_t, dx2, pieces)

    segs = [jnp.concatenate([dshift, dscale, dgate], axis=1), dgn, dgf, dwdp[:GLA_RANK], dbd, dgg, dsinks, loss_p]
    packed = [_rows8(a) for a in segs]
    offs = [0]
    for a in packed:
        offs.append(offs[-1] + a.shape[0])
    g_w_in_t, g_w_out, small = _epilogue(dw_in_t.reshape(4, in_cols, D_MODEL), dw_out.reshape(4, out_rows, D_MODEL),
                                         jnp.concatenate(packed, axis=0))

    def seg(i, size):
        return small[:, offs[i]:offs[i + 1]].reshape(8, -1)[:, :size]

    dmod_all = seg(0, 3 * D_MODEL)
    dwd_all = lax.dynamic_slice(seg(3, GLA_RANK * 256).reshape(8, GLA_RANK, 256), (0, 0, chip * 64), (8, GLA_RANK, 64))
    parts = [dmod_all.reshape(8, 1, 3 * D_MODEL), seg(1, D_MODEL).reshape(8, 1, D_MODEL), dwd_all,
             seg(4, 256).reshape(8, 1, 256), seg(5, 512).reshape(8, 1, 512), seg(6, SWA_HEADS).reshape(8, 1, SWA_HEADS),
             seg(2, D_MODEL).reshape(8, 1, D_MODEL), seg(7, LANES).reshape(8, 1, LANES)]
    smalls = _small_update(
        parts,
        [b_ada, g_norm, w_dec2, b_decay, g_gla_head, sinks, g_final.reshape(1, D_MODEL)],
        [m_b_ada, m_g_norm, m_w_decay[0], m_b_decay, m_g_gla_head, m_sinks, m_g_final.reshape(1, D_MODEL)],
        [v_b_ada, v_g_norm, v_w_decay[0], v_b_decay, v_g_gla_head, v_sinks, v_g_final.reshape(1, D_MODEL)])
    (g_b_ada, d_b_ada, nm_b_ada, nv_b_ada, g_gn, d_gn, nm_gn, nv_gn, g_wd, d_wd, nm_wd, nv_wd,
     g_bd, d_bd, nm_bd, nv_bd, g_gg, d_gg, nm_gg, nv_gg, g_sk, d_sk, nm_sk, nv_sk,
     g_gf, d_gf, nm_gf, nv_gf, loss_row) = smalls
    loss = loss_row[0, 0]

    dmod_cols = lax.dynamic_slice(dmod_all, (0, chip * ada_cols), (8, ada_cols))
    g_w_ada, d_w_ada, nm_w_ada, nv_w_ada = _ada_update(c_all, dmod_cols, w_ada2, m_w_ada[0], v_w_ada[0])
    to3 = lambda a: jnp.transpose(a, (2, 0, 1))
    from3 = lambda a: jnp.transpose(a, (1, 2, 0))[0]
    d3, nm3, nv3, g3 = _adamw_t(to3(w_in), g_w_in_t, to3(m_w_in), to3(v_w_in), "adamw_w_in")
    g_w_in, d_w_in, nm_w_in, nv_w_in = from3(g3), from3(d3), from3(nm3), from3(nv3)
    d_w_out, nm_w_out, nv_w_out = _adamw(w_out2, g_w_out, m_w_out[0], v_w_out[0], "adamw_w_out")

    flat = lambda a: a.reshape(D_MODEL)
    grads = [g_w_ada[None], g_b_ada, g_gn, g_w_in[None], g_wd[None], g_bd, g_gg, g_sk, g_w_out[None], flat(g_gf)]
    deltas = [d_w_ada[None], d_b_ada, d_gn, d_w_in[None], d_wd[None], d_bd, d_gg, d_sk, d_w_out[None], flat(d_gf)]
    new_m = [nm_w_ada[None], nm_b_ada, nm_gn, nm_w_in[None], nm_wd[None], nm_bd, nm_gg, nm_sk, nm_w_out[None], flat(nm_gf)]
    new_v = [nv_w_ada[None], nv_b_ada, nv_gn, nv_w_in[None], nv_wd[None], nv_bd, nv_gg, nv_sk, nv_w_out[None], flat(nv_gf)]
    return (loss, gx[None], *grads, *deltas, *new_m, *new_v)
```

```python
import jax
import jax.numpy as jnp
from jax import lax
from jax.experimental import pallas as pl
from jax.experimental.pallas import tpu as pltpu

F32 = jnp.float32
BF = jnp.bfloat16

D_MODEL = 1024
GLA_HEADS = 4
GLA_DK = 64
GLA_CHUNK = 64
GLA_RANK = 16
GLA_TAU = 16.0
GLA_SUB = 256
GLA_ROWS = 512
SWA_HEADS = 8
SWA_BLOCK = 128
SWA_QBLOCKS = 8
RMS_EPS = 1e-6
ROPE_THETA = 10000.0

OFF_QK, OFF_V, OFF_GZ, OFF_SQ, OFF_SZ, OFF_SK, OFF_SV, OFF_GA = 0, 512, 1024, 1536, 2048, 2560, 2688, 2816
D_PAD = 2944
D_IN = 2832
LANES = 128
VMEM_LIMIT = 56 * 1024 * 1024

ADAM_LR, ADAM_B1, ADAM_B2, ADAM_EPS, ADAM_WD, ADAM_STEP = 0.001, 0.9, 0.999, 1e-08, 0.01, 10

NT = (((1,), (1,)), ((), ()))
TN = (((0,), (0,)), ((), ()))
MESH = pl.DeviceIdType.MESH


def _dot(a, b, dims=None):
    if dims is None:
        return jnp.dot(a, b, preferred_element_type=F32)
    return lax.dot_general(a, b, dims, preferred_element_type=F32)


def _sigmoid(x):
    return 1.0 / (1.0 + jnp.exp(-x))


def _params(sem=None):
    return pltpu.CompilerParams(dimension_semantics=sem, vmem_limit_bytes=VMEM_LIMIT)


def _full(shape):
    return pl.BlockSpec(shape, lambda i: (0,) * len(shape))


def _subtiles(rows, size=256):
    size = min(size, rows)
    return [slice(k * size, (k + 1) * size) for k in range(rows // size)]


_GATHER_SEMS = [pltpu.SemaphoreType.DMA((7,)), pltpu.SemaphoreType.DMA((7,)), pltpu.SemaphoreType.DMA]


class _Gather:
    def __init__(self, x_ref, out_ref, send_sems, recv_sems, local_sem):
        x, y, c = lax.axis_index("x"), lax.axis_index("y"), lax.axis_index("c")
        self.me, self.sibling, self.c = (x, y, c), (x, y, 1 - c), c
        self.xn, self.yn, self.dg = (1 - x, y), (x, 1 - y), (1 - x, 1 - y)
        self.pass_from = (lax.rem(x + 1 - c, 2), lax.rem(y + c, 2))
        self.pass_to = (lax.rem(x + c, 2), lax.rem(y + 1 - c, 2))
        self.x_ref, self.out_ref, self.send_sems, self.recv_sems = x_ref, out_ref, send_sems, recv_sems
        self.mine = pltpu.make_async_copy(x_ref, self._slab(*self.me), local_sem)

    def _slab(self, px, py, pc):
        return self.out_ref.at[4 * px + 2 * py + pc]

    def _copy(self, k, blk, to, src=None):
        return pltpu.make_async_remote_copy(
            src_ref=self._slab(*blk) if src is None else src, dst_ref=self._slab(*blk),
            send_sem=self.send_sems.at[k], recv_sem=self.recv_sems.at[k], device_id=to, device_id_type=MESH)

    def _sends(self):
        c = self.c
        return [self._copy(0, self.me, self.sibling, src=self.x_ref),
                self._copy(1, self.me, (*self.xn, c), src=self.x_ref),
                self._copy(2, self.me, (*self.yn, c), src=self.x_ref),
                self._copy(3, (*self.pass_from, c), (*self.pass_to, c)),
                self._copy(4, (*self.xn, c), self.sibling),
                self._copy(5, (*self.yn, c), self.sibling),
                self._copy(6, (*self.dg, c), self.sibling)]

    def start(self):
        self.mine.start()
        for cp in self._sends()[0:3]:
            cp.start()

    def pass_on(self):
        sends = self._sends()
        self._copy(1, (*self.xn, self.c), self.me).wait_recv()
        self._copy(2, (*self.yn, self.c), self.me).wait_recv()
        for k in (3, 4, 5):
            sends[k].start()

    def relay_diagonal(self):
        self._copy(3, (*self.dg, self.c), self.me).wait_recv()
        self._sends()[6].start()

    def relay(self):
        self.pass_on()
        self.relay_diagonal()

    def finish(self):
        c = self.c
        self._copy(0, self.sibling, self.me).wait_recv()
        for k, chip in ((4, self.xn), (5, self.yn), (6, self.dg)):
            self._copy(k, (*chip, 1 - c), self.me).wait_recv()
        for cp in self._sends():
            cp.wait_send()
        self.mine.wait()


def _prologue(cw, w_ada, b_shard, half_in, half_out, pos_col, inv_freq):
    s = pos_col.shape[0]
    rt = min(512, s)

    def body(cw_ref, wada_ref, b_ref, hin_ref, hout_ref, pos_ref, f_ref,
             first_ref, mod_ref, win_ref, wout_ref, cos_hbm, sin_hbm, mod_blk, cos_ref, sin_ref, table_sems, *sems):
        g_c = _Gather(cw_ref, first_ref, *sems[0:3])
        g_in = _Gather(hin_ref, win_ref, *sems[3:6])
        g_out = _Gather(hout_ref, wout_ref, *sems[6:9])
        g_mod = _Gather(mod_blk, mod_ref, *sems[9:12])
        g_c.start()
        g_in.start()
        g_out.start()
        g_c.relay()
        g_c.finish()
        c_rows = [jnp.concatenate([first_ref[d, r:r + 1, :] for r in range(8)], axis=1) for d in range(8)]
        c_all = jnp.concatenate(c_rows, axis=0)
        sc = (c_all * _sigmoid(c_all)).astype(BF)
        mod_blk[...] = _dot(sc, wada_ref[...].astype(BF)) + b_ref[...]
        g_mod.start()

        def rope_rows(i, carry):
            rows = pl.ds(pl.multiple_of(i * rt, rt), rt)
            ang = pos_ref[rows, :].astype(F32) * f_ref[...]
            lane = lax.broadcasted_iota(jnp.int32, ang.shape, 1)
            cos_ref[rows, :] = jnp.cos(ang)
            sn = jnp.sin(ang)
            sin_ref[rows, :] = jnp.where((lane % 64) < 32, -sn, sn)
            pltpu.make_async_copy(cos_ref.at[rows, :], cos_hbm.at[rows, :], table_sems.at[0]).start()
            pltpu.make_async_copy(sin_ref.at[rows, :], sin_hbm.at[rows, :], table_sems.at[1]).start()
            return carry

        steps = s // rt
        lax.fori_loop(0, steps // 2, rope_rows, 0)
        g_in.pass_on()
        g_out.pass_on()
        lax.fori_loop(steps // 2, steps, rope_rows, 0)
        g_in.relay_diagonal()
        g_out.relay_diagonal()
        g_mod.relay()
        g_in.finish()
        g_out.finish()
        g_mod.finish()
        pltpu.make_async_copy(cos_ref, cos_hbm, table_sems.at[0]).wait()
        pltpu.make_async_copy(sin_ref, sin_hbm, table_sems.at[1]).wait()

    vm = pl.BlockSpec(memory_space=pltpu.VMEM)
    hbm = pl.BlockSpec(memory_space=pl.ANY)
    return pl.pallas_call(
        body, name="prologue",
        out_shape=[jax.ShapeDtypeStruct((8,) + cw.shape, F32), jax.ShapeDtypeStruct((8, 8, w_ada.shape[1]), F32),
                   jax.ShapeDtypeStruct((8,) + half_in.shape, half_in.dtype),
                   jax.ShapeDtypeStruct((8,) + half_out.shape, half_out.dtype),
                   jax.ShapeDtypeStruct((s, LANES), F32), jax.ShapeDtypeStruct((s, LANES), F32)],
        in_specs=[vm, vm, vm, hbm, hbm, vm, vm], out_specs=[vm, vm, hbm, hbm, hbm, hbm],
        scratch_shapes=[pltpu.VMEM((8, w_ada.shape[1]), F32), pltpu.VMEM((s, LANES), F32), pltpu.VMEM((s, LANES), F32),
                        pltpu.SemaphoreType.DMA((2,))] + _GATHER_SEMS * 4,
        compiler_params=pltpu.CompilerParams(vmem_limit_bytes=VMEM_LIMIT),
    )(cw, w_ada, b_shard, half_in, half_out, pos_col, inv_freq)


def _reduce_scratch(rr, cc):
    c2 = cc // 2
    return [pltpu.VMEM((4, rr, c2), F32), pltpu.VMEM((4, rr, c2), F32), pltpu.VMEM((3, rr, c2), BF),
            pltpu.VMEM((2, rr, c2), BF), pltpu.VMEM((rr, c2), BF), pltpu.VMEM((rr, c2), F32),
            pltpu.SemaphoreType.DMA((8,)), pltpu.SemaphoreType.DMA((8,)), pltpu.SemaphoreType.DMA((5,))]


class _Reduce:
    def __init__(self, p_hbm, out_ref, acc_ref, own_ref, send_ref, land_ref, relay_ref, res_ref,
                 send_sems, recv_sems, local_sems):
        x, y, c = lax.axis_index("x"), lax.axis_index("y"), lax.axis_index("c")
        c2 = out_ref.shape[1] // 2
        sibling = (x, y, 1 - c)
        first = (lax.rem(x + 1 - c, 2), lax.rem(y + c, 2))
        second = (lax.rem(x + c, 2), lax.rem(y + 1 - c, 2))
        shards = [2 * first[0] + first[1], 2 * second[0] + second[1], 2 * (1 - x) + (1 - y), 2 * x + y]
        sibling_slot = (1, 0, 2, 3)
        mine = pl.ds(pl.multiple_of(c * c2, c2), c2)
        other = pl.ds(pl.multiple_of((1 - c) * c2, c2), c2)
        self.acc_ref, self.own_ref, self.send_ref, self.land_ref = acc_ref, own_ref, send_ref, land_ref
        self.relay_ref, self.res_ref = relay_ref, res_ref
        self.own = [pltpu.make_async_copy(p_hbm.at[j, :, mine], own_ref.at[k], local_sems.at[k])
                    for k, j in enumerate(shards)]
        self.swap_out = [pltpu.make_async_remote_copy(
            src_ref=p_hbm.at[j, :, other], dst_ref=acc_ref.at[sibling_slot[k]], send_sem=send_sems.at[k],
            recv_sem=recv_sems.at[sibling_slot[k]], device_id=sibling, device_id_type=MESH) for k, j in enumerate(shards)]
        self.swap_in = [pltpu.make_async_remote_copy(
            src_ref=p_hbm.at[j, :, other], dst_ref=acc_ref.at[k], send_sem=send_sems.at[k], recv_sem=recv_sems.at[k],
            device_id=sibling, device_id_type=MESH) for k, j in enumerate(shards)]

        def message(k, src, dst, to):
            return pltpu.make_async_remote_copy(src_ref=src, dst_ref=dst, send_sem=send_sems.at[k], recv_sem=recv_sems.at[k],
                                                device_id=(*to, c), device_id_type=MESH)

        self.direct = message(4, send_ref.at[0], land_ref.at[0], first)
        self.passed = message(5, send_ref.at[1], relay_ref, first)
        self.joint = message(6, send_ref.at[2], land_ref.at[1], second)
        self.put = pltpu.make_async_copy(res_ref, out_ref.at[:, mine], local_sems.at[4])
        self.share = pltpu.make_async_remote_copy(
            src_ref=res_ref, dst_ref=out_ref.at[:, mine], send_sem=send_sems.at[7],
            recv_sem=recv_sems.at[7], device_id=sibling, device_id_type=MESH)

    def start(self):
        for k in (0, 2, 1, 3):
            self.own[k].start()
            self.swap_out[k].start()

    def _combine(self, k):
        self.own[k].wait()
        self.swap_out[k].wait_send()
        self.swap_in[k].wait_recv()
        self.acc_ref[k] = self.acc_ref[k] + self.own_ref[k]

    def combine_and_send(self):
        dt = self.send_ref.dtype
        self._combine(0)
        self.send_ref[0] = self.acc_ref[0].astype(dt)
        self.direct.start()
        self._combine(2)
        self.send_ref[1] = self.acc_ref[2].astype(dt)
        self.passed.start()
        self._combine(1)
        self.passed.wait_recv()
        self.send_ref[2] = (self.acc_ref[1] + self.relay_ref[...].astype(F32)).astype(dt)
        self.joint.start()
        self._combine(3)

    def total_and_share(self):
        self.direct.wait_recv()
        self.joint.wait_recv()
        self.res_ref[...] = self.acc_ref[3] + self.land_ref[0].astype(F32) + self.land_ref[1].astype(F32)
        for cp in (self.direct, self.passed, self.joint):
            cp.wait_send()
        self.put.start()
        self.share.start()

    def finish(self):
        self.put.wait()
        self.share.wait()


def _epilogue(dw_in_parts, dw_out_parts, small):
    _, r_in, cc = dw_in_parts.shape
    _, r_out, _ = dw_out_parts.shape
    n_red = len(_reduce_scratch(r_in, cc))

    def body(pin_hbm, pout_hbm, small_ref, gin_ref, gout_ref, small_all_ref, *scratch):
        red_in = _Reduce(pin_hbm, gin_ref, *scratch[0:n_red])
        red_out = _Reduce(pout_hbm, gout_ref, *scratch[n_red:2 * n_red])
        gat = _Gather(small_ref, small_all_ref, *scratch[2 * n_red:])
        red_out.start()
        red_in.start()
        gat.start()
        red_out.combine_and_send()
        red_in.combine_and_send()
        gat.relay()
        red_out.total_and_share()
        red_in.total_and_share()
        gat.finish()
        red_out.finish()
        red_in.finish()

    vm = pl.BlockSpec(memory_space=pltpu.VMEM)
    anyspec = pl.BlockSpec(memory_space=pl.ANY)
    return pl.pallas_call(
        body, name="epilogue",
        out_shape=[jax.ShapeDtypeStruct((r_in, cc), F32), jax.ShapeDtypeStruct((r_out, cc), F32),
                   jax.ShapeDtypeStruct((8,) + small.shape, F32)],
        in_specs=[anyspec, anyspec, vm], out_specs=[anyspec, anyspec, vm],
        scratch_shapes=_reduce_scratch(r_in, cc) + _reduce_scratch(r_out, cc) + _GATHER_SEMS,
        compiler_params=pltpu.CompilerParams(vmem_limit_bytes=VMEM_LIMIT),
    )(dw_in_parts, dw_out_parts, small)


def _rope(t, cosb, sinb, first_half):
    partner = jnp.where(first_half, pltpu.roll(t, 96, 1), pltpu.roll(t, 32, 1))
    return t * cosb + partner * sinb


def _rope_t(g, cosb, sinb, first_half):
    gs = g * sinb
    partner = jnp.where(first_half, pltpu.roll(gs, 96, 1), pltpu.roll(gs, 32, 1))
    return g * cosb + partner


def _modnorm(x, g, sc1p, shift):
    r = lax.rsqrt(jnp.mean(x * x, axis=-1, keepdims=True) + RMS_EPS)
    xn = x * r
    return xn, r, (xn * g) * sc1p + shift


def _inproj_fwd(x2d, shift, sc1p, g_norm, wpad_t):
    s = x2d.shape[0]
    tm = min(512, s)

    def body(x_ref, sh_ref, sc_ref, g_ref, w_ref, o_ref):
        subs = _subtiles(tm)
        hs = [_modnorm(x_ref[sl, :], g_ref[...], sc_ref[...], sh_ref[...])[2].astype(BF) for sl in subs]
        for sl, h in zip(subs, hs):
            o_ref[sl, :] = _dot(h, w_ref[...], NT)

    vec = _full((1, D_MODEL))
    return pl.pallas_call(
        body, name="inproj_fwd", grid=(s // tm,),
        in_specs=[pl.BlockSpec((tm, D_MODEL), lambda i: (i, 0)), vec, vec, vec, _full((D_PAD, D_MODEL))],
        out_specs=pl.BlockSpec((tm, D_PAD), lambda i: (i, 0)),
        out_shape=jax.ShapeDtypeStruct((s, D_PAD), F32),
        compiler_params=_params(("arbitrary",)),
    )(x2d, shift, sc1p, g_norm, wpad_t)


def _split3(a):
    hi = a.astype(BF)
    r1 = a - hi.astype(F32)
    mid = r1.astype(BF)
    lo = (r1 - mid.astype(F32)).astype(BF)
    return hi, mid, lo


def _tri_matmul(tri, a):
    hi, mid, lo = _split3(a)
    return _dot(tri, hi) + _dot(tri, mid) + _dot(tri, lo)


def _chunks(tb):
    return [slice(c * GLA_CHUNK, (c + 1) * GLA_CHUNK) for c in range(tb // GLA_CHUNK)]


def _per_chunk_rows(rows, width):
    return jnp.concatenate([jnp.broadcast_to(r, (GLA_CHUNK, width)) for r in rows], axis=0)


def _gla_triangle(tb):
    row = lax.broadcasted_iota(jnp.int32, (tb, tb), 0)
    col = lax.broadcasted_iota(jnp.int32, (tb, tb), 1)
    return (((row // GLA_CHUNK) == (col // GLA_CHUNK)) & (col <= row)).astype(F32)


def _lane_mean(x, ones_b):
    hi = x.astype(BF)
    lo = (x - hi.astype(F32)).astype(BF)
    return (_dot(hi, ones_b) + _dot(lo, ones_b)) * (1.0 / LANES)


def _head(t, h, lo_h):
    blk = t[:, LANES * (h // 2):LANES * (h // 2 + 1)]
    return jnp.where(lo_h, blk, 0.0) if h % 2 == 0 else jnp.where(lo_h, 0.0, blk)


def _gla_block_common(qk, ga, wd, bd, tril_b):
    tb = qk.shape[0]
    q, k = qk[:, :256], qk[:, 256:]
    z = _dot(ga.astype(BF), wd) + bd
    la = (jnp.minimum(z, 0.0) - jnp.log(1.0 + jnp.exp(-jnp.abs(z)))) * (1.0 / GLA_TAU)
    b = _tri_matmul(tril_b, la)
    bls = [b[rs.stop - 1:rs.stop, :] for rs in _chunks(tb)]
    eq = jnp.exp(b)
    ek = jnp.exp(-b)
    f = jnp.exp(_per_chunk_rows(bls, 256) - b)
    return z, eq, ek, f, q * (eq * GLA_DK ** -0.5), k * ek, k * f, bls


def _gla_units(s):
    sub = min(GLA_SUB, s)
    tb = min(GLA_ROWS, s)
    subs = [slice(i * sub, (i + 1) * sub) for i in range(tb // sub)]
    units = [(i, h) for i in range(len(subs)) for h in range(GLA_HEADS)]
    return tb, sub, subs, units


def _gla_fwd(proj, wdecp, bdec, ggla):
    s = proj.shape[0]
    tb, sub, subs, units = _gla_units(s)
    nch = sub // GLA_CHUNK

    def body(qk_ref, v_ref, gz_ref, ga_ref, wd_ref, bd_ref, gg_ref, tri_ref, og_ref, opre_ref, sprev_ref, st_ref):
        @pl.when(pl.program_id(0) == 0)
        def _():
            st_ref[...] = jnp.zeros_like(st_ref)

        lo_h = lax.broadcasted_iota(jnp.int32, (sub, LANES), 1) < GLA_DK
        tril = tri_ref[...] > 0.5
        tril_b = tri_ref[...].astype(BF)
        ones_b = jnp.ones((LANES, LANES), BF)
        gg, wd, bd = gg_ref[...], wd_ref[...], bd_ref[...]
        chunks = _chunks(sub)
        lanes = [slice(h * LANES, (h + 1) * LANES) for h in range(GLA_HEADS)]
        com = [_gla_block_common(qk_ref[sl, :], ga_ref[sl, :], wd, bd, tril_b) for sl in subs]
        decs = [[jnp.exp(bl) for bl in cm[7]] for cm in com]
        a = {(i, h): _head(com[i][4], h, lo_h).astype(BF) for i, h in units}
        bm = {(i, h): _head(com[i][5], h, lo_h).astype(BF) for i, h in units}
        ktl = {(i, h): _head(com[i][6], h, lo_h).astype(BF) for i, h in units}
        vh = {(i, h): v_ref[subs[i], lanes[h]].astype(BF) for i, h in units}
        sc = {u: _dot(a[u], bm[u], NT) for u in units}
        upd = {u: [_dot(vh[u][rs], ktl[u][rs], TN) for rs in chunks] for u in units}
        p = {u: jnp.where(tril, sc[u], 0.0).astype(BF) for u in units}
        o = {u: _dot(p[u], vh[u]) for u in units}
        states = {}
        for h in range(GLA_HEADS):
            st = st_ref[h]
            for i in range(len(subs)):
                entering = []
                for c in range(nch):
                    entering.append(st)
                    sprev_ref[i * nch + c, h] = st
                    st = st * decs[i][c][:, LANES * (h // 2):LANES * (h // 2 + 1)] + upd[(i, h)][c]
                states[(i, h)] = entering
            st_ref[h] = st
        inter = {u: [_dot(a[u][rs], states[u][c].astype(BF), NT) for c, rs in enumerate(chunks)] for u in units}
        o = {u: o[u] + jnp.concatenate(inter[u], axis=0) for u in units}
        ms = {u: _lane_mean(o[u] * o[u], ones_b) for u in units}
        for i, h in units:
            gzh = gz_ref[subs[i], lanes[h]]
            opre_ref[subs[i], lanes[h]] = o[(i, h)]
            og_ref[subs[i], lanes[h]] = (((o[(i, h)] * lax.rsqrt(ms[(i, h)] + RMS_EPS)) * gg[:, lanes[h]])
                                         * (gzh * _sigmoid(gzh))).astype(og_ref.dtype)

    def col(width, off):
        return pl.BlockSpec((tb, width), lambda i: (i, off // width))

    return pl.pallas_call(
        body, name="gla_fwd", grid=(s // tb,),
        in_specs=[col(512, OFF_QK), col(512, OFF_V), col(512, OFF_GZ), col(LANES, OFF_GA),
                  _full((LANES, 256)), _full((1, 256)), _full((1, 512)), _full((sub, sub))],
        out_specs=[pl.BlockSpec((tb, 512), lambda i: (i, 0)), pl.BlockSpec((tb, 512), lambda i: (i, 0)),
                   pl.BlockSpec((tb // GLA_CHUNK, GLA_HEADS, LANES, LANES), lambda i: (i, 0, 0, 0))],
        out_shape=[jax.ShapeDtypeStruct((s, 512), BF), jax.ShapeDtypeStruct((s, 512), F32),
                   jax.ShapeDtypeStruct((s // GLA_CHUNK, GLA_HEADS, LANES, LANES), F32)],
        scratch_shapes=[pltpu.VMEM((GLA_HEADS, LANES, LANES), F32)],
        compiler_params=_params(("arbitrary",)),
    )(proj, proj, proj, proj, wdecp, bdec, ggla, _gla_triangle(sub))


def _gla_bwd(proj, dog, opre, sprev, wdecp, bdec, ggla):
    s = proj.shape[0]
    tb, sub, subs, units = _gla_units(s)
    nsub = len(subs)
    nch = sub // GLA_CHUNK
    nb = s // tb

    def body(qk_ref, v_ref, gz_ref, ga_ref, dog_ref, opre_ref, sprev_ref, wd_ref, bd_ref, gg_ref, tri_ref, triu_ref,
             dqk_ref, dv_ref, dgz_ref, dga_ref, dwd_ref, dbd_ref, dgg_ref, dst_ref):
        @pl.when(pl.program_id(0) == 0)
        def _():
            dst_ref[...] = jnp.zeros_like(dst_ref)
            dwd_ref[...] = jnp.zeros_like(dwd_ref)
            dbd_ref[...] = jnp.zeros_like(dbd_ref)
            dgg_ref[...] = jnp.zeros_like(dgg_ref)

        lo_h = lax.broadcasted_iota(jnp.int32, (sub, LANES), 1) < GLA_DK
        tril = tri_ref[...] > 0.5
        tril_b = tri_ref[...].astype(BF)
        triu_b = triu_ref[...].astype(BF)
        ones_b = jnp.ones((LANES, LANES), BF)
        last_row = (lax.broadcasted_iota(jnp.int32, (sub, LANES), 0) % GLA_CHUNK) == GLA_CHUNK - 1
        wd, gg, bd = wd_ref[...], gg_ref[...], bd_ref[...]
        chunks = _chunks(sub)
        lanes = [slice(h * LANES, (h + 1) * LANES) for h in range(GLA_HEADS)]
        blks = [slice(LANES * (h // 2), LANES * (h // 2 + 1)) for h in range(GLA_HEADS)]
        ga = [ga_ref[sl, :] for sl in subs]
        com = [_gla_block_common(qk_ref[sl, :], ga[i], wd, bd, tril_b) for i, sl in enumerate(subs)]
        decs = [[jnp.exp(bl) for bl in cm[7]] for cm in com]
        a = {(i, h): _head(com[i][4], h, lo_h).astype(BF) for i, h in units}
        bm = {(i, h): _head(com[i][5], h, lo_h).astype(BF) for i, h in units}
        ktl = {(i, h): _head(com[i][6], h, lo_h).astype(BF) for i, h in units}
        vh = {(i, h): v_ref[subs[i], lanes[h]].astype(BF) for i, h in units}
        sc = {u: _dot(a[u], bm[u], NT) for u in units}

        o = {(i, h): opre_ref[subs[i], lanes[h]] for i, h in units}
        ms = {u: _lane_mean(o[u] * o[u], ones_b) for u in units}
        gz = {(i, h): gz_ref[subs[i], lanes[h]] for i, h in units}
        dog = {(i, h): dog_ref[subs[i], lanes[h]] for i, h in units}
        sg = {u: _sigmoid(gz[u]) for u in units}
        r = {u: lax.rsqrt(ms[u] + RMS_EPS) for u in units}
        ohat = {u: o[u] * r[u] for u in units}
        sil = {u: gz[u] * sg[u] for u in units}
        for i, h in units:
            u = (i, h)
            dgz_ref[subs[i], lanes[h]] = (dog[u] * (ohat[u] * gg[:, lanes[h]])
                                          * (sg[u] * (1.0 + gz[u] * (1.0 - sg[u])))).astype(dgz_ref.dtype)
            dgg_ref[:, lanes[h]] += jnp.sum(dog[u] * sil[u] * ohat[u], axis=0, keepdims=True)
        dn = {(i, h): dog[(i, h)] * sil[(i, h)] * gg[:, lanes[h]] for i, h in units}
        mdn = {u: _lane_mean(dn[u] * ohat[u], ones_b) for u in units}
        do = {u: (r[u] * (dn[u] - ohat[u] * mdn[u])).astype(BF) for u in units}

        p = {u: jnp.where(tril, sc[u], 0.0).astype(BF) for u in units}
        dpr = {u: _dot(do[u], vh[u], NT) for u in units}
        incr = {u: [_dot(do[u][rs], a[u][rs], TN) for rs in chunks] for u in units}
        dv = {u: _dot(p[u], do[u], TN) for u in units}
        dp = {u: jnp.where(tril, dpr[u], 0.0).astype(BF) for u in units}
        dqd = {u: _dot(dp[u], bm[u]) for u in units}
        dkd = {u: _dot(dp[u], a[u], TN) for u in units}
        st = {(i, h): [sprev_ref[i * nch + c, h] for c in range(nch)] for i, h in units}
        leaving = {}
        for h in range(GLA_HEADS):
            d = dst_ref[h]
            for i in reversed(range(nsub)):
                out = [None] * nch
                for c in reversed(range(nch)):
                    out[c] = d
                    d = d * decs[i][c][:, blks[h]] + incr[(i, h)][c]
                leaving[(i, h)] = out
            dst_ref[h] = d
        lv_b = {u: [leaving[u][c].astype(BF) for c in range(nch)] for u in units}
        dv_s = {u: [_dot(ktl[u][rs], lv_b[u][c], NT) for c, rs in enumerate(chunks)] for u in units}
        dqd_s = {u: [_dot(do[u][rs], st[u][c].astype(BF)) for c, rs in enumerate(chunks)] for u in units}
        dkt_s = {u: [_dot(vh[u][rs], lv_b[u][c]) for c, rs in enumerate(chunks)] for u in units}
        ddec = {u: [jnp.sum(leaving[u][c] * st[u][c], axis=0, keepdims=True) for c in range(nch)] for u in units}
        for i, h in units:
            dv_ref[subs[i], lanes[h]] = (dv[(i, h)] + jnp.concatenate(dv_s[(i, h)], axis=0)).astype(dv_ref.dtype)
        dqd = {u: dqd[u] + jnp.concatenate(dqd_s[u], axis=0) for u in units}
        dkt = {u: jnp.concatenate(dkt_s[u], axis=0) for u in units}

        db = []
        for i, sl in enumerate(subs):
            _, eq, ek, f, qd, kd, kt, _ = com[i]
            parts = []
            for pair in range(GLA_HEADS // 2):
                blk, u0, u1 = blks[2 * pair], (i, 2 * pair), (i, 2 * pair + 1)
                dqd_b, dkd_b, dkt_b = dqd[u0] + dqd[u1], dkd[u0] + dkd[u1], dkt[u0] + dkt[u1]
                dqk_ref[sl, blk] = (dqd_b * (eq[:, blk] * GLA_DK ** -0.5)).astype(dqk_ref.dtype)
                dqk_ref[sl, 256 + LANES * pair:256 + LANES * (pair + 1)] = (dkd_b * ek[:, blk] + dkt_b * f[:, blk]).astype(dqk_ref.dtype)
                dkt_kt = dkt_b * kt[:, blk]
                dbp = dqd_b * qd[:, blk] - dkd_b * kd[:, blk] - dkt_kt
                dbl = [jnp.sum(dkt_kt[rs], axis=0, keepdims=True) + (ddec[u0][c] + ddec[u1][c]) * decs[i][c][:, blk]
                       for c, rs in enumerate(chunks)]
                parts.append(jnp.where(last_row, dbp + _per_chunk_rows(dbl, LANES), dbp))
            db.append(jnp.concatenate(parts, axis=1))
        dla = [_tri_matmul(triu_b, db[i]) for i in range(nsub)]
        dz32 = [dla[i] * (1.0 / GLA_TAU) * _sigmoid(-com[i][0]) for i in range(nsub)]
        dz = [t.astype(BF) for t in dz32]
        for i, sl in enumerate(subs):
            dga_ref[sl, :] = _dot(dz[i], wd, NT).astype(dga_ref.dtype)
            dwd_ref[...] += _dot(ga[i].astype(BF), dz[i], TN)
            dbd_ref[...] += jnp.sum(dz32[i], axis=0, keepdims=True)

    def col(width, off):
        return pl.BlockSpec((tb, width), lambda i: (nb - 1 - i, off // width))

    def rev(width):
        return pl.BlockSpec((tb, width), lambda i: (nb - 1 - i, 0))

    return pl.pallas_call(
        body, name="gla_bwd", grid=(nb,),
        in_specs=[col(512, OFF_QK), col(512, OFF_V), col(512, OFF_GZ), col(LANES, OFF_GA), rev(512), rev(512),
                  pl.BlockSpec((tb // GLA_CHUNK, GLA_HEADS, LANES, LANES), lambda i: (nb - 1 - i, 0, 0, 0)),
                  _full((LANES, 256)), _full((1, 256)), _full((1, 512)), _full((sub, sub)), _full((sub, sub))],
        out_specs=[rev(512), rev(512), rev(512), rev(LANES), _full((LANES, 256)), _full((1, 256)), _full((1, 512))],
        out_shape=[jax.ShapeDtypeStruct((s, 512), BF), jax.ShapeDtypeStruct((s, 512), BF),
                   jax.ShapeDtypeStruct((s, 512), BF), jax.ShapeDtypeStruct((s, LANES), BF),
                   jax.ShapeDtypeStruct((LANES, 256), F32), jax.ShapeDtypeStruct((1, 256), F32),
                   jax.ShapeDtypeStruct((1, 512), F32)],
        scratch_shapes=[pltpu.VMEM((GLA_HEADS, LANES, LANES), F32)],
        compiler_params=_params(("arbitrary",)),
    )(proj, proj, proj, proj, dog, opre, sprev, wdecp, bdec, ggla, _gla_triangle(sub), _gla_triangle(sub).T)


_SWA_COL_HEADS = (0, 2, 1, 3, 4, 6, 5, 7)
_SWA_COLS = SWA_HEADS * SWA_BLOCK


def _swa_masks():
    lo2 = lax.broadcasted_iota(jnp.int32, (2 * SWA_BLOCK, LANES), 1) < 64
    lane1 = lax.broadcasted_iota(jnp.int32, (SWA_BLOCK, LANES), 1)
    first_half = (lane1 % 64) < 32
    key = lax.broadcasted_iota(jnp.int32, (SWA_BLOCK, _SWA_COLS), 0)
    query = lax.broadcasted_iota(jnp.int32, (SWA_BLOCK, _SWA_COLS), 1) % SWA_BLOCK
    return lo2, lane1 < 64, first_half, key > query


def _merge_band(t, prev_mask, prev_bias=None):
    prev = t[:SWA_BLOCK] if prev_bias is None else t[:SWA_BLOCK] + prev_bias
    return jnp.where(prev_mask, prev, t[SWA_BLOCK:])


def _split_band(t, prev_mask_b):
    prev = t * prev_mask_b
    return jnp.concatenate([prev, t - prev], axis=0)


def _kv_variants(t, lo2):
    tr = pltpu.roll(t, 64, 1)
    lo_v = [jnp.where(lo2, t, 0.0).astype(BF), jnp.where(lo2, tr, 0.0).astype(BF)]
    hi_v = [jnp.where(lo2, 0.0, tr).astype(BF), jnp.where(lo2, 0.0, t).astype(BF)]
    return lo_v, hi_v


def _kv_variants_t(t):
    tt = t.T
    sw = jnp.concatenate([tt[64:], tt[:64]], axis=0)
    top = lax.broadcasted_iota(jnp.int32, tt.shape, 0) < 64
    lo_v = [jnp.where(top, tt, 0.0).astype(BF), jnp.where(top, sw, 0.0).astype(BF)]
    hi_v = [jnp.where(top, 0.0, sw).astype(BF), jnp.where(top, 0.0, tt).astype(BF)]
    return lo_v, hi_v


def _swa_scores(qg, k_lo, k_hi):
    return jnp.concatenate([_dot(k_lo[0], qg[0], NT), _dot(k_hi[0], qg[0], NT),
                            _dot(k_lo[1], qg[1], NT), _dot(k_hi[1], qg[1], NT)], axis=1)


def _sink_row(sinks_ref):
    return jnp.concatenate([jnp.full((1, SWA_BLOCK), sinks_ref[0, hd], F32) for hd in _SWA_COL_HEADS], axis=1)


def _swa_softmax(st, prev_mask, prev_bias, sink):
    st = _merge_band(st, prev_mask, prev_bias)
    m = jnp.maximum(jnp.max(st, axis=0, keepdims=True), sink)
    ex = jnp.exp(st - m)
    es = jnp.exp(sink - m)
    inv = 1.0 / (jnp.sum(ex, axis=0, keepdims=True) + es)
    return ex, es, inv


def _no_prev_bias(block_index):
    return jnp.where(block_index > 0, 0.0, -1e30).astype(F32)


def _swa_queries(sq_ref, rows, cosb, sinb, first_half):
    qs = [_rope(sq_ref[rows, p * LANES:(p + 1) * LANES], cosb, sinb, first_half) * 0.125 for p in range(4)]
    return [jnp.concatenate(qs[0:2], axis=0), jnp.concatenate(qs[2:4], axis=0)]


def _swa_fwd(proj, cos, sin, sinks):
    s = proj.shape[0]
    nq = min(SWA_QBLOCKS, s // SWA_BLOCK)
    tq = nq * SWA_BLOCK

    def body(sq_ref, sz_ref, sk_ref, sv_ref, cos_ref, sin_ref, sinks_ref, os_ref, opre_ref, kprev, vprev):
        n = pl.program_id(0)

        @pl.when(n == 0)
        def _():
            kprev[...] = jnp.zeros_like(kprev)
            vprev[...] = jnp.zeros_like(vprev)

        lo2, _, first_half, prev_mask = _swa_masks()
        prev_mask_b = jnp.where(prev_mask, 1.0, 0.0).astype(BF)
        sink = _sink_row(sinks_ref)
        blocks = range(nq)
        rows = [slice(j * SWA_BLOCK, (j + 1) * SWA_BLOCK) for j in blocks]
        cosb = [cos_ref[rows[j], :] for j in blocks]
        sinb = [sin_ref[rows[j], :] for j in blocks]
        kc = [_rope(sk_ref[rows[j], :], cosb[j], sinb[j], first_half) for j in blocks]
        vc = [sv_ref[rows[j], :] for j in blocks]
        kcat = [jnp.concatenate([kprev[...] if j == 0 else kc[j - 1], kc[j]], axis=0) for j in blocks]
        vcat = [jnp.concatenate([vprev[...] if j == 0 else vc[j - 1], vc[j]], axis=0) for j in blocks]
        kprev[...] = kc[-1]
        vprev[...] = vc[-1]
        kvar = [_kv_variants(kcat[j], lo2) for j in blocks]
        vtvar = [_kv_variants_t(vcat[j]) for j in blocks]
        qg = [[q.astype(BF) for q in _swa_queries(sq_ref, rows[j], cosb[j], sinb[j], first_half)] for j in blocks]
        st = [_swa_scores(qg[j], *kvar[j]) for j in blocks]
        soft = [_swa_softmax(st[j], prev_mask, _no_prev_bias(n) if j == 0 else None, sink) for j in blocks]
        pt = [_split_band(soft[j][0].astype(BF), prev_mask_b) for j in blocks]
        og = {}
        for j in blocks:
            inv = soft[j][2]
            for g in range(2):
                c0, c1, c2 = 512 * g, 512 * g + 256, 512 * g + 512
                ot = (_dot(vtvar[j][0][g], pt[j][:, c0:c1]) * inv[:, c0:c1]
                      + _dot(vtvar[j][1][g], pt[j][:, c1:c2]) * inv[:, c1:c2])
                og[(j, g)] = ot.T
        for j in blocks:
            for g in range(2):
                for i in range(2):
                    ls = slice((2 * g + i) * LANES, (2 * g + i + 1) * LANES)
                    o = og[(j, g)][i * SWA_BLOCK:(i + 1) * SWA_BLOCK]
                    sz = sz_ref[rows[j], ls]
                    opre_ref[rows[j], ls] = o
                    os_ref[rows[j], ls] = (o * (sz * _sigmoid(sz))).astype(os_ref.dtype)

    def col(width, off):
        return pl.BlockSpec((tq, width), lambda i: (i, off // width))

    row = pl.BlockSpec((tq, LANES), lambda i: (i, 0))
    return pl.pallas_call(
        body, name="swa_fwd", grid=(s // tq,),
        in_specs=[col(512, OFF_SQ), col(512, OFF_SZ), col(LANES, OFF_SK), col(LANES, OFF_SV), row, row,
                  pl.BlockSpec(memory_space=pltpu.SMEM)],
        out_specs=[pl.BlockSpec((tq, 512), lambda i: (i, 0))] * 2,
        out_shape=[jax.ShapeDtypeStruct((s, 512), BF), jax.ShapeDtypeStruct((s, 512), F32)],
        scratch_shapes=[pltpu.VMEM((SWA_BLOCK, LANES), F32)] * 2,
        compiler_params=_params(("arbitrary",)),
    )(proj, proj, proj, proj, cos, sin, sinks)


def _swa_bwd(proj, dos, opre, cos, sin, sinks):
    s = proj.shape[0]
    nq = min(SWA_QBLOCKS, s // SWA_BLOCK)
    tq = nq * SWA_BLOCK

    def body(sq_ref, sz_ref, sk_ref, sv_ref, dos_ref, opre_ref, cos_ref, sin_ref, sinks_ref,
             dsq_ref, dsz_ref, dsk_ref, dsv_ref, dsink_ref, kprev, vprev, cprev, sprev):
        n = pl.program_id(0)

        @pl.when(n == 0)
        def _():
            kprev[...] = jnp.zeros_like(kprev)
            vprev[...] = jnp.zeros_like(vprev)
            cprev[...] = jnp.zeros_like(cprev)
            sprev[...] = jnp.zeros_like(sprev)
            for hd in range(SWA_HEADS):
                dsink_ref[0, hd] = 0.0

        lo2, lo1, first_half, prev_mask = _swa_masks()
        prev_mask_b = jnp.where(prev_mask, 1.0, 0.0).astype(BF)
        lo1s = jnp.concatenate([lo1, lo1], axis=0)
        sink = _sink_row(sinks_ref)

        def home(m0, m1):
            t0 = m0 + pltpu.roll(m0, 64, 1)
            t1 = m1 + pltpu.roll(m1, 64, 1)
            return jnp.where(lo2, t0, t1)

        kp, vp, cp_, sp_ = kprev[...], vprev[...], cprev[...], sprev[...]
        for j in range(nq):
            rows = slice(j * SWA_BLOCK, (j + 1) * SWA_BLOCK)
            blk = n * nq + j
            cosb, sinb = cos_ref[rows, :], sin_ref[rows, :]
            kc = _rope(sk_ref[rows, :], cosb, sinb, first_half)
            vc = sv_ref[rows, :]
            kcat = jnp.concatenate([kp, kc], axis=0)
            k_lo, k_hi = _kv_variants(kcat, lo2)
            kt_lo, kt_hi = _kv_variants_t(kcat)
            v_lo, v_hi = _kv_variants(jnp.concatenate([vp, vc], axis=0), lo2)
            qg32 = _swa_queries(sq_ref, rows, cosb, sinb, first_half)
            qg = [q.astype(BF) for q in qg32]
            ex, es, inv = _swa_softmax(_swa_scores(qg, k_lo, k_hi), prev_mask, _no_prev_bias(n) if j == 0 else None, sink)
            pr, ps = ex * inv, es * inv

            dog32 = []
            for g in range(2):
                parts = []
                for i in range(2):
                    ls = slice((2 * g + i) * LANES, (2 * g + i + 1) * LANES)
                    sz = sz_ref[rows, ls]
                    sg = _sigmoid(sz)
                    dos_p = dos_ref[rows, ls]
                    dsz_ref[rows, ls] = (dos_p * opre_ref[rows, ls] * (sg * (1.0 + sz * (1.0 - sg)))).astype(dsz_ref.dtype)
                    parts.append(dos_p * (sz * sg))
                dog32.append(jnp.concatenate(parts, axis=0))
            dog = [t.astype(BF) for t in dog32]
            dpr = _merge_band(jnp.concatenate([_dot(v_lo[0], dog[0], NT), _dot(v_hi[0], dog[0], NT),
                                               _dot(v_lo[1], dog[1], NT), _dot(v_hi[1], dog[1], NT)], axis=1), prev_mask)
            rd = jnp.sum(pr * dpr, axis=0, keepdims=True)
            ds = _split_band((pr * (dpr - rd)).astype(BF), prev_mask_b)
            prb = _split_band(pr.astype(BF), prev_mask_b)
            sink_term = ps * rd
            for r, hd in enumerate(_SWA_COL_HEADS):
                dsink_ref[0, hd] += -jnp.sum(sink_term[:, r * SWA_BLOCK:(r + 1) * SWA_BLOCK])

            dk_g, dv_g = [], []
            for g in range(2):
                c0, c1, c2 = 512 * g, 512 * g + 256, 512 * g + 512
                dq = (_dot(kt_lo[g], ds[:, c0:c1]) + _dot(kt_hi[g], ds[:, c1:c2])).T
                for i in range(2):
                    ls = slice((2 * g + i) * LANES, (2 * g + i + 1) * LANES)
                    dsq_ref[rows, ls] = _rope_t(dq[i * SWA_BLOCK:(i + 1) * SWA_BLOCK] * 0.125, cosb, sinb,
                                                first_half).astype(dsq_ref.dtype)
                q_split = jnp.concatenate([jnp.where(lo1s, qg32[g], 0.0), jnp.where(lo1s, 0.0, qg32[g])], axis=0).astype(BF)
                do_split = jnp.concatenate([jnp.where(lo1s, dog32[g], 0.0), jnp.where(lo1s, 0.0, dog32[g])], axis=0).astype(BF)
                dk_g.append(_dot(ds[:, c0:c2], q_split))
                dv_g.append(_dot(prb[:, c0:c2], do_split))
            dk = home(dk_g[0], dk_g[1])
            dv = home(dv_g[0], dv_g[1])
            cur = pl.ds(pl.multiple_of(blk * SWA_BLOCK, SWA_BLOCK), SWA_BLOCK)
            dsk_ref[cur, :] = _rope_t(dk[SWA_BLOCK:], cosb, sinb, first_half)
            dsv_ref[cur, :] = dv[SWA_BLOCK:]
            dk_prev = _rope_t(dk[:SWA_BLOCK], cp_, sp_, first_half)
            dv_prev = dv[:SWA_BLOCK]
            if j == 0:
                @pl.when(n > 0)
                def _():
                    prv = pl.ds(pl.multiple_of((blk - 1) * SWA_BLOCK, SWA_BLOCK), SWA_BLOCK)
                    dsk_ref[prv, :] += dk_prev
                    dsv_ref[prv, :] += dv_prev
            else:
                prv = pl.ds(pl.multiple_of((blk - 1) * SWA_BLOCK, SWA_BLOCK), SWA_BLOCK)
                dsk_ref[prv, :] += dk_prev
                dsv_ref[prv, :] += dv_prev
            kp, vp, cp_, sp_ = kc, vc, cosb, sinb
        kprev[...] = kp
        vprev[...] = vp
        cprev[...] = cp_
        sprev[...] = sp_

    def col(width, off):
        return pl.BlockSpec((tq, width), lambda i: (i, off // width))

    row = pl.BlockSpec((tq, LANES), lambda i: (i, 0))
    wide = pl.BlockSpec((tq, 512), lambda i: (i, 0))
    return pl.pallas_call(
        body, name="swa_bwd", grid=(s // tq,),
        in_specs=[col(512, OFF_SQ), col(512, OFF_SZ), col(LANES, OFF_SK), col(LANES, OFF_SV), wide, wide, row, row,
                  pl.BlockSpec(memory_space=pltpu.SMEM)],
        out_specs=[wide, wide, _full((s, LANES)), _full((s, LANES)), pl.BlockSpec(memory_space=pltpu.SMEM)],
        out_shape=[jax.ShapeDtypeStruct((s, 512), BF), jax.ShapeDtypeStruct((s, 512), BF),
                   jax.ShapeDtypeStruct((s, LANES), F32), jax.ShapeDtypeStruct((s, LANES), F32),
                   jax.ShapeDtypeStruct((1, SWA_HEADS), F32)],
        scratch_shapes=[pltpu.VMEM((SWA_BLOCK, LANES), F32)] * 4,
        compiler_params=_params(("arbitrary",)),
    )(proj, proj, proj, proj, dos, opre, cos, sin, sinks)


def _outproj(og, osw, w_out, x2d, target, gate, g_final):
    s = x2d.shape[0]
    tm = min(512, s)

    def body(og_ref, os_ref, w_ref, x_ref, t_ref, gate_ref, gf_ref,
             dx2_ref, dog_ref, dos_ref, dw_ref, loss_ref, dgf_ref, dgate_ref):
        @pl.when(pl.program_id(0) == 0)
        def _():
            dw_ref[...] = jnp.zeros_like(dw_ref)
            loss_ref[...] = jnp.zeros_like(loss_ref)
            dgf_ref[...] = jnp.zeros_like(dgf_ref)
            dgate_ref[...] = jnp.zeros_like(dgate_ref)

        w = w_ref[...]
        gate, gf = gate_ref[...], gf_ref[...]
        subs = _subtiles(tm)
        ogv = [og_ref[sl, :] for sl in subs]
        osv = [os_ref[sl, :] for sl in subs]
        y = [_dot(ogv[k], w[:512]) + _dot(osv[k], w[512:]) for k in range(len(subs))]
        dys = []
        for k, sl in enumerate(subs):
            x2 = x_ref[sl, :] + gate * y[k]
            r = lax.rsqrt(jnp.mean(x2 * x2, axis=-1, keepdims=True) + RMS_EPS)
            xn = x2 * r
            err = xn * gf - t_ref[sl, :]
            loss_ref[...] += 0.5 * jnp.sum(jnp.mean(err * err, axis=-1, keepdims=True), axis=0, keepdims=True)
            dyf = err * (1.0 / D_MODEL)
            dgf_ref[...] += jnp.sum(dyf * xn, axis=0, keepdims=True)
            t = dyf * gf
            dx2 = r * (t - xn * jnp.mean(t * xn, axis=-1, keepdims=True))
            dx2_ref[sl, :] = dx2
            dgate_ref[...] += jnp.sum(dx2 * y[k], axis=0, keepdims=True)
            dys.append((dx2 * gate).astype(BF))
            dmix = _dot(dys[k], w, NT)
            dog_ref[sl, :] = dmix[:, :512]
            dos_ref[sl, :] = dmix[:, 512:]
        dy = jnp.concatenate(dys, axis=0)
        dw_ref[:512, :] += _dot(og_ref[...], dy, TN)
        dw_ref[512:, :] += _dot(os_ref[...], dy, TN)

    half = pl.BlockSpec((tm, 512), lambda i: (i, 0))
    rowb = pl.BlockSpec((tm, D_MODEL), lambda i: (i, 0))
    vec = _full((1, D_MODEL))
    return pl.pallas_call(
        body, name="outproj", grid=(s // tm,),
        in_specs=[half, half, _full((D_MODEL, D_MODEL)), rowb, rowb, vec, vec],
        out_specs=[rowb, half, half, _full((D_MODEL, D_MODEL)), _full((1, 1)), vec, vec],
        out_shape=[jax.ShapeDtypeStruct((s, D_MODEL), F32), jax.ShapeDtypeStruct((s, 512), F32),
                   jax.ShapeDtypeStruct((s, 512), F32), jax.ShapeDtypeStruct((D_MODEL, D_MODEL), F32),
                   jax.ShapeDtypeStruct((1, 1), F32), jax.ShapeDtypeStruct((1, D_MODEL), F32),
                   jax.ShapeDtypeStruct((1, D_MODEL), F32)],
        compiler_params=_params(("arbitrary",)),
    )(og, osw, w_out, x2d, target, gate, g_final)


_PIECES = ((OFF_QK, 512), (OFF_V, 512), (OFF_GZ, 512), (OFF_SQ, 512), (OFF_SZ, 512),
           (OFF_SK, LANES), (OFF_SV, LANES), (OFF_GA, LANES))

_UNPAD_ROWS = ((OFF_QK, 0, 1024),
               (OFF_GA, 1024, GLA_RANK),
               (OFF_GZ, 1040, 1024),
               (OFF_SK, 2064, 256),
               (OFF_SZ, 2320, 512))


def _inproj_bwd(x2d, shift, sc1p, g_norm, wpad_t, dx2, pieces):
    s = x2d.shape[0]
    tm = min(512, s)
    nsteps = s // tm

    def body(x_ref, sh_ref, sc_ref, g_ref, w_hbm, dx2_ref, *rest):
        piece_refs = rest[:len(_PIECES)]
        gx_ref, dw_hbm, dsh_ref, dsc_ref, dg_ref, w_vm, dw_vm, sem, out_sems = rest[len(_PIECES):]
        i = pl.program_id(0)

        @pl.when(i == 0)
        def _():
            cp = pltpu.make_async_copy(w_hbm, w_vm, sem)
            cp.start()
            dw_vm[...] = jnp.zeros_like(dw_vm)
            dsh_ref[...] = jnp.zeros_like(dsh_ref)
            dsc_ref[...] = jnp.zeros_like(dsc_ref)
            dg_ref[...] = jnp.zeros_like(dg_ref)
            cp.wait()

        g, sc1p_v, shift_v = g_ref[...], sc_ref[...], sh_ref[...]
        subs = _subtiles(tm)
        dhs = []
        for sl in subs:
            dh = None
            for (off, width), pr in zip(_PIECES, piece_refs):
                part = _dot(pr[sl, :].astype(BF), w_vm[off:off + width, :])
                dh = part if dh is None else dh + part
            dhs.append(dh)
        norm = [_modnorm(x_ref[sl, :], g, sc1p_v, shift_v) for sl in subs]
        hb = jnp.concatenate([h.astype(BF) for _, _, h in norm], axis=0)
        for (off, width), pr in zip(_PIECES, piece_refs):
            dw_vm[off:off + width, :] += _dot(pr[...].astype(BF), hb, TN)
        for sl, (xn, r, _), dh in zip(subs, norm, dhs):
            dsh_ref[...] += jnp.sum(dh, axis=0, keepdims=True)
            dsc_ref[...] += jnp.sum(dh * (xn * g), axis=0, keepdims=True)
            dg_ref[...] += jnp.sum(dh * xn * sc1p_v, axis=0, keepdims=True)
            dxn = dh * g * sc1p_v
            gx_ref[sl, :] = dx2_ref[sl, :] + r * (dxn - xn * jnp.mean(dxn * xn, axis=-1, keepdims=True))

        @pl.when(i == nsteps - 1)
        def _():
            copies = [pltpu.make_async_copy(dw_vm.at[src:src + n], dw_hbm.at[dst:dst + n], out_sems.at[k])
                      for k, (src, dst, n) in enumerate(_UNPAD_ROWS)]
            for cp in copies:
                cp.start()
            for cp in copies:
                cp.wait()

    rowb = pl.BlockSpec((tm, D_MODEL), lambda i: (i, 0))
    vec = _full((1, D_MODEL))
    anyspec = pl.BlockSpec(memory_space=pl.ANY)
    piece_specs = [pl.BlockSpec((tm, width), lambda i: (i, 0)) for _, width in _PIECES]
    return pl.pallas_call(
        body, name="inproj_bwd", grid=(nsteps,),
        in_specs=[rowb, vec, vec, vec, anyspec, rowb] + piece_specs,
        out_specs=[rowb, anyspec, vec, vec, vec],
        out_shape=[jax.ShapeDtypeStruct((s, D_MODEL), F32), jax.ShapeDtypeStruct((D_IN, D_MODEL), F32),
                   jax.ShapeDtypeStruct((1, D_MODEL), F32), jax.ShapeDtypeStruct((1, D_MODEL), F32),
                   jax.ShapeDtypeStruct((1, D_MODEL), F32)],
        scratch_shapes=[pltpu.VMEM((D_PAD, D_MODEL), BF), pltpu.VMEM((D_PAD, D_MODEL), F32), pltpu.SemaphoreType.DMA,
                        pltpu.SemaphoreType.DMA((len(_UNPAD_ROWS),))],
        compiler_params=_params(("arbitrary",)),
    )(x2d, shift, sc1p, g_norm, wpad_t, dx2, *pieces)


def _adam(w, g, m, v):
    m2 = ADAM_B1 * m + (1.0 - ADAM_B1) * g
    v2 = ADAM_B2 * v + (1.0 - ADAM_B2) * (g * g)
    m_hat = m2 / (1.0 - ADAM_B1 ** ADAM_STEP)
    v_hat = v2 / (1.0 - ADAM_B2 ** ADAM_STEP)
    delta = -ADAM_LR * (m_hat / (jnp.sqrt(v_hat) + ADAM_EPS) + ADAM_WD * w)
    return delta, m2, v2


def _adamw(w, g, m, v, name):
    rr, cc = w.shape
    tc = min(256, cc)

    def body(w_ref, g_ref, m_ref, v_ref, d_ref, m2_ref, v2_ref):
        d_ref[...], m2_ref[...], v2_ref[...] = _adam(w_ref[...], g_ref[...], m_ref[...], v_ref[...])

    blk = pl.BlockSpec((rr, tc), lambda i: (0, i))
    return pl.pallas_call(
        body, name=name, grid=(cc // tc,), in_specs=[blk] * 4, out_specs=[blk] * 3,
        out_shape=[jax.ShapeDtypeStruct((rr, cc), F32)] * 3,
        compiler_params=_params(("arbitrary",)),
    )(w, g, m, v)


def _adamw_t(w3, g, m3, v3, name):
    rr, _, cc = w3.shape
    tc = min(256, cc)

    def body(w_ref, g_ref, m_ref, v_ref, d_ref, m2_ref, v2_ref, g3_ref):
        g = g_ref[...]
        d_ref[:, 0, :], m2_ref[:, 0, :], v2_ref[:, 0, :] = _adam(w_ref[:, 0, :], g, m_ref[:, 0, :], v_ref[:, 0, :])
        g3_ref[:, 0, :] = g

    b3 = pl.BlockSpec((rr, 1, tc), lambda i: (0, 0, i))
    return pl.pallas_call(
        body, name=name, grid=(cc // tc,), in_specs=[b3, pl.BlockSpec((rr, tc), lambda i: (0, i)), b3, b3],
        out_specs=[b3] * 4, out_shape=[jax.ShapeDtypeStruct((rr, 1, cc), F32)] * 4,
        compiler_params=_params(("arbitrary",)),
    )(w3, g, m3, v3)


def _ada_update(c_all, dmod_cols, w, m, v):
    rr, cc = w.shape
    tr = min(256, rr)
    c_all = jnp.pad(c_all, ((0, 8), (0, 0)))
    dmod_cols = jnp.pad(dmod_cols, ((0, 8), (0, 0)))

    def body(c_ref, dm_ref, w_ref, m_ref, v_ref, g_ref, d_ref, m2_ref, v2_ref):
        cv = c_ref[...]
        sc = (cv * _sigmoid(cv)).astype(BF)
        g = _dot(sc, dm_ref[...].astype(BF), TN)
        g_ref[...] = g
        d_ref[...], m2_ref[...], v2_ref[...] = _adam(w_ref[...], g, m_ref[...], v_ref[...])

    blk = pl.BlockSpec((tr, cc), lambda i: (i, 0))
    return pl.pallas_call(
        body, name="ada_update", grid=(rr // tr,),
        in_specs=[pl.BlockSpec((16, tr), lambda i: (0, i)), _full((16, cc)), blk, blk, blk],
        out_specs=[blk] * 4, out_shape=[jax.ShapeDtypeStruct((rr, cc), F32)] * 4,
        compiler_params=_params(("arbitrary",)),
    )(c_all, dmod_cols, w, m, v)


def _small_update(parts, weights, moms, vels):
    n = len(weights)

    def body(*refs):
        p_refs, w_refs, m_refs, v_refs = refs[:n + 1], refs[n + 1:2 * n + 1], refs[2 * n + 1:3 * n + 1], refs[3 * n + 1:4 * n + 1]
        outs = refs[4 * n + 1:]
        for i in range(n):
            g = p_refs[i][0]
            for d in range(1, 8):
                g = g + p_refs[i][d]
            delta, m2, v2 = _adam(w_refs[i][...], g, m_refs[i][...], v_refs[i][...])
            outs[4 * i][...] = g
            outs[4 * i + 1][...] = delta
            outs[4 * i + 2][...] = m2
            outs[4 * i + 3][...] = v2
        tot = p_refs[n][0]
        for d in range(1, 8):
            tot = tot + p_refs[n][d]
        outs[4 * n][...] = tot

    out_shape = []
    for w in weights:
        out_shape += [jax.ShapeDtypeStruct(w.shape, F32)] * 4
    out_shape.append(jax.ShapeDtypeStruct(parts[n].shape[1:], F32))
    return pl.pallas_call(body, name="small_update", out_shape=out_shape, compiler_params=_params())(
        *parts, *weights, *moms, *vels)


def _pad_w_in_t(w):
    pad = jnp.zeros((LANES - GLA_RANK, w.shape[1]), w.dtype)
    return jnp.concatenate([w[dst:dst + n] for _, dst, n in sorted(_UNPAD_ROWS)] + [pad], axis=0)


def _rows8(a):
    flat = a.reshape(-1)
    rows = -(-flat.shape[0] // LANES)
    rows8 = -(-rows // 8) * 8
    flat = jnp.pad(flat, (0, rows8 * LANES - flat.shape[0]))
    return flat.reshape(rows8, LANES)


def kernel(x, c, positions, w_ada, b_ada, g_norm, w_in, w_decay, b_decay, g_gla_head, sinks, w_out, g_final, loss_target, m_w_ada, m_b_ada, m_g_norm, m_w_in, m_w_decay, m_b_decay, m_g_gla_head, m_sinks, m_w_out, m_g_final, v_w_ada, v_b_ada, v_g_norm, v_w_in, v_w_decay, v_b_decay, v_g_gla_head, v_sinks, v_w_out, v_g_final):
    ax, ay, ac = lax.axis_index("x"), lax.axis_index("y"), lax.axis_index("c")
    chip = 2 * ax + ay
    dev = 2 * chip + ac
    s = x.shape[1]
    x2d = x[0]
    target = loss_target[0]
    w_ada2, w_out2, w_dec2 = w_ada[0], w_out[0], w_decay[0]
    w_in_t = w_in[0].T
    ada_cols = w_ada2.shape[1]
    in_cols = w_in_t.shape[0]
    out_rows = w_out2.shape[0]
    half = D_MODEL // 2

    cw = jnp.concatenate([c.reshape(8, LANES), w_dec2.reshape(8, LANES)], axis=0)
    b_shard = lax.dynamic_slice(b_ada, (0, chip * ada_cols), (1, ada_cols))
    half_in = lax.dynamic_slice(w_in_t, (0, ac * half), (in_cols, half)).astype(BF)
    half_out = lax.dynamic_slice(w_out2, (ac * (out_rows // 2), 0), (out_rows // 2, D_MODEL)).astype(BF)
    inv_freq = 1.0 / (ROPE_THETA ** (jnp.arange(0, 64, 2, dtype=F32) / 64))
    first, mod_all, w_in_all, w_out_all, cos, sin = _prologue(
        cw, w_ada2, b_shard, half_in, half_out, positions.reshape(s, 1), jnp.tile(inv_freq, 4).reshape(1, LANES))

    first = first.reshape(8, 2, 8, LANES)
    c_all = first[:, 0].reshape(8, D_MODEL)
    w_dec_full = first[0::2, 1].reshape(4, GLA_RANK, 64).transpose(1, 0, 2).reshape(GLA_RANK, 256)
    mod = mod_all.reshape(4, 2, 8, ada_cols)[:, 0]
    mod = lax.dynamic_slice(mod, (0, dev, 0), (4, 1, ada_cols)).reshape(1, 4 * ada_cols)
    shift, sc1p, gate = mod[:, :D_MODEL], 1.0 + mod[:, D_MODEL:2 * D_MODEL], mod[:, 2 * D_MODEL:]
    w_in_all = w_in_all.reshape(4, 2, in_cols, half)
    wpad_t = _pad_w_in_t(w_in_all.transpose(0, 2, 1, 3).reshape(4 * in_cols, D_MODEL))
    w_out_all = w_out_all.reshape(D_MODEL, D_MODEL)

    wdecp = jnp.pad(w_dec_full, ((0, LANES - GLA_RANK), (0, 0))).astype(BF)

    proj = _inproj_fwd(x2d, shift, sc1p, g_norm, wpad_t)
    og, o_gla, sprev = _gla_fwd(proj, wdecp, b_decay, g_gla_head)
    osw, o_swa = _swa_fwd(proj, cos, sin, sinks)
    dx2, dog, dos, dw_out, loss_p, dgf, dgate = _outproj(og, osw, w_out_all, x2d, target, gate, g_final.reshape(1, D_MODEL))
    dsq, dsz, dsk, dsv, dsinks = _swa_bwd(proj, dos, o_swa, cos, sin, sinks)
    dqk, dv, dgz, dga, dwdp, dbd, dgg = _gla_bwd(proj, dog, o_gla, sprev, wdecp, b_decay, g_gla_head)
    pieces = (dqk, dv, dgz, dsq, dsz, dsk, dsv, dga)
    gx, dw_in_t, dshift, dscale, dgn = _inproj_bwd(x2d, shift, sc1p, g_norm, wpad_t, dx2, pieces)

    segs = [jnp.concatenate([dshift, dscale, dgate], axis=1), dgn, dgf, dwdp[:GLA_RANK], dbd, dgg, dsinks, loss_p]
    packed = [_rows8(a) for a in segs]
    offs = [0]
    for a in packed:
        offs.append(offs[-1] + a.shape[0])
    g_w_in_t, g_w_out, small = _epilogue(dw_in_t.reshape(4, in_cols, D_MODEL), dw_out.reshape(4, out_rows, D_MODEL),
                                         jnp.concatenate(packed, axis=0))

    def seg(i, size):
        return small[:, offs[i]:offs[i + 1]].reshape(8, -1)[:, :size]

    dmod_all = seg(0, 3 * D_MODEL)
    dwd_all = lax.dynamic_slice(seg(3, GLA_RANK * 256).reshape(8, GLA_RANK, 256), (0, 0, chip * 64), (8, GLA_RANK, 64))
    parts = [dmod_all.reshape(8, 1, 3 * D_MODEL), seg(1, D_MODEL).reshape(8, 1, D_MODEL), dwd_all,
             seg(4, 256).reshape(8, 1, 256), seg(5, 512).reshape(8, 1, 512), seg(6, SWA_HEADS).reshape(8, 1, SWA_HEADS),
             seg(2, D_MODEL).reshape(8, 1, D_MODEL), seg(7, LANES).reshape(8, 1, LANES)]
    smalls = _small_update(
        parts,
        [b_ada, g_norm, w_dec2, b_decay, g_gla_head, sinks, g_final.reshape(1, D_MODEL)],
        [m_b_ada, m_g_norm, m_w_decay[0], m_b_decay, m_g_gla_head, m_sinks, m_g_final.reshape(1, D_MODEL)],
        [v_b_ada, v_g_norm, v_w_decay[0], v_b_decay, v_g_gla_head, v_sinks, v_g_final.reshape(1, D_MODEL)])
    (g_b_ada, d_b_ada, nm_b_ada, nv_b_ada, g_gn, d_gn, nm_gn, nv_gn, g_wd, d_wd, nm_wd, nv_wd,
     g_bd, d_bd, nm_bd, nv_bd, g_gg, d_gg, nm_gg, nv_gg, g_sk, d_sk, nm_sk, nv_sk,
     g_gf, d_gf, nm_gf, nv_gf, loss_row) = smalls
    loss = loss_row[0, 0]

    dmod_cols = lax.dynamic_slice(dmod_all, (0, chip * ada_cols), (8, ada_cols))
    g_w_ada, d_w_ada, nm_w_ada, nv_w_ada = _ada_update(c_all, dmod_cols, w_ada2, m_w_ada[0], v_w_ada[0])
    to3 = lambda a: jnp.transpose(a, (2, 0, 1))
    from3 = lambda a: jnp.transpose(a, (1, 2, 0))[0]
    d3, nm3, nv3, g3 = _adamw_t(to3(w_in), g_w_in_t, to3(m_w_in), to3(v_w_in), "adamw_w_in")
    g_w_in, d_w_in, nm_w_in, nv_w_in = from3(g3), from3(d3), from3(nm3), from3(nv3)
    d_w_out, nm_w_out, nv_w_out = _adamw(w_out2, g_w_out, m_w_out[0], v_w_out[0], "adamw_w_out")

    flat = lambda a: a.reshape(D_MODEL)
    grads = [g_w_ada[None], g_b_ada, g_gn, g_w_in[None], g_wd[None], g_bd, g_gg, g_sk, g_w_out[None], flat(g_gf)]
    deltas = [d_w_ada[None], d_b_ada, d_gn, d_w_in[None], d_wd[None], d_bd, d_gg, d_sk, d_w_out[None], flat(d_gf)]
    new_m = [nm_w_ada[None], nm_b_ada, nm_gn, nm_w_in[None], nm_wd[None], nm_bd, nm_gg, nm_sk, nm_w_out[None], flat(nm_gf)]
    new_v = [nv_w_ada[None], nv_b_ada, nv_gn, nv_w_in[None], nv_wd[None], nv_bd, nv_gg, nv_sk, nv_w_out[None], flat(nv_gf)]
    return (loss, gx[None], *grads, *deltas, *new_m, *new_v)
```

```python
import jax
import jax.numpy as jnp
from jax import lax
from jax.experimental import pallas as pl
from jax.experimental.pallas import tpu as pltpu

F32 = jnp.float32
BF = jnp.bfloat16

D_MODEL = 1024
GLA_HEADS = 4
GLA_DK = 64
GLA_CHUNK = 64
GLA_RANK = 16
GLA_TAU = 16.0
GLA_SUB = 256
GLA_ROWS = 512
SWA_HEADS = 8
SWA_BLOCK = 128
SWA_QBLOCKS = 8
RMS_EPS = 1e-6
ROPE_THETA = 10000.0

OFF_QK, OFF_V, OFF_GZ, OFF_SQ, OFF_SZ, OFF_SK, OFF_SV, OFF_GA = 0, 512, 1024, 1536, 2048, 2560, 2688, 2816
D_PAD = 2944
D_IN = 2832
LANES = 128
VMEM_LIMIT = 56 * 1024 * 1024

ADAM_LR, ADAM_B1, ADAM_B2, ADAM_EPS, ADAM_WD, ADAM_STEP = 0.001, 0.9, 0.999, 1e-08, 0.01, 10

NT = (((1,), (1,)), ((), ()))
TN = (((0,), (0,)), ((), ()))
MESH = pl.DeviceIdType.MESH


def _dot(a, b, dims=None):
    if dims is None:
        return jnp.dot(a, b, preferred_element_type=F32)
    return lax.dot_general(a, b, dims, preferred_element_type=F32)


def _sigmoid(x):
    return 1.0 / (1.0 + jnp.exp(-x))


def _params(sem=None):
    return pltpu.CompilerParams(dimension_semantics=sem, vmem_limit_bytes=VMEM_LIMIT)


def _full(shape):
    return pl.BlockSpec(shape, lambda i: (0,) * len(shape))


def _subtiles(rows, size=256):
    size = min(size, rows)
    return [slice(k * size, (k + 1) * size) for k in range(rows // size)]


_GATHER_SEMS = [pltpu.SemaphoreType.DMA((7,)), pltpu.SemaphoreType.DMA((7,)), pltpu.SemaphoreType.DMA]


class _Gather:
    def __init__(self, x_ref, out_ref, send_sems, recv_sems, local_sem):
        x, y, c = lax.axis_index("x"), lax.axis_index("y"), lax.axis_index("c")
        self.me, self.sibling, self.c = (x, y, c), (x, y, 1 - c), c
        self.xn, self.yn, self.dg = (1 - x, y), (x, 1 - y), (1 - x, 1 - y)
        self.pass_from = (lax.rem(x + 1 - c, 2), lax.rem(y + c, 2))
        self.pass_to = (lax.rem(x + c, 2), lax.rem(y + 1 - c, 2))
        self.x_ref, self.out_ref, self.send_sems, self.recv_sems = x_ref, out_ref, send_sems, recv_sems
        self.mine = pltpu.make_async_copy(x_ref, self._slab(*self.me), local_sem)

    def _slab(self, px, py, pc):
        return self.out_ref.at[4 * px + 2 * py + pc]

    def _copy(self, k, blk, to, src=None):
        return pltpu.make_async_remote_copy(
            src_ref=self._slab(*blk) if src is None else src, dst_ref=self._slab(*blk),
            send_sem=self.send_sems.at[k], recv_sem=self.recv_sems.at[k], device_id=to, device_id_type=MESH)

    def _sends(self):
        c = self.c
        return [self._copy(0, self.me, self.sibling, src=self.x_ref),
                self._copy(1, self.me, (*self.xn, c), src=self.x_ref),
                self._copy(2, self.me, (*self.yn, c), src=self.x_ref),
                self._copy(3, (*self.pass_from, c), (*self.pass_to, c)),
                self._copy(4, (*self.xn, c), self.sibling),
                self._copy(5, (*self.yn, c), self.sibling),
                self._copy(6, (*self.dg, c), self.sibling)]

    def start(self):
        self.mine.start()
        for cp in self._sends()[0:3]:
            cp.start()

    def pass_on(self):
        sends = self._sends()
        self._copy(1, (*self.xn, self.c), self.me).wait_recv()
        self._copy(2, (*self.yn, self.c), self.me).wait_recv()
        for k in (3, 4, 5):
            sends[k].start()

    def relay_diagonal(self):
        self._copy(3, (*self.dg, self.c), self.me).wait_recv()
        self._sends()[6].start()

    def relay(self):
        self.pass_on()
        self.relay_diagonal()

    def finish(self):
        c = self.c
        self._copy(0, self.sibling, self.me).wait_recv()
        for k, chip in ((4, self.xn), (5, self.yn), (6, self.dg)):
            self._copy(k, (*chip, 1 - c), self.me).wait_recv()
        for cp in self._sends():
            cp.wait_send()
        self.mine.wait()


def _prologue(cw, w_ada, b_shard, half_in, half_out, pos_col, inv_freq):
    s = pos_col.shape[0]
    rt = min(512, s)

    def body(cw_ref, wada_hbm, b_ref, hin_ref, hout_ref, pos_hbm, f_ref,
             first_ref, mod_ref, win_ref, wout_ref, cos_hbm, sin_hbm,
             mod_blk, cos_ref, sin_ref, wada_ref, pos_ref, table_sems, local_sems, *sems):
        fetch_w = pltpu.make_async_copy(wada_hbm, wada_ref, local_sems.at[0])
        fetch_p = pltpu.make_async_copy(pos_hbm, pos_ref, local_sems.at[1])
        fetch_w.start()
        fetch_p.start()
        g_c = _Gather(cw_ref, first_ref, *sems[0:3])
        g_in = _Gather(hin_ref, win_ref, *sems[3:6])
        g_out = _Gather(hout_ref, wout_ref, *sems[6:9])
        g_mod = _Gather(mod_blk, mod_ref, *sems[9:12])
        g_c.start()
        g_in.start()
        g_out.start()
        g_c.relay()
        g_c.finish()
        c_rows = [jnp.concatenate([first_ref[d, r:r + 1, :] for r in range(8)], axis=1) for d in range(8)]
        c_all = jnp.concatenate(c_rows, axis=0)
        sc = (c_all * _sigmoid(c_all)).astype(BF)
        fetch_w.wait()
        mod_blk[...] = _dot(sc, wada_ref[...].astype(BF)) + b_ref[...]
        g_mod.start()
        fetch_p.wait()

        def rope_rows(i, carry):
            rows = pl.ds(pl.multiple_of(i * rt, rt), rt)
            ang = pos_ref[rows, :].astype(F32) * f_ref[...]
            lane = lax.broadcasted_iota(jnp.int32, ang.shape, 1)
            cos_ref[rows, :] = jnp.cos(ang)
            sn = jnp.sin(ang)
            sin_ref[rows, :] = jnp.where((lane % 64) < 32, -sn, sn)
            pltpu.make_async_copy(cos_ref.at[rows, :], cos_hbm.at[rows, :], table_sems.at[0]).start()
            pltpu.make_async_copy(sin_ref.at[rows, :], sin_hbm.at[rows, :], table_sems.at[1]).start()
            return carry

        steps = s // rt
        lax.fori_loop(0, steps // 2, rope_rows, 0)
        g_in.pass_on()
        g_out.pass_on()
        lax.fori_loop(steps // 2, steps, rope_rows, 0)
        g_in.relay_diagonal()
        g_out.relay_diagonal()
        g_mod.relay()
        g_in.finish()
        g_out.finish()
        g_mod.finish()
        pltpu.make_async_copy(cos_ref, cos_hbm, table_sems.at[0]).wait()
        pltpu.make_async_copy(sin_ref, sin_hbm, table_sems.at[1]).wait()

    vm = pl.BlockSpec(memory_space=pltpu.VMEM)
    hbm = pl.BlockSpec(memory_space=pl.ANY)
    return pl.pallas_call(
        body, name="prologue",
        out_shape=[jax.ShapeDtypeStruct((8,) + cw.shape, F32), jax.ShapeDtypeStruct((8, 8, w_ada.shape[1]), F32),
                   jax.ShapeDtypeStruct((8,) + half_in.shape, half_in.dtype),
                   jax.ShapeDtypeStruct((8,) + half_out.shape, half_out.dtype),
                   jax.ShapeDtypeStruct((s, LANES), F32), jax.ShapeDtypeStruct((s, LANES), F32)],
        in_specs=[vm, hbm, vm, hbm, hbm, hbm, vm], out_specs=[vm, vm, hbm, hbm, hbm, hbm],
        scratch_shapes=[pltpu.VMEM((8, w_ada.shape[1]), F32), pltpu.VMEM((s, LANES), F32), pltpu.VMEM((s, LANES), F32),
                        pltpu.VMEM(w_ada.shape, F32), pltpu.VMEM(pos_col.shape, jnp.int32),
                        pltpu.SemaphoreType.DMA((2,)), pltpu.SemaphoreType.DMA((2,))] + _GATHER_SEMS * 4,
        compiler_params=pltpu.CompilerParams(vmem_limit_bytes=VMEM_LIMIT),
    )(cw, w_ada, b_shard, half_in, half_out, pos_col, inv_freq)


def _reduce_scratch(rr, cc):
    c2 = cc // 2
    return [pltpu.VMEM((4, rr, c2), F32), pltpu.VMEM((4, rr, c2), F32), pltpu.VMEM((3, rr, c2), BF),
            pltpu.VMEM((2, rr, c2), BF), pltpu.VMEM((rr, c2), BF), pltpu.VMEM((rr, c2), F32),
            pltpu.SemaphoreType.DMA((8,)), pltpu.SemaphoreType.DMA((8,)), pltpu.SemaphoreType.DMA((5,))]


class _Reduce:
    def __init__(self, p_hbm, out_ref, acc_ref, own_ref, send_ref, land_ref, relay_ref, res_ref,
                 send_sems, recv_sems, local_sems):
        x, y, c = lax.axis_index("x"), lax.axis_index("y"), lax.axis_index("c")
        c2 = out_ref.shape[1] // 2
        sibling = (x, y, 1 - c)
        first = (lax.rem(x + 1 - c, 2), lax.rem(y + c, 2))
        second = (lax.rem(x + c, 2), lax.rem(y + 1 - c, 2))
        shards = [2 * first[0] + first[1], 2 * second[0] + second[1], 2 * (1 - x) + (1 - y), 2 * x + y]
        sibling_slot = (1, 0, 2, 3)
        mine = pl.ds(pl.multiple_of(c * c2, c2), c2)
        other = pl.ds(pl.multiple_of((1 - c) * c2, c2), c2)
        self.acc_ref, self.own_ref, self.send_ref, self.land_ref = acc_ref, own_ref, send_ref, land_ref
        self.relay_ref, self.res_ref = relay_ref, res_ref
        self.own = [pltpu.make_async_copy(p_hbm.at[j, :, mine], own_ref.at[k], local_sems.at[k])
                    for k, j in enumerate(shards)]
        self.swap_out = [pltpu.make_async_remote_copy(
            src_ref=p_hbm.at[j, :, other], dst_ref=acc_ref.at[sibling_slot[k]], send_sem=send_sems.at[k],
            recv_sem=recv_sems.at[sibling_slot[k]], device_id=sibling, device_id_type=MESH) for k, j in enumerate(shards)]
        self.swap_in = [pltpu.make_async_remote_copy(
            src_ref=p_hbm.at[j, :, other], dst_ref=acc_ref.at[k], send_sem=send_sems.at[k], recv_sem=recv_sems.at[k],
            device_id=sibling, device_id_type=MESH) for k, j in enumerate(shards)]

        def message(k, src, dst, to):
            return pltpu.make_async_remote_copy(src_ref=src, dst_ref=dst, send_sem=send_sems.at[k], recv_sem=recv_sems.at[k],
                                                device_id=(*to, c), device_id_type=MESH)

        self.direct = message(4, send_ref.at[0], land_ref.at[0], first)
        self.passed = message(5, send_ref.at[1], relay_ref, first)
        self.joint = message(6, send_ref.at[2], land_ref.at[1], second)
        self.put = pltpu.make_async_copy(res_ref, out_ref.at[:, mine], local_sems.at[4])
        self.share = pltpu.make_async_remote_copy(
            src_ref=res_ref, dst_ref=out_ref.at[:, mine], send_sem=send_sems.at[7],
            recv_sem=recv_sems.at[7], device_id=sibling, device_id_type=MESH)

    def start(self):
        for k in (0, 2, 1, 3):
            self.own[k].start()
            self.swap_out[k].start()

    def _combine(self, k):
        self.own[k].wait()
        self.swap_out[k].wait_send()
        self.swap_in[k].wait_recv()
        self.acc_ref[k] = self.acc_ref[k] + self.own_ref[k]

    def combine_and_send(self):
        dt = self.send_ref.dtype
        self._combine(0)
        self.send_ref[0] = self.acc_ref[0].astype(dt)
        self.direct.start()
        self._combine(2)
        self.send_ref[1] = self.acc_ref[2].astype(dt)
        self.passed.start()
        self._combine(1)
        self.passed.wait_recv()
        self.send_ref[2] = (self.acc_ref[1] + self.relay_ref[...].astype(F32)).astype(dt)
        self.joint.start()
        self._combine(3)

    def total_and_share(self):
        self.direct.wait_recv()
        self.joint.wait_recv()
        self.res_ref[...] = self.acc_ref[3] + self.land_ref[0].astype(F32) + self.land_ref[1].astype(F32)
        for cp in (self.direct, self.passed, self.joint):
            cp.wait_send()
        self.put.start()
        self.share.start()

    def finish(self):
        self.put.wait()
        self.share.wait()


def _epilogue(dw_in_parts, dw_out_parts, small):
    _, r_in, cc = dw_in_parts.shape
    _, r_out, _ = dw_out_parts.shape
    n_red = len(_reduce_scratch(r_in, cc))

    def body(pin_hbm, pout_hbm, small_ref, gin_ref, gout_ref, small_all_ref, *scratch):
        red_in = _Reduce(pin_hbm, gin_ref, *scratch[0:n_red])
        red_out = _Reduce(pout_hbm, gout_ref, *scratch[n_red:2 * n_red])
        gat = _Gather(small_ref, small_all_ref, *scratch[2 * n_red:])
        red_out.start()
        red_in.start()
        gat.start()
        red_out.combine_and_send()
        red_in.combine_and_send()
        gat.relay()
        red_out.total_and_share()
        red_in.total_and_share()
        gat.finish()
        red_out.finish()
        red_in.finish()

    vm = pl.BlockSpec(memory_space=pltpu.VMEM)
    anyspec = pl.BlockSpec(memory_space=pl.ANY)
    return pl.pallas_call(
        body, name="epilogue",
        out_shape=[jax.ShapeDtypeStruct((r_in, cc), F32), jax.ShapeDtypeStruct((r_out, cc), F32),
                   jax.ShapeDtypeStruct((8,) + small.shape, F32)],
        in_specs=[anyspec, anyspec, vm], out_specs=[anyspec, anyspec, vm],
        scratch_shapes=_reduce_scratch(r_in, cc) + _reduce_scratch(r_out, cc) + _GATHER_SEMS,
        compiler_params=pltpu.CompilerParams(vmem_limit_bytes=VMEM_LIMIT),
    )(dw_in_parts, dw_out_parts, small)


def _rope(t, cosb, sinb, first_half):
    partner = jnp.where(first_half, pltpu.roll(t, 96, 1), pltpu.roll(t, 32, 1))
    return t * cosb + partner * sinb


def _rope_t(g, cosb, sinb, first_half):
    gs = g * sinb
    partner = jnp.where(first_half, pltpu.roll(gs, 96, 1), pltpu.roll(gs, 32, 1))
    return g * cosb + partner


def _modnorm(x, g, sc1p, shift):
    r = lax.rsqrt(jnp.mean(x * x, axis=-1, keepdims=True) + RMS_EPS)
    xn = x * r
    return xn, r, (xn * g) * sc1p + shift


def _inproj_fwd(x2d, shift, sc1p, g_norm, wpad_t):
    s = x2d.shape[0]
    tm = min(512, s)

    def body(x_ref, sh_ref, sc_ref, g_ref, w_ref, o_ref):
        subs = _subtiles(tm)
        hs = [_modnorm(x_ref[sl, :], g_ref[...], sc_ref[...], sh_ref[...])[2].astype(BF) for sl in subs]
        for sl, h in zip(subs, hs):
            o_ref[sl, :] = _dot(h, w_ref[...], NT)

    vec = _full((1, D_MODEL))
    return pl.pallas_call(
        body, name="inproj_fwd", grid=(s // tm,),
        in_specs=[pl.BlockSpec((tm, D_MODEL), lambda i: (i, 0)), vec, vec, vec, _full((D_PAD, D_MODEL))],
        out_specs=pl.BlockSpec((tm, D_PAD), lambda i: (i, 0)),
        out_shape=jax.ShapeDtypeStruct((s, D_PAD), F32),
        compiler_params=_params(("arbitrary",)),
    )(x2d, shift, sc1p, g_norm, wpad_t)


def _split3(a):
    hi = a.astype(BF)
    r1 = a - hi.astype(F32)
    mid = r1.astype(BF)
    lo = (r1 - mid.astype(F32)).astype(BF)
    return hi, mid, lo


def _tri_matmul(tri, a):
    hi, mid, lo = _split3(a)
    return _dot(tri, hi) + _dot(tri, mid) + _dot(tri, lo)


def _chunks(tb):
    return [slice(c * GLA_CHUNK, (c + 1) * GLA_CHUNK) for c in range(tb // GLA_CHUNK)]


def _per_chunk_rows(rows, width):
    return jnp.concatenate([jnp.broadcast_to(r, (GLA_CHUNK, width)) for r in rows], axis=0)


def _gla_triangle(tb):
    row = lax.broadcasted_iota(jnp.int32, (tb, tb), 0)
    col = lax.broadcasted_iota(jnp.int32, (tb, tb), 1)
    return (((row // GLA_CHUNK) == (col // GLA_CHUNK)) & (col <= row)).astype(F32)


def _lane_mean(x, ones_b):
    hi = x.astype(BF)
    lo = (x - hi.astype(F32)).astype(BF)
    return (_dot(hi, ones_b) + _dot(lo, ones_b)) * (1.0 / LANES)


def _head(t, h, lo_h):
    blk = t[:, LANES * (h // 2):LANES * (h // 2 + 1)]
    return jnp.where(lo_h, blk, 0.0) if h % 2 == 0 else jnp.where(lo_h, 0.0, blk)


def _gla_block_common(qk, ga, wd, bd, tril_b):
    tb = qk.shape[0]
    q, k = qk[:, :256], qk[:, 256:]
    z = _dot(ga.astype(BF), wd) + bd
    la = (jnp.minimum(z, 0.0) - jnp.log(1.0 + jnp.exp(-jnp.abs(z)))) * (1.0 / GLA_TAU)
    b = _tri_matmul(tril_b, la)
    bls = [b[rs.stop - 1:rs.stop, :] for rs in _chunks(tb)]
    eq = jnp.exp(b)
    ek = jnp.exp(-b)
    f = jnp.exp(_per_chunk_rows(bls, 256) - b)
    return z, eq, ek, f, q * (eq * GLA_DK ** -0.5), k * ek, k * f, bls


def _gla_units(s):
    sub = min(GLA_SUB, s)
    tb = min(GLA_ROWS, s)
    subs = [slice(i * sub, (i + 1) * sub) for i in range(tb // sub)]
    units = [(i, h) for i in range(len(subs)) for h in range(GLA_HEADS)]
    return tb, sub, subs, units


def _gla_fwd(proj, wdecp, bdec, ggla):
    s = proj.shape[0]
    tb, sub, subs, units = _gla_units(s)
    nch = sub // GLA_CHUNK

    def body(qk_ref, v_ref, gz_ref, ga_ref, wd_ref, bd_ref, gg_ref, tri_ref, og_ref, opre_ref, sprev_ref, st_ref):
        @pl.when(pl.program_id(0) == 0)
        def _():
            st_ref[...] = jnp.zeros_like(st_ref)

        lo_h = lax.broadcasted_iota(jnp.int32, (sub, LANES), 1) < GLA_DK
        tril = tri_ref[...] > 0.5
        tril_b = tri_ref[...].astype(BF)
        ones_b = jnp.ones((LANES, LANES), BF)
        gg, wd, bd = gg_ref[...], wd_ref[...], bd_ref[...]
        chunks = _chunks(sub)
        lanes = [slice(h * LANES, (h + 1) * LANES) for h in range(GLA_HEADS)]
        com = [_gla_block_common(qk_ref[sl, :], ga_ref[sl, :], wd, bd, tril_b) for sl in subs]
        decs = [[jnp.exp(bl) for bl in cm[7]] for cm in com]
        a = {(i, h): _head(com[i][4], h, lo_h).astype(BF) for i, h in units}
        bm = {(i, h): _head(com[i][5], h, lo_h).astype(BF) for i, h in units}
        ktl = {(i, h): _head(com[i][6], h, lo_h).astype(BF) for i, h in units}
        vh = {(i, h): v_ref[subs[i], lanes[h]].astype(BF) for i, h in units}
        sc = {u: _dot(a[u], bm[u], NT) for u in units}
        upd = {u: [_dot(vh[u][rs], ktl[u][rs], TN) for rs in chunks] for u in units}
        p = {u: jnp.where(tril, sc[u], 0.0).astype(BF) for u in units}
        o = {u: _dot(p[u], vh[u]) for u in units}
        states = {}
        for h in range(GLA_HEADS):
            st = st_ref[h]
            for i in range(len(subs)):
                entering = []
                for c in range(nch):
                    entering.append(st)
                    sprev_ref[i * nch + c, h] = st
                    st = st * decs[i][c][:, LANES * (h // 2):LANES * (h // 2 + 1)] + upd[(i, h)][c]
                states[(i, h)] = entering
            st_ref[h] = st
        inter = {u: [_dot(a[u][rs], states[u][c].astype(BF), NT) for c, rs in enumerate(chunks)] for u in units}
        o = {u: o[u] + jnp.concatenate(inter[u], axis=0) for u in units}
        ms = {u: _lane_mean(o[u] * o[u], ones_b) for u in units}
        for i, h in units:
            gzh = gz_ref[subs[i], lanes[h]]
            opre_ref[subs[i], lanes[h]] = o[(i, h)]
            og_ref[subs[i], lanes[h]] = (((o[(i, h)] * lax.rsqrt(ms[(i, h)] + RMS_EPS)) * gg[:, lanes[h]])
                                         * (gzh * _sigmoid(gzh))).astype(og_ref.dtype)

    def col(width, off):
        return pl.BlockSpec((tb, width), lambda i: (i, off // width))

    return pl.pallas_call(
        body, name="gla_fwd", grid=(s // tb,),
        in_specs=[col(512, OFF_QK), col(512, OFF_V), col(512, OFF_GZ), col(LANES, OFF_GA),
                  _full((LANES, 256)), _full((1, 256)), _full((1, 512)), _full((sub, sub))],
        out_specs=[pl.BlockSpec((tb, 512), lambda i: (i, 0)), pl.BlockSpec((tb, 512), lambda i: (i, 0)),
                   pl.BlockSpec((tb // GLA_CHUNK, GLA_HEADS, LANES, LANES), lambda i: (i, 0, 0, 0))],
        out_shape=[jax.ShapeDtypeStruct((s, 512), BF), jax.ShapeDtypeStruct((s, 512), F32),
                   jax.ShapeDtypeStruct((s // GLA_CHUNK, GLA_HEADS, LANES, LANES), F32)],
        scratch_shapes=[pltpu.VMEM((GLA_HEADS, LANES, LANES), F32)],
        compiler_params=_params(("arbitrary",)),
    )(proj, proj, proj, proj, wdecp, bdec, ggla, _gla_triangle(sub))


def _gla_bwd(proj, dog, opre, sprev, wdecp, bdec, ggla):
    s = proj.shape[0]
    tb, sub, subs, units = _gla_units(s)
    nsub = len(subs)
    nch = sub // GLA_CHUNK
    nb = s // tb

    def body(qk_ref, v_ref, gz_ref, ga_ref, dog_ref, opre_ref, sprev_ref, wd_ref, bd_ref, gg_ref, tri_ref, triu_ref,
             dqk_ref, dv_ref, dgz_ref, dga_ref, dwd_ref, dbd_ref, dgg_ref, dst_ref):
        @pl.when(pl.program_id(0) == 0)
        def _():
            dst_ref[...] = jnp.zeros_like(dst_ref)
            dwd_ref[...] = jnp.zeros_like(dwd_ref)
            dbd_ref[...] = jnp.zeros_like(dbd_ref)
            dgg_ref[...] = jnp.zeros_like(dgg_ref)

        lo_h = lax.broadcasted_iota(jnp.int32, (sub, LANES), 1) < GLA_DK
        tril = tri_ref[...] > 0.5
        tril_b = tri_ref[...].astype(BF)
        triu_b = triu_ref[...].astype(BF)
        ones_b = jnp.ones((LANES, LANES), BF)
        last_row = (lax.broadcasted_iota(jnp.int32, (sub, LANES), 0) % GLA_CHUNK) == GLA_CHUNK - 1
        wd, gg, bd = wd_ref[...], gg_ref[...], bd_ref[...]
        chunks = _chunks(sub)
        lanes = [slice(h * LANES, (h + 1) * LANES) for h in range(GLA_HEADS)]
        blks = [slice(LANES * (h // 2), LANES * (h // 2 + 1)) for h in range(GLA_HEADS)]
        ga = [ga_ref[sl, :] for sl in subs]
        com = [_gla_block_common(qk_ref[sl, :], ga[i], wd, bd, tril_b) for i, sl in enumerate(subs)]
        decs = [[jnp.exp(bl) for bl in cm[7]] for cm in com]
        a = {(i, h): _head(com[i][4], h, lo_h).astype(BF) for i, h in units}
        bm = {(i, h): _head(com[i][5], h, lo_h).astype(BF) for i, h in units}
        ktl = {(i, h): _head(com[i][6], h, lo_h).astype(BF) for i, h in units}
        vh = {(i, h): v_ref[subs[i], lanes[h]].astype(BF) for i, h in units}
        sc = {u: _dot(a[u], bm[u], NT) for u in units}

        o = {(i, h): opre_ref[subs[i], lanes[h]] for i, h in units}
        ms = {u: _lane_mean(o[u] * o[u], ones_b) for u in units}
        gz = {(i, h): gz_ref[subs[i], lanes[h]] for i, h in units}
        dog = {(i, h): dog_ref[subs[i], lanes[h]] for i, h in units}
        sg = {u: _sigmoid(gz[u]) for u in units}
        r = {u: lax.rsqrt(ms[u] + RMS_EPS) for u in units}
        ohat = {u: o[u] * r[u] for u in units}
        sil = {u: gz[u] * sg[u] for u in units}
        for i, h in units:
            u = (i, h)
            dgz_ref[subs[i], lanes[h]] = (dog[u] * (ohat[u] * gg[:, lanes[h]])
                                          * (sg[u] * (1.0 + gz[u] * (1.0 - sg[u])))).astype(dgz_ref.dtype)
            dgg_ref[:, lanes[h]] += jnp.sum(dog[u] * sil[u] * ohat[u], axis=0, keepdims=True)
        dn = {(i, h): dog[(i, h)] * sil[(i, h)] * gg[:, lanes[h]] for i, h in units}
        mdn = {u: _lane_mean(dn[u] * ohat[u], ones_b) for u in units}
        do = {u: (r[u] * (dn[u] - ohat[u] * mdn[u])).astype(BF) for u in units}

        p = {u: jnp.where(tril, sc[u], 0.0).astype(BF) for u in units}
        dpr = {u: _dot(do[u], vh[u], NT) for u in units}
        incr = {u: [_dot(do[u][rs], a[u][rs], TN) for rs in chunks] for u in units}
        dv = {u: _dot(p[u], do[u], TN) for u in units}
        dp = {u: jnp.where(tril, dpr[u], 0.0).astype(BF) for u in units}
        dqd = {u: _dot(dp[u], bm[u]) for u in units}
        dkd = {u: _dot(dp[u], a[u], TN) for u in units}
        st = {(i, h): [sprev_ref[i * nch + c, h] for c in range(nch)] for i, h in units}
        leaving = {}
        for h in range(GLA_HEADS):
            d = dst_ref[h]
            for i in reversed(range(nsub)):
                out = [None] * nch
                for c in reversed(range(nch)):
                    out[c] = d
                    d = d * decs[i][c][:, blks[h]] + incr[(i, h)][c]
                leaving[(i, h)] = out
            dst_ref[h] = d
        lv_b = {u: [leaving[u][c].astype(BF) for c in range(nch)] for u in units}
        dv_s = {u: [_dot(ktl[u][rs], lv_b[u][c], NT) for c, rs in enumerate(chunks)] for u in units}
        dqd_s = {u: [_dot(do[u][rs], st[u][c].astype(BF)) for c, rs in enumerate(chunks)] for u in units}
        dkt_s = {u: [_dot(vh[u][rs], lv_b[u][c]) for c, rs in enumerate(chunks)] for u in units}
        ddec = {u: [jnp.sum(leaving[u][c] * st[u][c], axis=0, keepdims=True) for c in range(nch)] for u in units}
        for i, h in units:
            dv_ref[subs[i], lanes[h]] = (dv[(i, h)] + jnp.concatenate(dv_s[(i, h)], axis=0)).astype(dv_ref.dtype)
        dqd = {u: dqd[u] + jnp.concatenate(dqd_s[u], axis=0) for u in units}
        dkt = {u: jnp.concatenate(dkt_s[u], axis=0) for u in units}

        db = []
        for i, sl in enumerate(subs):
            _, eq, ek, f, qd, kd, kt, _ = com[i]
            parts = []
            for pair in range(GLA_HEADS // 2):
                blk, u0, u1 = blks[2 * pair], (i, 2 * pair), (i, 2 * pair + 1)
                dqd_b, dkd_b, dkt_b = dqd[u0] + dqd[u1], dkd[u0] + dkd[u1], dkt[u0] + dkt[u1]
                dqk_ref[sl, blk] = (dqd_b * (eq[:, blk] * GLA_DK ** -0.5)).astype(dqk_ref.dtype)
                dqk_ref[sl, 256 + LANES * pair:256 + LANES * (pair + 1)] = (dkd_b * ek[:, blk] + dkt_b * f[:, blk]).astype(dqk_ref.dtype)
                dkt_kt = dkt_b * kt[:, blk]
                dbp = dqd_b * qd[:, blk] - dkd_b * kd[:, blk] - dkt_kt
                dbl = [jnp.sum(dkt_kt[rs], axis=0, keepdims=True) + (ddec[u0][c] + ddec[u1][c]) * decs[i][c][:, blk]
                       for c, rs in enumerate(chunks)]
                parts.append(jnp.where(last_row, dbp + _per_chunk_rows(dbl, LANES), dbp))
            db.append(jnp.concatenate(parts, axis=1))
        dla = [_tri_matmul(triu_b, db[i]) for i in range(nsub)]
        dz32 = [dla[i] * (1.0 / GLA_TAU) * _sigmoid(-com[i][0]) for i in range(nsub)]
        dz = [t.astype(BF) for t in dz32]
        for i, sl in enumerate(subs):
            dga_ref[sl, :] = _dot(dz[i], wd, NT).astype(dga_ref.dtype)
            dwd_ref[...] += _dot(ga[i].astype(BF), dz[i], TN)
            dbd_ref[...] += jnp.sum(dz32[i], axis=0, keepdims=True)

    def col(width, off):
        return pl.BlockSpec((tb, width), lambda i: (nb - 1 - i, off // width))

    def rev(width):
        return pl.BlockSpec((tb, width), lambda i: (nb - 1 - i, 0))

    return pl.pallas_call(
        body, name="gla_bwd", grid=(nb,),
        in_specs=[col(512, OFF_QK), col(512, OFF_V), col(512, OFF_GZ), col(LANES, OFF_GA), rev(512), rev(512),
                  pl.BlockSpec((tb // GLA_CHUNK, GLA_HEADS, LANES, LANES), lambda i: (nb - 1 - i, 0, 0, 0)),
                  _full((LANES, 256)), _full((1, 256)), _full((1, 512)), _full((sub, sub)), _full((sub, sub))],
        out_specs=[rev(512), rev(512), rev(512), rev(LANES), _full((LANES, 256)), _full((1, 256)), _full((1, 512))],
        out_shape=[jax.ShapeDtypeStruct((s, 512), BF), jax.ShapeDtypeStruct((s, 512), BF),
                   jax.ShapeDtypeStruct((s, 512), BF), jax.ShapeDtypeStruct((s, LANES), BF),
                   jax.ShapeDtypeStruct((LANES, 256), F32), jax.ShapeDtypeStruct((1, 256), F32),
                   jax.ShapeDtypeStruct((1, 512), F32)],
        scratch_shapes=[pltpu.VMEM((GLA_HEADS, LANES, LANES), F32)],
        compiler_params=_params(("arbitrary",)),
    )(proj, proj, proj, proj, dog, opre, sprev, wdecp, bdec, ggla, _gla_triangle(sub), _gla_triangle(sub).T)


_SWA_COL_HEADS = (0, 2, 1, 3, 4, 6, 5, 7)
_SWA_COLS = SWA_HEADS * SWA_BLOCK


def _swa_masks():
    lo2 = lax.broadcasted_iota(jnp.int32, (2 * SWA_BLOCK, LANES), 1) < 64
    lane1 = lax.broadcasted_iota(jnp.int32, (SWA_BLOCK, LANES), 1)
    first_half = (lane1 % 64) < 32
    key = lax.broadcasted_iota(jnp.int32, (SWA_BLOCK, _SWA_COLS), 0)
    query = lax.broadcasted_iota(jnp.int32, (SWA_BLOCK, _SWA_COLS), 1) % SWA_BLOCK
    return lo2, lane1 < 64, first_half, key > query


def _merge_band(t, prev_mask, prev_bias=None):
    prev = t[:SWA_BLOCK] if prev_bias is None else t[:SWA_BLOCK] + prev_bias
    return jnp.where(prev_mask, prev, t[SWA_BLOCK:])


def _split_band(t, prev_mask_b):
    prev = t * prev_mask_b
    return jnp.concatenate([prev, t - prev], axis=0)


def _kv_variants(t, lo2):
    tr = pltpu.roll(t, 64, 1)
    lo_v = [jnp.where(lo2, t, 0.0).astype(BF), jnp.where(lo2, tr, 0.0).astype(BF)]
    hi_v = [jnp.where(lo2, 0.0, tr).astype(BF), jnp.where(lo2, 0.0, t).astype(BF)]
    return lo_v, hi_v


def _kv_variants_t(t):
    tt = t.T
    sw = jnp.concatenate([tt[64:], tt[:64]], axis=0)
    top = lax.broadcasted_iota(jnp.int32, tt.shape, 0) < 64
    lo_v = [jnp.where(top, tt, 0.0).astype(BF), jnp.where(top, sw, 0.0).astype(BF)]
    hi_v = [jnp.where(top, 0.0, sw).astype(BF), jnp.where(top, 0.0, tt).astype(BF)]
    return lo_v, hi_v


def _swa_scores(qg, k_lo, k_hi):
    return jnp.concatenate([_dot(k_lo[0], qg[0], NT), _dot(k_hi[0], qg[0], NT),
                            _dot(k_lo[1], qg[1], NT), _dot(k_hi[1], qg[1], NT)], axis=1)


def _sink_row(sinks_ref):
    return jnp.concatenate([jnp.full((1, SWA_BLOCK), sinks_ref[0, hd], F32) for hd in _SWA_COL_HEADS], axis=1)


def _swa_softmax(st, prev_mask, prev_bias, sink):
    st = _merge_band(st, prev_mask, prev_bias)
    m = jnp.maximum(jnp.max(st, axis=0, keepdims=True), sink)
    ex = jnp.exp(st - m)
    es = jnp.exp(sink - m)
    inv = 1.0 / (jnp.sum(ex, axis=0, keepdims=True) + es)
    return ex, es, inv


def _no_prev_bias(block_index):
    return jnp.where(block_index > 0, 0.0, -1e30).astype(F32)


def _swa_queries(sq_ref, rows, cosb, sinb, first_half):
    qs = [_rope(sq_ref[rows, p * LANES:(p + 1) * LANES], cosb, sinb, first_half) * 0.125 for p in range(4)]
    return [jnp.concatenate(qs[0:2], axis=0), jnp.concatenate(qs[2:4], axis=0)]


def _swa_fwd(proj, cos, sin, sinks):
    s = proj.shape[0]
    nq = min(SWA_QBLOCKS, s // SWA_BLOCK)
    tq = nq * SWA_BLOCK

    def body(sq_ref, sz_ref, sk_ref, sv_ref, cos_ref, sin_ref, sinks_ref, os_ref, opre_ref, kprev, vprev):
        n = pl.program_id(0)

        @pl.when(n == 0)
        def _():
            kprev[...] = jnp.zeros_like(kprev)
            vprev[...] = jnp.zeros_like(vprev)

        lo2, _, first_half, prev_mask = _swa_masks()
        prev_mask_b = jnp.where(prev_mask, 1.0, 0.0).astype(BF)
        sink = _sink_row(sinks_ref)
        blocks = range(nq)
        rows = [slice(j * SWA_BLOCK, (j + 1) * SWA_BLOCK) for j in blocks]
        cosb = [cos_ref[rows[j], :] for j in blocks]
        sinb = [sin_ref[rows[j], :] for j in blocks]
        kc = [_rope(sk_ref[rows[j], :], cosb[j], sinb[j], first_half) for j in blocks]
        vc = [sv_ref[rows[j], :] for j in blocks]
        kcat = [jnp.concatenate([kprev[...] if j == 0 else kc[j - 1], kc[j]], axis=0) for j in blocks]
        vcat = [jnp.concatenate([vprev[...] if j == 0 else vc[j - 1], vc[j]], axis=0) for j in blocks]
        kprev[...] = kc[-1]
        vprev[...] = vc[-1]
        kvar = [_kv_variants(kcat[j], lo2) for j in blocks]
        vtvar = [_kv_variants_t(vcat[j]) for j in blocks]
        qg = [[q.astype(BF) for q in _swa_queries(sq_ref, rows[j], cosb[j], sinb[j], first_half)] for j in blocks]
        st = [_swa_scores(qg[j], *kvar[j]) for j in blocks]
        soft = [_swa_softmax(st[j], prev_mask, _no_prev_bias(n) if j == 0 else None, sink) for j in blocks]
        pt = [_split_band(soft[j][0].astype(BF), prev_mask_b) for j in blocks]
        og = {}
        for j in blocks:
            inv = soft[j][2]
            for g in range(2):
                c0, c1, c2 = 512 * g, 512 * g + 256, 512 * g + 512
                ot = (_dot(vtvar[j][0][g], pt[j][:, c0:c1]) * inv[:, c0:c1]
                      + _dot(vtvar[j][1][g], pt[j][:, c1:c2]) * inv[:, c1:c2])
                og[(j, g)] = ot.T
        for j in blocks:
            for g in range(2):
                for i in range(2):
                    ls = slice((2 * g + i) * LANES, (2 * g + i + 1) * LANES)
                    o = og[(j, g)][i * SWA_BLOCK:(i + 1) * SWA_BLOCK]
                    sz = sz_ref[rows[j], ls]
                    opre_ref[rows[j], ls] = o
                    os_ref[rows[j], ls] = (o * (sz * _sigmoid(sz))).astype(os_ref.dtype)

    def col(width, off):
        return pl.BlockSpec((tq, width), lambda i: (i, off // width))

    row = pl.BlockSpec((tq, LANES), lambda i: (i, 0))
    return pl.pallas_call(
        body, name="swa_fwd", grid=(s // tq,),
        in_specs=[col(512, OFF_SQ), col(512, OFF_SZ), col(LANES, OFF_SK), col(LANES, OFF_SV), row, row,
                  pl.BlockSpec(memory_space=pltpu.SMEM)],
        out_specs=[pl.BlockSpec((tq, 512), lambda i: (i, 0))] * 2,
        out_shape=[jax.ShapeDtypeStruct((s, 512), BF), jax.ShapeDtypeStruct((s, 512), F32)],
        scratch_shapes=[pltpu.VMEM((SWA_BLOCK, LANES), F32)] * 2,
        compiler_params=_params(("arbitrary",)),
    )(proj, proj, proj, proj, cos, sin, sinks)


def _swa_bwd(proj, dos, opre, cos, sin, sinks):
    s = proj.shape[0]
    nq = min(SWA_QBLOCKS, s // SWA_BLOCK)
    tq = nq * SWA_BLOCK

    def body(sq_ref, sz_ref, sk_ref, sv_ref, dos_ref, opre_ref, cos_ref, sin_ref, sinks_ref,
             dsq_ref, dsz_ref, dsk_ref, dsv_ref, dsink_ref, kprev, vprev, cprev, sprev):
        n = pl.program_id(0)

        @pl.when(n == 0)
        def _():
            kprev[...] = jnp.zeros_like(kprev)
            vprev[...] = jnp.zeros_like(vprev)
            cprev[...] = jnp.zeros_like(cprev)
            sprev[...] = jnp.zeros_like(sprev)
            for hd in range(SWA_HEADS):
                dsink_ref[0, hd] = 0.0

        lo2, lo1, first_half, prev_mask = _swa_masks()
        prev_mask_b = jnp.where(prev_mask, 1.0, 0.0).astype(BF)
        lo1s = jnp.concatenate([lo1, lo1], axis=0)
        sink = _sink_row(sinks_ref)

        def home(m0, m1):
            t0 = m0 + pltpu.roll(m0, 64, 1)
            t1 = m1 + pltpu.roll(m1, 64, 1)
            return jnp.where(lo2, t0, t1)

        kp, vp, cp_, sp_ = kprev[...], vprev[...], cprev[...], sprev[...]
        for j in range(nq):
            rows = slice(j * SWA_BLOCK, (j + 1) * SWA_BLOCK)
            blk = n * nq + j
            cosb, sinb = cos_ref[rows, :], sin_ref[rows, :]
            kc = _rope(sk_ref[rows, :], cosb, sinb, first_half)
            vc = sv_ref[rows, :]
            kcat = jnp.concatenate([kp, kc], axis=0)
            k_lo, k_hi = _kv_variants(kcat, lo2)
            kt_lo, kt_hi = _kv_variants_t(kcat)
            v_lo, v_hi = _kv_variants(jnp.concatenate([vp, vc], axis=0), lo2)
            qg32 = _swa_queries(sq_ref, rows, cosb, sinb, first_half)
            qg = [q.astype(BF) for q in qg32]
            ex, es, inv = _swa_softmax(_swa_scores(qg, k_lo, k_hi), prev_mask, _no_prev_bias(n) if j == 0 else None, sink)
            pr, ps = ex * inv, es * inv

            dog32 = []
            for g in range(2):
                parts = []
                for i in range(2):
                    ls = slice((2 * g + i) * LANES, (2 * g + i + 1) * LANES)
                    sz = sz_ref[rows, ls]
                    sg = _sigmoid(sz)
                    dos_p = dos_ref[rows, ls]
                    dsz_ref[rows, ls] = (dos_p * opre_ref[rows, ls] * (sg * (1.0 + sz * (1.0 - sg)))).astype(dsz_ref.dtype)
                    parts.append(dos_p * (sz * sg))
                dog32.append(jnp.concatenate(parts, axis=0))
            dog = [t.astype(BF) for t in dog32]
            dpr = _merge_band(jnp.concatenate([_dot(v_lo[0], dog[0], NT), _dot(v_hi[0], dog[0], NT),
                                               _dot(v_lo[1], dog[1], NT), _dot(v_hi[1], dog[1], NT)], axis=1), prev_mask)
            rd = jnp.sum(pr * dpr, axis=0, keepdims=True)
            ds = _split_band((pr * (dpr - rd)).astype(BF), prev_mask_b)
            prb = _split_band(pr.astype(BF), prev_mask_b)
            sink_term = ps * rd
            for r, hd in enumerate(_SWA_COL_HEADS):
                dsink_ref[0, hd] += -jnp.sum(sink_term[:, r * SWA_BLOCK:(r + 1) * SWA_BLOCK])

            dk_g, dv_g = [], []
            for g in range(2):
                c0, c1, c2 = 512 * g, 512 * g + 256, 512 * g + 512
                dq = (_dot(kt_lo[g], ds[:, c0:c1]) + _dot(kt_hi[g], ds[:, c1:c2])).T
                for i in range(2):
                    ls = slice((2 * g + i) * LANES, (2 * g + i + 1) * LANES)
                    dsq_ref[rows, ls] = _rope_t(dq[i * SWA_BLOCK:(i + 1) * SWA_BLOCK] * 0.125, cosb, sinb,
                                                first_half).astype(dsq_ref.dtype)
                q_split = jnp.concatenate([jnp.where(lo1s, qg32[g], 0.0), jnp.where(lo1s, 0.0, qg32[g])], axis=0).astype(BF)
                do_split = jnp.concatenate([jnp.where(lo1s, dog32[g], 0.0), jnp.where(lo1s, 0.0, dog32[g])], axis=0).astype(BF)
                dk_g.append(_dot(ds[:, c0:c2], q_split))
                dv_g.append(_dot(prb[:, c0:c2], do_split))
            dk = home(dk_g[0], dk_g[1])
            dv = home(dv_g[0], dv_g[1])
            cur = pl.ds(pl.multiple_of(blk * SWA_BLOCK, SWA_BLOCK), SWA_BLOCK)
            dsk_ref[cur, :] = _rope_t(dk[SWA_BLOCK:], cosb, sinb, first_half)
            dsv_ref[cur, :] = dv[SWA_BLOCK:]
            dk_prev = _rope_t(dk[:SWA_BLOCK], cp_, sp_, first_half)
            dv_prev = dv[:SWA_BLOCK]
            if j == 0:
                @pl.when(n > 0)
                def _():
                    prv = pl.ds(pl.multiple_of((blk - 1) * SWA_BLOCK, SWA_BLOCK), SWA_BLOCK)
                    dsk_ref[prv, :] += dk_prev
                    dsv_ref[prv, :] += dv_prev
            else:
                prv = pl.ds(pl.multiple_of((blk - 1) * SWA_BLOCK, SWA_BLOCK), SWA_BLOCK)
                dsk_ref[prv, :] += dk_prev
                dsv_ref[prv, :] += dv_prev
            kp, vp, cp_, sp_ = kc, vc, cosb, sinb
        kprev[...] = kp
        vprev[...] = vp
        cprev[...] = cp_
        sprev[...] = sp_

    def col(width, off):
        return pl.BlockSpec((tq, width), lambda i: (i, off // width))

    row = pl.BlockSpec((tq, LANES), lambda i: (i, 0))
    wide = pl.BlockSpec((tq, 512), lambda i: (i, 0))
    return pl.pallas_call(
        body, name="swa_bwd", grid=(s // tq,),
        in_specs=[col(512, OFF_SQ), col(512, OFF_SZ), col(LANES, OFF_SK), col(LANES, OFF_SV), wide, wide, row, row,
                  pl.BlockSpec(memory_space=pltpu.SMEM)],
        out_specs=[wide, wide, _full((s, LANES)), _full((s, LANES)), pl.BlockSpec(memory_space=pltpu.SMEM)],
        out_shape=[jax.ShapeDtypeStruct((s, 512), BF), jax.ShapeDtypeStruct((s, 512), BF),
                   jax.ShapeDtypeStruct((s, LANES), F32), jax.ShapeDtypeStruct((s, LANES), F32),
                   jax.ShapeDtypeStruct((1, SWA_HEADS), F32)],
        scratch_shapes=[pltpu.VMEM((SWA_BLOCK, LANES), F32)] * 4,
        compiler_params=_params(("arbitrary",)),
    )(proj, proj, proj, proj, dos, opre, cos, sin, sinks)


def _outproj(og, osw, w_out, x2d, target, gate, g_final):
    s = x2d.shape[0]
    tm = min(512, s)

    def body(og_ref, os_ref, w_ref, x_ref, t_ref, gate_ref, gf_ref,
             dx2_ref, dog_ref, dos_ref, dw_ref, loss_ref, dgf_ref, dgate_ref):
        @pl.when(pl.program_id(0) == 0)
        def _():
            dw_ref[...] = jnp.zeros_like(dw_ref)
            loss_ref[...] = jnp.zeros_like(loss_ref)
            dgf_ref[...] = jnp.zeros_like(dgf_ref)
            dgate_ref[...] = jnp.zeros_like(dgate_ref)

        w = w_ref[...]
        gate, gf = gate_ref[...], gf_ref[...]
        subs = _subtiles(tm)
        ogv = [og_ref[sl, :] for sl in subs]
        osv = [os_ref[sl, :] for sl in subs]
        y = [_dot(ogv[k], w[:512]) + _dot(osv[k], w[512:]) for k in range(len(subs))]
        dys = []
        for k, sl in enumerate(subs):
            x2 = x_ref[sl, :] + gate * y[k]
            r = lax.rsqrt(jnp.mean(x2 * x2, axis=-1, keepdims=True) + RMS_EPS)
            xn = x2 * r
            err = xn * gf - t_ref[sl, :]
            loss_ref[...] += 0.5 * jnp.sum(jnp.mean(err * err, axis=-1, keepdims=True), axis=0, keepdims=True)
            dyf = err * (1.0 / D_MODEL)
            dgf_ref[...] += jnp.sum(dyf * xn, axis=0, keepdims=True)
            t = dyf * gf
            dx2 = r * (t - xn * jnp.mean(t * xn, axis=-1, keepdims=True))
            dx2_ref[sl, :] = dx2
            dgate_ref[...] += jnp.sum(dx2 * y[k], axis=0, keepdims=True)
            dys.append((dx2 * gate).astype(BF))
            dmix = _dot(dys[k], w, NT)
            dog_ref[sl, :] = dmix[:, :512]
            dos_ref[sl, :] = dmix[:, 512:]
        dy = jnp.concatenate(dys, axis=0)
        dw_ref[:512, :] += _dot(og_ref[...], dy, TN)
        dw_ref[512:, :] += _dot(os_ref[...], dy, TN)

    half = pl.BlockSpec((tm, 512), lambda i: (i, 0))
    rowb = pl.BlockSpec((tm, D_MODEL), lambda i: (i, 0))
    vec = _full((1, D_MODEL))
    return pl.pallas_call(
        body, name="outproj", grid=(s // tm,),
        in_specs=[half, half, _full((D_MODEL, D_MODEL)), rowb, rowb, vec, vec],
        out_specs=[rowb, half, half, _full((D_MODEL, D_MODEL)), _full((1, 1)), vec, vec],
        out_shape=[jax.ShapeDtypeStruct((s, D_MODEL), F32), jax.ShapeDtypeStruct((s, 512), F32),
                   jax.ShapeDtypeStruct((s, 512), F32), jax.ShapeDtypeStruct((D_MODEL, D_MODEL), F32),
                   jax.ShapeDtypeStruct((1, 1), F32), jax.ShapeDtypeStruct((1, D_MODEL), F32),
                   jax.ShapeDtypeStruct((1, D_MODEL), F32)],
        compiler_params=_params(("arbitrary",)),
    )(og, osw, w_out, x2d, target, gate, g_final)


_PIECES = ((OFF_QK, 512), (OFF_V, 512), (OFF_GZ, 512), (OFF_SQ, 512), (OFF_SZ, 512),
           (OFF_SK, LANES), (OFF_SV, LANES), (OFF_GA, LANES))

_UNPAD_ROWS = ((OFF_QK, 0, 1024),
               (OFF_GA, 1024, GLA_RANK),
               (OFF_GZ, 1040, 1024),
               (OFF_SK, 2064, 256),
               (OFF_SZ, 2320, 512))


def _inproj_bwd(x2d, shift, sc1p, g_norm, wpad_t, dx2, pieces):
    s = x2d.shape[0]
    tm = min(512, s)
    nsteps = s // tm

    def body(x_ref, sh_ref, sc_ref, g_ref, w_hbm, dx2_ref, *rest):
        piece_refs = rest[:len(_PIECES)]
        gx_ref, dw_hbm, dsh_ref, dsc_ref, dg_ref, w_vm, dw_vm, sem, out_sems = rest[len(_PIECES):]
        i = pl.program_id(0)

        @pl.when(i == 0)
        def _():
            cp = pltpu.make_async_copy(w_hbm, w_vm, sem)
            cp.start()
            dw_vm[...] = jnp.zeros_like(dw_vm)
            dsh_ref[...] = jnp.zeros_like(dsh_ref)
            dsc_ref[...] = jnp.zeros_like(dsc_ref)
            dg_ref[...] = jnp.zeros_like(dg_ref)
            cp.wait()

        g, sc1p_v, shift_v = g_ref[...], sc_ref[...], sh_ref[...]
        subs = _subtiles(tm)
        dhs = []
        for sl in subs:
            dh = None
            for (off, width), pr in zip(_PIECES, piece_refs):
                part = _dot(pr[sl, :].astype(BF), w_vm[off:off + width, :])
                dh = part if dh is None else dh + part
            dhs.append(dh)
        norm = [_modnorm(x_ref[sl, :], g, sc1p_v, shift_v) for sl in subs]
        hb = jnp.concatenate([h.astype(BF) for _, _, h in norm], axis=0)
        for (off, width), pr in zip(_PIECES, piece_refs):
            dw_vm[off:off + width, :] += _dot(pr[...].astype(BF), hb, TN)
        for sl, (xn, r, _), dh in zip(subs, norm, dhs):
            dsh_ref[...] += jnp.sum(dh, axis=0, keepdims=True)
            dsc_ref[...] += jnp.sum(dh * (xn * g), axis=0, keepdims=True)
            dg_ref[...] += jnp.sum(dh * xn * sc1p_v, axis=0, keepdims=True)
            dxn = dh * g * sc1p_v
            gx_ref[sl, :] = dx2_ref[sl, :] + r * (dxn - xn * jnp.mean(dxn * xn, axis=-1, keepdims=True))

        @pl.when(i == nsteps - 1)
        def _():
            copies = [pltpu.make_async_copy(dw_vm.at[src:src + n], dw_hbm.at[dst:dst + n], out_sems.at[k])
                      for k, (src, dst, n) in enumerate(_UNPAD_ROWS)]
            for cp in copies:
                cp.start()
            for cp in copies:
                cp.wait()

    rowb = pl.BlockSpec((tm, D_MODEL), lambda i: (i, 0))
    vec = _full((1, D_MODEL))
    anyspec = pl.BlockSpec(memory_space=pl.ANY)
    piece_specs = [pl.BlockSpec((tm, width), lambda i: (i, 0)) for _, width in _PIECES]
    return pl.pallas_call(
        body, name="inproj_bwd", grid=(nsteps,),
        in_specs=[rowb, vec, vec, vec, anyspec, rowb] + piece_specs,
        out_specs=[rowb, anyspec, vec, vec, vec],
        out_shape=[jax.ShapeDtypeStruct((s, D_MODEL), F32), jax.ShapeDtypeStruct((D_IN, D_MODEL), F32),
                   jax.ShapeDtypeStruct((1, D_MODEL), F32), jax.ShapeDtypeStruct((1, D_MODEL), F32),
                   jax.ShapeDtypeStruct((1, D_MODEL), F32)],
        scratch_shapes=[pltpu.VMEM((D_PAD, D_MODEL), BF), pltpu.VMEM((D_PAD, D_MODEL), F32), pltpu.SemaphoreType.DMA,
                        pltpu.SemaphoreType.DMA((len(_UNPAD_ROWS),))],
        compiler_params=_params(("arbitrary",)),
    )(x2d, shift, sc1p, g_norm, wpad_t, dx2, *pieces)


def _adam(w, g, m, v):
    m2 = ADAM_B1 * m + (1.0 - ADAM_B1) * g
    v2 = ADAM_B2 * v + (1.0 - ADAM_B2) * (g * g)
    m_hat = m2 / (1.0 - ADAM_B1 ** ADAM_STEP)
    v_hat = v2 / (1.0 - ADAM_B2 ** ADAM_STEP)
    delta = -ADAM_LR * (m_hat / (jnp.sqrt(v_hat) + ADAM_EPS) + ADAM_WD * w)
    return delta, m2, v2


def _adamw(w, g, m, v, name):
    rr, cc = w.shape
    tc = min(256, cc)

    def body(w_ref, g_ref, m_ref, v_ref, d_ref, m2_ref, v2_ref):
        d_ref[...], m2_ref[...], v2_ref[...] = _adam(w_ref[...], g_ref[...], m_ref[...], v_ref[...])

    blk = pl.BlockSpec((rr, tc), lambda i: (0, i))
    return pl.pallas_call(
        body, name=name, grid=(cc // tc,), in_specs=[blk] * 4, out_specs=[blk] * 3,
        out_shape=[jax.ShapeDtypeStruct((rr, cc), F32)] * 3,
        compiler_params=_params(("arbitrary",)),
    )(w, g, m, v)


def _adamw_t(w3, g, m3, v3, name):
    rr, _, cc = w3.shape
    tc = min(256, cc)

    def body(w_hbm, g_ref, m_hbm, v_hbm, d_hbm, m2_hbm, v2_hbm, g3_hbm, w_vm, m_vm, v_vm, d_vm, m2_vm, v2_vm, in_sems, out_sems):
        cols = pl.ds(pl.multiple_of(pl.program_id(0) * tc, tc), tc)
        loads = [pltpu.make_async_copy(src.at[:, 0, cols], dst, in_sems.at[k])
                 for k, (src, dst) in enumerate(((w_hbm, w_vm), (m_hbm, m_vm), (v_hbm, v_vm)))]
        for cp in loads:
            cp.start()
        for cp in loads:
            cp.wait()
        d_vm[...], m2_vm[...], v2_vm[...] = _adam(w_vm[...], g_ref[...], m_vm[...], v_vm[...])
        stores = [pltpu.make_async_copy(src, dst.at[:, 0, cols], out_sems.at[k])
                  for k, (src, dst) in enumerate(((d_vm, d_hbm), (m2_vm, m2_hbm), (v2_vm, v2_hbm), (g_ref, g3_hbm)))]
        for cp in stores:
            cp.start()
        for cp in stores:
            cp.wait()

    hbm = pl.BlockSpec(memory_space=pl.ANY)
    return pl.pallas_call(
        body, name=name, grid=(cc // tc,), in_specs=[hbm, pl.BlockSpec((rr, tc), lambda i: (0, i)), hbm, hbm],
        out_specs=[hbm] * 4, out_shape=[jax.ShapeDtypeStruct((rr, 1, cc), F32)] * 4,
        scratch_shapes=[pltpu.VMEM((rr, tc), F32)] * 6 + [pltpu.SemaphoreType.DMA((3,)), pltpu.SemaphoreType.DMA((4,))],
        compiler_params=_params(("arbitrary",)),
    )(w3, g, m3, v3)


def _ada_update(c_all, dmod_cols, w, m, v):
    rr, cc = w.shape
    tr = min(256, rr)
    c_all = jnp.pad(c_all, ((0, 8), (0, 0)))
    dmod_cols = jnp.pad(dmod_cols, ((0, 8), (0, 0)))

    def body(c_ref, dm_ref, w_ref, m_ref, v_ref, g_ref, d_ref, m2_ref, v2_ref):
        cv = c_ref[...]
        sc = (cv * _sigmoid(cv)).astype(BF)
        g = _dot(sc, dm_ref[...].astype(BF), TN)
        g_ref[...] = g
        d_ref[...], m2_ref[...], v2_ref[...] = _adam(w_ref[...], g, m_ref[...], v_ref[...])

    blk = pl.BlockSpec((tr, cc), lambda i: (i, 0))
    return pl.pallas_call(
        body, name="ada_update", grid=(rr // tr,),
        in_specs=[pl.BlockSpec((16, tr), lambda i: (0, i)), _full((16, cc)), blk, blk, blk],
        out_specs=[blk] * 4, out_shape=[jax.ShapeDtypeStruct((rr, cc), F32)] * 4,
        compiler_params=_params(("arbitrary",)),
    )(c_all, dmod_cols, w, m, v)


def _small_update(parts, weights, moms, vels):
    n = len(weights)

    def body(*refs):
        p_refs, w_refs, m_refs, v_refs = refs[:n + 1], refs[n + 1:2 * n + 1], refs[2 * n + 1:3 * n + 1], refs[3 * n + 1:4 * n + 1]
        outs = refs[4 * n + 1:]
        for i in range(n):
            g = p_refs[i][0]
            for d in range(1, 8):
                g = g + p_refs[i][d]
            delta, m2, v2 = _adam(w_refs[i][...], g, m_refs[i][...], v_refs[i][...])
            outs[4 * i][...] = g
            outs[4 * i + 1][...] = delta
            outs[4 * i + 2][...] = m2
            outs[4 * i + 3][...] = v2
        tot = p_refs[n][0]
        for d in range(1, 8):
            tot = tot + p_refs[n][d]
        outs[4 * n][...] = tot

    out_shape = []
    for w in weights:
        out_shape += [jax.ShapeDtypeStruct(w.shape, F32)] * 4
    out_shape.append(jax.ShapeDtypeStruct(parts[n].shape[1:], F32))
    return pl.pallas_call(body, name="small_update", out_shape=out_shape, compiler_params=_params())(
        *parts, *weights, *moms, *vels)


def _pad_w_in_t(w):
    pad = jnp.zeros((LANES - GLA_RANK, w.shape[1]), w.dtype)
    return jnp.concatenate([w[dst:dst + n] for _, dst, n in sorted(_UNPAD_ROWS)] + [pad], axis=0)


def _rows8(a):
    flat = a.reshape(-1)
    rows = -(-flat.shape[0] // LANES)
    rows8 = -(-rows // 8) * 8
    flat = jnp.pad(flat, (0, rows8 * LANES - flat.shape[0]))
    return flat.reshape(rows8, LANES)


def kernel(x, c, positions, w_ada, b_ada, g_norm, w_in, w_decay, b_decay, g_gla_head, sinks, w_out, g_final, loss_target, m_w_ada, m_b_ada, m_g_norm, m_w_in, m_w_decay, m_b_decay, m_g_gla_head, m_sinks, m_w_out, m_g_final, v_w_ada, v_b_ada, v_g_norm, v_w_in, v_w_decay, v_b_decay, v_g_gla_head, v_sinks, v_w_out, v_g_final):
    ax, ay, ac = lax.axis_index("x"), lax.axis_index("y"), lax.axis_index("c")
    chip = 2 * ax + ay
    dev = 2 * chip + ac
    s = x.shape[1]
    x2d = x[0]
    target = loss_target[0]
    w_ada2, w_out2, w_dec2 = w_ada[0], w_out[0], w_decay[0]
    w_in_t = w_in[0].T
    ada_cols = w_ada2.shape[1]
    in_cols = w_in_t.shape[0]
    out_rows = w_out2.shape[0]
    half = D_MODEL // 2

    cw = jnp.concatenate([c.reshape(8, LANES), w_dec2.reshape(8, LANES)], axis=0)
    b_shard = lax.dynamic_slice(b_ada, (0, chip * ada_cols), (1, ada_cols))
    half_in = lax.dynamic_slice(w_in_t, (0, ac * half), (in_cols, half)).astype(BF)
    half_out = lax.dynamic_slice(w_out2, (ac * (out_rows // 2), 0), (out_rows // 2, D_MODEL)).astype(BF)
    inv_freq = 1.0 / (ROPE_THETA ** (jnp.arange(0, 64, 2, dtype=F32) / 64))
    first, mod_all, w_in_all, w_out_all, cos, sin = _prologue(
        cw, w_ada2, b_shard, half_in, half_out, positions.reshape(s, 1), jnp.tile(inv_freq, 4).reshape(1, LANES))

    first = first.reshape(8, 2, 8, LANES)
    c_all = first[:, 0].reshape(8, D_MODEL)
    w_dec_full = first[0::2, 1].reshape(4, GLA_RANK, 64).transpose(1, 0, 2).reshape(GLA_RANK, 256)
    mod = mod_all.reshape(4, 2, 8, ada_cols)[:, 0]
    mod = lax.dynamic_slice(mod, (0, dev, 0), (4, 1, ada_cols)).reshape(1, 4 * ada_cols)
    shift, sc1p, gate = mod[:, :D_MODEL], 1.0 + mod[:, D_MODEL:2 * D_MODEL], mod[:, 2 * D_MODEL:]
    w_in_all = w_in_all.reshape(4, 2, in_cols, half)
    wpad_t = _pad_w_in_t(w_in_all.transpose(0, 2, 1, 3).reshape(4 * in_cols, D_MODEL))
    w_out_all = w_out_all.reshape(D_MODEL, D_MODEL)

    wdecp = jnp.pad(w_dec_full, ((0, LANES - GLA_RANK), (0, 0))).astype(BF)

    proj = _inproj_fwd(x2d, shift, sc1p, g_norm, wpad_t)
    og, o_gla, sprev = _gla_fwd(proj, wdecp, b_decay, g_gla_head)
    osw, o_swa = _swa_fwd(proj, cos, sin, sinks)
    dx2, dog, dos, dw_out, loss_p, dgf, dgate = _outproj(og, osw, w_out_all, x2d, target, gate, g_final.reshape(1, D_MODEL))
    dsq, dsz, dsk, dsv, dsinks = _swa_bwd(proj, dos, o_swa, cos, sin, sinks)
    dqk, dv, dgz, dga, dwdp, dbd, dgg = _gla_bwd(proj, dog, o_gla, sprev, wdecp, b_decay, g_gla_head)
    pieces = (dqk, dv, dgz, dsq, dsz, dsk, dsv, dga)
    gx, dw_in_t, dshift, dscale, dgn = _inproj_bwd(x2d, shift, sc1p, g_norm, wpad_t, dx2, pieces)

    segs = [jnp.concatenate([dshift, dscale, dgate], axis=1), dgn, dgf, dwdp[:GLA_RANK], dbd, dgg, dsinks, loss_p]
    packed = [_rows8(a) for a in segs]
    offs = [0]
    for a in packed:
        offs.append(offs[-1] + a.shape[0])
    g_w_in_t, g_w_out, small = _epilogue(dw_in_t.reshape(4, in_cols, D_MODEL), dw_out.reshape(4, out_rows, D_MODEL),
                                         jnp.concatenate(packed, axis=0))

    def seg(i, size):
        return small[:, offs[i]:offs[i + 1]].reshape(8, -1)[:, :size]

    dmod_all = seg(0, 3 * D_MODEL)
    dwd_all = lax.dynamic_slice(seg(3, GLA_RANK * 256).reshape(8, GLA_RANK, 256), (0, 0, chip * 64), (8, GLA_RANK, 64))
    parts = [dmod_all.reshape(8, 1, 3 * D_MODEL), seg(1, D_MODEL).reshape(8, 1, D_MODEL), dwd_all,
             seg(4, 256).reshape(8, 1, 256), seg(5, 512).reshape(8, 1, 512), seg(6, SWA_HEADS).reshape(8, 1, SWA_HEADS),
             seg(2, D_MODEL).reshape(8, 1, D_MODEL), seg(7, LANES).reshape(8, 1, LANES)]
    smalls = _small_update(
        parts,
        [b_ada, g_norm, w_dec2, b_decay, g_gla_head, sinks, g_final.reshape(1, D_MODEL)],
        [m_b_ada, m_g_norm, m_w_decay[0], m_b_decay, m_g_gla_head, m_sinks, m_g_final.reshape(1, D_MODEL)],
        [v_b_ada, v_g_norm, v_w_decay[0], v_b_decay, v_g_gla_head, v_sinks, v_g_final.reshape(1, D_MODEL)])
    (g_b_ada, d_b_ada, nm_b_ada, nv_b_ada, g_gn, d_gn, nm_gn, nv_gn, g_wd, d_wd, nm_wd, nv_wd,
     g_bd, d_bd, nm_bd, nv_bd, g_gg, d_gg, nm_gg, nv_gg, g_sk, d_sk, nm_sk, nv_sk,
     g_gf, d_gf, nm_gf, nv_gf, loss_row) = smalls
    loss = loss_row[0, 0]

    dmod_cols = lax.dynamic_slice(dmod_all, (0, chip * ada_cols), (8, ada_cols))
    g_w_ada, d_w_ada, nm_w_ada, nv_w_ada = _ada_update(c_all, dmod_cols, w_ada2, m_w_ada[0], v_w_ada[0])
    to3 = lambda a: jnp.transpose(a, (2, 0, 1))
    from3 = lambda a: jnp.transpose(a, (1, 2, 0))[0]
    d3, nm3, nv3, g3 = _adamw_t(to3(w_in), g_w_in_t, to3(m_w_in), to3(v_w_in), "adamw_w_in")
    g_w_in, d_w_in, nm_w_in, nv_w_in = from3(g3), from3(d3), from3(nm3), from3(nv3)
    d_w_out, nm_w_out, nv_w_out = _adamw(w_out2, g_w_out, m_w_out[0], v_w_out[0], "adamw_w_out")

    flat = lambda a: a.reshape(D_MODEL)
    grads = [g_w_ada[None], g_b_ada, g_gn, g_w_in[None], g_wd[None], g_bd, g_gg, g_sk, g_w_out[None], flat(g_gf)]
    deltas = [d_w_ada[None], d_b_ada, d_gn, d_w_in[None], d_wd[None], d_bd, d_gg, d_sk, d_w_out[None], flat(d_gf)]
    new_m = [nm_w_ada[None], nm_b_ada, nm_gn, nm_w_in[None], nm_wd[None], nm_bd, nm_gg, nm_sk, nm_w_out[None], flat(nm_gf)]
    new_v = [nv_w_ada[None], nv_b_ada, nv_gn, nv_w_in[None], nv_wd[None], nv_bd, nv_gg, nv_sk, nv_w_out[None], flat(nv_gf)]
    return (loss, gx[None], *grads, *deltas, *new_m, *new_v)
```

```python
import jax
import jax.numpy as jnp
from jax import lax
from jax.experimental import pallas as pl
from jax.experimental.pallas import tpu as pltpu

F32 = jnp.float32
BF = jnp.bfloat16

D_MODEL = 1024
GLA_HEADS = 4
GLA_DK = 64
GLA_CHUNK = 64
GLA_RANK = 16
GLA_TAU = 16.0
GLA_SUB = 256
GLA_ROWS = 512
SWA_HEADS = 8
SWA_BLOCK = 128
SWA_QBLOCKS = 8
RMS_EPS = 1e-6
ROPE_THETA = 10000.0

OFF_QK, OFF_V, OFF_GZ, OFF_SQ, OFF_SZ, OFF_SK, OFF_SV, OFF_GA = 0, 512, 1024, 1536, 2048, 2560, 2688, 2816
D_PAD = 2944
D_IN = 2832
LANES = 128
VMEM_LIMIT = 56 * 1024 * 1024

ADAM_LR, ADAM_B1, ADAM_B2, ADAM_EPS, ADAM_WD, ADAM_STEP = 0.001, 0.9, 0.999, 1e-08, 0.01, 10

NT = (((1,), (1,)), ((), ()))
TN = (((0,), (0,)), ((), ()))
MESH = pl.DeviceIdType.MESH


def _dot(a, b, dims=None):
    if dims is None:
        return jnp.dot(a, b, preferred_element_type=F32)
    return lax.dot_general(a, b, dims, preferred_element_type=F32)


def _sigmoid(x):
    return 1.0 / (1.0 + jnp.exp(-x))


def _params(sem=None):
    return pltpu.CompilerParams(dimension_semantics=sem, vmem_limit_bytes=VMEM_LIMIT)


def _full(shape):
    return pl.BlockSpec(shape, lambda i: (0,) * len(shape))


def _subtiles(rows, size=256):
    size = min(size, rows)
    return [slice(k * size, (k + 1) * size) for k in range(rows // size)]


_GATHER_SEMS = [pltpu.SemaphoreType.DMA((7,)), pltpu.SemaphoreType.DMA((7,)), pltpu.SemaphoreType.DMA]


class _Gather:
    def __init__(self, x_ref, out_ref, send_sems, recv_sems, local_sem, slab=None):
        self.slab_of = slab
        x, y, c = lax.axis_index("x"), lax.axis_index("y"), lax.axis_index("c")
        self.me, self.sibling, self.c = (x, y, c), (x, y, 1 - c), c
        self.xn, self.yn, self.dg = (1 - x, y), (x, 1 - y), (1 - x, 1 - y)
        self.pass_from = (lax.rem(x + 1 - c, 2), lax.rem(y + c, 2))
        self.pass_to = (lax.rem(x + c, 2), lax.rem(y + 1 - c, 2))
        self.x_ref, self.out_ref, self.send_sems, self.recv_sems = x_ref, out_ref, send_sems, recv_sems
        self.mine = pltpu.make_async_copy(x_ref, self._slab(*self.me), local_sem)

    def _slab(self, px, py, pc):
        if self.slab_of is not None:
            return self.slab_of(self.out_ref, px, py, pc)
        return self.out_ref.at[4 * px + 2 * py + pc]

    def _copy(self, k, blk, to, src=None):
        return pltpu.make_async_remote_copy(
            src_ref=self._slab(*blk) if src is None else src, dst_ref=self._slab(*blk),
            send_sem=self.send_sems.at[k], recv_sem=self.recv_sems.at[k], device_id=to, device_id_type=MESH)

    def _sends(self):
        c = self.c
        return [self._copy(0, self.me, self.sibling, src=self.x_ref),
                self._copy(1, self.me, (*self.xn, c), src=self.x_ref),
                self._copy(2, self.me, (*self.yn, c), src=self.x_ref),
                self._copy(3, (*self.pass_from, c), (*self.pass_to, c)),
                self._copy(4, (*self.xn, c), self.sibling),
                self._copy(5, (*self.yn, c), self.sibling),
                self._copy(6, (*self.dg, c), self.sibling)]

    def start(self):
        self.mine.start()
        for cp in self._sends()[0:3]:
            cp.start()

    def pass_on(self):
        sends = self._sends()
        self._copy(1, (*self.xn, self.c), self.me).wait_recv()
        self._copy(2, (*self.yn, self.c), self.me).wait_recv()
        for k in (3, 4, 5):
            sends[k].start()

    def relay_diagonal(self):
        self._copy(3, (*self.dg, self.c), self.me).wait_recv()
        self._sends()[6].start()

    def relay(self):
        self.pass_on()
        self.relay_diagonal()

    def finish(self):
        c = self.c
        self._copy(0, self.sibling, self.me).wait_recv()
        for k, chip in ((4, self.xn), (5, self.yn), (6, self.dg)):
            self._copy(k, (*chip, 1 - c), self.me).wait_recv()
        for cp in self._sends():
            cp.wait_send()
        self.mine.wait()


def _prologue(cw, w_ada, b_shard, half_in, half_out, pos_col, inv_freq):
    s = pos_col.shape[0]
    rt = min(512, s)

    def body(cw_ref, wada_hbm, b_ref, hin_ref, hout_ref, pos_hbm, f_ref,
             first_ref, mod_ref, win_ref, wout_ref, cos_hbm, sin_hbm,
             mod_blk, cos_ref, sin_ref, wada_ref, pos_ref, table_sems, local_sems, *sems):
        fetch_w = pltpu.make_async_copy(wada_hbm, wada_ref, local_sems.at[0])
        fetch_p = pltpu.make_async_copy(pos_hbm, pos_ref, local_sems.at[1])
        fetch_w.start()
        fetch_p.start()
        g_c = _Gather(cw_ref, first_ref, *sems[0:3])
        half_lanes = hin_ref.shape[1]
        g_in = _Gather(hin_ref, win_ref, *sems[3:6],
                       slab=lambda ref, px, py, pc: ref.at[2 * px + py, :, pl.ds(pl.multiple_of(pc * half_lanes, half_lanes), half_lanes)])
        g_out = _Gather(hout_ref, wout_ref, *sems[6:9])
        g_mod = _Gather(mod_blk, mod_ref, *sems[9:12])
        g_c.start()
        g_in.start()
        g_out.start()
        g_c.relay()
        g_c.finish()
        c_rows = [jnp.concatenate([first_ref[d, r:r + 1, :] for r in range(8)], axis=1) for d in range(8)]
        c_all = jnp.concatenate(c_rows, axis=0)
        sc = (c_all * _sigmoid(c_all)).astype(BF)
        fetch_w.wait()
        mod_blk[...] = _dot(sc, wada_ref[...].astype(BF)) + b_ref[...]
        g_mod.start()
        fetch_p.wait()

        def rope_rows(i, carry):
            rows = pl.ds(pl.multiple_of(i * rt, rt), rt)
            ang = pos_ref[rows, :].astype(F32) * f_ref[...]
            lane = lax.broadcasted_iota(jnp.int32, ang.shape, 1)
            cos_ref[rows, :] = jnp.cos(ang)
            sn = jnp.sin(ang)
            sin_ref[rows, :] = jnp.where((lane % 64) < 32, -sn, sn)
            pltpu.make_async_copy(cos_ref.at[rows, :], cos_hbm.at[rows, :], table_sems.at[0]).start()
            pltpu.make_async_copy(sin_ref.at[rows, :], sin_hbm.at[rows, :], table_sems.at[1]).start()
            return carry

        steps = s // rt
        lax.fori_loop(0, steps // 2, rope_rows, 0)
        g_in.pass_on()
        g_out.pass_on()
        lax.fori_loop(steps // 2, steps, rope_rows, 0)
        g_in.relay_diagonal()
        g_out.relay_diagonal()
        g_mod.relay()
        g_in.finish()
        g_out.finish()
        g_mod.finish()
        pltpu.make_async_copy(cos_ref, cos_hbm, table_sems.at[0]).wait()
        pltpu.make_async_copy(sin_ref, sin_hbm, table_sems.at[1]).wait()

    vm = pl.BlockSpec(memory_space=pltpu.VMEM)
    hbm = pl.BlockSpec(memory_space=pl.ANY)
    return pl.pallas_call(
        body, name="prologue",
        out_shape=[jax.ShapeDtypeStruct((8,) + cw.shape, F32), jax.ShapeDtypeStruct((8, 8, w_ada.shape[1]), F32),
                   jax.ShapeDtypeStruct((4, half_in.shape[0], 2 * half_in.shape[1]), half_in.dtype),
                   jax.ShapeDtypeStruct((8,) + half_out.shape, half_out.dtype),
                   jax.ShapeDtypeStruct((s, LANES), F32), jax.ShapeDtypeStruct((s, LANES), F32)],
        in_specs=[vm, hbm, vm, hbm, hbm, hbm, vm], out_specs=[vm, vm, hbm, hbm, hbm, hbm],
        scratch_shapes=[pltpu.VMEM((8, w_ada.shape[1]), F32), pltpu.VMEM((s, LANES), F32), pltpu.VMEM((s, LANES), F32),
                        pltpu.VMEM(w_ada.shape, F32), pltpu.VMEM(pos_col.shape, jnp.int32),
                        pltpu.SemaphoreType.DMA((2,)), pltpu.SemaphoreType.DMA((2,))] + _GATHER_SEMS * 4,
        compiler_params=pltpu.CompilerParams(vmem_limit_bytes=VMEM_LIMIT),
    )(cw, w_ada, b_shard, half_in, half_out, pos_col, inv_freq)


def _reduce_scratch(rr, cc):
    c2 = cc // 2
    return [pltpu.VMEM((4, rr, c2), F32), pltpu.VMEM((4, rr, c2), F32), pltpu.VMEM((3, rr, c2), BF),
            pltpu.VMEM((2, rr, c2), BF), pltpu.VMEM((rr, c2), BF), pltpu.VMEM((rr, c2), F32),
            pltpu.SemaphoreType.DMA((8,)), pltpu.SemaphoreType.DMA((8,)), pltpu.SemaphoreType.DMA((5,))]


class _Reduce:
    def __init__(self, p_hbm, out_ref, acc_ref, own_ref, send_ref, land_ref, relay_ref, res_ref,
                 send_sems, recv_sems, local_sems):
        x, y, c = lax.axis_index("x"), lax.axis_index("y"), lax.axis_index("c")
        c2 = out_ref.shape[1] // 2
        sibling = (x, y, 1 - c)
        first = (lax.rem(x + 1 - c, 2), lax.rem(y + c, 2))
        second = (lax.rem(x + c, 2), lax.rem(y + 1 - c, 2))
        shards = [2 * first[0] + first[1], 2 * second[0] + second[1], 2 * (1 - x) + (1 - y), 2 * x + y]
        sibling_slot = (1, 0, 2, 3)
        mine = pl.ds(pl.multiple_of(c * c2, c2), c2)
        other = pl.ds(pl.multiple_of((1 - c) * c2, c2), c2)
        self.acc_ref, self.own_ref, self.send_ref, self.land_ref = acc_ref, own_ref, send_ref, land_ref
        self.relay_ref, self.res_ref = relay_ref, res_ref
        self.own = [pltpu.make_async_copy(p_hbm.at[j, :, mine], own_ref.at[k], local_sems.at[k])
                    for k, j in enumerate(shards)]
        self.swap_out = [pltpu.make_async_remote_copy(
            src_ref=p_hbm.at[j, :, other], dst_ref=acc_ref.at[sibling_slot[k]], send_sem=send_sems.at[k],
            recv_sem=recv_sems.at[sibling_slot[k]], device_id=sibling, device_id_type=MESH) for k, j in enumerate(shards)]
        self.swap_in = [pltpu.make_async_remote_copy(
            src_ref=p_hbm.at[j, :, other], dst_ref=acc_ref.at[k], send_sem=send_sems.at[k], recv_sem=recv_sems.at[k],
            device_id=sibling, device_id_type=MESH) for k, j in enumerate(shards)]

        def message(k, src, dst, to):
            return pltpu.make_async_remote_copy(src_ref=src, dst_ref=dst, send_sem=send_sems.at[k], recv_sem=recv_sems.at[k],
                                                device_id=(*to, c), device_id_type=MESH)

        self.direct = message(4, send_ref.at[0], land_ref.at[0], first)
        self.passed = message(5, send_ref.at[1], relay_ref, first)
        self.joint = message(6, send_ref.at[2], land_ref.at[1], second)
        self.put = pltpu.make_async_copy(res_ref, out_ref.at[:, mine], local_sems.at[4])
        self.share = pltpu.make_async_remote_copy(
            src_ref=res_ref, dst_ref=out_ref.at[:, mine], send_sem=send_sems.at[7],
            recv_sem=recv_sems.at[7], device_id=sibling, device_id_type=MESH)

    def start(self):
        for k in (0, 2, 1, 3):
            self.own[k].start()
            self.swap_out[k].start()

    def _combine(self, k):
        self.own[k].wait()
        self.swap_out[k].wait_send()
        self.swap_in[k].wait_recv()
        self.acc_ref[k] = self.acc_ref[k] + self.own_ref[k]

    def combine_and_send(self):
        dt = self.send_ref.dtype
        self._combine(0)
        self.send_ref[0] = self.acc_ref[0].astype(dt)
        self.direct.start()
        self._combine(2)
        self.send_ref[1] = self.acc_ref[2].astype(dt)
        self.passed.start()
        self._combine(1)
        self.passed.wait_recv()
        self.send_ref[2] = (self.acc_ref[1] + self.relay_ref[...].astype(F32)).astype(dt)
        self.joint.start()
        self._combine(3)

    def total_and_share(self):
        self.direct.wait_recv()
        self.joint.wait_recv()
        self.res_ref[...] = self.acc_ref[3] + self.land_ref[0].astype(F32) + self.land_ref[1].astype(F32)
        for cp in (self.direct, self.passed, self.joint):
            cp.wait_send()
        self.put.start()
        self.share.start()

    def finish(self):
        self.put.wait()
        self.share.wait()


def _epilogue(dw_in_parts, dw_out_parts, small):
    _, r_in, cc = dw_in_parts.shape
    _, r_out, _ = dw_out_parts.shape
    n_red = len(_reduce_scratch(r_in, cc))

    def body(pin_hbm, pout_hbm, small_ref, gin_ref, gout_ref, small_all_ref, *scratch):
        red_in = _Reduce(pin_hbm, gin_ref, *scratch[0:n_red])
        red_out = _Reduce(pout_hbm, gout_ref, *scratch[n_red:2 * n_red])
        gat = _Gather(small_ref, small_all_ref, *scratch[2 * n_red:])
        red_out.start()
        red_in.start()
        gat.start()
        red_out.combine_and_send()
        red_in.combine_and_send()
        gat.relay()
        red_out.total_and_share()
        red_in.total_and_share()
        gat.finish()
        red_out.finish()
        red_in.finish()

    vm = pl.BlockSpec(memory_space=pltpu.VMEM)
    anyspec = pl.BlockSpec(memory_space=pl.ANY)
    return pl.pallas_call(
        body, name="epilogue",
        out_shape=[jax.ShapeDtypeStruct((r_in, cc), F32), jax.ShapeDtypeStruct((r_out, cc), F32),
                   jax.ShapeDtypeStruct((8,) + small.shape, F32)],
        in_specs=[anyspec, anyspec, vm], out_specs=[anyspec, anyspec, vm],
        scratch_shapes=_reduce_scratch(r_in, cc) + _reduce_scratch(r_out, cc) + _GATHER_SEMS,
        compiler_params=pltpu.CompilerParams(vmem_limit_bytes=VMEM_LIMIT),
    )(dw_in_parts, dw_out_parts, small)


def _rope(t, cosb, sinb, first_half):
    partner = jnp.where(first_half, pltpu.roll(t, 96, 1), pltpu.roll(t, 32, 1))
    return t * cosb + partner * sinb


def _rope_t(g, cosb, sinb, first_half):
    gs = g * sinb
    partner = jnp.where(first_half, pltpu.roll(gs, 96, 1), pltpu.roll(gs, 32, 1))
    return g * cosb + partner


def _modnorm(x, g, sc1p, shift):
    r = lax.rsqrt(jnp.mean(x * x, axis=-1, keepdims=True) + RMS_EPS)
    xn = x * r
    return xn, r, (xn * g) * sc1p + shift


def _inproj_fwd(x2d, shift, sc1p, g_norm, wpad_t):
    s = x2d.shape[0]
    tm = min(512, s)

    def body(x_ref, sh_ref, sc_ref, g_ref, w_ref, o_ref):
        subs = _subtiles(tm)
        hs = [_modnorm(x_ref[sl, :], g_ref[...], sc_ref[...], sh_ref[...])[2].astype(BF) for sl in subs]
        for sl, h in zip(subs, hs):
            o_ref[sl, :] = _dot(h, w_ref[...], NT)

    vec = _full((1, D_MODEL))
    return pl.pallas_call(
        body, name="inproj_fwd", grid=(s // tm,),
        in_specs=[pl.BlockSpec((tm, D_MODEL), lambda i: (i, 0)), vec, vec, vec, _full((D_PAD, D_MODEL))],
        out_specs=pl.BlockSpec((tm, D_PAD), lambda i: (i, 0)),
        out_shape=jax.ShapeDtypeStruct((s, D_PAD), F32),
        compiler_params=_params(("arbitrary",)),
    )(x2d, shift, sc1p, g_norm, wpad_t)


def _split3(a):
    hi = a.astype(BF)
    r1 = a - hi.astype(F32)
    mid = r1.astype(BF)
    lo = (r1 - mid.astype(F32)).astype(BF)
    return hi, mid, lo


def _tri_matmul(tri, a):
    hi, mid, lo = _split3(a)
    return _dot(tri, hi) + _dot(tri, mid) + _dot(tri, lo)


def _chunks(tb):
    return [slice(c * GLA_CHUNK, (c + 1) * GLA_CHUNK) for c in range(tb // GLA_CHUNK)]


def _per_chunk_rows(rows, width):
    return jnp.concatenate([jnp.broadcast_to(r, (GLA_CHUNK, width)) for r in rows], axis=0)


def _gla_triangle(tb):
    row = lax.broadcasted_iota(jnp.int32, (tb, tb), 0)
    col = lax.broadcasted_iota(jnp.int32, (tb, tb), 1)
    return (((row // GLA_CHUNK) == (col // GLA_CHUNK)) & (col <= row)).astype(F32)


def _lane_mean(x, ones_b):
    hi = x.astype(BF)
    lo = (x - hi.astype(F32)).astype(BF)
    return (_dot(hi, ones_b) + _dot(lo, ones_b)) * (1.0 / LANES)


def _head(t, h, lo_h):
    blk = t[:, LANES * (h // 2):LANES * (h // 2 + 1)]
    return jnp.where(lo_h, blk, 0.0) if h % 2 == 0 else jnp.where(lo_h, 0.0, blk)


def _gla_block_common(qk, ga, wd, bd, tril_b):
    tb = qk.shape[0]
    q, k = qk[:, :256], qk[:, 256:]
    z = _dot(ga.astype(BF), wd) + bd
    la = (jnp.minimum(z, 0.0) - jnp.log(1.0 + jnp.exp(-jnp.abs(z)))) * (1.0 / GLA_TAU)
    b = _tri_matmul(tril_b, la)
    bls = [b[rs.stop - 1:rs.stop, :] for rs in _chunks(tb)]
    eq = jnp.exp(b)
    ek = jnp.exp(-b)
    f = jnp.exp(_per_chunk_rows(bls, 256) - b)
    return z, eq, ek, f, q * (eq * GLA_DK ** -0.5), k * ek, k * f, bls


def _gla_units(s):
    sub = min(GLA_SUB, s)
    tb = min(GLA_ROWS, s)
    subs = [slice(i * sub, (i + 1) * sub) for i in range(tb // sub)]
    units = [(i, h) for i in range(len(subs)) for h in range(GLA_HEADS)]
    return tb, sub, subs, units


def _gla_fwd(proj, wdecp, bdec, ggla):
    s = proj.shape[0]
    tb, sub, subs, units = _gla_units(s)
    nch = sub // GLA_CHUNK

    def body(qk_ref, v_ref, gz_ref, ga_ref, wd_ref, bd_ref, gg_ref, tri_ref, og_ref, opre_ref, sprev_ref, st_ref):
        @pl.when(pl.program_id(0) == 0)
        def _():
            st_ref[...] = jnp.zeros_like(st_ref)

        lo_h = lax.broadcasted_iota(jnp.int32, (sub, LANES), 1) < GLA_DK
        tril = tri_ref[...] > 0.5
        tril_b = tri_ref[...].astype(BF)
        ones_b = jnp.ones((LANES, LANES), BF)
        gg, wd, bd = gg_ref[...], wd_ref[...], bd_ref[...]
        chunks = _chunks(sub)
        lanes = [slice(h * LANES, (h + 1) * LANES) for h in range(GLA_HEADS)]
        com = [_gla_block_common(qk_ref[sl, :], ga_ref[sl, :], wd, bd, tril_b) for sl in subs]
        decs = [[jnp.exp(bl) for bl in cm[7]] for cm in com]
        a = {(i, h): _head(com[i][4], h, lo_h).astype(BF) for i, h in units}
        bm = {(i, h): _head(com[i][5], h, lo_h).astype(BF) for i, h in units}
        ktl = {(i, h): _head(com[i][6], h, lo_h).astype(BF) for i, h in units}
        vh = {(i, h): v_ref[subs[i], lanes[h]].astype(BF) for i, h in units}
        sc = {u: _dot(a[u], bm[u], NT) for u in units}
        upd = {u: [_dot(vh[u][rs], ktl[u][rs], TN) for rs in chunks] for u in units}
        p = {u: jnp.where(tril, sc[u], 0.0).astype(BF) for u in units}
        o = {u: _dot(p[u], vh[u]) for u in units}
        states = {}
        for h in range(GLA_HEADS):
            st = st_ref[h]
            for i in range(len(subs)):
                entering = []
                for c in range(nch):
                    entering.append(st)
                    sprev_ref[i * nch + c, h] = st
                    st = st * decs[i][c][:, LANES * (h // 2):LANES * (h // 2 + 1)] + upd[(i, h)][c]
                states[(i, h)] = entering
            st_ref[h] = st
        inter = {u: [_dot(a[u][rs], states[u][c].astype(BF), NT) for c, rs in enumerate(chunks)] for u in units}
        o = {u: o[u] + jnp.concatenate(inter[u], axis=0) for u in units}
        ms = {u: _lane_mean(o[u] * o[u], ones_b) for u in units}
        for i, h in units:
            gzh = gz_ref[subs[i], lanes[h]]
            opre_ref[subs[i], lanes[h]] = o[(i, h)]
            og_ref[subs[i], lanes[h]] = (((o[(i, h)] * lax.rsqrt(ms[(i, h)] + RMS_EPS)) * gg[:, lanes[h]])
                                         * (gzh * _sigmoid(gzh))).astype(og_ref.dtype)

    def col(width, off):
        return pl.BlockSpec((tb, width), lambda i: (i, off // width))

    return pl.pallas_call(
        body, name="gla_fwd", grid=(s // tb,),
        in_specs=[col(512, OFF_QK), col(512, OFF_V), col(512, OFF_GZ), col(LANES, OFF_GA),
                  _full((LANES, 256)), _full((1, 256)), _full((1, 512)), _full((sub, sub))],
        out_specs=[pl.BlockSpec((tb, 512), lambda i: (i, 0)), pl.BlockSpec((tb, 512), lambda i: (i, 0)),
                   pl.BlockSpec((tb // GLA_CHUNK, GLA_HEADS, LANES, LANES), lambda i: (i, 0, 0, 0))],
        out_shape=[jax.ShapeDtypeStruct((s, 512), BF), jax.ShapeDtypeStruct((s, 512), F32),
                   jax.ShapeDtypeStruct((s // GLA_CHUNK, GLA_HEADS, LANES, LANES), F32)],
        scratch_shapes=[pltpu.VMEM((GLA_HEADS, LANES, LANES), F32)],
        compiler_params=_params(("arbitrary",)),
    )(proj, proj, proj, proj, wdecp, bdec, ggla, _gla_triangle(sub))


def _gla_bwd(proj, dog, opre, sprev, wdecp, bdec, ggla):
    s = proj.shape[0]
    tb, sub, subs, units = _gla_units(s)
    nsub = len(subs)
    nch = sub // GLA_CHUNK
    nb = s // tb

    def body(qk_ref, v_ref, gz_ref, ga_ref, dog_ref, opre_ref, sprev_ref, wd_ref, bd_ref, gg_ref, tri_ref, triu_ref,
             dqk_ref, dv_ref, dgz_ref, dga_ref, dwd_ref, dbd_ref, dgg_ref, dst_ref):
        @pl.when(pl.program_id(0) == 0)
        def _():
            dst_ref[...] = jnp.zeros_like(dst_ref)
            dwd_ref[...] = jnp.zeros_like(dwd_ref)
            dbd_ref[...] = jnp.zeros_like(dbd_ref)
            dgg_ref[...] = jnp.zeros_like(dgg_ref)

        lo_h = lax.broadcasted_iota(jnp.int32, (sub, LANES), 1) < GLA_DK
        tril = tri_ref[...] > 0.5
        tril_b = tri_ref[...].astype(BF)
        triu_b = triu_ref[...].astype(BF)
        ones_b = jnp.ones((LANES, LANES), BF)
        last_row = (lax.broadcasted_iota(jnp.int32, (sub, LANES), 0) % GLA_CHUNK) == GLA_CHUNK - 1
        wd, gg, bd = wd_ref[...], gg_ref[...], bd_ref[...]
        chunks = _chunks(sub)
        lanes = [slice(h * LANES, (h + 1) * LANES) for h in range(GLA_HEADS)]
        blks = [slice(LANES * (h // 2), LANES * (h // 2 + 1)) for h in range(GLA_HEADS)]
        ga = [ga_ref[sl, :] for sl in subs]
        com = [_gla_block_common(qk_ref[sl, :], ga[i], wd, bd, tril_b) for i, sl in enumerate(subs)]
        decs = [[jnp.exp(bl) for bl in cm[7]] for cm in com]
        a = {(i, h): _head(com[i][4], h, lo_h).astype(BF) for i, h in units}
        bm = {(i, h): _head(com[i][5], h, lo_h).astype(BF) for i, h in units}
        ktl = {(i, h): _head(com[i][6], h, lo_h).astype(BF) for i, h in units}
        vh = {(i, h): v_ref[subs[i], lanes[h]].astype(BF) for i, h in units}
        sc = {u: _dot(a[u], bm[u], NT) for u in units}

        o = {(i, h): opre_ref[subs[i], lanes[h]] for i, h in units}
        ms = {u: _lane_mean(o[u] * o[u], ones_b) for u in units}
        gz = {(i, h): gz_ref[subs[i], lanes[h]] for i, h in units}
        dog = {(i, h): dog_ref[subs[i], lanes[h]] for i, h in units}
        sg = {u: _sigmoid(gz[u]) for u in units}
        r = {u: lax.rsqrt(ms[u] + RMS_EPS) for u in units}
        ohat = {u: o[u] * r[u] for u in units}
        sil = {u: gz[u] * sg[u] for u in units}
        for i, h in units:
            u = (i, h)
            dgz_ref[subs[i], lanes[h]] = (dog[u] * (ohat[u] * gg[:, lanes[h]])
                                          * (sg[u] * (1.0 + gz[u] * (1.0 - sg[u])))).astype(dgz_ref.dtype)
            dgg_ref[:, lanes[h]] += jnp.sum(dog[u] * sil[u] * ohat[u], axis=0, keepdims=True)
        dn = {(i, h): dog[(i, h)] * sil[(i, h)] * gg[:, lanes[h]] for i, h in units}
        mdn = {u: _lane_mean(dn[u] * ohat[u], ones_b) for u in units}
        do = {u: (r[u] * (dn[u] - ohat[u] * mdn[u])).astype(BF) for u in units}

        p = {u: jnp.where(tril, sc[u], 0.0).astype(BF) for u in units}
        dpr = {u: _dot(do[u], vh[u], NT) for u in units}
        incr = {u: [_dot(do[u][rs], a[u][rs], TN) for rs in chunks] for u in units}
        dv = {u: _dot(p[u], do[u], TN) for u in units}
        dp = {u: jnp.where(tril, dpr[u], 0.0).astype(BF) for u in units}
        dqd = {u: _dot(dp[u], bm[u]) for u in units}
        dkd = {u: _dot(dp[u], a[u], TN) for u in units}
        st = {(i, h): [sprev_ref[i * nch + c, h] for c in range(nch)] for i, h in units}
        leaving = {}
        for h in range(GLA_HEADS):
            d = dst_ref[h]
            for i in reversed(range(nsub)):
                out = [None] * nch
                for c in reversed(range(nch)):
                    out[c] = d
                    d = d * decs[i][c][:, blks[h]] + incr[(i, h)][c]
                leaving[(i, h)] = out
            dst_ref[h] = d
        lv_b = {u: [leaving[u][c].astype(BF) for c in range(nch)] for u in units}
        dv_s = {u: [_dot(ktl[u][rs], lv_b[u][c], NT) for c, rs in enumerate(chunks)] for u in units}
        dqd_s = {u: [_dot(do[u][rs], st[u][c].astype(BF)) for c, rs in enumerate(chunks)] for u in units}
        dkt_s = {u: [_dot(vh[u][rs], lv_b[u][c]) for c, rs in enumerate(chunks)] for u in units}
        ddec = {u: [jnp.sum(leaving[u][c] * st[u][c], axis=0, keepdims=True) for c in range(nch)] for u in units}
        for i, h in units:
            dv_ref[subs[i], lanes[h]] = (dv[(i, h)] + jnp.concatenate(dv_s[(i, h)], axis=0)).astype(dv_ref.dtype)
        dqd = {u: dqd[u] + jnp.concatenate(dqd_s[u], axis=0) for u in units}
        dkt = {u: jnp.concatenate(dkt_s[u], axis=0) for u in units}

        db = []
        for i, sl in enumerate(subs):
            _, eq, ek, f, qd, kd, kt, _ = com[i]
            parts = []
            for pair in range(GLA_HEADS // 2):
                blk, u0, u1 = blks[2 * pair], (i, 2 * pair), (i, 2 * pair + 1)
                dqd_b, dkd_b, dkt_b = dqd[u0] + dqd[u1], dkd[u0] + dkd[u1], dkt[u0] + dkt[u1]
                dqk_ref[sl, blk] = (dqd_b * (eq[:, blk] * GLA_DK ** -0.5)).astype(dqk_ref.dtype)
                dqk_ref[sl, 256 + LANES * pair:256 + LANES * (pair + 1)] = (dkd_b * ek[:, blk] + dkt_b * f[:, blk]).astype(dqk_ref.dtype)
                dkt_kt = dkt_b * kt[:, blk]
                dbp = dqd_b * qd[:, blk] - dkd_b * kd[:, blk] - dkt_kt
                dbl = [jnp.sum(dkt_kt[rs], axis=0, keepdims=True) + (ddec[u0][c] + ddec[u1][c]) * decs[i][c][:, blk]
                       for c, rs in enumerate(chunks)]
                parts.append(jnp.where(last_row, dbp + _per_chunk_rows(dbl, LANES), dbp))
            db.append(jnp.concatenate(parts, axis=1))
        dla = [_tri_matmul(triu_b, db[i]) for i in range(nsub)]
        dz32 = [dla[i] * (1.0 / GLA_TAU) * _sigmoid(-com[i][0]) for i in range(nsub)]
        dz = [t.astype(BF) for t in dz32]
        for i, sl in enumerate(subs):
            dga_ref[sl, :] = _dot(dz[i], wd, NT).astype(dga_ref.dtype)
            dwd_ref[...] += _dot(ga[i].astype(BF), dz[i], TN)
            dbd_ref[...] += jnp.sum(dz32[i], axis=0, keepdims=True)

    def col(width, off):
        return pl.BlockSpec((tb, width), lambda i: (nb - 1 - i, off // width))

    def rev(width):
        return pl.BlockSpec((tb, width), lambda i: (nb - 1 - i, 0))

    return pl.pallas_call(
        body, name="gla_bwd", grid=(nb,),
        in_specs=[col(512, OFF_QK), col(512, OFF_V), col(512, OFF_GZ), col(LANES, OFF_GA), rev(512), rev(512),
                  pl.BlockSpec((tb // GLA_CHUNK, GLA_HEADS, LANES, LANES), lambda i: (nb - 1 - i, 0, 0, 0)),
                  _full((LANES, 256)), _full((1, 256)), _full((1, 512)), _full((sub, sub)), _full((sub, sub))],
        out_specs=[rev(512), rev(512), rev(512), rev(LANES), _full((LANES, 256)), _full((1, 256)), _full((1, 512))],
        out_shape=[jax.ShapeDtypeStruct((s, 512), BF), jax.ShapeDtypeStruct((s, 512), BF),
                   jax.ShapeDtypeStruct((s, 512), BF), jax.ShapeDtypeStruct((s, LANES), BF),
                   jax.ShapeDtypeStruct((LANES, 256), F32), jax.ShapeDtypeStruct((1, 256), F32),
                   jax.ShapeDtypeStruct((1, 512), F32)],
        scratch_shapes=[pltpu.VMEM((GLA_HEADS, LANES, LANES), F32)],
        compiler_params=_params(("arbitrary",)),
    )(proj, proj, proj, proj, dog, opre, sprev, wdecp, bdec, ggla, _gla_triangle(sub), _gla_triangle(sub).T)


_SWA_COL_HEADS = (0, 2, 1, 3, 4, 6, 5, 7)
_SWA_COLS = SWA_HEADS * SWA_BLOCK


def _swa_masks():
    lo2 = lax.broadcasted_iota(jnp.int32, (2 * SWA_BLOCK, LANES), 1) < 64
    lane1 = lax.broadcasted_iota(jnp.int32, (SWA_BLOCK, LANES), 1)
    first_half = (lane1 % 64) < 32
    key = lax.broadcasted_iota(jnp.int32, (SWA_BLOCK, _SWA_COLS), 0)
    query = lax.broadcasted_iota(jnp.int32, (SWA_BLOCK, _SWA_COLS), 1) % SWA_BLOCK
    return lo2, lane1 < 64, first_half, key > query


def _merge_band(t, prev_mask, prev_bias=None):
    prev = t[:SWA_BLOCK] if prev_bias is None else t[:SWA_BLOCK] + prev_bias
    return jnp.where(prev_mask, prev, t[SWA_BLOCK:])


def _split_band(t, prev_mask_b):
    prev = t * prev_mask_b
    return jnp.concatenate([prev, t - prev], axis=0)


def _kv_variants(t, lo2):
    tr = pltpu.roll(t, 64, 1)
    lo_v = [jnp.where(lo2, t, 0.0).astype(BF), jnp.where(lo2, tr, 0.0).astype(BF)]
    hi_v = [jnp.where(lo2, 0.0, tr).astype(BF), jnp.where(lo2, 0.0, t).astype(BF)]
    return lo_v, hi_v


def _kv_variants_t(t):
    tt = t.T
    sw = jnp.concatenate([tt[64:], tt[:64]], axis=0)
    top = lax.broadcasted_iota(jnp.int32, tt.shape, 0) < 64
    lo_v = [jnp.where(top, tt, 0.0).astype(BF), jnp.where(top, sw, 0.0).astype(BF)]
    hi_v = [jnp.where(top, 0.0, sw).astype(BF), jnp.where(top, 0.0, tt).astype(BF)]
    return lo_v, hi_v


def _swa_scores(qg, k_lo, k_hi):
    return jnp.concatenate([_dot(k_lo[0], qg[0], NT), _dot(k_hi[0], qg[0], NT),
                            _dot(k_lo[1], qg[1], NT), _dot(k_hi[1], qg[1], NT)], axis=1)


def _sink_row(sinks_ref):
    return jnp.concatenate([jnp.full((1, SWA_BLOCK), sinks_ref[0, hd], F32) for hd in _SWA_COL_HEADS], axis=1)


def _swa_softmax(st, prev_mask, prev_bias, sink):
    st = _merge_band(st, prev_mask, prev_bias)
    m = jnp.maximum(jnp.max(st, axis=0, keepdims=True), sink)
    ex = jnp.exp(st - m)
    es = jnp.exp(sink - m)
    inv = 1.0 / (jnp.sum(ex, axis=0, keepdims=True) + es)
    return ex, es, inv


def _no_prev_bias(block_index):
    return jnp.where(block_index > 0, 0.0, -1e30).astype(F32)


def _swa_queries(sq_ref, rows, cosb, sinb, first_half):
    qs = [_rope(sq_ref[rows, p * LANES:(p + 1) * LANES], cosb, sinb, first_half) * 0.125 for p in range(4)]
    return [jnp.concatenate(qs[0:2], axis=0), jnp.concatenate(qs[2:4], axis=0)]


def _swa_fwd(proj, cos, sin, sinks):
    s = proj.shape[0]
    nq = min(SWA_QBLOCKS, s // SWA_BLOCK)
    tq = nq * SWA_BLOCK

    def body(sq_ref, sz_ref, sk_ref, sv_ref, cos_ref, sin_ref, sinks_ref, os_ref, opre_ref, kprev, vprev):
        n = pl.program_id(0)

        @pl.when(n == 0)
        def _():
            kprev[...] = jnp.zeros_like(kprev)
            vprev[...] = jnp.zeros_like(vprev)

        lo2, _, first_half, prev_mask = _swa_masks()
        prev_mask_b = jnp.where(prev_mask, 1.0, 0.0).astype(BF)
        sink = _sink_row(sinks_ref)
        blocks = range(nq)
        rows = [slice(j * SWA_BLOCK, (j + 1) * SWA_BLOCK) for j in blocks]
        cosb = [cos_ref[rows[j], :] for j in blocks]
        sinb = [sin_ref[rows[j], :] for j in blocks]
        kc = [_rope(sk_ref[rows[j], :], cosb[j], sinb[j], first_half) for j in blocks]
        vc = [sv_ref[rows[j], :] for j in blocks]
        kcat = [jnp.concatenate([kprev[...] if j == 0 else kc[j - 1], kc[j]], axis=0) for j in blocks]
        vcat = [jnp.concatenate([vprev[...] if j == 0 else vc[j - 1], vc[j]], axis=0) for j in blocks]
        kprev[...] = kc[-1]
        vprev[...] = vc[-1]
        kvar = [_kv_variants(kcat[j], lo2) for j in blocks]
        vtvar = [_kv_variants_t(vcat[j]) for j in blocks]
        qg = [[q.astype(BF) for q in _swa_queries(sq_ref, rows[j], cosb[j], sinb[j], first_half)] for j in blocks]
        st = [_swa_scores(qg[j], *kvar[j]) for j in blocks]
        soft = [_swa_softmax(st[j], prev_mask, _no_prev_bias(n) if j == 0 else None, sink) for j in blocks]
        pt = [_split_band(soft[j][0].astype(BF), prev_mask_b) for j in blocks]
        og = {}
        for j in blocks:
            inv = soft[j][2]
            for g in range(2):
                c0, c1, c2 = 512 * g, 512 * g + 256, 512 * g + 512
                ot = (_dot(vtvar[j][0][g], pt[j][:, c0:c1]) * inv[:, c0:c1]
                      + _dot(vtvar[j][1][g], pt[j][:, c1:c2]) * inv[:, c1:c2])
                og[(j, g)] = ot.T
        for j in blocks:
            for g in range(2):
                for i in range(2):
                    ls = slice((2 * g + i) * LANES, (2 * g + i + 1) * LANES)
                    o = og[(j, g)][i * SWA_BLOCK:(i + 1) * SWA_BLOCK]
                    sz = sz_ref[rows[j], ls]
                    opre_ref[rows[j], ls] = o
                    os_ref[rows[j], ls] = (o * (sz * _sigmoid(sz))).astype(os_ref.dtype)

    def col(width, off):
        return pl.BlockSpec((tq, width), lambda i: (i, off // width))

    row = pl.BlockSpec((tq, LANES), lambda i: (i, 0))
    return pl.pallas_call(
        body, name="swa_fwd", grid=(s // tq,),
        in_specs=[col(512, OFF_SQ), col(512, OFF_SZ), col(LANES, OFF_SK), col(LANES, OFF_SV), row, row,
                  pl.BlockSpec(memory_space=pltpu.SMEM)],
        out_specs=[pl.BlockSpec((tq, 512), lambda i: (i, 0))] * 2,
        out_shape=[jax.ShapeDtypeStruct((s, 512), BF), jax.ShapeDtypeStruct((s, 512), F32)],
        scratch_shapes=[pltpu.VMEM((SWA_BLOCK, LANES), F32)] * 2,
        compiler_params=_params(("arbitrary",)),
    )(proj, proj, proj, proj, cos, sin, sinks)


def _swa_bwd(proj, dos, opre, cos, sin, sinks):
    s = proj.shape[0]
    nq = min(SWA_QBLOCKS, s // SWA_BLOCK)
    tq = nq * SWA_BLOCK

    def body(sq_ref, sz_ref, sk_ref, sv_ref, dos_ref, opre_ref, cos_ref, sin_ref, sinks_ref,
             dsq_ref, dsz_ref, dsk_ref, dsv_ref, dsink_ref, kprev, vprev, cprev, sprev):
        n = pl.program_id(0)

        @pl.when(n == 0)
        def _():
            kprev[...] = jnp.zeros_like(kprev)
            vprev[...] = jnp.zeros_like(vprev)
            cprev[...] = jnp.zeros_like(cprev)
            sprev[...] = jnp.zeros_like(sprev)
            for hd in range(SWA_HEADS):
                dsink_ref[0, hd] = 0.0

        lo2, lo1, first_half, prev_mask = _swa_masks()
        prev_mask_b = jnp.where(prev_mask, 1.0, 0.0).astype(BF)
        lo1s = jnp.concatenate([lo1, lo1], axis=0)
        sink = _sink_row(sinks_ref)

        def home(m0, m1):
            t0 = m0 + pltpu.roll(m0, 64, 1)
            t1 = m1 + pltpu.roll(m1, 64, 1)
            return jnp.where(lo2, t0, t1)

        kp, vp, cp_, sp_ = kprev[...], vprev[...], cprev[...], sprev[...]
        for j in range(nq):
            rows = slice(j * SWA_BLOCK, (j + 1) * SWA_BLOCK)
            blk = n * nq + j
            cosb, sinb = cos_ref[rows, :], sin_ref[rows, :]
            kc = _rope(sk_ref[rows, :], cosb, sinb, first_half)
            vc = sv_ref[rows, :]
            kcat = jnp.concatenate([kp, kc], axis=0)
            k_lo, k_hi = _kv_variants(kcat, lo2)
            kt_lo, kt_hi = _kv_variants_t(kcat)
            v_lo, v_hi = _kv_variants(jnp.concatenate([vp, vc], axis=0), lo2)
            qg32 = _swa_queries(sq_ref, rows, cosb, sinb, first_half)
            qg = [q.astype(BF) for q in qg32]
            ex, es, inv = _swa_softmax(_swa_scores(qg, k_lo, k_hi), prev_mask, _no_prev_bias(n) if j == 0 else None, sink)
            pr, ps = ex * inv, es * inv

            dog32 = []
            for g in range(2):
                parts = []
                for i in range(2):
                    ls = slice((2 * g + i) * LANES, (2 * g + i + 1) * LANES)
                    sz = sz_ref[rows, ls]
                    sg = _sigmoid(sz)
                    dos_p = dos_ref[rows, ls]
                    dsz_ref[rows, ls] = (dos_p * opre_ref[rows, ls] * (sg * (1.0 + sz * (1.0 - sg)))).astype(dsz_ref.dtype)
                    parts.append(dos_p * (sz * sg))
                dog32.append(jnp.concatenate(parts, axis=0))
            dog = [t.astype(BF) for t in dog32]
            dpr = _merge_band(jnp.concatenate([_dot(v_lo[0], dog[0], NT), _dot(v_hi[0], dog[0], NT),
                                               _dot(v_lo[1], dog[1], NT), _dot(v_hi[1], dog[1], NT)], axis=1), prev_mask)
            rd = jnp.sum(pr * dpr, axis=0, keepdims=True)
            ds = _split_band((pr * (dpr - rd)).astype(BF), prev_mask_b)
            prb = _split_band(pr.astype(BF), prev_mask_b)
            sink_term = ps * rd
            for r, hd in enumerate(_SWA_COL_HEADS):
                dsink_ref[0, hd] += -jnp.sum(sink_term[:, r * SWA_BLOCK:(r + 1) * SWA_BLOCK])

            dk_g, dv_g = [], []
            for g in range(2):
                c0, c1, c2 = 512 * g, 512 * g + 256, 512 * g + 512
                dq = (_dot(kt_lo[g], ds[:, c0:c1]) + _dot(kt_hi[g], ds[:, c1:c2])).T
                for i in range(2):
                    ls = slice((2 * g + i) * LANES, (2 * g + i + 1) * LANES)
                    dsq_ref[rows, ls] = _rope_t(dq[i * SWA_BLOCK:(i + 1) * SWA_BLOCK] * 0.125, cosb, sinb,
                                                first_half).astype(dsq_ref.dtype)
                q_split = jnp.concatenate([jnp.where(lo1s, qg32[g], 0.0), jnp.where(lo1s, 0.0, qg32[g])], axis=0).astype(BF)
                do_split = jnp.concatenate([jnp.where(lo1s, dog32[g], 0.0), jnp.where(lo1s, 0.0, dog32[g])], axis=0).astype(BF)
                dk_g.append(_dot(ds[:, c0:c2], q_split))
                dv_g.append(_dot(prb[:, c0:c2], do_split))
            dk = home(dk_g[0], dk_g[1])
            dv = home(dv_g[0], dv_g[1])
            cur = pl.ds(pl.multiple_of(blk * SWA_BLOCK, SWA_BLOCK), SWA_BLOCK)
            dsk_ref[cur, :] = _rope_t(dk[SWA_BLOCK:], cosb, sinb, first_half)
            dsv_ref[cur, :] = dv[SWA_BLOCK:]
            dk_prev = _rope_t(dk[:SWA_BLOCK], cp_, sp_, first_half)
            dv_prev = dv[:SWA_BLOCK]
            if j == 0:
                @pl.when(n > 0)
                def _():
                    prv = pl.ds(pl.multiple_of((blk - 1) * SWA_BLOCK, SWA_BLOCK), SWA_BLOCK)
                    dsk_ref[prv, :] += dk_prev
                    dsv_ref[prv, :] += dv_prev
            else:
                prv = pl.ds(pl.multiple_of((blk - 1) * SWA_BLOCK, SWA_BLOCK), SWA_BLOCK)
                dsk_ref[prv, :] += dk_prev
                dsv_ref[prv, :] += dv_prev
            kp, vp, cp_, sp_ = kc, vc, cosb, sinb
        kprev[...] = kp
        vprev[...] = vp
        cprev[...] = cp_
        sprev[...] = sp_

    def col(width, off):
        return pl.BlockSpec((tq, width), lambda i: (i, off // width))

    row = pl.BlockSpec((tq, LANES), lambda i: (i, 0))
    wide = pl.BlockSpec((tq, 512), lambda i: (i, 0))
    return pl.pallas_call(
        body, name="swa_bwd", grid=(s // tq,),
        in_specs=[col(512, OFF_SQ), col(512, OFF_SZ), col(LANES, OFF_SK), col(LANES, OFF_SV), wide, wide, row, row,
                  pl.BlockSpec(memory_space=pltpu.SMEM)],
        out_specs=[wide, wide, _full((s, LANES)), _full((s, LANES)), pl.BlockSpec(memory_space=pltpu.SMEM)],
        out_shape=[jax.ShapeDtypeStruct((s, 512), BF), jax.ShapeDtypeStruct((s, 512), BF),
                   jax.ShapeDtypeStruct((s, LANES), F32), jax.ShapeDtypeStruct((s, LANES), F32),
                   jax.ShapeDtypeStruct((1, SWA_HEADS), F32)],
        scratch_shapes=[pltpu.VMEM((SWA_BLOCK, LANES), F32)] * 4,
        compiler_params=_params(("arbitrary",)),
    )(proj, proj, proj, proj, dos, opre, cos, sin, sinks)


def _outproj(og, osw, w_out, x2d, target, gate, g_final):
    s = x2d.shape[0]
    tm = min(512, s)

    def body(og_ref, os_ref, w_ref, x_ref, t_ref, gate_ref, gf_ref,
             dx2_ref, dog_ref, dos_ref, dw_ref, loss_ref, dgf_ref, dgate_ref):
        @pl.when(pl.program_id(0) == 0)
        def _():
            dw_ref[...] = jnp.zeros_like(dw_ref)
            loss_ref[...] = jnp.zeros_like(loss_ref)
            dgf_ref[...] = jnp.zeros_like(dgf_ref)
            dgate_ref[...] = jnp.zeros_like(dgate_ref)

        w = w_ref[...]
        gate, gf = gate_ref[...], gf_ref[...]
        subs = _subtiles(tm)
        ogv = [og_ref[sl, :] for sl in subs]
        osv = [os_ref[sl, :] for sl in subs]
        y = [_dot(ogv[k], w[:512]) + _dot(osv[k], w[512:]) for k in range(len(subs))]
        dys = []
        for k, sl in enumerate(subs):
            x2 = x_ref[sl, :] + gate * y[k]
            r = lax.rsqrt(jnp.mean(x2 * x2, axis=-1, keepdims=True) + RMS_EPS)
            xn = x2 * r
            err = xn * gf - t_ref[sl, :]
            loss_ref[...] += 0.5 * jnp.sum(jnp.mean(err * err, axis=-1, keepdims=True), axis=0, keepdims=True)
            dyf = err * (1.0 / D_MODEL)
            dgf_ref[...] += jnp.sum(dyf * xn, axis=0, keepdims=True)
            t = dyf * gf
            dx2 = r * (t - xn * jnp.mean(t * xn, axis=-1, keepdims=True))
            dx2_ref[sl, :] = dx2
            dgate_ref[...] += jnp.sum(dx2 * y[k], axis=0, keepdims=True)
            dys.append((dx2 * gate).astype(BF))
            dmix = _dot(dys[k], w, NT)
            dog_ref[sl, :] = dmix[:, :512]
            dos_ref[sl, :] = dmix[:, 512:]
        dy = jnp.concatenate(dys, axis=0)
        dw_ref[:512, :] += _dot(og_ref[...], dy, TN)
        dw_ref[512:, :] += _dot(os_ref[...], dy, TN)

    half = pl.BlockSpec((tm, 512), lambda i: (i, 0))
    rowb = pl.BlockSpec((tm, D_MODEL), lambda i: (i, 0))
    vec = _full((1, D_MODEL))
    return pl.pallas_call(
        body, name="outproj", grid=(s // tm,),
        in_specs=[half, half, _full((D_MODEL, D_MODEL)), rowb, rowb, vec, vec],
        out_specs=[rowb, half, half, _full((D_MODEL, D_MODEL)), _full((1, 1)), vec, vec],
        out_shape=[jax.ShapeDtypeStruct((s, D_MODEL), F32), jax.ShapeDtypeStruct((s, 512), F32),
                   jax.ShapeDtypeStruct((s, 512), F32), jax.ShapeDtypeStruct((D_MODEL, D_MODEL), F32),
                   jax.ShapeDtypeStruct((1, 1), F32), jax.ShapeDtypeStruct((1, D_MODEL), F32),
                   jax.ShapeDtypeStruct((1, D_MODEL), F32)],
        compiler_params=_params(("arbitrary",)),
    )(og, osw, w_out, x2d, target, gate, g_final)


_PIECES = ((OFF_QK, 512), (OFF_V, 512), (OFF_GZ, 512), (OFF_SQ, 512), (OFF_SZ, 512),
           (OFF_SK, LANES), (OFF_SV, LANES), (OFF_GA, LANES))

_UNPAD_ROWS = ((OFF_QK, 0, 1024),
               (OFF_GA, 1024, GLA_RANK),
               (OFF_GZ, 1040, 1024),
               (OFF_SK, 2064, 256),
               (OFF_SZ, 2320, 512))


def _inproj_bwd(x2d, shift, sc1p, g_norm, wpad_t, dx2, pieces):
    s = x2d.shape[0]
    tm = min(512, s)
    nsteps = s // tm

    def body(x_ref, sh_ref, sc_ref, g_ref, w_hbm, dx2_ref, *rest):
        piece_refs = rest[:len(_PIECES)]
        gx_ref, dw_hbm, dsh_ref, dsc_ref, dg_ref, w_vm, dw_vm, sem, out_sems = rest[len(_PIECES):]
        i = pl.program_id(0)

        @pl.when(i == 0)
        def _():
            cp = pltpu.make_async_copy(w_hbm, w_vm, sem)
            cp.start()
            dw_vm[...] = jnp.zeros_like(dw_vm)
            dsh_ref[...] = jnp.zeros_like(dsh_ref)
            dsc_ref[...] = jnp.zeros_like(dsc_ref)
            dg_ref[...] = jnp.zeros_like(dg_ref)
            cp.wait()

        g, sc1p_v, shift_v = g_ref[...], sc_ref[...], sh_ref[...]
        subs = _subtiles(tm)
        dhs = []
        for sl in subs:
            dh = None
            for (off, width), pr in zip(_PIECES, piece_refs):
                part = _dot(pr[sl, :].astype(BF), w_vm[off:off + width, :])
                dh = part if dh is None else dh + part
            dhs.append(dh)
        norm = [_modnorm(x_ref[sl, :], g, sc1p_v, shift_v) for sl in subs]
        hb = jnp.concatenate([h.astype(BF) for _, _, h in norm], axis=0)
        for (off, width), pr in zip(_PIECES, piece_refs):
            dw_vm[off:off + width, :] += _dot(pr[...].astype(BF), hb, TN)
        for sl, (xn, r, _), dh in zip(subs, norm, dhs):
            dsh_ref[...] += jnp.sum(dh, axis=0, keepdims=True)
            dsc_ref[...] += jnp.sum(dh * (xn * g), axis=0, keepdims=True)
            dg_ref[...] += jnp.sum(dh * xn * sc1p_v, axis=0, keepdims=True)
            dxn = dh * g * sc1p_v
            gx_ref[sl, :] = dx2_ref[sl, :] + r * (dxn - xn * jnp.mean(dxn * xn, axis=-1, keepdims=True))

        @pl.when(i == nsteps - 1)
        def _():
            copies = [pltpu.make_async_copy(dw_vm.at[src:src + n], dw_hbm.at[dst:dst + n], out_sems.at[k])
                      for k, (src, dst, n) in enumerate(_UNPAD_ROWS)]
            for cp in copies:
                cp.start()
            for cp in copies:
                cp.wait()

    rowb = pl.BlockSpec((tm, D_MODEL), lambda i: (i, 0))
    vec = _full((1, D_MODEL))
    anyspec = pl.BlockSpec(memory_space=pl.ANY)
    piece_specs = [pl.BlockSpec((tm, width), lambda i: (i, 0)) for _, width in _PIECES]
    return pl.pallas_call(
        body, name="inproj_bwd", grid=(nsteps,),
        in_specs=[rowb, vec, vec, vec, anyspec, rowb] + piece_specs,
        out_specs=[rowb, anyspec, vec, vec, vec],
        out_shape=[jax.ShapeDtypeStruct((s, D_MODEL), F32), jax.ShapeDtypeStruct((D_IN, D_MODEL), F32),
                   jax.ShapeDtypeStruct((1, D_MODEL), F32), jax.ShapeDtypeStruct((1, D_MODEL), F32),
                   jax.ShapeDtypeStruct((1, D_MODEL), F32)],
        scratch_shapes=[pltpu.VMEM((D_PAD, D_MODEL), BF), pltpu.VMEM((D_PAD, D_MODEL), F32), pltpu.SemaphoreType.DMA,
                        pltpu.SemaphoreType.DMA((len(_UNPAD_ROWS),))],
        compiler_params=_params(("arbitrary",)),
    )(x2d, shift, sc1p, g_norm, wpad_t, dx2, *pieces)


def _adam(w, g, m, v):
    m2 = ADAM_B1 * m + (1.0 - ADAM_B1) * g
    v2 = ADAM_B2 * v + (1.0 - ADAM_B2) * (g * g)
    m_hat = m2 / (1.0 - ADAM_B1 ** ADAM_STEP)
    v_hat = v2 / (1.0 - ADAM_B2 ** ADAM_STEP)
    delta = -ADAM_LR * (m_hat / (jnp.sqrt(v_hat) + ADAM_EPS) + ADAM_WD * w)
    return delta, m2, v2


def _adamw(w, g, m, v, name):
    rr, cc = w.shape
    tc = min(256, cc)

    def body(w_ref, g_ref, m_ref, v_ref, d_ref, m2_ref, v2_ref):
        d_ref[...], m2_ref[...], v2_ref[...] = _adam(w_ref[...], g_ref[...], m_ref[...], v_ref[...])

    blk = pl.BlockSpec((rr, tc), lambda i: (0, i))
    return pl.pallas_call(
        body, name=name, grid=(cc // tc,), in_specs=[blk] * 4, out_specs=[blk] * 3,
        out_shape=[jax.ShapeDtypeStruct((rr, cc), F32)] * 3,
        compiler_params=_params(("arbitrary",)),
    )(w, g, m, v)


def _adamw_t(w3, g, m3, v3, name):
    rr, _, cc = w3.shape
    tc = min(256, cc)

    def body(w_hbm, g_ref, m_hbm, v_hbm, d_hbm, m2_hbm, v2_hbm, g3_hbm, w_vm, m_vm, v_vm, d_vm, m2_vm, v2_vm, in_sems, out_sems):
        cols = pl.ds(pl.multiple_of(pl.program_id(0) * tc, tc), tc)
        loads = [pltpu.make_async_copy(src.at[:, 0, cols], dst, in_sems.at[k])
                 for k, (src, dst) in enumerate(((w_hbm, w_vm), (m_hbm, m_vm), (v_hbm, v_vm)))]
        for cp in loads:
            cp.start()
        for cp in loads:
            cp.wait()
        d_vm[...], m2_vm[...], v2_vm[...] = _adam(w_vm[...], g_ref[...], m_vm[...], v_vm[...])
        stores = [pltpu.make_async_copy(src, dst.at[:, 0, cols], out_sems.at[k])
                  for k, (src, dst) in enumerate(((d_vm, d_hbm), (m2_vm, m2_hbm), (v2_vm, v2_hbm), (g_ref, g3_hbm)))]
        for cp in stores:
            cp.start()
        for cp in stores:
            cp.wait()

    hbm = pl.BlockSpec(memory_space=pl.ANY)
    return pl.pallas_call(
        body, name=name, grid=(cc // tc,), in_specs=[hbm, pl.BlockSpec((rr, tc), lambda i: (0, i)), hbm, hbm],
        out_specs=[hbm] * 4, out_shape=[jax.ShapeDtypeStruct((rr, 1, cc), F32)] * 4,
        scratch_shapes=[pltpu.VMEM((rr, tc), F32)] * 6 + [pltpu.SemaphoreType.DMA((3,)), pltpu.SemaphoreType.DMA((4,))],
        compiler_params=_params(("arbitrary",)),
    )(w3, g, m3, v3)


def _ada_update(c_all, dmod_cols, w, m, v):
    rr, cc = w.shape
    tr = min(256, rr)
    c_all = jnp.pad(c_all, ((0, 8), (0, 0)))
    dmod_cols = jnp.pad(dmod_cols, ((0, 8), (0, 0)))

    def body(c_ref, dm_ref, w_ref, m_ref, v_ref, g_ref, d_ref, m2_ref, v2_ref):
        cv = c_ref[...]
        sc = (cv * _sigmoid(cv)).astype(BF)
        g = _dot(sc, dm_ref[...].astype(BF), TN)
        g_ref[...] = g
        d_ref[...], m2_ref[...], v2_ref[...] = _adam(w_ref[...], g, m_ref[...], v_ref[...])

    blk = pl.BlockSpec((tr, cc), lambda i: (i, 0))
    return pl.pallas_call(
        body, name="ada_update", grid=(rr // tr,),
        in_specs=[pl.BlockSpec((16, tr), lambda i: (0, i)), _full((16, cc)), blk, blk, blk],
        out_specs=[blk] * 4, out_shape=[jax.ShapeDtypeStruct((rr, cc), F32)] * 4,
        compiler_params=_params(("arbitrary",)),
    )(c_all, dmod_cols, w, m, v)


def _small_update(parts, weights, moms, vels):
    n = len(weights)

    def body(*refs):
        p_refs, w_refs, m_refs, v_refs = refs[:n + 1], refs[n + 1:2 * n + 1], refs[2 * n + 1:3 * n + 1], refs[3 * n + 1:4 * n + 1]
        outs = refs[4 * n + 1:]
        for i in range(n):
            g = p_refs[i][0]
            for d in range(1, 8):
                g = g + p_refs[i][d]
            delta, m2, v2 = _adam(w_refs[i][...], g, m_refs[i][...], v_refs[i][...])
            outs[4 * i][...] = g
            outs[4 * i + 1][...] = delta
            outs[4 * i + 2][...] = m2
            outs[4 * i + 3][...] = v2
        tot = p_refs[n][0]
        for d in range(1, 8):
            tot = tot + p_refs[n][d]
        outs[4 * n][...] = tot

    out_shape = []
    for w in weights:
        out_shape += [jax.ShapeDtypeStruct(w.shape, F32)] * 4
    out_shape.append(jax.ShapeDtypeStruct(parts[n].shape[1:], F32))
    return pl.pallas_call(body, name="small_update", out_shape=out_shape, compiler_params=_params())(
        *parts, *weights, *moms, *vels)


def _pad_w_in_t(w):
    w = w.reshape(-1, w.shape[2])
    pad = jnp.zeros((LANES - GLA_RANK, w.shape[1]), w.dtype)
    return jnp.concatenate([w[dst:dst + n] for _, dst, n in sorted(_UNPAD_ROWS)] + [pad], axis=0)


def _rows8(a):
    flat = a.reshape(-1)
    rows = -(-flat.shape[0] // LANES)
    rows8 = -(-rows // 8) * 8
    flat = jnp.pad(flat, (0, rows8 * LANES - flat.shape[0]))
    return flat.reshape(rows8, LANES)


def kernel(x, c, positions, w_ada, b_ada, g_norm, w_in, w_decay, b_decay, g_gla_head, sinks, w_out, g_final, loss_target, m_w_ada, m_b_ada, m_g_norm, m_w_in, m_w_decay, m_b_decay, m_g_gla_head, m_sinks, m_w_out, m_g_final, v_w_ada, v_b_ada, v_g_norm, v_w_in, v_w_decay, v_b_decay, v_g_gla_head, v_sinks, v_w_out, v_g_final):
    ax, ay, ac = lax.axis_index("x"), lax.axis_index("y"), lax.axis_index("c")
    chip = 2 * ax + ay
    dev = 2 * chip + ac
    s = x.shape[1]
    x2d = x[0]
    target = loss_target[0]
    w_ada2, w_out2, w_dec2 = w_ada[0], w_out[0], w_decay[0]
    w_in_t = w_in[0].T
    ada_cols = w_ada2.shape[1]
    in_cols = w_in_t.shape[0]
    out_rows = w_out2.shape[0]
    half = D_MODEL // 2

    cw = jnp.concatenate([c.reshape(8, LANES), w_dec2.reshape(8, LANES)], axis=0)
    b_shard = lax.dynamic_slice(b_ada, (0, chip * ada_cols), (1, ada_cols))
    half_in = lax.dynamic_slice(w_in_t, (0, ac * half), (in_cols, half)).astype(BF)
    half_out = lax.dynamic_slice(w_out2, (ac * (out_rows // 2), 0), (out_rows // 2, D_MODEL)).astype(BF)
    inv_freq = 1.0 / (ROPE_THETA ** (jnp.arange(0, 64, 2, dtype=F32) / 64))
    first, mod_all, w_in_all, w_out_all, cos, sin = _prologue(
        cw, w_ada2, b_shard, half_in, half_out, positions.reshape(s, 1), jnp.tile(inv_freq, 4).reshape(1, LANES))

    first = first.reshape(8, 2, 8, LANES)
    c_all = first[:, 0].reshape(8, D_MODEL)
    w_dec_full = first[0::2, 1].reshape(4, GLA_RANK, 64).transpose(1, 0, 2).reshape(GLA_RANK, 256)
    mod = mod_all.reshape(4, 2, 8, ada_cols)[:, 0]
    mod = lax.dynamic_slice(mod, (0, dev, 0), (4, 1, ada_cols)).reshape(1, 4 * ada_cols)
    shift, sc1p, gate = mod[:, :D_MODEL], 1.0 + mod[:, D_MODEL:2 * D_MODEL], mod[:, 2 * D_MODEL:]
    wpad_t = _pad_w_in_t(w_in_all)
    w_out_all = w_out_all.reshape(D_MODEL, D_MODEL)

    wdecp = jnp.pad(w_dec_full, ((0, LANES - GLA_RANK), (0, 0))).astype(BF)

    proj = _inproj_fwd(x2d, shift, sc1p, g_norm, wpad_t)
    og, o_gla, sprev = _gla_fwd(proj, wdecp, b_decay, g_gla_head)
    osw, o_swa = _swa_fwd(proj, cos, sin, sinks)
    dx2, dog, dos, dw_out, loss_p, dgf, dgate = _outproj(og, osw, w_out_all, x2d, target, gate, g_final.reshape(1, D_MODEL))
    dsq, dsz, dsk, dsv, dsinks = _swa_bwd(proj, dos, o_swa, cos, sin, sinks)
    dqk, dv, dgz, dga, dwdp, dbd, dgg = _gla_bwd(proj, dog, o_gla, sprev, wdecp, b_decay, g_gla_head)
    pieces = (dqk, dv, dgz, dsq, dsz, dsk, dsv, dga)
    gx, dw_in_t, dshift, dscale, dgn = _inproj_bwd(x2d, shift, sc1p, g_norm, wpad_t, dx2, pieces)

    segs = [jnp.concatenate([dshift, dscale, dgate], axis=1), dgn, dgf, dwdp[:GLA_RANK], dbd, dgg, dsinks, loss_p]
    packed = [_rows8(a) for a in segs]
    offs = [0]
    for a in packed:
        offs.append(offs[-1] + a.shape[0])
    g_w_in_t, g_w_out, small = _epilogue(dw_in_t.reshape(4, in_cols, D_MODEL), dw_out.reshape(4, out_rows, D_MODEL),
                                         jnp.concatenate(packed, axis=0))

    def seg(i, size):
        return small[:, offs[i]:offs[i + 1]].reshape(8, -1)[:, :size]

    dmod_all = seg(0, 3 * D_MODEL)
    dwd_all = lax.dynamic_slice(seg(3, GLA_RANK * 256).reshape(8, GLA_RANK, 256), (0, 0, chip * 64), (8, GLA_RANK, 64))
    parts = [dmod_all.reshape(8, 1, 3 * D_MODEL), seg(1, D_MODEL).reshape(8, 1, D_MODEL), dwd_all,
             seg(4, 256).reshape(8, 1, 256), seg(5, 512).reshape(8, 1, 512), seg(6, SWA_HEADS).reshape(8, 1, SWA_HEADS),
             seg(2, D_MODEL).reshape(8, 1, D_MODEL), seg(7, LANES).reshape(8, 1, LANES)]
    smalls = _small_update(
        parts,
        [b_ada, g_norm, w_dec2, b_decay, g_gla_head, sinks, g_final.reshape(1, D_MODEL)],
        [m_b_ada, m_g_norm, m_w_decay[0], m_b_decay, m_g_gla_head, m_sinks, m_g_final.reshape(1, D_MODEL)],
        [v_b_ada, v_g_norm, v_w_decay[0], v_b_decay, v_g_gla_head, v_sinks, v_g_final.reshape(1, D_MODEL)])
    (g_b_ada, d_b_ada, nm_b_ada, nv_b_ada, g_gn, d_gn, nm_gn, nv_gn, g_wd, d_wd, nm_wd, nv_wd,
     g_bd, d_bd, nm_bd, nv_bd, g_gg, d_gg, nm_gg, nv_gg, g_sk, d_sk, nm_sk, nv_sk,
     g_gf, d_gf, nm_gf, nv_gf, loss_row) = smalls
    loss = loss_row[0, 0]

    dmod_cols = lax.dynamic_slice(dmod_all, (0, chip * ada_cols), (8, ada_cols))
    g_w_ada, d_w_ada, nm_w_ada, nv_w_ada = _ada_update(c_all, dmod_cols, w_ada2, m_w_ada[0], v_w_ada[0])
    to3 = lambda a: jnp.transpose(a, (2, 0, 1))
    from3 = lambda a: jnp.transpose(a, (1, 2, 0))[0]
    d3, nm3, nv3, g3 = _adamw_t(to3(w_in), g_w_in_t, to3(m_w_in), to3(v_w_in), "adamw_w_in")
    g_w_in, d_w_in, nm_w_in, nv_w_in = from3(g3), from3(d3), from3(nm3), from3(nv3)
    d_w_out, nm_w_out, nv_w_out = _adamw(w_out2, g_w_out, m_w_out[0], v_w_out[0], "adamw_w_out")

    flat = lambda a: a.reshape(D_MODEL)
    grads = [g_w_ada[None], g_b_ada, g_gn, g_w_in[None], g_wd[None], g_bd, g_gg, g_sk, g_w_out[None], flat(g_gf)]
    deltas = [d_w_ada[None], d_b_ada, d_gn, d_w_in[None], d_wd[None], d_bd, d_gg, d_sk, d_w_out[None], flat(d_gf)]
    new_m = [nm_w_ada[None], nm_b_ada, nm_gn, nm_w_in[None], nm_wd[None], nm_bd, nm_gg, nm_sk, nm_w_out[None], flat(nm_gf)]
    new_v = [nv_w_ada[None], nv_b_ada, nv_gn, nv_w_in[None], nv_wd[None], nv_bd, nv_gg, nv_sk, nv_w_out[None], flat(nv_gf)]
    return (loss, gx[None], *grads, *deltas, *new_m, *new_v)
```

```python
import jax
import jax.numpy as jnp
from jax import lax
from jax.experimental import pallas as pl
from jax.experimental.pallas import tpu as pltpu

F32 = jnp.float32
BF = jnp.bfloat16

D_MODEL = 1024
GLA_HEADS = 4
GLA_DK = 64
GLA_CHUNK = 64
GLA_RANK = 16
GLA_TAU = 16.0
GLA_SUB = 256
GLA_ROWS_FWD = 1024
GLA_ROWS_BWD = 512
SWA_HEADS = 8
SWA_BLOCK = 128
SWA_QBLOCKS_FWD = 16
SWA_QBLOCKS = 8
RMS_EPS = 1e-6
ROPE_THETA = 10000.0

OFF_QK, OFF_V, OFF_GZ, OFF_SQ, OFF_SZ, OFF_SK, OFF_SV, OFF_GA = 0, 512, 1024, 1536, 2048, 2560, 2688, 2816
D_PAD = 2944
D_IN = 2832
LANES = 128
VMEM_LIMIT = 56 * 1024 * 1024

ADAM_LR, ADAM_B1, ADAM_B2, ADAM_EPS, ADAM_WD, ADAM_STEP = 0.001, 0.9, 0.999, 1e-08, 0.01, 10

NT = (((1,), (1,)), ((), ()))
TN = (((0,), (0,)), ((), ()))
MESH = pl.DeviceIdType.MESH


def _dot(a, b, dims=None):
    if dims is None:
        return jnp.dot(a, b, preferred_element_type=F32)
    return lax.dot_general(a, b, dims, preferred_element_type=F32)


def _sigmoid(x):
    return 1.0 / (1.0 + jnp.exp(-x))


def _params(sem=None):
    return pltpu.CompilerParams(dimension_semantics=sem, vmem_limit_bytes=VMEM_LIMIT)


def _full(shape):
    return pl.BlockSpec(shape, lambda i: (0,) * len(shape))


def _subtiles(rows, size=256):
    size = min(size, rows)
    return [slice(k * size, (k + 1) * size) for k in range(rows // size)]


_GATHER_SEMS = [pltpu.SemaphoreType.DMA((7,)), pltpu.SemaphoreType.DMA((7,)), pltpu.SemaphoreType.DMA]


class _Gather:
    def __init__(self, x_ref, out_ref, send_sems, recv_sems, local_sem, slab=None):
        self.slab_of = slab
        x, y, c = lax.axis_index("x"), lax.axis_index("y"), lax.axis_index("c")
        self.me, self.sibling, self.c = (x, y, c), (x, y, 1 - c), c
        self.xn, self.yn, self.dg = (1 - x, y), (x, 1 - y), (1 - x, 1 - y)
        self.pass_from = (lax.rem(x + 1 - c, 2), lax.rem(y + c, 2))
        self.pass_to = (lax.rem(x + c, 2), lax.rem(y + 1 - c, 2))
        self.x_ref, self.out_ref, self.send_sems, self.recv_sems = x_ref, out_ref, send_sems, recv_sems
        self.mine = pltpu.make_async_copy(x_ref, self._slab(*self.me), local_sem)

    def _slab(self, px, py, pc):
        if self.slab_of is not None:
            return self.slab_of(self.out_ref, px, py, pc)
        return self.out_ref.at[4 * px + 2 * py + pc]

    def _copy(self, k, blk, to, src=None):
        return pltpu.make_async_remote_copy(
            src_ref=self._slab(*blk) if src is None else src, dst_ref=self._slab(*blk),
            send_sem=self.send_sems.at[k], recv_sem=self.recv_sems.at[k], device_id=to, device_id_type=MESH)

    def _sends(self):
        c = self.c
        return [self._copy(0, self.me, self.sibling, src=self.x_ref),
                self._copy(1, self.me, (*self.xn, c), src=self.x_ref),
                self._copy(2, self.me, (*self.yn, c), src=self.x_ref),
                self._copy(3, (*self.pass_from, c), (*self.pass_to, c)),
                self._copy(4, (*self.xn, c), self.sibling),
                self._copy(5, (*self.yn, c), self.sibling),
                self._copy(6, (*self.dg, c), self.sibling)]

    def start(self):
        self.mine.start()
        for cp in self._sends()[0:3]:
            cp.start()

    def pass_on(self):
        sends = self._sends()
        self._copy(1, (*self.xn, self.c), self.me).wait_recv()
        self._copy(2, (*self.yn, self.c), self.me).wait_recv()
        for k in (3, 4, 5):
            sends[k].start()

    def relay_diagonal(self):
        self._copy(3, (*self.dg, self.c), self.me).wait_recv()
        self._sends()[6].start()

    def relay(self):
        self.pass_on()
        self.relay_diagonal()

    def finish(self):
        c = self.c
        self._copy(0, self.sibling, self.me).wait_recv()
        for k, chip in ((4, self.xn), (5, self.yn), (6, self.dg)):
            self._copy(k, (*chip, 1 - c), self.me).wait_recv()
        for cp in self._sends():
            cp.wait_send()
        self.mine.wait()


def _prologue(cw, w_ada, b_shard, half_in, half_out, pos_col, inv_freq):
    s = pos_col.shape[0]
    rt = min(512, s)

    def body(cw_ref, wada_hbm, b_ref, hin_ref, hout_ref, pos_hbm, f_ref,
             first_ref, mod_ref, win_ref, wout_ref, cos_hbm, sin_hbm,
             mod_blk, cos_ref, sin_ref, wada_ref, pos_ref, table_sems, local_sems, *sems):
        fetch_w = pltpu.make_async_copy(wada_hbm, wada_ref, local_sems.at[0])
        fetch_p = pltpu.make_async_copy(pos_hbm, pos_ref, local_sems.at[1])
        fetch_w.start()
        fetch_p.start()
        g_c = _Gather(cw_ref, first_ref, *sems[0:3])
        half_lanes = hin_ref.shape[1]
        g_in = _Gather(hin_ref, win_ref, *sems[3:6],
                       slab=lambda ref, px, py, pc: ref.at[2 * px + py, :, pl.ds(pl.multiple_of(pc * half_lanes, half_lanes), half_lanes)])
        g_out = _Gather(hout_ref, wout_ref, *sems[6:9])
        g_mod = _Gather(mod_blk, mod_ref, *sems[9:12])
        g_c.start()
        g_in.start()
        g_out.start()
        g_c.relay()
        g_c.finish()
        c_rows = [jnp.concatenate([first_ref[d, r:r + 1, :] for r in range(8)], axis=1) for d in range(8)]
        c_all = jnp.concatenate(c_rows, axis=0)
        sc = (c_all * _sigmoid(c_all)).astype(BF)
        fetch_w.wait()
        mod_blk[...] = _dot(sc, wada_ref[...].astype(BF)) + b_ref[...]
        g_mod.start()
        fetch_p.wait()

        def rope_rows(i, carry):
            rows = pl.ds(pl.multiple_of(i * rt, rt), rt)
            ang = pos_ref[rows, :].astype(F32) * f_ref[...]
            lane = lax.broadcasted_iota(jnp.int32, ang.shape, 1)
            cos_ref[rows, :] = jnp.cos(ang)
            sn = jnp.sin(ang)
            sin_ref[rows, :] = jnp.where((lane % 64) < 32, -sn, sn)
            pltpu.make_async_copy(cos_ref.at[rows, :], cos_hbm.at[rows, :], table_sems.at[0]).start()
            pltpu.make_async_copy(sin_ref.at[rows, :], sin_hbm.at[rows, :], table_sems.at[1]).start()
            return carry

        steps = s // rt
        lax.fori_loop(0, steps // 2, rope_rows, 0)
        g_in.pass_on()
        g_out.pass_on()
        lax.fori_loop(steps // 2, steps, rope_rows, 0)
        g_in.relay_diagonal()
        g_out.relay_diagonal()
        g_mod.relay()
        g_in.finish()
        g_out.finish()
        g_mod.finish()
        pltpu.make_async_copy(cos_ref, cos_hbm, table_sems.at[0]).wait()
        pltpu.make_async_copy(sin_ref, sin_hbm, table_sems.at[1]).wait()

    vm = pl.BlockSpec(memory_space=pltpu.VMEM)
    hbm = pl.BlockSpec(memory_space=pl.ANY)
    return pl.pallas_call(
        body, name="prologue",
        out_shape=[jax.ShapeDtypeStruct((8,) + cw.shape, F32), jax.ShapeDtypeStruct((8, 8, w_ada.shape[1]), F32),
                   jax.ShapeDtypeStruct((4, half_in.shape[0], 2 * half_in.shape[1]), half_in.dtype),
                   jax.ShapeDtypeStruct((8,) + half_out.shape, half_out.dtype),
                   jax.ShapeDtypeStruct((s, LANES), F32), jax.ShapeDtypeStruct((s, LANES), F32)],
        in_specs=[vm, hbm, vm, hbm, hbm, hbm, vm], out_specs=[vm, vm, hbm, hbm, hbm, hbm],
        scratch_shapes=[pltpu.VMEM((8, w_ada.shape[1]), F32), pltpu.VMEM((s, LANES), F32), pltpu.VMEM((s, LANES), F32),
                        pltpu.VMEM(w_ada.shape, F32), pltpu.VMEM(pos_col.shape, jnp.int32),
                        pltpu.SemaphoreType.DMA((2,)), pltpu.SemaphoreType.DMA((2,))] + _GATHER_SEMS * 4,
        compiler_params=pltpu.CompilerParams(vmem_limit_bytes=VMEM_LIMIT),
    )(cw, w_ada, b_shard, half_in, half_out, pos_col, inv_freq)


def _reduce_scratch(rr, cc):
    c2 = cc // 2
    return [pltpu.VMEM((4, rr, c2), F32), pltpu.VMEM((4, rr, c2), F32), pltpu.VMEM((3, rr, c2), BF),
            pltpu.VMEM((2, rr, c2), BF), pltpu.VMEM((rr, c2), BF), pltpu.VMEM((rr, c2), F32),
            pltpu.SemaphoreType.DMA((8,)), pltpu.SemaphoreType.DMA((8,)), pltpu.SemaphoreType.DMA((5,))]


class _Reduce:
    def __init__(self, p_hbm, out_ref, acc_ref, own_ref, send_ref, land_ref, relay_ref, res_ref,
                 send_sems, recv_sems, local_sems):
        x, y, c = lax.axis_index("x"), lax.axis_index("y"), lax.axis_index("c")
        c2 = out_ref.shape[1] // 2
        sibling = (x, y, 1 - c)
        first = (lax.rem(x + 1 - c, 2), lax.rem(y + c, 2))
        second = (lax.rem(x + c, 2), lax.rem(y + 1 - c, 2))
        shards = [2 * first[0] + first[1], 2 * second[0] + second[1], 2 * (1 - x) + (1 - y), 2 * x + y]
        sibling_slot = (1, 0, 2, 3)
        mine = pl.ds(pl.multiple_of(c * c2, c2), c2)
        other = pl.ds(pl.multiple_of((1 - c) * c2, c2), c2)
        self.acc_ref, self.own_ref, self.send_ref, self.land_ref = acc_ref, own_ref, send_ref, land_ref
        self.relay_ref, self.res_ref = relay_ref, res_ref
        self.own = [pltpu.make_async_copy(p_hbm.at[j, :, mine], own_ref.at[k], local_sems.at[k])
                    for k, j in enumerate(shards)]
        self.swap_out = [pltpu.make_async_remote_copy(
            src_ref=p_hbm.at[j, :, other], dst_ref=acc_ref.at[sibling_slot[k]], send_sem=send_sems.at[k],
            recv_sem=recv_sems.at[sibling_slot[k]], device_id=sibling, device_id_type=MESH) for k, j in enumerate(shards)]
        self.swap_in = [pltpu.make_async_remote_copy(
            src_ref=p_hbm.at[j, :, other], dst_ref=acc_ref.at[k], send_sem=send_sems.at[k], recv_sem=recv_sems.at[k],
            device_id=sibling, device_id_type=MESH) for k, j in enumerate(shards)]

        def message(k, src, dst, to):
            return pltpu.make_async_remote_copy(src_ref=src, dst_ref=dst, send_sem=send_sems.at[k], recv_sem=recv_sems.at[k],
                                                device_id=(*to, c), device_id_type=MESH)

        self.direct = message(4, send_ref.at[0], land_ref.at[0], first)
        self.passed = message(5, send_ref.at[1], relay_ref, first)
        self.joint = message(6, send_ref.at[2], land_ref.at[1], second)
        self.put = pltpu.make_async_copy(res_ref, out_ref.at[:, mine], local_sems.at[4])
        self.share = pltpu.make_async_remote_copy(
            src_ref=res_ref, dst_ref=out_ref.at[:, mine], send_sem=send_sems.at[7],
            recv_sem=recv_sems.at[7], device_id=sibling, device_id_type=MESH)

    def start(self):
        for k in (0, 2, 1, 3):
            self.own[k].start()
            self.swap_out[k].start()

    def _combine(self, k):
        self.own[k].wait()
        self.swap_out[k].wait_send()
        self.swap_in[k].wait_recv()
        self.acc_ref[k] = self.acc_ref[k] + self.own_ref[k]

    def combine_and_send(self):
        dt = self.send_ref.dtype
        self._combine(0)
        self.send_ref[0] = self.acc_ref[0].astype(dt)
        self.direct.start()
        self._combine(2)
        self.send_ref[1] = self.acc_ref[2].astype(dt)
        self.passed.start()
        self._combine(1)
        self.passed.wait_recv()
        self.send_ref[2] = (self.acc_ref[1] + self.relay_ref[...].astype(F32)).astype(dt)
        self.joint.start()
        self._combine(3)

    def total_and_share(self):
        self.direct.wait_recv()
        self.joint.wait_recv()
        self.res_ref[...] = self.acc_ref[3] + self.land_ref[0].astype(F32) + self.land_ref[1].astype(F32)
        for cp in (self.direct, self.passed, self.joint):
            cp.wait_send()
        self.put.start()
        self.share.start()

    def finish(self):
        self.put.wait()
        self.share.wait()


def _epilogue(dw_in_parts, dw_out_parts, small):
    _, r_in, cc = dw_in_parts.shape
    _, r_out, _ = dw_out_parts.shape
    n_red = len(_reduce_scratch(r_in, cc))

    def body(pin_hbm, pout_hbm, small_ref, gin_ref, gout_ref, small_all_ref, *scratch):
        red_in = _Reduce(pin_hbm, gin_ref, *scratch[0:n_red])
        red_out = _Reduce(pout_hbm, gout_ref, *scratch[n_red:2 * n_red])
        gat = _Gather(small_ref, small_all_ref, *scratch[2 * n_red:])
        red_out.start()
        red_in.start()
        gat.start()
        red_out.combine_and_send()
        red_in.combine_and_send()
        gat.relay()
        red_out.total_and_share()
        red_in.total_and_share()
        gat.finish()
        red_out.finish()
        red_in.finish()

    vm = pl.BlockSpec(memory_space=pltpu.VMEM)
    anyspec = pl.BlockSpec(memory_space=pl.ANY)
    return pl.pallas_call(
        body, name="epilogue",
        out_shape=[jax.ShapeDtypeStruct((r_in, cc), F32), jax.ShapeDtypeStruct((r_out, cc), F32),
                   jax.ShapeDtypeStruct((8,) + small.shape, F32)],
        in_specs=[anyspec, anyspec, vm], out_specs=[anyspec, anyspec, vm],
        scratch_shapes=_reduce_scratch(r_in, cc) + _reduce_scratch(r_out, cc) + _GATHER_SEMS,
        compiler_params=pltpu.CompilerParams(vmem_limit_bytes=VMEM_LIMIT),
    )(dw_in_parts, dw_out_parts, small)


def _rope(t, cosb, sinb, first_half):
    partner = jnp.where(first_half, pltpu.roll(t, 96, 1), pltpu.roll(t, 32, 1))
    return t * cosb + partner * sinb


def _rope_t(g, cosb, sinb, first_half):
    gs = g * sinb
    partner = jnp.where(first_half, pltpu.roll(gs, 96, 1), pltpu.roll(gs, 32, 1))
    return g * cosb + partner


def _modnorm(x, g, sc1p, shift):
    r = lax.rsqrt(jnp.mean(x * x, axis=-1, keepdims=True) + RMS_EPS)
    xn = x * r
    return xn, r, (xn * g) * sc1p + shift


def _inproj_fwd(x2d, shift, sc1p, g_norm, wpad_t):
    s = x2d.shape[0]
    tm = min(1024, s)

    def body(x_ref, sh_ref, sc_ref, g_ref, w_ref, o_ref):
        subs = _subtiles(tm)
        hs = [_modnorm(x_ref[sl, :], g_ref[...], sc_ref[...], sh_ref[...])[2].astype(BF) for sl in subs]
        for sl, h in zip(subs, hs):
            o_ref[sl, :] = _dot(h, w_ref[...], NT)

    vec = _full((1, D_MODEL))
    return pl.pallas_call(
        body, name="inproj_fwd", grid=(s // tm,),
        in_specs=[pl.BlockSpec((tm, D_MODEL), lambda i: (i, 0)), vec, vec, vec, _full((D_PAD, D_MODEL))],
        out_specs=pl.BlockSpec((tm, D_PAD), lambda i: (i, 0)),
        out_shape=jax.ShapeDtypeStruct((s, D_PAD), F32),
        compiler_params=_params(("arbitrary",)),
    )(x2d, shift, sc1p, g_norm, wpad_t)


def _split3(a):
    hi = a.astype(BF)
    r1 = a - hi.astype(F32)
    mid = r1.astype(BF)
    lo = (r1 - mid.astype(F32)).astype(BF)
    return hi, mid, lo


def _tri_matmul(tri, a):
    hi, mid, lo = _split3(a)
    return _dot(tri, hi) + _dot(tri, mid) + _dot(tri, lo)


def _chunks(tb):
    return [slice(c * GLA_CHUNK, (c + 1) * GLA_CHUNK) for c in range(tb // GLA_CHUNK)]


def _per_chunk_rows(rows, width):
    return jnp.concatenate([jnp.broadcast_to(r, (GLA_CHUNK, width)) for r in rows], axis=0)


def _gla_triangle(tb):
    row = lax.broadcasted_iota(jnp.int32, (tb, tb), 0)
    col = lax.broadcasted_iota(jnp.int32, (tb, tb), 1)
    return (((row // GLA_CHUNK) == (col // GLA_CHUNK)) & (col <= row)).astype(F32)


def _lane_mean(x, ones_b):
    hi = x.astype(BF)
    lo = (x - hi.astype(F32)).astype(BF)
    return (_dot(hi, ones_b) + _dot(lo, ones_b)) * (1.0 / LANES)


def _head(t, h, lo_h):
    blk = t[:, LANES * (h // 2):LANES * (h // 2 + 1)]
    return jnp.where(lo_h, blk, 0.0) if h % 2 == 0 else jnp.where(lo_h, 0.0, blk)


def _gla_block_common(qk, ga, wd, bd, tril_b):
    tb = qk.shape[0]
    q, k = qk[:, :256], qk[:, 256:]
    z = _dot(ga.astype(BF), wd) + bd
    la = (jnp.minimum(z, 0.0) - jnp.log(1.0 + jnp.exp(-jnp.abs(z)))) * (1.0 / GLA_TAU)
    b = _tri_matmul(tril_b, la)
    bls = [b[rs.stop - 1:rs.stop, :] for rs in _chunks(tb)]
    eq = jnp.exp(b)
    ek = jnp.exp(-b)
    f = jnp.exp(_per_chunk_rows(bls, 256) - b)
    return z, eq, ek, f, q * (eq * GLA_DK ** -0.5), k * ek, k * f, bls


def _gla_units(s, rows):
    sub = min(GLA_SUB, s)
    tb = min(rows, s)
    subs = [slice(i * sub, (i + 1) * sub) for i in range(tb // sub)]
    units = [(i, h) for i in range(len(subs)) for h in range(GLA_HEADS)]
    return tb, sub, subs, units


def _gla_fwd(proj, wdecp, bdec, ggla):
    s = proj.shape[0]
    tb, sub, subs, units = _gla_units(s, GLA_ROWS_FWD)
    nch = sub // GLA_CHUNK

    def body(qk_ref, v_ref, gz_ref, ga_ref, wd_ref, bd_ref, gg_ref, tri_ref, og_ref, opre_ref, sprev_ref, st_ref):
        @pl.when(pl.program_id(0) == 0)
        def _():
            st_ref[...] = jnp.zeros_like(st_ref)

        lo_h = lax.broadcasted_iota(jnp.int32, (sub, LANES), 1) < GLA_DK
        tril = tri_ref[...] > 0.5
        tril_b = tri_ref[...].astype(BF)
        ones_b = jnp.ones((LANES, LANES), BF)
        gg, wd, bd = gg_ref[...], wd_ref[...], bd_ref[...]
        chunks = _chunks(sub)
        lanes = [slice(h * LANES, (h + 1) * LANES) for h in range(GLA_HEADS)]
        com = [_gla_block_common(qk_ref[sl, :], ga_ref[sl, :], wd, bd, tril_b) for sl in subs]
        decs = [[jnp.exp(bl) for bl in cm[7]] for cm in com]
        a = {(i, h): _head(com[i][4], h, lo_h).astype(BF) for i, h in units}
        bm = {(i, h): _head(com[i][5], h, lo_h).astype(BF) for i, h in units}
        ktl = {(i, h): _head(com[i][6], h, lo_h).astype(BF) for i, h in units}
        vh = {(i, h): v_ref[subs[i], lanes[h]].astype(BF) for i, h in units}
        sc = {u: _dot(a[u], bm[u], NT) for u in units}
        upd = {u: [_dot(vh[u][rs], ktl[u][rs], TN) for rs in chunks] for u in units}
        p = {u: jnp.where(tril, sc[u], 0.0).astype(BF) for u in units}
        o = {u: _dot(p[u], vh[u]) for u in units}
        states = {}
        for h in range(GLA_HEADS):
            st = st_ref[h]
            for i in range(len(subs)):
                entering = []
                for c in range(nch):
                    entering.append(st)
                    sprev_ref[i * nch + c, h] = st
                    st = st * decs[i][c][:, LANES * (h // 2):LANES * (h // 2 + 1)] + upd[(i, h)][c]
                states[(i, h)] = entering
            st_ref[h] = st
        inter = {u: [_dot(a[u][rs], states[u][c].astype(BF), NT) for c, rs in enumerate(chunks)] for u in units}
        o = {u: o[u] + jnp.concatenate(inter[u], axis=0) for u in units}
        ms = {u: _lane_mean(o[u] * o[u], ones_b) for u in units}
        for i, h in units:
            gzh = gz_ref[subs[i], lanes[h]]
            opre_ref[subs[i], lanes[h]] = o[(i, h)]
            og_ref[subs[i], lanes[h]] = (((o[(i, h)] * lax.rsqrt(ms[(i, h)] + RMS_EPS)) * gg[:, lanes[h]])
                                         * (gzh * _sigmoid(gzh))).astype(og_ref.dtype)

    def col(width, off):
        return pl.BlockSpec((tb, width), lambda i: (i, off // width))

    return pl.pallas_call(
        body, name="gla_fwd", grid=(s // tb,),
        in_specs=[col(512, OFF_QK), col(512, OFF_V), col(512, OFF_GZ), col(LANES, OFF_GA),
                  _full((LANES, 256)), _full((1, 256)), _full((1, 512)), _full((sub, sub))],
        out_specs=[pl.BlockSpec((tb, 512), lambda i: (i, 0)), pl.BlockSpec((tb, 512), lambda i: (i, 0)),
                   pl.BlockSpec((tb // GLA_CHUNK, GLA_HEADS, LANES, LANES), lambda i: (i, 0, 0, 0))],
        out_shape=[jax.ShapeDtypeStruct((s, 512), BF), jax.ShapeDtypeStruct((s, 512), F32),
                   jax.ShapeDtypeStruct((s // GLA_CHUNK, GLA_HEADS, LANES, LANES), F32)],
        scratch_shapes=[pltpu.VMEM((GLA_HEADS, LANES, LANES), F32)],
        compiler_params=_params(("arbitrary",)),
    )(proj, proj, proj, proj, wdecp, bdec, ggla, _gla_triangle(sub))


def _gla_bwd(proj, dog, opre, sprev, wdecp, bdec, ggla):
    s = proj.shape[0]
    tb, sub, subs, units = _gla_units(s, GLA_ROWS_BWD)
    nsub = len(subs)
    nch = sub // GLA_CHUNK
    nb = s // tb

    def body(qk_ref, v_ref, gz_ref, ga_ref, dog_ref, opre_ref, sprev_ref, wd_ref, bd_ref, gg_ref, tri_ref, triu_ref,
             dqk_ref, dv_ref, dgz_ref, dga_ref, dwd_ref, dbd_ref, dgg_ref, dst_ref):
        @pl.when(pl.program_id(0) == 0)
        def _():
            dst_ref[...] = jnp.zeros_like(dst_ref)
            dwd_ref[...] = jnp.zeros_like(dwd_ref)
            dbd_ref[...] = jnp.zeros_like(dbd_ref)
            dgg_ref[...] = jnp.zeros_like(dgg_ref)

        lo_h = lax.broadcasted_iota(jnp.int32, (sub, LANES), 1) < GLA_DK
        tril = tri_ref[...] > 0.5
        tril_b = tri_ref[...].astype(BF)
        triu_b = triu_ref[...].astype(BF)
        ones_b = jnp.ones((LANES, LANES), BF)
        last_row = (lax.broadcasted_iota(jnp.int32, (sub, LANES), 0) % GLA_CHUNK) == GLA_CHUNK - 1
        wd, gg, bd = wd_ref[...], gg_ref[...], bd_ref[...]
        chunks = _chunks(sub)
        lanes = [slice(h * LANES, (h + 1) * LANES) for h in range(GLA_HEADS)]
        blks = [slice(LANES * (h // 2), LANES * (h // 2 + 1)) for h in range(GLA_HEADS)]
        ga = [ga_ref[sl, :] for sl in subs]
        com = [_gla_block_common(qk_ref[sl, :], ga[i], wd, bd, tril_b) for i, sl in enumerate(subs)]
        decs = [[jnp.exp(bl) for bl in cm[7]] for cm in com]
        a = {(i, h): _head(com[i][4], h, lo_h).astype(BF) for i, h in units}
        bm = {(i, h): _head(com[i][5], h, lo_h).astype(BF) for i, h in units}
        ktl = {(i, h): _head(com[i][6], h, lo_h).astype(BF) for i, h in units}
        vh = {(i, h): v_ref[subs[i], lanes[h]].astype(BF) for i, h in units}
        sc = {u: _dot(a[u], bm[u], NT) for u in units}

        o = {(i, h): opre_ref[subs[i], lanes[h]] for i, h in units}
        ms = {u: _lane_mean(o[u] * o[u], ones_b) for u in units}
        gz = {(i, h): gz_ref[subs[i], lanes[h]] for i, h in units}
        dog = {(i, h): dog_ref[subs[i], lanes[h]] for i, h in units}
        sg = {u: _sigmoid(gz[u]) for u in units}
        r = {u: lax.rsqrt(ms[u] + RMS_EPS) for u in units}
        ohat = {u: o[u] * r[u] for u in units}
        sil = {u: gz[u] * sg[u] for u in units}
        for i, h in units:
            u = (i, h)
            dgz_ref[subs[i], lanes[h]] = (dog[u] * (ohat[u] * gg[:, lanes[h]])
                                          * (sg[u] * (1.0 + gz[u] * (1.0 - sg[u])))).astype(dgz_ref.dtype)
            dgg_ref[:, lanes[h]] += jnp.sum(dog[u] * sil[u] * ohat[u], axis=0, keepdims=True)
        dn = {(i, h): dog[(i, h)] * sil[(i, h)] * gg[:, lanes[h]] for i, h in units}
        mdn = {u: _lane_mean(dn[u] * ohat[u], ones_b) for u in units}
        do = {u: (r[u] * (dn[u] - ohat[u] * mdn[u])).astype(BF) for u in units}

        p = {u: jnp.where(tril, sc[u], 0.0).astype(BF) for u in units}
        dpr = {u: _dot(do[u], vh[u], NT) for u in units}
        incr = {u: [_dot(do[u][rs], a[u][rs], TN) for rs in chunks] for u in units}
        dv = {u: _dot(p[u], do[u], TN) for u in units}
        dp = {u: jnp.where(tril, dpr[u], 0.0).astype(BF) for u in units}
        dqd = {u: _dot(dp[u], bm[u]) for u in units}
        dkd = {u: _dot(dp[u], a[u], TN) for u in units}
        st = {(i, h): [sprev_ref[i * nch + c, h] for c in range(nch)] for i, h in units}
        leaving = {}
        for h in range(GLA_HEADS):
            d = dst_ref[h]
            for i in reversed(range(nsub)):
                out = [None] * nch
                for c in reversed(range(nch)):
                    out[c] = d
                    d = d * decs[i][c][:, blks[h]] + incr[(i, h)][c]
                leaving[(i, h)] = out
            dst_ref[h] = d
        lv_b = {u: [leaving[u][c].astype(BF) for c in range(nch)] for u in units}
        dv_s = {u: [_dot(ktl[u][rs], lv_b[u][c], NT) for c, rs in enumerate(chunks)] for u in units}
        dqd_s = {u: [_dot(do[u][rs], st[u][c].astype(BF)) for c, rs in enumerate(chunks)] for u in units}
        dkt_s = {u: [_dot(vh[u][rs], lv_b[u][c]) for c, rs in enumerate(chunks)] for u in units}
        ddec = {u: [jnp.sum(leaving[u][c] * st[u][c], axis=0, keepdims=True) for c in range(nch)] for u in units}
        for i, h in units:
            dv_ref[subs[i], lanes[h]] = (dv[(i, h)] + jnp.concatenate(dv_s[(i, h)], axis=0)).astype(dv_ref.dtype)
        dqd = {u: dqd[u] + jnp.concatenate(dqd_s[u], axis=0) for u in units}
        dkt = {u: jnp.concatenate(dkt_s[u], axis=0) for u in units}

        db = []
        for i, sl in enumerate(subs):
            _, eq, ek, f, qd, kd, kt, _ = com[i]
            parts = []
            for pair in range(GLA_HEADS // 2):
                blk, u0, u1 = blks[2 * pair], (i, 2 * pair), (i, 2 * pair + 1)
                dqd_b, dkd_b, dkt_b = dqd[u0] + dqd[u1], dkd[u0] + dkd[u1], dkt[u0] + dkt[u1]
                dqk_ref[sl, blk] = (dqd_b * (eq[:, blk] * GLA_DK ** -0.5)).astype(dqk_ref.dtype)
                dqk_ref[sl, 256 + LANES * pair:256 + LANES * (pair + 1)] = (dkd_b * ek[:, blk] + dkt_b * f[:, blk]).astype(dqk_ref.dtype)
                dkt_kt = dkt_b * kt[:, blk]
                dbp = dqd_b * qd[:, blk] - dkd_b * kd[:, blk] - dkt_kt
                dbl = [jnp.sum(dkt_kt[rs], axis=0, keepdims=True) + (ddec[u0][c] + ddec[u1][c]) * decs[i][c][:, blk]
                       for c, rs in enumerate(chunks)]
                parts.append(jnp.where(last_row, dbp + _per_chunk_rows(dbl, LANES), dbp))
            db.append(jnp.concatenate(parts, axis=1))
        dla = [_tri_matmul(triu_b, db[i]) for i in range(nsub)]
        dz32 = [dla[i] * (1.0 / GLA_TAU) * _sigmoid(-com[i][0]) for i in range(nsub)]
        dz = [t.astype(BF) for t in dz32]
        for i, sl in enumerate(subs):
            dga_ref[sl, :] = _dot(dz[i], wd, NT).astype(dga_ref.dtype)
            dwd_ref[...] += _dot(ga[i].astype(BF), dz[i], TN)
            dbd_ref[...] += jnp.sum(dz32[i], axis=0, keepdims=True)

    def col(width, off):
        return pl.BlockSpec((tb, width), lambda i: (nb - 1 - i, off // width))

    def rev(width):
        return pl.BlockSpec((tb, width), lambda i: (nb - 1 - i, 0))

    return pl.pallas_call(
        body, name="gla_bwd", grid=(nb,),
        in_specs=[col(512, OFF_QK), col(512, OFF_V), col(512, OFF_GZ), col(LANES, OFF_GA), rev(512), rev(512),
                  pl.BlockSpec((tb // GLA_CHUNK, GLA_HEADS, LANES, LANES), lambda i: (nb - 1 - i, 0, 0, 0)),
                  _full((LANES, 256)), _full((1, 256)), _full((1, 512)), _full((sub, sub)), _full((sub, sub))],
        out_specs=[rev(512), rev(512), rev(512), rev(LANES), _full((LANES, 256)), _full((1, 256)), _full((1, 512))],
        out_shape=[jax.ShapeDtypeStruct((s, 512), BF), jax.ShapeDtypeStruct((s, 512), BF),
                   jax.ShapeDtypeStruct((s, 512), BF), jax.ShapeDtypeStruct((s, LANES), BF),
                   jax.ShapeDtypeStruct((LANES, 256), F32), jax.ShapeDtypeStruct((1, 256), F32),
                   jax.ShapeDtypeStruct((1, 512), F32)],
        scratch_shapes=[pltpu.VMEM((GLA_HEADS, LANES, LANES), F32)],
        compiler_params=_params(("arbitrary",)),
    )(proj, proj, proj, proj, dog, opre, sprev, wdecp, bdec, ggla, _gla_triangle(sub), _gla_triangle(sub).T)


_SWA_COL_HEADS = (0, 2, 1, 3, 4, 6, 5, 7)
_SWA_COLS = SWA_HEADS * SWA_BLOCK


def _swa_masks():
    lo2 = lax.broadcasted_iota(jnp.int32, (2 * SWA_BLOCK, LANES), 1) < 64
    lane1 = lax.broadcasted_iota(jnp.int32, (SWA_BLOCK, LANES), 1)
    first_half = (lane1 % 64) < 32
    key = lax.broadcasted_iota(jnp.int32, (SWA_BLOCK, _SWA_COLS), 0)
    query = lax.broadcasted_iota(jnp.int32, (SWA_BLOCK, _SWA_COLS), 1) % SWA_BLOCK
    return lo2, lane1 < 64, first_half, key > query


def _merge_band(t, prev_mask, prev_bias=None):
    prev = t[:SWA_BLOCK] if prev_bias is None else t[:SWA_BLOCK] + prev_bias
    return jnp.where(prev_mask, prev, t[SWA_BLOCK:])


def _split_band(t, prev_mask_b):
    prev = t * prev_mask_b
    return jnp.concatenate([prev, t - prev], axis=0)


def _kv_variants(t, lo2):
    tr = pltpu.roll(t, 64, 1)
    lo_v = [jnp.where(lo2, t, 0.0).astype(BF), jnp.where(lo2, tr, 0.0).astype(BF)]
    hi_v = [jnp.where(lo2, 0.0, tr).astype(BF), jnp.where(lo2, 0.0, t).astype(BF)]
    return lo_v, hi_v


def _kv_variants_t(t):
    tt = t.T
    sw = jnp.concatenate([tt[64:], tt[:64]], axis=0)
    top = lax.broadcasted_iota(jnp.int32, tt.shape, 0) < 64
    lo_v = [jnp.where(top, tt, 0.0).astype(BF), jnp.where(top, sw, 0.0).astype(BF)]
    hi_v = [jnp.where(top, 0.0, sw).astype(BF), jnp.where(top, 0.0, tt).astype(BF)]
    return lo_v, hi_v


def _swa_scores(qg, k_lo, k_hi):
    return jnp.concatenate([_dot(k_lo[0], qg[0], NT), _dot(k_hi[0], qg[0], NT),
                            _dot(k_lo[1], qg[1], NT), _dot(k_hi[1], qg[1], NT)], axis=1)


def _sink_row(sinks_ref):
    return jnp.concatenate([jnp.full((1, SWA_BLOCK), sinks_ref[0, hd], F32) for hd in _SWA_COL_HEADS], axis=1)


def _swa_softmax(st, prev_mask, prev_bias, sink):
    st = _merge_band(st, prev_mask, prev_bias)
    m = jnp.maximum(jnp.max(st, axis=0, keepdims=True), sink)
    ex = jnp.exp(st - m)
    es = jnp.exp(sink - m)
    inv = 1.0 / (jnp.sum(ex, axis=0, keepdims=True) + es)
    return ex, es, inv


def _no_prev_bias(block_index):
    return jnp.where(block_index > 0, 0.0, -1e30).astype(F32)


def _swa_queries(sq_ref, rows, cosb, sinb, first_half):
    qs = [_rope(sq_ref[rows, p * LANES:(p + 1) * LANES], cosb, sinb, first_half) * 0.125 for p in range(4)]
    return [jnp.concatenate(qs[0:2], axis=0), jnp.concatenate(qs[2:4], axis=0)]


def _swa_fwd(proj, cos, sin, sinks):
    s = proj.shape[0]
    nq = min(SWA_QBLOCKS_FWD, s // SWA_BLOCK)
    tq = nq * SWA_BLOCK

    def body(sq_ref, sz_ref, sk_ref, sv_ref, cos_ref, sin_ref, sinks_ref, os_ref, opre_ref, kprev, vprev):
        n = pl.program_id(0)

        @pl.when(n == 0)
        def _():
            kprev[...] = jnp.zeros_like(kprev)
            vprev[...] = jnp.zeros_like(vprev)

        lo2, _, first_half, prev_mask = _swa_masks()
        prev_mask_b = jnp.where(prev_mask, 1.0, 0.0).astype(BF)
        sink = _sink_row(sinks_ref)
        blocks = range(nq)
        rows = [slice(j * SWA_BLOCK, (j + 1) * SWA_BLOCK) for j in blocks]
        cosb = [cos_ref[rows[j], :] for j in blocks]
        sinb = [sin_ref[rows[j], :] for j in blocks]
        kc = [_rope(sk_ref[rows[j], :], cosb[j], sinb[j], first_half) for j in blocks]
        vc = [sv_ref[rows[j], :] for j in blocks]
        kcat = [jnp.concatenate([kprev[...] if j == 0 else kc[j - 1], kc[j]], axis=0) for j in blocks]
        vcat = [jnp.concatenate([vprev[...] if j == 0 else vc[j - 1], vc[j]], axis=0) for j in blocks]
        kprev[...] = kc[-1]
        vprev[...] = vc[-1]
        kvar = [_kv_variants(kcat[j], lo2) for j in blocks]
        vtvar = [_kv_variants_t(vcat[j]) for j in blocks]
        qg = [[q.astype(BF) for q in _swa_queries(sq_ref, rows[j], cosb[j], sinb[j], first_half)] for j in blocks]
        st = [_swa_scores(qg[j], *kvar[j]) for j in blocks]
        soft = [_swa_softmax(st[j], prev_mask, _no_prev_bias(n) if j == 0 else None, sink) for j in blocks]
        pt = [_split_band(soft[j][0].astype(BF), prev_mask_b) for j in blocks]
        og = {}
        for j in blocks:
            inv = soft[j][2]
            for g in range(2):
                c0, c1, c2 = 512 * g, 512 * g + 256, 512 * g + 512
                ot = (_dot(vtvar[j][0][g], pt[j][:, c0:c1]) * inv[:, c0:c1]
                      + _dot(vtvar[j][1][g], pt[j][:, c1:c2]) * inv[:, c1:c2])
                og[(j, g)] = ot.T
        for j in blocks:
            for g in range(2):
                for i in range(2):
                    ls = slice((2 * g + i) * LANES, (2 * g + i + 1) * LANES)
                    o = og[(j, g)][i * SWA_BLOCK:(i + 1) * SWA_BLOCK]
                    sz = sz_ref[rows[j], ls]
                    opre_ref[rows[j], ls] = o
                    os_ref[rows[j], ls] = (o * (sz * _sigmoid(sz))).astype(os_ref.dtype)

    def col(width, off):
        return pl.BlockSpec((tq, width), lambda i: (i, off // width))

    row = pl.BlockSpec((tq, LANES), lambda i: (i, 0))
    return pl.pallas_call(
        body, name="swa_fwd", grid=(s // tq,),
        in_specs=[col(512, OFF_SQ), col(512, OFF_SZ), col(LANES, OFF_SK), col(LANES, OFF_SV), row, row,
                  pl.BlockSpec(memory_space=pltpu.SMEM)],
        out_specs=[pl.BlockSpec((tq, 512), lambda i: (i, 0))] * 2,
        out_shape=[jax.ShapeDtypeStruct((s, 512), BF), jax.ShapeDtypeStruct((s, 512), F32)],
        scratch_shapes=[pltpu.VMEM((SWA_BLOCK, LANES), F32)] * 2,
        compiler_params=_params(("arbitrary",)),
    )(proj, proj, proj, proj, cos, sin, sinks)


def _swa_bwd(proj, dos, opre, cos, sin, sinks):
    s = proj.shape[0]
    nq = min(SWA_QBLOCKS, s // SWA_BLOCK)
    tq = nq * SWA_BLOCK

    def body(sq_ref, sz_ref, sk_ref, sv_ref, dos_ref, opre_ref, cos_ref, sin_ref, sinks_ref,
             dsq_ref, dsz_ref, dsk_ref, dsv_ref, dsink_ref, kprev, vprev, cprev, sprev):
        n = pl.program_id(0)

        @pl.when(n == 0)
        def _():
            kprev[...] = jnp.zeros_like(kprev)
            vprev[...] = jnp.zeros_like(vprev)
            cprev[...] = jnp.zeros_like(cprev)
            sprev[...] = jnp.zeros_like(sprev)
            for hd in range(SWA_HEADS):
                dsink_ref[0, hd] = 0.0

        lo2, lo1, first_half, prev_mask = _swa_masks()
        prev_mask_b = jnp.where(prev_mask, 1.0, 0.0).astype(BF)
        lo1s = jnp.concatenate([lo1, lo1], axis=0)
        sink = _sink_row(sinks_ref)

        def home(m0, m1):
            t0 = m0 + pltpu.roll(m0, 64, 1)
            t1 = m1 + pltpu.roll(m1, 64, 1)
            return jnp.where(lo2, t0, t1)

        kp, vp, cp_, sp_ = kprev[...], vprev[...], cprev[...], sprev[...]
        for j in range(nq):
            rows = slice(j * SWA_BLOCK, (j + 1) * SWA_BLOCK)
            blk = n * nq + j
            cosb, sinb = cos_ref[rows, :], sin_ref[rows, :]
            kc = _rope(sk_ref[rows, :], cosb, sinb, first_half)
            vc = sv_ref[rows, :]
            kcat = jnp.concatenate([kp, kc], axis=0)
            k_lo, k_hi = _kv_variants(kcat, lo2)
            kt_lo, kt_hi = _kv_variants_t(kcat)
            v_lo, v_hi = _kv_variants(jnp.concatenate([vp, vc], axis=0), lo2)
            qg32 = _swa_queries(sq_ref, rows, cosb, sinb, first_half)
            qg = [q.astype(BF) for q in qg32]
            ex, es, inv = _swa_softmax(_swa_scores(qg, k_lo, k_hi), prev_mask, _no_prev_bias(n) if j == 0 else None, sink)
            pr, ps = ex * inv, es * inv

            dog32 = []
            for g in range(2):
                parts = []
                for i in range(2):
                    ls = slice((2 * g + i) * LANES, (2 * g + i + 1) * LANES)
                    sz = sz_ref[rows, ls]
                    sg = _sigmoid(sz)
                    dos_p = dos_ref[rows, ls]
                    dsz_ref[rows, ls] = (dos_p * opre_ref[rows, ls] * (sg * (1.0 + sz * (1.0 - sg)))).astype(dsz_ref.dtype)
                    parts.append(dos_p * (sz * sg))
                dog32.append(jnp.concatenate(parts, axis=0))
            dog = [t.astype(BF) for t in dog32]
            dpr = _merge_band(jnp.concatenate([_dot(v_lo[0], dog[0], NT), _dot(v_hi[0], dog[0], NT),
                                               _dot(v_lo[1], dog[1], NT), _dot(v_hi[1], dog[1], NT)], axis=1), prev_mask)
            rd = jnp.sum(pr * dpr, axis=0, keepdims=True)
            ds = _split_band((pr * (dpr - rd)).astype(BF), prev_mask_b)
            prb = _split_band(pr.astype(BF), prev_mask_b)
            sink_term = ps * rd
            for r, hd in enumerate(_SWA_COL_HEADS):
                dsink_ref[0, hd] += -jnp.sum(sink_term[:, r * SWA_BLOCK:(r + 1) * SWA_BLOCK])

            dk_g, dv_g = [], []
            for g in range(2):
                c0, c1, c2 = 512 * g, 512 * g + 256, 512 * g + 512
                dq = (_dot(kt_lo[g], ds[:, c0:c1]) + _dot(kt_hi[g], ds[:, c1:c2])).T
                for i in range(2):
                    ls = slice((2 * g + i) * LANES, (2 * g + i + 1) * LANES)
                    dsq_ref[rows, ls] = _rope_t(dq[i * SWA_BLOCK:(i + 1) * SWA_BLOCK] * 0.125, cosb, sinb,
                                                first_half).astype(dsq_ref.dtype)
                q_split = jnp.concatenate([jnp.where(lo1s, qg32[g], 0.0), jnp.where(lo1s, 0.0, qg32[g])], axis=0).astype(BF)
                do_split = jnp.concatenate([jnp.where(lo1s, dog32[g], 0.0), jnp.where(lo1s, 0.0, dog32[g])], axis=0).astype(BF)
                dk_g.append(_dot(ds[:, c0:c2], q_split))
                dv_g.append(_dot(prb[:, c0:c2], do_split))
            dk = home(dk_g[0], dk_g[1])
            dv = home(dv_g[0], dv_g[1])
            cur = pl.ds(pl.multiple_of(blk * SWA_BLOCK, SWA_BLOCK), SWA_BLOCK)
            dsk_ref[cur, :] = _rope_t(dk[SWA_BLOCK:], cosb, sinb, first_half)
            dsv_ref[cur, :] = dv[SWA_BLOCK:]
            dk_prev = _rope_t(dk[:SWA_BLOCK], cp_, sp_, first_half)
            dv_prev = dv[:SWA_BLOCK]
            if j == 0:
                @pl.when(n > 0)
                def _():
                    prv = pl.ds(pl.multiple_of((blk - 1) * SWA_BLOCK, SWA_BLOCK), SWA_BLOCK)
                    dsk_ref[prv, :] += dk_prev
                    dsv_ref[prv, :] += dv_prev
            else:
                prv = pl.ds(pl.multiple_of((blk - 1) * SWA_BLOCK, SWA_BLOCK), SWA_BLOCK)
                dsk_ref[prv, :] += dk_prev
                dsv_ref[prv, :] += dv_prev
            kp, vp, cp_, sp_ = kc, vc, cosb, sinb
        kprev[...] = kp
        vprev[...] = vp
        cprev[...] = cp_
        sprev[...] = sp_

    def col(width, off):
        return pl.BlockSpec((tq, width), lambda i: (i, off // width))

    row = pl.BlockSpec((tq, LANES), lambda i: (i, 0))
    wide = pl.BlockSpec((tq, 512), lambda i: (i, 0))
    return pl.pallas_call(
        body, name="swa_bwd", grid=(s // tq,),
        in_specs=[col(512, OFF_SQ), col(512, OFF_SZ), col(LANES, OFF_SK), col(LANES, OFF_SV), wide, wide, row, row,
                  pl.BlockSpec(memory_space=pltpu.SMEM)],
        out_specs=[wide, wide, _full((s, LANES)), _full((s, LANES)), pl.BlockSpec(memory_space=pltpu.SMEM)],
        out_shape=[jax.ShapeDtypeStruct((s, 512), BF), jax.ShapeDtypeStruct((s, 512), BF),
                   jax.ShapeDtypeStruct((s, LANES), F32), jax.ShapeDtypeStruct((s, LANES), F32),
                   jax.ShapeDtypeStruct((1, SWA_HEADS), F32)],
        scratch_shapes=[pltpu.VMEM((SWA_BLOCK, LANES), F32)] * 4,
        compiler_params=_params(("arbitrary",)),
    )(proj, proj, proj, proj, dos, opre, cos, sin, sinks)


def _outproj(og, osw, w_out, x2d, target, gate, g_final):
    s = x2d.shape[0]
    tm = min(512, s)

    def body(og_ref, os_ref, w_ref, x_ref, t_ref, gate_ref, gf_ref,
             dx2_ref, dog_ref, dos_ref, dw_ref, loss_ref, dgf_ref, dgate_ref):
        @pl.when(pl.program_id(0) == 0)
        def _():
            dw_ref[...] = jnp.zeros_like(dw_ref)
            loss_ref[...] = jnp.zeros_like(loss_ref)
            dgf_ref[...] = jnp.zeros_like(dgf_ref)
            dgate_ref[...] = jnp.zeros_like(dgate_ref)

        w = w_ref[...]
        gate, gf = gate_ref[...], gf_ref[...]
        subs = _subtiles(tm)
        ogv = [og_ref[sl, :] for sl in subs]
        osv = [os_ref[sl, :] for sl in subs]
        y = [_dot(ogv[k], w[:512]) + _dot(osv[k], w[512:]) for k in range(len(subs))]
        dys = []
        for k, sl in enumerate(subs):
            x2 = x_ref[sl, :] + gate * y[k]
            r = lax.rsqrt(jnp.mean(x2 * x2, axis=-1, keepdims=True) + RMS_EPS)
            xn = x2 * r
            err = xn * gf - t_ref[sl, :]
            loss_ref[...] += 0.5 * jnp.sum(jnp.mean(err * err, axis=-1, keepdims=True), axis=0, keepdims=True)
            dyf = err * (1.0 / D_MODEL)
            dgf_ref[...] += jnp.sum(dyf * xn, axis=0, keepdims=True)
            t = dyf * gf
            dx2 = r * (t - xn * jnp.mean(t * xn, axis=-1, keepdims=True))
            dx2_ref[sl, :] = dx2
            dgate_ref[...] += jnp.sum(dx2 * y[k], axis=0, keepdims=True)
            dys.append((dx2 * gate).astype(BF))
            dmix = _dot(dys[k], w, NT)
            dog_ref[sl, :] = dmix[:, :512]
            dos_ref[sl, :] = dmix[:, 512:]
        dy = jnp.concatenate(dys, axis=0)
        dw_ref[:512, :] += _dot(og_ref[...], dy, TN)
        dw_ref[512:, :] += _dot(os_ref[...], dy, TN)

    half = pl.BlockSpec((tm, 512), lambda i: (i, 0))
    rowb = pl.BlockSpec((tm, D_MODEL), lambda i: (i, 0))
    vec = _full((1, D_MODEL))
    return pl.pallas_call(
        body, name="outproj", grid=(s // tm,),
        in_specs=[half, half, _full((D_MODEL, D_MODEL)), rowb, rowb, vec, vec],
        out_specs=[rowb, half, half, _full((D_MODEL, D_MODEL)), _full((1, 1)), vec, vec],
        out_shape=[jax.ShapeDtypeStruct((s, D_MODEL), F32), jax.ShapeDtypeStruct((s, 512), F32),
                   jax.ShapeDtypeStruct((s, 512), F32), jax.ShapeDtypeStruct((D_MODEL, D_MODEL), F32),
                   jax.ShapeDtypeStruct((1, 1), F32), jax.ShapeDtypeStruct((1, D_MODEL), F32),
                   jax.ShapeDtypeStruct((1, D_MODEL), F32)],
        compiler_params=_params(("arbitrary",)),
    )(og, osw, w_out, x2d, target, gate, g_final)


_PIECES = ((OFF_QK, 512), (OFF_V, 512), (OFF_GZ, 512), (OFF_SQ, 512), (OFF_SZ, 512),
           (OFF_SK, LANES), (OFF_SV, LANES), (OFF_GA, LANES))

_UNPAD_ROWS = ((OFF_QK, 0, 1024),
               (OFF_GA, 1024, GLA_RANK),
               (OFF_GZ, 1040, 1024),
               (OFF_SK, 2064, 256),
               (OFF_SZ, 2320, 512))


def _inproj_bwd(x2d, shift, sc1p, g_norm, wpad_t, dx2, pieces):
    s = x2d.shape[0]
    tm = min(512, s)
    nsteps = s // tm

    def body(x_ref, sh_ref, sc_ref, g_ref, w_hbm, dx2_ref, *rest):
        piece_refs = rest[:len(_PIECES)]
        gx_ref, dw_hbm, dsh_ref, dsc_ref, dg_ref, w_vm, dw_vm, sem, out_sems = rest[len(_PIECES):]
        i = pl.program_id(0)

        @pl.when(i == 0)
        def _():
            cp = pltpu.make_async_copy(w_hbm, w_vm, sem)
            cp.start()
            dw_vm[...] = jnp.zeros_like(dw_vm)
            dsh_ref[...] = jnp.zeros_like(dsh_ref)
            dsc_ref[...] = jnp.zeros_like(dsc_ref)
            dg_ref[...] = jnp.zeros_like(dg_ref)
            cp.wait()

        g, sc1p_v, shift_v = g_ref[...], sc_ref[...], sh_ref[...]
        subs = _subtiles(tm)
        dhs = []
        for sl in subs:
            dh = None
            for (off, width), pr in zip(_PIECES, piece_refs):
                part = _dot(pr[sl, :].astype(BF), w_vm[off:off + width, :])
                dh = part if dh is None else dh + part
            dhs.append(dh)
        norm = [_modnorm(x_ref[sl, :], g, sc1p_v, shift_v) for sl in subs]
        hb = jnp.concatenate([h.astype(BF) for _, _, h in norm], axis=0)
        for (off, width), pr in zip(_PIECES, piece_refs):
            dw_vm[off:off + width, :] += _dot(pr[...].astype(BF), hb, TN)
        for sl, (xn, r, _), dh in zip(subs, norm, dhs):
            dsh_ref[...] += jnp.sum(dh, axis=0, keepdims=True)
            dsc_ref[...] += jnp.sum(dh * (xn * g), axis=0, keepdims=True)
            dg_ref[...] += jnp.sum(dh * xn * sc1p_v, axis=0, keepdims=True)
            dxn = dh * g * sc1p_v
            gx_ref[sl, :] = dx2_ref[sl, :] + r * (dxn - xn * jnp.mean(dxn * xn, axis=-1, keepdims=True))

        @pl.when(i == nsteps - 1)
        def _():
            copies = [pltpu.make_async_copy(dw_vm.at[src:src + n], dw_hbm.at[dst:dst + n], out_sems.at[k])
                      for k, (src, dst, n) in enumerate(_UNPAD_ROWS)]
            for cp in copies:
                cp.start()
            for cp in copies:
                cp.wait()

    rowb = pl.BlockSpec((tm, D_MODEL), lambda i: (i, 0))
    vec = _full((1, D_MODEL))
    anyspec = pl.BlockSpec(memory_space=pl.ANY)
    piece_specs = [pl.BlockSpec((tm, width), lambda i: (i, 0)) for _, width in _PIECES]
    return pl.pallas_call(
        body, name="inproj_bwd", grid=(nsteps,),
        in_specs=[rowb, vec, vec, vec, anyspec, rowb] + piece_specs,
        out_specs=[rowb, anyspec, vec, vec, vec],
        out_shape=[jax.ShapeDtypeStruct((s, D_MODEL), F32), jax.ShapeDtypeStruct((D_IN, D_MODEL), F32),
                   jax.ShapeDtypeStruct((1, D_MODEL), F32), jax.ShapeDtypeStruct((1, D_MODEL), F32),
                   jax.ShapeDtypeStruct((1, D_MODEL), F32)],
        scratch_shapes=[pltpu.VMEM((D_PAD, D_MODEL), BF), pltpu.VMEM((D_PAD, D_MODEL), F32), pltpu.SemaphoreType.DMA,
                        pltpu.SemaphoreType.DMA((len(_UNPAD_ROWS),))],
        compiler_params=_params(("arbitrary",)),
    )(x2d, shift, sc1p, g_norm, wpad_t, dx2, *pieces)


def _adam(w, g, m, v):
    m2 = ADAM_B1 * m + (1.0 - ADAM_B1) * g
    v2 = ADAM_B2 * v + (1.0 - ADAM_B2) * (g * g)
    m_hat = m2 / (1.0 - ADAM_B1 ** ADAM_STEP)
    v_hat = v2 / (1.0 - ADAM_B2 ** ADAM_STEP)
    delta = -ADAM_LR * (m_hat / (jnp.sqrt(v_hat) + ADAM_EPS) + ADAM_WD * w)
    return delta, m2, v2


def _adamw(w, g, m, v, name):
    rr, cc = w.shape
    tc = min(256, cc)

    def body(w_ref, g_ref, m_ref, v_ref, d_ref, m2_ref, v2_ref):
        d_ref[...], m2_ref[...], v2_ref[...] = _adam(w_ref[...], g_ref[...], m_ref[...], v_ref[...])

    blk = pl.BlockSpec((rr, tc), lambda i: (0, i))
    return pl.pallas_call(
        body, name=name, grid=(cc // tc,), in_specs=[blk] * 4, out_specs=[blk] * 3,
        out_shape=[jax.ShapeDtypeStruct((rr, cc), F32)] * 3,
        compiler_params=_params(("arbitrary",)),
    )(w, g, m, v)


def _adamw_t(w3, g, m3, v3, name):
    rr, _, cc = w3.shape
    tc = min(512, cc)

    def body(w_hbm, g_ref, m_hbm, v_hbm, d_hbm, m2_hbm, v2_hbm, g3_hbm, w_vm, m_vm, v_vm, d_vm, m2_vm, v2_vm, in_sems, out_sems):
        cols = pl.ds(pl.multiple_of(pl.program_id(0) * tc, tc), tc)
        loads = [pltpu.make_async_copy(src.at[:, 0, cols], dst, in_sems.at[k])
                 for k, (src, dst) in enumerate(((w_hbm, w_vm), (m_hbm, m_vm), (v_hbm, v_vm)))]
        for cp in loads:
            cp.start()
        for cp in loads:
            cp.wait()
        d_vm[...], m2_vm[...], v2_vm[...] = _adam(w_vm[...], g_ref[...], m_vm[...], v_vm[...])
        stores = [pltpu.make_async_copy(src, dst.at[:, 0, cols], out_sems.at[k])
                  for k, (src, dst) in enumerate(((d_vm, d_hbm), (m2_vm, m2_hbm), (v2_vm, v2_hbm), (g_ref, g3_hbm)))]
        for cp in stores:
            cp.start()
        for cp in stores:
            cp.wait()

    hbm = pl.BlockSpec(memory_space=pl.ANY)
    return pl.pallas_call(
        body, name=name, grid=(cc // tc,), in_specs=[hbm, pl.BlockSpec((rr, tc), lambda i: (0, i)), hbm, hbm],
        out_specs=[hbm] * 4, out_shape=[jax.ShapeDtypeStruct((rr, 1, cc), F32)] * 4,
        scratch_shapes=[pltpu.VMEM((rr, tc), F32)] * 6 + [pltpu.SemaphoreType.DMA((3,)), pltpu.SemaphoreType.DMA((4,))],
        compiler_params=_params(("arbitrary",)),
    )(w3, g, m3, v3)


def _ada_update(c_all, dmod_cols, w, m, v):
    rr, cc = w.shape
    tr = min(256, rr)
    c_all = jnp.pad(c_all, ((0, 8), (0, 0)))
    dmod_cols = jnp.pad(dmod_cols, ((0, 8), (0, 0)))

    def body(c_ref, dm_ref, w_ref, m_ref, v_ref, g_ref, d_ref, m2_ref, v2_ref):
        cv = c_ref[...]
        sc = (cv * _sigmoid(cv)).astype(BF)
        g = _dot(sc, dm_ref[...].astype(BF), TN)
        g_ref[...] = g
        d_ref[...], m2_ref[...], v2_ref[...] = _adam(w_ref[...], g, m_ref[...], v_ref[...])

    blk = pl.BlockSpec((tr, cc), lambda i: (i, 0))
    return pl.pallas_call(
        body, name="ada_update", grid=(rr // tr,),
        in_specs=[pl.BlockSpec((16, tr), lambda i: (0, i)), _full((16, cc)), blk, blk, blk],
        out_specs=[blk] * 4, out_shape=[jax.ShapeDtypeStruct((rr, cc), F32)] * 4,
        compiler_params=_params(("arbitrary",)),
    )(c_all, dmod_cols, w, m, v)


def _small_update(parts, weights, moms, vels):
    n = len(weights)

    def body(*refs):
        p_refs, w_refs, m_refs, v_refs = refs[:n + 1], refs[n + 1:2 * n + 1], refs[2 * n + 1:3 * n + 1], refs[3 * n + 1:4 * n + 1]
        outs = refs[4 * n + 1:]
        for i in range(n):
            g = p_refs[i][0]
            for d in range(1, 8):
                g = g + p_refs[i][d]
            delta, m2, v2 = _adam(w_refs[i][...], g, m_refs[i][...], v_refs[i][...])
            outs[4 * i][...] = g
            outs[4 * i + 1][...] = delta
            outs[4 * i + 2][...] = m2
            outs[4 * i + 3][...] = v2
        tot = p_refs[n][0]
        for d in range(1, 8):
            tot = tot + p_refs[n][d]
        outs[4 * n][...] = tot

    out_shape = []
    for w in weights:
        out_shape += [jax.ShapeDtypeStruct(w.shape, F32)] * 4
    out_shape.append(jax.ShapeDtypeStruct(parts[n].shape[1:], F32))
    return pl.pallas_call(body, name="small_update", out_shape=out_shape, compiler_params=_params())(
        *parts, *weights, *moms, *vels)


def _pad_w_in_t(w):
    w = w.reshape(-1, w.shape[2])
    pad = jnp.zeros((LANES - GLA_RANK, w.shape[1]), w.dtype)
    return jnp.concatenate([w[dst:dst + n] for _, dst, n in sorted(_UNPAD_ROWS)] + [pad], axis=0)


def _rows8(a):
    flat = a.reshape(-1)
    rows = -(-flat.shape[0] // LANES)
    rows8 = -(-rows // 8) * 8
    flat = jnp.pad(flat, (0, rows8 * LANES - flat.shape[0]))
    return flat.reshape(rows8, LANES)


def kernel(x, c, positions, w_ada, b_ada, g_norm, w_in, w_decay, b_decay, g_gla_head, sinks, w_out, g_final, loss_target, m_w_ada, m_b_ada, m_g_norm, m_w_in, m_w_decay, m_b_decay, m_g_gla_head, m_sinks, m_w_out, m_g_final, v_w_ada, v_b_ada, v_g_norm, v_w_in, v_w_decay, v_b_decay, v_g_gla_head, v_sinks, v_w_out, v_g_final):
    ax, ay, ac = lax.axis_index("x"), lax.axis_index("y"), lax.axis_index("c")
    chip = 2 * ax + ay
    dev = 2 * chip + ac
    s = x.shape[1]
    x2d = x[0]
    target = loss_target[0]
    w_ada2, w_out2, w_dec2 = w_ada[0], w_out[0], w_decay[0]
    w_in_t = w_in[0].T
    ada_cols = w_ada2.shape[1]
    in_cols = w_in_t.shape[0]
    out_rows = w_out2.shape[0]
    half = D_MODEL // 2

    cw = jnp.concatenate([c.reshape(8, LANES), w_dec2.reshape(8, LANES)], axis=0)
    b_shard = lax.dynamic_slice(b_ada, (0, chip * ada_cols), (1, ada_cols))
    half_in = lax.dynamic_slice(w_in_t, (0, ac * half), (in_cols, half)).astype(BF)
    half_out = lax.dynamic_slice(w_out2, (ac * (out_rows // 2), 0), (out_rows // 2, D_MODEL)).astype(BF)
    inv_freq = 1.0 / (ROPE_THETA ** (jnp.arange(0, 64, 2, dtype=F32) / 64))
    first, mod_all, w_in_all, w_out_all, cos, sin = _prologue(
        cw, w_ada2, b_shard, half_in, half_out, positions.reshape(s, 1), jnp.tile(inv_freq, 4).reshape(1, LANES))

    first = first.reshape(8, 2, 8, LANES)
    c_all = first[:, 0].reshape(8, D_MODEL)
    w_dec_full = first[0::2, 1].reshape(4, GLA_RANK, 64).transpose(1, 0, 2).reshape(GLA_RANK, 256)
    mod = mod_all.reshape(4, 2, 8, ada_cols)[:, 0]
    mod = lax.dynamic_slice(mod, (0, dev, 0), (4, 1, ada_cols)).reshape(1, 4 * ada_cols)
    shift, sc1p, gate = mod[:, :D_MODEL], 1.0 + mod[:, D_MODEL:2 * D_MODEL], mod[:, 2 * D_MODEL:]
    wpad_t = _pad_w_in_t(w_in_all)
    w_out_all = w_out_all.reshape(D_MODEL, D_MODEL)

    wdecp = jnp.pad(w_dec_full, ((0, LANES - GLA_RANK), (0, 0))).astype(BF)

    proj = _inproj_fwd(x2d, shift, sc1p, g_norm, wpad_t)
    og, o_gla, sprev = _gla_fwd(proj, wdecp, b_decay, g_gla_head)
    osw, o_swa = _swa_fwd(proj, cos, sin, sinks)
    dx2, dog, dos, dw_out, loss_p, dgf, dgate = _outproj(og, osw, w_out_all, x2d, target, gate, g_final.reshape(1, D_MODEL))
    dsq, dsz, dsk, dsv, dsinks = _swa_bwd(proj, dos, o_swa, cos, sin, sinks)
    dqk, dv, dgz, dga, dwdp, dbd, dgg = _gla_bwd(proj, dog, o_gla, sprev, wdecp, b_decay, g_gla_head)
    pieces = (dqk, dv, dgz, dsq, dsz, dsk, dsv, dga)
    gx, dw_in_t, dshift, dscale, dgn = _inproj_bwd(x2d, shift, sc1p, g_norm, wpad_t, dx2, pieces)

    segs = [jnp.concatenate([dshift, dscale, dgate], axis=1), dgn, dgf, dwdp[:GLA_RANK], dbd, dgg, dsinks, loss_p]
    packed = [_rows8(a) for a in segs]
    offs = [0]
    for a in packed:
        offs.append(offs[-1] + a.shape[0])
    g_w_in_t, g_w_out, small = _epilogue(dw_in_t.reshape(4, in_cols, D_MODEL), dw_out.reshape(4, out_rows, D_MODEL),
                                         jnp.concatenate(packed, axis=0))

    def seg(i, size):
        return small[:, offs[i]:offs[i + 1]].reshape(8, -1)[:, :size]

    dmod_all = seg(0, 3 * D_MODEL)
    dwd_all = lax.dynamic_slice(seg(3, GLA_RANK * 256).reshape(8, GLA_RANK, 256), (0, 0, chip * 64), (8, GLA_RANK, 64))
    parts = [dmod_all.reshape(8, 1, 3 * D_MODEL), seg(1, D_MODEL).reshape(8, 1, D_MODEL), dwd_all,
             seg(4, 256).reshape(8, 1, 256), seg(5, 512).reshape(8, 1, 512), seg(6, SWA_HEADS).reshape(8, 1, SWA_HEADS),
             seg(2, D_MODEL).reshape(8, 1, D_MODEL), seg(7, LANES).reshape(8, 1, LANES)]
    smalls = _small_update(
        parts,
        [b_ada, g_norm, w_dec2, b_decay, g_gla_head, sinks, g_final.reshape(1, D_MODEL)],
        [m_b_ada, m_g_norm, m_w_decay[0], m_b_decay, m_g_gla_head, m_sinks, m_g_final.reshape(1, D_MODEL)],
        [v_b_ada, v_g_norm, v_w_decay[0], v_b_decay, v_g_gla_head, v_sinks, v_g_final.reshape(1, D_MODEL)])
    (g_b_ada, d_b_ada, nm_b_ada, nv_b_ada, g_gn, d_gn, nm_gn, nv_gn, g_wd, d_wd, nm_wd, nv_wd,
     g_bd, d_bd, nm_bd, nv_bd, g_gg, d_gg, nm_gg, nv_gg, g_sk, d_sk, nm_sk, nv_sk,
     g_gf, d_gf, nm_gf, nv_gf, loss_row) = smalls
    loss = loss_row[0, 0]

    dmod_cols = lax.dynamic_slice(dmod_all, (0, chip * ada_cols), (8, ada_cols))
    g_w_ada, d_w_ada, nm_w_ada, nv_w_ada = _ada_update(c_all, dmod_cols, w_ada2, m_w_ada[0], v_w_ada[0])
    to3 = lambda a: jnp.transpose(a, (2, 0, 1))
    from3 = lambda a: jnp.transpose(a, (1, 2, 0))[0]
    d3, nm3, nv3, g3 = _adamw_t(to3(w_in), g_w_in_t, to3(m_w_in), to3(v_w_in), "adamw_w_in")
    g_w_in, d_w_in, nm_w_in, nv_w_in = from3(g3), from3(d3), from3(nm3), from3(nv3)
    d_w_out, nm_w_out, nv_w_out = _adamw(w_out2, g_w_out, m_w_out[0], v_w_out[0], "adamw_w_out")

    flat = lambda a: a.reshape(D_MODEL)
    grads = [g_w_ada[None], g_b_ada, g_gn, g_w_in[None], g_wd[None], g_bd, g_gg, g_sk, g_w_out[None], flat(g_gf)]
    deltas = [d_w_ada[None], d_b_ada, d_gn, d_w_in[None], d_wd[None], d_bd, d_gg, d_sk, d_w_out[None], flat(d_gf)]
    new_m = [nm_w_ada[None], nm_b_ada, nm_gn, nm_w_in[None], nm_wd[None], nm_bd, nm_gg, nm_sk, nm_w_out[None], flat(nm_gf)]
    new_v = [nv_w_ada[None], nv_b_ada, nv_gn, nv_w_in[None], nv_wd[None], nv_bd, nv_gg, nv_sk, nv_w_out[None], flat(nv_gf)]
    return (loss, gx[None], *grads, *deltas, *new_m, *new_v)
```

```python
import jax
import jax.numpy as jnp
from jax import lax
from jax.experimental import pallas as pl
from jax.experimental.pallas import tpu as pltpu

F32 = jnp.float32
BF = jnp.bfloat16

D_MODEL = 1024
GLA_HEADS = 4
GLA_DK = 64
GLA_CHUNK = 64
GLA_RANK = 16
GLA_TAU = 16.0
GLA_SUB = 256
GLA_ROWS_FWD = 1024
GLA_ROWS_BWD = 512
SWA_HEADS = 8
SWA_BLOCK = 128
SWA_QBLOCKS_FWD = 8
SWA_QBLOCKS = 8
RMS_EPS = 1e-6
ROPE_THETA = 10000.0

OFF_QK, OFF_V, OFF_GZ, OFF_SQ, OFF_SZ, OFF_SK, OFF_SV, OFF_GA = 0, 512, 1024, 1536, 2048, 2560, 2688, 2816
D_PAD = 2944
D_IN = 2832
LANES = 128
VMEM_LIMIT = 56 * 1024 * 1024

ADAM_LR, ADAM_B1, ADAM_B2, ADAM_EPS, ADAM_WD, ADAM_STEP = 0.001, 0.9, 0.999, 1e-08, 0.01, 10

NT = (((1,), (1,)), ((), ()))
TN = (((0,), (0,)), ((), ()))
MESH = pl.DeviceIdType.MESH


def _dot(a, b, dims=None):
    if dims is None:
        return jnp.dot(a, b, preferred_element_type=F32)
    return lax.dot_general(a, b, dims, preferred_element_type=F32)


def _sigmoid(x):
    return 1.0 / (1.0 + jnp.exp(-x))


def _params(sem=None):
    return pltpu.CompilerParams(dimension_semantics=sem, vmem_limit_bytes=VMEM_LIMIT)


def _full(shape):
    return pl.BlockSpec(shape, lambda i: (0,) * len(shape))


def _subtiles(rows, size=256):
    size = min(size, rows)
    return [slice(k * size, (k + 1) * size) for k in range(rows // size)]


_GATHER_SEMS = [pltpu.SemaphoreType.DMA((7,)), pltpu.SemaphoreType.DMA((7,)), pltpu.SemaphoreType.DMA]


class _Gather:
    def __init__(self, x_ref, out_ref, send_sems, recv_sems, local_sem, slab=None):
        self.slab_of = slab
        x, y, c = lax.axis_index("x"), lax.axis_index("y"), lax.axis_index("c")
        self.me, self.sibling, self.c = (x, y, c), (x, y, 1 - c), c
        self.xn, self.yn, self.dg = (1 - x, y), (x, 1 - y), (1 - x, 1 - y)
        self.pass_from = (lax.rem(x + 1 - c, 2), lax.rem(y + c, 2))
        self.pass_to = (lax.rem(x + c, 2), lax.rem(y + 1 - c, 2))
        self.x_ref, self.out_ref, self.send_sems, self.recv_sems = x_ref, out_ref, send_sems, recv_sems
        self.mine = pltpu.make_async_copy(x_ref, self._slab(*self.me), local_sem)

    def _slab(self, px, py, pc):
        if self.slab_of is not None:
            return self.slab_of(self.out_ref, px, py, pc)
        return self.out_ref.at[4 * px + 2 * py + pc]

    def _copy(self, k, blk, to, src=None):
        return pltpu.make_async_remote_copy(
            src_ref=self._slab(*blk) if src is None else src, dst_ref=self._slab(*blk),
            send_sem=self.send_sems.at[k], recv_sem=self.recv_sems.at[k], device_id=to, device_id_type=MESH)

    def _sends(self):
        c = self.c
        return [self._copy(0, self.me, self.sibling, src=self.x_ref),
                self._copy(1, self.me, (*self.xn, c), src=self.x_ref),
                self._copy(2, self.me, (*self.yn, c), src=self.x_ref),
                self._copy(3, (*self.pass_from, c), (*self.pass_to, c)),
                self._copy(4, (*self.xn, c), self.sibling),
                self._copy(5, (*self.yn, c), self.sibling),
                self._copy(6, (*self.dg, c), self.sibling)]

    def start(self):
        self.mine.start()
        for cp in self._sends()[0:3]:
            cp.start()

    def pass_on(self):
        sends = self._sends()
        self._copy(1, (*self.xn, self.c), self.me).wait_recv()
        self._copy(2, (*self.yn, self.c), self.me).wait_recv()
        for k in (3, 4, 5):
            sends[k].start()

    def relay_diagonal(self):
        self._copy(3, (*self.dg, self.c), self.me).wait_recv()
        self._sends()[6].start()

    def relay(self):
        self.pass_on()
        self.relay_diagonal()

    def finish(self):
        c = self.c
        self._copy(0, self.sibling, self.me).wait_recv()
        for k, chip in ((4, self.xn), (5, self.yn), (6, self.dg)):
            self._copy(k, (*chip, 1 - c), self.me).wait_recv()
        for cp in self._sends():
            cp.wait_send()
        self.mine.wait()


def _prologue(cw, w_ada, b_shard, half_in, half_out, pos_col, inv_freq):
    s = pos_col.shape[0]
    rt = min(512, s)

    def body(cw_ref, wada_hbm, b_ref, hin_ref, hout_ref, pos_hbm, f_ref,
             first_ref, mod_ref, win_ref, wout_ref, cos_hbm, sin_hbm,
             mod_blk, cos_ref, sin_ref, wada_ref, pos_ref, table_sems, local_sems, *sems):
        fetch_w = pltpu.make_async_copy(wada_hbm, wada_ref, local_sems.at[0])
        fetch_p = pltpu.make_async_copy(pos_hbm, pos_ref, local_sems.at[1])
        fetch_w.start()
        fetch_p.start()
        g_c = _Gather(cw_ref, first_ref, *sems[0:3])
        half_lanes = hin_ref.shape[1]
        g_in = _Gather(hin_ref, win_ref, *sems[3:6],
                       slab=lambda ref, px, py, pc: ref.at[2 * px + py, :, pl.ds(pl.multiple_of(pc * half_lanes, half_lanes), half_lanes)])
        g_out = _Gather(hout_ref, wout_ref, *sems[6:9])
        g_mod = _Gather(mod_blk, mod_ref, *sems[9:12])
        g_c.start()
        g_in.start()
        g_out.start()
        g_c.relay()
        g_c.finish()
        c_rows = [jnp.concatenate([first_ref[d, r:r + 1, :] for r in range(8)], axis=1) for d in range(8)]
        c_all = jnp.concatenate(c_rows, axis=0)
        sc = (c_all * _sigmoid(c_all)).astype(BF)
        fetch_w.wait()
        mod_blk[...] = _dot(sc, wada_ref[...].astype(BF)) + b_ref[...]
        g_mod.start()
        fetch_p.wait()

        def rope_rows(i, carry):
            rows = pl.ds(pl.multiple_of(i * rt, rt), rt)
            ang = pos_ref[rows, :].astype(F32) * f_ref[...]
            lane = lax.broadcasted_iota(jnp.int32, ang.shape, 1)
            cos_ref[rows, :] = jnp.cos(ang)
            sn = jnp.sin(ang)
            sin_ref[rows, :] = jnp.where((lane % 64) < 32, -sn, sn)
            pltpu.make_async_copy(cos_ref.at[rows, :], cos_hbm.at[rows, :], table_sems.at[0]).start()
            pltpu.make_async_copy(sin_ref.at[rows, :], sin_hbm.at[rows, :], table_sems.at[1]).start()
            return carry

        steps = s // rt
        lax.fori_loop(0, steps // 2, rope_rows, 0)
        g_in.pass_on()
        g_out.pass_on()
        lax.fori_loop(steps // 2, steps, rope_rows, 0)
        g_in.relay_diagonal()
        g_out.relay_diagonal()
        g_mod.relay()
        g_in.finish()
        g_out.finish()
        g_mod.finish()
        pltpu.make_async_copy(cos_ref, cos_hbm, table_sems.at[0]).wait()
        pltpu.make_async_copy(sin_ref, sin_hbm, table_sems.at[1]).wait()

    vm = pl.BlockSpec(memory_space=pltpu.VMEM)
    hbm = pl.BlockSpec(memory_space=pl.ANY)
    return pl.pallas_call(
        body, name="prologue",
        out_shape=[jax.ShapeDtypeStruct((8,) + cw.shape, F32), jax.ShapeDtypeStruct((8, 8, w_ada.shape[1]), F32),
                   jax.ShapeDtypeStruct((4, half_in.shape[0], 2 * half_in.shape[1]), half_in.dtype),
                   jax.ShapeDtypeStruct((8,) + half_out.shape, half_out.dtype),
                   jax.ShapeDtypeStruct((s, LANES), F32), jax.ShapeDtypeStruct((s, LANES), F32)],
        in_specs=[vm, hbm, vm, hbm, hbm, hbm, vm], out_specs=[vm, vm, hbm, hbm, hbm, hbm],
        scratch_shapes=[pltpu.VMEM((8, w_ada.shape[1]), F32), pltpu.VMEM((s, LANES), F32), pltpu.VMEM((s, LANES), F32),
                        pltpu.VMEM(w_ada.shape, F32), pltpu.VMEM(pos_col.shape, jnp.int32),
                        pltpu.SemaphoreType.DMA((2,)), pltpu.SemaphoreType.DMA((2,))] + _GATHER_SEMS * 4,
        compiler_params=pltpu.CompilerParams(vmem_limit_bytes=VMEM_LIMIT),
    )(cw, w_ada, b_shard, half_in, half_out, pos_col, inv_freq)


def _reduce_scratch(rr, cc):
    c2 = cc // 2
    return [pltpu.VMEM((4, rr, c2), F32), pltpu.VMEM((4, rr, c2), F32), pltpu.VMEM((3, rr, c2), BF),
            pltpu.VMEM((2, rr, c2), BF), pltpu.VMEM((rr, c2), BF), pltpu.VMEM((rr, c2), F32),
            pltpu.SemaphoreType.DMA((8,)), pltpu.SemaphoreType.DMA((8,)), pltpu.SemaphoreType.DMA((5,))]


class _Reduce:
    def __init__(self, p_hbm, out_ref, acc_ref, own_ref, send_ref, land_ref, relay_ref, res_ref,
                 send_sems, recv_sems, local_sems):
        x, y, c = lax.axis_index("x"), lax.axis_index("y"), lax.axis_index("c")
        c2 = out_ref.shape[1] // 2
        sibling = (x, y, 1 - c)
        first = (lax.rem(x + 1 - c, 2), lax.rem(y + c, 2))
        second = (lax.rem(x + c, 2), lax.rem(y + 1 - c, 2))
        shards = [2 * first[0] + first[1], 2 * second[0] + second[1], 2 * (1 - x) + (1 - y), 2 * x + y]
        sibling_slot = (1, 0, 2, 3)
        mine = pl.ds(pl.multiple_of(c * c2, c2), c2)
        other = pl.ds(pl.multiple_of((1 - c) * c2, c2), c2)
        self.acc_ref, self.own_ref, self.send_ref, self.land_ref = acc_ref, own_ref, send_ref, land_ref
        self.relay_ref, self.res_ref = relay_ref, res_ref
        self.own = [pltpu.make_async_copy(p_hbm.at[j, :, mine], own_ref.at[k], local_sems.at[k])
                    for k, j in enumerate(shards)]
        self.swap_out = [pltpu.make_async_remote_copy(
            src_ref=p_hbm.at[j, :, other], dst_ref=acc_ref.at[sibling_slot[k]], send_sem=send_sems.at[k],
            recv_sem=recv_sems.at[sibling_slot[k]], device_id=sibling, device_id_type=MESH) for k, j in enumerate(shards)]
        self.swap_in = [pltpu.make_async_remote_copy(
            src_ref=p_hbm.at[j, :, other], dst_ref=acc_ref.at[k], send_sem=send_sems.at[k], recv_sem=recv_sems.at[k],
            device_id=sibling, device_id_type=MESH) for k, j in enumerate(shards)]

        def message(k, src, dst, to):
            return pltpu.make_async_remote_copy(src_ref=src, dst_ref=dst, send_sem=send_sems.at[k], recv_sem=recv_sems.at[k],
                                                device_id=(*to, c), device_id_type=MESH)

        self.direct = message(4, send_ref.at[0], land_ref.at[0], first)
        self.passed = message(5, send_ref.at[1], relay_ref, first)
        self.joint = message(6, send_ref.at[2], land_ref.at[1], second)
        self.put = pltpu.make_async_copy(res_ref, out_ref.at[:, mine], local_sems.at[4])
        self.share = pltpu.make_async_remote_copy(
            src_ref=res_ref, dst_ref=out_ref.at[:, mine], send_sem=send_sems.at[7],
            recv_sem=recv_sems.at[7], device_id=sibling, device_id_type=MESH)

    def start(self):
        for k in (0, 2, 1, 3):
            self.own[k].start()
            self.swap_out[k].start()

    def _combine(self, k):
        self.own[k].wait()
        self.swap_out[k].wait_send()
        self.swap_in[k].wait_recv()
        self.acc_ref[k] = self.acc_ref[k] + self.own_ref[k]

    def combine_and_send(self):
        dt = self.send_ref.dtype
        self._combine(0)
        self.send_ref[0] = self.acc_ref[0].astype(dt)
        self.direct.start()
        self._combine(2)
        self.send_ref[1] = self.acc_ref[2].astype(dt)
        self.passed.start()
        self._combine(1)
        self.passed.wait_recv()
        self.send_ref[2] = (self.acc_ref[1] + self.relay_ref[...].astype(F32)).astype(dt)
        self.joint.start()
        self._combine(3)

    def total_and_share(self):
        self.direct.wait_recv()
        self.joint.wait_recv()
        self.res_ref[...] = self.acc_ref[3] + self.land_ref[0].astype(F32) + self.land_ref[1].astype(F32)
        for cp in (self.direct, self.passed, self.joint):
            cp.wait_send()
        self.put.start()
        self.share.start()

    def finish(self):
        self.put.wait()
        self.share.wait()


def _epilogue(dw_in_parts, dw_out_parts, small):
    _, r_in, cc = dw_in_parts.shape
    _, r_out, _ = dw_out_parts.shape
    n_red = len(_reduce_scratch(r_in, cc))

    def body(pin_hbm, pout_hbm, small_ref, gin_ref, gout_ref, small_all_ref, *scratch):
        red_in = _Reduce(pin_hbm, gin_ref, *scratch[0:n_red])
        red_out = _Reduce(pout_hbm, gout_ref, *scratch[n_red:2 * n_red])
        gat = _Gather(small_ref, small_all_ref, *scratch[2 * n_red:])
        red_out.start()
        red_in.start()
        gat.start()
        red_out.combine_and_send()
        red_in.combine_and_send()
        gat.relay()
        red_out.total_and_share()
        red_in.total_and_share()
        gat.finish()
        red_out.finish()
        red_in.finish()

    vm = pl.BlockSpec(memory_space=pltpu.VMEM)
    anyspec = pl.BlockSpec(memory_space=pl.ANY)
    return pl.pallas_call(
        body, name="epilogue",
        out_shape=[jax.ShapeDtypeStruct((r_in, cc), F32), jax.ShapeDtypeStruct((r_out, cc), F32),
                   jax.ShapeDtypeStruct((8,) + small.shape, F32)],
        in_specs=[anyspec, anyspec, vm], out_specs=[anyspec, anyspec, vm],
        scratch_shapes=_reduce_scratch(r_in, cc) + _reduce_scratch(r_out, cc) + _GATHER_SEMS,
        compiler_params=pltpu.CompilerParams(vmem_limit_bytes=VMEM_LIMIT),
    )(dw_in_parts, dw_out_parts, small)


def _rope(t, cosb, sinb, first_half):
    partner = jnp.where(first_half, pltpu.roll(t, 96, 1), pltpu.roll(t, 32, 1))
    return t * cosb + partner * sinb


def _rope_t(g, cosb, sinb, first_half):
    gs = g * sinb
    partner = jnp.where(first_half, pltpu.roll(gs, 96, 1), pltpu.roll(gs, 32, 1))
    return g * cosb + partner


def _modnorm(x, g, sc1p, shift):
    r = lax.rsqrt(jnp.mean(x * x, axis=-1, keepdims=True) + RMS_EPS)
    xn = x * r
    return xn, r, (xn * g) * sc1p + shift


def _inproj_fwd(x2d, shift, sc1p, g_norm, wpad_t):
    s = x2d.shape[0]
    tm = min(1024, s)

    def body(x_ref, sh_ref, sc_ref, g_ref, w_ref, o_ref):
        subs = _subtiles(tm)
        hs = [_modnorm(x_ref[sl, :], g_ref[...], sc_ref[...], sh_ref[...])[2].astype(BF) for sl in subs]
        for sl, h in zip(subs, hs):
            o_ref[sl, :] = _dot(h, w_ref[...], NT)

    vec = _full((1, D_MODEL))
    return pl.pallas_call(
        body, name="inproj_fwd", grid=(s // tm,),
        in_specs=[pl.BlockSpec((tm, D_MODEL), lambda i: (i, 0)), vec, vec, vec, _full((D_PAD, D_MODEL))],
        out_specs=pl.BlockSpec((tm, D_PAD), lambda i: (i, 0)),
        out_shape=jax.ShapeDtypeStruct((s, D_PAD), F32),
        compiler_params=_params(("arbitrary",)),
    )(x2d, shift, sc1p, g_norm, wpad_t)


def _split3(a):
    hi = a.astype(BF)
    r1 = a - hi.astype(F32)
    mid = r1.astype(BF)
    lo = (r1 - mid.astype(F32)).astype(BF)
    return hi, mid, lo


def _tri_matmul(tri, a):
    hi, mid, lo = _split3(a)
    return _dot(tri, hi) + _dot(tri, mid) + _dot(tri, lo)


def _chunks(tb):
    return [slice(c * GLA_CHUNK, (c + 1) * GLA_CHUNK) for c in range(tb // GLA_CHUNK)]


def _per_chunk_rows(rows, width):
    return jnp.concatenate([jnp.broadcast_to(r, (GLA_CHUNK, width)) for r in rows], axis=0)


def _gla_triangle(tb):
    row = lax.broadcasted_iota(jnp.int32, (tb, tb), 0)
    col = lax.broadcasted_iota(jnp.int32, (tb, tb), 1)
    return (((row // GLA_CHUNK) == (col // GLA_CHUNK)) & (col <= row)).astype(F32)


def _lane_mean(x, ones_b):
    hi = x.astype(BF)
    lo = (x - hi.astype(F32)).astype(BF)
    return (_dot(hi, ones_b) + _dot(lo, ones_b)) * (1.0 / LANES)


def _head(t, h, lo_h):
    blk = t[:, LANES * (h // 2):LANES * (h // 2 + 1)]
    return jnp.where(lo_h, blk, 0.0) if h % 2 == 0 else jnp.where(lo_h, 0.0, blk)


def _gla_block_common(qk, ga, wd, bd, tril_b):
    tb = qk.shape[0]
    q, k = qk[:, :256], qk[:, 256:]
    z = _dot(ga.astype(BF), wd) + bd
    la = (jnp.minimum(z, 0.0) - jnp.log(1.0 + jnp.exp(-jnp.abs(z)))) * (1.0 / GLA_TAU)
    b = _tri_matmul(tril_b, la)
    bls = [b[rs.stop - 1:rs.stop, :] for rs in _chunks(tb)]
    eq = jnp.exp(b)
    ek = jnp.exp(-b)
    f = jnp.exp(_per_chunk_rows(bls, 256) - b)
    return z, eq, ek, f, q * (eq * GLA_DK ** -0.5), k * ek, k * f, bls


def _gla_units(s, rows):
    sub = min(GLA_SUB, s)
    tb = min(rows, s)
    subs = [slice(i * sub, (i + 1) * sub) for i in range(tb // sub)]
    units = [(i, h) for i in range(len(subs)) for h in range(GLA_HEADS)]
    return tb, sub, subs, units


def _gla_fwd(proj, wdecp, bdec, ggla):
    s = proj.shape[0]
    tb, sub, subs, units = _gla_units(s, GLA_ROWS_FWD)
    nch = sub // GLA_CHUNK

    def body(qk_ref, v_ref, gz_ref, ga_ref, wd_ref, bd_ref, gg_ref, tri_ref, og_ref, opre_ref, sprev_ref, st_ref):
        @pl.when(pl.program_id(0) == 0)
        def _():
            st_ref[...] = jnp.zeros_like(st_ref)

        lo_h = lax.broadcasted_iota(jnp.int32, (sub, LANES), 1) < GLA_DK
        tril = tri_ref[...] > 0.5
        tril_b = tri_ref[...].astype(BF)
        ones_b = jnp.ones((LANES, LANES), BF)
        gg, wd, bd = gg_ref[...], wd_ref[...], bd_ref[...]
        chunks = _chunks(sub)
        lanes = [slice(h * LANES, (h + 1) * LANES) for h in range(GLA_HEADS)]
        com = [_gla_block_common(qk_ref[sl, :], ga_ref[sl, :], wd, bd, tril_b) for sl in subs]
        decs = [[jnp.exp(bl) for bl in cm[7]] for cm in com]
        a = {(i, h): _head(com[i][4], h, lo_h).astype(BF) for i, h in units}
        bm = {(i, h): _head(com[i][5], h, lo_h).astype(BF) for i, h in units}
        ktl = {(i, h): _head(com[i][6], h, lo_h).astype(BF) for i, h in units}
        vh = {(i, h): v_ref[subs[i], lanes[h]].astype(BF) for i, h in units}
        sc = {u: _dot(a[u], bm[u], NT) for u in units}
        upd = {u: [_dot(vh[u][rs], ktl[u][rs], TN) for rs in chunks] for u in units}
        p = {u: jnp.where(tril, sc[u], 0.0).astype(BF) for u in units}
        o = {u: _dot(p[u], vh[u]) for u in units}
        states = {}
        for h in range(GLA_HEADS):
            st = st_ref[h]
            for i in range(len(subs)):
                entering = []
                for c in range(nch):
                    entering.append(st)
                    sprev_ref[i * nch + c, h] = st
                    st = st * decs[i][c][:, LANES * (h // 2):LANES * (h // 2 + 1)] + upd[(i, h)][c]
                states[(i, h)] = entering
            st_ref[h] = st
        inter = {u: [_dot(a[u][rs], states[u][c].astype(BF), NT) for c, rs in enumerate(chunks)] for u in units}
        o = {u: o[u] + jnp.concatenate(inter[u], axis=0) for u in units}
        ms = {u: _lane_mean(o[u] * o[u], ones_b) for u in units}
        for i, h in units:
            gzh = gz_ref[subs[i], lanes[h]]
            opre_ref[subs[i], lanes[h]] = o[(i, h)]
            og_ref[subs[i], lanes[h]] = (((o[(i, h)] * lax.rsqrt(ms[(i, h)] + RMS_EPS)) * gg[:, lanes[h]])
                                         * (gzh * _sigmoid(gzh))).astype(og_ref.dtype)

    def col(width, off):
        return pl.BlockSpec((tb, width), lambda i: (i, off // width))

    return pl.pallas_call(
        body, name="gla_fwd", grid=(s // tb,),
        in_specs=[col(512, OFF_QK), col(512, OFF_V), col(512, OFF_GZ), col(LANES, OFF_GA),
                  _full((LANES, 256)), _full((1, 256)), _full((1, 512)), _full((sub, sub))],
        out_specs=[pl.BlockSpec((tb, 512), lambda i: (i, 0)), pl.BlockSpec((tb, 512), lambda i: (i, 0)),
                   pl.BlockSpec((tb // GLA_CHUNK, GLA_HEADS, LANES, LANES), lambda i: (i, 0, 0, 0))],
        out_shape=[jax.ShapeDtypeStruct((s, 512), BF), jax.ShapeDtypeStruct((s, 512), F32),
                   jax.ShapeDtypeStruct((s // GLA_CHUNK, GLA_HEADS, LANES, LANES), F32)],
        scratch_shapes=[pltpu.VMEM((GLA_HEADS, LANES, LANES), F32)],
        compiler_params=_params(("arbitrary",)),
    )(proj, proj, proj, proj, wdecp, bdec, ggla, _gla_triangle(sub))


def _gla_bwd(proj, dog, opre, sprev, wdecp, bdec, ggla):
    s = proj.shape[0]
    tb, sub, subs, units = _gla_units(s, GLA_ROWS_BWD)
    nsub = len(subs)
    nch = sub // GLA_CHUNK
    nb = s // tb

    def body(qk_ref, v_ref, gz_ref, ga_ref, dog_ref, opre_ref, sprev_ref, wd_ref, bd_ref, gg_ref, tri_ref, triu_ref,
             dqk_ref, dv_ref, dgz_ref, dga_ref, dwd_ref, dbd_ref, dgg_ref, dst_ref):
        @pl.when(pl.program_id(0) == 0)
        def _():
            dst_ref[...] = jnp.zeros_like(dst_ref)
            dwd_ref[...] = jnp.zeros_like(dwd_ref)
            dbd_ref[...] = jnp.zeros_like(dbd_ref)
            dgg_ref[...] = jnp.zeros_like(dgg_ref)

        lo_h = lax.broadcasted_iota(jnp.int32, (sub, LANES), 1) < GLA_DK
        tril = tri_ref[...] > 0.5
        tril_b = tri_ref[...].astype(BF)
        triu_b = triu_ref[...].astype(BF)
        ones_b = jnp.ones((LANES, LANES), BF)
        last_row = (lax.broadcasted_iota(jnp.int32, (sub, LANES), 0) % GLA_CHUNK) == GLA_CHUNK - 1
        wd, gg, bd = wd_ref[...], gg_ref[...], bd_ref[...]
        chunks = _chunks(sub)
        lanes = [slice(h * LANES, (h + 1) * LANES) for h in range(GLA_HEADS)]
        blks = [slice(LANES * (h // 2), LANES * (h // 2 + 1)) for h in range(GLA_HEADS)]
        ga = [ga_ref[sl, :] for sl in subs]
        com = [_gla_block_common(qk_ref[sl, :], ga[i], wd, bd, tril_b) for i, sl in enumerate(subs)]
        decs = [[jnp.exp(bl) for bl in cm[7]] for cm in com]
        a = {(i, h): _head(com[i][4], h, lo_h).astype(BF) for i, h in units}
        bm = {(i, h): _head(com[i][5], h, lo_h).astype(BF) for i, h in units}
        ktl = {(i, h): _head(com[i][6], h, lo_h).astype(BF) for i, h in units}
        vh = {(i, h): v_ref[subs[i], lanes[h]].astype(BF) for i, h in units}
        sc = {u: _dot(a[u], bm[u], NT) for u in units}

        o = {(i, h): opre_ref[subs[i], lanes[h]] for i, h in units}
        ms = {u: _lane_mean(o[u] * o[u], ones_b) for u in units}
        gz = {(i, h): gz_ref[subs[i], lanes[h]] for i, h in units}
        dog = {(i, h): dog_ref[subs[i], lanes[h]] for i, h in units}
        sg = {u: _sigmoid(gz[u]) for u in units}
        r = {u: lax.rsqrt(ms[u] + RMS_EPS) for u in units}
        ohat = {u: o[u] * r[u] for u in units}
        sil = {u: gz[u] * sg[u] for u in units}
        for i, h in units:
            u = (i, h)
            dgz_ref[subs[i], lanes[h]] = (dog[u] * (ohat[u] * gg[:, lanes[h]])
                                          * (sg[u] * (1.0 + gz[u] * (1.0 - sg[u])))).astype(dgz_ref.dtype)
            dgg_ref[:, lanes[h]] += jnp.sum(dog[u] * sil[u] * ohat[u], axis=0, keepdims=True)
        dn = {(i, h): dog[(i, h)] * sil[(i, h)] * gg[:, lanes[h]] for i, h in units}
        mdn = {u: _lane_mean(dn[u] * ohat[u], ones_b) for u in units}
        do = {u: (r[u] * (dn[u] - ohat[u] * mdn[u])).astype(BF) for u in units}

        p = {u: jnp.where(tril, sc[u], 0.0).astype(BF) for u in units}
        dpr = {u: _dot(do[u], vh[u], NT) for u in units}
        incr = {u: [_dot(do[u][rs], a[u][rs], TN) for rs in chunks] for u in units}
        dv = {u: _dot(p[u], do[u], TN) for u in units}
        dp = {u: jnp.where(tril, dpr[u], 0.0).astype(BF) for u in units}
        dqd = {u: _dot(dp[u], bm[u]) for u in units}
        dkd = {u: _dot(dp[u], a[u], TN) for u in units}
        st = {(i, h): [sprev_ref[i * nch + c, h] for c in range(nch)] for i, h in units}
        leaving = {}
        for h in range(GLA_HEADS):
            d = dst_ref[h]
            for i in reversed(range(nsub)):
                out = [None] * nch
                for c in reversed(range(nch)):
                    out[c] = d
                    d = d * decs[i][c][:, blks[h]] + incr[(i, h)][c]
                leaving[(i, h)] = out
            dst_ref[h] = d
        lv_b = {u: [leaving[u][c].astype(BF) for c in range(nch)] for u in units}
        dv_s = {u: [_dot(ktl[u][rs], lv_b[u][c], NT) for c, rs in enumerate(chunks)] for u in units}
        dqd_s = {u: [_dot(do[u][rs], st[u][c].astype(BF)) for c, rs in enumerate(chunks)] for u in units}
        dkt_s = {u: [_dot(vh[u][rs], lv_b[u][c]) for c, rs in enumerate(chunks)] for u in units}
        ddec = {u: [jnp.sum(leaving[u][c] * st[u][c], axis=0, keepdims=True) for c in range(nch)] for u in units}
        for i, h in units:
            dv_ref[subs[i], lanes[h]] = (dv[(i, h)] + jnp.concatenate(dv_s[(i, h)], axis=0)).astype(dv_ref.dtype)
        dqd = {u: dqd[u] + jnp.concatenate(dqd_s[u], axis=0) for u in units}
        dkt = {u: jnp.concatenate(dkt_s[u], axis=0) for u in units}

        db = []
        for i, sl in enumerate(subs):
            _, eq, ek, f, qd, kd, kt, _ = com[i]
            parts = []
            for pair in range(GLA_HEADS // 2):
                blk, u0, u1 = blks[2 * pair], (i, 2 * pair), (i, 2 * pair + 1)
                dqd_b, dkd_b, dkt_b = dqd[u0] + dqd[u1], dkd[u0] + dkd[u1], dkt[u0] + dkt[u1]
                dqk_ref[sl, blk] = (dqd_b * (eq[:, blk] * GLA_DK ** -0.5)).astype(dqk_ref.dtype)
                dqk_ref[sl, 256 + LANES * pair:256 + LANES * (pair + 1)] = (dkd_b * ek[:, blk] + dkt_b * f[:, blk]).astype(dqk_ref.dtype)
                dkt_kt = dkt_b * kt[:, blk]
                dbp = dqd_b * qd[:, blk] - dkd_b * kd[:, blk] - dkt_kt
                dbl = [jnp.sum(dkt_kt[rs], axis=0, keepdims=True) + (ddec[u0][c] + ddec[u1][c]) * decs[i][c][:, blk]
                       for c, rs in enumerate(chunks)]
                parts.append(jnp.where(last_row, dbp + _per_chunk_rows(dbl, LANES), dbp))
            db.append(jnp.concatenate(parts, axis=1))
        dla = [_tri_matmul(triu_b, db[i]) for i in range(nsub)]
        dz32 = [dla[i] * (1.0 / GLA_TAU) * _sigmoid(-com[i][0]) for i in range(nsub)]
        dz = [t.astype(BF) for t in dz32]
        for i, sl in enumerate(subs):
            dga_ref[sl, :] = _dot(dz[i], wd, NT).astype(dga_ref.dtype)
            dwd_ref[...] += _dot(ga[i].astype(BF), dz[i], TN)
            dbd_ref[...] += jnp.sum(dz32[i], axis=0, keepdims=True)

    def col(width, off):
        return pl.BlockSpec((tb, width), lambda i: (nb - 1 - i, off // width))

    def rev(width):
        return pl.BlockSpec((tb, width), lambda i: (nb - 1 - i, 0))

    return pl.pallas_call(
        body, name="gla_bwd", grid=(nb,),
        in_specs=[col(512, OFF_QK), col(512, OFF_V), col(512, OFF_GZ), col(LANES, OFF_GA), rev(512), rev(512),
                  pl.BlockSpec((tb // GLA_CHUNK, GLA_HEADS, LANES, LANES), lambda i: (nb - 1 - i, 0, 0, 0)),
                  _full((LANES, 256)), _full((1, 256)), _full((1, 512)), _full((sub, sub)), _full((sub, sub))],
        out_specs=[rev(512), rev(512), rev(512), rev(LANES), _full((LANES, 256)), _full((1, 256)), _full((1, 512))],
        out_shape=[jax.ShapeDtypeStruct((s, 512), BF), jax.ShapeDtypeStruct((s, 512), BF),
                   jax.ShapeDtypeStruct((s, 512), BF), jax.ShapeDtypeStruct((s, LANES), BF),
                   jax.ShapeDtypeStruct((LANES, 256), F32), jax.ShapeDtypeStruct((1, 256), F32),
                   jax.ShapeDtypeStruct((1, 512), F32)],
        scratch_shapes=[pltpu.VMEM((GLA_HEADS, LANES, LANES), F32)],
        compiler_params=_params(("arbitrary",)),
    )(proj, proj, proj, proj, dog, opre, sprev, wdecp, bdec, ggla, _gla_triangle(sub), _gla_triangle(sub).T)


_SWA_COL_HEADS = (0, 2, 1, 3, 4, 6, 5, 7)
_SWA_COLS = SWA_HEADS * SWA_BLOCK


def _swa_masks():
    lo2 = lax.broadcasted_iota(jnp.int32, (2 * SWA_BLOCK, LANES), 1) < 64
    lane1 = lax.broadcasted_iota(jnp.int32, (SWA_BLOCK, LANES), 1)
    first_half = (lane1 % 64) < 32
    key = lax.broadcasted_iota(jnp.int32, (SWA_BLOCK, _SWA_COLS), 0)
    query = lax.broadcasted_iota(jnp.int32, (SWA_BLOCK, _SWA_COLS), 1) % SWA_BLOCK
    return lo2, lane1 < 64, first_half, key > query


def _merge_band(t, prev_mask, prev_bias=None):
    prev = t[:SWA_BLOCK] if prev_bias is None else t[:SWA_BLOCK] + prev_bias
    return jnp.where(prev_mask, prev, t[SWA_BLOCK:])


def _split_band(t, prev_mask_b):
    prev = t * prev_mask_b
    return jnp.concatenate([prev, t - prev], axis=0)


def _kv_variants(t, lo2):
    tr = pltpu.roll(t, 64, 1)
    lo_v = [jnp.where(lo2, t, 0.0).astype(BF), jnp.where(lo2, tr, 0.0).astype(BF)]
    hi_v = [jnp.where(lo2, 0.0, tr).astype(BF), jnp.where(lo2, 0.0, t).astype(BF)]
    return lo_v, hi_v


def _kv_variants_t(t):
    tt = t.T
    sw = jnp.concatenate([tt[64:], tt[:64]], axis=0)
    top = lax.broadcasted_iota(jnp.int32, tt.shape, 0) < 64
    lo_v = [jnp.where(top, tt, 0.0).astype(BF), jnp.where(top, sw, 0.0).astype(BF)]
    hi_v = [jnp.where(top, 0.0, sw).astype(BF), jnp.where(top, 0.0, tt).astype(BF)]
    return lo_v, hi_v


def _swa_scores(qg, k_lo, k_hi):
    return jnp.concatenate([_dot(k_lo[0], qg[0], NT), _dot(k_hi[0], qg[0], NT),
                            _dot(k_lo[1], qg[1], NT), _dot(k_hi[1], qg[1], NT)], axis=1)


def _sink_row(sinks_ref):
    return jnp.concatenate([jnp.full((1, SWA_BLOCK), sinks_ref[0, hd], F32) for hd in _SWA_COL_HEADS], axis=1)


def _swa_softmax(st, prev_mask, prev_bias, sink):
    st = _merge_band(st, prev_mask, prev_bias)
    m = jnp.maximum(jnp.max(st, axis=0, keepdims=True), sink)
    ex = jnp.exp(st - m)
    es = jnp.exp(sink - m)
    inv = 1.0 / (jnp.sum(ex, axis=0, keepdims=True) + es)
    return ex, es, inv


def _no_prev_bias(block_index):
    return jnp.where(block_index > 0, 0.0, -1e30).astype(F32)


def _swa_queries(sq_ref, rows, cosb, sinb, first_half):
    qs = [_rope(sq_ref[rows, p * LANES:(p + 1) * LANES], cosb, sinb, first_half) * 0.125 for p in range(4)]
    return [jnp.concatenate(qs[0:2], axis=0), jnp.concatenate(qs[2:4], axis=0)]


def _swa_fwd(proj, cos, sin, sinks):
    s = proj.shape[0]
    nq = min(SWA_QBLOCKS_FWD, s // SWA_BLOCK)
    tq = nq * SWA_BLOCK

    def body(sq_ref, sz_ref, sk_ref, sv_ref, cos_ref, sin_ref, sinks_ref, os_ref, opre_ref, kprev, vprev):
        n = pl.program_id(0)

        @pl.when(n == 0)
        def _():
            kprev[...] = jnp.zeros_like(kprev)
            vprev[...] = jnp.zeros_like(vprev)

        lo2, _, first_half, prev_mask = _swa_masks()
        prev_mask_b = jnp.where(prev_mask, 1.0, 0.0).astype(BF)
        sink = _sink_row(sinks_ref)
        blocks = range(nq)
        rows = [slice(j * SWA_BLOCK, (j + 1) * SWA_BLOCK) for j in blocks]
        cosb = [cos_ref[rows[j], :] for j in blocks]
        sinb = [sin_ref[rows[j], :] for j in blocks]
        kc = [_rope(sk_ref[rows[j], :], cosb[j], sinb[j], first_half) for j in blocks]
        vc = [sv_ref[rows[j], :] for j in blocks]
        kcat = [jnp.concatenate([kprev[...] if j == 0 else kc[j - 1], kc[j]], axis=0) for j in blocks]
        vcat = [jnp.concatenate([vprev[...] if j == 0 else vc[j - 1], vc[j]], axis=0) for j in blocks]
        kprev[...] = kc[-1]
        vprev[...] = vc[-1]
        kvar = [_kv_variants(kcat[j], lo2) for j in blocks]
        vtvar = [_kv_variants_t(vcat[j]) for j in blocks]
        qg = [[q.astype(BF) for q in _swa_queries(sq_ref, rows[j], cosb[j], sinb[j], first_half)] for j in blocks]
        st = [_swa_scores(qg[j], *kvar[j]) for j in blocks]
        soft = [_swa_softmax(st[j], prev_mask, _no_prev_bias(n) if j == 0 else None, sink) for j in blocks]
        pt = [_split_band(soft[j][0].astype(BF), prev_mask_b) for j in blocks]
        og = {}
        for j in blocks:
            inv = soft[j][2]
            for g in range(2):
                c0, c1, c2 = 512 * g, 512 * g + 256, 512 * g + 512
                ot = (_dot(vtvar[j][0][g], pt[j][:, c0:c1]) * inv[:, c0:c1]
                      + _dot(vtvar[j][1][g], pt[j][:, c1:c2]) * inv[:, c1:c2])
                og[(j, g)] = ot.T
        for j in blocks:
            for g in range(2):
                for i in range(2):
                    ls = slice((2 * g + i) * LANES, (2 * g + i + 1) * LANES)
                    o = og[(j, g)][i * SWA_BLOCK:(i + 1) * SWA_BLOCK]
                    sz = sz_ref[rows[j], ls]
                    opre_ref[rows[j], ls] = o
                    os_ref[rows[j], ls] = (o * (sz * _sigmoid(sz))).astype(os_ref.dtype)

    def col(width, off):
        return pl.BlockSpec((tq, width), lambda i: (i, off // width))

    row = pl.BlockSpec((tq, LANES), lambda i: (i, 0))
    return pl.pallas_call(
        body, name="swa_fwd", grid=(s // tq,),
        in_specs=[col(512, OFF_SQ), col(512, OFF_SZ), col(LANES, OFF_SK), col(LANES, OFF_SV), row, row,
                  pl.BlockSpec(memory_space=pltpu.SMEM)],
        out_specs=[pl.BlockSpec((tq, 512), lambda i: (i, 0))] * 2,
        out_shape=[jax.ShapeDtypeStruct((s, 512), BF), jax.ShapeDtypeStruct((s, 512), F32)],
        scratch_shapes=[pltpu.VMEM((SWA_BLOCK, LANES), F32)] * 2,
        compiler_params=_params(("arbitrary",)),
    )(proj, proj, proj, proj, cos, sin, sinks)


def _swa_bwd(proj, dos, opre, cos, sin, sinks):
    s = proj.shape[0]
    nq = min(SWA_QBLOCKS, s // SWA_BLOCK)
    tq = nq * SWA_BLOCK

    def body(sq_ref, sz_ref, sk_ref, sv_ref, dos_ref, opre_ref, cos_ref, sin_ref, sinks_ref,
             dsq_ref, dsz_ref, dsk_ref, dsv_ref, dsink_ref, kprev, vprev, cprev, sprev):
        n = pl.program_id(0)

        @pl.when(n == 0)
        def _():
            kprev[...] = jnp.zeros_like(kprev)
            vprev[...] = jnp.zeros_like(vprev)
            cprev[...] = jnp.zeros_like(cprev)
            sprev[...] = jnp.zeros_like(sprev)
            for hd in range(SWA_HEADS):
                dsink_ref[0, hd] = 0.0

        lo2, lo1, first_half, prev_mask = _swa_masks()
        prev_mask_b = jnp.where(prev_mask, 1.0, 0.0).astype(BF)
        lo1s = jnp.concatenate([lo1, lo1], axis=0)
        sink = _sink_row(sinks_ref)

        def home(m0, m1):
            t0 = m0 + pltpu.roll(m0, 64, 1)
            t1 = m1 + pltpu.roll(m1, 64, 1)
            return jnp.where(lo2, t0, t1)

        kp, vp, cp_, sp_ = kprev[...], vprev[...], cprev[...], sprev[...]
        for j in range(nq):
            rows = slice(j * SWA_BLOCK, (j + 1) * SWA_BLOCK)
            blk = n * nq + j
            cosb, sinb = cos_ref[rows, :], sin_ref[rows, :]
            kc = _rope(sk_ref[rows, :], cosb, sinb, first_half)
            vc = sv_ref[rows, :]
            kcat = jnp.concatenate([kp, kc], axis=0)
            k_lo, k_hi = _kv_variants(kcat, lo2)
            kt_lo, kt_hi = _kv_variants_t(kcat)
            v_lo, v_hi = _kv_variants(jnp.concatenate([vp, vc], axis=0), lo2)
            qg32 = _swa_queries(sq_ref, rows, cosb, sinb, first_half)
            qg = [q.astype(BF) for q in qg32]
            ex, es, inv = _swa_softmax(_swa_scores(qg, k_lo, k_hi), prev_mask, _no_prev_bias(n) if j == 0 else None, sink)
            pr, ps = ex * inv, es * inv

            dog32 = []
            for g in range(2):
                parts = []
                for i in range(2):
                    ls = slice((2 * g + i) * LANES, (2 * g + i + 1) * LANES)
                    sz = sz_ref[rows, ls]
                    sg = _sigmoid(sz)
                    dos_p = dos_ref[rows, ls]
                    dsz_ref[rows, ls] = (dos_p * opre_ref[rows, ls] * (sg * (1.0 + sz * (1.0 - sg)))).astype(dsz_ref.dtype)
                    parts.append(dos_p * (sz * sg))
                dog32.append(jnp.concatenate(parts, axis=0))
            dog = [t.astype(BF) for t in dog32]
            dpr = _merge_band(jnp.concatenate([_dot(v_lo[0], dog[0], NT), _dot(v_hi[0], dog[0], NT),
                                               _dot(v_lo[1], dog[1], NT), _dot(v_hi[1], dog[1], NT)], axis=1), prev_mask)
            rd = jnp.sum(pr * dpr, axis=0, keepdims=True)
            ds = _split_band((pr * (dpr - rd)).astype(BF), prev_mask_b)
            prb = _split_band(pr.astype(BF), prev_mask_b)
            sink_term = ps * rd
            for r, hd in enumerate(_SWA_COL_HEADS):
                dsink_ref[0, hd] += -jnp.sum(sink_term[:, r * SWA_BLOCK:(r + 1) * SWA_BLOCK])

            dk_g, dv_g = [], []
            for g in range(2):
                c0, c1, c2 = 512 * g, 512 * g + 256, 512 * g + 512
                dq = (_dot(kt_lo[g], ds[:, c0:c1]) + _dot(kt_hi[g], ds[:, c1:c2])).T
                for i in range(2):
                    ls = slice((2 * g + i) * LANES, (2 * g + i + 1) * LANES)
                    dsq_ref[rows, ls] = _rope_t(dq[i * SWA_BLOCK:(i + 1) * SWA_BLOCK] * 0.125, cosb, sinb,
                                                first_half).astype(dsq_ref.dtype)
                q_split = jnp.concatenate([jnp.where(lo1s, qg32[g], 0.0), jnp.where(lo1s, 0.0, qg32[g])], axis=0).astype(BF)
                do_split = jnp.concatenate([jnp.where(lo1s, dog32[g], 0.0), jnp.where(lo1s, 0.0, dog32[g])], axis=0).astype(BF)
                dk_g.append(_dot(ds[:, c0:c2], q_split))
                dv_g.append(_dot(prb[:, c0:c2], do_split))
            dk = home(dk_g[0], dk_g[1])
            dv = home(dv_g[0], dv_g[1])
            cur = pl.ds(pl.multiple_of(blk * SWA_BLOCK, SWA_BLOCK), SWA_BLOCK)
            dsk_ref[cur, :] = _rope_t(dk[SWA_BLOCK:], cosb, sinb, first_half)
            dsv_ref[cur, :] = dv[SWA_BLOCK:]
            dk_prev = _rope_t(dk[:SWA_BLOCK], cp_, sp_, first_half)
            dv_prev = dv[:SWA_BLOCK]
            if j == 0:
                @pl.when(n > 0)
                def _():
                    prv = pl.ds(pl.multiple_of((blk - 1) * SWA_BLOCK, SWA_BLOCK), SWA_BLOCK)
                    dsk_ref[prv, :] += dk_prev
                    dsv_ref[prv, :] += dv_prev
            else:
                prv = pl.ds(pl.multiple_of((blk - 1) * SWA_BLOCK, SWA_BLOCK), SWA_BLOCK)
                dsk_ref[prv, :] += dk_prev
                dsv_ref[prv, :] += dv_prev
            kp, vp, cp_, sp_ = kc, vc, cosb, sinb
        kprev[...] = kp
        vprev[...] = vp
        cprev[...] = cp_
        sprev[...] = sp_

    def col(width, off):
        return pl.BlockSpec((tq, width), lambda i: (i, off // width))

    row = pl.BlockSpec((tq, LANES), lambda i: (i, 0))
    wide = pl.BlockSpec((tq, 512), lambda i: (i, 0))
    return pl.pallas_call(
        body, name="swa_bwd", grid=(s // tq,),
        in_specs=[col(512, OFF_SQ), col(512, OFF_SZ), col(LANES, OFF_SK), col(LANES, OFF_SV), wide, wide, row, row,
                  pl.BlockSpec(memory_space=pltpu.SMEM)],
        out_specs=[wide, wide, _full((s, LANES)), _full((s, LANES)), pl.BlockSpec(memory_space=pltpu.SMEM)],
        out_shape=[jax.ShapeDtypeStruct((s, 512), BF), jax.ShapeDtypeStruct((s, 512), BF),
                   jax.ShapeDtypeStruct((s, LANES), F32), jax.ShapeDtypeStruct((s, LANES), F32),
                   jax.ShapeDtypeStruct((1, SWA_HEADS), F32)],
        scratch_shapes=[pltpu.VMEM((SWA_BLOCK, LANES), F32)] * 4,
        compiler_params=_params(("arbitrary",)),
    )(proj, proj, proj, proj, dos, opre, cos, sin, sinks)


def _outproj(og, osw, w_out, x2d, target, gate, g_final):
    s = x2d.shape[0]
    tm = min(512, s)

    def body(og_ref, os_ref, w_ref, x_ref, t_ref, gate_ref, gf_ref,
             dx2_ref, dog_ref, dos_ref, dw_ref, loss_ref, dgf_ref, dgate_ref):
        @pl.when(pl.program_id(0) == 0)
        def _():
            dw_ref[...] = jnp.zeros_like(dw_ref)
            loss_ref[...] = jnp.zeros_like(loss_ref)
            dgf_ref[...] = jnp.zeros_like(dgf_ref)
            dgate_ref[...] = jnp.zeros_like(dgate_ref)

        w = w_ref[...]
        gate, gf = gate_ref[...], gf_ref[...]
        subs = _subtiles(tm)
        ogv = [og_ref[sl, :] for sl in subs]
        osv = [os_ref[sl, :] for sl in subs]
        y = [_dot(ogv[k], w[:512]) + _dot(osv[k], w[512:]) for k in range(len(subs))]
        dys = []
        for k, sl in enumerate(subs):
            x2 = x_ref[sl, :] + gate * y[k]
            r = lax.rsqrt(jnp.mean(x2 * x2, axis=-1, keepdims=True) + RMS_EPS)
            xn = x2 * r
            err = xn * gf - t_ref[sl, :]
            loss_ref[...] += 0.5 * jnp.sum(jnp.mean(err * err, axis=-1, keepdims=True), axis=0, keepdims=True)
            dyf = err * (1.0 / D_MODEL)
            dgf_ref[...] += jnp.sum(dyf * xn, axis=0, keepdims=True)
            t = dyf * gf
            dx2 = r * (t - xn * jnp.mean(t * xn, axis=-1, keepdims=True))
            dx2_ref[sl, :] = dx2
            dgate_ref[...] += jnp.sum(dx2 * y[k], axis=0, keepdims=True)
            dys.append((dx2 * gate).astype(BF))
            dmix = _dot(dys[k], w, NT)
            dog_ref[sl, :] = dmix[:, :512]
            dos_ref[sl, :] = dmix[:, 512:]
        dy = jnp.concatenate(dys, axis=0)
        dw_ref[:512, :] += _dot(og_ref[...], dy, TN)
        dw_ref[512:, :] += _dot(os_ref[...], dy, TN)

    half = pl.BlockSpec((tm, 512), lambda i: (i, 0))
    rowb = pl.BlockSpec((tm, D_MODEL), lambda i: (i, 0))
    vec = _full((1, D_MODEL))
    return pl.pallas_call(
        body, name="outproj", grid=(s // tm,),
        in_specs=[half, half, _full((D_MODEL, D_MODEL)), rowb, rowb, vec, vec],
        out_specs=[rowb, half, half, _full((D_MODEL, D_MODEL)), _full((1, 1)), vec, vec],
        out_shape=[jax.ShapeDtypeStruct((s, D_MODEL), F32), jax.ShapeDtypeStruct((s, 512), F32),
                   jax.ShapeDtypeStruct((s, 512), F32), jax.ShapeDtypeStruct((D_MODEL, D_MODEL), F32),
                   jax.ShapeDtypeStruct((1, 1), F32), jax.ShapeDtypeStruct((1, D_MODEL), F32),
                   jax.ShapeDtypeStruct((1, D_MODEL), F32)],
        compiler_params=_params(("arbitrary",)),
    )(og, osw, w_out, x2d, target, gate, g_final)


_PIECES = ((OFF_QK, 512), (OFF_V, 512), (OFF_GZ, 512), (OFF_SQ, 512), (OFF_SZ, 512),
           (OFF_SK, LANES), (OFF_SV, LANES), (OFF_GA, LANES))

_UNPAD_ROWS = ((OFF_QK, 0, 1024),
               (OFF_GA, 1024, GLA_RANK),
               (OFF_GZ, 1040, 1024),
               (OFF_SK, 2064, 256),
               (OFF_SZ, 2320, 512))


def _inproj_bwd(x2d, shift, sc1p, g_norm, wpad_t, dx2, pieces):
    s = x2d.shape[0]
    tm = min(512, s)
    nsteps = s // tm

    def body(x_ref, sh_ref, sc_ref, g_ref, w_hbm, dx2_ref, *rest):
        piece_refs = rest[:len(_PIECES)]
        gx_ref, dw_hbm, dsh_ref, dsc_ref, dg_ref, w_vm, dw_vm, sem, out_sems = rest[len(_PIECES):]
        i = pl.program_id(0)

        @pl.when(i == 0)
        def _():
            cp = pltpu.make_async_copy(w_hbm, w_vm, sem)
            cp.start()
            dw_vm[...] = jnp.zeros_like(dw_vm)
            dsh_ref[...] = jnp.zeros_like(dsh_ref)
            dsc_ref[...] = jnp.zeros_like(dsc_ref)
            dg_ref[...] = jnp.zeros_like(dg_ref)
            cp.wait()

        g, sc1p_v, shift_v = g_ref[...], sc_ref[...], sh_ref[...]
        subs = _subtiles(tm)
        dhs = []
        for sl in subs:
            dh = None
            for (off, width), pr in zip(_PIECES, piece_refs):
                part = _dot(pr[sl, :].astype(BF), w_vm[off:off + width, :])
                dh = part if dh is None else dh + part
            dhs.append(dh)
        norm = [_modnorm(x_ref[sl, :], g, sc1p_v, shift_v) for sl in subs]
        hb = jnp.concatenate([h.astype(BF) for _, _, h in norm], axis=0)
        for (off, width), pr in zip(_PIECES, piece_refs):
            dw_vm[off:off + width, :] += _dot(pr[...].astype(BF), hb, TN)
        for sl, (xn, r, _), dh in zip(subs, norm, dhs):
            dsh_ref[...] += jnp.sum(dh, axis=0, keepdims=True)
            dsc_ref[...] += jnp.sum(dh * (xn * g), axis=0, keepdims=True)
            dg_ref[...] += jnp.sum(dh * xn * sc1p_v, axis=0, keepdims=True)
            dxn = dh * g * sc1p_v
            gx_ref[sl, :] = dx2_ref[sl, :] + r * (dxn - xn * jnp.mean(dxn * xn, axis=-1, keepdims=True))

        @pl.when(i == nsteps - 1)
        def _():
            copies = [pltpu.make_async_copy(dw_vm.at[src:src + n], dw_hbm.at[dst:dst + n], out_sems.at[k])
                      for k, (src, dst, n) in enumerate(_UNPAD_ROWS)]
            for cp in copies:
                cp.start()
            for cp in copies:
                cp.wait()

    rowb = pl.BlockSpec((tm, D_MODEL), lambda i: (i, 0))
    vec = _full((1, D_MODEL))
    anyspec = pl.BlockSpec(memory_space=pl.ANY)
    piece_specs = [pl.BlockSpec((tm, width), lambda i: (i, 0)) for _, width in _PIECES]
    return pl.pallas_call(
        body, name="inproj_bwd", grid=(nsteps,),
        in_specs=[rowb, vec, vec, vec, anyspec, rowb] + piece_specs,
        out_specs=[rowb, anyspec, vec, vec, vec],
        out_shape=[jax.ShapeDtypeStruct((s, D_MODEL), F32), jax.ShapeDtypeStruct((D_IN, D_MODEL), F32),
                   jax.ShapeDtypeStruct((1, D_MODEL), F32), jax.ShapeDtypeStruct((1, D_MODEL), F32),
                   jax.ShapeDtypeStruct((1, D_MODEL), F32)],
        scratch_shapes=[pltpu.VMEM((D_PAD, D_MODEL), BF), pltpu.VMEM((D_PAD, D_MODEL), F32), pltpu.SemaphoreType.DMA,
                        pltpu.SemaphoreType.DMA((len(_UNPAD_ROWS),))],
        compiler_params=_params(("arbitrary",)),
    )(x2d, shift, sc1p, g_norm, wpad_t, dx2, *pieces)


def _adam(w, g, m, v):
    m2 = ADAM_B1 * m + (1.0 - ADAM_B1) * g
    v2 = ADAM_B2 * v + (1.0 - ADAM_B2) * (g * g)
    m_hat = m2 / (1.0 - ADAM_B1 ** ADAM_STEP)
    v_hat = v2 / (1.0 - ADAM_B2 ** ADAM_STEP)
    delta = -ADAM_LR * (m_hat / (jnp.sqrt(v_hat) + ADAM_EPS) + ADAM_WD * w)
    return delta, m2, v2


def _adamw(w, g, m, v, name):
    rr, cc = w.shape
    tc = min(256, cc)

    def body(w_ref, g_ref, m_ref, v_ref, d_ref, m2_ref, v2_ref):
        d_ref[...], m2_ref[...], v2_ref[...] = _adam(w_ref[...], g_ref[...], m_ref[...], v_ref[...])

    blk = pl.BlockSpec((rr, tc), lambda i: (0, i))
    return pl.pallas_call(
        body, name=name, grid=(cc // tc,), in_specs=[blk] * 4, out_specs=[blk] * 3,
        out_shape=[jax.ShapeDtypeStruct((rr, cc), F32)] * 3,
        compiler_params=_params(("arbitrary",)),
    )(w, g, m, v)


def _adamw_t(w3, g, m3, v3, name):
    rr, _, cc = w3.shape
    tc = cc

    def body(w_hbm, g_ref, m_hbm, v_hbm, d_hbm, m2_hbm, v2_hbm, g3_hbm, w_vm, m_vm, v_vm, d_vm, m2_vm, v2_vm, in_sems, out_sems):
        cols = pl.ds(pl.multiple_of(pl.program_id(0) * tc, tc), tc)
        loads = [pltpu.make_async_copy(src.at[:, 0, cols], dst, in_sems.at[k])
                 for k, (src, dst) in enumerate(((w_hbm, w_vm), (m_hbm, m_vm), (v_hbm, v_vm)))]
        for cp in loads:
            cp.start()
        for cp in loads:
            cp.wait()
        d_vm[...], m2_vm[...], v2_vm[...] = _adam(w_vm[...], g_ref[...], m_vm[...], v_vm[...])
        stores = [pltpu.make_async_copy(src, dst.at[:, 0, cols], out_sems.at[k])
                  for k, (src, dst) in enumerate(((d_vm, d_hbm), (m2_vm, m2_hbm), (v2_vm, v2_hbm), (g_ref, g3_hbm)))]
        for cp in stores:
            cp.start()
        for cp in stores:
            cp.wait()

    hbm = pl.BlockSpec(memory_space=pl.ANY)
    return pl.pallas_call(
        body, name=name, grid=(cc // tc,), in_specs=[hbm, pl.BlockSpec((rr, tc), lambda i: (0, i)), hbm, hbm],
        out_specs=[hbm] * 4, out_shape=[jax.ShapeDtypeStruct((rr, 1, cc), F32)] * 4,
        scratch_shapes=[pltpu.VMEM((rr, tc), F32)] * 6 + [pltpu.SemaphoreType.DMA((3,)), pltpu.SemaphoreType.DMA((4,))],
        compiler_params=_params(("arbitrary",)),
    )(w3, g, m3, v3)


def _ada_update(c_all, dmod_cols, w, m, v):
    rr, cc = w.shape
    tr = min(256, rr)
    c_all = jnp.pad(c_all, ((0, 8), (0, 0)))
    dmod_cols = jnp.pad(dmod_cols, ((0, 8), (0, 0)))

    def body(c_ref, dm_ref, w_ref, m_ref, v_ref, g_ref, d_ref, m2_ref, v2_ref):
        cv = c_ref[...]
        sc = (cv * _sigmoid(cv)).astype(BF)
        g = _dot(sc, dm_ref[...].astype(BF), TN)
        g_ref[...] = g
        d_ref[...], m2_ref[...], v2_ref[...] = _adam(w_ref[...], g, m_ref[...], v_ref[...])

    blk = pl.BlockSpec((tr, cc), lambda i: (i, 0))
    return pl.pallas_call(
        body, name="ada_update", grid=(rr // tr,),
        in_specs=[pl.BlockSpec((16, tr), lambda i: (0, i)), _full((16, cc)), blk, blk, blk],
        out_specs=[blk] * 4, out_shape=[jax.ShapeDtypeStruct((rr, cc), F32)] * 4,
        compiler_params=_params(("arbitrary",)),
    )(c_all, dmod_cols, w, m, v)


def _small_update(parts, weights, moms, vels):
    n = len(weights)

    def body(*refs):
        p_refs, w_refs, m_refs, v_refs = refs[:n + 1], refs[n + 1:2 * n + 1], refs[2 * n + 1:3 * n + 1], refs[3 * n + 1:4 * n + 1]
        outs = refs[4 * n + 1:]
        for i in range(n):
            g = p_refs[i][0]
            for d in range(1, 8):
                g = g + p_refs[i][d]
            delta, m2, v2 = _adam(w_refs[i][...], g, m_refs[i][...], v_refs[i][...])
            outs[4 * i][...] = g
            outs[4 * i + 1][...] = delta
            outs[4 * i + 2][...] = m2
            outs[4 * i + 3][...] = v2
        tot = p_refs[n][0]
        for d in range(1, 8):
            tot = tot + p_refs[n][d]
        outs[4 * n][...] = tot

    out_shape = []
    for w in weights:
        out_shape += [jax.ShapeDtypeStruct(w.shape, F32)] * 4
    out_shape.append(jax.ShapeDtypeStruct(parts[n].shape[1:], F32))
    return pl.pallas_call(body, name="small_update", out_shape=out_shape, compiler_params=_params())(
        *parts, *weights, *moms, *vels)


def _pad_w_in_t(w):
    w = w.reshape(-1, w.shape[2])
    pad = jnp.zeros((LANES - GLA_RANK, w.shape[1]), w.dtype)
    return jnp.concatenate([w[dst:dst + n] for _, dst, n in sorted(_UNPAD_ROWS)] + [pad], axis=0)


def _rows8(a):
    flat = a.reshape(-1)
    rows = -(-flat.shape[0] // LANES)
    rows8 = -(-rows // 8) * 8
    flat = jnp.pad(flat, (0, rows8 * LANES - flat.shape[0]))
    return flat.reshape(rows8, LANES)


def kernel(x, c, positions, w_ada, b_ada, g_norm, w_in, w_decay, b_decay, g_gla_head, sinks, w_out, g_final, loss_target, m_w_ada, m_b_ada, m_g_norm, m_w_in, m_w_decay, m_b_decay, m_g_gla_head, m_sinks, m_w_out, m_g_final, v_w_ada, v_b_ada, v_g_norm, v_w_in, v_w_decay, v_b_decay, v_g_gla_head, v_sinks, v_w_out, v_g_final):
    ax, ay, ac = lax.axis_index("x"), lax.axis_index("y"), lax.axis_index("c")
    chip = 2 * ax + ay
    dev = 2 * chip + ac
    s = x.shape[1]
    x2d = x[0]
    target = loss_target[0]
    w_ada2, w_out2, w_dec2 = w_ada[0], w_out[0], w_decay[0]
    w_in_t = w_in[0].T
    ada_cols = w_ada2.shape[1]
    in_cols = w_in_t.shape[0]
    out_rows = w_out2.shape[0]
    half = D_MODEL // 2

    cw = jnp.concatenate([c.reshape(8, LANES), w_dec2.reshape(8, LANES)], axis=0)
    b_shard = lax.dynamic_slice(b_ada, (0, chip * ada_cols), (1, ada_cols))
    half_in = lax.dynamic_slice(w_in_t, (0, ac * half), (in_cols, half)).astype(BF)
    half_out = lax.dynamic_slice(w_out2, (ac * (out_rows // 2), 0), (out_rows // 2, D_MODEL)).astype(BF)
    inv_freq = 1.0 / (ROPE_THETA ** (jnp.arange(0, 64, 2, dtype=F32) / 64))
    first, mod_all, w_in_all, w_out_all, cos, sin = _prologue(
        cw, w_ada2, b_shard, half_in, half_out, positions.reshape(s, 1), jnp.tile(inv_freq, 4).reshape(1, LANES))

    first = first.reshape(8, 2, 8, LANES)
    c_all = first[:, 0].reshape(8, D_MODEL)
    w_dec_full = first[0::2, 1].reshape(4, GLA_RANK, 64).transpose(1, 0, 2).reshape(GLA_RANK, 256)
    mod = mod_all.reshape(4, 2, 8, ada_cols)[:, 0]
    mod = lax.dynamic_slice(mod, (0, dev, 0), (4, 1, ada_cols)).reshape(1, 4 * ada_cols)
    shift, sc1p, gate = mod[:, :D_MODEL], 1.0 + mod[:, D_MODEL:2 * D_MODEL], mod[:, 2 * D_MODEL:]
    wpad_t = _pad_w_in_t(w_in_all)
    w_out_all = w_out_all.reshape(D_MODEL, D_MODEL)

    wdecp = jnp.pad(w_dec_full, ((0, LANES - GLA_RANK), (0, 0))).astype(BF)

    proj = _inproj_fwd(x2d, shift, sc1p, g_norm, wpad_t)
    og, o_gla, sprev = _gla_fwd(proj, wdecp, b_decay, g_gla_head)
    osw, o_swa = _swa_fwd(proj, cos, sin, sinks)
    dx2, dog, dos, dw_out, loss_p, dgf, dgate = _outproj(og, osw, w_out_all, x2d, target, gate, g_final.reshape(1, D_MODEL))
    dsq, dsz, dsk, dsv, dsinks = _swa_bwd(proj, dos, o_swa, cos, sin, sinks)
    dqk, dv, dgz, dga, dwdp, dbd, dgg = _gla_bwd(proj, dog, o_gla, sprev, wdecp, b_decay, g_gla_head)
    pieces = (dqk, dv, dgz, dsq, dsz, dsk, dsv, dga)
    gx, dw_in_t, dshift, dscale, dgn = _inproj_bwd(x2d, shift, sc1p, g_norm, wpad_t, dx2, pieces)

    segs = [jnp.concatenate([dshift, dscale, dgate], axis=1), dgn, dgf, dwdp[:GLA_RANK], dbd, dgg, dsinks, loss_p]
    packed = [_rows8(a) for a in segs]
    offs = [0]
    for a in packed:
        offs.append(offs[-1] + a.shape[0])
    g_w_in_t, g_w_out, small = _epilogue(dw_in_t.reshape(4, in_cols, D_MODEL), dw_out.reshape(4, out_rows, D_MODEL),
                                         jnp.concatenate(packed, axis=0))

    def seg(i, size):
        return small[:, offs[i]:offs[i + 1]].reshape(8, -1)[:, :size]

    dmod_all = seg(0, 3 * D_MODEL)
    dwd_all = lax.dynamic_slice(seg(3, GLA_RANK * 256).reshape(8, GLA_RANK, 256), (0, 0, chip * 64), (8, GLA_RANK, 64))
    parts = [dmod_all.reshape(8, 1, 3 * D_MODEL), seg(1, D_MODEL).reshape(8, 1, D_MODEL), dwd_all,
             seg(4, 256).reshape(8, 1, 256), seg(5, 512).reshape(8, 1, 512), seg(6, SWA_HEADS).reshape(8, 1, SWA_HEADS),
             seg(2, D_MODEL).reshape(8, 1, D_MODEL), seg(7, LANES).reshape(8, 1, LANES)]
    smalls = _small_update(
        parts,
        [b_ada, g_norm, w_dec2, b_decay, g_gla_head, sinks, g_final.reshape(1, D_MODEL)],
        [m_b_ada, m_g_norm, m_w_decay[0], m_b_decay, m_g_gla_head, m_sinks, m_g_final.reshape(1, D_MODEL)],
        [v_b_ada, v_g_norm, v_w_decay[0], v_b_decay, v_g_gla_head, v_sinks, v_g_final.reshape(1, D_MODEL)])
    (g_b_ada, d_b_ada, nm_b_ada, nv_b_ada, g_gn, d_gn, nm_gn, nv_gn, g_wd, d_wd, nm_wd, nv_wd,
     g_bd, d_bd, nm_bd, nv_bd, g_gg, d_gg, nm_gg, nv_gg, g_sk, d_sk, nm_sk, nv_sk,
     g_gf, d_gf, nm_gf, nv_gf, loss_row) = smalls
    loss = loss_row[0, 0]

    dmod_cols = lax.dynamic_slice(dmod_all, (0, chip * ada_cols), (8, ada_cols))
    g_w_ada, d_w_ada, nm_w_ada, nv_w_ada = _ada_update(c_all, dmod_cols, w_ada2, m_w_ada[0], v_w_ada[0])
    to3 = lambda a: jnp.transpose(a, (2, 0, 1))
    from3 = lambda a: jnp.transpose(a, (1, 2, 0))[0]
    d3, nm3, nv3, g3 = _adamw_t(to3(w_in), g_w_in_t, to3(m_w_in), to3(v_w_in), "adamw_w_in")
    g_w_in, d_w_in, nm_w_in, nv_w_in = from3(g3), from3(d3), from3(nm3), from3(nv3)
    d_w_out, nm_w_out, nv_w_out = _adamw(w_out2, g_w_out, m_w_out[0], v_w_out[0], "adamw_w_out")

    flat = lambda a: a.reshape(D_MODEL)
    grads = [g_w_ada[None], g_b_ada, g_gn, g_w_in[None], g_wd[None], g_bd, g_gg, g_sk, g_w_out[None], flat(g_gf)]
    deltas = [d_w_ada[None], d_b_ada, d_gn, d_w_in[None], d_wd[None], d_bd, d_gg, d_sk, d_w_out[None], flat(d_gf)]
    new_m = [nm_w_ada[None], nm_b_ada, nm_gn, nm_w_in[None], nm_wd[None], nm_bd, nm_gg, nm_sk, nm_w_out[None], flat(nm_gf)]
    new_v = [nv_w_ada[None], nv_b_ada, nv_gn, nv_w_in[None], nv_wd[None], nv_bd, nv_gg, nv_sk, nv_w_out[None], flat(nv_gf)]
    return (loss, gx[None], *grads, *deltas, *new_m, *new_v)
```

```python
import jax
import jax.numpy as jnp
from jax import lax
from jax.experimental import pallas as pl
from jax.experimental.pallas import tpu as pltpu

F32 = jnp.float32
BF = jnp.bfloat16

D_MODEL = 1024
GLA_HEADS = 4
GLA_DK = 64
GLA_CHUNK = 64
GLA_RANK = 16
GLA_TAU = 16.0
GLA_SUB = 256
GLA_ROWS_FWD = 1024
GLA_ROWS_BWD = 512
SWA_HEADS = 8
SWA_BLOCK = 128
SWA_QBLOCKS_FWD = 8
SWA_QBLOCKS = 8
RMS_EPS = 1e-6
ROPE_THETA = 10000.0

OFF_QK, OFF_V, OFF_GZ, OFF_SQ, OFF_SZ, OFF_SK, OFF_SV, OFF_GA = 0, 512, 1024, 1536, 2048, 2560, 2688, 2816
D_PAD = 2944
D_IN = 2832
LANES = 128
VMEM_LIMIT = 56 * 1024 * 1024

ADAM_LR, ADAM_B1, ADAM_B2, ADAM_EPS, ADAM_WD, ADAM_STEP = 0.001, 0.9, 0.999, 1e-08, 0.01, 10

NT = (((1,), (1,)), ((), ()))
TN = (((0,), (0,)), ((), ()))
MESH = pl.DeviceIdType.MESH


def _dot(a, b, dims=None):
    if dims is None:
        return jnp.dot(a, b, preferred_element_type=F32)
    return lax.dot_general(a, b, dims, preferred_element_type=F32)


def _sigmoid(x):
    return 1.0 / (1.0 + jnp.exp(-x))


def _params(sem=None):
    return pltpu.CompilerParams(dimension_semantics=sem, vmem_limit_bytes=VMEM_LIMIT)


def _full(shape):
    return pl.BlockSpec(shape, lambda i: (0,) * len(shape))


def _subtiles(rows, size=256):
    size = min(size, rows)
    return [slice(k * size, (k + 1) * size) for k in range(rows // size)]


_GATHER_SEMS = [pltpu.SemaphoreType.DMA((7,)), pltpu.SemaphoreType.DMA((7,)), pltpu.SemaphoreType.DMA]


class _Gather:
    def __init__(self, x_ref, out_ref, send_sems, recv_sems, local_sem, slab=None):
        self.slab_of = slab
        x, y, c = lax.axis_index("x"), lax.axis_index("y"), lax.axis_index("c")
        self.me, self.sibling, self.c = (x, y, c), (x, y, 1 - c), c
        self.xn, self.yn, self.dg = (1 - x, y), (x, 1 - y), (1 - x, 1 - y)
        self.pass_from = (lax.rem(x + 1 - c, 2), lax.rem(y + c, 2))
        self.pass_to = (lax.rem(x + c, 2), lax.rem(y + 1 - c, 2))
        self.x_ref, self.out_ref, self.send_sems, self.recv_sems = x_ref, out_ref, send_sems, recv_sems
        self.mine = pltpu.make_async_copy(x_ref, self._slab(*self.me), local_sem)

    def _slab(self, px, py, pc):
        if self.slab_of is not None:
            return self.slab_of(self.out_ref, px, py, pc)
        return self.out_ref.at[4 * px + 2 * py + pc]

    def _copy(self, k, blk, to, src=None):
        return pltpu.make_async_remote_copy(
            src_ref=self._slab(*blk) if src is None else src, dst_ref=self._slab(*blk),
            send_sem=self.send_sems.at[k], recv_sem=self.recv_sems.at[k], device_id=to, device_id_type=MESH)

    def _sends(self):
        c = self.c
        return [self._copy(0, self.me, self.sibling, src=self.x_ref),
                self._copy(1, self.me, (*self.xn, c), src=self.x_ref),
                self._copy(2, self.me, (*self.yn, c), src=self.x_ref),
                self._copy(3, (*self.pass_from, c), (*self.pass_to, c)),
                self._copy(4, (*self.xn, c), self.sibling),
                self._copy(5, (*self.yn, c), self.sibling),
                self._copy(6, (*self.dg, c), self.sibling)]

    def start(self):
        self.mine.start()
        for cp in self._sends()[0:3]:
            cp.start()

    def pass_on(self):
        sends = self._sends()
        self._copy(1, (*self.xn, self.c), self.me).wait_recv()
        self._copy(2, (*self.yn, self.c), self.me).wait_recv()
        for k in (3, 4, 5):
            sends[k].start()

    def relay_diagonal(self):
        self._copy(3, (*self.dg, self.c), self.me).wait_recv()
        self._sends()[6].start()

    def relay(self):
        self.pass_on()
        self.relay_diagonal()

    def finish(self):
        c = self.c
        self._copy(0, self.sibling, self.me).wait_recv()
        for k, chip in ((4, self.xn), (5, self.yn), (6, self.dg)):
            self._copy(k, (*chip, 1 - c), self.me).wait_recv()
        for cp in self._sends():
            cp.wait_send()
        self.mine.wait()


def _prologue(cw, w_ada, b_shard, half_in, half_out, pos_col, inv_freq):
    s = pos_col.shape[0]
    rt = min(512, s)

    def body(cw_ref, wada_hbm, b_ref, hin_ref, hout_ref, pos_hbm, f_ref,
             first_ref, mod_ref, win_ref, wout_ref, cos_hbm, sin_hbm,
             mod_blk, cos_ref, sin_ref, wada_ref, pos_ref, table_sems, local_sems, *sems):
        fetch_w = pltpu.make_async_copy(wada_hbm, wada_ref, local_sems.at[0])
        fetch_p = pltpu.make_async_copy(pos_hbm, pos_ref, local_sems.at[1])
        fetch_w.start()
        fetch_p.start()
        g_c = _Gather(cw_ref, first_ref, *sems[0:3])
        half_lanes = hin_ref.shape[1]
        g_in = _Gather(hin_ref, win_ref, *sems[3:6],
                       slab=lambda ref, px, py, pc: ref.at[2 * px + py, :, pl.ds(pl.multiple_of(pc * half_lanes, half_lanes), half_lanes)])
        g_out = _Gather(hout_ref, wout_ref, *sems[6:9])
        g_mod = _Gather(mod_blk, mod_ref, *sems[9:12])
        g_c.start()
        g_in.start()
        g_out.start()
        g_c.relay()
        g_c.finish()
        c_rows = [jnp.concatenate([first_ref[d, r:r + 1, :] for r in range(8)], axis=1) for d in range(8)]
        c_all = jnp.concatenate(c_rows, axis=0)
        sc = (c_all * _sigmoid(c_all)).astype(BF)
        fetch_w.wait()
        mod_blk[...] = _dot(sc, wada_ref[...].astype(BF)) + b_ref[...]
        g_mod.start()
        fetch_p.wait()

        def rope_rows(i, carry):
            rows = pl.ds(pl.multiple_of(i * rt, rt), rt)
            ang = pos_ref[rows, :].astype(F32) * f_ref[...]
            lane = lax.broadcasted_iota(jnp.int32, ang.shape, 1)
            cos_ref[rows, :] = jnp.cos(ang)
            sn = jnp.sin(ang)
            sin_ref[rows, :] = jnp.where((lane % 64) < 32, -sn, sn)
            pltpu.make_async_copy(cos_ref.at[rows, :], cos_hbm.at[rows, :], table_sems.at[0]).start()
            pltpu.make_async_copy(sin_ref.at[rows, :], sin_hbm.at[rows, :], table_sems.at[1]).start()
            return carry

        steps = s // rt
        lax.fori_loop(0, steps // 2, rope_rows, 0)
        g_in.pass_on()
        g_out.pass_on()
        lax.fori_loop(steps // 2, steps, rope_rows, 0)
        g_in.relay_diagonal()
        g_out.relay_diagonal()
        g_mod.relay()
        g_in.finish()
        g_out.finish()
        g_mod.finish()
        pltpu.make_async_copy(cos_ref, cos_hbm, table_sems.at[0]).wait()
        pltpu.make_async_copy(sin_ref, sin_hbm, table_sems.at[1]).wait()

    vm = pl.BlockSpec(memory_space=pltpu.VMEM)
    hbm = pl.BlockSpec(memory_space=pl.ANY)
    return pl.pallas_call(
        body, name="prologue",
        out_shape=[jax.ShapeDtypeStruct((8,) + cw.shape, F32), jax.ShapeDtypeStruct((8, 8, w_ada.shape[1]), F32),
                   jax.ShapeDtypeStruct((4, half_in.shape[0], 2 * half_in.shape[1]), half_in.dtype),
                   jax.ShapeDtypeStruct((8,) + half_out.shape, half_out.dtype),
                   jax.ShapeDtypeStruct((s, LANES), F32), jax.ShapeDtypeStruct((s, LANES), F32)],
        in_specs=[vm, hbm, vm, hbm, hbm, hbm, vm], out_specs=[vm, vm, hbm, hbm, hbm, hbm],
        scratch_shapes=[pltpu.VMEM((8, w_ada.shape[1]), F32), pltpu.VMEM((s, LANES), F32), pltpu.VMEM((s, LANES), F32),
                        pltpu.VMEM(w_ada.shape, F32), pltpu.VMEM(pos_col.shape, jnp.int32),
                        pltpu.SemaphoreType.DMA((2,)), pltpu.SemaphoreType.DMA((2,))] + _GATHER_SEMS * 4,
        compiler_params=pltpu.CompilerParams(vmem_limit_bytes=VMEM_LIMIT),
    )(cw, w_ada, b_shard, half_in, half_out, pos_col, inv_freq)


def _reduce_scratch(rr, cc):
    c2 = cc // 2
    return [pltpu.VMEM((4, rr, c2), F32), pltpu.VMEM((4, rr, c2), F32), pltpu.VMEM((3, rr, c2), BF),
            pltpu.VMEM((2, rr, c2), BF), pltpu.VMEM((rr, c2), BF), pltpu.VMEM((rr, c2), F32),
            pltpu.SemaphoreType.DMA((8,)), pltpu.SemaphoreType.DMA((8,)), pltpu.SemaphoreType.DMA((5,))]


class _Reduce:
    def __init__(self, p_hbm, out_ref, acc_ref, own_ref, send_ref, land_ref, relay_ref, res_ref,
                 send_sems, recv_sems, local_sems):
        x, y, c = lax.axis_index("x"), lax.axis_index("y"), lax.axis_index("c")
        c2 = out_ref.shape[1] // 2
        sibling = (x, y, 1 - c)
        first = (lax.rem(x + 1 - c, 2), lax.rem(y + c, 2))
        second = (lax.rem(x + c, 2), lax.rem(y + 1 - c, 2))
        shards = [2 * first[0] + first[1], 2 * second[0] + second[1], 2 * (1 - x) + (1 - y), 2 * x + y]
        sibling_slot = (1, 0, 2, 3)
        mine = pl.ds(pl.multiple_of(c * c2, c2), c2)
        other = pl.ds(pl.multiple_of((1 - c) * c2, c2), c2)
        self.acc_ref, self.own_ref, self.send_ref, self.land_ref = acc_ref, own_ref, send_ref, land_ref
        self.relay_ref, self.res_ref = relay_ref, res_ref
        self.own = [pltpu.make_async_copy(p_hbm.at[j, :, mine], own_ref.at[k], local_sems.at[k])
                    for k, j in enumerate(shards)]
        self.swap_out = [pltpu.make_async_remote_copy(
            src_ref=p_hbm.at[j, :, other], dst_ref=acc_ref.at[sibling_slot[k]], send_sem=send_sems.at[k],
            recv_sem=recv_sems.at[sibling_slot[k]], device_id=sibling, device_id_type=MESH) for k, j in enumerate(shards)]
        self.swap_in = [pltpu.make_async_remote_copy(
            src_ref=p_hbm.at[j, :, other], dst_ref=acc_ref.at[k], send_sem=send_sems.at[k], recv_sem=recv_sems.at[k],
            device_id=sibling, device_id_type=MESH) for k, j in enumerate(shards)]

        def message(k, src, dst, to):
            return pltpu.make_async_remote_copy(src_ref=src, dst_ref=dst, send_sem=send_sems.at[k], recv_sem=recv_sems.at[k],
                                                device_id=(*to, c), device_id_type=MESH)

        self.direct = message(4, send_ref.at[0], land_ref.at[0], first)
        self.passed = message(5, send_ref.at[1], relay_ref, first)
        self.joint = message(6, send_ref.at[2], land_ref.at[1], second)
        self.put = pltpu.make_async_copy(res_ref, out_ref.at[:, mine], local_sems.at[4])
        self.share = pltpu.make_async_remote_copy(
            src_ref=res_ref, dst_ref=out_ref.at[:, mine], send_sem=send_sems.at[7],
            recv_sem=recv_sems.at[7], device_id=sibling, device_id_type=MESH)

    def start(self):
        for k in (0, 2, 1, 3):
            self.own[k].start()
            self.swap_out[k].start()

    def _combine(self, k):
        self.own[k].wait()
        self.swap_out[k].wait_send()
        self.swap_in[k].wait_recv()
        self.acc_ref[k] = self.acc_ref[k] + self.own_ref[k]

    def combine_and_send(self):
        dt = self.send_ref.dtype
        self._combine(0)
        self.send_ref[0] = self.acc_ref[0].astype(dt)
        self.direct.start()
        self._combine(2)
        self.send_ref[1] = self.acc_ref[2].astype(dt)
        self.passed.start()
        self._combine(1)
        self.passed.wait_recv()
        self.send_ref[2] = (self.acc_ref[1] + self.relay_ref[...].astype(F32)).astype(dt)
        self.joint.start()
        self._combine(3)

    def total_and_share(self):
        self.direct.wait_recv()
        self.joint.wait_recv()
        self.res_ref[...] = self.acc_ref[3] + self.land_ref[0].astype(F32) + self.land_ref[1].astype(F32)
        for cp in (self.direct, self.passed, self.joint):
            cp.wait_send()
        self.put.start()
        self.share.start()

    def finish(self):
        self.put.wait()
        self.share.wait()


def _epilogue(dw_in_parts, dw_out_parts, small):
    _, r_in, cc = dw_in_parts.shape
    _, r_out, _ = dw_out_parts.shape
    n_red = len(_reduce_scratch(r_in, cc))

    def body(pin_hbm, pout_hbm, small_ref, gin_ref, gout_ref, small_all_ref, *scratch):
        red_in = _Reduce(pin_hbm, gin_ref, *scratch[0:n_red])
        red_out = _Reduce(pout_hbm, gout_ref, *scratch[n_red:2 * n_red])
        gat = _Gather(small_ref, small_all_ref, *scratch[2 * n_red:])
        red_out.start()
        red_in.start()
        gat.start()
        red_out.combine_and_send()
        red_in.combine_and_send()
        gat.relay()
        red_out.total_and_share()
        red_in.total_and_share()
        gat.finish()
        red_out.finish()
        red_in.finish()

    vm = pl.BlockSpec(memory_space=pltpu.VMEM)
    anyspec = pl.BlockSpec(memory_space=pl.ANY)
    return pl.pallas_call(
        body, name="epilogue",
        out_shape=[jax.ShapeDtypeStruct((r_in, cc), F32), jax.ShapeDtypeStruct((r_out, cc), F32),
                   jax.ShapeDtypeStruct((8,) + small.shape, F32)],
        in_specs=[anyspec, anyspec, vm], out_specs=[anyspec, anyspec, vm],
        scratch_shapes=_reduce_scratch(r_in, cc) + _reduce_scratch(r_out, cc) + _GATHER_SEMS,
        compiler_params=pltpu.CompilerParams(vmem_limit_bytes=VMEM_LIMIT),
    )(dw_in_parts, dw_out_parts, small)


def _rope(t, cosb, sinb, first_half):
    partner = jnp.where(first_half, pltpu.roll(t, 96, 1), pltpu.roll(t, 32, 1))
    return t * cosb + partner * sinb


def _rope_t(g, cosb, sinb, first_half):
    gs = g * sinb
    partner = jnp.where(first_half, pltpu.roll(gs, 96, 1), pltpu.roll(gs, 32, 1))
    return g * cosb + partner


def _modnorm(x, g, sc1p, shift):
    r = lax.rsqrt(jnp.mean(x * x, axis=-1, keepdims=True) + RMS_EPS)
    xn = x * r
    return xn, r, (xn * g) * sc1p + shift


def _inproj_fwd(x2d, shift, sc1p, g_norm, wpad_t):
    s = x2d.shape[0]
    tm = min(1024, s)

    def body(x_ref, sh_ref, sc_ref, g_ref, w_ref, o_ref):
        subs = _subtiles(tm)
        hs = [_modnorm(x_ref[sl, :], g_ref[...], sc_ref[...], sh_ref[...])[2].astype(BF) for sl in subs]
        for sl, h in zip(subs, hs):
            o_ref[sl, :] = _dot(h, w_ref[...], NT)

    vec = _full((1, D_MODEL))
    return pl.pallas_call(
        body, name="inproj_fwd", grid=(s // tm,),
        in_specs=[pl.BlockSpec((tm, D_MODEL), lambda i: (i, 0)), vec, vec, vec, _full((D_PAD, D_MODEL))],
        out_specs=pl.BlockSpec((tm, D_PAD), lambda i: (i, 0)),
        out_shape=jax.ShapeDtypeStruct((s, D_PAD), F32),
        compiler_params=_params(("arbitrary",)),
    )(x2d, shift, sc1p, g_norm, wpad_t)


def _split3(a):
    hi = a.astype(BF)
    r1 = a - hi.astype(F32)
    mid = r1.astype(BF)
    lo = (r1 - mid.astype(F32)).astype(BF)
    return hi, mid, lo


def _tri_matmul(tri, a):
    hi, mid, lo = _split3(a)
    return _dot(tri, hi) + _dot(tri, mid) + _dot(tri, lo)


def _chunks(tb):
    return [slice(c * GLA_CHUNK, (c + 1) * GLA_CHUNK) for c in range(tb // GLA_CHUNK)]


def _per_chunk_rows(rows, width):
    return jnp.concatenate([jnp.broadcast_to(r, (GLA_CHUNK, width)) for r in rows], axis=0)


def _gla_triangle(tb):
    row = lax.broadcasted_iota(jnp.int32, (tb, tb), 0)
    col = lax.broadcasted_iota(jnp.int32, (tb, tb), 1)
    return (((row // GLA_CHUNK) == (col // GLA_CHUNK)) & (col <= row)).astype(F32)


def _lane_mean(x, ones_b):
    hi = x.astype(BF)
    lo = (x - hi.astype(F32)).astype(BF)
    return (_dot(hi, ones_b) + _dot(lo, ones_b)) * (1.0 / LANES)


def _head(t, h, lo_h):
    blk = t[:, LANES * (h // 2):LANES * (h // 2 + 1)]
    return jnp.where(lo_h, blk, 0.0) if h % 2 == 0 else jnp.where(lo_h, 0.0, blk)


def _gla_block_common(qk, ga, wd, bd, tril_b):
    tb = qk.shape[0]
    q, k = qk[:, :256], qk[:, 256:]
    z = _dot(ga.astype(BF), wd) + bd
    la = (jnp.minimum(z, 0.0) - jnp.log(1.0 + jnp.exp(-jnp.abs(z)))) * (1.0 / GLA_TAU)
    b = _tri_matmul(tril_b, la)
    bls = [b[rs.stop - 1:rs.stop, :] for rs in _chunks(tb)]
    eq = jnp.exp(b)
    ek = jnp.exp(-b)
    f = jnp.exp(_per_chunk_rows(bls, 256) - b)
    return z, eq, ek, f, q * (eq * GLA_DK ** -0.5), k * ek, k * f, bls


def _gla_units(s, rows):
    sub = min(GLA_SUB, s)
    tb = min(rows, s)
    subs = [slice(i * sub, (i + 1) * sub) for i in range(tb // sub)]
    units = [(i, h) for i in range(len(subs)) for h in range(GLA_HEADS)]
    return tb, sub, subs, units


def _gla_fwd(proj, wdecp, bdec, ggla):
    s = proj.shape[0]
    tb, sub, subs, units = _gla_units(s, GLA_ROWS_FWD)
    nch = sub // GLA_CHUNK

    def body(qk_ref, v_ref, gz_ref, ga_ref, wd_ref, bd_ref, gg_ref, tri_ref, og_ref, opre_ref, sprev_ref, st_ref):
        @pl.when(pl.program_id(0) == 0)
        def _():
            st_ref[...] = jnp.zeros_like(st_ref)

        lo_h = lax.broadcasted_iota(jnp.int32, (sub, LANES), 1) < GLA_DK
        tril = tri_ref[...] > 0.5
        tril_b = tri_ref[...].astype(BF)
        ones_b = jnp.ones((LANES, LANES), BF)
        gg, wd, bd = gg_ref[...], wd_ref[...], bd_ref[...]
        chunks = _chunks(sub)
        lanes = [slice(h * LANES, (h + 1) * LANES) for h in range(GLA_HEADS)]
        com = [_gla_block_common(qk_ref[sl, :], ga_ref[sl, :], wd, bd, tril_b) for sl in subs]
        decs = [[jnp.exp(bl) for bl in cm[7]] for cm in com]
        a = {(i, h): _head(com[i][4], h, lo_h).astype(BF) for i, h in units}
        bm = {(i, h): _head(com[i][5], h, lo_h).astype(BF) for i, h in units}
        ktl = {(i, h): _head(com[i][6], h, lo_h).astype(BF) for i, h in units}
        vh = {(i, h): v_ref[subs[i], lanes[h]].astype(BF) for i, h in units}
        sc = {u: _dot(a[u], bm[u], NT) for u in units}
        upd = {u: [_dot(vh[u][rs], ktl[u][rs], TN) for rs in chunks] for u in units}
        p = {u: jnp.where(tril, sc[u], 0.0).astype(BF) for u in units}
        o = {u: _dot(p[u], vh[u]) for u in units}
        states = {}
        for h in range(GLA_HEADS):
            st = st_ref[h]
            for i in range(len(subs)):
                entering = []
                for c in range(nch):
                    entering.append(st)
                    sprev_ref[i * nch + c, h] = st
                    st = st * decs[i][c][:, LANES * (h // 2):LANES * (h // 2 + 1)] + upd[(i, h)][c]
                states[(i, h)] = entering
            st_ref[h] = st
        inter = {u: [_dot(a[u][rs], states[u][c].astype(BF), NT) for c, rs in enumerate(chunks)] for u in units}
        o = {u: o[u] + jnp.concatenate(inter[u], axis=0) for u in units}
        ms = {u: _lane_mean(o[u] * o[u], ones_b) for u in units}
        for i, h in units:
            gzh = gz_ref[subs[i], lanes[h]]
            opre_ref[subs[i], lanes[h]] = o[(i, h)]
            og_ref[subs[i], lanes[h]] = (((o[(i, h)] * lax.rsqrt(ms[(i, h)] + RMS_EPS)) * gg[:, lanes[h]])
                                         * (gzh * _sigmoid(gzh))).astype(og_ref.dtype)

    def col(width, off):
        return pl.BlockSpec((tb, width), lambda i: (i, off // width))

    return pl.pallas_call(
        body, name="gla_fwd", grid=(s // tb,),
        in_specs=[col(512, OFF_QK), col(512, OFF_V), col(512, OFF_GZ), col(LANES, OFF_GA),
                  _full((LANES, 256)), _full((1, 256)), _full((1, 512)), _full((sub, sub))],
        out_specs=[pl.BlockSpec((tb, 512), lambda i: (i, 0)), pl.BlockSpec((tb, 512), lambda i: (i, 0)),
                   pl.BlockSpec((tb // GLA_CHUNK, GLA_HEADS, LANES, LANES), lambda i: (i, 0, 0, 0))],
        out_shape=[jax.ShapeDtypeStruct((s, 512), BF), jax.ShapeDtypeStruct((s, 512), F32),
                   jax.ShapeDtypeStruct((s // GLA_CHUNK, GLA_HEADS, LANES, LANES), F32)],
        scratch_shapes=[pltpu.VMEM((GLA_HEADS, LANES, LANES), F32)],
        compiler_params=_params(("arbitrary",)),
    )(proj, proj, proj, proj, wdecp, bdec, ggla, _gla_triangle(sub))


def _gla_bwd(proj, dog, opre, sprev, wdecp, bdec, ggla):
    s = proj.shape[0]
    tb, sub, subs, units = _gla_units(s, GLA_ROWS_BWD)
    nsub = len(subs)
    nch = sub // GLA_CHUNK
    nb = s // tb

    def body(qk_ref, v_ref, gz_ref, ga_ref, dog_ref, opre_ref, sprev_ref, wd_ref, bd_ref, gg_ref, tri_ref, triu_ref,
             dqk_ref, dv_ref, dgz_ref, dga_ref, dwd_ref, dbd_ref, dgg_ref, dst_ref):
        @pl.when(pl.program_id(0) == 0)
        def _():
            dst_ref[...] = jnp.zeros_like(dst_ref)
            dwd_ref[...] = jnp.zeros_like(dwd_ref)
            dbd_ref[...] = jnp.zeros_like(dbd_ref)
            dgg_ref[...] = jnp.zeros_like(dgg_ref)

        lo_h = lax.broadcasted_iota(jnp.int32, (sub, LANES), 1) < GLA_DK
        tril = tri_ref[...] > 0.5
        tril_b = tri_ref[...].astype(BF)
        triu_b = triu_ref[...].astype(BF)
        ones_b = jnp.ones((LANES, LANES), BF)
        last_row = (lax.broadcasted_iota(jnp.int32, (sub, LANES), 0) % GLA_CHUNK) == GLA_CHUNK - 1
        wd, gg, bd = wd_ref[...], gg_ref[...], bd_ref[...]
        chunks = _chunks(sub)
        lanes = [slice(h * LANES, (h + 1) * LANES) for h in range(GLA_HEADS)]
        blks = [slice(LANES * (h // 2), LANES * (h // 2 + 1)) for h in range(GLA_HEADS)]
        ga = [ga_ref[sl, :] for sl in subs]
        com = [_gla_block_common(qk_ref[sl, :], ga[i], wd, bd, tril_b) for i, sl in enumerate(subs)]
        decs = [[jnp.exp(bl) for bl in cm[7]] for cm in com]
        a = {(i, h): _head(com[i][4], h, lo_h).astype(BF) for i, h in units}
        bm = {(i, h): _head(com[i][5], h, lo_h).astype(BF) for i, h in units}
        ktl = {(i, h): _head(com[i][6], h, lo_h).astype(BF) for i, h in units}
        vh = {(i, h): v_ref[subs[i], lanes[h]].astype(BF) for i, h in units}
        sc = {u: _dot(a[u], bm[u], NT) for u in units}

        o = {(i, h): opre_ref[subs[i], lanes[h]] for i, h in units}
        ms = {u: _lane_mean(o[u] * o[u], ones_b) for u in units}
        gz = {(i, h): gz_ref[subs[i], lanes[h]] for i, h in units}
        dog = {(i, h): dog_ref[subs[i], lanes[h]] for i, h in units}
        sg = {u: _sigmoid(gz[u]) for u in units}
        r = {u: lax.rsqrt(ms[u] + RMS_EPS) for u in units}
        ohat = {u: o[u] * r[u] for u in units}
        sil = {u: gz[u] * sg[u] for u in units}
        for i, h in units:
            u = (i, h)
            dgz_ref[subs[i], lanes[h]] = (dog[u] * (ohat[u] * gg[:, lanes[h]])
                                          * (sg[u] * (1.0 + gz[u] * (1.0 - sg[u])))).astype(dgz_ref.dtype)
            dgg_ref[:, lanes[h]] += jnp.sum(dog[u] * sil[u] * ohat[u], axis=0, keepdims=True)
        dn = {(i, h): dog[(i, h)] * sil[(i, h)] * gg[:, lanes[h]] for i, h in units}
        mdn = {u: _lane_mean(dn[u] * ohat[u], ones_b) for u in units}
        do = {u: (r[u] * (dn[u] - ohat[u] * mdn[u])).astype(BF) for u in units}

        p = {u: jnp.where(tril, sc[u], 0.0).astype(BF) for u in units}
        dpr = {u: _dot(do[u], vh[u], NT) for u in units}
        incr = {u: [_dot(do[u][rs], a[u][rs], TN) for rs in chunks] for u in units}
        dv = {u: _dot(p[u], do[u], TN) for u in units}
        dp = {u: jnp.where(tril, dpr[u], 0.0).astype(BF) for u in units}
        dqd = {u: _dot(dp[u], bm[u]) for u in units}
        dkd = {u: _dot(dp[u], a[u], TN) for u in units}
        st = {(i, h): [sprev_ref[i * nch + c, h] for c in range(nch)] for i, h in units}
        leaving = {}
        for h in range(GLA_HEADS):
            d = dst_ref[h]
            for i in reversed(range(nsub)):
                out = [None] * nch
                for c in reversed(range(nch)):
                    out[c] = d
                    d = d * decs[i][c][:, blks[h]] + incr[(i, h)][c]
                leaving[(i, h)] = out
            dst_ref[h] = d
        lv_b = {u: [leaving[u][c].astype(BF) for c in range(nch)] for u in units}
        dv_s = {u: [_dot(ktl[u][rs], lv_b[u][c], NT) for c, rs in enumerate(chunks)] for u in units}
        dqd_s = {u: [_dot(do[u][rs], st[u][c].astype(BF)) for c, rs in enumerate(chunks)] for u in units}
        dkt_s = {u: [_dot(vh[u][rs], lv_b[u][c]) for c, rs in enumerate(chunks)] for u in units}
        ddec = {u: [jnp.sum(leaving[u][c] * st[u][c], axis=0, keepdims=True) for c in range(nch)] for u in units}
        for i, h in units:
            dv_ref[subs[i], lanes[h]] = (dv[(i, h)] + jnp.concatenate(dv_s[(i, h)], axis=0)).astype(dv_ref.dtype)
        dqd = {u: dqd[u] + jnp.concatenate(dqd_s[u], axis=0) for u in units}
        dkt = {u: jnp.concatenate(dkt_s[u], axis=0) for u in units}

        db = []
        for i, sl in enumerate(subs):
            _, eq, ek, f, qd, kd, kt, _ = com[i]
            parts = []
            for pair in range(GLA_HEADS // 2):
                blk, u0, u1 = blks[2 * pair], (i, 2 * pair), (i, 2 * pair + 1)
                dqd_b, dkd_b, dkt_b = dqd[u0] + dqd[u1], dkd[u0] + dkd[u1], dkt[u0] + dkt[u1]
                dqk_ref[sl, blk] = (dqd_b * (eq[:, blk] * GLA_DK ** -0.5)).astype(dqk_ref.dtype)
                dqk_ref[sl, 256 + LANES * pair:256 + LANES * (pair + 1)] = (dkd_b * ek[:, blk] + dkt_b * f[:, blk]).astype(dqk_ref.dtype)
                dkt_kt = dkt_b * kt[:, blk]
                dbp = dqd_b * qd[:, blk] - dkd_b * kd[:, blk] - dkt_kt
                dbl = [jnp.sum(dkt_kt[rs], axis=0, keepdims=True) + (ddec[u0][c] + ddec[u1][c]) * decs[i][c][:, blk]
                       for c, rs in enumerate(chunks)]
                parts.append(jnp.where(last_row, dbp + _per_chunk_rows(dbl, LANES), dbp))
            db.append(jnp.concatenate(parts, axis=1))
        dla = [_tri_matmul(triu_b, db[i]) for i in range(nsub)]
        dz32 = [dla[i] * (1.0 / GLA_TAU) * _sigmoid(-com[i][0]) for i in range(nsub)]
        dz = [t.astype(BF) for t in dz32]
        for i, sl in enumerate(subs):
            dga_ref[sl, :] = _dot(dz[i], wd, NT).astype(dga_ref.dtype)
            dwd_ref[...] += _dot(ga[i].astype(BF), dz[i], TN)
            dbd_ref[...] += jnp.sum(dz32[i], axis=0, keepdims=True)

    def col(width, off):
        return pl.BlockSpec((tb, width), lambda i: (nb - 1 - i, off // width))

    def rev(width):
        return pl.BlockSpec((tb, width), lambda i: (nb - 1 - i, 0))

    return pl.pallas_call(
        body, name="gla_bwd", grid=(nb,),
        in_specs=[col(512, OFF_QK), col(512, OFF_V), col(512, OFF_GZ), col(LANES, OFF_GA), rev(512), rev(512),
                  pl.BlockSpec((tb // GLA_CHUNK, GLA_HEADS, LANES, LANES), lambda i: (nb - 1 - i, 0, 0, 0)),
                  _full((LANES, 256)), _full((1, 256)), _full((1, 512)), _full((sub, sub)), _full((sub, sub))],
        out_specs=[rev(512), rev(512), rev(512), rev(LANES), _full((LANES, 256)), _full((1, 256)), _full((1, 512))],
        out_shape=[jax.ShapeDtypeStruct((s, 512), BF), jax.ShapeDtypeStruct((s, 512), BF),
                   jax.ShapeDtypeStruct((s, 512), BF), jax.ShapeDtypeStruct((s, LANES), BF),
                   jax.ShapeDtypeStruct((LANES, 256), F32), jax.ShapeDtypeStruct((1, 256), F32),
                   jax.ShapeDtypeStruct((1, 512), F32)],
        scratch_shapes=[pltpu.VMEM((GLA_HEADS, LANES, LANES), F32)],
        compiler_params=_params(("arbitrary",)),
    )(proj, proj, proj, proj, dog, opre, sprev, wdecp, bdec, ggla, _gla_triangle(sub), _gla_triangle(sub).T)


_SWA_COL_HEADS = (0, 2, 1, 3, 4, 6, 5, 7)
_SWA_COLS = SWA_HEADS * SWA_BLOCK


def _swa_masks():
    lo2 = lax.broadcasted_iota(jnp.int32, (2 * SWA_BLOCK, LANES), 1) < 64
    lane1 = lax.broadcasted_iota(jnp.int32, (SWA_BLOCK, LANES), 1)
    first_half = (lane1 % 64) < 32
    key = lax.broadcasted_iota(jnp.int32, (SWA_BLOCK, _SWA_COLS), 0)
    query = lax.broadcasted_iota(jnp.int32, (SWA_BLOCK, _SWA_COLS), 1) % SWA_BLOCK
    return lo2, lane1 < 64, first_half, key > query


def _merge_band(t, prev_mask, prev_bias=None):
    prev = t[:SWA_BLOCK] if prev_bias is None else t[:SWA_BLOCK] + prev_bias
    return jnp.where(prev_mask, prev, t[SWA_BLOCK:])


def _split_band(t, prev_mask_b):
    prev = t * prev_mask_b
    return jnp.concatenate([prev, t - prev], axis=0)


def _kv_variants(t, lo2):
    tr = pltpu.roll(t, 64, 1)
    lo_v = [jnp.where(lo2, t, 0.0).astype(BF), jnp.where(lo2, tr, 0.0).astype(BF)]
    hi_v = [jnp.where(lo2, 0.0, tr).astype(BF), jnp.where(lo2, 0.0, t).astype(BF)]
    return lo_v, hi_v


def _kv_variants_t(t):
    tt = t.T
    sw = jnp.concatenate([tt[64:], tt[:64]], axis=0)
    top = lax.broadcasted_iota(jnp.int32, tt.shape, 0) < 64
    lo_v = [jnp.where(top, tt, 0.0).astype(BF), jnp.where(top, sw, 0.0).astype(BF)]
    hi_v = [jnp.where(top, 0.0, sw).astype(BF), jnp.where(top, 0.0, tt).astype(BF)]
    return lo_v, hi_v


def _swa_scores(qg, k_lo, k_hi):
    return jnp.concatenate([_dot(k_lo[0], qg[0], NT), _dot(k_hi[0], qg[0], NT),
                            _dot(k_lo[1], qg[1], NT), _dot(k_hi[1], qg[1], NT)], axis=1)


def _sink_row(sinks_ref):
    return jnp.concatenate([jnp.full((1, SWA_BLOCK), sinks_ref[0, hd], F32) for hd in _SWA_COL_HEADS], axis=1)


def _swa_softmax(st, prev_mask, prev_bias, sink):
    st = _merge_band(st, prev_mask, prev_bias)
    m = jnp.maximum(jnp.max(st, axis=0, keepdims=True), sink)
    ex = jnp.exp(st - m)
    es = jnp.exp(sink - m)
    inv = 1.0 / (jnp.sum(ex, axis=0, keepdims=True) + es)
    return ex, es, inv


def _no_prev_bias(block_index):
    return jnp.where(block_index > 0, 0.0, -1e30).astype(F32)


def _swa_queries(sq_ref, rows, cosb, sinb, first_half):
    qs = [_rope(sq_ref[rows, p * LANES:(p + 1) * LANES], cosb, sinb, first_half) * 0.125 for p in range(4)]
    return [jnp.concatenate(qs[0:2], axis=0), jnp.concatenate(qs[2:4], axis=0)]


def _swa_fwd(proj, cos, sin, sinks):
    s = proj.shape[0]
    nq = min(SWA_QBLOCKS_FWD, s // SWA_BLOCK)
    tq = nq * SWA_BLOCK

    def body(sq_ref, sz_ref, sk_ref, sv_ref, cos_ref, sin_ref, sinks_ref, os_ref, opre_ref, kprev, vprev):
        n = pl.program_id(0)

        @pl.when(n == 0)
        def _():
            kprev[...] = jnp.zeros_like(kprev)
            vprev[...] = jnp.zeros_like(vprev)

        lo2, _, first_half, prev_mask = _swa_masks()
        prev_mask_b = jnp.where(prev_mask, 1.0, 0.0).astype(BF)
        sink = _sink_row(sinks_ref)
        blocks = range(nq)
        rows = [slice(j * SWA_BLOCK, (j + 1) * SWA_BLOCK) for j in blocks]
        cosb = [cos_ref[rows[j], :] for j in blocks]
        sinb = [sin_ref[rows[j], :] for j in blocks]
        kc = [_rope(sk_ref[rows[j], :], cosb[j], sinb[j], first_half) for j in blocks]
        vc = [sv_ref[rows[j], :] for j in blocks]
        kcat = [jnp.concatenate([kprev[...] if j == 0 else kc[j - 1], kc[j]], axis=0) for j in blocks]
        vcat = [jnp.concatenate([vprev[...] if j == 0 else vc[j - 1], vc[j]], axis=0) for j in blocks]
        kprev[...] = kc[-1]
        vprev[...] = vc[-1]
        kvar = [_kv_variants(kcat[j], lo2) for j in blocks]
        vtvar = [_kv_variants_t(vcat[j]) for j in blocks]
        qg = [[q.astype(BF) for q in _swa_queries(sq_ref, rows[j], cosb[j], sinb[j], first_half)] for j in blocks]
        st = [_swa_scores(qg[j], *kvar[j]) for j in blocks]
        soft = [_swa_softmax(st[j], prev_mask, _no_prev_bias(n) if j == 0 else None, sink) for j in blocks]
        pt = [_split_band(soft[j][0].astype(BF), prev_mask_b) for j in blocks]
        og = {}
        for j in blocks:
            inv = soft[j][2]
            for g in range(2):
                c0, c1, c2 = 512 * g, 512 * g + 256, 512 * g + 512
                ot = (_dot(vtvar[j][0][g], pt[j][:, c0:c1]) * inv[:, c0:c1]
                      + _dot(vtvar[j][1][g], pt[j][:, c1:c2]) * inv[:, c1:c2])
                og[(j, g)] = ot.T
        for j in blocks:
            for g in range(2):
                for i in range(2):
                    ls = slice((2 * g + i) * LANES, (2 * g + i + 1) * LANES)
                    o = og[(j, g)][i * SWA_BLOCK:(i + 1) * SWA_BLOCK]
                    sz = sz_ref[rows[j], ls]
                    opre_ref[rows[j], ls] = o
                    os_ref[rows[j], ls] = (o * (sz * _sigmoid(sz))).astype(os_ref.dtype)

    def col(width, off):
        return pl.BlockSpec((tq, width), lambda i: (i, off // width))

    row = pl.BlockSpec((tq, LANES), lambda i: (i, 0))
    return pl.pallas_call(
        body, name="swa_fwd", grid=(s // tq,),
        in_specs=[col(512, OFF_SQ), col(512, OFF_SZ), col(LANES, OFF_SK), col(LANES, OFF_SV), row, row,
                  pl.BlockSpec(memory_space=pltpu.SMEM)],
        out_specs=[pl.BlockSpec((tq, 512), lambda i: (i, 0))] * 2,
        out_shape=[jax.ShapeDtypeStruct((s, 512), BF), jax.ShapeDtypeStruct((s, 512), F32)],
        scratch_shapes=[pltpu.VMEM((SWA_BLOCK, LANES), F32)] * 2,
        compiler_params=_params(("arbitrary",)),
    )(proj, proj, proj, proj, cos, sin, sinks)


def _swa_bwd(proj, dos, opre, cos, sin, sinks):
    s = proj.shape[0]
    nq = min(SWA_QBLOCKS, s // SWA_BLOCK)
    tq = nq * SWA_BLOCK

    def body(sq_ref, sz_ref, sk_ref, sv_ref, dos_ref, opre_ref, cos_ref, sin_ref, sinks_ref,
             dsq_ref, dsz_ref, dsk_ref, dsv_ref, dsink_ref, kprev, vprev, cprev, sprev):
        n = pl.program_id(0)

        @pl.when(n == 0)
        def _():
            kprev[...] = jnp.zeros_like(kprev)
            vprev[...] = jnp.zeros_like(vprev)
            cprev[...] = jnp.zeros_like(cprev)
            sprev[...] = jnp.zeros_like(sprev)
            for hd in range(SWA_HEADS):
                dsink_ref[0, hd] = 0.0

        lo2, lo1, first_half, prev_mask = _swa_masks()
        prev_mask_b = jnp.where(prev_mask, 1.0, 0.0).astype(BF)
        lo1s = jnp.concatenate([lo1, lo1], axis=0)
        sink = _sink_row(sinks_ref)

        def home(m0, m1):
            t0 = m0 + pltpu.roll(m0, 64, 1)
            t1 = m1 + pltpu.roll(m1, 64, 1)
            return jnp.where(lo2, t0, t1)

        kp, vp, cp_, sp_ = kprev[...], vprev[...], cprev[...], sprev[...]
        for j in range(nq):
            rows = slice(j * SWA_BLOCK, (j + 1) * SWA_BLOCK)
            blk = n * nq + j
            cosb, sinb = cos_ref[rows, :], sin_ref[rows, :]
            kc = _rope(sk_ref[rows, :], cosb, sinb, first_half)
            vc = sv_ref[rows, :]
            kcat = jnp.concatenate([kp, kc], axis=0)
            k_lo, k_hi = _kv_variants(kcat, lo2)
            kt_lo, kt_hi = _kv_variants_t(kcat)
            v_lo, v_hi = _kv_variants(jnp.concatenate([vp, vc], axis=0), lo2)
            qg32 = _swa_queries(sq_ref, rows, cosb, sinb, first_half)
            qg = [q.astype(BF) for q in qg32]
            ex, es, inv = _swa_softmax(_swa_scores(qg, k_lo, k_hi), prev_mask, _no_prev_bias(n) if j == 0 else None, sink)
            pr, ps = ex * inv, es * inv

            dog32 = []
            for g in range(2):
                parts = []
                for i in range(2):
                    ls = slice((2 * g + i) * LANES, (2 * g + i + 1) * LANES)
                    sz = sz_ref[rows, ls]
                    sg = _sigmoid(sz)
                    dos_p = dos_ref[rows, ls]
                    dsz_ref[rows, ls] = (dos_p * opre_ref[rows, ls] * (sg * (1.0 + sz * (1.0 - sg)))).astype(dsz_ref.dtype)
                    parts.append(dos_p * (sz * sg))
                dog32.append(jnp.concatenate(parts, axis=0))
            dog = [t.astype(BF) for t in dog32]
            dpr = _merge_band(jnp.concatenate([_dot(v_lo[0], dog[0], NT), _dot(v_hi[0], dog[0], NT),
                                               _dot(v_lo[1], dog[1], NT), _dot(v_hi[1], dog[1], NT)], axis=1), prev_mask)
            rd = jnp.sum(pr * dpr, axis=0, keepdims=True)
            ds = _split_band((pr * (dpr - rd)).astype(BF), prev_mask_b)
            prb = _split_band(pr.astype(BF), prev_mask_b)
            sink_term = ps * rd
            for r, hd in enumerate(_SWA_COL_HEADS):
                dsink_ref[0, hd] += -jnp.sum(sink_term[:, r * SWA_BLOCK:(r + 1) * SWA_BLOCK])

            dk_g, dv_g = [], []
            for g in range(2):
                c0, c1, c2 = 512 * g, 512 * g + 256, 512 * g + 512
                dq = (_dot(kt_lo[g], ds[:, c0:c1]) + _dot(kt_hi[g], ds[:, c1:c2])).T
                for i in range(2):
                    ls = slice((2 * g + i) * LANES, (2 * g + i + 1) * LANES)
                    dsq_ref[rows, ls] = _rope_t(dq[i * SWA_BLOCK:(i + 1) * SWA_BLOCK] * 0.125, cosb, sinb,
                                                first_half).astype(dsq_ref.dtype)
                q_split = jnp.concatenate([jnp.where(lo1s, qg32[g], 0.0), jnp.where(lo1s, 0.0, qg32[g])], axis=0).astype(BF)
                do_split = jnp.concatenate([jnp.where(lo1s, dog32[g], 0.0), jnp.where(lo1s, 0.0, dog32[g])], axis=0).astype(BF)
                dk_g.append(_dot(ds[:, c0:c2], q_split))
                dv_g.append(_dot(prb[:, c0:c2], do_split))
            dk = home(dk_g[0], dk_g[1])
            dv = home(dv_g[0], dv_g[1])
            cur = pl.ds(pl.multiple_of(blk * SWA_BLOCK, SWA_BLOCK), SWA_BLOCK)
            dsk_ref[cur, :] = _rope_t(dk[SWA_BLOCK:], cosb, sinb, first_half)
            dsv_ref[cur, :] = dv[SWA_BLOCK:]
            dk_prev = _rope_t(dk[:SWA_BLOCK], cp_, sp_, first_half)
            dv_prev = dv[:SWA_BLOCK]
            if j == 0:
                @pl.when(n > 0)
                def _():
                    prv = pl.ds(pl.multiple_of((blk - 1) * SWA_BLOCK, SWA_BLOCK), SWA_BLOCK)
                    dsk_ref[prv, :] += dk_prev
                    dsv_ref[prv, :] += dv_prev
            else:
                prv = pl.ds(pl.multiple_of((blk - 1) * SWA_BLOCK, SWA_BLOCK), SWA_BLOCK)
                dsk_ref[prv, :] += dk_prev
                dsv_ref[prv, :] += dv_prev
            kp, vp, cp_, sp_ = kc, vc, cosb, sinb
        kprev[...] = kp
        vprev[...] = vp
        cprev[...] = cp_
        sprev[...] = sp_

    def col(width, off):
        return pl.BlockSpec((tq, width), lambda i: (i, off // width))

    row = pl.BlockSpec((tq, LANES), lambda i: (i, 0))
    wide = pl.BlockSpec((tq, 512), lambda i: (i, 0))
    return pl.pallas_call(
        body, name="swa_bwd", grid=(s // tq,),
        in_specs=[col(512, OFF_SQ), col(512, OFF_SZ), col(LANES, OFF_SK), col(LANES, OFF_SV), wide, wide, row, row,
                  pl.BlockSpec(memory_space=pltpu.SMEM)],
        out_specs=[wide, wide, _full((s, LANES)), _full((s, LANES)), pl.BlockSpec(memory_space=pltpu.SMEM)],
        out_shape=[jax.ShapeDtypeStruct((s, 512), BF), jax.ShapeDtypeStruct((s, 512), BF),
                   jax.ShapeDtypeStruct((s, LANES), F32), jax.ShapeDtypeStruct((s, LANES), F32),
                   jax.ShapeDtypeStruct((1, SWA_HEADS), F32)],
        scratch_shapes=[pltpu.VMEM((SWA_BLOCK, LANES), F32)] * 4,
        compiler_params=_params(("arbitrary",)),
    )(proj, proj, proj, proj, dos, opre, cos, sin, sinks)


def _outproj(og, osw, w_out, x2d, target, gate, g_final):
    s = x2d.shape[0]
    tm = min(512, s)

    def body(og_ref, os_ref, w_ref, x_ref, t_ref, gate_ref, gf_ref,
             dx2_ref, dog_ref, dos_ref, dw_ref, loss_ref, dgf_ref, dgate_ref):
        @pl.when(pl.program_id(0) == 0)
        def _():
            dw_ref[...] = jnp.zeros_like(dw_ref)
            loss_ref[...] = jnp.zeros_like(loss_ref)
            dgf_ref[...] = jnp.zeros_like(dgf_ref)
            dgate_ref[...] = jnp.zeros_like(dgate_ref)

        w = w_ref[...]
        gate, gf = gate_ref[...], gf_ref[...]
        subs = _subtiles(tm)
        ogv = [og_ref[sl, :] for sl in subs]
        osv = [os_ref[sl, :] for sl in subs]
        y = [_dot(ogv[k], w[:512]) + _dot(osv[k], w[512:]) for k in range(len(subs))]
        dys = []
        for k, sl in enumerate(subs):
            x2 = x_ref[sl, :] + gate * y[k]
            r = lax.rsqrt(jnp.mean(x2 * x2, axis=-1, keepdims=True) + RMS_EPS)
            xn = x2 * r
            err = xn * gf - t_ref[sl, :]
            loss_ref[...] += 0.5 * jnp.sum(jnp.mean(err * err, axis=-1, keepdims=True), axis=0, keepdims=True)
            dyf = err * (1.0 / D_MODEL)
            dgf_ref[...] += jnp.sum(dyf * xn, axis=0, keepdims=True)
            t = dyf * gf
            dx2 = r * (t - xn * jnp.mean(t * xn, axis=-1, keepdims=True))
            dx2_ref[sl, :] = dx2
            dgate_ref[...] += jnp.sum(dx2 * y[k], axis=0, keepdims=True)
            dys.append((dx2 * gate).astype(BF))
            dmix = _dot(dys[k], w, NT)
            dog_ref[sl, :] = dmix[:, :512]
            dos_ref[sl, :] = dmix[:, 512:]
        dy = jnp.concatenate(dys, axis=0)
        dw_ref[:512, :] += _dot(og_ref[...], dy, TN)
        dw_ref[512:, :] += _dot(os_ref[...], dy, TN)

    half = pl.BlockSpec((tm, 512), lambda i: (i, 0))
    rowb = pl.BlockSpec((tm, D_MODEL), lambda i: (i, 0))
    vec = _full((1, D_MODEL))
    return pl.pallas_call(
        body, name="outproj", grid=(s // tm,),
        in_specs=[half, half, _full((D_MODEL, D_MODEL)), rowb, rowb, vec, vec],
        out_specs=[rowb, half, half, _full((D_MODEL, D_MODEL)), _full((1, 1)), vec, vec],
        out_shape=[jax.ShapeDtypeStruct((s, D_MODEL), F32), jax.ShapeDtypeStruct((s, 512), F32),
                   jax.ShapeDtypeStruct((s, 512), F32), jax.ShapeDtypeStruct((D_MODEL, D_MODEL), F32),
                   jax.ShapeDtypeStruct((1, 1), F32), jax.ShapeDtypeStruct((1, D_MODEL), F32),
                   jax.ShapeDtypeStruct((1, D_MODEL), F32)],
        compiler_params=_params(("arbitrary",)),
    )(og, osw, w_out, x2d, target, gate, g_final)


_PIECES = ((OFF_QK, 512), (OFF_V, 512), (OFF_GZ, 512), (OFF_SQ, 512), (OFF_SZ, 512),
           (OFF_SK, LANES), (OFF_SV, LANES), (OFF_GA, LANES))

_UNPAD_ROWS = ((OFF_QK, 0, 1024),
               (OFF_GA, 1024, GLA_RANK),
               (OFF_GZ, 1040, 1024),
               (OFF_SK, 2064, 256),
               (OFF_SZ, 2320, 512))


def _inproj_bwd(x2d, shift, sc1p, g_norm, wpad_t, dx2, pieces):
    s = x2d.shape[0]
    tm = min(512, s)
    nsteps = s // tm

    def body(x_ref, sh_ref, sc_ref, g_ref, w_hbm, dx2_ref, *rest):
        piece_refs = rest[:len(_PIECES)]
        gx_ref, dw_hbm, dsh_ref, dsc_ref, dg_ref, w_vm, dw_vm, sem, out_sems = rest[len(_PIECES):]
        i = pl.program_id(0)

        @pl.when(i == 0)
        def _():
            cp = pltpu.make_async_copy(w_hbm, w_vm, sem)
            cp.start()
            dw_vm[...] = jnp.zeros_like(dw_vm)
            dsh_ref[...] = jnp.zeros_like(dsh_ref)
            dsc_ref[...] = jnp.zeros_like(dsc_ref)
            dg_ref[...] = jnp.zeros_like(dg_ref)
            cp.wait()

        g, sc1p_v, shift_v = g_ref[...], sc_ref[...], sh_ref[...]
        subs = _subtiles(tm)
        dhs = []
        for sl in subs:
            dh = None
            for (off, width), pr in zip(_PIECES, piece_refs):
                part = _dot(pr[sl, :].astype(BF), w_vm[off:off + width, :])
                dh = part if dh is None else dh + part
            dhs.append(dh)
        norm = [_modnorm(x_ref[sl, :], g, sc1p_v, shift_v) for sl in subs]
        hb = jnp.concatenate([h.astype(BF) for _, _, h in norm], axis=0)
        for (off, width), pr in zip(_PIECES, piece_refs):
            dw_vm[off:off + width, :] += _dot(pr[...].astype(BF), hb, TN)
        for sl, (xn, r, _), dh in zip(subs, norm, dhs):
            dsh_ref[...] += jnp.sum(dh, axis=0, keepdims=True)
            dsc_ref[...] += jnp.sum(dh * (xn * g), axis=0, keepdims=True)
            dg_ref[...] += jnp.sum(dh * xn * sc1p_v, axis=0, keepdims=True)
            dxn = dh * g * sc1p_v
            gx_ref[sl, :] = dx2_ref[sl, :] + r * (dxn - xn * jnp.mean(dxn * xn, axis=-1, keepdims=True))

        @pl.when(i == nsteps - 1)
        def _():
            copies = [pltpu.make_async_copy(dw_vm.at[src:src + n], dw_hbm.at[dst:dst + n], out_sems.at[k])
                      for k, (src, dst, n) in enumerate(_UNPAD_ROWS)]
            for cp in copies:
                cp.start()
            for cp in copies:
                cp.wait()

    rowb = pl.BlockSpec((tm, D_MODEL), lambda i: (i, 0))
    vec = _full((1, D_MODEL))
    anyspec = pl.BlockSpec(memory_space=pl.ANY)
    piece_specs = [pl.BlockSpec((tm, width), lambda i: (i, 0)) for _, width in _PIECES]
    return pl.pallas_call(
        body, name="inproj_bwd", grid=(nsteps,),
        in_specs=[rowb, vec, vec, vec, anyspec, rowb] + piece_specs,
        out_specs=[rowb, anyspec, vec, vec, vec],
        out_shape=[jax.ShapeDtypeStruct((s, D_MODEL), F32), jax.ShapeDtypeStruct((D_IN, D_MODEL), F32),
                   jax.ShapeDtypeStruct((1, D_MODEL), F32), jax.ShapeDtypeStruct((1, D_MODEL), F32),
                   jax.ShapeDtypeStruct((1, D_MODEL), F32)],
        scratch_shapes=[pltpu.VMEM((D_PAD, D_MODEL), BF), pltpu.VMEM((D_PAD, D_MODEL), F32), pltpu.SemaphoreType.DMA,
                        pltpu.SemaphoreType.DMA((len(_UNPAD_ROWS),))],
        compiler_params=_params(("arbitrary",)),
    )(x2d, shift, sc1p, g_norm, wpad_t, dx2, *pieces)


def _adam(w, g, m, v):
    m2 = ADAM_B1 * m + (1.0 - ADAM_B1) * g
    v2 = ADAM_B2 * v + (1.0 - ADAM_B2) * (g * g)
    m_hat = m2 / (1.0 - ADAM_B1 ** ADAM_STEP)
    v_hat = v2 / (1.0 - ADAM_B2 ** ADAM_STEP)
    delta = -ADAM_LR * (m_hat / (jnp.sqrt(v_hat) + ADAM_EPS) + ADAM_WD * w)
    return delta, m2, v2


def _adamw(w, g, m, v, name):
    rr, cc = w.shape
    tc = min(512, cc)

    def body(w_ref, g_ref, m_ref, v_ref, d_ref, m2_ref, v2_ref):
        d_ref[...], m2_ref[...], v2_ref[...] = _adam(w_ref[...], g_ref[...], m_ref[...], v_ref[...])

    blk = pl.BlockSpec((rr, tc), lambda i: (0, i))
    return pl.pallas_call(
        body, name=name, grid=(cc // tc,), in_specs=[blk] * 4, out_specs=[blk] * 3,
        out_shape=[jax.ShapeDtypeStruct((rr, cc), F32)] * 3,
        compiler_params=_params(("arbitrary",)),
    )(w, g, m, v)


def _adamw_t(w3, g, m3, v3, name):
    rr, _, cc = w3.shape
    tc = cc

    def body(w_hbm, g_ref, m_hbm, v_hbm, d_hbm, m2_hbm, v2_hbm, g3_hbm, w_vm, m_vm, v_vm, d_vm, m2_vm, v2_vm, in_sems, out_sems):
        cols = pl.ds(pl.multiple_of(pl.program_id(0) * tc, tc), tc)
        loads = [pltpu.make_async_copy(src.at[:, 0, cols], dst, in_sems.at[k])
                 for k, (src, dst) in enumerate(((w_hbm, w_vm), (m_hbm, m_vm), (v_hbm, v_vm)))]
        for cp in loads:
            cp.start()
        for cp in loads:
            cp.wait()
        d_vm[...], m2_vm[...], v2_vm[...] = _adam(w_vm[...], g_ref[...], m_vm[...], v_vm[...])
        stores = [pltpu.make_async_copy(src, dst.at[:, 0, cols], out_sems.at[k])
                  for k, (src, dst) in enumerate(((d_vm, d_hbm), (m2_vm, m2_hbm), (v2_vm, v2_hbm), (g_ref, g3_hbm)))]
        for cp in stores:
            cp.start()
        for cp in stores:
            cp.wait()

    hbm = pl.BlockSpec(memory_space=pl.ANY)
    return pl.pallas_call(
        body, name=name, grid=(cc // tc,), in_specs=[hbm, pl.BlockSpec((rr, tc), lambda i: (0, i)), hbm, hbm],
        out_specs=[hbm] * 4, out_shape=[jax.ShapeDtypeStruct((rr, 1, cc), F32)] * 4,
        scratch_shapes=[pltpu.VMEM((rr, tc), F32)] * 6 + [pltpu.SemaphoreType.DMA((3,)), pltpu.SemaphoreType.DMA((4,))],
        compiler_params=_params(("arbitrary",)),
    )(w3, g, m3, v3)


def _ada_update(c_all, dmod_cols, w, m, v):
    rr, cc = w.shape
    tr = min(512, rr)
    c_all = jnp.pad(c_all, ((0, 8), (0, 0)))
    dmod_cols = jnp.pad(dmod_cols, ((0, 8), (0, 0)))

    def body(c_ref, dm_ref, w_ref, m_ref, v_ref, g_ref, d_ref, m2_ref, v2_ref):
        cv = c_ref[...]
        sc = (cv * _sigmoid(cv)).astype(BF)
        g = _dot(sc, dm_ref[...].astype(BF), TN)
        g_ref[...] = g
        d_ref[...], m2_ref[...], v2_ref[...] = _adam(w_ref[...], g, m_ref[...], v_ref[...])

    blk = pl.BlockSpec((tr, cc), lambda i: (i, 0))
    return pl.pallas_call(
        body, name="ada_update", grid=(rr // tr,),
        in_specs=[pl.BlockSpec((16, tr), lambda i: (0, i)), _full((16, cc)), blk, blk, blk],
        out_specs=[blk] * 4, out_shape=[jax.ShapeDtypeStruct((rr, cc), F32)] * 4,
        compiler_params=_params(("arbitrary",)),
    )(c_all, dmod_cols, w, m, v)


def _small_update(parts, weights, moms, vels):
    n = len(weights)

    def body(*refs):
        p_refs, w_refs, m_refs, v_refs = refs[:n + 1], refs[n + 1:2 * n + 1], refs[2 * n + 1:3 * n + 1], refs[3 * n + 1:4 * n + 1]
        outs = refs[4 * n + 1:]
        for i in range(n):
            g = p_refs[i][0]
            for d in range(1, 8):
                g = g + p_refs[i][d]
            delta, m2, v2 = _adam(w_refs[i][...], g, m_refs[i][...], v_refs[i][...])
            outs[4 * i][...] = g
            outs[4 * i + 1][...] = delta
            outs[4 * i + 2][...] = m2
            outs[4 * i + 3][...] = v2
        tot = p_refs[n][0]
        for d in range(1, 8):
            tot = tot + p_refs[n][d]
        outs[4 * n][...] = tot

    out_shape = []
    for w in weights:
        out_shape += [jax.ShapeDtypeStruct(w.shape, F32)] * 4
    out_shape.append(jax.ShapeDtypeStruct(parts[n].shape[1:], F32))
    return pl.pallas_call(body, name="small_update", out_shape=out_shape, compiler_params=_params())(
        *parts, *weights, *moms, *vels)


def _pad_w_in_t(w):
    w = w.reshape(-1, w.shape[2])
    pad = jnp.zeros((LANES - GLA_RANK, w.shape[1]), w.dtype)
    return jnp.concatenate([w[dst:dst + n] for _, dst, n in sorted(_UNPAD_ROWS)] + [pad], axis=0)


def _rows8(a):
    flat = a.reshape(-1)
    rows = -(-flat.shape[0] // LANES)
    rows8 = -(-rows // 8) * 8
    flat = jnp.pad(flat, (0, rows8 * LANES - flat.shape[0]))
    return flat.reshape(rows8, LANES)


def kernel(x, c, positions, w_ada, b_ada, g_norm, w_in, w_decay, b_decay, g_gla_head, sinks, w_out, g_final, loss_target, m_w_ada, m_b_ada, m_g_norm, m_w_in, m_w_decay, m_b_decay, m_g_gla_head, m_sinks, m_w_out, m_g_final, v_w_ada, v_b_ada, v_g_norm, v_w_in, v_w_decay, v_b_decay, v_g_gla_head, v_sinks, v_w_out, v_g_final):
    ax, ay, ac = lax.axis_index("x"), lax.axis_index("y"), lax.axis_index("c")
    chip = 2 * ax + ay
    dev = 2 * chip + ac
    s = x.shape[1]
    x2d = x[0]
    target = loss_target[0]
    w_ada2, w_out2, w_dec2 = w_ada[0], w_out[0], w_decay[0]
    w_in_t = w_in[0].T
    ada_cols = w_ada2.shape[1]
    in_cols = w_in_t.shape[0]
    out_rows = w_out2.shape[0]
    half = D_MODEL // 2

    cw = jnp.concatenate([c.reshape(8, LANES), w_dec2.reshape(8, LANES)], axis=0)
    b_shard = lax.dynamic_slice(b_ada, (0, chip * ada_cols), (1, ada_cols))
    half_in = lax.dynamic_slice(w_in_t, (0, ac * half), (in_cols, half)).astype(BF)
    half_out = lax.dynamic_slice(w_out2, (ac * (out_rows // 2), 0), (out_rows // 2, D_MODEL)).astype(BF)
    inv_freq = 1.0 / (ROPE_THETA ** (jnp.arange(0, 64, 2, dtype=F32) / 64))
    first, mod_all, w_in_all, w_out_all, cos, sin = _prologue(
        cw, w_ada2, b_shard, half_in, half_out, positions.reshape(s, 1), jnp.tile(inv_freq, 4).reshape(1, LANES))

    first = first.reshape(8, 2, 8, LANES)
    c_all = first[:, 0].reshape(8, D_MODEL)
    w_dec_full = first[0::2, 1].reshape(4, GLA_RANK, 64).transpose(1, 0, 2).reshape(GLA_RANK, 256)
    mod = mod_all.reshape(4, 2, 8, ada_cols)[:, 0]
    mod = lax.dynamic_slice(mod, (0, dev, 0), (4, 1, ada_cols)).reshape(1, 4 * ada_cols)
    shift, sc1p, gate = mod[:, :D_MODEL], 1.0 + mod[:, D_MODEL:2 * D_MODEL], mod[:, 2 * D_MODEL:]
    wpad_t = _pad_w_in_t(w_in_all)
    w_out_all = w_out_all.reshape(D_MODEL, D_MODEL)

    wdecp = jnp.pad(w_dec_full, ((0, LANES - GLA_RANK), (0, 0))).astype(BF)

    proj = _inproj_fwd(x2d, shift, sc1p, g_norm, wpad_t)
    og, o_gla, sprev = _gla_fwd(proj, wdecp, b_decay, g_gla_head)
    osw, o_swa = _swa_fwd(proj, cos, sin, sinks)
    dx2, dog, dos, dw_out, loss_p, dgf, dgate = _outproj(og, osw, w_out_all, x2d, target, gate, g_final.reshape(1, D_MODEL))
    dsq, dsz, dsk, dsv, dsinks = _swa_bwd(proj, dos, o_swa, cos, sin, sinks)
    dqk, dv, dgz, dga, dwdp, dbd, dgg = _gla_bwd(proj, dog, o_gla, sprev, wdecp, b_decay, g_gla_head)
    pieces = (dqk, dv, dgz, dsq, dsz, dsk, dsv, dga)
    gx, dw_in_t, dshift, dscale, dgn = _inproj_bwd(x2d, shift, sc1p, g_norm, wpad_t, dx2, pieces)

    segs = [jnp.concatenate([dshift, dscale, dgate], axis=1), dgn, dgf, dwdp[:GLA_RANK], dbd, dgg, dsinks, loss_p]
    packed = [_rows8(a) for a in segs]
    offs = [0]
    for a in packed:
        offs.append(offs[-1] + a.shape[0])
    g_w_in_t, g_w_out, small = _epilogue(dw_in_t.reshape(4, in_cols, D_MODEL), dw_out.reshape(4, out_rows, D_MODEL),
                                         jnp.concatenate(packed, axis=0))

    def seg(i, size):
        return small[:, offs[i]:offs[i + 1]].reshape(8, -1)[:, :size]

    dmod_all = seg(0, 3 * D_MODEL)
    dwd_all = lax.dynamic_slice(seg(3, GLA_RANK * 256).reshape(8, GLA_RANK, 256), (0, 0, chip * 64), (8, GLA_RANK, 64))
    parts = [dmod_all.reshape(8, 1, 3 * D_MODEL), seg(1, D_MODEL).reshape(8, 1, D_MODEL), dwd_all,
             seg(4, 256).reshape(8, 1, 256), seg(5, 512).reshape(8, 1, 512), seg(6, SWA_HEADS).reshape(8, 1, SWA_HEADS),
             seg(2, D_MODEL).reshape(8, 1, D_MODEL), seg(7, LANES).reshape(8, 1, LANES)]
    smalls = _small_update(
        parts,
        [b_ada, g_norm, w_dec2, b_decay, g_gla_head, sinks, g_final.reshape(1, D_MODEL)],
        [m_b_ada, m_g_norm, m_w_decay[0], m_b_decay, m_g_gla_head, m_sinks, m_g_final.reshape(1, D_MODEL)],
        [v_b_ada, v_g_norm, v_w_decay[0], v_b_decay, v_g_gla_head, v_sinks, v_g_final.reshape(1, D_MODEL)])
    (g_b_ada, d_b_ada, nm_b_ada, nv_b_ada, g_gn, d_gn, nm_gn, nv_gn, g_wd, d_wd, nm_wd, nv_wd,
     g_bd, d_bd, nm_bd, nv_bd, g_gg, d_gg, nm_gg, nv_gg, g_sk, d_sk, nm_sk, nv_sk,
     g_gf, d_gf, nm_gf, nv_gf, loss_row) = smalls
    loss = loss_row[0, 0]

    dmod_cols = lax.dynamic_slice(dmod_all, (0, chip * ada_cols), (8, ada_cols))
    g_w_ada, d_w_ada, nm_w_ada, nv_w_ada = _ada_update(c_all, dmod_cols, w_ada2, m_w_ada[0], v_w_ada[0])
    to3 = lambda a: jnp.transpose(a, (2, 0, 1))
    from3 = lambda a: jnp.transpose(a, (1, 2, 0))[0]
    d3, nm3, nv3, g3 = _adamw_t(to3(w_in), g_w_in_t, to3(m_w_in), to3(v_w_in), "adamw_w_in")
    g_w_in, d_w_in, nm_w_in, nv_w_in = from3(g3), from3(d3), from3(nm3), from3(nv3)
    d_w_out, nm_w_out, nv_w_out = _adamw(w_out2, g_w_out, m_w_out[0], v_w_out[0], "adamw_w_out")

    flat = lambda a: a.reshape(D_MODEL)
    grads = [g_w_ada[None], g_b_ada, g_gn, g_w_in[None], g_wd[None], g_bd, g_gg, g_sk, g_w_out[None], flat(g_gf)]
    deltas = [d_w_ada[None], d_b_ada, d_gn, d_w_in[None], d_wd[None], d_bd, d_gg, d_sk, d_w_out[None], flat(d_gf)]
    new_m = [nm_w_ada[None], nm_b_ada, nm_gn, nm_w_in[None], nm_wd[None], nm_bd, nm_gg, nm_sk, nm_w_out[None], flat(nm_gf)]
    new_v = [nv_w_ada[None], nv_b_ada, nv_gn, nv_w_in[None], nv_wd[None], nv_bd, nv_gg, nv_sk, nv_w_out[None], flat(nv_gf)]
    return (loss, gx[None], *grads, *deltas, *new_m, *new_v)
```

```python
import jax
import jax.numpy as jnp
from jax import lax
from jax.experimental import pallas as pl
from jax.experimental.pallas import tpu as pltpu

F32 = jnp.float32
BF = jnp.bfloat16

D_MODEL = 1024
GLA_HEADS = 4
GLA_DK = 64
GLA_CHUNK = 64
GLA_RANK = 16
GLA_TAU = 16.0
GLA_SUB = 256
GLA_ROWS_FWD = 1024
GLA_ROWS_BWD = 512
SWA_HEADS = 8
SWA_BLOCK = 128
SWA_QBLOCKS_FWD = 8
SWA_QBLOCKS = 8
RMS_EPS = 1e-6
ROPE_THETA = 10000.0

OFF_QK, OFF_V, OFF_GZ, OFF_SQ, OFF_SZ, OFF_SK, OFF_SV, OFF_GA = 0, 512, 1024, 1536, 2048, 2560, 2688, 2816
D_PAD = 2944
D_IN = 2832
LANES = 128
VMEM_LIMIT = 56 * 1024 * 1024

ADAM_LR, ADAM_B1, ADAM_B2, ADAM_EPS, ADAM_WD, ADAM_STEP = 0.001, 0.9, 0.999, 1e-08, 0.01, 10

NT = (((1,), (1,)), ((), ()))
TN = (((0,), (0,)), ((), ()))
MESH = pl.DeviceIdType.MESH


def _dot(a, b, dims=None):
    if dims is None:
        return jnp.dot(a, b, preferred_element_type=F32)
    return lax.dot_general(a, b, dims, preferred_element_type=F32)


def _sigmoid(x):
    return 1.0 / (1.0 + jnp.exp(-x))


def _params(sem=None):
    return pltpu.CompilerParams(dimension_semantics=sem, vmem_limit_bytes=VMEM_LIMIT)


def _full(shape):
    return pl.BlockSpec(shape, lambda i: (0,) * len(shape))


def _subtiles(rows, size=256):
    size = min(size, rows)
    return [slice(k * size, (k + 1) * size) for k in range(rows // size)]


_GATHER_SEMS = [pltpu.SemaphoreType.DMA((7,)), pltpu.SemaphoreType.DMA((7,)), pltpu.SemaphoreType.DMA]


class _Gather:
    def __init__(self, x_ref, out_ref, send_sems, recv_sems, local_sem, slab=None):
        self.slab_of = slab
        x, y, c = lax.axis_index("x"), lax.axis_index("y"), lax.axis_index("c")
        self.me, self.sibling, self.c = (x, y, c), (x, y, 1 - c), c
        self.xn, self.yn, self.dg = (1 - x, y), (x, 1 - y), (1 - x, 1 - y)
        self.pass_from = (lax.rem(x + 1 - c, 2), lax.rem(y + c, 2))
        self.pass_to = (lax.rem(x + c, 2), lax.rem(y + 1 - c, 2))
        self.x_ref, self.out_ref, self.send_sems, self.recv_sems = x_ref, out_ref, send_sems, recv_sems
        self.mine = pltpu.make_async_copy(x_ref, self._slab(*self.me), local_sem)

    def _slab(self, px, py, pc):
        if self.slab_of is not None:
            return self.slab_of(self.out_ref, px, py, pc)
        return self.out_ref.at[4 * px + 2 * py + pc]

    def _copy(self, k, blk, to, src=None):
        return pltpu.make_async_remote_copy(
            src_ref=self._slab(*blk) if src is None else src, dst_ref=self._slab(*blk),
            send_sem=self.send_sems.at[k], recv_sem=self.recv_sems.at[k], device_id=to, device_id_type=MESH)

    def _sends(self):
        c = self.c
        return [self._copy(0, self.me, self.sibling, src=self.x_ref),
                self._copy(1, self.me, (*self.xn, c), src=self.x_ref),
                self._copy(2, self.me, (*self.yn, c), src=self.x_ref),
                self._copy(3, (*self.pass_from, c), (*self.pass_to, c)),
                self._copy(4, (*self.xn, c), self.sibling),
                self._copy(5, (*self.yn, c), self.sibling),
                self._copy(6, (*self.dg, c), self.sibling)]

    def start(self):
        self.mine.start()
        for cp in self._sends()[0:3]:
            cp.start()

    def pass_on(self):
        sends = self._sends()
        self._copy(1, (*self.xn, self.c), self.me).wait_recv()
        self._copy(2, (*self.yn, self.c), self.me).wait_recv()
        for k in (3, 4, 5):
            sends[k].start()

    def relay_diagonal(self):
        self._copy(3, (*self.dg, self.c), self.me).wait_recv()
        self._sends()[6].start()

    def relay(self):
        self.pass_on()
        self.relay_diagonal()

    def finish(self):
        c = self.c
        self._copy(0, self.sibling, self.me).wait_recv()
        for k, chip in ((4, self.xn), (5, self.yn), (6, self.dg)):
            self._copy(k, (*chip, 1 - c), self.me).wait_recv()
        for cp in self._sends():
            cp.wait_send()
        self.mine.wait()


def _prologue(cw, w_ada, b_shard, half_in, half_out, pos_col, inv_freq):
    s = pos_col.shape[0]
    rt = min(512, s)

    def body(cw_ref, wada_hbm, b_ref, hin_ref, hout_ref, pos_hbm, f_ref,
             first_ref, mod_ref, win_ref, wout_ref, cos_hbm, sin_hbm,
             mod_blk, cos_ref, sin_ref, wada_ref, pos_ref, table_sems, local_sems, *sems):
        fetch_w = pltpu.make_async_copy(wada_hbm, wada_ref, local_sems.at[0])
        fetch_p = pltpu.make_async_copy(pos_hbm, pos_ref, local_sems.at[1])
        fetch_w.start()
        fetch_p.start()
        g_c = _Gather(cw_ref, first_ref, *sems[0:3])
        half_lanes = hin_ref.shape[1]
        g_in = _Gather(hin_ref, win_ref, *sems[3:6],
                       slab=lambda ref, px, py, pc: ref.at[2 * px + py, :, pl.ds(pl.multiple_of(pc * half_lanes, half_lanes), half_lanes)])
        g_out = _Gather(hout_ref, wout_ref, *sems[6:9])
        g_mod = _Gather(mod_blk, mod_ref, *sems[9:12])
        g_c.start()
        g_in.start()
        g_out.start()
        g_c.relay()
        g_c.finish()
        c_rows = [jnp.concatenate([first_ref[d, r:r + 1, :] for r in range(8)], axis=1) for d in range(8)]
        c_all = jnp.concatenate(c_rows, axis=0)
        sc = (c_all * _sigmoid(c_all)).astype(BF)
        fetch_w.wait()
        mod_blk[...] = _dot(sc, wada_ref[...].astype(BF)) + b_ref[...]
        g_mod.start()
        fetch_p.wait()

        def rope_rows(i, carry):
            rows = pl.ds(pl.multiple_of(i * rt, rt), rt)
            ang = pos_ref[rows, :].astype(F32) * f_ref[...]
            lane = lax.broadcasted_iota(jnp.int32, ang.shape, 1)
            cos_ref[rows, :] = jnp.cos(ang)
            sn = jnp.sin(ang)
            sin_ref[rows, :] = jnp.where((lane % 64) < 32, -sn, sn)
            pltpu.make_async_copy(cos_ref.at[rows, :], cos_hbm.at[rows, :], table_sems.at[0]).start()
            pltpu.make_async_copy(sin_ref.at[rows, :], sin_hbm.at[rows, :], table_sems.at[1]).start()
            return carry

        steps = s // rt
        lax.fori_loop(0, steps // 2, rope_rows, 0)
        g_in.pass_on()
        g_out.pass_on()
        lax.fori_loop(steps // 2, steps, rope_rows, 0)
        g_in.relay_diagonal()
        g_out.relay_diagonal()
        g_mod.relay()
        g_in.finish()
        g_out.finish()
        g_mod.finish()
        pltpu.make_async_copy(cos_ref, cos_hbm, table_sems.at[0]).wait()
        pltpu.make_async_copy(sin_ref, sin_hbm, table_sems.at[1]).wait()

    vm = pl.BlockSpec(memory_space=pltpu.VMEM)
    hbm = pl.BlockSpec(memory_space=pl.ANY)
    return pl.pallas_call(
        body, name="prologue",
        out_shape=[jax.ShapeDtypeStruct((8,) + cw.shape, F32), jax.ShapeDtypeStruct((8, 8, w_ada.shape[1]), F32),
                   jax.ShapeDtypeStruct((4, half_in.shape[0], 2 * half_in.shape[1]), half_in.dtype),
                   jax.ShapeDtypeStruct((8,) + half_out.shape, half_out.dtype),
                   jax.ShapeDtypeStruct((s, LANES), F32), jax.ShapeDtypeStruct((s, LANES), F32)],
        in_specs=[vm, hbm, vm, hbm, hbm, hbm, vm], out_specs=[vm, vm, hbm, hbm, hbm, hbm],
        scratch_shapes=[pltpu.VMEM((8, w_ada.shape[1]), F32), pltpu.VMEM((s, LANES), F32), pltpu.VMEM((s, LANES), F32),
                        pltpu.VMEM(w_ada.shape, F32), pltpu.VMEM(pos_col.shape, jnp.int32),
                        pltpu.SemaphoreType.DMA((2,)), pltpu.SemaphoreType.DMA((2,))] + _GATHER_SEMS * 4,
        compiler_params=pltpu.CompilerParams(vmem_limit_bytes=VMEM_LIMIT),
    )(cw, w_ada, b_shard, half_in, half_out, pos_col, inv_freq)


def _reduce_scratch(rr, cc):
    c2 = cc // 2
    return [pltpu.VMEM((4, rr, c2), F32), pltpu.VMEM((4, rr, c2), F32), pltpu.VMEM((3, rr, c2), BF),
            pltpu.VMEM((2, rr, c2), BF), pltpu.VMEM((rr, c2), BF), pltpu.VMEM((rr, c2), F32),
            pltpu.SemaphoreType.DMA((8,)), pltpu.SemaphoreType.DMA((8,)), pltpu.SemaphoreType.DMA((5,))]


class _Reduce:
    def __init__(self, p_hbm, out_ref, acc_ref, own_ref, send_ref, land_ref, relay_ref, res_ref,
                 send_sems, recv_sems, local_sems):
        x, y, c = lax.axis_index("x"), lax.axis_index("y"), lax.axis_index("c")
        c2 = out_ref.shape[1] // 2
        sibling = (x, y, 1 - c)
        first = (lax.rem(x + 1 - c, 2), lax.rem(y + c, 2))
        second = (lax.rem(x + c, 2), lax.rem(y + 1 - c, 2))
        shards = [2 * first[0] + first[1], 2 * second[0] + second[1], 2 * (1 - x) + (1 - y), 2 * x + y]
        sibling_slot = (1, 0, 2, 3)
        mine = pl.ds(pl.multiple_of(c * c2, c2), c2)
        other = pl.ds(pl.multiple_of((1 - c) * c2, c2), c2)
        self.acc_ref, self.own_ref, self.send_ref, self.land_ref = acc_ref, own_ref, send_ref, land_ref
        self.relay_ref, self.res_ref = relay_ref, res_ref
        self.own = [pltpu.make_async_copy(p_hbm.at[j, :, mine], own_ref.at[k], local_sems.at[k])
                    for k, j in enumerate(shards)]
        self.swap_out = [pltpu.make_async_remote_copy(
            src_ref=p_hbm.at[j, :, other], dst_ref=acc_ref.at[sibling_slot[k]], send_sem=send_sems.at[k],
            recv_sem=recv_sems.at[sibling_slot[k]], device_id=sibling, device_id_type=MESH) for k, j in enumerate(shards)]
        self.swap_in = [pltpu.make_async_remote_copy(
            src_ref=p_hbm.at[j, :, other], dst_ref=acc_ref.at[k], send_sem=send_sems.at[k], recv_sem=recv_sems.at[k],
            device_id=sibling, device_id_type=MESH) for k, j in enumerate(shards)]

        def message(k, src, dst, to):
            return pltpu.make_async_remote_copy(src_ref=src, dst_ref=dst, send_sem=send_sems.at[k], recv_sem=recv_sems.at[k],
                                                device_id=(*to, c), device_id_type=MESH)

        self.direct = message(4, send_ref.at[0], land_ref.at[0], first)
        self.passed = message(5, send_ref.at[1], relay_ref, first)
        self.joint = message(6, send_ref.at[2], land_ref.at[1], second)
        self.put = pltpu.make_async_copy(res_ref, out_ref.at[:, mine], local_sems.at[4])
        self.share = pltpu.make_async_remote_copy(
            src_ref=res_ref, dst_ref=out_ref.at[:, mine], send_sem=send_sems.at[7],
            recv_sem=recv_sems.at[7], device_id=sibling, device_id_type=MESH)

    def start(self):
        for k in (0, 2, 1, 3):
            self.own[k].start()
            self.swap_out[k].start()

    def _combine(self, k):
        self.own[k].wait()
        self.swap_out[k].wait_send()
        self.swap_in[k].wait_recv()
        self.acc_ref[k] = self.acc_ref[k] + self.own_ref[k]

    def combine_and_send(self):
        dt = self.send_ref.dtype
        self._combine(0)
        self.send_ref[0] = self.acc_ref[0].astype(dt)
        self.direct.start()
        self._combine(2)
        self.send_ref[1] = self.acc_ref[2].astype(dt)
        self.passed.start()
        self._combine(1)
        self.passed.wait_recv()
        self.send_ref[2] = (self.acc_ref[1] + self.relay_ref[...].astype(F32)).astype(dt)
        self.joint.start()
        self._combine(3)

    def total_and_share(self):
        self.direct.wait_recv()
        self.joint.wait_recv()
        self.res_ref[...] = self.acc_ref[3] + self.land_ref[0].astype(F32) + self.land_ref[1].astype(F32)
        for cp in (self.direct, self.passed, self.joint):
            cp.wait_send()
        self.put.start()
        self.share.start()

    def finish(self):
        self.put.wait()
        self.share.wait()


def _epilogue(dw_in_parts, dw_out_parts, small):
    _, r_in, cc = dw_in_parts.shape
    _, r_out, _ = dw_out_parts.shape
    n_red = len(_reduce_scratch(r_in, cc))

    def body(pin_hbm, pout_hbm, small_ref, gin_ref, gout_ref, small_all_ref, *scratch):
        red_in = _Reduce(pin_hbm, gin_ref, *scratch[0:n_red])
        red_out = _Reduce(pout_hbm, gout_ref, *scratch[n_red:2 * n_red])
        gat = _Gather(small_ref, small_all_ref, *scratch[2 * n_red:])
        red_out.start()
        red_in.start()
        gat.start()
        red_out.combine_and_send()
        red_in.combine_and_send()
        gat.relay()
        red_out.total_and_share()
        red_in.total_and_share()
        gat.finish()
        red_out.finish()
        red_in.finish()

    vm = pl.BlockSpec(memory_space=pltpu.VMEM)
    anyspec = pl.BlockSpec(memory_space=pl.ANY)
    return pl.pallas_call(
        body, name="epilogue",
        out_shape=[jax.ShapeDtypeStruct((r_in, cc), F32), jax.ShapeDtypeStruct((r_out, cc), F32),
                   jax.ShapeDtypeStruct((8,) + small.shape, F32)],
        in_specs=[anyspec, anyspec, vm], out_specs=[anyspec, anyspec, vm],
        scratch_shapes=_reduce_scratch(r_in, cc) + _reduce_scratch(r_out, cc) + _GATHER_SEMS,
        compiler_params=pltpu.CompilerParams(vmem_limit_bytes=VMEM_LIMIT),
    )(dw_in_parts, dw_out_parts, small)


def _rope(t, cosb, sinb, first_half):
    partner = jnp.where(first_half, pltpu.roll(t, 96, 1), pltpu.roll(t, 32, 1))
    return t * cosb + partner * sinb


def _rope_t(g, cosb, sinb, first_half):
    gs = g * sinb
    partner = jnp.where(first_half, pltpu.roll(gs, 96, 1), pltpu.roll(gs, 32, 1))
    return g * cosb + partner


def _modnorm(x, g, sc1p, shift):
    r = lax.rsqrt(jnp.mean(x * x, axis=-1, keepdims=True) + RMS_EPS)
    xn = x * r
    return xn, r, (xn * g) * sc1p + shift


def _load_w_padded(w_hbm, w_vm, sems):
    copies = [pltpu.make_async_copy(w_hbm.at[ref:ref + n], w_vm.at[pad:pad + n], sems.at[k])
              for k, (pad, ref, n) in enumerate(_UNPAD_ROWS)]
    for cp in copies:
        cp.start()
    w_vm[OFF_GA + GLA_RANK:, :] = jnp.zeros((D_PAD - OFF_GA - GLA_RANK, D_MODEL), w_vm.dtype)
    return copies


def _inproj_fwd(x2d, shift, sc1p, g_norm, w_t):
    s = x2d.shape[0]
    tm = min(1024, s)

    def body(x_ref, sh_ref, sc_ref, g_ref, w_hbm, o_ref, w_vm, sems):
        @pl.when(pl.program_id(0) == 0)
        def _():
            for cp in _load_w_padded(w_hbm, w_vm, sems):
                cp.wait()

        subs = _subtiles(tm)
        hs = [_modnorm(x_ref[sl, :], g_ref[...], sc_ref[...], sh_ref[...])[2].astype(BF) for sl in subs]
        for sl, h in zip(subs, hs):
            o_ref[sl, :] = _dot(h, w_vm[...], NT)

    vec = _full((1, D_MODEL))
    return pl.pallas_call(
        body, name="inproj_fwd", grid=(s // tm,),
        in_specs=[pl.BlockSpec((tm, D_MODEL), lambda i: (i, 0)), vec, vec, vec, pl.BlockSpec(memory_space=pl.ANY)],
        out_specs=pl.BlockSpec((tm, D_PAD), lambda i: (i, 0)),
        out_shape=jax.ShapeDtypeStruct((s, D_PAD), F32),
        scratch_shapes=[pltpu.VMEM((D_PAD, D_MODEL), BF), pltpu.SemaphoreType.DMA((len(_UNPAD_ROWS),))],
        compiler_params=_params(("arbitrary",)),
    )(x2d, shift, sc1p, g_norm, w_t)


def _split3(a):
    hi = a.astype(BF)
    r1 = a - hi.astype(F32)
    mid = r1.astype(BF)
    lo = (r1 - mid.astype(F32)).astype(BF)
    return hi, mid, lo


def _tri_matmul(tri, a):
    hi, mid, lo = _split3(a)
    return _dot(tri, hi) + _dot(tri, mid) + _dot(tri, lo)


def _chunks(tb):
    return [slice(c * GLA_CHUNK, (c + 1) * GLA_CHUNK) for c in range(tb // GLA_CHUNK)]


def _per_chunk_rows(rows, width):
    return jnp.concatenate([jnp.broadcast_to(r, (GLA_CHUNK, width)) for r in rows], axis=0)


def _gla_triangle(tb):
    row = lax.broadcasted_iota(jnp.int32, (tb, tb), 0)
    col = lax.broadcasted_iota(jnp.int32, (tb, tb), 1)
    return (((row // GLA_CHUNK) == (col // GLA_CHUNK)) & (col <= row)).astype(F32)


def _lane_mean(x, ones_b):
    hi = x.astype(BF)
    lo = (x - hi.astype(F32)).astype(BF)
    return (_dot(hi, ones_b) + _dot(lo, ones_b)) * (1.0 / LANES)


def _head(t, h, lo_h):
    blk = t[:, LANES * (h // 2):LANES * (h // 2 + 1)]
    return jnp.where(lo_h, blk, 0.0) if h % 2 == 0 else jnp.where(lo_h, 0.0, blk)


def _gla_block_common(qk, ga, wd, bd, tril_b):
    tb = qk.shape[0]
    q, k = qk[:, :256], qk[:, 256:]
    z = _dot(ga.astype(BF), wd) + bd
    la = (jnp.minimum(z, 0.0) - jnp.log(1.0 + jnp.exp(-jnp.abs(z)))) * (1.0 / GLA_TAU)
    b = _tri_matmul(tril_b, la)
    bls = [b[rs.stop - 1:rs.stop, :] for rs in _chunks(tb)]
    eq = jnp.exp(b)
    ek = jnp.exp(-b)
    f = jnp.exp(_per_chunk_rows(bls, 256) - b)
    return z, eq, ek, f, q * (eq * GLA_DK ** -0.5), k * ek, k * f, bls


def _gla_units(s, rows):
    sub = min(GLA_SUB, s)
    tb = min(rows, s)
    subs = [slice(i * sub, (i + 1) * sub) for i in range(tb // sub)]
    units = [(i, h) for i in range(len(subs)) for h in range(GLA_HEADS)]
    return tb, sub, subs, units


def _gla_fwd(proj, wdecp, bdec, ggla):
    s = proj.shape[0]
    tb, sub, subs, units = _gla_units(s, GLA_ROWS_FWD)
    nch = sub // GLA_CHUNK

    def body(qk_ref, v_ref, gz_ref, ga_ref, wd_ref, bd_ref, gg_ref, tri_ref, og_ref, opre_ref, sprev_ref, st_ref):
        @pl.when(pl.program_id(0) == 0)
        def _():
            st_ref[...] = jnp.zeros_like(st_ref)

        lo_h = lax.broadcasted_iota(jnp.int32, (sub, LANES), 1) < GLA_DK
        tril = tri_ref[...] > 0.5
        tril_b = tri_ref[...].astype(BF)
        ones_b = jnp.ones((LANES, LANES), BF)
        gg, wd, bd = gg_ref[...], wd_ref[...], bd_ref[...]
        chunks = _chunks(sub)
        lanes = [slice(h * LANES, (h + 1) * LANES) for h in range(GLA_HEADS)]
        com = [_gla_block_common(qk_ref[sl, :], ga_ref[sl, :], wd, bd, tril_b) for sl in subs]
        decs = [[jnp.exp(bl) for bl in cm[7]] for cm in com]
        a = {(i, h): _head(com[i][4], h, lo_h).astype(BF) for i, h in units}
        bm = {(i, h): _head(com[i][5], h, lo_h).astype(BF) for i, h in units}
        ktl = {(i, h): _head(com[i][6], h, lo_h).astype(BF) for i, h in units}
        vh = {(i, h): v_ref[subs[i], lanes[h]].astype(BF) for i, h in units}
        sc = {u: _dot(a[u], bm[u], NT) for u in units}
        upd = {u: [_dot(vh[u][rs], ktl[u][rs], TN) for rs in chunks] for u in units}
        p = {u: jnp.where(tril, sc[u], 0.0).astype(BF) for u in units}
        o = {u: _dot(p[u], vh[u]) for u in units}
        states = {}
        for h in range(GLA_HEADS):
            st = st_ref[h]
            for i in range(len(subs)):
                entering = []
                for c in range(nch):
                    entering.append(st)
                    sprev_ref[i * nch + c, h] = st
                    st = st * decs[i][c][:, LANES * (h // 2):LANES * (h // 2 + 1)] + upd[(i, h)][c]
                states[(i, h)] = entering
            st_ref[h] = st
        inter = {u: [_dot(a[u][rs], states[u][c].astype(BF), NT) for c, rs in enumerate(chunks)] for u in units}
        o = {u: o[u] + jnp.concatenate(inter[u], axis=0) for u in units}
        ms = {u: _lane_mean(o[u] * o[u], ones_b) for u in units}
        for i, h in units:
            gzh = gz_ref[subs[i], lanes[h]]
            opre_ref[subs[i], lanes[h]] = o[(i, h)]
            og_ref[subs[i], lanes[h]] = (((o[(i, h)] * lax.rsqrt(ms[(i, h)] + RMS_EPS)) * gg[:, lanes[h]])
                                         * (gzh * _sigmoid(gzh))).astype(og_ref.dtype)

    def col(width, off):
        return pl.BlockSpec((tb, width), lambda i: (i, off // width))

    return pl.pallas_call(
        body, name="gla_fwd", grid=(s // tb,),
        in_specs=[col(512, OFF_QK), col(512, OFF_V), col(512, OFF_GZ), col(LANES, OFF_GA),
                  _full((LANES, 256)), _full((1, 256)), _full((1, 512)), _full((sub, sub))],
        out_specs=[pl.BlockSpec((tb, 512), lambda i: (i, 0)), pl.BlockSpec((tb, 512), lambda i: (i, 0)),
                   pl.BlockSpec((tb // GLA_CHUNK, GLA_HEADS, LANES, LANES), lambda i: (i, 0, 0, 0))],
        out_shape=[jax.ShapeDtypeStruct((s, 512), BF), jax.ShapeDtypeStruct((s, 512), F32),
                   jax.ShapeDtypeStruct((s // GLA_CHUNK, GLA_HEADS, LANES, LANES), F32)],
        scratch_shapes=[pltpu.VMEM((GLA_HEADS, LANES, LANES), F32)],
        compiler_params=_params(("arbitrary",)),
    )(proj, proj, proj, proj, wdecp, bdec, ggla, _gla_triangle(sub))


def _gla_bwd(proj, dog, opre, sprev, wdecp, bdec, ggla):
    s = proj.shape[0]
    tb, sub, subs, units = _gla_units(s, GLA_ROWS_BWD)
    nsub = len(subs)
    nch = sub // GLA_CHUNK
    nb = s // tb

    def body(qk_ref, v_ref, gz_ref, ga_ref, dog_ref, opre_ref, sprev_ref, wd_ref, bd_ref, gg_ref, tri_ref, triu_ref,
             dqk_ref, dv_ref, dgz_ref, dga_ref, dwd_ref, dbd_ref, dgg_ref, dst_ref):
        @pl.when(pl.program_id(0) == 0)
        def _():
            dst_ref[...] = jnp.zeros_like(dst_ref)
            dwd_ref[...] = jnp.zeros_like(dwd_ref)
            dbd_ref[...] = jnp.zeros_like(dbd_ref)
            dgg_ref[...] = jnp.zeros_like(dgg_ref)

        lo_h = lax.broadcasted_iota(jnp.int32, (sub, LANES), 1) < GLA_DK
        tril = tri_ref[...] > 0.5
        tril_b = tri_ref[...].astype(BF)
        triu_b = triu_ref[...].astype(BF)
        ones_b = jnp.ones((LANES, LANES), BF)
        last_row = (lax.broadcasted_iota(jnp.int32, (sub, LANES), 0) % GLA_CHUNK) == GLA_CHUNK - 1
        wd, gg, bd = wd_ref[...], gg_ref[...], bd_ref[...]
        chunks = _chunks(sub)
        lanes = [slice(h * LANES, (h + 1) * LANES) for h in range(GLA_HEADS)]
        blks = [slice(LANES * (h // 2), LANES * (h // 2 + 1)) for h in range(GLA_HEADS)]
        ga = [ga_ref[sl, :] for sl in subs]
        com = [_gla_block_common(qk_ref[sl, :], ga[i], wd, bd, tril_b) for i, sl in enumerate(subs)]
        decs = [[jnp.exp(bl) for bl in cm[7]] for cm in com]
        a = {(i, h): _head(com[i][4], h, lo_h).astype(BF) for i, h in units}
        bm = {(i, h): _head(com[i][5], h, lo_h).astype(BF) for i, h in units}
        ktl = {(i, h): _head(com[i][6], h, lo_h).astype(BF) for i, h in units}
        vh = {(i, h): v_ref[subs[i], lanes[h]].astype(BF) for i, h in units}
        sc = {u: _dot(a[u], bm[u], NT) for u in units}

        o = {(i, h): opre_ref[subs[i], lanes[h]] for i, h in units}
        ms = {u: _lane_mean(o[u] * o[u], ones_b) for u in units}
        gz = {(i, h): gz_ref[subs[i], lanes[h]] for i, h in units}
        dog = {(i, h): dog_ref[subs[i], lanes[h]] for i, h in units}
        sg = {u: _sigmoid(gz[u]) for u in units}
        r = {u: lax.rsqrt(ms[u] + RMS_EPS) for u in units}
        ohat = {u: o[u] * r[u] for u in units}
        sil = {u: gz[u] * sg[u] for u in units}
        for i, h in units:
            u = (i, h)
            dgz_ref[subs[i], lanes[h]] = (dog[u] * (ohat[u] * gg[:, lanes[h]])
                                          * (sg[u] * (1.0 + gz[u] * (1.0 - sg[u])))).astype(dgz_ref.dtype)
            dgg_ref[:, lanes[h]] += jnp.sum(dog[u] * sil[u] * ohat[u], axis=0, keepdims=True)
        dn = {(i, h): dog[(i, h)] * sil[(i, h)] * gg[:, lanes[h]] for i, h in units}
        mdn = {u: _lane_mean(dn[u] * ohat[u], ones_b) for u in units}
        do = {u: (r[u] * (dn[u] - ohat[u] * mdn[u])).astype(BF) for u in units}

        p = {u: jnp.where(tril, sc[u], 0.0).astype(BF) for u in units}
        dpr = {u: _dot(do[u], vh[u], NT) for u in units}
        incr = {u: [_dot(do[u][rs], a[u][rs], TN) for rs in chunks] for u in units}
        dv = {u: _dot(p[u], do[u], TN) for u in units}
        dp = {u: jnp.where(tril, dpr[u], 0.0).astype(BF) for u in units}
        dqd = {u: _dot(dp[u], bm[u]) for u in units}
        dkd = {u: _dot(dp[u], a[u], TN) for u in units}
        st = {(i, h): [sprev_ref[i * nch + c, h] for c in range(nch)] for i, h in units}
        leaving = {}
        for h in range(GLA_HEADS):
            d = dst_ref[h]
            for i in reversed(range(nsub)):
                out = [None] * nch
                for c in reversed(range(nch)):
                    out[c] = d
                    d = d * decs[i][c][:, blks[h]] + incr[(i, h)][c]
                leaving[(i, h)] = out
            dst_ref[h] = d
        lv_b = {u: [leaving[u][c].astype(BF) for c in range(nch)] for u in units}
        dv_s = {u: [_dot(ktl[u][rs], lv_b[u][c], NT) for c, rs in enumerate(chunks)] for u in units}
        dqd_s = {u: [_dot(do[u][rs], st[u][c].astype(BF)) for c, rs in enumerate(chunks)] for u in units}
        dkt_s = {u: [_dot(vh[u][rs], lv_b[u][c]) for c, rs in enumerate(chunks)] for u in units}
        ddec = {u: [jnp.sum(leaving[u][c] * st[u][c], axis=0, keepdims=True) for c in range(nch)] for u in units}
        for i, h in units:
            dv_ref[subs[i], lanes[h]] = (dv[(i, h)] + jnp.concatenate(dv_s[(i, h)], axis=0)).astype(dv_ref.dtype)
        dqd = {u: dqd[u] + jnp.concatenate(dqd_s[u], axis=0) for u in units}
        dkt = {u: jnp.concatenate(dkt_s[u], axis=0) for u in units}

        db = []
        for i, sl in enumerate(subs):
            _, eq, ek, f, qd, kd, kt, _ = com[i]
            parts = []
            for pair in range(GLA_HEADS // 2):
                blk, u0, u1 = blks[2 * pair], (i, 2 * pair), (i, 2 * pair + 1)
                dqd_b, dkd_b, dkt_b = dqd[u0] + dqd[u1], dkd[u0] + dkd[u1], dkt[u0] + dkt[u1]
                dqk_ref[sl, blk] = (dqd_b * (eq[:, blk] * GLA_DK ** -0.5)).astype(dqk_ref.dtype)
                dqk_ref[sl, 256 + LANES * pair:256 + LANES * (pair + 1)] = (dkd_b * ek[:, blk] + dkt_b * f[:, blk]).astype(dqk_ref.dtype)
                dkt_kt = dkt_b * kt[:, blk]
                dbp = dqd_b * qd[:, blk] - dkd_b * kd[:, blk] - dkt_kt
                dbl = [jnp.sum(dkt_kt[rs], axis=0, keepdims=True) + (ddec[u0][c] + ddec[u1][c]) * decs[i][c][:, blk]
                       for c, rs in enumerate(chunks)]
                parts.append(jnp.where(last_row, dbp + _per_chunk_rows(dbl, LANES), dbp))
            db.append(jnp.concatenate(parts, axis=1))
        dla = [_tri_matmul(triu_b, db[i]) for i in range(nsub)]
        dz32 = [dla[i] * (1.0 / GLA_TAU) * _sigmoid(-com[i][0]) for i in range(nsub)]
        dz = [t.astype(BF) for t in dz32]
        for i, sl in enumerate(subs):
            dga_ref[sl, :] = _dot(dz[i], wd, NT).astype(dga_ref.dtype)
            dwd_ref[...] += _dot(ga[i].astype(BF), dz[i], TN)
            dbd_ref[...] += jnp.sum(dz32[i], axis=0, keepdims=True)

    def col(width, off):
        return pl.BlockSpec((tb, width), lambda i: (nb - 1 - i, off // width))

    def rev(width):
        return pl.BlockSpec((tb, width), lambda i: (nb - 1 - i, 0))

    return pl.pallas_call(
        body, name="gla_bwd", grid=(nb,),
        in_specs=[col(512, OFF_QK), col(512, OFF_V), col(512, OFF_GZ), col(LANES, OFF_GA), rev(512), rev(512),
                  pl.BlockSpec((tb // GLA_CHUNK, GLA_HEADS, LANES, LANES), lambda i: (nb - 1 - i, 0, 0, 0)),
                  _full((LANES, 256)), _full((1, 256)), _full((1, 512)), _full((sub, sub)), _full((sub, sub))],
        out_specs=[rev(512), rev(512), rev(512), rev(LANES), _full((LANES, 256)), _full((1, 256)), _full((1, 512))],
        out_shape=[jax.ShapeDtypeStruct((s, 512), BF), jax.ShapeDtypeStruct((s, 512), BF),
                   jax.ShapeDtypeStruct((s, 512), BF), jax.ShapeDtypeStruct((s, LANES), BF),
                   jax.ShapeDtypeStruct((LANES, 256), F32), jax.ShapeDtypeStruct((1, 256), F32),
                   jax.ShapeDtypeStruct((1, 512), F32)],
        scratch_shapes=[pltpu.VMEM((GLA_HEADS, LANES, LANES), F32)],
        compiler_params=_params(("arbitrary",)),
    )(proj, proj, proj, proj, dog, opre, sprev, wdecp, bdec, ggla, _gla_triangle(sub), _gla_triangle(sub).T)


_SWA_COL_HEADS = (0, 2, 1, 3, 4, 6, 5, 7)
_SWA_COLS = SWA_HEADS * SWA_BLOCK


def _swa_masks():
    lo2 = lax.broadcasted_iota(jnp.int32, (2 * SWA_BLOCK, LANES), 1) < 64
    lane1 = lax.broadcasted_iota(jnp.int32, (SWA_BLOCK, LANES), 1)
    first_half = (lane1 % 64) < 32
    key = lax.broadcasted_iota(jnp.int32, (SWA_BLOCK, _SWA_COLS), 0)
    query = lax.broadcasted_iota(jnp.int32, (SWA_BLOCK, _SWA_COLS), 1) % SWA_BLOCK
    return lo2, lane1 < 64, first_half, key > query


def _merge_band(t, prev_mask, prev_bias=None):
    prev = t[:SWA_BLOCK] if prev_bias is None else t[:SWA_BLOCK] + prev_bias
    return jnp.where(prev_mask, prev, t[SWA_BLOCK:])


def _split_band(t, prev_mask_b):
    prev = t * prev_mask_b
    return jnp.concatenate([prev, t - prev], axis=0)


def _kv_variants(t, lo2):
    tr = pltpu.roll(t, 64, 1)
    lo_v = [jnp.where(lo2, t, 0.0).astype(BF), jnp.where(lo2, tr, 0.0).astype(BF)]
    hi_v = [jnp.where(lo2, 0.0, tr).astype(BF), jnp.where(lo2, 0.0, t).astype(BF)]
    return lo_v, hi_v


def _kv_variants_t(t):
    tt = t.T
    sw = jnp.concatenate([tt[64:], tt[:64]], axis=0)
    top = lax.broadcasted_iota(jnp.int32, tt.shape, 0) < 64
    lo_v = [jnp.where(top, tt, 0.0).astype(BF), jnp.where(top, sw, 0.0).astype(BF)]
    hi_v = [jnp.where(top, 0.0, sw).astype(BF), jnp.where(top, 0.0, tt).astype(BF)]
    return lo_v, hi_v


def _swa_scores(qg, k_lo, k_hi):
    return jnp.concatenate([_dot(k_lo[0], qg[0], NT), _dot(k_hi[0], qg[0], NT),
                            _dot(k_lo[1], qg[1], NT), _dot(k_hi[1], qg[1], NT)], axis=1)


def _sink_row(sinks_ref):
    return jnp.concatenate([jnp.full((1, SWA_BLOCK), sinks_ref[0, hd], F32) for hd in _SWA_COL_HEADS], axis=1)


def _swa_softmax(st, prev_mask, prev_bias, sink):
    st = _merge_band(st, prev_mask, prev_bias)
    m = jnp.maximum(jnp.max(st, axis=0, keepdims=True), sink)
    ex = jnp.exp(st - m)
    es = jnp.exp(sink - m)
    inv = 1.0 / (jnp.sum(ex, axis=0, keepdims=True) + es)
    return ex, es, inv


def _no_prev_bias(block_index):
    return jnp.where(block_index > 0, 0.0, -1e30).astype(F32)


def _swa_queries(sq_ref, rows, cosb, sinb, first_half):
    qs = [_rope(sq_ref[rows, p * LANES:(p + 1) * LANES], cosb, sinb, first_half) * 0.125 for p in range(4)]
    return [jnp.concatenate(qs[0:2], axis=0), jnp.concatenate(qs[2:4], axis=0)]


def _swa_fwd(proj, cos, sin, sinks):
    s = proj.shape[0]
    nq = min(SWA_QBLOCKS_FWD, s // SWA_BLOCK)
    tq = nq * SWA_BLOCK

    def body(sq_ref, sz_ref, sk_ref, sv_ref, cos_ref, sin_ref, sinks_ref, os_ref, opre_ref, kprev, vprev):
        n = pl.program_id(0)

        @pl.when(n == 0)
        def _():
            kprev[...] = jnp.zeros_like(kprev)
            vprev[...] = jnp.zeros_like(vprev)

        lo2, _, first_half, prev_mask = _swa_masks()
        prev_mask_b = jnp.where(prev_mask, 1.0, 0.0).astype(BF)
        sink = _sink_row(sinks_ref)
        blocks = range(nq)
        rows = [slice(j * SWA_BLOCK, (j + 1) * SWA_BLOCK) for j in blocks]
        cosb = [cos_ref[rows[j], :] for j in blocks]
        sinb = [sin_ref[rows[j], :] for j in blocks]
        kc = [_rope(sk_ref[rows[j], :], cosb[j], sinb[j], first_half) for j in blocks]
        vc = [sv_ref[rows[j], :] for j in blocks]
        kcat = [jnp.concatenate([kprev[...] if j == 0 else kc[j - 1], kc[j]], axis=0) for j in blocks]
        vcat = [jnp.concatenate([vprev[...] if j == 0 else vc[j - 1], vc[j]], axis=0) for j in blocks]
        kprev[...] = kc[-1]
        vprev[...] = vc[-1]
        kvar = [_kv_variants(kcat[j], lo2) for j in blocks]
        vtvar = [_kv_variants_t(vcat[j]) for j in blocks]
        qg = [[q.astype(BF) for q in _swa_queries(sq_ref, rows[j], cosb[j], sinb[j], first_half)] for j in blocks]
        st = [_swa_scores(qg[j], *kvar[j]) for j in blocks]
        soft = [_swa_softmax(st[j], prev_mask, _no_prev_bias(n) if j == 0 else None, sink) for j in blocks]
        pt = [_split_band(soft[j][0].astype(BF), prev_mask_b) for j in blocks]
        og = {}
        for j in blocks:
            inv = soft[j][2]
            for g in range(2):
                c0, c1, c2 = 512 * g, 512 * g + 256, 512 * g + 512
                ot = (_dot(vtvar[j][0][g], pt[j][:, c0:c1]) * inv[:, c0:c1]
                      + _dot(vtvar[j][1][g], pt[j][:, c1:c2]) * inv[:, c1:c2])
                og[(j, g)] = ot.T
        for j in blocks:
            for g in range(2):
                for i in range(2):
                    ls = slice((2 * g + i) * LANES, (2 * g + i + 1) * LANES)
                    o = og[(j, g)][i * SWA_BLOCK:(i + 1) * SWA_BLOCK]
                    sz = sz_ref[rows[j], ls]
                    opre_ref[rows[j], ls] = o
                    os_ref[rows[j], ls] = (o * (sz * _sigmoid(sz))).astype(os_ref.dtype)

    def col(width, off):
        return pl.BlockSpec((tq, width), lambda i: (i, off // width))

    row = pl.BlockSpec((tq, LANES), lambda i: (i, 0))
    return pl.pallas_call(
        body, name="swa_fwd", grid=(s // tq,),
        in_specs=[col(512, OFF_SQ), col(512, OFF_SZ), col(LANES, OFF_SK), col(LANES, OFF_SV), row, row,
                  pl.BlockSpec(memory_space=pltpu.SMEM)],
        out_specs=[pl.BlockSpec((tq, 512), lambda i: (i, 0))] * 2,
        out_shape=[jax.ShapeDtypeStruct((s, 512), BF), jax.ShapeDtypeStruct((s, 512), F32)],
        scratch_shapes=[pltpu.VMEM((SWA_BLOCK, LANES), F32)] * 2,
        compiler_params=_params(("arbitrary",)),
    )(proj, proj, proj, proj, cos, sin, sinks)


def _swa_bwd(proj, dos, opre, cos, sin, sinks):
    s = proj.shape[0]
    nq = min(SWA_QBLOCKS, s // SWA_BLOCK)
    tq = nq * SWA_BLOCK

    def body(sq_ref, sz_ref, sk_ref, sv_ref, dos_ref, opre_ref, cos_ref, sin_ref, sinks_ref,
             dsq_ref, dsz_ref, dsk_ref, dsv_ref, dsink_ref, kprev, vprev, cprev, sprev):
        n = pl.program_id(0)

        @pl.when(n == 0)
        def _():
            kprev[...] = jnp.zeros_like(kprev)
            vprev[...] = jnp.zeros_like(vprev)
            cprev[...] = jnp.zeros_like(cprev)
            sprev[...] = jnp.zeros_like(sprev)
            for hd in range(SWA_HEADS):
                dsink_ref[0, hd] = 0.0

        lo2, lo1, first_half, prev_mask = _swa_masks()
        prev_mask_b = jnp.where(prev_mask, 1.0, 0.0).astype(BF)
        lo1s = jnp.concatenate([lo1, lo1], axis=0)
        sink = _sink_row(sinks_ref)

        def home(m0, m1):
            t0 = m0 + pltpu.roll(m0, 64, 1)
            t1 = m1 + pltpu.roll(m1, 64, 1)
            return jnp.where(lo2, t0, t1)

        kp, vp, cp_, sp_ = kprev[...], vprev[...], cprev[...], sprev[...]
        for j in range(nq):
            rows = slice(j * SWA_BLOCK, (j + 1) * SWA_BLOCK)
            blk = n * nq + j
            cosb, sinb = cos_ref[rows, :], sin_ref[rows, :]
            kc = _rope(sk_ref[rows, :], cosb, sinb, first_half)
            vc = sv_ref[rows, :]
            kcat = jnp.concatenate([kp, kc], axis=0)
            k_lo, k_hi = _kv_variants(kcat, lo2)
            kt_lo, kt_hi = _kv_variants_t(kcat)
            v_lo, v_hi = _kv_variants(jnp.concatenate([vp, vc], axis=0), lo2)
            qg32 = _swa_queries(sq_ref, rows, cosb, sinb, first_half)
            qg = [q.astype(BF) for q in qg32]
            ex, es, inv = _swa_softmax(_swa_scores(qg, k_lo, k_hi), prev_mask, _no_prev_bias(n) if j == 0 else None, sink)
            pr, ps = ex * inv, es * inv

            dog32 = []
            for g in range(2):
                parts = []
                for i in range(2):
                    ls = slice((2 * g + i) * LANES, (2 * g + i + 1) * LANES)
                    sz = sz_ref[rows, ls]
                    sg = _sigmoid(sz)
                    dos_p = dos_ref[rows, ls]
                    dsz_ref[rows, ls] = (dos_p * opre_ref[rows, ls] * (sg * (1.0 + sz * (1.0 - sg)))).astype(dsz_ref.dtype)
                    parts.append(dos_p * (sz * sg))
                dog32.append(jnp.concatenate(parts, axis=0))
            dog = [t.astype(BF) for t in dog32]
            dpr = _merge_band(jnp.concatenate([_dot(v_lo[0], dog[0], NT), _dot(v_hi[0], dog[0], NT),
                                               _dot(v_lo[1], dog[1], NT), _dot(v_hi[1], dog[1], NT)], axis=1), prev_mask)
            rd = jnp.sum(pr * dpr, axis=0, keepdims=True)
            ds = _split_band((pr * (dpr - rd)).astype(BF), prev_mask_b)
            prb = _split_band(pr.astype(BF), prev_mask_b)
            sink_term = ps * rd
            for r, hd in enumerate(_SWA_COL_HEADS):
                dsink_ref[0, hd] += -jnp.sum(sink_term[:, r * SWA_BLOCK:(r + 1) * SWA_BLOCK])

            dk_g, dv_g = [], []
            for g in range(2):
                c0, c1, c2 = 512 * g, 512 * g + 256, 512 * g + 512
                dq = (_dot(kt_lo[g], ds[:, c0:c1]) + _dot(kt_hi[g], ds[:, c1:c2])).T
                for i in range(2):
                    ls = slice((2 * g + i) * LANES, (2 * g + i + 1) * LANES)
                    dsq_ref[rows, ls] = _rope_t(dq[i * SWA_BLOCK:(i + 1) * SWA_BLOCK] * 0.125, cosb, sinb,
                                                first_half).astype(dsq_ref.dtype)
                q_split = jnp.concatenate([jnp.where(lo1s, qg32[g], 0.0), jnp.where(lo1s, 0.0, qg32[g])], axis=0).astype(BF)
                do_split = jnp.concatenate([jnp.where(lo1s, dog32[g], 0.0), jnp.where(lo1s, 0.0, dog32[g])], axis=0).astype(BF)
                dk_g.append(_dot(ds[:, c0:c2], q_split))
                dv_g.append(_dot(prb[:, c0:c2], do_split))
            dk = home(dk_g[0], dk_g[1])
            dv = home(dv_g[0], dv_g[1])
            cur = pl.ds(pl.multiple_of(blk * SWA_BLOCK, SWA_BLOCK), SWA_BLOCK)
            dsk_ref[cur, :] = _rope_t(dk[SWA_BLOCK:], cosb, sinb, first_half)
            dsv_ref[cur, :] = dv[SWA_BLOCK:]
            dk_prev = _rope_t(dk[:SWA_BLOCK], cp_, sp_, first_half)
            dv_prev = dv[:SWA_BLOCK]
            if j == 0:
                @pl.when(n > 0)
                def _():
                    prv = pl.ds(pl.multiple_of((blk - 1) * SWA_BLOCK, SWA_BLOCK), SWA_BLOCK)
                    dsk_ref[prv, :] += dk_prev
                    dsv_ref[prv, :] += dv_prev
            else:
                prv = pl.ds(pl.multiple_of((blk - 1) * SWA_BLOCK, SWA_BLOCK), SWA_BLOCK)
                dsk_ref[prv, :] += dk_prev
                dsv_ref[prv, :] += dv_prev
            kp, vp, cp_, sp_ = kc, vc, cosb, sinb
        kprev[...] = kp
        vprev[...] = vp
        cprev[...] = cp_
        sprev[...] = sp_

    def col(width, off):
        return pl.BlockSpec((tq, width), lambda i: (i, off // width))

    row = pl.BlockSpec((tq, LANES), lambda i: (i, 0))
    wide = pl.BlockSpec((tq, 512), lambda i: (i, 0))
    return pl.pallas_call(
        body, name="swa_bwd", grid=(s // tq,),
        in_specs=[col(512, OFF_SQ), col(512, OFF_SZ), col(LANES, OFF_SK), col(LANES, OFF_SV), wide, wide, row, row,
                  pl.BlockSpec(memory_space=pltpu.SMEM)],
        out_specs=[wide, wide, _full((s, LANES)), _full((s, LANES)), pl.BlockSpec(memory_space=pltpu.SMEM)],
        out_shape=[jax.ShapeDtypeStruct((s, 512), BF), jax.ShapeDtypeStruct((s, 512), BF),
                   jax.ShapeDtypeStruct((s, LANES), F32), jax.ShapeDtypeStruct((s, LANES), F32),
                   jax.ShapeDtypeStruct((1, SWA_HEADS), F32)],
        scratch_shapes=[pltpu.VMEM((SWA_BLOCK, LANES), F32)] * 4,
        compiler_params=_params(("arbitrary",)),
    )(proj, proj, proj, proj, dos, opre, cos, sin, sinks)


def _outproj(og, osw, w_out, x2d, target, gate, g_final):
    s = x2d.shape[0]
    tm = min(512, s)

    def body(og_ref, os_ref, w_ref, x_ref, t_ref, gate_ref, gf_ref,
             dx2_ref, dog_ref, dos_ref, dw_ref, loss_ref, dgf_ref, dgate_ref):
        @pl.when(pl.program_id(0) == 0)
        def _():
            dw_ref[...] = jnp.zeros_like(dw_ref)
            loss_ref[...] = jnp.zeros_like(loss_ref)
            dgf_ref[...] = jnp.zeros_like(dgf_ref)
            dgate_ref[...] = jnp.zeros_like(dgate_ref)

        w = w_ref[...]
        gate, gf = gate_ref[...], gf_ref[...]
        subs = _subtiles(tm)
        ogv = [og_ref[sl, :] for sl in subs]
        osv = [os_ref[sl, :] for sl in subs]
        y = [_dot(ogv[k], w[:512]) + _dot(osv[k], w[512:]) for k in range(len(subs))]
        dys = []
        for k, sl in enumerate(subs):
            x2 = x_ref[sl, :] + gate * y[k]
            r = lax.rsqrt(jnp.mean(x2 * x2, axis=-1, keepdims=True) + RMS_EPS)
            xn = x2 * r
            err = xn * gf - t_ref[sl, :]
            loss_ref[...] += 0.5 * jnp.sum(jnp.mean(err * err, axis=-1, keepdims=True), axis=0, keepdims=True)
            dyf = err * (1.0 / D_MODEL)
            dgf_ref[...] += jnp.sum(dyf * xn, axis=0, keepdims=True)
            t = dyf * gf
            dx2 = r * (t - xn * jnp.mean(t * xn, axis=-1, keepdims=True))
            dx2_ref[sl, :] = dx2
            dgate_ref[...] += jnp.sum(dx2 * y[k], axis=0, keepdims=True)
            dys.append((dx2 * gate).astype(BF))
            dmix = _dot(dys[k], w, NT)
            dog_ref[sl, :] = dmix[:, :512]
            dos_ref[sl, :] = dmix[:, 512:]
        dy = jnp.concatenate(dys, axis=0)
        dw_ref[:512, :] += _dot(og_ref[...], dy, TN)
        dw_ref[512:, :] += _dot(os_ref[...], dy, TN)

    half = pl.BlockSpec((tm, 512), lambda i: (i, 0))
    rowb = pl.BlockSpec((tm, D_MODEL), lambda i: (i, 0))
    vec = _full((1, D_MODEL))
    return pl.pallas_call(
        body, name="outproj", grid=(s // tm,),
        in_specs=[half, half, _full((D_MODEL, D_MODEL)), rowb, rowb, vec, vec],
        out_specs=[rowb, half, half, _full((D_MODEL, D_MODEL)), _full((1, 1)), vec, vec],
        out_shape=[jax.ShapeDtypeStruct((s, D_MODEL), F32), jax.ShapeDtypeStruct((s, 512), F32),
                   jax.ShapeDtypeStruct((s, 512), F32), jax.ShapeDtypeStruct((D_MODEL, D_MODEL), F32),
                   jax.ShapeDtypeStruct((1, 1), F32), jax.ShapeDtypeStruct((1, D_MODEL), F32),
                   jax.ShapeDtypeStruct((1, D_MODEL), F32)],
        compiler_params=_params(("arbitrary",)),
    )(og, osw, w_out, x2d, target, gate, g_final)


_PIECES = ((OFF_QK, 512), (OFF_V, 512), (OFF_GZ, 512), (OFF_SQ, 512), (OFF_SZ, 512),
           (OFF_SK, LANES), (OFF_SV, LANES), (OFF_GA, LANES))

_UNPAD_ROWS = ((OFF_QK, 0, 1024),
               (OFF_GA, 1024, GLA_RANK),
               (OFF_GZ, 1040, 1024),
               (OFF_SK, 2064, 256),
               (OFF_SZ, 2320, 512))


def _inproj_bwd(x2d, shift, sc1p, g_norm, w_t, dx2, pieces):
    s = x2d.shape[0]
    tm = min(512, s)
    nsteps = s // tm

    def body(x_ref, sh_ref, sc_ref, g_ref, w_hbm, dx2_ref, *rest):
        piece_refs = rest[:len(_PIECES)]
        gx_ref, dw_hbm, dsh_ref, dsc_ref, dg_ref, w_vm, dw_vm, in_sems, out_sems = rest[len(_PIECES):]
        i = pl.program_id(0)

        @pl.when(i == 0)
        def _():
            loads = _load_w_padded(w_hbm, w_vm, in_sems)
            dw_vm[...] = jnp.zeros_like(dw_vm)
            dsh_ref[...] = jnp.zeros_like(dsh_ref)
            dsc_ref[...] = jnp.zeros_like(dsc_ref)
            dg_ref[...] = jnp.zeros_like(dg_ref)
            for cp in loads:
                cp.wait()

        g, sc1p_v, shift_v = g_ref[...], sc_ref[...], sh_ref[...]
        subs = _subtiles(tm)
        dhs = []
        for sl in subs:
            dh = None
            for (off, width), pr in zip(_PIECES, piece_refs):
                part = _dot(pr[sl, :].astype(BF), w_vm[off:off + width, :])
                dh = part if dh is None else dh + part
            dhs.append(dh)
        norm = [_modnorm(x_ref[sl, :], g, sc1p_v, shift_v) for sl in subs]
        hb = jnp.concatenate([h.astype(BF) for _, _, h in norm], axis=0)
        for (off, width), pr in zip(_PIECES, piece_refs):
            dw_vm[off:off + width, :] += _dot(pr[...].astype(BF), hb, TN)
        for sl, (xn, r, _), dh in zip(subs, norm, dhs):
            dsh_ref[...] += jnp.sum(dh, axis=0, keepdims=True)
            dsc_ref[...] += jnp.sum(dh * (xn * g), axis=0, keepdims=True)
            dg_ref[...] += jnp.sum(dh * xn * sc1p_v, axis=0, keepdims=True)
            dxn = dh * g * sc1p_v
            gx_ref[sl, :] = dx2_ref[sl, :] + r * (dxn - xn * jnp.mean(dxn * xn, axis=-1, keepdims=True))

        @pl.when(i == nsteps - 1)
        def _():
            copies = [pltpu.make_async_copy(dw_vm.at[src:src + n], dw_hbm.at[dst:dst + n], out_sems.at[k])
                      for k, (src, dst, n) in enumerate(_UNPAD_ROWS)]
            for cp in copies:
                cp.start()
            for cp in copies:
                cp.wait()

    rowb = pl.BlockSpec((tm, D_MODEL), lambda i: (i, 0))
    vec = _full((1, D_MODEL))
    anyspec = pl.BlockSpec(memory_space=pl.ANY)
    piece_specs = [pl.BlockSpec((tm, width), lambda i: (i, 0)) for _, width in _PIECES]
    return pl.pallas_call(
        body, name="inproj_bwd", grid=(nsteps,),
        in_specs=[rowb, vec, vec, vec, anyspec, rowb] + piece_specs,
        out_specs=[rowb, anyspec, vec, vec, vec],
        out_shape=[jax.ShapeDtypeStruct((s, D_MODEL), F32), jax.ShapeDtypeStruct((D_IN, D_MODEL), F32),
                   jax.ShapeDtypeStruct((1, D_MODEL), F32), jax.ShapeDtypeStruct((1, D_MODEL), F32),
                   jax.ShapeDtypeStruct((1, D_MODEL), F32)],
        scratch_shapes=[pltpu.VMEM((D_PAD, D_MODEL), BF), pltpu.VMEM((D_PAD, D_MODEL), F32),
                        pltpu.SemaphoreType.DMA((len(_UNPAD_ROWS),)), pltpu.SemaphoreType.DMA((len(_UNPAD_ROWS),))],
        compiler_params=_params(("arbitrary",)),
    )(x2d, shift, sc1p, g_norm, w_t, dx2, *pieces)


def _adam(w, g, m, v):
    m2 = ADAM_B1 * m + (1.0 - ADAM_B1) * g
    v2 = ADAM_B2 * v + (1.0 - ADAM_B2) * (g * g)
    m_hat = m2 / (1.0 - ADAM_B1 ** ADAM_STEP)
    v_hat = v2 / (1.0 - ADAM_B2 ** ADAM_STEP)
    delta = -ADAM_LR * (m_hat / (jnp.sqrt(v_hat) + ADAM_EPS) + ADAM_WD * w)
    return delta, m2, v2


def _adamw(w, g, m, v, name):
    rr, cc = w.shape
    tc = min(512, cc)

    def body(w_ref, g_ref, m_ref, v_ref, d_ref, m2_ref, v2_ref):
        d_ref[...], m2_ref[...], v2_ref[...] = _adam(w_ref[...], g_ref[...], m_ref[...], v_ref[...])

    blk = pl.BlockSpec((rr, tc), lambda i: (0, i))
    return pl.pallas_call(
        body, name=name, grid=(cc // tc,), in_specs=[blk] * 4, out_specs=[blk] * 3,
        out_shape=[jax.ShapeDtypeStruct((rr, cc), F32)] * 3,
        compiler_params=_params(("arbitrary",)),
    )(w, g, m, v)


def _adamw_t(w3, g, m3, v3, name):
    rr, _, cc = w3.shape
    tc = cc

    def body(w_hbm, g_ref, m_hbm, v_hbm, d_hbm, m2_hbm, v2_hbm, g3_hbm, w_vm, m_vm, v_vm, d_vm, m2_vm, v2_vm, in_sems, out_sems):
        cols = pl.ds(pl.multiple_of(pl.program_id(0) * tc, tc), tc)
        loads = [pltpu.make_async_copy(src.at[:, 0, cols], dst, in_sems.at[k])
                 for k, (src, dst) in enumerate(((w_hbm, w_vm), (m_hbm, m_vm), (v_hbm, v_vm)))]
        for cp in loads:
            cp.start()
        for cp in loads:
            cp.wait()
        d_vm[...], m2_vm[...], v2_vm[...] = _adam(w_vm[...], g_ref[...], m_vm[...], v_vm[...])
        stores = [pltpu.make_async_copy(src, dst.at[:, 0, cols], out_sems.at[k])
                  for k, (src, dst) in enumerate(((d_vm, d_hbm), (m2_vm, m2_hbm), (v2_vm, v2_hbm), (g_ref, g3_hbm)))]
        for cp in stores:
            cp.start()
        for cp in stores:
            cp.wait()

    hbm = pl.BlockSpec(memory_space=pl.ANY)
    return pl.pallas_call(
        body, name=name, grid=(cc // tc,), in_specs=[hbm, pl.BlockSpec((rr, tc), lambda i: (0, i)), hbm, hbm],
        out_specs=[hbm] * 4, out_shape=[jax.ShapeDtypeStruct((rr, 1, cc), F32)] * 4,
        scratch_shapes=[pltpu.VMEM((rr, tc), F32)] * 6 + [pltpu.SemaphoreType.DMA((3,)), pltpu.SemaphoreType.DMA((4,))],
        compiler_params=_params(("arbitrary",)),
    )(w3, g, m3, v3)


def _ada_update(c_all, dmod_cols, w, m, v):
    rr, cc = w.shape
    tr = min(512, rr)
    c_all = jnp.pad(c_all, ((0, 8), (0, 0)))
    dmod_cols = jnp.pad(dmod_cols, ((0, 8), (0, 0)))

    def body(c_ref, dm_ref, w_ref, m_ref, v_ref, g_ref, d_ref, m2_ref, v2_ref):
        cv = c_ref[...]
        sc = (cv * _sigmoid(cv)).astype(BF)
        g = _dot(sc, dm_ref[...].astype(BF), TN)
        g_ref[...] = g
        d_ref[...], m2_ref[...], v2_ref[...] = _adam(w_ref[...], g, m_ref[...], v_ref[...])

    blk = pl.BlockSpec((tr, cc), lambda i: (i, 0))
    return pl.pallas_call(
        body, name="ada_update", grid=(rr // tr,),
        in_specs=[pl.BlockSpec((16, tr), lambda i: (0, i)), _full((16, cc)), blk, blk, blk],
        out_specs=[blk] * 4, out_shape=[jax.ShapeDtypeStruct((rr, cc), F32)] * 4,
        compiler_params=_params(("arbitrary",)),
    )(c_all, dmod_cols, w, m, v)


def _small_update(parts, weights, moms, vels):
    n = len(weights)

    def body(*refs):
        p_refs, w_refs, m_refs, v_refs = refs[:n + 1], refs[n + 1:2 * n + 1], refs[2 * n + 1:3 * n + 1], refs[3 * n + 1:4 * n + 1]
        outs = refs[4 * n + 1:]
        for i in range(n):
            g = p_refs[i][0]
            for d in range(1, 8):
                g = g + p_refs[i][d]
            delta, m2, v2 = _adam(w_refs[i][...], g, m_refs[i][...], v_refs[i][...])
            outs[4 * i][...] = g
            outs[4 * i + 1][...] = delta
            outs[4 * i + 2][...] = m2
            outs[4 * i + 3][...] = v2
        tot = p_refs[n][0]
        for d in range(1, 8):
            tot = tot + p_refs[n][d]
        outs[4 * n][...] = tot

    out_shape = []
    for w in weights:
        out_shape += [jax.ShapeDtypeStruct(w.shape, F32)] * 4
    out_shape.append(jax.ShapeDtypeStruct(parts[n].shape[1:], F32))
    return pl.pallas_call(body, name="small_update", out_shape=out_shape, compiler_params=_params())(
        *parts, *weights, *moms, *vels)


def _rows8(a):
    flat = a.reshape(-1)
    rows = -(-flat.shape[0] // LANES)
    rows8 = -(-rows // 8) * 8
    flat = jnp.pad(flat, (0, rows8 * LANES - flat.shape[0]))
    return flat.reshape(rows8, LANES)


def kernel(x, c, positions, w_ada, b_ada, g_norm, w_in, w_decay, b_decay, g_gla_head, sinks, w_out, g_final, loss_target, m_w_ada, m_b_ada, m_g_norm, m_w_in, m_w_decay, m_b_decay, m_g_gla_head, m_sinks, m_w_out, m_g_final, v_w_ada, v_b_ada, v_g_norm, v_w_in, v_w_decay, v_b_decay, v_g_gla_head, v_sinks, v_w_out, v_g_final):
    ax, ay, ac = lax.axis_index("x"), lax.axis_index("y"), lax.axis_index("c")
    chip = 2 * ax + ay
    dev = 2 * chip + ac
    s = x.shape[1]
    x2d = x[0]
    target = loss_target[0]
    w_ada2, w_out2, w_dec2 = w_ada[0], w_out[0], w_decay[0]
    w_in_t = w_in[0].T
    ada_cols = w_ada2.shape[1]
    in_cols = w_in_t.shape[0]
    out_rows = w_out2.shape[0]
    half = D_MODEL // 2

    cw = jnp.concatenate([c.reshape(8, LANES), w_dec2.reshape(8, LANES)], axis=0)
    b_shard = lax.dynamic_slice(b_ada, (0, chip * ada_cols), (1, ada_cols))
    half_in = lax.dynamic_slice(w_in_t, (0, ac * half), (in_cols, half)).astype(BF)
    half_out = lax.dynamic_slice(w_out2, (ac * (out_rows // 2), 0), (out_rows // 2, D_MODEL)).astype(BF)
    inv_freq = 1.0 / (ROPE_THETA ** (jnp.arange(0, 64, 2, dtype=F32) / 64))
    first, mod_all, w_in_all, w_out_all, cos, sin = _prologue(
        cw, w_ada2, b_shard, half_in, half_out, positions.reshape(s, 1), jnp.tile(inv_freq, 4).reshape(1, LANES))

    first = first.reshape(8, 2, 8, LANES)
    c_all = first[:, 0].reshape(8, D_MODEL)
    w_dec_full = first[0::2, 1].reshape(4, GLA_RANK, 64).transpose(1, 0, 2).reshape(GLA_RANK, 256)
    mod = mod_all.reshape(4, 2, 8, ada_cols)[:, 0]
    mod = lax.dynamic_slice(mod, (0, dev, 0), (4, 1, ada_cols)).reshape(1, 4 * ada_cols)
    shift, sc1p, gate = mod[:, :D_MODEL], 1.0 + mod[:, D_MODEL:2 * D_MODEL], mod[:, 2 * D_MODEL:]
    w_t = w_in_all.reshape(4 * in_cols, D_MODEL)
    w_out_all = w_out_all.reshape(D_MODEL, D_MODEL)

    wdecp = jnp.pad(w_dec_full, ((0, LANES - GLA_RANK), (0, 0))).astype(BF)

    proj = _inproj_fwd(x2d, shift, sc1p, g_norm, w_t)
    og, o_gla, sprev = _gla_fwd(proj, wdecp, b_decay, g_gla_head)
    osw, o_swa = _swa_fwd(proj, cos, sin, sinks)
    dx2, dog, dos, dw_out, loss_p, dgf, dgate = _outproj(og, osw, w_out_all, x2d, target, gate, g_final.reshape(1, D_MODEL))
    dsq, dsz, dsk, dsv, dsinks = _swa_bwd(proj, dos, o_swa, cos, sin, sinks)
    dqk, dv, dgz, dga, dwdp, dbd, dgg = _gla_bwd(proj, dog, o_gla, sprev, wdecp, b_decay, g_gla_head)
    pieces = (dqk, dv, dgz, dsq, dsz, dsk, dsv, dga)
    gx, dw_in_t, dshift, dscale, dgn = _inproj_bwd(x2d, shift, sc1p, g_norm, w_t, dx2, pieces)

    segs = [jnp.concatenate([dshift, dscale, dgate], axis=1), dgn, dgf, dwdp[:GLA_RANK], dbd, dgg, dsinks, loss_p]
    packed = [_rows8(a) for a in segs]
    offs = [0]
    for a in packed:
        offs.append(offs[-1] + a.shape[0])
    g_w_in_t, g_w_out, small = _epilogue(dw_in_t.reshape(4, in_cols, D_MODEL), dw_out.reshape(4, out_rows, D_MODEL),
                                         jnp.concatenate(packed, axis=0))

    def seg(i, size):
        return small[:, offs[i]:offs[i + 1]].reshape(8, -1)[:, :size]

    dmod_all = seg(0, 3 * D_MODEL)
    dwd_all = lax.dynamic_slice(seg(3, GLA_RANK * 256).reshape(8, GLA_RANK, 256), (0, 0, chip * 64), (8, GLA_RANK, 64))
    parts = [dmod_all.reshape(8, 1, 3 * D_MODEL), seg(1, D_MODEL).reshape(8, 1, D_MODEL), dwd_all,
             seg(4, 256).reshape(8, 1, 256), seg(5, 512).reshape(8, 1, 512), seg(6, SWA_HEADS).reshape(8, 1, SWA_HEADS),
             seg(2, D_MODEL).reshape(8, 1, D_MODEL), seg(7, LANES).reshape(8, 1, LANES)]
    smalls = _small_update(
        parts,
        [b_ada, g_norm, w_dec2, b_decay, g_gla_head, sinks, g_final.reshape(1, D_MODEL)],
        [m_b_ada, m_g_norm, m_w_decay[0], m_b_decay, m_g_gla_head, m_sinks, m_g_final.reshape(1, D_MODEL)],
        [v_b_ada, v_g_norm, v_w_decay[0], v_b_decay, v_g_gla_head, v_sinks, v_g_final.reshape(1, D_MODEL)])
    (g_b_ada, d_b_ada, nm_b_ada, nv_b_ada, g_gn, d_gn, nm_gn, nv_gn, g_wd, d_wd, nm_wd, nv_wd,
     g_bd, d_bd, nm_bd, nv_bd, g_gg, d_gg, nm_gg, nv_gg, g_sk, d_sk, nm_sk, nv_sk,
     g_gf, d_gf, nm_gf, nv_gf, loss_row) = smalls
    loss = loss_row[0, 0]

    dmod_cols = lax.dynamic_slice(dmod_all, (0, chip * ada_cols), (8, ada_cols))
    g_w_ada, d_w_ada, nm_w_ada, nv_w_ada = _ada_update(c_all, dmod_cols, w_ada2, m_w_ada[0], v_w_ada[0])
    to3 = lambda a: jnp.transpose(a, (2, 0, 1))
    from3 = lambda a: jnp.transpose(a, (1, 2, 0))[0]
    d3, nm3, nv3, g3 = _adamw_t(to3(w_in), g_w_in_t, to3(m_w_in), to3(v_w_in), "adamw_w_in")
    g_w_in, d_w_in, nm_w_in, nv_w_in = from3(g3), from3(d3), from3(nm3), from3(nv3)
    d_w_out, nm_w_out, nv_w_out = _adamw(w_out2, g_w_out, m_w_out[0], v_w_out[0], "adamw_w_out")

    flat = lambda a: a.reshape(D_MODEL)
    grads = [g_w_ada[None], g_b_ada, g_gn, g_w_in[None], g_wd[None], g_bd, g_gg, g_sk, g_w_out[None], flat(g_gf)]
    deltas = [d_w_ada[None], d_b_ada, d_gn, d_w_in[None], d_wd[None], d_bd, d_gg, d_sk, d_w_out[None], flat(d_gf)]
    new_m = [nm_w_ada[None], nm_b_ada, nm_gn, nm_w_in[None], nm_wd[None], nm_bd, nm_gg, nm_sk, nm_w_out[None], flat(nm_gf)]
    new_v = [nv_w_ada[None], nv_b_ada, nv_gn, nv_w_in[None], nv_wd[None], nv_bd, nv_gg, nv_sk, nv_w_out[None], flat(nv_gf)]
    return (loss, gx[None], *grads, *deltas, *new_m, *new_v)
```

```python
import jax
import jax.numpy as jnp
from jax import lax
from jax.experimental import pallas as pl
from jax.experimental.pallas import tpu as pltpu

F32 = jnp.float32
BF = jnp.bfloat16

D_MODEL = 1024
GLA_HEADS = 4
GLA_DK = 64
GLA_CHUNK = 64
GLA_RANK = 16
GLA_TAU = 16.0
GLA_SUB = 256
GLA_ROWS_FWD = 1024
GLA_ROWS_BWD = 512
SWA_HEADS = 8
SWA_BLOCK = 128
SWA_QBLOCKS_FWD = 8
SWA_QBLOCKS = 8
RMS_EPS = 1e-6
ROPE_THETA = 10000.0

OFF_QK, OFF_V, OFF_GZ, OFF_SQ, OFF_SZ, OFF_SK, OFF_SV, OFF_GA = 0, 512, 1024, 1536, 2048, 2560, 2688, 2816
D_PAD = 2944
D_IN = 2832
LANES = 128
VMEM_LIMIT = 56 * 1024 * 1024

ADAM_LR, ADAM_B1, ADAM_B2, ADAM_EPS, ADAM_WD, ADAM_STEP = 0.001, 0.9, 0.999, 1e-08, 0.01, 10

NT = (((1,), (1,)), ((), ()))
TN = (((0,), (0,)), ((), ()))
MESH = pl.DeviceIdType.MESH


def _dot(a, b, dims=None):
    if dims is None:
        return jnp.dot(a, b, preferred_element_type=F32)
    return lax.dot_general(a, b, dims, preferred_element_type=F32)


def _sigmoid(x):
    return 1.0 / (1.0 + jnp.exp(-x))


def _params(sem=None):
    return pltpu.CompilerParams(dimension_semantics=sem, vmem_limit_bytes=VMEM_LIMIT)


def _full(shape):
    return pl.BlockSpec(shape, lambda i: (0,) * len(shape))


def _subtiles(rows, size=256):
    size = min(size, rows)
    return [slice(k * size, (k + 1) * size) for k in range(rows // size)]


WEIGHT_CHUNKS = 4


def _gather_sems(chunks=1):
    return [pltpu.SemaphoreType.DMA((7 * chunks,)), pltpu.SemaphoreType.DMA((7 * chunks,)), pltpu.SemaphoreType.DMA]


_GATHER_SEMS = _gather_sems()


class _Gather:
    def __init__(self, x_ref, out_ref, send_sems, recv_sems, local_sem, slab=None, chunks=1):
        self.slab_of = slab
        self.chunks = chunks
        self.width = x_ref.shape[-1] // chunks
        x, y, c = lax.axis_index("x"), lax.axis_index("y"), lax.axis_index("c")
        self.me, self.sibling, self.c = (x, y, c), (x, y, 1 - c), c
        self.xn, self.yn, self.dg = (1 - x, y), (x, 1 - y), (1 - x, 1 - y)
        self.pass_from = (lax.rem(x + 1 - c, 2), lax.rem(y + c, 2))
        self.pass_to = (lax.rem(x + c, 2), lax.rem(y + 1 - c, 2))
        self.x_ref, self.out_ref, self.send_sems, self.recv_sems = x_ref, out_ref, send_sems, recv_sems
        self.mine = pltpu.make_async_copy(x_ref, self._slab(*self.me), local_sem)

    def _slab(self, px, py, pc):
        if self.slab_of is not None:
            return self.slab_of(self.out_ref, px, py, pc)
        return self.out_ref.at[4 * px + 2 * py + pc]

    def _part(self, ref, q):
        if self.chunks == 1:
            return ref
        lanes = slice(q * self.width, (q + 1) * self.width)
        return ref.at[(slice(None),) * (len(ref.shape) - 1) + (lanes,)]

    def _copy(self, k, q, blk, to, src=None):
        i = k * self.chunks + q
        return pltpu.make_async_remote_copy(
            src_ref=self._part(self._slab(*blk) if src is None else src, q), dst_ref=self._part(self._slab(*blk), q),
            send_sem=self.send_sems.at[i], recv_sem=self.recv_sems.at[i], device_id=to, device_id_type=MESH)

    def _sends(self, q):
        c = self.c
        return [self._copy(0, q, self.me, self.sibling, src=self.x_ref),
                self._copy(1, q, self.me, (*self.xn, c), src=self.x_ref),
                self._copy(2, q, self.me, (*self.yn, c), src=self.x_ref),
                self._copy(3, q, (*self.pass_from, c), (*self.pass_to, c)),
                self._copy(4, q, (*self.xn, c), self.sibling),
                self._copy(5, q, (*self.yn, c), self.sibling),
                self._copy(6, q, (*self.dg, c), self.sibling)]

    def start(self):
        self.mine.start()
        for q in range(self.chunks):
            sends = self._sends(q)
            for k in (1, 2, 0):
                sends[k].start()

    def pass_on(self):
        for q in range(self.chunks):
            sends = self._sends(q)
            self._copy(1, q, (*self.xn, self.c), self.me).wait_recv()
            self._copy(2, q, (*self.yn, self.c), self.me).wait_recv()
            for k in (3, 4, 5):
                sends[k].start()

    def relay_diagonal(self):
        for q in range(self.chunks):
            self._copy(3, q, (*self.dg, self.c), self.me).wait_recv()
            self._sends(q)[6].start()

    def relay(self):
        self.pass_on()
        self.relay_diagonal()

    def finish(self):
        c = self.c
        for q in range(self.chunks):
            self._copy(0, q, self.sibling, self.me).wait_recv()
            for k, chip in ((4, self.xn), (5, self.yn), (6, self.dg)):
                self._copy(k, q, (*chip, 1 - c), self.me).wait_recv()
            for cp in self._sends(q):
                cp.wait_send()
        self.mine.wait()


def _prologue(cw, w_ada, b_shard, half_in, half_out, pos_col, inv_freq):
    s = pos_col.shape[0]
    rt = min(512, s)

    def body(cw_ref, wada_hbm, b_ref, hin_ref, hout_ref, pos_hbm, f_ref,
             first_ref, mod_ref, win_ref, wout_ref, cos_hbm, sin_hbm,
             mod_blk, cos_ref, sin_ref, wada_ref, pos_ref, table_sems, local_sems, *sems):
        fetch_w = pltpu.make_async_copy(wada_hbm, wada_ref, local_sems.at[0])
        fetch_p = pltpu.make_async_copy(pos_hbm, pos_ref, local_sems.at[1])
        fetch_w.start()
        fetch_p.start()
        g_c = _Gather(cw_ref, first_ref, *sems[0:3])
        half_lanes = hin_ref.shape[1]
        g_in = _Gather(hin_ref, win_ref, *sems[3:6], chunks=WEIGHT_CHUNKS,
                       slab=lambda ref, px, py, pc: ref.at[2 * px + py, :, pl.ds(pl.multiple_of(pc * half_lanes, half_lanes), half_lanes)])
        g_out = _Gather(hout_ref, wout_ref, *sems[6:9], chunks=WEIGHT_CHUNKS)
        g_mod = _Gather(mod_blk, mod_ref, *sems[9:12])
        g_c.start()
        g_in.start()
        g_out.start()
        g_c.relay()
        g_c.finish()
        c_rows = [jnp.concatenate([first_ref[d, r:r + 1, :] for r in range(8)], axis=1) for d in range(8)]
        c_all = jnp.concatenate(c_rows, axis=0)
        sc = (c_all * _sigmoid(c_all)).astype(BF)
        fetch_w.wait()
        mod_blk[...] = _dot(sc, wada_ref[...].astype(BF)) + b_ref[...]
        g_mod.start()
        fetch_p.wait()

        def rope_rows(i, carry):
            rows = pl.ds(pl.multiple_of(i * rt, rt), rt)
            ang = pos_ref[rows, :].astype(F32) * f_ref[...]
            lane = lax.broadcasted_iota(jnp.int32, ang.shape, 1)
            cos_ref[rows, :] = jnp.cos(ang)
            sn = jnp.sin(ang)
            sin_ref[rows, :] = jnp.where((lane % 64) < 32, -sn, sn)
            pltpu.make_async_copy(cos_ref.at[rows, :], cos_hbm.at[rows, :], table_sems.at[0]).start()
            pltpu.make_async_copy(sin_ref.at[rows, :], sin_hbm.at[rows, :], table_sems.at[1]).start()
            return carry

        steps = s // rt
        lax.fori_loop(0, steps // 2, rope_rows, 0)
        g_in.pass_on()
        g_out.pass_on()
        lax.fori_loop(steps // 2, steps, rope_rows, 0)
        g_in.relay_diagonal()
        g_out.relay_diagonal()
        g_mod.relay()
        g_in.finish()
        g_out.finish()
        g_mod.finish()
        pltpu.make_async_copy(cos_ref, cos_hbm, table_sems.at[0]).wait()
        pltpu.make_async_copy(sin_ref, sin_hbm, table_sems.at[1]).wait()

    vm = pl.BlockSpec(memory_space=pltpu.VMEM)
    hbm = pl.BlockSpec(memory_space=pl.ANY)
    return pl.pallas_call(
        body, name="prologue",
        out_shape=[jax.ShapeDtypeStruct((8,) + cw.shape, F32), jax.ShapeDtypeStruct((8, 8, w_ada.shape[1]), F32),
                   jax.ShapeDtypeStruct((4, half_in.shape[0], 2 * half_in.shape[1]), half_in.dtype),
                   jax.ShapeDtypeStruct((8,) + half_out.shape, half_out.dtype),
                   jax.ShapeDtypeStruct((s, LANES), F32), jax.ShapeDtypeStruct((s, LANES), F32)],
        in_specs=[vm, hbm, vm, hbm, hbm, hbm, vm], out_specs=[vm, vm, hbm, hbm, hbm, hbm],
        scratch_shapes=[pltpu.VMEM((8, w_ada.shape[1]), F32), pltpu.VMEM((s, LANES), F32), pltpu.VMEM((s, LANES), F32),
                        pltpu.VMEM(w_ada.shape, F32), pltpu.VMEM(pos_col.shape, jnp.int32),
                        pltpu.SemaphoreType.DMA((2,)), pltpu.SemaphoreType.DMA((2,))]
        + _GATHER_SEMS + _gather_sems(WEIGHT_CHUNKS) * 2 + _GATHER_SEMS,
        compiler_params=pltpu.CompilerParams(vmem_limit_bytes=VMEM_LIMIT),
    )(cw, w_ada, b_shard, half_in, half_out, pos_col, inv_freq)


def _reduce_scratch(rr, cc):
    c2 = cc // 2
    return [pltpu.VMEM((4, rr, c2), F32), pltpu.VMEM((4, rr, c2), F32), pltpu.VMEM((3, rr, c2), BF),
            pltpu.VMEM((2, rr, c2), BF), pltpu.VMEM((rr, c2), BF), pltpu.VMEM((rr, c2), F32),
            pltpu.SemaphoreType.DMA((8 + 3 * WEIGHT_CHUNKS,)), pltpu.SemaphoreType.DMA((8 + 3 * WEIGHT_CHUNKS,)),
            pltpu.SemaphoreType.DMA((5,))]


class _Reduce:
    def __init__(self, p_hbm, out_ref, acc_ref, own_ref, send_ref, land_ref, relay_ref, res_ref,
                 send_sems, recv_sems, local_sems):
        x, y, c = lax.axis_index("x"), lax.axis_index("y"), lax.axis_index("c")
        c2 = out_ref.shape[1] // 2
        sibling = (x, y, 1 - c)
        first = (lax.rem(x + 1 - c, 2), lax.rem(y + c, 2))
        second = (lax.rem(x + c, 2), lax.rem(y + 1 - c, 2))
        shards = [2 * first[0] + first[1], 2 * second[0] + second[1], 2 * (1 - x) + (1 - y), 2 * x + y]
        sibling_slot = (1, 0, 2, 3)
        mine = pl.ds(pl.multiple_of(c * c2, c2), c2)
        other = pl.ds(pl.multiple_of((1 - c) * c2, c2), c2)
        self.acc_ref, self.own_ref, self.send_ref, self.land_ref = acc_ref, own_ref, send_ref, land_ref
        self.relay_ref, self.res_ref = relay_ref, res_ref
        self.own = [pltpu.make_async_copy(p_hbm.at[j, :, mine], own_ref.at[k], local_sems.at[k])
                    for k, j in enumerate(shards)]
        self.swap_out = [pltpu.make_async_remote_copy(
            src_ref=p_hbm.at[j, :, other], dst_ref=acc_ref.at[sibling_slot[k]], send_sem=send_sems.at[k],
            recv_sem=recv_sems.at[sibling_slot[k]], device_id=sibling, device_id_type=MESH) for k, j in enumerate(shards)]
        self.swap_in = [pltpu.make_async_remote_copy(
            src_ref=p_hbm.at[j, :, other], dst_ref=acc_ref.at[k], send_sem=send_sems.at[k], recv_sem=recv_sems.at[k],
            device_id=sibling, device_id_type=MESH) for k, j in enumerate(shards)]

        self.lanes = [slice(q * (c2 // WEIGHT_CHUNKS), (q + 1) * (c2 // WEIGHT_CHUNKS)) for q in range(WEIGHT_CHUNKS)]

        def message(m, src, dst, to):
            return [pltpu.make_async_remote_copy(
                src_ref=src.at[:, ln], dst_ref=dst.at[:, ln], send_sem=send_sems.at[8 + m * WEIGHT_CHUNKS + q],
                recv_sem=recv_sems.at[8 + m * WEIGHT_CHUNKS + q], device_id=(*to, c), device_id_type=MESH)
                for q, ln in enumerate(self.lanes)]

        self.direct = message(0, send_ref.at[0], land_ref.at[0], first)
        self.passed = message(1, send_ref.at[1], relay_ref, first)
        self.joint = message(2, send_ref.at[2], land_ref.at[1], second)
        self.put = pltpu.make_async_copy(res_ref, out_ref.at[:, mine], local_sems.at[4])
        self.share = pltpu.make_async_remote_copy(
            src_ref=res_ref, dst_ref=out_ref.at[:, mine], send_sem=send_sems.at[7],
            recv_sem=recv_sems.at[7], device_id=sibling, device_id_type=MESH)

    def start(self):
        for k in (2, 0, 1, 3):
            self.own[k].start()
            self.swap_out[k].start()

    def _combine(self, k):
        self.own[k].wait()
        self.swap_out[k].wait_send()
        self.swap_in[k].wait_recv()
        self.acc_ref[k] = self.acc_ref[k] + self.own_ref[k]

    def combine_and_send(self):
        dt = self.send_ref.dtype
        self._combine(2)
        self.send_ref[1] = self.acc_ref[2].astype(dt)
        for cp in self.passed:
            cp.start()
        self._combine(0)
        self.send_ref[0] = self.acc_ref[0].astype(dt)
        for cp in self.direct:
            cp.start()
        self._combine(1)
        for q, ln in enumerate(self.lanes):
            self.passed[q].wait_recv()
            self.send_ref[2, :, ln] = (self.acc_ref[1, :, ln] + self.relay_ref[:, ln].astype(F32)).astype(dt)
            self.joint[q].start()
        self._combine(3)

    def total_and_share(self):
        for cp in self.direct + self.joint:
            cp.wait_recv()
        self.res_ref[...] = self.acc_ref[3] + self.land_ref[0].astype(F32) + self.land_ref[1].astype(F32)
        for cp in self.direct + self.passed + self.joint:
            cp.wait_send()
        self.put.start()
        self.share.start()

    def finish(self):
        self.put.wait()
        self.share.wait()


def _epilogue(dw_in_parts, dw_out_parts, small):
    _, r_in, cc = dw_in_parts.shape
    _, r_out, _ = dw_out_parts.shape
    n_red = len(_reduce_scratch(r_in, cc))

    def body(pin_hbm, pout_hbm, small_ref, gin_ref, gout_ref, small_all_ref, *scratch):
        red_in = _Reduce(pin_hbm, gin_ref, *scratch[0:n_red])
        red_out = _Reduce(pout_hbm, gout_ref, *scratch[n_red:2 * n_red])
        gat = _Gather(small_ref, small_all_ref, *scratch[2 * n_red:])
        red_out.start()
        red_in.start()
        gat.start()
        red_out.combine_and_send()
        red_in.combine_and_send()
        gat.relay()
        red_out.total_and_share()
        red_in.total_and_share()
        gat.finish()
        red_out.finish()
        red_in.finish()

    vm = pl.BlockSpec(memory_space=pltpu.VMEM)
    anyspec = pl.BlockSpec(memory_space=pl.ANY)
    return pl.pallas_call(
        body, name="epilogue",
        out_shape=[jax.ShapeDtypeStruct((r_in, cc), F32), jax.ShapeDtypeStruct((r_out, cc), F32),
                   jax.ShapeDtypeStruct((8,) + small.shape, F32)],
        in_specs=[anyspec, anyspec, vm], out_specs=[anyspec, anyspec, vm],
        scratch_shapes=_reduce_scratch(r_in, cc) + _reduce_scratch(r_out, cc) + _GATHER_SEMS,
        compiler_params=pltpu.CompilerParams(vmem_limit_bytes=VMEM_LIMIT),
    )(dw_in_parts, dw_out_parts, small)


def _rope(t, cosb, sinb, first_half):
    partner = jnp.where(first_half, pltpu.roll(t, 96, 1), pltpu.roll(t, 32, 1))
    return t * cosb + partner * sinb


def _rope_t(g, cosb, sinb, first_half):
    gs = g * sinb
    partner = jnp.where(first_half, pltpu.roll(gs, 96, 1), pltpu.roll(gs, 32, 1))
    return g * cosb + partner


def _modnorm(x, g, sc1p, shift):
    r = lax.rsqrt(jnp.mean(x * x, axis=-1, keepdims=True) + RMS_EPS)
    xn = x * r
    return xn, r, (xn * g) * sc1p + shift


def _load_w_padded(w_hbm, w_vm, sems):
    copies = [pltpu.make_async_copy(w_hbm.at[ref:ref + n], w_vm.at[pad:pad + n], sems.at[k])
              for k, (pad, ref, n) in enumerate(_UNPAD_ROWS)]
    for cp in copies:
        cp.start()
    w_vm[OFF_GA + GLA_RANK:, :] = jnp.zeros((D_PAD - OFF_GA - GLA_RANK, D_MODEL), w_vm.dtype)
    return copies


def _inproj_fwd(x2d, shift, sc1p, g_norm, w_t):
    s = x2d.shape[0]
    tm = min(1024, s)

    def body(x_ref, sh_ref, sc_ref, g_ref, w_hbm, o_ref, w_vm, sems):
        @pl.when(pl.program_id(0) == 0)
        def _():
            for cp in _load_w_padded(w_hbm, w_vm, sems):
                cp.wait()

        subs = _subtiles(tm)
        hs = [_modnorm(x_ref[sl, :], g_ref[...], sc_ref[...], sh_ref[...])[2].astype(BF) for sl in subs]
        for sl, h in zip(subs, hs):
            o_ref[sl, :] = _dot(h, w_vm[...], NT)

    vec = _full((1, D_MODEL))
    return pl.pallas_call(
        body, name="inproj_fwd", grid=(s // tm,),
        in_specs=[pl.BlockSpec((tm, D_MODEL), lambda i: (i, 0)), vec, vec, vec, pl.BlockSpec(memory_space=pl.ANY)],
        out_specs=pl.BlockSpec((tm, D_PAD), lambda i: (i, 0)),
        out_shape=jax.ShapeDtypeStruct((s, D_PAD), F32),
        scratch_shapes=[pltpu.VMEM((D_PAD, D_MODEL), BF), pltpu.SemaphoreType.DMA((len(_UNPAD_ROWS),))],
        compiler_params=_params(("arbitrary",)),
    )(x2d, shift, sc1p, g_norm, w_t)


def _split3(a):
    hi = a.astype(BF)
    r1 = a - hi.astype(F32)
    mid = r1.astype(BF)
    lo = (r1 - mid.astype(F32)).astype(BF)
    return hi, mid, lo


def _tri_matmul(tri, a):
    hi, mid, lo = _split3(a)
    return _dot(tri, hi) + _dot(tri, mid) + _dot(tri, lo)


def _chunks(tb):
    return [slice(c * GLA_CHUNK, (c + 1) * GLA_CHUNK) for c in range(tb // GLA_CHUNK)]


def _per_chunk_rows(rows, width):
    return jnp.concatenate([jnp.broadcast_to(r, (GLA_CHUNK, width)) for r in rows], axis=0)


def _gla_triangle(tb):
    row = lax.broadcasted_iota(jnp.int32, (tb, tb), 0)
    col = lax.broadcasted_iota(jnp.int32, (tb, tb), 1)
    return (((row // GLA_CHUNK) == (col // GLA_CHUNK)) & (col <= row)).astype(F32)


def _lane_mean(x, ones_b):
    hi = x.astype(BF)
    lo = (x - hi.astype(F32)).astype(BF)
    return (_dot(hi, ones_b) + _dot(lo, ones_b)) * (1.0 / LANES)


def _head(t, h, lo_h):
    blk = t[:, LANES * (h // 2):LANES * (h // 2 + 1)]
    return jnp.where(lo_h, blk, 0.0) if h % 2 == 0 else jnp.where(lo_h, 0.0, blk)


def _gla_block_common(qk, ga, wd, bd, tril_b):
    tb = qk.shape[0]
    q, k = qk[:, :256], qk[:, 256:]
    z = _dot(ga.astype(BF), wd) + bd
    la = (jnp.minimum(z, 0.0) - jnp.log(1.0 + jnp.exp(-jnp.abs(z)))) * (1.0 / GLA_TAU)
    b = _tri_matmul(tril_b, la)
    bls = [b[rs.stop - 1:rs.stop, :] for rs in _chunks(tb)]
    eq = jnp.exp(b)
    ek = jnp.exp(-b)
    f = jnp.exp(_per_chunk_rows(bls, 256) - b)
    return z, eq, ek, f, q * (eq * GLA_DK ** -0.5), k * ek, k * f, bls


def _gla_units(s, rows):
    sub = min(GLA_SUB, s)
    tb = min(rows, s)
    subs = [slice(i * sub, (i + 1) * sub) for i in range(tb // sub)]
    units = [(i, h) for i in range(len(subs)) for h in range(GLA_HEADS)]
    return tb, sub, subs, units


def _gla_fwd(proj, wdecp, bdec, ggla):
    s = proj.shape[0]
    tb, sub, subs, units = _gla_units(s, GLA_ROWS_FWD)
    nch = sub // GLA_CHUNK

    def body(qk_ref, v_ref, gz_ref, ga_ref, wd_ref, bd_ref, gg_ref, tri_ref, og_ref, opre_ref, sprev_ref, st_ref):
        @pl.when(pl.program_id(0) == 0)
        def _():
            st_ref[...] = jnp.zeros_like(st_ref)

        lo_h = lax.broadcasted_iota(jnp.int32, (sub, LANES), 1) < GLA_DK
        tril = tri_ref[...] > 0.5
        tril_b = tri_ref[...].astype(BF)
        ones_b = jnp.ones((LANES, LANES), BF)
        gg, wd, bd = gg_ref[...], wd_ref[...], bd_ref[...]
        chunks = _chunks(sub)
        lanes = [slice(h * LANES, (h + 1) * LANES) for h in range(GLA_HEADS)]
        com = [_gla_block_common(qk_ref[sl, :], ga_ref[sl, :], wd, bd, tril_b) for sl in subs]
        decs = [[jnp.exp(bl) for bl in cm[7]] for cm in com]
        a = {(i, h): _head(com[i][4], h, lo_h).astype(BF) for i, h in units}
        bm = {(i, h): _head(com[i][5], h, lo_h).astype(BF) for i, h in units}
        ktl = {(i, h): _head(com[i][6], h, lo_h).astype(BF) for i, h in units}
        vh = {(i, h): v_ref[subs[i], lanes[h]].astype(BF) for i, h in units}
        sc = {u: _dot(a[u], bm[u], NT) for u in units}
        upd = {u: [_dot(vh[u][rs], ktl[u][rs], TN) for rs in chunks] for u in units}
        p = {u: jnp.where(tril, sc[u], 0.0).astype(BF) for u in units}
        o = {u: _dot(p[u], vh[u]) for u in units}
        states = {}
        for h in range(GLA_HEADS):
            st = st_ref[h]
            for i in range(len(subs)):
                entering = []
                for c in range(nch):
                    entering.append(st)
                    sprev_ref[i * nch + c, h] = st
                    st = st * decs[i][c][:, LANES * (h // 2):LANES * (h // 2 + 1)] + upd[(i, h)][c]
                states[(i, h)] = entering
            st_ref[h] = st
        inter = {u: [_dot(a[u][rs], states[u][c].astype(BF), NT) for c, rs in enumerate(chunks)] for u in units}
        o = {u: o[u] + jnp.concatenate(inter[u], axis=0) for u in units}
        ms = {u: _lane_mean(o[u] * o[u], ones_b) for u in units}
        for i, h in units:
            gzh = gz_ref[subs[i], lanes[h]]
            opre_ref[subs[i], lanes[h]] = o[(i, h)]
            og_ref[subs[i], lanes[h]] = (((o[(i, h)] * lax.rsqrt(ms[(i, h)] + RMS_EPS)) * gg[:, lanes[h]])
                                         * (gzh * _sigmoid(gzh))).astype(og_ref.dtype)

    def col(width, off):
        return pl.BlockSpec((tb, width), lambda i: (i, off // width))

    return pl.pallas_call(
        body, name="gla_fwd", grid=(s // tb,),
        in_specs=[col(512, OFF_QK), col(512, OFF_V), col(512, OFF_GZ), col(LANES, OFF_GA),
                  _full((LANES, 256)), _full((1, 256)), _full((1, 512)), _full((sub, sub))],
        out_specs=[pl.BlockSpec((tb, 512), lambda i: (i, 0)), pl.BlockSpec((tb, 512), lambda i: (i, 0)),
                   pl.BlockSpec((tb // GLA_CHUNK, GLA_HEADS, LANES, LANES), lambda i: (i, 0, 0, 0))],
        out_shape=[jax.ShapeDtypeStruct((s, 512), BF), jax.ShapeDtypeStruct((s, 512), F32),
                   jax.ShapeDtypeStruct((s // GLA_CHUNK, GLA_HEADS, LANES, LANES), F32)],
        scratch_shapes=[pltpu.VMEM((GLA_HEADS, LANES, LANES), F32)],
        compiler_params=_params(("arbitrary",)),
    )(proj, proj, proj, proj, wdecp, bdec, ggla, _gla_triangle(sub))


def _gla_bwd(proj, dog, opre, sprev, wdecp, bdec, ggla):
    s = proj.shape[0]
    tb, sub, subs, units = _gla_units(s, GLA_ROWS_BWD)
    nsub = len(subs)
    nch = sub // GLA_CHUNK
    nb = s // tb

    def body(qk_ref, v_ref, gz_ref, ga_ref, dog_ref, opre_ref, sprev_ref, wd_ref, bd_ref, gg_ref, tri_ref, triu_ref,
             dqk_ref, dv_ref, dgz_ref, dga_ref, dwd_ref, dbd_ref, dgg_ref, dst_ref):
        @pl.when(pl.program_id(0) == 0)
        def _():
            dst_ref[...] = jnp.zeros_like(dst_ref)
            dwd_ref[...] = jnp.zeros_like(dwd_ref)
            dbd_ref[...] = jnp.zeros_like(dbd_ref)
            dgg_ref[...] = jnp.zeros_like(dgg_ref)

        lo_h = lax.broadcasted_iota(jnp.int32, (sub, LANES), 1) < GLA_DK
        tril = tri_ref[...] > 0.5
        tril_b = tri_ref[...].astype(BF)
        triu_b = triu_ref[...].astype(BF)
        ones_b = jnp.ones((LANES, LANES), BF)
        last_row = (lax.broadcasted_iota(jnp.int32, (sub, LANES), 0) % GLA_CHUNK) == GLA_CHUNK - 1
        wd, gg, bd = wd_ref[...], gg_ref[...], bd_ref[...]
        chunks = _chunks(sub)
        lanes = [slice(h * LANES, (h + 1) * LANES) for h in range(GLA_HEADS)]
        blks = [slice(LANES * (h // 2), LANES * (h // 2 + 1)) for h in range(GLA_HEADS)]
        ga = [ga_ref[sl, :] for sl in subs]
        com = [_gla_block_common(qk_ref[sl, :], ga[i], wd, bd, tril_b) for i, sl in enumerate(subs)]
        decs = [[jnp.exp(bl) for bl in cm[7]] for cm in com]
        a = {(i, h): _head(com[i][4], h, lo_h).astype(BF) for i, h in units}
        bm = {(i, h): _head(com[i][5], h, lo_h).astype(BF) for i, h in units}
        ktl = {(i, h): _head(com[i][6], h, lo_h).astype(BF) for i, h in units}
        vh = {(i, h): v_ref[subs[i], lanes[h]].astype(BF) for i, h in units}
        sc = {u: _dot(a[u], bm[u], NT) for u in units}

        o = {(i, h): opre_ref[subs[i], lanes[h]] for i, h in units}
        ms = {u: _lane_mean(o[u] * o[u], ones_b) for u in units}
        gz = {(i, h): gz_ref[subs[i], lanes[h]] for i, h in units}
        dog = {(i, h): dog_ref[subs[i], lanes[h]] for i, h in units}
        sg = {u: _sigmoid(gz[u]) for u in units}
        r = {u: lax.rsqrt(ms[u] + RMS_EPS) for u in units}
        ohat = {u: o[u] * r[u] for u in units}
        sil = {u: gz[u] * sg[u] for u in units}
        for i, h in units:
            u = (i, h)
            dgz_ref[subs[i], lanes[h]] = (dog[u] * (ohat[u] * gg[:, lanes[h]])
                                          * (sg[u] * (1.0 + gz[u] * (1.0 - sg[u])))).astype(dgz_ref.dtype)
            dgg_ref[:, lanes[h]] += jnp.sum(dog[u] * sil[u] * ohat[u], axis=0, keepdims=True)
        dn = {(i, h): dog[(i, h)] * sil[(i, h)] * gg[:, lanes[h]] for i, h in units}
        mdn = {u: _lane_mean(dn[u] * ohat[u], ones_b) for u in units}
        do = {u: (r[u] * (dn[u] - ohat[u] * mdn[u])).astype(BF) for u in units}

        p = {u: jnp.where(tril, sc[u], 0.0).astype(BF) for u in units}
        dpr = {u: _dot(do[u], vh[u], NT) for u in units}
        incr = {u: [_dot(do[u][rs], a[u][rs], TN) for rs in chunks] for u in units}
        dv = {u: _dot(p[u], do[u], TN) for u in units}
        dp = {u: jnp.where(tril, dpr[u], 0.0).astype(BF) for u in units}
        dqd = {u: _dot(dp[u], bm[u]) for u in units}
        dkd = {u: _dot(dp[u], a[u], TN) for u in units}
        st = {(i, h): [sprev_ref[i * nch + c, h] for c in range(nch)] for i, h in units}
        leaving = {}
        for h in range(GLA_HEADS):
            d = dst_ref[h]
            for i in reversed(range(nsub)):
                out = [None] * nch
                for c in reversed(range(nch)):
                    out[c] = d
                    d = d * decs[i][c][:, blks[h]] + incr[(i, h)][c]
                leaving[(i, h)] = out
            dst_ref[h] = d
        lv_b = {u: [leaving[u][c].astype(BF) for c in range(nch)] for u in units}
        dv_s = {u: [_dot(ktl[u][rs], lv_b[u][c], NT) for c, rs in enumerate(chunks)] for u in units}
        dqd_s = {u: [_dot(do[u][rs], st[u][c].astype(BF)) for c, rs in enumerate(chunks)] for u in units}
        dkt_s = {u: [_dot(vh[u][rs], lv_b[u][c]) for c, rs in enumerate(chunks)] for u in units}
        ddec = {u: [jnp.sum(leaving[u][c] * st[u][c], axis=0, keepdims=True) for c in range(nch)] for u in units}
        for i, h in units:
            dv_ref[subs[i], lanes[h]] = (dv[(i, h)] + jnp.concatenate(dv_s[(i, h)], axis=0)).astype(dv_ref.dtype)
        dqd = {u: dqd[u] + jnp.concatenate(dqd_s[u], axis=0) for u in units}
        dkt = {u: jnp.concatenate(dkt_s[u], axis=0) for u in units}

        db = []
        for i, sl in enumerate(subs):
            _, eq, ek, f, qd, kd, kt, _ = com[i]
            parts = []
            for pair in range(GLA_HEADS // 2):
                blk, u0, u1 = blks[2 * pair], (i, 2 * pair), (i, 2 * pair + 1)
                dqd_b, dkd_b, dkt_b = dqd[u0] + dqd[u1], dkd[u0] + dkd[u1], dkt[u0] + dkt[u1]
                dqk_ref[sl, blk] = (dqd_b * (eq[:, blk] * GLA_DK ** -0.5)).astype(dqk_ref.dtype)
                dqk_ref[sl, 256 + LANES * pair:256 + LANES * (pair + 1)] = (dkd_b * ek[:, blk] + dkt_b * f[:, blk]).astype(dqk_ref.dtype)
                dkt_kt = dkt_b * kt[:, blk]
                dbp = dqd_b * qd[:, blk] - dkd_b * kd[:, blk] - dkt_kt
                dbl = [jnp.sum(dkt_kt[rs], axis=0, keepdims=True) + (ddec[u0][c] + ddec[u1][c]) * decs[i][c][:, blk]
                       for c, rs in enumerate(chunks)]
                parts.append(jnp.where(last_row, dbp + _per_chunk_rows(dbl, LANES), dbp))
            db.append(jnp.concatenate(parts, axis=1))
        dla = [_tri_matmul(triu_b, db[i]) for i in range(nsub)]
        dz32 = [dla[i] * (1.0 / GLA_TAU) * _sigmoid(-com[i][0]) for i in range(nsub)]
        dz = [t.astype(BF) for t in dz32]
        for i, sl in enumerate(subs):
            dga_ref[sl, :] = _dot(dz[i], wd, NT).astype(dga_ref.dtype)
            dwd_ref[...] += _dot(ga[i].astype(BF), dz[i], TN)
            dbd_ref[...] += jnp.sum(dz32[i], axis=0, keepdims=True)

    def col(width, off):
        return pl.BlockSpec((tb, width), lambda i: (nb - 1 - i, off // width))

    def rev(width):
        return pl.BlockSpec((tb, width), lambda i: (nb - 1 - i, 0))

    return pl.pallas_call(
        body, name="gla_bwd", grid=(nb,),
        in_specs=[col(512, OFF_QK), col(512, OFF_V), col(512, OFF_GZ), col(LANES, OFF_GA), rev(512), rev(512),
                  pl.BlockSpec((tb // GLA_CHUNK, GLA_HEADS, LANES, LANES), lambda i: (nb - 1 - i, 0, 0, 0)),
                  _full((LANES, 256)), _full((1, 256)), _full((1, 512)), _full((sub, sub)), _full((sub, sub))],
        out_specs=[rev(512), rev(512), rev(512), rev(LANES), _full((LANES, 256)), _full((1, 256)), _full((1, 512))],
        out_shape=[jax.ShapeDtypeStruct((s, 512), BF), jax.ShapeDtypeStruct((s, 512), BF),
                   jax.ShapeDtypeStruct((s, 512), BF), jax.ShapeDtypeStruct((s, LANES), BF),
                   jax.ShapeDtypeStruct((LANES, 256), F32), jax.ShapeDtypeStruct((1, 256), F32),
                   jax.ShapeDtypeStruct((1, 512), F32)],
        scratch_shapes=[pltpu.VMEM((GLA_HEADS, LANES, LANES), F32)],
        compiler_params=_params(("arbitrary",)),
    )(proj, proj, proj, proj, dog, opre, sprev, wdecp, bdec, ggla, _gla_triangle(sub), _gla_triangle(sub).T)


_SWA_COL_HEADS = (0, 2, 1, 3, 4, 6, 5, 7)
_SWA_COLS = SWA_HEADS * SWA_BLOCK


def _swa_masks():
    lo2 = lax.broadcasted_iota(jnp.int32, (2 * SWA_BLOCK, LANES), 1) < 64
    lane1 = lax.broadcasted_iota(jnp.int32, (SWA_BLOCK, LANES), 1)
    first_half = (lane1 % 64) < 32
    key = lax.broadcasted_iota(jnp.int32, (SWA_BLOCK, _SWA_COLS), 0)
    query = lax.broadcasted_iota(jnp.int32, (SWA_BLOCK, _SWA_COLS), 1) % SWA_BLOCK
    return lo2, lane1 < 64, first_half, key > query


def _merge_band(t, prev_mask, prev_bias=None):
    prev = t[:SWA_BLOCK] if prev_bias is None else t[:SWA_BLOCK] + prev_bias
    return jnp.where(prev_mask, prev, t[SWA_BLOCK:])


def _split_band(t, prev_mask_b):
    prev = t * prev_mask_b
    return jnp.concatenate([prev, t - prev], axis=0)


def _kv_variants(t, lo2):
    tr = pltpu.roll(t, 64, 1)
    lo_v = [jnp.where(lo2, t, 0.0).astype(BF), jnp.where(lo2, tr, 0.0).astype(BF)]
    hi_v = [jnp.where(lo2, 0.0, tr).astype(BF), jnp.where(lo2, 0.0, t).astype(BF)]
    return lo_v, hi_v


def _kv_variants_t(t):
    tt = t.T
    sw = jnp.concatenate([tt[64:], tt[:64]], axis=0)
    top = lax.broadcasted_iota(jnp.int32, tt.shape, 0) < 64
    lo_v = [jnp.where(top, tt, 0.0).astype(BF), jnp.where(top, sw, 0.0).astype(BF)]
    hi_v = [jnp.where(top, 0.0, sw).astype(BF), jnp.where(top, 0.0, tt).astype(BF)]
    return lo_v, hi_v


def _swa_scores(qg, k_lo, k_hi):
    return jnp.concatenate([_dot(k_lo[0], qg[0], NT), _dot(k_hi[0], qg[0], NT),
                            _dot(k_lo[1], qg[1], NT), _dot(k_hi[1], qg[1], NT)], axis=1)


def _sink_row(sinks_ref):
    return jnp.concatenate([jnp.full((1, SWA_BLOCK), sinks_ref[0, hd], F32) for hd in _SWA_COL_HEADS], axis=1)


def _swa_softmax(st, prev_mask, prev_bias, sink):
    st = _merge_band(st, prev_mask, prev_bias)
    m = jnp.maximum(jnp.max(st, axis=0, keepdims=True), sink)
    ex = jnp.exp(st - m)
    es = jnp.exp(sink - m)
    inv = 1.0 / (jnp.sum(ex, axis=0, keepdims=True) + es)
    return ex, es, inv


def _no_prev_bias(block_index):
    return jnp.where(block_index > 0, 0.0, -1e30).astype(F32)


def _swa_queries(sq_ref, rows, cosb, sinb, first_half):
    qs = [_rope(sq_ref[rows, p * LANES:(p + 1) * LANES], cosb, sinb, first_half) * 0.125 for p in range(4)]
    return [jnp.concatenate(qs[0:2], axis=0), jnp.concatenate(qs[2:4], axis=0)]


def _swa_fwd(proj, cos, sin, sinks):
    s = proj.shape[0]
    nq = min(SWA_QBLOCKS_FWD, s // SWA_BLOCK)
    tq = nq * SWA_BLOCK

    def body(sq_ref, sz_ref, sk_ref, sv_ref, cos_ref, sin_ref, sinks_ref, os_ref, opre_ref, kprev, vprev):
        n = pl.program_id(0)

        @pl.when(n == 0)
        def _():
            kprev[...] = jnp.zeros_like(kprev)
            vprev[...] = jnp.zeros_like(vprev)

        lo2, _, first_half, prev_mask = _swa_masks()
        prev_mask_b = jnp.where(prev_mask, 1.0, 0.0).astype(BF)
        sink = _sink_row(sinks_ref)
        blocks = range(nq)
        rows = [slice(j * SWA_BLOCK, (j + 1) * SWA_BLOCK) for j in blocks]
        cosb = [cos_ref[rows[j], :] for j in blocks]
        sinb = [sin_ref[rows[j], :] for j in blocks]
        kc = [_rope(sk_ref[rows[j], :], cosb[j], sinb[j], first_half) for j in blocks]
        vc = [sv_ref[rows[j], :] for j in blocks]
        kcat = [jnp.concatenate([kprev[...] if j == 0 else kc[j - 1], kc[j]], axis=0) for j in blocks]
        vcat = [jnp.concatenate([vprev[...] if j == 0 else vc[j - 1], vc[j]], axis=0) for j in blocks]
        kprev[...] = kc[-1]
        vprev[...] = vc[-1]
        kvar = [_kv_variants(kcat[j], lo2) for j in blocks]
        vtvar = [_kv_variants_t(vcat[j]) for j in blocks]
        qg = [[q.astype(BF) for q in _swa_queries(sq_ref, rows[j], cosb[j], sinb[j], first_half)] for j in blocks]
        st = [_swa_scores(qg[j], *kvar[j]) for j in blocks]
        soft = [_swa_softmax(st[j], prev_mask, _no_prev_bias(n) if j == 0 else None, sink) for j in blocks]
        pt = [_split_band(soft[j][0].astype(BF), prev_mask_b) for j in blocks]
        og = {}
        for j in blocks:
            inv = soft[j][2]
            for g in range(2):
                c0, c1, c2 = 512 * g, 512 * g + 256, 512 * g + 512
                ot = (_dot(vtvar[j][0][g], pt[j][:, c0:c1]) * inv[:, c0:c1]
                      + _dot(vtvar[j][1][g], pt[j][:, c1:c2]) * inv[:, c1:c2])
                og[(j, g)] = ot.T
        for j in blocks:
            for g in range(2):
                for i in range(2):
                    ls = slice((2 * g + i) * LANES, (2 * g + i + 1) * LANES)
                    o = og[(j, g)][i * SWA_BLOCK:(i + 1) * SWA_BLOCK]
                    sz = sz_ref[rows[j], ls]
                    opre_ref[rows[j], ls] = o
                    os_ref[rows[j], ls] = (o * (sz * _sigmoid(sz))).astype(os_ref.dtype)

    def col(width, off):
        return pl.BlockSpec((tq, width), lambda i: (i, off // width))

    row = pl.BlockSpec((tq, LANES), lambda i: (i, 0))
    return pl.pallas_call(
        body, name="swa_fwd", grid=(s // tq,),
        in_specs=[col(512, OFF_SQ), col(512, OFF_SZ), col(LANES, OFF_SK), col(LANES, OFF_SV), row, row,
                  pl.BlockSpec(memory_space=pltpu.SMEM)],
        out_specs=[pl.BlockSpec((tq, 512), lambda i: (i, 0))] * 2,
        out_shape=[jax.ShapeDtypeStruct((s, 512), BF), jax.ShapeDtypeStruct((s, 512), F32)],
        scratch_shapes=[pltpu.VMEM((SWA_BLOCK, LANES), F32)] * 2,
        compiler_params=_params(("arbitrary",)),
    )(proj, proj, proj, proj, cos, sin, sinks)


def _swa_bwd(proj, dos, opre, cos, sin, sinks):
    s = proj.shape[0]
    nq = min(SWA_QBLOCKS, s // SWA_BLOCK)
    tq = nq * SWA_BLOCK

    def body(sq_ref, sz_ref, sk_ref, sv_ref, dos_ref, opre_ref, cos_ref, sin_ref, sinks_ref,
             dsq_ref, dsz_ref, dsk_ref, dsv_ref, dsink_ref, kprev, vprev, cprev, sprev):
        n = pl.program_id(0)

        @pl.when(n == 0)
        def _():
            kprev[...] = jnp.zeros_like(kprev)
            vprev[...] = jnp.zeros_like(vprev)
            cprev[...] = jnp.zeros_like(cprev)
            sprev[...] = jnp.zeros_like(sprev)
            for hd in range(SWA_HEADS):
                dsink_ref[0, hd] = 0.0

        lo2, lo1, first_half, prev_mask = _swa_masks()
        prev_mask_b = jnp.where(prev_mask, 1.0, 0.0).astype(BF)
        lo1s = jnp.concatenate([lo1, lo1], axis=0)
        sink = _sink_row(sinks_ref)

        def home(m0, m1):
            t0 = m0 + pltpu.roll(m0, 64, 1)
            t1 = m1 + pltpu.roll(m1, 64, 1)
            return jnp.where(lo2, t0, t1)

        kp, vp, cp_, sp_ = kprev[...], vprev[...], cprev[...], sprev[...]
        for j in range(nq):
            rows = slice(j * SWA_BLOCK, (j + 1) * SWA_BLOCK)
            blk = n * nq + j
            cosb, sinb = cos_ref[rows, :], sin_ref[rows, :]
            kc = _rope(sk_ref[rows, :], cosb, sinb, first_half)
            vc = sv_ref[rows, :]
            kcat = jnp.concatenate([kp, kc], axis=0)
            k_lo, k_hi = _kv_variants(kcat, lo2)
            kt_lo, kt_hi = _kv_variants_t(kcat)
            v_lo, v_hi = _kv_variants(jnp.concatenate([vp, vc], axis=0), lo2)
            qg32 = _swa_queries(sq_ref, rows, cosb, sinb, first_half)
            qg = [q.astype(BF) for q in qg32]
            ex, es, inv = _swa_softmax(_swa_scores(qg, k_lo, k_hi), prev_mask, _no_prev_bias(n) if j == 0 else None, sink)
            pr, ps = ex * inv, es * inv

            dog32 = []
            for g in range(2):
                parts = []
                for i in range(2):
                    ls = slice((2 * g + i) * LANES, (2 * g + i + 1) * LANES)
                    sz = sz_ref[rows, ls]
                    sg = _sigmoid(sz)
                    dos_p = dos_ref[rows, ls]
                    dsz_ref[rows, ls] = (dos_p * opre_ref[rows, ls] * (sg * (1.0 + sz * (1.0 - sg)))).astype(dsz_ref.dtype)
                    parts.append(dos_p * (sz * sg))
                dog32.append(jnp.concatenate(parts, axis=0))
            dog = [t.astype(BF) for t in dog32]
            dpr = _merge_band(jnp.concatenate([_dot(v_lo[0], dog[0], NT), _dot(v_hi[0], dog[0], NT),
                                               _dot(v_lo[1], dog[1], NT), _dot(v_hi[1], dog[1], NT)], axis=1), prev_mask)
            rd = jnp.sum(pr * dpr, axis=0, keepdims=True)
            ds = _split_band((pr * (dpr - rd)).astype(BF), prev_mask_b)
            prb = _split_band(pr.astype(BF), prev_mask_b)
            sink_term = ps * rd
            for r, hd in enumerate(_SWA_COL_HEADS):
                dsink_ref[0, hd] += -jnp.sum(sink_term[:, r * SWA_BLOCK:(r + 1) * SWA_BLOCK])

            dk_g, dv_g = [], []
            for g in range(2):
                c0, c1, c2 = 512 * g, 512 * g + 256, 512 * g + 512
                dq = (_dot(kt_lo[g], ds[:, c0:c1]) + _dot(kt_hi[g], ds[:, c1:c2])).T
                for i in range(2):
                    ls = slice((2 * g + i) * LANES, (2 * g + i + 1) * LANES)
                    dsq_ref[rows, ls] = _rope_t(dq[i * SWA_BLOCK:(i + 1) * SWA_BLOCK] * 0.125, cosb, sinb,
                                                first_half).astype(dsq_ref.dtype)
                q_split = jnp.concatenate([jnp.where(lo1s, qg32[g], 0.0), jnp.where(lo1s, 0.0, qg32[g])], axis=0).astype(BF)
                do_split = jnp.concatenate([jnp.where(lo1s, dog32[g], 0.0), jnp.where(lo1s, 0.0, dog32[g])], axis=0).astype(BF)
                dk_g.append(_dot(ds[:, c0:c2], q_split))
                dv_g.append(_dot(prb[:, c0:c2], do_split))
            dk = home(dk_g[0], dk_g[1])
            dv = home(dv_g[0], dv_g[1])
            cur = pl.ds(pl.multiple_of(blk * SWA_BLOCK, SWA_BLOCK), SWA_BLOCK)
            dsk_ref[cur, :] = _rope_t(dk[SWA_BLOCK:], cosb, sinb, first_half)
            dsv_ref[cur, :] = dv[SWA_BLOCK:]
            dk_prev = _rope_t(dk[:SWA_BLOCK], cp_, sp_, first_half)
            dv_prev = dv[:SWA_BLOCK]
            if j == 0:
                @pl.when(n > 0)
                def _():
                    prv = pl.ds(pl.multiple_of((blk - 1) * SWA_BLOCK, SWA_BLOCK), SWA_BLOCK)
                    dsk_ref[prv, :] += dk_prev
                    dsv_ref[prv, :] += dv_prev
            else:
                prv = pl.ds(pl.multiple_of((blk - 1) * SWA_BLOCK, SWA_BLOCK), SWA_BLOCK)
                dsk_ref[prv, :] += dk_prev
                dsv_ref[prv, :] += dv_prev
            kp, vp, cp_, sp_ = kc, vc, cosb, sinb
        kprev[...] = kp
        vprev[...] = vp
        cprev[...] = cp_
        sprev[...] = sp_

    def col(width, off):
        return pl.BlockSpec((tq, width), lambda i: (i, off // width))

    row = pl.BlockSpec((tq, LANES), lambda i: (i, 0))
    wide = pl.BlockSpec((tq, 512), lambda i: (i, 0))
    return pl.pallas_call(
        body, name="swa_bwd", grid=(s // tq,),
        in_specs=[col(512, OFF_SQ), col(512, OFF_SZ), col(LANES, OFF_SK), col(LANES, OFF_SV), wide, wide, row, row,
                  pl.BlockSpec(memory_space=pltpu.SMEM)],
        out_specs=[wide, wide, _full((s, LANES)), _full((s, LANES)), pl.BlockSpec(memory_space=pltpu.SMEM)],
        out_shape=[jax.ShapeDtypeStruct((s, 512), BF), jax.ShapeDtypeStruct((s, 512), BF),
                   jax.ShapeDtypeStruct((s, LANES), F32), jax.ShapeDtypeStruct((s, LANES), F32),
                   jax.ShapeDtypeStruct((1, SWA_HEADS), F32)],
        scratch_shapes=[pltpu.VMEM((SWA_BLOCK, LANES), F32)] * 4,
        compiler_params=_params(("arbitrary",)),
    )(proj, proj, proj, proj, dos, opre, cos, sin, sinks)


def _outproj(og, osw, w_out, x2d, target, gate, g_final):
    s = x2d.shape[0]
    tm = min(512, s)

    def body(og_ref, os_ref, w_ref, x_ref, t_ref, gate_ref, gf_ref,
             dx2_ref, dog_ref, dos_ref, dw_ref, loss_ref, dgf_ref, dgate_ref):
        @pl.when(pl.program_id(0) == 0)
        def _():
            dw_ref[...] = jnp.zeros_like(dw_ref)
            loss_ref[...] = jnp.zeros_like(loss_ref)
            dgf_ref[...] = jnp.zeros_like(dgf_ref)
            dgate_ref[...] = jnp.zeros_like(dgate_ref)

        w = w_ref[...]
        gate, gf = gate_ref[...], gf_ref[...]
        subs = _subtiles(tm)
        ogv = [og_ref[sl, :] for sl in subs]
        osv = [os_ref[sl, :] for sl in subs]
        y = [_dot(ogv[k], w[:512]) + _dot(osv[k], w[512:]) for k in range(len(subs))]
        dys = []
        for k, sl in enumerate(subs):
            x2 = x_ref[sl, :] + gate * y[k]
            r = lax.rsqrt(jnp.mean(x2 * x2, axis=-1, keepdims=True) + RMS_EPS)
            xn = x2 * r
            err = xn * gf - t_ref[sl, :]
            loss_ref[...] += 0.5 * jnp.sum(jnp.mean(err * err, axis=-1, keepdims=True), axis=0, keepdims=True)
            dyf = err * (1.0 / D_MODEL)
            dgf_ref[...] += jnp.sum(dyf * xn, axis=0, keepdims=True)
            t = dyf * gf
            dx2 = r * (t - xn * jnp.mean(t * xn, axis=-1, keepdims=True))
            dx2_ref[sl, :] = dx2
            dgate_ref[...] += jnp.sum(dx2 * y[k], axis=0, keepdims=True)
            dys.append((dx2 * gate).astype(BF))
            dmix = _dot(dys[k], w, NT)
            dog_ref[sl, :] = dmix[:, :512]
            dos_ref[sl, :] = dmix[:, 512:]
        dy = jnp.concatenate(dys, axis=0)
        dw_ref[:512, :] += _dot(og_ref[...], dy, TN)
        dw_ref[512:, :] += _dot(os_ref[...], dy, TN)

    half = pl.BlockSpec((tm, 512), lambda i: (i, 0))
    rowb = pl.BlockSpec((tm, D_MODEL), lambda i: (i, 0))
    vec = _full((1, D_MODEL))
    return pl.pallas_call(
        body, name="outproj", grid=(s // tm,),
        in_specs=[half, half, _full((D_MODEL, D_MODEL)), rowb, rowb, vec, vec],
        out_specs=[rowb, half, half, _full((D_MODEL, D_MODEL)), _full((1, 1)), vec, vec],
        out_shape=[jax.ShapeDtypeStruct((s, D_MODEL), F32), jax.ShapeDtypeStruct((s, 512), F32),
                   jax.ShapeDtypeStruct((s, 512), F32), jax.ShapeDtypeStruct((D_MODEL, D_MODEL), F32),
                   jax.ShapeDtypeStruct((1, 1), F32), jax.ShapeDtypeStruct((1, D_MODEL), F32),
                   jax.ShapeDtypeStruct((1, D_MODEL), F32)],
        compiler_params=_params(("arbitrary",)),
    )(og, osw, w_out, x2d, target, gate, g_final)


_PIECES = ((OFF_QK, 512), (OFF_V, 512), (OFF_GZ, 512), (OFF_SQ, 512), (OFF_SZ, 512),
           (OFF_SK, LANES), (OFF_SV, LANES), (OFF_GA, LANES))

_UNPAD_ROWS = ((OFF_QK, 0, 1024),
               (OFF_GA, 1024, GLA_RANK),
               (OFF_GZ, 1040, 1024),
               (OFF_SK, 2064, 256),
               (OFF_SZ, 2320, 512))


def _inproj_bwd(x2d, shift, sc1p, g_norm, w_t, dx2, pieces):
    s = x2d.shape[0]
    tm = min(512, s)
    nsteps = s // tm

    def body(x_ref, sh_ref, sc_ref, g_ref, w_hbm, dx2_ref, *rest):
        piece_refs = rest[:len(_PIECES)]
        gx_ref, dw_hbm, dsh_ref, dsc_ref, dg_ref, w_vm, dw_vm, in_sems, out_sems = rest[len(_PIECES):]
        i = pl.program_id(0)

        @pl.when(i == 0)
        def _():
            loads = _load_w_padded(w_hbm, w_vm, in_sems)
            dw_vm[...] = jnp.zeros_like(dw_vm)
            dsh_ref[...] = jnp.zeros_like(dsh_ref)
            dsc_ref[...] = jnp.zeros_like(dsc_ref)
            dg_ref[...] = jnp.zeros_like(dg_ref)
            for cp in loads:
                cp.wait()

        g, sc1p_v, shift_v = g_ref[...], sc_ref[...], sh_ref[...]
        subs = _subtiles(tm)
        dhs = []
        for sl in subs:
            dh = None
            for (off, width), pr in zip(_PIECES, piece_refs):
                part = _dot(pr[sl, :].astype(BF), w_vm[off:off + width, :])
                dh = part if dh is None else dh + part
            dhs.append(dh)
        norm = [_modnorm(x_ref[sl, :], g, sc1p_v, shift_v) for sl in subs]
        hb = jnp.concatenate([h.astype(BF) for _, _, h in norm], axis=0)
        for (off, width), pr in zip(_PIECES, piece_refs):
            dw_vm[off:off + width, :] += _dot(pr[...].astype(BF), hb, TN)
        for sl, (xn, r, _), dh in zip(subs, norm, dhs):
            dsh_ref[...] += jnp.sum(dh, axis=0, keepdims=True)
            dsc_ref[...] += jnp.sum(dh * (xn * g), axis=0, keepdims=True)
            dg_ref[...] += jnp.sum(dh * xn * sc1p_v, axis=0, keepdims=True)
            dxn = dh * g * sc1p_v
            gx_ref[sl, :] = dx2_ref[sl, :] + r * (dxn - xn * jnp.mean(dxn * xn, axis=-1, keepdims=True))

        @pl.when(i == nsteps - 1)
        def _():
            copies = [pltpu.make_async_copy(dw_vm.at[src:src + n], dw_hbm.at[dst:dst + n], out_sems.at[k])
                      for k, (src, dst, n) in enumerate(_UNPAD_ROWS)]
            for cp in copies:
                cp.start()
            for cp in copies:
                cp.wait()

    rowb = pl.BlockSpec((tm, D_MODEL), lambda i: (i, 0))
    vec = _full((1, D_MODEL))
    anyspec = pl.BlockSpec(memory_space=pl.ANY)
    piece_specs = [pl.BlockSpec((tm, width), lambda i: (i, 0)) for _, width in _PIECES]
    return pl.pallas_call(
        body, name="inproj_bwd", grid=(nsteps,),
        in_specs=[rowb, vec, vec, vec, anyspec, rowb] + piece_specs,
        out_specs=[rowb, anyspec, vec, vec, vec],
        out_shape=[jax.ShapeDtypeStruct((s, D_MODEL), F32), jax.ShapeDtypeStruct((D_IN, D_MODEL), F32),
                   jax.ShapeDtypeStruct((1, D_MODEL), F32), jax.ShapeDtypeStruct((1, D_MODEL), F32),
                   jax.ShapeDtypeStruct((1, D_MODEL), F32)],
        scratch_shapes=[pltpu.VMEM((D_PAD, D_MODEL), BF), pltpu.VMEM((D_PAD, D_MODEL), F32),
                        pltpu.SemaphoreType.DMA((len(_UNPAD_ROWS),)), pltpu.SemaphoreType.DMA((len(_UNPAD_ROWS),))],
        compiler_params=_params(("arbitrary",)),
    )(x2d, shift, sc1p, g_norm, w_t, dx2, *pieces)


def _adam(w, g, m, v):
    m2 = ADAM_B1 * m + (1.0 - ADAM_B1) * g
    v2 = ADAM_B2 * v + (1.0 - ADAM_B2) * (g * g)
    m_hat = m2 / (1.0 - ADAM_B1 ** ADAM_STEP)
    v_hat = v2 / (1.0 - ADAM_B2 ** ADAM_STEP)
    delta = -ADAM_LR * (m_hat / (jnp.sqrt(v_hat) + ADAM_EPS) + ADAM_WD * w)
    return delta, m2, v2


def _adamw(w, g, m, v, name):
    rr, cc = w.shape
    tc = min(512, cc)

    def body(w_ref, g_ref, m_ref, v_ref, d_ref, m2_ref, v2_ref):
        d_ref[...], m2_ref[...], v2_ref[...] = _adam(w_ref[...], g_ref[...], m_ref[...], v_ref[...])

    blk = pl.BlockSpec((rr, tc), lambda i: (0, i))
    return pl.pallas_call(
        body, name=name, grid=(cc // tc,), in_specs=[blk] * 4, out_specs=[blk] * 3,
        out_shape=[jax.ShapeDtypeStruct((rr, cc), F32)] * 3,
        compiler_params=_params(("arbitrary",)),
    )(w, g, m, v)


def _adamw_t(w3, g, m3, v3, name):
    rr, _, cc = w3.shape
    tc = cc

    def body(w_hbm, g_ref, m_hbm, v_hbm, d_hbm, m2_hbm, v2_hbm, g3_hbm, w_vm, m_vm, v_vm, d_vm, m2_vm, v2_vm, in_sems, out_sems):
        cols = pl.ds(pl.multiple_of(pl.program_id(0) * tc, tc), tc)
        loads = [pltpu.make_async_copy(src.at[:, 0, cols], dst, in_sems.at[k])
                 for k, (src, dst) in enumerate(((w_hbm, w_vm), (m_hbm, m_vm), (v_hbm, v_vm)))]
        for cp in loads:
            cp.start()
        for cp in loads:
            cp.wait()
        d_vm[...], m2_vm[...], v2_vm[...] = _adam(w_vm[...], g_ref[...], m_vm[...], v_vm[...])
        stores = [pltpu.make_async_copy(src, dst.at[:, 0, cols], out_sems.at[k])
                  for k, (src, dst) in enumerate(((d_vm, d_hbm), (m2_vm, m2_hbm), (v2_vm, v2_hbm), (g_ref, g3_hbm)))]
        for cp in stores:
            cp.start()
        for cp in stores:
            cp.wait()

    hbm = pl.BlockSpec(memory_space=pl.ANY)
    return pl.pallas_call(
        body, name=name, grid=(cc // tc,), in_specs=[hbm, pl.BlockSpec((rr, tc), lambda i: (0, i)), hbm, hbm],
        out_specs=[hbm] * 4, out_shape=[jax.ShapeDtypeStruct((rr, 1, cc), F32)] * 4,
        scratch_shapes=[pltpu.VMEM((rr, tc), F32)] * 6 + [pltpu.SemaphoreType.DMA((3,)), pltpu.SemaphoreType.DMA((4,))],
        compiler_params=_params(("arbitrary",)),
    )(w3, g, m3, v3)


def _ada_update(c_all, dmod_cols, w, m, v):
    rr, cc = w.shape
    tr = min(512, rr)
    c_all = jnp.pad(c_all, ((0, 8), (0, 0)))
    dmod_cols = jnp.pad(dmod_cols, ((0, 8), (0, 0)))

    def body(c_ref, dm_ref, w_ref, m_ref, v_ref, g_ref, d_ref, m2_ref, v2_ref):
        cv = c_ref[...]
        sc = (cv * _sigmoid(cv)).astype(BF)
        g = _dot(sc, dm_ref[...].astype(BF), TN)
        g_ref[...] = g
        d_ref[...], m2_ref[...], v2_ref[...] = _adam(w_ref[...], g, m_ref[...], v_ref[...])

    blk = pl.BlockSpec((tr, cc), lambda i: (i, 0))
    return pl.pallas_call(
        body, name="ada_update", grid=(rr // tr,),
        in_specs=[pl.BlockSpec((16, tr), lambda i: (0, i)), _full((16, cc)), blk, blk, blk],
        out_specs=[blk] * 4, out_shape=[jax.ShapeDtypeStruct((rr, cc), F32)] * 4,
        compiler_params=_params(("arbitrary",)),
    )(c_all, dmod_cols, w, m, v)


def _small_update(parts, weights, moms, vels):
    n = len(weights)

    def body(*refs):
        p_refs, w_refs, m_refs, v_refs = refs[:n + 1], refs[n + 1:2 * n + 1], refs[2 * n + 1:3 * n + 1], refs[3 * n + 1:4 * n + 1]
        outs = refs[4 * n + 1:]
        for i in range(n):
            g = p_refs[i][0]
            for d in range(1, 8):
                g = g + p_refs[i][d]
            delta, m2, v2 = _adam(w_refs[i][...], g, m_refs[i][...], v_refs[i][...])
            outs[4 * i][...] = g
            outs[4 * i + 1][...] = delta
            outs[4 * i + 2][...] = m2
            outs[4 * i + 3][...] = v2
        tot = p_refs[n][0]
        for d in range(1, 8):
            tot = tot + p_refs[n][d]
        outs[4 * n][...] = tot

    out_shape = []
    for w in weights:
        out_shape += [jax.ShapeDtypeStruct(w.shape, F32)] * 4
    out_shape.append(jax.ShapeDtypeStruct(parts[n].shape[1:], F32))
    return pl.pallas_call(body, name="small_update", out_shape=out_shape, compiler_params=_params())(
        *parts, *weights, *moms, *vels)


def _rows8(a):
    flat = a.reshape(-1)
    rows = -(-flat.shape[0] // LANES)
    rows8 = -(-rows // 8) * 8
    flat = jnp.pad(flat, (0, rows8 * LANES - flat.shape[0]))
    return flat.reshape(rows8, LANES)


def kernel(x, c, positions, w_ada, b_ada, g_norm, w_in, w_decay, b_decay, g_gla_head, sinks, w_out, g_final, loss_target, m_w_ada, m_b_ada, m_g_norm, m_w_in, m_w_decay, m_b_decay, m_g_gla_head, m_sinks, m_w_out, m_g_final, v_w_ada, v_b_ada, v_g_norm, v_w_in, v_w_decay, v_b_decay, v_g_gla_head, v_sinks, v_w_out, v_g_final):
    ax, ay, ac = lax.axis_index("x"), lax.axis_index("y"), lax.axis_index("c")
    chip = 2 * ax + ay
    dev = 2 * chip + ac
    s = x.shape[1]
    x2d = x[0]
    target = loss_target[0]
    w_ada2, w_out2, w_dec2 = w_ada[0], w_out[0], w_decay[0]
    w_in_t = w_in[0].T
    ada_cols = w_ada2.shape[1]
    in_cols = w_in_t.shape[0]
    out_rows = w_out2.shape[0]
    half = D_MODEL // 2

    cw = jnp.concatenate([c.reshape(8, LANES), w_dec2.reshape(8, LANES)], axis=0)
    b_shard = lax.dynamic_slice(b_ada, (0, chip * ada_cols), (1, ada_cols))
    half_in = lax.dynamic_slice(w_in_t, (0, ac * half), (in_cols, half)).astype(BF)
    half_out = lax.dynamic_slice(w_out2, (ac * (out_rows // 2), 0), (out_rows // 2, D_MODEL)).astype(BF)
    inv_freq = 1.0 / (ROPE_THETA ** (jnp.arange(0, 64, 2, dtype=F32) / 64))
    first, mod_all, w_in_all, w_out_all, cos, sin = _prologue(
        cw, w_ada2, b_shard, half_in, half_out, positions.reshape(s, 1), jnp.tile(inv_freq, 4).reshape(1, LANES))

    first = first.reshape(8, 2, 8, LANES)
    c_all = first[:, 0].reshape(8, D_MODEL)
    w_dec_full = first[0::2, 1].reshape(4, GLA_RANK, 64).transpose(1, 0, 2).reshape(GLA_RANK, 256)
    mod = mod_all.reshape(4, 2, 8, ada_cols)[:, 0]
    mod = lax.dynamic_slice(mod, (0, dev, 0), (4, 1, ada_cols)).reshape(1, 4 * ada_cols)
    shift, sc1p, gate = mod[:, :D_MODEL], 1.0 + mod[:, D_MODEL:2 * D_MODEL], mod[:, 2 * D_MODEL:]
    w_t = w_in_all.reshape(4 * in_cols, D_MODEL)
    w_out_all = w_out_all.reshape(D_MODEL, D_MODEL)

    wdecp = jnp.pad(w_dec_full, ((0, LANES - GLA_RANK), (0, 0))).astype(BF)

    proj = _inproj_fwd(x2d, shift, sc1p, g_norm, w_t)
    og, o_gla, sprev = _gla_fwd(proj, wdecp, b_decay, g_gla_head)
    osw, o_swa = _swa_fwd(proj, cos, sin, sinks)
    dx2, dog, dos, dw_out, loss_p, dgf, dgate = _outproj(og, osw, w_out_all, x2d, target, gate, g_final.reshape(1, D_MODEL))
    dsq, dsz, dsk, dsv, dsinks = _swa_bwd(proj, dos, o_swa, cos, sin, sinks)
    dqk, dv, dgz, dga, dwdp, dbd, dgg = _gla_bwd(proj, dog, o_gla, sprev, wdecp, b_decay, g_gla_head)
    pieces = (dqk, dv, dgz, dsq, dsz, dsk, dsv, dga)
    gx, dw_in_t, dshift, dscale, dgn = _inproj_bwd(x2d, shift, sc1p, g_norm, w_t, dx2, pieces)

    segs = [jnp.concatenate([dshift, dscale, dgate], axis=1), dgn, dgf, dwdp[:GLA_RANK], dbd, dgg, dsinks, loss_p]
    packed = [_rows8(a) for a in segs]
    offs = [0]
    for a in packed:
        offs.append(offs[-1] + a.shape[0])
    g_w_in_t, g_w_out, small = _epilogue(dw_in_t.reshape(4, in_cols, D_MODEL), dw_out.reshape(4, out_rows, D_MODEL),
                                         jnp.concatenate(packed, axis=0))

    def seg(i, size):
        return small[:, offs[i]:offs[i + 1]].reshape(8, -1)[:, :size]

    dmod_all = seg(0, 3 * D_MODEL)
    dwd_all = lax.dynamic_slice(seg(3, GLA_RANK * 256).reshape(8, GLA_RANK, 256), (0, 0, chip * 64), (8, GLA_RANK, 64))
    parts = [dmod_all.reshape(8, 1, 3 * D_MODEL), seg(1, D_MODEL).reshape(8, 1, D_MODEL), dwd_all,
             seg(4, 256).reshape(8, 1, 256), seg(5, 512).reshape(8, 1, 512), seg(6, SWA_HEADS).reshape(8, 1, SWA_HEADS),
             seg(2, D_MODEL).reshape(8, 1, D_MODEL), seg(7, LANES).reshape(8, 1, LANES)]
    smalls = _small_update(
        parts,
        [b_ada, g_norm, w_dec2, b_decay, g_gla_head, sinks, g_final.reshape(1, D_MODEL)],
        [m_b_ada, m_g_norm, m_w_decay[0], m_b_decay, m_g_gla_head, m_sinks, m_g_final.reshape(1, D_MODEL)],
        [v_b_ada, v_g_norm, v_w_decay[0], v_b_decay, v_g_gla_head, v_sinks, v_g_final.reshape(1, D_MODEL)])
    (g_b_ada, d_b_ada, nm_b_ada, nv_b_ada, g_gn, d_gn, nm_gn, nv_gn, g_wd, d_wd, nm_wd, nv_wd,
     g_bd, d_bd, nm_bd, nv_bd, g_gg, d_gg, nm_gg, nv_gg, g_sk, d_sk, nm_sk, nv_sk,
     g_gf, d_gf, nm_gf, nv_gf, loss_row) = smalls
    loss = loss_row[0, 0]

    dmod_cols = lax.dynamic_slice(dmod_all, (0, chip * ada_cols), (8, ada_cols))
    g_w_ada, d_w_ada, nm_w_ada, nv_w_ada = _ada_update(c_all, dmod_cols, w_ada2, m_w_ada[0], v_w_ada[0])
    to3 = lambda a: jnp.transpose(a, (2, 0, 1))
    from3 = lambda a: jnp.transpose(a, (1, 2, 0))[0]
    d3, nm3, nv3, g3 = _adamw_t(to3(w_in), g_w_in_t, to3(m_w_in), to3(v_w_in), "adamw_w_in")
    g_w_in, d_w_in, nm_w_in, nv_w_in = from3(g3), from3(d3), from3(nm3), from3(nv3)
    d_w_out, nm_w_out, nv_w_out = _adamw(w_out2, g_w_out, m_w_out[0], v_w_out[0], "adamw_w_out")

    flat = lambda a: a.reshape(D_MODEL)
    grads = [g_w_ada[None], g_b_ada, g_gn, g_w_in[None], g_wd[None], g_bd, g_gg, g_sk, g_w_out[None], flat(g_gf)]
    deltas = [d_w_ada[None], d_b_ada, d_gn, d_w_in[None], d_wd[None], d_bd, d_gg, d_sk, d_w_out[None], flat(d_gf)]
    new_m = [nm_w_ada[None], nm_b_ada, nm_gn, nm_w_in[None], nm_wd[None], nm_bd, nm_gg, nm_sk, nm_w_out[None], flat(nm_gf)]
    new_v = [nv_w_ada[None], nv_b_ada, nv_gn, nv_w_in[None], nv_wd[None], nv_bd, nv_gg, nv_sk, nv_w_out[None], flat(nv_gf)]
    return (loss, gx[None], *grads, *deltas, *new_m, *new_v)
```

```python
import jax
import jax.numpy as jnp
from jax import lax
from jax.experimental import pallas as pl
from jax.experimental.pallas import tpu as pltpu

F32 = jnp.float32
BF = jnp.bfloat16

D_MODEL = 1024
GLA_HEADS = 4
GLA_DK = 64
GLA_CHUNK = 64
GLA_RANK = 16
GLA_TAU = 16.0
GLA_SUB = 256
GLA_ROWS_FWD = 1024
GLA_ROWS_BWD = 512
SWA_HEADS = 8
SWA_BLOCK = 128
SWA_QBLOCKS_FWD = 8
SWA_QBLOCKS = 8
RMS_EPS = 1e-6
ROPE_THETA = 10000.0

OFF_QK, OFF_V, OFF_GZ, OFF_SQ, OFF_SZ, OFF_SK, OFF_SV, OFF_GA = 0, 512, 1024, 1536, 2048, 2560, 2688, 2816
D_PAD = 2944
D_IN = 2832
LANES = 128
VMEM_LIMIT = 56 * 1024 * 1024

ADAM_LR, ADAM_B1, ADAM_B2, ADAM_EPS, ADAM_WD, ADAM_STEP = 0.001, 0.9, 0.999, 1e-08, 0.01, 10

NT = (((1,), (1,)), ((), ()))
TN = (((0,), (0,)), ((), ()))
MESH = pl.DeviceIdType.MESH


def _dot(a, b, dims=None):
    if dims is None:
        return jnp.dot(a, b, preferred_element_type=F32)
    return lax.dot_general(a, b, dims, preferred_element_type=F32)


def _sigmoid(x):
    return 1.0 / (1.0 + jnp.exp(-x))


def _params(sem=None):
    return pltpu.CompilerParams(dimension_semantics=sem, vmem_limit_bytes=VMEM_LIMIT)


def _full(shape):
    return pl.BlockSpec(shape, lambda i: (0,) * len(shape))


def _subtiles(rows, size=256):
    size = min(size, rows)
    return [slice(k * size, (k + 1) * size) for k in range(rows // size)]


WEIGHT_CHUNKS = 4


def _gather_sems(chunks=1):
    return [pltpu.SemaphoreType.DMA((7 * chunks,)), pltpu.SemaphoreType.DMA((7 * chunks,)), pltpu.SemaphoreType.DMA]


_GATHER_SEMS = _gather_sems()


class _Gather:
    def __init__(self, x_ref, out_ref, send_sems, recv_sems, local_sem, slab=None, chunks=1):
        self.slab_of = slab
        self.chunks = chunks
        self.width = x_ref.shape[-1] // chunks
        x, y, c = lax.axis_index("x"), lax.axis_index("y"), lax.axis_index("c")
        self.me, self.sibling, self.c = (x, y, c), (x, y, 1 - c), c
        self.xn, self.yn, self.dg = (1 - x, y), (x, 1 - y), (1 - x, 1 - y)
        self.pass_from = (lax.rem(x + 1 - c, 2), lax.rem(y + c, 2))
        self.pass_to = (lax.rem(x + c, 2), lax.rem(y + 1 - c, 2))
        self.x_ref, self.out_ref, self.send_sems, self.recv_sems = x_ref, out_ref, send_sems, recv_sems
        self.mine = pltpu.make_async_copy(x_ref, self._slab(*self.me), local_sem)

    def _slab(self, px, py, pc):
        if self.slab_of is not None:
            return self.slab_of(self.out_ref, px, py, pc)
        return self.out_ref.at[4 * px + 2 * py + pc]

    def _part(self, ref, q):
        if self.chunks == 1:
            return ref
        lanes = slice(q * self.width, (q + 1) * self.width)
        return ref.at[(slice(None),) * (len(ref.shape) - 1) + (lanes,)]

    def _copy(self, k, q, blk, to, src=None):
        i = k * self.chunks + q
        return pltpu.make_async_remote_copy(
            src_ref=self._part(self._slab(*blk) if src is None else src, q), dst_ref=self._part(self._slab(*blk), q),
            send_sem=self.send_sems.at[i], recv_sem=self.recv_sems.at[i], device_id=to, device_id_type=MESH)

    def _sends(self, q):
        c = self.c
        return [self._copy(0, q, self.me, self.sibling, src=self.x_ref),
                self._copy(1, q, self.me, (*self.xn, c), src=self.x_ref),
                self._copy(2, q, self.me, (*self.yn, c), src=self.x_ref),
                self._copy(3, q, (*self.pass_from, c), (*self.pass_to, c)),
                self._copy(4, q, (*self.xn, c), self.sibling),
                self._copy(5, q, (*self.yn, c), self.sibling),
                self._copy(6, q, (*self.dg, c), self.sibling)]

    def start(self):
        self.mine.start()
        for q in range(self.chunks):
            sends = self._sends(q)
            for k in (1, 2, 0):
                sends[k].start()

    def pass_on(self, only=None):
        for q in range(self.chunks) if only is None else (only,):
            sends = self._sends(q)
            self._copy(1, q, (*self.xn, self.c), self.me).wait_recv()
            self._copy(2, q, (*self.yn, self.c), self.me).wait_recv()
            for k in (3, 4, 5):
                sends[k].start()

    def relay_diagonal(self, only=None):
        for q in range(self.chunks) if only is None else (only,):
            self._copy(3, q, (*self.dg, self.c), self.me).wait_recv()
            self._sends(q)[6].start()

    def relay(self):
        self.pass_on()
        self.relay_diagonal()

    def finish(self):
        c = self.c
        for q in range(self.chunks):
            self._copy(0, q, self.sibling, self.me).wait_recv()
            for k, chip in ((4, self.xn), (5, self.yn), (6, self.dg)):
                self._copy(k, q, (*chip, 1 - c), self.me).wait_recv()
            for cp in self._sends(q):
                cp.wait_send()
        self.mine.wait()


def _prologue(cw, w_ada, b_shard, half_in, half_out, pos_col, inv_freq):
    s = pos_col.shape[0]
    rt = min(512, s)

    def body(cw_ref, wada_hbm, b_ref, hin_ref, hout_ref, pos_hbm, f_ref,
             first_ref, mod_ref, win_ref, wout_ref, cos_hbm, sin_hbm,
             mod_blk, cos_ref, sin_ref, wada_ref, pos_ref, table_sems, local_sems, *sems):
        fetch_w = pltpu.make_async_copy(wada_hbm, wada_ref, local_sems.at[0])
        fetch_p = pltpu.make_async_copy(pos_hbm, pos_ref, local_sems.at[1])
        fetch_w.start()
        fetch_p.start()
        g_c = _Gather(cw_ref, first_ref, *sems[0:3])
        half_lanes = hin_ref.shape[1]
        g_in = _Gather(hin_ref, win_ref, *sems[3:6], chunks=WEIGHT_CHUNKS,
                       slab=lambda ref, px, py, pc: ref.at[2 * px + py, :, pl.ds(pl.multiple_of(pc * half_lanes, half_lanes), half_lanes)])
        g_out = _Gather(hout_ref, wout_ref, *sems[6:9], chunks=WEIGHT_CHUNKS)
        g_mod = _Gather(mod_blk, mod_ref, *sems[9:12])
        g_c.start()
        g_in.start()
        g_out.start()
        g_c.relay()
        g_c.finish()
        c_rows = [jnp.concatenate([first_ref[d, r:r + 1, :] for r in range(8)], axis=1) for d in range(8)]
        c_all = jnp.concatenate(c_rows, axis=0)
        sc = (c_all * _sigmoid(c_all)).astype(BF)
        fetch_w.wait()
        mod_blk[...] = _dot(sc, wada_ref[...].astype(BF)) + b_ref[...]
        g_mod.start()
        fetch_p.wait()

        def rope_rows(i, carry):
            rows = pl.ds(pl.multiple_of(i * rt, rt), rt)
            ang = pos_ref[rows, :].astype(F32) * f_ref[...]
            lane = lax.broadcasted_iota(jnp.int32, ang.shape, 1)
            cos_ref[rows, :] = jnp.cos(ang)
            sn = jnp.sin(ang)
            sin_ref[rows, :] = jnp.where((lane % 64) < 32, -sn, sn)
            pltpu.make_async_copy(cos_ref.at[rows, :], cos_hbm.at[rows, :], table_sems.at[0]).start()
            pltpu.make_async_copy(sin_ref.at[rows, :], sin_hbm.at[rows, :], table_sems.at[1]).start()
            return carry

        waits = ([lambda q=q: g_in.pass_on(q) for q in range(WEIGHT_CHUNKS)] + [g_out.pass_on]
                 + [lambda q=q: g_in.relay_diagonal(q) for q in range(WEIGHT_CHUNKS)] + [g_out.relay_diagonal, g_mod.relay])
        steps = s // rt
        lead = steps // 4
        per_wait = max((steps - lead) // len(waits), 1)
        lax.fori_loop(0, lead, rope_rows, 0)
        done = lead
        for wait in waits:
            wait()
            nxt = min(done + per_wait, steps)
            lax.fori_loop(done, nxt, rope_rows, 0)
            done = nxt
        lax.fori_loop(done, steps, rope_rows, 0)
        g_in.finish()
        g_out.finish()
        g_mod.finish()
        pltpu.make_async_copy(cos_ref, cos_hbm, table_sems.at[0]).wait()
        pltpu.make_async_copy(sin_ref, sin_hbm, table_sems.at[1]).wait()

    vm = pl.BlockSpec(memory_space=pltpu.VMEM)
    hbm = pl.BlockSpec(memory_space=pl.ANY)
    return pl.pallas_call(
        body, name="prologue",
        out_shape=[jax.ShapeDtypeStruct((8,) + cw.shape, F32), jax.ShapeDtypeStruct((8, 8, w_ada.shape[1]), F32),
                   jax.ShapeDtypeStruct((4, half_in.shape[0], 2 * half_in.shape[1]), half_in.dtype),
                   jax.ShapeDtypeStruct((8,) + half_out.shape, half_out.dtype),
                   jax.ShapeDtypeStruct((s, LANES), F32), jax.ShapeDtypeStruct((s, LANES), F32)],
        in_specs=[vm, hbm, vm, hbm, hbm, hbm, vm], out_specs=[vm, vm, hbm, hbm, hbm, hbm],
        scratch_shapes=[pltpu.VMEM((8, w_ada.shape[1]), F32), pltpu.VMEM((s, LANES), F32), pltpu.VMEM((s, LANES), F32),
                        pltpu.VMEM(w_ada.shape, F32), pltpu.VMEM(pos_col.shape, jnp.int32),
                        pltpu.SemaphoreType.DMA((2,)), pltpu.SemaphoreType.DMA((2,))]
        + _GATHER_SEMS + _gather_sems(WEIGHT_CHUNKS) * 2 + _GATHER_SEMS,
        compiler_params=pltpu.CompilerParams(vmem_limit_bytes=VMEM_LIMIT),
    )(cw, w_ada, b_shard, half_in, half_out, pos_col, inv_freq)


def _reduce_scratch(rr, cc):
    c2 = cc // 2
    return [pltpu.VMEM((4, rr, c2), F32), pltpu.VMEM((4, rr, c2), F32), pltpu.VMEM((3, rr, c2), BF),
            pltpu.VMEM((2, rr, c2), BF), pltpu.VMEM((rr, c2), BF), pltpu.VMEM((rr, c2), F32),
            pltpu.SemaphoreType.DMA((8 + 3 * WEIGHT_CHUNKS,)), pltpu.SemaphoreType.DMA((8 + 3 * WEIGHT_CHUNKS,)),
            pltpu.SemaphoreType.DMA((5,))]


class _Reduce:
    def __init__(self, p_hbm, out_ref, acc_ref, own_ref, send_ref, land_ref, relay_ref, res_ref,
                 send_sems, recv_sems, local_sems):
        x, y, c = lax.axis_index("x"), lax.axis_index("y"), lax.axis_index("c")
        c2 = out_ref.shape[1] // 2
        sibling = (x, y, 1 - c)
        first = (lax.rem(x + 1 - c, 2), lax.rem(y + c, 2))
        second = (lax.rem(x + c, 2), lax.rem(y + 1 - c, 2))
        shards = [2 * first[0] + first[1], 2 * second[0] + second[1], 2 * (1 - x) + (1 - y), 2 * x + y]
        sibling_slot = (1, 0, 2, 3)
        mine = pl.ds(pl.multiple_of(c * c2, c2), c2)
        other = pl.ds(pl.multiple_of((1 - c) * c2, c2), c2)
        self.acc_ref, self.own_ref, self.send_ref, self.land_ref = acc_ref, own_ref, send_ref, land_ref
        self.relay_ref, self.res_ref = relay_ref, res_ref
        self.own = [pltpu.make_async_copy(p_hbm.at[j, :, mine], own_ref.at[k], local_sems.at[k])
                    for k, j in enumerate(shards)]
        self.swap_out = [pltpu.make_async_remote_copy(
            src_ref=p_hbm.at[j, :, other], dst_ref=acc_ref.at[sibling_slot[k]], send_sem=send_sems.at[k],
            recv_sem=recv_sems.at[sibling_slot[k]], device_id=sibling, device_id_type=MESH) for k, j in enumerate(shards)]
        self.swap_in = [pltpu.make_async_remote_copy(
            src_ref=p_hbm.at[j, :, other], dst_ref=acc_ref.at[k], send_sem=send_sems.at[k], recv_sem=recv_sems.at[k],
            device_id=sibling, device_id_type=MESH) for k, j in enumerate(shards)]

        self.lanes = [slice(q * (c2 // WEIGHT_CHUNKS), (q + 1) * (c2 // WEIGHT_CHUNKS)) for q in range(WEIGHT_CHUNKS)]

        def message(m, src, dst, to):
            return [pltpu.make_async_remote_copy(
                src_ref=src.at[:, ln], dst_ref=dst.at[:, ln], send_sem=send_sems.at[8 + m * WEIGHT_CHUNKS + q],
                recv_sem=recv_sems.at[8 + m * WEIGHT_CHUNKS + q], device_id=(*to, c), device_id_type=MESH)
                for q, ln in enumerate(self.lanes)]

        self.direct = message(0, send_ref.at[0], land_ref.at[0], first)
        self.passed = message(1, send_ref.at[1], relay_ref, first)
        self.joint = message(2, send_ref.at[2], land_ref.at[1], second)
        self.put = pltpu.make_async_copy(res_ref, out_ref.at[:, mine], local_sems.at[4])
        self.share = pltpu.make_async_remote_copy(
            src_ref=res_ref, dst_ref=out_ref.at[:, mine], send_sem=send_sems.at[7],
            recv_sem=recv_sems.at[7], device_id=sibling, device_id_type=MESH)

    def start(self):
        for k in (2, 0, 1, 3):
            self.own[k].start()
            self.swap_out[k].start()

    def _combine(self, k):
        self.own[k].wait()
        self.swap_out[k].wait_send()
        self.swap_in[k].wait_recv()
        self.acc_ref[k] = self.acc_ref[k] + self.own_ref[k]

    def combine_and_send(self):
        dt = self.send_ref.dtype
        self._combine(2)
        self.send_ref[1] = self.acc_ref[2].astype(dt)
        for cp in self.passed:
            cp.start()
        self._combine(0)
        self.send_ref[0] = self.acc_ref[0].astype(dt)
        for cp in self.direct:
            cp.start()
        self._combine(1)
        for q, ln in enumerate(self.lanes):
            self.passed[q].wait_recv()
            self.send_ref[2, :, ln] = (self.acc_ref[1, :, ln] + self.relay_ref[:, ln].astype(F32)).astype(dt)
            self.joint[q].start()
        self._combine(3)

    def total_and_share(self):
        for cp in self.direct + self.joint:
            cp.wait_recv()
        self.res_ref[...] = self.acc_ref[3] + self.land_ref[0].astype(F32) + self.land_ref[1].astype(F32)
        for cp in self.direct + self.passed + self.joint:
            cp.wait_send()
        self.put.start()
        self.share.start()

    def finish(self):
        self.put.wait()
        self.share.wait()


def _epilogue(dw_in_parts, dw_out_parts, small):
    _, r_in, cc = dw_in_parts.shape
    _, r_out, _ = dw_out_parts.shape
    n_red = len(_reduce_scratch(r_in, cc))

    def body(pin_hbm, pout_hbm, small_ref, gin_ref, gout_ref, small_all_ref, *scratch):
        red_in = _Reduce(pin_hbm, gin_ref, *scratch[0:n_red])
        red_out = _Reduce(pout_hbm, gout_ref, *scratch[n_red:2 * n_red])
        gat = _Gather(small_ref, small_all_ref, *scratch[2 * n_red:])
        red_out.start()
        red_in.start()
        gat.start()
        red_out.combine_and_send()
        red_in.combine_and_send()
        gat.relay()
        red_out.total_and_share()
        red_in.total_and_share()
        gat.finish()
        red_out.finish()
        red_in.finish()

    vm = pl.BlockSpec(memory_space=pltpu.VMEM)
    anyspec = pl.BlockSpec(memory_space=pl.ANY)
    return pl.pallas_call(
        body, name="epilogue",
        out_shape=[jax.ShapeDtypeStruct((r_in, cc), F32), jax.ShapeDtypeStruct((r_out, cc), F32),
                   jax.ShapeDtypeStruct((8,) + small.shape, F32)],
        in_specs=[anyspec, anyspec, vm], out_specs=[anyspec, anyspec, vm],
        scratch_shapes=_reduce_scratch(r_in, cc) + _reduce_scratch(r_out, cc) + _GATHER_SEMS,
        compiler_params=pltpu.CompilerParams(vmem_limit_bytes=VMEM_LIMIT),
    )(dw_in_parts, dw_out_parts, small)


def _rope(t, cosb, sinb, first_half):
    partner = jnp.where(first_half, pltpu.roll(t, 96, 1), pltpu.roll(t, 32, 1))
    return t * cosb + partner * sinb


def _rope_t(g, cosb, sinb, first_half):
    gs = g * sinb
    partner = jnp.where(first_half, pltpu.roll(gs, 96, 1), pltpu.roll(gs, 32, 1))
    return g * cosb + partner


def _modnorm(x, g, sc1p, shift):
    r = lax.rsqrt(jnp.mean(x * x, axis=-1, keepdims=True) + RMS_EPS)
    xn = x * r
    return xn, r, (xn * g) * sc1p + shift


def _load_w_padded(w_hbm, w_vm, sems):
    copies = [pltpu.make_async_copy(w_hbm.at[ref:ref + n], w_vm.at[pad:pad + n], sems.at[k])
              for k, (pad, ref, n) in enumerate(_UNPAD_ROWS)]
    for cp in copies:
        cp.start()
    w_vm[OFF_GA + GLA_RANK:, :] = jnp.zeros((D_PAD - OFF_GA - GLA_RANK, D_MODEL), w_vm.dtype)
    return copies


def _inproj_fwd(x2d, shift, sc1p, g_norm, w_t):
    s = x2d.shape[0]
    tm = min(1024, s)

    def body(x_ref, sh_ref, sc_ref, g_ref, w_hbm, o_ref, w_vm, sems):
        @pl.when(pl.program_id(0) == 0)
        def _():
            for cp in _load_w_padded(w_hbm, w_vm, sems):
                cp.wait()

        subs = _subtiles(tm)
        hs = [_modnorm(x_ref[sl, :], g_ref[...], sc_ref[...], sh_ref[...])[2].astype(BF) for sl in subs]
        for sl, h in zip(subs, hs):
            o_ref[sl, :] = _dot(h, w_vm[...], NT)

    vec = _full((1, D_MODEL))
    return pl.pallas_call(
        body, name="inproj_fwd", grid=(s // tm,),
        in_specs=[pl.BlockSpec((tm, D_MODEL), lambda i: (i, 0)), vec, vec, vec, pl.BlockSpec(memory_space=pl.ANY)],
        out_specs=pl.BlockSpec((tm, D_PAD), lambda i: (i, 0)),
        out_shape=jax.ShapeDtypeStruct((s, D_PAD), F32),
        scratch_shapes=[pltpu.VMEM((D_PAD, D_MODEL), BF), pltpu.SemaphoreType.DMA((len(_UNPAD_ROWS),))],
        compiler_params=_params(("arbitrary",)),
    )(x2d, shift, sc1p, g_norm, w_t)


def _split3(a):
    hi = a.astype(BF)
    r1 = a - hi.astype(F32)
    mid = r1.astype(BF)
    lo = (r1 - mid.astype(F32)).astype(BF)
    return hi, mid, lo


def _tri_matmul(tri, a):
    hi, mid, lo = _split3(a)
    return _dot(tri, hi) + _dot(tri, mid) + _dot(tri, lo)


def _chunks(tb):
    return [slice(c * GLA_CHUNK, (c + 1) * GLA_CHUNK) for c in range(tb // GLA_CHUNK)]


def _per_chunk_rows(rows, width):
    return jnp.concatenate([jnp.broadcast_to(r, (GLA_CHUNK, width)) for r in rows], axis=0)


def _gla_triangle(tb):
    row = lax.broadcasted_iota(jnp.int32, (tb, tb), 0)
    col = lax.broadcasted_iota(jnp.int32, (tb, tb), 1)
    return (((row // GLA_CHUNK) == (col // GLA_CHUNK)) & (col <= row)).astype(F32)


def _lane_mean(x, ones_b):
    hi = x.astype(BF)
    lo = (x - hi.astype(F32)).astype(BF)
    return (_dot(hi, ones_b) + _dot(lo, ones_b)) * (1.0 / LANES)


def _head(t, h, lo_h):
    blk = t[:, LANES * (h // 2):LANES * (h // 2 + 1)]
    return jnp.where(lo_h, blk, 0.0) if h % 2 == 0 else jnp.where(lo_h, 0.0, blk)


def _gla_block_common(qk, ga, wd, bd, tril_b):
    tb = qk.shape[0]
    q, k = qk[:, :256], qk[:, 256:]
    z = _dot(ga.astype(BF), wd) + bd
    la = (jnp.minimum(z, 0.0) - jnp.log(1.0 + jnp.exp(-jnp.abs(z)))) * (1.0 / GLA_TAU)
    b = _tri_matmul(tril_b, la)
    bls = [b[rs.stop - 1:rs.stop, :] for rs in _chunks(tb)]
    eq = jnp.exp(b)
    ek = jnp.exp(-b)
    f = jnp.exp(_per_chunk_rows(bls, 256) - b)
    return z, eq, ek, f, q * (eq * GLA_DK ** -0.5), k * ek, k * f, bls


def _gla_units(s, rows):
    sub = min(GLA_SUB, s)
    tb = min(rows, s)
    subs = [slice(i * sub, (i + 1) * sub) for i in range(tb // sub)]
    units = [(i, h) for i in range(len(subs)) for h in range(GLA_HEADS)]
    return tb, sub, subs, units


def _gla_fwd(proj, wdecp, bdec, ggla):
    s = proj.shape[0]
    tb, sub, subs, units = _gla_units(s, GLA_ROWS_FWD)
    nch = sub // GLA_CHUNK

    def body(qk_ref, v_ref, gz_ref, ga_ref, wd_ref, bd_ref, gg_ref, tri_ref, og_ref, opre_ref, sprev_ref, st_ref):
        @pl.when(pl.program_id(0) == 0)
        def _():
            st_ref[...] = jnp.zeros_like(st_ref)

        lo_h = lax.broadcasted_iota(jnp.int32, (sub, LANES), 1) < GLA_DK
        tril = tri_ref[...] > 0.5
        tril_b = tri_ref[...].astype(BF)
        ones_b = jnp.ones((LANES, LANES), BF)
        gg, wd, bd = gg_ref[...], wd_ref[...], bd_ref[...]
        chunks = _chunks(sub)
        lanes = [slice(h * LANES, (h + 1) * LANES) for h in range(GLA_HEADS)]
        com = [_gla_block_common(qk_ref[sl, :], ga_ref[sl, :], wd, bd, tril_b) for sl in subs]
        decs = [[jnp.exp(bl) for bl in cm[7]] for cm in com]
        a = {(i, h): _head(com[i][4], h, lo_h).astype(BF) for i, h in units}
        bm = {(i, h): _head(com[i][5], h, lo_h).astype(BF) for i, h in units}
        ktl = {(i, h): _head(com[i][6], h, lo_h).astype(BF) for i, h in units}
        vh = {(i, h): v_ref[subs[i], lanes[h]].astype(BF) for i, h in units}
        sc = {u: _dot(a[u], bm[u], NT) for u in units}
        upd = {u: [_dot(vh[u][rs], ktl[u][rs], TN) for rs in chunks] for u in units}
        p = {u: jnp.where(tril, sc[u], 0.0).astype(BF) for u in units}
        o = {u: _dot(p[u], vh[u]) for u in units}
        states = {}
        for h in range(GLA_HEADS):
            st = st_ref[h]
            for i in range(len(subs)):
                entering = []
                for c in range(nch):
                    entering.append(st)
                    sprev_ref[i * nch + c, h] = st
                    st = st * decs[i][c][:, LANES * (h // 2):LANES * (h // 2 + 1)] + upd[(i, h)][c]
                states[(i, h)] = entering
            st_ref[h] = st
        inter = {u: [_dot(a[u][rs], states[u][c].astype(BF), NT) for c, rs in enumerate(chunks)] for u in units}
        o = {u: o[u] + jnp.concatenate(inter[u], axis=0) for u in units}
        ms = {u: _lane_mean(o[u] * o[u], ones_b) for u in units}
        for i, h in units:
            gzh = gz_ref[subs[i], lanes[h]]
            opre_ref[subs[i], lanes[h]] = o[(i, h)]
            og_ref[subs[i], lanes[h]] = (((o[(i, h)] * lax.rsqrt(ms[(i, h)] + RMS_EPS)) * gg[:, lanes[h]])
                                         * (gzh * _sigmoid(gzh))).astype(og_ref.dtype)

    def col(width, off):
        return pl.BlockSpec((tb, width), lambda i: (i, off // width))

    return pl.pallas_call(
        body, name="gla_fwd", grid=(s // tb,),
        in_specs=[col(512, OFF_QK), col(512, OFF_V), col(512, OFF_GZ), col(LANES, OFF_GA),
                  _full((LANES, 256)), _full((1, 256)), _full((1, 512)), _full((sub, sub))],
        out_specs=[pl.BlockSpec((tb, 512), lambda i: (i, 0)), pl.BlockSpec((tb, 512), lambda i: (i, 0)),
                   pl.BlockSpec((tb // GLA_CHUNK, GLA_HEADS, LANES, LANES), lambda i: (i, 0, 0, 0))],
        out_shape=[jax.ShapeDtypeStruct((s, 512), BF), jax.ShapeDtypeStruct((s, 512), F32),
                   jax.ShapeDtypeStruct((s // GLA_CHUNK, GLA_HEADS, LANES, LANES), F32)],
        scratch_shapes=[pltpu.VMEM((GLA_HEADS, LANES, LANES), F32)],
        compiler_params=_params(("arbitrary",)),
    )(proj, proj, proj, proj, wdecp, bdec, ggla, _gla_triangle(sub))


def _gla_bwd(proj, dog, opre, sprev, wdecp, bdec, ggla):
    s = proj.shape[0]
    tb, sub, subs, units = _gla_units(s, GLA_ROWS_BWD)
    nsub = len(subs)
    nch = sub // GLA_CHUNK
    nb = s // tb

    def body(qk_ref, v_ref, gz_ref, ga_ref, dog_ref, opre_ref, sprev_ref, wd_ref, bd_ref, gg_ref, tri_ref, triu_ref,
             dqk_ref, dv_ref, dgz_ref, dga_ref, dwd_ref, dbd_ref, dgg_ref, dst_ref):
        @pl.when(pl.program_id(0) == 0)
        def _():
            dst_ref[...] = jnp.zeros_like(dst_ref)
            dwd_ref[...] = jnp.zeros_like(dwd_ref)
            dbd_ref[...] = jnp.zeros_like(dbd_ref)
            dgg_ref[...] = jnp.zeros_like(dgg_ref)

        lo_h = lax.broadcasted_iota(jnp.int32, (sub, LANES), 1) < GLA_DK
        tril = tri_ref[...] > 0.5
        tril_b = tri_ref[...].astype(BF)
        triu_b = triu_ref[...].astype(BF)
        ones_b = jnp.ones((LANES, LANES), BF)
        last_row = (lax.broadcasted_iota(jnp.int32, (sub, LANES), 0) % GLA_CHUNK) == GLA_CHUNK - 1
        wd, gg, bd = wd_ref[...], gg_ref[...], bd_ref[...]
        chunks = _chunks(sub)
        lanes = [slice(h * LANES, (h + 1) * LANES) for h in range(GLA_HEADS)]
        blks = [slice(LANES * (h // 2), LANES * (h // 2 + 1)) for h in range(GLA_HEADS)]
        ga = [ga_ref[sl, :] for sl in subs]
        com = [_gla_block_common(qk_ref[sl, :], ga[i], wd, bd, tril_b) for i, sl in enumerate(subs)]
        decs = [[jnp.exp(bl) for bl in cm[7]] for cm in com]
        a = {(i, h): _head(com[i][4], h, lo_h).astype(BF) for i, h in units}
        bm = {(i, h): _head(com[i][5], h, lo_h).astype(BF) for i, h in units}
        ktl = {(i, h): _head(com[i][6], h, lo_h).astype(BF) for i, h in units}
        vh = {(i, h): v_ref[subs[i], lanes[h]].astype(BF) for i, h in units}
        sc = {u: _dot(a[u], bm[u], NT) for u in units}

        o = {(i, h): opre_ref[subs[i], lanes[h]] for i, h in units}
        ms = {u: _lane_mean(o[u] * o[u], ones_b) for u in units}
        gz = {(i, h): gz_ref[subs[i], lanes[h]] for i, h in units}
        dog = {(i, h): dog_ref[subs[i], lanes[h]] for i, h in units}
        sg = {u: _sigmoid(gz[u]) for u in units}
        r = {u: lax.rsqrt(ms[u] + RMS_EPS) for u in units}
        ohat = {u: o[u] * r[u] for u in units}
        sil = {u: gz[u] * sg[u] for u in units}
        for i, h in units:
            u = (i, h)
            dgz_ref[subs[i], lanes[h]] = (dog[u] * (ohat[u] * gg[:, lanes[h]])
                                          * (sg[u] * (1.0 + gz[u] * (1.0 - sg[u])))).astype(dgz_ref.dtype)
            dgg_ref[:, lanes[h]] += jnp.sum(dog[u] * sil[u] * ohat[u], axis=0, keepdims=True)
        dn = {(i, h): dog[(i, h)] * sil[(i, h)] * gg[:, lanes[h]] for i, h in units}
        mdn = {u: _lane_mean(dn[u] * ohat[u], ones_b) for u in units}
        do = {u: (r[u] * (dn[u] - ohat[u] * mdn[u])).astype(BF) for u in units}

        p = {u: jnp.where(tril, sc[u], 0.0).astype(BF) for u in units}
        dpr = {u: _dot(do[u], vh[u], NT) for u in units}
        incr = {u: [_dot(do[u][rs], a[u][rs], TN) for rs in chunks] for u in units}
        dv = {u: _dot(p[u], do[u], TN) for u in units}
        dp = {u: jnp.where(tril, dpr[u], 0.0).astype(BF) for u in units}
        dqd = {u: _dot(dp[u], bm[u]) for u in units}
        dkd = {u: _dot(dp[u], a[u], TN) for u in units}
        st = {(i, h): [sprev_ref[i * nch + c, h] for c in range(nch)] for i, h in units}
        leaving = {}
        for h in range(GLA_HEADS):
            d = dst_ref[h]
            for i in reversed(range(nsub)):
                out = [None] * nch
                for c in reversed(range(nch)):
                    out[c] = d
                    d = d * decs[i][c][:, blks[h]] + incr[(i, h)][c]
                leaving[(i, h)] = out
            dst_ref[h] = d
        lv_b = {u: [leaving[u][c].astype(BF) for c in range(nch)] for u in units}
        dv_s = {u: [_dot(ktl[u][rs], lv_b[u][c], NT) for c, rs in enumerate(chunks)] for u in units}
        dqd_s = {u: [_dot(do[u][rs], st[u][c].astype(BF)) for c, rs in enumerate(chunks)] for u in units}
        dkt_s = {u: [_dot(vh[u][rs], lv_b[u][c]) for c, rs in enumerate(chunks)] for u in units}
        ddec = {u: [jnp.sum(leaving[u][c] * st[u][c], axis=0, keepdims=True) for c in range(nch)] for u in units}
        for i, h in units:
            dv_ref[subs[i], lanes[h]] = (dv[(i, h)] + jnp.concatenate(dv_s[(i, h)], axis=0)).astype(dv_ref.dtype)
        dqd = {u: dqd[u] + jnp.concatenate(dqd_s[u], axis=0) for u in units}
        dkt = {u: jnp.concatenate(dkt_s[u], axis=0) for u in units}

        db = []
        for i, sl in enumerate(subs):
            _, eq, ek, f, qd, kd, kt, _ = com[i]
            parts = []
            for pair in range(GLA_HEADS // 2):
                blk, u0, u1 = blks[2 * pair], (i, 2 * pair), (i, 2 * pair + 1)
                dqd_b, dkd_b, dkt_b = dqd[u0] + dqd[u1], dkd[u0] + dkd[u1], dkt[u0] + dkt[u1]
                dqk_ref[sl, blk] = (dqd_b * (eq[:, blk] * GLA_DK ** -0.5)).astype(dqk_ref.dtype)
                dqk_ref[sl, 256 + LANES * pair:256 + LANES * (pair + 1)] = (dkd_b * ek[:, blk] + dkt_b * f[:, blk]).astype(dqk_ref.dtype)
                dkt_kt = dkt_b * kt[:, blk]
                dbp = dqd_b * qd[:, blk] - dkd_b * kd[:, blk] - dkt_kt
                dbl = [jnp.sum(dkt_kt[rs], axis=0, keepdims=True) + (ddec[u0][c] + ddec[u1][c]) * decs[i][c][:, blk]
                       for c, rs in enumerate(chunks)]
                parts.append(jnp.where(last_row, dbp + _per_chunk_rows(dbl, LANES), dbp))
            db.append(jnp.concatenate(parts, axis=1))
        dla = [_tri_matmul(triu_b, db[i]) for i in range(nsub)]
        dz32 = [dla[i] * (1.0 / GLA_TAU) * _sigmoid(-com[i][0]) for i in range(nsub)]
        dz = [t.astype(BF) for t in dz32]
        for i, sl in enumerate(subs):
            dga_ref[sl, :] = _dot(dz[i], wd, NT).astype(dga_ref.dtype)
            dwd_ref[...] += _dot(ga[i].astype(BF), dz[i], TN)
            dbd_ref[...] += jnp.sum(dz32[i], axis=0, keepdims=True)

    def col(width, off):
        return pl.BlockSpec((tb, width), lambda i: (nb - 1 - i, off // width))

    def rev(width):
        return pl.BlockSpec((tb, width), lambda i: (nb - 1 - i, 0))

    return pl.pallas_call(
        body, name="gla_bwd", grid=(nb,),
        in_specs=[col(512, OFF_QK), col(512, OFF_V), col(512, OFF_GZ), col(LANES, OFF_GA), rev(512), rev(512),
                  pl.BlockSpec((tb // GLA_CHUNK, GLA_HEADS, LANES, LANES), lambda i: (nb - 1 - i, 0, 0, 0)),
                  _full((LANES, 256)), _full((1, 256)), _full((1, 512)), _full((sub, sub)), _full((sub, sub))],
        out_specs=[rev(512), rev(512), rev(512), rev(LANES), _full((LANES, 256)), _full((1, 256)), _full((1, 512))],
        out_shape=[jax.ShapeDtypeStruct((s, 512), BF), jax.ShapeDtypeStruct((s, 512), BF),
                   jax.ShapeDtypeStruct((s, 512), BF), jax.ShapeDtypeStruct((s, LANES), BF),
                   jax.ShapeDtypeStruct((LANES, 256), F32), jax.ShapeDtypeStruct((1, 256), F32),
                   jax.ShapeDtypeStruct((1, 512), F32)],
        scratch_shapes=[pltpu.VMEM((GLA_HEADS, LANES, LANES), F32)],
        compiler_params=_params(("arbitrary",)),
    )(proj, proj, proj, proj, dog, opre, sprev, wdecp, bdec, ggla, _gla_triangle(sub), _gla_triangle(sub).T)


_SWA_COL_HEADS = (0, 2, 1, 3, 4, 6, 5, 7)
_SWA_COLS = SWA_HEADS * SWA_BLOCK


def _swa_masks():
    lo2 = lax.broadcasted_iota(jnp.int32, (2 * SWA_BLOCK, LANES), 1) < 64
    lane1 = lax.broadcasted_iota(jnp.int32, (SWA_BLOCK, LANES), 1)
    first_half = (lane1 % 64) < 32
    key = lax.broadcasted_iota(jnp.int32, (SWA_BLOCK, _SWA_COLS), 0)
    query = lax.broadcasted_iota(jnp.int32, (SWA_BLOCK, _SWA_COLS), 1) % SWA_BLOCK
    return lo2, lane1 < 64, first_half, key > query


def _merge_band(t, prev_mask, prev_bias=None):
    prev = t[:SWA_BLOCK] if prev_bias is None else t[:SWA_BLOCK] + prev_bias
    return jnp.where(prev_mask, prev, t[SWA_BLOCK:])


def _split_band(t, prev_mask_b):
    prev = t * prev_mask_b
    return jnp.concatenate([prev, t - prev], axis=0)


def _kv_variants(t, lo2):
    tr = pltpu.roll(t, 64, 1)
    lo_v = [jnp.where(lo2, t, 0.0).astype(BF), jnp.where(lo2, tr, 0.0).astype(BF)]
    hi_v = [jnp.where(lo2, 0.0, tr).astype(BF), jnp.where(lo2, 0.0, t).astype(BF)]
    return lo_v, hi_v


def _kv_variants_t(t):
    tt = t.T
    sw = jnp.concatenate([tt[64:], tt[:64]], axis=0)
    top = lax.broadcasted_iota(jnp.int32, tt.shape, 0) < 64
    lo_v = [jnp.where(top, tt, 0.0).astype(BF), jnp.where(top, sw, 0.0).astype(BF)]
    hi_v = [jnp.where(top, 0.0, sw).astype(BF), jnp.where(top, 0.0, tt).astype(BF)]
    return lo_v, hi_v


def _swa_scores(qg, k_lo, k_hi):
    return jnp.concatenate([_dot(k_lo[0], qg[0], NT), _dot(k_hi[0], qg[0], NT),
                            _dot(k_lo[1], qg[1], NT), _dot(k_hi[1], qg[1], NT)], axis=1)


def _sink_row(sinks_ref):
    return jnp.concatenate([jnp.full((1, SWA_BLOCK), sinks_ref[0, hd], F32) for hd in _SWA_COL_HEADS], axis=1)


def _swa_softmax(st, prev_mask, prev_bias, sink):
    st = _merge_band(st, prev_mask, prev_bias)
    m = jnp.maximum(jnp.max(st, axis=0, keepdims=True), sink)
    ex = jnp.exp(st - m)
    es = jnp.exp(sink - m)
    inv = 1.0 / (jnp.sum(ex, axis=0, keepdims=True) + es)
    return ex, es, inv


def _no_prev_bias(block_index):
    return jnp.where(block_index > 0, 0.0, -1e30).astype(F32)


def _swa_queries(sq_ref, rows, cosb, sinb, first_half):
    qs = [_rope(sq_ref[rows, p * LANES:(p + 1) * LANES], cosb, sinb, first_half) * 0.125 for p in range(4)]
    return [jnp.concatenate(qs[0:2], axis=0), jnp.concatenate(qs[2:4], axis=0)]


def _swa_fwd(proj, cos, sin, sinks):
    s = proj.shape[0]
    nq = min(SWA_QBLOCKS_FWD, s // SWA_BLOCK)
    tq = nq * SWA_BLOCK

    def body(sq_ref, sz_ref, sk_ref, sv_ref, cos_ref, sin_ref, sinks_ref, os_ref, opre_ref, kprev, vprev):
        n = pl.program_id(0)

        @pl.when(n == 0)
        def _():
            kprev[...] = jnp.zeros_like(kprev)
            vprev[...] = jnp.zeros_like(vprev)

        lo2, _, first_half, prev_mask = _swa_masks()
        prev_mask_b = jnp.where(prev_mask, 1.0, 0.0).astype(BF)
        sink = _sink_row(sinks_ref)
        blocks = range(nq)
        rows = [slice(j * SWA_BLOCK, (j + 1) * SWA_BLOCK) for j in blocks]
        cosb = [cos_ref[rows[j], :] for j in blocks]
        sinb = [sin_ref[rows[j], :] for j in blocks]
        kc = [_rope(sk_ref[rows[j], :], cosb[j], sinb[j], first_half) for j in blocks]
        vc = [sv_ref[rows[j], :] for j in blocks]
        kcat = [jnp.concatenate([kprev[...] if j == 0 else kc[j - 1], kc[j]], axis=0) for j in blocks]
        vcat = [jnp.concatenate([vprev[...] if j == 0 else vc[j - 1], vc[j]], axis=0) for j in blocks]
        kprev[...] = kc[-1]
        vprev[...] = vc[-1]
        kvar = [_kv_variants(kcat[j], lo2) for j in blocks]
        vtvar = [_kv_variants_t(vcat[j]) for j in blocks]
        qg = [[q.astype(BF) for q in _swa_queries(sq_ref, rows[j], cosb[j], sinb[j], first_half)] for j in blocks]
        st = [_swa_scores(qg[j], *kvar[j]) for j in blocks]
        soft = [_swa_softmax(st[j], prev_mask, _no_prev_bias(n) if j == 0 else None, sink) for j in blocks]
        pt = [_split_band(soft[j][0].astype(BF), prev_mask_b) for j in blocks]
        og = {}
        for j in blocks:
            inv = soft[j][2]
            for g in range(2):
                c0, c1, c2 = 512 * g, 512 * g + 256, 512 * g + 512
                ot = (_dot(vtvar[j][0][g], pt[j][:, c0:c1]) * inv[:, c0:c1]
                      + _dot(vtvar[j][1][g], pt[j][:, c1:c2]) * inv[:, c1:c2])
                og[(j, g)] = ot.T
        for j in blocks:
            for g in range(2):
                for i in range(2):
                    ls = slice((2 * g + i) * LANES, (2 * g + i + 1) * LANES)
                    o = og[(j, g)][i * SWA_BLOCK:(i + 1) * SWA_BLOCK]
                    sz = sz_ref[rows[j], ls]
                    opre_ref[rows[j], ls] = o
                    os_ref[rows[j], ls] = (o * (sz * _sigmoid(sz))).astype(os_ref.dtype)

    def col(width, off):
        return pl.BlockSpec((tq, width), lambda i: (i, off // width))

    row = pl.BlockSpec((tq, LANES), lambda i: (i, 0))
    return pl.pallas_call(
        body, name="swa_fwd", grid=(s // tq,),
        in_specs=[col(512, OFF_SQ), col(512, OFF_SZ), col(LANES, OFF_SK), col(LANES, OFF_SV), row, row,
                  pl.BlockSpec(memory_space=pltpu.SMEM)],
        out_specs=[pl.BlockSpec((tq, 512), lambda i: (i, 0))] * 2,
        out_shape=[jax.ShapeDtypeStruct((s, 512), BF), jax.ShapeDtypeStruct((s, 512), F32)],
        scratch_shapes=[pltpu.VMEM((SWA_BLOCK, LANES), F32)] * 2,
        compiler_params=_params(("arbitrary",)),
    )(proj, proj, proj, proj, cos, sin, sinks)


def _swa_bwd(proj, dos, opre, cos, sin, sinks):
    s = proj.shape[0]
    nq = min(SWA_QBLOCKS, s // SWA_BLOCK)
    tq = nq * SWA_BLOCK

    def body(sq_ref, sz_ref, sk_ref, sv_ref, dos_ref, opre_ref, cos_ref, sin_ref, sinks_ref,
             dsq_ref, dsz_ref, dsk_ref, dsv_ref, dsink_ref, kprev, vprev, cprev, sprev):
        n = pl.program_id(0)

        @pl.when(n == 0)
        def _():
            kprev[...] = jnp.zeros_like(kprev)
            vprev[...] = jnp.zeros_like(vprev)
            cprev[...] = jnp.zeros_like(cprev)
            sprev[...] = jnp.zeros_like(sprev)
            for hd in range(SWA_HEADS):
                dsink_ref[0, hd] = 0.0

        lo2, lo1, first_half, prev_mask = _swa_masks()
        prev_mask_b = jnp.where(prev_mask, 1.0, 0.0).astype(BF)
        lo1s = jnp.concatenate([lo1, lo1], axis=0)
        sink = _sink_row(sinks_ref)

        def home(m0, m1):
            t0 = m0 + pltpu.roll(m0, 64, 1)
            t1 = m1 + pltpu.roll(m1, 64, 1)
            return jnp.where(lo2, t0, t1)

        kp, vp, cp_, sp_ = kprev[...], vprev[...], cprev[...], sprev[...]
        for j in range(nq):
            rows = slice(j * SWA_BLOCK, (j + 1) * SWA_BLOCK)
            blk = n * nq + j
            cosb, sinb = cos_ref[rows, :], sin_ref[rows, :]
            kc = _rope(sk_ref[rows, :], cosb, sinb, first_half)
            vc = sv_ref[rows, :]
            kcat = jnp.concatenate([kp, kc], axis=0)
            k_lo, k_hi = _kv_variants(kcat, lo2)
            kt_lo, kt_hi = _kv_variants_t(kcat)
            v_lo, v_hi = _kv_variants(jnp.concatenate([vp, vc], axis=0), lo2)
            qg32 = _swa_queries(sq_ref, rows, cosb, sinb, first_half)
            qg = [q.astype(BF) for q in qg32]
            ex, es, inv = _swa_softmax(_swa_scores(qg, k_lo, k_hi), prev_mask, _no_prev_bias(n) if j == 0 else None, sink)
            pr, ps = ex * inv, es * inv

            dog32 = []
            for g in range(2):
                parts = []
                for i in range(2):
                    ls = slice((2 * g + i) * LANES, (2 * g + i + 1) * LANES)
                    sz = sz_ref[rows, ls]
                    sg = _sigmoid(sz)
                    dos_p = dos_ref[rows, ls]
                    dsz_ref[rows, ls] = (dos_p * opre_ref[rows, ls] * (sg * (1.0 + sz * (1.0 - sg)))).astype(dsz_ref.dtype)
                    parts.append(dos_p * (sz * sg))
                dog32.append(jnp.concatenate(parts, axis=0))
            dog = [t.astype(BF) for t in dog32]
            dpr = _merge_band(jnp.concatenate([_dot(v_lo[0], dog[0], NT), _dot(v_hi[0], dog[0], NT),
                                               _dot(v_lo[1], dog[1], NT), _dot(v_hi[1], dog[1], NT)], axis=1), prev_mask)
            rd = jnp.sum(pr * dpr, axis=0, keepdims=True)
            ds = _split_band((pr * (dpr - rd)).astype(BF), prev_mask_b)
            prb = _split_band(pr.astype(BF), prev_mask_b)
            sink_term = ps * rd
            for r, hd in enumerate(_SWA_COL_HEADS):
                dsink_ref[0, hd] += -jnp.sum(sink_term[:, r * SWA_BLOCK:(r + 1) * SWA_BLOCK])

            dk_g, dv_g = [], []
            for g in range(2):
                c0, c1, c2 = 512 * g, 512 * g + 256, 512 * g + 512
                dq = (_dot(kt_lo[g], ds[:, c0:c1]) + _dot(kt_hi[g], ds[:, c1:c2])).T
                for i in range(2):
                    ls = slice((2 * g + i) * LANES, (2 * g + i + 1) * LANES)
                    dsq_ref[rows, ls] = _rope_t(dq[i * SWA_BLOCK:(i + 1) * SWA_BLOCK] * 0.125, cosb, sinb,
                                                first_half).astype(dsq_ref.dtype)
                q_split = jnp.concatenate([jnp.where(lo1s, qg32[g], 0.0), jnp.where(lo1s, 0.0, qg32[g])], axis=0).astype(BF)
                do_split = jnp.concatenate([jnp.where(lo1s, dog32[g], 0.0), jnp.where(lo1s, 0.0, dog32[g])], axis=0).astype(BF)
                dk_g.append(_dot(ds[:, c0:c2], q_split))
                dv_g.append(_dot(prb[:, c0:c2], do_split))
            dk = home(dk_g[0], dk_g[1])
            dv = home(dv_g[0], dv_g[1])
            cur = pl.ds(pl.multiple_of(blk * SWA_BLOCK, SWA_BLOCK), SWA_BLOCK)
            dsk_ref[cur, :] = _rope_t(dk[SWA_BLOCK:], cosb, sinb, first_half)
            dsv_ref[cur, :] = dv[SWA_BLOCK:]
            dk_prev = _rope_t(dk[:SWA_BLOCK], cp_, sp_, first_half)
            dv_prev = dv[:SWA_BLOCK]
            if j == 0:
                @pl.when(n > 0)
                def _():
                    prv = pl.ds(pl.multiple_of((blk - 1) * SWA_BLOCK, SWA_BLOCK), SWA_BLOCK)
                    dsk_ref[prv, :] += dk_prev
                    dsv_ref[prv, :] += dv_prev
            else:
                prv = pl.ds(pl.multiple_of((blk - 1) * SWA_BLOCK, SWA_BLOCK), SWA_BLOCK)
                dsk_ref[prv, :] += dk_prev
                dsv_ref[prv, :] += dv_prev
            kp, vp, cp_, sp_ = kc, vc, cosb, sinb
        kprev[...] = kp
        vprev[...] = vp
        cprev[...] = cp_
        sprev[...] = sp_

    def col(width, off):
        return pl.BlockSpec((tq, width), lambda i: (i, off // width))

    row = pl.BlockSpec((tq, LANES), lambda i: (i, 0))
    wide = pl.BlockSpec((tq, 512), lambda i: (i, 0))
    return pl.pallas_call(
        body, name="swa_bwd", grid=(s // tq,),
        in_specs=[col(512, OFF_SQ), col(512, OFF_SZ), col(LANES, OFF_SK), col(LANES, OFF_SV), wide, wide, row, row,
                  pl.BlockSpec(memory_space=pltpu.SMEM)],
        out_specs=[wide, wide, _full((s, LANES)), _full((s, LANES)), pl.BlockSpec(memory_space=pltpu.SMEM)],
        out_shape=[jax.ShapeDtypeStruct((s, 512), BF), jax.ShapeDtypeStruct((s, 512), BF),
                   jax.ShapeDtypeStruct((s, LANES), F32), jax.ShapeDtypeStruct((s, LANES), F32),
                   jax.ShapeDtypeStruct((1, SWA_HEADS), F32)],
        scratch_shapes=[pltpu.VMEM((SWA_BLOCK, LANES), F32)] * 4,
        compiler_params=_params(("arbitrary",)),
    )(proj, proj, proj, proj, dos, opre, cos, sin, sinks)


def _outproj(og, osw, w_out, x2d, target, gate, g_final):
    s = x2d.shape[0]
    tm = min(512, s)

    def body(og_ref, os_ref, w_ref, x_ref, t_ref, gate_ref, gf_ref,
             dx2_ref, dog_ref, dos_ref, dw_ref, loss_ref, dgf_ref, dgate_ref):
        @pl.when(pl.program_id(0) == 0)
        def _():
            dw_ref[...] = jnp.zeros_like(dw_ref)
            loss_ref[...] = jnp.zeros_like(loss_ref)
            dgf_ref[...] = jnp.zeros_like(dgf_ref)
            dgate_ref[...] = jnp.zeros_like(dgate_ref)

        w = w_ref[...]
        gate, gf = gate_ref[...], gf_ref[...]
        subs = _subtiles(tm)
        ogv = [og_ref[sl, :] for sl in subs]
        osv = [os_ref[sl, :] for sl in subs]
        y = [_dot(ogv[k], w[:512]) + _dot(osv[k], w[512:]) for k in range(len(subs))]
        dys = []
        for k, sl in enumerate(subs):
            x2 = x_ref[sl, :] + gate * y[k]
            r = lax.rsqrt(jnp.mean(x2 * x2, axis=-1, keepdims=True) + RMS_EPS)
            xn = x2 * r
            err = xn * gf - t_ref[sl, :]
            loss_ref[...] += 0.5 * jnp.sum(jnp.mean(err * err, axis=-1, keepdims=True), axis=0, keepdims=True)
            dyf = err * (1.0 / D_MODEL)
            dgf_ref[...] += jnp.sum(dyf * xn, axis=0, keepdims=True)
            t = dyf * gf
            dx2 = r * (t - xn * jnp.mean(t * xn, axis=-1, keepdims=True))
            dx2_ref[sl, :] = dx2
            dgate_ref[...] += jnp.sum(dx2 * y[k], axis=0, keepdims=True)
            dys.append((dx2 * gate).astype(BF))
            dmix = _dot(dys[k], w, NT)
            dog_ref[sl, :] = dmix[:, :512]
            dos_ref[sl, :] = dmix[:, 512:]
        dy = jnp.concatenate(dys, axis=0)
        dw_ref[:512, :] += _dot(og_ref[...], dy, TN)
        dw_ref[512:, :] += _dot(os_ref[...], dy, TN)

    half = pl.BlockSpec((tm, 512), lambda i: (i, 0))
    rowb = pl.BlockSpec((tm, D_MODEL), lambda i: (i, 0))
    vec = _full((1, D_MODEL))
    return pl.pallas_call(
        body, name="outproj", grid=(s // tm,),
        in_specs=[half, half, _full((D_MODEL, D_MODEL)), rowb, rowb, vec, vec],
        out_specs=[rowb, half, half, _full((D_MODEL, D_MODEL)), _full((1, 1)), vec, vec],
        out_shape=[jax.ShapeDtypeStruct((s, D_MODEL), F32), jax.ShapeDtypeStruct((s, 512), F32),
                   jax.ShapeDtypeStruct((s, 512), F32), jax.ShapeDtypeStruct((D_MODEL, D_MODEL), F32),
                   jax.ShapeDtypeStruct((1, 1), F32), jax.ShapeDtypeStruct((1, D_MODEL), F32),
                   jax.ShapeDtypeStruct((1, D_MODEL), F32)],
        compiler_params=_params(("arbitrary",)),
    )(og, osw, w_out, x2d, target, gate, g_final)


_PIECES = ((OFF_QK, 512), (OFF_V, 512), (OFF_GZ, 512), (OFF_SQ, 512), (OFF_SZ, 512),
           (OFF_SK, LANES), (OFF_SV, LANES), (OFF_GA, LANES))

_UNPAD_ROWS = ((OFF_QK, 0, 1024),
               (OFF_GA, 1024, GLA_RANK),
               (OFF_GZ, 1040, 1024),
               (OFF_SK, 2064, 256),
               (OFF_SZ, 2320, 512))


def _inproj_bwd(x2d, shift, sc1p, g_norm, w_t, dx2, pieces):
    s = x2d.shape[0]
    tm = min(512, s)
    nsteps = s // tm

    def body(x_ref, sh_ref, sc_ref, g_ref, w_hbm, dx2_ref, *rest):
        piece_refs = rest[:len(_PIECES)]
        gx_ref, dw_hbm, dsh_ref, dsc_ref, dg_ref, w_vm, dw_vm, in_sems, out_sems = rest[len(_PIECES):]
        i = pl.program_id(0)

        @pl.when(i == 0)
        def _():
            loads = _load_w_padded(w_hbm, w_vm, in_sems)
            dw_vm[...] = jnp.zeros_like(dw_vm)
            dsh_ref[...] = jnp.zeros_like(dsh_ref)
            dsc_ref[...] = jnp.zeros_like(dsc_ref)
            dg_ref[...] = jnp.zeros_like(dg_ref)
            for cp in loads:
                cp.wait()

        g, sc1p_v, shift_v = g_ref[...], sc_ref[...], sh_ref[...]
        subs = _subtiles(tm)
        dhs = []
        for sl in subs:
            dh = None
            for (off, width), pr in zip(_PIECES, piece_refs):
                part = _dot(pr[sl, :].astype(BF), w_vm[off:off + width, :])
                dh = part if dh is None else dh + part
            dhs.append(dh)
        norm = [_modnorm(x_ref[sl, :], g, sc1p_v, shift_v) for sl in subs]
        hb = jnp.concatenate([h.astype(BF) for _, _, h in norm], axis=0)
        for (off, width), pr in zip(_PIECES, piece_refs):
            dw_vm[off:off + width, :] += _dot(pr[...].astype(BF), hb, TN)
        for sl, (xn, r, _), dh in zip(subs, norm, dhs):
            dsh_ref[...] += jnp.sum(dh, axis=0, keepdims=True)
            dsc_ref[...] += jnp.sum(dh * (xn * g), axis=0, keepdims=True)
            dg_ref[...] += jnp.sum(dh * xn * sc1p_v, axis=0, keepdims=True)
            dxn = dh * g * sc1p_v
            gx_ref[sl, :] = dx2_ref[sl, :] + r * (dxn - xn * jnp.mean(dxn * xn, axis=-1, keepdims=True))

        @pl.when(i == nsteps - 1)
        def _():
            copies = [pltpu.make_async_copy(dw_vm.at[src:src + n], dw_hbm.at[dst:dst + n], out_sems.at[k])
                      for k, (src, dst, n) in enumerate(_UNPAD_ROWS)]
            for cp in copies:
                cp.start()
            for cp in copies:
                cp.wait()

    rowb = pl.BlockSpec((tm, D_MODEL), lambda i: (i, 0))
    vec = _full((1, D_MODEL))
    anyspec = pl.BlockSpec(memory_space=pl.ANY)
    piece_specs = [pl.BlockSpec((tm, width), lambda i: (i, 0)) for _, width in _PIECES]
    return pl.pallas_call(
        body, name="inproj_bwd", grid=(nsteps,),
        in_specs=[rowb, vec, vec, vec, anyspec, rowb] + piece_specs,
        out_specs=[rowb, anyspec, vec, vec, vec],
        out_shape=[jax.ShapeDtypeStruct((s, D_MODEL), F32), jax.ShapeDtypeStruct((D_IN, D_MODEL), F32),
                   jax.ShapeDtypeStruct((1, D_MODEL), F32), jax.ShapeDtypeStruct((1, D_MODEL), F32),
                   jax.ShapeDtypeStruct((1, D_MODEL), F32)],
        scratch_shapes=[pltpu.VMEM((D_PAD, D_MODEL), BF), pltpu.VMEM((D_PAD, D_MODEL), F32),
                        pltpu.SemaphoreType.DMA((len(_UNPAD_ROWS),)), pltpu.SemaphoreType.DMA((len(_UNPAD_ROWS),))],
        compiler_params=_params(("arbitrary",)),
    )(x2d, shift, sc1p, g_norm, w_t, dx2, *pieces)


def _adam(w, g, m, v):
    m2 = ADAM_B1 * m + (1.0 - ADAM_B1) * g
    v2 = ADAM_B2 * v + (1.0 - ADAM_B2) * (g * g)
    m_hat = m2 / (1.0 - ADAM_B1 ** ADAM_STEP)
    v_hat = v2 / (1.0 - ADAM_B2 ** ADAM_STEP)
    delta = -ADAM_LR * (m_hat / (jnp.sqrt(v_hat) + ADAM_EPS) + ADAM_WD * w)
    return delta, m2, v2


def _adamw(w, g, m, v, name):
    rr, cc = w.shape
    tc = min(512, cc)

    def body(w_ref, g_ref, m_ref, v_ref, d_ref, m2_ref, v2_ref):
        d_ref[...], m2_ref[...], v2_ref[...] = _adam(w_ref[...], g_ref[...], m_ref[...], v_ref[...])

    blk = pl.BlockSpec((rr, tc), lambda i: (0, i))
    return pl.pallas_call(
        body, name=name, grid=(cc // tc,), in_specs=[blk] * 4, out_specs=[blk] * 3,
        out_shape=[jax.ShapeDtypeStruct((rr, cc), F32)] * 3,
        compiler_params=_params(("arbitrary",)),
    )(w, g, m, v)


def _adamw_t(w3, g, m3, v3, name):
    rr, _, cc = w3.shape
    tc = cc

    def body(w_hbm, g_ref, m_hbm, v_hbm, d_hbm, m2_hbm, v2_hbm, g3_hbm, w_vm, m_vm, v_vm, d_vm, m2_vm, v2_vm, in_sems, out_sems):
        cols = pl.ds(pl.multiple_of(pl.program_id(0) * tc, tc), tc)
        loads = [pltpu.make_async_copy(src.at[:, 0, cols], dst, in_sems.at[k])
                 for k, (src, dst) in enumerate(((w_hbm, w_vm), (m_hbm, m_vm), (v_hbm, v_vm)))]
        for cp in loads:
            cp.start()
        for cp in loads:
            cp.wait()
        d_vm[...], m2_vm[...], v2_vm[...] = _adam(w_vm[...], g_ref[...], m_vm[...], v_vm[...])
        stores = [pltpu.make_async_copy(src, dst.at[:, 0, cols], out_sems.at[k])
                  for k, (src, dst) in enumerate(((d_vm, d_hbm), (m2_vm, m2_hbm), (v2_vm, v2_hbm), (g_ref, g3_hbm)))]
        for cp in stores:
            cp.start()
        for cp in stores:
            cp.wait()

    hbm = pl.BlockSpec(memory_space=pl.ANY)
    return pl.pallas_call(
        body, name=name, grid=(cc // tc,), in_specs=[hbm, pl.BlockSpec((rr, tc), lambda i: (0, i)), hbm, hbm],
        out_specs=[hbm] * 4, out_shape=[jax.ShapeDtypeStruct((rr, 1, cc), F32)] * 4,
        scratch_shapes=[pltpu.VMEM((rr, tc), F32)] * 6 + [pltpu.SemaphoreType.DMA((3,)), pltpu.SemaphoreType.DMA((4,))],
        compiler_params=_params(("arbitrary",)),
    )(w3, g, m3, v3)


def _ada_update(c_all, dmod_cols, w, m, v):
    rr, cc = w.shape
    tr = min(512, rr)
    c_all = jnp.pad(c_all, ((0, 8), (0, 0)))
    dmod_cols = jnp.pad(dmod_cols, ((0, 8), (0, 0)))

    def body(c_ref, dm_ref, w_ref, m_ref, v_ref, g_ref, d_ref, m2_ref, v2_ref):
        cv = c_ref[...]
        sc = (cv * _sigmoid(cv)).astype(BF)
        g = _dot(sc, dm_ref[...].astype(BF), TN)
        g_ref[...] = g
        d_ref[...], m2_ref[...], v2_ref[...] = _adam(w_ref[...], g, m_ref[...], v_ref[...])

    blk = pl.BlockSpec((tr, cc), lambda i: (i, 0))
    return pl.pallas_call(
        body, name="ada_update", grid=(rr // tr,),
        in_specs=[pl.BlockSpec((16, tr), lambda i: (0, i)), _full((16, cc)), blk, blk, blk],
        out_specs=[blk] * 4, out_shape=[jax.ShapeDtypeStruct((rr, cc), F32)] * 4,
        compiler_params=_params(("arbitrary",)),
    )(c_all, dmod_cols, w, m, v)


def _small_update(parts, weights, moms, vels):
    n = len(weights)

    def body(*refs):
        p_refs, w_refs, m_refs, v_refs = refs[:n + 1], refs[n + 1:2 * n + 1], refs[2 * n + 1:3 * n + 1], refs[3 * n + 1:4 * n + 1]
        outs = refs[4 * n + 1:]
        for i in range(n):
            g = p_refs[i][0]
            for d in range(1, 8):
                g = g + p_refs[i][d]
            delta, m2, v2 = _adam(w_refs[i][...], g, m_refs[i][...], v_refs[i][...])
            outs[4 * i][...] = g
            outs[4 * i + 1][...] = delta
            outs[4 * i + 2][...] = m2
            outs[4 * i + 3][...] = v2
        tot = p_refs[n][0]
        for d in range(1, 8):
            tot = tot + p_refs[n][d]
        outs[4 * n][...] = tot

    out_shape = []
    for w in weights:
        out_shape += [jax.ShapeDtypeStruct(w.shape, F32)] * 4
    out_shape.append(jax.ShapeDtypeStruct(parts[n].shape[1:], F32))
    return pl.pallas_call(body, name="small_update", out_shape=out_shape, compiler_params=_params())(
        *parts, *weights, *moms, *vels)


def _rows8(a):
    flat = a.reshape(-1)
    rows = -(-flat.shape[0] // LANES)
    rows8 = -(-rows // 8) * 8
    flat = jnp.pad(flat, (0, rows8 * LANES - flat.shape[0]))
    return flat.reshape(rows8, LANES)


def kernel(x, c, positions, w_ada, b_ada, g_norm, w_in, w_decay, b_decay, g_gla_head, sinks, w_out, g_final, loss_target, m_w_ada, m_b_ada, m_g_norm, m_w_in, m_w_decay, m_b_decay, m_g_gla_head, m_sinks, m_w_out, m_g_final, v_w_ada, v_b_ada, v_g_norm, v_w_in, v_w_decay, v_b_decay, v_g_gla_head, v_sinks, v_w_out, v_g_final):
    ax, ay, ac = lax.axis_index("x"), lax.axis_index("y"), lax.axis_index("c")
    chip = 2 * ax + ay
    dev = 2 * chip + ac
    s = x.shape[1]
    x2d = x[0]
    target = loss_target[0]
    w_ada2, w_out2, w_dec2 = w_ada[0], w_out[0], w_decay[0]
    w_in_t = w_in[0].T
    ada_cols = w_ada2.shape[1]
    in_cols = w_in_t.shape[0]
    out_rows = w_out2.shape[0]
    half = D_MODEL // 2

    cw = jnp.concatenate([c.reshape(8, LANES), w_dec2.reshape(8, LANES)], axis=0)
    b_shard = lax.dynamic_slice(b_ada, (0, chip * ada_cols), (1, ada_cols))
    half_in = lax.dynamic_slice(w_in_t, (0, ac * half), (in_cols, half)).astype(BF)
    half_out = lax.dynamic_slice(w_out2, (ac * (out_rows // 2), 0), (out_rows // 2, D_MODEL)).astype(BF)
    inv_freq = 1.0 / (ROPE_THETA ** (jnp.arange(0, 64, 2, dtype=F32) / 64))
    first, mod_all, w_in_all, w_out_all, cos, sin = _prologue(
        cw, w_ada2, b_shard, half_in, half_out, positions.reshape(s, 1), jnp.tile(inv_freq, 4).reshape(1, LANES))

    first = first.reshape(8, 2, 8, LANES)
    c_all = first[:, 0].reshape(8, D_MODEL)
    w_dec_full = first[0::2, 1].reshape(4, GLA_RANK, 64).transpose(1, 0, 2).reshape(GLA_RANK, 256)
    mod = mod_all.reshape(4, 2, 8, ada_cols)[:, 0]
    mod = lax.dynamic_slice(mod, (0, dev, 0), (4, 1, ada_cols)).reshape(1, 4 * ada_cols)
    shift, sc1p, gate = mod[:, :D_MODEL], 1.0 + mod[:, D_MODEL:2 * D_MODEL], mod[:, 2 * D_MODEL:]
    w_t = w_in_all.reshape(4 * in_cols, D_MODEL)
    w_out_all = w_out_all.reshape(D_MODEL, D_MODEL)

    wdecp = jnp.pad(w_dec_full, ((0, LANES - GLA_RANK), (0, 0))).astype(BF)

    proj = _inproj_fwd(x2d, shift, sc1p, g_norm, w_t)
    og, o_gla, sprev = _gla_fwd(proj, wdecp, b_decay, g_gla_head)
    osw, o_swa = _swa_fwd(proj, cos, sin, sinks)
    dx2, dog, dos, dw_out, loss_p, dgf, dgate = _outproj(og, osw, w_out_all, x2d, target, gate, g_final.reshape(1, D_MODEL))
    dsq, dsz, dsk, dsv, dsinks = _swa_bwd(proj, dos, o_swa, cos, sin, sinks)
    dqk, dv, dgz, dga, dwdp, dbd, dgg = _gla_bwd(proj, dog, o_gla, sprev, wdecp, b_decay, g_gla_head)
    pieces = (dqk, dv, dgz, dsq, dsz, dsk, dsv, dga)
    gx, dw_in_t, dshift, dscale, dgn = _inproj_bwd(x2d, shift, sc1p, g_norm, w_t, dx2, pieces)

    segs = [jnp.concatenate([dshift, dscale, dgate], axis=1), dgn, dgf, dwdp[:GLA_RANK], dbd, dgg, dsinks, loss_p]
    packed = [_rows8(a) for a in segs]
    offs = [0]
    for a in packed:
        offs.append(offs[-1] + a.shape[0])
    g_w_in_t, g_w_out, small = _epilogue(dw_in_t.reshape(4, in_cols, D_MODEL), dw_out.reshape(4, out_rows, D_MODEL),
                                         jnp.concatenate(packed, axis=0))

    def seg(i, size):
        return small[:, offs[i]:offs[i + 1]].reshape(8, -1)[:, :size]

    dmod_all = seg(0, 3 * D_MODEL)
    dwd_all = lax.dynamic_slice(seg(3, GLA_RANK * 256).reshape(8, GLA_RANK, 256), (0, 0, chip * 64), (8, GLA_RANK, 64))
    parts = [dmod_all.reshape(8, 1, 3 * D_MODEL), seg(1, D_MODEL).reshape(8, 1, D_MODEL), dwd_all,
             seg(4, 256).reshape(8, 1, 256), seg(5, 512).reshape(8, 1, 512), seg(6, SWA_HEADS).reshape(8, 1, SWA_HEADS),
             seg(2, D_MODEL).reshape(8, 1, D_MODEL), seg(7, LANES).reshape(8, 1, LANES)]
    smalls = _small_update(
        parts,
        [b_ada, g_norm, w_dec2, b_decay, g_gla_head, sinks, g_final.reshape(1, D_MODEL)],
        [m_b_ada, m_g_norm, m_w_decay[0], m_b_decay, m_g_gla_head, m_sinks, m_g_final.reshape(1, D_MODEL)],
        [v_b_ada, v_g_norm, v_w_decay[0], v_b_decay, v_g_gla_head, v_sinks, v_g_final.reshape(1, D_MODEL)])
    (g_b_ada, d_b_ada, nm_b_ada, nv_b_ada, g_gn, d_gn, nm_gn, nv_gn, g_wd, d_wd, nm_wd, nv_wd,
     g_bd, d_bd, nm_bd, nv_bd, g_gg, d_gg, nm_gg, nv_gg, g_sk, d_sk, nm_sk, nv_sk,
     g_gf, d_gf, nm_gf, nv_gf, loss_row) = smalls
    loss = loss_row[0, 0]

    dmod_cols = lax.dynamic_slice(dmod_all, (0, chip * ada_cols), (8, ada_cols))
    g_w_ada, d_w_ada, nm_w_ada, nv_w_ada = _ada_update(c_all, dmod_cols, w_ada2, m_w_ada[0], v_w_ada[0])
    to3 = lambda a: jnp.transpose(a, (2, 0, 1))
    from3 = lambda a: jnp.transpose(a, (1, 2, 0))[0]
    d3, nm3, nv3, g3 = _adamw_t(to3(w_in), g_w_in_t, to3(m_w_in), to3(v_w_in), "adamw_w_in")
    g_w_in, d_w_in, nm_w_in, nv_w_in = from3(g3), from3(d3), from3(nm3), from3(nv3)
    d_w_out, nm_w_out, nv_w_out = _adamw(w_out2, g_w_out, m_w_out[0], v_w_out[0], "adamw_w_out")

    flat = lambda a: a.reshape(D_MODEL)
    grads = [g_w_ada[None], g_b_ada, g_gn, g_w_in[None], g_wd[None], g_bd, g_gg, g_sk, g_w_out[None], flat(g_gf)]
    deltas = [d_w_ada[None], d_b_ada, d_gn, d_w_in[None], d_wd[None], d_bd, d_gg, d_sk, d_w_out[None], flat(d_gf)]
    new_m = [nm_w_ada[None], nm_b_ada, nm_gn, nm_w_in[None], nm_wd[None], nm_bd, nm_gg, nm_sk, nm_w_out[None], flat(nm_gf)]
    new_v = [nv_w_ada[None], nv_b_ada, nv_gn, nv_w_in[None], nv_wd[None], nv_bd, nv_gg, nv_sk, nv_w_out[None], flat(nv_gf)]
    return (loss, gx[None], *grads, *deltas, *new_m, *new_v)
```

```python
import jax
import jax.numpy as jnp
from jax import lax
from jax.experimental import pallas as pl
from jax.experimental.pallas import tpu as pltpu

F32 = jnp.float32
BF = jnp.bfloat16

D_MODEL = 1024
GLA_HEADS = 4
GLA_DK = 64
GLA_CHUNK = 64
GLA_RANK = 16
GLA_TAU = 16.0
GLA_SUB = 256
GLA_ROWS_FWD = 1024
GLA_ROWS_BWD = 512
SWA_HEADS = 8
SWA_BLOCK = 128
SWA_QBLOCKS_FWD = 8
SWA_QBLOCKS = 8
RMS_EPS = 1e-6
ROPE_THETA = 10000.0

OFF_QK, OFF_V, OFF_GZ, OFF_SQ, OFF_SZ, OFF_SK, OFF_SV, OFF_GA = 0, 512, 1024, 1536, 2048, 2560, 2688, 2816
D_PAD = 2944
D_IN = 2832
LANES = 128
VMEM_LIMIT = 56 * 1024 * 1024

ADAM_LR, ADAM_B1, ADAM_B2, ADAM_EPS, ADAM_WD, ADAM_STEP = 0.001, 0.9, 0.999, 1e-08, 0.01, 10

NT = (((1,), (1,)), ((), ()))
TN = (((0,), (0,)), ((), ()))
MESH = pl.DeviceIdType.MESH


def _dot(a, b, dims=None):
    if dims is None:
        return jnp.dot(a, b, preferred_element_type=F32)
    return lax.dot_general(a, b, dims, preferred_element_type=F32)


def _sigmoid(x):
    return 1.0 / (1.0 + jnp.exp(-x))


def _params(sem=None):
    return pltpu.CompilerParams(dimension_semantics=sem, vmem_limit_bytes=VMEM_LIMIT)


def _full(shape):
    return pl.BlockSpec(shape, lambda i: (0,) * len(shape))


def _subtiles(rows, size=256):
    size = min(size, rows)
    return [slice(k * size, (k + 1) * size) for k in range(rows // size)]


WEIGHT_CHUNKS = 4


def _gather_sems(chunks=1):
    return [pltpu.SemaphoreType.DMA((7 * chunks,)), pltpu.SemaphoreType.DMA((7 * chunks,)), pltpu.SemaphoreType.DMA]


_GATHER_SEMS = _gather_sems()


class _Gather:
    def __init__(self, x_ref, out_ref, send_sems, recv_sems, local_sem, slab=None, chunks=1):
        self.slab_of = slab
        self.chunks = chunks
        self.width = x_ref.shape[-1] // chunks
        x, y, c = lax.axis_index("x"), lax.axis_index("y"), lax.axis_index("c")
        self.me, self.sibling, self.c = (x, y, c), (x, y, 1 - c), c
        self.xn, self.yn, self.dg = (1 - x, y), (x, 1 - y), (1 - x, 1 - y)
        self.pass_from = (lax.rem(x + 1 - c, 2), lax.rem(y + c, 2))
        self.pass_to = (lax.rem(x + c, 2), lax.rem(y + 1 - c, 2))
        self.x_ref, self.out_ref, self.send_sems, self.recv_sems = x_ref, out_ref, send_sems, recv_sems
        self.mine = pltpu.make_async_copy(x_ref, self._slab(*self.me), local_sem)

    def _slab(self, px, py, pc):
        if self.slab_of is not None:
            return self.slab_of(self.out_ref, px, py, pc)
        return self.out_ref.at[4 * px + 2 * py + pc]

    def _part(self, ref, q):
        if self.chunks == 1:
            return ref
        lanes = slice(q * self.width, (q + 1) * self.width)
        return ref.at[(slice(None),) * (len(ref.shape) - 1) + (lanes,)]

    def _copy(self, k, q, blk, to, src=None):
        i = k * self.chunks + q
        return pltpu.make_async_remote_copy(
            src_ref=self._part(self._slab(*blk) if src is None else src, q), dst_ref=self._part(self._slab(*blk), q),
            send_sem=self.send_sems.at[i], recv_sem=self.recv_sems.at[i], device_id=to, device_id_type=MESH)

    def _sends(self, q):
        c = self.c
        return [self._copy(0, q, self.me, self.sibling, src=self.x_ref),
                self._copy(1, q, self.me, (*self.xn, c), src=self.x_ref),
                self._copy(2, q, self.me, (*self.yn, c), src=self.x_ref),
                self._copy(3, q, (*self.pass_from, c), (*self.pass_to, c)),
                self._copy(4, q, (*self.xn, c), self.sibling),
                self._copy(5, q, (*self.yn, c), self.sibling),
                self._copy(6, q, (*self.dg, c), self.sibling)]

    def start(self):
        self.mine.start()
        for q in range(self.chunks):
            sends = self._sends(q)
            for k in (1, 2, 0):
                sends[k].start()

    def pass_on(self, only=None):
        for q in range(self.chunks) if only is None else (only,):
            sends = self._sends(q)
            self._copy(1, q, (*self.xn, self.c), self.me).wait_recv()
            self._copy(2, q, (*self.yn, self.c), self.me).wait_recv()
            for k in (3, 4, 5):
                sends[k].start()

    def relay_diagonal(self, only=None):
        for q in range(self.chunks) if only is None else (only,):
            self._copy(3, q, (*self.dg, self.c), self.me).wait_recv()
            self._sends(q)[6].start()

    def relay(self):
        self.pass_on()
        self.relay_diagonal()

    def finish(self):
        c = self.c
        for q in range(self.chunks):
            self._copy(0, q, self.sibling, self.me).wait_recv()
            for k, chip in ((4, self.xn), (5, self.yn), (6, self.dg)):
                self._copy(k, q, (*chip, 1 - c), self.me).wait_recv()
            for cp in self._sends(q):
                cp.wait_send()
        self.mine.wait()


def _prologue(cw, w_ada, b_shard, half_in, pos_col, inv_freq):
    s = pos_col.shape[0]
    rt = min(512, s)

    def body(cw_ref, wada_hbm, b_ref, hin_ref, pos_hbm, f_ref,
             first_ref, mod_ref, win_ref, cos_hbm, sin_hbm,
             mod_blk, cos_ref, sin_ref, wada_ref, pos_ref, table_sems, local_sems, *sems):
        fetch_w = pltpu.make_async_copy(wada_hbm, wada_ref, local_sems.at[0])
        fetch_p = pltpu.make_async_copy(pos_hbm, pos_ref, local_sems.at[1])
        fetch_w.start()
        fetch_p.start()
        g_c = _Gather(cw_ref, first_ref, *sems[0:3])
        half_lanes = hin_ref.shape[1]
        g_in = _Gather(hin_ref, win_ref, *sems[3:6], chunks=WEIGHT_CHUNKS,
                       slab=lambda ref, px, py, pc: ref.at[2 * px + py, :, pl.ds(pl.multiple_of(pc * half_lanes, half_lanes), half_lanes)])
        g_mod = _Gather(mod_blk, mod_ref, *sems[6:9])
        g_c.start()
        g_in.start()
        g_c.relay()
        g_c.finish()
        c_rows = [jnp.concatenate([first_ref[d, r:r + 1, :] for r in range(8)], axis=1) for d in range(8)]
        c_all = jnp.concatenate(c_rows, axis=0)
        sc = (c_all * _sigmoid(c_all)).astype(BF)
        fetch_w.wait()
        mod_blk[...] = _dot(sc, wada_ref[...].astype(BF)) + b_ref[...]
        g_mod.start()
        fetch_p.wait()

        def rope_rows(i, carry):
            rows = pl.ds(pl.multiple_of(i * rt, rt), rt)
            ang = pos_ref[rows, :].astype(F32) * f_ref[...]
            lane = lax.broadcasted_iota(jnp.int32, ang.shape, 1)
            cos_ref[rows, :] = jnp.cos(ang)
            sn = jnp.sin(ang)
            sin_ref[rows, :] = jnp.where((lane % 64) < 32, -sn, sn)
            pltpu.make_async_copy(cos_ref.at[rows, :], cos_hbm.at[rows, :], table_sems.at[0]).start()
            pltpu.make_async_copy(sin_ref.at[rows, :], sin_hbm.at[rows, :], table_sems.at[1]).start()
            return carry

        waits = ([lambda q=q: g_in.pass_on(q) for q in range(WEIGHT_CHUNKS)]
                 + [lambda q=q: g_in.relay_diagonal(q) for q in range(WEIGHT_CHUNKS)] + [g_mod.relay])
        steps = s // rt
        lead = steps // 4
        per_wait = max((steps - lead) // len(waits), 1)
        lax.fori_loop(0, lead, rope_rows, 0)
        done = lead
        for wait in waits:
            wait()
            nxt = min(done + per_wait, steps)
            lax.fori_loop(done, nxt, rope_rows, 0)
            done = nxt
        lax.fori_loop(done, steps, rope_rows, 0)
        g_in.finish()
        g_mod.finish()
        pltpu.make_async_copy(cos_ref, cos_hbm, table_sems.at[0]).wait()
        pltpu.make_async_copy(sin_ref, sin_hbm, table_sems.at[1]).wait()

    vm = pl.BlockSpec(memory_space=pltpu.VMEM)
    hbm = pl.BlockSpec(memory_space=pl.ANY)
    return pl.pallas_call(
        body, name="prologue",
        out_shape=[jax.ShapeDtypeStruct((8,) + cw.shape, F32), jax.ShapeDtypeStruct((8, 8, w_ada.shape[1]), F32),
                   jax.ShapeDtypeStruct((4, half_in.shape[0], 2 * half_in.shape[1]), half_in.dtype),
                   jax.ShapeDtypeStruct((s, LANES), F32), jax.ShapeDtypeStruct((s, LANES), F32)],
        in_specs=[vm, hbm, vm, hbm, hbm, vm], out_specs=[vm, vm, hbm, hbm, hbm],
        scratch_shapes=[pltpu.VMEM((8, w_ada.shape[1]), F32), pltpu.VMEM((s, LANES), F32), pltpu.VMEM((s, LANES), F32),
                        pltpu.VMEM(w_ada.shape, F32), pltpu.VMEM(pos_col.shape, jnp.int32),
                        pltpu.SemaphoreType.DMA((2,)), pltpu.SemaphoreType.DMA((2,))]
        + _GATHER_SEMS + _gather_sems(WEIGHT_CHUNKS) + _GATHER_SEMS,
        compiler_params=pltpu.CompilerParams(vmem_limit_bytes=VMEM_LIMIT),
    )(cw, w_ada, b_shard, half_in, pos_col, inv_freq)


def _reduce_scratch(rr, cc):
    c2 = cc // 2
    return [pltpu.VMEM((4, rr, c2), F32), pltpu.VMEM((4, rr, c2), F32), pltpu.VMEM((3, rr, c2), BF),
            pltpu.VMEM((2, rr, c2), BF), pltpu.VMEM((rr, c2), BF), pltpu.VMEM((rr, c2), F32),
            pltpu.SemaphoreType.DMA((8 + 3 * WEIGHT_CHUNKS,)), pltpu.SemaphoreType.DMA((8 + 3 * WEIGHT_CHUNKS,)),
            pltpu.SemaphoreType.DMA((5,))]


class _Reduce:
    def __init__(self, p_hbm, out_ref, acc_ref, own_ref, send_ref, land_ref, relay_ref, res_ref,
                 send_sems, recv_sems, local_sems):
        x, y, c = lax.axis_index("x"), lax.axis_index("y"), lax.axis_index("c")
        c2 = out_ref.shape[1] // 2
        sibling = (x, y, 1 - c)
        first = (lax.rem(x + 1 - c, 2), lax.rem(y + c, 2))
        second = (lax.rem(x + c, 2), lax.rem(y + 1 - c, 2))
        shards = [2 * first[0] + first[1], 2 * second[0] + second[1], 2 * (1 - x) + (1 - y), 2 * x + y]
        sibling_slot = (1, 0, 2, 3)
        mine = pl.ds(pl.multiple_of(c * c2, c2), c2)
        other = pl.ds(pl.multiple_of((1 - c) * c2, c2), c2)
        self.acc_ref, self.own_ref, self.send_ref, self.land_ref = acc_ref, own_ref, send_ref, land_ref
        self.relay_ref, self.res_ref = relay_ref, res_ref
        self.own = [pltpu.make_async_copy(p_hbm.at[j, :, mine], own_ref.at[k], local_sems.at[k])
                    for k, j in enumerate(shards)]
        self.swap_out = [pltpu.make_async_remote_copy(
            src_ref=p_hbm.at[j, :, other], dst_ref=acc_ref.at[sibling_slot[k]], send_sem=send_sems.at[k],
            recv_sem=recv_sems.at[sibling_slot[k]], device_id=sibling, device_id_type=MESH) for k, j in enumerate(shards)]
        self.swap_in = [pltpu.make_async_remote_copy(
            src_ref=p_hbm.at[j, :, other], dst_ref=acc_ref.at[k], send_sem=send_sems.at[k], recv_sem=recv_sems.at[k],
            device_id=sibling, device_id_type=MESH) for k, j in enumerate(shards)]

        self.lanes = [slice(q * (c2 // WEIGHT_CHUNKS), (q + 1) * (c2 // WEIGHT_CHUNKS)) for q in range(WEIGHT_CHUNKS)]

        def message(m, src, dst, to):
            return [pltpu.make_async_remote_copy(
                src_ref=src.at[:, ln], dst_ref=dst.at[:, ln], send_sem=send_sems.at[8 + m * WEIGHT_CHUNKS + q],
                recv_sem=recv_sems.at[8 + m * WEIGHT_CHUNKS + q], device_id=(*to, c), device_id_type=MESH)
                for q, ln in enumerate(self.lanes)]

        self.direct = message(0, send_ref.at[0], land_ref.at[0], first)
        self.passed = message(1, send_ref.at[1], relay_ref, first)
        self.joint = message(2, send_ref.at[2], land_ref.at[1], second)
        self.put = pltpu.make_async_copy(res_ref, out_ref.at[:, mine], local_sems.at[4])
        self.share = pltpu.make_async_remote_copy(
            src_ref=res_ref, dst_ref=out_ref.at[:, mine], send_sem=send_sems.at[7],
            recv_sem=recv_sems.at[7], device_id=sibling, device_id_type=MESH)

    def start(self):
        for k in (2, 0, 1, 3):
            self.own[k].start()
            self.swap_out[k].start()

    def _combine(self, k):
        self.own[k].wait()
        self.swap_out[k].wait_send()
        self.swap_in[k].wait_recv()
        self.acc_ref[k] = self.acc_ref[k] + self.own_ref[k]

    def combine_and_send(self):
        dt = self.send_ref.dtype
        self._combine(2)
        self.send_ref[1] = self.acc_ref[2].astype(dt)
        for cp in self.passed:
            cp.start()
        self._combine(0)
        self.send_ref[0] = self.acc_ref[0].astype(dt)
        for cp in self.direct:
            cp.start()
        self._combine(1)
        for q, ln in enumerate(self.lanes):
            self.passed[q].wait_recv()
            self.send_ref[2, :, ln] = (self.acc_ref[1, :, ln] + self.relay_ref[:, ln].astype(F32)).astype(dt)
            self.joint[q].start()
        self._combine(3)

    def total_and_share(self):
        for cp in self.direct + self.joint:
            cp.wait_recv()
        self.res_ref[...] = self.acc_ref[3] + self.land_ref[0].astype(F32) + self.land_ref[1].astype(F32)
        for cp in self.direct + self.passed + self.joint:
            cp.wait_send()
        self.put.start()
        self.share.start()

    def finish(self):
        self.put.wait()
        self.share.wait()


def _epilogue(dw_in_parts, small):
    _, r_in, cc = dw_in_parts.shape
    n_red = len(_reduce_scratch(r_in, cc))

    def body(pin_hbm, small_ref, gin_ref, small_all_ref, *scratch):
        red_in = _Reduce(pin_hbm, gin_ref, *scratch[0:n_red])
        gat = _Gather(small_ref, small_all_ref, *scratch[n_red:])
        red_in.start()
        gat.start()
        gat.relay()
        red_in.combine_and_send()
        gat.finish()
        red_in.total_and_share()
        red_in.finish()

    vm = pl.BlockSpec(memory_space=pltpu.VMEM)
    anyspec = pl.BlockSpec(memory_space=pl.ANY)
    return pl.pallas_call(
        body, name="epilogue",
        out_shape=[jax.ShapeDtypeStruct((r_in, cc), F32), jax.ShapeDtypeStruct((8,) + small.shape, F32)],
        in_specs=[anyspec, vm], out_specs=[anyspec, vm],
        scratch_shapes=_reduce_scratch(r_in, cc) + _GATHER_SEMS,
        compiler_params=pltpu.CompilerParams(vmem_limit_bytes=VMEM_LIMIT),
    )(dw_in_parts, small)


def _rope(t, cosb, sinb, first_half):
    partner = jnp.where(first_half, pltpu.roll(t, 96, 1), pltpu.roll(t, 32, 1))
    return t * cosb + partner * sinb


def _rope_t(g, cosb, sinb, first_half):
    gs = g * sinb
    partner = jnp.where(first_half, pltpu.roll(gs, 96, 1), pltpu.roll(gs, 32, 1))
    return g * cosb + partner


def _modnorm(x, g, sc1p, shift):
    r = lax.rsqrt(jnp.mean(x * x, axis=-1, keepdims=True) + RMS_EPS)
    xn = x * r
    return xn, r, (xn * g) * sc1p + shift


def _load_w_padded(w_hbm, w_vm, sems):
    copies = [pltpu.make_async_copy(w_hbm.at[ref:ref + n], w_vm.at[pad:pad + n], sems.at[k])
              for k, (pad, ref, n) in enumerate(_UNPAD_ROWS)]
    for cp in copies:
        cp.start()
    w_vm[OFF_GA + GLA_RANK:, :] = jnp.zeros((D_PAD - OFF_GA - GLA_RANK, D_MODEL), w_vm.dtype)
    return copies


def _inproj_fwd(x2d, shift, sc1p, g_norm, w_t):
    s = x2d.shape[0]
    tm = min(1024, s)

    def body(x_ref, sh_ref, sc_ref, g_ref, w_hbm, o_ref, w_vm, sems):
        @pl.when(pl.program_id(0) == 0)
        def _():
            for cp in _load_w_padded(w_hbm, w_vm, sems):
                cp.wait()

        subs = _subtiles(tm)
        hs = [_modnorm(x_ref[sl, :], g_ref[...], sc_ref[...], sh_ref[...])[2].astype(BF) for sl in subs]
        for sl, h in zip(subs, hs):
            o_ref[sl, :] = _dot(h, w_vm[...], NT)

    vec = _full((1, D_MODEL))
    return pl.pallas_call(
        body, name="inproj_fwd", grid=(s // tm,),
        in_specs=[pl.BlockSpec((tm, D_MODEL), lambda i: (i, 0)), vec, vec, vec, pl.BlockSpec(memory_space=pl.ANY)],
        out_specs=pl.BlockSpec((tm, D_PAD), lambda i: (i, 0)),
        out_shape=jax.ShapeDtypeStruct((s, D_PAD), F32),
        scratch_shapes=[pltpu.VMEM((D_PAD, D_MODEL), BF), pltpu.SemaphoreType.DMA((len(_UNPAD_ROWS),))],
        compiler_params=_params(("arbitrary",)),
    )(x2d, shift, sc1p, g_norm, w_t)


def _split3(a):
    hi = a.astype(BF)
    r1 = a - hi.astype(F32)
    mid = r1.astype(BF)
    lo = (r1 - mid.astype(F32)).astype(BF)
    return hi, mid, lo


def _tri_matmul(tri, a):
    hi, mid, lo = _split3(a)
    return _dot(tri, hi) + _dot(tri, mid) + _dot(tri, lo)


def _chunks(tb):
    return [slice(c * GLA_CHUNK, (c + 1) * GLA_CHUNK) for c in range(tb // GLA_CHUNK)]


def _per_chunk_rows(rows, width):
    return jnp.concatenate([jnp.broadcast_to(r, (GLA_CHUNK, width)) for r in rows], axis=0)


def _gla_triangle(tb):
    row = lax.broadcasted_iota(jnp.int32, (tb, tb), 0)
    col = lax.broadcasted_iota(jnp.int32, (tb, tb), 1)
    return (((row // GLA_CHUNK) == (col // GLA_CHUNK)) & (col <= row)).astype(F32)


def _lane_mean(x, ones_b):
    hi = x.astype(BF)
    lo = (x - hi.astype(F32)).astype(BF)
    return (_dot(hi, ones_b) + _dot(lo, ones_b)) * (1.0 / LANES)


def _head(t, h, lo_h):
    blk = t[:, LANES * (h // 2):LANES * (h // 2 + 1)]
    return jnp.where(lo_h, blk, 0.0) if h % 2 == 0 else jnp.where(lo_h, 0.0, blk)


def _gla_block_common(qk, ga, wd, bd, tril_b):
    tb = qk.shape[0]
    q, k = qk[:, :256], qk[:, 256:]
    z = _dot(ga.astype(BF), wd) + bd
    la = (jnp.minimum(z, 0.0) - jnp.log(1.0 + jnp.exp(-jnp.abs(z)))) * (1.0 / GLA_TAU)
    b = _tri_matmul(tril_b, la)
    bls = [b[rs.stop - 1:rs.stop, :] for rs in _chunks(tb)]
    eq = jnp.exp(b)
    ek = jnp.exp(-b)
    f = jnp.exp(_per_chunk_rows(bls, 256) - b)
    return z, eq, ek, f, q * (eq * GLA_DK ** -0.5), k * ek, k * f, bls


def _gla_units(s, rows):
    sub = min(GLA_SUB, s)
    tb = min(rows, s)
    subs = [slice(i * sub, (i + 1) * sub) for i in range(tb // sub)]
    units = [(i, h) for i in range(len(subs)) for h in range(GLA_HEADS)]
    return tb, sub, subs, units


def _gla_fwd(proj, wdecp, bdec, ggla):
    s = proj.shape[0]
    tb, sub, subs, units = _gla_units(s, GLA_ROWS_FWD)
    nch = sub // GLA_CHUNK

    def body(qk_ref, v_ref, gz_ref, ga_ref, wd_ref, bd_ref, gg_ref, tri_ref, og_ref, opre_ref, sprev_ref, st_ref):
        @pl.when(pl.program_id(0) == 0)
        def _():
            st_ref[...] = jnp.zeros_like(st_ref)

        lo_h = lax.broadcasted_iota(jnp.int32, (sub, LANES), 1) < GLA_DK
        tril = tri_ref[...] > 0.5
        tril_b = tri_ref[...].astype(BF)
        ones_b = jnp.ones((LANES, LANES), BF)
        gg, wd, bd = gg_ref[...], wd_ref[...], bd_ref[...]
        chunks = _chunks(sub)
        lanes = [slice(h * LANES, (h + 1) * LANES) for h in range(GLA_HEADS)]
        com = [_gla_block_common(qk_ref[sl, :], ga_ref[sl, :], wd, bd, tril_b) for sl in subs]
        decs = [[jnp.exp(bl) for bl in cm[7]] for cm in com]
        a = {(i, h): _head(com[i][4], h, lo_h).astype(BF) for i, h in units}
        bm = {(i, h): _head(com[i][5], h, lo_h).astype(BF) for i, h in units}
        ktl = {(i, h): _head(com[i][6], h, lo_h).astype(BF) for i, h in units}
        vh = {(i, h): v_ref[subs[i], lanes[h]].astype(BF) for i, h in units}
        sc = {u: _dot(a[u], bm[u], NT) for u in units}
        upd = {u: [_dot(vh[u][rs], ktl[u][rs], TN) for rs in chunks] for u in units}
        p = {u: jnp.where(tril, sc[u], 0.0).astype(BF) for u in units}
        o = {u: _dot(p[u], vh[u]) for u in units}
        states = {}
        for h in range(GLA_HEADS):
            st = st_ref[h]
            for i in range(len(subs)):
                entering = []
                for c in range(nch):
                    entering.append(st)
                    sprev_ref[i * nch + c, h] = st
                    st = st * decs[i][c][:, LANES * (h // 2):LANES * (h // 2 + 1)] + upd[(i, h)][c]
                states[(i, h)] = entering
            st_ref[h] = st
        inter = {u: [_dot(a[u][rs], states[u][c].astype(BF), NT) for c, rs in enumerate(chunks)] for u in units}
        o = {u: o[u] + jnp.concatenate(inter[u], axis=0) for u in units}
        ms = {u: _lane_mean(o[u] * o[u], ones_b) for u in units}
        for i, h in units:
            gzh = gz_ref[subs[i], lanes[h]]
            opre_ref[subs[i], lanes[h]] = o[(i, h)]
            og_ref[subs[i], lanes[h]] = (((o[(i, h)] * lax.rsqrt(ms[(i, h)] + RMS_EPS)) * gg[:, lanes[h]])
                                         * (gzh * _sigmoid(gzh))).astype(og_ref.dtype)

    def col(width, off):
        return pl.BlockSpec((tb, width), lambda i: (i, off // width))

    return pl.pallas_call(
        body, name="gla_fwd", grid=(s // tb,),
        in_specs=[col(512, OFF_QK), col(512, OFF_V), col(512, OFF_GZ), col(LANES, OFF_GA),
                  _full((LANES, 256)), _full((1, 256)), _full((1, 512)), _full((sub, sub))],
        out_specs=[pl.BlockSpec((tb, 512), lambda i: (i, 0)), pl.BlockSpec((tb, 512), lambda i: (i, 0)),
                   pl.BlockSpec((tb // GLA_CHUNK, GLA_HEADS, LANES, LANES), lambda i: (i, 0, 0, 0))],
        out_shape=[jax.ShapeDtypeStruct((s, 512), BF), jax.ShapeDtypeStruct((s, 512), F32),
                   jax.ShapeDtypeStruct((s // GLA_CHUNK, GLA_HEADS, LANES, LANES), F32)],
        scratch_shapes=[pltpu.VMEM((GLA_HEADS, LANES, LANES), F32)],
        compiler_params=_params(("arbitrary",)),
    )(proj, proj, proj, proj, wdecp, bdec, ggla, _gla_triangle(sub))


def _gla_bwd(proj, dog, opre, sprev, wdecp, bdec, ggla):
    s = proj.shape[0]
    tb, sub, subs, units = _gla_units(s, GLA_ROWS_BWD)
    nsub = len(subs)
    nch = sub // GLA_CHUNK
    nb = s // tb

    def body(qk_ref, v_ref, gz_ref, ga_ref, dog_ref, opre_ref, sprev_ref, wd_ref, bd_ref, gg_ref, tri_ref, triu_ref,
             dqk_ref, dv_ref, dgz_ref, dga_ref, dwd_ref, dbd_ref, dgg_ref, dst_ref):
        @pl.when(pl.program_id(0) == 0)
        def _():
            dst_ref[...] = jnp.zeros_like(dst_ref)
            dwd_ref[...] = jnp.zeros_like(dwd_ref)
            dbd_ref[...] = jnp.zeros_like(dbd_ref)
            dgg_ref[...] = jnp.zeros_like(dgg_ref)

        lo_h = lax.broadcasted_iota(jnp.int32, (sub, LANES), 1) < GLA_DK
        tril = tri_ref[...] > 0.5
        tril_b = tri_ref[...].astype(BF)
        triu_b = triu_ref[...].astype(BF)
        ones_b = jnp.ones((LANES, LANES), BF)
        last_row = (lax.broadcasted_iota(jnp.int32, (sub, LANES), 0) % GLA_CHUNK) == GLA_CHUNK - 1
        wd, gg, bd = wd_ref[...], gg_ref[...], bd_ref[...]
        chunks = _chunks(sub)
        lanes = [slice(h * LANES, (h + 1) * LANES) for h in range(GLA_HEADS)]
        blks = [slice(LANES * (h // 2), LANES * (h // 2 + 1)) for h in range(GLA_HEADS)]
        ga = [ga_ref[sl, :] for sl in subs]
        com = [_gla_block_common(qk_ref[sl, :], ga[i], wd, bd, tril_b) for i, sl in enumerate(subs)]
        decs = [[jnp.exp(bl) for bl in cm[7]] for cm in com]
        a = {(i, h): _head(com[i][4], h, lo_h).astype(BF) for i, h in units}
        bm = {(i, h): _head(com[i][5], h, lo_h).astype(BF) for i, h in units}
        ktl = {(i, h): _head(com[i][6], h, lo_h).astype(BF) for i, h in units}
        vh = {(i, h): v_ref[subs[i], lanes[h]].astype(BF) for i, h in units}
        sc = {u: _dot(a[u], bm[u], NT) for u in units}

        o = {(i, h): opre_ref[subs[i], lanes[h]] for i, h in units}
        ms = {u: _lane_mean(o[u] * o[u], ones_b) for u in units}
        gz = {(i, h): gz_ref[subs[i], lanes[h]] for i, h in units}
        dog = {(i, h): dog_ref[subs[i], lanes[h]] for i, h in units}
        sg = {u: _sigmoid(gz[u]) for u in units}
        r = {u: lax.rsqrt(ms[u] + RMS_EPS) for u in units}
        ohat = {u: o[u] * r[u] for u in units}
        sil = {u: gz[u] * sg[u] for u in units}
        for i, h in units:
            u = (i, h)
            dgz_ref[subs[i], lanes[h]] = (dog[u] * (ohat[u] * gg[:, lanes[h]])
                                          * (sg[u] * (1.0 + gz[u] * (1.0 - sg[u])))).astype(dgz_ref.dtype)
            dgg_ref[:, lanes[h]] += jnp.sum(dog[u] * sil[u] * ohat[u], axis=0, keepdims=True)
        dn = {(i, h): dog[(i, h)] * sil[(i, h)] * gg[:, lanes[h]] for i, h in units}
        mdn = {u: _lane_mean(dn[u] * ohat[u], ones_b) for u in units}
        do = {u: (r[u] * (dn[u] - ohat[u] * mdn[u])).astype(BF) for u in units}

        p = {u: jnp.where(tril, sc[u], 0.0).astype(BF) for u in units}
        dpr = {u: _dot(do[u], vh[u], NT) for u in units}
        incr = {u: [_dot(do[u][rs], a[u][rs], TN) for rs in chunks] for u in units}
        dv = {u: _dot(p[u], do[u], TN) for u in units}
        dp = {u: jnp.where(tril, dpr[u], 0.0).astype(BF) for u in units}
        dqd = {u: _dot(dp[u], bm[u]) for u in units}
        dkd = {u: _dot(dp[u], a[u], TN) for u in units}
        st = {(i, h): [sprev_ref[i * nch + c, h] for c in range(nch)] for i, h in units}
        leaving = {}
        for h in range(GLA_HEADS):
            d = dst_ref[h]
            for i in reversed(range(nsub)):
                out = [None] * nch
                for c in reversed(range(nch)):
                    out[c] = d
                    d = d * decs[i][c][:, blks[h]] + incr[(i, h)][c]
                leaving[(i, h)] = out
            dst_ref[h] = d
        lv_b = {u: [leaving[u][c].astype(BF) for c in range(nch)] for u in units}
        dv_s = {u: [_dot(ktl[u][rs], lv_b[u][c], NT) for c, rs in enumerate(chunks)] for u in units}
        dqd_s = {u: [_dot(do[u][rs], st[u][c].astype(BF)) for c, rs in enumerate(chunks)] for u in units}
        dkt_s = {u: [_dot(vh[u][rs], lv_b[u][c]) for c, rs in enumerate(chunks)] for u in units}
        ddec = {u: [jnp.sum(leaving[u][c] * st[u][c], axis=0, keepdims=True) for c in range(nch)] for u in units}
        for i, h in units:
            dv_ref[subs[i], lanes[h]] = (dv[(i, h)] + jnp.concatenate(dv_s[(i, h)], axis=0)).astype(dv_ref.dtype)
        dqd = {u: dqd[u] + jnp.concatenate(dqd_s[u], axis=0) for u in units}
        dkt = {u: jnp.concatenate(dkt_s[u], axis=0) for u in units}

        db = []
        for i, sl in enumerate(subs):
            _, eq, ek, f, qd, kd, kt, _ = com[i]
            parts = []
            for pair in range(GLA_HEADS // 2):
                blk, u0, u1 = blks[2 * pair], (i, 2 * pair), (i, 2 * pair + 1)
                dqd_b, dkd_b, dkt_b = dqd[u0] + dqd[u1], dkd[u0] + dkd[u1], dkt[u0] + dkt[u1]
                dqk_ref[sl, blk] = (dqd_b * (eq[:, blk] * GLA_DK ** -0.5)).astype(dqk_ref.dtype)
                dqk_ref[sl, 256 + LANES * pair:256 + LANES * (pair + 1)] = (dkd_b * ek[:, blk] + dkt_b * f[:, blk]).astype(dqk_ref.dtype)
                dkt_kt = dkt_b * kt[:, blk]
                dbp = dqd_b * qd[:, blk] - dkd_b * kd[:, blk] - dkt_kt
                dbl = [jnp.sum(dkt_kt[rs], axis=0, keepdims=True) + (ddec[u0][c] + ddec[u1][c]) * decs[i][c][:, blk]
                       for c, rs in enumerate(chunks)]
                parts.append(jnp.where(last_row, dbp + _per_chunk_rows(dbl, LANES), dbp))
            db.append(jnp.concatenate(parts, axis=1))
        dla = [_tri_matmul(triu_b, db[i]) for i in range(nsub)]
        dz32 = [dla[i] * (1.0 / GLA_TAU) * _sigmoid(-com[i][0]) for i in range(nsub)]
        dz = [t.astype(BF) for t in dz32]
        for i, sl in enumerate(subs):
            dga_ref[sl, :] = _dot(dz[i], wd, NT).astype(dga_ref.dtype)
            dwd_ref[...] += _dot(ga[i].astype(BF), dz[i], TN)
            dbd_ref[...] += jnp.sum(dz32[i], axis=0, keepdims=True)

    def col(width, off):
        return pl.BlockSpec((tb, width), lambda i: (nb - 1 - i, off // width))

    def rev(width):
        return pl.BlockSpec((tb, width), lambda i: (nb - 1 - i, 0))

    return pl.pallas_call(
        body, name="gla_bwd", grid=(nb,),
        in_specs=[col(512, OFF_QK), col(512, OFF_V), col(512, OFF_GZ), col(LANES, OFF_GA), rev(512), rev(512),
                  pl.BlockSpec((tb // GLA_CHUNK, GLA_HEADS, LANES, LANES), lambda i: (nb - 1 - i, 0, 0, 0)),
                  _full((LANES, 256)), _full((1, 256)), _full((1, 512)), _full((sub, sub)), _full((sub, sub))],
        out_specs=[rev(512), rev(512), rev(512), rev(LANES), _full((LANES, 256)), _full((1, 256)), _full((1, 512))],
        out_shape=[jax.ShapeDtypeStruct((s, 512), BF), jax.ShapeDtypeStruct((s, 512), BF),
                   jax.ShapeDtypeStruct((s, 512), BF), jax.ShapeDtypeStruct((s, LANES), BF),
                   jax.ShapeDtypeStruct((LANES, 256), F32), jax.ShapeDtypeStruct((1, 256), F32),
                   jax.ShapeDtypeStruct((1, 512), F32)],
        scratch_shapes=[pltpu.VMEM((GLA_HEADS, LANES, LANES), F32)],
        compiler_params=_params(("arbitrary",)),
    )(proj, proj, proj, proj, dog, opre, sprev, wdecp, bdec, ggla, _gla_triangle(sub), _gla_triangle(sub).T)


_SWA_COL_HEADS = (0, 2, 1, 3, 4, 6, 5, 7)
_SWA_COLS = SWA_HEADS * SWA_BLOCK


def _swa_masks():
    lo2 = lax.broadcasted_iota(jnp.int32, (2 * SWA_BLOCK, LANES), 1) < 64
    lane1 = lax.broadcasted_iota(jnp.int32, (SWA_BLOCK, LANES), 1)
    first_half = (lane1 % 64) < 32
    key = lax.broadcasted_iota(jnp.int32, (SWA_BLOCK, _SWA_COLS), 0)
    query = lax.broadcasted_iota(jnp.int32, (SWA_BLOCK, _SWA_COLS), 1) % SWA_BLOCK
    return lo2, lane1 < 64, first_half, key > query


def _merge_band(t, prev_mask, prev_bias=None):
    prev = t[:SWA_BLOCK] if prev_bias is None else t[:SWA_BLOCK] + prev_bias
    return jnp.where(prev_mask, prev, t[SWA_BLOCK:])


def _split_band(t, prev_mask_b):
    prev = t * prev_mask_b
    return jnp.concatenate([prev, t - prev], axis=0)


def _kv_variants(t, lo2):
    tr = pltpu.roll(t, 64, 1)
    lo_v = [jnp.where(lo2, t, 0.0).astype(BF), jnp.where(lo2, tr, 0.0).astype(BF)]
    hi_v = [jnp.where(lo2, 0.0, tr).astype(BF), jnp.where(lo2, 0.0, t).astype(BF)]
    return lo_v, hi_v


def _kv_variants_t(t):
    tt = t.T
    sw = jnp.concatenate([tt[64:], tt[:64]], axis=0)
    top = lax.broadcasted_iota(jnp.int32, tt.shape, 0) < 64
    lo_v = [jnp.where(top, tt, 0.0).astype(BF), jnp.where(top, sw, 0.0).astype(BF)]
    hi_v = [jnp.where(top, 0.0, sw).astype(BF), jnp.where(top, 0.0, tt).astype(BF)]
    return lo_v, hi_v


def _swa_scores(qg, k_lo, k_hi):
    return jnp.concatenate([_dot(k_lo[0], qg[0], NT), _dot(k_hi[0], qg[0], NT),
                            _dot(k_lo[1], qg[1], NT), _dot(k_hi[1], qg[1], NT)], axis=1)


def _sink_row(sinks_ref):
    return jnp.concatenate([jnp.full((1, SWA_BLOCK), sinks_ref[0, hd], F32) for hd in _SWA_COL_HEADS], axis=1)


def _swa_softmax(st, prev_mask, prev_bias, sink):
    st = _merge_band(st, prev_mask, prev_bias)
    m = jnp.maximum(jnp.max(st, axis=0, keepdims=True), sink)
    ex = jnp.exp(st - m)
    es = jnp.exp(sink - m)
    inv = 1.0 / (jnp.sum(ex, axis=0, keepdims=True) + es)
    return ex, es, inv


def _no_prev_bias(block_index):
    return jnp.where(block_index > 0, 0.0, -1e30).astype(F32)


def _swa_queries(sq_ref, rows, cosb, sinb, first_half):
    qs = [_rope(sq_ref[rows, p * LANES:(p + 1) * LANES], cosb, sinb, first_half) * 0.125 for p in range(4)]
    return [jnp.concatenate(qs[0:2], axis=0), jnp.concatenate(qs[2:4], axis=0)]


def _phase_steps(nsteps, phases):
    return [min(nsteps - 1, (k * nsteps) // phases) for k in range(phases - 1)] + [nsteps - 1]


def _swa_fwd(proj, cos, sin, sinks, half_out):
    s = proj.shape[0]
    nq = min(SWA_QBLOCKS_FWD, s // SWA_BLOCK)
    tq = nq * SWA_BLOCK
    steps = _phase_steps(s // tq, 4)

    def body(sq_ref, sz_ref, sk_ref, sv_ref, cos_ref, sin_ref, sinks_ref, hout_hbm, os_ref, opre_ref, wout_hbm,
             kprev, vprev, *gather_sems):
        n = pl.program_id(0)

        @pl.when(n == 0)
        def _():
            kprev[...] = jnp.zeros_like(kprev)
            vprev[...] = jnp.zeros_like(vprev)

        gather = _Gather(hout_hbm, wout_hbm, *gather_sems, chunks=WEIGHT_CHUNKS)
        for step, phase in zip(steps, (gather.start, gather.pass_on, gather.relay_diagonal, gather.finish)):
            pl.when(n == step)(phase)

        lo2, _, first_half, prev_mask = _swa_masks()
        prev_mask_b = jnp.where(prev_mask, 1.0, 0.0).astype(BF)
        sink = _sink_row(sinks_ref)
        blocks = range(nq)
        rows = [slice(j * SWA_BLOCK, (j + 1) * SWA_BLOCK) for j in blocks]
        cosb = [cos_ref[rows[j], :] for j in blocks]
        sinb = [sin_ref[rows[j], :] for j in blocks]
        kc = [_rope(sk_ref[rows[j], :], cosb[j], sinb[j], first_half) for j in blocks]
        vc = [sv_ref[rows[j], :] for j in blocks]
        kcat = [jnp.concatenate([kprev[...] if j == 0 else kc[j - 1], kc[j]], axis=0) for j in blocks]
        vcat = [jnp.concatenate([vprev[...] if j == 0 else vc[j - 1], vc[j]], axis=0) for j in blocks]
        kprev[...] = kc[-1]
        vprev[...] = vc[-1]
        kvar = [_kv_variants(kcat[j], lo2) for j in blocks]
        vtvar = [_kv_variants_t(vcat[j]) for j in blocks]
        qg = [[q.astype(BF) for q in _swa_queries(sq_ref, rows[j], cosb[j], sinb[j], first_half)] for j in blocks]
        st = [_swa_scores(qg[j], *kvar[j]) for j in blocks]
        soft = [_swa_softmax(st[j], prev_mask, _no_prev_bias(n) if j == 0 else None, sink) for j in blocks]
        pt = [_split_band(soft[j][0].astype(BF), prev_mask_b) for j in blocks]
        og = {}
        for j in blocks:
            inv = soft[j][2]
            for g in range(2):
                c0, c1, c2 = 512 * g, 512 * g + 256, 512 * g + 512
                ot = (_dot(vtvar[j][0][g], pt[j][:, c0:c1]) * inv[:, c0:c1]
                      + _dot(vtvar[j][1][g], pt[j][:, c1:c2]) * inv[:, c1:c2])
                og[(j, g)] = ot.T
        for j in blocks:
            for g in range(2):
                for i in range(2):
                    ls = slice((2 * g + i) * LANES, (2 * g + i + 1) * LANES)
                    o = og[(j, g)][i * SWA_BLOCK:(i + 1) * SWA_BLOCK]
                    sz = sz_ref[rows[j], ls]
                    opre_ref[rows[j], ls] = o
                    os_ref[rows[j], ls] = (o * (sz * _sigmoid(sz))).astype(os_ref.dtype)

    def col(width, off):
        return pl.BlockSpec((tq, width), lambda i: (i, off // width))

    row = pl.BlockSpec((tq, LANES), lambda i: (i, 0))
    return pl.pallas_call(
        body, name="swa_fwd", grid=(s // tq,),
        in_specs=[col(512, OFF_SQ), col(512, OFF_SZ), col(LANES, OFF_SK), col(LANES, OFF_SV), row, row,
                  pl.BlockSpec(memory_space=pltpu.SMEM), pl.BlockSpec(memory_space=pl.ANY)],
        out_specs=[pl.BlockSpec((tq, 512), lambda i: (i, 0))] * 2 + [pl.BlockSpec(memory_space=pl.ANY)],
        out_shape=[jax.ShapeDtypeStruct((s, 512), BF), jax.ShapeDtypeStruct((s, 512), F32),
                   jax.ShapeDtypeStruct((8,) + half_out.shape, half_out.dtype)],
        scratch_shapes=[pltpu.VMEM((SWA_BLOCK, LANES), F32)] * 2 + _gather_sems(WEIGHT_CHUNKS),
        compiler_params=_params(("arbitrary",)),
    )(proj, proj, proj, proj, cos, sin, sinks, half_out)


def _swa_bwd(proj, dos, opre, cos, sin, sinks, dw_out_parts):
    s = proj.shape[0]
    nq = min(SWA_QBLOCKS, s // SWA_BLOCK)
    tq = nq * SWA_BLOCK
    steps = _phase_steps(s // tq, 4)
    _, r_out, c_out = dw_out_parts.shape

    def body(sq_ref, sz_ref, sk_ref, sv_ref, dos_ref, opre_ref, cos_ref, sin_ref, sinks_ref, pout_hbm,
             dsq_ref, dsz_ref, dsk_ref, dsv_ref, dsink_ref, gout_hbm, kprev, vprev, cprev, sprev, *reduce_scratch):
        n = pl.program_id(0)

        @pl.when(n == 0)
        def _():
            kprev[...] = jnp.zeros_like(kprev)
            vprev[...] = jnp.zeros_like(vprev)
            cprev[...] = jnp.zeros_like(cprev)
            sprev[...] = jnp.zeros_like(sprev)
            for hd in range(SWA_HEADS):
                dsink_ref[0, hd] = 0.0

        reduce = _Reduce(pout_hbm, gout_hbm, *reduce_scratch)
        for step, phase in zip(steps, (reduce.start, reduce.combine_and_send, reduce.total_and_share, reduce.finish)):
            pl.when(n == step)(phase)

        lo2, lo1, first_half, prev_mask = _swa_masks()
        prev_mask_b = jnp.where(prev_mask, 1.0, 0.0).astype(BF)
        lo1s = jnp.concatenate([lo1, lo1], axis=0)
        sink = _sink_row(sinks_ref)

        def home(m0, m1):
            t0 = m0 + pltpu.roll(m0, 64, 1)
            t1 = m1 + pltpu.roll(m1, 64, 1)
            return jnp.where(lo2, t0, t1)

        kp, vp, cp_, sp_ = kprev[...], vprev[...], cprev[...], sprev[...]
        for j in range(nq):
            rows = slice(j * SWA_BLOCK, (j + 1) * SWA_BLOCK)
            blk = n * nq + j
            cosb, sinb = cos_ref[rows, :], sin_ref[rows, :]
            kc = _rope(sk_ref[rows, :], cosb, sinb, first_half)
            vc = sv_ref[rows, :]
            kcat = jnp.concatenate([kp, kc], axis=0)
            k_lo, k_hi = _kv_variants(kcat, lo2)
            kt_lo, kt_hi = _kv_variants_t(kcat)
            v_lo, v_hi = _kv_variants(jnp.concatenate([vp, vc], axis=0), lo2)
            qg32 = _swa_queries(sq_ref, rows, cosb, sinb, first_half)
            qg = [q.astype(BF) for q in qg32]
            ex, es, inv = _swa_softmax(_swa_scores(qg, k_lo, k_hi), prev_mask, _no_prev_bias(n) if j == 0 else None, sink)
            pr, ps = ex * inv, es * inv

            dog32 = []
            for g in range(2):
                parts = []
                for i in range(2):
                    ls = slice((2 * g + i) * LANES, (2 * g + i + 1) * LANES)
                    sz = sz_ref[rows, ls]
                    sg = _sigmoid(sz)
                    dos_p = dos_ref[rows, ls]
                    dsz_ref[rows, ls] = (dos_p * opre_ref[rows, ls] * (sg * (1.0 + sz * (1.0 - sg)))).astype(dsz_ref.dtype)
                    parts.append(dos_p * (sz * sg))
                dog32.append(jnp.concatenate(parts, axis=0))
            dog = [t.astype(BF) for t in dog32]
            dpr = _merge_band(jnp.concatenate([_dot(v_lo[0], dog[0], NT), _dot(v_hi[0], dog[0], NT),
                                               _dot(v_lo[1], dog[1], NT), _dot(v_hi[1], dog[1], NT)], axis=1), prev_mask)
            rd = jnp.sum(pr * dpr, axis=0, keepdims=True)
            ds = _split_band((pr * (dpr - rd)).astype(BF), prev_mask_b)
            prb = _split_band(pr.astype(BF), prev_mask_b)
            sink_term = ps * rd
            for r, hd in enumerate(_SWA_COL_HEADS):
                dsink_ref[0, hd] += -jnp.sum(sink_term[:, r * SWA_BLOCK:(r + 1) * SWA_BLOCK])

            dk_g, dv_g = [], []
            for g in range(2):
                c0, c1, c2 = 512 * g, 512 * g + 256, 512 * g + 512
                dq = (_dot(kt_lo[g], ds[:, c0:c1]) + _dot(kt_hi[g], ds[:, c1:c2])).T
                for i in range(2):
                    ls = slice((2 * g + i) * LANES, (2 * g + i + 1) * LANES)
                    dsq_ref[rows, ls] = _rope_t(dq[i * SWA_BLOCK:(i + 1) * SWA_BLOCK] * 0.125, cosb, sinb,
                                                first_half).astype(dsq_ref.dtype)
                q_split = jnp.concatenate([jnp.where(lo1s, qg32[g], 0.0), jnp.where(lo1s, 0.0, qg32[g])], axis=0).astype(BF)
                do_split = jnp.concatenate([jnp.where(lo1s, dog32[g], 0.0), jnp.where(lo1s, 0.0, dog32[g])], axis=0).astype(BF)
                dk_g.append(_dot(ds[:, c0:c2], q_split))
                dv_g.append(_dot(prb[:, c0:c2], do_split))
            dk = home(dk_g[0], dk_g[1])
            dv = home(dv_g[0], dv_g[1])
            cur = pl.ds(pl.multiple_of(blk * SWA_BLOCK, SWA_BLOCK), SWA_BLOCK)
            dsk_ref[cur, :] = _rope_t(dk[SWA_BLOCK:], cosb, sinb, first_half)
            dsv_ref[cur, :] = dv[SWA_BLOCK:]
            dk_prev = _rope_t(dk[:SWA_BLOCK], cp_, sp_, first_half)
            dv_prev = dv[:SWA_BLOCK]
            if j == 0:
                @pl.when(n > 0)
                def _():
                    prv = pl.ds(pl.multiple_of((blk - 1) * SWA_BLOCK, SWA_BLOCK), SWA_BLOCK)
                    dsk_ref[prv, :] += dk_prev
                    dsv_ref[prv, :] += dv_prev
            else:
                prv = pl.ds(pl.multiple_of((blk - 1) * SWA_BLOCK, SWA_BLOCK), SWA_BLOCK)
                dsk_ref[prv, :] += dk_prev
                dsv_ref[prv, :] += dv_prev
            kp, vp, cp_, sp_ = kc, vc, cosb, sinb
        kprev[...] = kp
        vprev[...] = vp
        cprev[...] = cp_
        sprev[...] = sp_

    def col(width, off):
        return pl.BlockSpec((tq, width), lambda i: (i, off // width))

    row = pl.BlockSpec((tq, LANES), lambda i: (i, 0))
    wide = pl.BlockSpec((tq, 512), lambda i: (i, 0))
    return pl.pallas_call(
        body, name="swa_bwd", grid=(s // tq,),
        in_specs=[col(512, OFF_SQ), col(512, OFF_SZ), col(LANES, OFF_SK), col(LANES, OFF_SV), wide, wide, row, row,
                  pl.BlockSpec(memory_space=pltpu.SMEM), pl.BlockSpec(memory_space=pl.ANY)],
        out_specs=[wide, wide, _full((s, LANES)), _full((s, LANES)), pl.BlockSpec(memory_space=pltpu.SMEM),
                   pl.BlockSpec(memory_space=pl.ANY)],
        out_shape=[jax.ShapeDtypeStruct((s, 512), BF), jax.ShapeDtypeStruct((s, 512), BF),
                   jax.ShapeDtypeStruct((s, LANES), F32), jax.ShapeDtypeStruct((s, LANES), F32),
                   jax.ShapeDtypeStruct((1, SWA_HEADS), F32), jax.ShapeDtypeStruct((r_out, c_out), F32)],
        scratch_shapes=[pltpu.VMEM((SWA_BLOCK, LANES), F32)] * 4 + _reduce_scratch(r_out, c_out),
        compiler_params=_params(("arbitrary",)),
    )(proj, proj, proj, proj, dos, opre, cos, sin, sinks, dw_out_parts)


def _outproj(og, osw, w_out, x2d, target, gate, g_final):
    s = x2d.shape[0]
    tm = min(512, s)

    def body(og_ref, os_ref, w_ref, x_ref, t_ref, gate_ref, gf_ref,
             dx2_ref, dog_ref, dos_ref, dw_ref, loss_ref, dgf_ref, dgate_ref):
        @pl.when(pl.program_id(0) == 0)
        def _():
            dw_ref[...] = jnp.zeros_like(dw_ref)
            loss_ref[...] = jnp.zeros_like(loss_ref)
            dgf_ref[...] = jnp.zeros_like(dgf_ref)
            dgate_ref[...] = jnp.zeros_like(dgate_ref)

        w = w_ref[...]
        gate, gf = gate_ref[...], gf_ref[...]
        subs = _subtiles(tm)
        ogv = [og_ref[sl, :] for sl in subs]
        osv = [os_ref[sl, :] for sl in subs]
        y = [_dot(ogv[k], w[:512]) + _dot(osv[k], w[512:]) for k in range(len(subs))]
        dys = []
        for k, sl in enumerate(subs):
            x2 = x_ref[sl, :] + gate * y[k]
            r = lax.rsqrt(jnp.mean(x2 * x2, axis=-1, keepdims=True) + RMS_EPS)
            xn = x2 * r
            err = xn * gf - t_ref[sl, :]
            loss_ref[...] += 0.5 * jnp.sum(jnp.mean(err * err, axis=-1, keepdims=True), axis=0, keepdims=True)
            dyf = err * (1.0 / D_MODEL)
            dgf_ref[...] += jnp.sum(dyf * xn, axis=0, keepdims=True)
            t = dyf * gf
            dx2 = r * (t - xn * jnp.mean(t * xn, axis=-1, keepdims=True))
            dx2_ref[sl, :] = dx2
            dgate_ref[...] += jnp.sum(dx2 * y[k], axis=0, keepdims=True)
            dys.append((dx2 * gate).astype(BF))
            dmix = _dot(dys[k], w, NT)
            dog_ref[sl, :] = dmix[:, :512]
            dos_ref[sl, :] = dmix[:, 512:]
        dy = jnp.concatenate(dys, axis=0)
        dw_ref[:512, :] += _dot(og_ref[...], dy, TN)
        dw_ref[512:, :] += _dot(os_ref[...], dy, TN)

    half = pl.BlockSpec((tm, 512), lambda i: (i, 0))
    rowb = pl.BlockSpec((tm, D_MODEL), lambda i: (i, 0))
    vec = _full((1, D_MODEL))
    return pl.pallas_call(
        body, name="outproj", grid=(s // tm,),
        in_specs=[half, half, _full((D_MODEL, D_MODEL)), rowb, rowb, vec, vec],
        out_specs=[rowb, half, half, _full((D_MODEL, D_MODEL)), _full((1, 1)), vec, vec],
        out_shape=[jax.ShapeDtypeStruct((s, D_MODEL), F32), jax.ShapeDtypeStruct((s, 512), F32),
                   jax.ShapeDtypeStruct((s, 512), F32), jax.ShapeDtypeStruct((D_MODEL, D_MODEL), F32),
                   jax.ShapeDtypeStruct((1, 1), F32), jax.ShapeDtypeStruct((1, D_MODEL), F32),
                   jax.ShapeDtypeStruct((1, D_MODEL), F32)],
        compiler_params=_params(("arbitrary",)),
    )(og, osw, w_out, x2d, target, gate, g_final)


_PIECES = ((OFF_QK, 512), (OFF_V, 512), (OFF_GZ, 512), (OFF_SQ, 512), (OFF_SZ, 512),
           (OFF_SK, LANES), (OFF_SV, LANES), (OFF_GA, LANES))

_UNPAD_ROWS = ((OFF_QK, 0, 1024),
               (OFF_GA, 1024, GLA_RANK),
               (OFF_GZ, 1040, 1024),
               (OFF_SK, 2064, 256),
               (OFF_SZ, 2320, 512))


def _inproj_bwd(x2d, shift, sc1p, g_norm, w_t, dx2, pieces):
    s = x2d.shape[0]
    tm = min(512, s)
    nsteps = s // tm

    def body(x_ref, sh_ref, sc_ref, g_ref, w_hbm, dx2_ref, *rest):
        piece_refs = rest[:len(_PIECES)]
        gx_ref, dw_hbm, dsh_ref, dsc_ref, dg_ref, w_vm, dw_vm, in_sems, out_sems = rest[len(_PIECES):]
        i = pl.program_id(0)

        @pl.when(i == 0)
        def _():
            loads = _load_w_padded(w_hbm, w_vm, in_sems)
            dw_vm[...] = jnp.zeros_like(dw_vm)
            dsh_ref[...] = jnp.zeros_like(dsh_ref)
            dsc_ref[...] = jnp.zeros_like(dsc_ref)
            dg_ref[...] = jnp.zeros_like(dg_ref)
            for cp in loads:
                cp.wait()

        g, sc1p_v, shift_v = g_ref[...], sc_ref[...], sh_ref[...]
        subs = _subtiles(tm)
        dhs = []
        for sl in subs:
            dh = None
            for (off, width), pr in zip(_PIECES, piece_refs):
                part = _dot(pr[sl, :].astype(BF), w_vm[off:off + width, :])
                dh = part if dh is None else dh + part
            dhs.append(dh)
        norm = [_modnorm(x_ref[sl, :], g, sc1p_v, shift_v) for sl in subs]
        hb = jnp.concatenate([h.astype(BF) for _, _, h in norm], axis=0)
        for (off, width), pr in zip(_PIECES, piece_refs):
            dw_vm[off:off + width, :] += _dot(pr[...].astype(BF), hb, TN)
        for sl, (xn, r, _), dh in zip(subs, norm, dhs):
            dsh_ref[...] += jnp.sum(dh, axis=0, keepdims=True)
            dsc_ref[...] += jnp.sum(dh * (xn * g), axis=0, keepdims=True)
            dg_ref[...] += jnp.sum(dh * xn * sc1p_v, axis=0, keepdims=True)
            dxn = dh * g * sc1p_v
            gx_ref[sl, :] = dx2_ref[sl, :] + r * (dxn - xn * jnp.mean(dxn * xn, axis=-1, keepdims=True))

        @pl.when(i == nsteps - 1)
        def _():
            copies = [pltpu.make_async_copy(dw_vm.at[src:src + n], dw_hbm.at[dst:dst + n], out_sems.at[k])
                      for k, (src, dst, n) in enumerate(_UNPAD_ROWS)]
            for cp in copies:
                cp.start()
            for cp in copies:
                cp.wait()

    rowb = pl.BlockSpec((tm, D_MODEL), lambda i: (i, 0))
    vec = _full((1, D_MODEL))
    anyspec = pl.BlockSpec(memory_space=pl.ANY)
    piece_specs = [pl.BlockSpec((tm, width), lambda i: (i, 0)) for _, width in _PIECES]
    return pl.pallas_call(
        body, name="inproj_bwd", grid=(nsteps,),
        in_specs=[rowb, vec, vec, vec, anyspec, rowb] + piece_specs,
        out_specs=[rowb, anyspec, vec, vec, vec],
        out_shape=[jax.ShapeDtypeStruct((s, D_MODEL), F32), jax.ShapeDtypeStruct((D_IN, D_MODEL), F32),
                   jax.ShapeDtypeStruct((1, D_MODEL), F32), jax.ShapeDtypeStruct((1, D_MODEL), F32),
                   jax.ShapeDtypeStruct((1, D_MODEL), F32)],
        scratch_shapes=[pltpu.VMEM((D_PAD, D_MODEL), BF), pltpu.VMEM((D_PAD, D_MODEL), F32),
                        pltpu.SemaphoreType.DMA((len(_UNPAD_ROWS),)), pltpu.SemaphoreType.DMA((len(_UNPAD_ROWS),))],
        compiler_params=_params(("arbitrary",)),
    )(x2d, shift, sc1p, g_norm, w_t, dx2, *pieces)


def _adam(w, g, m, v):
    m2 = ADAM_B1 * m + (1.0 - ADAM_B1) * g
    v2 = ADAM_B2 * v + (1.0 - ADAM_B2) * (g * g)
    m_hat = m2 / (1.0 - ADAM_B1 ** ADAM_STEP)
    v_hat = v2 / (1.0 - ADAM_B2 ** ADAM_STEP)
    delta = -ADAM_LR * (m_hat / (jnp.sqrt(v_hat) + ADAM_EPS) + ADAM_WD * w)
    return delta, m2, v2


def _adamw(w, g, m, v, name):
    rr, cc = w.shape
    tc = min(512, cc)

    def body(w_ref, g_ref, m_ref, v_ref, d_ref, m2_ref, v2_ref):
        d_ref[...], m2_ref[...], v2_ref[...] = _adam(w_ref[...], g_ref[...], m_ref[...], v_ref[...])

    blk = pl.BlockSpec((rr, tc), lambda i: (0, i))
    return pl.pallas_call(
        body, name=name, grid=(cc // tc,), in_specs=[blk] * 4, out_specs=[blk] * 3,
        out_shape=[jax.ShapeDtypeStruct((rr, cc), F32)] * 3,
        compiler_params=_params(("arbitrary",)),
    )(w, g, m, v)


def _adamw_t(w3, g, m3, v3, name):
    rr, _, cc = w3.shape
    tc = cc

    def body(w_hbm, g_ref, m_hbm, v_hbm, d_hbm, m2_hbm, v2_hbm, g3_hbm, w_vm, m_vm, v_vm, d_vm, m2_vm, v2_vm, in_sems, out_sems):
        cols = pl.ds(pl.multiple_of(pl.program_id(0) * tc, tc), tc)
        loads = [pltpu.make_async_copy(src.at[:, 0, cols], dst, in_sems.at[k])
                 for k, (src, dst) in enumerate(((w_hbm, w_vm), (m_hbm, m_vm), (v_hbm, v_vm)))]
        for cp in loads:
            cp.start()
        for cp in loads:
            cp.wait()
        d_vm[...], m2_vm[...], v2_vm[...] = _adam(w_vm[...], g_ref[...], m_vm[...], v_vm[...])
        stores = [pltpu.make_async_copy(src, dst.at[:, 0, cols], out_sems.at[k])
                  for k, (src, dst) in enumerate(((d_vm, d_hbm), (m2_vm, m2_hbm), (v2_vm, v2_hbm), (g_ref, g3_hbm)))]
        for cp in stores:
            cp.start()
        for cp in stores:
            cp.wait()

    hbm = pl.BlockSpec(memory_space=pl.ANY)
    return pl.pallas_call(
        body, name=name, grid=(cc // tc,), in_specs=[hbm, pl.BlockSpec((rr, tc), lambda i: (0, i)), hbm, hbm],
        out_specs=[hbm] * 4, out_shape=[jax.ShapeDtypeStruct((rr, 1, cc), F32)] * 4,
        scratch_shapes=[pltpu.VMEM((rr, tc), F32)] * 6 + [pltpu.SemaphoreType.DMA((3,)), pltpu.SemaphoreType.DMA((4,))],
        compiler_params=_params(("arbitrary",)),
    )(w3, g, m3, v3)


def _ada_update(c_all, dmod_cols, w, m, v):
    rr, cc = w.shape
    tr = min(512, rr)
    c_all = jnp.pad(c_all, ((0, 8), (0, 0)))
    dmod_cols = jnp.pad(dmod_cols, ((0, 8), (0, 0)))

    def body(c_ref, dm_ref, w_ref, m_ref, v_ref, g_ref, d_ref, m2_ref, v2_ref):
        cv = c_ref[...]
        sc = (cv * _sigmoid(cv)).astype(BF)
        g = _dot(sc, dm_ref[...].astype(BF), TN)
        g_ref[...] = g
        d_ref[...], m2_ref[...], v2_ref[...] = _adam(w_ref[...], g, m_ref[...], v_ref[...])

    blk = pl.BlockSpec((tr, cc), lambda i: (i, 0))
    return pl.pallas_call(
        body, name="ada_update", grid=(rr // tr,),
        in_specs=[pl.BlockSpec((16, tr), lambda i: (0, i)), _full((16, cc)), blk, blk, blk],
        out_specs=[blk] * 4, out_shape=[jax.ShapeDtypeStruct((rr, cc), F32)] * 4,
        compiler_params=_params(("arbitrary",)),
    )(c_all, dmod_cols, w, m, v)


def _small_update(parts, weights, moms, vels):
    n = len(weights)

    def body(*refs):
        p_refs, w_refs, m_refs, v_refs = refs[:n + 1], refs[n + 1:2 * n + 1], refs[2 * n + 1:3 * n + 1], refs[3 * n + 1:4 * n + 1]
        outs = refs[4 * n + 1:]
        for i in range(n):
            g = p_refs[i][0]
            for d in range(1, 8):
                g = g + p_refs[i][d]
            delta, m2, v2 = _adam(w_refs[i][...], g, m_refs[i][...], v_refs[i][...])
            outs[4 * i][...] = g
            outs[4 * i + 1][...] = delta
            outs[4 * i + 2][...] = m2
            outs[4 * i + 3][...] = v2
        tot = p_refs[n][0]
        for d in range(1, 8):
            tot = tot + p_refs[n][d]
        outs[4 * n][...] = tot

    out_shape = []
    for w in weights:
        out_shape += [jax.ShapeDtypeStruct(w.shape, F32)] * 4
    out_shape.append(jax.ShapeDtypeStruct(parts[n].shape[1:], F32))
    return pl.pallas_call(body, name="small_update", out_shape=out_shape, compiler_params=_params())(
        *parts, *weights, *moms, *vels)


def _rows8(a):
    flat = a.reshape(-1)
    rows = -(-flat.shape[0] // LANES)
    rows8 = -(-rows // 8) * 8
    flat = jnp.pad(flat, (0, rows8 * LANES - flat.shape[0]))
    return flat.reshape(rows8, LANES)


def kernel(x, c, positions, w_ada, b_ada, g_norm, w_in, w_decay, b_decay, g_gla_head, sinks, w_out, g_final, loss_target, m_w_ada, m_b_ada, m_g_norm, m_w_in, m_w_decay, m_b_decay, m_g_gla_head, m_sinks, m_w_out, m_g_final, v_w_ada, v_b_ada, v_g_norm, v_w_in, v_w_decay, v_b_decay, v_g_gla_head, v_sinks, v_w_out, v_g_final):
    ax, ay, ac = lax.axis_index("x"), lax.axis_index("y"), lax.axis_index("c")
    chip = 2 * ax + ay
    dev = 2 * chip + ac
    s = x.shape[1]
    x2d = x[0]
    target = loss_target[0]
    w_ada2, w_out2, w_dec2 = w_ada[0], w_out[0], w_decay[0]
    w_in_t = w_in[0].T
    ada_cols = w_ada2.shape[1]
    in_cols = w_in_t.shape[0]
    out_rows = w_out2.shape[0]
    half = D_MODEL // 2

    cw = jnp.concatenate([c.reshape(8, LANES), w_dec2.reshape(8, LANES)], axis=0)
    b_shard = lax.dynamic_slice(b_ada, (0, chip * ada_cols), (1, ada_cols))
    half_in = lax.dynamic_slice(w_in_t, (0, ac * half), (in_cols, half)).astype(BF)
    half_out = lax.dynamic_slice(w_out2, (ac * (out_rows // 2), 0), (out_rows // 2, D_MODEL)).astype(BF)
    inv_freq = 1.0 / (ROPE_THETA ** (jnp.arange(0, 64, 2, dtype=F32) / 64))
    first, mod_all, w_in_all, cos, sin = _prologue(
        cw, w_ada2, b_shard, half_in, positions.reshape(s, 1), jnp.tile(inv_freq, 4).reshape(1, LANES))

    first = first.reshape(8, 2, 8, LANES)
    c_all = first[:, 0].reshape(8, D_MODEL)
    w_dec_full = first[0::2, 1].reshape(4, GLA_RANK, 64).transpose(1, 0, 2).reshape(GLA_RANK, 256)
    mod = mod_all.reshape(4, 2, 8, ada_cols)[:, 0]
    mod = lax.dynamic_slice(mod, (0, dev, 0), (4, 1, ada_cols)).reshape(1, 4 * ada_cols)
    shift, sc1p, gate = mod[:, :D_MODEL], 1.0 + mod[:, D_MODEL:2 * D_MODEL], mod[:, 2 * D_MODEL:]
    w_t = w_in_all.reshape(4 * in_cols, D_MODEL)

    wdecp = jnp.pad(w_dec_full, ((0, LANES - GLA_RANK), (0, 0))).astype(BF)

    proj = _inproj_fwd(x2d, shift, sc1p, g_norm, w_t)
    og, o_gla, sprev = _gla_fwd(proj, wdecp, b_decay, g_gla_head)
    osw, o_swa, w_out_all = _swa_fwd(proj, cos, sin, sinks, half_out)
    w_out_all = w_out_all.reshape(D_MODEL, D_MODEL)
    dx2, dog, dos, dw_out, loss_p, dgf, dgate = _outproj(og, osw, w_out_all, x2d, target, gate, g_final.reshape(1, D_MODEL))
    dsq, dsz, dsk, dsv, dsinks, g_w_out = _swa_bwd(proj, dos, o_swa, cos, sin, sinks, dw_out.reshape(4, out_rows, D_MODEL))
    dqk, dv, dgz, dga, dwdp, dbd, dgg = _gla_bwd(proj, dog, o_gla, sprev, wdecp, b_decay, g_gla_head)
    pieces = (dqk, dv, dgz, dsq, dsz, dsk, dsv, dga)
    gx, dw_in_t, dshift, dscale, dgn = _inproj_bwd(x2d, shift, sc1p, g_norm, w_t, dx2, pieces)

    segs = [jnp.concatenate([dshift, dscale, dgate], axis=1), dgn, dgf, dwdp[:GLA_RANK], dbd, dgg, dsinks, loss_p]
    packed = [_rows8(a) for a in segs]
    offs = [0]
    for a in packed:
        offs.append(offs[-1] + a.shape[0])
    g_w_in_t, small = _epilogue(dw_in_t.reshape(4, in_cols, D_MODEL), jnp.concatenate(packed, axis=0))

    def seg(i, size):
        return small[:, offs[i]:offs[i + 1]].reshape(8, -1)[:, :size]

    dmod_all = seg(0, 3 * D_MODEL)
    dwd_all = lax.dynamic_slice(seg(3, GLA_RANK * 256).reshape(8, GLA_RANK, 256), (0, 0, chip * 64), (8, GLA_RANK, 64))
    parts = [dmod_all.reshape(8, 1, 3 * D_MODEL), seg(1, D_MODEL).reshape(8, 1, D_MODEL), dwd_all,
             seg(4, 256).reshape(8, 1, 256), seg(5, 512).reshape(8, 1, 512), seg(6, SWA_HEADS).reshape(8, 1, SWA_HEADS),
             seg(2, D_MODEL).reshape(8, 1, D_MODEL), seg(7, LANES).reshape(8, 1, LANES)]
    smalls = _small_update(
        parts,
        [b_ada, g_norm, w_dec2, b_decay, g_gla_head, sinks, g_final.reshape(1, D_MODEL)],
        [m_b_ada, m_g_norm, m_w_decay[0], m_b_decay, m_g_gla_head, m_sinks, m_g_final.reshape(1, D_MODEL)],
        [v_b_ada, v_g_norm, v_w_decay[0], v_b_decay, v_g_gla_head, v_sinks, v_g_final.reshape(1, D_MODEL)])
    (g_b_ada, d_b_ada, nm_b_ada, nv_b_ada, g_gn, d_gn, nm_gn, nv_gn, g_wd, d_wd, nm_wd, nv_wd,
     g_bd, d_bd, nm_bd, nv_bd, g_gg, d_gg, nm_gg, nv_gg, g_sk, d_sk, nm_sk, nv_sk,
     g_gf, d_gf, nm_gf, nv_gf, loss_row) = smalls
    loss = loss_row[0, 0]

    dmod_cols = lax.dynamic_slice(dmod_all, (0, chip * ada_cols), (8, ada_cols))
    g_w_ada, d_w_ada, nm_w_ada, nv_w_ada = _ada_update(c_all, dmod_cols, w_ada2, m_w_ada[0], v_w_ada[0])
    to3 = lambda a: jnp.transpose(a, (2, 0, 1))
    from3 = lambda a: jnp.transpose(a, (1, 2, 0))[0]
    d3, nm3, nv3, g3 = _adamw_t(to3(w_in), g_w_in_t, to3(m_w_in), to3(v_w_in), "adamw_w_in")
    g_w_in, d_w_in, nm_w_in, nv_w_in = from3(g3), from3(d3), from3(nm3), from3(nv3)
    d_w_out, nm_w_out, nv_w_out = _adamw(w_out2, g_w_out, m_w_out[0], v_w_out[0], "adamw_w_out")

    flat = lambda a: a.reshape(D_MODEL)
    grads = [g_w_ada[None], g_b_ada, g_gn, g_w_in[None], g_wd[None], g_bd, g_gg, g_sk, g_w_out[None], flat(g_gf)]
    deltas = [d_w_ada[None], d_b_ada, d_gn, d_w_in[None], d_wd[None], d_bd, d_gg, d_sk, d_w_out[None], flat(d_gf)]
    new_m = [nm_w_ada[None], nm_b_ada, nm_gn, nm_w_in[None], nm_wd[None], nm_bd, nm_gg, nm_sk, nm_w_out[None], flat(nm_gf)]
    new_v = [nv_w_ada[None], nv_b_ada, nv_gn, nv_w_in[None], nv_wd[None], nv_bd, nv_gg, nv_sk, nv_w_out[None], flat(nv_gf)]
    return (loss, gx[None], *grads, *deltas, *new_m, *new_v)
```

```python
import jax
import jax.numpy as jnp
from jax import lax
from jax.experimental import pallas as pl
from jax.experimental.pallas import tpu as pltpu

F32 = jnp.float32
BF = jnp.bfloat16

D_MODEL = 1024
GLA_HEADS = 4
GLA_DK = 64
GLA_CHUNK = 64
GLA_RANK = 16
GLA_TAU = 16.0
GLA_SUB = 256
GLA_ROWS_FWD = 1024
GLA_ROWS_BWD = 512
SWA_HEADS = 8
SWA_BLOCK = 128
SWA_QBLOCKS_FWD = 8
SWA_QBLOCKS = 8
RMS_EPS = 1e-6
ROPE_THETA = 10000.0

OFF_QK, OFF_V, OFF_GZ, OFF_SQ, OFF_SZ, OFF_SK, OFF_SV, OFF_GA = 0, 512, 1024, 1536, 2048, 2560, 2688, 2816
D_PAD = 2944
D_IN = 2832
LANES = 128
VMEM_LIMIT = 56 * 1024 * 1024

ADAM_LR, ADAM_B1, ADAM_B2, ADAM_EPS, ADAM_WD, ADAM_STEP = 0.001, 0.9, 0.999, 1e-08, 0.01, 10

NT = (((1,), (1,)), ((), ()))
TN = (((0,), (0,)), ((), ()))
MESH = pl.DeviceIdType.MESH


def _dot(a, b, dims=None):
    if dims is None:
        return jnp.dot(a, b, preferred_element_type=F32)
    return lax.dot_general(a, b, dims, preferred_element_type=F32)


def _sigmoid(x):
    return 1.0 / (1.0 + jnp.exp(-x))


def _params(sem=None):
    return pltpu.CompilerParams(dimension_semantics=sem, vmem_limit_bytes=VMEM_LIMIT)


def _full(shape):
    return pl.BlockSpec(shape, lambda i: (0,) * len(shape))


def _subtiles(rows, size=256):
    size = min(size, rows)
    return [slice(k * size, (k + 1) * size) for k in range(rows // size)]


WEIGHT_CHUNKS = 4


def _gather_sems(chunks=1):
    return [pltpu.SemaphoreType.DMA((7 * chunks,)), pltpu.SemaphoreType.DMA((7 * chunks,)), pltpu.SemaphoreType.DMA]


_GATHER_SEMS = _gather_sems()


class _Gather:
    def __init__(self, x_ref, out_ref, send_sems, recv_sems, local_sem, slab=None, chunks=1):
        self.slab_of = slab
        self.chunks = chunks
        self.width = x_ref.shape[-1] // chunks
        x, y, c = lax.axis_index("x"), lax.axis_index("y"), lax.axis_index("c")
        self.me, self.sibling, self.c = (x, y, c), (x, y, 1 - c), c
        self.xn, self.yn, self.dg = (1 - x, y), (x, 1 - y), (1 - x, 1 - y)
        self.pass_from = (lax.rem(x + 1 - c, 2), lax.rem(y + c, 2))
        self.pass_to = (lax.rem(x + c, 2), lax.rem(y + 1 - c, 2))
        self.x_ref, self.out_ref, self.send_sems, self.recv_sems = x_ref, out_ref, send_sems, recv_sems
        self.mine = pltpu.make_async_copy(x_ref, self._slab(*self.me), local_sem)

    def _slab(self, px, py, pc):
        if self.slab_of is not None:
            return self.slab_of(self.out_ref, px, py, pc)
        return self.out_ref.at[4 * px + 2 * py + pc]

    def _part(self, ref, q):
        if self.chunks == 1:
            return ref
        lanes = slice(q * self.width, (q + 1) * self.width)
        return ref.at[(slice(None),) * (len(ref.shape) - 1) + (lanes,)]

    def _copy(self, k, q, blk, to, src=None):
        i = k * self.chunks + q
        return pltpu.make_async_remote_copy(
            src_ref=self._part(self._slab(*blk) if src is None else src, q), dst_ref=self._part(self._slab(*blk), q),
            send_sem=self.send_sems.at[i], recv_sem=self.recv_sems.at[i], device_id=to, device_id_type=MESH)

    def _sends(self, q):
        c = self.c
        return [self._copy(0, q, self.me, self.sibling, src=self.x_ref),
                self._copy(1, q, self.me, (*self.xn, c), src=self.x_ref),
                self._copy(2, q, self.me, (*self.yn, c), src=self.x_ref),
                self._copy(3, q, (*self.pass_from, c), (*self.pass_to, c)),
                self._copy(4, q, (*self.xn, c), self.sibling),
                self._copy(5, q, (*self.yn, c), self.sibling),
                self._copy(6, q, (*self.dg, c), self.sibling)]

    def start(self):
        self.mine.start()
        for q in range(self.chunks):
            sends = self._sends(q)
            for k in (1, 2, 0):
                sends[k].start()

    def pass_on(self, only=None):
        for q in range(self.chunks) if only is None else (only,):
            sends = self._sends(q)
            self._copy(1, q, (*self.xn, self.c), self.me).wait_recv()
            self._copy(2, q, (*self.yn, self.c), self.me).wait_recv()
            for k in (3, 4, 5):
                sends[k].start()

    def relay_diagonal(self, only=None):
        for q in range(self.chunks) if only is None else (only,):
            self._copy(3, q, (*self.dg, self.c), self.me).wait_recv()
            self._sends(q)[6].start()

    def relay(self):
        self.pass_on()
        self.relay_diagonal()

    def finish(self):
        c = self.c
        for q in range(self.chunks):
            self._copy(0, q, self.sibling, self.me).wait_recv()
            for k, chip in ((4, self.xn), (5, self.yn), (6, self.dg)):
                self._copy(k, q, (*chip, 1 - c), self.me).wait_recv()
            for cp in self._sends(q):
                cp.wait_send()
        self.mine.wait()


def _prologue(cw, w_ada, b_shard, half_in, pos_col, inv_freq):
    s = pos_col.shape[0]
    rt = min(512, s)

    def body(cw_ref, wada_hbm, b_ref, hin_ref, pos_hbm, f_ref,
             first_ref, mod_ref, win_ref, cos_hbm, sin_hbm,
             mod_blk, cos_ref, sin_ref, wada_ref, pos_ref, table_sems, local_sems, *sems):
        fetch_w = pltpu.make_async_copy(wada_hbm, wada_ref, local_sems.at[0])
        fetch_p = pltpu.make_async_copy(pos_hbm, pos_ref, local_sems.at[1])
        fetch_w.start()
        fetch_p.start()
        g_c = _Gather(cw_ref, first_ref, *sems[0:3])
        half_lanes = hin_ref.shape[1]
        g_in = _Gather(hin_ref, win_ref, *sems[3:6], chunks=WEIGHT_CHUNKS,
                       slab=lambda ref, px, py, pc: ref.at[2 * px + py, :, pl.ds(pl.multiple_of(pc * half_lanes, half_lanes), half_lanes)])
        g_mod = _Gather(mod_blk, mod_ref, *sems[6:9])
        g_c.start()
        g_in.start()
        g_c.relay()
        g_c.finish()
        c_rows = [jnp.concatenate([first_ref[d, r:r + 1, :] for r in range(8)], axis=1) for d in range(8)]
        c_all = jnp.concatenate(c_rows, axis=0)
        sc = (c_all * _sigmoid(c_all)).astype(BF)
        fetch_w.wait()
        mod_blk[...] = _dot(sc, wada_ref[...].astype(BF)) + b_ref[...]
        g_mod.start()
        fetch_p.wait()

        def rope_rows(i, carry):
            rows = pl.ds(pl.multiple_of(i * rt, rt), rt)
            ang = pos_ref[rows, :].astype(F32) * f_ref[...]
            lane = lax.broadcasted_iota(jnp.int32, ang.shape, 1)
            cos_ref[rows, :] = jnp.cos(ang)
            sn = jnp.sin(ang)
            sin_ref[rows, :] = jnp.where((lane % 64) < 32, -sn, sn)
            pltpu.make_async_copy(cos_ref.at[rows, :], cos_hbm.at[rows, :], table_sems.at[0]).start()
            pltpu.make_async_copy(sin_ref.at[rows, :], sin_hbm.at[rows, :], table_sems.at[1]).start()
            return carry

        waits = ([lambda q=q: g_in.pass_on(q) for q in range(WEIGHT_CHUNKS)]
                 + [lambda q=q: g_in.relay_diagonal(q) for q in range(WEIGHT_CHUNKS)] + [g_mod.relay])
        steps = s // rt
        lead = steps // 4
        per_wait = max((steps - lead) // len(waits), 1)
        lax.fori_loop(0, lead, rope_rows, 0)
        done = lead
        for wait in waits:
            wait()
            nxt = min(done + per_wait, steps)
            lax.fori_loop(done, nxt, rope_rows, 0)
            done = nxt
        lax.fori_loop(done, steps, rope_rows, 0)
        g_in.finish()
        g_mod.finish()
        pltpu.make_async_copy(cos_ref, cos_hbm, table_sems.at[0]).wait()
        pltpu.make_async_copy(sin_ref, sin_hbm, table_sems.at[1]).wait()

    vm = pl.BlockSpec(memory_space=pltpu.VMEM)
    hbm = pl.BlockSpec(memory_space=pl.ANY)
    return pl.pallas_call(
        body, name="prologue",
        out_shape=[jax.ShapeDtypeStruct((8,) + cw.shape, F32), jax.ShapeDtypeStruct((8, 8, w_ada.shape[1]), F32),
                   jax.ShapeDtypeStruct((4, half_in.shape[0], 2 * half_in.shape[1]), half_in.dtype),
                   jax.ShapeDtypeStruct((s, LANES), F32), jax.ShapeDtypeStruct((s, LANES), F32)],
        in_specs=[vm, hbm, vm, hbm, hbm, vm], out_specs=[vm, vm, hbm, hbm, hbm],
        scratch_shapes=[pltpu.VMEM((8, w_ada.shape[1]), F32), pltpu.VMEM((s, LANES), F32), pltpu.VMEM((s, LANES), F32),
                        pltpu.VMEM(w_ada.shape, F32), pltpu.VMEM(pos_col.shape, jnp.int32),
                        pltpu.SemaphoreType.DMA((2,)), pltpu.SemaphoreType.DMA((2,))]
        + _GATHER_SEMS + _gather_sems(WEIGHT_CHUNKS) + _GATHER_SEMS,
        compiler_params=pltpu.CompilerParams(vmem_limit_bytes=VMEM_LIMIT),
    )(cw, w_ada, b_shard, half_in, pos_col, inv_freq)


def _reduce_scratch(rr, cc):
    c2 = cc // 2
    return [pltpu.VMEM((4, rr, c2), F32), pltpu.VMEM((4, rr, c2), F32), pltpu.VMEM((3, rr, c2), BF),
            pltpu.VMEM((2, rr, c2), BF), pltpu.VMEM((rr, c2), BF), pltpu.VMEM((rr, c2), F32),
            pltpu.SemaphoreType.DMA((8 + 3 * WEIGHT_CHUNKS,)), pltpu.SemaphoreType.DMA((8 + 3 * WEIGHT_CHUNKS,)),
            pltpu.SemaphoreType.DMA((5,))]


class _Reduce:
    def __init__(self, p_hbm, out_ref, acc_ref, own_ref, send_ref, land_ref, relay_ref, res_ref,
                 send_sems, recv_sems, local_sems):
        x, y, c = lax.axis_index("x"), lax.axis_index("y"), lax.axis_index("c")
        c2 = out_ref.shape[1] // 2
        sibling = (x, y, 1 - c)
        first = (lax.rem(x + 1 - c, 2), lax.rem(y + c, 2))
        second = (lax.rem(x + c, 2), lax.rem(y + 1 - c, 2))
        shards = [2 * first[0] + first[1], 2 * second[0] + second[1], 2 * (1 - x) + (1 - y), 2 * x + y]
        sibling_slot = (1, 0, 2, 3)
        mine = pl.ds(pl.multiple_of(c * c2, c2), c2)
        other = pl.ds(pl.multiple_of((1 - c) * c2, c2), c2)
        self.acc_ref, self.own_ref, self.send_ref, self.land_ref = acc_ref, own_ref, send_ref, land_ref
        self.relay_ref, self.res_ref = relay_ref, res_ref
        self.own = [pltpu.make_async_copy(p_hbm.at[j, :, mine], own_ref.at[k], local_sems.at[k])
                    for k, j in enumerate(shards)]
        self.swap_out = [pltpu.make_async_remote_copy(
            src_ref=p_hbm.at[j, :, other], dst_ref=acc_ref.at[sibling_slot[k]], send_sem=send_sems.at[k],
            recv_sem=recv_sems.at[sibling_slot[k]], device_id=sibling, device_id_type=MESH) for k, j in enumerate(shards)]
        self.swap_in = [pltpu.make_async_remote_copy(
            src_ref=p_hbm.at[j, :, other], dst_ref=acc_ref.at[k], send_sem=send_sems.at[k], recv_sem=recv_sems.at[k],
            device_id=sibling, device_id_type=MESH) for k, j in enumerate(shards)]

        self.lanes = [slice(q * (c2 // WEIGHT_CHUNKS), (q + 1) * (c2 // WEIGHT_CHUNKS)) for q in range(WEIGHT_CHUNKS)]

        def message(m, src, dst, to):
            return [pltpu.make_async_remote_copy(
                src_ref=src.at[:, ln], dst_ref=dst.at[:, ln], send_sem=send_sems.at[8 + m * WEIGHT_CHUNKS + q],
                recv_sem=recv_sems.at[8 + m * WEIGHT_CHUNKS + q], device_id=(*to, c), device_id_type=MESH)
                for q, ln in enumerate(self.lanes)]

        self.direct = message(0, send_ref.at[0], land_ref.at[0], first)
        self.passed = message(1, send_ref.at[1], relay_ref, first)
        self.joint = message(2, send_ref.at[2], land_ref.at[1], second)
        self.put = pltpu.make_async_copy(res_ref, out_ref.at[:, mine], local_sems.at[4])
        self.share = pltpu.make_async_remote_copy(
            src_ref=res_ref, dst_ref=out_ref.at[:, mine], send_sem=send_sems.at[7],
            recv_sem=recv_sems.at[7], device_id=sibling, device_id_type=MESH)

    def start(self):
        for k in (2, 0, 1, 3):
            self.own[k].start()
            self.swap_out[k].start()

    def _combine(self, k):
        self.own[k].wait()
        self.swap_out[k].wait_send()
        self.swap_in[k].wait_recv()
        self.acc_ref[k] = self.acc_ref[k] + self.own_ref[k]

    def combine_and_send(self):
        dt = self.send_ref.dtype
        self._combine(2)
        self.send_ref[1] = self.acc_ref[2].astype(dt)
        for cp in self.passed:
            cp.start()
        self._combine(0)
        self.send_ref[0] = self.acc_ref[0].astype(dt)
        for cp in self.direct:
            cp.start()
        self._combine(1)
        self._combine(3)

    def send_joint(self):
        dt = self.send_ref.dtype
        for q, ln in enumerate(self.lanes):
            self.passed[q].wait_recv()
            self.send_ref[2, :, ln] = (self.acc_ref[1, :, ln] + self.relay_ref[:, ln].astype(F32)).astype(dt)
            self.joint[q].start()

    def total_and_share(self):
        for cp in self.direct + self.joint:
            cp.wait_recv()
        self.res_ref[...] = self.acc_ref[3] + self.land_ref[0].astype(F32) + self.land_ref[1].astype(F32)
        for cp in self.direct + self.passed + self.joint:
            cp.wait_send()
        self.put.start()
        self.share.start()

    def finish(self):
        self.put.wait()
        self.share.wait()


def _epilogue(dw_in_parts, small):
    _, r_in, cc = dw_in_parts.shape
    n_red = len(_reduce_scratch(r_in, cc))

    def body(pin_hbm, small_ref, gin_ref, small_all_ref, *scratch):
        red_in = _Reduce(pin_hbm, gin_ref, *scratch[0:n_red])
        gat = _Gather(small_ref, small_all_ref, *scratch[n_red:])
        red_in.start()
        gat.start()
        gat.relay()
        red_in.combine_and_send()
        gat.finish()
        red_in.send_joint()
        red_in.total_and_share()
        red_in.finish()

    vm = pl.BlockSpec(memory_space=pltpu.VMEM)
    anyspec = pl.BlockSpec(memory_space=pl.ANY)
    return pl.pallas_call(
        body, name="epilogue",
        out_shape=[jax.ShapeDtypeStruct((r_in, cc), F32), jax.ShapeDtypeStruct((8,) + small.shape, F32)],
        in_specs=[anyspec, vm], out_specs=[anyspec, vm],
        scratch_shapes=_reduce_scratch(r_in, cc) + _GATHER_SEMS,
        compiler_params=pltpu.CompilerParams(vmem_limit_bytes=VMEM_LIMIT),
    )(dw_in_parts, small)


def _rope(t, cosb, sinb, first_half):
    partner = jnp.where(first_half, pltpu.roll(t, 96, 1), pltpu.roll(t, 32, 1))
    return t * cosb + partner * sinb


def _rope_t(g, cosb, sinb, first_half):
    gs = g * sinb
    partner = jnp.where(first_half, pltpu.roll(gs, 96, 1), pltpu.roll(gs, 32, 1))
    return g * cosb + partner


def _modnorm(x, g, sc1p, shift):
    r = lax.rsqrt(jnp.mean(x * x, axis=-1, keepdims=True) + RMS_EPS)
    xn = x * r
    return xn, r, (xn * g) * sc1p + shift


def _load_w_padded(w_hbm, w_vm, sems):
    copies = [pltpu.make_async_copy(w_hbm.at[ref:ref + n], w_vm.at[pad:pad + n], sems.at[k])
              for k, (pad, ref, n) in enumerate(_UNPAD_ROWS)]
    for cp in copies:
        cp.start()
    w_vm[OFF_GA + GLA_RANK:, :] = jnp.zeros((D_PAD - OFF_GA - GLA_RANK, D_MODEL), w_vm.dtype)
    return copies


def _inproj_fwd(x2d, shift, sc1p, g_norm, w_t):
    s = x2d.shape[0]
    tm = min(1024, s)

    def body(x_ref, sh_ref, sc_ref, g_ref, w_hbm, o_ref, w_vm, sems):
        @pl.when(pl.program_id(0) == 0)
        def _():
            for cp in _load_w_padded(w_hbm, w_vm, sems):
                cp.wait()

        subs = _subtiles(tm)
        hs = [_modnorm(x_ref[sl, :], g_ref[...], sc_ref[...], sh_ref[...])[2].astype(BF) for sl in subs]
        for sl, h in zip(subs, hs):
            o_ref[sl, :] = _dot(h, w_vm[...], NT)

    vec = _full((1, D_MODEL))
    return pl.pallas_call(
        body, name="inproj_fwd", grid=(s // tm,),
        in_specs=[pl.BlockSpec((tm, D_MODEL), lambda i: (i, 0)), vec, vec, vec, pl.BlockSpec(memory_space=pl.ANY)],
        out_specs=pl.BlockSpec((tm, D_PAD), lambda i: (i, 0)),
        out_shape=jax.ShapeDtypeStruct((s, D_PAD), F32),
        scratch_shapes=[pltpu.VMEM((D_PAD, D_MODEL), BF), pltpu.SemaphoreType.DMA((len(_UNPAD_ROWS),))],
        compiler_params=_params(("arbitrary",)),
    )(x2d, shift, sc1p, g_norm, w_t)


def _split3(a):
    hi = a.astype(BF)
    r1 = a - hi.astype(F32)
    mid = r1.astype(BF)
    lo = (r1 - mid.astype(F32)).astype(BF)
    return hi, mid, lo


def _tri_matmul(tri, a):
    hi, mid, lo = _split3(a)
    return _dot(tri, hi) + _dot(tri, mid) + _dot(tri, lo)


def _chunks(tb):
    return [slice(c * GLA_CHUNK, (c + 1) * GLA_CHUNK) for c in range(tb // GLA_CHUNK)]


def _per_chunk_rows(rows, width):
    return jnp.concatenate([jnp.broadcast_to(r, (GLA_CHUNK, width)) for r in rows], axis=0)


def _gla_triangle(tb):
    row = lax.broadcasted_iota(jnp.int32, (tb, tb), 0)
    col = lax.broadcasted_iota(jnp.int32, (tb, tb), 1)
    return (((row // GLA_CHUNK) == (col // GLA_CHUNK)) & (col <= row)).astype(F32)


def _lane_mean(x, ones_b):
    hi = x.astype(BF)
    lo = (x - hi.astype(F32)).astype(BF)
    return (_dot(hi, ones_b) + _dot(lo, ones_b)) * (1.0 / LANES)


def _head(t, h, lo_h):
    blk = t[:, LANES * (h // 2):LANES * (h // 2 + 1)]
    return jnp.where(lo_h, blk, 0.0) if h % 2 == 0 else jnp.where(lo_h, 0.0, blk)


def _gla_block_common(qk, ga, wd, bd, tril_b):
    tb = qk.shape[0]
    q, k = qk[:, :256], qk[:, 256:]
    z = _dot(ga.astype(BF), wd) + bd
    la = (jnp.minimum(z, 0.0) - jnp.log(1.0 + jnp.exp(-jnp.abs(z)))) * (1.0 / GLA_TAU)
    b = _tri_matmul(tril_b, la)
    bls = [b[rs.stop - 1:rs.stop, :] for rs in _chunks(tb)]
    eq = jnp.exp(b)
    ek = jnp.exp(-b)
    f = jnp.exp(_per_chunk_rows(bls, 256) - b)
    return z, eq, ek, f, q * (eq * GLA_DK ** -0.5), k * ek, k * f, bls


def _gla_units(s, rows):
    sub = min(GLA_SUB, s)
    tb = min(rows, s)
    subs = [slice(i * sub, (i + 1) * sub) for i in range(tb // sub)]
    units = [(i, h) for i in range(len(subs)) for h in range(GLA_HEADS)]
    return tb, sub, subs, units


def _gla_fwd(proj, wdecp, bdec, ggla):
    s = proj.shape[0]
    tb, sub, subs, units = _gla_units(s, GLA_ROWS_FWD)
    nch = sub // GLA_CHUNK

    def body(qk_ref, v_ref, gz_ref, ga_ref, wd_ref, bd_ref, gg_ref, tri_ref, og_ref, opre_ref, sprev_ref, st_ref):
        @pl.when(pl.program_id(0) == 0)
        def _():
            st_ref[...] = jnp.zeros_like(st_ref)

        lo_h = lax.broadcasted_iota(jnp.int32, (sub, LANES), 1) < GLA_DK
        tril = tri_ref[...] > 0.5
        tril_b = tri_ref[...].astype(BF)
        ones_b = jnp.ones((LANES, LANES), BF)
        gg, wd, bd = gg_ref[...], wd_ref[...], bd_ref[...]
        chunks = _chunks(sub)
        lanes = [slice(h * LANES, (h + 1) * LANES) for h in range(GLA_HEADS)]
        com = [_gla_block_common(qk_ref[sl, :], ga_ref[sl, :], wd, bd, tril_b) for sl in subs]
        decs = [[jnp.exp(bl) for bl in cm[7]] for cm in com]
        a = {(i, h): _head(com[i][4], h, lo_h).astype(BF) for i, h in units}
        bm = {(i, h): _head(com[i][5], h, lo_h).astype(BF) for i, h in units}
        ktl = {(i, h): _head(com[i][6], h, lo_h).astype(BF) for i, h in units}
        vh = {(i, h): v_ref[subs[i], lanes[h]].astype(BF) for i, h in units}
        sc = {u: _dot(a[u], bm[u], NT) for u in units}
        upd = {u: [_dot(vh[u][rs], ktl[u][rs], TN) for rs in chunks] for u in units}
        p = {u: jnp.where(tril, sc[u], 0.0).astype(BF) for u in units}
        o = {u: _dot(p[u], vh[u]) for u in units}
        states = {}
        for h in range(GLA_HEADS):
            st = st_ref[h]
            for i in range(len(subs)):
                entering = []
                for c in range(nch):
                    entering.append(st)
                    sprev_ref[i * nch + c, h] = st
                    st = st * decs[i][c][:, LANES * (h // 2):LANES * (h // 2 + 1)] + upd[(i, h)][c]
                states[(i, h)] = entering
            st_ref[h] = st
        inter = {u: [_dot(a[u][rs], states[u][c].astype(BF), NT) for c, rs in enumerate(chunks)] for u in units}
        o = {u: o[u] + jnp.concatenate(inter[u], axis=0) for u in units}
        ms = {u: _lane_mean(o[u] * o[u], ones_b) for u in units}
        for i, h in units:
            gzh = gz_ref[subs[i], lanes[h]]
            opre_ref[subs[i], lanes[h]] = o[(i, h)]
            og_ref[subs[i], lanes[h]] = (((o[(i, h)] * lax.rsqrt(ms[(i, h)] + RMS_EPS)) * gg[:, lanes[h]])
                                         * (gzh * _sigmoid(gzh))).astype(og_ref.dtype)

    def col(width, off):
        return pl.BlockSpec((tb, width), lambda i: (i, off // width))

    return pl.pallas_call(
        body, name="gla_fwd", grid=(s // tb,),
        in_specs=[col(512, OFF_QK), col(512, OFF_V), col(512, OFF_GZ), col(LANES, OFF_GA),
                  _full((LANES, 256)), _full((1, 256)), _full((1, 512)), _full((sub, sub))],
        out_specs=[pl.BlockSpec((tb, 512), lambda i: (i, 0)), pl.BlockSpec((tb, 512), lambda i: (i, 0)),
                   pl.BlockSpec((tb // GLA_CHUNK, GLA_HEADS, LANES, LANES), lambda i: (i, 0, 0, 0))],
        out_shape=[jax.ShapeDtypeStruct((s, 512), BF), jax.ShapeDtypeStruct((s, 512), F32),
                   jax.ShapeDtypeStruct((s // GLA_CHUNK, GLA_HEADS, LANES, LANES), F32)],
        scratch_shapes=[pltpu.VMEM((GLA_HEADS, LANES, LANES), F32)],
        compiler_params=_params(("arbitrary",)),
    )(proj, proj, proj, proj, wdecp, bdec, ggla, _gla_triangle(sub))


def _gla_bwd(proj, dog, opre, sprev, wdecp, bdec, ggla):
    s = proj.shape[0]
    tb, sub, subs, units = _gla_units(s, GLA_ROWS_BWD)
    nsub = len(subs)
    nch = sub // GLA_CHUNK
    nb = s // tb

    def body(qk_ref, v_ref, gz_ref, ga_ref, dog_ref, opre_ref, sprev_ref, wd_ref, bd_ref, gg_ref, tri_ref, triu_ref,
             dqk_ref, dv_ref, dgz_ref, dga_ref, dwd_ref, dbd_ref, dgg_ref, dst_ref):
        @pl.when(pl.program_id(0) == 0)
        def _():
            dst_ref[...] = jnp.zeros_like(dst_ref)
            dwd_ref[...] = jnp.zeros_like(dwd_ref)
            dbd_ref[...] = jnp.zeros_like(dbd_ref)
            dgg_ref[...] = jnp.zeros_like(dgg_ref)

        lo_h = lax.broadcasted_iota(jnp.int32, (sub, LANES), 1) < GLA_DK
        tril = tri_ref[...] > 0.5
        tril_b = tri_ref[...].astype(BF)
        triu_b = triu_ref[...].astype(BF)
        ones_b = jnp.ones((LANES, LANES), BF)
        last_row = (lax.broadcasted_iota(jnp.int32, (sub, LANES), 0) % GLA_CHUNK) == GLA_CHUNK - 1
        wd, gg, bd = wd_ref[...], gg_ref[...], bd_ref[...]
        chunks = _chunks(sub)
        lanes = [slice(h * LANES, (h + 1) * LANES) for h in range(GLA_HEADS)]
        blks = [slice(LANES * (h // 2), LANES * (h // 2 + 1)) for h in range(GLA_HEADS)]
        ga = [ga_ref[sl, :] for sl in subs]
        com = [_gla_block_common(qk_ref[sl, :], ga[i], wd, bd, tril_b) for i, sl in enumerate(subs)]
        decs = [[jnp.exp(bl) for bl in cm[7]] for cm in com]
        a = {(i, h): _head(com[i][4], h, lo_h).astype(BF) for i, h in units}
        bm = {(i, h): _head(com[i][5], h, lo_h).astype(BF) for i, h in units}
        ktl = {(i, h): _head(com[i][6], h, lo_h).astype(BF) for i, h in units}
        vh = {(i, h): v_ref[subs[i], lanes[h]].astype(BF) for i, h in units}
        sc = {u: _dot(a[u], bm[u], NT) for u in units}

        o = {(i, h): opre_ref[subs[i], lanes[h]] for i, h in units}
        ms = {u: _lane_mean(o[u] * o[u], ones_b) for u in units}
        gz = {(i, h): gz_ref[subs[i], lanes[h]] for i, h in units}
        dog = {(i, h): dog_ref[subs[i], lanes[h]] for i, h in units}
        sg = {u: _sigmoid(gz[u]) for u in units}
        r = {u: lax.rsqrt(ms[u] + RMS_EPS) for u in units}
        ohat = {u: o[u] * r[u] for u in units}
        sil = {u: gz[u] * sg[u] for u in units}
        for i, h in units:
            u = (i, h)
            dgz_ref[subs[i], lanes[h]] = (dog[u] * (ohat[u] * gg[:, lanes[h]])
                                          * (sg[u] * (1.0 + gz[u] * (1.0 - sg[u])))).astype(dgz_ref.dtype)
            dgg_ref[:, lanes[h]] += jnp.sum(dog[u] * sil[u] * ohat[u], axis=0, keepdims=True)
        dn = {(i, h): dog[(i, h)] * sil[(i, h)] * gg[:, lanes[h]] for i, h in units}
        mdn = {u: _lane_mean(dn[u] * ohat[u], ones_b) for u in units}
        do = {u: (r[u] * (dn[u] - ohat[u] * mdn[u])).astype(BF) for u in units}

        p = {u: jnp.where(tril, sc[u], 0.0).astype(BF) for u in units}
        dpr = {u: _dot(do[u], vh[u], NT) for u in units}
        incr = {u: [_dot(do[u][rs], a[u][rs], TN) for rs in chunks] for u in units}
        dv = {u: _dot(p[u], do[u], TN) for u in units}
        dp = {u: jnp.where(tril, dpr[u], 0.0).astype(BF) for u in units}
        dqd = {u: _dot(dp[u], bm[u]) for u in units}
        dkd = {u: _dot(dp[u], a[u], TN) for u in units}
        st = {(i, h): [sprev_ref[i * nch + c, h] for c in range(nch)] for i, h in units}
        leaving = {}
        for h in range(GLA_HEADS):
            d = dst_ref[h]
            for i in reversed(range(nsub)):
                out = [None] * nch
                for c in reversed(range(nch)):
                    out[c] = d
                    d = d * decs[i][c][:, blks[h]] + incr[(i, h)][c]
                leaving[(i, h)] = out
            dst_ref[h] = d
        lv_b = {u: [leaving[u][c].astype(BF) for c in range(nch)] for u in units}
        dv_s = {u: [_dot(ktl[u][rs], lv_b[u][c], NT) for c, rs in enumerate(chunks)] for u in units}
        dqd_s = {u: [_dot(do[u][rs], st[u][c].astype(BF)) for c, rs in enumerate(chunks)] for u in units}
        dkt_s = {u: [_dot(vh[u][rs], lv_b[u][c]) for c, rs in enumerate(chunks)] for u in units}
        ddec = {u: [jnp.sum(leaving[u][c] * st[u][c], axis=0, keepdims=True) for c in range(nch)] for u in units}
        for i, h in units:
            dv_ref[subs[i], lanes[h]] = (dv[(i, h)] + jnp.concatenate(dv_s[(i, h)], axis=0)).astype(dv_ref.dtype)
        dqd = {u: dqd[u] + jnp.concatenate(dqd_s[u], axis=0) for u in units}
        dkt = {u: jnp.concatenate(dkt_s[u], axis=0) for u in units}

        db = []
        for i, sl in enumerate(subs):
            _, eq, ek, f, qd, kd, kt, _ = com[i]
            parts = []
            for pair in range(GLA_HEADS // 2):
                blk, u0, u1 = blks[2 * pair], (i, 2 * pair), (i, 2 * pair + 1)
                dqd_b, dkd_b, dkt_b = dqd[u0] + dqd[u1], dkd[u0] + dkd[u1], dkt[u0] + dkt[u1]
                dqk_ref[sl, blk] = (dqd_b * (eq[:, blk] * GLA_DK ** -0.5)).astype(dqk_ref.dtype)
                dqk_ref[sl, 256 + LANES * pair:256 + LANES * (pair + 1)] = (dkd_b * ek[:, blk] + dkt_b * f[:, blk]).astype(dqk_ref.dtype)
                dkt_kt = dkt_b * kt[:, blk]
                dbp = dqd_b * qd[:, blk] - dkd_b * kd[:, blk] - dkt_kt
                dbl = [jnp.sum(dkt_kt[rs], axis=0, keepdims=True) + (ddec[u0][c] + ddec[u1][c]) * decs[i][c][:, blk]
                       for c, rs in enumerate(chunks)]
                parts.append(jnp.where(last_row, dbp + _per_chunk_rows(dbl, LANES), dbp))
            db.append(jnp.concatenate(parts, axis=1))
        dla = [_tri_matmul(triu_b, db[i]) for i in range(nsub)]
        dz32 = [dla[i] * (1.0 / GLA_TAU) * _sigmoid(-com[i][0]) for i in range(nsub)]
        dz = [t.astype(BF) for t in dz32]
        for i, sl in enumerate(subs):
            dga_ref[sl, :] = _dot(dz[i], wd, NT).astype(dga_ref.dtype)
            dwd_ref[...] += _dot(ga[i].astype(BF), dz[i], TN)
            dbd_ref[...] += jnp.sum(dz32[i], axis=0, keepdims=True)

    def col(width, off):
        return pl.BlockSpec((tb, width), lambda i: (nb - 1 - i, off // width))

    def rev(width):
        return pl.BlockSpec((tb, width), lambda i: (nb - 1 - i, 0))

    return pl.pallas_call(
        body, name="gla_bwd", grid=(nb,),
        in_specs=[col(512, OFF_QK), col(512, OFF_V), col(512, OFF_GZ), col(LANES, OFF_GA), rev(512), rev(512),
                  pl.BlockSpec((tb // GLA_CHUNK, GLA_HEADS, LANES, LANES), lambda i: (nb - 1 - i, 0, 0, 0)),
                  _full((LANES, 256)), _full((1, 256)), _full((1, 512)), _full((sub, sub)), _full((sub, sub))],
        out_specs=[rev(512), rev(512), rev(512), rev(LANES), _full((LANES, 256)), _full((1, 256)), _full((1, 512))],
        out_shape=[jax.ShapeDtypeStruct((s, 512), BF), jax.ShapeDtypeStruct((s, 512), BF),
                   jax.ShapeDtypeStruct((s, 512), BF), jax.ShapeDtypeStruct((s, LANES), BF),
                   jax.ShapeDtypeStruct((LANES, 256), F32), jax.ShapeDtypeStruct((1, 256), F32),
                   jax.ShapeDtypeStruct((1, 512), F32)],
        scratch_shapes=[pltpu.VMEM((GLA_HEADS, LANES, LANES), F32)],
        compiler_params=_params(("arbitrary",)),
    )(proj, proj, proj, proj, dog, opre, sprev, wdecp, bdec, ggla, _gla_triangle(sub), _gla_triangle(sub).T)


_SWA_COL_HEADS = (0, 2, 1, 3, 4, 6, 5, 7)
_SWA_COLS = SWA_HEADS * SWA_BLOCK


def _swa_masks():
    lo2 = lax.broadcasted_iota(jnp.int32, (2 * SWA_BLOCK, LANES), 1) < 64
    lane1 = lax.broadcasted_iota(jnp.int32, (SWA_BLOCK, LANES), 1)
    first_half = (lane1 % 64) < 32
    key = lax.broadcasted_iota(jnp.int32, (SWA_BLOCK, _SWA_COLS), 0)
    query = lax.broadcasted_iota(jnp.int32, (SWA_BLOCK, _SWA_COLS), 1) % SWA_BLOCK
    return lo2, lane1 < 64, first_half, key > query


def _merge_band(t, prev_mask, prev_bias=None):
    prev = t[:SWA_BLOCK] if prev_bias is None else t[:SWA_BLOCK] + prev_bias
    return jnp.where(prev_mask, prev, t[SWA_BLOCK:])


def _split_band(t, prev_mask_b):
    prev = t * prev_mask_b
    return jnp.concatenate([prev, t - prev], axis=0)


def _kv_variants(t, lo2):
    tr = pltpu.roll(t, 64, 1)
    lo_v = [jnp.where(lo2, t, 0.0).astype(BF), jnp.where(lo2, tr, 0.0).astype(BF)]
    hi_v = [jnp.where(lo2, 0.0, tr).astype(BF), jnp.where(lo2, 0.0, t).astype(BF)]
    return lo_v, hi_v


def _kv_variants_t(t):
    tt = t.T
    sw = jnp.concatenate([tt[64:], tt[:64]], axis=0)
    top = lax.broadcasted_iota(jnp.int32, tt.shape, 0) < 64
    lo_v = [jnp.where(top, tt, 0.0).astype(BF), jnp.where(top, sw, 0.0).astype(BF)]
    hi_v = [jnp.where(top, 0.0, sw).astype(BF), jnp.where(top, 0.0, tt).astype(BF)]
    return lo_v, hi_v


def _swa_scores(qg, k_lo, k_hi):
    return jnp.concatenate([_dot(k_lo[0], qg[0], NT), _dot(k_hi[0], qg[0], NT),
                            _dot(k_lo[1], qg[1], NT), _dot(k_hi[1], qg[1], NT)], axis=1)


def _sink_row(sinks_ref):
    return jnp.concatenate([jnp.full((1, SWA_BLOCK), sinks_ref[0, hd], F32) for hd in _SWA_COL_HEADS], axis=1)


def _swa_softmax(st, prev_mask, prev_bias, sink):
    st = _merge_band(st, prev_mask, prev_bias)
    m = jnp.maximum(jnp.max(st, axis=0, keepdims=True), sink)
    ex = jnp.exp(st - m)
    es = jnp.exp(sink - m)
    inv = 1.0 / (jnp.sum(ex, axis=0, keepdims=True) + es)
    return ex, es, inv


def _no_prev_bias(block_index):
    return jnp.where(block_index > 0, 0.0, -1e30).astype(F32)


def _swa_queries(sq_ref, rows, cosb, sinb, first_half):
    qs = [_rope(sq_ref[rows, p * LANES:(p + 1) * LANES], cosb, sinb, first_half) * 0.125 for p in range(4)]
    return [jnp.concatenate(qs[0:2], axis=0), jnp.concatenate(qs[2:4], axis=0)]


def _phase_steps(nsteps, phases):
    return [min(nsteps - 1, (k * nsteps) // phases) for k in range(phases - 1)] + [nsteps - 1]


def _swa_fwd(proj, cos, sin, sinks, half_out):
    s = proj.shape[0]
    nq = min(SWA_QBLOCKS_FWD, s // SWA_BLOCK)
    tq = nq * SWA_BLOCK
    steps = _phase_steps(s // tq, 4)

    def body(sq_ref, sz_ref, sk_ref, sv_ref, cos_ref, sin_ref, sinks_ref, hout_hbm, os_ref, opre_ref, wout_hbm,
             kprev, vprev, *gather_sems):
        n = pl.program_id(0)

        @pl.when(n == 0)
        def _():
            kprev[...] = jnp.zeros_like(kprev)
            vprev[...] = jnp.zeros_like(vprev)

        gather = _Gather(hout_hbm, wout_hbm, *gather_sems, chunks=WEIGHT_CHUNKS)
        for step, phase in zip(steps, (gather.start, gather.pass_on, gather.relay_diagonal, gather.finish)):
            pl.when(n == step)(phase)

        lo2, _, first_half, prev_mask = _swa_masks()
        prev_mask_b = jnp.where(prev_mask, 1.0, 0.0).astype(BF)
        sink = _sink_row(sinks_ref)
        blocks = range(nq)
        rows = [slice(j * SWA_BLOCK, (j + 1) * SWA_BLOCK) for j in blocks]
        cosb = [cos_ref[rows[j], :] for j in blocks]
        sinb = [sin_ref[rows[j], :] for j in blocks]
        kc = [_rope(sk_ref[rows[j], :], cosb[j], sinb[j], first_half) for j in blocks]
        vc = [sv_ref[rows[j], :] for j in blocks]
        kcat = [jnp.concatenate([kprev[...] if j == 0 else kc[j - 1], kc[j]], axis=0) for j in blocks]
        vcat = [jnp.concatenate([vprev[...] if j == 0 else vc[j - 1], vc[j]], axis=0) for j in blocks]
        kprev[...] = kc[-1]
        vprev[...] = vc[-1]
        kvar = [_kv_variants(kcat[j], lo2) for j in blocks]
        vtvar = [_kv_variants_t(vcat[j]) for j in blocks]
        qg = [[q.astype(BF) for q in _swa_queries(sq_ref, rows[j], cosb[j], sinb[j], first_half)] for j in blocks]
        st = [_swa_scores(qg[j], *kvar[j]) for j in blocks]
        soft = [_swa_softmax(st[j], prev_mask, _no_prev_bias(n) if j == 0 else None, sink) for j in blocks]
        pt = [_split_band(soft[j][0].astype(BF), prev_mask_b) for j in blocks]
        og = {}
        for j in blocks:
            inv = soft[j][2]
            for g in range(2):
                c0, c1, c2 = 512 * g, 512 * g + 256, 512 * g + 512
                ot = (_dot(vtvar[j][0][g], pt[j][:, c0:c1]) * inv[:, c0:c1]
                      + _dot(vtvar[j][1][g], pt[j][:, c1:c2]) * inv[:, c1:c2])
                og[(j, g)] = ot.T
        for j in blocks:
            for g in range(2):
                for i in range(2):
                    ls = slice((2 * g + i) * LANES, (2 * g + i + 1) * LANES)
                    o = og[(j, g)][i * SWA_BLOCK:(i + 1) * SWA_BLOCK]
                    sz = sz_ref[rows[j], ls]
                    opre_ref[rows[j], ls] = o
                    os_ref[rows[j], ls] = (o * (sz * _sigmoid(sz))).astype(os_ref.dtype)

    def col(width, off):
        return pl.BlockSpec((tq, width), lambda i: (i, off // width))

    row = pl.BlockSpec((tq, LANES), lambda i: (i, 0))
    return pl.pallas_call(
        body, name="swa_fwd", grid=(s // tq,),
        in_specs=[col(512, OFF_SQ), col(512, OFF_SZ), col(LANES, OFF_SK), col(LANES, OFF_SV), row, row,
                  pl.BlockSpec(memory_space=pltpu.SMEM), pl.BlockSpec(memory_space=pl.ANY)],
        out_specs=[pl.BlockSpec((tq, 512), lambda i: (i, 0))] * 2 + [pl.BlockSpec(memory_space=pl.ANY)],
        out_shape=[jax.ShapeDtypeStruct((s, 512), BF), jax.ShapeDtypeStruct((s, 512), F32),
                   jax.ShapeDtypeStruct((8,) + half_out.shape, half_out.dtype)],
        scratch_shapes=[pltpu.VMEM((SWA_BLOCK, LANES), F32)] * 2 + _gather_sems(WEIGHT_CHUNKS),
        compiler_params=_params(("arbitrary",)),
    )(proj, proj, proj, proj, cos, sin, sinks, half_out)


def _swa_bwd(proj, dos, opre, cos, sin, sinks, dw_out_parts):
    s = proj.shape[0]
    nq = min(SWA_QBLOCKS, s // SWA_BLOCK)
    tq = nq * SWA_BLOCK
    steps = _phase_steps(s // tq, 5)
    _, r_out, c_out = dw_out_parts.shape

    def body(sq_ref, sz_ref, sk_ref, sv_ref, dos_ref, opre_ref, cos_ref, sin_ref, sinks_ref, pout_hbm,
             dsq_ref, dsz_ref, dsk_ref, dsv_ref, dsink_ref, gout_hbm, kprev, vprev, cprev, sprev, *reduce_scratch):
        n = pl.program_id(0)

        @pl.when(n == 0)
        def _():
            kprev[...] = jnp.zeros_like(kprev)
            vprev[...] = jnp.zeros_like(vprev)
            cprev[...] = jnp.zeros_like(cprev)
            sprev[...] = jnp.zeros_like(sprev)
            for hd in range(SWA_HEADS):
                dsink_ref[0, hd] = 0.0

        reduce = _Reduce(pout_hbm, gout_hbm, *reduce_scratch)
        phases = (reduce.start, reduce.combine_and_send, reduce.send_joint, reduce.total_and_share, reduce.finish)
        for step, phase in zip(steps, phases):
            pl.when(n == step)(phase)

        lo2, lo1, first_half, prev_mask = _swa_masks()
        prev_mask_b = jnp.where(prev_mask, 1.0, 0.0).astype(BF)
        lo1s = jnp.concatenate([lo1, lo1], axis=0)
        sink = _sink_row(sinks_ref)

        def home(m0, m1):
            t0 = m0 + pltpu.roll(m0, 64, 1)
            t1 = m1 + pltpu.roll(m1, 64, 1)
            return jnp.where(lo2, t0, t1)

        kp, vp, cp_, sp_ = kprev[...], vprev[...], cprev[...], sprev[...]
        for j in range(nq):
            rows = slice(j * SWA_BLOCK, (j + 1) * SWA_BLOCK)
            blk = n * nq + j
            cosb, sinb = cos_ref[rows, :], sin_ref[rows, :]
            kc = _rope(sk_ref[rows, :], cosb, sinb, first_half)
            vc = sv_ref[rows, :]
            kcat = jnp.concatenate([kp, kc], axis=0)
            k_lo, k_hi = _kv_variants(kcat, lo2)
            kt_lo, kt_hi = _kv_variants_t(kcat)
            v_lo, v_hi = _kv_variants(jnp.concatenate([vp, vc], axis=0), lo2)
            qg32 = _swa_queries(sq_ref, rows, cosb, sinb, first_half)
            qg = [q.astype(BF) for q in qg32]
            ex, es, inv = _swa_softmax(_swa_scores(qg, k_lo, k_hi), prev_mask, _no_prev_bias(n) if j == 0 else None, sink)
            pr, ps = ex * inv, es * inv

            dog32 = []
            for g in range(2):
                parts = []
                for i in range(2):
                    ls = slice((2 * g + i) * LANES, (2 * g + i + 1) * LANES)
                    sz = sz_ref[rows, ls]
                    sg = _sigmoid(sz)
                    dos_p = dos_ref[rows, ls]
                    dsz_ref[rows, ls] = (dos_p * opre_ref[rows, ls] * (sg * (1.0 + sz * (1.0 - sg)))).astype(dsz_ref.dtype)
                    parts.append(dos_p * (sz * sg))
                dog32.append(jnp.concatenate(parts, axis=0))
            dog = [t.astype(BF) for t in dog32]
            dpr = _merge_band(jnp.concatenate([_dot(v_lo[0], dog[0], NT), _dot(v_hi[0], dog[0], NT),
                                               _dot(v_lo[1], dog[1], NT), _dot(v_hi[1], dog[1], NT)], axis=1), prev_mask)
            rd = jnp.sum(pr * dpr, axis=0, keepdims=True)
            ds = _split_band((pr * (dpr - rd)).astype(BF), prev_mask_b)
            prb = _split_band(pr.astype(BF), prev_mask_b)
            sink_term = ps * rd
            for r, hd in enumerate(_SWA_COL_HEADS):
                dsink_ref[0, hd] += -jnp.sum(sink_term[:, r * SWA_BLOCK:(r + 1) * SWA_BLOCK])

            dk_g, dv_g = [], []
            for g in range(2):
                c0, c1, c2 = 512 * g, 512 * g + 256, 512 * g + 512
                dq = (_dot(kt_lo[g], ds[:, c0:c1]) + _dot(kt_hi[g], ds[:, c1:c2])).T
                for i in range(2):
                    ls = slice((2 * g + i) * LANES, (2 * g + i + 1) * LANES)
                    dsq_ref[rows, ls] = _rope_t(dq[i * SWA_BLOCK:(i + 1) * SWA_BLOCK] * 0.125, cosb, sinb,
                                                first_half).astype(dsq_ref.dtype)
                q_split = jnp.concatenate([jnp.where(lo1s, qg32[g], 0.0), jnp.where(lo1s, 0.0, qg32[g])], axis=0).astype(BF)
                do_split = jnp.concatenate([jnp.where(lo1s, dog32[g], 0.0), jnp.where(lo1s, 0.0, dog32[g])], axis=0).astype(BF)
                dk_g.append(_dot(ds[:, c0:c2], q_split))
                dv_g.append(_dot(prb[:, c0:c2], do_split))
            dk = home(dk_g[0], dk_g[1])
            dv = home(dv_g[0], dv_g[1])
            cur = pl.ds(pl.multiple_of(blk * SWA_BLOCK, SWA_BLOCK), SWA_BLOCK)
            dsk_ref[cur, :] = _rope_t(dk[SWA_BLOCK:], cosb, sinb, first_half)
            dsv_ref[cur, :] = dv[SWA_BLOCK:]
            dk_prev = _rope_t(dk[:SWA_BLOCK], cp_, sp_, first_half)
            dv_prev = dv[:SWA_BLOCK]
            if j == 0:
                @pl.when(n > 0)
                def _():
                    prv = pl.ds(pl.multiple_of((blk - 1) * SWA_BLOCK, SWA_BLOCK), SWA_BLOCK)
                    dsk_ref[prv, :] += dk_prev
                    dsv_ref[prv, :] += dv_prev
            else:
                prv = pl.ds(pl.multiple_of((blk - 1) * SWA_BLOCK, SWA_BLOCK), SWA_BLOCK)
                dsk_ref[prv, :] += dk_prev
                dsv_ref[prv, :] += dv_prev
            kp, vp, cp_, sp_ = kc, vc, cosb, sinb
        kprev[...] = kp
        vprev[...] = vp
        cprev[...] = cp_
        sprev[...] = sp_

    def col(width, off):
        return pl.BlockSpec((tq, width), lambda i: (i, off // width))

    row = pl.BlockSpec((tq, LANES), lambda i: (i, 0))
    wide = pl.BlockSpec((tq, 512), lambda i: (i, 0))
    return pl.pallas_call(
        body, name="swa_bwd", grid=(s // tq,),
        in_specs=[col(512, OFF_SQ), col(512, OFF_SZ), col(LANES, OFF_SK), col(LANES, OFF_SV), wide, wide, row, row,
                  pl.BlockSpec(memory_space=pltpu.SMEM), pl.BlockSpec(memory_space=pl.ANY)],
        out_specs=[wide, wide, _full((s, LANES)), _full((s, LANES)), pl.BlockSpec(memory_space=pltpu.SMEM),
                   pl.BlockSpec(memory_space=pl.ANY)],
        out_shape=[jax.ShapeDtypeStruct((s, 512), BF), jax.ShapeDtypeStruct((s, 512), BF),
                   jax.ShapeDtypeStruct((s, LANES), F32), jax.ShapeDtypeStruct((s, LANES), F32),
                   jax.ShapeDtypeStruct((1, SWA_HEADS), F32), jax.ShapeDtypeStruct((r_out, c_out), F32)],
        scratch_shapes=[pltpu.VMEM((SWA_BLOCK, LANES), F32)] * 4 + _reduce_scratch(r_out, c_out),
        compiler_params=_params(("arbitrary",)),
    )(proj, proj, proj, proj, dos, opre, cos, sin, sinks, dw_out_parts)


def _outproj(og, osw, w_out, x2d, target, gate, g_final):
    s = x2d.shape[0]
    tm = min(512, s)

    def body(og_ref, os_ref, w_ref, x_ref, t_ref, gate_ref, gf_ref,
             dx2_ref, dog_ref, dos_ref, dw_ref, loss_ref, dgf_ref, dgate_ref):
        @pl.when(pl.program_id(0) == 0)
        def _():
            dw_ref[...] = jnp.zeros_like(dw_ref)
            loss_ref[...] = jnp.zeros_like(loss_ref)
            dgf_ref[...] = jnp.zeros_like(dgf_ref)
            dgate_ref[...] = jnp.zeros_like(dgate_ref)

        w = w_ref[...]
        gate, gf = gate_ref[...], gf_ref[...]
        subs = _subtiles(tm)
        ogv = [og_ref[sl, :] for sl in subs]
        osv = [os_ref[sl, :] for sl in subs]
        y = [_dot(ogv[k], w[:512]) + _dot(osv[k], w[512:]) for k in range(len(subs))]
        dys = []
        for k, sl in enumerate(subs):
            x2 = x_ref[sl, :] + gate * y[k]
            r = lax.rsqrt(jnp.mean(x2 * x2, axis=-1, keepdims=True) + RMS_EPS)
            xn = x2 * r
            err = xn * gf - t_ref[sl, :]
            loss_ref[...] += 0.5 * jnp.sum(jnp.mean(err * err, axis=-1, keepdims=True), axis=0, keepdims=True)
            dyf = err * (1.0 / D_MODEL)
            dgf_ref[...] += jnp.sum(dyf * xn, axis=0, keepdims=True)
            t = dyf * gf
            dx2 = r * (t - xn * jnp.mean(t * xn, axis=-1, keepdims=True))
            dx2_ref[sl, :] = dx2
            dgate_ref[...] += jnp.sum(dx2 * y[k], axis=0, keepdims=True)
            dys.append((dx2 * gate).astype(BF))
            dmix = _dot(dys[k], w, NT)
            dog_ref[sl, :] = dmix[:, :512]
            dos_ref[sl, :] = dmix[:, 512:]
        dy = jnp.concatenate(dys, axis=0)
        dw_ref[:512, :] += _dot(og_ref[...], dy, TN)
        dw_ref[512:, :] += _dot(os_ref[...], dy, TN)

    half = pl.BlockSpec((tm, 512), lambda i: (i, 0))
    rowb = pl.BlockSpec((tm, D_MODEL), lambda i: (i, 0))
    vec = _full((1, D_MODEL))
    return pl.pallas_call(
        body, name="outproj", grid=(s // tm,),
        in_specs=[half, half, _full((D_MODEL, D_MODEL)), rowb, rowb, vec, vec],
        out_specs=[rowb, half, half, _full((D_MODEL, D_MODEL)), _full((1, 1)), vec, vec],
        out_shape=[jax.ShapeDtypeStruct((s, D_MODEL), F32), jax.ShapeDtypeStruct((s, 512), F32),
                   jax.ShapeDtypeStruct((s, 512), F32), jax.ShapeDtypeStruct((D_MODEL, D_MODEL), F32),
                   jax.ShapeDtypeStruct((1, 1), F32), jax.ShapeDtypeStruct((1, D_MODEL), F32),
                   jax.ShapeDtypeStruct((1, D_MODEL), F32)],
        compiler_params=_params(("arbitrary",)),
    )(og, osw, w_out, x2d, target, gate, g_final)


_PIECES = ((OFF_QK, 512), (OFF_V, 512), (OFF_GZ, 512), (OFF_SQ, 512), (OFF_SZ, 512),
           (OFF_SK, LANES), (OFF_SV, LANES), (OFF_GA, LANES))

_UNPAD_ROWS = ((OFF_QK, 0, 1024),
               (OFF_GA, 1024, GLA_RANK),
               (OFF_GZ, 1040, 1024),
               (OFF_SK, 2064, 256),
               (OFF_SZ, 2320, 512))


def _inproj_bwd(x2d, shift, sc1p, g_norm, w_t, dx2, pieces):
    s = x2d.shape[0]
    tm = min(512, s)
    nsteps = s // tm

    def body(x_ref, sh_ref, sc_ref, g_ref, w_hbm, dx2_ref, *rest):
        piece_refs = rest[:len(_PIECES)]
        gx_ref, dw_hbm, dsh_ref, dsc_ref, dg_ref, w_vm, dw_vm, in_sems, out_sems = rest[len(_PIECES):]
        i = pl.program_id(0)

        @pl.when(i == 0)
        def _():
            loads = _load_w_padded(w_hbm, w_vm, in_sems)
            dw_vm[...] = jnp.zeros_like(dw_vm)
            dsh_ref[...] = jnp.zeros_like(dsh_ref)
            dsc_ref[...] = jnp.zeros_like(dsc_ref)
            dg_ref[...] = jnp.zeros_like(dg_ref)
            for cp in loads:
                cp.wait()

        g, sc1p_v, shift_v = g_ref[...], sc_ref[...], sh_ref[...]
        subs = _subtiles(tm)
        dhs = []
        for sl in subs:
            dh = None
            for (off, width), pr in zip(_PIECES, piece_refs):
                part = _dot(pr[sl, :].astype(BF), w_vm[off:off + width, :])
                dh = part if dh is None else dh + part
            dhs.append(dh)
        norm = [_modnorm(x_ref[sl, :], g, sc1p_v, shift_v) for sl in subs]
        hb = jnp.concatenate([h.astype(BF) for _, _, h in norm], axis=0)
        for (off, width), pr in zip(_PIECES, piece_refs):
            dw_vm[off:off + width, :] += _dot(pr[...].astype(BF), hb, TN)
        for sl, (xn, r, _), dh in zip(subs, norm, dhs):
            dsh_ref[...] += jnp.sum(dh, axis=0, keepdims=True)
            dsc_ref[...] += jnp.sum(dh * (xn * g), axis=0, keepdims=True)
            dg_ref[...] += jnp.sum(dh * xn * sc1p_v, axis=0, keepdims=True)
            dxn = dh * g * sc1p_v
            gx_ref[sl, :] = dx2_ref[sl, :] + r * (dxn - xn * jnp.mean(dxn * xn, axis=-1, keepdims=True))

        @pl.when(i == nsteps - 1)
        def _():
            copies = [pltpu.make_async_copy(dw_vm.at[src:src + n], dw_hbm.at[dst:dst + n], out_sems.at[k])
                      for k, (src, dst, n) in enumerate(_UNPAD_ROWS)]
            for cp in copies:
                cp.start()
            for cp in copies:
                cp.wait()

    rowb = pl.BlockSpec((tm, D_MODEL), lambda i: (i, 0))
    vec = _full((1, D_MODEL))
    anyspec = pl.BlockSpec(memory_space=pl.ANY)
    piece_specs = [pl.BlockSpec((tm, width), lambda i: (i, 0)) for _, width in _PIECES]
    return pl.pallas_call(
        body, name="inproj_bwd", grid=(nsteps,),
        in_specs=[rowb, vec, vec, vec, anyspec, rowb] + piece_specs,
        out_specs=[rowb, anyspec, vec, vec, vec],
        out_shape=[jax.ShapeDtypeStruct((s, D_MODEL), F32), jax.ShapeDtypeStruct((D_IN, D_MODEL), F32),
                   jax.ShapeDtypeStruct((1, D_MODEL), F32), jax.ShapeDtypeStruct((1, D_MODEL), F32),
                   jax.ShapeDtypeStruct((1, D_MODEL), F32)],
        scratch_shapes=[pltpu.VMEM((D_PAD, D_MODEL), BF), pltpu.VMEM((D_PAD, D_MODEL), F32),
                        pltpu.SemaphoreType.DMA((len(_UNPAD_ROWS),)), pltpu.SemaphoreType.DMA((len(_UNPAD_ROWS),))],
        compiler_params=_params(("arbitrary",)),
    )(x2d, shift, sc1p, g_norm, w_t, dx2, *pieces)


def _adam(w, g, m, v):
    m2 = ADAM_B1 * m + (1.0 - ADAM_B1) * g
    v2 = ADAM_B2 * v + (1.0 - ADAM_B2) * (g * g)
    m_hat = m2 / (1.0 - ADAM_B1 ** ADAM_STEP)
    v_hat = v2 / (1.0 - ADAM_B2 ** ADAM_STEP)
    delta = -ADAM_LR * (m_hat / (jnp.sqrt(v_hat) + ADAM_EPS) + ADAM_WD * w)
    return delta, m2, v2


def _adamw(w, g, m, v, name):
    rr, cc = w.shape
    tc = min(512, cc)

    def body(w_ref, g_ref, m_ref, v_ref, d_ref, m2_ref, v2_ref):
        d_ref[...], m2_ref[...], v2_ref[...] = _adam(w_ref[...], g_ref[...], m_ref[...], v_ref[...])

    blk = pl.BlockSpec((rr, tc), lambda i: (0, i))
    return pl.pallas_call(
        body, name=name, grid=(cc // tc,), in_specs=[blk] * 4, out_specs=[blk] * 3,
        out_shape=[jax.ShapeDtypeStruct((rr, cc), F32)] * 3,
        compiler_params=_params(("arbitrary",)),
    )(w, g, m, v)


def _adamw_t(w3, g, m3, v3, name):
    rr, _, cc = w3.shape
    tc = cc

    def body(w_hbm, g_ref, m_hbm, v_hbm, d_hbm, m2_hbm, v2_hbm, g3_hbm, w_vm, m_vm, v_vm, d_vm, m2_vm, v2_vm, in_sems, out_sems):
        cols = pl.ds(pl.multiple_of(pl.program_id(0) * tc, tc), tc)
        loads = [pltpu.make_async_copy(src.at[:, 0, cols], dst, in_sems.at[k])
                 for k, (src, dst) in enumerate(((w_hbm, w_vm), (m_hbm, m_vm), (v_hbm, v_vm)))]
        for cp in loads:
            cp.start()
        for cp in loads:
            cp.wait()
        d_vm[...], m2_vm[...], v2_vm[...] = _adam(w_vm[...], g_ref[...], m_vm[...], v_vm[...])
        stores = [pltpu.make_async_copy(src, dst.at[:, 0, cols], out_sems.at[k])
                  for k, (src, dst) in enumerate(((d_vm, d_hbm), (m2_vm, m2_hbm), (v2_vm, v2_hbm), (g_ref, g3_hbm)))]
        for cp in stores:
            cp.start()
        for cp in stores:
            cp.wait()

    hbm = pl.BlockSpec(memory_space=pl.ANY)
    return pl.pallas_call(
        body, name=name, grid=(cc // tc,), in_specs=[hbm, pl.BlockSpec((rr, tc), lambda i: (0, i)), hbm, hbm],
        out_specs=[hbm] * 4, out_shape=[jax.ShapeDtypeStruct((rr, 1, cc), F32)] * 4,
        scratch_shapes=[pltpu.VMEM((rr, tc), F32)] * 6 + [pltpu.SemaphoreType.DMA((3,)), pltpu.SemaphoreType.DMA((4,))],
        compiler_params=_params(("arbitrary",)),
    )(w3, g, m3, v3)


def _ada_update(c_all, dmod_cols, w, m, v):
    rr, cc = w.shape
    tr = min(512, rr)
    c_all = jnp.pad(c_all, ((0, 8), (0, 0)))
    dmod_cols = jnp.pad(dmod_cols, ((0, 8), (0, 0)))

    def body(c_ref, dm_ref, w_ref, m_ref, v_ref, g_ref, d_ref, m2_ref, v2_ref):
        cv = c_ref[...]
        sc = (cv * _sigmoid(cv)).astype(BF)
        g = _dot(sc, dm_ref[...].astype(BF), TN)
        g_ref[...] = g
        d_ref[...], m2_ref[...], v2_ref[...] = _adam(w_ref[...], g, m_ref[...], v_ref[...])

    blk = pl.BlockSpec((tr, cc), lambda i: (i, 0))
    return pl.pallas_call(
        body, name="ada_update", grid=(rr // tr,),
        in_specs=[pl.BlockSpec((16, tr), lambda i: (0, i)), _full((16, cc)), blk, blk, blk],
        out_specs=[blk] * 4, out_shape=[jax.ShapeDtypeStruct((rr, cc), F32)] * 4,
        compiler_params=_params(("arbitrary",)),
    )(c_all, dmod_cols, w, m, v)


def _small_update(parts, weights, moms, vels):
    n = len(weights)

    def body(*refs):
        p_refs, w_refs, m_refs, v_refs = refs[:n + 1], refs[n + 1:2 * n + 1], refs[2 * n + 1:3 * n + 1], refs[3 * n + 1:4 * n + 1]
        outs = refs[4 * n + 1:]
        for i in range(n):
            g = p_refs[i][0]
            for d in range(1, 8):
                g = g + p_refs[i][d]
            delta, m2, v2 = _adam(w_refs[i][...], g, m_refs[i][...], v_refs[i][...])
            outs[4 * i][...] = g
            outs[4 * i + 1][...] = delta
            outs[4 * i + 2][...] = m2
            outs[4 * i + 3][...] = v2
        tot = p_refs[n][0]
        for d in range(1, 8):
            tot = tot + p_refs[n][d]
        outs[4 * n][...] = tot

    out_shape = []
    for w in weights:
        out_shape += [jax.ShapeDtypeStruct(w.shape, F32)] * 4
    out_shape.append(jax.ShapeDtypeStruct(parts[n].shape[1:], F32))
    return pl.pallas_call(body, name="small_update", out_shape=out_shape, compiler_params=_params())(
        *parts, *weights, *moms, *vels)


def _rows8(a):
    flat = a.reshape(-1)
    rows = -(-flat.shape[0] // LANES)
    rows8 = -(-rows // 8) * 8
    flat = jnp.pad(flat, (0, rows8 * LANES - flat.shape[0]))
    return flat.reshape(rows8, LANES)


def kernel(x, c, positions, w_ada, b_ada, g_norm, w_in, w_decay, b_decay, g_gla_head, sinks, w_out, g_final, loss_target, m_w_ada, m_b_ada, m_g_norm, m_w_in, m_w_decay, m_b_decay, m_g_gla_head, m_sinks, m_w_out, m_g_final, v_w_ada, v_b_ada, v_g_norm, v_w_in, v_w_decay, v_b_decay, v_g_gla_head, v_sinks, v_w_out, v_g_final):
    ax, ay, ac = lax.axis_index("x"), lax.axis_index("y"), lax.axis_index("c")
    chip = 2 * ax + ay
    dev = 2 * chip + ac
    s = x.shape[1]
    x2d = x[0]
    target = loss_target[0]
    w_ada2, w_out2, w_dec2 = w_ada[0], w_out[0], w_decay[0]
    w_in_t = w_in[0].T
    ada_cols = w_ada2.shape[1]
    in_cols = w_in_t.shape[0]
    out_rows = w_out2.shape[0]
    half = D_MODEL // 2

    cw = jnp.concatenate([c.reshape(8, LANES), w_dec2.reshape(8, LANES)], axis=0)
    b_shard = lax.dynamic_slice(b_ada, (0, chip * ada_cols), (1, ada_cols))
    half_in = lax.dynamic_slice(w_in_t, (0, ac * half), (in_cols, half)).astype(BF)
    half_out = lax.dynamic_slice(w_out2, (ac * (out_rows // 2), 0), (out_rows // 2, D_MODEL)).astype(BF)
    inv_freq = 1.0 / (ROPE_THETA ** (jnp.arange(0, 64, 2, dtype=F32) / 64))
    first, mod_all, w_in_all, cos, sin = _prologue(
        cw, w_ada2, b_shard, half_in, positions.reshape(s, 1), jnp.tile(inv_freq, 4).reshape(1, LANES))

    first = first.reshape(8, 2, 8, LANES)
    c_all = first[:, 0].reshape(8, D_MODEL)
    w_dec_full = first[0::2, 1].reshape(4, GLA_RANK, 64).transpose(1, 0, 2).reshape(GLA_RANK, 256)
    mod = mod_all.reshape(4, 2, 8, ada_cols)[:, 0]
    mod = lax.dynamic_slice(mod, (0, dev, 0), (4, 1, ada_cols)).reshape(1, 4 * ada_cols)
    shift, sc1p, gate = mod[:, :D_MODEL], 1.0 + mod[:, D_MODEL:2 * D_MODEL], mod[:, 2 * D_MODEL:]
    w_t = w_in_all.reshape(4 * in_cols, D_MODEL)

    wdecp = jnp.pad(w_dec_full, ((0, LANES - GLA_RANK), (0, 0))).astype(BF)

    proj = _inproj_fwd(x2d, shift, sc1p, g_norm, w_t)
    og, o_gla, sprev = _gla_fwd(proj, wdecp, b_decay, g_gla_head)
    osw, o_swa, w_out_all = _swa_fwd(proj, cos, sin, sinks, half_out)
    w_out_all = w_out_all.reshape(D_MODEL, D_MODEL)
    dx2, dog, dos, dw_out, loss_p, dgf, dgate = _outproj(og, osw, w_out_all, x2d, target, gate, g_final.reshape(1, D_MODEL))
    dsq, dsz, dsk, dsv, dsinks, g_w_out = _swa_bwd(proj, dos, o_swa, cos, sin, sinks, dw_out.reshape(4, out_rows, D_MODEL))
    dqk, dv, dgz, dga, dwdp, dbd, dgg = _gla_bwd(proj, dog, o_gla, sprev, wdecp, b_decay, g_gla_head)
    pieces = (dqk, dv, dgz, dsq, dsz, dsk, dsv, dga)
    gx, dw_in_t, dshift, dscale, dgn = _inproj_bwd(x2d, shift, sc1p, g_norm, w_t, dx2, pieces)

    segs = [jnp.concatenate([dshift, dscale, dgate], axis=1), dgn, dgf, dwdp[:GLA_RANK], dbd, dgg, dsinks, loss_p]
    packed = [_rows8(a) for a in segs]
    offs = [0]
    for a in packed:
        offs.append(offs[-1] + a.shape[0])
    g_w_in_t, small = _epilogue(dw_in_t.reshape(4, in_cols, D_MODEL), jnp.concatenate(packed, axis=0))

    def seg(i, size):
        return small[:, offs[i]:offs[i + 1]].reshape(8, -1)[:, :size]

    dmod_all = seg(0, 3 * D_MODEL)
    dwd_all = lax.dynamic_slice(seg(3, GLA_RANK * 256).reshape(8, GLA_RANK, 256), (0, 0, chip * 64), (8, GLA_RANK, 64))
    parts = [dmod_all.reshape(8, 1, 3 * D_MODEL), seg(1, D_MODEL).reshape(8, 1, D_MODEL), dwd_all,
             seg(4, 256).reshape(8, 1, 256), seg(5, 512).reshape(8, 1, 512), seg(6, SWA_HEADS).reshape(8, 1, SWA_HEADS),
             seg(2, D_MODEL).reshape(8, 1, D_MODEL), seg(7, LANES).reshape(8, 1, LANES)]
    smalls = _small_update(
        parts,
        [b_ada, g_norm, w_dec2, b_decay, g_gla_head, sinks, g_final.reshape(1, D_MODEL)],
        [m_b_ada, m_g_norm, m_w_decay[0], m_b_decay, m_g_gla_head, m_sinks, m_g_final.reshape(1, D_MODEL)],
        [v_b_ada, v_g_norm, v_w_decay[0], v_b_decay, v_g_gla_head, v_sinks, v_g_final.reshape(1, D_MODEL)])
    (g_b_ada, d_b_ada, nm_b_ada, nv_b_ada, g_gn, d_gn, nm_gn, nv_gn, g_wd, d_wd, nm_wd, nv_wd,
     g_bd, d_bd, nm_bd, nv_bd, g_gg, d_gg, nm_gg, nv_gg, g_sk, d_sk, nm_sk, nv_sk,
     g_gf, d_gf, nm_gf, nv_gf, loss_row) = smalls
    loss = loss_row[0, 0]

    dmod_cols = lax.dynamic_slice(dmod_all, (0, chip * ada_cols), (8, ada_cols))
    g_w_ada, d_w_ada, nm_w_ada, nv_w_ada = _ada_update(c_all, dmod_cols, w_ada2, m_w_ada[0], v_w_ada[0])
    to3 = lambda a: jnp.transpose(a, (2, 0, 1))
    from3 = lambda a: jnp.transpose(a, (1, 2, 0))[0]
    d3, nm3, nv3, g3 = _adamw_t(to3(w_in), g_w_in_t, to3(m_w_in), to3(v_w_in), "adamw_w_in")
    g_w_in, d_w_in, nm_w_in, nv_w_in = from3(g3), from3(d3), from3(nm3), from3(nv3)
    d_w_out, nm_w_out, nv_w_out = _adamw(w_out2, g_w_out, m_w_out[0], v_w_out[0], "adamw_w_out")

    flat = lambda a: a.reshape(D_MODEL)
    grads = [g_w_ada[None], g_b_ada, g_gn, g_w_in[None], g_wd[None], g_bd, g_gg, g_sk, g_w_out[None], flat(g_gf)]
    deltas = [d_w_ada[None], d_b_ada, d_gn, d_w_in[None], d_wd[None], d_bd, d_gg, d_sk, d_w_out[None], flat(d_gf)]
    new_m = [nm_w_ada[None], nm_b_ada, nm_gn, nm_w_in[None], nm_wd[None], nm_bd, nm_gg, nm_sk, nm_w_out[None], flat(nm_gf)]
    new_v = [nv_w_ada[None], nv_b_ada, nv_gn, nv_w_in[None], nv_wd[None], nv_bd, nv_gg, nv_sk, nv_w_out[None], flat(nv_gf)]
    return (loss, gx[None], *grads, *deltas, *new_m, *new_v)
```

```python
import jax
import jax.numpy as jnp
from jax import lax
from jax.experimental import pallas as pl
from jax.experimental.pallas import tpu as pltpu

F32 = jnp.float32
BF = jnp.bfloat16

D_MODEL = 1024
GLA_HEADS = 4
GLA_DK = 64
GLA_CHUNK = 64
GLA_RANK = 16
GLA_TAU = 16.0
GLA_SUB = 256
GLA_ROWS_FWD = 1024
GLA_ROWS_BWD = 512
SWA_HEADS = 8
SWA_BLOCK = 128
SWA_QBLOCKS_FWD = 8
SWA_QBLOCKS = 8
RMS_EPS = 1e-6
ROPE_THETA = 10000.0

OFF_QK, OFF_V, OFF_GZ, OFF_SQ, OFF_SZ, OFF_SK, OFF_SV, OFF_GA = 0, 512, 1024, 1536, 2048, 2560, 2688, 2816
D_PAD = 2944
D_IN = 2832
LANES = 128
VMEM_LIMIT = 56 * 1024 * 1024

ADAM_LR, ADAM_B1, ADAM_B2, ADAM_EPS, ADAM_WD, ADAM_STEP = 0.001, 0.9, 0.999, 1e-08, 0.01, 10

NT = (((1,), (1,)), ((), ()))
TN = (((0,), (0,)), ((), ()))
MESH = pl.DeviceIdType.MESH


def _dot(a, b, dims=None):
    if dims is None:
        return jnp.dot(a, b, preferred_element_type=F32)
    return lax.dot_general(a, b, dims, preferred_element_type=F32)


def _sigmoid(x):
    return 1.0 / (1.0 + jnp.exp(-x))


def _params(sem=None):
    return pltpu.CompilerParams(dimension_semantics=sem, vmem_limit_bytes=VMEM_LIMIT)


def _full(shape):
    return pl.BlockSpec(shape, lambda i: (0,) * len(shape))


def _subtiles(rows, size=256):
    size = min(size, rows)
    return [slice(k * size, (k + 1) * size) for k in range(rows // size)]


WEIGHT_CHUNKS = 4


def _gather_sems(chunks=1):
    return [pltpu.SemaphoreType.DMA((7 * chunks,)), pltpu.SemaphoreType.DMA((7 * chunks,)), pltpu.SemaphoreType.DMA]


_GATHER_SEMS = _gather_sems()


class _Gather:
    def __init__(self, x_ref, out_ref, send_sems, recv_sems, local_sem, slab=None, chunks=1):
        self.slab_of = slab
        self.chunks = chunks
        self.width = x_ref.shape[-1] // chunks
        x, y, c = lax.axis_index("x"), lax.axis_index("y"), lax.axis_index("c")
        self.me, self.sibling, self.c = (x, y, c), (x, y, 1 - c), c
        self.xn, self.yn, self.dg = (1 - x, y), (x, 1 - y), (1 - x, 1 - y)
        self.pass_from = (lax.rem(x + 1 - c, 2), lax.rem(y + c, 2))
        self.pass_to = (lax.rem(x + c, 2), lax.rem(y + 1 - c, 2))
        self.x_ref, self.out_ref, self.send_sems, self.recv_sems = x_ref, out_ref, send_sems, recv_sems
        self.mine = pltpu.make_async_copy(x_ref, self._slab(*self.me), local_sem)

    def _slab(self, px, py, pc):
        if self.slab_of is not None:
            return self.slab_of(self.out_ref, px, py, pc)
        return self.out_ref.at[4 * px + 2 * py + pc]

    def _part(self, ref, q):
        if self.chunks == 1:
            return ref
        lanes = slice(q * self.width, (q + 1) * self.width)
        return ref.at[(slice(None),) * (len(ref.shape) - 1) + (lanes,)]

    def _copy(self, k, q, blk, to, src=None):
        i = k * self.chunks + q
        return pltpu.make_async_remote_copy(
            src_ref=self._part(self._slab(*blk) if src is None else src, q), dst_ref=self._part(self._slab(*blk), q),
            send_sem=self.send_sems.at[i], recv_sem=self.recv_sems.at[i], device_id=to, device_id_type=MESH)

    def _sends(self, q):
        c = self.c
        return [self._copy(0, q, self.me, self.sibling, src=self.x_ref),
                self._copy(1, q, self.me, (*self.xn, c), src=self.x_ref),
                self._copy(2, q, self.me, (*self.yn, c), src=self.x_ref),
                self._copy(3, q, (*self.pass_from, c), (*self.pass_to, c)),
                self._copy(4, q, (*self.xn, c), self.sibling),
                self._copy(5, q, (*self.yn, c), self.sibling),
                self._copy(6, q, (*self.dg, c), self.sibling)]

    def start(self):
        self.mine.start()
        for q in range(self.chunks):
            sends = self._sends(q)
            for k in (1, 2, 0):
                sends[k].start()

    def pass_on(self, only=None):
        for q in range(self.chunks) if only is None else (only,):
            sends = self._sends(q)
            self._copy(1, q, (*self.xn, self.c), self.me).wait_recv()
            self._copy(2, q, (*self.yn, self.c), self.me).wait_recv()
            for k in (3, 4, 5):
                sends[k].start()

    def relay_diagonal(self, only=None):
        for q in range(self.chunks) if only is None else (only,):
            self._copy(3, q, (*self.dg, self.c), self.me).wait_recv()
            self._sends(q)[6].start()

    def relay(self):
        self.pass_on()
        self.relay_diagonal()

    def finish(self):
        c = self.c
        for q in range(self.chunks):
            self._copy(0, q, self.sibling, self.me).wait_recv()
            for k, chip in ((4, self.xn), (5, self.yn), (6, self.dg)):
                self._copy(k, q, (*chip, 1 - c), self.me).wait_recv()
            for cp in self._sends(q):
                cp.wait_send()
        self.mine.wait()


def _prologue(cw, w_ada, b_shard, half_in, pos_col, inv_freq):
    s = pos_col.shape[0]
    rt = min(512, s)

    def body(cw_ref, wada_hbm, b_ref, hin_ref, pos_hbm, f_ref,
             first_ref, mod_ref, win_ref, cos_hbm, sin_hbm,
             mod_blk, cos_ref, sin_ref, wada_ref, pos_ref, table_sems, local_sems, *sems):
        fetch_w = pltpu.make_async_copy(wada_hbm, wada_ref, local_sems.at[0])
        fetch_p = pltpu.make_async_copy(pos_hbm, pos_ref, local_sems.at[1])
        fetch_w.start()
        fetch_p.start()
        g_c = _Gather(cw_ref, first_ref, *sems[0:3])
        half_lanes = hin_ref.shape[1]
        g_in = _Gather(hin_ref, win_ref, *sems[3:6], chunks=WEIGHT_CHUNKS,
                       slab=lambda ref, px, py, pc: ref.at[2 * px + py, :, pl.ds(pl.multiple_of(pc * half_lanes, half_lanes), half_lanes)])
        g_mod = _Gather(mod_blk, mod_ref, *sems[6:9])
        g_c.start()
        g_in.start()
        g_c.relay()
        g_c.finish()
        c_rows = [jnp.concatenate([first_ref[d, r:r + 1, :] for r in range(8)], axis=1) for d in range(8)]
        c_all = jnp.concatenate(c_rows, axis=0)
        sc = (c_all * _sigmoid(c_all)).astype(BF)
        fetch_w.wait()
        mod_blk[...] = _dot(sc, wada_ref[...].astype(BF)) + b_ref[...]
        g_mod.start()
        fetch_p.wait()

        def rope_rows(i, carry):
            rows = pl.ds(pl.multiple_of(i * rt, rt), rt)
            ang = pos_ref[rows, :].astype(F32) * f_ref[...]
            lane = lax.broadcasted_iota(jnp.int32, ang.shape, 1)
            cos_ref[rows, :] = jnp.cos(ang)
            sn = jnp.sin(ang)
            sin_ref[rows, :] = jnp.where((lane % 64) < 32, -sn, sn)
            pltpu.make_async_copy(cos_ref.at[rows, :], cos_hbm.at[rows, :], table_sems.at[0]).start()
            pltpu.make_async_copy(sin_ref.at[rows, :], sin_hbm.at[rows, :], table_sems.at[1]).start()
            return carry

        waits = ([lambda q=q: g_in.pass_on(q) for q in range(WEIGHT_CHUNKS)]
                 + [lambda q=q: g_in.relay_diagonal(q) for q in range(WEIGHT_CHUNKS)] + [g_mod.relay])
        steps = s // rt
        lead = steps // 4
        per_wait = max((steps - lead) // len(waits), 1)
        lax.fori_loop(0, lead, rope_rows, 0)
        done = lead
        for wait in waits:
            wait()
            nxt = min(done + per_wait, steps)
            lax.fori_loop(done, nxt, rope_rows, 0)
            done = nxt
        lax.fori_loop(done, steps, rope_rows, 0)
        g_in.finish()
        g_mod.finish()
        pltpu.make_async_copy(cos_ref, cos_hbm, table_sems.at[0]).wait()
        pltpu.make_async_copy(sin_ref, sin_hbm, table_sems.at[1]).wait()

    vm = pl.BlockSpec(memory_space=pltpu.VMEM)
    hbm = pl.BlockSpec(memory_space=pl.ANY)
    return pl.pallas_call(
        body, name="prologue",
        out_shape=[jax.ShapeDtypeStruct((8,) + cw.shape, F32), jax.ShapeDtypeStruct((8, 8, w_ada.shape[1]), F32),
                   jax.ShapeDtypeStruct((4, half_in.shape[0], 2 * half_in.shape[1]), half_in.dtype),
                   jax.ShapeDtypeStruct((s, LANES), F32), jax.ShapeDtypeStruct((s, LANES), F32)],
        in_specs=[vm, hbm, vm, hbm, hbm, vm], out_specs=[vm, vm, hbm, hbm, hbm],
        scratch_shapes=[pltpu.VMEM((8, w_ada.shape[1]), F32), pltpu.VMEM((s, LANES), F32), pltpu.VMEM((s, LANES), F32),
                        pltpu.VMEM(w_ada.shape, F32), pltpu.VMEM(pos_col.shape, jnp.int32),
                        pltpu.SemaphoreType.DMA((2,)), pltpu.SemaphoreType.DMA((2,))]
        + _GATHER_SEMS + _gather_sems(WEIGHT_CHUNKS) + _GATHER_SEMS,
        compiler_params=pltpu.CompilerParams(vmem_limit_bytes=VMEM_LIMIT),
    )(cw, w_ada, b_shard, half_in, pos_col, inv_freq)


def _reduce_scratch(rr, cc):
    c2 = cc // 2
    return [pltpu.VMEM((4, rr, c2), F32), pltpu.VMEM((4, rr, c2), F32), pltpu.VMEM((3, rr, c2), BF),
            pltpu.VMEM((2, rr, c2), BF), pltpu.VMEM((rr, c2), BF), pltpu.VMEM((rr, c2), F32),
            pltpu.SemaphoreType.DMA((8 + 3 * WEIGHT_CHUNKS,)), pltpu.SemaphoreType.DMA((8 + 3 * WEIGHT_CHUNKS,)),
            pltpu.SemaphoreType.DMA((5,))]


class _Reduce:
    def __init__(self, p_hbm, out_ref, acc_ref, own_ref, send_ref, land_ref, relay_ref, res_ref,
                 send_sems, recv_sems, local_sems):
        x, y, c = lax.axis_index("x"), lax.axis_index("y"), lax.axis_index("c")
        c2 = out_ref.shape[1] // 2
        sibling = (x, y, 1 - c)
        first = (lax.rem(x + 1 - c, 2), lax.rem(y + c, 2))
        second = (lax.rem(x + c, 2), lax.rem(y + 1 - c, 2))
        shards = [2 * first[0] + first[1], 2 * second[0] + second[1], 2 * (1 - x) + (1 - y), 2 * x + y]
        sibling_slot = (1, 0, 2, 3)
        mine = pl.ds(pl.multiple_of(c * c2, c2), c2)
        other = pl.ds(pl.multiple_of((1 - c) * c2, c2), c2)
        self.acc_ref, self.own_ref, self.send_ref, self.land_ref = acc_ref, own_ref, send_ref, land_ref
        self.relay_ref, self.res_ref = relay_ref, res_ref
        self.own = [pltpu.make_async_copy(p_hbm.at[j, :, mine], own_ref.at[k], local_sems.at[k])
                    for k, j in enumerate(shards)]
        self.swap_out = [pltpu.make_async_remote_copy(
            src_ref=p_hbm.at[j, :, other], dst_ref=acc_ref.at[sibling_slot[k]], send_sem=send_sems.at[k],
            recv_sem=recv_sems.at[sibling_slot[k]], device_id=sibling, device_id_type=MESH) for k, j in enumerate(shards)]
        self.swap_in = [pltpu.make_async_remote_copy(
            src_ref=p_hbm.at[j, :, other], dst_ref=acc_ref.at[k], send_sem=send_sems.at[k], recv_sem=recv_sems.at[k],
            device_id=sibling, device_id_type=MESH) for k, j in enumerate(shards)]

        self.lanes = [slice(q * (c2 // WEIGHT_CHUNKS), (q + 1) * (c2 // WEIGHT_CHUNKS)) for q in range(WEIGHT_CHUNKS)]

        def message(m, src, dst, to):
            return [pltpu.make_async_remote_copy(
                src_ref=src.at[:, ln], dst_ref=dst.at[:, ln], send_sem=send_sems.at[8 + m * WEIGHT_CHUNKS + q],
                recv_sem=recv_sems.at[8 + m * WEIGHT_CHUNKS + q], device_id=(*to, c), device_id_type=MESH)
                for q, ln in enumerate(self.lanes)]

        self.direct = message(0, send_ref.at[0], land_ref.at[0], first)
        self.passed = message(1, send_ref.at[1], relay_ref, first)
        self.joint = message(2, send_ref.at[2], land_ref.at[1], second)
        self.put = pltpu.make_async_copy(res_ref, out_ref.at[:, mine], local_sems.at[4])
        self.share = pltpu.make_async_remote_copy(
            src_ref=res_ref, dst_ref=out_ref.at[:, mine], send_sem=send_sems.at[7],
            recv_sem=recv_sems.at[7], device_id=sibling, device_id_type=MESH)

    def start(self):
        for k in (2, 0, 1, 3):
            self.own[k].start()
            self.swap_out[k].start()

    def _combine(self, k):
        self.own[k].wait()
        self.swap_out[k].wait_send()
        self.swap_in[k].wait_recv()
        self.acc_ref[k] = self.acc_ref[k] + self.own_ref[k]

    def combine_and_send(self):
        dt = self.send_ref.dtype
        self._combine(2)
        self.send_ref[1] = self.acc_ref[2].astype(dt)
        for cp in self.passed:
            cp.start()
        self._combine(0)
        self.send_ref[0] = self.acc_ref[0].astype(dt)
        for cp in self.direct:
            cp.start()
        self._combine(1)
        self._combine(3)

    def send_joint(self):
        dt = self.send_ref.dtype
        for q, ln in enumerate(self.lanes):
            self.passed[q].wait_recv()
            self.send_ref[2, :, ln] = (self.acc_ref[1, :, ln] + self.relay_ref[:, ln].astype(F32)).astype(dt)
            self.joint[q].start()

    def total_and_share(self):
        for cp in self.direct + self.joint:
            cp.wait_recv()
        self.res_ref[...] = self.acc_ref[3] + self.land_ref[0].astype(F32) + self.land_ref[1].astype(F32)
        for cp in self.direct + self.passed + self.joint:
            cp.wait_send()
        self.put.start()
        self.share.start()

    def finish(self):
        self.put.wait()
        self.share.wait()


def _epilogue(dw_in_parts, small):
    _, r_in, cc = dw_in_parts.shape
    n_red = len(_reduce_scratch(r_in, cc))

    def body(pin_hbm, small_ref, gin_ref, small_all_ref, *scratch):
        red_in = _Reduce(pin_hbm, gin_ref, *scratch[0:n_red])
        gat = _Gather(small_ref, small_all_ref, *scratch[n_red:])
        red_in.start()
        gat.start()
        gat.relay()
        red_in.combine_and_send()
        gat.finish()
        red_in.send_joint()
        red_in.total_and_share()
        red_in.finish()

    vm = pl.BlockSpec(memory_space=pltpu.VMEM)
    anyspec = pl.BlockSpec(memory_space=pl.ANY)
    return pl.pallas_call(
        body, name="epilogue",
        out_shape=[jax.ShapeDtypeStruct((r_in, cc), F32), jax.ShapeDtypeStruct((8,) + small.shape, F32)],
        in_specs=[anyspec, vm], out_specs=[anyspec, vm],
        scratch_shapes=_reduce_scratch(r_in, cc) + _GATHER_SEMS,
        compiler_params=pltpu.CompilerParams(vmem_limit_bytes=VMEM_LIMIT),
    )(dw_in_parts, small)


def _rope(t, cosb, sinb, first_half):
    partner = jnp.where(first_half, pltpu.roll(t, 96, 1), pltpu.roll(t, 32, 1))
    return t * cosb + partner * sinb


def _rope_t(g, cosb, sinb, first_half):
    gs = g * sinb
    partner = jnp.where(first_half, pltpu.roll(gs, 96, 1), pltpu.roll(gs, 32, 1))
    return g * cosb + partner


def _modnorm(x, g, sc1p, shift):
    r = lax.rsqrt(jnp.mean(x * x, axis=-1, keepdims=True) + RMS_EPS)
    xn = x * r
    return xn, r, (xn * g) * sc1p + shift


def _load_w_padded(w_hbm, w_vm, sems):
    copies = [pltpu.make_async_copy(w_hbm.at[ref:ref + n], w_vm.at[pad:pad + n], sems.at[k])
              for k, (pad, ref, n) in enumerate(_UNPAD_ROWS)]
    for cp in copies:
        cp.start()
    w_vm[OFF_GA + GLA_RANK:, :] = jnp.zeros((D_PAD - OFF_GA - GLA_RANK, D_MODEL), w_vm.dtype)
    return copies


def _inproj_fwd(x2d, shift, sc1p, g_norm, w_t):
    s = x2d.shape[0]
    tm = min(1024, s)

    def body(x_ref, sh_ref, sc_ref, g_ref, w_hbm, o_ref, w_vm, sems):
        @pl.when(pl.program_id(0) == 0)
        def _():
            for cp in _load_w_padded(w_hbm, w_vm, sems):
                cp.wait()

        subs = _subtiles(tm)
        hs = [_modnorm(x_ref[sl, :], g_ref[...], sc_ref[...], sh_ref[...])[2].astype(BF) for sl in subs]
        for sl, h in zip(subs, hs):
            o_ref[sl, :] = _dot(h, w_vm[...], NT)

    vec = _full((1, D_MODEL))
    return pl.pallas_call(
        body, name="inproj_fwd", grid=(s // tm,),
        in_specs=[pl.BlockSpec((tm, D_MODEL), lambda i: (i, 0)), vec, vec, vec, pl.BlockSpec(memory_space=pl.ANY)],
        out_specs=pl.BlockSpec((tm, D_PAD), lambda i: (i, 0)),
        out_shape=jax.ShapeDtypeStruct((s, D_PAD), F32),
        scratch_shapes=[pltpu.VMEM((D_PAD, D_MODEL), BF), pltpu.SemaphoreType.DMA((len(_UNPAD_ROWS),))],
        compiler_params=_params(("arbitrary",)),
    )(x2d, shift, sc1p, g_norm, w_t)


def _split3(a):
    hi = a.astype(BF)
    r1 = a - hi.astype(F32)
    mid = r1.astype(BF)
    lo = (r1 - mid.astype(F32)).astype(BF)
    return hi, mid, lo


def _tri_matmul(tri, a):
    hi, mid, lo = _split3(a)
    return _dot(tri, hi) + _dot(tri, mid) + _dot(tri, lo)


def _chunks(tb):
    return [slice(c * GLA_CHUNK, (c + 1) * GLA_CHUNK) for c in range(tb // GLA_CHUNK)]


def _per_chunk_rows(rows, width):
    return jnp.concatenate([jnp.broadcast_to(r, (GLA_CHUNK, width)) for r in rows], axis=0)


def _gla_triangle(tb):
    row = lax.broadcasted_iota(jnp.int32, (tb, tb), 0)
    col = lax.broadcasted_iota(jnp.int32, (tb, tb), 1)
    return (((row // GLA_CHUNK) == (col // GLA_CHUNK)) & (col <= row)).astype(F32)


def _lane_mean(x, ones_b):
    hi = x.astype(BF)
    lo = (x - hi.astype(F32)).astype(BF)
    return (_dot(hi, ones_b) + _dot(lo, ones_b)) * (1.0 / LANES)


def _head(t, h, lo_h):
    blk = t[:, LANES * (h // 2):LANES * (h // 2 + 1)]
    return jnp.where(lo_h, blk, 0.0) if h % 2 == 0 else jnp.where(lo_h, 0.0, blk)


def _gla_block_common(qk, ga, wd, bd, tril_b):
    tb = qk.shape[0]
    q, k = qk[:, :256], qk[:, 256:]
    z = _dot(ga.astype(BF), wd) + bd
    la = (jnp.minimum(z, 0.0) - jnp.log(1.0 + jnp.exp(-jnp.abs(z)))) * (1.0 / GLA_TAU)
    b = _tri_matmul(tril_b, la)
    bls = [b[rs.stop - 1:rs.stop, :] for rs in _chunks(tb)]
    eq = jnp.exp(b)
    ek = jnp.exp(-b)
    f = jnp.exp(_per_chunk_rows(bls, 256) - b)
    return z, eq, ek, f, q * (eq * GLA_DK ** -0.5), k * ek, k * f, bls


def _gla_units(s, rows):
    sub = min(GLA_SUB, s)
    tb = min(rows, s)
    subs = [slice(i * sub, (i + 1) * sub) for i in range(tb // sub)]
    units = [(i, h) for i in range(len(subs)) for h in range(GLA_HEADS)]
    return tb, sub, subs, units


def _gla_fwd(proj, wdecp, bdec, ggla):
    s = proj.shape[0]
    tb, sub, subs, units = _gla_units(s, GLA_ROWS_FWD)
    nch = sub // GLA_CHUNK

    def body(qk_ref, v_ref, gz_ref, ga_ref, wd_ref, bd_ref, gg_ref, tri_ref, og_ref, opre_ref, sprev_ref, st_ref):
        @pl.when(pl.program_id(0) == 0)
        def _():
            st_ref[...] = jnp.zeros_like(st_ref)

        lo_h = lax.broadcasted_iota(jnp.int32, (sub, LANES), 1) < GLA_DK
        tril = tri_ref[...] > 0.5
        tril_b = tri_ref[...].astype(BF)
        ones_b = jnp.ones((LANES, LANES), BF)
        gg, wd, bd = gg_ref[...], wd_ref[...], bd_ref[...]
        chunks = _chunks(sub)
        lanes = [slice(h * LANES, (h + 1) * LANES) for h in range(GLA_HEADS)]
        com = [_gla_block_common(qk_ref[sl, :], ga_ref[sl, :], wd, bd, tril_b) for sl in subs]
        decs = [[jnp.exp(bl) for bl in cm[7]] for cm in com]
        a = {(i, h): _head(com[i][4], h, lo_h).astype(BF) for i, h in units}
        bm = {(i, h): _head(com[i][5], h, lo_h).astype(BF) for i, h in units}
        ktl = {(i, h): _head(com[i][6], h, lo_h).astype(BF) for i, h in units}
        vh = {(i, h): v_ref[subs[i], lanes[h]].astype(BF) for i, h in units}
        sc = {u: _dot(a[u], bm[u], NT) for u in units}
        upd = {u: [_dot(vh[u][rs], ktl[u][rs], TN) for rs in chunks] for u in units}
        p = {u: jnp.where(tril, sc[u], 0.0).astype(BF) for u in units}
        o = {u: _dot(p[u], vh[u]) for u in units}
        states = {}
        for h in range(GLA_HEADS):
            st = st_ref[h]
            for i in range(len(subs)):
                entering = []
                for c in range(nch):
                    entering.append(st)
                    sprev_ref[i * nch + c, h] = st
                    st = st * decs[i][c][:, LANES * (h // 2):LANES * (h // 2 + 1)] + upd[(i, h)][c]
                states[(i, h)] = entering
            st_ref[h] = st
        inter = {u: [_dot(a[u][rs], states[u][c].astype(BF), NT) for c, rs in enumerate(chunks)] for u in units}
        o = {u: o[u] + jnp.concatenate(inter[u], axis=0) for u in units}
        ms = {u: _lane_mean(o[u] * o[u], ones_b) for u in units}
        for i, h in units:
            gzh = gz_ref[subs[i], lanes[h]]
            opre_ref[subs[i], lanes[h]] = o[(i, h)]
            og_ref[subs[i], lanes[h]] = (((o[(i, h)] * lax.rsqrt(ms[(i, h)] + RMS_EPS)) * gg[:, lanes[h]])
                                         * (gzh * _sigmoid(gzh))).astype(og_ref.dtype)

    def col(width, off):
        return pl.BlockSpec((tb, width), lambda i: (i, off // width))

    return pl.pallas_call(
        body, name="gla_fwd", grid=(s // tb,),
        in_specs=[col(512, OFF_QK), col(512, OFF_V), col(512, OFF_GZ), col(LANES, OFF_GA),
                  _full((LANES, 256)), _full((1, 256)), _full((1, 512)), _full((sub, sub))],
        out_specs=[pl.BlockSpec((tb, 512), lambda i: (i, 0)), pl.BlockSpec((tb, 512), lambda i: (i, 0)),
                   pl.BlockSpec((tb // GLA_CHUNK, GLA_HEADS, LANES, LANES), lambda i: (i, 0, 0, 0))],
        out_shape=[jax.ShapeDtypeStruct((s, 512), BF), jax.ShapeDtypeStruct((s, 512), F32),
                   jax.ShapeDtypeStruct((s // GLA_CHUNK, GLA_HEADS, LANES, LANES), F32)],
        scratch_shapes=[pltpu.VMEM((GLA_HEADS, LANES, LANES), F32)],
        compiler_params=_params(("arbitrary",)),
    )(proj, proj, proj, proj, wdecp, bdec, ggla, _gla_triangle(sub))


def _gla_bwd(proj, dog, opre, sprev, wdecp, bdec, ggla):
    s = proj.shape[0]
    tb, sub, subs, units = _gla_units(s, GLA_ROWS_BWD)
    nsub = len(subs)
    nch = sub // GLA_CHUNK
    nb = s // tb

    def body(qk_ref, v_ref, gz_ref, ga_ref, dog_ref, opre_ref, sprev_ref, wd_ref, bd_ref, gg_ref, tri_ref, triu_ref,
             dqk_ref, dv_ref, dgz_ref, dga_ref, dwd_ref, dbd_ref, dgg_ref, dst_ref):
        @pl.when(pl.program_id(0) == 0)
        def _():
            dst_ref[...] = jnp.zeros_like(dst_ref)
            dwd_ref[...] = jnp.zeros_like(dwd_ref)
            dbd_ref[...] = jnp.zeros_like(dbd_ref)
            dgg_ref[...] = jnp.zeros_like(dgg_ref)

        lo_h = lax.broadcasted_iota(jnp.int32, (sub, LANES), 1) < GLA_DK
        tril = tri_ref[...] > 0.5
        tril_b = tri_ref[...].astype(BF)
        triu_b = triu_ref[...].astype(BF)
        ones_b = jnp.ones((LANES, LANES), BF)
        last_row = (lax.broadcasted_iota(jnp.int32, (sub, LANES), 0) % GLA_CHUNK) == GLA_CHUNK - 1
        wd, gg, bd = wd_ref[...], gg_ref[...], bd_ref[...]
        chunks = _chunks(sub)
        lanes = [slice(h * LANES, (h + 1) * LANES) for h in range(GLA_HEADS)]
        blks = [slice(LANES * (h // 2), LANES * (h // 2 + 1)) for h in range(GLA_HEADS)]
        ga = [ga_ref[sl, :] for sl in subs]
        com = [_gla_block_common(qk_ref[sl, :], ga[i], wd, bd, tril_b) for i, sl in enumerate(subs)]
        decs = [[jnp.exp(bl) for bl in cm[7]] for cm in com]
        a = {(i, h): _head(com[i][4], h, lo_h).astype(BF) for i, h in units}
        bm = {(i, h): _head(com[i][5], h, lo_h).astype(BF) for i, h in units}
        ktl = {(i, h): _head(com[i][6], h, lo_h).astype(BF) for i, h in units}
        vh = {(i, h): v_ref[subs[i], lanes[h]].astype(BF) for i, h in units}
        sc = {u: _dot(a[u], bm[u], NT) for u in units}

        o = {(i, h): opre_ref[subs[i], lanes[h]] for i, h in units}
        ms = {u: _lane_mean(o[u] * o[u], ones_b) for u in units}
        gz = {(i, h): gz_ref[subs[i], lanes[h]] for i, h in units}
        dog = {(i, h): dog_ref[subs[i], lanes[h]] for i, h in units}
        sg = {u: _sigmoid(gz[u]) for u in units}
        r = {u: lax.rsqrt(ms[u] + RMS_EPS) for u in units}
        ohat = {u: o[u] * r[u] for u in units}
        sil = {u: gz[u] * sg[u] for u in units}
        for i, h in units:
            u = (i, h)
            dgz_ref[subs[i], lanes[h]] = (dog[u] * (ohat[u] * gg[:, lanes[h]])
                                          * (sg[u] * (1.0 + gz[u] * (1.0 - sg[u])))).astype(dgz_ref.dtype)
            dgg_ref[:, lanes[h]] += jnp.sum(dog[u] * sil[u] * ohat[u], axis=0, keepdims=True)
        dn = {(i, h): dog[(i, h)] * sil[(i, h)] * gg[:, lanes[h]] for i, h in units}
        mdn = {u: _lane_mean(dn[u] * ohat[u], ones_b) for u in units}
        do = {u: (r[u] * (dn[u] - ohat[u] * mdn[u])).astype(BF) for u in units}

        p = {u: jnp.where(tril, sc[u], 0.0).astype(BF) for u in units}
        dpr = {u: _dot(do[u], vh[u], NT) for u in units}
        incr = {u: [_dot(do[u][rs], a[u][rs], TN) for rs in chunks] for u in units}
        dv = {u: _dot(p[u], do[u], TN) for u in units}
        dp = {u: jnp.where(tril, dpr[u], 0.0).astype(BF) for u in units}
        dqd = {u: _dot(dp[u], bm[u]) for u in units}
        dkd = {u: _dot(dp[u], a[u], TN) for u in units}
        st = {(i, h): [sprev_ref[i * nch + c, h] for c in range(nch)] for i, h in units}
        leaving = {}
        for h in range(GLA_HEADS):
            d = dst_ref[h]
            for i in reversed(range(nsub)):
                out = [None] * nch
                for c in reversed(range(nch)):
                    out[c] = d
                    d = d * decs[i][c][:, blks[h]] + incr[(i, h)][c]
                leaving[(i, h)] = out
            dst_ref[h] = d
        lv_b = {u: [leaving[u][c].astype(BF) for c in range(nch)] for u in units}
        dv_s = {u: [_dot(ktl[u][rs], lv_b[u][c], NT) for c, rs in enumerate(chunks)] for u in units}
        dqd_s = {u: [_dot(do[u][rs], st[u][c].astype(BF)) for c, rs in enumerate(chunks)] for u in units}
        dkt_s = {u: [_dot(vh[u][rs], lv_b[u][c]) for c, rs in enumerate(chunks)] for u in units}
        ddec = {u: [jnp.sum(leaving[u][c] * st[u][c], axis=0, keepdims=True) for c in range(nch)] for u in units}
        for i, h in units:
            dv_ref[subs[i], lanes[h]] = (dv[(i, h)] + jnp.concatenate(dv_s[(i, h)], axis=0)).astype(dv_ref.dtype)
        dqd = {u: dqd[u] + jnp.concatenate(dqd_s[u], axis=0) for u in units}
        dkt = {u: jnp.concatenate(dkt_s[u], axis=0) for u in units}

        db = []
        for i, sl in enumerate(subs):
            _, eq, ek, f, qd, kd, kt, _ = com[i]
            parts = []
            for pair in range(GLA_HEADS // 2):
                blk, u0, u1 = blks[2 * pair], (i, 2 * pair), (i, 2 * pair + 1)
                dqd_b, dkd_b, dkt_b = dqd[u0] + dqd[u1], dkd[u0] + dkd[u1], dkt[u0] + dkt[u1]
                dqk_ref[sl, blk] = (dqd_b * (eq[:, blk] * GLA_DK ** -0.5)).astype(dqk_ref.dtype)
                dqk_ref[sl, 256 + LANES * pair:256 + LANES * (pair + 1)] = (dkd_b * ek[:, blk] + dkt_b * f[:, blk]).astype(dqk_ref.dtype)
                dkt_kt = dkt_b * kt[:, blk]
                dbp = dqd_b * qd[:, blk] - dkd_b * kd[:, blk] - dkt_kt
                dbl = [jnp.sum(dkt_kt[rs], axis=0, keepdims=True) + (ddec[u0][c] + ddec[u1][c]) * decs[i][c][:, blk]
                       for c, rs in enumerate(chunks)]
                parts.append(jnp.where(last_row, dbp + _per_chunk_rows(dbl, LANES), dbp))
            db.append(jnp.concatenate(parts, axis=1))
        dla = [_tri_matmul(triu_b, db[i]) for i in range(nsub)]
        dz32 = [dla[i] * (1.0 / GLA_TAU) * _sigmoid(-com[i][0]) for i in range(nsub)]
        dz = [t.astype(BF) for t in dz32]
        for i, sl in enumerate(subs):
            dga_ref[sl, :] = _dot(dz[i], wd, NT).astype(dga_ref.dtype)
            dwd_ref[...] += _dot(ga[i].astype(BF), dz[i], TN)
            dbd_ref[...] += jnp.sum(dz32[i], axis=0, keepdims=True)

    def col(width, off):
        return pl.BlockSpec((tb, width), lambda i: (nb - 1 - i, off // width))

    def rev(width):
        return pl.BlockSpec((tb, width), lambda i: (nb - 1 - i, 0))

    return pl.pallas_call(
        body, name="gla_bwd", grid=(nb,),
        in_specs=[col(512, OFF_QK), col(512, OFF_V), col(512, OFF_GZ), col(LANES, OFF_GA), rev(512), rev(512),
                  pl.BlockSpec((tb // GLA_CHUNK, GLA_HEADS, LANES, LANES), lambda i: (nb - 1 - i, 0, 0, 0)),
                  _full((LANES, 256)), _full((1, 256)), _full((1, 512)), _full((sub, sub)), _full((sub, sub))],
        out_specs=[rev(512), rev(512), rev(512), rev(LANES), _full((LANES, 256)), _full((1, 256)), _full((1, 512))],
        out_shape=[jax.ShapeDtypeStruct((s, 512), BF), jax.ShapeDtypeStruct((s, 512), BF),
                   jax.ShapeDtypeStruct((s, 512), BF), jax.ShapeDtypeStruct((s, LANES), BF),
                   jax.ShapeDtypeStruct((LANES, 256), F32), jax.ShapeDtypeStruct((1, 256), F32),
                   jax.ShapeDtypeStruct((1, 512), F32)],
        scratch_shapes=[pltpu.VMEM((GLA_HEADS, LANES, LANES), F32)],
        compiler_params=_params(("arbitrary",)),
    )(proj, proj, proj, proj, dog, opre, sprev, wdecp, bdec, ggla, _gla_triangle(sub), _gla_triangle(sub).T)


_SWA_COL_HEADS = (0, 2, 1, 3, 4, 6, 5, 7)
_SWA_COLS = SWA_HEADS * SWA_BLOCK


def _swa_masks():
    lo2 = lax.broadcasted_iota(jnp.int32, (2 * SWA_BLOCK, LANES), 1) < 64
    lane1 = lax.broadcasted_iota(jnp.int32, (SWA_BLOCK, LANES), 1)
    first_half = (lane1 % 64) < 32
    key = lax.broadcasted_iota(jnp.int32, (SWA_BLOCK, _SWA_COLS), 0)
    query = lax.broadcasted_iota(jnp.int32, (SWA_BLOCK, _SWA_COLS), 1) % SWA_BLOCK
    return lo2, lane1 < 64, first_half, key > query


def _merge_band(t, prev_mask, prev_bias=None):
    prev = t[:SWA_BLOCK] if prev_bias is None else t[:SWA_BLOCK] + prev_bias
    return jnp.where(prev_mask, prev, t[SWA_BLOCK:])


def _split_band(t, prev_mask_b):
    prev = t * prev_mask_b
    return jnp.concatenate([prev, t - prev], axis=0)


def _kv_variants(t, lo2):
    tr = pltpu.roll(t, 64, 1)
    lo_v = [jnp.where(lo2, t, 0.0).astype(BF), jnp.where(lo2, tr, 0.0).astype(BF)]
    hi_v = [jnp.where(lo2, 0.0, tr).astype(BF), jnp.where(lo2, 0.0, t).astype(BF)]
    return lo_v, hi_v


def _kv_variants_t(t):
    tt = t.T
    sw = jnp.concatenate([tt[64:], tt[:64]], axis=0)
    top = lax.broadcasted_iota(jnp.int32, tt.shape, 0) < 64
    lo_v = [jnp.where(top, tt, 0.0).astype(BF), jnp.where(top, sw, 0.0).astype(BF)]
    hi_v = [jnp.where(top, 0.0, sw).astype(BF), jnp.where(top, 0.0, tt).astype(BF)]
    return lo_v, hi_v


def _swa_scores(qg, k_lo, k_hi):
    return jnp.concatenate([_dot(k_lo[0], qg[0], NT), _dot(k_hi[0], qg[0], NT),
                            _dot(k_lo[1], qg[1], NT), _dot(k_hi[1], qg[1], NT)], axis=1)


def _sink_row(sinks_ref):
    return jnp.concatenate([jnp.full((1, SWA_BLOCK), sinks_ref[0, hd], F32) for hd in _SWA_COL_HEADS], axis=1)


def _swa_softmax(st, prev_mask, prev_bias, sink):
    st = _merge_band(st, prev_mask, prev_bias)
    m = jnp.maximum(jnp.max(st, axis=0, keepdims=True), sink)
    ex = jnp.exp(st - m)
    es = jnp.exp(sink - m)
    inv = 1.0 / (jnp.sum(ex, axis=0, keepdims=True) + es)
    return ex, es, inv


def _no_prev_bias(block_index):
    return jnp.where(block_index > 0, 0.0, -1e30).astype(F32)


def _swa_queries(sq_ref, rows, cosb, sinb, first_half):
    qs = [_rope(sq_ref[rows, p * LANES:(p + 1) * LANES], cosb, sinb, first_half) * 0.125 for p in range(4)]
    return [jnp.concatenate(qs[0:2], axis=0), jnp.concatenate(qs[2:4], axis=0)]


def _phase_steps(nsteps, phases):
    return [min(nsteps - 1, (k * nsteps) // phases) for k in range(phases - 1)] + [nsteps - 1]


def _swa_fwd(proj, cos, sin, sinks, half_out):
    s = proj.shape[0]
    nq = min(SWA_QBLOCKS_FWD, s // SWA_BLOCK)
    tq = nq * SWA_BLOCK
    steps = _phase_steps(s // tq, 4)

    def body(sq_ref, sz_ref, sk_ref, sv_ref, cos_ref, sin_ref, sinks_ref, hout_hbm, os_ref, opre_ref, wout_hbm,
             kprev, vprev, *gather_sems):
        n = pl.program_id(0)

        @pl.when(n == 0)
        def _():
            kprev[...] = jnp.zeros_like(kprev)
            vprev[...] = jnp.zeros_like(vprev)

        gather = _Gather(hout_hbm, wout_hbm, *gather_sems, chunks=WEIGHT_CHUNKS)
        for step, phase in zip(steps, (gather.start, gather.pass_on, gather.relay_diagonal, gather.finish)):
            pl.when(n == step)(phase)

        lo2, _, first_half, prev_mask = _swa_masks()
        prev_mask_b = jnp.where(prev_mask, 1.0, 0.0).astype(BF)
        sink = _sink_row(sinks_ref)
        blocks = range(nq)
        rows = [slice(j * SWA_BLOCK, (j + 1) * SWA_BLOCK) for j in blocks]
        cosb = [cos_ref[rows[j], :] for j in blocks]
        sinb = [sin_ref[rows[j], :] for j in blocks]
        kc = [_rope(sk_ref[rows[j], :], cosb[j], sinb[j], first_half) for j in blocks]
        vc = [sv_ref[rows[j], :] for j in blocks]
        kcat = [jnp.concatenate([kprev[...] if j == 0 else kc[j - 1], kc[j]], axis=0) for j in blocks]
        vcat = [jnp.concatenate([vprev[...] if j == 0 else vc[j - 1], vc[j]], axis=0) for j in blocks]
        kprev[...] = kc[-1]
        vprev[...] = vc[-1]
        kvar = [_kv_variants(kcat[j], lo2) for j in blocks]
        vtvar = [_kv_variants_t(vcat[j]) for j in blocks]
        qg = [[q.astype(BF) for q in _swa_queries(sq_ref, rows[j], cosb[j], sinb[j], first_half)] for j in blocks]
        st = [_swa_scores(qg[j], *kvar[j]) for j in blocks]
        soft = [_swa_softmax(st[j], prev_mask, _no_prev_bias(n) if j == 0 else None, sink) for j in blocks]
        pt = [_split_band(soft[j][0].astype(BF), prev_mask_b) for j in blocks]
        og = {}
        for j in blocks:
            inv = soft[j][2]
            for g in range(2):
                c0, c1, c2 = 512 * g, 512 * g + 256, 512 * g + 512
                ot = (_dot(vtvar[j][0][g], pt[j][:, c0:c1]) * inv[:, c0:c1]
                      + _dot(vtvar[j][1][g], pt[j][:, c1:c2]) * inv[:, c1:c2])
                og[(j, g)] = ot.T
        for j in blocks:
            for g in range(2):
                for i in range(2):
                    ls = slice((2 * g + i) * LANES, (2 * g + i + 1) * LANES)
                    o = og[(j, g)][i * SWA_BLOCK:(i + 1) * SWA_BLOCK]
                    sz = sz_ref[rows[j], ls]
                    opre_ref[rows[j], ls] = o
                    os_ref[rows[j], ls] = (o * (sz * _sigmoid(sz))).astype(os_ref.dtype)

    def col(width, off):
        return pl.BlockSpec((tq, width), lambda i: (i, off // width))

    row = pl.BlockSpec((tq, LANES), lambda i: (i, 0))
    return pl.pallas_call(
        body, name="swa_fwd", grid=(s // tq,),
        in_specs=[col(512, OFF_SQ), col(512, OFF_SZ), col(LANES, OFF_SK), col(LANES, OFF_SV), row, row,
                  pl.BlockSpec(memory_space=pltpu.SMEM), pl.BlockSpec(memory_space=pl.ANY)],
        out_specs=[pl.BlockSpec((tq, 512), lambda i: (i, 0))] * 2 + [pl.BlockSpec(memory_space=pl.ANY)],
        out_shape=[jax.ShapeDtypeStruct((s, 512), BF), jax.ShapeDtypeStruct((s, 512), F32),
                   jax.ShapeDtypeStruct((8,) + half_out.shape, half_out.dtype)],
        scratch_shapes=[pltpu.VMEM((SWA_BLOCK, LANES), F32)] * 2 + _gather_sems(WEIGHT_CHUNKS),
        compiler_params=_params(("arbitrary",)),
    )(proj, proj, proj, proj, cos, sin, sinks, half_out)


def _swa_bwd(proj, dos, opre, cos, sin, sinks, dw_out_parts):
    s = proj.shape[0]
    nq = min(SWA_QBLOCKS, s // SWA_BLOCK)
    tq = nq * SWA_BLOCK
    steps = _phase_steps(s // tq, 5)
    _, r_out, c_out = dw_out_parts.shape

    def body(sq_ref, sz_ref, sk_ref, sv_ref, dos_ref, opre_ref, cos_ref, sin_ref, sinks_ref, pout_hbm,
             dsq_ref, dsz_ref, dsk_ref, dsv_ref, dsink_ref, gout_hbm, kprev, vprev, cprev, sprev, *reduce_scratch):
        n = pl.program_id(0)

        @pl.when(n == 0)
        def _():
            kprev[...] = jnp.zeros_like(kprev)
            vprev[...] = jnp.zeros_like(vprev)
            cprev[...] = jnp.zeros_like(cprev)
            sprev[...] = jnp.zeros_like(sprev)
            for hd in range(SWA_HEADS):
                dsink_ref[0, hd] = 0.0

        reduce = _Reduce(pout_hbm, gout_hbm, *reduce_scratch)
        phases = (reduce.start, reduce.combine_and_send, reduce.send_joint, reduce.total_and_share, reduce.finish)
        for step, phase in zip(steps, phases):
            pl.when(n == step)(phase)

        lo2, lo1, first_half, prev_mask = _swa_masks()
        prev_mask_b = jnp.where(prev_mask, 1.0, 0.0).astype(BF)
        lo1s = jnp.concatenate([lo1, lo1], axis=0)
        sink = _sink_row(sinks_ref)

        def home(m0, m1):
            t0 = m0 + pltpu.roll(m0, 64, 1)
            t1 = m1 + pltpu.roll(m1, 64, 1)
            return jnp.where(lo2, t0, t1)

        kp, vp, cp_, sp_ = kprev[...], vprev[...], cprev[...], sprev[...]
        for j in range(nq):
            rows = slice(j * SWA_BLOCK, (j + 1) * SWA_BLOCK)
            blk = n * nq + j
            cosb, sinb = cos_ref[rows, :], sin_ref[rows, :]
            kc = _rope(sk_ref[rows, :], cosb, sinb, first_half)
            vc = sv_ref[rows, :]
            kcat = jnp.concatenate([kp, kc], axis=0)
            k_lo, k_hi = _kv_variants(kcat, lo2)
            kt_lo, kt_hi = _kv_variants_t(kcat)
            v_lo, v_hi = _kv_variants(jnp.concatenate([vp, vc], axis=0), lo2)
            qg32 = _swa_queries(sq_ref, rows, cosb, sinb, first_half)
            qg = [q.astype(BF) for q in qg32]
            ex, es, inv = _swa_softmax(_swa_scores(qg, k_lo, k_hi), prev_mask, _no_prev_bias(n) if j == 0 else None, sink)
            pr, ps = ex * inv, es * inv

            dog32 = []
            for g in range(2):
                parts = []
                for i in range(2):
                    ls = slice((2 * g + i) * LANES, (2 * g + i + 1) * LANES)
                    sz = sz_ref[rows, ls]
                    sg = _sigmoid(sz)
                    dos_p = dos_ref[rows, ls]
                    dsz_ref[rows, ls] = (dos_p * opre_ref[rows, ls] * (sg * (1.0 + sz * (1.0 - sg)))).astype(dsz_ref.dtype)
                    parts.append(dos_p * (sz * sg))
                dog32.append(jnp.concatenate(parts, axis=0))
            dog = [t.astype(BF) for t in dog32]
            dpr = _merge_band(jnp.concatenate([_dot(v_lo[0], dog[0], NT), _dot(v_hi[0], dog[0], NT),
                                               _dot(v_lo[1], dog[1], NT), _dot(v_hi[1], dog[1], NT)], axis=1), prev_mask)
            rd = jnp.sum(pr * dpr, axis=0, keepdims=True)
            ds = _split_band((pr * (dpr - rd)).astype(BF), prev_mask_b)
            prb = _split_band(pr.astype(BF), prev_mask_b)
            sink_term = ps * rd
            for r, hd in enumerate(_SWA_COL_HEADS):
                dsink_ref[0, hd] += -jnp.sum(sink_term[:, r * SWA_BLOCK:(r + 1) * SWA_BLOCK])

            dk_g, dv_g = [], []
            for g in range(2):
                c0, c1, c2 = 512 * g, 512 * g + 256, 512 * g + 512
                dq = (_dot(kt_lo[g], ds[:, c0:c1]) + _dot(kt_hi[g], ds[:, c1:c2])).T
                for i in range(2):
                    ls = slice((2 * g + i) * LANES, (2 * g + i + 1) * LANES)
                    dsq_ref[rows, ls] = _rope_t(dq[i * SWA_BLOCK:(i + 1) * SWA_BLOCK] * 0.125, cosb, sinb,
                                                first_half).astype(dsq_ref.dtype)
                q_split = jnp.concatenate([jnp.where(lo1s, qg32[g], 0.0), jnp.where(lo1s, 0.0, qg32[g])], axis=0).astype(BF)
                do_split = jnp.concatenate([jnp.where(lo1s, dog32[g], 0.0), jnp.where(lo1s, 0.0, dog32[g])], axis=0).astype(BF)
                dk_g.append(_dot(ds[:, c0:c2], q_split))
                dv_g.append(_dot(prb[:, c0:c2], do_split))
            dk = home(dk_g[0], dk_g[1])
            dv = home(dv_g[0], dv_g[1])
            cur = pl.ds(pl.multiple_of(blk * SWA_BLOCK, SWA_BLOCK), SWA_BLOCK)
            dsk_ref[cur, :] = _rope_t(dk[SWA_BLOCK:], cosb, sinb, first_half)
            dsv_ref[cur, :] = dv[SWA_BLOCK:]
            dk_prev = _rope_t(dk[:SWA_BLOCK], cp_, sp_, first_half)
            dv_prev = dv[:SWA_BLOCK]
            if j == 0:
                @pl.when(n > 0)
                def _():
                    prv = pl.ds(pl.multiple_of((blk - 1) * SWA_BLOCK, SWA_BLOCK), SWA_BLOCK)
                    dsk_ref[prv, :] += dk_prev
                    dsv_ref[prv, :] += dv_prev
            else:
                prv = pl.ds(pl.multiple_of((blk - 1) * SWA_BLOCK, SWA_BLOCK), SWA_BLOCK)
                dsk_ref[prv, :] += dk_prev
                dsv_ref[prv, :] += dv_prev
            kp, vp, cp_, sp_ = kc, vc, cosb, sinb
        kprev[...] = kp
        vprev[...] = vp
        cprev[...] = cp_
        sprev[...] = sp_

    def col(width, off):
        return pl.BlockSpec((tq, width), lambda i: (i, off // width))

    row = pl.BlockSpec((tq, LANES), lambda i: (i, 0))
    wide = pl.BlockSpec((tq, 512), lambda i: (i, 0))
    return pl.pallas_call(
        body, name="swa_bwd", grid=(s // tq,),
        in_specs=[col(512, OFF_SQ), col(512, OFF_SZ), col(LANES, OFF_SK), col(LANES, OFF_SV), wide, wide, row, row,
                  pl.BlockSpec(memory_space=pltpu.SMEM), pl.BlockSpec(memory_space=pl.ANY)],
        out_specs=[wide, wide, _full((s, LANES)), _full((s, LANES)), pl.BlockSpec(memory_space=pltpu.SMEM),
                   pl.BlockSpec(memory_space=pl.ANY)],
        out_shape=[jax.ShapeDtypeStruct((s, 512), BF), jax.ShapeDtypeStruct((s, 512), BF),
                   jax.ShapeDtypeStruct((s, LANES), F32), jax.ShapeDtypeStruct((s, LANES), F32),
                   jax.ShapeDtypeStruct((1, SWA_HEADS), F32), jax.ShapeDtypeStruct((r_out, c_out), F32)],
        scratch_shapes=[pltpu.VMEM((SWA_BLOCK, LANES), F32)] * 4 + _reduce_scratch(r_out, c_out),
        compiler_params=_params(("arbitrary",)),
    )(proj, proj, proj, proj, dos, opre, cos, sin, sinks, dw_out_parts)


def _outproj(og, osw, w_out, x2d, target, gate, g_final):
    s = x2d.shape[0]
    tm = min(512, s)

    def body(og_ref, os_ref, w_ref, x_ref, t_ref, gate_ref, gf_ref,
             dx2_ref, dog_ref, dos_ref, dw_ref, loss_ref, dgf_ref, dgate_ref):
        @pl.when(pl.program_id(0) == 0)
        def _():
            dw_ref[...] = jnp.zeros_like(dw_ref)
            loss_ref[...] = jnp.zeros_like(loss_ref)
            dgf_ref[...] = jnp.zeros_like(dgf_ref)
            dgate_ref[...] = jnp.zeros_like(dgate_ref)

        w = w_ref[...]
        gate, gf = gate_ref[...], gf_ref[...]
        subs = _subtiles(tm)
        ogv = [og_ref[sl, :] for sl in subs]
        osv = [os_ref[sl, :] for sl in subs]
        y = [_dot(ogv[k], w[:512]) + _dot(osv[k], w[512:]) for k in range(len(subs))]
        dys = []
        for k, sl in enumerate(subs):
            x2 = x_ref[sl, :] + gate * y[k]
            r = lax.rsqrt(jnp.mean(x2 * x2, axis=-1, keepdims=True) + RMS_EPS)
            xn = x2 * r
            err = xn * gf - t_ref[sl, :]
            loss_ref[...] += 0.5 * jnp.sum(jnp.mean(err * err, axis=-1, keepdims=True), axis=0, keepdims=True)
            dyf = err * (1.0 / D_MODEL)
            dgf_ref[...] += jnp.sum(dyf * xn, axis=0, keepdims=True)
            t = dyf * gf
            dx2 = r * (t - xn * jnp.mean(t * xn, axis=-1, keepdims=True))
            dx2_ref[sl, :] = dx2
            dgate_ref[...] += jnp.sum(dx2 * y[k], axis=0, keepdims=True)
            dys.append((dx2 * gate).astype(BF))
            dmix = _dot(dys[k], w, NT)
            dog_ref[sl, :] = dmix[:, :512]
            dos_ref[sl, :] = dmix[:, 512:]
        dy = jnp.concatenate(dys, axis=0)
        dw_ref[:512, :] += _dot(og_ref[...], dy, TN)
        dw_ref[512:, :] += _dot(os_ref[...], dy, TN)

    half = pl.BlockSpec((tm, 512), lambda i: (i, 0))
    rowb = pl.BlockSpec((tm, D_MODEL), lambda i: (i, 0))
    vec = _full((1, D_MODEL))
    return pl.pallas_call(
        body, name="outproj", grid=(s // tm,),
        in_specs=[half, half, _full((D_MODEL, D_MODEL)), rowb, rowb, vec, vec],
        out_specs=[rowb, half, half, _full((D_MODEL, D_MODEL)), _full((1, 1)), vec, vec],
        out_shape=[jax.ShapeDtypeStruct((s, D_MODEL), F32), jax.ShapeDtypeStruct((s, 512), F32),
                   jax.ShapeDtypeStruct((s, 512), F32), jax.ShapeDtypeStruct((D_MODEL, D_MODEL), F32),
                   jax.ShapeDtypeStruct((1, 1), F32), jax.ShapeDtypeStruct((1, D_MODEL), F32),
                   jax.ShapeDtypeStruct((1, D_MODEL), F32)],
        compiler_params=_params(("arbitrary",)),
    )(og, osw, w_out, x2d, target, gate, g_final)


_PIECES = ((OFF_QK, 512), (OFF_V, 512), (OFF_GZ, 512), (OFF_SQ, 512), (OFF_SZ, 512),
           (OFF_SK, LANES), (OFF_SV, LANES), (OFF_GA, LANES))

_UNPAD_ROWS = ((OFF_QK, 0, 1024),
               (OFF_GA, 1024, GLA_RANK),
               (OFF_GZ, 1040, 1024),
               (OFF_SK, 2064, 256),
               (OFF_SZ, 2320, 512))


def _inproj_bwd(x2d, shift, sc1p, g_norm, w_t, dx2, pieces):
    s = x2d.shape[0]
    tm = min(512, s)
    nsteps = s // tm

    def body(x_ref, sh_ref, sc_ref, g_ref, w_hbm, dx2_ref, *rest):
        piece_refs = rest[:len(_PIECES)]
        gx_ref, dw_hbm, dsh_ref, dsc_ref, dg_ref, w_vm, dw_vm, in_sems, out_sems = rest[len(_PIECES):]
        i = pl.program_id(0)

        @pl.when(i == 0)
        def _():
            loads = _load_w_padded(w_hbm, w_vm, in_sems)
            dw_vm[...] = jnp.zeros_like(dw_vm)
            dsh_ref[...] = jnp.zeros_like(dsh_ref)
            dsc_ref[...] = jnp.zeros_like(dsc_ref)
            dg_ref[...] = jnp.zeros_like(dg_ref)
            for cp in loads:
                cp.wait()

        g, sc1p_v, shift_v = g_ref[...], sc_ref[...], sh_ref[...]
        subs = _subtiles(tm)
        dhs = []
        for sl in subs:
            dh = None
            for (off, width), pr in zip(_PIECES, piece_refs):
                part = _dot(pr[sl, :].astype(BF), w_vm[off:off + width, :])
                dh = part if dh is None else dh + part
            dhs.append(dh)
        norm = [_modnorm(x_ref[sl, :], g, sc1p_v, shift_v) for sl in subs]
        hb = jnp.concatenate([h.astype(BF) for _, _, h in norm], axis=0)
        for (off, width), pr in zip(_PIECES, piece_refs):
            dw_vm[off:off + width, :] += _dot(pr[...].astype(BF), hb, TN)
        for sl, (xn, r, _), dh in zip(subs, norm, dhs):
            dsh_ref[...] += jnp.sum(dh, axis=0, keepdims=True)
            dsc_ref[...] += jnp.sum(dh * (xn * g), axis=0, keepdims=True)
            dg_ref[...] += jnp.sum(dh * xn * sc1p_v, axis=0, keepdims=True)
            dxn = dh * g * sc1p_v
            gx_ref[sl, :] = dx2_ref[sl, :] + r * (dxn - xn * jnp.mean(dxn * xn, axis=-1, keepdims=True))

        @pl.when(i == nsteps - 1)
        def _():
            copies = [pltpu.make_async_copy(dw_vm.at[src:src + n], dw_hbm.at[dst:dst + n], out_sems.at[k])
                      for k, (src, dst, n) in enumerate(_UNPAD_ROWS)]
            for cp in copies:
                cp.start()
            for cp in copies:
                cp.wait()

    rowb = pl.BlockSpec((tm, D_MODEL), lambda i: (i, 0))
    vec = _full((1, D_MODEL))
    anyspec = pl.BlockSpec(memory_space=pl.ANY)
    piece_specs = [pl.BlockSpec((tm, width), lambda i: (i, 0)) for _, width in _PIECES]
    return pl.pallas_call(
        body, name="inproj_bwd", grid=(nsteps,),
        in_specs=[rowb, vec, vec, vec, anyspec, rowb] + piece_specs,
        out_specs=[rowb, anyspec, vec, vec, vec],
        out_shape=[jax.ShapeDtypeStruct((s, D_MODEL), F32), jax.ShapeDtypeStruct((D_IN, D_MODEL), F32),
                   jax.ShapeDtypeStruct((1, D_MODEL), F32), jax.ShapeDtypeStruct((1, D_MODEL), F32),
                   jax.ShapeDtypeStruct((1, D_MODEL), F32)],
        scratch_shapes=[pltpu.VMEM((D_PAD, D_MODEL), BF), pltpu.VMEM((D_PAD, D_MODEL), F32),
                        pltpu.SemaphoreType.DMA((len(_UNPAD_ROWS),)), pltpu.SemaphoreType.DMA((len(_UNPAD_ROWS),))],
        compiler_params=_params(("arbitrary",)),
    )(x2d, shift, sc1p, g_norm, w_t, dx2, *pieces)


def _adam(w, g, m, v):
    m2 = ADAM_B1 * m + (1.0 - ADAM_B1) * g
    v2 = ADAM_B2 * v + (1.0 - ADAM_B2) * (g * g)
    m_hat = m2 / (1.0 - ADAM_B1 ** ADAM_STEP)
    v_hat = v2 / (1.0 - ADAM_B2 ** ADAM_STEP)
    delta = -ADAM_LR * (m_hat / (jnp.sqrt(v_hat) + ADAM_EPS) + ADAM_WD * w)
    return delta, m2, v2


def _adamw(w, g, m, v, name):
    rr, cc = w.shape
    tc = min(512, cc)

    def body(w_ref, g_ref, m_ref, v_ref, d_ref, m2_ref, v2_ref):
        d_ref[...], m2_ref[...], v2_ref[...] = _adam(w_ref[...], g_ref[...], m_ref[...], v_ref[...])

    blk = pl.BlockSpec((rr, tc), lambda i: (0, i))
    return pl.pallas_call(
        body, name=name, grid=(cc // tc,), in_specs=[blk] * 4, out_specs=[blk] * 3,
        out_shape=[jax.ShapeDtypeStruct((rr, cc), F32)] * 3,
        compiler_params=_params(("arbitrary",)),
    )(w, g, m, v)


def _adamw_t(w3, g, m3, v3, name):
    rr, _, cc = w3.shape
    parts = [slice(q * (cc // 4), (q + 1) * (cc // 4)) for q in range(4)]

    def body(w_hbm, g_ref, m_hbm, v_hbm, d_hbm, m2_hbm, v2_hbm, g3_hbm, w_vm, m_vm, v_vm, d_vm, m2_vm, v2_vm, in_sems, out_sems):
        ins = ((w_hbm, w_vm), (m_hbm, m_vm), (v_hbm, v_vm))
        outs = ((d_vm, d_hbm), (m2_vm, m2_hbm), (v2_vm, v2_hbm), (g_ref, g3_hbm))
        loads = [[pltpu.make_async_copy(src.at[:, 0, p], dst.at[:, p], in_sems.at[3 * q + k]) for k, (src, dst) in enumerate(ins)]
                 for q, p in enumerate(parts)]
        stores = [[pltpu.make_async_copy(src.at[:, p], dst.at[:, 0, p], out_sems.at[4 * q + k]) for k, (src, dst) in enumerate(outs)]
                  for q, p in enumerate(parts)]
        for group in loads:
            for cp in group:
                cp.start()
        for q, p in enumerate(parts):
            for cp in loads[q]:
                cp.wait()
            d_vm[:, p], m2_vm[:, p], v2_vm[:, p] = _adam(w_vm[:, p], g_ref[:, p], m_vm[:, p], v_vm[:, p])
            for cp in stores[q]:
                cp.start()
        for group in stores:
            for cp in group:
                cp.wait()

    hbm = pl.BlockSpec(memory_space=pl.ANY)
    return pl.pallas_call(
        body, name=name, grid=(1,), in_specs=[hbm, pl.BlockSpec((rr, cc), lambda i: (0, 0)), hbm, hbm],
        out_specs=[hbm] * 4, out_shape=[jax.ShapeDtypeStruct((rr, 1, cc), F32)] * 4,
        scratch_shapes=[pltpu.VMEM((rr, cc), F32)] * 6 + [pltpu.SemaphoreType.DMA((12,)), pltpu.SemaphoreType.DMA((16,))],
        compiler_params=_params(("arbitrary",)),
    )(w3, g, m3, v3)


def _ada_update(c_all, dmod_cols, w, m, v):
    rr, cc = w.shape
    tr = min(512, rr)
    c_all = jnp.pad(c_all, ((0, 8), (0, 0)))
    dmod_cols = jnp.pad(dmod_cols, ((0, 8), (0, 0)))

    def body(c_ref, dm_ref, w_ref, m_ref, v_ref, g_ref, d_ref, m2_ref, v2_ref):
        cv = c_ref[...]
        sc = (cv * _sigmoid(cv)).astype(BF)
        g = _dot(sc, dm_ref[...].astype(BF), TN)
        g_ref[...] = g
        d_ref[...], m2_ref[...], v2_ref[...] = _adam(w_ref[...], g, m_ref[...], v_ref[...])

    blk = pl.BlockSpec((tr, cc), lambda i: (i, 0))
    return pl.pallas_call(
        body, name="ada_update", grid=(rr // tr,),
        in_specs=[pl.BlockSpec((16, tr), lambda i: (0, i)), _full((16, cc)), blk, blk, blk],
        out_specs=[blk] * 4, out_shape=[jax.ShapeDtypeStruct((rr, cc), F32)] * 4,
        compiler_params=_params(("arbitrary",)),
    )(c_all, dmod_cols, w, m, v)


def _small_update(parts, weights, moms, vels):
    n = len(weights)

    def body(*refs):
        p_refs, w_refs, m_refs, v_refs = refs[:n + 1], refs[n + 1:2 * n + 1], refs[2 * n + 1:3 * n + 1], refs[3 * n + 1:4 * n + 1]
        outs = refs[4 * n + 1:]
        for i in range(n):
            g = p_refs[i][0]
            for d in range(1, 8):
                g = g + p_refs[i][d]
            delta, m2, v2 = _adam(w_refs[i][...], g, m_refs[i][...], v_refs[i][...])
            outs[4 * i][...] = g
            outs[4 * i + 1][...] = delta
            outs[4 * i + 2][...] = m2
            outs[4 * i + 3][...] = v2
        tot = p_refs[n][0]
        for d in range(1, 8):
            tot = tot + p_refs[n][d]
        outs[4 * n][...] = tot

    out_shape = []
    for w in weights:
        out_shape += [jax.ShapeDtypeStruct(w.shape, F32)] * 4
    out_shape.append(jax.ShapeDtypeStruct(parts[n].shape[1:], F32))
    return pl.pallas_call(body, name="small_update", out_shape=out_shape, compiler_params=_params())(
        *parts, *weights, *moms, *vels)


def _rows8(a):
    flat = a.reshape(-1)
    rows = -(-flat.shape[0] // LANES)
    rows8 = -(-rows // 8) * 8
    flat = jnp.pad(flat, (0, rows8 * LANES - flat.shape[0]))
    return flat.reshape(rows8, LANES)


def kernel(x, c, positions, w_ada, b_ada, g_norm, w_in, w_decay, b_decay, g_gla_head, sinks, w_out, g_final, loss_target, m_w_ada, m_b_ada, m_g_norm, m_w_in, m_w_decay, m_b_decay, m_g_gla_head, m_sinks, m_w_out, m_g_final, v_w_ada, v_b_ada, v_g_norm, v_w_in, v_w_decay, v_b_decay, v_g_gla_head, v_sinks, v_w_out, v_g_final):
    ax, ay, ac = lax.axis_index("x"), lax.axis_index("y"), lax.axis_index("c")
    chip = 2 * ax + ay
    dev = 2 * chip + ac
    s = x.shape[1]
    x2d = x[0]
    target = loss_target[0]
    w_ada2, w_out2, w_dec2 = w_ada[0], w_out[0], w_decay[0]
    w_in_t = w_in[0].T
    ada_cols = w_ada2.shape[1]
    in_cols = w_in_t.shape[0]
    out_rows = w_out2.shape[0]
    half = D_MODEL // 2

    cw = jnp.concatenate([c.reshape(8, LANES), w_dec2.reshape(8, LANES)], axis=0)
    b_shard = lax.dynamic_slice(b_ada, (0, chip * ada_cols), (1, ada_cols))
    half_in = lax.dynamic_slice(w_in_t, (0, ac * half), (in_cols, half)).astype(BF)
    half_out = lax.dynamic_slice(w_out2, (ac * (out_rows // 2), 0), (out_rows // 2, D_MODEL)).astype(BF)
    inv_freq = 1.0 / (ROPE_THETA ** (jnp.arange(0, 64, 2, dtype=F32) / 64))
    first, mod_all, w_in_all, cos, sin = _prologue(
        cw, w_ada2, b_shard, half_in, positions.reshape(s, 1), jnp.tile(inv_freq, 4).reshape(1, LANES))

    first = first.reshape(8, 2, 8, LANES)
    c_all = first[:, 0].reshape(8, D_MODEL)
    w_dec_full = first[0::2, 1].reshape(4, GLA_RANK, 64).transpose(1, 0, 2).reshape(GLA_RANK, 256)
    mod = mod_all.reshape(4, 2, 8, ada_cols)[:, 0]
    mod = lax.dynamic_slice(mod, (0, dev, 0), (4, 1, ada_cols)).reshape(1, 4 * ada_cols)
    shift, sc1p, gate = mod[:, :D_MODEL], 1.0 + mod[:, D_MODEL:2 * D_MODEL], mod[:, 2 * D_MODEL:]
    w_t = w_in_all.reshape(4 * in_cols, D_MODEL)

    wdecp = jnp.pad(w_dec_full, ((0, LANES - GLA_RANK), (0, 0))).astype(BF)

    proj = _inproj_fwd(x2d, shift, sc1p, g_norm, w_t)
    og, o_gla, sprev = _gla_fwd(proj, wdecp, b_decay, g_gla_head)
    osw, o_swa, w_out_all = _swa_fwd(proj, cos, sin, sinks, half_out)
    w_out_all = w_out_all.reshape(D_MODEL, D_MODEL)
    dx2, dog, dos, dw_out, loss_p, dgf, dgate = _outproj(og, osw, w_out_all, x2d, target, gate, g_final.reshape(1, D_MODEL))
    dsq, dsz, dsk, dsv, dsinks, g_w_out = _swa_bwd(proj, dos, o_swa, cos, sin, sinks, dw_out.reshape(4, out_rows, D_MODEL))
    dqk, dv, dgz, dga, dwdp, dbd, dgg = _gla_bwd(proj, dog, o_gla, sprev, wdecp, b_decay, g_gla_head)
    pieces = (dqk, dv, dgz, dsq, dsz, dsk, dsv, dga)
    gx, dw_in_t, dshift, dscale, dgn = _inproj_bwd(x2d, shift, sc1p, g_norm, w_t, dx2, pieces)

    segs = [jnp.concatenate([dshift, dscale, dgate], axis=1), dgn, dgf, dwdp[:GLA_RANK], dbd, dgg, dsinks, loss_p]
    packed = [_rows8(a) for a in segs]
    offs = [0]
    for a in packed:
        offs.append(offs[-1] + a.shape[0])
    g_w_in_t, small = _epilogue(dw_in_t.reshape(4, in_cols, D_MODEL), jnp.concatenate(packed, axis=0))

    def seg(i, size):
        return small[:, offs[i]:offs[i + 1]].reshape(8, -1)[:, :size]

    dmod_all = seg(0, 3 * D_MODEL)
    dwd_all = lax.dynamic_slice(seg(3, GLA_RANK * 256).reshape(8, GLA_RANK, 256), (0, 0, chip * 64), (8, GLA_RANK, 64))
    parts = [dmod_all.reshape(8, 1, 3 * D_MODEL), seg(1, D_MODEL).reshape(8, 1, D_MODEL), dwd_all,
             seg(4, 256).reshape(8, 1, 256), seg(5, 512).reshape(8, 1, 512), seg(6, SWA_HEADS).reshape(8, 1, SWA_HEADS),
             seg(2, D_MODEL).reshape(8, 1, D_MODEL), seg(7, LANES).reshape(8, 1, LANES)]
    smalls = _small_update(
        parts,
        [b_ada, g_norm, w_dec2, b_decay, g_gla_head, sinks, g_final.reshape(1, D_MODEL)],
        [m_b_ada, m_g_norm, m_w_decay[0], m_b_decay, m_g_gla_head, m_sinks, m_g_final.reshape(1, D_MODEL)],
        [v_b_ada, v_g_norm, v_w_decay[0], v_b_decay, v_g_gla_head, v_sinks, v_g_final.reshape(1, D_MODEL)])
    (g_b_ada, d_b_ada, nm_b_ada, nv_b_ada, g_gn, d_gn, nm_gn, nv_gn, g_wd, d_wd, nm_wd, nv_wd,
     g_bd, d_bd, nm_bd, nv_bd, g_gg, d_gg, nm_gg, nv_gg, g_sk, d_sk, nm_sk, nv_sk,
     g_gf, d_gf, nm_gf, nv_gf, loss_row) = smalls
    loss = loss_row[0, 0]

    dmod_cols = lax.dynamic_slice(dmod_all, (0, chip * ada_cols), (8, ada_cols))
    g_w_ada, d_w_ada, nm_w_ada, nv_w_ada = _ada_update(c_all, dmod_cols, w_ada2, m_w_ada[0], v_w_ada[0])
    to3 = lambda a: jnp.transpose(a, (2, 0, 1))
    from3 = lambda a: jnp.transpose(a, (1, 2, 0))[0]
    d3, nm3, nv3, g3 = _adamw_t(to3(w_in), g_w_in_t, to3(m_w_in), to3(v_w_in), "adamw_w_in")
    g_w_in, d_w_in, nm_w_in, nv_w_in = from3(g3), from3(d3), from3(nm3), from3(nv3)
    d_w_out, nm_w_out, nv_w_out = _adamw(w_out2, g_w_out, m_w_out[0], v_w_out[0], "adamw_w_out")

    flat = lambda a: a.reshape(D_MODEL)
    grads = [g_w_ada[None], g_b_ada, g_gn, g_w_in[None], g_wd[None], g_bd, g_gg, g_sk, g_w_out[None], flat(g_gf)]
    deltas = [d_w_ada[None], d_b_ada, d_gn, d_w_in[None], d_wd[None], d_bd, d_gg, d_sk, d_w_out[None], flat(d_gf)]
    new_m = [nm_w_ada[None], nm_b_ada, nm_gn, nm_w_in[None], nm_wd[None], nm_bd, nm_gg, nm_sk, nm_w_out[None], flat(nm_gf)]
    new_v = [nv_w_ada[None], nv_b_ada, nv_gn, nv_w_in[None], nv_wd[None], nv_bd, nv_gg, nv_sk, nv_w_out[None], flat(nv_gf)]
    return (loss, gx[None], *grads, *deltas, *new_m, *new_v)
```

```python
import jax
import jax.numpy as jnp
from jax import lax
from jax.experimental import pallas as pl
from jax.experimental.pallas import tpu as pltpu

F32 = jnp.float32
BF = jnp.bfloat16

D_MODEL = 1024
GLA_HEADS = 4
GLA_DK = 64
GLA_CHUNK = 64
GLA_RANK = 16
GLA_TAU = 16.0
GLA_SUB = 256
GLA_ROWS_FWD = 1024
GLA_ROWS_BWD = 512
SWA_HEADS = 8
SWA_BLOCK = 128
SWA_QBLOCKS_FWD = 8
SWA_QBLOCKS = 8
RMS_EPS = 1e-6
ROPE_THETA = 10000.0

OFF_QK, OFF_V, OFF_GZ, OFF_SQ, OFF_SZ, OFF_SK, OFF_SV, OFF_GA = 0, 512, 1024, 1536, 2048, 2560, 2688, 2816
D_PAD = 2944
D_IN = 2832
LANES = 128
VMEM_LIMIT = 56 * 1024 * 1024

ADAM_LR, ADAM_B1, ADAM_B2, ADAM_EPS, ADAM_WD, ADAM_STEP = 0.001, 0.9, 0.999, 1e-08, 0.01, 10

NT = (((1,), (1,)), ((), ()))
TN = (((0,), (0,)), ((), ()))
MESH = pl.DeviceIdType.MESH


def _dot(a, b, dims=None):
    if dims is None:
        return jnp.dot(a, b, preferred_element_type=F32)
    return lax.dot_general(a, b, dims, preferred_element_type=F32)


def _sigmoid(x):
    return 1.0 / (1.0 + jnp.exp(-x))


def _params(sem=None):
    return pltpu.CompilerParams(dimension_semantics=sem, vmem_limit_bytes=VMEM_LIMIT)


def _full(shape):
    return pl.BlockSpec(shape, lambda i: (0,) * len(shape))


def _subtiles(rows, size=256):
    size = min(size, rows)
    return [slice(k * size, (k + 1) * size) for k in range(rows // size)]


WEIGHT_CHUNKS = 4


def _gather_sems(chunks=1):
    return [pltpu.SemaphoreType.DMA((7 * chunks,)), pltpu.SemaphoreType.DMA((7 * chunks,)), pltpu.SemaphoreType.DMA]


_GATHER_SEMS = _gather_sems()


class _Gather:
    def __init__(self, x_ref, out_ref, send_sems, recv_sems, local_sem, slab=None, chunks=1):
        self.slab_of = slab
        self.chunks = chunks
        self.width = x_ref.shape[-1] // chunks
        x, y, c = lax.axis_index("x"), lax.axis_index("y"), lax.axis_index("c")
        self.me, self.sibling, self.c = (x, y, c), (x, y, 1 - c), c
        self.xn, self.yn, self.dg = (1 - x, y), (x, 1 - y), (1 - x, 1 - y)
        self.pass_from = (lax.rem(x + 1 - c, 2), lax.rem(y + c, 2))
        self.pass_to = (lax.rem(x + c, 2), lax.rem(y + 1 - c, 2))
        self.x_ref, self.out_ref, self.send_sems, self.recv_sems = x_ref, out_ref, send_sems, recv_sems
        self.mine = pltpu.make_async_copy(x_ref, self._slab(*self.me), local_sem)

    def _slab(self, px, py, pc):
        if self.slab_of is not None:
            return self.slab_of(self.out_ref, px, py, pc)
        return self.out_ref.at[4 * px + 2 * py + pc]

    def _part(self, ref, q):
        if self.chunks == 1:
            return ref
        lanes = slice(q * self.width, (q + 1) * self.width)
        return ref.at[(slice(None),) * (len(ref.shape) - 1) + (lanes,)]

    def _copy(self, k, q, blk, to, src=None):
        i = k * self.chunks + q
        return pltpu.make_async_remote_copy(
            src_ref=self._part(self._slab(*blk) if src is None else src, q), dst_ref=self._part(self._slab(*blk), q),
            send_sem=self.send_sems.at[i], recv_sem=self.recv_sems.at[i], device_id=to, device_id_type=MESH)

    def _sends(self, q):
        c = self.c
        return [self._copy(0, q, self.me, self.sibling, src=self.x_ref),
                self._copy(1, q, self.me, (*self.xn, c), src=self.x_ref),
                self._copy(2, q, self.me, (*self.yn, c), src=self.x_ref),
                self._copy(3, q, (*self.pass_from, c), (*self.pass_to, c)),
                self._copy(4, q, (*self.xn, c), self.sibling),
                self._copy(5, q, (*self.yn, c), self.sibling),
                self._copy(6, q, (*self.dg, c), self.sibling)]

    def start(self):
        self.mine.start()
        for q in range(self.chunks):
            sends = self._sends(q)
            for k in (1, 2, 0):
                sends[k].start()

    def pass_on(self, only=None):
        for q in range(self.chunks) if only is None else (only,):
            sends = self._sends(q)
            self._copy(1, q, (*self.xn, self.c), self.me).wait_recv()
            self._copy(2, q, (*self.yn, self.c), self.me).wait_recv()
            for k in (3, 4, 5):
                sends[k].start()

    def relay_diagonal(self, only=None):
        for q in range(self.chunks) if only is None else (only,):
            self._copy(3, q, (*self.dg, self.c), self.me).wait_recv()
            self._sends(q)[6].start()

    def relay(self):
        self.pass_on()
        self.relay_diagonal()

    def finish(self):
        c = self.c
        for q in range(self.chunks):
            self._copy(0, q, self.sibling, self.me).wait_recv()
            for k, chip in ((4, self.xn), (5, self.yn), (6, self.dg)):
                self._copy(k, q, (*chip, 1 - c), self.me).wait_recv()
            for cp in self._sends(q):
                cp.wait_send()
        self.mine.wait()


def _prologue(cw, w_ada, b_shard, half_in, pos_col, inv_freq):
    s = pos_col.shape[0]
    rt = min(512, s)

    def body(cw_ref, wada_hbm, b_ref, hin_ref, pos_hbm, f_ref,
             first_ref, mod_ref, win_ref, cos_hbm, sin_hbm,
             mod_blk, cos_ref, sin_ref, wada_ref, pos_ref, table_sems, local_sems, *sems):
        fetch_w = pltpu.make_async_copy(wada_hbm, wada_ref, local_sems.at[0])
        fetch_p = pltpu.make_async_copy(pos_hbm, pos_ref, local_sems.at[1])
        fetch_w.start()
        fetch_p.start()
        g_c = _Gather(cw_ref, first_ref, *sems[0:3])
        half_lanes = hin_ref.shape[1]
        g_in = _Gather(hin_ref, win_ref, *sems[3:6], chunks=WEIGHT_CHUNKS,
                       slab=lambda ref, px, py, pc: ref.at[2 * px + py, :, pl.ds(pl.multiple_of(pc * half_lanes, half_lanes), half_lanes)])
        g_mod = _Gather(mod_blk, mod_ref, *sems[6:9])
        g_c.start()
        g_in.start()
        g_c.relay()
        g_c.finish()
        c_rows = [jnp.concatenate([first_ref[d, r:r + 1, :] for r in range(8)], axis=1) for d in range(8)]
        c_all = jnp.concatenate(c_rows, axis=0)
        sc = (c_all * _sigmoid(c_all)).astype(BF)
        fetch_w.wait()
        mod_blk[...] = _dot(sc, wada_ref[...].astype(BF)) + b_ref[...]
        g_mod.start()
        fetch_p.wait()

        def rope_rows(i, carry):
            rows = pl.ds(pl.multiple_of(i * rt, rt), rt)
            ang = pos_ref[rows, :].astype(F32) * f_ref[...]
            lane = lax.broadcasted_iota(jnp.int32, ang.shape, 1)
            cos_ref[rows, :] = jnp.cos(ang)
            sn = jnp.sin(ang)
            sin_ref[rows, :] = jnp.where((lane % 64) < 32, -sn, sn)
            pltpu.make_async_copy(cos_ref.at[rows, :], cos_hbm.at[rows, :], table_sems.at[0]).start()
            pltpu.make_async_copy(sin_ref.at[rows, :], sin_hbm.at[rows, :], table_sems.at[1]).start()
            return carry

        waits = ([lambda q=q: g_in.pass_on(q) for q in range(WEIGHT_CHUNKS)]
                 + [lambda q=q: g_in.relay_diagonal(q) for q in range(WEIGHT_CHUNKS)] + [g_mod.relay])
        steps = s // rt
        lead = steps // 4
        per_wait = max((steps - lead) // len(waits), 1)
        lax.fori_loop(0, lead, rope_rows, 0)
        done = lead
        for wait in waits:
            wait()
            nxt = min(done + per_wait, steps)
            lax.fori_loop(done, nxt, rope_rows, 0)
            done = nxt
        lax.fori_loop(done, steps, rope_rows, 0)
        g_in.finish()
        g_mod.finish()
        pltpu.make_async_copy(cos_ref, cos_hbm, table_sems.at[0]).wait()
        pltpu.make_async_copy(sin_ref, sin_hbm, table_sems.at[1]).wait()

    vm = pl.BlockSpec(memory_space=pltpu.VMEM)
    hbm = pl.BlockSpec(memory_space=pl.ANY)
    return pl.pallas_call(
        body, name="prologue",
        out_shape=[jax.ShapeDtypeStruct((8,) + cw.shape, F32), jax.ShapeDtypeStruct((8, 8, w_ada.shape[1]), F32),
                   jax.ShapeDtypeStruct((4, half_in.shape[0], 2 * half_in.shape[1]), half_in.dtype),
                   jax.ShapeDtypeStruct((s, LANES), F32), jax.ShapeDtypeStruct((s, LANES), F32)],
        in_specs=[vm, hbm, vm, hbm, hbm, vm], out_specs=[vm, vm, hbm, hbm, hbm],
        scratch_shapes=[pltpu.VMEM((8, w_ada.shape[1]), F32), pltpu.VMEM((s, LANES), F32), pltpu.VMEM((s, LANES), F32),
                        pltpu.VMEM(w_ada.shape, F32), pltpu.VMEM(pos_col.shape, jnp.int32),
                        pltpu.SemaphoreType.DMA((2,)), pltpu.SemaphoreType.DMA((2,))]
        + _GATHER_SEMS + _gather_sems(WEIGHT_CHUNKS) + _GATHER_SEMS,
        compiler_params=pltpu.CompilerParams(vmem_limit_bytes=VMEM_LIMIT),
    )(cw, w_ada, b_shard, half_in, pos_col, inv_freq)


def _reduce_scratch(rr, cc):
    c2 = cc // 2
    return [pltpu.VMEM((4, rr, c2), F32), pltpu.VMEM((4, rr, c2), F32), pltpu.VMEM((3, rr, c2), BF),
            pltpu.VMEM((2, rr, c2), BF), pltpu.VMEM((rr, c2), BF), pltpu.VMEM((rr, c2), F32),
            pltpu.SemaphoreType.DMA((8 + 3 * WEIGHT_CHUNKS,)), pltpu.SemaphoreType.DMA((8 + 3 * WEIGHT_CHUNKS,)),
            pltpu.SemaphoreType.DMA((5,))]


class _Reduce:
    def __init__(self, p_hbm, out_ref, acc_ref, own_ref, send_ref, land_ref, relay_ref, res_ref,
                 send_sems, recv_sems, local_sems, rows=None):
        x, y, c = lax.axis_index("x"), lax.axis_index("y"), lax.axis_index("c")
        part = (lambda j, ln: p_hbm.at[j, :, ln]) if rows is None else (lambda j, ln: p_hbm.at[rows(j), ln])
        c2 = out_ref.shape[1] // 2
        sibling = (x, y, 1 - c)
        first = (lax.rem(x + 1 - c, 2), lax.rem(y + c, 2))
        second = (lax.rem(x + c, 2), lax.rem(y + 1 - c, 2))
        shards = [2 * first[0] + first[1], 2 * second[0] + second[1], 2 * (1 - x) + (1 - y), 2 * x + y]
        sibling_slot = (1, 0, 2, 3)
        mine = pl.ds(pl.multiple_of(c * c2, c2), c2)
        other = pl.ds(pl.multiple_of((1 - c) * c2, c2), c2)
        self.acc_ref, self.own_ref, self.send_ref, self.land_ref = acc_ref, own_ref, send_ref, land_ref
        self.relay_ref, self.res_ref = relay_ref, res_ref
        self.own = [pltpu.make_async_copy(part(j, mine), own_ref.at[k], local_sems.at[k])
                    for k, j in enumerate(shards)]
        self.swap_out = [pltpu.make_async_remote_copy(
            src_ref=part(j, other), dst_ref=acc_ref.at[sibling_slot[k]], send_sem=send_sems.at[k],
            recv_sem=recv_sems.at[sibling_slot[k]], device_id=sibling, device_id_type=MESH) for k, j in enumerate(shards)]
        self.swap_in = [pltpu.make_async_remote_copy(
            src_ref=part(j, other), dst_ref=acc_ref.at[k], send_sem=send_sems.at[k], recv_sem=recv_sems.at[k],
            device_id=sibling, device_id_type=MESH) for k, j in enumerate(shards)]

        self.lanes = [slice(q * (c2 // WEIGHT_CHUNKS), (q + 1) * (c2 // WEIGHT_CHUNKS)) for q in range(WEIGHT_CHUNKS)]

        def message(m, src, dst, to):
            return [pltpu.make_async_remote_copy(
                src_ref=src.at[:, ln], dst_ref=dst.at[:, ln], send_sem=send_sems.at[8 + m * WEIGHT_CHUNKS + q],
                recv_sem=recv_sems.at[8 + m * WEIGHT_CHUNKS + q], device_id=(*to, c), device_id_type=MESH)
                for q, ln in enumerate(self.lanes)]

        self.direct = message(0, send_ref.at[0], land_ref.at[0], first)
        self.passed = message(1, send_ref.at[1], relay_ref, first)
        self.joint = message(2, send_ref.at[2], land_ref.at[1], second)
        self.put = pltpu.make_async_copy(res_ref, out_ref.at[:, mine], local_sems.at[4])
        self.share = pltpu.make_async_remote_copy(
            src_ref=res_ref, dst_ref=out_ref.at[:, mine], send_sem=send_sems.at[7],
            recv_sem=recv_sems.at[7], device_id=sibling, device_id_type=MESH)

    def start(self):
        for k in (2, 0, 1, 3):
            self.own[k].start()
            self.swap_out[k].start()

    def _combine(self, k):
        self.own[k].wait()
        self.swap_out[k].wait_send()
        self.swap_in[k].wait_recv()
        self.acc_ref[k] = self.acc_ref[k] + self.own_ref[k]

    def combine_and_send(self):
        dt = self.send_ref.dtype
        self._combine(2)
        self.send_ref[1] = self.acc_ref[2].astype(dt)
        for cp in self.passed:
            cp.start()
        self._combine(0)
        self.send_ref[0] = self.acc_ref[0].astype(dt)
        for cp in self.direct:
            cp.start()
        self._combine(1)
        self._combine(3)

    def send_joint(self):
        dt = self.send_ref.dtype
        for q, ln in enumerate(self.lanes):
            self.passed[q].wait_recv()
            self.send_ref[2, :, ln] = (self.acc_ref[1, :, ln] + self.relay_ref[:, ln].astype(F32)).astype(dt)
            self.joint[q].start()

    def total_and_share(self):
        for cp in self.direct + self.joint:
            cp.wait_recv()
        self.res_ref[...] = self.acc_ref[3] + self.land_ref[0].astype(F32) + self.land_ref[1].astype(F32)
        for cp in self.direct + self.passed + self.joint:
            cp.wait_send()
        self.put.start()
        self.share.start()

    def finish(self):
        self.put.wait()
        self.share.wait()


def _shard_window(n):
    return max(-(-(n * (j + 1)) // 8) * 8 - (n * j) // 8 * 8 for j in range(4))


def _epilogue(dw_in_t, small):
    cc = dw_in_t.shape[1]
    n = dw_in_t.shape[0] // 4
    r_in = _shard_window(n)
    n_red = len(_reduce_scratch(r_in, cc))

    def body(pin_hbm, small_ref, gin_ref, small_all_ref, *scratch):
        red_in = _Reduce(pin_hbm, gin_ref, *scratch[0:n_red],
                         rows=lambda j: pl.ds(pl.multiple_of((n * j) // 8 * 8, 8), r_in))
        gat = _Gather(small_ref, small_all_ref, *scratch[n_red:])
        red_in.start()
        gat.start()
        gat.relay()
        red_in.combine_and_send()
        gat.finish()
        red_in.send_joint()
        red_in.total_and_share()
        red_in.finish()

    vm = pl.BlockSpec(memory_space=pltpu.VMEM)
    anyspec = pl.BlockSpec(memory_space=pl.ANY)
    return pl.pallas_call(
        body, name="epilogue",
        out_shape=[jax.ShapeDtypeStruct((r_in, cc), F32), jax.ShapeDtypeStruct((8,) + small.shape, F32)],
        in_specs=[anyspec, vm], out_specs=[anyspec, vm],
        scratch_shapes=_reduce_scratch(r_in, cc) + _GATHER_SEMS,
        compiler_params=pltpu.CompilerParams(vmem_limit_bytes=VMEM_LIMIT),
    )(dw_in_t, small)


def _rope(t, cosb, sinb, first_half):
    partner = jnp.where(first_half, pltpu.roll(t, 96, 1), pltpu.roll(t, 32, 1))
    return t * cosb + partner * sinb


def _rope_t(g, cosb, sinb, first_half):
    gs = g * sinb
    partner = jnp.where(first_half, pltpu.roll(gs, 96, 1), pltpu.roll(gs, 32, 1))
    return g * cosb + partner


def _modnorm(x, g, sc1p, shift):
    r = lax.rsqrt(jnp.mean(x * x, axis=-1, keepdims=True) + RMS_EPS)
    xn = x * r
    return xn, r, (xn * g) * sc1p + shift


def _load_w_padded(w_hbm, w_vm, sems):
    copies = [pltpu.make_async_copy(w_hbm.at[ref:ref + n], w_vm.at[pad:pad + n], sems.at[k])
              for k, (pad, ref, n) in enumerate(_UNPAD_ROWS)]
    for cp in copies:
        cp.start()
    w_vm[OFF_GA + GLA_RANK:, :] = jnp.zeros((D_PAD - OFF_GA - GLA_RANK, D_MODEL), w_vm.dtype)
    return copies


def _inproj_fwd(x2d, shift, sc1p, g_norm, w_t):
    s = x2d.shape[0]
    tm = min(1024, s)

    def body(x_ref, sh_ref, sc_ref, g_ref, w_hbm, o_ref, w_vm, sems):
        @pl.when(pl.program_id(0) == 0)
        def _():
            for cp in _load_w_padded(w_hbm, w_vm, sems):
                cp.wait()

        subs = _subtiles(tm)
        hs = [_modnorm(x_ref[sl, :], g_ref[...], sc_ref[...], sh_ref[...])[2].astype(BF) for sl in subs]
        for sl, h in zip(subs, hs):
            o_ref[sl, :] = _dot(h, w_vm[...], NT)

    vec = _full((1, D_MODEL))
    return pl.pallas_call(
        body, name="inproj_fwd", grid=(s // tm,),
        in_specs=[pl.BlockSpec((tm, D_MODEL), lambda i: (i, 0)), vec, vec, vec, pl.BlockSpec(memory_space=pl.ANY)],
        out_specs=pl.BlockSpec((tm, D_PAD), lambda i: (i, 0)),
        out_shape=jax.ShapeDtypeStruct((s, D_PAD), F32),
        scratch_shapes=[pltpu.VMEM((D_PAD, D_MODEL), BF), pltpu.SemaphoreType.DMA((len(_UNPAD_ROWS),))],
        compiler_params=_params(("arbitrary",)),
    )(x2d, shift, sc1p, g_norm, w_t)


def _split3(a):
    hi = a.astype(BF)
    r1 = a - hi.astype(F32)
    mid = r1.astype(BF)
    lo = (r1 - mid.astype(F32)).astype(BF)
    return hi, mid, lo


def _tri_matmul(tri, a):
    hi, mid, lo = _split3(a)
    return _dot(tri, hi) + _dot(tri, mid) + _dot(tri, lo)


def _chunks(tb):
    return [slice(c * GLA_CHUNK, (c + 1) * GLA_CHUNK) for c in range(tb // GLA_CHUNK)]


def _per_chunk_rows(rows, width):
    return jnp.concatenate([jnp.broadcast_to(r, (GLA_CHUNK, width)) for r in rows], axis=0)


def _gla_triangle(tb):
    row = lax.broadcasted_iota(jnp.int32, (tb, tb), 0)
    col = lax.broadcasted_iota(jnp.int32, (tb, tb), 1)
    return (((row // GLA_CHUNK) == (col // GLA_CHUNK)) & (col <= row)).astype(F32)


def _lane_mean(x, ones_b):
    hi = x.astype(BF)
    lo = (x - hi.astype(F32)).astype(BF)
    return (_dot(hi, ones_b) + _dot(lo, ones_b)) * (1.0 / LANES)


def _head(t, h, lo_h):
    blk = t[:, LANES * (h // 2):LANES * (h // 2 + 1)]
    return jnp.where(lo_h, blk, 0.0) if h % 2 == 0 else jnp.where(lo_h, 0.0, blk)


def _gla_block_common(qk, ga, wd, bd, tril_b):
    tb = qk.shape[0]
    q, k = qk[:, :256], qk[:, 256:]
    z = _dot(ga.astype(BF), wd) + bd
    la = (jnp.minimum(z, 0.0) - jnp.log(1.0 + jnp.exp(-jnp.abs(z)))) * (1.0 / GLA_TAU)
    b = _tri_matmul(tril_b, la)
    bls = [b[rs.stop - 1:rs.stop, :] for rs in _chunks(tb)]
    eq = jnp.exp(b)
    ek = jnp.exp(-b)
    f = jnp.exp(_per_chunk_rows(bls, 256) - b)
    return z, eq, ek, f, q * (eq * GLA_DK ** -0.5), k * ek, k * f, bls


def _gla_units(s, rows):
    sub = min(GLA_SUB, s)
    tb = min(rows, s)
    subs = [slice(i * sub, (i + 1) * sub) for i in range(tb // sub)]
    units = [(i, h) for i in range(len(subs)) for h in range(GLA_HEADS)]
    return tb, sub, subs, units


def _gla_fwd(proj, wdecp, bdec, ggla):
    s = proj.shape[0]
    tb, sub, subs, units = _gla_units(s, GLA_ROWS_FWD)
    nch = sub // GLA_CHUNK

    def body(qk_ref, v_ref, gz_ref, ga_ref, wd_ref, bd_ref, gg_ref, tri_ref, og_ref, opre_ref, sprev_ref, st_ref):
        @pl.when(pl.program_id(0) == 0)
        def _():
            st_ref[...] = jnp.zeros_like(st_ref)

        lo_h = lax.broadcasted_iota(jnp.int32, (sub, LANES), 1) < GLA_DK
        tril = tri_ref[...] > 0.5
        tril_b = tri_ref[...].astype(BF)
        ones_b = jnp.ones((LANES, LANES), BF)
        gg, wd, bd = gg_ref[...], wd_ref[...], bd_ref[...]
        chunks = _chunks(sub)
        lanes = [slice(h * LANES, (h + 1) * LANES) for h in range(GLA_HEADS)]
        com = [_gla_block_common(qk_ref[sl, :], ga_ref[sl, :], wd, bd, tril_b) for sl in subs]
        decs = [[jnp.exp(bl) for bl in cm[7]] for cm in com]
        a = {(i, h): _head(com[i][4], h, lo_h).astype(BF) for i, h in units}
        bm = {(i, h): _head(com[i][5], h, lo_h).astype(BF) for i, h in units}
        ktl = {(i, h): _head(com[i][6], h, lo_h).astype(BF) for i, h in units}
        vh = {(i, h): v_ref[subs[i], lanes[h]].astype(BF) for i, h in units}
        sc = {u: _dot(a[u], bm[u], NT) for u in units}
        upd = {u: [_dot(vh[u][rs], ktl[u][rs], TN) for rs in chunks] for u in units}
        p = {u: jnp.where(tril, sc[u], 0.0).astype(BF) for u in units}
        o = {u: _dot(p[u], vh[u]) for u in units}
        states = {}
        for h in range(GLA_HEADS):
            st = st_ref[h]
            for i in range(len(subs)):
                entering = []
                for c in range(nch):
                    entering.append(st)
                    sprev_ref[i * nch + c, h] = st
                    st = st * decs[i][c][:, LANES * (h // 2):LANES * (h // 2 + 1)] + upd[(i, h)][c]
                states[(i, h)] = entering
            st_ref[h] = st
        inter = {u: [_dot(a[u][rs], states[u][c].astype(BF), NT) for c, rs in enumerate(chunks)] for u in units}
        o = {u: o[u] + jnp.concatenate(inter[u], axis=0) for u in units}
        ms = {u: _lane_mean(o[u] * o[u], ones_b) for u in units}
        for i, h in units:
            gzh = gz_ref[subs[i], lanes[h]]
            opre_ref[subs[i], lanes[h]] = o[(i, h)]
            og_ref[subs[i], lanes[h]] = (((o[(i, h)] * lax.rsqrt(ms[(i, h)] + RMS_EPS)) * gg[:, lanes[h]])
                                         * (gzh * _sigmoid(gzh))).astype(og_ref.dtype)

    def col(width, off):
        return pl.BlockSpec((tb, width), lambda i: (i, off // width))

    return pl.pallas_call(
        body, name="gla_fwd", grid=(s // tb,),
        in_specs=[col(512, OFF_QK), col(512, OFF_V), col(512, OFF_GZ), col(LANES, OFF_GA),
                  _full((LANES, 256)), _full((1, 256)), _full((1, 512)), _full((sub, sub))],
        out_specs=[pl.BlockSpec((tb, 512), lambda i: (i, 0)), pl.BlockSpec((tb, 512), lambda i: (i, 0)),
                   pl.BlockSpec((tb // GLA_CHUNK, GLA_HEADS, LANES, LANES), lambda i: (i, 0, 0, 0))],
        out_shape=[jax.ShapeDtypeStruct((s, 512), BF), jax.ShapeDtypeStruct((s, 512), F32),
                   jax.ShapeDtypeStruct((s // GLA_CHUNK, GLA_HEADS, LANES, LANES), F32)],
        scratch_shapes=[pltpu.VMEM((GLA_HEADS, LANES, LANES), F32)],
        compiler_params=_params(("arbitrary",)),
    )(proj, proj, proj, proj, wdecp, bdec, ggla, _gla_triangle(sub))


def _gla_bwd(proj, dog, opre, sprev, wdecp, bdec, ggla):
    s = proj.shape[0]
    tb, sub, subs, units = _gla_units(s, GLA_ROWS_BWD)
    nsub = len(subs)
    nch = sub // GLA_CHUNK
    nb = s // tb

    def body(qk_ref, v_ref, gz_ref, ga_ref, dog_ref, opre_ref, sprev_ref, wd_ref, bd_ref, gg_ref, tri_ref, triu_ref,
             dqk_ref, dv_ref, dgz_ref, dga_ref, dwd_ref, dbd_ref, dgg_ref, dst_ref):
        @pl.when(pl.program_id(0) == 0)
        def _():
            dst_ref[...] = jnp.zeros_like(dst_ref)
            dwd_ref[...] = jnp.zeros_like(dwd_ref)
            dbd_ref[...] = jnp.zeros_like(dbd_ref)
            dgg_ref[...] = jnp.zeros_like(dgg_ref)

        lo_h = lax.broadcasted_iota(jnp.int32, (sub, LANES), 1) < GLA_DK
        tril = tri_ref[...] > 0.5
        tril_b = tri_ref[...].astype(BF)
        triu_b = triu_ref[...].astype(BF)
        ones_b = jnp.ones((LANES, LANES), BF)
        last_row = (lax.broadcasted_iota(jnp.int32, (sub, LANES), 0) % GLA_CHUNK) == GLA_CHUNK - 1
        wd, gg, bd = wd_ref[...], gg_ref[...], bd_ref[...]
        chunks = _chunks(sub)
        lanes = [slice(h * LANES, (h + 1) * LANES) for h in range(GLA_HEADS)]
        blks = [slice(LANES * (h // 2), LANES * (h // 2 + 1)) for h in range(GLA_HEADS)]
        ga = [ga_ref[sl, :] for sl in subs]
        com = [_gla_block_common(qk_ref[sl, :], ga[i], wd, bd, tril_b) for i, sl in enumerate(subs)]
        decs = [[jnp.exp(bl) for bl in cm[7]] for cm in com]
        a = {(i, h): _head(com[i][4], h, lo_h).astype(BF) for i, h in units}
        bm = {(i, h): _head(com[i][5], h, lo_h).astype(BF) for i, h in units}
        ktl = {(i, h): _head(com[i][6], h, lo_h).astype(BF) for i, h in units}
        vh = {(i, h): v_ref[subs[i], lanes[h]].astype(BF) for i, h in units}
        sc = {u: _dot(a[u], bm[u], NT) for u in units}

        o = {(i, h): opre_ref[subs[i], lanes[h]] for i, h in units}
        ms = {u: _lane_mean(o[u] * o[u], ones_b) for u in units}
        gz = {(i, h): gz_ref[subs[i], lanes[h]] for i, h in units}
        dog = {(i, h): dog_ref[subs[i], lanes[h]] for i, h in units}
        sg = {u: _sigmoid(gz[u]) for u in units}
        r = {u: lax.rsqrt(ms[u] + RMS_EPS) for u in units}
        ohat = {u: o[u] * r[u] for u in units}
        sil = {u: gz[u] * sg[u] for u in units}
        for i, h in units:
            u = (i, h)
            dgz_ref[subs[i], lanes[h]] = (dog[u] * (ohat[u] * gg[:, lanes[h]])
                                          * (sg[u] * (1.0 + gz[u] * (1.0 - sg[u])))).astype(dgz_ref.dtype)
            dgg_ref[:, lanes[h]] += jnp.sum(dog[u] * sil[u] * ohat[u], axis=0, keepdims=True)
        dn = {(i, h): dog[(i, h)] * sil[(i, h)] * gg[:, lanes[h]] for i, h in units}
        mdn = {u: _lane_mean(dn[u] * ohat[u], ones_b) for u in units}
        do = {u: (r[u] * (dn[u] - ohat[u] * mdn[u])).astype(BF) for u in units}

        p = {u: jnp.where(tril, sc[u], 0.0).astype(BF) for u in units}
        dpr = {u: _dot(do[u], vh[u], NT) for u in units}
        incr = {u: [_dot(do[u][rs], a[u][rs], TN) for rs in chunks] for u in units}
        dv = {u: _dot(p[u], do[u], TN) for u in units}
        dp = {u: jnp.where(tril, dpr[u], 0.0).astype(BF) for u in units}
        dqd = {u: _dot(dp[u], bm[u]) for u in units}
        dkd = {u: _dot(dp[u], a[u], TN) for u in units}
        st = {(i, h): [sprev_ref[i * nch + c, h] for c in range(nch)] for i, h in units}
        leaving = {}
        for h in range(GLA_HEADS):
            d = dst_ref[h]
            for i in reversed(range(nsub)):
                out = [None] * nch
                for c in reversed(range(nch)):
                    out[c] = d
                    d = d * decs[i][c][:, blks[h]] + incr[(i, h)][c]
                leaving[(i, h)] = out
            dst_ref[h] = d
        lv_b = {u: [leaving[u][c].astype(BF) for c in range(nch)] for u in units}
        dv_s = {u: [_dot(ktl[u][rs], lv_b[u][c], NT) for c, rs in enumerate(chunks)] for u in units}
        dqd_s = {u: [_dot(do[u][rs], st[u][c].astype(BF)) for c, rs in enumerate(chunks)] for u in units}
        dkt_s = {u: [_dot(vh[u][rs], lv_b[u][c]) for c, rs in enumerate(chunks)] for u in units}
        ddec = {u: [jnp.sum(leaving[u][c] * st[u][c], axis=0, keepdims=True) for c in range(nch)] for u in units}
        for i, h in units:
            dv_ref[subs[i], lanes[h]] = (dv[(i, h)] + jnp.concatenate(dv_s[(i, h)], axis=0)).astype(dv_ref.dtype)
        dqd = {u: dqd[u] + jnp.concatenate(dqd_s[u], axis=0) for u in units}
        dkt = {u: jnp.concatenate(dkt_s[u], axis=0) for u in units}

        db = []
        for i, sl in enumerate(subs):
            _, eq, ek, f, qd, kd, kt, _ = com[i]
            parts = []
            for pair in range(GLA_HEADS // 2):
                blk, u0, u1 = blks[2 * pair], (i, 2 * pair), (i, 2 * pair + 1)
                dqd_b, dkd_b, dkt_b = dqd[u0] + dqd[u1], dkd[u0] + dkd[u1], dkt[u0] + dkt[u1]
                dqk_ref[sl, blk] = (dqd_b * (eq[:, blk] * GLA_DK ** -0.5)).astype(dqk_ref.dtype)
                dqk_ref[sl, 256 + LANES * pair:256 + LANES * (pair + 1)] = (dkd_b * ek[:, blk] + dkt_b * f[:, blk]).astype(dqk_ref.dtype)
                dkt_kt = dkt_b * kt[:, blk]
                dbp = dqd_b * qd[:, blk] - dkd_b * kd[:, blk] - dkt_kt
                dbl = [jnp.sum(dkt_kt[rs], axis=0, keepdims=True) + (ddec[u0][c] + ddec[u1][c]) * decs[i][c][:, blk]
                       for c, rs in enumerate(chunks)]
                parts.append(jnp.where(last_row, dbp + _per_chunk_rows(dbl, LANES), dbp))
            db.append(jnp.concatenate(parts, axis=1))
        dla = [_tri_matmul(triu_b, db[i]) for i in range(nsub)]
        dz32 = [dla[i] * (1.0 / GLA_TAU) * _sigmoid(-com[i][0]) for i in range(nsub)]
        dz = [t.astype(BF) for t in dz32]
        for i, sl in enumerate(subs):
            dga_ref[sl, :] = _dot(dz[i], wd, NT).astype(dga_ref.dtype)
            dwd_ref[...] += _dot(ga[i].astype(BF), dz[i], TN)
            dbd_ref[...] += jnp.sum(dz32[i], axis=0, keepdims=True)

    def col(width, off):
        return pl.BlockSpec((tb, width), lambda i: (nb - 1 - i, off // width))

    def rev(width):
        return pl.BlockSpec((tb, width), lambda i: (nb - 1 - i, 0))

    return pl.pallas_call(
        body, name="gla_bwd", grid=(nb,),
        in_specs=[col(512, OFF_QK), col(512, OFF_V), col(512, OFF_GZ), col(LANES, OFF_GA), rev(512), rev(512),
                  pl.BlockSpec((tb // GLA_CHUNK, GLA_HEADS, LANES, LANES), lambda i: (nb - 1 - i, 0, 0, 0)),
                  _full((LANES, 256)), _full((1, 256)), _full((1, 512)), _full((sub, sub)), _full((sub, sub))],
        out_specs=[rev(512), rev(512), rev(512), rev(LANES), _full((LANES, 256)), _full((1, 256)), _full((1, 512))],
        out_shape=[jax.ShapeDtypeStruct((s, 512), BF), jax.ShapeDtypeStruct((s, 512), BF),
                   jax.ShapeDtypeStruct((s, 512), BF), jax.ShapeDtypeStruct((s, LANES), BF),
                   jax.ShapeDtypeStruct((LANES, 256), F32), jax.ShapeDtypeStruct((1, 256), F32),
                   jax.ShapeDtypeStruct((1, 512), F32)],
        scratch_shapes=[pltpu.VMEM((GLA_HEADS, LANES, LANES), F32)],
        compiler_params=_params(("arbitrary",)),
    )(proj, proj, proj, proj, dog, opre, sprev, wdecp, bdec, ggla, _gla_triangle(sub), _gla_triangle(sub).T)


_SWA_COL_HEADS = (0, 2, 1, 3, 4, 6, 5, 7)
_SWA_COLS = SWA_HEADS * SWA_BLOCK


def _swa_masks():
    lo2 = lax.broadcasted_iota(jnp.int32, (2 * SWA_BLOCK, LANES), 1) < 64
    lane1 = lax.broadcasted_iota(jnp.int32, (SWA_BLOCK, LANES), 1)
    first_half = (lane1 % 64) < 32
    key = lax.broadcasted_iota(jnp.int32, (SWA_BLOCK, _SWA_COLS), 0)
    query = lax.broadcasted_iota(jnp.int32, (SWA_BLOCK, _SWA_COLS), 1) % SWA_BLOCK
    return lo2, lane1 < 64, first_half, key > query


def _merge_band(t, prev_mask, prev_bias=None):
    prev = t[:SWA_BLOCK] if prev_bias is None else t[:SWA_BLOCK] + prev_bias
    return jnp.where(prev_mask, prev, t[SWA_BLOCK:])


def _split_band(t, prev_mask_b):
    prev = t * prev_mask_b
    return jnp.concatenate([prev, t - prev], axis=0)


def _kv_variants(t, lo2):
    tr = pltpu.roll(t, 64, 1)
    lo_v = [jnp.where(lo2, t, 0.0).astype(BF), jnp.where(lo2, tr, 0.0).astype(BF)]
    hi_v = [jnp.where(lo2, 0.0, tr).astype(BF), jnp.where(lo2, 0.0, t).astype(BF)]
    return lo_v, hi_v


def _kv_variants_t(t):
    tt = t.T
    sw = jnp.concatenate([tt[64:], tt[:64]], axis=0)
    top = lax.broadcasted_iota(jnp.int32, tt.shape, 0) < 64
    lo_v = [jnp.where(top, tt, 0.0).astype(BF), jnp.where(top, sw, 0.0).astype(BF)]
    hi_v = [jnp.where(top, 0.0, sw).astype(BF), jnp.where(top, 0.0, tt).astype(BF)]
    return lo_v, hi_v


def _swa_scores(qg, k_lo, k_hi):
    return jnp.concatenate([_dot(k_lo[0], qg[0], NT), _dot(k_hi[0], qg[0], NT),
                            _dot(k_lo[1], qg[1], NT), _dot(k_hi[1], qg[1], NT)], axis=1)


def _sink_row(sinks_ref):
    return jnp.concatenate([jnp.full((1, SWA_BLOCK), sinks_ref[0, hd], F32) for hd in _SWA_COL_HEADS], axis=1)


def _swa_softmax(st, prev_mask, prev_bias, sink):
    st = _merge_band(st, prev_mask, prev_bias)
    m = jnp.maximum(jnp.max(st, axis=0, keepdims=True), sink)
    ex = jnp.exp(st - m)
    es = jnp.exp(sink - m)
    inv = 1.0 / (jnp.sum(ex, axis=0, keepdims=True) + es)
    return ex, es, inv


def _no_prev_bias(block_index):
    return jnp.where(block_index > 0, 0.0, -1e30).astype(F32)


def _swa_queries(sq_ref, rows, cosb, sinb, first_half):
    qs = [_rope(sq_ref[rows, p * LANES:(p + 1) * LANES], cosb, sinb, first_half) * 0.125 for p in range(4)]
    return [jnp.concatenate(qs[0:2], axis=0), jnp.concatenate(qs[2:4], axis=0)]


def _phase_steps(nsteps, phases):
    return [min(nsteps - 1, (k * nsteps) // phases) for k in range(phases - 1)] + [nsteps - 1]


def _swa_fwd(proj, cos, sin, sinks, half_out):
    s = proj.shape[0]
    nq = min(SWA_QBLOCKS_FWD, s // SWA_BLOCK)
    tq = nq * SWA_BLOCK
    steps = _phase_steps(s // tq, 4)

    def body(sq_ref, sz_ref, sk_ref, sv_ref, cos_ref, sin_ref, sinks_ref, hout_hbm, os_ref, opre_ref, wout_hbm,
             kprev, vprev, *gather_sems):
        n = pl.program_id(0)

        @pl.when(n == 0)
        def _():
            kprev[...] = jnp.zeros_like(kprev)
            vprev[...] = jnp.zeros_like(vprev)

        gather = _Gather(hout_hbm, wout_hbm, *gather_sems, chunks=WEIGHT_CHUNKS)
        for step, phase in zip(steps, (gather.start, gather.pass_on, gather.relay_diagonal, gather.finish)):
            pl.when(n == step)(phase)

        lo2, _, first_half, prev_mask = _swa_masks()
        prev_mask_b = jnp.where(prev_mask, 1.0, 0.0).astype(BF)
        sink = _sink_row(sinks_ref)
        blocks = range(nq)
        rows = [slice(j * SWA_BLOCK, (j + 1) * SWA_BLOCK) for j in blocks]
        cosb = [cos_ref[rows[j], :] for j in blocks]
        sinb = [sin_ref[rows[j], :] for j in blocks]
        kc = [_rope(sk_ref[rows[j], :], cosb[j], sinb[j], first_half) for j in blocks]
        vc = [sv_ref[rows[j], :] for j in blocks]
        kcat = [jnp.concatenate([kprev[...] if j == 0 else kc[j - 1], kc[j]], axis=0) for j in blocks]
        vcat = [jnp.concatenate([vprev[...] if j == 0 else vc[j - 1], vc[j]], axis=0) for j in blocks]
        kprev[...] = kc[-1]
        vprev[...] = vc[-1]
        kvar = [_kv_variants(kcat[j], lo2) for j in blocks]
        vtvar = [_kv_variants_t(vcat[j]) for j in blocks]
        qg = [[q.astype(BF) for q in _swa_queries(sq_ref, rows[j], cosb[j], sinb[j], first_half)] for j in blocks]
        st = [_swa_scores(qg[j], *kvar[j]) for j in blocks]
        soft = [_swa_softmax(st[j], prev_mask, _no_prev_bias(n) if j == 0 else None, sink) for j in blocks]
        pt = [_split_band(soft[j][0].astype(BF), prev_mask_b) for j in blocks]
        og = {}
        for j in blocks:
            inv = soft[j][2]
            for g in range(2):
                c0, c1, c2 = 512 * g, 512 * g + 256, 512 * g + 512
                ot = (_dot(vtvar[j][0][g], pt[j][:, c0:c1]) * inv[:, c0:c1]
                      + _dot(vtvar[j][1][g], pt[j][:, c1:c2]) * inv[:, c1:c2])
                og[(j, g)] = ot.T
        for j in blocks:
            for g in range(2):
                for i in range(2):
                    ls = slice((2 * g + i) * LANES, (2 * g + i + 1) * LANES)
                    o = og[(j, g)][i * SWA_BLOCK:(i + 1) * SWA_BLOCK]
                    sz = sz_ref[rows[j], ls]
                    opre_ref[rows[j], ls] = o
                    os_ref[rows[j], ls] = (o * (sz * _sigmoid(sz))).astype(os_ref.dtype)

    def col(width, off):
        return pl.BlockSpec((tq, width), lambda i: (i, off // width))

    row = pl.BlockSpec((tq, LANES), lambda i: (i, 0))
    return pl.pallas_call(
        body, name="swa_fwd", grid=(s // tq,),
        in_specs=[col(512, OFF_SQ), col(512, OFF_SZ), col(LANES, OFF_SK), col(LANES, OFF_SV), row, row,
                  pl.BlockSpec(memory_space=pltpu.SMEM), pl.BlockSpec(memory_space=pl.ANY)],
        out_specs=[pl.BlockSpec((tq, 512), lambda i: (i, 0))] * 2 + [pl.BlockSpec(memory_space=pl.ANY)],
        out_shape=[jax.ShapeDtypeStruct((s, 512), BF), jax.ShapeDtypeStruct((s, 512), F32),
                   jax.ShapeDtypeStruct((8,) + half_out.shape, half_out.dtype)],
        scratch_shapes=[pltpu.VMEM((SWA_BLOCK, LANES), F32)] * 2 + _gather_sems(WEIGHT_CHUNKS),
        compiler_params=_params(("arbitrary",)),
    )(proj, proj, proj, proj, cos, sin, sinks, half_out)


def _swa_bwd(proj, dos, opre, cos, sin, sinks, dw_out_parts):
    s = proj.shape[0]
    nq = min(SWA_QBLOCKS, s // SWA_BLOCK)
    tq = nq * SWA_BLOCK
    steps = _phase_steps(s // tq, 5)
    _, r_out, c_out = dw_out_parts.shape

    def body(sq_ref, sz_ref, sk_ref, sv_ref, dos_ref, opre_ref, cos_ref, sin_ref, sinks_ref, pout_hbm,
             dsq_ref, dsz_ref, dsk_ref, dsv_ref, dsink_ref, gout_hbm, kprev, vprev, cprev, sprev, *reduce_scratch):
        n = pl.program_id(0)

        @pl.when(n == 0)
        def _():
            kprev[...] = jnp.zeros_like(kprev)
            vprev[...] = jnp.zeros_like(vprev)
            cprev[...] = jnp.zeros_like(cprev)
            sprev[...] = jnp.zeros_like(sprev)
            for hd in range(SWA_HEADS):
                dsink_ref[0, hd] = 0.0

        reduce = _Reduce(pout_hbm, gout_hbm, *reduce_scratch)
        phases = (reduce.start, reduce.combine_and_send, reduce.send_joint, reduce.total_and_share, reduce.finish)
        for step, phase in zip(steps, phases):
            pl.when(n == step)(phase)

        lo2, lo1, first_half, prev_mask = _swa_masks()
        prev_mask_b = jnp.where(prev_mask, 1.0, 0.0).astype(BF)
        lo1s = jnp.concatenate([lo1, lo1], axis=0)
        sink = _sink_row(sinks_ref)

        def home(m0, m1):
            t0 = m0 + pltpu.roll(m0, 64, 1)
            t1 = m1 + pltpu.roll(m1, 64, 1)
            return jnp.where(lo2, t0, t1)

        kp, vp, cp_, sp_ = kprev[...], vprev[...], cprev[...], sprev[...]
        for j in range(nq):
            rows = slice(j * SWA_BLOCK, (j + 1) * SWA_BLOCK)
            blk = n * nq + j
            cosb, sinb = cos_ref[rows, :], sin_ref[rows, :]
            kc = _rope(sk_ref[rows, :], cosb, sinb, first_half)
            vc = sv_ref[rows, :]
            kcat = jnp.concatenate([kp, kc], axis=0)
            k_lo, k_hi = _kv_variants(kcat, lo2)
            kt_lo, kt_hi = _kv_variants_t(kcat)
            v_lo, v_hi = _kv_variants(jnp.concatenate([vp, vc], axis=0), lo2)
            qg32 = _swa_queries(sq_ref, rows, cosb, sinb, first_half)
            qg = [q.astype(BF) for q in qg32]
            ex, es, inv = _swa_softmax(_swa_scores(qg, k_lo, k_hi), prev_mask, _no_prev_bias(n) if j == 0 else None, sink)
            pr, ps = ex * inv, es * inv

            dog32 = []
            for g in range(2):
                parts = []
                for i in range(2):
                    ls = slice((2 * g + i) * LANES, (2 * g + i + 1) * LANES)
                    sz = sz_ref[rows, ls]
                    sg = _sigmoid(sz)
                    dos_p = dos_ref[rows, ls]
                    dsz_ref[rows, ls] = (dos_p * opre_ref[rows, ls] * (sg * (1.0 + sz * (1.0 - sg)))).astype(dsz_ref.dtype)
                    parts.append(dos_p * (sz * sg))
                dog32.append(jnp.concatenate(parts, axis=0))
            dog = [t.astype(BF) for t in dog32]
            dpr = _merge_band(jnp.concatenate([_dot(v_lo[0], dog[0], NT), _dot(v_hi[0], dog[0], NT),
                                               _dot(v_lo[1], dog[1], NT), _dot(v_hi[1], dog[1], NT)], axis=1), prev_mask)
            rd = jnp.sum(pr * dpr, axis=0, keepdims=True)
            ds = _split_band((pr * (dpr - rd)).astype(BF), prev_mask_b)
            prb = _split_band(pr.astype(BF), prev_mask_b)
            sink_term = ps * rd
            for r, hd in enumerate(_SWA_COL_HEADS):
                dsink_ref[0, hd] += -jnp.sum(sink_term[:, r * SWA_BLOCK:(r + 1) * SWA_BLOCK])

            dk_g, dv_g = [], []
            for g in range(2):
                c0, c1, c2 = 512 * g, 512 * g + 256, 512 * g + 512
                dq = (_dot(kt_lo[g], ds[:, c0:c1]) + _dot(kt_hi[g], ds[:, c1:c2])).T
                for i in range(2):
                    ls = slice((2 * g + i) * LANES, (2 * g + i + 1) * LANES)
                    dsq_ref[rows, ls] = _rope_t(dq[i * SWA_BLOCK:(i + 1) * SWA_BLOCK] * 0.125, cosb, sinb,
                                                first_half).astype(dsq_ref.dtype)
                q_split = jnp.concatenate([jnp.where(lo1s, qg32[g], 0.0), jnp.where(lo1s, 0.0, qg32[g])], axis=0).astype(BF)
                do_split = jnp.concatenate([jnp.where(lo1s, dog32[g], 0.0), jnp.where(lo1s, 0.0, dog32[g])], axis=0).astype(BF)
                dk_g.append(_dot(ds[:, c0:c2], q_split))
                dv_g.append(_dot(prb[:, c0:c2], do_split))
            dk = home(dk_g[0], dk_g[1])
            dv = home(dv_g[0], dv_g[1])
            cur = pl.ds(pl.multiple_of(blk * SWA_BLOCK, SWA_BLOCK), SWA_BLOCK)
            dsk_ref[cur, :] = _rope_t(dk[SWA_BLOCK:], cosb, sinb, first_half)
            dsv_ref[cur, :] = dv[SWA_BLOCK:]
            dk_prev = _rope_t(dk[:SWA_BLOCK], cp_, sp_, first_half)
            dv_prev = dv[:SWA_BLOCK]
            if j == 0:
                @pl.when(n > 0)
                def _():
                    prv = pl.ds(pl.multiple_of((blk - 1) * SWA_BLOCK, SWA_BLOCK), SWA_BLOCK)
                    dsk_ref[prv, :] += dk_prev
                    dsv_ref[prv, :] += dv_prev
            else:
                prv = pl.ds(pl.multiple_of((blk - 1) * SWA_BLOCK, SWA_BLOCK), SWA_BLOCK)
                dsk_ref[prv, :] += dk_prev
                dsv_ref[prv, :] += dv_prev
            kp, vp, cp_, sp_ = kc, vc, cosb, sinb
        kprev[...] = kp
        vprev[...] = vp
        cprev[...] = cp_
        sprev[...] = sp_

    def col(width, off):
        return pl.BlockSpec((tq, width), lambda i: (i, off // width))

    row = pl.BlockSpec((tq, LANES), lambda i: (i, 0))
    wide = pl.BlockSpec((tq, 512), lambda i: (i, 0))
    return pl.pallas_call(
        body, name="swa_bwd", grid=(s // tq,),
        in_specs=[col(512, OFF_SQ), col(512, OFF_SZ), col(LANES, OFF_SK), col(LANES, OFF_SV), wide, wide, row, row,
                  pl.BlockSpec(memory_space=pltpu.SMEM), pl.BlockSpec(memory_space=pl.ANY)],
        out_specs=[wide, wide, _full((s, LANES)), _full((s, LANES)), pl.BlockSpec(memory_space=pltpu.SMEM),
                   pl.BlockSpec(memory_space=pl.ANY)],
        out_shape=[jax.ShapeDtypeStruct((s, 512), BF), jax.ShapeDtypeStruct((s, 512), BF),
                   jax.ShapeDtypeStruct((s, LANES), F32), jax.ShapeDtypeStruct((s, LANES), F32),
                   jax.ShapeDtypeStruct((1, SWA_HEADS), F32), jax.ShapeDtypeStruct((r_out, c_out), F32)],
        scratch_shapes=[pltpu.VMEM((SWA_BLOCK, LANES), F32)] * 4 + _reduce_scratch(r_out, c_out),
        compiler_params=_params(("arbitrary",)),
    )(proj, proj, proj, proj, dos, opre, cos, sin, sinks, dw_out_parts)


def _outproj(og, osw, w_out, x2d, target, gate, g_final):
    s = x2d.shape[0]
    tm = min(512, s)

    def body(og_ref, os_ref, w_ref, x_ref, t_ref, gate_ref, gf_ref,
             dx2_ref, dog_ref, dos_ref, dw_ref, loss_ref, dgf_ref, dgate_ref):
        @pl.when(pl.program_id(0) == 0)
        def _():
            dw_ref[...] = jnp.zeros_like(dw_ref)
            loss_ref[...] = jnp.zeros_like(loss_ref)
            dgf_ref[...] = jnp.zeros_like(dgf_ref)
            dgate_ref[...] = jnp.zeros_like(dgate_ref)

        w = w_ref[...]
        gate, gf = gate_ref[...], gf_ref[...]
        subs = _subtiles(tm)
        ogv = [og_ref[sl, :] for sl in subs]
        osv = [os_ref[sl, :] for sl in subs]
        y = [_dot(ogv[k], w[:512]) + _dot(osv[k], w[512:]) for k in range(len(subs))]
        dys = []
        for k, sl in enumerate(subs):
            x2 = x_ref[sl, :] + gate * y[k]
            r = lax.rsqrt(jnp.mean(x2 * x2, axis=-1, keepdims=True) + RMS_EPS)
            xn = x2 * r
            err = xn * gf - t_ref[sl, :]
            loss_ref[...] += 0.5 * jnp.sum(jnp.mean(err * err, axis=-1, keepdims=True), axis=0, keepdims=True)
            dyf = err * (1.0 / D_MODEL)
            dgf_ref[...] += jnp.sum(dyf * xn, axis=0, keepdims=True)
            t = dyf * gf
            dx2 = r * (t - xn * jnp.mean(t * xn, axis=-1, keepdims=True))
            dx2_ref[sl, :] = dx2
            dgate_ref[...] += jnp.sum(dx2 * y[k], axis=0, keepdims=True)
            dys.append((dx2 * gate).astype(BF))
            dmix = _dot(dys[k], w, NT)
            dog_ref[sl, :] = dmix[:, :512]
            dos_ref[sl, :] = dmix[:, 512:]
        dy = jnp.concatenate(dys, axis=0)
        dw_ref[:512, :] += _dot(og_ref[...], dy, TN)
        dw_ref[512:, :] += _dot(os_ref[...], dy, TN)

    half = pl.BlockSpec((tm, 512), lambda i: (i, 0))
    rowb = pl.BlockSpec((tm, D_MODEL), lambda i: (i, 0))
    vec = _full((1, D_MODEL))
    return pl.pallas_call(
        body, name="outproj", grid=(s // tm,),
        in_specs=[half, half, _full((D_MODEL, D_MODEL)), rowb, rowb, vec, vec],
        out_specs=[rowb, half, half, _full((D_MODEL, D_MODEL)), _full((1, 1)), vec, vec],
        out_shape=[jax.ShapeDtypeStruct((s, D_MODEL), F32), jax.ShapeDtypeStruct((s, 512), F32),
                   jax.ShapeDtypeStruct((s, 512), F32), jax.ShapeDtypeStruct((D_MODEL, D_MODEL), F32),
                   jax.ShapeDtypeStruct((1, 1), F32), jax.ShapeDtypeStruct((1, D_MODEL), F32),
                   jax.ShapeDtypeStruct((1, D_MODEL), F32)],
        compiler_params=_params(("arbitrary",)),
    )(og, osw, w_out, x2d, target, gate, g_final)


_PIECES = ((OFF_QK, 512), (OFF_V, 512), (OFF_GZ, 512), (OFF_SQ, 512), (OFF_SZ, 512),
           (OFF_SK, LANES), (OFF_SV, LANES), (OFF_GA, LANES))

_UNPAD_ROWS = ((OFF_QK, 0, 1024),
               (OFF_GA, 1024, GLA_RANK),
               (OFF_GZ, 1040, 1024),
               (OFF_SK, 2064, 256),
               (OFF_SZ, 2320, 512))


def _inproj_bwd(x2d, shift, sc1p, g_norm, w_t, dx2, pieces):
    s = x2d.shape[0]
    tm = min(512, s)
    nsteps = s // tm

    def body(x_ref, sh_ref, sc_ref, g_ref, w_hbm, dx2_ref, *rest):
        piece_refs = rest[:len(_PIECES)]
        gx_ref, dw_hbm, dsh_ref, dsc_ref, dg_ref, w_vm, dw_vm, in_sems, out_sems = rest[len(_PIECES):]
        i = pl.program_id(0)

        @pl.when(i == 0)
        def _():
            loads = _load_w_padded(w_hbm, w_vm, in_sems)
            dw_vm[...] = jnp.zeros_like(dw_vm)
            dsh_ref[...] = jnp.zeros_like(dsh_ref)
            dsc_ref[...] = jnp.zeros_like(dsc_ref)
            dg_ref[...] = jnp.zeros_like(dg_ref)
            for cp in loads:
                cp.wait()

        g, sc1p_v, shift_v = g_ref[...], sc_ref[...], sh_ref[...]
        subs = _subtiles(tm)
        dhs = []
        for sl in subs:
            dh = None
            for (off, width), pr in zip(_PIECES, piece_refs):
                part = _dot(pr[sl, :].astype(BF), w_vm[off:off + width, :])
                dh = part if dh is None else dh + part
            dhs.append(dh)
        norm = [_modnorm(x_ref[sl, :], g, sc1p_v, shift_v) for sl in subs]
        hb = jnp.concatenate([h.astype(BF) for _, _, h in norm], axis=0)
        for (off, width), pr in zip(_PIECES, piece_refs):
            dw_vm[off:off + width, :] += _dot(pr[...].astype(BF), hb, TN)
        for sl, (xn, r, _), dh in zip(subs, norm, dhs):
            dsh_ref[...] += jnp.sum(dh, axis=0, keepdims=True)
            dsc_ref[...] += jnp.sum(dh * (xn * g), axis=0, keepdims=True)
            dg_ref[...] += jnp.sum(dh * xn * sc1p_v, axis=0, keepdims=True)
            dxn = dh * g * sc1p_v
            gx_ref[sl, :] = dx2_ref[sl, :] + r * (dxn - xn * jnp.mean(dxn * xn, axis=-1, keepdims=True))

        @pl.when(i == nsteps - 1)
        def _():
            copies = [pltpu.make_async_copy(dw_vm.at[src:src + n], dw_hbm.at[dst:dst + n], out_sems.at[k])
                      for k, (src, dst, n) in enumerate(_UNPAD_ROWS)]
            for cp in copies:
                cp.start()
            for cp in copies:
                cp.wait()

    rowb = pl.BlockSpec((tm, D_MODEL), lambda i: (i, 0))
    vec = _full((1, D_MODEL))
    anyspec = pl.BlockSpec(memory_space=pl.ANY)
    piece_specs = [pl.BlockSpec((tm, width), lambda i: (i, 0)) for _, width in _PIECES]
    return pl.pallas_call(
        body, name="inproj_bwd", grid=(nsteps,),
        in_specs=[rowb, vec, vec, vec, anyspec, rowb] + piece_specs,
        out_specs=[rowb, anyspec, vec, vec, vec],
        out_shape=[jax.ShapeDtypeStruct((s, D_MODEL), F32), jax.ShapeDtypeStruct((D_IN, D_MODEL), F32),
                   jax.ShapeDtypeStruct((1, D_MODEL), F32), jax.ShapeDtypeStruct((1, D_MODEL), F32),
                   jax.ShapeDtypeStruct((1, D_MODEL), F32)],
        scratch_shapes=[pltpu.VMEM((D_PAD, D_MODEL), BF), pltpu.VMEM((D_PAD, D_MODEL), F32),
                        pltpu.SemaphoreType.DMA((len(_UNPAD_ROWS),)), pltpu.SemaphoreType.DMA((len(_UNPAD_ROWS),))],
        compiler_params=_params(("arbitrary",)),
    )(x2d, shift, sc1p, g_norm, w_t, dx2, *pieces)


def _adam(w, g, m, v):
    m2 = ADAM_B1 * m + (1.0 - ADAM_B1) * g
    v2 = ADAM_B2 * v + (1.0 - ADAM_B2) * (g * g)
    m_hat = m2 / (1.0 - ADAM_B1 ** ADAM_STEP)
    v_hat = v2 / (1.0 - ADAM_B2 ** ADAM_STEP)
    delta = -ADAM_LR * (m_hat / (jnp.sqrt(v_hat) + ADAM_EPS) + ADAM_WD * w)
    return delta, m2, v2


def _adamw(w, g, m, v, name):
    rr, cc = w.shape
    tc = min(512, cc)

    def body(w_ref, g_ref, m_ref, v_ref, d_ref, m2_ref, v2_ref):
        d_ref[...], m2_ref[...], v2_ref[...] = _adam(w_ref[...], g_ref[...], m_ref[...], v_ref[...])

    blk = pl.BlockSpec((rr, tc), lambda i: (0, i))
    return pl.pallas_call(
        body, name=name, grid=(cc // tc,), in_specs=[blk] * 4, out_specs=[blk] * 3,
        out_shape=[jax.ShapeDtypeStruct((rr, cc), F32)] * 3,
        compiler_params=_params(("arbitrary",)),
    )(w, g, m, v)


def _adamw_t(w3, g, m3, v3, name):
    rr, _, cc = w3.shape
    parts = [slice(q * (cc // 4), (q + 1) * (cc // 4)) for q in range(4)]

    def body(w_hbm, g_ref, m_hbm, v_hbm, d_hbm, m2_hbm, v2_hbm, g3_hbm, w_vm, m_vm, v_vm, d_vm, m2_vm, v2_vm, in_sems, out_sems):
        ins = ((w_hbm, w_vm), (m_hbm, m_vm), (v_hbm, v_vm))
        outs = ((d_vm, d_hbm), (m2_vm, m2_hbm), (v2_vm, v2_hbm), (g_ref, g3_hbm))
        loads = [[pltpu.make_async_copy(src.at[:, 0, p], dst.at[:, p], in_sems.at[3 * q + k]) for k, (src, dst) in enumerate(ins)]
                 for q, p in enumerate(parts)]
        stores = [[pltpu.make_async_copy(src.at[:, p], dst.at[:, 0, p], out_sems.at[4 * q + k]) for k, (src, dst) in enumerate(outs)]
                  for q, p in enumerate(parts)]
        for group in loads:
            for cp in group:
                cp.start()
        for q, p in enumerate(parts):
            for cp in loads[q]:
                cp.wait()
            d_vm[:, p], m2_vm[:, p], v2_vm[:, p] = _adam(w_vm[:, p], g_ref[:, p], m_vm[:, p], v_vm[:, p])
            for cp in stores[q]:
                cp.start()
        for group in stores:
            for cp in group:
                cp.wait()

    hbm = pl.BlockSpec(memory_space=pl.ANY)
    return pl.pallas_call(
        body, name=name, grid=(1,), in_specs=[hbm, pl.BlockSpec((rr, cc), lambda i: (0, 0)), hbm, hbm],
        out_specs=[hbm] * 4, out_shape=[jax.ShapeDtypeStruct((rr, 1, cc), F32)] * 4,
        scratch_shapes=[pltpu.VMEM((rr, cc), F32)] * 6 + [pltpu.SemaphoreType.DMA((12,)), pltpu.SemaphoreType.DMA((16,))],
        compiler_params=_params(("arbitrary",)),
    )(w3, g, m3, v3)


def _ada_update(c_all, dmod_cols, w, m, v):
    rr, cc = w.shape
    tr = min(512, rr)
    c_all = jnp.pad(c_all, ((0, 8), (0, 0)))
    dmod_cols = jnp.pad(dmod_cols, ((0, 8), (0, 0)))

    def body(c_ref, dm_ref, w_ref, m_ref, v_ref, g_ref, d_ref, m2_ref, v2_ref):
        cv = c_ref[...]
        sc = (cv * _sigmoid(cv)).astype(BF)
        g = _dot(sc, dm_ref[...].astype(BF), TN)
        g_ref[...] = g
        d_ref[...], m2_ref[...], v2_ref[...] = _adam(w_ref[...], g, m_ref[...], v_ref[...])

    blk = pl.BlockSpec((tr, cc), lambda i: (i, 0))
    return pl.pallas_call(
        body, name="ada_update", grid=(rr // tr,),
        in_specs=[pl.BlockSpec((16, tr), lambda i: (0, i)), _full((16, cc)), blk, blk, blk],
        out_specs=[blk] * 4, out_shape=[jax.ShapeDtypeStruct((rr, cc), F32)] * 4,
        compiler_params=_params(("arbitrary",)),
    )(c_all, dmod_cols, w, m, v)


def _small_update(parts, weights, moms, vels):
    n = len(weights)

    def body(*refs):
        p_refs, w_refs, m_refs, v_refs = refs[:n + 1], refs[n + 1:2 * n + 1], refs[2 * n + 1:3 * n + 1], refs[3 * n + 1:4 * n + 1]
        outs = refs[4 * n + 1:]
        for i in range(n):
            g = p_refs[i][0]
            for d in range(1, 8):
                g = g + p_refs[i][d]
            delta, m2, v2 = _adam(w_refs[i][...], g, m_refs[i][...], v_refs[i][...])
            outs[4 * i][...] = g
            outs[4 * i + 1][...] = delta
            outs[4 * i + 2][...] = m2
            outs[4 * i + 3][...] = v2
        tot = p_refs[n][0]
        for d in range(1, 8):
            tot = tot + p_refs[n][d]
        outs[4 * n][...] = tot

    out_shape = []
    for w in weights:
        out_shape += [jax.ShapeDtypeStruct(w.shape, F32)] * 4
    out_shape.append(jax.ShapeDtypeStruct(parts[n].shape[1:], F32))
    return pl.pallas_call(body, name="small_update", out_shape=out_shape, compiler_params=_params())(
        *parts, *weights, *moms, *vels)


def _rows8(a):
    flat = a.reshape(-1)
    rows = -(-flat.shape[0] // LANES)
    rows8 = -(-rows // 8) * 8
    flat = jnp.pad(flat, (0, rows8 * LANES - flat.shape[0]))
    return flat.reshape(rows8, LANES)


def kernel(x, c, positions, w_ada, b_ada, g_norm, w_in, w_decay, b_decay, g_gla_head, sinks, w_out, g_final, loss_target, m_w_ada, m_b_ada, m_g_norm, m_w_in, m_w_decay, m_b_decay, m_g_gla_head, m_sinks, m_w_out, m_g_final, v_w_ada, v_b_ada, v_g_norm, v_w_in, v_w_decay, v_b_decay, v_g_gla_head, v_sinks, v_w_out, v_g_final):
    ax, ay, ac = lax.axis_index("x"), lax.axis_index("y"), lax.axis_index("c")
    chip = 2 * ax + ay
    dev = 2 * chip + ac
    s = x.shape[1]
    x2d = x[0]
    target = loss_target[0]
    w_ada2, w_out2, w_dec2 = w_ada[0], w_out[0], w_decay[0]
    w_in_t = w_in[0].T
    ada_cols = w_ada2.shape[1]
    in_cols = w_in_t.shape[0]
    out_rows = w_out2.shape[0]
    half = D_MODEL // 2

    cw = jnp.concatenate([c.reshape(8, LANES), w_dec2.reshape(8, LANES)], axis=0)
    b_shard = lax.dynamic_slice(b_ada, (0, chip * ada_cols), (1, ada_cols))
    half_in = lax.dynamic_slice(w_in_t, (0, ac * half), (in_cols, half)).astype(BF)
    half_out = lax.dynamic_slice(w_out2, (ac * (out_rows // 2), 0), (out_rows // 2, D_MODEL)).astype(BF)
    inv_freq = 1.0 / (ROPE_THETA ** (jnp.arange(0, 64, 2, dtype=F32) / 64))
    first, mod_all, w_in_all, cos, sin = _prologue(
        cw, w_ada2, b_shard, half_in, positions.reshape(s, 1), jnp.tile(inv_freq, 4).reshape(1, LANES))

    first = first.reshape(8, 2, 8, LANES)
    c_all = first[:, 0].reshape(8, D_MODEL)
    w_dec_full = first[0::2, 1].reshape(4, GLA_RANK, 64).transpose(1, 0, 2).reshape(GLA_RANK, 256)
    mod = mod_all.reshape(4, 2, 8, ada_cols)[:, 0]
    mod = lax.dynamic_slice(mod, (0, dev, 0), (4, 1, ada_cols)).reshape(1, 4 * ada_cols)
    shift, sc1p, gate = mod[:, :D_MODEL], 1.0 + mod[:, D_MODEL:2 * D_MODEL], mod[:, 2 * D_MODEL:]
    w_t = w_in_all.reshape(4 * in_cols, D_MODEL)

    wdecp = jnp.pad(w_dec_full, ((0, LANES - GLA_RANK), (0, 0))).astype(BF)

    proj = _inproj_fwd(x2d, shift, sc1p, g_norm, w_t)
    og, o_gla, sprev = _gla_fwd(proj, wdecp, b_decay, g_gla_head)
    osw, o_swa, w_out_all = _swa_fwd(proj, cos, sin, sinks, half_out)
    w_out_all = w_out_all.reshape(D_MODEL, D_MODEL)
    dx2, dog, dos, dw_out, loss_p, dgf, dgate = _outproj(og, osw, w_out_all, x2d, target, gate, g_final.reshape(1, D_MODEL))
    dsq, dsz, dsk, dsv, dsinks, g_w_out = _swa_bwd(proj, dos, o_swa, cos, sin, sinks, dw_out.reshape(4, out_rows, D_MODEL))
    dqk, dv, dgz, dga, dwdp, dbd, dgg = _gla_bwd(proj, dog, o_gla, sprev, wdecp, b_decay, g_gla_head)
    pieces = (dqk, dv, dgz, dsq, dsz, dsk, dsv, dga)
    gx, dw_in_t, dshift, dscale, dgn = _inproj_bwd(x2d, shift, sc1p, g_norm, w_t, dx2, pieces)

    segs = [jnp.concatenate([dshift, dscale, dgate], axis=1), dgn, dgf, dwdp[:GLA_RANK], dbd, dgg, dsinks, loss_p]
    packed = [_rows8(a) for a in segs]
    offs = [0]
    for a in packed:
        offs.append(offs[-1] + a.shape[0])
    g_window, small = _epilogue(dw_in_t, jnp.concatenate(packed, axis=0))
    g_w_in_t = lax.dynamic_slice(g_window, ((in_cols * chip) % 8, 0), (in_cols, D_MODEL))

    def seg(i, size):
        return small[:, offs[i]:offs[i + 1]].reshape(8, -1)[:, :size]

    dmod_all = seg(0, 3 * D_MODEL)
    dwd_all = lax.dynamic_slice(seg(3, GLA_RANK * 256).reshape(8, GLA_RANK, 256), (0, 0, chip * 64), (8, GLA_RANK, 64))
    parts = [dmod_all.reshape(8, 1, 3 * D_MODEL), seg(1, D_MODEL).reshape(8, 1, D_MODEL), dwd_all,
             seg(4, 256).reshape(8, 1, 256), seg(5, 512).reshape(8, 1, 512), seg(6, SWA_HEADS).reshape(8, 1, SWA_HEADS),
             seg(2, D_MODEL).reshape(8, 1, D_MODEL), seg(7, LANES).reshape(8, 1, LANES)]
    smalls = _small_update(
        parts,
        [b_ada, g_norm, w_dec2, b_decay, g_gla_head, sinks, g_final.reshape(1, D_MODEL)],
        [m_b_ada, m_g_norm, m_w_decay[0], m_b_decay, m_g_gla_head, m_sinks, m_g_final.reshape(1, D_MODEL)],
        [v_b_ada, v_g_norm, v_w_decay[0], v_b_decay, v_g_gla_head, v_sinks, v_g_final.reshape(1, D_MODEL)])
    (g_b_ada, d_b_ada, nm_b_ada, nv_b_ada, g_gn, d_gn, nm_gn, nv_gn, g_wd, d_wd, nm_wd, nv_wd,
     g_bd, d_bd, nm_bd, nv_bd, g_gg, d_gg, nm_gg, nv_gg, g_sk, d_sk, nm_sk, nv_sk,
     g_gf, d_gf, nm_gf, nv_gf, loss_row) = smalls
    loss = loss_row[0, 0]

    dmod_cols = lax.dynamic_slice(dmod_all, (0, chip * ada_cols), (8, ada_cols))
    g_w_ada, d_w_ada, nm_w_ada, nv_w_ada = _ada_update(c_all, dmod_cols, w_ada2, m_w_ada[0], v_w_ada[0])
    to3 = lambda a: jnp.transpose(a, (2, 0, 1))
    from3 = lambda a: jnp.transpose(a, (1, 2, 0))[0]
    d3, nm3, nv3, g3 = _adamw_t(to3(w_in), g_w_in_t, to3(m_w_in), to3(v_w_in), "adamw_w_in")
    g_w_in, d_w_in, nm_w_in, nv_w_in = from3(g3), from3(d3), from3(nm3), from3(nv3)
    d_w_out, nm_w_out, nv_w_out = _adamw(w_out2, g_w_out, m_w_out[0], v_w_out[0], "adamw_w_out")

    flat = lambda a: a.reshape(D_MODEL)
    grads = [g_w_ada[None], g_b_ada, g_gn, g_w_in[None], g_wd[None], g_bd, g_gg, g_sk, g_w_out[None], flat(g_gf)]
    deltas = [d_w_ada[None], d_b_ada, d_gn, d_w_in[None], d_wd[None], d_bd, d_gg, d_sk, d_w_out[None], flat(d_gf)]
    new_m = [nm_w_ada[None], nm_b_ada, nm_gn, nm_w_in[None], nm_wd[None], nm_bd, nm_gg, nm_sk, nm_w_out[None], flat(nm_gf)]
    new_v = [nv_w_ada[None], nv_b_ada, nv_gn, nv_w_in[None], nv_wd[None], nv_bd, nv_gg, nv_sk, nv_w_out[None], flat(nv_gf)]
    return (loss, gx[None], *grads, *deltas, *new_m, *new_v)
```

```python
import jax
import jax.numpy as jnp
from jax import lax
from jax.experimental import pallas as pl
from jax.experimental.pallas import tpu as pltpu

F32 = jnp.float32
BF = jnp.bfloat16

D_MODEL = 1024
GLA_HEADS = 4
GLA_DK = 64
GLA_CHUNK = 64
GLA_RANK = 16
GLA_TAU = 16.0
GLA_SUB = 256
GLA_ROWS_FWD = 1024
GLA_ROWS_BWD = 512
SWA_HEADS = 8
SWA_BLOCK = 128
SWA_QBLOCKS_FWD = 8
SWA_QBLOCKS = 8
RMS_EPS = 1e-6
ROPE_THETA = 10000.0

OFF_QK, OFF_V, OFF_GZ, OFF_SQ, OFF_SZ, OFF_SK, OFF_SV, OFF_GA = 0, 512, 1024, 1536, 2048, 2560, 2688, 2816
D_PAD = 2944
D_IN = 2832
LANES = 128
VMEM_LIMIT = 56 * 1024 * 1024

ADAM_LR, ADAM_B1, ADAM_B2, ADAM_EPS, ADAM_WD, ADAM_STEP = 0.001, 0.9, 0.999, 1e-08, 0.01, 10

NT = (((1,), (1,)), ((), ()))
TN = (((0,), (0,)), ((), ()))
MESH = pl.DeviceIdType.MESH


def _dot(a, b, dims=None):
    if dims is None:
        return jnp.dot(a, b, preferred_element_type=F32)
    return lax.dot_general(a, b, dims, preferred_element_type=F32)


def _sigmoid(x):
    return 1.0 / (1.0 + jnp.exp(-x))


def _params(sem=None):
    return pltpu.CompilerParams(dimension_semantics=sem, vmem_limit_bytes=VMEM_LIMIT)


def _full(shape):
    return pl.BlockSpec(shape, lambda i: (0,) * len(shape))


def _subtiles(rows, size=256):
    size = min(size, rows)
    return [slice(k * size, (k + 1) * size) for k in range(rows // size)]


WEIGHT_CHUNKS = 4


def _gather_sems(chunks=1):
    return [pltpu.SemaphoreType.DMA((7 * chunks,)), pltpu.SemaphoreType.DMA((7 * chunks,)), pltpu.SemaphoreType.DMA]


_GATHER_SEMS = _gather_sems()


class _Gather:
    def __init__(self, x_ref, out_ref, send_sems, recv_sems, local_sem, slab=None, chunks=1):
        self.slab_of = slab
        self.chunks = chunks
        self.width = x_ref.shape[-1] // chunks
        x, y, c = lax.axis_index("x"), lax.axis_index("y"), lax.axis_index("c")
        self.me, self.sibling, self.c = (x, y, c), (x, y, 1 - c), c
        self.xn, self.yn, self.dg = (1 - x, y), (x, 1 - y), (1 - x, 1 - y)
        self.pass_from = (lax.rem(x + 1 - c, 2), lax.rem(y + c, 2))
        self.pass_to = (lax.rem(x + c, 2), lax.rem(y + 1 - c, 2))
        self.x_ref, self.out_ref, self.send_sems, self.recv_sems = x_ref, out_ref, send_sems, recv_sems
        self.mine = pltpu.make_async_copy(x_ref, self._slab(*self.me), local_sem)

    def _slab(self, px, py, pc):
        if self.slab_of is not None:
            return self.slab_of(self.out_ref, px, py, pc)
        return self.out_ref.at[4 * px + 2 * py + pc]

    def _part(self, ref, q):
        if self.chunks == 1:
            return ref
        lanes = slice(q * self.width, (q + 1) * self.width)
        return ref.at[(slice(None),) * (len(ref.shape) - 1) + (lanes,)]

    def _copy(self, k, q, blk, to, src=None):
        i = k * self.chunks + q
        return pltpu.make_async_remote_copy(
            src_ref=self._part(self._slab(*blk) if src is None else src, q), dst_ref=self._part(self._slab(*blk), q),
            send_sem=self.send_sems.at[i], recv_sem=self.recv_sems.at[i], device_id=to, device_id_type=MESH)

    def _sends(self, q):
        c = self.c
        return [self._copy(0, q, self.me, self.sibling, src=self.x_ref),
                self._copy(1, q, self.me, (*self.xn, c), src=self.x_ref),
                self._copy(2, q, self.me, (*self.yn, c), src=self.x_ref),
                self._copy(3, q, (*self.pass_from, c), (*self.pass_to, c)),
                self._copy(4, q, (*self.xn, c), self.sibling),
                self._copy(5, q, (*self.yn, c), self.sibling),
                self._copy(6, q, (*self.dg, c), self.sibling)]

    def start(self):
        self.mine.start()
        for q in range(self.chunks):
            sends = self._sends(q)
            for k in (1, 2, 0):
                sends[k].start()

    def pass_on(self, only=None):
        for q in range(self.chunks) if only is None else (only,):
            sends = self._sends(q)
            self._copy(1, q, (*self.xn, self.c), self.me).wait_recv()
            self._copy(2, q, (*self.yn, self.c), self.me).wait_recv()
            for k in (3, 4, 5):
                sends[k].start()

    def relay_diagonal(self, only=None):
        for q in range(self.chunks) if only is None else (only,):
            self._copy(3, q, (*self.dg, self.c), self.me).wait_recv()
            self._sends(q)[6].start()

    def relay(self):
        self.pass_on()
        self.relay_diagonal()

    def finish(self):
        c = self.c
        for q in range(self.chunks):
            self._copy(0, q, self.sibling, self.me).wait_recv()
            for k, chip in ((4, self.xn), (5, self.yn), (6, self.dg)):
                self._copy(k, q, (*chip, 1 - c), self.me).wait_recv()
            for cp in self._sends(q):
                cp.wait_send()
        self.mine.wait()


def _prologue(cw, w_ada, b_shard, half_in, pos_col, inv_freq):
    s = pos_col.shape[0]
    rt = min(512, s)

    def body(cw_ref, wada_hbm, b_ref, hin_ref, pos_hbm, f_ref,
             first_ref, mod_ref, win_ref, cos_hbm, sin_hbm,
             mod_blk, cos_ref, sin_ref, wada_ref, pos_ref, table_sems, local_sems, *sems):
        fetch_w = pltpu.make_async_copy(wada_hbm, wada_ref, local_sems.at[0])
        fetch_p = pltpu.make_async_copy(pos_hbm, pos_ref, local_sems.at[1])
        fetch_w.start()
        fetch_p.start()
        g_c = _Gather(cw_ref, first_ref, *sems[0:3])
        half_lanes = hin_ref.shape[1]
        g_in = _Gather(hin_ref, win_ref, *sems[3:6], chunks=WEIGHT_CHUNKS,
                       slab=lambda ref, px, py, pc: ref.at[2 * px + py, :, pl.ds(pl.multiple_of(pc * half_lanes, half_lanes), half_lanes)])
        g_mod = _Gather(mod_blk, mod_ref, *sems[6:9])
        g_c.start()
        g_in.start()
        g_c.relay()
        g_c.finish()
        c_rows = [jnp.concatenate([first_ref[d, r:r + 1, :] for r in range(8)], axis=1) for d in range(8)]
        c_all = jnp.concatenate(c_rows, axis=0)
        sc = (c_all * _sigmoid(c_all)).astype(BF)
        fetch_w.wait()
        mod_blk[...] = _dot(sc, wada_ref[...].astype(BF)) + b_ref[...]
        g_mod.start()
        fetch_p.wait()

        def rope_rows(i, carry):
            rows = pl.ds(pl.multiple_of(i * rt, rt), rt)
            ang = pos_ref[rows, :].astype(F32) * f_ref[...]
            lane = lax.broadcasted_iota(jnp.int32, ang.shape, 1)
            cos_ref[rows, :] = jnp.cos(ang)
            sn = jnp.sin(ang)
            sin_ref[rows, :] = jnp.where((lane % 64) < 32, -sn, sn)
            pltpu.make_async_copy(cos_ref.at[rows, :], cos_hbm.at[rows, :], table_sems.at[0]).start()
            pltpu.make_async_copy(sin_ref.at[rows, :], sin_hbm.at[rows, :], table_sems.at[1]).start()
            return carry

        waits = ([lambda q=q: g_in.pass_on(q) for q in range(WEIGHT_CHUNKS)]
                 + [lambda q=q: g_in.relay_diagonal(q) for q in range(WEIGHT_CHUNKS)] + [g_mod.relay])
        steps = s // rt
        lead = steps // 4
        per_wait = max((steps - lead) // len(waits), 1)
        lax.fori_loop(0, lead, rope_rows, 0)
        done = lead
        for wait in waits:
            wait()
            nxt = min(done + per_wait, steps)
            lax.fori_loop(done, nxt, rope_rows, 0)
            done = nxt
        lax.fori_loop(done, steps, rope_rows, 0)
        g_in.finish()
        g_mod.finish()
        pltpu.make_async_copy(cos_ref, cos_hbm, table_sems.at[0]).wait()
        pltpu.make_async_copy(sin_ref, sin_hbm, table_sems.at[1]).wait()

    vm = pl.BlockSpec(memory_space=pltpu.VMEM)
    hbm = pl.BlockSpec(memory_space=pl.ANY)
    return pl.pallas_call(
        body, name="prologue",
        out_shape=[jax.ShapeDtypeStruct((8,) + cw.shape, F32), jax.ShapeDtypeStruct((8, 8, w_ada.shape[1]), F32),
                   jax.ShapeDtypeStruct((4, half_in.shape[0], 2 * half_in.shape[1]), half_in.dtype),
                   jax.ShapeDtypeStruct((s, LANES), F32), jax.ShapeDtypeStruct((s, LANES), F32)],
        in_specs=[vm, hbm, vm, hbm, hbm, vm], out_specs=[vm, vm, hbm, hbm, hbm],
        scratch_shapes=[pltpu.VMEM((8, w_ada.shape[1]), F32), pltpu.VMEM((s, LANES), F32), pltpu.VMEM((s, LANES), F32),
                        pltpu.VMEM(w_ada.shape, F32), pltpu.VMEM(pos_col.shape, jnp.int32),
                        pltpu.SemaphoreType.DMA((2,)), pltpu.SemaphoreType.DMA((2,))]
        + _GATHER_SEMS + _gather_sems(WEIGHT_CHUNKS) + _GATHER_SEMS,
        compiler_params=pltpu.CompilerParams(vmem_limit_bytes=VMEM_LIMIT),
    )(cw, w_ada, b_shard, half_in, pos_col, inv_freq)


def _reduce_scratch(rr, cc):
    c2 = cc // 2
    return [pltpu.VMEM((4, rr, c2), F32), pltpu.VMEM((4, rr, c2), F32), pltpu.VMEM((3, rr, c2), BF),
            pltpu.VMEM((2, rr, c2), BF), pltpu.VMEM((rr, c2), BF), pltpu.VMEM((rr, c2), F32),
            pltpu.SemaphoreType.DMA((8 + 3 * WEIGHT_CHUNKS,)), pltpu.SemaphoreType.DMA((8 + 3 * WEIGHT_CHUNKS,)),
            pltpu.SemaphoreType.DMA((5,))]


class _Reduce:
    def __init__(self, p_hbm, out_ref, acc_ref, own_ref, send_ref, land_ref, relay_ref, res_ref,
                 send_sems, recv_sems, local_sems, rows=None):
        x, y, c = lax.axis_index("x"), lax.axis_index("y"), lax.axis_index("c")
        part = (lambda j, ln: p_hbm.at[j, :, ln]) if rows is None else (lambda j, ln: p_hbm.at[rows(j), ln])
        c2 = out_ref.shape[1] // 2
        sibling = (x, y, 1 - c)
        first = (lax.rem(x + 1 - c, 2), lax.rem(y + c, 2))
        second = (lax.rem(x + c, 2), lax.rem(y + 1 - c, 2))
        shards = [2 * first[0] + first[1], 2 * second[0] + second[1], 2 * (1 - x) + (1 - y), 2 * x + y]
        sibling_slot = (1, 0, 2, 3)
        mine = pl.ds(pl.multiple_of(c * c2, c2), c2)
        other = pl.ds(pl.multiple_of((1 - c) * c2, c2), c2)
        self.acc_ref, self.own_ref, self.send_ref, self.land_ref = acc_ref, own_ref, send_ref, land_ref
        self.relay_ref, self.res_ref = relay_ref, res_ref
        self.own = [pltpu.make_async_copy(part(j, mine), own_ref.at[k], local_sems.at[k])
                    for k, j in enumerate(shards)]
        self.swap_out = [pltpu.make_async_remote_copy(
            src_ref=part(j, other), dst_ref=acc_ref.at[sibling_slot[k]], send_sem=send_sems.at[k],
            recv_sem=recv_sems.at[sibling_slot[k]], device_id=sibling, device_id_type=MESH) for k, j in enumerate(shards)]
        self.swap_in = [pltpu.make_async_remote_copy(
            src_ref=part(j, other), dst_ref=acc_ref.at[k], send_sem=send_sems.at[k], recv_sem=recv_sems.at[k],
            device_id=sibling, device_id_type=MESH) for k, j in enumerate(shards)]

        self.lanes = [slice(q * (c2 // WEIGHT_CHUNKS), (q + 1) * (c2 // WEIGHT_CHUNKS)) for q in range(WEIGHT_CHUNKS)]

        def message(m, src, dst, to):
            return [pltpu.make_async_remote_copy(
                src_ref=src.at[:, ln], dst_ref=dst.at[:, ln], send_sem=send_sems.at[8 + m * WEIGHT_CHUNKS + q],
                recv_sem=recv_sems.at[8 + m * WEIGHT_CHUNKS + q], device_id=(*to, c), device_id_type=MESH)
                for q, ln in enumerate(self.lanes)]

        self.direct = message(0, send_ref.at[0], land_ref.at[0], first)
        self.passed = message(1, send_ref.at[1], relay_ref, first)
        self.joint = message(2, send_ref.at[2], land_ref.at[1], second)
        self.put = pltpu.make_async_copy(res_ref, out_ref.at[:, mine], local_sems.at[4])
        self.share = pltpu.make_async_remote_copy(
            src_ref=res_ref, dst_ref=out_ref.at[:, mine], send_sem=send_sems.at[7],
            recv_sem=recv_sems.at[7], device_id=sibling, device_id_type=MESH)

    def start(self):
        for k in (2, 0, 1, 3):
            self.own[k].start()
            self.swap_out[k].start()

    def _combine(self, k):
        self.own[k].wait()
        self.swap_out[k].wait_send()
        self.swap_in[k].wait_recv()
        self.acc_ref[k] = self.acc_ref[k] + self.own_ref[k]

    def combine_and_send(self):
        dt = self.send_ref.dtype
        self._combine(2)
        self.send_ref[1] = self.acc_ref[2].astype(dt)
        for cp in self.passed:
            cp.start()
        self._combine(0)
        self.send_ref[0] = self.acc_ref[0].astype(dt)
        for cp in self.direct:
            cp.start()
        self._combine(1)
        self._combine(3)

    def send_joint(self):
        dt = self.send_ref.dtype
        for q, ln in enumerate(self.lanes):
            self.passed[q].wait_recv()
            self.send_ref[2, :, ln] = (self.acc_ref[1, :, ln] + self.relay_ref[:, ln].astype(F32)).astype(dt)
            self.joint[q].start()

    def total_and_share(self):
        for cp in self.direct + self.joint:
            cp.wait_recv()
        self.res_ref[...] = self.acc_ref[3] + self.land_ref[0].astype(F32) + self.land_ref[1].astype(F32)
        for cp in self.direct + self.passed + self.joint:
            cp.wait_send()
        self.put.start()
        self.share.start()

    def finish(self):
        self.put.wait()
        self.share.wait()


def _shard_window(n):
    return max(-(-(n * (j + 1)) // 8) * 8 - (n * j) // 8 * 8 for j in range(4))


def _epilogue(dw_in_t, small):
    cc = dw_in_t.shape[1]
    n = dw_in_t.shape[0] // 4
    r_in = _shard_window(n)
    n_red = len(_reduce_scratch(r_in, cc))

    def body(pin_hbm, small_ref, gin_ref, small_all_ref, *scratch):
        red_in = _Reduce(pin_hbm, gin_ref, *scratch[0:n_red],
                         rows=lambda j: pl.ds(pl.multiple_of((n * j) // 8 * 8, 8), r_in))
        gat = _Gather(small_ref, small_all_ref, *scratch[n_red:])
        red_in.start()
        gat.start()
        gat.relay()
        red_in.combine_and_send()
        gat.finish()
        red_in.send_joint()
        red_in.total_and_share()
        red_in.finish()

    vm = pl.BlockSpec(memory_space=pltpu.VMEM)
    anyspec = pl.BlockSpec(memory_space=pl.ANY)
    return pl.pallas_call(
        body, name="epilogue",
        out_shape=[jax.ShapeDtypeStruct((r_in, cc), F32), jax.ShapeDtypeStruct((8,) + small.shape, F32)],
        in_specs=[anyspec, vm], out_specs=[anyspec, vm],
        scratch_shapes=_reduce_scratch(r_in, cc) + _GATHER_SEMS,
        compiler_params=pltpu.CompilerParams(vmem_limit_bytes=VMEM_LIMIT),
    )(dw_in_t, small)


def _rope(t, cosb, sinb, first_half):
    partner = jnp.where(first_half, pltpu.roll(t, 96, 1), pltpu.roll(t, 32, 1))
    return t * cosb + partner * sinb


def _rope_t(g, cosb, sinb, first_half):
    gs = g * sinb
    partner = jnp.where(first_half, pltpu.roll(gs, 96, 1), pltpu.roll(gs, 32, 1))
    return g * cosb + partner


def _modnorm(x, g, sc1p, shift):
    r = lax.rsqrt(jnp.mean(x * x, axis=-1, keepdims=True) + RMS_EPS)
    xn = x * r
    return xn, r, (xn * g) * sc1p + shift


def _load_w_padded(w_hbm, w_vm, sems):
    copies = [pltpu.make_async_copy(w_hbm.at[ref:ref + n], w_vm.at[pad:pad + n], sems.at[k])
              for k, (pad, ref, n) in enumerate(_UNPAD_ROWS)]
    for cp in copies:
        cp.start()
    w_vm[OFF_GA + GLA_RANK:, :] = jnp.zeros((D_PAD - OFF_GA - GLA_RANK, D_MODEL), w_vm.dtype)
    return copies


def _inproj_fwd(x2d, shift, sc1p, g_norm, w_t):
    s = x2d.shape[0]
    tm = min(1024, s)

    def body(x_ref, sh_ref, sc_ref, g_ref, w_hbm, o_ref, w_vm, sems):
        @pl.when(pl.program_id(0) == 0)
        def _():
            for cp in _load_w_padded(w_hbm, w_vm, sems):
                cp.wait()

        subs = _subtiles(tm)
        hs = [_modnorm(x_ref[sl, :], g_ref[...], sc_ref[...], sh_ref[...])[2].astype(BF) for sl in subs]
        for sl, h in zip(subs, hs):
            o_ref[sl, :] = _dot(h, w_vm[...], NT)

    vec = _full((1, D_MODEL))
    return pl.pallas_call(
        body, name="inproj_fwd", grid=(s // tm,),
        in_specs=[pl.BlockSpec((tm, D_MODEL), lambda i: (i, 0)), vec, vec, vec, pl.BlockSpec(memory_space=pl.ANY)],
        out_specs=pl.BlockSpec((tm, D_PAD), lambda i: (i, 0)),
        out_shape=jax.ShapeDtypeStruct((s, D_PAD), F32),
        scratch_shapes=[pltpu.VMEM((D_PAD, D_MODEL), BF), pltpu.SemaphoreType.DMA((len(_UNPAD_ROWS),))],
        compiler_params=_params(("arbitrary",)),
    )(x2d, shift, sc1p, g_norm, w_t)


def _split3(a):
    hi = a.astype(BF)
    r1 = a - hi.astype(F32)
    mid = r1.astype(BF)
    lo = (r1 - mid.astype(F32)).astype(BF)
    return hi, mid, lo


def _tri_matmul(tri, a):
    hi, mid, lo = _split3(a)
    return _dot(tri, hi) + _dot(tri, mid) + _dot(tri, lo)


def _chunks(tb):
    return [slice(c * GLA_CHUNK, (c + 1) * GLA_CHUNK) for c in range(tb // GLA_CHUNK)]


def _per_chunk_rows(rows, width):
    return jnp.concatenate([jnp.broadcast_to(r, (GLA_CHUNK, width)) for r in rows], axis=0)


def _gla_triangle(tb):
    row = lax.broadcasted_iota(jnp.int32, (tb, tb), 0)
    col = lax.broadcasted_iota(jnp.int32, (tb, tb), 1)
    return (((row // GLA_CHUNK) == (col // GLA_CHUNK)) & (col <= row)).astype(F32)


def _lane_mean(x, ones_b):
    hi = x.astype(BF)
    lo = (x - hi.astype(F32)).astype(BF)
    return (_dot(hi, ones_b) + _dot(lo, ones_b)) * (1.0 / LANES)


def _head(t, h, lo_h):
    blk = t[:, LANES * (h // 2):LANES * (h // 2 + 1)]
    return jnp.where(lo_h, blk, 0.0) if h % 2 == 0 else jnp.where(lo_h, 0.0, blk)


def _gla_block_common(qk, ga, wd, bd, tril_b):
    tb = qk.shape[0]
    q, k = qk[:, :256], qk[:, 256:]
    z = _dot(ga.astype(BF), wd) + bd
    la = (jnp.minimum(z, 0.0) - jnp.log(1.0 + jnp.exp(-jnp.abs(z)))) * (1.0 / GLA_TAU)
    b = _tri_matmul(tril_b, la)
    bls = [b[rs.stop - 1:rs.stop, :] for rs in _chunks(tb)]
    eq = jnp.exp(b)
    ek = jnp.exp(-b)
    f = jnp.exp(_per_chunk_rows(bls, 256) - b)
    return z, eq, ek, f, q * (eq * GLA_DK ** -0.5), k * ek, k * f, bls


def _gla_units(s, rows):
    sub = min(GLA_SUB, s)
    tb = min(rows, s)
    subs = [slice(i * sub, (i + 1) * sub) for i in range(tb // sub)]
    units = [(i, h) for i in range(len(subs)) for h in range(GLA_HEADS)]
    return tb, sub, subs, units


def _gla_fwd(proj, wdecp, bdec, ggla):
    s = proj.shape[0]
    tb, sub, subs, units = _gla_units(s, GLA_ROWS_FWD)
    nch = sub // GLA_CHUNK

    def body(qk_ref, v_ref, gz_ref, ga_ref, wd_ref, bd_ref, gg_ref, tri_ref, og_ref, opre_ref, sprev_ref, st_ref):
        @pl.when(pl.program_id(0) == 0)
        def _():
            st_ref[...] = jnp.zeros_like(st_ref)

        lo_h = lax.broadcasted_iota(jnp.int32, (sub, LANES), 1) < GLA_DK
        tril = tri_ref[...] > 0.5
        tril_b = tri_ref[...].astype(BF)
        ones_b = jnp.ones((LANES, LANES), BF)
        gg, wd, bd = gg_ref[...], wd_ref[...], bd_ref[...]
        chunks = _chunks(sub)
        lanes = [slice(h * LANES, (h + 1) * LANES) for h in range(GLA_HEADS)]
        com = [_gla_block_common(qk_ref[sl, :], ga_ref[sl, :], wd, bd, tril_b) for sl in subs]
        decs = [[jnp.exp(bl) for bl in cm[7]] for cm in com]
        a = {(i, h): _head(com[i][4], h, lo_h).astype(BF) for i, h in units}
        bm = {(i, h): _head(com[i][5], h, lo_h).astype(BF) for i, h in units}
        ktl = {(i, h): _head(com[i][6], h, lo_h).astype(BF) for i, h in units}
        vh = {(i, h): v_ref[subs[i], lanes[h]].astype(BF) for i, h in units}
        sc = {u: _dot(a[u], bm[u], NT) for u in units}
        upd = {u: [_dot(vh[u][rs], ktl[u][rs], TN) for rs in chunks] for u in units}
        p = {u: jnp.where(tril, sc[u], 0.0).astype(BF) for u in units}
        o = {u: _dot(p[u], vh[u]) for u in units}
        states = {}
        for h in range(GLA_HEADS):
            st = st_ref[h]
            for i in range(len(subs)):
                entering = []
                for c in range(nch):
                    entering.append(st)
                    sprev_ref[i * nch + c, h] = st
                    st = st * decs[i][c][:, LANES * (h // 2):LANES * (h // 2 + 1)] + upd[(i, h)][c]
                states[(i, h)] = entering
            st_ref[h] = st
        inter = {u: [_dot(a[u][rs], states[u][c].astype(BF), NT) for c, rs in enumerate(chunks)] for u in units}
        o = {u: o[u] + jnp.concatenate(inter[u], axis=0) for u in units}
        ms = {u: _lane_mean(o[u] * o[u], ones_b) for u in units}
        for i, h in units:
            gzh = gz_ref[subs[i], lanes[h]]
            opre_ref[subs[i], lanes[h]] = o[(i, h)]
            og_ref[subs[i], lanes[h]] = (((o[(i, h)] * lax.rsqrt(ms[(i, h)] + RMS_EPS)) * gg[:, lanes[h]])
                                         * (gzh * _sigmoid(gzh))).astype(og_ref.dtype)

    def col(width, off):
        return pl.BlockSpec((tb, width), lambda i: (i, off // width))

    return pl.pallas_call(
        body, name="gla_fwd", grid=(s // tb,),
        in_specs=[col(512, OFF_QK), col(512, OFF_V), col(512, OFF_GZ), col(LANES, OFF_GA),
                  _full((LANES, 256)), _full((1, 256)), _full((1, 512)), _full((sub, sub))],
        out_specs=[pl.BlockSpec((tb, 512), lambda i: (i, 0)), pl.BlockSpec((tb, 512), lambda i: (i, 0)),
                   pl.BlockSpec((tb // GLA_CHUNK, GLA_HEADS, LANES, LANES), lambda i: (i, 0, 0, 0))],
        out_shape=[jax.ShapeDtypeStruct((s, 512), BF), jax.ShapeDtypeStruct((s, 512), F32),
                   jax.ShapeDtypeStruct((s // GLA_CHUNK, GLA_HEADS, LANES, LANES), F32)],
        scratch_shapes=[pltpu.VMEM((GLA_HEADS, LANES, LANES), F32)],
        compiler_params=_params(("arbitrary",)),
    )(proj, proj, proj, proj, wdecp, bdec, ggla, _gla_triangle(sub))


def _gla_bwd(proj, dog, opre, sprev, wdecp, bdec, ggla):
    s = proj.shape[0]
    tb, sub, subs, units = _gla_units(s, GLA_ROWS_BWD)
    nsub = len(subs)
    nch = sub // GLA_CHUNK
    nb = s // tb

    def body(qk_ref, v_ref, gz_ref, ga_ref, dog_ref, opre_ref, sprev_ref, wd_ref, bd_ref, gg_ref, tri_ref, triu_ref,
             dqk_ref, dv_ref, dgz_ref, dga_ref, dwd_ref, dbd_ref, dgg_ref, dst_ref):
        @pl.when(pl.program_id(0) == 0)
        def _():
            dst_ref[...] = jnp.zeros_like(dst_ref)
            dwd_ref[...] = jnp.zeros_like(dwd_ref)
            dbd_ref[...] = jnp.zeros_like(dbd_ref)
            dgg_ref[...] = jnp.zeros_like(dgg_ref)

        lo_h = lax.broadcasted_iota(jnp.int32, (sub, LANES), 1) < GLA_DK
        tril = tri_ref[...] > 0.5
        tril_b = tri_ref[...].astype(BF)
        triu_b = triu_ref[...].astype(BF)
        ones_b = jnp.ones((LANES, LANES), BF)
        last_row = (lax.broadcasted_iota(jnp.int32, (sub, LANES), 0) % GLA_CHUNK) == GLA_CHUNK - 1
        wd, gg, bd = wd_ref[...], gg_ref[...], bd_ref[...]
        chunks = _chunks(sub)
        lanes = [slice(h * LANES, (h + 1) * LANES) for h in range(GLA_HEADS)]
        blks = [slice(LANES * (h // 2), LANES * (h // 2 + 1)) for h in range(GLA_HEADS)]
        ga = [ga_ref[sl, :] for sl in subs]
        com = [_gla_block_common(qk_ref[sl, :], ga[i], wd, bd, tril_b) for i, sl in enumerate(subs)]
        decs = [[jnp.exp(bl) for bl in cm[7]] for cm in com]
        a = {(i, h): _head(com[i][4], h, lo_h).astype(BF) for i, h in units}
        bm = {(i, h): _head(com[i][5], h, lo_h).astype(BF) for i, h in units}
        ktl = {(i, h): _head(com[i][6], h, lo_h).astype(BF) for i, h in units}
        vh = {(i, h): v_ref[subs[i], lanes[h]].astype(BF) for i, h in units}
        sc = {u: _dot(a[u], bm[u], NT) for u in units}

        o = {(i, h): opre_ref[subs[i], lanes[h]] for i, h in units}
        ms = {u: _lane_mean(o[u] * o[u], ones_b) for u in units}
        gz = {(i, h): gz_ref[subs[i], lanes[h]] for i, h in units}
        dog = {(i, h): dog_ref[subs[i], lanes[h]] for i, h in units}
        sg = {u: _sigmoid(gz[u]) for u in units}
        r = {u: lax.rsqrt(ms[u] + RMS_EPS) for u in units}
        ohat = {u: o[u] * r[u] for u in units}
        sil = {u: gz[u] * sg[u] for u in units}
        for i, h in units:
            u = (i, h)
            dgz_ref[subs[i], lanes[h]] = (dog[u] * (ohat[u] * gg[:, lanes[h]])
                                          * (sg[u] * (1.0 + gz[u] * (1.0 - sg[u])))).astype(dgz_ref.dtype)
            dgg_ref[:, lanes[h]] += jnp.sum(dog[u] * sil[u] * ohat[u], axis=0, keepdims=True)
        dn = {(i, h): dog[(i, h)] * sil[(i, h)] * gg[:, lanes[h]] for i, h in units}
        mdn = {u: _lane_mean(dn[u] * ohat[u], ones_b) for u in units}
        do = {u: (r[u] * (dn[u] - ohat[u] * mdn[u])).astype(BF) for u in units}

        p = {u: jnp.where(tril, sc[u], 0.0).astype(BF) for u in units}
        dpr = {u: _dot(do[u], vh[u], NT) for u in units}
        incr = {u: [_dot(do[u][rs], a[u][rs], TN) for rs in chunks] for u in units}
        dv = {u: _dot(p[u], do[u], TN) for u in units}
        dp = {u: jnp.where(tril, dpr[u], 0.0).astype(BF) for u in units}
        dqd = {u: _dot(dp[u], bm[u]) for u in units}
        dkd = {u: _dot(dp[u], a[u], TN) for u in units}
        st = {(i, h): [sprev_ref[i * nch + c, h] for c in range(nch)] for i, h in units}
        leaving = {}
        for h in range(GLA_HEADS):
            d = dst_ref[h]
            for i in reversed(range(nsub)):
                out = [None] * nch
                for c in reversed(range(nch)):
                    out[c] = d
                    d = d * decs[i][c][:, blks[h]] + incr[(i, h)][c]
                leaving[(i, h)] = out
            dst_ref[h] = d
        lv_b = {u: [leaving[u][c].astype(BF) for c in range(nch)] for u in units}
        dv_s = {u: [_dot(ktl[u][rs], lv_b[u][c], NT) for c, rs in enumerate(chunks)] for u in units}
        dqd_s = {u: [_dot(do[u][rs], st[u][c].astype(BF)) for c, rs in enumerate(chunks)] for u in units}
        dkt_s = {u: [_dot(vh[u][rs], lv_b[u][c]) for c, rs in enumerate(chunks)] for u in units}
        ddec = {u: [jnp.sum(leaving[u][c] * st[u][c], axis=0, keepdims=True) for c in range(nch)] for u in units}
        for i, h in units:
            dv_ref[subs[i], lanes[h]] = (dv[(i, h)] + jnp.concatenate(dv_s[(i, h)], axis=0)).astype(dv_ref.dtype)
        dqd = {u: dqd[u] + jnp.concatenate(dqd_s[u], axis=0) for u in units}
        dkt = {u: jnp.concatenate(dkt_s[u], axis=0) for u in units}

        db = []
        for i, sl in enumerate(subs):
            _, eq, ek, f, qd, kd, kt, _ = com[i]
            parts = []
            for pair in range(GLA_HEADS // 2):
                blk, u0, u1 = blks[2 * pair], (i, 2 * pair), (i, 2 * pair + 1)
                dqd_b, dkd_b, dkt_b = dqd[u0] + dqd[u1], dkd[u0] + dkd[u1], dkt[u0] + dkt[u1]
                dqk_ref[sl, blk] = (dqd_b * (eq[:, blk] * GLA_DK ** -0.5)).astype(dqk_ref.dtype)
                dqk_ref[sl, 256 + LANES * pair:256 + LANES * (pair + 1)] = (dkd_b * ek[:, blk] + dkt_b * f[:, blk]).astype(dqk_ref.dtype)
                dkt_kt = dkt_b * kt[:, blk]
                dbp = dqd_b * qd[:, blk] - dkd_b * kd[:, blk] - dkt_kt
                dbl = [jnp.sum(dkt_kt[rs], axis=0, keepdims=True) + (ddec[u0][c] + ddec[u1][c]) * decs[i][c][:, blk]
                       for c, rs in enumerate(chunks)]
                parts.append(jnp.where(last_row, dbp + _per_chunk_rows(dbl, LANES), dbp))
            db.append(jnp.concatenate(parts, axis=1))
        dla = [_tri_matmul(triu_b, db[i]) for i in range(nsub)]
        dz32 = [dla[i] * (1.0 / GLA_TAU) * _sigmoid(-com[i][0]) for i in range(nsub)]
        dz = [t.astype(BF) for t in dz32]
        for i, sl in enumerate(subs):
            dga_ref[sl, :] = _dot(dz[i], wd, NT).astype(dga_ref.dtype)
            dwd_ref[...] += _dot(ga[i].astype(BF), dz[i], TN)
            dbd_ref[...] += jnp.sum(dz32[i], axis=0, keepdims=True)

    def col(width, off):
        return pl.BlockSpec((tb, width), lambda i: (nb - 1 - i, off // width))

    def rev(width):
        return pl.BlockSpec((tb, width), lambda i: (nb - 1 - i, 0))

    return pl.pallas_call(
        body, name="gla_bwd", grid=(nb,),
        in_specs=[col(512, OFF_QK), col(512, OFF_V), col(512, OFF_GZ), col(LANES, OFF_GA), rev(512), rev(512),
                  pl.BlockSpec((tb // GLA_CHUNK, GLA_HEADS, LANES, LANES), lambda i: (nb - 1 - i, 0, 0, 0)),
                  _full((LANES, 256)), _full((1, 256)), _full((1, 512)), _full((sub, sub)), _full((sub, sub))],
        out_specs=[rev(512), rev(512), rev(512), rev(LANES), _full((LANES, 256)), _full((1, 256)), _full((1, 512))],
        out_shape=[jax.ShapeDtypeStruct((s, 512), BF), jax.ShapeDtypeStruct((s, 512), BF),
                   jax.ShapeDtypeStruct((s, 512), BF), jax.ShapeDtypeStruct((s, LANES), BF),
                   jax.ShapeDtypeStruct((LANES, 256), F32), jax.ShapeDtypeStruct((1, 256), F32),
                   jax.ShapeDtypeStruct((1, 512), F32)],
        scratch_shapes=[pltpu.VMEM((GLA_HEADS, LANES, LANES), F32)],
        compiler_params=_params(("arbitrary",)),
    )(proj, proj, proj, proj, dog, opre, sprev, wdecp, bdec, ggla, _gla_triangle(sub), _gla_triangle(sub).T)


_SWA_COL_HEADS = (0, 2, 1, 3, 4, 6, 5, 7)
_SWA_COLS = SWA_HEADS * SWA_BLOCK


def _swa_masks():
    lo2 = lax.broadcasted_iota(jnp.int32, (2 * SWA_BLOCK, LANES), 1) < 64
    lane1 = lax.broadcasted_iota(jnp.int32, (SWA_BLOCK, LANES), 1)
    first_half = (lane1 % 64) < 32
    key = lax.broadcasted_iota(jnp.int32, (SWA_BLOCK, _SWA_COLS), 0)
    query = lax.broadcasted_iota(jnp.int32, (SWA_BLOCK, _SWA_COLS), 1) % SWA_BLOCK
    return lo2, lane1 < 64, first_half, key > query


def _merge_band(t, prev_mask, prev_bias=None):
    prev = t[:SWA_BLOCK] if prev_bias is None else t[:SWA_BLOCK] + prev_bias
    return jnp.where(prev_mask, prev, t[SWA_BLOCK:])


def _split_band(t, prev_mask_b):
    prev = t * prev_mask_b
    return jnp.concatenate([prev, t - prev], axis=0)


def _kv_variants(t, lo2):
    tr = pltpu.roll(t, 64, 1)
    lo_v = [jnp.where(lo2, t, 0.0).astype(BF), jnp.where(lo2, tr, 0.0).astype(BF)]
    hi_v = [jnp.where(lo2, 0.0, tr).astype(BF), jnp.where(lo2, 0.0, t).astype(BF)]
    return lo_v, hi_v


def _kv_variants_t(t):
    tt = t.T
    sw = jnp.concatenate([tt[64:], tt[:64]], axis=0)
    top = lax.broadcasted_iota(jnp.int32, tt.shape, 0) < 64
    lo_v = [jnp.where(top, tt, 0.0).astype(BF), jnp.where(top, sw, 0.0).astype(BF)]
    hi_v = [jnp.where(top, 0.0, sw).astype(BF), jnp.where(top, 0.0, tt).astype(BF)]
    return lo_v, hi_v


def _swa_scores(qg, k_lo, k_hi):
    return jnp.concatenate([_dot(k_lo[0], qg[0], NT), _dot(k_hi[0], qg[0], NT),
                            _dot(k_lo[1], qg[1], NT), _dot(k_hi[1], qg[1], NT)], axis=1)


def _sink_row(sinks_ref):
    return jnp.concatenate([jnp.full((1, SWA_BLOCK), sinks_ref[0, hd], F32) for hd in _SWA_COL_HEADS], axis=1)


def _swa_softmax(st, prev_mask, prev_bias, sink):
    st = _merge_band(st, prev_mask, prev_bias)
    m = jnp.maximum(jnp.max(st, axis=0, keepdims=True), sink)
    ex = jnp.exp(st - m)
    es = jnp.exp(sink - m)
    inv = 1.0 / (jnp.sum(ex, axis=0, keepdims=True) + es)
    return ex, es, inv


def _no_prev_bias(block_index):
    return jnp.where(block_index > 0, 0.0, -1e30).astype(F32)


def _swa_queries(sq_ref, rows, cosb, sinb, first_half):
    qs = [_rope(sq_ref[rows, p * LANES:(p + 1) * LANES], cosb, sinb, first_half) * 0.125 for p in range(4)]
    return [jnp.concatenate(qs[0:2], axis=0), jnp.concatenate(qs[2:4], axis=0)]


def _phase_steps(nsteps, phases):
    return [min(nsteps - 1, (k * nsteps) // phases) for k in range(phases - 1)] + [nsteps - 1]


def _swa_fwd(proj, cos, sin, sinks, half_out):
    s = proj.shape[0]
    nq = min(SWA_QBLOCKS_FWD, s // SWA_BLOCK)
    tq = nq * SWA_BLOCK
    steps = _phase_steps(s // tq, 4)

    def body(sq_ref, sz_ref, sk_ref, sv_ref, cos_ref, sin_ref, sinks_ref, hout_hbm, os_ref, opre_ref, wout_hbm,
             kprev, vprev, *gather_sems):
        n = pl.program_id(0)

        @pl.when(n == 0)
        def _():
            kprev[...] = jnp.zeros_like(kprev)
            vprev[...] = jnp.zeros_like(vprev)

        gather = _Gather(hout_hbm, wout_hbm, *gather_sems, chunks=WEIGHT_CHUNKS)
        for step, phase in zip(steps, (gather.start, gather.pass_on, gather.relay_diagonal, gather.finish)):
            pl.when(n == step)(phase)

        lo2, _, first_half, prev_mask = _swa_masks()
        prev_mask_b = jnp.where(prev_mask, 1.0, 0.0).astype(BF)
        sink = _sink_row(sinks_ref)
        blocks = range(nq)
        rows = [slice(j * SWA_BLOCK, (j + 1) * SWA_BLOCK) for j in blocks]
        cosb = [cos_ref[rows[j], :] for j in blocks]
        sinb = [sin_ref[rows[j], :] for j in blocks]
        kc = [_rope(sk_ref[rows[j], :], cosb[j], sinb[j], first_half) for j in blocks]
        vc = [sv_ref[rows[j], :] for j in blocks]
        kcat = [jnp.concatenate([kprev[...] if j == 0 else kc[j - 1], kc[j]], axis=0) for j in blocks]
        vcat = [jnp.concatenate([vprev[...] if j == 0 else vc[j - 1], vc[j]], axis=0) for j in blocks]
        kprev[...] = kc[-1]
        vprev[...] = vc[-1]
        kvar = [_kv_variants(kcat[j], lo2) for j in blocks]
        vtvar = [_kv_variants_t(vcat[j]) for j in blocks]
        qg = [[q.astype(BF) for q in _swa_queries(sq_ref, rows[j], cosb[j], sinb[j], first_half)] for j in blocks]
        st = [_swa_scores(qg[j], *kvar[j]) for j in blocks]
        soft = [_swa_softmax(st[j], prev_mask, _no_prev_bias(n) if j == 0 else None, sink) for j in blocks]
        pt = [_split_band(soft[j][0].astype(BF), prev_mask_b) for j in blocks]
        og = {}
        for j in blocks:
            inv = soft[j][2]
            for g in range(2):
                c0, c1, c2 = 512 * g, 512 * g + 256, 512 * g + 512
                ot = (_dot(vtvar[j][0][g], pt[j][:, c0:c1]) * inv[:, c0:c1]
                      + _dot(vtvar[j][1][g], pt[j][:, c1:c2]) * inv[:, c1:c2])
                og[(j, g)] = ot.T
        for j in blocks:
            for g in range(2):
                for i in range(2):
                    ls = slice((2 * g + i) * LANES, (2 * g + i + 1) * LANES)
                    o = og[(j, g)][i * SWA_BLOCK:(i + 1) * SWA_BLOCK]
                    sz = sz_ref[rows[j], ls]
                    opre_ref[rows[j], ls] = o
                    os_ref[rows[j], ls] = (o * (sz * _sigmoid(sz))).astype(os_ref.dtype)

    def col(width, off):
        return pl.BlockSpec((tq, width), lambda i: (i, off // width))

    row = pl.BlockSpec((tq, LANES), lambda i: (i, 0))
    return pl.pallas_call(
        body, name="swa_fwd", grid=(s // tq,),
        in_specs=[col(512, OFF_SQ), col(512, OFF_SZ), col(LANES, OFF_SK), col(LANES, OFF_SV), row, row,
                  pl.BlockSpec(memory_space=pltpu.SMEM), pl.BlockSpec(memory_space=pl.ANY)],
        out_specs=[pl.BlockSpec((tq, 512), lambda i: (i, 0))] * 2 + [pl.BlockSpec(memory_space=pl.ANY)],
        out_shape=[jax.ShapeDtypeStruct((s, 512), BF), jax.ShapeDtypeStruct((s, 512), F32),
                   jax.ShapeDtypeStruct((8,) + half_out.shape, half_out.dtype)],
        scratch_shapes=[pltpu.VMEM((SWA_BLOCK, LANES), F32)] * 2 + _gather_sems(WEIGHT_CHUNKS),
        compiler_params=_params(("arbitrary",)),
    )(proj, proj, proj, proj, cos, sin, sinks, half_out)


def _swa_bwd(proj, dos, opre, cos, sin, sinks, dw_out_parts):
    s = proj.shape[0]
    nq = min(SWA_QBLOCKS, s // SWA_BLOCK)
    tq = nq * SWA_BLOCK
    steps = _phase_steps(s // tq, 5)
    _, r_out, c_out = dw_out_parts.shape

    def body(sq_ref, sz_ref, sk_ref, sv_ref, dos_ref, opre_ref, cos_ref, sin_ref, sinks_ref, pout_hbm,
             dsq_ref, dsz_ref, dsk_ref, dsv_ref, dsink_ref, gout_hbm, kprev, vprev, cprev, sprev, *reduce_scratch):
        n = pl.program_id(0)

        @pl.when(n == 0)
        def _():
            kprev[...] = jnp.zeros_like(kprev)
            vprev[...] = jnp.zeros_like(vprev)
            cprev[...] = jnp.zeros_like(cprev)
            sprev[...] = jnp.zeros_like(sprev)
            for hd in range(SWA_HEADS):
                dsink_ref[0, hd] = 0.0

        reduce = _Reduce(pout_hbm, gout_hbm, *reduce_scratch)
        phases = (reduce.start, reduce.combine_and_send, reduce.send_joint, reduce.total_and_share, reduce.finish)
        for step, phase in zip(steps, phases):
            pl.when(n == step)(phase)

        lo2, lo1, first_half, prev_mask = _swa_masks()
        prev_mask_b = jnp.where(prev_mask, 1.0, 0.0).astype(BF)
        lo1s = jnp.concatenate([lo1, lo1], axis=0)
        sink = _sink_row(sinks_ref)

        def home(m0, m1):
            t0 = m0 + pltpu.roll(m0, 64, 1)
            t1 = m1 + pltpu.roll(m1, 64, 1)
            return jnp.where(lo2, t0, t1)

        kp, vp, cp_, sp_ = kprev[...], vprev[...], cprev[...], sprev[...]
        for j in range(nq):
            rows = slice(j * SWA_BLOCK, (j + 1) * SWA_BLOCK)
            blk = n * nq + j
            cosb, sinb = cos_ref[rows, :], sin_ref[rows, :]
            kc = _rope(sk_ref[rows, :], cosb, sinb, first_half)
            vc = sv_ref[rows, :]
            kcat = jnp.concatenate([kp, kc], axis=0)
            k_lo, k_hi = _kv_variants(kcat, lo2)
            kt_lo, kt_hi = _kv_variants_t(kcat)
            v_lo, v_hi = _kv_variants(jnp.concatenate([vp, vc], axis=0), lo2)
            qg32 = _swa_queries(sq_ref, rows, cosb, sinb, first_half)
            qg = [q.astype(BF) for q in qg32]
            ex, es, inv = _swa_softmax(_swa_scores(qg, k_lo, k_hi), prev_mask, _no_prev_bias(n) if j == 0 else None, sink)
            pr, ps = ex * inv, es * inv

            dog32 = []
            for g in range(2):
                parts = []
                for i in range(2):
                    ls = slice((2 * g + i) * LANES, (2 * g + i + 1) * LANES)
                    sz = sz_ref[rows, ls]
                    sg = _sigmoid(sz)
                    dos_p = dos_ref[rows, ls]
                    dsz_ref[rows, ls] = (dos_p * opre_ref[rows, ls] * (sg * (1.0 + sz * (1.0 - sg)))).astype(dsz_ref.dtype)
                    parts.append(dos_p * (sz * sg))
                dog32.append(jnp.concatenate(parts, axis=0))
            dog = [t.astype(BF) for t in dog32]
            dpr = _merge_band(jnp.concatenate([_dot(v_lo[0], dog[0], NT), _dot(v_hi[0], dog[0], NT),
                                               _dot(v_lo[1], dog[1], NT), _dot(v_hi[1], dog[1], NT)], axis=1), prev_mask)
            rd = jnp.sum(pr * dpr, axis=0, keepdims=True)
            ds = _split_band((pr * (dpr - rd)).astype(BF), prev_mask_b)
            prb = _split_band(pr.astype(BF), prev_mask_b)
            sink_term = ps * rd
            for r, hd in enumerate(_SWA_COL_HEADS):
                dsink_ref[0, hd] += -jnp.sum(sink_term[:, r * SWA_BLOCK:(r + 1) * SWA_BLOCK])

            dk_g, dv_g = [], []
            for g in range(2):
                c0, c1, c2 = 512 * g, 512 * g + 256, 512 * g + 512
                dq = (_dot(kt_lo[g], ds[:, c0:c1]) + _dot(kt_hi[g], ds[:, c1:c2])).T
                for i in range(2):
                    ls = slice((2 * g + i) * LANES, (2 * g + i + 1) * LANES)
                    dsq_ref[rows, ls] = _rope_t(dq[i * SWA_BLOCK:(i + 1) * SWA_BLOCK] * 0.125, cosb, sinb,
                                                first_half).astype(dsq_ref.dtype)
                q_split = jnp.concatenate([jnp.where(lo1s, qg32[g], 0.0), jnp.where(lo1s, 0.0, qg32[g])], axis=0).astype(BF)
                do_split = jnp.concatenate([jnp.where(lo1s, dog32[g], 0.0), jnp.where(lo1s, 0.0, dog32[g])], axis=0).astype(BF)
                dk_g.append(_dot(ds[:, c0:c2], q_split))
                dv_g.append(_dot(prb[:, c0:c2], do_split))
            dk = home(dk_g[0], dk_g[1])
            dv = home(dv_g[0], dv_g[1])
            cur = pl.ds(pl.multiple_of(blk * SWA_BLOCK, SWA_BLOCK), SWA_BLOCK)
            dsk_ref[cur, :] = _rope_t(dk[SWA_BLOCK:], cosb, sinb, first_half)
            dsv_ref[cur, :] = dv[SWA_BLOCK:]
            dk_prev = _rope_t(dk[:SWA_BLOCK], cp_, sp_, first_half)
            dv_prev = dv[:SWA_BLOCK]
            if j == 0:
                @pl.when(n > 0)
                def _():
                    prv = pl.ds(pl.multiple_of((blk - 1) * SWA_BLOCK, SWA_BLOCK), SWA_BLOCK)
                    dsk_ref[prv, :] += dk_prev
                    dsv_ref[prv, :] += dv_prev
            else:
                prv = pl.ds(pl.multiple_of((blk - 1) * SWA_BLOCK, SWA_BLOCK), SWA_BLOCK)
                dsk_ref[prv, :] += dk_prev
                dsv_ref[prv, :] += dv_prev
            kp, vp, cp_, sp_ = kc, vc, cosb, sinb
        kprev[...] = kp
        vprev[...] = vp
        cprev[...] = cp_
        sprev[...] = sp_

    def col(width, off):
        return pl.BlockSpec((tq, width), lambda i: (i, off // width))

    row = pl.BlockSpec((tq, LANES), lambda i: (i, 0))
    wide = pl.BlockSpec((tq, 512), lambda i: (i, 0))
    return pl.pallas_call(
        body, name="swa_bwd", grid=(s // tq,),
        in_specs=[col(512, OFF_SQ), col(512, OFF_SZ), col(LANES, OFF_SK), col(LANES, OFF_SV), wide, wide, row, row,
                  pl.BlockSpec(memory_space=pltpu.SMEM), pl.BlockSpec(memory_space=pl.ANY)],
        out_specs=[wide, wide, _full((s, LANES)), _full((s, LANES)), pl.BlockSpec(memory_space=pltpu.SMEM),
                   pl.BlockSpec(memory_space=pl.ANY)],
        out_shape=[jax.ShapeDtypeStruct((s, 512), BF), jax.ShapeDtypeStruct((s, 512), BF),
                   jax.ShapeDtypeStruct((s, LANES), F32), jax.ShapeDtypeStruct((s, LANES), F32),
                   jax.ShapeDtypeStruct((1, SWA_HEADS), F32), jax.ShapeDtypeStruct((r_out, c_out), F32)],
        scratch_shapes=[pltpu.VMEM((SWA_BLOCK, LANES), F32)] * 4 + _reduce_scratch(r_out, c_out),
        compiler_params=_params(("arbitrary",)),
    )(proj, proj, proj, proj, dos, opre, cos, sin, sinks, dw_out_parts)


def _outproj(og, osw, w_out, x2d, target, gate, g_final):
    s = x2d.shape[0]
    tm = min(512, s)

    def body(og_ref, os_ref, w_ref, x_ref, t_ref, gate_ref, gf_ref,
             dx2_ref, dog_ref, dos_ref, dw_ref, loss_ref, dgf_ref, dgate_ref):
        @pl.when(pl.program_id(0) == 0)
        def _():
            dw_ref[...] = jnp.zeros_like(dw_ref)
            loss_ref[...] = jnp.zeros_like(loss_ref)
            dgf_ref[...] = jnp.zeros_like(dgf_ref)
            dgate_ref[...] = jnp.zeros_like(dgate_ref)

        w = w_ref[...]
        gate, gf = gate_ref[...], gf_ref[...]
        subs = _subtiles(tm)
        ogv = [og_ref[sl, :] for sl in subs]
        osv = [os_ref[sl, :] for sl in subs]
        y = [_dot(ogv[k], w[:512]) + _dot(osv[k], w[512:]) for k in range(len(subs))]
        dys = []
        for k, sl in enumerate(subs):
            x2 = x_ref[sl, :] + gate * y[k]
            r = lax.rsqrt(jnp.mean(x2 * x2, axis=-1, keepdims=True) + RMS_EPS)
            xn = x2 * r
            err = xn * gf - t_ref[sl, :]
            loss_ref[...] += 0.5 * jnp.sum(jnp.mean(err * err, axis=-1, keepdims=True), axis=0, keepdims=True)
            dyf = err * (1.0 / D_MODEL)
            dgf_ref[...] += jnp.sum(dyf * xn, axis=0, keepdims=True)
            t = dyf * gf
            dx2 = r * (t - xn * jnp.mean(t * xn, axis=-1, keepdims=True))
            dx2_ref[sl, :] = dx2
            dgate_ref[...] += jnp.sum(dx2 * y[k], axis=0, keepdims=True)
            dys.append((dx2 * gate).astype(BF))
            dmix = _dot(dys[k], w, NT)
            dog_ref[sl, :] = dmix[:, :512]
            dos_ref[sl, :] = dmix[:, 512:]
        dy = jnp.concatenate(dys, axis=0)
        dw_ref[:512, :] += _dot(og_ref[...], dy, TN)
        dw_ref[512:, :] += _dot(os_ref[...], dy, TN)

    half = pl.BlockSpec((tm, 512), lambda i: (i, 0))
    rowb = pl.BlockSpec((tm, D_MODEL), lambda i: (i, 0))
    vec = _full((1, D_MODEL))
    return pl.pallas_call(
        body, name="outproj", grid=(s // tm,),
        in_specs=[half, half, _full((D_MODEL, D_MODEL)), rowb, rowb, vec, vec],
        out_specs=[rowb, half, half, _full((D_MODEL, D_MODEL)), _full((1, 1)), vec, vec],
        out_shape=[jax.ShapeDtypeStruct((s, D_MODEL), F32), jax.ShapeDtypeStruct((s, 512), F32),
                   jax.ShapeDtypeStruct((s, 512), F32), jax.ShapeDtypeStruct((D_MODEL, D_MODEL), F32),
                   jax.ShapeDtypeStruct((1, 1), F32), jax.ShapeDtypeStruct((1, D_MODEL), F32),
                   jax.ShapeDtypeStruct((1, D_MODEL), F32)],
        compiler_params=_params(("arbitrary",)),
    )(og, osw, w_out, x2d, target, gate, g_final)


_PIECES = ((OFF_QK, 512), (OFF_V, 512), (OFF_GZ, 512), (OFF_SQ, 512), (OFF_SZ, 512),
           (OFF_SK, LANES), (OFF_SV, LANES), (OFF_GA, LANES))

_UNPAD_ROWS = ((OFF_QK, 0, 1024),
               (OFF_GA, 1024, GLA_RANK),
               (OFF_GZ, 1040, 1024),
               (OFF_SK, 2064, 256),
               (OFF_SZ, 2320, 512))


def _inproj_bwd(x2d, shift, sc1p, g_norm, w_t, dx2, pieces):
    s = x2d.shape[0]
    tm = min(512, s)
    nsteps = s // tm

    def body(x_ref, sh_ref, sc_ref, g_ref, w_hbm, dx2_ref, *rest):
        piece_refs = rest[:len(_PIECES)]
        gx_ref, dw_hbm, dsh_ref, dsc_ref, dg_ref, w_vm, dw_vm, in_sems, out_sems = rest[len(_PIECES):]
        i = pl.program_id(0)

        @pl.when(i == 0)
        def _():
            loads = _load_w_padded(w_hbm, w_vm, in_sems)
            dw_vm[...] = jnp.zeros_like(dw_vm)
            dsh_ref[...] = jnp.zeros_like(dsh_ref)
            dsc_ref[...] = jnp.zeros_like(dsc_ref)
            dg_ref[...] = jnp.zeros_like(dg_ref)
            for cp in loads:
                cp.wait()

        g, sc1p_v, shift_v = g_ref[...], sc_ref[...], sh_ref[...]
        subs = _subtiles(tm)
        dhs = []
        for sl in subs:
            dh = None
            for (off, width), pr in zip(_PIECES, piece_refs):
                part = _dot(pr[sl, :].astype(BF), w_vm[off:off + width, :])
                dh = part if dh is None else dh + part
            dhs.append(dh)
        norm = [_modnorm(x_ref[sl, :], g, sc1p_v, shift_v) for sl in subs]
        hb = jnp.concatenate([h.astype(BF) for _, _, h in norm], axis=0)
        for (off, width), pr in zip(_PIECES, piece_refs):
            dw_vm[off:off + width, :] += _dot(pr[...].astype(BF), hb, TN)
        for sl, (xn, r, _), dh in zip(subs, norm, dhs):
            dsh_ref[...] += jnp.sum(dh, axis=0, keepdims=True)
            dsc_ref[...] += jnp.sum(dh * (xn * g), axis=0, keepdims=True)
            dg_ref[...] += jnp.sum(dh * xn * sc1p_v, axis=0, keepdims=True)
            dxn = dh * g * sc1p_v
            gx_ref[sl, :] = dx2_ref[sl, :] + r * (dxn - xn * jnp.mean(dxn * xn, axis=-1, keepdims=True))

        @pl.when(i == nsteps - 1)
        def _():
            copies = [pltpu.make_async_copy(dw_vm.at[src:src + n], dw_hbm.at[dst:dst + n], out_sems.at[k])
                      for k, (src, dst, n) in enumerate(_UNPAD_ROWS)]
            for cp in copies:
                cp.start()
            for cp in copies:
                cp.wait()

    rowb = pl.BlockSpec((tm, D_MODEL), lambda i: (i, 0))
    vec = _full((1, D_MODEL))
    anyspec = pl.BlockSpec(memory_space=pl.ANY)
    piece_specs = [pl.BlockSpec((tm, width), lambda i: (i, 0)) for _, width in _PIECES]
    return pl.pallas_call(
        body, name="inproj_bwd", grid=(nsteps,),
        in_specs=[rowb, vec, vec, vec, anyspec, rowb] + piece_specs,
        out_specs=[rowb, anyspec, vec, vec, vec],
        out_shape=[jax.ShapeDtypeStruct((s, D_MODEL), F32), jax.ShapeDtypeStruct((D_IN, D_MODEL), F32),
                   jax.ShapeDtypeStruct((1, D_MODEL), F32), jax.ShapeDtypeStruct((1, D_MODEL), F32),
                   jax.ShapeDtypeStruct((1, D_MODEL), F32)],
        scratch_shapes=[pltpu.VMEM((D_PAD, D_MODEL), BF), pltpu.VMEM((D_PAD, D_MODEL), F32),
                        pltpu.SemaphoreType.DMA((len(_UNPAD_ROWS),)), pltpu.SemaphoreType.DMA((len(_UNPAD_ROWS),))],
        compiler_params=_params(("arbitrary",)),
    )(x2d, shift, sc1p, g_norm, w_t, dx2, *pieces)


def _adam(w, g, m, v):
    m2 = ADAM_B1 * m + (1.0 - ADAM_B1) * g
    v2 = ADAM_B2 * v + (1.0 - ADAM_B2) * (g * g)
    m_hat = m2 / (1.0 - ADAM_B1 ** ADAM_STEP)
    v_hat = v2 / (1.0 - ADAM_B2 ** ADAM_STEP)
    delta = -ADAM_LR * (m_hat / (jnp.sqrt(v_hat) + ADAM_EPS) + ADAM_WD * w)
    return delta, m2, v2


def _adamw(w, g, m, v, name):
    rr, cc = w.shape
    tc = min(512, cc)

    def body(w_ref, g_ref, m_ref, v_ref, d_ref, m2_ref, v2_ref):
        d_ref[...], m2_ref[...], v2_ref[...] = _adam(w_ref[...], g_ref[...], m_ref[...], v_ref[...])

    blk = pl.BlockSpec((rr, tc), lambda i: (0, i))
    return pl.pallas_call(
        body, name=name, grid=(cc // tc,), in_specs=[blk] * 4, out_specs=[blk] * 3,
        out_shape=[jax.ShapeDtypeStruct((rr, cc), F32)] * 3,
        compiler_params=_params(("arbitrary",)),
    )(w, g, m, v)


def _adamw_t(w3, g_window, m3, v3, name):
    rr, _, cc = w3.shape
    parts = [slice(q * (cc // 4), (q + 1) * (cc // 4)) for q in range(4)]
    starts = sorted({(rr * j) % 8 for j in range(4)})

    def body(w_hbm, gw_hbm, m_hbm, v_hbm, d_hbm, m2_hbm, v2_hbm, g3_hbm,
             w_vm, m_vm, v_vm, gw_vm, d_vm, m2_vm, v2_vm, g_vm, in_sems, out_sems):
        start = lax.rem(rr * (2 * lax.axis_index("x") + lax.axis_index("y")), 8)
        ins = ((w_hbm, w_vm), (m_hbm, m_vm), (v_hbm, v_vm))
        outs = ((d_vm, d_hbm), (m2_vm, m2_hbm), (v2_vm, v2_hbm), (g_vm, g3_hbm))
        loads = [[pltpu.make_async_copy(src.at[:, 0, p], dst.at[:, p], in_sems.at[4 * q + k]) for k, (src, dst) in enumerate(ins)]
                 + [pltpu.make_async_copy(gw_hbm.at[:, p], gw_vm.at[:, p], in_sems.at[4 * q + 3])]
                 for q, p in enumerate(parts)]
        stores = [[pltpu.make_async_copy(src.at[:, p], dst.at[:, 0, p], out_sems.at[4 * q + k]) for k, (src, dst) in enumerate(outs)]
                  for q, p in enumerate(parts)]
        for group in loads:
            for cp in group:
                cp.start()
        for q, p in enumerate(parts):
            for cp in loads[q]:
                cp.wait()
            g = gw_vm[starts[0]:starts[0] + rr, p]
            for o in starts[1:]:
                g = jnp.where(start == o, gw_vm[o:o + rr, p], g)
            g_vm[:, p] = g
            d_vm[:, p], m2_vm[:, p], v2_vm[:, p] = _adam(w_vm[:, p], g, m_vm[:, p], v_vm[:, p])
            for cp in stores[q]:
                cp.start()
        for group in stores:
            for cp in group:
                cp.wait()

    hbm = pl.BlockSpec(memory_space=pl.ANY)
    return pl.pallas_call(
        body, name=name, grid=(1,), in_specs=[hbm] * 4,
        out_specs=[hbm] * 4, out_shape=[jax.ShapeDtypeStruct((rr, 1, cc), F32)] * 4,
        scratch_shapes=[pltpu.VMEM((rr, cc), F32)] * 3 + [pltpu.VMEM(g_window.shape, F32)] + [pltpu.VMEM((rr, cc), F32)] * 4
        + [pltpu.SemaphoreType.DMA((16,)), pltpu.SemaphoreType.DMA((16,))],
        compiler_params=_params(("arbitrary",)),
    )(w3, g_window, m3, v3)


def _ada_update(c_all, dmod_cols, w, m, v):
    rr, cc = w.shape
    tr = min(512, rr)
    c_all = jnp.pad(c_all, ((0, 8), (0, 0)))
    dmod_cols = jnp.pad(dmod_cols, ((0, 8), (0, 0)))

    def body(c_ref, dm_ref, w_ref, m_ref, v_ref, g_ref, d_ref, m2_ref, v2_ref):
        cv = c_ref[...]
        sc = (cv * _sigmoid(cv)).astype(BF)
        g = _dot(sc, dm_ref[...].astype(BF), TN)
        g_ref[...] = g
        d_ref[...], m2_ref[...], v2_ref[...] = _adam(w_ref[...], g, m_ref[...], v_ref[...])

    blk = pl.BlockSpec((tr, cc), lambda i: (i, 0))
    return pl.pallas_call(
        body, name="ada_update", grid=(rr // tr,),
        in_specs=[pl.BlockSpec((16, tr), lambda i: (0, i)), _full((16, cc)), blk, blk, blk],
        out_specs=[blk] * 4, out_shape=[jax.ShapeDtypeStruct((rr, cc), F32)] * 4,
        compiler_params=_params(("arbitrary",)),
    )(c_all, dmod_cols, w, m, v)


def _small_update(parts, weights, moms, vels):
    n = len(weights)

    def body(*refs):
        p_refs, w_refs, m_refs, v_refs = refs[:n + 1], refs[n + 1:2 * n + 1], refs[2 * n + 1:3 * n + 1], refs[3 * n + 1:4 * n + 1]
        outs = refs[4 * n + 1:]
        for i in range(n):
            g = p_refs[i][0]
            for d in range(1, 8):
                g = g + p_refs[i][d]
            delta, m2, v2 = _adam(w_refs[i][...], g, m_refs[i][...], v_refs[i][...])
            outs[4 * i][...] = g
            outs[4 * i + 1][...] = delta
            outs[4 * i + 2][...] = m2
            outs[4 * i + 3][...] = v2
        tot = p_refs[n][0]
        for d in range(1, 8):
            tot = tot + p_refs[n][d]
        outs[4 * n][...] = tot

    out_shape = []
    for w in weights:
        out_shape += [jax.ShapeDtypeStruct(w.shape, F32)] * 4
    out_shape.append(jax.ShapeDtypeStruct(parts[n].shape[1:], F32))
    return pl.pallas_call(body, name="small_update", out_shape=out_shape, compiler_params=_params())(
        *parts, *weights, *moms, *vels)


def _rows8(a):
    flat = a.reshape(-1)
    rows = -(-flat.shape[0] // LANES)
    rows8 = -(-rows // 8) * 8
    flat = jnp.pad(flat, (0, rows8 * LANES - flat.shape[0]))
    return flat.reshape(rows8, LANES)


def kernel(x, c, positions, w_ada, b_ada, g_norm, w_in, w_decay, b_decay, g_gla_head, sinks, w_out, g_final, loss_target, m_w_ada, m_b_ada, m_g_norm, m_w_in, m_w_decay, m_b_decay, m_g_gla_head, m_sinks, m_w_out, m_g_final, v_w_ada, v_b_ada, v_g_norm, v_w_in, v_w_decay, v_b_decay, v_g_gla_head, v_sinks, v_w_out, v_g_final):
    ax, ay, ac = lax.axis_index("x"), lax.axis_index("y"), lax.axis_index("c")
    chip = 2 * ax + ay
    dev = 2 * chip + ac
    s = x.shape[1]
    x2d = x[0]
    target = loss_target[0]
    w_ada2, w_out2, w_dec2 = w_ada[0], w_out[0], w_decay[0]
    w_in_t = w_in[0].T
    ada_cols = w_ada2.shape[1]
    in_cols = w_in_t.shape[0]
    out_rows = w_out2.shape[0]
    half = D_MODEL // 2

    cw = jnp.concatenate([c.reshape(8, LANES), w_dec2.reshape(8, LANES)], axis=0)
    b_shard = lax.dynamic_slice(b_ada, (0, chip * ada_cols), (1, ada_cols))
    half_in = lax.dynamic_slice(w_in_t, (0, ac * half), (in_cols, half)).astype(BF)
    half_out = lax.dynamic_slice(w_out2, (ac * (out_rows // 2), 0), (out_rows // 2, D_MODEL)).astype(BF)
    inv_freq = 1.0 / (ROPE_THETA ** (jnp.arange(0, 64, 2, dtype=F32) / 64))
    first, mod_all, w_in_all, cos, sin = _prologue(
        cw, w_ada2, b_shard, half_in, positions.reshape(s, 1), jnp.tile(inv_freq, 4).reshape(1, LANES))

    first = first.reshape(8, 2, 8, LANES)
    c_all = first[:, 0].reshape(8, D_MODEL)
    w_dec_full = first[0::2, 1].reshape(4, GLA_RANK, 64).transpose(1, 0, 2).reshape(GLA_RANK, 256)
    mod = mod_all.reshape(4, 2, 8, ada_cols)[:, 0]
    mod = lax.dynamic_slice(mod, (0, dev, 0), (4, 1, ada_cols)).reshape(1, 4 * ada_cols)
    shift, sc1p, gate = mod[:, :D_MODEL], 1.0 + mod[:, D_MODEL:2 * D_MODEL], mod[:, 2 * D_MODEL:]
    w_t = w_in_all.reshape(4 * in_cols, D_MODEL)

    wdecp = jnp.pad(w_dec_full, ((0, LANES - GLA_RANK), (0, 0))).astype(BF)

    proj = _inproj_fwd(x2d, shift, sc1p, g_norm, w_t)
    og, o_gla, sprev = _gla_fwd(proj, wdecp, b_decay, g_gla_head)
    osw, o_swa, w_out_all = _swa_fwd(proj, cos, sin, sinks, half_out)
    w_out_all = w_out_all.reshape(D_MODEL, D_MODEL)
    dx2, dog, dos, dw_out, loss_p, dgf, dgate = _outproj(og, osw, w_out_all, x2d, target, gate, g_final.reshape(1, D_MODEL))
    dsq, dsz, dsk, dsv, dsinks, g_w_out = _swa_bwd(proj, dos, o_swa, cos, sin, sinks, dw_out.reshape(4, out_rows, D_MODEL))
    dqk, dv, dgz, dga, dwdp, dbd, dgg = _gla_bwd(proj, dog, o_gla, sprev, wdecp, b_decay, g_gla_head)
    pieces = (dqk, dv, dgz, dsq, dsz, dsk, dsv, dga)
    gx, dw_in_t, dshift, dscale, dgn = _inproj_bwd(x2d, shift, sc1p, g_norm, w_t, dx2, pieces)

    segs = [jnp.concatenate([dshift, dscale, dgate], axis=1), dgn, dgf, dwdp[:GLA_RANK], dbd, dgg, dsinks, loss_p]
    packed = [_rows8(a) for a in segs]
    offs = [0]
    for a in packed:
        offs.append(offs[-1] + a.shape[0])
    g_window, small = _epilogue(dw_in_t, jnp.concatenate(packed, axis=0))

    def seg(i, size):
        return small[:, offs[i]:offs[i + 1]].reshape(8, -1)[:, :size]

    dmod_all = seg(0, 3 * D_MODEL)
    dwd_all = lax.dynamic_slice(seg(3, GLA_RANK * 256).reshape(8, GLA_RANK, 256), (0, 0, chip * 64), (8, GLA_RANK, 64))
    parts = [dmod_all.reshape(8, 1, 3 * D_MODEL), seg(1, D_MODEL).reshape(8, 1, D_MODEL), dwd_all,
             seg(4, 256).reshape(8, 1, 256), seg(5, 512).reshape(8, 1, 512), seg(6, SWA_HEADS).reshape(8, 1, SWA_HEADS),
             seg(2, D_MODEL).reshape(8, 1, D_MODEL), seg(7, LANES).reshape(8, 1, LANES)]
    smalls = _small_update(
        parts,
        [b_ada, g_norm, w_dec2, b_decay, g_gla_head, sinks, g_final.reshape(1, D_MODEL)],
        [m_b_ada, m_g_norm, m_w_decay[0], m_b_decay, m_g_gla_head, m_sinks, m_g_final.reshape(1, D_MODEL)],
        [v_b_ada, v_g_norm, v_w_decay[0], v_b_decay, v_g_gla_head, v_sinks, v_g_final.reshape(1, D_MODEL)])
    (g_b_ada, d_b_ada, nm_b_ada, nv_b_ada, g_gn, d_gn, nm_gn, nv_gn, g_wd, d_wd, nm_wd, nv_wd,
     g_bd, d_bd, nm_bd, nv_bd, g_gg, d_gg, nm_gg, nv_gg, g_sk, d_sk, nm_sk, nv_sk,
     g_gf, d_gf, nm_gf, nv_gf, loss_row) = smalls
    loss = loss_row[0, 0]

    dmod_cols = lax.dynamic_slice(dmod_all, (0, chip * ada_cols), (8, ada_cols))
    g_w_ada, d_w_ada, nm_w_ada, nv_w_ada = _ada_update(c_all, dmod_cols, w_ada2, m_w_ada[0], v_w_ada[0])
    to3 = lambda a: jnp.transpose(a, (2, 0, 1))
    from3 = lambda a: jnp.transpose(a, (1, 2, 0))[0]
    d3, nm3, nv3, g3 = _adamw_t(to3(w_in), g_window, to3(m_w_in), to3(v_w_in), "adamw_w_in")
    g_w_in, d_w_in, nm_w_in, nv_w_in = from3(g3), from3(d3), from3(nm3), from3(nv3)
    d_w_out, nm_w_out, nv_w_out = _adamw(w_out2, g_w_out, m_w_out[0], v_w_out[0], "adamw_w_out")

    flat = lambda a: a.reshape(D_MODEL)
    grads = [g_w_ada[None], g_b_ada, g_gn, g_w_in[None], g_wd[None], g_bd, g_gg, g_sk, g_w_out[None], flat(g_gf)]
    deltas = [d_w_ada[None], d_b_ada, d_gn, d_w_in[None], d_wd[None], d_bd, d_gg, d_sk, d_w_out[None], flat(d_gf)]
    new_m = [nm_w_ada[None], nm_b_ada, nm_gn, nm_w_in[None], nm_wd[None], nm_bd, nm_gg, nm_sk, nm_w_out[None], flat(nm_gf)]
    new_v = [nv_w_ada[None], nv_b_ada, nv_gn, nv_w_in[None], nv_wd[None], nv_bd, nv_gg, nv_sk, nv_w_out[None], flat(nv_gf)]
    return (loss, gx[None], *grads, *deltas, *new_m, *new_v)
```

```python
import jax
import jax.numpy as jnp
from jax import lax
from jax.experimental import pallas as pl
from jax.experimental.pallas import tpu as pltpu

F32 = jnp.float32
BF = jnp.bfloat16

D_MODEL = 1024
GLA_HEADS = 4
GLA_DK = 64
GLA_CHUNK = 64
GLA_RANK = 16
GLA_TAU = 16.0
GLA_SUB = 256
GLA_ROWS_FWD = 1024
GLA_ROWS_BWD = 512
SWA_HEADS = 8
SWA_BLOCK = 128
SWA_QBLOCKS_FWD = 8
SWA_QBLOCKS = 8
RMS_EPS = 1e-6
ROPE_THETA = 10000.0

OFF_QK, OFF_V, OFF_GZ, OFF_SQ, OFF_SZ, OFF_SK, OFF_SV, OFF_GA = 0, 512, 1024, 1536, 2048, 2560, 2688, 2816
D_PAD = 2944
D_IN = 2832
LANES = 128
VMEM_LIMIT = 56 * 1024 * 1024

ADAM_LR, ADAM_B1, ADAM_B2, ADAM_EPS, ADAM_WD, ADAM_STEP = 0.001, 0.9, 0.999, 1e-08, 0.01, 10

NT = (((1,), (1,)), ((), ()))
TN = (((0,), (0,)), ((), ()))
MESH = pl.DeviceIdType.MESH


def _dot(a, b, dims=None):
    if dims is None:
        return jnp.dot(a, b, preferred_element_type=F32)
    return lax.dot_general(a, b, dims, preferred_element_type=F32)


def _sigmoid(x):
    return 1.0 / (1.0 + jnp.exp(-x))


def _params(sem=None):
    return pltpu.CompilerParams(dimension_semantics=sem, vmem_limit_bytes=VMEM_LIMIT)


def _full(shape):
    return pl.BlockSpec(shape, lambda i: (0,) * len(shape))


def _subtiles(rows, size=256):
    size = min(size, rows)
    return [slice(k * size, (k + 1) * size) for k in range(rows // size)]


WEIGHT_CHUNKS = 4


def _gather_sems(chunks=1):
    return [pltpu.SemaphoreType.DMA((7 * chunks,)), pltpu.SemaphoreType.DMA((7 * chunks,)), pltpu.SemaphoreType.DMA]


_GATHER_SEMS = _gather_sems()


class _Gather:
    def __init__(self, x_ref, out_ref, send_sems, recv_sems, local_sem, slab=None, chunks=1):
        self.slab_of = slab
        self.chunks = chunks
        self.width = x_ref.shape[-1] // chunks
        x, y, c = lax.axis_index("x"), lax.axis_index("y"), lax.axis_index("c")
        self.me, self.sibling, self.c = (x, y, c), (x, y, 1 - c), c
        self.xn, self.yn, self.dg = (1 - x, y), (x, 1 - y), (1 - x, 1 - y)
        self.pass_from = (lax.rem(x + 1 - c, 2), lax.rem(y + c, 2))
        self.pass_to = (lax.rem(x + c, 2), lax.rem(y + 1 - c, 2))
        self.x_ref, self.out_ref, self.send_sems, self.recv_sems = x_ref, out_ref, send_sems, recv_sems
        self.mine = pltpu.make_async_copy(x_ref, self._slab(*self.me), local_sem)

    def _slab(self, px, py, pc):
        if self.slab_of is not None:
            return self.slab_of(self.out_ref, px, py, pc)
        return self.out_ref.at[4 * px + 2 * py + pc]

    def _part(self, ref, q):
        if self.chunks == 1:
            return ref
        lanes = slice(q * self.width, (q + 1) * self.width)
        return ref.at[(slice(None),) * (len(ref.shape) - 1) + (lanes,)]

    def _copy(self, k, q, blk, to, src=None):
        i = k * self.chunks + q
        return pltpu.make_async_remote_copy(
            src_ref=self._part(self._slab(*blk) if src is None else src, q), dst_ref=self._part(self._slab(*blk), q),
            send_sem=self.send_sems.at[i], recv_sem=self.recv_sems.at[i], device_id=to, device_id_type=MESH)

    def _sends(self, q):
        c = self.c
        return [self._copy(0, q, self.me, self.sibling, src=self.x_ref),
                self._copy(1, q, self.me, (*self.xn, c), src=self.x_ref),
                self._copy(2, q, self.me, (*self.yn, c), src=self.x_ref),
                self._copy(3, q, (*self.pass_from, c), (*self.pass_to, c)),
                self._copy(4, q, (*self.xn, c), self.sibling),
                self._copy(5, q, (*self.yn, c), self.sibling),
                self._copy(6, q, (*self.dg, c), self.sibling)]

    def start(self):
        self.mine.start()
        for q in range(self.chunks):
            sends = self._sends(q)
            for k in (1, 2, 0):
                sends[k].start()

    def pass_on(self, only=None):
        for q in range(self.chunks) if only is None else (only,):
            sends = self._sends(q)
            self._copy(1, q, (*self.xn, self.c), self.me).wait_recv()
            self._copy(2, q, (*self.yn, self.c), self.me).wait_recv()
            for k in (3, 4, 5):
                sends[k].start()

    def relay_diagonal(self, only=None):
        for q in range(self.chunks) if only is None else (only,):
            self._copy(3, q, (*self.dg, self.c), self.me).wait_recv()
            self._sends(q)[6].start()

    def relay(self):
        self.pass_on()
        self.relay_diagonal()

    def finish(self):
        c = self.c
        for q in range(self.chunks):
            self._copy(0, q, self.sibling, self.me).wait_recv()
            for k, chip in ((4, self.xn), (5, self.yn), (6, self.dg)):
                self._copy(k, q, (*chip, 1 - c), self.me).wait_recv()
            for cp in self._sends(q):
                cp.wait_send()
        self.mine.wait()


def _prologue(cw, w_ada, b_shard, win_window, win_edges, n_in, pos_col, inv_freq):
    s = pos_col.shape[0]
    rt = min(512, s)
    inner = win_window.shape[0] - 2 * W_TILE

    def body(cw_ref, wada_hbm, b_ref, hin_ref, hedge_ref, pos_hbm, f_ref,
             first_ref, mod_ref, win_ref, edge_ref, cos_hbm, sin_hbm,
             mod_blk, cos_ref, sin_ref, wada_ref, pos_ref, table_sems, local_sems, *sems):
        fetch_w = pltpu.make_async_copy(wada_hbm, wada_ref, local_sems.at[0])
        fetch_p = pltpu.make_async_copy(pos_hbm, pos_ref, local_sems.at[1])
        fetch_w.start()
        fetch_p.start()
        g_c = _Gather(cw_ref, first_ref, *sems[0:3])
        half_lanes = hin_ref.shape[1]

        def lanes_of(pc):
            return pl.ds(pl.multiple_of(pc * half_lanes, half_lanes), half_lanes)

        def inner_rows(px, py):
            return pl.ds(pl.multiple_of((n_in * (2 * px + py)) // W_TILE * W_TILE + W_TILE, W_TILE), inner)

        g_in = _Gather(hin_ref.at[pl.ds(W_TILE, inner), :], win_ref, *sems[3:6], chunks=WEIGHT_CHUNKS,
                       slab=lambda ref, px, py, pc: ref.at[inner_rows(px, py), lanes_of(pc)])
        g_mod = _Gather(mod_blk, mod_ref, *sems[6:9])
        g_edge = _Gather(hedge_ref, edge_ref, *sems[9:12],
                         slab=lambda ref, px, py, pc: ref.at[2 * px + py, :, :, lanes_of(pc)])
        g_c.start()
        g_edge.start()
        g_in.start()
        g_c.relay()
        g_edge.relay()
        g_c.finish()
        c_rows = [jnp.concatenate([first_ref[d, r:r + 1, :] for r in range(8)], axis=1) for d in range(8)]
        c_all = jnp.concatenate(c_rows, axis=0)
        sc = (c_all * _sigmoid(c_all)).astype(BF)
        fetch_w.wait()
        mod_blk[...] = _dot(sc, wada_ref[...].astype(BF)) + b_ref[...]
        g_mod.start()
        fetch_p.wait()

        def rope_rows(i, carry):
            rows = pl.ds(pl.multiple_of(i * rt, rt), rt)
            ang = pos_ref[rows, :].astype(F32) * f_ref[...]
            lane = lax.broadcasted_iota(jnp.int32, ang.shape, 1)
            cos_ref[rows, :] = jnp.cos(ang)
            sn = jnp.sin(ang)
            sin_ref[rows, :] = jnp.where((lane % 64) < 32, -sn, sn)
            pltpu.make_async_copy(cos_ref.at[rows, :], cos_hbm.at[rows, :], table_sems.at[0]).start()
            pltpu.make_async_copy(sin_ref.at[rows, :], sin_hbm.at[rows, :], table_sems.at[1]).start()
            return carry

        waits = ([lambda q=q: g_in.pass_on(q) for q in range(WEIGHT_CHUNKS)]
                 + [lambda q=q: g_in.relay_diagonal(q) for q in range(WEIGHT_CHUNKS)] + [g_mod.relay])
        steps = s // rt
        lead = steps // 4
        per_wait = max((steps - lead) // len(waits), 1)
        lax.fori_loop(0, lead, rope_rows, 0)
        done = lead
        for wait in waits:
            wait()
            nxt = min(done + per_wait, steps)
            lax.fori_loop(done, nxt, rope_rows, 0)
            done = nxt
        lax.fori_loop(done, steps, rope_rows, 0)
        g_in.finish()
        g_mod.finish()
        g_edge.finish()
        pltpu.make_async_copy(cos_ref, cos_hbm, table_sems.at[0]).wait()
        pltpu.make_async_copy(sin_ref, sin_hbm, table_sems.at[1]).wait()

    vm = pl.BlockSpec(memory_space=pltpu.VMEM)
    hbm = pl.BlockSpec(memory_space=pl.ANY)
    half_lanes = win_window.shape[1]
    return pl.pallas_call(
        body, name="prologue",
        out_shape=[jax.ShapeDtypeStruct((8,) + cw.shape, F32), jax.ShapeDtypeStruct((8, 8, w_ada.shape[1]), F32),
                   jax.ShapeDtypeStruct((4 * n_in, 2 * half_lanes), win_window.dtype),
                   jax.ShapeDtypeStruct((4, 2, W_TILE, 2 * half_lanes), win_window.dtype),
                   jax.ShapeDtypeStruct((s, LANES), F32), jax.ShapeDtypeStruct((s, LANES), F32)],
        in_specs=[vm, hbm, vm, hbm, vm, hbm, vm], out_specs=[vm, vm, hbm, vm, hbm, hbm],
        scratch_shapes=[pltpu.VMEM((8, w_ada.shape[1]), F32), pltpu.VMEM((s, LANES), F32), pltpu.VMEM((s, LANES), F32),
                        pltpu.VMEM(w_ada.shape, F32), pltpu.VMEM(pos_col.shape, jnp.int32),
                        pltpu.SemaphoreType.DMA((2,)), pltpu.SemaphoreType.DMA((2,))]
        + _GATHER_SEMS + _gather_sems(WEIGHT_CHUNKS) + _GATHER_SEMS + _GATHER_SEMS,
        compiler_params=pltpu.CompilerParams(vmem_limit_bytes=VMEM_LIMIT),
    )(cw, w_ada, b_shard, win_window, win_edges, pos_col, inv_freq)


def _reduce_scratch(rr, cc):
    c2 = cc // 2
    return [pltpu.VMEM((4, rr, c2), F32), pltpu.VMEM((4, rr, c2), F32), pltpu.VMEM((3, rr, c2), BF),
            pltpu.VMEM((2, rr, c2), BF), pltpu.VMEM((rr, c2), BF), pltpu.VMEM((rr, c2), F32),
            pltpu.SemaphoreType.DMA((8 + 3 * WEIGHT_CHUNKS,)), pltpu.SemaphoreType.DMA((8 + 3 * WEIGHT_CHUNKS,)),
            pltpu.SemaphoreType.DMA((5,))]


class _Reduce:
    def __init__(self, p_hbm, out_ref, acc_ref, own_ref, send_ref, land_ref, relay_ref, res_ref,
                 send_sems, recv_sems, local_sems, rows=None):
        x, y, c = lax.axis_index("x"), lax.axis_index("y"), lax.axis_index("c")
        part = (lambda j, ln: p_hbm.at[j, :, ln]) if rows is None else (lambda j, ln: p_hbm.at[rows(j), ln])
        c2 = out_ref.shape[1] // 2
        sibling = (x, y, 1 - c)
        first = (lax.rem(x + 1 - c, 2), lax.rem(y + c, 2))
        second = (lax.rem(x + c, 2), lax.rem(y + 1 - c, 2))
        shards = [2 * first[0] + first[1], 2 * second[0] + second[1], 2 * (1 - x) + (1 - y), 2 * x + y]
        sibling_slot = (1, 0, 2, 3)
        mine = pl.ds(pl.multiple_of(c * c2, c2), c2)
        other = pl.ds(pl.multiple_of((1 - c) * c2, c2), c2)
        self.acc_ref, self.own_ref, self.send_ref, self.land_ref = acc_ref, own_ref, send_ref, land_ref
        self.relay_ref, self.res_ref = relay_ref, res_ref
        self.own = [pltpu.make_async_copy(part(j, mine), own_ref.at[k], local_sems.at[k])
                    for k, j in enumerate(shards)]
        self.swap_out = [pltpu.make_async_remote_copy(
            src_ref=part(j, other), dst_ref=acc_ref.at[sibling_slot[k]], send_sem=send_sems.at[k],
            recv_sem=recv_sems.at[sibling_slot[k]], device_id=sibling, device_id_type=MESH) for k, j in enumerate(shards)]
        self.swap_in = [pltpu.make_async_remote_copy(
            src_ref=part(j, other), dst_ref=acc_ref.at[k], send_sem=send_sems.at[k], recv_sem=recv_sems.at[k],
            device_id=sibling, device_id_type=MESH) for k, j in enumerate(shards)]

        self.lanes = [slice(q * (c2 // WEIGHT_CHUNKS), (q + 1) * (c2 // WEIGHT_CHUNKS)) for q in range(WEIGHT_CHUNKS)]

        def message(m, src, dst, to):
            return [pltpu.make_async_remote_copy(
                src_ref=src.at[:, ln], dst_ref=dst.at[:, ln], send_sem=send_sems.at[8 + m * WEIGHT_CHUNKS + q],
                recv_sem=recv_sems.at[8 + m * WEIGHT_CHUNKS + q], device_id=(*to, c), device_id_type=MESH)
                for q, ln in enumerate(self.lanes)]

        self.direct = message(0, send_ref.at[0], land_ref.at[0], first)
        self.passed = message(1, send_ref.at[1], relay_ref, first)
        self.joint = message(2, send_ref.at[2], land_ref.at[1], second)
        self.put = pltpu.make_async_copy(res_ref, out_ref.at[:, mine], local_sems.at[4])
        self.share = pltpu.make_async_remote_copy(
            src_ref=res_ref, dst_ref=out_ref.at[:, mine], send_sem=send_sems.at[7],
            recv_sem=recv_sems.at[7], device_id=sibling, device_id_type=MESH)

    def start(self):
        for k in (2, 0, 1, 3):
            self.own[k].start()
            self.swap_out[k].start()

    def _combine(self, k):
        self.own[k].wait()
        self.swap_out[k].wait_send()
        self.swap_in[k].wait_recv()
        self.acc_ref[k] = self.acc_ref[k] + self.own_ref[k]

    def combine_and_send(self):
        dt = self.send_ref.dtype
        self._combine(2)
        self.send_ref[1] = self.acc_ref[2].astype(dt)
        for cp in self.passed:
            cp.start()
        self._combine(0)
        self.send_ref[0] = self.acc_ref[0].astype(dt)
        for cp in self.direct:
            cp.start()
        self._combine(1)
        self._combine(3)

    def send_joint(self):
        dt = self.send_ref.dtype
        for q, ln in enumerate(self.lanes):
            self.passed[q].wait_recv()
            self.send_ref[2, :, ln] = (self.acc_ref[1, :, ln] + self.relay_ref[:, ln].astype(F32)).astype(dt)
            self.joint[q].start()

    def total_and_share(self):
        for cp in self.direct + self.joint:
            cp.wait_recv()
        self.res_ref[...] = self.acc_ref[3] + self.land_ref[0].astype(F32) + self.land_ref[1].astype(F32)
        for cp in self.direct + self.passed + self.joint:
            cp.wait_send()
        self.put.start()
        self.share.start()

    def finish(self):
        self.put.wait()
        self.share.wait()


def _shard_window(n):
    return max(-(-(n * (j + 1)) // 8) * 8 - (n * j) // 8 * 8 for j in range(4))


def _epilogue(dw_in_t, small):
    cc = dw_in_t.shape[1]
    n = dw_in_t.shape[0] // 4
    r_in = _shard_window(n)
    n_red = len(_reduce_scratch(r_in, cc))

    def body(pin_hbm, small_ref, gin_ref, small_all_ref, *scratch):
        red_in = _Reduce(pin_hbm, gin_ref, *scratch[0:n_red],
                         rows=lambda j: pl.ds(pl.multiple_of((n * j) // 8 * 8, 8), r_in))
        gat = _Gather(small_ref, small_all_ref, *scratch[n_red:])
        red_in.start()
        gat.start()
        gat.relay()
        red_in.combine_and_send()
        gat.finish()
        red_in.send_joint()
        red_in.total_and_share()
        red_in.finish()

    vm = pl.BlockSpec(memory_space=pltpu.VMEM)
    anyspec = pl.BlockSpec(memory_space=pl.ANY)
    return pl.pallas_call(
        body, name="epilogue",
        out_shape=[jax.ShapeDtypeStruct((r_in, cc), F32), jax.ShapeDtypeStruct((8,) + small.shape, F32)],
        in_specs=[anyspec, vm], out_specs=[anyspec, vm],
        scratch_shapes=_reduce_scratch(r_in, cc) + _GATHER_SEMS,
        compiler_params=pltpu.CompilerParams(vmem_limit_bytes=VMEM_LIMIT),
    )(dw_in_t, small)


def _rope(t, cosb, sinb, first_half):
    partner = jnp.where(first_half, pltpu.roll(t, 96, 1), pltpu.roll(t, 32, 1))
    return t * cosb + partner * sinb


def _rope_t(g, cosb, sinb, first_half):
    gs = g * sinb
    partner = jnp.where(first_half, pltpu.roll(gs, 96, 1), pltpu.roll(gs, 32, 1))
    return g * cosb + partner


def _modnorm(x, g, sc1p, shift):
    r = lax.rsqrt(jnp.mean(x * x, axis=-1, keepdims=True) + RMS_EPS)
    xn = x * r
    return xn, r, (xn * g) * sc1p + shift


W_TILE = 16


def _w_window(n):
    return max(-(-(n * (j + 1)) // W_TILE) * W_TILE - (n * j) // W_TILE * W_TILE for j in range(4))


def _w_load_plan(n):
    starts = [(n * j) // W_TILE * W_TILE for j in range(4)]
    edges = starts + [starts[3] + _w_window(n) - W_TILE]
    assert all(starts[j] + _w_window(n) - W_TILE == edges[j + 1] for j in range(4))
    pieces, tiles = [], {}
    for pad, ref, rows in _UNPAD_ROWS:
        lo = ref
        for k, e in enumerate(edges):
            if ref <= e < ref + rows:
                if e > lo:
                    pieces.append((pad + lo - ref, lo, e - lo))
                tiles[k] = pad + e - ref
                lo = e + W_TILE
        if ref + rows > lo:
            pieces.append((pad + lo - ref, lo, ref + rows - lo))
    return pieces, tiles


def _load_w_padded(w_hbm, edge_ref, w_vm, sems):
    n = w_hbm.shape[0] // 4
    pieces, tiles = _w_load_plan(n)
    copies = [pltpu.make_async_copy(w_hbm.at[ref:ref + rows], w_vm.at[pad:pad + rows], sems.at[k])
              for k, (pad, ref, rows) in enumerate(pieces)]
    for cp in copies:
        cp.start()
    w_vm[OFF_GA + GLA_RANK:, :] = jnp.zeros((D_PAD - OFF_GA - GLA_RANK, D_MODEL), w_vm.dtype)
    row = lax.broadcasted_iota(jnp.int32, (W_TILE, w_vm.shape[1]), 0)
    for k, pad in tiles.items():
        last = edge_ref[max(k - 1, 0), 1].astype(F32)
        first = edge_ref[min(k, 3), 0].astype(F32)
        cut = W_TILE if k == 4 else (n * k) % W_TILE
        w_vm[pad:pad + W_TILE, :] = jnp.where(row < cut, last, first).astype(w_vm.dtype)
    return copies


def _inproj_fwd(x2d, shift, sc1p, g_norm, w_t, w_edges):
    s = x2d.shape[0]
    tm = min(1024, s)

    def body(x_ref, sh_ref, sc_ref, g_ref, w_hbm, edge_ref, o_ref, w_vm, sems):
        @pl.when(pl.program_id(0) == 0)
        def _():
            for cp in _load_w_padded(w_hbm, edge_ref, w_vm, sems):
                cp.wait()

        subs = _subtiles(tm)
        hs = [_modnorm(x_ref[sl, :], g_ref[...], sc_ref[...], sh_ref[...])[2].astype(BF) for sl in subs]
        for sl, h in zip(subs, hs):
            o_ref[sl, :] = _dot(h, w_vm[...], NT)

    vec = _full((1, D_MODEL))
    return pl.pallas_call(
        body, name="inproj_fwd", grid=(s // tm,),
        in_specs=[pl.BlockSpec((tm, D_MODEL), lambda i: (i, 0)), vec, vec, vec, pl.BlockSpec(memory_space=pl.ANY),
                  _full(w_edges.shape)],
        out_specs=pl.BlockSpec((tm, D_PAD), lambda i: (i, 0)),
        out_shape=jax.ShapeDtypeStruct((s, D_PAD), F32),
        scratch_shapes=[pltpu.VMEM((D_PAD, D_MODEL), BF),
                        pltpu.SemaphoreType.DMA((len(_w_load_plan(w_t.shape[0] // 4)[0]),))],
        compiler_params=_params(("arbitrary",)),
    )(x2d, shift, sc1p, g_norm, w_t, w_edges)


def _split3(a):
    hi = a.astype(BF)
    r1 = a - hi.astype(F32)
    mid = r1.astype(BF)
    lo = (r1 - mid.astype(F32)).astype(BF)
    return hi, mid, lo


def _tri_matmul(tri, a):
    hi, mid, lo = _split3(a)
    return _dot(tri, hi) + _dot(tri, mid) + _dot(tri, lo)


def _chunks(tb):
    return [slice(c * GLA_CHUNK, (c + 1) * GLA_CHUNK) for c in range(tb // GLA_CHUNK)]


def _per_chunk_rows(rows, width):
    return jnp.concatenate([jnp.broadcast_to(r, (GLA_CHUNK, width)) for r in rows], axis=0)


def _gla_triangle(tb):
    row = lax.broadcasted_iota(jnp.int32, (tb, tb), 0)
    col = lax.broadcasted_iota(jnp.int32, (tb, tb), 1)
    return (((row // GLA_CHUNK) == (col // GLA_CHUNK)) & (col <= row)).astype(F32)


def _lane_mean(x, ones_b):
    hi = x.astype(BF)
    lo = (x - hi.astype(F32)).astype(BF)
    return (_dot(hi, ones_b) + _dot(lo, ones_b)) * (1.0 / LANES)


def _head(t, h, lo_h):
    blk = t[:, LANES * (h // 2):LANES * (h // 2 + 1)]
    return jnp.where(lo_h, blk, 0.0) if h % 2 == 0 else jnp.where(lo_h, 0.0, blk)


def _gla_block_common(qk, ga, wd, bd, tril_b):
    tb = qk.shape[0]
    q, k = qk[:, :256], qk[:, 256:]
    z = _dot(ga.astype(BF), wd) + bd
    la = (jnp.minimum(z, 0.0) - jnp.log(1.0 + jnp.exp(-jnp.abs(z)))) * (1.0 / GLA_TAU)
    b = _tri_matmul(tril_b, la)
    bls = [b[rs.stop - 1:rs.stop, :] for rs in _chunks(tb)]
    eq = jnp.exp(b)
    ek = jnp.exp(-b)
    f = jnp.exp(_per_chunk_rows(bls, 256) - b)
    return z, eq, ek, f, q * (eq * GLA_DK ** -0.5), k * ek, k * f, bls


def _gla_units(s, rows):
    sub = min(GLA_SUB, s)
    tb = min(rows, s)
    subs = [slice(i * sub, (i + 1) * sub) for i in range(tb // sub)]
    units = [(i, h) for i in range(len(subs)) for h in range(GLA_HEADS)]
    return tb, sub, subs, units


def _gla_fwd(proj, wdecp, bdec, ggla):
    s = proj.shape[0]
    tb, sub, subs, units = _gla_units(s, GLA_ROWS_FWD)
    nch = sub // GLA_CHUNK

    def body(qk_ref, v_ref, gz_ref, ga_ref, wd_ref, bd_ref, gg_ref, tri_ref, og_ref, opre_ref, sprev_ref, st_ref):
        @pl.when(pl.program_id(0) == 0)
        def _():
            st_ref[...] = jnp.zeros_like(st_ref)

        lo_h = lax.broadcasted_iota(jnp.int32, (sub, LANES), 1) < GLA_DK
        tril = tri_ref[...] > 0.5
        tril_b = tri_ref[...].astype(BF)
        ones_b = jnp.ones((LANES, LANES), BF)
        gg, wd, bd = gg_ref[...], wd_ref[...], bd_ref[...]
        chunks = _chunks(sub)
        lanes = [slice(h * LANES, (h + 1) * LANES) for h in range(GLA_HEADS)]
        com = [_gla_block_common(qk_ref[sl, :], ga_ref[sl, :], wd, bd, tril_b) for sl in subs]
        decs = [[jnp.exp(bl) for bl in cm[7]] for cm in com]
        a = {(i, h): _head(com[i][4], h, lo_h).astype(BF) for i, h in units}
        bm = {(i, h): _head(com[i][5], h, lo_h).astype(BF) for i, h in units}
        ktl = {(i, h): _head(com[i][6], h, lo_h).astype(BF) for i, h in units}
        vh = {(i, h): v_ref[subs[i], lanes[h]].astype(BF) for i, h in units}
        sc = {u: _dot(a[u], bm[u], NT) for u in units}
        upd = {u: [_dot(vh[u][rs], ktl[u][rs], TN) for rs in chunks] for u in units}
        p = {u: jnp.where(tril, sc[u], 0.0).astype(BF) for u in units}
        o = {u: _dot(p[u], vh[u]) for u in units}
        states = {}
        for h in range(GLA_HEADS):
            st = st_ref[h]
            for i in range(len(subs)):
                entering = []
                for c in range(nch):
                    entering.append(st)
                    sprev_ref[i * nch + c, h] = st
                    st = st * decs[i][c][:, LANES * (h // 2):LANES * (h // 2 + 1)] + upd[(i, h)][c]
                states[(i, h)] = entering
            st_ref[h] = st
        inter = {u: [_dot(a[u][rs], states[u][c].astype(BF), NT) for c, rs in enumerate(chunks)] for u in units}
        o = {u: o[u] + jnp.concatenate(inter[u], axis=0) for u in units}
        ms = {u: _lane_mean(o[u] * o[u], ones_b) for u in units}
        for i, h in units:
            gzh = gz_ref[subs[i], lanes[h]]
            opre_ref[subs[i], lanes[h]] = o[(i, h)]
            og_ref[subs[i], lanes[h]] = (((o[(i, h)] * lax.rsqrt(ms[(i, h)] + RMS_EPS)) * gg[:, lanes[h]])
                                         * (gzh * _sigmoid(gzh))).astype(og_ref.dtype)

    def col(width, off):
        return pl.BlockSpec((tb, width), lambda i: (i, off // width))

    return pl.pallas_call(
        body, name="gla_fwd", grid=(s // tb,),
        in_specs=[col(512, OFF_QK), col(512, OFF_V), col(512, OFF_GZ), col(LANES, OFF_GA),
                  _full((LANES, 256)), _full((1, 256)), _full((1, 512)), _full((sub, sub))],
        out_specs=[pl.BlockSpec((tb, 512), lambda i: (i, 0)), pl.BlockSpec((tb, 512), lambda i: (i, 0)),
                   pl.BlockSpec((tb // GLA_CHUNK, GLA_HEADS, LANES, LANES), lambda i: (i, 0, 0, 0))],
        out_shape=[jax.ShapeDtypeStruct((s, 512), BF), jax.ShapeDtypeStruct((s, 512), F32),
                   jax.ShapeDtypeStruct((s // GLA_CHUNK, GLA_HEADS, LANES, LANES), F32)],
        scratch_shapes=[pltpu.VMEM((GLA_HEADS, LANES, LANES), F32)],
        compiler_params=_params(("arbitrary",)),
    )(proj, proj, proj, proj, wdecp, bdec, ggla, _gla_triangle(sub))


def _gla_bwd(proj, dog, opre, sprev, wdecp, bdec, ggla):
    s = proj.shape[0]
    tb, sub, subs, units = _gla_units(s, GLA_ROWS_BWD)
    nsub = len(subs)
    nch = sub // GLA_CHUNK
    nb = s // tb

    def body(qk_ref, v_ref, gz_ref, ga_ref, dog_ref, opre_ref, sprev_ref, wd_ref, bd_ref, gg_ref, tri_ref, triu_ref,
             dqk_ref, dv_ref, dgz_ref, dga_ref, dwd_ref, dbd_ref, dgg_ref, dst_ref):
        @pl.when(pl.program_id(0) == 0)
        def _():
            dst_ref[...] = jnp.zeros_like(dst_ref)
            dwd_ref[...] = jnp.zeros_like(dwd_ref)
            dbd_ref[...] = jnp.zeros_like(dbd_ref)
            dgg_ref[...] = jnp.zeros_like(dgg_ref)

        lo_h = lax.broadcasted_iota(jnp.int32, (sub, LANES), 1) < GLA_DK
        tril = tri_ref[...] > 0.5
        tril_b = tri_ref[...].astype(BF)
        triu_b = triu_ref[...].astype(BF)
        ones_b = jnp.ones((LANES, LANES), BF)
        last_row = (lax.broadcasted_iota(jnp.int32, (sub, LANES), 0) % GLA_CHUNK) == GLA_CHUNK - 1
        wd, gg, bd = wd_ref[...], gg_ref[...], bd_ref[...]
        chunks = _chunks(sub)
        lanes = [slice(h * LANES, (h + 1) * LANES) for h in range(GLA_HEADS)]
        blks = [slice(LANES * (h // 2), LANES * (h // 2 + 1)) for h in range(GLA_HEADS)]
        ga = [ga_ref[sl, :] for sl in subs]
        com = [_gla_block_common(qk_ref[sl, :], ga[i], wd, bd, tril_b) for i, sl in enumerate(subs)]
        decs = [[jnp.exp(bl) for bl in cm[7]] for cm in com]
        a = {(i, h): _head(com[i][4], h, lo_h).astype(BF) for i, h in units}
        bm = {(i, h): _head(com[i][5], h, lo_h).astype(BF) for i, h in units}
        ktl = {(i, h): _head(com[i][6], h, lo_h).astype(BF) for i, h in units}
        vh = {(i, h): v_ref[subs[i], lanes[h]].astype(BF) for i, h in units}
        sc = {u: _dot(a[u], bm[u], NT) for u in units}

        o = {(i, h): opre_ref[subs[i], lanes[h]] for i, h in units}
        ms = {u: _lane_mean(o[u] * o[u], ones_b) for u in units}
        gz = {(i, h): gz_ref[subs[i], lanes[h]] for i, h in units}
        dog = {(i, h): dog_ref[subs[i], lanes[h]] for i, h in units}
        sg = {u: _sigmoid(gz[u]) for u in units}
        r = {u: lax.rsqrt(ms[u] + RMS_EPS) for u in units}
        ohat = {u: o[u] * r[u] for u in units}
        sil = {u: gz[u] * sg[u] for u in units}
        for i, h in units:
            u = (i, h)
            dgz_ref[subs[i], lanes[h]] = (dog[u] * (ohat[u] * gg[:, lanes[h]])
                                          * (sg[u] * (1.0 + gz[u] * (1.0 - sg[u])))).astype(dgz_ref.dtype)
            dgg_ref[:, lanes[h]] += jnp.sum(dog[u] * sil[u] * ohat[u], axis=0, keepdims=True)
        dn = {(i, h): dog[(i, h)] * sil[(i, h)] * gg[:, lanes[h]] for i, h in units}
        mdn = {u: _lane_mean(dn[u] * ohat[u], ones_b) for u in units}
        do = {u: (r[u] * (dn[u] - ohat[u] * mdn[u])).astype(BF) for u in units}

        p = {u: jnp.where(tril, sc[u], 0.0).astype(BF) for u in units}
        dpr = {u: _dot(do[u], vh[u], NT) for u in units}
        incr = {u: [_dot(do[u][rs], a[u][rs], TN) for rs in chunks] for u in units}
        dv = {u: _dot(p[u], do[u], TN) for u in units}
        dp = {u: jnp.where(tril, dpr[u], 0.0).astype(BF) for u in units}
        dqd = {u: _dot(dp[u], bm[u]) for u in units}
        dkd = {u: _dot(dp[u], a[u], TN) for u in units}
        st = {(i, h): [sprev_ref[i * nch + c, h] for c in range(nch)] for i, h in units}
        leaving = {}
        for h in range(GLA_HEADS):
            d = dst_ref[h]
            for i in reversed(range(nsub)):
                out = [None] * nch
                for c in reversed(range(nch)):
                    out[c] = d
                    d = d * decs[i][c][:, blks[h]] + incr[(i, h)][c]
                leaving[(i, h)] = out
            dst_ref[h] = d
        lv_b = {u: [leaving[u][c].astype(BF) for c in range(nch)] for u in units}
        dv_s = {u: [_dot(ktl[u][rs], lv_b[u][c], NT) for c, rs in enumerate(chunks)] for u in units}
        dqd_s = {u: [_dot(do[u][rs], st[u][c].astype(BF)) for c, rs in enumerate(chunks)] for u in units}
        dkt_s = {u: [_dot(vh[u][rs], lv_b[u][c]) for c, rs in enumerate(chunks)] for u in units}
        ddec = {u: [jnp.sum(leaving[u][c] * st[u][c], axis=0, keepdims=True) for c in range(nch)] for u in units}
        for i, h in units:
            dv_ref[subs[i], lanes[h]] = (dv[(i, h)] + jnp.concatenate(dv_s[(i, h)], axis=0)).astype(dv_ref.dtype)
        dqd = {u: dqd[u] + jnp.concatenate(dqd_s[u], axis=0) for u in units}
        dkt = {u: jnp.concatenate(dkt_s[u], axis=0) for u in units}

        db = []
        for i, sl in enumerate(subs):
            _, eq, ek, f, qd, kd, kt, _ = com[i]
            parts = []
            for pair in range(GLA_HEADS // 2):
                blk, u0, u1 = blks[2 * pair], (i, 2 * pair), (i, 2 * pair + 1)
                dqd_b, dkd_b, dkt_b = dqd[u0] + dqd[u1], dkd[u0] + dkd[u1], dkt[u0] + dkt[u1]
                dqk_ref[sl, blk] = (dqd_b * (eq[:, blk] * GLA_DK ** -0.5)).astype(dqk_ref.dtype)
                dqk_ref[sl, 256 + LANES * pair:256 + LANES * (pair + 1)] = (dkd_b * ek[:, blk] + dkt_b * f[:, blk]).astype(dqk_ref.dtype)
                dkt_kt = dkt_b * kt[:, blk]
                dbp = dqd_b * qd[:, blk] - dkd_b * kd[:, blk] - dkt_kt
                dbl = [jnp.sum(dkt_kt[rs], axis=0, keepdims=True) + (ddec[u0][c] + ddec[u1][c]) * decs[i][c][:, blk]
                       for c, rs in enumerate(chunks)]
                parts.append(jnp.where(last_row, dbp + _per_chunk_rows(dbl, LANES), dbp))
            db.append(jnp.concatenate(parts, axis=1))
        dla = [_tri_matmul(triu_b, db[i]) for i in range(nsub)]
        dz32 = [dla[i] * (1.0 / GLA_TAU) * _sigmoid(-com[i][0]) for i in range(nsub)]
        dz = [t.astype(BF) for t in dz32]
        for i, sl in enumerate(subs):
            dga_ref[sl, :] = _dot(dz[i], wd, NT).astype(dga_ref.dtype)
            dwd_ref[...] += _dot(ga[i].astype(BF), dz[i], TN)
            dbd_ref[...] += jnp.sum(dz32[i], axis=0, keepdims=True)

    def col(width, off):
        return pl.BlockSpec((tb, width), lambda i: (nb - 1 - i, off // width))

    def rev(width):
        return pl.BlockSpec((tb, width), lambda i: (nb - 1 - i, 0))

    return pl.pallas_call(
        body, name="gla_bwd", grid=(nb,),
        in_specs=[col(512, OFF_QK), col(512, OFF_V), col(512, OFF_GZ), col(LANES, OFF_GA), rev(512), rev(512),
                  pl.BlockSpec((tb // GLA_CHUNK, GLA_HEADS, LANES, LANES), lambda i: (nb - 1 - i, 0, 0, 0)),
                  _full((LANES, 256)), _full((1, 256)), _full((1, 512)), _full((sub, sub)), _full((sub, sub))],
        out_specs=[rev(512), rev(512), rev(512), rev(LANES), _full((LANES, 256)), _full((1, 256)), _full((1, 512))],
        out_shape=[jax.ShapeDtypeStruct((s, 512), BF), jax.ShapeDtypeStruct((s, 512), BF),
                   jax.ShapeDtypeStruct((s, 512), BF), jax.ShapeDtypeStruct((s, LANES), BF),
                   jax.ShapeDtypeStruct((LANES, 256), F32), jax.ShapeDtypeStruct((1, 256), F32),
                   jax.ShapeDtypeStruct((1, 512), F32)],
        scratch_shapes=[pltpu.VMEM((GLA_HEADS, LANES, LANES), F32)],
        compiler_params=_params(("arbitrary",)),
    )(proj, proj, proj, proj, dog, opre, sprev, wdecp, bdec, ggla, _gla_triangle(sub), _gla_triangle(sub).T)


_SWA_COL_HEADS = (0, 2, 1, 3, 4, 6, 5, 7)
_SWA_COLS = SWA_HEADS * SWA_BLOCK


def _swa_masks():
    lo2 = lax.broadcasted_iota(jnp.int32, (2 * SWA_BLOCK, LANES), 1) < 64
    lane1 = lax.broadcasted_iota(jnp.int32, (SWA_BLOCK, LANES), 1)
    first_half = (lane1 % 64) < 32
    key = lax.broadcasted_iota(jnp.int32, (SWA_BLOCK, _SWA_COLS), 0)
    query = lax.broadcasted_iota(jnp.int32, (SWA_BLOCK, _SWA_COLS), 1) % SWA_BLOCK
    return lo2, lane1 < 64, first_half, key > query


def _merge_band(t, prev_mask, prev_bias=None):
    prev = t[:SWA_BLOCK] if prev_bias is None else t[:SWA_BLOCK] + prev_bias
    return jnp.where(prev_mask, prev, t[SWA_BLOCK:])


def _split_band(t, prev_mask_b):
    prev = t * prev_mask_b
    return jnp.concatenate([prev, t - prev], axis=0)


def _kv_variants(t, lo2):
    tr = pltpu.roll(t, 64, 1)
    lo_v = [jnp.where(lo2, t, 0.0).astype(BF), jnp.where(lo2, tr, 0.0).astype(BF)]
    hi_v = [jnp.where(lo2, 0.0, tr).astype(BF), jnp.where(lo2, 0.0, t).astype(BF)]
    return lo_v, hi_v


def _kv_variants_t(t):
    tt = t.T
    sw = jnp.concatenate([tt[64:], tt[:64]], axis=0)
    top = lax.broadcasted_iota(jnp.int32, tt.shape, 0) < 64
    lo_v = [jnp.where(top, tt, 0.0).astype(BF), jnp.where(top, sw, 0.0).astype(BF)]
    hi_v = [jnp.where(top, 0.0, sw).astype(BF), jnp.where(top, 0.0, tt).astype(BF)]
    return lo_v, hi_v


def _swa_scores(qg, k_lo, k_hi):
    return jnp.concatenate([_dot(k_lo[0], qg[0], NT), _dot(k_hi[0], qg[0], NT),
                            _dot(k_lo[1], qg[1], NT), _dot(k_hi[1], qg[1], NT)], axis=1)


def _sink_row(sinks_ref):
    return jnp.concatenate([jnp.full((1, SWA_BLOCK), sinks_ref[0, hd], F32) for hd in _SWA_COL_HEADS], axis=1)


def _swa_softmax(st, prev_mask, prev_bias, sink):
    st = _merge_band(st, prev_mask, prev_bias)
    m = jnp.maximum(jnp.max(st, axis=0, keepdims=True), sink)
    ex = jnp.exp(st - m)
    es = jnp.exp(sink - m)
    inv = 1.0 / (jnp.sum(ex, axis=0, keepdims=True) + es)
    return ex, es, inv


def _no_prev_bias(block_index):
    return jnp.where(block_index > 0, 0.0, -1e30).astype(F32)


def _swa_queries(sq_ref, rows, cosb, sinb, first_half):
    qs = [_rope(sq_ref[rows, p * LANES:(p + 1) * LANES], cosb, sinb, first_half) * 0.125 for p in range(4)]
    return [jnp.concatenate(qs[0:2], axis=0), jnp.concatenate(qs[2:4], axis=0)]


def _phase_steps(nsteps, phases):
    return [min(nsteps - 1, (k * nsteps) // phases) for k in range(phases - 1)] + [nsteps - 1]


def _swa_fwd(proj, cos, sin, sinks, half_out):
    s = proj.shape[0]
    nq = min(SWA_QBLOCKS_FWD, s // SWA_BLOCK)
    tq = nq * SWA_BLOCK
    steps = _phase_steps(s // tq, 4)

    def body(sq_ref, sz_ref, sk_ref, sv_ref, cos_ref, sin_ref, sinks_ref, hout_hbm, os_ref, opre_ref, wout_hbm,
             kprev, vprev, *gather_sems):
        n = pl.program_id(0)

        @pl.when(n == 0)
        def _():
            kprev[...] = jnp.zeros_like(kprev)
            vprev[...] = jnp.zeros_like(vprev)

        gather = _Gather(hout_hbm, wout_hbm, *gather_sems, chunks=WEIGHT_CHUNKS)
        for step, phase in zip(steps, (gather.start, gather.pass_on, gather.relay_diagonal, gather.finish)):
            pl.when(n == step)(phase)

        lo2, _, first_half, prev_mask = _swa_masks()
        prev_mask_b = jnp.where(prev_mask, 1.0, 0.0).astype(BF)
        sink = _sink_row(sinks_ref)
        blocks = range(nq)
        rows = [slice(j * SWA_BLOCK, (j + 1) * SWA_BLOCK) for j in blocks]
        cosb = [cos_ref[rows[j], :] for j in blocks]
        sinb = [sin_ref[rows[j], :] for j in blocks]
        kc = [_rope(sk_ref[rows[j], :], cosb[j], sinb[j], first_half) for j in blocks]
        vc = [sv_ref[rows[j], :] for j in blocks]
        kcat = [jnp.concatenate([kprev[...] if j == 0 else kc[j - 1], kc[j]], axis=0) for j in blocks]
        vcat = [jnp.concatenate([vprev[...] if j == 0 else vc[j - 1], vc[j]], axis=0) for j in blocks]
        kprev[...] = kc[-1]
        vprev[...] = vc[-1]
        kvar = [_kv_variants(kcat[j], lo2) for j in blocks]
        vtvar = [_kv_variants_t(vcat[j]) for j in blocks]
        qg = [[q.astype(BF) for q in _swa_queries(sq_ref, rows[j], cosb[j], sinb[j], first_half)] for j in blocks]
        st = [_swa_scores(qg[j], *kvar[j]) for j in blocks]
        soft = [_swa_softmax(st[j], prev_mask, _no_prev_bias(n) if j == 0 else None, sink) for j in blocks]
        pt = [_split_band(soft[j][0].astype(BF), prev_mask_b) for j in blocks]
        og = {}
        for j in blocks:
            inv = soft[j][2]
            for g in range(2):
                c0, c1, c2 = 512 * g, 512 * g + 256, 512 * g + 512
                ot = (_dot(vtvar[j][0][g], pt[j][:, c0:c1]) * inv[:, c0:c1]
                      + _dot(vtvar[j][1][g], pt[j][:, c1:c2]) * inv[:, c1:c2])
                og[(j, g)] = ot.T
        for j in blocks:
            for g in range(2):
                for i in range(2):
                    ls = slice((2 * g + i) * LANES, (2 * g + i + 1) * LANES)
                    o = og[(j, g)][i * SWA_BLOCK:(i + 1) * SWA_BLOCK]
                    sz = sz_ref[rows[j], ls]
                    opre_ref[rows[j], ls] = o
                    os_ref[rows[j], ls] = (o * (sz * _sigmoid(sz))).astype(os_ref.dtype)

    def col(width, off):
        return pl.BlockSpec((tq, width), lambda i: (i, off // width))

    row = pl.BlockSpec((tq, LANES), lambda i: (i, 0))
    return pl.pallas_call(
        body, name="swa_fwd", grid=(s // tq,),
        in_specs=[col(512, OFF_SQ), col(512, OFF_SZ), col(LANES, OFF_SK), col(LANES, OFF_SV), row, row,
                  pl.BlockSpec(memory_space=pltpu.SMEM), pl.BlockSpec(memory_space=pl.ANY)],
        out_specs=[pl.BlockSpec((tq, 512), lambda i: (i, 0))] * 2 + [pl.BlockSpec(memory_space=pl.ANY)],
        out_shape=[jax.ShapeDtypeStruct((s, 512), BF), jax.ShapeDtypeStruct((s, 512), F32),
                   jax.ShapeDtypeStruct((8,) + half_out.shape, half_out.dtype)],
        scratch_shapes=[pltpu.VMEM((SWA_BLOCK, LANES), F32)] * 2 + _gather_sems(WEIGHT_CHUNKS),
        compiler_params=_params(("arbitrary",)),
    )(proj, proj, proj, proj, cos, sin, sinks, half_out)


def _swa_bwd(proj, dos, opre, cos, sin, sinks, dw_out_parts):
    s = proj.shape[0]
    nq = min(SWA_QBLOCKS, s // SWA_BLOCK)
    tq = nq * SWA_BLOCK
    steps = _phase_steps(s // tq, 5)
    _, r_out, c_out = dw_out_parts.shape

    def body(sq_ref, sz_ref, sk_ref, sv_ref, dos_ref, opre_ref, cos_ref, sin_ref, sinks_ref, pout_hbm,
             dsq_ref, dsz_ref, dsk_ref, dsv_ref, dsink_ref, gout_hbm, kprev, vprev, cprev, sprev, *reduce_scratch):
        n = pl.program_id(0)

        @pl.when(n == 0)
        def _():
            kprev[...] = jnp.zeros_like(kprev)
            vprev[...] = jnp.zeros_like(vprev)
            cprev[...] = jnp.zeros_like(cprev)
            sprev[...] = jnp.zeros_like(sprev)
            for hd in range(SWA_HEADS):
                dsink_ref[0, hd] = 0.0

        reduce = _Reduce(pout_hbm, gout_hbm, *reduce_scratch)
        phases = (reduce.start, reduce.combine_and_send, reduce.send_joint, reduce.total_and_share, reduce.finish)
        for step, phase in zip(steps, phases):
            pl.when(n == step)(phase)

        lo2, lo1, first_half, prev_mask = _swa_masks()
        prev_mask_b = jnp.where(prev_mask, 1.0, 0.0).astype(BF)
        lo1s = jnp.concatenate([lo1, lo1], axis=0)
        sink = _sink_row(sinks_ref)

        def home(m0, m1):
            t0 = m0 + pltpu.roll(m0, 64, 1)
            t1 = m1 + pltpu.roll(m1, 64, 1)
            return jnp.where(lo2, t0, t1)

        kp, vp, cp_, sp_ = kprev[...], vprev[...], cprev[...], sprev[...]
        for j in range(nq):
            rows = slice(j * SWA_BLOCK, (j + 1) * SWA_BLOCK)
            blk = n * nq + j
            cosb, sinb = cos_ref[rows, :], sin_ref[rows, :]
            kc = _rope(sk_ref[rows, :], cosb, sinb, first_half)
            vc = sv_ref[rows, :]
            kcat = jnp.concatenate([kp, kc], axis=0)
            k_lo, k_hi = _kv_variants(kcat, lo2)
            kt_lo, kt_hi = _kv_variants_t(kcat)
            v_lo, v_hi = _kv_variants(jnp.concatenate([vp, vc], axis=0), lo2)
            qg32 = _swa_queries(sq_ref, rows, cosb, sinb, first_half)
            qg = [q.astype(BF) for q in qg32]
            ex, es, inv = _swa_softmax(_swa_scores(qg, k_lo, k_hi), prev_mask, _no_prev_bias(n) if j == 0 else None, sink)
            pr, ps = ex * inv, es * inv

            dog32 = []
            for g in range(2):
                parts = []
                for i in range(2):
                    ls = slice((2 * g + i) * LANES, (2 * g + i + 1) * LANES)
                    sz = sz_ref[rows, ls]
                    sg = _sigmoid(sz)
                    dos_p = dos_ref[rows, ls]
                    dsz_ref[rows, ls] = (dos_p * opre_ref[rows, ls] * (sg * (1.0 + sz * (1.0 - sg)))).astype(dsz_ref.dtype)
                    parts.append(dos_p * (sz * sg))
                dog32.append(jnp.concatenate(parts, axis=0))
            dog = [t.astype(BF) for t in dog32]
            dpr = _merge_band(jnp.concatenate([_dot(v_lo[0], dog[0], NT), _dot(v_hi[0], dog[0], NT),
                                               _dot(v_lo[1], dog[1], NT), _dot(v_hi[1], dog[1], NT)], axis=1), prev_mask)
            rd = jnp.sum(pr * dpr, axis=0, keepdims=True)
            ds = _split_band((pr * (dpr - rd)).astype(BF), prev_mask_b)
            prb = _split_band(pr.astype(BF), prev_mask_b)
            sink_term = ps * rd
            for r, hd in enumerate(_SWA_COL_HEADS):
                dsink_ref[0, hd] += -jnp.sum(sink_term[:, r * SWA_BLOCK:(r + 1) * SWA_BLOCK])

            dk_g, dv_g = [], []
            for g in range(2):
                c0, c1, c2 = 512 * g, 512 * g + 256, 512 * g + 512
                dq = (_dot(kt_lo[g], ds[:, c0:c1]) + _dot(kt_hi[g], ds[:, c1:c2])).T
                for i in range(2):
                    ls = slice((2 * g + i) * LANES, (2 * g + i + 1) * LANES)
                    dsq_ref[rows, ls] = _rope_t(dq[i * SWA_BLOCK:(i + 1) * SWA_BLOCK] * 0.125, cosb, sinb,
                                                first_half).astype(dsq_ref.dtype)
                q_split = jnp.concatenate([jnp.where(lo1s, qg32[g], 0.0), jnp.where(lo1s, 0.0, qg32[g])], axis=0).astype(BF)
                do_split = jnp.concatenate([jnp.where(lo1s, dog32[g], 0.0), jnp.where(lo1s, 0.0, dog32[g])], axis=0).astype(BF)
                dk_g.append(_dot(ds[:, c0:c2], q_split))
                dv_g.append(_dot(prb[:, c0:c2], do_split))
            dk = home(dk_g[0], dk_g[1])
            dv = home(dv_g[0], dv_g[1])
            cur = pl.ds(pl.multiple_of(blk * SWA_BLOCK, SWA_BLOCK), SWA_BLOCK)
            dsk_ref[cur, :] = _rope_t(dk[SWA_BLOCK:], cosb, sinb, first_half)
            dsv_ref[cur, :] = dv[SWA_BLOCK:]
            dk_prev = _rope_t(dk[:SWA_BLOCK], cp_, sp_, first_half)
            dv_prev = dv[:SWA_BLOCK]
            if j == 0:
                @pl.when(n > 0)
                def _():
                    prv = pl.ds(pl.multiple_of((blk - 1) * SWA_BLOCK, SWA_BLOCK), SWA_BLOCK)
                    dsk_ref[prv, :] += dk_prev
                    dsv_ref[prv, :] += dv_prev
            else:
                prv = pl.ds(pl.multiple_of((blk - 1) * SWA_BLOCK, SWA_BLOCK), SWA_BLOCK)
                dsk_ref[prv, :] += dk_prev
                dsv_ref[prv, :] += dv_prev
            kp, vp, cp_, sp_ = kc, vc, cosb, sinb
        kprev[...] = kp
        vprev[...] = vp
        cprev[...] = cp_
        sprev[...] = sp_

    def col(width, off):
        return pl.BlockSpec((tq, width), lambda i: (i, off // width))

    row = pl.BlockSpec((tq, LANES), lambda i: (i, 0))
    wide = pl.BlockSpec((tq, 512), lambda i: (i, 0))
    return pl.pallas_call(
        body, name="swa_bwd", grid=(s // tq,),
        in_specs=[col(512, OFF_SQ), col(512, OFF_SZ), col(LANES, OFF_SK), col(LANES, OFF_SV), wide, wide, row, row,
                  pl.BlockSpec(memory_space=pltpu.SMEM), pl.BlockSpec(memory_space=pl.ANY)],
        out_specs=[wide, wide, _full((s, LANES)), _full((s, LANES)), pl.BlockSpec(memory_space=pltpu.SMEM),
                   pl.BlockSpec(memory_space=pl.ANY)],
        out_shape=[jax.ShapeDtypeStruct((s, 512), BF), jax.ShapeDtypeStruct((s, 512), BF),
                   jax.ShapeDtypeStruct((s, LANES), F32), jax.ShapeDtypeStruct((s, LANES), F32),
                   jax.ShapeDtypeStruct((1, SWA_HEADS), F32), jax.ShapeDtypeStruct((r_out, c_out), F32)],
        scratch_shapes=[pltpu.VMEM((SWA_BLOCK, LANES), F32)] * 4 + _reduce_scratch(r_out, c_out),
        compiler_params=_params(("arbitrary",)),
    )(proj, proj, proj, proj, dos, opre, cos, sin, sinks, dw_out_parts)


def _outproj(og, osw, w_out, x2d, target, gate, g_final):
    s = x2d.shape[0]
    tm = min(512, s)

    def body(og_ref, os_ref, w_ref, x_ref, t_ref, gate_ref, gf_ref,
             dx2_ref, dog_ref, dos_ref, dw_ref, loss_ref, dgf_ref, dgate_ref):
        @pl.when(pl.program_id(0) == 0)
        def _():
            dw_ref[...] = jnp.zeros_like(dw_ref)
            loss_ref[...] = jnp.zeros_like(loss_ref)
            dgf_ref[...] = jnp.zeros_like(dgf_ref)
            dgate_ref[...] = jnp.zeros_like(dgate_ref)

        w = w_ref[...]
        gate, gf = gate_ref[...], gf_ref[...]
        subs = _subtiles(tm)
        ogv = [og_ref[sl, :] for sl in subs]
        osv = [os_ref[sl, :] for sl in subs]
        y = [_dot(ogv[k], w[:512]) + _dot(osv[k], w[512:]) for k in range(len(subs))]
        dys = []
        for k, sl in enumerate(subs):
            x2 = x_ref[sl, :] + gate * y[k]
            r = lax.rsqrt(jnp.mean(x2 * x2, axis=-1, keepdims=True) + RMS_EPS)
            xn = x2 * r
            err = xn * gf - t_ref[sl, :]
            loss_ref[...] += 0.5 * jnp.sum(jnp.mean(err * err, axis=-1, keepdims=True), axis=0, keepdims=True)
            dyf = err * (1.0 / D_MODEL)
            dgf_ref[...] += jnp.sum(dyf * xn, axis=0, keepdims=True)
            t = dyf * gf
            dx2 = r * (t - xn * jnp.mean(t * xn, axis=-1, keepdims=True))
            dx2_ref[sl, :] = dx2
            dgate_ref[...] += jnp.sum(dx2 * y[k], axis=0, keepdims=True)
            dys.append((dx2 * gate).astype(BF))
            dmix = _dot(dys[k], w, NT)
            dog_ref[sl, :] = dmix[:, :512]
            dos_ref[sl, :] = dmix[:, 512:]
        dy = jnp.concatenate(dys, axis=0)
        dw_ref[:512, :] += _dot(og_ref[...], dy, TN)
        dw_ref[512:, :] += _dot(os_ref[...], dy, TN)

    half = pl.BlockSpec((tm, 512), lambda i: (i, 0))
    rowb = pl.BlockSpec((tm, D_MODEL), lambda i: (i, 0))
    vec = _full((1, D_MODEL))
    return pl.pallas_call(
        body, name="outproj", grid=(s // tm,),
        in_specs=[half, half, _full((D_MODEL, D_MODEL)), rowb, rowb, vec, vec],
        out_specs=[rowb, half, half, _full((D_MODEL, D_MODEL)), _full((1, 1)), vec, vec],
        out_shape=[jax.ShapeDtypeStruct((s, D_MODEL), F32), jax.ShapeDtypeStruct((s, 512), F32),
                   jax.ShapeDtypeStruct((s, 512), F32), jax.ShapeDtypeStruct((D_MODEL, D_MODEL), F32),
                   jax.ShapeDtypeStruct((1, 1), F32), jax.ShapeDtypeStruct((1, D_MODEL), F32),
                   jax.ShapeDtypeStruct((1, D_MODEL), F32)],
        compiler_params=_params(("arbitrary",)),
    )(og, osw, w_out, x2d, target, gate, g_final)


_PIECES = ((OFF_QK, 512), (OFF_V, 512), (OFF_GZ, 512), (OFF_SQ, 512), (OFF_SZ, 512),
           (OFF_SK, LANES), (OFF_SV, LANES), (OFF_GA, LANES))

_UNPAD_ROWS = ((OFF_QK, 0, 1024),
               (OFF_GA, 1024, GLA_RANK),
               (OFF_GZ, 1040, 1024),
               (OFF_SK, 2064, 256),
               (OFF_SZ, 2320, 512))


def _inproj_bwd(x2d, shift, sc1p, g_norm, w_t, w_edges, dx2, pieces):
    s = x2d.shape[0]
    tm = min(512, s)
    nsteps = s // tm

    def body(x_ref, sh_ref, sc_ref, g_ref, w_hbm, edge_ref, dx2_ref, *rest):
        piece_refs = rest[:len(_PIECES)]
        gx_ref, dw_hbm, dsh_ref, dsc_ref, dg_ref, w_vm, dw_vm, in_sems, out_sems = rest[len(_PIECES):]
        i = pl.program_id(0)

        @pl.when(i == 0)
        def _():
            loads = _load_w_padded(w_hbm, edge_ref, w_vm, in_sems)
            dw_vm[...] = jnp.zeros_like(dw_vm)
            dsh_ref[...] = jnp.zeros_like(dsh_ref)
            dsc_ref[...] = jnp.zeros_like(dsc_ref)
            dg_ref[...] = jnp.zeros_like(dg_ref)
            for cp in loads:
                cp.wait()

        g, sc1p_v, shift_v = g_ref[...], sc_ref[...], sh_ref[...]
        subs = _subtiles(tm)
        dhs = []
        for sl in subs:
            dh = None
            for (off, width), pr in zip(_PIECES, piece_refs):
                part = _dot(pr[sl, :].astype(BF), w_vm[off:off + width, :])
                dh = part if dh is None else dh + part
            dhs.append(dh)
        norm = [_modnorm(x_ref[sl, :], g, sc1p_v, shift_v) for sl in subs]
        hb = jnp.concatenate([h.astype(BF) for _, _, h in norm], axis=0)
        for (off, width), pr in zip(_PIECES, piece_refs):
            dw_vm[off:off + width, :] += _dot(pr[...].astype(BF), hb, TN)
        for sl, (xn, r, _), dh in zip(subs, norm, dhs):
            dsh_ref[...] += jnp.sum(dh, axis=0, keepdims=True)
            dsc_ref[...] += jnp.sum(dh * (xn * g), axis=0, keepdims=True)
            dg_ref[...] += jnp.sum(dh * xn * sc1p_v, axis=0, keepdims=True)
            dxn = dh * g * sc1p_v
            gx_ref[sl, :] = dx2_ref[sl, :] + r * (dxn - xn * jnp.mean(dxn * xn, axis=-1, keepdims=True))

        @pl.when(i == nsteps - 1)
        def _():
            copies = [pltpu.make_async_copy(dw_vm.at[src:src + n], dw_hbm.at[dst:dst + n], out_sems.at[k])
                      for k, (src, dst, n) in enumerate(_UNPAD_ROWS)]
            for cp in copies:
                cp.start()
            for cp in copies:
                cp.wait()

    rowb = pl.BlockSpec((tm, D_MODEL), lambda i: (i, 0))
    vec = _full((1, D_MODEL))
    anyspec = pl.BlockSpec(memory_space=pl.ANY)
    piece_specs = [pl.BlockSpec((tm, width), lambda i: (i, 0)) for _, width in _PIECES]
    return pl.pallas_call(
        body, name="inproj_bwd", grid=(nsteps,),
        in_specs=[rowb, vec, vec, vec, anyspec, _full(w_edges.shape), rowb] + piece_specs,
        out_specs=[rowb, anyspec, vec, vec, vec],
        out_shape=[jax.ShapeDtypeStruct((s, D_MODEL), F32), jax.ShapeDtypeStruct((D_IN, D_MODEL), F32),
                   jax.ShapeDtypeStruct((1, D_MODEL), F32), jax.ShapeDtypeStruct((1, D_MODEL), F32),
                   jax.ShapeDtypeStruct((1, D_MODEL), F32)],
        scratch_shapes=[pltpu.VMEM((D_PAD, D_MODEL), BF), pltpu.VMEM((D_PAD, D_MODEL), F32),
                        pltpu.SemaphoreType.DMA((len(_w_load_plan(w_t.shape[0] // 4)[0]),)),
                        pltpu.SemaphoreType.DMA((len(_UNPAD_ROWS),))],
        compiler_params=_params(("arbitrary",)),
    )(x2d, shift, sc1p, g_norm, w_t, w_edges, dx2, *pieces)


def _adam(w, g, m, v):
    m2 = ADAM_B1 * m + (1.0 - ADAM_B1) * g
    v2 = ADAM_B2 * v + (1.0 - ADAM_B2) * (g * g)
    m_hat = m2 / (1.0 - ADAM_B1 ** ADAM_STEP)
    v_hat = v2 / (1.0 - ADAM_B2 ** ADAM_STEP)
    delta = -ADAM_LR * (m_hat / (jnp.sqrt(v_hat) + ADAM_EPS) + ADAM_WD * w)
    return delta, m2, v2


def _adamw(w, g, m, v, name):
    rr, cc = w.shape
    tc = min(512, cc)

    def body(w_ref, g_ref, m_ref, v_ref, d_ref, m2_ref, v2_ref):
        d_ref[...], m2_ref[...], v2_ref[...] = _adam(w_ref[...], g_ref[...], m_ref[...], v_ref[...])

    blk = pl.BlockSpec((rr, tc), lambda i: (0, i))
    return pl.pallas_call(
        body, name=name, grid=(cc // tc,), in_specs=[blk] * 4, out_specs=[blk] * 3,
        out_shape=[jax.ShapeDtypeStruct((rr, cc), F32)] * 3,
        compiler_params=_params(("arbitrary",)),
    )(w, g, m, v)


def _adamw_t(w3, g_window, m3, v3, name):
    rr, _, cc = w3.shape
    parts = [slice(q * (cc // 4), (q + 1) * (cc // 4)) for q in range(4)]
    starts = sorted({(rr * j) % 8 for j in range(4)})

    def body(w_hbm, gw_hbm, m_hbm, v_hbm, d_hbm, m2_hbm, v2_hbm, g3_hbm,
             w_vm, m_vm, v_vm, gw_vm, d_vm, m2_vm, v2_vm, g_vm, in_sems, out_sems):
        start = lax.rem(rr * (2 * lax.axis_index("x") + lax.axis_index("y")), 8)
        ins = ((w_hbm, w_vm), (m_hbm, m_vm), (v_hbm, v_vm))
        outs = ((d_vm, d_hbm), (m2_vm, m2_hbm), (v2_vm, v2_hbm), (g_vm, g3_hbm))
        loads = [[pltpu.make_async_copy(src.at[:, 0, p], dst.at[:, p], in_sems.at[4 * q + k]) for k, (src, dst) in enumerate(ins)]
                 + [pltpu.make_async_copy(gw_hbm.at[:, p], gw_vm.at[:, p], in_sems.at[4 * q + 3])]
                 for q, p in enumerate(parts)]
        stores = [[pltpu.make_async_copy(src.at[:, p], dst.at[:, 0, p], out_sems.at[4 * q + k]) for k, (src, dst) in enumerate(outs)]
                  for q, p in enumerate(parts)]
        for group in loads:
            for cp in group:
                cp.start()
        for q, p in enumerate(parts):
            for cp in loads[q]:
                cp.wait()
            g = gw_vm[starts[0]:starts[0] + rr, p]
            for o in starts[1:]:
                g = jnp.where(start == o, gw_vm[o:o + rr, p], g)
            g_vm[:, p] = g
            d_vm[:, p], m2_vm[:, p], v2_vm[:, p] = _adam(w_vm[:, p], g, m_vm[:, p], v_vm[:, p])
            for cp in stores[q]:
                cp.start()
        for group in stores:
            for cp in group:
                cp.wait()

    hbm = pl.BlockSpec(memory_space=pl.ANY)
    return pl.pallas_call(
        body, name=name, grid=(1,), in_specs=[hbm] * 4,
        out_specs=[hbm] * 4, out_shape=[jax.ShapeDtypeStruct((rr, 1, cc), F32)] * 4,
        scratch_shapes=[pltpu.VMEM((rr, cc), F32)] * 3 + [pltpu.VMEM(g_window.shape, F32)] + [pltpu.VMEM((rr, cc), F32)] * 4
        + [pltpu.SemaphoreType.DMA((16,)), pltpu.SemaphoreType.DMA((16,))],
        compiler_params=_params(("arbitrary",)),
    )(w3, g_window, m3, v3)


def _ada_update(c_all, dmod_cols, w, m, v):
    rr, cc = w.shape
    tr = min(512, rr)
    c_all = jnp.pad(c_all, ((0, 8), (0, 0)))
    dmod_cols = jnp.pad(dmod_cols, ((0, 8), (0, 0)))

    def body(c_ref, dm_ref, w_ref, m_ref, v_ref, g_ref, d_ref, m2_ref, v2_ref):
        cv = c_ref[...]
        sc = (cv * _sigmoid(cv)).astype(BF)
        g = _dot(sc, dm_ref[...].astype(BF), TN)
        g_ref[...] = g
        d_ref[...], m2_ref[...], v2_ref[...] = _adam(w_ref[...], g, m_ref[...], v_ref[...])

    blk = pl.BlockSpec((tr, cc), lambda i: (i, 0))
    return pl.pallas_call(
        body, name="ada_update", grid=(rr // tr,),
        in_specs=[pl.BlockSpec((16, tr), lambda i: (0, i)), _full((16, cc)), blk, blk, blk],
        out_specs=[blk] * 4, out_shape=[jax.ShapeDtypeStruct((rr, cc), F32)] * 4,
        compiler_params=_params(("arbitrary",)),
    )(c_all, dmod_cols, w, m, v)


def _small_update(parts, weights, moms, vels):
    n = len(weights)

    def body(*refs):
        p_refs, w_refs, m_refs, v_refs = refs[:n + 1], refs[n + 1:2 * n + 1], refs[2 * n + 1:3 * n + 1], refs[3 * n + 1:4 * n + 1]
        outs = refs[4 * n + 1:]
        for i in range(n):
            g = p_refs[i][0]
            for d in range(1, 8):
                g = g + p_refs[i][d]
            delta, m2, v2 = _adam(w_refs[i][...], g, m_refs[i][...], v_refs[i][...])
            outs[4 * i][...] = g
            outs[4 * i + 1][...] = delta
            outs[4 * i + 2][...] = m2
            outs[4 * i + 3][...] = v2
        tot = p_refs[n][0]
        for d in range(1, 8):
            tot = tot + p_refs[n][d]
        outs[4 * n][...] = tot

    out_shape = []
    for w in weights:
        out_shape += [jax.ShapeDtypeStruct(w.shape, F32)] * 4
    out_shape.append(jax.ShapeDtypeStruct(parts[n].shape[1:], F32))
    return pl.pallas_call(body, name="small_update", out_shape=out_shape, compiler_params=_params())(
        *parts, *weights, *moms, *vels)


def _rows8(a):
    flat = a.reshape(-1)
    rows = -(-flat.shape[0] // LANES)
    rows8 = -(-rows // 8) * 8
    flat = jnp.pad(flat, (0, rows8 * LANES - flat.shape[0]))
    return flat.reshape(rows8, LANES)


def kernel(x, c, positions, w_ada, b_ada, g_norm, w_in, w_decay, b_decay, g_gla_head, sinks, w_out, g_final, loss_target, m_w_ada, m_b_ada, m_g_norm, m_w_in, m_w_decay, m_b_decay, m_g_gla_head, m_sinks, m_w_out, m_g_final, v_w_ada, v_b_ada, v_g_norm, v_w_in, v_w_decay, v_b_decay, v_g_gla_head, v_sinks, v_w_out, v_g_final):
    ax, ay, ac = lax.axis_index("x"), lax.axis_index("y"), lax.axis_index("c")
    chip = 2 * ax + ay
    dev = 2 * chip + ac
    s = x.shape[1]
    x2d = x[0]
    target = loss_target[0]
    w_ada2, w_out2, w_dec2 = w_ada[0], w_out[0], w_decay[0]
    w_in_t = w_in[0].T
    ada_cols = w_ada2.shape[1]
    in_cols = w_in_t.shape[0]
    out_rows = w_out2.shape[0]
    half = D_MODEL // 2

    cw = jnp.concatenate([c.reshape(8, LANES), w_dec2.reshape(8, LANES)], axis=0)
    b_shard = lax.dynamic_slice(b_ada, (0, chip * ada_cols), (1, ada_cols))
    half_in = lax.dynamic_slice(w_in_t, (0, ac * half), (in_cols, half)).astype(BF)
    half_out = lax.dynamic_slice(w_out2, (ac * (out_rows // 2), 0), (out_rows // 2, D_MODEL)).astype(BF)
    inv_freq = 1.0 / (ROPE_THETA ** (jnp.arange(0, 64, 2, dtype=F32) / 64))
    win_window = lax.dynamic_update_slice(jnp.zeros((_w_window(in_cols), half), BF), half_in, ((in_cols * chip) % W_TILE, 0))
    win_edges = jnp.stack([win_window[:W_TILE], win_window[-W_TILE:]])
    first, mod_all, w_t, w_edges, cos, sin = _prologue(
        cw, w_ada2, b_shard, win_window, win_edges, in_cols, positions.reshape(s, 1), jnp.tile(inv_freq, 4).reshape(1, LANES))

    first = first.reshape(8, 2, 8, LANES)
    c_all = first[:, 0].reshape(8, D_MODEL)
    w_dec_full = first[0::2, 1].reshape(4, GLA_RANK, 64).transpose(1, 0, 2).reshape(GLA_RANK, 256)
    mod = mod_all.reshape(4, 2, 8, ada_cols)[:, 0]
    mod = lax.dynamic_slice(mod, (0, dev, 0), (4, 1, ada_cols)).reshape(1, 4 * ada_cols)
    shift, sc1p, gate = mod[:, :D_MODEL], 1.0 + mod[:, D_MODEL:2 * D_MODEL], mod[:, 2 * D_MODEL:]
    wdecp = jnp.pad(w_dec_full, ((0, LANES - GLA_RANK), (0, 0))).astype(BF)

    proj = _inproj_fwd(x2d, shift, sc1p, g_norm, w_t, w_edges)
    og, o_gla, sprev = _gla_fwd(proj, wdecp, b_decay, g_gla_head)
    osw, o_swa, w_out_all = _swa_fwd(proj, cos, sin, sinks, half_out)
    w_out_all = w_out_all.reshape(D_MODEL, D_MODEL)
    dx2, dog, dos, dw_out, loss_p, dgf, dgate = _outproj(og, osw, w_out_all, x2d, target, gate, g_final.reshape(1, D_MODEL))
    dsq, dsz, dsk, dsv, dsinks, g_w_out = _swa_bwd(proj, dos, o_swa, cos, sin, sinks, dw_out.reshape(4, out_rows, D_MODEL))
    dqk, dv, dgz, dga, dwdp, dbd, dgg = _gla_bwd(proj, dog, o_gla, sprev, wdecp, b_decay, g_gla_head)
    pieces = (dqk, dv, dgz, dsq, dsz, dsk, dsv, dga)
    gx, dw_in_t, dshift, dscale, dgn = _inproj_bwd(x2d, shift, sc1p, g_norm, w_t, w_edges, dx2, pieces)

    segs = [jnp.concatenate([dshift, dscale, dgate], axis=1), dgn, dgf, dwdp[:GLA_RANK], dbd, dgg, dsinks, loss_p]
    packed = [_rows8(a) for a in segs]
    offs = [0]
    for a in packed:
        offs.append(offs[-1] + a.shape[0])
    g_window, small = _epilogue(dw_in_t, jnp.concatenate(packed, axis=0))

    def seg(i, size):
        return small[:, offs[i]:offs[i + 1]].reshape(8, -1)[:, :size]

    dmod_all = seg(0, 3 * D_MODEL)
    dwd_all = lax.dynamic_slice(seg(3, GLA_RANK * 256).reshape(8, GLA_RANK, 256), (0, 0, chip * 64), (8, GLA_RANK, 64))
    parts = [dmod_all.reshape(8, 1, 3 * D_MODEL), seg(1, D_MODEL).reshape(8, 1, D_MODEL), dwd_all,
             seg(4, 256).reshape(8, 1, 256), seg(5, 512).reshape(8, 1, 512), seg(6, SWA_HEADS).reshape(8, 1, SWA_HEADS),
             seg(2, D_MODEL).reshape(8, 1, D_MODEL), seg(7, LANES).reshape(8, 1, LANES)]
    smalls = _small_update(
        parts,
        [b_ada, g_norm, w_dec2, b_decay, g_gla_head, sinks, g_final.reshape(1, D_MODEL)],
        [m_b_ada, m_g_norm, m_w_decay[0], m_b_decay, m_g_gla_head, m_sinks, m_g_final.reshape(1, D_MODEL)],
        [v_b_ada, v_g_norm, v_w_decay[0], v_b_decay, v_g_gla_head, v_sinks, v_g_final.reshape(1, D_MODEL)])
    (g_b_ada, d_b_ada, nm_b_ada, nv_b_ada, g_gn, d_gn, nm_gn, nv_gn, g_wd, d_wd, nm_wd, nv_wd,
     g_bd, d_bd, nm_bd, nv_bd, g_gg, d_gg, nm_gg, nv_gg, g_sk, d_sk, nm_sk, nv_sk,
     g_gf, d_gf, nm_gf, nv_gf, loss_row) = smalls
    loss = loss_row[0, 0]

    dmod_cols = lax.dynamic_slice(dmod_all, (0, chip * ada_cols), (8, ada_cols))
    g_w_ada, d_w_ada, nm_w_ada, nv_w_ada = _ada_update(c_all, dmod_cols, w_ada2, m_w_ada[0], v_w_ada[0])
    to3 = lambda a: jnp.transpose(a, (2, 0, 1))
    from3 = lambda a: jnp.transpose(a, (1, 2, 0))[0]
    d3, nm3, nv3, g3 = _adamw_t(to3(w_in), g_window, to3(m_w_in), to3(v_w_in), "adamw_w_in")
    g_w_in, d_w_in, nm_w_in, nv_w_in = from3(g3), from3(d3), from3(nm3), from3(nv3)
    d_w_out, nm_w_out, nv_w_out = _adamw(w_out2, g_w_out, m_w_out[0], v_w_out[0], "adamw_w_out")

    flat = lambda a: a.reshape(D_MODEL)
    grads = [g_w_ada[None], g_b_ada, g_gn, g_w_in[None], g_wd[None], g_bd, g_gg, g_sk, g_w_out[None], flat(g_gf)]
    deltas = [d_w_ada[None], d_b_ada, d_gn, d_w_in[None], d_wd[None], d_bd, d_gg, d_sk, d_w_out[None], flat(d_gf)]
    new_m = [nm_w_ada[None], nm_b_ada, nm_gn, nm_w_in[None], nm_wd[None], nm_bd, nm_gg, nm_sk, nm_w_out[None], flat(nm_gf)]
    new_v = [nv_w_ada[None], nv_b_ada, nv_gn, nv_w_in[None], nv_wd[None], nv_bd, nv_gg, nv_sk, nv_w_out[None], flat(nv_gf)]
    return (loss, gx[None], *grads, *deltas, *new_m, *new_v)
```

```python
import jax
import jax.numpy as jnp
from jax import lax
from jax.experimental import pallas as pl
from jax.experimental.pallas import tpu as pltpu

F32 = jnp.float32
BF = jnp.bfloat16

D_MODEL = 1024
GLA_HEADS = 4
GLA_DK = 64
GLA_CHUNK = 64
GLA_RANK = 16
GLA_TAU = 16.0
GLA_SUB = 256
GLA_ROWS_FWD = 1024
GLA_ROWS_BWD = 512
SWA_HEADS = 8
SWA_BLOCK = 128
SWA_QBLOCKS_FWD = 8
SWA_QBLOCKS = 8
RMS_EPS = 1e-6
ROPE_THETA = 10000.0

OFF_QK, OFF_V, OFF_GZ, OFF_SQ, OFF_SZ, OFF_SK, OFF_SV, OFF_GA = 0, 512, 1024, 1536, 2048, 2560, 2688, 2816
D_PAD = 2944
D_IN = 2832
LANES = 128
VMEM_LIMIT = 56 * 1024 * 1024

ADAM_LR, ADAM_B1, ADAM_B2, ADAM_EPS, ADAM_WD, ADAM_STEP = 0.001, 0.9, 0.999, 1e-08, 0.01, 10

NT = (((1,), (1,)), ((), ()))
TN = (((0,), (0,)), ((), ()))
MESH = pl.DeviceIdType.MESH


def _dot(a, b, dims=None):
    if dims is None:
        return jnp.dot(a, b, preferred_element_type=F32)
    return lax.dot_general(a, b, dims, preferred_element_type=F32)


def _sigmoid(x):
    return 1.0 / (1.0 + jnp.exp(-x))


def _params(sem=None):
    return pltpu.CompilerParams(dimension_semantics=sem, vmem_limit_bytes=VMEM_LIMIT)


def _full(shape):
    return pl.BlockSpec(shape, lambda i: (0,) * len(shape))


def _subtiles(rows, size=256):
    size = min(size, rows)
    return [slice(k * size, (k + 1) * size) for k in range(rows // size)]


WEIGHT_CHUNKS = 4


def _gather_sems(chunks=1):
    return [pltpu.SemaphoreType.DMA((7 * chunks,)), pltpu.SemaphoreType.DMA((7 * chunks,)), pltpu.SemaphoreType.DMA]


_GATHER_SEMS = _gather_sems()


class _Gather:
    def __init__(self, x_ref, out_ref, send_sems, recv_sems, local_sem, slab=None, chunks=1):
        self.slab_of = slab
        self.chunks = chunks
        self.width = x_ref.shape[-1] // chunks
        x, y, c = lax.axis_index("x"), lax.axis_index("y"), lax.axis_index("c")
        self.me, self.sibling, self.c = (x, y, c), (x, y, 1 - c), c
        self.xn, self.yn, self.dg = (1 - x, y), (x, 1 - y), (1 - x, 1 - y)
        self.pass_from = (lax.rem(x + 1 - c, 2), lax.rem(y + c, 2))
        self.pass_to = (lax.rem(x + c, 2), lax.rem(y + 1 - c, 2))
        self.x_ref, self.out_ref, self.send_sems, self.recv_sems = x_ref, out_ref, send_sems, recv_sems
        self.mine = pltpu.make_async_copy(x_ref, self._slab(*self.me), local_sem)

    def _slab(self, px, py, pc):
        if self.slab_of is not None:
            return self.slab_of(self.out_ref, px, py, pc)
        return self.out_ref.at[4 * px + 2 * py + pc]

    def _part(self, ref, q):
        if self.chunks == 1:
            return ref
        lanes = slice(q * self.width, (q + 1) * self.width)
        return ref.at[(slice(None),) * (len(ref.shape) - 1) + (lanes,)]

    def _copy(self, k, q, blk, to, src=None):
        i = k * self.chunks + q
        return pltpu.make_async_remote_copy(
            src_ref=self._part(self._slab(*blk) if src is None else src, q), dst_ref=self._part(self._slab(*blk), q),
            send_sem=self.send_sems.at[i], recv_sem=self.recv_sems.at[i], device_id=to, device_id_type=MESH)

    def _sends(self, q):
        c = self.c
        return [self._copy(0, q, self.me, self.sibling, src=self.x_ref),
                self._copy(1, q, self.me, (*self.xn, c), src=self.x_ref),
                self._copy(2, q, self.me, (*self.yn, c), src=self.x_ref),
                self._copy(3, q, (*self.pass_from, c), (*self.pass_to, c)),
                self._copy(4, q, (*self.xn, c), self.sibling),
                self._copy(5, q, (*self.yn, c), self.sibling),
                self._copy(6, q, (*self.dg, c), self.sibling)]

    def start(self):
        self.mine.start()
        for q in range(self.chunks):
            sends = self._sends(q)
            for k in (1, 2, 0):
                sends[k].start()

    def pass_on(self, only=None):
        for q in range(self.chunks) if only is None else (only,):
            sends = self._sends(q)
            self._copy(1, q, (*self.xn, self.c), self.me).wait_recv()
            self._copy(2, q, (*self.yn, self.c), self.me).wait_recv()
            for k in (3, 4, 5):
                sends[k].start()

    def relay_diagonal(self, only=None):
        for q in range(self.chunks) if only is None else (only,):
            self._copy(3, q, (*self.dg, self.c), self.me).wait_recv()
            self._sends(q)[6].start()

    def relay(self):
        self.pass_on()
        self.relay_diagonal()

    def finish(self):
        c = self.c
        for q in range(self.chunks):
            self._copy(0, q, self.sibling, self.me).wait_recv()
            for k, chip in ((4, self.xn), (5, self.yn), (6, self.dg)):
                self._copy(k, q, (*chip, 1 - c), self.me).wait_recv()
            for cp in self._sends(q):
                cp.wait_send()
        self.mine.wait()


def _prologue(cw, w_ada, b_shard, win_window, win_edges, n_in, pos_rows, inv_freq):
    s = pos_rows.shape[0] * LANES
    rt = min(512, s)
    inner = win_window.shape[0] - 2 * W_TILE

    def body(cw_ref, wada_hbm, b_ref, hin_ref, hedge_ref, pos_ref, f_ref,
             first_ref, mod_ref, win_ref, edge_ref, cos_hbm, sin_hbm,
             mod_blk, cos_ref, sin_ref, wada_ref, table_sems, local_sems, *sems):
        fetch_w = pltpu.make_async_copy(wada_hbm, wada_ref, local_sems.at[0])
        fetch_w.start()
        g_c = _Gather(cw_ref, first_ref, *sems[0:3])
        half_lanes = hin_ref.shape[1]

        def lanes_of(pc):
            return pl.ds(pl.multiple_of(pc * half_lanes, half_lanes), half_lanes)

        def inner_rows(px, py):
            return pl.ds(pl.multiple_of((n_in * (2 * px + py)) // W_TILE * W_TILE + W_TILE, W_TILE), inner)

        g_in = _Gather(hin_ref.at[pl.ds(W_TILE, inner), :], win_ref, *sems[3:6], chunks=WEIGHT_CHUNKS,
                       slab=lambda ref, px, py, pc: ref.at[inner_rows(px, py), lanes_of(pc)])
        g_mod = _Gather(mod_blk, mod_ref, *sems[6:9])
        g_edge = _Gather(hedge_ref, edge_ref, *sems[9:12],
                         slab=lambda ref, px, py, pc: ref.at[2 * px + py, :, :, lanes_of(pc)])
        g_c.start()
        g_edge.start()
        g_in.start()
        g_c.relay()
        g_edge.relay()
        g_c.finish()
        c_rows = [jnp.concatenate([first_ref[d, r:r + 1, :] for r in range(8)], axis=1) for d in range(8)]
        c_all = jnp.concatenate(c_rows, axis=0)
        sc = (c_all * _sigmoid(c_all)).astype(BF)
        fetch_w.wait()
        mod_blk[...] = _dot(sc, wada_ref[...].astype(BF)) + b_ref[...]
        g_mod.start()

        def rope_rows(i, carry):
            rows = pl.ds(pl.multiple_of(i * rt, rt), rt)
            cols = [jnp.transpose(jnp.broadcast_to(pos_ref[pl.ds(i * (rt // LANES) + b, 1), :].astype(F32), (LANES, LANES)))
                    for b in range(rt // LANES)]
            ang = jnp.concatenate(cols, axis=0) * f_ref[...]
            lane = lax.broadcasted_iota(jnp.int32, ang.shape, 1)
            cos_ref[rows, :] = jnp.cos(ang)
            sn = jnp.sin(ang)
            sin_ref[rows, :] = jnp.where((lane % 64) < 32, -sn, sn)
            pltpu.make_async_copy(cos_ref.at[rows, :], cos_hbm.at[rows, :], table_sems.at[0]).start()
            pltpu.make_async_copy(sin_ref.at[rows, :], sin_hbm.at[rows, :], table_sems.at[1]).start()
            return carry

        waits = ([lambda q=q: g_in.pass_on(q) for q in range(WEIGHT_CHUNKS)]
                 + [lambda q=q: g_in.relay_diagonal(q) for q in range(WEIGHT_CHUNKS)] + [g_mod.relay])
        steps = s // rt
        lead = steps // 4
        per_wait = max((steps - lead) // len(waits), 1)
        lax.fori_loop(0, lead, rope_rows, 0)
        done = lead
        for wait in waits:
            wait()
            nxt = min(done + per_wait, steps)
            lax.fori_loop(done, nxt, rope_rows, 0)
            done = nxt
        lax.fori_loop(done, steps, rope_rows, 0)
        g_in.finish()
        g_mod.finish()
        g_edge.finish()
        pltpu.make_async_copy(cos_ref, cos_hbm, table_sems.at[0]).wait()
        pltpu.make_async_copy(sin_ref, sin_hbm, table_sems.at[1]).wait()

    vm = pl.BlockSpec(memory_space=pltpu.VMEM)
    hbm = pl.BlockSpec(memory_space=pl.ANY)
    half_lanes = win_window.shape[1]
    return pl.pallas_call(
        body, name="prologue",
        out_shape=[jax.ShapeDtypeStruct((8,) + cw.shape, F32), jax.ShapeDtypeStruct((8, 8, w_ada.shape[1]), F32),
                   jax.ShapeDtypeStruct((4 * n_in, 2 * half_lanes), win_window.dtype),
                   jax.ShapeDtypeStruct((4, 2, W_TILE, 2 * half_lanes), win_window.dtype),
                   jax.ShapeDtypeStruct((s, LANES), F32), jax.ShapeDtypeStruct((s, LANES), F32)],
        in_specs=[vm, hbm, vm, hbm, vm, vm, vm], out_specs=[vm, vm, hbm, vm, hbm, hbm],
        scratch_shapes=[pltpu.VMEM((8, w_ada.shape[1]), F32), pltpu.VMEM((s, LANES), F32), pltpu.VMEM((s, LANES), F32),
                        pltpu.VMEM(w_ada.shape, F32),
                        pltpu.SemaphoreType.DMA((2,)), pltpu.SemaphoreType.DMA((1,))]
        + _GATHER_SEMS + _gather_sems(WEIGHT_CHUNKS) + _GATHER_SEMS + _GATHER_SEMS,
        compiler_params=pltpu.CompilerParams(vmem_limit_bytes=VMEM_LIMIT),
    )(cw, w_ada, b_shard, win_window, win_edges, pos_rows, inv_freq)


def _reduce_scratch(rr, cc):
    c2 = cc // 2
    return [pltpu.VMEM((4, rr, c2), F32), pltpu.VMEM((4, rr, c2), F32), pltpu.VMEM((3, rr, c2), BF),
            pltpu.VMEM((2, rr, c2), BF), pltpu.VMEM((rr, c2), BF), pltpu.VMEM((rr, c2), F32),
            pltpu.SemaphoreType.DMA((8 + 3 * WEIGHT_CHUNKS,)), pltpu.SemaphoreType.DMA((8 + 3 * WEIGHT_CHUNKS,)),
            pltpu.SemaphoreType.DMA((5,))]


class _Reduce:
    def __init__(self, p_hbm, out_ref, acc_ref, own_ref, send_ref, land_ref, relay_ref, res_ref,
                 send_sems, recv_sems, local_sems, rows=None):
        x, y, c = lax.axis_index("x"), lax.axis_index("y"), lax.axis_index("c")
        part = (lambda j, ln: p_hbm.at[j, :, ln]) if rows is None else (lambda j, ln: p_hbm.at[rows(j), ln])
        c2 = out_ref.shape[1] // 2
        sibling = (x, y, 1 - c)
        first = (lax.rem(x + 1 - c, 2), lax.rem(y + c, 2))
        second = (lax.rem(x + c, 2), lax.rem(y + 1 - c, 2))
        shards = [2 * first[0] + first[1], 2 * second[0] + second[1], 2 * (1 - x) + (1 - y), 2 * x + y]
        sibling_slot = (1, 0, 2, 3)
        mine = pl.ds(pl.multiple_of(c * c2, c2), c2)
        other = pl.ds(pl.multiple_of((1 - c) * c2, c2), c2)
        self.acc_ref, self.own_ref, self.send_ref, self.land_ref = acc_ref, own_ref, send_ref, land_ref
        self.relay_ref, self.res_ref = relay_ref, res_ref
        self.own = [pltpu.make_async_copy(part(j, mine), own_ref.at[k], local_sems.at[k])
                    for k, j in enumerate(shards)]
        self.swap_out = [pltpu.make_async_remote_copy(
            src_ref=part(j, other), dst_ref=acc_ref.at[sibling_slot[k]], send_sem=send_sems.at[k],
            recv_sem=recv_sems.at[sibling_slot[k]], device_id=sibling, device_id_type=MESH) for k, j in enumerate(shards)]
        self.swap_in = [pltpu.make_async_remote_copy(
            src_ref=part(j, other), dst_ref=acc_ref.at[k], send_sem=send_sems.at[k], recv_sem=recv_sems.at[k],
            device_id=sibling, device_id_type=MESH) for k, j in enumerate(shards)]

        self.lanes = [slice(q * (c2 // WEIGHT_CHUNKS), (q + 1) * (c2 // WEIGHT_CHUNKS)) for q in range(WEIGHT_CHUNKS)]

        def message(m, src, dst, to):
            return [pltpu.make_async_remote_copy(
                src_ref=src.at[:, ln], dst_ref=dst.at[:, ln], send_sem=send_sems.at[8 + m * WEIGHT_CHUNKS + q],
                recv_sem=recv_sems.at[8 + m * WEIGHT_CHUNKS + q], device_id=(*to, c), device_id_type=MESH)
                for q, ln in enumerate(self.lanes)]

        self.direct = message(0, send_ref.at[0], land_ref.at[0], first)
        self.passed = message(1, send_ref.at[1], relay_ref, first)
        self.joint = message(2, send_ref.at[2], land_ref.at[1], second)
        self.put = pltpu.make_async_copy(res_ref, out_ref.at[:, mine], local_sems.at[4])
        self.share = pltpu.make_async_remote_copy(
            src_ref=res_ref, dst_ref=out_ref.at[:, mine], send_sem=send_sems.at[7],
            recv_sem=recv_sems.at[7], device_id=sibling, device_id_type=MESH)

    def start(self):
        for k in (2, 0, 1, 3):
            self.own[k].start()
            self.swap_out[k].start()

    def _combine(self, k):
        self.own[k].wait()
        self.swap_out[k].wait_send()
        self.swap_in[k].wait_recv()
        self.acc_ref[k] = self.acc_ref[k] + self.own_ref[k]

    def combine_and_send(self):
        dt = self.send_ref.dtype
        self._combine(2)
        self.send_ref[1] = self.acc_ref[2].astype(dt)
        for cp in self.passed:
            cp.start()
        self._combine(0)
        self.send_ref[0] = self.acc_ref[0].astype(dt)
        for cp in self.direct:
            cp.start()
        self._combine(1)
        self._combine(3)

    def send_joint(self):
        dt = self.send_ref.dtype
        for q, ln in enumerate(self.lanes):
            self.passed[q].wait_recv()
            self.send_ref[2, :, ln] = (self.acc_ref[1, :, ln] + self.relay_ref[:, ln].astype(F32)).astype(dt)
            self.joint[q].start()

    def total_and_share(self):
        for cp in self.direct + self.joint:
            cp.wait_recv()
        self.res_ref[...] = self.acc_ref[3] + self.land_ref[0].astype(F32) + self.land_ref[1].astype(F32)
        for cp in self.direct + self.passed + self.joint:
            cp.wait_send()
        self.put.start()
        self.share.start()

    def finish(self):
        self.put.wait()
        self.share.wait()


def _shard_window(n):
    return max(-(-(n * (j + 1)) // 8) * 8 - (n * j) // 8 * 8 for j in range(4))


def _epilogue(dw_in_t, small):
    cc = dw_in_t.shape[1]
    n = dw_in_t.shape[0] // 4
    r_in = _shard_window(n)
    n_red = len(_reduce_scratch(r_in, cc))

    def body(pin_hbm, small_ref, gin_ref, small_all_ref, *scratch):
        red_in = _Reduce(pin_hbm, gin_ref, *scratch[0:n_red],
                         rows=lambda j: pl.ds(pl.multiple_of((n * j) // 8 * 8, 8), r_in))
        gat = _Gather(small_ref, small_all_ref, *scratch[n_red:])
        red_in.start()
        gat.start()
        gat.relay()
        red_in.combine_and_send()
        gat.finish()
        red_in.send_joint()
        red_in.total_and_share()
        red_in.finish()

    vm = pl.BlockSpec(memory_space=pltpu.VMEM)
    anyspec = pl.BlockSpec(memory_space=pl.ANY)
    return pl.pallas_call(
        body, name="epilogue",
        out_shape=[jax.ShapeDtypeStruct((r_in, cc), F32), jax.ShapeDtypeStruct((8,) + small.shape, F32)],
        in_specs=[anyspec, vm], out_specs=[anyspec, vm],
        scratch_shapes=_reduce_scratch(r_in, cc) + _GATHER_SEMS,
        compiler_params=pltpu.CompilerParams(vmem_limit_bytes=VMEM_LIMIT),
    )(dw_in_t, small)


def _rope(t, cosb, sinb, first_half):
    partner = jnp.where(first_half, pltpu.roll(t, 96, 1), pltpu.roll(t, 32, 1))
    return t * cosb + partner * sinb


def _rope_t(g, cosb, sinb, first_half):
    gs = g * sinb
    partner = jnp.where(first_half, pltpu.roll(gs, 96, 1), pltpu.roll(gs, 32, 1))
    return g * cosb + partner


def _modnorm(x, g, sc1p, shift):
    r = lax.rsqrt(jnp.mean(x * x, axis=-1, keepdims=True) + RMS_EPS)
    xn = x * r
    return xn, r, (xn * g) * sc1p + shift


W_TILE = 16


def _w_window(n):
    return max(-(-(n * (j + 1)) // W_TILE) * W_TILE - (n * j) // W_TILE * W_TILE for j in range(4))


def _w_load_plan(n):
    starts = [(n * j) // W_TILE * W_TILE for j in range(4)]
    edges = starts + [starts[3] + _w_window(n) - W_TILE]
    assert all(starts[j] + _w_window(n) - W_TILE == edges[j + 1] for j in range(4))
    pieces, tiles = [], {}
    for pad, ref, rows in _UNPAD_ROWS:
        lo = ref
        for k, e in enumerate(edges):
            if ref <= e < ref + rows:
                if e > lo:
                    pieces.append((pad + lo - ref, lo, e - lo))
                tiles[k] = pad + e - ref
                lo = e + W_TILE
        if ref + rows > lo:
            pieces.append((pad + lo - ref, lo, ref + rows - lo))
    return pieces, tiles


def _load_w_padded(w_hbm, edge_ref, w_vm, sems):
    n = w_hbm.shape[0] // 4
    pieces, tiles = _w_load_plan(n)
    copies = [pltpu.make_async_copy(w_hbm.at[ref:ref + rows], w_vm.at[pad:pad + rows], sems.at[k])
              for k, (pad, ref, rows) in enumerate(pieces)]
    for cp in copies:
        cp.start()
    w_vm[OFF_GA + GLA_RANK:, :] = jnp.zeros((D_PAD - OFF_GA - GLA_RANK, D_MODEL), w_vm.dtype)
    row = lax.broadcasted_iota(jnp.int32, (W_TILE, w_vm.shape[1]), 0)
    for k, pad in tiles.items():
        last = edge_ref[max(k - 1, 0), 1].astype(F32)
        first = edge_ref[min(k, 3), 0].astype(F32)
        cut = W_TILE if k == 4 else (n * k) % W_TILE
        w_vm[pad:pad + W_TILE, :] = jnp.where(row < cut, last, first).astype(w_vm.dtype)
    return copies


def _inproj_fwd(x2d, shift, sc1p, g_norm, w_t, w_edges):
    s = x2d.shape[0]
    tm = min(1024, s)

    def body(x_ref, sh_ref, sc_ref, g_ref, w_hbm, edge_ref, o_ref, w_vm, sems):
        @pl.when(pl.program_id(0) == 0)
        def _():
            for cp in _load_w_padded(w_hbm, edge_ref, w_vm, sems):
                cp.wait()

        subs = _subtiles(tm)
        hs = [_modnorm(x_ref[sl, :], g_ref[...], sc_ref[...], sh_ref[...])[2].astype(BF) for sl in subs]
        for sl, h in zip(subs, hs):
            o_ref[sl, :] = _dot(h, w_vm[...], NT)

    vec = _full((1, D_MODEL))
    return pl.pallas_call(
        body, name="inproj_fwd", grid=(s // tm,),
        in_specs=[pl.BlockSpec((tm, D_MODEL), lambda i: (i, 0)), vec, vec, vec, pl.BlockSpec(memory_space=pl.ANY),
                  _full(w_edges.shape)],
        out_specs=pl.BlockSpec((tm, D_PAD), lambda i: (i, 0)),
        out_shape=jax.ShapeDtypeStruct((s, D_PAD), F32),
        scratch_shapes=[pltpu.VMEM((D_PAD, D_MODEL), BF),
                        pltpu.SemaphoreType.DMA((len(_w_load_plan(w_t.shape[0] // 4)[0]),))],
        compiler_params=_params(("arbitrary",)),
    )(x2d, shift, sc1p, g_norm, w_t, w_edges)


def _split3(a):
    hi = a.astype(BF)
    r1 = a - hi.astype(F32)
    mid = r1.astype(BF)
    lo = (r1 - mid.astype(F32)).astype(BF)
    return hi, mid, lo


def _tri_matmul(tri, a):
    hi, mid, lo = _split3(a)
    return _dot(tri, hi) + _dot(tri, mid) + _dot(tri, lo)


def _chunks(tb):
    return [slice(c * GLA_CHUNK, (c + 1) * GLA_CHUNK) for c in range(tb // GLA_CHUNK)]


def _per_chunk_rows(rows, width):
    return jnp.concatenate([jnp.broadcast_to(r, (GLA_CHUNK, width)) for r in rows], axis=0)


def _gla_triangle(tb):
    row = lax.broadcasted_iota(jnp.int32, (tb, tb), 0)
    col = lax.broadcasted_iota(jnp.int32, (tb, tb), 1)
    return (((row // GLA_CHUNK) == (col // GLA_CHUNK)) & (col <= row)).astype(F32)


def _lane_mean(x, ones_b):
    hi = x.astype(BF)
    lo = (x - hi.astype(F32)).astype(BF)
    return (_dot(hi, ones_b) + _dot(lo, ones_b)) * (1.0 / LANES)


def _head(t, h, lo_h):
    blk = t[:, LANES * (h // 2):LANES * (h // 2 + 1)]
    return jnp.where(lo_h, blk, 0.0) if h % 2 == 0 else jnp.where(lo_h, 0.0, blk)


def _gla_block_common(qk, ga, wd, bd, tril_b):
    tb = qk.shape[0]
    q, k = qk[:, :256], qk[:, 256:]
    z = _dot(ga.astype(BF), wd) + bd
    la = (jnp.minimum(z, 0.0) - jnp.log(1.0 + jnp.exp(-jnp.abs(z)))) * (1.0 / GLA_TAU)
    b = _tri_matmul(tril_b, la)
    bls = [b[rs.stop - 1:rs.stop, :] for rs in _chunks(tb)]
    eq = jnp.exp(b)
    ek = jnp.exp(-b)
    f = jnp.exp(_per_chunk_rows(bls, 256) - b)
    return z, eq, ek, f, q * (eq * GLA_DK ** -0.5), k * ek, k * f, bls


def _gla_units(s, rows):
    sub = min(GLA_SUB, s)
    tb = min(rows, s)
    subs = [slice(i * sub, (i + 1) * sub) for i in range(tb // sub)]
    units = [(i, h) for i in range(len(subs)) for h in range(GLA_HEADS)]
    return tb, sub, subs, units


def _gla_fwd(proj, wdecp, bdec, ggla):
    s = proj.shape[0]
    tb, sub, subs, units = _gla_units(s, GLA_ROWS_FWD)
    nch = sub // GLA_CHUNK

    def body(qk_ref, v_ref, gz_ref, ga_ref, wd_ref, bd_ref, gg_ref, tri_ref, og_ref, opre_ref, sprev_ref, st_ref):
        @pl.when(pl.program_id(0) == 0)
        def _():
            st_ref[...] = jnp.zeros_like(st_ref)

        lo_h = lax.broadcasted_iota(jnp.int32, (sub, LANES), 1) < GLA_DK
        tril = tri_ref[...] > 0.5
        tril_b = tri_ref[...].astype(BF)
        ones_b = jnp.ones((LANES, LANES), BF)
        gg, wd, bd = gg_ref[...], wd_ref[...], bd_ref[...]
        chunks = _chunks(sub)
        lanes = [slice(h * LANES, (h + 1) * LANES) for h in range(GLA_HEADS)]
        com = [_gla_block_common(qk_ref[sl, :], ga_ref[sl, :], wd, bd, tril_b) for sl in subs]
        decs = [[jnp.exp(bl) for bl in cm[7]] for cm in com]
        a = {(i, h): _head(com[i][4], h, lo_h).astype(BF) for i, h in units}
        bm = {(i, h): _head(com[i][5], h, lo_h).astype(BF) for i, h in units}
        ktl = {(i, h): _head(com[i][6], h, lo_h).astype(BF) for i, h in units}
        vh = {(i, h): v_ref[subs[i], lanes[h]].astype(BF) for i, h in units}
        sc = {u: _dot(a[u], bm[u], NT) for u in units}
        upd = {u: [_dot(vh[u][rs], ktl[u][rs], TN) for rs in chunks] for u in units}
        p = {u: jnp.where(tril, sc[u], 0.0).astype(BF) for u in units}
        o = {u: _dot(p[u], vh[u]) for u in units}
        states = {}
        for h in range(GLA_HEADS):
            st = st_ref[h]
            for i in range(len(subs)):
                entering = []
                for c in range(nch):
                    entering.append(st)
                    sprev_ref[i * nch + c, h] = st
                    st = st * decs[i][c][:, LANES * (h // 2):LANES * (h // 2 + 1)] + upd[(i, h)][c]
                states[(i, h)] = entering
            st_ref[h] = st
        inter = {u: [_dot(a[u][rs], states[u][c].astype(BF), NT) for c, rs in enumerate(chunks)] for u in units}
        o = {u: o[u] + jnp.concatenate(inter[u], axis=0) for u in units}
        ms = {u: _lane_mean(o[u] * o[u], ones_b) for u in units}
        for i, h in units:
            gzh = gz_ref[subs[i], lanes[h]]
            opre_ref[subs[i], lanes[h]] = o[(i, h)]
            og_ref[subs[i], lanes[h]] = (((o[(i, h)] * lax.rsqrt(ms[(i, h)] + RMS_EPS)) * gg[:, lanes[h]])
                                         * (gzh * _sigmoid(gzh))).astype(og_ref.dtype)

    def col(width, off):
        return pl.BlockSpec((tb, width), lambda i: (i, off // width))

    return pl.pallas_call(
        body, name="gla_fwd", grid=(s // tb,),
        in_specs=[col(512, OFF_QK), col(512, OFF_V), col(512, OFF_GZ), col(LANES, OFF_GA),
                  _full((LANES, 256)), _full((1, 256)), _full((1, 512)), _full((sub, sub))],
        out_specs=[pl.BlockSpec((tb, 512), lambda i: (i, 0)), pl.BlockSpec((tb, 512), lambda i: (i, 0)),
                   pl.BlockSpec((tb // GLA_CHUNK, GLA_HEADS, LANES, LANES), lambda i: (i, 0, 0, 0))],
        out_shape=[jax.ShapeDtypeStruct((s, 512), BF), jax.ShapeDtypeStruct((s, 512), F32),
                   jax.ShapeDtypeStruct((s // GLA_CHUNK, GLA_HEADS, LANES, LANES), F32)],
        scratch_shapes=[pltpu.VMEM((GLA_HEADS, LANES, LANES), F32)],
        compiler_params=_params(("arbitrary",)),
    )(proj, proj, proj, proj, wdecp, bdec, ggla, _gla_triangle(sub))


def _gla_bwd(proj, dog, opre, sprev, wdecp, bdec, ggla):
    s = proj.shape[0]
    tb, sub, subs, units = _gla_units(s, GLA_ROWS_BWD)
    nsub = len(subs)
    nch = sub // GLA_CHUNK
    nb = s // tb

    def body(qk_ref, v_ref, gz_ref, ga_ref, dog_ref, opre_ref, sprev_ref, wd_ref, bd_ref, gg_ref, tri_ref, triu_ref,
             dqk_ref, dv_ref, dgz_ref, dga_ref, dwd_ref, dbd_ref, dgg_ref, dst_ref):
        @pl.when(pl.program_id(0) == 0)
        def _():
            dst_ref[...] = jnp.zeros_like(dst_ref)
            dwd_ref[...] = jnp.zeros_like(dwd_ref)
            dbd_ref[...] = jnp.zeros_like(dbd_ref)
            dgg_ref[...] = jnp.zeros_like(dgg_ref)

        lo_h = lax.broadcasted_iota(jnp.int32, (sub, LANES), 1) < GLA_DK
        tril = tri_ref[...] > 0.5
        tril_b = tri_ref[...].astype(BF)
        triu_b = triu_ref[...].astype(BF)
        ones_b = jnp.ones((LANES, LANES), BF)
        last_row = (lax.broadcasted_iota(jnp.int32, (sub, LANES), 0) % GLA_CHUNK) == GLA_CHUNK - 1
        wd, gg, bd = wd_ref[...], gg_ref[...], bd_ref[...]
        chunks = _chunks(sub)
        lanes = [slice(h * LANES, (h + 1) * LANES) for h in range(GLA_HEADS)]
        blks = [slice(LANES * (h // 2), LANES * (h // 2 + 1)) for h in range(GLA_HEADS)]
        ga = [ga_ref[sl, :] for sl in subs]
        com = [_gla_block_common(qk_ref[sl, :], ga[i], wd, bd, tril_b) for i, sl in enumerate(subs)]
        decs = [[jnp.exp(bl) for bl in cm[7]] for cm in com]
        a = {(i, h): _head(com[i][4], h, lo_h).astype(BF) for i, h in units}
        bm = {(i, h): _head(com[i][5], h, lo_h).astype(BF) for i, h in units}
        ktl = {(i, h): _head(com[i][6], h, lo_h).astype(BF) for i, h in units}
        vh = {(i, h): v_ref[subs[i], lanes[h]].astype(BF) for i, h in units}
        sc = {u: _dot(a[u], bm[u], NT) for u in units}

        o = {(i, h): opre_ref[subs[i], lanes[h]] for i, h in units}
        ms = {u: _lane_mean(o[u] * o[u], ones_b) for u in units}
        gz = {(i, h): gz_ref[subs[i], lanes[h]] for i, h in units}
        dog = {(i, h): dog_ref[subs[i], lanes[h]] for i, h in units}
        sg = {u: _sigmoid(gz[u]) for u in units}
        r = {u: lax.rsqrt(ms[u] + RMS_EPS) for u in units}
        ohat = {u: o[u] * r[u] for u in units}
        sil = {u: gz[u] * sg[u] for u in units}
        for i, h in units:
            u = (i, h)
            dgz_ref[subs[i], lanes[h]] = (dog[u] * (ohat[u] * gg[:, lanes[h]])
                                          * (sg[u] * (1.0 + gz[u] * (1.0 - sg[u])))).astype(dgz_ref.dtype)
            dgg_ref[:, lanes[h]] += jnp.sum(dog[u] * sil[u] * ohat[u], axis=0, keepdims=True)
        dn = {(i, h): dog[(i, h)] * sil[(i, h)] * gg[:, lanes[h]] for i, h in units}
        mdn = {u: _lane_mean(dn[u] * ohat[u], ones_b) for u in units}
        do = {u: (r[u] * (dn[u] - ohat[u] * mdn[u])).astype(BF) for u in units}

        p = {u: jnp.where(tril, sc[u], 0.0).astype(BF) for u in units}
        dpr = {u: _dot(do[u], vh[u], NT) for u in units}
        incr = {u: [_dot(do[u][rs], a[u][rs], TN) for rs in chunks] for u in units}
        dv = {u: _dot(p[u], do[u], TN) for u in units}
        dp = {u: jnp.where(tril, dpr[u], 0.0).astype(BF) for u in units}
        dqd = {u: _dot(dp[u], bm[u]) for u in units}
        dkd = {u: _dot(dp[u], a[u], TN) for u in units}
        st = {(i, h): [sprev_ref[i * nch + c, h] for c in range(nch)] for i, h in units}
        leaving = {}
        for h in range(GLA_HEADS):
            d = dst_ref[h]
            for i in reversed(range(nsub)):
                out = [None] * nch
                for c in reversed(range(nch)):
                    out[c] = d
                    d = d * decs[i][c][:, blks[h]] + incr[(i, h)][c]
                leaving[(i, h)] = out
            dst_ref[h] = d
        lv_b = {u: [leaving[u][c].astype(BF) for c in range(nch)] for u in units}
        dv_s = {u: [_dot(ktl[u][rs], lv_b[u][c], NT) for c, rs in enumerate(chunks)] for u in units}
        dqd_s = {u: [_dot(do[u][rs], st[u][c].astype(BF)) for c, rs in enumerate(chunks)] for u in units}
        dkt_s = {u: [_dot(vh[u][rs], lv_b[u][c]) for c, rs in enumerate(chunks)] for u in units}
        ddec = {u: [jnp.sum(leaving[u][c] * st[u][c], axis=0, keepdims=True) for c in range(nch)] for u in units}
        for i, h in units:
            dv_ref[subs[i], lanes[h]] = (dv[(i, h)] + jnp.concatenate(dv_s[(i, h)], axis=0)).astype(dv_ref.dtype)
        dqd = {u: dqd[u] + jnp.concatenate(dqd_s[u], axis=0) for u in units}
        dkt = {u: jnp.concatenate(dkt_s[u], axis=0) for u in units}

        db = []
        for i, sl in enumerate(subs):
            _, eq, ek, f, qd, kd, kt, _ = com[i]
            parts = []
            for pair in range(GLA_HEADS // 2):
                blk, u0, u1 = blks[2 * pair], (i, 2 * pair), (i, 2 * pair + 1)
                dqd_b, dkd_b, dkt_b = dqd[u0] + dqd[u1], dkd[u0] + dkd[u1], dkt[u0] + dkt[u1]
                dqk_ref[sl, blk] = (dqd_b * (eq[:, blk] * GLA_DK ** -0.5)).astype(dqk_ref.dtype)
                dqk_ref[sl, 256 + LANES * pair:256 + LANES * (pair + 1)] = (dkd_b * ek[:, blk] + dkt_b * f[:, blk]).astype(dqk_ref.dtype)
                dkt_kt = dkt_b * kt[:, blk]
                dbp = dqd_b * qd[:, blk] - dkd_b * kd[:, blk] - dkt_kt
                dbl = [jnp.sum(dkt_kt[rs], axis=0, keepdims=True) + (ddec[u0][c] + ddec[u1][c]) * decs[i][c][:, blk]
                       for c, rs in enumerate(chunks)]
                parts.append(jnp.where(last_row, dbp + _per_chunk_rows(dbl, LANES), dbp))
            db.append(jnp.concatenate(parts, axis=1))
        dla = [_tri_matmul(triu_b, db[i]) for i in range(nsub)]
        dz32 = [dla[i] * (1.0 / GLA_TAU) * _sigmoid(-com[i][0]) for i in range(nsub)]
        dz = [t.astype(BF) for t in dz32]
        for i, sl in enumerate(subs):
            dga_ref[sl, :] = _dot(dz[i], wd, NT).astype(dga_ref.dtype)
            dwd_ref[...] += _dot(ga[i].astype(BF), dz[i], TN)
            dbd_ref[...] += jnp.sum(dz32[i], axis=0, keepdims=True)

    def col(width, off):
        return pl.BlockSpec((tb, width), lambda i: (nb - 1 - i, off // width))

    def rev(width):
        return pl.BlockSpec((tb, width), lambda i: (nb - 1 - i, 0))

    return pl.pallas_call(
        body, name="gla_bwd", grid=(nb,),
        in_specs=[col(512, OFF_QK), col(512, OFF_V), col(512, OFF_GZ), col(LANES, OFF_GA), rev(512), rev(512),
                  pl.BlockSpec((tb // GLA_CHUNK, GLA_HEADS, LANES, LANES), lambda i: (nb - 1 - i, 0, 0, 0)),
                  _full((LANES, 256)), _full((1, 256)), _full((1, 512)), _full((sub, sub)), _full((sub, sub))],
        out_specs=[rev(512), rev(512), rev(512), rev(LANES), _full((LANES, 256)), _full((1, 256)), _full((1, 512))],
        out_shape=[jax.ShapeDtypeStruct((s, 512), BF), jax.ShapeDtypeStruct((s, 512), BF),
                   jax.ShapeDtypeStruct((s, 512), BF), jax.ShapeDtypeStruct((s, LANES), BF),
                   jax.ShapeDtypeStruct((LANES, 256), F32), jax.ShapeDtypeStruct((1, 256), F32),
                   jax.ShapeDtypeStruct((1, 512), F32)],
        scratch_shapes=[pltpu.VMEM((GLA_HEADS, LANES, LANES), F32)],
        compiler_params=_params(("arbitrary",)),
    )(proj, proj, proj, proj, dog, opre, sprev, wdecp, bdec, ggla, _gla_triangle(sub), _gla_triangle(sub).T)


_SWA_COL_HEADS = (0, 2, 1, 3, 4, 6, 5, 7)
_SWA_COLS = SWA_HEADS * SWA_BLOCK


def _swa_masks():
    lo2 = lax.broadcasted_iota(jnp.int32, (2 * SWA_BLOCK, LANES), 1) < 64
    lane1 = lax.broadcasted_iota(jnp.int32, (SWA_BLOCK, LANES), 1)
    first_half = (lane1 % 64) < 32
    key = lax.broadcasted_iota(jnp.int32, (SWA_BLOCK, _SWA_COLS), 0)
    query = lax.broadcasted_iota(jnp.int32, (SWA_BLOCK, _SWA_COLS), 1) % SWA_BLOCK
    return lo2, lane1 < 64, first_half, key > query


def _merge_band(t, prev_mask, prev_bias=None):
    prev = t[:SWA_BLOCK] if prev_bias is None else t[:SWA_BLOCK] + prev_bias
    return jnp.where(prev_mask, prev, t[SWA_BLOCK:])


def _split_band(t, prev_mask_b):
    prev = t * prev_mask_b
    return jnp.concatenate([prev, t - prev], axis=0)


def _kv_variants(t, lo2):
    tr = pltpu.roll(t, 64, 1)
    lo_v = [jnp.where(lo2, t, 0.0).astype(BF), jnp.where(lo2, tr, 0.0).astype(BF)]
    hi_v = [jnp.where(lo2, 0.0, tr).astype(BF), jnp.where(lo2, 0.0, t).astype(BF)]
    return lo_v, hi_v


def _kv_variants_t(t):
    tt = t.T
    sw = jnp.concatenate([tt[64:], tt[:64]], axis=0)
    top = lax.broadcasted_iota(jnp.int32, tt.shape, 0) < 64
    lo_v = [jnp.where(top, tt, 0.0).astype(BF), jnp.where(top, sw, 0.0).astype(BF)]
    hi_v = [jnp.where(top, 0.0, sw).astype(BF), jnp.where(top, 0.0, tt).astype(BF)]
    return lo_v, hi_v


def _swa_scores(qg, k_lo, k_hi):
    return jnp.concatenate([_dot(k_lo[0], qg[0], NT), _dot(k_hi[0], qg[0], NT),
                            _dot(k_lo[1], qg[1], NT), _dot(k_hi[1], qg[1], NT)], axis=1)


def _sink_row(sinks_ref):
    return jnp.concatenate([jnp.full((1, SWA_BLOCK), sinks_ref[0, hd], F32) for hd in _SWA_COL_HEADS], axis=1)


def _swa_softmax(st, prev_mask, prev_bias, sink):
    st = _merge_band(st, prev_mask, prev_bias)
    m = jnp.maximum(jnp.max(st, axis=0, keepdims=True), sink)
    ex = jnp.exp(st - m)
    es = jnp.exp(sink - m)
    inv = 1.0 / (jnp.sum(ex, axis=0, keepdims=True) + es)
    return ex, es, inv


def _no_prev_bias(block_index):
    return jnp.where(block_index > 0, 0.0, -1e30).astype(F32)


def _swa_queries(sq_ref, rows, cosb, sinb, first_half):
    qs = [_rope(sq_ref[rows, p * LANES:(p + 1) * LANES], cosb, sinb, first_half) * 0.125 for p in range(4)]
    return [jnp.concatenate(qs[0:2], axis=0), jnp.concatenate(qs[2:4], axis=0)]


def _phase_steps(nsteps, phases):
    return [min(nsteps - 1, (k * nsteps) // phases) for k in range(phases - 1)] + [nsteps - 1]


def _swa_fwd(proj, cos, sin, sinks, half_out):
    s = proj.shape[0]
    nq = min(SWA_QBLOCKS_FWD, s // SWA_BLOCK)
    tq = nq * SWA_BLOCK
    steps = _phase_steps(s // tq, 4)

    def body(sq_ref, sz_ref, sk_ref, sv_ref, cos_ref, sin_ref, sinks_ref, hout_hbm, os_ref, opre_ref, wout_hbm,
             kprev, vprev, *gather_sems):
        n = pl.program_id(0)

        @pl.when(n == 0)
        def _():
            kprev[...] = jnp.zeros_like(kprev)
            vprev[...] = jnp.zeros_like(vprev)

        gather = _Gather(hout_hbm, wout_hbm, *gather_sems, chunks=WEIGHT_CHUNKS)
        for step, phase in zip(steps, (gather.start, gather.pass_on, gather.relay_diagonal, gather.finish)):
            pl.when(n == step)(phase)

        lo2, _, first_half, prev_mask = _swa_masks()
        prev_mask_b = jnp.where(prev_mask, 1.0, 0.0).astype(BF)
        sink = _sink_row(sinks_ref)
        blocks = range(nq)
        rows = [slice(j * SWA_BLOCK, (j + 1) * SWA_BLOCK) for j in blocks]
        cosb = [cos_ref[rows[j], :] for j in blocks]
        sinb = [sin_ref[rows[j], :] for j in blocks]
        kc = [_rope(sk_ref[rows[j], :], cosb[j], sinb[j], first_half) for j in blocks]
        vc = [sv_ref[rows[j], :] for j in blocks]
        kcat = [jnp.concatenate([kprev[...] if j == 0 else kc[j - 1], kc[j]], axis=0) for j in blocks]
        vcat = [jnp.concatenate([vprev[...] if j == 0 else vc[j - 1], vc[j]], axis=0) for j in blocks]
        kprev[...] = kc[-1]
        vprev[...] = vc[-1]
        kvar = [_kv_variants(kcat[j], lo2) for j in blocks]
        vtvar = [_kv_variants_t(vcat[j]) for j in blocks]
        qg = [[q.astype(BF) for q in _swa_queries(sq_ref, rows[j], cosb[j], sinb[j], first_half)] for j in blocks]
        st = [_swa_scores(qg[j], *kvar[j]) for j in blocks]
        soft = [_swa_softmax(st[j], prev_mask, _no_prev_bias(n) if j == 0 else None, sink) for j in blocks]
        pt = [_split_band(soft[j][0].astype(BF), prev_mask_b) for j in blocks]
        og = {}
        for j in blocks:
            inv = soft[j][2]
            for g in range(2):
                c0, c1, c2 = 512 * g, 512 * g + 256, 512 * g + 512
                ot = (_dot(vtvar[j][0][g], pt[j][:, c0:c1]) * inv[:, c0:c1]
                      + _dot(vtvar[j][1][g], pt[j][:, c1:c2]) * inv[:, c1:c2])
                og[(j, g)] = ot.T
        for j in blocks:
            for g in range(2):
                for i in range(2):
                    ls = slice((2 * g + i) * LANES, (2 * g + i + 1) * LANES)
                    o = og[(j, g)][i * SWA_BLOCK:(i + 1) * SWA_BLOCK]
                    sz = sz_ref[rows[j], ls]
                    opre_ref[rows[j], ls] = o
                    os_ref[rows[j], ls] = (o * (sz * _sigmoid(sz))).astype(os_ref.dtype)

    def col(width, off):
        return pl.BlockSpec((tq, width), lambda i: (i, off // width))

    row = pl.BlockSpec((tq, LANES), lambda i: (i, 0))
    return pl.pallas_call(
        body, name="swa_fwd", grid=(s // tq,),
        in_specs=[col(512, OFF_SQ), col(512, OFF_SZ), col(LANES, OFF_SK), col(LANES, OFF_SV), row, row,
                  pl.BlockSpec(memory_space=pltpu.SMEM), pl.BlockSpec(memory_space=pl.ANY)],
        out_specs=[pl.BlockSpec((tq, 512), lambda i: (i, 0))] * 2 + [pl.BlockSpec(memory_space=pl.ANY)],
        out_shape=[jax.ShapeDtypeStruct((s, 512), BF), jax.ShapeDtypeStruct((s, 512), F32),
                   jax.ShapeDtypeStruct((8,) + half_out.shape, half_out.dtype)],
        scratch_shapes=[pltpu.VMEM((SWA_BLOCK, LANES), F32)] * 2 + _gather_sems(WEIGHT_CHUNKS),
        compiler_params=_params(("arbitrary",)),
    )(proj, proj, proj, proj, cos, sin, sinks, half_out)


def _swa_bwd(proj, dos, opre, cos, sin, sinks, dw_out_parts):
    s = proj.shape[0]
    nq = min(SWA_QBLOCKS, s // SWA_BLOCK)
    tq = nq * SWA_BLOCK
    steps = _phase_steps(s // tq, 5)
    _, r_out, c_out = dw_out_parts.shape

    def body(sq_ref, sz_ref, sk_ref, sv_ref, dos_ref, opre_ref, cos_ref, sin_ref, sinks_ref, pout_hbm,
             dsq_ref, dsz_ref, dsk_ref, dsv_ref, dsink_ref, gout_hbm, kprev, vprev, cprev, sprev, *reduce_scratch):
        n = pl.program_id(0)

        @pl.when(n == 0)
        def _():
            kprev[...] = jnp.zeros_like(kprev)
            vprev[...] = jnp.zeros_like(vprev)
            cprev[...] = jnp.zeros_like(cprev)
            sprev[...] = jnp.zeros_like(sprev)
            for hd in range(SWA_HEADS):
                dsink_ref[0, hd] = 0.0

        reduce = _Reduce(pout_hbm, gout_hbm, *reduce_scratch)
        phases = (reduce.start, reduce.combine_and_send, reduce.send_joint, reduce.total_and_share, reduce.finish)
        for step, phase in zip(steps, phases):
            pl.when(n == step)(phase)

        lo2, lo1, first_half, prev_mask = _swa_masks()
        prev_mask_b = jnp.where(prev_mask, 1.0, 0.0).astype(BF)
        lo1s = jnp.concatenate([lo1, lo1], axis=0)
        sink = _sink_row(sinks_ref)

        def home(m0, m1):
            t0 = m0 + pltpu.roll(m0, 64, 1)
            t1 = m1 + pltpu.roll(m1, 64, 1)
            return jnp.where(lo2, t0, t1)

        kp, vp, cp_, sp_ = kprev[...], vprev[...], cprev[...], sprev[...]
        for j in range(nq):
            rows = slice(j * SWA_BLOCK, (j + 1) * SWA_BLOCK)
            blk = n * nq + j
            cosb, sinb = cos_ref[rows, :], sin_ref[rows, :]
            kc = _rope(sk_ref[rows, :], cosb, sinb, first_half)
            vc = sv_ref[rows, :]
            kcat = jnp.concatenate([kp, kc], axis=0)
            k_lo, k_hi = _kv_variants(kcat, lo2)
            kt_lo, kt_hi = _kv_variants_t(kcat)
            v_lo, v_hi = _kv_variants(jnp.concatenate([vp, vc], axis=0), lo2)
            qg32 = _swa_queries(sq_ref, rows, cosb, sinb, first_half)
            qg = [q.astype(BF) for q in qg32]
            ex, es, inv = _swa_softmax(_swa_scores(qg, k_lo, k_hi), prev_mask, _no_prev_bias(n) if j == 0 else None, sink)
            pr, ps = ex * inv, es * inv

            dog32 = []
            for g in range(2):
                parts = []
                for i in range(2):
                    ls = slice((2 * g + i) * LANES, (2 * g + i + 1) * LANES)
                    sz = sz_ref[rows, ls]
                    sg = _sigmoid(sz)
                    dos_p = dos_ref[rows, ls]
                    dsz_ref[rows, ls] = (dos_p * opre_ref[rows, ls] * (sg * (1.0 + sz * (1.0 - sg)))).astype(dsz_ref.dtype)
                    parts.append(dos_p * (sz * sg))
                dog32.append(jnp.concatenate(parts, axis=0))
            dog = [t.astype(BF) for t in dog32]
            dpr = _merge_band(jnp.concatenate([_dot(v_lo[0], dog[0], NT), _dot(v_hi[0], dog[0], NT),
                                               _dot(v_lo[1], dog[1], NT), _dot(v_hi[1], dog[1], NT)], axis=1), prev_mask)
            rd = jnp.sum(pr * dpr, axis=0, keepdims=True)
            ds = _split_band((pr * (dpr - rd)).astype(BF), prev_mask_b)
            prb = _split_band(pr.astype(BF), prev_mask_b)
            sink_term = ps * rd
            for r, hd in enumerate(_SWA_COL_HEADS):
                dsink_ref[0, hd] += -jnp.sum(sink_term[:, r * SWA_BLOCK:(r + 1) * SWA_BLOCK])

            dk_g, dv_g = [], []
            for g in range(2):
                c0, c1, c2 = 512 * g, 512 * g + 256, 512 * g + 512
                dq = (_dot(kt_lo[g], ds[:, c0:c1]) + _dot(kt_hi[g], ds[:, c1:c2])).T
                for i in range(2):
                    ls = slice((2 * g + i) * LANES, (2 * g + i + 1) * LANES)
                    dsq_ref[rows, ls] = _rope_t(dq[i * SWA_BLOCK:(i + 1) * SWA_BLOCK] * 0.125, cosb, sinb,
                                                first_half).astype(dsq_ref.dtype)
                q_split = jnp.concatenate([jnp.where(lo1s, qg32[g], 0.0), jnp.where(lo1s, 0.0, qg32[g])], axis=0).astype(BF)
                do_split = jnp.concatenate([jnp.where(lo1s, dog32[g], 0.0), jnp.where(lo1s, 0.0, dog32[g])], axis=0).astype(BF)
                dk_g.append(_dot(ds[:, c0:c2], q_split))
                dv_g.append(_dot(prb[:, c0:c2], do_split))
            dk = home(dk_g[0], dk_g[1])
            dv = home(dv_g[0], dv_g[1])
            cur = pl.ds(pl.multiple_of(blk * SWA_BLOCK, SWA_BLOCK), SWA_BLOCK)
            dsk_ref[cur, :] = _rope_t(dk[SWA_BLOCK:], cosb, sinb, first_half)
            dsv_ref[cur, :] = dv[SWA_BLOCK:]
            dk_prev = _rope_t(dk[:SWA_BLOCK], cp_, sp_, first_half)
            dv_prev = dv[:SWA_BLOCK]
            if j == 0:
                @pl.when(n > 0)
                def _():
                    prv = pl.ds(pl.multiple_of((blk - 1) * SWA_BLOCK, SWA_BLOCK), SWA_BLOCK)
                    dsk_ref[prv, :] += dk_prev
                    dsv_ref[prv, :] += dv_prev
            else:
                prv = pl.ds(pl.multiple_of((blk - 1) * SWA_BLOCK, SWA_BLOCK), SWA_BLOCK)
                dsk_ref[prv, :] += dk_prev
                dsv_ref[prv, :] += dv_prev
            kp, vp, cp_, sp_ = kc, vc, cosb, sinb
        kprev[...] = kp
        vprev[...] = vp
        cprev[...] = cp_
        sprev[...] = sp_

    def col(width, off):
        return pl.BlockSpec((tq, width), lambda i: (i, off // width))

    row = pl.BlockSpec((tq, LANES), lambda i: (i, 0))
    wide = pl.BlockSpec((tq, 512), lambda i: (i, 0))
    return pl.pallas_call(
        body, name="swa_bwd", grid=(s // tq,),
        in_specs=[col(512, OFF_SQ), col(512, OFF_SZ), col(LANES, OFF_SK), col(LANES, OFF_SV), wide, wide, row, row,
                  pl.BlockSpec(memory_space=pltpu.SMEM), pl.BlockSpec(memory_space=pl.ANY)],
        out_specs=[wide, wide, _full((s, LANES)), _full((s, LANES)), pl.BlockSpec(memory_space=pltpu.SMEM),
                   pl.BlockSpec(memory_space=pl.ANY)],
        out_shape=[jax.ShapeDtypeStruct((s, 512), BF), jax.ShapeDtypeStruct((s, 512), BF),
                   jax.ShapeDtypeStruct((s, LANES), F32), jax.ShapeDtypeStruct((s, LANES), F32),
                   jax.ShapeDtypeStruct((1, SWA_HEADS), F32), jax.ShapeDtypeStruct((r_out, c_out), F32)],
        scratch_shapes=[pltpu.VMEM((SWA_BLOCK, LANES), F32)] * 4 + _reduce_scratch(r_out, c_out),
        compiler_params=_params(("arbitrary",)),
    )(proj, proj, proj, proj, dos, opre, cos, sin, sinks, dw_out_parts)


def _outproj(og, osw, w_out, x2d, target, gate, g_final):
    s = x2d.shape[0]
    tm = min(512, s)

    def body(og_ref, os_ref, w_ref, x_ref, t_ref, gate_ref, gf_ref,
             dx2_ref, dog_ref, dos_ref, dw_ref, loss_ref, dgf_ref, dgate_ref):
        @pl.when(pl.program_id(0) == 0)
        def _():
            dw_ref[...] = jnp.zeros_like(dw_ref)
            loss_ref[...] = jnp.zeros_like(loss_ref)
            dgf_ref[...] = jnp.zeros_like(dgf_ref)
            dgate_ref[...] = jnp.zeros_like(dgate_ref)

        w = w_ref[...]
        gate, gf = gate_ref[...], gf_ref[...]
        subs = _subtiles(tm)
        ogv = [og_ref[sl, :] for sl in subs]
        osv = [os_ref[sl, :] for sl in subs]
        y = [_dot(ogv[k], w[:512]) + _dot(osv[k], w[512:]) for k in range(len(subs))]
        dys = []
        for k, sl in enumerate(subs):
            x2 = x_ref[sl, :] + gate * y[k]
            r = lax.rsqrt(jnp.mean(x2 * x2, axis=-1, keepdims=True) + RMS_EPS)
            xn = x2 * r
            err = xn * gf - t_ref[sl, :]
            loss_ref[...] += 0.5 * jnp.sum(jnp.mean(err * err, axis=-1, keepdims=True), axis=0, keepdims=True)
            dyf = err * (1.0 / D_MODEL)
            dgf_ref[...] += jnp.sum(dyf * xn, axis=0, keepdims=True)
            t = dyf * gf
            dx2 = r * (t - xn * jnp.mean(t * xn, axis=-1, keepdims=True))
            dx2_ref[sl, :] = dx2
            dgate_ref[...] += jnp.sum(dx2 * y[k], axis=0, keepdims=True)
            dys.append((dx2 * gate).astype(BF))
            dmix = _dot(dys[k], w, NT)
            dog_ref[sl, :] = dmix[:, :512]
            dos_ref[sl, :] = dmix[:, 512:]
        dy = jnp.concatenate(dys, axis=0)
        dw_ref[:512, :] += _dot(og_ref[...], dy, TN)
        dw_ref[512:, :] += _dot(os_ref[...], dy, TN)

    half = pl.BlockSpec((tm, 512), lambda i: (i, 0))
    rowb = pl.BlockSpec((tm, D_MODEL), lambda i: (i, 0))
    vec = _full((1, D_MODEL))
    return pl.pallas_call(
        body, name="outproj", grid=(s // tm,),
        in_specs=[half, half, _full((D_MODEL, D_MODEL)), rowb, rowb, vec, vec],
        out_specs=[rowb, half, half, _full((D_MODEL, D_MODEL)), _full((1, 1)), vec, vec],
        out_shape=[jax.ShapeDtypeStruct((s, D_MODEL), F32), jax.ShapeDtypeStruct((s, 512), F32),
                   jax.ShapeDtypeStruct((s, 512), F32), jax.ShapeDtypeStruct((D_MODEL, D_MODEL), F32),
                   jax.ShapeDtypeStruct((1, 1), F32), jax.ShapeDtypeStruct((1, D_MODEL), F32),
                   jax.ShapeDtypeStruct((1, D_MODEL), F32)],
        compiler_params=_params(("arbitrary",)),
    )(og, osw, w_out, x2d, target, gate, g_final)


_PIECES = ((OFF_QK, 512), (OFF_V, 512), (OFF_GZ, 512), (OFF_SQ, 512), (OFF_SZ, 512),
           (OFF_SK, LANES), (OFF_SV, LANES), (OFF_GA, LANES))

_UNPAD_ROWS = ((OFF_QK, 0, 1024),
               (OFF_GA, 1024, GLA_RANK),
               (OFF_GZ, 1040, 1024),
               (OFF_SK, 2064, 256),
               (OFF_SZ, 2320, 512))


def _inproj_bwd(x2d, shift, sc1p, g_norm, w_t, w_edges, dx2, pieces):
    s = x2d.shape[0]
    tm = min(512, s)
    nsteps = s // tm

    def body(x_ref, sh_ref, sc_ref, g_ref, w_hbm, edge_ref, dx2_ref, *rest):
        piece_refs = rest[:len(_PIECES)]
        gx_ref, dw_hbm, dsh_ref, dsc_ref, dg_ref, w_vm, dw_vm, in_sems, out_sems = rest[len(_PIECES):]
        i = pl.program_id(0)

        @pl.when(i == 0)
        def _():
            loads = _load_w_padded(w_hbm, edge_ref, w_vm, in_sems)
            dw_vm[...] = jnp.zeros_like(dw_vm)
            dsh_ref[...] = jnp.zeros_like(dsh_ref)
            dsc_ref[...] = jnp.zeros_like(dsc_ref)
            dg_ref[...] = jnp.zeros_like(dg_ref)
            for cp in loads:
                cp.wait()

        g, sc1p_v, shift_v = g_ref[...], sc_ref[...], sh_ref[...]
        subs = _subtiles(tm)
        dhs = []
        for sl in subs:
            dh = None
            for (off, width), pr in zip(_PIECES, piece_refs):
                part = _dot(pr[sl, :].astype(BF), w_vm[off:off + width, :])
                dh = part if dh is None else dh + part
            dhs.append(dh)
        norm = [_modnorm(x_ref[sl, :], g, sc1p_v, shift_v) for sl in subs]
        hb = jnp.concatenate([h.astype(BF) for _, _, h in norm], axis=0)
        for (off, width), pr in zip(_PIECES, piece_refs):
            dw_vm[off:off + width, :] += _dot(pr[...].astype(BF), hb, TN)
        for sl, (xn, r, _), dh in zip(subs, norm, dhs):
            dsh_ref[...] += jnp.sum(dh, axis=0, keepdims=True)
            dsc_ref[...] += jnp.sum(dh * (xn * g), axis=0, keepdims=True)
            dg_ref[...] += jnp.sum(dh * xn * sc1p_v, axis=0, keepdims=True)
            dxn = dh * g * sc1p_v
            gx_ref[sl, :] = dx2_ref[sl, :] + r * (dxn - xn * jnp.mean(dxn * xn, axis=-1, keepdims=True))

        @pl.when(i == nsteps - 1)
        def _():
            copies = [pltpu.make_async_copy(dw_vm.at[src:src + n], dw_hbm.at[dst:dst + n], out_sems.at[k])
                      for k, (src, dst, n) in enumerate(_UNPAD_ROWS)]
            for cp in copies:
                cp.start()
            for cp in copies:
                cp.wait()

    rowb = pl.BlockSpec((tm, D_MODEL), lambda i: (i, 0))
    vec = _full((1, D_MODEL))
    anyspec = pl.BlockSpec(memory_space=pl.ANY)
    piece_specs = [pl.BlockSpec((tm, width), lambda i: (i, 0)) for _, width in _PIECES]
    return pl.pallas_call(
        body, name="inproj_bwd", grid=(nsteps,),
        in_specs=[rowb, vec, vec, vec, anyspec, _full(w_edges.shape), rowb] + piece_specs,
        out_specs=[rowb, anyspec, vec, vec, vec],
        out_shape=[jax.ShapeDtypeStruct((s, D_MODEL), F32), jax.ShapeDtypeStruct((D_IN, D_MODEL), F32),
                   jax.ShapeDtypeStruct((1, D_MODEL), F32), jax.ShapeDtypeStruct((1, D_MODEL), F32),
                   jax.ShapeDtypeStruct((1, D_MODEL), F32)],
        scratch_shapes=[pltpu.VMEM((D_PAD, D_MODEL), BF), pltpu.VMEM((D_PAD, D_MODEL), F32),
                        pltpu.SemaphoreType.DMA((len(_w_load_plan(w_t.shape[0] // 4)[0]),)),
                        pltpu.SemaphoreType.DMA((len(_UNPAD_ROWS),))],
        compiler_params=_params(("arbitrary",)),
    )(x2d, shift, sc1p, g_norm, w_t, w_edges, dx2, *pieces)


def _adam(w, g, m, v):
    m2 = ADAM_B1 * m + (1.0 - ADAM_B1) * g
    v2 = ADAM_B2 * v + (1.0 - ADAM_B2) * (g * g)
    m_hat = m2 / (1.0 - ADAM_B1 ** ADAM_STEP)
    v_hat = v2 / (1.0 - ADAM_B2 ** ADAM_STEP)
    delta = -ADAM_LR * (m_hat / (jnp.sqrt(v_hat) + ADAM_EPS) + ADAM_WD * w)
    return delta, m2, v2


def _adamw(w, g, m, v, name):
    rr, cc = w.shape
    tc = min(512, cc)

    def body(w_ref, g_ref, m_ref, v_ref, d_ref, m2_ref, v2_ref):
        d_ref[...], m2_ref[...], v2_ref[...] = _adam(w_ref[...], g_ref[...], m_ref[...], v_ref[...])

    blk = pl.BlockSpec((rr, tc), lambda i: (0, i))
    return pl.pallas_call(
        body, name=name, grid=(cc // tc,), in_specs=[blk] * 4, out_specs=[blk] * 3,
        out_shape=[jax.ShapeDtypeStruct((rr, cc), F32)] * 3,
        compiler_params=_params(("arbitrary",)),
    )(w, g, m, v)


def _adamw_t(w3, g_window, m3, v3, name):
    rr, _, cc = w3.shape
    parts = [slice(q * (cc // 4), (q + 1) * (cc // 4)) for q in range(4)]
    starts = sorted({(rr * j) % 8 for j in range(4)})

    def body(w_hbm, gw_hbm, m_hbm, v_hbm, d_hbm, m2_hbm, v2_hbm, g3_hbm,
             w_vm, m_vm, v_vm, gw_vm, d_vm, m2_vm, v2_vm, g_vm, in_sems, out_sems):
        start = lax.rem(rr * (2 * lax.axis_index("x") + lax.axis_index("y")), 8)
        ins = ((w_hbm, w_vm), (m_hbm, m_vm), (v_hbm, v_vm))
        outs = ((d_vm, d_hbm), (m2_vm, m2_hbm), (v2_vm, v2_hbm), (g_vm, g3_hbm))
        loads = [[pltpu.make_async_copy(src.at[:, 0, p], dst.at[:, p], in_sems.at[4 * q + k]) for k, (src, dst) in enumerate(ins)]
                 + [pltpu.make_async_copy(gw_hbm.at[:, p], gw_vm.at[:, p], in_sems.at[4 * q + 3])]
                 for q, p in enumerate(parts)]
        stores = [[pltpu.make_async_copy(src.at[:, p], dst.at[:, 0, p], out_sems.at[4 * q + k]) for k, (src, dst) in enumerate(outs)]
                  for q, p in enumerate(parts)]
        for group in loads:
            for cp in group:
                cp.start()
        for q, p in enumerate(parts):
            for cp in loads[q]:
                cp.wait()
            g = gw_vm[starts[0]:starts[0] + rr, p]
            for o in starts[1:]:
                g = jnp.where(start == o, gw_vm[o:o + rr, p], g)
            g_vm[:, p] = g
            d_vm[:, p], m2_vm[:, p], v2_vm[:, p] = _adam(w_vm[:, p], g, m_vm[:, p], v_vm[:, p])
            for cp in stores[q]:
                cp.start()
        for group in stores:
            for cp in group:
                cp.wait()

    hbm = pl.BlockSpec(memory_space=pl.ANY)
    return pl.pallas_call(
        body, name=name, grid=(1,), in_specs=[hbm] * 4,
        out_specs=[hbm] * 4, out_shape=[jax.ShapeDtypeStruct((rr, 1, cc), F32)] * 4,
        scratch_shapes=[pltpu.VMEM((rr, cc), F32)] * 3 + [pltpu.VMEM(g_window.shape, F32)] + [pltpu.VMEM((rr, cc), F32)] * 4
        + [pltpu.SemaphoreType.DMA((16,)), pltpu.SemaphoreType.DMA((16,))],
        compiler_params=_params(("arbitrary",)),
    )(w3, g_window, m3, v3)


def _ada_update(c_all, dmod_cols, w, m, v):
    rr, cc = w.shape
    tr = min(512, rr)
    c_all = jnp.pad(c_all, ((0, 8), (0, 0)))
    dmod_cols = jnp.pad(dmod_cols, ((0, 8), (0, 0)))

    def body(c_ref, dm_ref, w_ref, m_ref, v_ref, g_ref, d_ref, m2_ref, v2_ref):
        cv = c_ref[...]
        sc = (cv * _sigmoid(cv)).astype(BF)
        g = _dot(sc, dm_ref[...].astype(BF), TN)
        g_ref[...] = g
        d_ref[...], m2_ref[...], v2_ref[...] = _adam(w_ref[...], g, m_ref[...], v_ref[...])

    blk = pl.BlockSpec((tr, cc), lambda i: (i, 0))
    return pl.pallas_call(
        body, name="ada_update", grid=(rr // tr,),
        in_specs=[pl.BlockSpec((16, tr), lambda i: (0, i)), _full((16, cc)), blk, blk, blk],
        out_specs=[blk] * 4, out_shape=[jax.ShapeDtypeStruct((rr, cc), F32)] * 4,
        compiler_params=_params(("arbitrary",)),
    )(c_all, dmod_cols, w, m, v)


def _small_update(parts, weights, moms, vels):
    n = len(weights)

    def body(*refs):
        p_refs, w_refs, m_refs, v_refs = refs[:n + 1], refs[n + 1:2 * n + 1], refs[2 * n + 1:3 * n + 1], refs[3 * n + 1:4 * n + 1]
        outs = refs[4 * n + 1:]
        for i in range(n):
            g = p_refs[i][0]
            for d in range(1, 8):
                g = g + p_refs[i][d]
            delta, m2, v2 = _adam(w_refs[i][...], g, m_refs[i][...], v_refs[i][...])
            outs[4 * i][...] = g
            outs[4 * i + 1][...] = delta
            outs[4 * i + 2][...] = m2
            outs[4 * i + 3][...] = v2
        tot = p_refs[n][0]
        for d in range(1, 8):
            tot = tot + p_refs[n][d]
        outs[4 * n][...] = tot

    out_shape = []
    for w in weights:
        out_shape += [jax.ShapeDtypeStruct(w.shape, F32)] * 4
    out_shape.append(jax.ShapeDtypeStruct(parts[n].shape[1:], F32))
    return pl.pallas_call(body, name="small_update", out_shape=out_shape, compiler_params=_params())(
        *parts, *weights, *moms, *vels)


def _rows8(a):
    flat = a.reshape(-1)
    rows = -(-flat.shape[0] // LANES)
    rows8 = -(-rows // 8) * 8
    flat = jnp.pad(flat, (0, rows8 * LANES - flat.shape[0]))
    return flat.reshape(rows8, LANES)


def kernel(x, c, positions, w_ada, b_ada, g_norm, w_in, w_decay, b_decay, g_gla_head, sinks, w_out, g_final, loss_target, m_w_ada, m_b_ada, m_g_norm, m_w_in, m_w_decay, m_b_decay, m_g_gla_head, m_sinks, m_w_out, m_g_final, v_w_ada, v_b_ada, v_g_norm, v_w_in, v_w_decay, v_b_decay, v_g_gla_head, v_sinks, v_w_out, v_g_final):
    ax, ay, ac = lax.axis_index("x"), lax.axis_index("y"), lax.axis_index("c")
    chip = 2 * ax + ay
    dev = 2 * chip + ac
    s = x.shape[1]
    x2d = x[0]
    target = loss_target[0]
    w_ada2, w_out2, w_dec2 = w_ada[0], w_out[0], w_decay[0]
    w_in_t = w_in[0].T
    ada_cols = w_ada2.shape[1]
    in_cols = w_in_t.shape[0]
    out_rows = w_out2.shape[0]
    half = D_MODEL // 2

    cw = jnp.concatenate([c.reshape(8, LANES), w_dec2.reshape(8, LANES)], axis=0)
    b_shard = lax.dynamic_slice(b_ada, (0, chip * ada_cols), (1, ada_cols))
    half_out = lax.dynamic_slice(w_out2, (ac * (out_rows // 2), 0), (out_rows // 2, D_MODEL)).astype(BF)
    inv_freq = 1.0 / (ROPE_THETA ** (jnp.arange(0, 64, 2, dtype=F32) / 64))
    room = _w_window(in_cols) - in_cols
    win_window = lax.dynamic_slice(jnp.pad(w_in_t, ((room, room), (0, 0))), (room - (in_cols * chip) % W_TILE, ac * half),
                                   (_w_window(in_cols), half)).astype(BF)
    win_edges = jnp.stack([win_window[:W_TILE], win_window[-W_TILE:]])
    first, mod_all, w_t, w_edges, cos, sin = _prologue(
        cw, w_ada2, b_shard, win_window, win_edges, in_cols, positions.reshape(s // LANES, LANES), jnp.tile(inv_freq, 4).reshape(1, LANES))

    first = first.reshape(8, 2, 8, LANES)
    c_all = first[:, 0].reshape(8, D_MODEL)
    w_dec_full = first[0::2, 1].reshape(4, GLA_RANK, 64).transpose(1, 0, 2).reshape(GLA_RANK, 256)
    mod = mod_all.reshape(4, 2, 8, ada_cols)[:, 0]
    mod = lax.dynamic_slice(mod, (0, dev, 0), (4, 1, ada_cols)).reshape(1, 4 * ada_cols)
    shift, sc1p, gate = mod[:, :D_MODEL], 1.0 + mod[:, D_MODEL:2 * D_MODEL], mod[:, 2 * D_MODEL:]
    wdecp = jnp.pad(w_dec_full, ((0, LANES - GLA_RANK), (0, 0))).astype(BF)

    proj = _inproj_fwd(x2d, shift, sc1p, g_norm, w_t, w_edges)
    og, o_gla, sprev = _gla_fwd(proj, wdecp, b_decay, g_gla_head)
    osw, o_swa, w_out_all = _swa_fwd(proj, cos, sin, sinks, half_out)
    w_out_all = w_out_all.reshape(D_MODEL, D_MODEL)
    dx2, dog, dos, dw_out, loss_p, dgf, dgate = _outproj(og, osw, w_out_all, x2d, target, gate, g_final.reshape(1, D_MODEL))
    dsq, dsz, dsk, dsv, dsinks, g_w_out = _swa_bwd(proj, dos, o_swa, cos, sin, sinks, dw_out.reshape(4, out_rows, D_MODEL))
    dqk, dv, dgz, dga, dwdp, dbd, dgg = _gla_bwd(proj, dog, o_gla, sprev, wdecp, b_decay, g_gla_head)
    pieces = (dqk, dv, dgz, dsq, dsz, dsk, dsv, dga)
    gx, dw_in_t, dshift, dscale, dgn = _inproj_bwd(x2d, shift, sc1p, g_norm, w_t, w_edges, dx2, pieces)

    segs = [jnp.concatenate([dshift, dscale, dgate], axis=1), dgn, dgf, dwdp[:GLA_RANK], dbd, dgg, dsinks, loss_p]
    packed = [_rows8(a) for a in segs]
    offs = [0]
    for a in packed:
        offs.append(offs[-1] + a.shape[0])
    g_window, small = _epilogue(dw_in_t, jnp.concatenate(packed, axis=0))

    def seg(i, size):
        return small[:, offs[i]:offs[i + 1]].reshape(8, -1)[:, :size]

    dmod_all = seg(0, 3 * D_MODEL)
    dwd_all = lax.dynamic_slice(seg(3, GLA_RANK * 256).reshape(8, GLA_RANK, 256), (0, 0, chip * 64), (8, GLA_RANK, 64))
    parts = [dmod_all.reshape(8, 1, 3 * D_MODEL), seg(1, D_MODEL).reshape(8, 1, D_MODEL), dwd_all,
             seg(4, 256).reshape(8, 1, 256), seg(5, 512).reshape(8, 1, 512), seg(6, SWA_HEADS).reshape(8, 1, SWA_HEADS),
             seg(2, D_MODEL).reshape(8, 1, D_MODEL), seg(7, LANES).reshape(8, 1, LANES)]
    smalls = _small_update(
        parts,
        [b_ada, g_norm, w_dec2, b_decay, g_gla_head, sinks, g_final.reshape(1, D_MODEL)],
        [m_b_ada, m_g_norm, m_w_decay[0], m_b_decay, m_g_gla_head, m_sinks, m_g_final.reshape(1, D_MODEL)],
        [v_b_ada, v_g_norm, v_w_decay[0], v_b_decay, v_g_gla_head, v_sinks, v_g_final.reshape(1, D_MODEL)])
    (g_b_ada, d_b_ada, nm_b_ada, nv_b_ada, g_gn, d_gn, nm_gn, nv_gn, g_wd, d_wd, nm_wd, nv_wd,
     g_bd, d_bd, nm_bd, nv_bd, g_gg, d_gg, nm_gg, nv_gg, g_sk, d_sk, nm_sk, nv_sk,
     g_gf, d_gf, nm_gf, nv_gf, loss_row) = smalls
    loss = loss_row[0, 0]

    dmod_cols = lax.dynamic_slice(dmod_all, (0, chip * ada_cols), (8, ada_cols))
    g_w_ada, d_w_ada, nm_w_ada, nv_w_ada = _ada_update(c_all, dmod_cols, w_ada2, m_w_ada[0], v_w_ada[0])
    to3 = lambda a: jnp.transpose(a, (2, 0, 1))
    from3 = lambda a: jnp.transpose(a, (1, 2, 0))[0]
    d3, nm3, nv3, g3 = _adamw_t(to3(w_in), g_window, to3(m_w_in), to3(v_w_in), "adamw_w_in")
    g_w_in, d_w_in, nm_w_in, nv_w_in = from3(g3), from3(d3), from3(nm3), from3(nv3)
    d_w_out, nm_w_out, nv_w_out = _adamw(w_out2, g_w_out, m_w_out[0], v_w_out[0], "adamw_w_out")

    flat = lambda a: a.reshape(D_MODEL)
    grads = [g_w_ada[None], g_b_ada, g_gn, g_w_in[None], g_wd[None], g_bd, g_gg, g_sk, g_w_out[None], flat(g_gf)]
    deltas = [d_w_ada[None], d_b_ada, d_gn, d_w_in[None], d_wd[None], d_bd, d_gg, d_sk, d_w_out[None], flat(d_gf)]
    new_m = [nm_w_ada[None], nm_b_ada, nm_gn, nm_w_in[None], nm_wd[None], nm_bd, nm_gg, nm_sk, nm_w_out[None], flat(nm_gf)]
    new_v = [nv_w_ada[None], nv_b_ada, nv_gn, nv_w_in[None], nv_wd[None], nv_bd, nv_gg, nv_sk, nv_w_out[None], flat(nv_gf)]
    return (loss, gx[None], *grads, *deltas, *new_m, *new_v)
```

```python
import jax
import jax.numpy as jnp
from jax import lax
from jax.experimental import pallas as pl
from jax.experimental.pallas import tpu as pltpu

F32 = jnp.float32
BF = jnp.bfloat16

D_MODEL = 1024
GLA_HEADS = 4
GLA_DK = 64
GLA_CHUNK = 64
GLA_RANK = 16
GLA_TAU = 16.0
GLA_SUB = 256
GLA_ROWS_FWD = 1024
GLA_ROWS_BWD = 512
SWA_HEADS = 8
SWA_BLOCK = 128
SWA_QBLOCKS_FWD = 8
SWA_QBLOCKS = 8
RMS_EPS = 1e-6
ROPE_THETA = 10000.0

OFF_QK, OFF_V, OFF_GZ, OFF_SQ, OFF_SZ, OFF_SK, OFF_SV, OFF_GA = 0, 512, 1024, 1536, 2048, 2560, 2688, 2816
D_PAD = 2944
D_IN = 2832
LANES = 128
VMEM_LIMIT = 56 * 1024 * 1024

ADAM_LR, ADAM_B1, ADAM_B2, ADAM_EPS, ADAM_WD, ADAM_STEP = 0.001, 0.9, 0.999, 1e-08, 0.01, 10

NT = (((1,), (1,)), ((), ()))
TN = (((0,), (0,)), ((), ()))
MESH = pl.DeviceIdType.MESH


def _dot(a, b, dims=None):
    if dims is None:
        return jnp.dot(a, b, preferred_element_type=F32)
    return lax.dot_general(a, b, dims, preferred_element_type=F32)


def _sigmoid(x):
    return 1.0 / (1.0 + jnp.exp(-x))


def _params(sem=None):
    return pltpu.CompilerParams(dimension_semantics=sem, vmem_limit_bytes=VMEM_LIMIT)


def _full(shape):
    return pl.BlockSpec(shape, lambda i: (0,) * len(shape))


def _subtiles(rows, size=256):
    size = min(size, rows)
    return [slice(k * size, (k + 1) * size) for k in range(rows // size)]


WEIGHT_CHUNKS = 4


def _gather_sems(chunks=1):
    return [pltpu.SemaphoreType.DMA((7 * chunks,)), pltpu.SemaphoreType.DMA((7 * chunks,)), pltpu.SemaphoreType.DMA]


_GATHER_SEMS = _gather_sems()


class _Gather:
    def __init__(self, x_ref, out_ref, send_sems, recv_sems, local_sem, slab=None, chunks=1):
        self.slab_of = slab
        self.chunks = chunks
        self.width = x_ref.shape[-1] // chunks
        x, y, c = lax.axis_index("x"), lax.axis_index("y"), lax.axis_index("c")
        self.me, self.sibling, self.c = (x, y, c), (x, y, 1 - c), c
        self.xn, self.yn, self.dg = (1 - x, y), (x, 1 - y), (1 - x, 1 - y)
        self.pass_from = (lax.rem(x + 1 - c, 2), lax.rem(y + c, 2))
        self.pass_to = (lax.rem(x + c, 2), lax.rem(y + 1 - c, 2))
        self.x_ref, self.out_ref, self.send_sems, self.recv_sems = x_ref, out_ref, send_sems, recv_sems
        self.mine = pltpu.make_async_copy(x_ref, self._slab(*self.me), local_sem)

    def _slab(self, px, py, pc):
        if self.slab_of is not None:
            return self.slab_of(self.out_ref, px, py, pc)
        return self.out_ref.at[4 * px + 2 * py + pc]

    def _part(self, ref, q):
        if self.chunks == 1:
            return ref
        lanes = slice(q * self.width, (q + 1) * self.width)
        return ref.at[(slice(None),) * (len(ref.shape) - 1) + (lanes,)]

    def _copy(self, k, q, blk, to, src=None):
        i = k * self.chunks + q
        return pltpu.make_async_remote_copy(
            src_ref=self._part(self._slab(*blk) if src is None else src, q), dst_ref=self._part(self._slab(*blk), q),
            send_sem=self.send_sems.at[i], recv_sem=self.recv_sems.at[i], device_id=to, device_id_type=MESH)

    def _sends(self, q):
        c = self.c
        return [self._copy(0, q, self.me, self.sibling, src=self.x_ref),
                self._copy(1, q, self.me, (*self.xn, c), src=self.x_ref),
                self._copy(2, q, self.me, (*self.yn, c), src=self.x_ref),
                self._copy(3, q, (*self.pass_from, c), (*self.pass_to, c)),
                self._copy(4, q, (*self.xn, c), self.sibling),
                self._copy(5, q, (*self.yn, c), self.sibling),
                self._copy(6, q, (*self.dg, c), self.sibling)]

    def start(self):
        self.mine.start()
        for q in range(self.chunks):
            sends = self._sends(q)
            for k in (1, 2, 0):
                sends[k].start()

    def pass_on(self, only=None):
        for q in range(self.chunks) if only is None else (only,):
            sends = self._sends(q)
            self._copy(1, q, (*self.xn, self.c), self.me).wait_recv()
            self._copy(2, q, (*self.yn, self.c), self.me).wait_recv()
            for k in (3, 4, 5):
                sends[k].start()

    def relay_diagonal(self, only=None):
        for q in range(self.chunks) if only is None else (only,):
            self._copy(3, q, (*self.dg, self.c), self.me).wait_recv()
            self._sends(q)[6].start()

    def relay(self):
        self.pass_on()
        self.relay_diagonal()

    def finish(self):
        c = self.c
        for q in range(self.chunks):
            self._copy(0, q, self.sibling, self.me).wait_recv()
            for k, chip in ((4, self.xn), (5, self.yn), (6, self.dg)):
                self._copy(k, q, (*chip, 1 - c), self.me).wait_recv()
            for cp in self._sends(q):
                cp.wait_send()
        self.mine.wait()


def _prologue(cw, w_ada, b_shard, win_window, win_edges, n_in, pos_rows, inv_freq):
    s = pos_rows.shape[0] * LANES
    rt = min(512, s)
    inner = win_window.shape[0] - 2 * W_TILE

    def body(cw_ref, wada_hbm, b_ref, hin_ref, hedge_ref, pos_ref, f_ref,
             first_ref, mod_ref, win_ref, edge_ref, cos_hbm, sin_hbm,
             mod_blk, cos_ref, sin_ref, wada_ref, table_sems, local_sems, *sems):
        fetch_w = pltpu.make_async_copy(wada_hbm, wada_ref, local_sems.at[0])
        fetch_w.start()
        g_c = _Gather(cw_ref, first_ref, *sems[0:3])
        half_lanes = hin_ref.shape[1]

        def lanes_of(pc):
            return pl.ds(pl.multiple_of(pc * half_lanes, half_lanes), half_lanes)

        def inner_rows(px, py):
            return pl.ds(pl.multiple_of((n_in * (2 * px + py)) // W_TILE * W_TILE + W_TILE, W_TILE), inner)

        g_in = _Gather(hin_ref.at[pl.ds(W_TILE, inner), :], win_ref, *sems[3:6], chunks=WEIGHT_CHUNKS,
                       slab=lambda ref, px, py, pc: ref.at[inner_rows(px, py), lanes_of(pc)])
        g_mod = _Gather(mod_blk, mod_ref, *sems[6:9])
        g_edge = _Gather(hedge_ref, edge_ref, *sems[9:12],
                         slab=lambda ref, px, py, pc: ref.at[2 * px + py, :, :, lanes_of(pc)])
        g_c.start()
        g_edge.start()
        g_in.start()
        g_c.relay()
        g_edge.relay()
        g_c.finish()
        c_rows = [jnp.concatenate([first_ref[d, r:r + 1, :] for r in range(8)], axis=1) for d in range(8)]
        c_all = jnp.concatenate(c_rows, axis=0)
        sc = (c_all * _sigmoid(c_all)).astype(BF)
        fetch_w.wait()
        mod_blk[...] = _dot(sc, wada_ref[...].astype(BF)) + b_ref[...]
        g_mod.start()

        def rope_rows(i, carry):
            rows = pl.ds(pl.multiple_of(i * rt, rt), rt)
            cols = [jnp.transpose(jnp.broadcast_to(pos_ref[pl.ds(i * (rt // LANES) + b, 1), :].astype(F32), (LANES, LANES)))
                    for b in range(rt // LANES)]
            ang = jnp.concatenate(cols, axis=0) * f_ref[...]
            lane = lax.broadcasted_iota(jnp.int32, ang.shape, 1)
            cos_ref[rows, :] = jnp.cos(ang)
            sn = jnp.sin(ang)
            sin_ref[rows, :] = jnp.where((lane % 64) < 32, -sn, sn)
            pltpu.make_async_copy(cos_ref.at[rows, :], cos_hbm.at[rows, :], table_sems.at[0]).start()
            pltpu.make_async_copy(sin_ref.at[rows, :], sin_hbm.at[rows, :], table_sems.at[1]).start()
            return carry

        waits = ([lambda q=q: g_in.pass_on(q) for q in range(WEIGHT_CHUNKS)]
                 + [lambda q=q: g_in.relay_diagonal(q) for q in range(WEIGHT_CHUNKS)] + [g_mod.relay])
        steps = s // rt
        lead = steps // 4
        per_wait = max((steps - lead) // len(waits), 1)
        lax.fori_loop(0, lead, rope_rows, 0)
        done = lead
        for wait in waits:
            wait()
            nxt = min(done + per_wait, steps)
            lax.fori_loop(done, nxt, rope_rows, 0)
            done = nxt
        lax.fori_loop(done, steps, rope_rows, 0)
        g_in.finish()
        g_mod.finish()
        g_edge.finish()
        pltpu.make_async_copy(cos_ref, cos_hbm, table_sems.at[0]).wait()
        pltpu.make_async_copy(sin_ref, sin_hbm, table_sems.at[1]).wait()

    vm = pl.BlockSpec(memory_space=pltpu.VMEM)
    hbm = pl.BlockSpec(memory_space=pl.ANY)
    half_lanes = win_window.shape[1]
    return pl.pallas_call(
        body, name="prologue",
        out_shape=[jax.ShapeDtypeStruct((8,) + cw.shape, F32), jax.ShapeDtypeStruct((8, 8, w_ada.shape[1]), F32),
                   jax.ShapeDtypeStruct((4 * n_in, 2 * half_lanes), win_window.dtype),
                   jax.ShapeDtypeStruct((4, 2, W_TILE, 2 * half_lanes), win_window.dtype),
                   jax.ShapeDtypeStruct((s, LANES), F32), jax.ShapeDtypeStruct((s, LANES), F32)],
        in_specs=[vm, hbm, vm, hbm, vm, vm, vm], out_specs=[vm, vm, hbm, vm, hbm, hbm],
        scratch_shapes=[pltpu.VMEM((8, w_ada.shape[1]), F32), pltpu.VMEM((s, LANES), F32), pltpu.VMEM((s, LANES), F32),
                        pltpu.VMEM(w_ada.shape, F32),
                        pltpu.SemaphoreType.DMA((2,)), pltpu.SemaphoreType.DMA((1,))]
        + _GATHER_SEMS + _gather_sems(WEIGHT_CHUNKS) + _GATHER_SEMS + _GATHER_SEMS,
        compiler_params=pltpu.CompilerParams(vmem_limit_bytes=VMEM_LIMIT),
    )(cw, w_ada, b_shard, win_window, win_edges, pos_rows, inv_freq)


def _reduce_scratch(rr, cc):
    c2 = cc // 2
    return [pltpu.VMEM((4, rr, c2), F32), pltpu.VMEM((4, rr, c2), F32), pltpu.VMEM((3, rr, c2), BF),
            pltpu.VMEM((2, rr, c2), BF), pltpu.VMEM((rr, c2), BF), pltpu.VMEM((rr, c2), F32),
            pltpu.SemaphoreType.DMA((8 + 3 * WEIGHT_CHUNKS,)), pltpu.SemaphoreType.DMA((8 + 3 * WEIGHT_CHUNKS,)),
            pltpu.SemaphoreType.DMA((5,))]


class _Reduce:
    def __init__(self, p_hbm, out_ref, acc_ref, own_ref, send_ref, land_ref, relay_ref, res_ref,
                 send_sems, recv_sems, local_sems, rows=None):
        x, y, c = lax.axis_index("x"), lax.axis_index("y"), lax.axis_index("c")
        part = (lambda j, ln: p_hbm.at[j, :, ln]) if rows is None else (lambda j, ln: p_hbm.at[rows(j), ln])
        c2 = out_ref.shape[1] // 2
        sibling = (x, y, 1 - c)
        first = (lax.rem(x + 1 - c, 2), lax.rem(y + c, 2))
        second = (lax.rem(x + c, 2), lax.rem(y + 1 - c, 2))
        shards = [2 * first[0] + first[1], 2 * second[0] + second[1], 2 * (1 - x) + (1 - y), 2 * x + y]
        sibling_slot = (1, 0, 2, 3)
        mine = pl.ds(pl.multiple_of(c * c2, c2), c2)
        other = pl.ds(pl.multiple_of((1 - c) * c2, c2), c2)
        self.acc_ref, self.own_ref, self.send_ref, self.land_ref = acc_ref, own_ref, send_ref, land_ref
        self.relay_ref, self.res_ref = relay_ref, res_ref
        self.own = [pltpu.make_async_copy(part(j, mine), own_ref.at[k], local_sems.at[k])
                    for k, j in enumerate(shards)]
        self.swap_out = [pltpu.make_async_remote_copy(
            src_ref=part(j, other), dst_ref=acc_ref.at[sibling_slot[k]], send_sem=send_sems.at[k],
            recv_sem=recv_sems.at[sibling_slot[k]], device_id=sibling, device_id_type=MESH) for k, j in enumerate(shards)]
        self.swap_in = [pltpu.make_async_remote_copy(
            src_ref=part(j, other), dst_ref=acc_ref.at[k], send_sem=send_sems.at[k], recv_sem=recv_sems.at[k],
            device_id=sibling, device_id_type=MESH) for k, j in enumerate(shards)]

        self.lanes = [slice(q * (c2 // WEIGHT_CHUNKS), (q + 1) * (c2 // WEIGHT_CHUNKS)) for q in range(WEIGHT_CHUNKS)]

        def message(m, src, dst, to):
            return [pltpu.make_async_remote_copy(
                src_ref=src.at[:, ln], dst_ref=dst.at[:, ln], send_sem=send_sems.at[8 + m * WEIGHT_CHUNKS + q],
                recv_sem=recv_sems.at[8 + m * WEIGHT_CHUNKS + q], device_id=(*to, c), device_id_type=MESH)
                for q, ln in enumerate(self.lanes)]

        self.direct = message(0, send_ref.at[0], land_ref.at[0], first)
        self.passed = message(1, send_ref.at[1], relay_ref, first)
        self.joint = message(2, send_ref.at[2], land_ref.at[1], second)
        self.put = pltpu.make_async_copy(res_ref, out_ref.at[:, mine], local_sems.at[4])
        self.share = pltpu.make_async_remote_copy(
            src_ref=res_ref, dst_ref=out_ref.at[:, mine], send_sem=send_sems.at[7],
            recv_sem=recv_sems.at[7], device_id=sibling, device_id_type=MESH)

    def start(self):
        for k in (2, 0, 1, 3):
            self.own[k].start()
            self.swap_out[k].start()

    def _combine(self, k):
        self.own[k].wait()
        self.swap_out[k].wait_send()
        self.swap_in[k].wait_recv()
        self.acc_ref[k] = self.acc_ref[k] + self.own_ref[k]

    def combine_and_send(self):
        dt = self.send_ref.dtype
        self._combine(2)
        self.send_ref[1] = self.acc_ref[2].astype(dt)
        for cp in self.passed:
            cp.start()
        self._combine(0)
        self.send_ref[0] = self.acc_ref[0].astype(dt)
        for cp in self.direct:
            cp.start()
        self._combine(1)
        self._combine(3)

    def send_joint(self):
        dt = self.send_ref.dtype
        for q, ln in enumerate(self.lanes):
            self.passed[q].wait_recv()
            self.send_ref[2, :, ln] = (self.acc_ref[1, :, ln] + self.relay_ref[:, ln].astype(F32)).astype(dt)
            self.joint[q].start()

    def total_and_share(self):
        for cp in self.direct + self.joint:
            cp.wait_recv()
        self.res_ref[...] = self.acc_ref[3] + self.land_ref[0].astype(F32) + self.land_ref[1].astype(F32)
        for cp in self.direct + self.passed + self.joint:
            cp.wait_send()
        self.put.start()
        self.share.start()

    def finish(self):
        self.put.wait()
        self.share.wait()


def _shard_window(n):
    return max(-(-(n * (j + 1)) // 8) * 8 - (n * j) // 8 * 8 for j in range(4))


class _LocalUpdate:
    def __init__(self, ins, in_vm, out_vm, outs, in_sems, out_sems):
        self.loads = [pltpu.make_async_copy(a, b, in_sems.at[k]) for k, (a, b) in enumerate(zip(ins, in_vm))]
        self.stores = [pltpu.make_async_copy(a, b, out_sems.at[k]) for k, (a, b) in enumerate(zip(out_vm, outs))]

    def start(self):
        for cp in self.loads:
            cp.start()

    def loaded(self):
        for cp in self.loads:
            cp.wait()

    def store(self):
        for cp in self.stores:
            cp.start()

    def finish(self):
        for cp in self.stores:
            cp.wait()


def _epilogue(dw_in_t, small, c_all, ada, out, dmod_row):
    cc = dw_in_t.shape[1]
    n = dw_in_t.shape[0] // 4
    r_in = _shard_window(n)
    n_red = len(_reduce_scratch(r_in, cc))
    ra, ca = ada[0].shape
    dm_rows = ca // LANES
    tr = min(512, ra)

    def body(pin_hbm, small_ref, c_ref, *rest):
        ada_hbm, out_hbm = rest[0:3], rest[3:7]
        gin_ref, small_all_ref = rest[7:9]
        ada_res, out_res = rest[9:13], rest[13:16]
        scratch = rest[16:]
        red_in = _Reduce(pin_hbm, gin_ref, *scratch[0:n_red],
                         rows=lambda j: pl.ds(pl.multiple_of((n * j) // 8 * 8, 8), r_in))
        gat = _Gather(small_ref, small_all_ref, *scratch[n_red:n_red + 3])
        local = scratch[n_red + 3:]
        ada_in, ada_out, out_in, out_out = local[0:3], local[3:7], local[7:11], local[11:14]
        upd_ada = _LocalUpdate(ada_hbm, ada_in, ada_out, ada_res, local[14], local[15])
        upd_out = _LocalUpdate(out_hbm, out_in, out_out, out_res, local[16], local[17])
        red_in.start()
        gat.start()
        upd_out.start()
        upd_ada.start()
        gat.relay()
        red_in.combine_and_send()
        gat.finish()
        red_in.send_joint()

        upd_out.loaded()
        out_out[0][...], out_out[1][...], out_out[2][...] = _adam(*[r[...] for r in out_in])
        upd_out.store()
        chip = 2 * lax.axis_index("x") + lax.axis_index("y")
        dm = jnp.concatenate(
            [jnp.concatenate([small_all_ref[d, pl.ds(dmod_row + dm_rows * chip + r, 1), :] for r in range(dm_rows)], axis=1)
             for d in range(8)], axis=0)
        cv = c_ref[...]
        sc = jnp.concatenate([cv * _sigmoid(cv), jnp.zeros_like(cv)], axis=0).astype(BF)
        dmb = jnp.concatenate([dm, jnp.zeros_like(dm)], axis=0).astype(BF)
        upd_ada.loaded()
        for r0 in range(0, ra, tr):
            rows = slice(r0, r0 + tr)
            g = _dot(sc[:, rows], dmb, TN)
            ada_out[0][rows, :] = g
            ada_out[1][rows, :], ada_out[2][rows, :], ada_out[3][rows, :] = _adam(
                ada_in[0][rows, :], g, ada_in[1][rows, :], ada_in[2][rows, :])
        upd_ada.store()

        red_in.total_and_share()
        red_in.finish()
        upd_out.finish()
        upd_ada.finish()

    vm = pl.BlockSpec(memory_space=pltpu.VMEM)
    anyspec = pl.BlockSpec(memory_space=pl.ANY)
    ada_buf, out_buf = pltpu.VMEM((ra, ca), F32), pltpu.VMEM(out[0].shape, F32)
    return pl.pallas_call(
        body, name="epilogue",
        out_shape=[jax.ShapeDtypeStruct((r_in, cc), F32), jax.ShapeDtypeStruct((8,) + small.shape, F32)]
        + [jax.ShapeDtypeStruct((ra, ca), F32)] * 4 + [jax.ShapeDtypeStruct(out[0].shape, F32)] * 3,
        in_specs=[anyspec, vm, vm] + [anyspec] * 7, out_specs=[anyspec, vm] + [anyspec] * 7,
        scratch_shapes=_reduce_scratch(r_in, cc) + _GATHER_SEMS + [ada_buf] * 7 + [out_buf] * 7
        + [pltpu.SemaphoreType.DMA((3,)), pltpu.SemaphoreType.DMA((4,)), pltpu.SemaphoreType.DMA((4,)), pltpu.SemaphoreType.DMA((3,))],
        compiler_params=pltpu.CompilerParams(vmem_limit_bytes=VMEM_LIMIT),
    )(dw_in_t, small, c_all, *ada, *out)


def _rope(t, cosb, sinb, first_half):
    partner = jnp.where(first_half, pltpu.roll(t, 96, 1), pltpu.roll(t, 32, 1))
    return t * cosb + partner * sinb


def _rope_t(g, cosb, sinb, first_half):
    gs = g * sinb
    partner = jnp.where(first_half, pltpu.roll(gs, 96, 1), pltpu.roll(gs, 32, 1))
    return g * cosb + partner


def _modnorm(x, g, sc1p, shift):
    r = lax.rsqrt(jnp.mean(x * x, axis=-1, keepdims=True) + RMS_EPS)
    xn = x * r
    return xn, r, (xn * g) * sc1p + shift


W_TILE = 16


def _w_window(n):
    return max(-(-(n * (j + 1)) // W_TILE) * W_TILE - (n * j) // W_TILE * W_TILE for j in range(4))


def _w_load_plan(n):
    starts = [(n * j) // W_TILE * W_TILE for j in range(4)]
    edges = starts + [starts[3] + _w_window(n) - W_TILE]
    assert all(starts[j] + _w_window(n) - W_TILE == edges[j + 1] for j in range(4))
    pieces, tiles = [], {}
    for pad, ref, rows in _UNPAD_ROWS:
        lo = ref
        for k, e in enumerate(edges):
            if ref <= e < ref + rows:
                if e > lo:
                    pieces.append((pad + lo - ref, lo, e - lo))
                tiles[k] = pad + e - ref
                lo = e + W_TILE
        if ref + rows > lo:
            pieces.append((pad + lo - ref, lo, ref + rows - lo))
    return pieces, tiles


def _load_w_padded(w_hbm, edge_ref, w_vm, sems):
    n = w_hbm.shape[0] // 4
    pieces, tiles = _w_load_plan(n)
    copies = [pltpu.make_async_copy(w_hbm.at[ref:ref + rows], w_vm.at[pad:pad + rows], sems.at[k])
              for k, (pad, ref, rows) in enumerate(pieces)]
    for cp in copies:
        cp.start()
    w_vm[OFF_GA + GLA_RANK:, :] = jnp.zeros((D_PAD - OFF_GA - GLA_RANK, D_MODEL), w_vm.dtype)
    row = lax.broadcasted_iota(jnp.int32, (W_TILE, w_vm.shape[1]), 0)
    for k, pad in tiles.items():
        last = edge_ref[max(k - 1, 0), 1].astype(F32)
        first = edge_ref[min(k, 3), 0].astype(F32)
        cut = W_TILE if k == 4 else (n * k) % W_TILE
        w_vm[pad:pad + W_TILE, :] = jnp.where(row < cut, last, first).astype(w_vm.dtype)
    return copies


def _inproj_fwd(x2d, shift, sc1p, g_norm, w_t, w_edges):
    s = x2d.shape[0]
    tm = min(1024, s)

    def body(x_ref, sh_ref, sc_ref, g_ref, w_hbm, edge_ref, o_ref, w_vm, sems):
        @pl.when(pl.program_id(0) == 0)
        def _():
            for cp in _load_w_padded(w_hbm, edge_ref, w_vm, sems):
                cp.wait()

        subs = _subtiles(tm)
        hs = [_modnorm(x_ref[sl, :], g_ref[...], sc_ref[...], sh_ref[...])[2].astype(BF) for sl in subs]
        for sl, h in zip(subs, hs):
            o_ref[sl, :] = _dot(h, w_vm[...], NT)

    vec = _full((1, D_MODEL))
    return pl.pallas_call(
        body, name="inproj_fwd", grid=(s // tm,),
        in_specs=[pl.BlockSpec((tm, D_MODEL), lambda i: (i, 0)), vec, vec, vec, pl.BlockSpec(memory_space=pl.ANY),
                  _full(w_edges.shape)],
        out_specs=pl.BlockSpec((tm, D_PAD), lambda i: (i, 0)),
        out_shape=jax.ShapeDtypeStruct((s, D_PAD), F32),
        scratch_shapes=[pltpu.VMEM((D_PAD, D_MODEL), BF),
                        pltpu.SemaphoreType.DMA((len(_w_load_plan(w_t.shape[0] // 4)[0]),))],
        compiler_params=_params(("arbitrary",)),
    )(x2d, shift, sc1p, g_norm, w_t, w_edges)


def _split3(a):
    hi = a.astype(BF)
    r1 = a - hi.astype(F32)
    mid = r1.astype(BF)
    lo = (r1 - mid.astype(F32)).astype(BF)
    return hi, mid, lo


def _tri_matmul(tri, a):
    hi, mid, lo = _split3(a)
    return _dot(tri, hi) + _dot(tri, mid) + _dot(tri, lo)


def _chunks(tb):
    return [slice(c * GLA_CHUNK, (c + 1) * GLA_CHUNK) for c in range(tb // GLA_CHUNK)]


def _per_chunk_rows(rows, width):
    return jnp.concatenate([jnp.broadcast_to(r, (GLA_CHUNK, width)) for r in rows], axis=0)


def _gla_triangle(tb):
    row = lax.broadcasted_iota(jnp.int32, (tb, tb), 0)
    col = lax.broadcasted_iota(jnp.int32, (tb, tb), 1)
    return (((row // GLA_CHUNK) == (col // GLA_CHUNK)) & (col <= row)).astype(F32)


def _lane_mean(x, ones_b):
    hi = x.astype(BF)
    lo = (x - hi.astype(F32)).astype(BF)
    return (_dot(hi, ones_b) + _dot(lo, ones_b)) * (1.0 / LANES)


def _head(t, h, lo_h):
    blk = t[:, LANES * (h // 2):LANES * (h // 2 + 1)]
    return jnp.where(lo_h, blk, 0.0) if h % 2 == 0 else jnp.where(lo_h, 0.0, blk)


def _gla_block_common(qk, ga, wd, bd, tril_b):
    tb = qk.shape[0]
    q, k = qk[:, :256], qk[:, 256:]
    z = _dot(ga.astype(BF), wd) + bd
    la = (jnp.minimum(z, 0.0) - jnp.log(1.0 + jnp.exp(-jnp.abs(z)))) * (1.0 / GLA_TAU)
    b = _tri_matmul(tril_b, la)
    bls = [b[rs.stop - 1:rs.stop, :] for rs in _chunks(tb)]
    eq = jnp.exp(b)
    ek = jnp.exp(-b)
    f = jnp.exp(_per_chunk_rows(bls, 256) - b)
    return z, eq, ek, f, q * (eq * GLA_DK ** -0.5), k * ek, k * f, bls


def _gla_units(s, rows):
    sub = min(GLA_SUB, s)
    tb = min(rows, s)
    subs = [slice(i * sub, (i + 1) * sub) for i in range(tb // sub)]
    units = [(i, h) for i in range(len(subs)) for h in range(GLA_HEADS)]
    return tb, sub, subs, units


def _gla_fwd(proj, wdecp, bdec, ggla):
    s = proj.shape[0]
    tb, sub, subs, units = _gla_units(s, GLA_ROWS_FWD)
    nch = sub // GLA_CHUNK

    def body(qk_ref, v_ref, gz_ref, ga_ref, wd_ref, bd_ref, gg_ref, tri_ref, og_ref, opre_ref, sprev_ref, st_ref):
        @pl.when(pl.program_id(0) == 0)
        def _():
            st_ref[...] = jnp.zeros_like(st_ref)

        lo_h = lax.broadcasted_iota(jnp.int32, (sub, LANES), 1) < GLA_DK
        tril = tri_ref[...] > 0.5
        tril_b = tri_ref[...].astype(BF)
        ones_b = jnp.ones((LANES, LANES), BF)
        gg, wd, bd = gg_ref[...], wd_ref[...], bd_ref[...]
        chunks = _chunks(sub)
        lanes = [slice(h * LANES, (h + 1) * LANES) for h in range(GLA_HEADS)]
        com = [_gla_block_common(qk_ref[sl, :], ga_ref[sl, :], wd, bd, tril_b) for sl in subs]
        decs = [[jnp.exp(bl) for bl in cm[7]] for cm in com]
        a = {(i, h): _head(com[i][4], h, lo_h).astype(BF) for i, h in units}
        bm = {(i, h): _head(com[i][5], h, lo_h).astype(BF) for i, h in units}
        ktl = {(i, h): _head(com[i][6], h, lo_h).astype(BF) for i, h in units}
        vh = {(i, h): v_ref[subs[i], lanes[h]].astype(BF) for i, h in units}
        sc = {u: _dot(a[u], bm[u], NT) for u in units}
        upd = {u: [_dot(vh[u][rs], ktl[u][rs], TN) for rs in chunks] for u in units}
        p = {u: jnp.where(tril, sc[u], 0.0).astype(BF) for u in units}
        o = {u: _dot(p[u], vh[u]) for u in units}
        states = {}
        for h in range(GLA_HEADS):
            st = st_ref[h]
            for i in range(len(subs)):
                entering = []
                for c in range(nch):
                    entering.append(st)
                    sprev_ref[i * nch + c, h] = st
                    st = st * decs[i][c][:, LANES * (h // 2):LANES * (h // 2 + 1)] + upd[(i, h)][c]
                states[(i, h)] = entering
            st_ref[h] = st
        inter = {u: [_dot(a[u][rs], states[u][c].astype(BF), NT) for c, rs in enumerate(chunks)] for u in units}
        o = {u: o[u] + jnp.concatenate(inter[u], axis=0) for u in units}
        ms = {u: _lane_mean(o[u] * o[u], ones_b) for u in units}
        for i, h in units:
            gzh = gz_ref[subs[i], lanes[h]]
            opre_ref[subs[i], lanes[h]] = o[(i, h)]
            og_ref[subs[i], lanes[h]] = (((o[(i, h)] * lax.rsqrt(ms[(i, h)] + RMS_EPS)) * gg[:, lanes[h]])
                                         * (gzh * _sigmoid(gzh))).astype(og_ref.dtype)

    def col(width, off):
        return pl.BlockSpec((tb, width), lambda i: (i, off // width))

    return pl.pallas_call(
        body, name="gla_fwd", grid=(s // tb,),
        in_specs=[col(512, OFF_QK), col(512, OFF_V), col(512, OFF_GZ), col(LANES, OFF_GA),
                  _full((LANES, 256)), _full((1, 256)), _full((1, 512)), _full((sub, sub))],
        out_specs=[pl.BlockSpec((tb, 512), lambda i: (i, 0)), pl.BlockSpec((tb, 512), lambda i: (i, 0)),
                   pl.BlockSpec((tb // GLA_CHUNK, GLA_HEADS, LANES, LANES), lambda i: (i, 0, 0, 0))],
        out_shape=[jax.ShapeDtypeStruct((s, 512), BF), jax.ShapeDtypeStruct((s, 512), F32),
                   jax.ShapeDtypeStruct((s // GLA_CHUNK, GLA_HEADS, LANES, LANES), F32)],
        scratch_shapes=[pltpu.VMEM((GLA_HEADS, LANES, LANES), F32)],
        compiler_params=_params(("arbitrary",)),
    )(proj, proj, proj, proj, wdecp, bdec, ggla, _gla_triangle(sub))


def _gla_bwd(proj, dog, opre, sprev, wdecp, bdec, ggla):
    s = proj.shape[0]
    tb, sub, subs, units = _gla_units(s, GLA_ROWS_BWD)
    nsub = len(subs)
    nch = sub // GLA_CHUNK
    nb = s // tb

    def body(qk_ref, v_ref, gz_ref, ga_ref, dog_ref, opre_ref, sprev_ref, wd_ref, bd_ref, gg_ref, tri_ref, triu_ref,
             dqk_ref, dv_ref, dgz_ref, dga_ref, dwd_ref, dbd_ref, dgg_ref, dst_ref):
        @pl.when(pl.program_id(0) == 0)
        def _():
            dst_ref[...] = jnp.zeros_like(dst_ref)
            dwd_ref[...] = jnp.zeros_like(dwd_ref)
            dbd_ref[...] = jnp.zeros_like(dbd_ref)
            dgg_ref[...] = jnp.zeros_like(dgg_ref)

        lo_h = lax.broadcasted_iota(jnp.int32, (sub, LANES), 1) < GLA_DK
        tril = tri_ref[...] > 0.5
        tril_b = tri_ref[...].astype(BF)
        triu_b = triu_ref[...].astype(BF)
        ones_b = jnp.ones((LANES, LANES), BF)
        last_row = (lax.broadcasted_iota(jnp.int32, (sub, LANES), 0) % GLA_CHUNK) == GLA_CHUNK - 1
        wd, gg, bd = wd_ref[...], gg_ref[...], bd_ref[...]
        chunks = _chunks(sub)
        lanes = [slice(h * LANES, (h + 1) * LANES) for h in range(GLA_HEADS)]
        blks = [slice(LANES * (h // 2), LANES * (h // 2 + 1)) for h in range(GLA_HEADS)]
        ga = [ga_ref[sl, :] for sl in subs]
        com = [_gla_block_common(qk_ref[sl, :], ga[i], wd, bd, tril_b) for i, sl in enumerate(subs)]
        decs = [[jnp.exp(bl) for bl in cm[7]] for cm in com]
        a = {(i, h): _head(com[i][4], h, lo_h).astype(BF) for i, h in units}
        bm = {(i, h): _head(com[i][5], h, lo_h).astype(BF) for i, h in units}
        ktl = {(i, h): _head(com[i][6], h, lo_h).astype(BF) for i, h in units}
        vh = {(i, h): v_ref[subs[i], lanes[h]].astype(BF) for i, h in units}
        sc = {u: _dot(a[u], bm[u], NT) for u in units}

        o = {(i, h): opre_ref[subs[i], lanes[h]] for i, h in units}
        ms = {u: _lane_mean(o[u] * o[u], ones_b) for u in units}
        gz = {(i, h): gz_ref[subs[i], lanes[h]] for i, h in units}
        dog = {(i, h): dog_ref[subs[i], lanes[h]] for i, h in units}
        sg = {u: _sigmoid(gz[u]) for u in units}
        r = {u: lax.rsqrt(ms[u] + RMS_EPS) for u in units}
        ohat = {u: o[u] * r[u] for u in units}
        sil = {u: gz[u] * sg[u] for u in units}
        for i, h in units:
            u = (i, h)
            dgz_ref[subs[i], lanes[h]] = (dog[u] * (ohat[u] * gg[:, lanes[h]])
                                          * (sg[u] * (1.0 + gz[u] * (1.0 - sg[u])))).astype(dgz_ref.dtype)
            dgg_ref[:, lanes[h]] += jnp.sum(dog[u] * sil[u] * ohat[u], axis=0, keepdims=True)
        dn = {(i, h): dog[(i, h)] * sil[(i, h)] * gg[:, lanes[h]] for i, h in units}
        mdn = {u: _lane_mean(dn[u] * ohat[u], ones_b) for u in units}
        do = {u: (r[u] * (dn[u] - ohat[u] * mdn[u])).astype(BF) for u in units}

        p = {u: jnp.where(tril, sc[u], 0.0).astype(BF) for u in units}
        dpr = {u: _dot(do[u], vh[u], NT) for u in units}
        incr = {u: [_dot(do[u][rs], a[u][rs], TN) for rs in chunks] for u in units}
        dv = {u: _dot(p[u], do[u], TN) for u in units}
        dp = {u: jnp.where(tril, dpr[u], 0.0).astype(BF) for u in units}
        dqd = {u: _dot(dp[u], bm[u]) for u in units}
        dkd = {u: _dot(dp[u], a[u], TN) for u in units}
        st = {(i, h): [sprev_ref[i * nch + c, h] for c in range(nch)] for i, h in units}
        leaving = {}
        for h in range(GLA_HEADS):
            d = dst_ref[h]
            for i in reversed(range(nsub)):
                out = [None] * nch
                for c in reversed(range(nch)):
                    out[c] = d
                    d = d * decs[i][c][:, blks[h]] + incr[(i, h)][c]
                leaving[(i, h)] = out
            dst_ref[h] = d
        lv_b = {u: [leaving[u][c].astype(BF) for c in range(nch)] for u in units}
        dv_s = {u: [_dot(ktl[u][rs], lv_b[u][c], NT) for c, rs in enumerate(chunks)] for u in units}
        dqd_s = {u: [_dot(do[u][rs], st[u][c].astype(BF)) for c, rs in enumerate(chunks)] for u in units}
        dkt_s = {u: [_dot(vh[u][rs], lv_b[u][c]) for c, rs in enumerate(chunks)] for u in units}
        ddec = {u: [jnp.sum(leaving[u][c] * st[u][c], axis=0, keepdims=True) for c in range(nch)] for u in units}
        for i, h in units:
            dv_ref[subs[i], lanes[h]] = (dv[(i, h)] + jnp.concatenate(dv_s[(i, h)], axis=0)).astype(dv_ref.dtype)
        dqd = {u: dqd[u] + jnp.concatenate(dqd_s[u], axis=0) for u in units}
        dkt = {u: jnp.concatenate(dkt_s[u], axis=0) for u in units}

        db = []
        for i, sl in enumerate(subs):
            _, eq, ek, f, qd, kd, kt, _ = com[i]
            parts = []
            for pair in range(GLA_HEADS // 2):
                blk, u0, u1 = blks[2 * pair], (i, 2 * pair), (i, 2 * pair + 1)
                dqd_b, dkd_b, dkt_b = dqd[u0] + dqd[u1], dkd[u0] + dkd[u1], dkt[u0] + dkt[u1]
                dqk_ref[sl, blk] = (dqd_b * (eq[:, blk] * GLA_DK ** -0.5)).astype(dqk_ref.dtype)
                dqk_ref[sl, 256 + LANES * pair:256 + LANES * (pair + 1)] = (dkd_b * ek[:, blk] + dkt_b * f[:, blk]).astype(dqk_ref.dtype)
                dkt_kt = dkt_b * kt[:, blk]
                dbp = dqd_b * qd[:, blk] - dkd_b * kd[:, blk] - dkt_kt
                dbl = [jnp.sum(dkt_kt[rs], axis=0, keepdims=True) + (ddec[u0][c] + ddec[u1][c]) * decs[i][c][:, blk]
                       for c, rs in enumerate(chunks)]
                parts.append(jnp.where(last_row, dbp + _per_chunk_rows(dbl, LANES), dbp))
            db.append(jnp.concatenate(parts, axis=1))
        dla = [_tri_matmul(triu_b, db[i]) for i in range(nsub)]
        dz32 = [dla[i] * (1.0 / GLA_TAU) * _sigmoid(-com[i][0]) for i in range(nsub)]
        dz = [t.astype(BF) for t in dz32]
        for i, sl in enumerate(subs):
            dga_ref[sl, :] = _dot(dz[i], wd, NT).astype(dga_ref.dtype)
            dwd_ref[...] += _dot(ga[i].astype(BF), dz[i], TN)
            dbd_ref[...] += jnp.sum(dz32[i], axis=0, keepdims=True)

    def col(width, off):
        return pl.BlockSpec((tb, width), lambda i: (nb - 1 - i, off // width))

    def rev(width):
        return pl.BlockSpec((tb, width), lambda i: (nb - 1 - i, 0))

    return pl.pallas_call(
        body, name="gla_bwd", grid=(nb,),
        in_specs=[col(512, OFF_QK), col(512, OFF_V), col(512, OFF_GZ), col(LANES, OFF_GA), rev(512), rev(512),
                  pl.BlockSpec((tb // GLA_CHUNK, GLA_HEADS, LANES, LANES), lambda i: (nb - 1 - i, 0, 0, 0)),
                  _full((LANES, 256)), _full((1, 256)), _full((1, 512)), _full((sub, sub)), _full((sub, sub))],
        out_specs=[rev(512), rev(512), rev(512), rev(LANES), _full((LANES, 256)), _full((1, 256)), _full((1, 512))],
        out_shape=[jax.ShapeDtypeStruct((s, 512), BF), jax.ShapeDtypeStruct((s, 512), BF),
                   jax.ShapeDtypeStruct((s, 512), BF), jax.ShapeDtypeStruct((s, LANES), BF),
                   jax.ShapeDtypeStruct((LANES, 256), F32), jax.ShapeDtypeStruct((1, 256), F32),
                   jax.ShapeDtypeStruct((1, 512), F32)],
        scratch_shapes=[pltpu.VMEM((GLA_HEADS, LANES, LANES), F32)],
        compiler_params=_params(("arbitrary",)),
    )(proj, proj, proj, proj, dog, opre, sprev, wdecp, bdec, ggla, _gla_triangle(sub), _gla_triangle(sub).T)


_SWA_COL_HEADS = (0, 2, 1, 3, 4, 6, 5, 7)
_SWA_COLS = SWA_HEADS * SWA_BLOCK


def _swa_masks():
    lo2 = lax.broadcasted_iota(jnp.int32, (2 * SWA_BLOCK, LANES), 1) < 64
    lane1 = lax.broadcasted_iota(jnp.int32, (SWA_BLOCK, LANES), 1)
    first_half = (lane1 % 64) < 32
    key = lax.broadcasted_iota(jnp.int32, (SWA_BLOCK, _SWA_COLS), 0)
    query = lax.broadcasted_iota(jnp.int32, (SWA_BLOCK, _SWA_COLS), 1) % SWA_BLOCK
    return lo2, lane1 < 64, first_half, key > query


def _merge_band(t, prev_mask, prev_bias=None):
    prev = t[:SWA_BLOCK] if prev_bias is None else t[:SWA_BLOCK] + prev_bias
    return jnp.where(prev_mask, prev, t[SWA_BLOCK:])


def _split_band(t, prev_mask_b):
    prev = t * prev_mask_b
    return jnp.concatenate([prev, t - prev], axis=0)


def _kv_variants(t, lo2):
    tr = pltpu.roll(t, 64, 1)
    lo_v = [jnp.where(lo2, t, 0.0).astype(BF), jnp.where(lo2, tr, 0.0).astype(BF)]
    hi_v = [jnp.where(lo2, 0.0, tr).astype(BF), jnp.where(lo2, 0.0, t).astype(BF)]
    return lo_v, hi_v


def _kv_variants_t(t):
    tt = t.T
    sw = jnp.concatenate([tt[64:], tt[:64]], axis=0)
    top = lax.broadcasted_iota(jnp.int32, tt.shape, 0) < 64
    lo_v = [jnp.where(top, tt, 0.0).astype(BF), jnp.where(top, sw, 0.0).astype(BF)]
    hi_v = [jnp.where(top, 0.0, sw).astype(BF), jnp.where(top, 0.0, tt).astype(BF)]
    return lo_v, hi_v


def _swa_scores(qg, k_lo, k_hi):
    return jnp.concatenate([_dot(k_lo[0], qg[0], NT), _dot(k_hi[0], qg[0], NT),
                            _dot(k_lo[1], qg[1], NT), _dot(k_hi[1], qg[1], NT)], axis=1)


def _sink_row(sinks_ref):
    return jnp.concatenate([jnp.full((1, SWA_BLOCK), sinks_ref[0, hd], F32) for hd in _SWA_COL_HEADS], axis=1)


def _swa_softmax(st, prev_mask, prev_bias, sink):
    st = _merge_band(st, prev_mask, prev_bias)
    m = jnp.maximum(jnp.max(st, axis=0, keepdims=True), sink)
    ex = jnp.exp(st - m)
    es = jnp.exp(sink - m)
    inv = 1.0 / (jnp.sum(ex, axis=0, keepdims=True) + es)
    return ex, es, inv


def _no_prev_bias(block_index):
    return jnp.where(block_index > 0, 0.0, -1e30).astype(F32)


def _swa_queries(sq_ref, rows, cosb, sinb, first_half):
    qs = [_rope(sq_ref[rows, p * LANES:(p + 1) * LANES], cosb, sinb, first_half) * 0.125 for p in range(4)]
    return [jnp.concatenate(qs[0:2], axis=0), jnp.concatenate(qs[2:4], axis=0)]


def _phase_steps(nsteps, phases):
    return [min(nsteps - 1, (k * nsteps) // phases) for k in range(phases - 1)] + [nsteps - 1]


def _swa_fwd(proj, cos, sin, sinks, half_out):
    s = proj.shape[0]
    nq = min(SWA_QBLOCKS_FWD, s // SWA_BLOCK)
    tq = nq * SWA_BLOCK
    steps = _phase_steps(s // tq, 4)

    def body(sq_ref, sz_ref, sk_ref, sv_ref, cos_ref, sin_ref, sinks_ref, hout_hbm, os_ref, opre_ref, wout_hbm,
             kprev, vprev, *gather_sems):
        n = pl.program_id(0)

        @pl.when(n == 0)
        def _():
            kprev[...] = jnp.zeros_like(kprev)
            vprev[...] = jnp.zeros_like(vprev)

        gather = _Gather(hout_hbm, wout_hbm, *gather_sems, chunks=WEIGHT_CHUNKS)
        for step, phase in zip(steps, (gather.start, gather.pass_on, gather.relay_diagonal, gather.finish)):
            pl.when(n == step)(phase)

        lo2, _, first_half, prev_mask = _swa_masks()
        prev_mask_b = jnp.where(prev_mask, 1.0, 0.0).astype(BF)
        sink = _sink_row(sinks_ref)
        blocks = range(nq)
        rows = [slice(j * SWA_BLOCK, (j + 1) * SWA_BLOCK) for j in blocks]
        cosb = [cos_ref[rows[j], :] for j in blocks]
        sinb = [sin_ref[rows[j], :] for j in blocks]
        kc = [_rope(sk_ref[rows[j], :], cosb[j], sinb[j], first_half) for j in blocks]
        vc = [sv_ref[rows[j], :] for j in blocks]
        kcat = [jnp.concatenate([kprev[...] if j == 0 else kc[j - 1], kc[j]], axis=0) for j in blocks]
        vcat = [jnp.concatenate([vprev[...] if j == 0 else vc[j - 1], vc[j]], axis=0) for j in blocks]
        kprev[...] = kc[-1]
        vprev[...] = vc[-1]
        kvar = [_kv_variants(kcat[j], lo2) for j in blocks]
        vtvar = [_kv_variants_t(vcat[j]) for j in blocks]
        qg = [[q.astype(BF) for q in _swa_queries(sq_ref, rows[j], cosb[j], sinb[j], first_half)] for j in blocks]
        st = [_swa_scores(qg[j], *kvar[j]) for j in blocks]
        soft = [_swa_softmax(st[j], prev_mask, _no_prev_bias(n) if j == 0 else None, sink) for j in blocks]
        pt = [_split_band(soft[j][0].astype(BF), prev_mask_b) for j in blocks]
        og = {}
        for j in blocks:
            inv = soft[j][2]
            for g in range(2):
                c0, c1, c2 = 512 * g, 512 * g + 256, 512 * g + 512
                ot = (_dot(vtvar[j][0][g], pt[j][:, c0:c1]) * inv[:, c0:c1]
                      + _dot(vtvar[j][1][g], pt[j][:, c1:c2]) * inv[:, c1:c2])
                og[(j, g)] = ot.T
        for j in blocks:
            for g in range(2):
                for i in range(2):
                    ls = slice((2 * g + i) * LANES, (2 * g + i + 1) * LANES)
                    o = og[(j, g)][i * SWA_BLOCK:(i + 1) * SWA_BLOCK]
                    sz = sz_ref[rows[j], ls]
                    opre_ref[rows[j], ls] = o
                    os_ref[rows[j], ls] = (o * (sz * _sigmoid(sz))).astype(os_ref.dtype)

    def col(width, off):
        return pl.BlockSpec((tq, width), lambda i: (i, off // width))

    row = pl.BlockSpec((tq, LANES), lambda i: (i, 0))
    return pl.pallas_call(
        body, name="swa_fwd", grid=(s // tq,),
        in_specs=[col(512, OFF_SQ), col(512, OFF_SZ), col(LANES, OFF_SK), col(LANES, OFF_SV), row, row,
                  pl.BlockSpec(memory_space=pltpu.SMEM), pl.BlockSpec(memory_space=pl.ANY)],
        out_specs=[pl.BlockSpec((tq, 512), lambda i: (i, 0))] * 2 + [pl.BlockSpec(memory_space=pl.ANY)],
        out_shape=[jax.ShapeDtypeStruct((s, 512), BF), jax.ShapeDtypeStruct((s, 512), F32),
                   jax.ShapeDtypeStruct((8,) + half_out.shape, half_out.dtype)],
        scratch_shapes=[pltpu.VMEM((SWA_BLOCK, LANES), F32)] * 2 + _gather_sems(WEIGHT_CHUNKS),
        compiler_params=_params(("arbitrary",)),
    )(proj, proj, proj, proj, cos, sin, sinks, half_out)


def _swa_bwd(proj, dos, opre, cos, sin, sinks, dw_out_parts):
    s = proj.shape[0]
    nq = min(SWA_QBLOCKS, s // SWA_BLOCK)
    tq = nq * SWA_BLOCK
    steps = _phase_steps(s // tq, 5)
    _, r_out, c_out = dw_out_parts.shape

    def body(sq_ref, sz_ref, sk_ref, sv_ref, dos_ref, opre_ref, cos_ref, sin_ref, sinks_ref, pout_hbm,
             dsq_ref, dsz_ref, dsk_ref, dsv_ref, dsink_ref, gout_hbm, kprev, vprev, cprev, sprev, *reduce_scratch):
        n = pl.program_id(0)

        @pl.when(n == 0)
        def _():
            kprev[...] = jnp.zeros_like(kprev)
            vprev[...] = jnp.zeros_like(vprev)
            cprev[...] = jnp.zeros_like(cprev)
            sprev[...] = jnp.zeros_like(sprev)
            for hd in range(SWA_HEADS):
                dsink_ref[0, hd] = 0.0

        reduce = _Reduce(pout_hbm, gout_hbm, *reduce_scratch)
        phases = (reduce.start, reduce.combine_and_send, reduce.send_joint, reduce.total_and_share, reduce.finish)
        for step, phase in zip(steps, phases):
            pl.when(n == step)(phase)

        lo2, lo1, first_half, prev_mask = _swa_masks()
        prev_mask_b = jnp.where(prev_mask, 1.0, 0.0).astype(BF)
        lo1s = jnp.concatenate([lo1, lo1], axis=0)
        sink = _sink_row(sinks_ref)

        def home(m0, m1):
            t0 = m0 + pltpu.roll(m0, 64, 1)
            t1 = m1 + pltpu.roll(m1, 64, 1)
            return jnp.where(lo2, t0, t1)

        kp, vp, cp_, sp_ = kprev[...], vprev[...], cprev[...], sprev[...]
        for j in range(nq):
            rows = slice(j * SWA_BLOCK, (j + 1) * SWA_BLOCK)
            blk = n * nq + j
            cosb, sinb = cos_ref[rows, :], sin_ref[rows, :]
            kc = _rope(sk_ref[rows, :], cosb, sinb, first_half)
            vc = sv_ref[rows, :]
            kcat = jnp.concatenate([kp, kc], axis=0)
            k_lo, k_hi = _kv_variants(kcat, lo2)
            kt_lo, kt_hi = _kv_variants_t(kcat)
            v_lo, v_hi = _kv_variants(jnp.concatenate([vp, vc], axis=0), lo2)
            qg32 = _swa_queries(sq_ref, rows, cosb, sinb, first_half)
            qg = [q.astype(BF) for q in qg32]
            ex, es, inv = _swa_softmax(_swa_scores(qg, k_lo, k_hi), prev_mask, _no_prev_bias(n) if j == 0 else None, sink)
            pr, ps = ex * inv, es * inv

            dog32 = []
            for g in range(2):
                parts = []
                for i in range(2):
                    ls = slice((2 * g + i) * LANES, (2 * g + i + 1) * LANES)
                    sz = sz_ref[rows, ls]
                    sg = _sigmoid(sz)
                    dos_p = dos_ref[rows, ls]
                    dsz_ref[rows, ls] = (dos_p * opre_ref[rows, ls] * (sg * (1.0 + sz * (1.0 - sg)))).astype(dsz_ref.dtype)
                    parts.append(dos_p * (sz * sg))
                dog32.append(jnp.concatenate(parts, axis=0))
            dog = [t.astype(BF) for t in dog32]
            dpr = _merge_band(jnp.concatenate([_dot(v_lo[0], dog[0], NT), _dot(v_hi[0], dog[0], NT),
                                               _dot(v_lo[1], dog[1], NT), _dot(v_hi[1], dog[1], NT)], axis=1), prev_mask)
            rd = jnp.sum(pr * dpr, axis=0, keepdims=True)
            ds = _split_band((pr * (dpr - rd)).astype(BF), prev_mask_b)
            prb = _split_band(pr.astype(BF), prev_mask_b)
            sink_term = ps * rd
            for r, hd in enumerate(_SWA_COL_HEADS):
                dsink_ref[0, hd] += -jnp.sum(sink_term[:, r * SWA_BLOCK:(r + 1) * SWA_BLOCK])

            dk_g, dv_g = [], []
            for g in range(2):
                c0, c1, c2 = 512 * g, 512 * g + 256, 512 * g + 512
                dq = (_dot(kt_lo[g], ds[:, c0:c1]) + _dot(kt_hi[g], ds[:, c1:c2])).T
                for i in range(2):
                    ls = slice((2 * g + i) * LANES, (2 * g + i + 1) * LANES)
                    dsq_ref[rows, ls] = _rope_t(dq[i * SWA_BLOCK:(i + 1) * SWA_BLOCK] * 0.125, cosb, sinb,
                                                first_half).astype(dsq_ref.dtype)
                q_split = jnp.concatenate([jnp.where(lo1s, qg32[g], 0.0), jnp.where(lo1s, 0.0, qg32[g])], axis=0).astype(BF)
                do_split = jnp.concatenate([jnp.where(lo1s, dog32[g], 0.0), jnp.where(lo1s, 0.0, dog32[g])], axis=0).astype(BF)
                dk_g.append(_dot(ds[:, c0:c2], q_split))
                dv_g.append(_dot(prb[:, c0:c2], do_split))
            dk = home(dk_g[0], dk_g[1])
            dv = home(dv_g[0], dv_g[1])
            cur = pl.ds(pl.multiple_of(blk * SWA_BLOCK, SWA_BLOCK), SWA_BLOCK)
            dsk_ref[cur, :] = _rope_t(dk[SWA_BLOCK:], cosb, sinb, first_half)
            dsv_ref[cur, :] = dv[SWA_BLOCK:]
            dk_prev = _rope_t(dk[:SWA_BLOCK], cp_, sp_, first_half)
            dv_prev = dv[:SWA_BLOCK]
            if j == 0:
                @pl.when(n > 0)
                def _():
                    prv = pl.ds(pl.multiple_of((blk - 1) * SWA_BLOCK, SWA_BLOCK), SWA_BLOCK)
                    dsk_ref[prv, :] += dk_prev
                    dsv_ref[prv, :] += dv_prev
            else:
                prv = pl.ds(pl.multiple_of((blk - 1) * SWA_BLOCK, SWA_BLOCK), SWA_BLOCK)
                dsk_ref[prv, :] += dk_prev
                dsv_ref[prv, :] += dv_prev
            kp, vp, cp_, sp_ = kc, vc, cosb, sinb
        kprev[...] = kp
        vprev[...] = vp
        cprev[...] = cp_
        sprev[...] = sp_

    def col(width, off):
        return pl.BlockSpec((tq, width), lambda i: (i, off // width))

    row = pl.BlockSpec((tq, LANES), lambda i: (i, 0))
    wide = pl.BlockSpec((tq, 512), lambda i: (i, 0))
    return pl.pallas_call(
        body, name="swa_bwd", grid=(s // tq,),
        in_specs=[col(512, OFF_SQ), col(512, OFF_SZ), col(LANES, OFF_SK), col(LANES, OFF_SV), wide, wide, row, row,
                  pl.BlockSpec(memory_space=pltpu.SMEM), pl.BlockSpec(memory_space=pl.ANY)],
        out_specs=[wide, wide, _full((s, LANES)), _full((s, LANES)), pl.BlockSpec(memory_space=pltpu.SMEM),
                   pl.BlockSpec(memory_space=pl.ANY)],
        out_shape=[jax.ShapeDtypeStruct((s, 512), BF), jax.ShapeDtypeStruct((s, 512), BF),
                   jax.ShapeDtypeStruct((s, LANES), F32), jax.ShapeDtypeStruct((s, LANES), F32),
                   jax.ShapeDtypeStruct((1, SWA_HEADS), F32), jax.ShapeDtypeStruct((r_out, c_out), F32)],
        scratch_shapes=[pltpu.VMEM((SWA_BLOCK, LANES), F32)] * 4 + _reduce_scratch(r_out, c_out),
        compiler_params=_params(("arbitrary",)),
    )(proj, proj, proj, proj, dos, opre, cos, sin, sinks, dw_out_parts)


def _outproj(og, osw, w_out, x2d, target, gate, g_final):
    s = x2d.shape[0]
    tm = min(512, s)

    def body(og_ref, os_ref, w_ref, x_ref, t_ref, gate_ref, gf_ref,
             dx2_ref, dog_ref, dos_ref, dw_ref, loss_ref, dgf_ref, dgate_ref):
        @pl.when(pl.program_id(0) == 0)
        def _():
            dw_ref[...] = jnp.zeros_like(dw_ref)
            loss_ref[...] = jnp.zeros_like(loss_ref)
            dgf_ref[...] = jnp.zeros_like(dgf_ref)
            dgate_ref[...] = jnp.zeros_like(dgate_ref)

        w = w_ref[...]
        gate, gf = gate_ref[...], gf_ref[...]
        subs = _subtiles(tm)
        ogv = [og_ref[sl, :] for sl in subs]
        osv = [os_ref[sl, :] for sl in subs]
        y = [_dot(ogv[k], w[:512]) + _dot(osv[k], w[512:]) for k in range(len(subs))]
        dys = []
        for k, sl in enumerate(subs):
            x2 = x_ref[sl, :] + gate * y[k]
            r = lax.rsqrt(jnp.mean(x2 * x2, axis=-1, keepdims=True) + RMS_EPS)
            xn = x2 * r
            err = xn * gf - t_ref[sl, :]
            loss_ref[...] += 0.5 * jnp.sum(jnp.mean(err * err, axis=-1, keepdims=True), axis=0, keepdims=True)
            dyf = err * (1.0 / D_MODEL)
            dgf_ref[...] += jnp.sum(dyf * xn, axis=0, keepdims=True)
            t = dyf * gf
            dx2 = r * (t - xn * jnp.mean(t * xn, axis=-1, keepdims=True))
            dx2_ref[sl, :] = dx2
            dgate_ref[...] += jnp.sum(dx2 * y[k], axis=0, keepdims=True)
            dys.append((dx2 * gate).astype(BF))
            dmix = _dot(dys[k], w, NT)
            dog_ref[sl, :] = dmix[:, :512]
            dos_ref[sl, :] = dmix[:, 512:]
        dy = jnp.concatenate(dys, axis=0)
        dw_ref[:512, :] += _dot(og_ref[...], dy, TN)
        dw_ref[512:, :] += _dot(os_ref[...], dy, TN)

    half = pl.BlockSpec((tm, 512), lambda i: (i, 0))
    rowb = pl.BlockSpec((tm, D_MODEL), lambda i: (i, 0))
    vec = _full((1, D_MODEL))
    return pl.pallas_call(
        body, name="outproj", grid=(s // tm,),
        in_specs=[half, half, _full((D_MODEL, D_MODEL)), rowb, rowb, vec, vec],
        out_specs=[rowb, half, half, _full((D_MODEL, D_MODEL)), _full((1, 1)), vec, vec],
        out_shape=[jax.ShapeDtypeStruct((s, D_MODEL), F32), jax.ShapeDtypeStruct((s, 512), F32),
                   jax.ShapeDtypeStruct((s, 512), F32), jax.ShapeDtypeStruct((D_MODEL, D_MODEL), F32),
                   jax.ShapeDtypeStruct((1, 1), F32), jax.ShapeDtypeStruct((1, D_MODEL), F32),
                   jax.ShapeDtypeStruct((1, D_MODEL), F32)],
        compiler_params=_params(("arbitrary",)),
    )(og, osw, w_out, x2d, target, gate, g_final)


_PIECES = ((OFF_QK, 512), (OFF_V, 512), (OFF_GZ, 512), (OFF_SQ, 512), (OFF_SZ, 512),
           (OFF_SK, LANES), (OFF_SV, LANES), (OFF_GA, LANES))

_UNPAD_ROWS = ((OFF_QK, 0, 1024),
               (OFF_GA, 1024, GLA_RANK),
               (OFF_GZ, 1040, 1024),
               (OFF_SK, 2064, 256),
               (OFF_SZ, 2320, 512))


def _inproj_bwd(x2d, shift, sc1p, g_norm, w_t, w_edges, dx2, pieces):
    s = x2d.shape[0]
    tm = min(512, s)
    nsteps = s // tm

    def body(x_ref, sh_ref, sc_ref, g_ref, w_hbm, edge_ref, dx2_ref, *rest):
        piece_refs = rest[:len(_PIECES)]
        gx_ref, dw_hbm, dsh_ref, dsc_ref, dg_ref, w_vm, dw_vm, in_sems, out_sems = rest[len(_PIECES):]
        i = pl.program_id(0)

        @pl.when(i == 0)
        def _():
            loads = _load_w_padded(w_hbm, edge_ref, w_vm, in_sems)
            dw_vm[...] = jnp.zeros_like(dw_vm)
            dsh_ref[...] = jnp.zeros_like(dsh_ref)
            dsc_ref[...] = jnp.zeros_like(dsc_ref)
            dg_ref[...] = jnp.zeros_like(dg_ref)
            for cp in loads:
                cp.wait()

        g, sc1p_v, shift_v = g_ref[...], sc_ref[...], sh_ref[...]
        subs = _subtiles(tm)
        dhs = []
        for sl in subs:
            dh = None
            for (off, width), pr in zip(_PIECES, piece_refs):
                part = _dot(pr[sl, :].astype(BF), w_vm[off:off + width, :])
                dh = part if dh is None else dh + part
            dhs.append(dh)
        norm = [_modnorm(x_ref[sl, :], g, sc1p_v, shift_v) for sl in subs]
        hb = jnp.concatenate([h.astype(BF) for _, _, h in norm], axis=0)
        for (off, width), pr in zip(_PIECES, piece_refs):
            dw_vm[off:off + width, :] += _dot(pr[...].astype(BF), hb, TN)
        for sl, (xn, r, _), dh in zip(subs, norm, dhs):
            dsh_ref[...] += jnp.sum(dh, axis=0, keepdims=True)
            dsc_ref[...] += jnp.sum(dh * (xn * g), axis=0, keepdims=True)
            dg_ref[...] += jnp.sum(dh * xn * sc1p_v, axis=0, keepdims=True)
            dxn = dh * g * sc1p_v
            gx_ref[sl, :] = dx2_ref[sl, :] + r * (dxn - xn * jnp.mean(dxn * xn, axis=-1, keepdims=True))

        @pl.when(i == nsteps - 1)
        def _():
            copies = [pltpu.make_async_copy(dw_vm.at[src:src + n], dw_hbm.at[dst:dst + n], out_sems.at[k])
                      for k, (src, dst, n) in enumerate(_UNPAD_ROWS)]
            for cp in copies:
                cp.start()
            for cp in copies:
                cp.wait()

    rowb = pl.BlockSpec((tm, D_MODEL), lambda i: (i, 0))
    vec = _full((1, D_MODEL))
    anyspec = pl.BlockSpec(memory_space=pl.ANY)
    piece_specs = [pl.BlockSpec((tm, width), lambda i: (i, 0)) for _, width in _PIECES]
    return pl.pallas_call(
        body, name="inproj_bwd", grid=(nsteps,),
        in_specs=[rowb, vec, vec, vec, anyspec, _full(w_edges.shape), rowb] + piece_specs,
        out_specs=[rowb, anyspec, vec, vec, vec],
        out_shape=[jax.ShapeDtypeStruct((s, D_MODEL), F32), jax.ShapeDtypeStruct((D_IN, D_MODEL), F32),
                   jax.ShapeDtypeStruct((1, D_MODEL), F32), jax.ShapeDtypeStruct((1, D_MODEL), F32),
                   jax.ShapeDtypeStruct((1, D_MODEL), F32)],
        scratch_shapes=[pltpu.VMEM((D_PAD, D_MODEL), BF), pltpu.VMEM((D_PAD, D_MODEL), F32),
                        pltpu.SemaphoreType.DMA((len(_w_load_plan(w_t.shape[0] // 4)[0]),)),
                        pltpu.SemaphoreType.DMA((len(_UNPAD_ROWS),))],
        compiler_params=_params(("arbitrary",)),
    )(x2d, shift, sc1p, g_norm, w_t, w_edges, dx2, *pieces)


def _adam(w, g, m, v):
    m2 = ADAM_B1 * m + (1.0 - ADAM_B1) * g
    v2 = ADAM_B2 * v + (1.0 - ADAM_B2) * (g * g)
    m_hat = m2 / (1.0 - ADAM_B1 ** ADAM_STEP)
    v_hat = v2 / (1.0 - ADAM_B2 ** ADAM_STEP)
    delta = -ADAM_LR * (m_hat / (jnp.sqrt(v_hat) + ADAM_EPS) + ADAM_WD * w)
    return delta, m2, v2


def _adamw_t(w3, g_window, m3, v3, name):
    rr, _, cc = w3.shape
    parts = [slice(q * (cc // 4), (q + 1) * (cc // 4)) for q in range(4)]
    starts = sorted({(rr * j) % 8 for j in range(4)})

    def body(w_hbm, gw_hbm, m_hbm, v_hbm, d_hbm, m2_hbm, v2_hbm, g3_hbm,
             w_vm, m_vm, v_vm, gw_vm, d_vm, m2_vm, v2_vm, g_vm, in_sems, out_sems):
        start = lax.rem(rr * (2 * lax.axis_index("x") + lax.axis_index("y")), 8)
        ins = ((w_hbm, w_vm), (m_hbm, m_vm), (v_hbm, v_vm))
        outs = ((d_vm, d_hbm), (m2_vm, m2_hbm), (v2_vm, v2_hbm), (g_vm, g3_hbm))
        loads = [[pltpu.make_async_copy(src.at[:, 0, p], dst.at[:, p], in_sems.at[4 * q + k]) for k, (src, dst) in enumerate(ins)]
                 + [pltpu.make_async_copy(gw_hbm.at[:, p], gw_vm.at[:, p], in_sems.at[4 * q + 3])]
                 for q, p in enumerate(parts)]
        stores = [[pltpu.make_async_copy(src.at[:, p], dst.at[:, 0, p], out_sems.at[4 * q + k]) for k, (src, dst) in enumerate(outs)]
                  for q, p in enumerate(parts)]
        for group in loads:
            for cp in group:
                cp.start()
        for q, p in enumerate(parts):
            for cp in loads[q]:
                cp.wait()
            g = gw_vm[starts[0]:starts[0] + rr, p]
            for o in starts[1:]:
                g = jnp.where(start == o, gw_vm[o:o + rr, p], g)
            g_vm[:, p] = g
            d_vm[:, p], m2_vm[:, p], v2_vm[:, p] = _adam(w_vm[:, p], g, m_vm[:, p], v_vm[:, p])
            for cp in stores[q]:
                cp.start()
        for group in stores:
            for cp in group:
                cp.wait()

    hbm = pl.BlockSpec(memory_space=pl.ANY)
    return pl.pallas_call(
        body, name=name, grid=(1,), in_specs=[hbm] * 4,
        out_specs=[hbm] * 4, out_shape=[jax.ShapeDtypeStruct((rr, 1, cc), F32)] * 4,
        scratch_shapes=[pltpu.VMEM((rr, cc), F32)] * 3 + [pltpu.VMEM(g_window.shape, F32)] + [pltpu.VMEM((rr, cc), F32)] * 4
        + [pltpu.SemaphoreType.DMA((16,)), pltpu.SemaphoreType.DMA((16,))],
        compiler_params=_params(("arbitrary",)),
    )(w3, g_window, m3, v3)


def _small_update(parts, weights, moms, vels):
    n = len(weights)

    def body(*refs):
        p_refs, w_refs, m_refs, v_refs = refs[:n + 1], refs[n + 1:2 * n + 1], refs[2 * n + 1:3 * n + 1], refs[3 * n + 1:4 * n + 1]
        outs = refs[4 * n + 1:]
        for i in range(n):
            g = p_refs[i][0]
            for d in range(1, 8):
                g = g + p_refs[i][d]
            delta, m2, v2 = _adam(w_refs[i][...], g, m_refs[i][...], v_refs[i][...])
            outs[4 * i][...] = g
            outs[4 * i + 1][...] = delta
            outs[4 * i + 2][...] = m2
            outs[4 * i + 3][...] = v2
        tot = p_refs[n][0]
        for d in range(1, 8):
            tot = tot + p_refs[n][d]
        outs[4 * n][...] = tot

    out_shape = []
    for w in weights:
        out_shape += [jax.ShapeDtypeStruct(w.shape, F32)] * 4
    out_shape.append(jax.ShapeDtypeStruct(parts[n].shape[1:], F32))
    return pl.pallas_call(body, name="small_update", out_shape=out_shape, compiler_params=_params())(
        *parts, *weights, *moms, *vels)


def _rows8(a):
    flat = a.reshape(-1)
    rows = -(-flat.shape[0] // LANES)
    rows8 = -(-rows // 8) * 8
    flat = jnp.pad(flat, (0, rows8 * LANES - flat.shape[0]))
    return flat.reshape(rows8, LANES)


def kernel(x, c, positions, w_ada, b_ada, g_norm, w_in, w_decay, b_decay, g_gla_head, sinks, w_out, g_final, loss_target, m_w_ada, m_b_ada, m_g_norm, m_w_in, m_w_decay, m_b_decay, m_g_gla_head, m_sinks, m_w_out, m_g_final, v_w_ada, v_b_ada, v_g_norm, v_w_in, v_w_decay, v_b_decay, v_g_gla_head, v_sinks, v_w_out, v_g_final):
    ax, ay, ac = lax.axis_index("x"), lax.axis_index("y"), lax.axis_index("c")
    chip = 2 * ax + ay
    dev = 2 * chip + ac
    s = x.shape[1]
    x2d = x[0]
    target = loss_target[0]
    w_ada2, w_out2, w_dec2 = w_ada[0], w_out[0], w_decay[0]
    w_in_t = w_in[0].T
    ada_cols = w_ada2.shape[1]
    in_cols = w_in_t.shape[0]
    out_rows = w_out2.shape[0]
    half = D_MODEL // 2

    cw = jnp.concatenate([c.reshape(8, LANES), w_dec2.reshape(8, LANES)], axis=0)
    b_shard = lax.dynamic_slice(b_ada, (0, chip * ada_cols), (1, ada_cols))
    half_out = lax.dynamic_slice(w_out2, (ac * (out_rows // 2), 0), (out_rows // 2, D_MODEL)).astype(BF)
    inv_freq = 1.0 / (ROPE_THETA ** (jnp.arange(0, 64, 2, dtype=F32) / 64))
    room = _w_window(in_cols) - in_cols
    win_window = lax.dynamic_slice(jnp.pad(w_in_t, ((room, room), (0, 0))), (room - (in_cols * chip) % W_TILE, ac * half),
                                   (_w_window(in_cols), half)).astype(BF)
    win_edges = jnp.stack([win_window[:W_TILE], win_window[-W_TILE:]])
    first, mod_all, w_t, w_edges, cos, sin = _prologue(
        cw, w_ada2, b_shard, win_window, win_edges, in_cols, positions.reshape(s // LANES, LANES), jnp.tile(inv_freq, 4).reshape(1, LANES))

    first = first.reshape(8, 2, 8, LANES)
    c_all = first[:, 0].reshape(8, D_MODEL)
    w_dec_full = first[0::2, 1].reshape(4, GLA_RANK, 64).transpose(1, 0, 2).reshape(GLA_RANK, 256)
    mod = mod_all.reshape(4, 2, 8, ada_cols)[:, 0]
    mod = lax.dynamic_slice(mod, (0, dev, 0), (4, 1, ada_cols)).reshape(1, 4 * ada_cols)
    shift, sc1p, gate = mod[:, :D_MODEL], 1.0 + mod[:, D_MODEL:2 * D_MODEL], mod[:, 2 * D_MODEL:]
    wdecp = jnp.pad(w_dec_full, ((0, LANES - GLA_RANK), (0, 0))).astype(BF)

    proj = _inproj_fwd(x2d, shift, sc1p, g_norm, w_t, w_edges)
    og, o_gla, sprev = _gla_fwd(proj, wdecp, b_decay, g_gla_head)
    osw, o_swa, w_out_all = _swa_fwd(proj, cos, sin, sinks, half_out)
    w_out_all = w_out_all.reshape(D_MODEL, D_MODEL)
    dx2, dog, dos, dw_out, loss_p, dgf, dgate = _outproj(og, osw, w_out_all, x2d, target, gate, g_final.reshape(1, D_MODEL))
    dsq, dsz, dsk, dsv, dsinks, g_w_out = _swa_bwd(proj, dos, o_swa, cos, sin, sinks, dw_out.reshape(4, out_rows, D_MODEL))
    dqk, dv, dgz, dga, dwdp, dbd, dgg = _gla_bwd(proj, dog, o_gla, sprev, wdecp, b_decay, g_gla_head)
    pieces = (dqk, dv, dgz, dsq, dsz, dsk, dsv, dga)
    gx, dw_in_t, dshift, dscale, dgn = _inproj_bwd(x2d, shift, sc1p, g_norm, w_t, w_edges, dx2, pieces)

    segs = [jnp.concatenate([dshift, dscale, dgate], axis=1), dgn, dgf, dwdp[:GLA_RANK], dbd, dgg, dsinks, loss_p]
    packed = [_rows8(a) for a in segs]
    offs = [0]
    for a in packed:
        offs.append(offs[-1] + a.shape[0])
    (g_window, small, g_w_ada, d_w_ada, nm_w_ada, nv_w_ada, d_w_out, nm_w_out, nv_w_out) = _epilogue(
        dw_in_t, jnp.concatenate(packed, axis=0), c_all, (w_ada2, m_w_ada[0], v_w_ada[0]),
        (w_out2, g_w_out, m_w_out[0], v_w_out[0]), offs[0])

    def seg(i, size):
        return small[:, offs[i]:offs[i + 1]].reshape(8, -1)[:, :size]

    dmod_all = seg(0, 3 * D_MODEL)
    dwd_all = lax.dynamic_slice(seg(3, GLA_RANK * 256).reshape(8, GLA_RANK, 256), (0, 0, chip * 64), (8, GLA_RANK, 64))
    parts = [dmod_all.reshape(8, 1, 3 * D_MODEL), seg(1, D_MODEL).reshape(8, 1, D_MODEL), dwd_all,
             seg(4, 256).reshape(8, 1, 256), seg(5, 512).reshape(8, 1, 512), seg(6, SWA_HEADS).reshape(8, 1, SWA_HEADS),
             seg(2, D_MODEL).reshape(8, 1, D_MODEL), seg(7, LANES).reshape(8, 1, LANES)]
    smalls = _small_update(
        parts,
        [b_ada, g_norm, w_dec2, b_decay, g_gla_head, sinks, g_final.reshape(1, D_MODEL)],
        [m_b_ada, m_g_norm, m_w_decay[0], m_b_decay, m_g_gla_head, m_sinks, m_g_final.reshape(1, D_MODEL)],
        [v_b_ada, v_g_norm, v_w_decay[0], v_b_decay, v_g_gla_head, v_sinks, v_g_final.reshape(1, D_MODEL)])
    (g_b_ada, d_b_ada, nm_b_ada, nv_b_ada, g_gn, d_gn, nm_gn, nv_gn, g_wd, d_wd, nm_wd, nv_wd,
     g_bd, d_bd, nm_bd, nv_bd, g_gg, d_gg, nm_gg, nv_gg, g_sk, d_sk, nm_sk, nv_sk,
     g_gf, d_gf, nm_gf, nv_gf, loss_row) = smalls
    loss = loss_row[0, 0]

    to3 = lambda a: jnp.transpose(a, (2, 0, 1))
    from3 = lambda a: jnp.transpose(a, (1, 2, 0))[0]
    d3, nm3, nv3, g3 = _adamw_t(to3(w_in), g_window, to3(m_w_in), to3(v_w_in), "adamw_w_in")
    g_w_in, d_w_in, nm_w_in, nv_w_in = from3(g3), from3(d3), from3(nm3), from3(nv3)

    flat = lambda a: a.reshape(D_MODEL)
    grads = [g_w_ada[None], g_b_ada, g_gn, g_w_in[None], g_wd[None], g_bd, g_gg, g_sk, g_w_out[None], flat(g_gf)]
    deltas = [d_w_ada[None], d_b_ada, d_gn, d_w_in[None], d_wd[None], d_bd, d_gg, d_sk, d_w_out[None], flat(d_gf)]
    new_m = [nm_w_ada[None], nm_b_ada, nm_gn, nm_w_in[None], nm_wd[None], nm_bd, nm_gg, nm_sk, nm_w_out[None], flat(nm_gf)]
    new_v = [nv_w_ada[None], nv_b_ada, nv_gn, nv_w_in[None], nv_wd[None], nv_bd, nv_gg, nv_sk, nv_w_out[None], flat(nv_gf)]
    return (loss, gx[None], *grads, *deltas, *new_m, *new_v)
```

```python
import jax
import jax.numpy as jnp
from jax import lax
from jax.experimental import pallas as pl
from jax.experimental.pallas import tpu as pltpu

F32 = jnp.float32
BF = jnp.bfloat16

D_MODEL = 1024
GLA_HEADS = 4
GLA_DK = 64
GLA_CHUNK = 64
GLA_RANK = 16
GLA_TAU = 16.0
GLA_SUB = 256
GLA_ROWS_FWD = 1024
GLA_ROWS_BWD = 512
SWA_HEADS = 8
SWA_BLOCK = 128
SWA_QBLOCKS_FWD = 8
SWA_QBLOCKS = 8
RMS_EPS = 1e-6
ROPE_THETA = 10000.0

OFF_QK, OFF_V, OFF_GZ, OFF_SQ, OFF_SZ, OFF_SK, OFF_SV, OFF_GA = 0, 512, 1024, 1536, 2048, 2560, 2688, 2816
D_PAD = 2944
D_IN = 2832
LANES = 128
VMEM_LIMIT = 56 * 1024 * 1024

ADAM_LR, ADAM_B1, ADAM_B2, ADAM_EPS, ADAM_WD, ADAM_STEP = 0.001, 0.9, 0.999, 1e-08, 0.01, 10

NT = (((1,), (1,)), ((), ()))
TN = (((0,), (0,)), ((), ()))
MESH = pl.DeviceIdType.MESH


def _dot(a, b, dims=None):
    if dims is None:
        return jnp.dot(a, b, preferred_element_type=F32)
    return lax.dot_general(a, b, dims, preferred_element_type=F32)


def _sigmoid(x):
    return 1.0 / (1.0 + jnp.exp(-x))


def _params(sem=None):
    return pltpu.CompilerParams(dimension_semantics=sem, vmem_limit_bytes=VMEM_LIMIT)


def _full(shape):
    return pl.BlockSpec(shape, lambda i: (0,) * len(shape))


def _subtiles(rows, size=256):
    size = min(size, rows)
    return [slice(k * size, (k + 1) * size) for k in range(rows // size)]


WEIGHT_CHUNKS = 4


def _gather_sems(chunks=1):
    return [pltpu.SemaphoreType.DMA((7 * chunks,)), pltpu.SemaphoreType.DMA((7 * chunks,)), pltpu.SemaphoreType.DMA]


_GATHER_SEMS = _gather_sems()


class _Gather:
    def __init__(self, x_ref, out_ref, send_sems, recv_sems, local_sem, slab=None, chunks=1):
        self.slab_of = slab
        self.chunks = chunks
        self.width = x_ref.shape[-1] // chunks
        x, y, c = lax.axis_index("x"), lax.axis_index("y"), lax.axis_index("c")
        self.me, self.sibling, self.c = (x, y, c), (x, y, 1 - c), c
        self.xn, self.yn, self.dg = (1 - x, y), (x, 1 - y), (1 - x, 1 - y)
        self.pass_from = (lax.rem(x + 1 - c, 2), lax.rem(y + c, 2))
        self.pass_to = (lax.rem(x + c, 2), lax.rem(y + 1 - c, 2))
        self.x_ref, self.out_ref, self.send_sems, self.recv_sems = x_ref, out_ref, send_sems, recv_sems
        self.mine = pltpu.make_async_copy(x_ref, self._slab(*self.me), local_sem)

    def _slab(self, px, py, pc):
        if self.slab_of is not None:
            return self.slab_of(self.out_ref, px, py, pc)
        return self.out_ref.at[4 * px + 2 * py + pc]

    def _part(self, ref, q):
        if self.chunks == 1:
            return ref
        lanes = slice(q * self.width, (q + 1) * self.width)
        return ref.at[(slice(None),) * (len(ref.shape) - 1) + (lanes,)]

    def _copy(self, k, q, blk, to, src=None):
        i = k * self.chunks + q
        return pltpu.make_async_remote_copy(
            src_ref=self._part(self._slab(*blk) if src is None else src, q), dst_ref=self._part(self._slab(*blk), q),
            send_sem=self.send_sems.at[i], recv_sem=self.recv_sems.at[i], device_id=to, device_id_type=MESH)

    def _sends(self, q):
        c = self.c
        return [self._copy(0, q, self.me, self.sibling, src=self.x_ref),
                self._copy(1, q, self.me, (*self.xn, c), src=self.x_ref),
                self._copy(2, q, self.me, (*self.yn, c), src=self.x_ref),
                self._copy(3, q, (*self.pass_from, c), (*self.pass_to, c)),
                self._copy(4, q, (*self.xn, c), self.sibling),
                self._copy(5, q, (*self.yn, c), self.sibling),
                self._copy(6, q, (*self.dg, c), self.sibling)]

    def start(self):
        self.mine.start()
        for q in range(self.chunks):
            sends = self._sends(q)
            for k in (1, 2, 0):
                sends[k].start()

    def pass_on(self, only=None):
        for q in range(self.chunks) if only is None else (only,):
            sends = self._sends(q)
            self._copy(1, q, (*self.xn, self.c), self.me).wait_recv()
            self._copy(2, q, (*self.yn, self.c), self.me).wait_recv()
            for k in (3, 4, 5):
                sends[k].start()

    def relay_diagonal(self, only=None):
        for q in range(self.chunks) if only is None else (only,):
            self._copy(3, q, (*self.dg, self.c), self.me).wait_recv()
            self._sends(q)[6].start()

    def relay(self):
        self.pass_on()
        self.relay_diagonal()

    def finish(self):
        c = self.c
        for q in range(self.chunks):
            self._copy(0, q, self.sibling, self.me).wait_recv()
            for k, chip in ((4, self.xn), (5, self.yn), (6, self.dg)):
                self._copy(k, q, (*chip, 1 - c), self.me).wait_recv()
            for cp in self._sends(q):
                cp.wait_send()
        self.mine.wait()


def _prologue(cw, w_ada, b_shard, win_window, win_edges, n_in, pos_rows, inv_freq):
    s = pos_rows.shape[0] * LANES
    rt = min(512, s)
    inner = win_window.shape[0] - 2 * W_TILE
    starts = [(n_in * j) // W_TILE * W_TILE for j in range(4)]
    edge_rows = starts + [starts[3] + inner + W_TILE]
    assert all(starts[j] + inner + W_TILE == edge_rows[j + 1] for j in range(4))

    def body(cw_ref, wada_hbm, b_ref, hin_ref, hedge_ref, pos_ref, f_ref,
             first_ref, mod_ref, win_ref, cos_hbm, sin_hbm,
             mod_blk, cos_ref, sin_ref, wada_ref, edge_ref, tile_ref, table_sems, local_sems, tile_sems, *sems):
        fetch_w = pltpu.make_async_copy(wada_hbm, wada_ref, local_sems.at[0])
        fetch_w.start()
        g_c = _Gather(cw_ref, first_ref, *sems[0:3])
        half_lanes = hin_ref.shape[1]

        def lanes_of(pc):
            return pl.ds(pl.multiple_of(pc * half_lanes, half_lanes), half_lanes)

        def inner_rows(px, py):
            return pl.ds(pl.multiple_of((n_in * (2 * px + py)) // W_TILE * W_TILE + W_TILE, W_TILE), inner)

        g_in = _Gather(hin_ref.at[pl.ds(W_TILE, inner), :], win_ref, *sems[3:6], chunks=WEIGHT_CHUNKS,
                       slab=lambda ref, px, py, pc: ref.at[inner_rows(px, py), lanes_of(pc)])
        g_mod = _Gather(mod_blk, mod_ref, *sems[6:9])
        g_edge = _Gather(hedge_ref, edge_ref, *sems[9:12],
                         slab=lambda ref, px, py, pc: ref.at[2 * px + py, :, :, lanes_of(pc)])
        g_c.start()
        g_edge.start()
        g_in.start()
        g_c.relay()
        g_edge.relay()
        g_c.finish()
        c_rows = [jnp.concatenate([first_ref[d, r:r + 1, :] for r in range(8)], axis=1) for d in range(8)]
        c_all = jnp.concatenate(c_rows, axis=0)
        sc = (c_all * _sigmoid(c_all)).astype(BF)
        fetch_w.wait()
        mod_blk[...] = _dot(sc, wada_ref[...].astype(BF)) + b_ref[...]
        g_mod.start()

        def rope_rows(i, carry):
            rows = pl.ds(pl.multiple_of(i * rt, rt), rt)
            cols = [jnp.transpose(jnp.broadcast_to(pos_ref[pl.ds(i * (rt // LANES) + b, 1), :].astype(F32), (LANES, LANES)))
                    for b in range(rt // LANES)]
            ang = jnp.concatenate(cols, axis=0) * f_ref[...]
            lane = lax.broadcasted_iota(jnp.int32, ang.shape, 1)
            cos_ref[rows, :] = jnp.cos(ang)
            sn = jnp.sin(ang)
            sin_ref[rows, :] = jnp.where((lane % 64) < 32, -sn, sn)
            pltpu.make_async_copy(cos_ref.at[rows, :], cos_hbm.at[rows, :], table_sems.at[0]).start()
            pltpu.make_async_copy(sin_ref.at[rows, :], sin_hbm.at[rows, :], table_sems.at[1]).start()
            return carry

        waits = ([lambda q=q: g_in.pass_on(q) for q in range(WEIGHT_CHUNKS)]
                 + [lambda q=q: g_in.relay_diagonal(q) for q in range(WEIGHT_CHUNKS)] + [g_mod.relay])
        steps = s // rt
        lead = steps // 4
        per_wait = max((steps - lead) // len(waits), 1)
        lax.fori_loop(0, lead, rope_rows, 0)
        done = lead
        for wait in waits:
            wait()
            nxt = min(done + per_wait, steps)
            lax.fori_loop(done, nxt, rope_rows, 0)
            done = nxt
        lax.fori_loop(done, steps, rope_rows, 0)
        g_in.finish()
        g_mod.finish()
        g_edge.finish()
        row = lax.broadcasted_iota(jnp.int32, tile_ref.shape[1:], 0)
        tiles = []
        for k, at in enumerate(edge_rows):
            last = edge_ref[max(k - 1, 0), 1].astype(F32)
            first = edge_ref[min(k, 3), 0].astype(F32)
            cut = W_TILE if k == 4 else (n_in * k) % W_TILE
            tile_ref[k] = jnp.where(row < cut, last, first).astype(tile_ref.dtype)
            tiles.append(pltpu.make_async_copy(tile_ref.at[k], win_ref.at[at:at + W_TILE, :], tile_sems.at[k]))
            tiles[-1].start()
        for cp in tiles:
            cp.wait()
        pltpu.make_async_copy(cos_ref, cos_hbm, table_sems.at[0]).wait()
        pltpu.make_async_copy(sin_ref, sin_hbm, table_sems.at[1]).wait()

    vm = pl.BlockSpec(memory_space=pltpu.VMEM)
    hbm = pl.BlockSpec(memory_space=pl.ANY)
    half_lanes = win_window.shape[1]
    return pl.pallas_call(
        body, name="prologue",
        out_shape=[jax.ShapeDtypeStruct((8,) + cw.shape, F32), jax.ShapeDtypeStruct((8, 8, w_ada.shape[1]), F32),
                   jax.ShapeDtypeStruct((4 * n_in, 2 * half_lanes), win_window.dtype),
                   jax.ShapeDtypeStruct((s, LANES), F32), jax.ShapeDtypeStruct((s, LANES), F32)],
        in_specs=[vm, hbm, vm, hbm, vm, vm, vm], out_specs=[vm, vm, hbm, hbm, hbm],
        scratch_shapes=[pltpu.VMEM((8, w_ada.shape[1]), F32), pltpu.VMEM((s, LANES), F32), pltpu.VMEM((s, LANES), F32),
                        pltpu.VMEM(w_ada.shape, F32),
                        pltpu.VMEM((4, 2, W_TILE, 2 * half_lanes), win_window.dtype),
                        pltpu.VMEM((5, W_TILE, 2 * half_lanes), win_window.dtype),
                        pltpu.SemaphoreType.DMA((2,)), pltpu.SemaphoreType.DMA((1,)), pltpu.SemaphoreType.DMA((5,))]
        + _GATHER_SEMS + _gather_sems(WEIGHT_CHUNKS) + _GATHER_SEMS + _GATHER_SEMS,
        compiler_params=pltpu.CompilerParams(vmem_limit_bytes=VMEM_LIMIT),
    )(cw, w_ada, b_shard, win_window, win_edges, pos_rows, inv_freq)


def _reduce_scratch(rr, cc):
    c2 = cc // 2
    return [pltpu.VMEM((4, rr, c2), F32), pltpu.VMEM((4, rr, c2), F32), pltpu.VMEM((3, rr, c2), BF),
            pltpu.VMEM((2, rr, c2), BF), pltpu.VMEM((rr, c2), BF), pltpu.VMEM((rr, c2), F32),
            pltpu.SemaphoreType.DMA((8 + 3 * WEIGHT_CHUNKS,)), pltpu.SemaphoreType.DMA((8 + 3 * WEIGHT_CHUNKS,)),
            pltpu.SemaphoreType.DMA((5,))]


class _Reduce:
    def __init__(self, p_hbm, out_ref, acc_ref, own_ref, send_ref, land_ref, relay_ref, res_ref,
                 send_sems, recv_sems, local_sems, rows=None):
        x, y, c = lax.axis_index("x"), lax.axis_index("y"), lax.axis_index("c")
        part = (lambda j, ln: p_hbm.at[j, :, ln]) if rows is None else (lambda j, ln: p_hbm.at[rows(j), ln])
        c2 = out_ref.shape[1] // 2
        sibling = (x, y, 1 - c)
        first = (lax.rem(x + 1 - c, 2), lax.rem(y + c, 2))
        second = (lax.rem(x + c, 2), lax.rem(y + 1 - c, 2))
        shards = [2 * first[0] + first[1], 2 * second[0] + second[1], 2 * (1 - x) + (1 - y), 2 * x + y]
        sibling_slot = (1, 0, 2, 3)
        mine = pl.ds(pl.multiple_of(c * c2, c2), c2)
        other = pl.ds(pl.multiple_of((1 - c) * c2, c2), c2)
        self.acc_ref, self.own_ref, self.send_ref, self.land_ref = acc_ref, own_ref, send_ref, land_ref
        self.relay_ref, self.res_ref = relay_ref, res_ref
        self.own = [pltpu.make_async_copy(part(j, mine), own_ref.at[k], local_sems.at[k])
                    for k, j in enumerate(shards)]
        self.swap_out = [pltpu.make_async_remote_copy(
            src_ref=part(j, other), dst_ref=acc_ref.at[sibling_slot[k]], send_sem=send_sems.at[k],
            recv_sem=recv_sems.at[sibling_slot[k]], device_id=sibling, device_id_type=MESH) for k, j in enumerate(shards)]
        self.swap_in = [pltpu.make_async_remote_copy(
            src_ref=part(j, other), dst_ref=acc_ref.at[k], send_sem=send_sems.at[k], recv_sem=recv_sems.at[k],
            device_id=sibling, device_id_type=MESH) for k, j in enumerate(shards)]

        self.lanes = [slice(q * (c2 // WEIGHT_CHUNKS), (q + 1) * (c2 // WEIGHT_CHUNKS)) for q in range(WEIGHT_CHUNKS)]

        def message(m, src, dst, to):
            return [pltpu.make_async_remote_copy(
                src_ref=src.at[:, ln], dst_ref=dst.at[:, ln], send_sem=send_sems.at[8 + m * WEIGHT_CHUNKS + q],
                recv_sem=recv_sems.at[8 + m * WEIGHT_CHUNKS + q], device_id=(*to, c), device_id_type=MESH)
                for q, ln in enumerate(self.lanes)]

        self.direct = message(0, send_ref.at[0], land_ref.at[0], first)
        self.passed = message(1, send_ref.at[1], relay_ref, first)
        self.joint = message(2, send_ref.at[2], land_ref.at[1], second)
        self.put = pltpu.make_async_copy(res_ref, out_ref.at[:, mine], local_sems.at[4])
        self.share = pltpu.make_async_remote_copy(
            src_ref=res_ref, dst_ref=out_ref.at[:, mine], send_sem=send_sems.at[7],
            recv_sem=recv_sems.at[7], device_id=sibling, device_id_type=MESH)

    def start(self):
        for k in (2, 0, 1, 3):
            self.own[k].start()
            self.swap_out[k].start()

    def _combine(self, k):
        self.own[k].wait()
        self.swap_out[k].wait_send()
        self.swap_in[k].wait_recv()
        self.acc_ref[k] = self.acc_ref[k] + self.own_ref[k]

    def combine_and_send(self):
        dt = self.send_ref.dtype
        self._combine(2)
        self.send_ref[1] = self.acc_ref[2].astype(dt)
        for cp in self.passed:
            cp.start()
        self._combine(0)
        self.send_ref[0] = self.acc_ref[0].astype(dt)
        for cp in self.direct:
            cp.start()
        self._combine(1)
        self._combine(3)

    def send_joint(self):
        dt = self.send_ref.dtype
        for q, ln in enumerate(self.lanes):
            self.passed[q].wait_recv()
            self.send_ref[2, :, ln] = (self.acc_ref[1, :, ln] + self.relay_ref[:, ln].astype(F32)).astype(dt)
            self.joint[q].start()

    def total_and_share(self):
        for cp in self.direct + self.joint:
            cp.wait_recv()
        self.res_ref[...] = self.acc_ref[3] + self.land_ref[0].astype(F32) + self.land_ref[1].astype(F32)
        for cp in self.direct + self.passed + self.joint:
            cp.wait_send()
        self.put.start()
        self.share.start()

    def finish(self):
        self.put.wait()
        self.share.wait()


def _shard_window(n):
    return max(-(-(n * (j + 1)) // 8) * 8 - (n * j) // 8 * 8 for j in range(4))


class _LocalUpdate:
    def __init__(self, ins, in_vm, out_vm, outs, in_sems, out_sems):
        self.loads = [pltpu.make_async_copy(a, b, in_sems.at[k]) for k, (a, b) in enumerate(zip(ins, in_vm))]
        self.stores = [pltpu.make_async_copy(a, b, out_sems.at[k]) for k, (a, b) in enumerate(zip(out_vm, outs))]

    def start(self):
        for cp in self.loads:
            cp.start()

    def loaded(self):
        for cp in self.loads:
            cp.wait()

    def store(self):
        for cp in self.stores:
            cp.start()

    def finish(self):
        for cp in self.stores:
            cp.wait()


def _epilogue(dw_in_t, small, c_all, ada, out, dmod_row):
    cc = dw_in_t.shape[1]
    n = dw_in_t.shape[0] // 4
    r_in = _shard_window(n)
    n_red = len(_reduce_scratch(r_in, cc))
    ra, ca = ada[0].shape
    dm_rows = ca // LANES
    tr = min(512, ra)

    def body(pin_hbm, small_ref, c_ref, *rest):
        ada_hbm, out_hbm = rest[0:3], rest[3:7]
        gin_ref, small_all_ref = rest[7:9]
        ada_res, out_res = rest[9:13], rest[13:16]
        scratch = rest[16:]
        red_in = _Reduce(pin_hbm, gin_ref, *scratch[0:n_red],
                         rows=lambda j: pl.ds(pl.multiple_of((n * j) // 8 * 8, 8), r_in))
        gat = _Gather(small_ref, small_all_ref, *scratch[n_red:n_red + 3])
        local = scratch[n_red + 3:]
        ada_in, ada_out, out_in, out_out = local[0:3], local[3:7], local[7:11], local[11:14]
        upd_ada = _LocalUpdate(ada_hbm, ada_in, ada_out, ada_res, local[14], local[15])
        upd_out = _LocalUpdate(out_hbm, out_in, out_out, out_res, local[16], local[17])
        red_in.start()
        gat.start()
        upd_out.start()
        upd_ada.start()
        gat.relay()
        red_in.combine_and_send()
        gat.finish()
        red_in.send_joint()

        upd_out.loaded()
        out_out[0][...], out_out[1][...], out_out[2][...] = _adam(*[r[...] for r in out_in])
        upd_out.store()
        chip = 2 * lax.axis_index("x") + lax.axis_index("y")
        dm = jnp.concatenate(
            [jnp.concatenate([small_all_ref[d, pl.ds(dmod_row + dm_rows * chip + r, 1), :] for r in range(dm_rows)], axis=1)
             for d in range(8)], axis=0)
        cv = c_ref[...]
        sc = jnp.concatenate([cv * _sigmoid(cv), jnp.zeros_like(cv)], axis=0).astype(BF)
        dmb = jnp.concatenate([dm, jnp.zeros_like(dm)], axis=0).astype(BF)
        upd_ada.loaded()
        for r0 in range(0, ra, tr):
            rows = slice(r0, r0 + tr)
            g = _dot(sc[:, rows], dmb, TN)
            ada_out[0][rows, :] = g
            ada_out[1][rows, :], ada_out[2][rows, :], ada_out[3][rows, :] = _adam(
                ada_in[0][rows, :], g, ada_in[1][rows, :], ada_in[2][rows, :])
        upd_ada.store()

        red_in.total_and_share()
        red_in.finish()
        upd_out.finish()
        upd_ada.finish()

    vm = pl.BlockSpec(memory_space=pltpu.VMEM)
    anyspec = pl.BlockSpec(memory_space=pl.ANY)
    ada_buf, out_buf = pltpu.VMEM((ra, ca), F32), pltpu.VMEM(out[0].shape, F32)
    return pl.pallas_call(
        body, name="epilogue",
        out_shape=[jax.ShapeDtypeStruct((r_in, cc), F32), jax.ShapeDtypeStruct((8,) + small.shape, F32)]
        + [jax.ShapeDtypeStruct((ra, ca), F32)] * 4 + [jax.ShapeDtypeStruct(out[0].shape, F32)] * 3,
        in_specs=[anyspec, vm, vm] + [anyspec] * 7, out_specs=[anyspec, vm] + [anyspec] * 7,
        scratch_shapes=_reduce_scratch(r_in, cc) + _GATHER_SEMS + [ada_buf] * 7 + [out_buf] * 7
        + [pltpu.SemaphoreType.DMA((3,)), pltpu.SemaphoreType.DMA((4,)), pltpu.SemaphoreType.DMA((4,)), pltpu.SemaphoreType.DMA((3,))],
        compiler_params=pltpu.CompilerParams(vmem_limit_bytes=VMEM_LIMIT),
    )(dw_in_t, small, c_all, *ada, *out)


def _rope(t, cosb, sinb, first_half):
    partner = jnp.where(first_half, pltpu.roll(t, 96, 1), pltpu.roll(t, 32, 1))
    return t * cosb + partner * sinb


def _rope_t(g, cosb, sinb, first_half):
    gs = g * sinb
    partner = jnp.where(first_half, pltpu.roll(gs, 96, 1), pltpu.roll(gs, 32, 1))
    return g * cosb + partner


def _modnorm(x, g, sc1p, shift):
    r = lax.rsqrt(jnp.mean(x * x, axis=-1, keepdims=True) + RMS_EPS)
    xn = x * r
    return xn, r, (xn * g) * sc1p + shift


W_TILE = 16


def _w_window(n):
    return max(-(-(n * (j + 1)) // W_TILE) * W_TILE - (n * j) // W_TILE * W_TILE for j in range(4))


def _load_w_padded(w_hbm, w_vm, sems):
    copies = [pltpu.make_async_copy(w_hbm.at[ref:ref + n], w_vm.at[pad:pad + n], sems.at[k])
              for k, (pad, ref, n) in enumerate(_UNPAD_ROWS)]
    for cp in copies:
        cp.start()
    w_vm[OFF_GA + GLA_RANK:, :] = jnp.zeros((D_PAD - OFF_GA - GLA_RANK, D_MODEL), w_vm.dtype)
    return copies


def _inproj_fwd(x2d, shift, sc1p, g_norm, w_t):
    s = x2d.shape[0]
    tm = min(1024, s)

    def body(x_ref, sh_ref, sc_ref, g_ref, w_hbm, o_ref, w_vm, sems):
        @pl.when(pl.program_id(0) == 0)
        def _():
            for cp in _load_w_padded(w_hbm, w_vm, sems):
                cp.wait()

        subs = _subtiles(tm)
        hs = [_modnorm(x_ref[sl, :], g_ref[...], sc_ref[...], sh_ref[...])[2].astype(BF) for sl in subs]
        for sl, h in zip(subs, hs):
            o_ref[sl, :] = _dot(h, w_vm[...], NT)

    vec = _full((1, D_MODEL))
    return pl.pallas_call(
        body, name="inproj_fwd", grid=(s // tm,),
        in_specs=[pl.BlockSpec((tm, D_MODEL), lambda i: (i, 0)), vec, vec, vec, pl.BlockSpec(memory_space=pl.ANY)],
        out_specs=pl.BlockSpec((tm, D_PAD), lambda i: (i, 0)),
        out_shape=jax.ShapeDtypeStruct((s, D_PAD), F32),
        scratch_shapes=[pltpu.VMEM((D_PAD, D_MODEL), BF), pltpu.SemaphoreType.DMA((len(_UNPAD_ROWS),))],
        compiler_params=_params(("arbitrary",)),
    )(x2d, shift, sc1p, g_norm, w_t)


def _split3(a):
    hi = a.astype(BF)
    r1 = a - hi.astype(F32)
    mid = r1.astype(BF)
    lo = (r1 - mid.astype(F32)).astype(BF)
    return hi, mid, lo


def _tri_matmul(tri, a):
    hi, mid, lo = _split3(a)
    return _dot(tri, hi) + _dot(tri, mid) + _dot(tri, lo)


def _chunks(tb):
    return [slice(c * GLA_CHUNK, (c + 1) * GLA_CHUNK) for c in range(tb // GLA_CHUNK)]


def _per_chunk_rows(rows, width):
    return jnp.concatenate([jnp.broadcast_to(r, (GLA_CHUNK, width)) for r in rows], axis=0)


def _gla_triangle(tb):
    row = lax.broadcasted_iota(jnp.int32, (tb, tb), 0)
    col = lax.broadcasted_iota(jnp.int32, (tb, tb), 1)
    return (((row // GLA_CHUNK) == (col // GLA_CHUNK)) & (col <= row)).astype(F32)


def _lane_mean(x, ones_b):
    hi = x.astype(BF)
    lo = (x - hi.astype(F32)).astype(BF)
    return (_dot(hi, ones_b) + _dot(lo, ones_b)) * (1.0 / LANES)


def _head(t, h, lo_h):
    blk = t[:, LANES * (h // 2):LANES * (h // 2 + 1)]
    return jnp.where(lo_h, blk, 0.0) if h % 2 == 0 else jnp.where(lo_h, 0.0, blk)


def _gla_block_common(qk, ga, wd, bd, tril_b):
    tb = qk.shape[0]
    q, k = qk[:, :256], qk[:, 256:]
    z = _dot(ga.astype(BF), wd) + bd
    la = (jnp.minimum(z, 0.0) - jnp.log(1.0 + jnp.exp(-jnp.abs(z)))) * (1.0 / GLA_TAU)
    b = _tri_matmul(tril_b, la)
    bls = [b[rs.stop - 1:rs.stop, :] for rs in _chunks(tb)]
    eq = jnp.exp(b)
    ek = jnp.exp(-b)
    f = jnp.exp(_per_chunk_rows(bls, 256) - b)
    return z, eq, ek, f, q * (eq * GLA_DK ** -0.5), k * ek, k * f, bls


def _gla_units(s, rows):
    sub = min(GLA_SUB, s)
    tb = min(rows, s)
    subs = [slice(i * sub, (i + 1) * sub) for i in range(tb // sub)]
    units = [(i, h) for i in range(len(subs)) for h in range(GLA_HEADS)]
    return tb, sub, subs, units


def _gla_fwd(proj, wdecp, bdec, ggla):
    s = proj.shape[0]
    tb, sub, subs, units = _gla_units(s, GLA_ROWS_FWD)
    nch = sub // GLA_CHUNK

    def body(qk_ref, v_ref, gz_ref, ga_ref, wd_ref, bd_ref, gg_ref, tri_ref, og_ref, opre_ref, sprev_ref, st_ref):
        @pl.when(pl.program_id(0) == 0)
        def _():
            st_ref[...] = jnp.zeros_like(st_ref)

        lo_h = lax.broadcasted_iota(jnp.int32, (sub, LANES), 1) < GLA_DK
        tril = tri_ref[...] > 0.5
        tril_b = tri_ref[...].astype(BF)
        ones_b = jnp.ones((LANES, LANES), BF)
        gg, wd, bd = gg_ref[...], wd_ref[...], bd_ref[...]
        chunks = _chunks(sub)
        lanes = [slice(h * LANES, (h + 1) * LANES) for h in range(GLA_HEADS)]
        com = [_gla_block_common(qk_ref[sl, :], ga_ref[sl, :], wd, bd, tril_b) for sl in subs]
        decs = [[jnp.exp(bl) for bl in cm[7]] for cm in com]
        a = {(i, h): _head(com[i][4], h, lo_h).astype(BF) for i, h in units}
        bm = {(i, h): _head(com[i][5], h, lo_h).astype(BF) for i, h in units}
        ktl = {(i, h): _head(com[i][6], h, lo_h).astype(BF) for i, h in units}
        vh = {(i, h): v_ref[subs[i], lanes[h]].astype(BF) for i, h in units}
        sc = {u: _dot(a[u], bm[u], NT) for u in units}
        upd = {u: [_dot(vh[u][rs], ktl[u][rs], TN) for rs in chunks] for u in units}
        p = {u: jnp.where(tril, sc[u], 0.0).astype(BF) for u in units}
        o = {u: _dot(p[u], vh[u]) for u in units}
        states = {}
        for h in range(GLA_HEADS):
            st = st_ref[h]
            for i in range(len(subs)):
                entering = []
                for c in range(nch):
                    entering.append(st)
                    sprev_ref[i * nch + c, h] = st
                    st = st * decs[i][c][:, LANES * (h // 2):LANES * (h // 2 + 1)] + upd[(i, h)][c]
                states[(i, h)] = entering
            st_ref[h] = st
        inter = {u: [_dot(a[u][rs], states[u][c].astype(BF), NT) for c, rs in enumerate(chunks)] for u in units}
        o = {u: o[u] + jnp.concatenate(inter[u], axis=0) for u in units}
        ms = {u: _lane_mean(o[u] * o[u], ones_b) for u in units}
        for i, h in units:
            gzh = gz_ref[subs[i], lanes[h]]
            opre_ref[subs[i], lanes[h]] = o[(i, h)]
            og_ref[subs[i], lanes[h]] = (((o[(i, h)] * lax.rsqrt(ms[(i, h)] + RMS_EPS)) * gg[:, lanes[h]])
                                         * (gzh * _sigmoid(gzh))).astype(og_ref.dtype)

    def col(width, off):
        return pl.BlockSpec((tb, width), lambda i: (i, off // width))

    return pl.pallas_call(
        body, name="gla_fwd", grid=(s // tb,),
        in_specs=[col(512, OFF_QK), col(512, OFF_V), col(512, OFF_GZ), col(LANES, OFF_GA),
                  _full((LANES, 256)), _full((1, 256)), _full((1, 512)), _full((sub, sub))],
        out_specs=[pl.BlockSpec((tb, 512), lambda i: (i, 0)), pl.BlockSpec((tb, 512), lambda i: (i, 0)),
                   pl.BlockSpec((tb // GLA_CHUNK, GLA_HEADS, LANES, LANES), lambda i: (i, 0, 0, 0))],
        out_shape=[jax.ShapeDtypeStruct((s, 512), BF), jax.ShapeDtypeStruct((s, 512), F32),
                   jax.ShapeDtypeStruct((s // GLA_CHUNK, GLA_HEADS, LANES, LANES), F32)],
        scratch_shapes=[pltpu.VMEM((GLA_HEADS, LANES, LANES), F32)],
        compiler_params=_params(("arbitrary",)),
    )(proj, proj, proj, proj, wdecp, bdec, ggla, _gla_triangle(sub))


def _gla_bwd(proj, dog, opre, sprev, wdecp, bdec, ggla):
    s = proj.shape[0]
    tb, sub, subs, units = _gla_units(s, GLA_ROWS_BWD)
    nsub = len(subs)
    nch = sub // GLA_CHUNK
    nb = s // tb

    def body(qk_ref, v_ref, gz_ref, ga_ref, dog_ref, opre_ref, sprev_ref, wd_ref, bd_ref, gg_ref, tri_ref, triu_ref,
             dqk_ref, dv_ref, dgz_ref, dga_ref, dwd_ref, dbd_ref, dgg_ref, dst_ref):
        @pl.when(pl.program_id(0) == 0)
        def _():
            dst_ref[...] = jnp.zeros_like(dst_ref)
            dwd_ref[...] = jnp.zeros_like(dwd_ref)
            dbd_ref[...] = jnp.zeros_like(dbd_ref)
            dgg_ref[...] = jnp.zeros_like(dgg_ref)

        lo_h = lax.broadcasted_iota(jnp.int32, (sub, LANES), 1) < GLA_DK
        tril = tri_ref[...] > 0.5
        tril_b = tri_ref[...].astype(BF)
        triu_b = triu_ref[...].astype(BF)
        ones_b = jnp.ones((LANES, LANES), BF)
        last_row = (lax.broadcasted_iota(jnp.int32, (sub, LANES), 0) % GLA_CHUNK) == GLA_CHUNK - 1
        wd, gg, bd = wd_ref[...], gg_ref[...], bd_ref[...]
        chunks = _chunks(sub)
        lanes = [slice(h * LANES, (h + 1) * LANES) for h in range(GLA_HEADS)]
        blks = [slice(LANES * (h // 2), LANES * (h // 2 + 1)) for h in range(GLA_HEADS)]
        ga = [ga_ref[sl, :] for sl in subs]
        com = [_gla_block_common(qk_ref[sl, :], ga[i], wd, bd, tril_b) for i, sl in enumerate(subs)]
        decs = [[jnp.exp(bl) for bl in cm[7]] for cm in com]
        a = {(i, h): _head(com[i][4], h, lo_h).astype(BF) for i, h in units}
        bm = {(i, h): _head(com[i][5], h, lo_h).astype(BF) for i, h in units}
        ktl = {(i, h): _head(com[i][6], h, lo_h).astype(BF) for i, h in units}
        vh = {(i, h): v_ref[subs[i], lanes[h]].astype(BF) for i, h in units}
        sc = {u: _dot(a[u], bm[u], NT) for u in units}

        o = {(i, h): opre_ref[subs[i], lanes[h]] for i, h in units}
        ms = {u: _lane_mean(o[u] * o[u], ones_b) for u in units}
        gz = {(i, h): gz_ref[subs[i], lanes[h]] for i, h in units}
        dog = {(i, h): dog_ref[subs[i], lanes[h]] for i, h in units}
        sg = {u: _sigmoid(gz[u]) for u in units}
        r = {u: lax.rsqrt(ms[u] + RMS_EPS) for u in units}
        ohat = {u: o[u] * r[u] for u in units}
        sil = {u: gz[u] * sg[u] for u in units}
        for i, h in units:
            u = (i, h)
            dgz_ref[subs[i], lanes[h]] = (dog[u] * (ohat[u] * gg[:, lanes[h]])
                                          * (sg[u] * (1.0 + gz[u] * (1.0 - sg[u])))).astype(dgz_ref.dtype)
            dgg_ref[:, lanes[h]] += jnp.sum(dog[u] * sil[u] * ohat[u], axis=0, keepdims=True)
        dn = {(i, h): dog[(i, h)] * sil[(i, h)] * gg[:, lanes[h]] for i, h in units}
        mdn = {u: _lane_mean(dn[u] * ohat[u], ones_b) for u in units}
        do = {u: (r[u] * (dn[u] - ohat[u] * mdn[u])).astype(BF) for u in units}

        p = {u: jnp.where(tril, sc[u], 0.0).astype(BF) for u in units}
        dpr = {u: _dot(do[u], vh[u], NT) for u in units}
        incr = {u: [_dot(do[u][rs], a[u][rs], TN) for rs in chunks] for u in units}
        dv = {u: _dot(p[u], do[u], TN) for u in units}
        dp = {u: jnp.where(tril, dpr[u], 0.0).astype(BF) for u in units}
        dqd = {u: _dot(dp[u], bm[u]) for u in units}
        dkd = {u: _dot(dp[u], a[u], TN) for u in units}
        st = {(i, h): [sprev_ref[i * nch + c, h] for c in range(nch)] for i, h in units}
        leaving = {}
        for h in range(GLA_HEADS):
            d = dst_ref[h]
            for i in reversed(range(nsub)):
                out = [None] * nch
                for c in reversed(range(nch)):
                    out[c] = d
                    d = d * decs[i][c][:, blks[h]] + incr[(i, h)][c]
                leaving[(i, h)] = out
            dst_ref[h] = d
        lv_b = {u: [leaving[u][c].astype(BF) for c in range(nch)] for u in units}
        dv_s = {u: [_dot(ktl[u][rs], lv_b[u][c], NT) for c, rs in enumerate(chunks)] for u in units}
        dqd_s = {u: [_dot(do[u][rs], st[u][c].astype(BF)) for c, rs in enumerate(chunks)] for u in units}
        dkt_s = {u: [_dot(vh[u][rs], lv_b[u][c]) for c, rs in enumerate(chunks)] for u in units}
        ddec = {u: [jnp.sum(leaving[u][c] * st[u][c], axis=0, keepdims=True) for c in range(nch)] for u in units}
        for i, h in units:
            dv_ref[subs[i], lanes[h]] = (dv[(i, h)] + jnp.concatenate(dv_s[(i, h)], axis=0)).astype(dv_ref.dtype)
        dqd = {u: dqd[u] + jnp.concatenate(dqd_s[u], axis=0) for u in units}
        dkt = {u: jnp.concatenate(dkt_s[u], axis=0) for u in units}

        db = []
        for i, sl in enumerate(subs):
            _, eq, ek, f, qd, kd, kt, _ = com[i]
            parts = []
            for pair in range(GLA_HEADS // 2):
                blk, u0, u1 = blks[2 * pair], (i, 2 * pair), (i, 2 * pair + 1)
                dqd_b, dkd_b, dkt_b = dqd[u0] + dqd[u1], dkd[u0] + dkd[u1], dkt[u0] + dkt[u1]
                dqk_ref[sl, blk] = (dqd_b * (eq[:, blk] * GLA_DK ** -0.5)).astype(dqk_ref.dtype)
                dqk_ref[sl, 256 + LANES * pair:256 + LANES * (pair + 1)] = (dkd_b * ek[:, blk] + dkt_b * f[:, blk]).astype(dqk_ref.dtype)
                dkt_kt = dkt_b * kt[:, blk]
                dbp = dqd_b * qd[:, blk] - dkd_b * kd[:, blk] - dkt_kt
                dbl = [jnp.sum(dkt_kt[rs], axis=0, keepdims=True) + (ddec[u0][c] + ddec[u1][c]) * decs[i][c][:, blk]
                       for c, rs in enumerate(chunks)]
                parts.append(jnp.where(last_row, dbp + _per_chunk_rows(dbl, LANES), dbp))
            db.append(jnp.concatenate(parts, axis=1))
        dla = [_tri_matmul(triu_b, db[i]) for i in range(nsub)]
        dz32 = [dla[i] * (1.0 / GLA_TAU) * _sigmoid(-com[i][0]) for i in range(nsub)]
        dz = [t.astype(BF) for t in dz32]
        for i, sl in enumerate(subs):
            dga_ref[sl, :] = _dot(dz[i], wd, NT).astype(dga_ref.dtype)
            dwd_ref[...] += _dot(ga[i].astype(BF), dz[i], TN)
            dbd_ref[...] += jnp.sum(dz32[i], axis=0, keepdims=True)

    def col(width, off):
        return pl.BlockSpec((tb, width), lambda i: (nb - 1 - i, off // width))

    def rev(width):
        return pl.BlockSpec((tb, width), lambda i: (nb - 1 - i, 0))

    return pl.pallas_call(
        body, name="gla_bwd", grid=(nb,),
        in_specs=[col(512, OFF_QK), col(512, OFF_V), col(512, OFF_GZ), col(LANES, OFF_GA), rev(512), rev(512),
                  pl.BlockSpec((tb // GLA_CHUNK, GLA_HEADS, LANES, LANES), lambda i: (nb - 1 - i, 0, 0, 0)),
                  _full((LANES, 256)), _full((1, 256)), _full((1, 512)), _full((sub, sub)), _full((sub, sub))],
        out_specs=[rev(512), rev(512), rev(512), rev(LANES), _full((LANES, 256)), _full((1, 256)), _full((1, 512))],
        out_shape=[jax.ShapeDtypeStruct((s, 512), BF), jax.ShapeDtypeStruct((s, 512), BF),
                   jax.ShapeDtypeStruct((s, 512), BF), jax.ShapeDtypeStruct((s, LANES), BF),
                   jax.ShapeDtypeStruct((LANES, 256), F32), jax.ShapeDtypeStruct((1, 256), F32),
                   jax.ShapeDtypeStruct((1, 512), F32)],
        scratch_shapes=[pltpu.VMEM((GLA_HEADS, LANES, LANES), F32)],
        compiler_params=_params(("arbitrary",)),
    )(proj, proj, proj, proj, dog, opre, sprev, wdecp, bdec, ggla, _gla_triangle(sub), _gla_triangle(sub).T)


_SWA_COL_HEADS = (0, 2, 1, 3, 4, 6, 5, 7)
_SWA_COLS = SWA_HEADS * SWA_BLOCK


def _swa_masks():
    lo2 = lax.broadcasted_iota(jnp.int32, (2 * SWA_BLOCK, LANES), 1) < 64
    lane1 = lax.broadcasted_iota(jnp.int32, (SWA_BLOCK, LANES), 1)
    first_half = (lane1 % 64) < 32
    key = lax.broadcasted_iota(jnp.int32, (SWA_BLOCK, _SWA_COLS), 0)
    query = lax.broadcasted_iota(jnp.int32, (SWA_BLOCK, _SWA_COLS), 1) % SWA_BLOCK
    return lo2, lane1 < 64, first_half, key > query


def _merge_band(t, prev_mask, prev_bias=None):
    prev = t[:SWA_BLOCK] if prev_bias is None else t[:SWA_BLOCK] + prev_bias
    return jnp.where(prev_mask, prev, t[SWA_BLOCK:])


def _split_band(t, prev_mask_b):
    prev = t * prev_mask_b
    return jnp.concatenate([prev, t - prev], axis=0)


def _kv_variants(t, lo2):
    tr = pltpu.roll(t, 64, 1)
    lo_v = [jnp.where(lo2, t, 0.0).astype(BF), jnp.where(lo2, tr, 0.0).astype(BF)]
    hi_v = [jnp.where(lo2, 0.0, tr).astype(BF), jnp.where(lo2, 0.0, t).astype(BF)]
    return lo_v, hi_v


def _kv_variants_t(t):
    tt = t.T
    sw = jnp.concatenate([tt[64:], tt[:64]], axis=0)
    top = lax.broadcasted_iota(jnp.int32, tt.shape, 0) < 64
    lo_v = [jnp.where(top, tt, 0.0).astype(BF), jnp.where(top, sw, 0.0).astype(BF)]
    hi_v = [jnp.where(top, 0.0, sw).astype(BF), jnp.where(top, 0.0, tt).astype(BF)]
    return lo_v, hi_v


def _swa_scores(qg, k_lo, k_hi):
    return jnp.concatenate([_dot(k_lo[0], qg[0], NT), _dot(k_hi[0], qg[0], NT),
                            _dot(k_lo[1], qg[1], NT), _dot(k_hi[1], qg[1], NT)], axis=1)


def _sink_row(sinks_ref):
    return jnp.concatenate([jnp.full((1, SWA_BLOCK), sinks_ref[0, hd], F32) for hd in _SWA_COL_HEADS], axis=1)


def _swa_softmax(st, prev_mask, prev_bias, sink):
    st = _merge_band(st, prev_mask, prev_bias)
    m = jnp.maximum(jnp.max(st, axis=0, keepdims=True), sink)
    ex = jnp.exp(st - m)
    es = jnp.exp(sink - m)
    inv = 1.0 / (jnp.sum(ex, axis=0, keepdims=True) + es)
    return ex, es, inv


def _no_prev_bias(block_index):
    return jnp.where(block_index > 0, 0.0, -1e30).astype(F32)


def _swa_queries(sq_ref, rows, cosb, sinb, first_half):
    qs = [_rope(sq_ref[rows, p * LANES:(p + 1) * LANES], cosb, sinb, first_half) * 0.125 for p in range(4)]
    return [jnp.concatenate(qs[0:2], axis=0), jnp.concatenate(qs[2:4], axis=0)]


def _phase_steps(nsteps, phases):
    return [min(nsteps - 1, (k * nsteps) // phases) for k in range(phases - 1)] + [nsteps - 1]


def _swa_fwd(proj, cos, sin, sinks, half_out):
    s = proj.shape[0]
    nq = min(SWA_QBLOCKS_FWD, s // SWA_BLOCK)
    tq = nq * SWA_BLOCK
    steps = _phase_steps(s // tq, 4)

    def body(sq_ref, sz_ref, sk_ref, sv_ref, cos_ref, sin_ref, sinks_ref, hout_hbm, os_ref, opre_ref, wout_hbm,
             kprev, vprev, *gather_sems):
        n = pl.program_id(0)

        @pl.when(n == 0)
        def _():
            kprev[...] = jnp.zeros_like(kprev)
            vprev[...] = jnp.zeros_like(vprev)

        gather = _Gather(hout_hbm, wout_hbm, *gather_sems, chunks=WEIGHT_CHUNKS)
        for step, phase in zip(steps, (gather.start, gather.pass_on, gather.relay_diagonal, gather.finish)):
            pl.when(n == step)(phase)

        lo2, _, first_half, prev_mask = _swa_masks()
        prev_mask_b = jnp.where(prev_mask, 1.0, 0.0).astype(BF)
        sink = _sink_row(sinks_ref)
        blocks = range(nq)
        rows = [slice(j * SWA_BLOCK, (j + 1) * SWA_BLOCK) for j in blocks]
        cosb = [cos_ref[rows[j], :] for j in blocks]
        sinb = [sin_ref[rows[j], :] for j in blocks]
        kc = [_rope(sk_ref[rows[j], :], cosb[j], sinb[j], first_half) for j in blocks]
        vc = [sv_ref[rows[j], :] for j in blocks]
        kcat = [jnp.concatenate([kprev[...] if j == 0 else kc[j - 1], kc[j]], axis=0) for j in blocks]
        vcat = [jnp.concatenate([vprev[...] if j == 0 else vc[j - 1], vc[j]], axis=0) for j in blocks]
        kprev[...] = kc[-1]
        vprev[...] = vc[-1]
        kvar = [_kv_variants(kcat[j], lo2) for j in blocks]
        vtvar = [_kv_variants_t(vcat[j]) for j in blocks]
        qg = [[q.astype(BF) for q in _swa_queries(sq_ref, rows[j], cosb[j], sinb[j], first_half)] for j in blocks]
        st = [_swa_scores(qg[j], *kvar[j]) for j in blocks]
        soft = [_swa_softmax(st[j], prev_mask, _no_prev_bias(n) if j == 0 else None, sink) for j in blocks]
        pt = [_split_band(soft[j][0].astype(BF), prev_mask_b) for j in blocks]
        og = {}
        for j in blocks:
            inv = soft[j][2]
            for g in range(2):
                c0, c1, c2 = 512 * g, 512 * g + 256, 512 * g + 512
                ot = (_dot(vtvar[j][0][g], pt[j][:, c0:c1]) * inv[:, c0:c1]
                      + _dot(vtvar[j][1][g], pt[j][:, c1:c2]) * inv[:, c1:c2])
                og[(j, g)] = ot.T
        for j in blocks:
            for g in range(2):
                for i in range(2):
                    ls = slice((2 * g + i) * LANES, (2 * g + i + 1) * LANES)
                    o = og[(j, g)][i * SWA_BLOCK:(i + 1) * SWA_BLOCK]
                    sz = sz_ref[rows[j], ls]
                    opre_ref[rows[j], ls] = o
                    os_ref[rows[j], ls] = (o * (sz * _sigmoid(sz))).astype(os_ref.dtype)

    def col(width, off):
        return pl.BlockSpec((tq, width), lambda i: (i, off // width))

    row = pl.BlockSpec((tq, LANES), lambda i: (i, 0))
    return pl.pallas_call(
        body, name="swa_fwd", grid=(s // tq,),
        in_specs=[col(512, OFF_SQ), col(512, OFF_SZ), col(LANES, OFF_SK), col(LANES, OFF_SV), row, row,
                  pl.BlockSpec(memory_space=pltpu.SMEM), pl.BlockSpec(memory_space=pl.ANY)],
        out_specs=[pl.BlockSpec((tq, 512), lambda i: (i, 0))] * 2 + [pl.BlockSpec(memory_space=pl.ANY)],
        out_shape=[jax.ShapeDtypeStruct((s, 512), BF), jax.ShapeDtypeStruct((s, 512), F32),
                   jax.ShapeDtypeStruct((8,) + half_out.shape, half_out.dtype)],
        scratch_shapes=[pltpu.VMEM((SWA_BLOCK, LANES), F32)] * 2 + _gather_sems(WEIGHT_CHUNKS),
        compiler_params=_params(("arbitrary",)),
    )(proj, proj, proj, proj, cos, sin, sinks, half_out)


def _swa_bwd(proj, dos, opre, cos, sin, sinks, dw_out_parts):
    s = proj.shape[0]
    nq = min(SWA_QBLOCKS, s // SWA_BLOCK)
    tq = nq * SWA_BLOCK
    steps = _phase_steps(s // tq, 5)
    _, r_out, c_out = dw_out_parts.shape

    def body(sq_ref, sz_ref, sk_ref, sv_ref, dos_ref, opre_ref, cos_ref, sin_ref, sinks_ref, pout_hbm,
             dsq_ref, dsz_ref, dsk_ref, dsv_ref, dsink_ref, gout_hbm, kprev, vprev, cprev, sprev, *reduce_scratch):
        n = pl.program_id(0)

        @pl.when(n == 0)
        def _():
            kprev[...] = jnp.zeros_like(kprev)
            vprev[...] = jnp.zeros_like(vprev)
            cprev[...] = jnp.zeros_like(cprev)
            sprev[...] = jnp.zeros_like(sprev)
            for hd in range(SWA_HEADS):
                dsink_ref[0, hd] = 0.0

        reduce = _Reduce(pout_hbm, gout_hbm, *reduce_scratch)
        phases = (reduce.start, reduce.combine_and_send, reduce.send_joint, reduce.total_and_share, reduce.finish)
        for step, phase in zip(steps, phases):
            pl.when(n == step)(phase)

        lo2, lo1, first_half, prev_mask = _swa_masks()
        prev_mask_b = jnp.where(prev_mask, 1.0, 0.0).astype(BF)
        lo1s = jnp.concatenate([lo1, lo1], axis=0)
        sink = _sink_row(sinks_ref)

        def home(m0, m1):
            t0 = m0 + pltpu.roll(m0, 64, 1)
            t1 = m1 + pltpu.roll(m1, 64, 1)
            return jnp.where(lo2, t0, t1)

        kp, vp, cp_, sp_ = kprev[...], vprev[...], cprev[...], sprev[...]
        for j in range(nq):
            rows = slice(j * SWA_BLOCK, (j + 1) * SWA_BLOCK)
            blk = n * nq + j
            cosb, sinb = cos_ref[rows, :], sin_ref[rows, :]
            kc = _rope(sk_ref[rows, :], cosb, sinb, first_half)
            vc = sv_ref[rows, :]
            kcat = jnp.concatenate([kp, kc], axis=0)
            k_lo, k_hi = _kv_variants(kcat, lo2)
            kt_lo, kt_hi = _kv_variants_t(kcat)
            v_lo, v_hi = _kv_variants(jnp.concatenate([vp, vc], axis=0), lo2)
            qg32 = _swa_queries(sq_ref, rows, cosb, sinb, first_half)
            qg = [q.astype(BF) for q in qg32]
            ex, es, inv = _swa_softmax(_swa_scores(qg, k_lo, k_hi), prev_mask, _no_prev_bias(n) if j == 0 else None, sink)
            pr, ps = ex * inv, es * inv

            dog32 = []
            for g in range(2):
                parts = []
                for i in range(2):
                    ls = slice((2 * g + i) * LANES, (2 * g + i + 1) * LANES)
                    sz = sz_ref[rows, ls]
                    sg = _sigmoid(sz)
                    dos_p = dos_ref[rows, ls]
                    dsz_ref[rows, ls] = (dos_p * opre_ref[rows, ls] * (sg * (1.0 + sz * (1.0 - sg)))).astype(dsz_ref.dtype)
                    parts.append(dos_p * (sz * sg))
                dog32.append(jnp.concatenate(parts, axis=0))
            dog = [t.astype(BF) for t in dog32]
            dpr = _merge_band(jnp.concatenate([_dot(v_lo[0], dog[0], NT), _dot(v_hi[0], dog[0], NT),
                                               _dot(v_lo[1], dog[1], NT), _dot(v_hi[1], dog[1], NT)], axis=1), prev_mask)
            rd = jnp.sum(pr * dpr, axis=0, keepdims=True)
            ds = _split_band((pr * (dpr - rd)).astype(BF), prev_mask_b)
            prb = _split_band(pr.astype(BF), prev_mask_b)
            sink_term = ps * rd
            for r, hd in enumerate(_SWA_COL_HEADS):
                dsink_ref[0, hd] += -jnp.sum(sink_term[:, r * SWA_BLOCK:(r + 1) * SWA_BLOCK])

            dk_g, dv_g = [], []
            for g in range(2):
                c0, c1, c2 = 512 * g, 512 * g + 256, 512 * g + 512
                dq = (_dot(kt_lo[g], ds[:, c0:c1]) + _dot(kt_hi[g], ds[:, c1:c2])).T
                for i in range(2):
                    ls = slice((2 * g + i) * LANES, (2 * g + i + 1) * LANES)
                    dsq_ref[rows, ls] = _rope_t(dq[i * SWA_BLOCK:(i + 1) * SWA_BLOCK] * 0.125, cosb, sinb,
                                                first_half).astype(dsq_ref.dtype)
                q_split = jnp.concatenate([jnp.where(lo1s, qg32[g], 0.0), jnp.where(lo1s, 0.0, qg32[g])], axis=0).astype(BF)
                do_split = jnp.concatenate([jnp.where(lo1s, dog32[g], 0.0), jnp.where(lo1s, 0.0, dog32[g])], axis=0).astype(BF)
                dk_g.append(_dot(ds[:, c0:c2], q_split))
                dv_g.append(_dot(prb[:, c0:c2], do_split))
            dk = home(dk_g[0], dk_g[1])
            dv = home(dv_g[0], dv_g[1])
            cur = pl.ds(pl.multiple_of(blk * SWA_BLOCK, SWA_BLOCK), SWA_BLOCK)
            dsk_ref[cur, :] = _rope_t(dk[SWA_BLOCK:], cosb, sinb, first_half)
            dsv_ref[cur, :] = dv[SWA_BLOCK:]
            dk_prev = _rope_t(dk[:SWA_BLOCK], cp_, sp_, first_half)
            dv_prev = dv[:SWA_BLOCK]
            if j == 0:
                @pl.when(n > 0)
                def _():
                    prv = pl.ds(pl.multiple_of((blk - 1) * SWA_BLOCK, SWA_BLOCK), SWA_BLOCK)
                    dsk_ref[prv, :] += dk_prev
                    dsv_ref[prv, :] += dv_prev
            else:
                prv = pl.ds(pl.multiple_of((blk - 1) * SWA_BLOCK, SWA_BLOCK), SWA_BLOCK)
                dsk_ref[prv, :] += dk_prev
                dsv_ref[prv, :] += dv_prev
            kp, vp, cp_, sp_ = kc, vc, cosb, sinb
        kprev[...] = kp
        vprev[...] = vp
        cprev[...] = cp_
        sprev[...] = sp_

    def col(width, off):
        return pl.BlockSpec((tq, width), lambda i: (i, off // width))

    row = pl.BlockSpec((tq, LANES), lambda i: (i, 0))
    wide = pl.BlockSpec((tq, 512), lambda i: (i, 0))
    return pl.pallas_call(
        body, name="swa_bwd", grid=(s // tq,),
        in_specs=[col(512, OFF_SQ), col(512, OFF_SZ), col(LANES, OFF_SK), col(LANES, OFF_SV), wide, wide, row, row,
                  pl.BlockSpec(memory_space=pltpu.SMEM), pl.BlockSpec(memory_space=pl.ANY)],
        out_specs=[wide, wide, _full((s, LANES)), _full((s, LANES)), pl.BlockSpec(memory_space=pltpu.SMEM),
                   pl.BlockSpec(memory_space=pl.ANY)],
        out_shape=[jax.ShapeDtypeStruct((s, 512), BF), jax.ShapeDtypeStruct((s, 512), BF),
                   jax.ShapeDtypeStruct((s, LANES), F32), jax.ShapeDtypeStruct((s, LANES), F32),
                   jax.ShapeDtypeStruct((1, SWA_HEADS), F32), jax.ShapeDtypeStruct((r_out, c_out), F32)],
        scratch_shapes=[pltpu.VMEM((SWA_BLOCK, LANES), F32)] * 4 + _reduce_scratch(r_out, c_out),
        compiler_params=_params(("arbitrary",)),
    )(proj, proj, proj, proj, dos, opre, cos, sin, sinks, dw_out_parts)


def _outproj(og, osw, w_out, x2d, target, gate, g_final):
    s = x2d.shape[0]
    tm = min(512, s)

    def body(og_ref, os_ref, w_ref, x_ref, t_ref, gate_ref, gf_ref,
             dx2_ref, dog_ref, dos_ref, dw_ref, loss_ref, dgf_ref, dgate_ref):
        @pl.when(pl.program_id(0) == 0)
        def _():
            dw_ref[...] = jnp.zeros_like(dw_ref)
            loss_ref[...] = jnp.zeros_like(loss_ref)
            dgf_ref[...] = jnp.zeros_like(dgf_ref)
            dgate_ref[...] = jnp.zeros_like(dgate_ref)

        w = w_ref[...]
        gate, gf = gate_ref[...], gf_ref[...]
        subs = _subtiles(tm)
        ogv = [og_ref[sl, :] for sl in subs]
        osv = [os_ref[sl, :] for sl in subs]
        y = [_dot(ogv[k], w[:512]) + _dot(osv[k], w[512:]) for k in range(len(subs))]
        dys = []
        for k, sl in enumerate(subs):
            x2 = x_ref[sl, :] + gate * y[k]
            r = lax.rsqrt(jnp.mean(x2 * x2, axis=-1, keepdims=True) + RMS_EPS)
            xn = x2 * r
            err = xn * gf - t_ref[sl, :]
            loss_ref[...] += 0.5 * jnp.sum(jnp.mean(err * err, axis=-1, keepdims=True), axis=0, keepdims=True)
            dyf = err * (1.0 / D_MODEL)
            dgf_ref[...] += jnp.sum(dyf * xn, axis=0, keepdims=True)
            t = dyf * gf
            dx2 = r * (t - xn * jnp.mean(t * xn, axis=-1, keepdims=True))
            dx2_ref[sl, :] = dx2
            dgate_ref[...] += jnp.sum(dx2 * y[k], axis=0, keepdims=True)
            dys.append((dx2 * gate).astype(BF))
            dmix = _dot(dys[k], w, NT)
            dog_ref[sl, :] = dmix[:, :512]
            dos_ref[sl, :] = dmix[:, 512:]
        dy = jnp.concatenate(dys, axis=0)
        dw_ref[:512, :] += _dot(og_ref[...], dy, TN)
        dw_ref[512:, :] += _dot(os_ref[...], dy, TN)

    half = pl.BlockSpec((tm, 512), lambda i: (i, 0))
    rowb = pl.BlockSpec((tm, D_MODEL), lambda i: (i, 0))
    vec = _full((1, D_MODEL))
    return pl.pallas_call(
        body, name="outproj", grid=(s // tm,),
        in_specs=[half, half, _full((D_MODEL, D_MODEL)), rowb, rowb, vec, vec],
        out_specs=[rowb, half, half, _full((D_MODEL, D_MODEL)), _full((1, 1)), vec, vec],
        out_shape=[jax.ShapeDtypeStruct((s, D_MODEL), F32), jax.ShapeDtypeStruct((s, 512), F32),
                   jax.ShapeDtypeStruct((s, 512), F32), jax.ShapeDtypeStruct((D_MODEL, D_MODEL), F32),
                   jax.ShapeDtypeStruct((1, 1), F32), jax.ShapeDtypeStruct((1, D_MODEL), F32),
                   jax.ShapeDtypeStruct((1, D_MODEL), F32)],
        compiler_params=_params(("arbitrary",)),
    )(og, osw, w_out, x2d, target, gate, g_final)


_PIECES = ((OFF_QK, 512), (OFF_V, 512), (OFF_GZ, 512), (OFF_SQ, 512), (OFF_SZ, 512),
           (OFF_SK, LANES), (OFF_SV, LANES), (OFF_GA, LANES))

_UNPAD_ROWS = ((OFF_QK, 0, 1024),
               (OFF_GA, 1024, GLA_RANK),
               (OFF_GZ, 1040, 1024),
               (OFF_SK, 2064, 256),
               (OFF_SZ, 2320, 512))


def _inproj_bwd(x2d, shift, sc1p, g_norm, w_t, dx2, pieces):
    s = x2d.shape[0]
    tm = min(512, s)
    nsteps = s // tm

    def body(x_ref, sh_ref, sc_ref, g_ref, w_hbm, dx2_ref, *rest):
        piece_refs = rest[:len(_PIECES)]
        gx_ref, dw_hbm, dsh_ref, dsc_ref, dg_ref, w_vm, dw_vm, in_sems, out_sems = rest[len(_PIECES):]
        i = pl.program_id(0)

        @pl.when(i == 0)
        def _():
            loads = _load_w_padded(w_hbm, w_vm, in_sems)
            dw_vm[...] = jnp.zeros_like(dw_vm)
            dsh_ref[...] = jnp.zeros_like(dsh_ref)
            dsc_ref[...] = jnp.zeros_like(dsc_ref)
            dg_ref[...] = jnp.zeros_like(dg_ref)
            for cp in loads:
                cp.wait()

        g, sc1p_v, shift_v = g_ref[...], sc_ref[...], sh_ref[...]
        subs = _subtiles(tm)
        dhs = []
        for sl in subs:
            dh = None
            for (off, width), pr in zip(_PIECES, piece_refs):
                part = _dot(pr[sl, :].astype(BF), w_vm[off:off + width, :])
                dh = part if dh is None else dh + part
            dhs.append(dh)
        norm = [_modnorm(x_ref[sl, :], g, sc1p_v, shift_v) for sl in subs]
        hb = jnp.concatenate([h.astype(BF) for _, _, h in norm], axis=0)
        for (off, width), pr in zip(_PIECES, piece_refs):
            dw_vm[off:off + width, :] += _dot(pr[...].astype(BF), hb, TN)
        for sl, (xn, r, _), dh in zip(subs, norm, dhs):
            dsh_ref[...] += jnp.sum(dh, axis=0, keepdims=True)
            dsc_ref[...] += jnp.sum(dh * (xn * g), axis=0, keepdims=True)
            dg_ref[...] += jnp.sum(dh * xn * sc1p_v, axis=0, keepdims=True)
            dxn = dh * g * sc1p_v
            gx_ref[sl, :] = dx2_ref[sl, :] + r * (dxn - xn * jnp.mean(dxn * xn, axis=-1, keepdims=True))

        @pl.when(i == nsteps - 1)
        def _():
            copies = [pltpu.make_async_copy(dw_vm.at[src:src + n], dw_hbm.at[dst:dst + n], out_sems.at[k])
                      for k, (src, dst, n) in enumerate(_UNPAD_ROWS)]
            for cp in copies:
                cp.start()
            for cp in copies:
                cp.wait()

    rowb = pl.BlockSpec((tm, D_MODEL), lambda i: (i, 0))
    vec = _full((1, D_MODEL))
    anyspec = pl.BlockSpec(memory_space=pl.ANY)
    piece_specs = [pl.BlockSpec((tm, width), lambda i: (i, 0)) for _, width in _PIECES]
    return pl.pallas_call(
        body, name="inproj_bwd", grid=(nsteps,),
        in_specs=[rowb, vec, vec, vec, anyspec, rowb] + piece_specs,
        out_specs=[rowb, anyspec, vec, vec, vec],
        out_shape=[jax.ShapeDtypeStruct((s, D_MODEL), F32), jax.ShapeDtypeStruct((D_IN, D_MODEL), F32),
                   jax.ShapeDtypeStruct((1, D_MODEL), F32), jax.ShapeDtypeStruct((1, D_MODEL), F32),
                   jax.ShapeDtypeStruct((1, D_MODEL), F32)],
        scratch_shapes=[pltpu.VMEM((D_PAD, D_MODEL), BF), pltpu.VMEM((D_PAD, D_MODEL), F32),
                        pltpu.SemaphoreType.DMA((len(_UNPAD_ROWS),)), pltpu.SemaphoreType.DMA((len(_UNPAD_ROWS),))],
        compiler_params=_params(("arbitrary",)),
    )(x2d, shift, sc1p, g_norm, w_t, dx2, *pieces)


def _adam(w, g, m, v):
    m2 = ADAM_B1 * m + (1.0 - ADAM_B1) * g
    v2 = ADAM_B2 * v + (1.0 - ADAM_B2) * (g * g)
    m_hat = m2 / (1.0 - ADAM_B1 ** ADAM_STEP)
    v_hat = v2 / (1.0 - ADAM_B2 ** ADAM_STEP)
    delta = -ADAM_LR * (m_hat / (jnp.sqrt(v_hat) + ADAM_EPS) + ADAM_WD * w)
    return delta, m2, v2


def _adamw_t(w3, g_window, m3, v3, name):
    rr, _, cc = w3.shape
    parts = [slice(q * (cc // 4), (q + 1) * (cc // 4)) for q in range(4)]
    starts = sorted({(rr * j) % 8 for j in range(4)})

    def body(w_hbm, gw_hbm, m_hbm, v_hbm, d_hbm, m2_hbm, v2_hbm, g3_hbm,
             w_vm, m_vm, v_vm, gw_vm, d_vm, m2_vm, v2_vm, g_vm, in_sems, out_sems):
        start = lax.rem(rr * (2 * lax.axis_index("x") + lax.axis_index("y")), 8)
        ins = ((w_hbm, w_vm), (m_hbm, m_vm), (v_hbm, v_vm))
        outs = ((d_vm, d_hbm), (m2_vm, m2_hbm), (v2_vm, v2_hbm), (g_vm, g3_hbm))
        loads = [[pltpu.make_async_copy(src.at[:, 0, p], dst.at[:, p], in_sems.at[4 * q + k]) for k, (src, dst) in enumerate(ins)]
                 + [pltpu.make_async_copy(gw_hbm.at[:, p], gw_vm.at[:, p], in_sems.at[4 * q + 3])]
                 for q, p in enumerate(parts)]
        stores = [[pltpu.make_async_copy(src.at[:, p], dst.at[:, 0, p], out_sems.at[4 * q + k]) for k, (src, dst) in enumerate(outs)]
                  for q, p in enumerate(parts)]
        for group in loads:
            for cp in group:
                cp.start()
        for q, p in enumerate(parts):
            for cp in loads[q]:
                cp.wait()
            g = gw_vm[starts[0]:starts[0] + rr, p]
            for o in starts[1:]:
                g = jnp.where(start == o, gw_vm[o:o + rr, p], g)
            g_vm[:, p] = g
            d_vm[:, p], m2_vm[:, p], v2_vm[:, p] = _adam(w_vm[:, p], g, m_vm[:, p], v_vm[:, p])
            for cp in stores[q]:
                cp.start()
        for group in stores:
            for cp in group:
                cp.wait()

    hbm = pl.BlockSpec(memory_space=pl.ANY)
    return pl.pallas_call(
        body, name=name, grid=(1,), in_specs=[hbm] * 4,
        out_specs=[hbm] * 4, out_shape=[jax.ShapeDtypeStruct((rr, 1, cc), F32)] * 4,
        scratch_shapes=[pltpu.VMEM((rr, cc), F32)] * 3 + [pltpu.VMEM(g_window.shape, F32)] + [pltpu.VMEM((rr, cc), F32)] * 4
        + [pltpu.SemaphoreType.DMA((16,)), pltpu.SemaphoreType.DMA((16,))],
        compiler_params=_params(("arbitrary",)),
    )(w3, g_window, m3, v3)


def _small_update(parts, weights, moms, vels):
    n = len(weights)

    def body(*refs):
        p_refs, w_refs, m_refs, v_refs = refs[:n + 1], refs[n + 1:2 * n + 1], refs[2 * n + 1:3 * n + 1], refs[3 * n + 1:4 * n + 1]
        outs = refs[4 * n + 1:]
        for i in range(n):
            g = p_refs[i][0]
            for d in range(1, 8):
                g = g + p_refs[i][d]
            delta, m2, v2 = _adam(w_refs[i][...], g, m_refs[i][...], v_refs[i][...])
            outs[4 * i][...] = g
            outs[4 * i + 1][...] = delta
            outs[4 * i + 2][...] = m2
            outs[4 * i + 3][...] = v2
        tot = p_refs[n][0]
        for d in range(1, 8):
            tot = tot + p_refs[n][d]
        outs[4 * n][...] = tot

    out_shape = []
    for w in weights:
        out_shape += [jax.ShapeDtypeStruct(w.shape, F32)] * 4
    out_shape.append(jax.ShapeDtypeStruct(parts[n].shape[1:], F32))
    return pl.pallas_call(body, name="small_update", out_shape=out_shape, compiler_params=_params())(
        *parts, *weights, *moms, *vels)


def _rows8(a):
    flat = a.reshape(-1)
    rows = -(-flat.shape[0] // LANES)
    rows8 = -(-rows // 8) * 8
    flat = jnp.pad(flat, (0, rows8 * LANES - flat.shape[0]))
    return flat.reshape(rows8, LANES)


def kernel(x, c, positions, w_ada, b_ada, g_norm, w_in, w_decay, b_decay, g_gla_head, sinks, w_out, g_final, loss_target, m_w_ada, m_b_ada, m_g_norm, m_w_in, m_w_decay, m_b_decay, m_g_gla_head, m_sinks, m_w_out, m_g_final, v_w_ada, v_b_ada, v_g_norm, v_w_in, v_w_decay, v_b_decay, v_g_gla_head, v_sinks, v_w_out, v_g_final):
    ax, ay, ac = lax.axis_index("x"), lax.axis_index("y"), lax.axis_index("c")
    chip = 2 * ax + ay
    dev = 2 * chip + ac
    s = x.shape[1]
    x2d = x[0]
    target = loss_target[0]
    w_ada2, w_out2, w_dec2 = w_ada[0], w_out[0], w_decay[0]
    w_in_t = w_in[0].T
    ada_cols = w_ada2.shape[1]
    in_cols = w_in_t.shape[0]
    out_rows = w_out2.shape[0]
    half = D_MODEL // 2

    cw = jnp.concatenate([c.reshape(8, LANES), w_dec2.reshape(8, LANES)], axis=0)
    b_shard = lax.dynamic_slice(b_ada, (0, chip * ada_cols), (1, ada_cols))
    half_out = lax.dynamic_slice(w_out2, (ac * (out_rows // 2), 0), (out_rows // 2, D_MODEL)).astype(BF)
    inv_freq = 1.0 / (ROPE_THETA ** (jnp.arange(0, 64, 2, dtype=F32) / 64))
    room = _w_window(in_cols) - in_cols
    win_window = lax.dynamic_slice(jnp.pad(w_in_t, ((room, room), (0, 0))), (room - (in_cols * chip) % W_TILE, ac * half),
                                   (_w_window(in_cols), half)).astype(BF)
    win_edges = jnp.stack([win_window[:W_TILE], win_window[-W_TILE:]])
    first, mod_all, w_t, cos, sin = _prologue(
        cw, w_ada2, b_shard, win_window, win_edges, in_cols, positions.reshape(s // LANES, LANES), jnp.tile(inv_freq, 4).reshape(1, LANES))

    first = first.reshape(8, 2, 8, LANES)
    c_all = first[:, 0].reshape(8, D_MODEL)
    w_dec_full = first[0::2, 1].reshape(4, GLA_RANK, 64).transpose(1, 0, 2).reshape(GLA_RANK, 256)
    mod = mod_all.reshape(4, 2, 8, ada_cols)[:, 0]
    mod = lax.dynamic_slice(mod, (0, dev, 0), (4, 1, ada_cols)).reshape(1, 4 * ada_cols)
    shift, sc1p, gate = mod[:, :D_MODEL], 1.0 + mod[:, D_MODEL:2 * D_MODEL], mod[:, 2 * D_MODEL:]
    wdecp = jnp.pad(w_dec_full, ((0, LANES - GLA_RANK), (0, 0))).astype(BF)

    proj = _inproj_fwd(x2d, shift, sc1p, g_norm, w_t)
    og, o_gla, sprev = _gla_fwd(proj, wdecp, b_decay, g_gla_head)
    osw, o_swa, w_out_all = _swa_fwd(proj, cos, sin, sinks, half_out)
    w_out_all = w_out_all.reshape(D_MODEL, D_MODEL)
    dx2, dog, dos, dw_out, loss_p, dgf, dgate = _outproj(og, osw, w_out_all, x2d, target, gate, g_final.reshape(1, D_MODEL))
    dsq, dsz, dsk, dsv, dsinks, g_w_out = _swa_bwd(proj, dos, o_swa, cos, sin, sinks, dw_out.reshape(4, out_rows, D_MODEL))
    dqk, dv, dgz, dga, dwdp, dbd, dgg = _gla_bwd(proj, dog, o_gla, sprev, wdecp, b_decay, g_gla_head)
    pieces = (dqk, dv, dgz, dsq, dsz, dsk, dsv, dga)
    gx, dw_in_t, dshift, dscale, dgn = _inproj_bwd(x2d, shift, sc1p, g_norm, w_t, dx2, pieces)

    segs = [jnp.concatenate([dshift, dscale, dgate], axis=1), dgn, dgf, dwdp[:GLA_RANK], dbd, dgg, dsinks, loss_p]
    packed = [_rows8(a) for a in segs]
    offs = [0]
    for a in packed:
        offs.append(offs[-1] + a.shape[0])
    (g_window, small, g_w_ada, d_w_ada, nm_w_ada, nv_w_ada, d_w_out, nm_w_out, nv_w_out) = _epilogue(
        dw_in_t, jnp.concatenate(packed, axis=0), c_all, (w_ada2, m_w_ada[0], v_w_ada[0]),
        (w_out2, g_w_out, m_w_out[0], v_w_out[0]), offs[0])

    def seg(i, size):
        return small[:, offs[i]:offs[i + 1]].reshape(8, -1)[:, :size]

    dmod_all = seg(0, 3 * D_MODEL)
    dwd_all = lax.dynamic_slice(seg(3, GLA_RANK * 256).reshape(8, GLA_RANK, 256), (0, 0, chip * 64), (8, GLA_RANK, 64))
    parts = [dmod_all.reshape(8, 1, 3 * D_MODEL), seg(1, D_MODEL).reshape(8, 1, D_MODEL), dwd_all,
             seg(4, 256).reshape(8, 1, 256), seg(5, 512).reshape(8, 1, 512), seg(6, SWA_HEADS).reshape(8, 1, SWA_HEADS),
             seg(2, D_MODEL).reshape(8, 1, D_MODEL), seg(7, LANES).reshape(8, 1, LANES)]
    smalls = _small_update(
        parts,
        [b_ada, g_norm, w_dec2, b_decay, g_gla_head, sinks, g_final.reshape(1, D_MODEL)],
        [m_b_ada, m_g_norm, m_w_decay[0], m_b_decay, m_g_gla_head, m_sinks, m_g_final.reshape(1, D_MODEL)],
        [v_b_ada, v_g_norm, v_w_decay[0], v_b_decay, v_g_gla_head, v_sinks, v_g_final.reshape(1, D_MODEL)])
    (g_b_ada, d_b_ada, nm_b_ada, nv_b_ada, g_gn, d_gn, nm_gn, nv_gn, g_wd, d_wd, nm_wd, nv_wd,
     g_bd, d_bd, nm_bd, nv_bd, g_gg, d_gg, nm_gg, nv_gg, g_sk, d_sk, nm_sk, nv_sk,
     g_gf, d_gf, nm_gf, nv_gf, loss_row) = smalls
    loss = loss_row[0, 0]

    to3 = lambda a: jnp.transpose(a, (2, 0, 1))
    from3 = lambda a: jnp.transpose(a, (1, 2, 0))[0]
    d3, nm3, nv3, g3 = _adamw_t(to3(w_in), g_window, to3(m_w_in), to3(v_w_in), "adamw_w_in")
    g_w_in, d_w_in, nm_w_in, nv_w_in = from3(g3), from3(d3), from3(nm3), from3(nv3)

    flat = lambda a: a.reshape(D_MODEL)
    grads = [g_w_ada[None], g_b_ada, g_gn, g_w_in[None], g_wd[None], g_bd, g_gg, g_sk, g_w_out[None], flat(g_gf)]
    deltas = [d_w_ada[None], d_b_ada, d_gn, d_w_in[None], d_wd[None], d_bd, d_gg, d_sk, d_w_out[None], flat(d_gf)]
    new_m = [nm_w_ada[None], nm_b_ada, nm_gn, nm_w_in[None], nm_wd[None], nm_bd, nm_gg, nm_sk, nm_w_out[None], flat(nm_gf)]
    new_v = [nv_w_ada[None], nv_b_ada, nv_gn, nv_w_in[None], nv_wd[None], nv_bd, nv_gg, nv_sk, nv_w_out[None], flat(nv_gf)]
    return (loss, gx[None], *grads, *deltas, *new_m, *new_v)
```

```python
import jax
import jax.numpy as jnp
from jax import lax
from jax.experimental import pallas as pl
from jax.experimental.pallas import tpu as pltpu

F32 = jnp.float32
BF = jnp.bfloat16

D_MODEL = 1024
GLA_HEADS = 4
GLA_DK = 64
GLA_CHUNK = 64
GLA_RANK = 16
GLA_TAU = 16.0
GLA_SUB = 256
GLA_ROWS_FWD = 1024
GLA_ROWS_BWD = 512
SWA_HEADS = 8
SWA_BLOCK = 128
SWA_QBLOCKS_FWD = 8
SWA_QBLOCKS = 8
RMS_EPS = 1e-6
ROPE_THETA = 10000.0

OFF_QK, OFF_V, OFF_GZ, OFF_SQ, OFF_SZ, OFF_SK, OFF_SV, OFF_GA = 0, 512, 1024, 1536, 2048, 2560, 2688, 2816
D_PAD = 2944
D_IN = 2832
LANES = 128
VMEM_LIMIT = 56 * 1024 * 1024

ADAM_LR, ADAM_B1, ADAM_B2, ADAM_EPS, ADAM_WD, ADAM_STEP = 0.001, 0.9, 0.999, 1e-08, 0.01, 10

NT = (((1,), (1,)), ((), ()))
TN = (((0,), (0,)), ((), ()))
MESH = pl.DeviceIdType.MESH


def _dot(a, b, dims=None):
    if dims is None:
        return jnp.dot(a, b, preferred_element_type=F32)
    return lax.dot_general(a, b, dims, preferred_element_type=F32)


def _sigmoid(x):
    return 1.0 / (1.0 + jnp.exp(-x))


def _params(sem=None):
    return pltpu.CompilerParams(dimension_semantics=sem, vmem_limit_bytes=VMEM_LIMIT)


def _full(shape):
    return pl.BlockSpec(shape, lambda i: (0,) * len(shape))


def _subtiles(rows, size=256):
    size = min(size, rows)
    return [slice(k * size, (k + 1) * size) for k in range(rows // size)]


WEIGHT_CHUNKS = 4


def _gather_sems(chunks=1):
    return [pltpu.SemaphoreType.DMA((7 * chunks,)), pltpu.SemaphoreType.DMA((7 * chunks,)), pltpu.SemaphoreType.DMA]


_GATHER_SEMS = _gather_sems()


class _Gather:
    def __init__(self, x_ref, out_ref, send_sems, recv_sems, local_sem, slab=None, chunks=1):
        self.slab_of = slab
        self.chunks = chunks
        self.width = x_ref.shape[-1] // chunks
        x, y, c = lax.axis_index("x"), lax.axis_index("y"), lax.axis_index("c")
        self.me, self.sibling, self.c = (x, y, c), (x, y, 1 - c), c
        self.xn, self.yn, self.dg = (1 - x, y), (x, 1 - y), (1 - x, 1 - y)
        self.pass_from = (lax.rem(x + 1 - c, 2), lax.rem(y + c, 2))
        self.pass_to = (lax.rem(x + c, 2), lax.rem(y + 1 - c, 2))
        self.x_ref, self.out_ref, self.send_sems, self.recv_sems = x_ref, out_ref, send_sems, recv_sems
        self.mine = pltpu.make_async_copy(x_ref, self._slab(*self.me), local_sem)

    def _slab(self, px, py, pc):
        if self.slab_of is not None:
            return self.slab_of(self.out_ref, px, py, pc)
        return self.out_ref.at[4 * px + 2 * py + pc]

    def _part(self, ref, q):
        if self.chunks == 1:
            return ref
        lanes = slice(q * self.width, (q + 1) * self.width)
        return ref.at[(slice(None),) * (len(ref.shape) - 1) + (lanes,)]

    def _copy(self, k, q, blk, to, src=None):
        i = k * self.chunks + q
        return pltpu.make_async_remote_copy(
            src_ref=self._part(self._slab(*blk) if src is None else src, q), dst_ref=self._part(self._slab(*blk), q),
            send_sem=self.send_sems.at[i], recv_sem=self.recv_sems.at[i], device_id=to, device_id_type=MESH)

    def _sends(self, q):
        c = self.c
        return [self._copy(0, q, self.me, self.sibling, src=self.x_ref),
                self._copy(1, q, self.me, (*self.xn, c), src=self.x_ref),
                self._copy(2, q, self.me, (*self.yn, c), src=self.x_ref),
                self._copy(3, q, (*self.pass_from, c), (*self.pass_to, c)),
                self._copy(4, q, (*self.xn, c), self.sibling),
                self._copy(5, q, (*self.yn, c), self.sibling),
                self._copy(6, q, (*self.dg, c), self.sibling)]

    def start(self):
        self.mine.start()
        for q in range(self.chunks):
            sends = self._sends(q)
            for k in (1, 2, 0):
                sends[k].start()

    def pass_on(self, only=None):
        for q in range(self.chunks) if only is None else (only,):
            sends = self._sends(q)
            self._copy(1, q, (*self.xn, self.c), self.me).wait_recv()
            self._copy(2, q, (*self.yn, self.c), self.me).wait_recv()
            for k in (3, 4, 5):
                sends[k].start()

    def relay_diagonal(self, only=None):
        for q in range(self.chunks) if only is None else (only,):
            self._copy(3, q, (*self.dg, self.c), self.me).wait_recv()
            self._sends(q)[6].start()

    def relay(self):
        self.pass_on()
        self.relay_diagonal()

    def finish(self):
        c = self.c
        for q in range(self.chunks):
            self._copy(0, q, self.sibling, self.me).wait_recv()
            for k, chip in ((4, self.xn), (5, self.yn), (6, self.dg)):
                self._copy(k, q, (*chip, 1 - c), self.me).wait_recv()
            for cp in self._sends(q):
                cp.wait_send()
        self.mine.wait()


def _prologue(cw, w_ada, b_shard, win_window, win_edges, n_in, pos_rows, inv_freq):
    s = pos_rows.shape[0] * LANES
    rt = min(512, s)
    inner = win_window.shape[0] - 2 * W_TILE
    starts = [(n_in * j) // W_TILE * W_TILE for j in range(4)]
    edge_rows = starts + [starts[3] + inner + W_TILE]
    assert all(starts[j] + inner + W_TILE == edge_rows[j + 1] for j in range(4))

    def body(cw_ref, wada_hbm, b_ref, hin_ref, hedge_ref, pos_ref, f_ref,
             first_ref, mod_ref, win_ref, cos_hbm, sin_hbm,
             mod_blk, cos_ref, sin_ref, wada_ref, edge_ref, tile_ref, table_sems, local_sems, tile_sems, *sems):
        fetch_w = pltpu.make_async_copy(wada_hbm, wada_ref, local_sems.at[0])
        fetch_w.start()
        g_c = _Gather(cw_ref, first_ref, *sems[0:3])
        half_lanes = hin_ref.shape[1]

        def lanes_of(pc):
            return pl.ds(pl.multiple_of(pc * half_lanes, half_lanes), half_lanes)

        def inner_rows(px, py):
            return pl.ds(pl.multiple_of((n_in * (2 * px + py)) // W_TILE * W_TILE + W_TILE, W_TILE), inner)

        g_in = _Gather(hin_ref.at[pl.ds(W_TILE, inner), :], win_ref, *sems[3:6], chunks=WEIGHT_CHUNKS,
                       slab=lambda ref, px, py, pc: ref.at[inner_rows(px, py), lanes_of(pc)])
        g_mod = _Gather(mod_blk, mod_ref, *sems[6:9])
        g_edge = _Gather(hedge_ref, edge_ref, *sems[9:12],
                         slab=lambda ref, px, py, pc: ref.at[2 * px + py, :, :, lanes_of(pc)])
        g_c.start()
        g_edge.start()
        g_in.start()
        g_c.relay()
        g_edge.relay()
        g_c.finish()
        c_rows = [jnp.concatenate([first_ref[d, r:r + 1, :] for r in range(8)], axis=1) for d in range(8)]
        c_all = jnp.concatenate(c_rows, axis=0)
        sc = (c_all * _sigmoid(c_all)).astype(BF)
        fetch_w.wait()
        mod_blk[...] = _dot(sc, wada_ref[...].astype(BF)) + b_ref[...]
        g_mod.start()

        g_edge.finish()
        row = lax.broadcasted_iota(jnp.int32, tile_ref.shape[1:], 0)
        tiles = []
        for k, at in enumerate(edge_rows):
            last = edge_ref[max(k - 1, 0), 1].astype(F32)
            first = edge_ref[min(k, 3), 0].astype(F32)
            cut = W_TILE if k == 4 else (n_in * k) % W_TILE
            tile_ref[k] = jnp.where(row < cut, last, first).astype(tile_ref.dtype)
            tiles.append(pltpu.make_async_copy(tile_ref.at[k], win_ref.at[at:at + W_TILE, :], tile_sems.at[k]))
            tiles[-1].start()

        def rope_rows(i, carry):
            rows = pl.ds(pl.multiple_of(i * rt, rt), rt)
            cols = [jnp.transpose(jnp.broadcast_to(pos_ref[pl.ds(i * (rt // LANES) + b, 1), :].astype(F32), (LANES, LANES)))
                    for b in range(rt // LANES)]
            ang = jnp.concatenate(cols, axis=0) * f_ref[...]
            lane = lax.broadcasted_iota(jnp.int32, ang.shape, 1)
            cos_ref[rows, :] = jnp.cos(ang)
            sn = jnp.sin(ang)
            sin_ref[rows, :] = jnp.where((lane % 64) < 32, -sn, sn)
            pltpu.make_async_copy(cos_ref.at[rows, :], cos_hbm.at[rows, :], table_sems.at[0]).start()
            pltpu.make_async_copy(sin_ref.at[rows, :], sin_hbm.at[rows, :], table_sems.at[1]).start()
            return carry

        waits = ([lambda q=q: g_in.pass_on(q) for q in range(WEIGHT_CHUNKS)]
                 + [lambda q=q: g_in.relay_diagonal(q) for q in range(WEIGHT_CHUNKS)] + [g_mod.relay])
        steps = s // rt
        lead = steps // 4
        per_wait = max((steps - lead) // len(waits), 1)
        lax.fori_loop(0, lead, rope_rows, 0)
        done = lead
        for wait in waits:
            wait()
            nxt = min(done + per_wait, steps)
            lax.fori_loop(done, nxt, rope_rows, 0)
            done = nxt
        lax.fori_loop(done, steps, rope_rows, 0)
        g_in.finish()
        g_mod.finish()
        for cp in tiles:
            cp.wait()
        pltpu.make_async_copy(cos_ref, cos_hbm, table_sems.at[0]).wait()
        pltpu.make_async_copy(sin_ref, sin_hbm, table_sems.at[1]).wait()

    vm = pl.BlockSpec(memory_space=pltpu.VMEM)
    hbm = pl.BlockSpec(memory_space=pl.ANY)
    half_lanes = win_window.shape[1]
    return pl.pallas_call(
        body, name="prologue",
        out_shape=[jax.ShapeDtypeStruct((8,) + cw.shape, F32), jax.ShapeDtypeStruct((8, 8, w_ada.shape[1]), F32),
                   jax.ShapeDtypeStruct((4 * n_in, 2 * half_lanes), win_window.dtype),
                   jax.ShapeDtypeStruct((s, LANES), F32), jax.ShapeDtypeStruct((s, LANES), F32)],
        in_specs=[vm, hbm, vm, hbm, vm, vm, vm], out_specs=[vm, vm, hbm, hbm, hbm],
        scratch_shapes=[pltpu.VMEM((8, w_ada.shape[1]), F32), pltpu.VMEM((s, LANES), F32), pltpu.VMEM((s, LANES), F32),
                        pltpu.VMEM(w_ada.shape, F32),
                        pltpu.VMEM((4, 2, W_TILE, 2 * half_lanes), win_window.dtype),
                        pltpu.VMEM((5, W_TILE, 2 * half_lanes), win_window.dtype),
                        pltpu.SemaphoreType.DMA((2,)), pltpu.SemaphoreType.DMA((1,)), pltpu.SemaphoreType.DMA((5,))]
        + _GATHER_SEMS + _gather_sems(WEIGHT_CHUNKS) + _GATHER_SEMS + _GATHER_SEMS,
        compiler_params=pltpu.CompilerParams(vmem_limit_bytes=VMEM_LIMIT),
    )(cw, w_ada, b_shard, win_window, win_edges, pos_rows, inv_freq)


def _reduce_scratch(rr, cc):
    c2 = cc // 2
    return [pltpu.VMEM((4, rr, c2), F32), pltpu.VMEM((4, rr, c2), F32), pltpu.VMEM((3, rr, c2), BF),
            pltpu.VMEM((2, rr, c2), BF), pltpu.VMEM((rr, c2), BF), pltpu.VMEM((rr, c2), F32),
            pltpu.SemaphoreType.DMA((8 + 3 * WEIGHT_CHUNKS,)), pltpu.SemaphoreType.DMA((8 + 3 * WEIGHT_CHUNKS,)),
            pltpu.SemaphoreType.DMA((5,))]


class _Reduce:
    def __init__(self, p_hbm, out_ref, acc_ref, own_ref, send_ref, land_ref, relay_ref, res_ref,
                 send_sems, recv_sems, local_sems, rows=None):
        x, y, c = lax.axis_index("x"), lax.axis_index("y"), lax.axis_index("c")
        part = (lambda j, ln: p_hbm.at[j, :, ln]) if rows is None else (lambda j, ln: p_hbm.at[rows(j), ln])
        c2 = out_ref.shape[1] // 2
        sibling = (x, y, 1 - c)
        first = (lax.rem(x + 1 - c, 2), lax.rem(y + c, 2))
        second = (lax.rem(x + c, 2), lax.rem(y + 1 - c, 2))
        shards = [2 * first[0] + first[1], 2 * second[0] + second[1], 2 * (1 - x) + (1 - y), 2 * x + y]
        sibling_slot = (1, 0, 2, 3)
        mine = pl.ds(pl.multiple_of(c * c2, c2), c2)
        other = pl.ds(pl.multiple_of((1 - c) * c2, c2), c2)
        self.acc_ref, self.own_ref, self.send_ref, self.land_ref = acc_ref, own_ref, send_ref, land_ref
        self.relay_ref, self.res_ref = relay_ref, res_ref
        self.own = [pltpu.make_async_copy(part(j, mine), own_ref.at[k], local_sems.at[k])
                    for k, j in enumerate(shards)]
        self.swap_out = [pltpu.make_async_remote_copy(
            src_ref=part(j, other), dst_ref=acc_ref.at[sibling_slot[k]], send_sem=send_sems.at[k],
            recv_sem=recv_sems.at[sibling_slot[k]], device_id=sibling, device_id_type=MESH) for k, j in enumerate(shards)]
        self.swap_in = [pltpu.make_async_remote_copy(
            src_ref=part(j, other), dst_ref=acc_ref.at[k], send_sem=send_sems.at[k], recv_sem=recv_sems.at[k],
            device_id=sibling, device_id_type=MESH) for k, j in enumerate(shards)]

        self.lanes = [slice(q * (c2 // WEIGHT_CHUNKS), (q + 1) * (c2 // WEIGHT_CHUNKS)) for q in range(WEIGHT_CHUNKS)]

        def message(m, src, dst, to):
            return [pltpu.make_async_remote_copy(
                src_ref=src.at[:, ln], dst_ref=dst.at[:, ln], send_sem=send_sems.at[8 + m * WEIGHT_CHUNKS + q],
                recv_sem=recv_sems.at[8 + m * WEIGHT_CHUNKS + q], device_id=(*to, c), device_id_type=MESH)
                for q, ln in enumerate(self.lanes)]

        self.direct = message(0, send_ref.at[0], land_ref.at[0], first)
        self.passed = message(1, send_ref.at[1], relay_ref, first)
        self.joint = message(2, send_ref.at[2], land_ref.at[1], second)
        self.put = pltpu.make_async_copy(res_ref, out_ref.at[:, mine], local_sems.at[4])
        self.share = pltpu.make_async_remote_copy(
            src_ref=res_ref, dst_ref=out_ref.at[:, mine], send_sem=send_sems.at[7],
            recv_sem=recv_sems.at[7], device_id=sibling, device_id_type=MESH)

    def start(self):
        for k in (2, 0, 1, 3):
            self.own[k].start()
            self.swap_out[k].start()

    def _combine(self, k):
        self.own[k].wait()
        self.swap_out[k].wait_send()
        self.swap_in[k].wait_recv()
        self.acc_ref[k] = self.acc_ref[k] + self.own_ref[k]

    def combine_and_send(self):
        dt = self.send_ref.dtype
        self._combine(2)
        self.send_ref[1] = self.acc_ref[2].astype(dt)
        for cp in self.passed:
            cp.start()
        self._combine(0)
        self.send_ref[0] = self.acc_ref[0].astype(dt)
        for cp in self.direct:
            cp.start()
        self._combine(1)
        self._combine(3)

    def send_joint(self):
        dt = self.send_ref.dtype
        for q, ln in enumerate(self.lanes):
            self.passed[q].wait_recv()
            self.send_ref[2, :, ln] = (self.acc_ref[1, :, ln] + self.relay_ref[:, ln].astype(F32)).astype(dt)
            self.joint[q].start()

    def total_and_share(self):
        for cp in self.direct + self.joint:
            cp.wait_recv()
        self.res_ref[...] = self.acc_ref[3] + self.land_ref[0].astype(F32) + self.land_ref[1].astype(F32)
        for cp in self.direct + self.passed + self.joint:
            cp.wait_send()
        self.put.start()
        self.share.start()

    def finish(self):
        self.put.wait()
        self.share.wait()


def _shard_window(n):
    return max(-(-(n * (j + 1)) // 8) * 8 - (n * j) // 8 * 8 for j in range(4))


class _LocalUpdate:
    def __init__(self, ins, in_vm, out_vm, outs, in_sems, out_sems):
        self.loads = [pltpu.make_async_copy(a, b, in_sems.at[k]) for k, (a, b) in enumerate(zip(ins, in_vm))]
        self.stores = [pltpu.make_async_copy(a, b, out_sems.at[k]) for k, (a, b) in enumerate(zip(out_vm, outs))]

    def start(self):
        for cp in self.loads:
            cp.start()

    def loaded(self):
        for cp in self.loads:
            cp.wait()

    def store(self):
        for cp in self.stores:
            cp.start()

    def finish(self):
        for cp in self.stores:
            cp.wait()


def _epilogue(dw_in_t, small, c_all, ada, out, dmod_row):
    cc = dw_in_t.shape[1]
    n = dw_in_t.shape[0] // 4
    r_in = _shard_window(n)
    n_red = len(_reduce_scratch(r_in, cc))
    ra, ca = ada[0].shape
    dm_rows = ca // LANES
    tr = min(512, ra)

    def body(pin_hbm, small_ref, c_ref, *rest):
        ada_hbm, out_hbm = rest[0:3], rest[3:7]
        gin_ref, small_all_ref = rest[7:9]
        ada_res, out_res = rest[9:13], rest[13:16]
        scratch = rest[16:]
        red_in = _Reduce(pin_hbm, gin_ref, *scratch[0:n_red],
                         rows=lambda j: pl.ds(pl.multiple_of((n * j) // 8 * 8, 8), r_in))
        gat = _Gather(small_ref, small_all_ref, *scratch[n_red:n_red + 3])
        local = scratch[n_red + 3:]
        ada_in, ada_out, out_in, out_out = local[0:3], local[3:7], local[7:11], local[11:14]
        upd_ada = _LocalUpdate(ada_hbm, ada_in, ada_out, ada_res, local[14], local[15])
        upd_out = _LocalUpdate(out_hbm, out_in, out_out, out_res, local[16], local[17])
        red_in.start()
        gat.start()
        upd_out.start()
        upd_ada.start()
        gat.relay()
        red_in.combine_and_send()
        gat.finish()
        red_in.send_joint()

        upd_out.loaded()
        out_out[0][...], out_out[1][...], out_out[2][...] = _adam(*[r[...] for r in out_in])
        upd_out.store()
        chip = 2 * lax.axis_index("x") + lax.axis_index("y")
        dm = jnp.concatenate(
            [jnp.concatenate([small_all_ref[d, pl.ds(dmod_row + dm_rows * chip + r, 1), :] for r in range(dm_rows)], axis=1)
             for d in range(8)], axis=0)
        cv = c_ref[...]
        sc = jnp.concatenate([cv * _sigmoid(cv), jnp.zeros_like(cv)], axis=0).astype(BF)
        dmb = jnp.concatenate([dm, jnp.zeros_like(dm)], axis=0).astype(BF)
        upd_ada.loaded()
        for r0 in range(0, ra, tr):
            rows = slice(r0, r0 + tr)
            g = _dot(sc[:, rows], dmb, TN)
            ada_out[0][rows, :] = g
            ada_out[1][rows, :], ada_out[2][rows, :], ada_out[3][rows, :] = _adam(
                ada_in[0][rows, :], g, ada_in[1][rows, :], ada_in[2][rows, :])
        upd_ada.store()

        red_in.total_and_share()
        red_in.finish()
        upd_out.finish()
        upd_ada.finish()

    vm = pl.BlockSpec(memory_space=pltpu.VMEM)
    anyspec = pl.BlockSpec(memory_space=pl.ANY)
    ada_buf, out_buf = pltpu.VMEM((ra, ca), F32), pltpu.VMEM(out[0].shape, F32)
    return pl.pallas_call(
        body, name="epilogue",
        out_shape=[jax.ShapeDtypeStruct((r_in, cc), F32), jax.ShapeDtypeStruct((8,) + small.shape, F32)]
        + [jax.ShapeDtypeStruct((ra, ca), F32)] * 4 + [jax.ShapeDtypeStruct(out[0].shape, F32)] * 3,
        in_specs=[anyspec, vm, vm] + [anyspec] * 7, out_specs=[anyspec, vm] + [anyspec] * 7,
        scratch_shapes=_reduce_scratch(r_in, cc) + _GATHER_SEMS + [ada_buf] * 7 + [out_buf] * 7
        + [pltpu.SemaphoreType.DMA((3,)), pltpu.SemaphoreType.DMA((4,)), pltpu.SemaphoreType.DMA((4,)), pltpu.SemaphoreType.DMA((3,))],
        compiler_params=pltpu.CompilerParams(vmem_limit_bytes=VMEM_LIMIT),
    )(dw_in_t, small, c_all, *ada, *out)


def _rope(t, cosb, sinb, first_half):
    partner = jnp.where(first_half, pltpu.roll(t, 96, 1), pltpu.roll(t, 32, 1))
    return t * cosb + partner * sinb


def _rope_t(g, cosb, sinb, first_half):
    gs = g * sinb
    partner = jnp.where(first_half, pltpu.roll(gs, 96, 1), pltpu.roll(gs, 32, 1))
    return g * cosb + partner


def _modnorm(x, g, sc1p, shift):
    r = lax.rsqrt(jnp.mean(x * x, axis=-1, keepdims=True) + RMS_EPS)
    xn = x * r
    return xn, r, (xn * g) * sc1p + shift


W_TILE = 16


def _w_window(n):
    return max(-(-(n * (j + 1)) // W_TILE) * W_TILE - (n * j) // W_TILE * W_TILE for j in range(4))


def _load_w_padded(w_hbm, w_vm, sems):
    copies = [pltpu.make_async_copy(w_hbm.at[ref:ref + n], w_vm.at[pad:pad + n], sems.at[k])
              for k, (pad, ref, n) in enumerate(_UNPAD_ROWS)]
    for cp in copies:
        cp.start()
    w_vm[OFF_GA + GLA_RANK:, :] = jnp.zeros((D_PAD - OFF_GA - GLA_RANK, D_MODEL), w_vm.dtype)
    return copies


def _inproj_fwd(x2d, shift, sc1p, g_norm, w_t):
    s = x2d.shape[0]
    tm = min(1024, s)

    def body(x_ref, sh_ref, sc_ref, g_ref, w_hbm, o_ref, w_vm, sems):
        @pl.when(pl.program_id(0) == 0)
        def _():
            for cp in _load_w_padded(w_hbm, w_vm, sems):
                cp.wait()

        subs = _subtiles(tm)
        hs = [_modnorm(x_ref[sl, :], g_ref[...], sc_ref[...], sh_ref[...])[2].astype(BF) for sl in subs]
        for sl, h in zip(subs, hs):
            o_ref[sl, :] = _dot(h, w_vm[...], NT)

    vec = _full((1, D_MODEL))
    return pl.pallas_call(
        body, name="inproj_fwd", grid=(s // tm,),
        in_specs=[pl.BlockSpec((tm, D_MODEL), lambda i: (i, 0)), vec, vec, vec, pl.BlockSpec(memory_space=pl.ANY)],
        out_specs=pl.BlockSpec((tm, D_PAD), lambda i: (i, 0)),
        out_shape=jax.ShapeDtypeStruct((s, D_PAD), F32),
        scratch_shapes=[pltpu.VMEM((D_PAD, D_MODEL), BF), pltpu.SemaphoreType.DMA((len(_UNPAD_ROWS),))],
        compiler_params=_params(("arbitrary",)),
    )(x2d, shift, sc1p, g_norm, w_t)


def _split3(a):
    hi = a.astype(BF)
    r1 = a - hi.astype(F32)
    mid = r1.astype(BF)
    lo = (r1 - mid.astype(F32)).astype(BF)
    return hi, mid, lo


def _tri_matmul(tri, a):
    hi, mid, lo = _split3(a)
    return _dot(tri, hi) + _dot(tri, mid) + _dot(tri, lo)


def _chunks(tb):
    return [slice(c * GLA_CHUNK, (c + 1) * GLA_CHUNK) for c in range(tb // GLA_CHUNK)]


def _per_chunk_rows(rows, width):
    return jnp.concatenate([jnp.broadcast_to(r, (GLA_CHUNK, width)) for r in rows], axis=0)


def _gla_triangle(tb):
    row = lax.broadcasted_iota(jnp.int32, (tb, tb), 0)
    col = lax.broadcasted_iota(jnp.int32, (tb, tb), 1)
    return (((row // GLA_CHUNK) == (col // GLA_CHUNK)) & (col <= row)).astype(F32)


def _lane_mean(x, ones_b):
    hi = x.astype(BF)
    lo = (x - hi.astype(F32)).astype(BF)
    return (_dot(hi, ones_b) + _dot(lo, ones_b)) * (1.0 / LANES)


def _head(t, h, lo_h):
    blk = t[:, LANES * (h // 2):LANES * (h // 2 + 1)]
    return jnp.where(lo_h, blk, 0.0) if h % 2 == 0 else jnp.where(lo_h, 0.0, blk)


def _gla_block_common(qk, ga, wd, bd, tril_b):
    tb = qk.shape[0]
    q, k = qk[:, :256], qk[:, 256:]
    z = _dot(ga.astype(BF), wd) + bd
    la = (jnp.minimum(z, 0.0) - jnp.log(1.0 + jnp.exp(-jnp.abs(z)))) * (1.0 / GLA_TAU)
    b = _tri_matmul(tril_b, la)
    bls = [b[rs.stop - 1:rs.stop, :] for rs in _chunks(tb)]
    eq = jnp.exp(b)
    ek = jnp.exp(-b)
    f = jnp.exp(_per_chunk_rows(bls, 256) - b)
    return z, eq, ek, f, q * (eq * GLA_DK ** -0.5), k * ek, k * f, bls


def _gla_units(s, rows):
    sub = min(GLA_SUB, s)
    tb = min(rows, s)
    subs = [slice(i * sub, (i + 1) * sub) for i in range(tb // sub)]
    units = [(i, h) for i in range(len(subs)) for h in range(GLA_HEADS)]
    return tb, sub, subs, units


def _gla_fwd(proj, wdecp, bdec, ggla):
    s = proj.shape[0]
    tb, sub, subs, units = _gla_units(s, GLA_ROWS_FWD)
    nch = sub // GLA_CHUNK

    def body(qk_ref, v_ref, gz_ref, ga_ref, wd_ref, bd_ref, gg_ref, tri_ref, og_ref, opre_ref, sprev_ref, st_ref):
        @pl.when(pl.program_id(0) == 0)
        def _():
            st_ref[...] = jnp.zeros_like(st_ref)

        lo_h = lax.broadcasted_iota(jnp.int32, (sub, LANES), 1) < GLA_DK
        tril = tri_ref[...] > 0.5
        tril_b = tri_ref[...].astype(BF)
        ones_b = jnp.ones((LANES, LANES), BF)
        gg, wd, bd = gg_ref[...], wd_ref[...], bd_ref[...]
        chunks = _chunks(sub)
        lanes = [slice(h * LANES, (h + 1) * LANES) for h in range(GLA_HEADS)]
        com = [_gla_block_common(qk_ref[sl, :], ga_ref[sl, :], wd, bd, tril_b) for sl in subs]
        decs = [[jnp.exp(bl) for bl in cm[7]] for cm in com]
        a = {(i, h): _head(com[i][4], h, lo_h).astype(BF) for i, h in units}
        bm = {(i, h): _head(com[i][5], h, lo_h).astype(BF) for i, h in units}
        ktl = {(i, h): _head(com[i][6], h, lo_h).astype(BF) for i, h in units}
        vh = {(i, h): v_ref[subs[i], lanes[h]].astype(BF) for i, h in units}
        sc = {u: _dot(a[u], bm[u], NT) for u in units}
        upd = {u: [_dot(vh[u][rs], ktl[u][rs], TN) for rs in chunks] for u in units}
        p = {u: jnp.where(tril, sc[u], 0.0).astype(BF) for u in units}
        o = {u: _dot(p[u], vh[u]) for u in units}
        states = {}
        for h in range(GLA_HEADS):
            st = st_ref[h]
            for i in range(len(subs)):
                entering = []
                for c in range(nch):
                    entering.append(st)
                    sprev_ref[i * nch + c, h] = st
                    st = st * decs[i][c][:, LANES * (h // 2):LANES * (h // 2 + 1)] + upd[(i, h)][c]
                states[(i, h)] = entering
            st_ref[h] = st
        inter = {u: [_dot(a[u][rs], states[u][c].astype(BF), NT) for c, rs in enumerate(chunks)] for u in units}
        o = {u: o[u] + jnp.concatenate(inter[u], axis=0) for u in units}
        ms = {u: _lane_mean(o[u] * o[u], ones_b) for u in units}
        for i, h in units:
            gzh = gz_ref[subs[i], lanes[h]]
            opre_ref[subs[i], lanes[h]] = o[(i, h)]
            og_ref[subs[i], lanes[h]] = (((o[(i, h)] * lax.rsqrt(ms[(i, h)] + RMS_EPS)) * gg[:, lanes[h]])
                                         * (gzh * _sigmoid(gzh))).astype(og_ref.dtype)

    def col(width, off):
        return pl.BlockSpec((tb, width), lambda i: (i, off // width))

    return pl.pallas_call(
        body, name="gla_fwd", grid=(s // tb,),
        in_specs=[col(512, OFF_QK), col(512, OFF_V), col(512, OFF_GZ), col(LANES, OFF_GA),
                  _full((LANES, 256)), _full((1, 256)), _full((1, 512)), _full((sub, sub))],
        out_specs=[pl.BlockSpec((tb, 512), lambda i: (i, 0)), pl.BlockSpec((tb, 512), lambda i: (i, 0)),
                   pl.BlockSpec((tb // GLA_CHUNK, GLA_HEADS, LANES, LANES), lambda i: (i, 0, 0, 0))],
        out_shape=[jax.ShapeDtypeStruct((s, 512), BF), jax.ShapeDtypeStruct((s, 512), F32),
                   jax.ShapeDtypeStruct((s // GLA_CHUNK, GLA_HEADS, LANES, LANES), F32)],
        scratch_shapes=[pltpu.VMEM((GLA_HEADS, LANES, LANES), F32)],
        compiler_params=_params(("arbitrary",)),
    )(proj, proj, proj, proj, wdecp, bdec, ggla, _gla_triangle(sub))


def _gla_bwd(proj, dog, opre, sprev, wdecp, bdec, ggla):
    s = proj.shape[0]
    tb, sub, subs, units = _gla_units(s, GLA_ROWS_BWD)
    nsub = len(subs)
    nch = sub // GLA_CHUNK
    nb = s // tb

    def body(qk_ref, v_ref, gz_ref, ga_ref, dog_ref, opre_ref, sprev_ref, wd_ref, bd_ref, gg_ref, tri_ref, triu_ref,
             dqk_ref, dv_ref, dgz_ref, dga_ref, dwd_ref, dbd_ref, dgg_ref, dst_ref):
        @pl.when(pl.program_id(0) == 0)
        def _():
            dst_ref[...] = jnp.zeros_like(dst_ref)
            dwd_ref[...] = jnp.zeros_like(dwd_ref)
            dbd_ref[...] = jnp.zeros_like(dbd_ref)
            dgg_ref[...] = jnp.zeros_like(dgg_ref)

        lo_h = lax.broadcasted_iota(jnp.int32, (sub, LANES), 1) < GLA_DK
        tril = tri_ref[...] > 0.5
        tril_b = tri_ref[...].astype(BF)
        triu_b = triu_ref[...].astype(BF)
        ones_b = jnp.ones((LANES, LANES), BF)
        last_row = (lax.broadcasted_iota(jnp.int32, (sub, LANES), 0) % GLA_CHUNK) == GLA_CHUNK - 1
        wd, gg, bd = wd_ref[...], gg_ref[...], bd_ref[...]
        chunks = _chunks(sub)
        lanes = [slice(h * LANES, (h + 1) * LANES) for h in range(GLA_HEADS)]
        blks = [slice(LANES * (h // 2), LANES * (h // 2 + 1)) for h in range(GLA_HEADS)]
        ga = [ga_ref[sl, :] for sl in subs]
        com = [_gla_block_common(qk_ref[sl, :], ga[i], wd, bd, tril_b) for i, sl in enumerate(subs)]
        decs = [[jnp.exp(bl) for bl in cm[7]] for cm in com]
        a = {(i, h): _head(com[i][4], h, lo_h).astype(BF) for i, h in units}
        bm = {(i, h): _head(com[i][5], h, lo_h).astype(BF) for i, h in units}
        ktl = {(i, h): _head(com[i][6], h, lo_h).astype(BF) for i, h in units}
        vh = {(i, h): v_ref[subs[i], lanes[h]].astype(BF) for i, h in units}
        sc = {u: _dot(a[u], bm[u], NT) for u in units}

        o = {(i, h): opre_ref[subs[i], lanes[h]] for i, h in units}
        ms = {u: _lane_mean(o[u] * o[u], ones_b) for u in units}
        gz = {(i, h): gz_ref[subs[i], lanes[h]] for i, h in units}
        dog = {(i, h): dog_ref[subs[i], lanes[h]] for i, h in units}
        sg = {u: _sigmoid(gz[u]) for u in units}
        r = {u: lax.rsqrt(ms[u] + RMS_EPS) for u in units}
        ohat = {u: o[u] * r[u] for u in units}
        sil = {u: gz[u] * sg[u] for u in units}
        for i, h in units:
            u = (i, h)
            dgz_ref[subs[i], lanes[h]] = (dog[u] * (ohat[u] * gg[:, lanes[h]])
                                          * (sg[u] * (1.0 + gz[u] * (1.0 - sg[u])))).astype(dgz_ref.dtype)
            dgg_ref[:, lanes[h]] += jnp.sum(dog[u] * sil[u] * ohat[u], axis=0, keepdims=True)
        dn = {(i, h): dog[(i, h)] * sil[(i, h)] * gg[:, lanes[h]] for i, h in units}
        mdn = {u: _lane_mean(dn[u] * ohat[u], ones_b) for u in units}
        do = {u: (r[u] * (dn[u] - ohat[u] * mdn[u])).astype(BF) for u in units}

        p = {u: jnp.where(tril, sc[u], 0.0).astype(BF) for u in units}
        dpr = {u: _dot(do[u], vh[u], NT) for u in units}
        incr = {u: [_dot(do[u][rs], a[u][rs], TN) for rs in chunks] for u in units}
        dv = {u: _dot(p[u], do[u], TN) for u in units}
        dp = {u: jnp.where(tril, dpr[u], 0.0).astype(BF) for u in units}
        dqd = {u: _dot(dp[u], bm[u]) for u in units}
        dkd = {u: _dot(dp[u], a[u], TN) for u in units}
        st = {(i, h): [sprev_ref[i * nch + c, h] for c in range(nch)] for i, h in units}
        leaving = {}
        for h in range(GLA_HEADS):
            d = dst_ref[h]
            for i in reversed(range(nsub)):
                out = [None] * nch
                for c in reversed(range(nch)):
                    out[c] = d
                    d = d * decs[i][c][:, blks[h]] + incr[(i, h)][c]
                leaving[(i, h)] = out
            dst_ref[h] = d
        lv_b = {u: [leaving[u][c].astype(BF) for c in range(nch)] for u in units}
        dv_s = {u: [_dot(ktl[u][rs], lv_b[u][c], NT) for c, rs in enumerate(chunks)] for u in units}
        dqd_s = {u: [_dot(do[u][rs], st[u][c].astype(BF)) for c, rs in enumerate(chunks)] for u in units}
        dkt_s = {u: [_dot(vh[u][rs], lv_b[u][c]) for c, rs in enumerate(chunks)] for u in units}
        ddec = {u: [jnp.sum(leaving[u][c] * st[u][c], axis=0, keepdims=True) for c in range(nch)] for u in units}
        for i, h in units:
            dv_ref[subs[i], lanes[h]] = (dv[(i, h)] + jnp.concatenate(dv_s[(i, h)], axis=0)).astype(dv_ref.dtype)
        dqd = {u: dqd[u] + jnp.concatenate(dqd_s[u], axis=0) for u in units}
        dkt = {u: jnp.concatenate(dkt_s[u], axis=0) for u in units}

        db = []
        for i, sl in enumerate(subs):
            _, eq, ek, f, qd, kd, kt, _ = com[i]
            parts = []
            for pair in range(GLA_HEADS // 2):
                blk, u0, u1 = blks[2 * pair], (i, 2 * pair), (i, 2 * pair + 1)
                dqd_b, dkd_b, dkt_b = dqd[u0] + dqd[u1], dkd[u0] + dkd[u1], dkt[u0] + dkt[u1]
                dqk_ref[sl, blk] = (dqd_b * (eq[:, blk] * GLA_DK ** -0.5)).astype(dqk_ref.dtype)
                dqk_ref[sl, 256 + LANES * pair:256 + LANES * (pair + 1)] = (dkd_b * ek[:, blk] + dkt_b * f[:, blk]).astype(dqk_ref.dtype)
                dkt_kt = dkt_b * kt[:, blk]
                dbp = dqd_b * qd[:, blk] - dkd_b * kd[:, blk] - dkt_kt
                dbl = [jnp.sum(dkt_kt[rs], axis=0, keepdims=True) + (ddec[u0][c] + ddec[u1][c]) * decs[i][c][:, blk]
                       for c, rs in enumerate(chunks)]
                parts.append(jnp.where(last_row, dbp + _per_chunk_rows(dbl, LANES), dbp))
            db.append(jnp.concatenate(parts, axis=1))
        dla = [_tri_matmul(triu_b, db[i]) for i in range(nsub)]
        dz32 = [dla[i] * (1.0 / GLA_TAU) * _sigmoid(-com[i][0]) for i in range(nsub)]
        dz = [t.astype(BF) for t in dz32]
        for i, sl in enumerate(subs):
            dga_ref[sl, :] = _dot(dz[i], wd, NT).astype(dga_ref.dtype)
            dwd_ref[...] += _dot(ga[i].astype(BF), dz[i], TN)
            dbd_ref[...] += jnp.sum(dz32[i], axis=0, keepdims=True)

    def col(width, off):
        return pl.BlockSpec((tb, width), lambda i: (nb - 1 - i, off // width))

    def rev(width):
        return pl.BlockSpec((tb, width), lambda i: (nb - 1 - i, 0))

    return pl.pallas_call(
        body, name="gla_bwd", grid=(nb,),
        in_specs=[col(512, OFF_QK), col(512, OFF_V), col(512, OFF_GZ), col(LANES, OFF_GA), rev(512), rev(512),
                  pl.BlockSpec((tb // GLA_CHUNK, GLA_HEADS, LANES, LANES), lambda i: (nb - 1 - i, 0, 0, 0)),
                  _full((LANES, 256)), _full((1, 256)), _full((1, 512)), _full((sub, sub)), _full((sub, sub))],
        out_specs=[rev(512), rev(512), rev(512), rev(LANES), _full((LANES, 256)), _full((1, 256)), _full((1, 512))],
        out_shape=[jax.ShapeDtypeStruct((s, 512), BF), jax.ShapeDtypeStruct((s, 512), BF),
                   jax.ShapeDtypeStruct((s, 512), BF), jax.ShapeDtypeStruct((s, LANES), BF),
                   jax.ShapeDtypeStruct((LANES, 256), F32), jax.ShapeDtypeStruct((1, 256), F32),
                   jax.ShapeDtypeStruct((1, 512), F32)],
        scratch_shapes=[pltpu.VMEM((GLA_HEADS, LANES, LANES), F32)],
        compiler_params=_params(("arbitrary",)),
    )(proj, proj, proj, proj, dog, opre, sprev, wdecp, bdec, ggla, _gla_triangle(sub), _gla_triangle(sub).T)


_SWA_COL_HEADS = (0, 2, 1, 3, 4, 6, 5, 7)
_SWA_COLS = SWA_HEADS * SWA_BLOCK


def _swa_masks():
    lo2 = lax.broadcasted_iota(jnp.int32, (2 * SWA_BLOCK, LANES), 1) < 64
    lane1 = lax.broadcasted_iota(jnp.int32, (SWA_BLOCK, LANES), 1)
    first_half = (lane1 % 64) < 32
    key = lax.broadcasted_iota(jnp.int32, (SWA_BLOCK, _SWA_COLS), 0)
    query = lax.broadcasted_iota(jnp.int32, (SWA_BLOCK, _SWA_COLS), 1) % SWA_BLOCK
    return lo2, lane1 < 64, first_half, key > query


def _merge_band(t, prev_mask, prev_bias=None):
    prev = t[:SWA_BLOCK] if prev_bias is None else t[:SWA_BLOCK] + prev_bias
    return jnp.where(prev_mask, prev, t[SWA_BLOCK:])


def _split_band(t, prev_mask_b):
    prev = t * prev_mask_b
    return jnp.concatenate([prev, t - prev], axis=0)


def _kv_variants(t, lo2):
    tr = pltpu.roll(t, 64, 1)
    lo_v = [jnp.where(lo2, t, 0.0).astype(BF), jnp.where(lo2, tr, 0.0).astype(BF)]
    hi_v = [jnp.where(lo2, 0.0, tr).astype(BF), jnp.where(lo2, 0.0, t).astype(BF)]
    return lo_v, hi_v


def _kv_variants_t(t):
    tt = t.T
    sw = jnp.concatenate([tt[64:], tt[:64]], axis=0)
    top = lax.broadcasted_iota(jnp.int32, tt.shape, 0) < 64
    lo_v = [jnp.where(top, tt, 0.0).astype(BF), jnp.where(top, sw, 0.0).astype(BF)]
    hi_v = [jnp.where(top, 0.0, sw).astype(BF), jnp.where(top, 0.0, tt).astype(BF)]
    return lo_v, hi_v


def _swa_scores(qg, k_lo, k_hi):
    return jnp.concatenate([_dot(k_lo[0], qg[0], NT), _dot(k_hi[0], qg[0], NT),
                            _dot(k_lo[1], qg[1], NT), _dot(k_hi[1], qg[1], NT)], axis=1)


def _sink_row(sinks_ref):
    return jnp.concatenate([jnp.full((1, SWA_BLOCK), sinks_ref[0, hd], F32) for hd in _SWA_COL_HEADS], axis=1)


def _swa_softmax(st, prev_mask, prev_bias, sink):
    st = _merge_band(st, prev_mask, prev_bias)
    m = jnp.maximum(jnp.max(st, axis=0, keepdims=True), sink)
    ex = jnp.exp(st - m)
    es = jnp.exp(sink - m)
    inv = 1.0 / (jnp.sum(ex, axis=0, keepdims=True) + es)
    return ex, es, inv


def _no_prev_bias(block_index):
    return jnp.where(block_index > 0, 0.0, -1e30).astype(F32)


def _swa_queries(sq_ref, rows, cosb, sinb, first_half):
    qs = [_rope(sq_ref[rows, p * LANES:(p + 1) * LANES], cosb, sinb, first_half) * 0.125 for p in range(4)]
    return [jnp.concatenate(qs[0:2], axis=0), jnp.concatenate(qs[2:4], axis=0)]


def _phase_steps(nsteps, phases):
    return [min(nsteps - 1, (k * nsteps) // phases) for k in range(phases - 1)] + [nsteps - 1]


def _swa_fwd(proj, cos, sin, sinks, half_out):
    s = proj.shape[0]
    nq = min(SWA_QBLOCKS_FWD, s // SWA_BLOCK)
    tq = nq * SWA_BLOCK
    steps = _phase_steps(s // tq, 4)

    def body(sq_ref, sz_ref, sk_ref, sv_ref, cos_ref, sin_ref, sinks_ref, hout_hbm, os_ref, opre_ref, wout_hbm,
             kprev, vprev, *gather_sems):
        n = pl.program_id(0)

        @pl.when(n == 0)
        def _():
            kprev[...] = jnp.zeros_like(kprev)
            vprev[...] = jnp.zeros_like(vprev)

        gather = _Gather(hout_hbm, wout_hbm, *gather_sems, chunks=WEIGHT_CHUNKS)
        for step, phase in zip(steps, (gather.start, gather.pass_on, gather.relay_diagonal, gather.finish)):
            pl.when(n == step)(phase)

        lo2, _, first_half, prev_mask = _swa_masks()
        prev_mask_b = jnp.where(prev_mask, 1.0, 0.0).astype(BF)
        sink = _sink_row(sinks_ref)
        blocks = range(nq)
        rows = [slice(j * SWA_BLOCK, (j + 1) * SWA_BLOCK) for j in blocks]
        cosb = [cos_ref[rows[j], :] for j in blocks]
        sinb = [sin_ref[rows[j], :] for j in blocks]
        kc = [_rope(sk_ref[rows[j], :], cosb[j], sinb[j], first_half) for j in blocks]
        vc = [sv_ref[rows[j], :] for j in blocks]
        kcat = [jnp.concatenate([kprev[...] if j == 0 else kc[j - 1], kc[j]], axis=0) for j in blocks]
        vcat = [jnp.concatenate([vprev[...] if j == 0 else vc[j - 1], vc[j]], axis=0) for j in blocks]
        kprev[...] = kc[-1]
        vprev[...] = vc[-1]
        kvar = [_kv_variants(kcat[j], lo2) for j in blocks]
        vtvar = [_kv_variants_t(vcat[j]) for j in blocks]
        qg = [[q.astype(BF) for q in _swa_queries(sq_ref, rows[j], cosb[j], sinb[j], first_half)] for j in blocks]
        st = [_swa_scores(qg[j], *kvar[j]) for j in blocks]
        soft = [_swa_softmax(st[j], prev_mask, _no_prev_bias(n) if j == 0 else None, sink) for j in blocks]
        pt = [_split_band(soft[j][0].astype(BF), prev_mask_b) for j in blocks]
        og = {}
        for j in blocks:
            inv = soft[j][2]
            for g in range(2):
                c0, c1, c2 = 512 * g, 512 * g + 256, 512 * g + 512
                ot = (_dot(vtvar[j][0][g], pt[j][:, c0:c1]) * inv[:, c0:c1]
                      + _dot(vtvar[j][1][g], pt[j][:, c1:c2]) * inv[:, c1:c2])
                og[(j, g)] = ot.T
        for j in blocks:
            for g in range(2):
                for i in range(2):
                    ls = slice((2 * g + i) * LANES, (2 * g + i + 1) * LANES)
                    o = og[(j, g)][i * SWA_BLOCK:(i + 1) * SWA_BLOCK]
                    sz = sz_ref[rows[j], ls]
                    opre_ref[rows[j], ls] = o
                    os_ref[rows[j], ls] = (o * (sz * _sigmoid(sz))).astype(os_ref.dtype)

    def col(width, off):
        return pl.BlockSpec((tq, width), lambda i: (i, off // width))

    row = pl.BlockSpec((tq, LANES), lambda i: (i, 0))
    return pl.pallas_call(
        body, name="swa_fwd", grid=(s // tq,),
        in_specs=[col(512, OFF_SQ), col(512, OFF_SZ), col(LANES, OFF_SK), col(LANES, OFF_SV), row, row,
                  pl.BlockSpec(memory_space=pltpu.SMEM), pl.BlockSpec(memory_space=pl.ANY)],
        out_specs=[pl.BlockSpec((tq, 512), lambda i: (i, 0))] * 2 + [pl.BlockSpec(memory_space=pl.ANY)],
        out_shape=[jax.ShapeDtypeStruct((s, 512), BF), jax.ShapeDtypeStruct((s, 512), F32),
                   jax.ShapeDtypeStruct((8,) + half_out.shape, half_out.dtype)],
        scratch_shapes=[pltpu.VMEM((SWA_BLOCK, LANES), F32)] * 2 + _gather_sems(WEIGHT_CHUNKS),
        compiler_params=_params(("arbitrary",)),
    )(proj, proj, proj, proj, cos, sin, sinks, half_out)


def _swa_bwd(proj, dos, opre, cos, sin, sinks, dw_out_parts):
    s = proj.shape[0]
    nq = min(SWA_QBLOCKS, s // SWA_BLOCK)
    tq = nq * SWA_BLOCK
    steps = _phase_steps(s // tq, 5)
    _, r_out, c_out = dw_out_parts.shape

    def body(sq_ref, sz_ref, sk_ref, sv_ref, dos_ref, opre_ref, cos_ref, sin_ref, sinks_ref, pout_hbm,
             dsq_ref, dsz_ref, dsk_ref, dsv_ref, dsink_ref, gout_hbm, kprev, vprev, cprev, sprev, *reduce_scratch):
        n = pl.program_id(0)

        @pl.when(n == 0)
        def _():
            kprev[...] = jnp.zeros_like(kprev)
            vprev[...] = jnp.zeros_like(vprev)
            cprev[...] = jnp.zeros_like(cprev)
            sprev[...] = jnp.zeros_like(sprev)
            for hd in range(SWA_HEADS):
                dsink_ref[0, hd] = 0.0

        reduce = _Reduce(pout_hbm, gout_hbm, *reduce_scratch)
        phases = (reduce.start, reduce.combine_and_send, reduce.send_joint, reduce.total_and_share, reduce.finish)
        for step, phase in zip(steps, phases):
            pl.when(n == step)(phase)

        lo2, lo1, first_half, prev_mask = _swa_masks()
        prev_mask_b = jnp.where(prev_mask, 1.0, 0.0).astype(BF)
        lo1s = jnp.concatenate([lo1, lo1], axis=0)
        sink = _sink_row(sinks_ref)

        def home(m0, m1):
            t0 = m0 + pltpu.roll(m0, 64, 1)
            t1 = m1 + pltpu.roll(m1, 64, 1)
            return jnp.where(lo2, t0, t1)

        kp, vp, cp_, sp_ = kprev[...], vprev[...], cprev[...], sprev[...]
        for j in range(nq):
            rows = slice(j * SWA_BLOCK, (j + 1) * SWA_BLOCK)
            blk = n * nq + j
            cosb, sinb = cos_ref[rows, :], sin_ref[rows, :]
            kc = _rope(sk_ref[rows, :], cosb, sinb, first_half)
            vc = sv_ref[rows, :]
            kcat = jnp.concatenate([kp, kc], axis=0)
            k_lo, k_hi = _kv_variants(kcat, lo2)
            kt_lo, kt_hi = _kv_variants_t(kcat)
            v_lo, v_hi = _kv_variants(jnp.concatenate([vp, vc], axis=0), lo2)
            qg32 = _swa_queries(sq_ref, rows, cosb, sinb, first_half)
            qg = [q.astype(BF) for q in qg32]
            ex, es, inv = _swa_softmax(_swa_scores(qg, k_lo, k_hi), prev_mask, _no_prev_bias(n) if j == 0 else None, sink)
            pr, ps = ex * inv, es * inv

            dog32 = []
            for g in range(2):
                parts = []
                for i in range(2):
                    ls = slice((2 * g + i) * LANES, (2 * g + i + 1) * LANES)
                    sz = sz_ref[rows, ls]
                    sg = _sigmoid(sz)
                    dos_p = dos_ref[rows, ls]
                    dsz_ref[rows, ls] = (dos_p * opre_ref[rows, ls] * (sg * (1.0 + sz * (1.0 - sg)))).astype(dsz_ref.dtype)
                    parts.append(dos_p * (sz * sg))
                dog32.append(jnp.concatenate(parts, axis=0))
            dog = [t.astype(BF) for t in dog32]
            dpr = _merge_band(jnp.concatenate([_dot(v_lo[0], dog[0], NT), _dot(v_hi[0], dog[0], NT),
                                               _dot(v_lo[1], dog[1], NT), _dot(v_hi[1], dog[1], NT)], axis=1), prev_mask)
            rd = jnp.sum(pr * dpr, axis=0, keepdims=True)
            ds = _split_band((pr * (dpr - rd)).astype(BF), prev_mask_b)
            prb = _split_band(pr.astype(BF), prev_mask_b)
            sink_term = ps * rd
            for r, hd in enumerate(_SWA_COL_HEADS):
                dsink_ref[0, hd] += -jnp.sum(sink_term[:, r * SWA_BLOCK:(r + 1) * SWA_BLOCK])

            dk_g, dv_g = [], []
            for g in range(2):
                c0, c1, c2 = 512 * g, 512 * g + 256, 512 * g + 512
                dq = (_dot(kt_lo[g], ds[:, c0:c1]) + _dot(kt_hi[g], ds[:, c1:c2])).T
                for i in range(2):
                    ls = slice((2 * g + i) * LANES, (2 * g + i + 1) * LANES)
                    dsq_ref[rows, ls] = _rope_t(dq[i * SWA_BLOCK:(i + 1) * SWA_BLOCK] * 0.125, cosb, sinb,
                                                first_half).astype(dsq_ref.dtype)
                q_split = jnp.concatenate([jnp.where(lo1s, qg32[g], 0.0), jnp.where(lo1s, 0.0, qg32[g])], axis=0).astype(BF)
                do_split = jnp.concatenate([jnp.where(lo1s, dog32[g], 0.0), jnp.where(lo1s, 0.0, dog32[g])], axis=0).astype(BF)
                dk_g.append(_dot(ds[:, c0:c2], q_split))
                dv_g.append(_dot(prb[:, c0:c2], do_split))
            dk = home(dk_g[0], dk_g[1])
            dv = home(dv_g[0], dv_g[1])
            cur = pl.ds(pl.multiple_of(blk * SWA_BLOCK, SWA_BLOCK), SWA_BLOCK)
            dsk_ref[cur, :] = _rope_t(dk[SWA_BLOCK:], cosb, sinb, first_half)
            dsv_ref[cur, :] = dv[SWA_BLOCK:]
            dk_prev = _rope_t(dk[:SWA_BLOCK], cp_, sp_, first_half)
            dv_prev = dv[:SWA_BLOCK]
            if j == 0:
                @pl.when(n > 0)
                def _():
                    prv = pl.ds(pl.multiple_of((blk - 1) * SWA_BLOCK, SWA_BLOCK), SWA_BLOCK)
                    dsk_ref[prv, :] += dk_prev
                    dsv_ref[prv, :] += dv_prev
            else:
                prv = pl.ds(pl.multiple_of((blk - 1) * SWA_BLOCK, SWA_BLOCK), SWA_BLOCK)
                dsk_ref[prv, :] += dk_prev
                dsv_ref[prv, :] += dv_prev
            kp, vp, cp_, sp_ = kc, vc, cosb, sinb
        kprev[...] = kp
        vprev[...] = vp
        cprev[...] = cp_
        sprev[...] = sp_

    def col(width, off):
        return pl.BlockSpec((tq, width), lambda i: (i, off // width))

    row = pl.BlockSpec((tq, LANES), lambda i: (i, 0))
    wide = pl.BlockSpec((tq, 512), lambda i: (i, 0))
    return pl.pallas_call(
        body, name="swa_bwd", grid=(s // tq,),
        in_specs=[col(512, OFF_SQ), col(512, OFF_SZ), col(LANES, OFF_SK), col(LANES, OFF_SV), wide, wide, row, row,
                  pl.BlockSpec(memory_space=pltpu.SMEM), pl.BlockSpec(memory_space=pl.ANY)],
        out_specs=[wide, wide, _full((s, LANES)), _full((s, LANES)), pl.BlockSpec(memory_space=pltpu.SMEM),
                   pl.BlockSpec(memory_space=pl.ANY)],
        out_shape=[jax.ShapeDtypeStruct((s, 512), BF), jax.ShapeDtypeStruct((s, 512), BF),
                   jax.ShapeDtypeStruct((s, LANES), F32), jax.ShapeDtypeStruct((s, LANES), F32),
                   jax.ShapeDtypeStruct((1, SWA_HEADS), F32), jax.ShapeDtypeStruct((r_out, c_out), F32)],
        scratch_shapes=[pltpu.VMEM((SWA_BLOCK, LANES), F32)] * 4 + _reduce_scratch(r_out, c_out),
        compiler_params=_params(("arbitrary",)),
    )(proj, proj, proj, proj, dos, opre, cos, sin, sinks, dw_out_parts)


def _outproj(og, osw, w_out, x2d, target, gate, g_final):
    s = x2d.shape[0]
    tm = min(512, s)

    def body(og_ref, os_ref, w_ref, x_ref, t_ref, gate_ref, gf_ref,
             dx2_ref, dog_ref, dos_ref, dw_ref, loss_ref, dgf_ref, dgate_ref):
        @pl.when(pl.program_id(0) == 0)
        def _():
            dw_ref[...] = jnp.zeros_like(dw_ref)
            loss_ref[...] = jnp.zeros_like(loss_ref)
            dgf_ref[...] = jnp.zeros_like(dgf_ref)
            dgate_ref[...] = jnp.zeros_like(dgate_ref)

        w = w_ref[...]
        gate, gf = gate_ref[...], gf_ref[...]
        subs = _subtiles(tm)
        ogv = [og_ref[sl, :] for sl in subs]
        osv = [os_ref[sl, :] for sl in subs]
        y = [_dot(ogv[k], w[:512]) + _dot(osv[k], w[512:]) for k in range(len(subs))]
        dys = []
        for k, sl in enumerate(subs):
            x2 = x_ref[sl, :] + gate * y[k]
            r = lax.rsqrt(jnp.mean(x2 * x2, axis=-1, keepdims=True) + RMS_EPS)
            xn = x2 * r
            err = xn * gf - t_ref[sl, :]
            loss_ref[...] += 0.5 * jnp.sum(jnp.mean(err * err, axis=-1, keepdims=True), axis=0, keepdims=True)
            dyf = err * (1.0 / D_MODEL)
            dgf_ref[...] += jnp.sum(dyf * xn, axis=0, keepdims=True)
            t = dyf * gf
            dx2 = r * (t - xn * jnp.mean(t * xn, axis=-1, keepdims=True))
            dx2_ref[sl, :] = dx2
            dgate_ref[...] += jnp.sum(dx2 * y[k], axis=0, keepdims=True)
            dys.append((dx2 * gate).astype(BF))
            dmix = _dot(dys[k], w, NT)
            dog_ref[sl, :] = dmix[:, :512]
            dos_ref[sl, :] = dmix[:, 512:]
        dy = jnp.concatenate(dys, axis=0)
        dw_ref[:512, :] += _dot(og_ref[...], dy, TN)
        dw_ref[512:, :] += _dot(os_ref[...], dy, TN)

    half = pl.BlockSpec((tm, 512), lambda i: (i, 0))
    rowb = pl.BlockSpec((tm, D_MODEL), lambda i: (i, 0))
    vec = _full((1, D_MODEL))
    return pl.pallas_call(
        body, name="outproj", grid=(s // tm,),
        in_specs=[half, half, _full((D_MODEL, D_MODEL)), rowb, rowb, vec, vec],
        out_specs=[rowb, half, half, _full((D_MODEL, D_MODEL)), _full((1, 1)), vec, vec],
        out_shape=[jax.ShapeDtypeStruct((s, D_MODEL), F32), jax.ShapeDtypeStruct((s, 512), F32),
                   jax.ShapeDtypeStruct((s, 512), F32), jax.ShapeDtypeStruct((D_MODEL, D_MODEL), F32),
                   jax.ShapeDtypeStruct((1, 1), F32), jax.ShapeDtypeStruct((1, D_MODEL), F32),
                   jax.ShapeDtypeStruct((1, D_MODEL), F32)],
        compiler_params=_params(("arbitrary",)),
    )(og, osw, w_out, x2d, target, gate, g_final)


_PIECES = ((OFF_QK, 512), (OFF_V, 512), (OFF_GZ, 512), (OFF_SQ, 512), (OFF_SZ, 512),
           (OFF_SK, LANES), (OFF_SV, LANES), (OFF_GA, LANES))

_UNPAD_ROWS = ((OFF_QK, 0, 1024),
               (OFF_GA, 1024, GLA_RANK),
               (OFF_GZ, 1040, 1024),
               (OFF_SK, 2064, 256),
               (OFF_SZ, 2320, 512))


def _inproj_bwd(x2d, shift, sc1p, g_norm, w_t, dx2, pieces):
    s = x2d.shape[0]
    tm = min(512, s)
    nsteps = s // tm

    def body(x_ref, sh_ref, sc_ref, g_ref, w_hbm, dx2_ref, *rest):
        piece_refs = rest[:len(_PIECES)]
        gx_ref, dw_hbm, dsh_ref, dsc_ref, dg_ref, w_vm, dw_vm, in_sems, out_sems = rest[len(_PIECES):]
        i = pl.program_id(0)

        @pl.when(i == 0)
        def _():
            loads = _load_w_padded(w_hbm, w_vm, in_sems)
            dw_vm[...] = jnp.zeros_like(dw_vm)
            dsh_ref[...] = jnp.zeros_like(dsh_ref)
            dsc_ref[...] = jnp.zeros_like(dsc_ref)
            dg_ref[...] = jnp.zeros_like(dg_ref)
            for cp in loads:
                cp.wait()

        g, sc1p_v, shift_v = g_ref[...], sc_ref[...], sh_ref[...]
        subs = _subtiles(tm)
        dhs = []
        for sl in subs:
            dh = None
            for (off, width), pr in zip(_PIECES, piece_refs):
                part = _dot(pr[sl, :].astype(BF), w_vm[off:off + width, :])
                dh = part if dh is None else dh + part
            dhs.append(dh)
        norm = [_modnorm(x_ref[sl, :], g, sc1p_v, shift_v) for sl in subs]
        hb = jnp.concatenate([h.astype(BF) for _, _, h in norm], axis=0)
        for (off, width), pr in zip(_PIECES, piece_refs):
            dw_vm[off:off + width, :] += _dot(pr[...].astype(BF), hb, TN)
        for sl, (xn, r, _), dh in zip(subs, norm, dhs):
            dsh_ref[...] += jnp.sum(dh, axis=0, keepdims=True)
            dsc_ref[...] += jnp.sum(dh * (xn * g), axis=0, keepdims=True)
            dg_ref[...] += jnp.sum(dh * xn * sc1p_v, axis=0, keepdims=True)
            dxn = dh * g * sc1p_v
            gx_ref[sl, :] = dx2_ref[sl, :] + r * (dxn - xn * jnp.mean(dxn * xn, axis=-1, keepdims=True))

        @pl.when(i == nsteps - 1)
        def _():
            copies = [pltpu.make_async_copy(dw_vm.at[src:src + n], dw_hbm.at[dst:dst + n], out_sems.at[k])
                      for k, (src, dst, n) in enumerate(_UNPAD_ROWS)]
            for cp in copies:
                cp.start()
            for cp in copies:
                cp.wait()

    rowb = pl.BlockSpec((tm, D_MODEL), lambda i: (i, 0))
    vec = _full((1, D_MODEL))
    anyspec = pl.BlockSpec(memory_space=pl.ANY)
    piece_specs = [pl.BlockSpec((tm, width), lambda i: (i, 0)) for _, width in _PIECES]
    return pl.pallas_call(
        body, name="inproj_bwd", grid=(nsteps,),
        in_specs=[rowb, vec, vec, vec, anyspec, rowb] + piece_specs,
        out_specs=[rowb, anyspec, vec, vec, vec],
        out_shape=[jax.ShapeDtypeStruct((s, D_MODEL), F32), jax.ShapeDtypeStruct((D_IN, D_MODEL), F32),
                   jax.ShapeDtypeStruct((1, D_MODEL), F32), jax.ShapeDtypeStruct((1, D_MODEL), F32),
                   jax.ShapeDtypeStruct((1, D_MODEL), F32)],
        scratch_shapes=[pltpu.VMEM((D_PAD, D_MODEL), BF), pltpu.VMEM((D_PAD, D_MODEL), F32),
                        pltpu.SemaphoreType.DMA((len(_UNPAD_ROWS),)), pltpu.SemaphoreType.DMA((len(_UNPAD_ROWS),))],
        compiler_params=_params(("arbitrary",)),
    )(x2d, shift, sc1p, g_norm, w_t, dx2, *pieces)


def _adam(w, g, m, v):
    m2 = ADAM_B1 * m + (1.0 - ADAM_B1) * g
    v2 = ADAM_B2 * v + (1.0 - ADAM_B2) * (g * g)
    m_hat = m2 / (1.0 - ADAM_B1 ** ADAM_STEP)
    v_hat = v2 / (1.0 - ADAM_B2 ** ADAM_STEP)
    delta = -ADAM_LR * (m_hat / (jnp.sqrt(v_hat) + ADAM_EPS) + ADAM_WD * w)
    return delta, m2, v2


def _adamw_t(w3, g_window, m3, v3, name):
    rr, _, cc = w3.shape
    parts = [slice(q * (cc // 4), (q + 1) * (cc // 4)) for q in range(4)]
    starts = sorted({(rr * j) % 8 for j in range(4)})

    def body(w_hbm, gw_hbm, m_hbm, v_hbm, d_hbm, m2_hbm, v2_hbm, g3_hbm,
             w_vm, m_vm, v_vm, gw_vm, d_vm, m2_vm, v2_vm, g_vm, in_sems, out_sems):
        start = lax.rem(rr * (2 * lax.axis_index("x") + lax.axis_index("y")), 8)
        ins = ((w_hbm, w_vm), (m_hbm, m_vm), (v_hbm, v_vm))
        outs = ((d_vm, d_hbm), (m2_vm, m2_hbm), (v2_vm, v2_hbm), (g_vm, g3_hbm))
        loads = [[pltpu.make_async_copy(src.at[:, 0, p], dst.at[:, p], in_sems.at[4 * q + k]) for k, (src, dst) in enumerate(ins)]
                 + [pltpu.make_async_copy(gw_hbm.at[:, p], gw_vm.at[:, p], in_sems.at[4 * q + 3])]
                 for q, p in enumerate(parts)]
        stores = [[pltpu.make_async_copy(src.at[:, p], dst.at[:, 0, p], out_sems.at[4 * q + k]) for k, (src, dst) in enumerate(outs)]
                  for q, p in enumerate(parts)]
        for group in loads:
            for cp in group:
                cp.start()
        for q, p in enumerate(parts):
            for cp in loads[q]:
                cp.wait()
            g = gw_vm[starts[0]:starts[0] + rr, p]
            for o in starts[1:]:
                g = jnp.where(start == o, gw_vm[o:o + rr, p], g)
            g_vm[:, p] = g
            d_vm[:, p], m2_vm[:, p], v2_vm[:, p] = _adam(w_vm[:, p], g, m_vm[:, p], v_vm[:, p])
            for cp in stores[q]:
                cp.start()
        for group in stores:
            for cp in group:
                cp.wait()

    hbm = pl.BlockSpec(memory_space=pl.ANY)
    return pl.pallas_call(
        body, name=name, grid=(1,), in_specs=[hbm] * 4,
        out_specs=[hbm] * 4, out_shape=[jax.ShapeDtypeStruct((rr, 1, cc), F32)] * 4,
        scratch_shapes=[pltpu.VMEM((rr, cc), F32)] * 3 + [pltpu.VMEM(g_window.shape, F32)] + [pltpu.VMEM((rr, cc), F32)] * 4
        + [pltpu.SemaphoreType.DMA((16,)), pltpu.SemaphoreType.DMA((16,))],
        compiler_params=_params(("arbitrary",)),
    )(w3, g_window, m3, v3)


def _small_update(parts, weights, moms, vels):
    n = len(weights)

    def body(*refs):
        p_refs, w_refs, m_refs, v_refs = refs[:n + 1], refs[n + 1:2 * n + 1], refs[2 * n + 1:3 * n + 1], refs[3 * n + 1:4 * n + 1]
        outs = refs[4 * n + 1:]
        for i in range(n):
            g = p_refs[i][0]
            for d in range(1, 8):
                g = g + p_refs[i][d]
            delta, m2, v2 = _adam(w_refs[i][...], g, m_refs[i][...], v_refs[i][...])
            outs[4 * i][...] = g
            outs[4 * i + 1][...] = delta
            outs[4 * i + 2][...] = m2
            outs[4 * i + 3][...] = v2
        tot = p_refs[n][0]
        for d in range(1, 8):
            tot = tot + p_refs[n][d]
        outs[4 * n][...] = tot

    out_shape = []
    for w in weights:
        out_shape += [jax.ShapeDtypeStruct(w.shape, F32)] * 4
    out_shape.append(jax.ShapeDtypeStruct(parts[n].shape[1:], F32))
    return pl.pallas_call(body, name="small_update", out_shape=out_shape, compiler_params=_params())(
        *parts, *weights, *moms, *vels)


def _rows8(a):
    flat = a.reshape(-1)
    rows = -(-flat.shape[0] // LANES)
    rows8 = -(-rows // 8) * 8
    flat = jnp.pad(flat, (0, rows8 * LANES - flat.shape[0]))
    return flat.reshape(rows8, LANES)


def kernel(x, c, positions, w_ada, b_ada, g_norm, w_in, w_decay, b_decay, g_gla_head, sinks, w_out, g_final, loss_target, m_w_ada, m_b_ada, m_g_norm, m_w_in, m_w_decay, m_b_decay, m_g_gla_head, m_sinks, m_w_out, m_g_final, v_w_ada, v_b_ada, v_g_norm, v_w_in, v_w_decay, v_b_decay, v_g_gla_head, v_sinks, v_w_out, v_g_final):
    ax, ay, ac = lax.axis_index("x"), lax.axis_index("y"), lax.axis_index("c")
    chip = 2 * ax + ay
    dev = 2 * chip + ac
    s = x.shape[1]
    x2d = x[0]
    target = loss_target[0]
    w_ada2, w_out2, w_dec2 = w_ada[0], w_out[0], w_decay[0]
    w_in_t = w_in[0].T
    ada_cols = w_ada2.shape[1]
    in_cols = w_in_t.shape[0]
    out_rows = w_out2.shape[0]
    half = D_MODEL // 2

    cw = jnp.concatenate([c.reshape(8, LANES), w_dec2.reshape(8, LANES)], axis=0)
    b_shard = lax.dynamic_slice(b_ada, (0, chip * ada_cols), (1, ada_cols))
    half_out = lax.dynamic_slice(w_out2, (ac * (out_rows // 2), 0), (out_rows // 2, D_MODEL)).astype(BF)
    inv_freq = 1.0 / (ROPE_THETA ** (jnp.arange(0, 64, 2, dtype=F32) / 64))
    room = _w_window(in_cols) - in_cols
    win_window = lax.dynamic_slice(jnp.pad(w_in_t, ((room, room), (0, 0))), (room - (in_cols * chip) % W_TILE, ac * half),
                                   (_w_window(in_cols), half)).astype(BF)
    win_edges = jnp.stack([win_window[:W_TILE], win_window[-W_TILE:]])
    first, mod_all, w_t, cos, sin = _prologue(
        cw, w_ada2, b_shard, win_window, win_edges, in_cols, positions.reshape(s // LANES, LANES), jnp.tile(inv_freq, 4).reshape(1, LANES))

    first = first.reshape(8, 2, 8, LANES)
    c_all = first[:, 0].reshape(8, D_MODEL)
    w_dec_full = first[0::2, 1].reshape(4, GLA_RANK, 64).transpose(1, 0, 2).reshape(GLA_RANK, 256)
    mod = mod_all.reshape(4, 2, 8, ada_cols)[:, 0]
    mod = lax.dynamic_slice(mod, (0, dev, 0), (4, 1, ada_cols)).reshape(1, 4 * ada_cols)
    shift, sc1p, gate = mod[:, :D_MODEL], 1.0 + mod[:, D_MODEL:2 * D_MODEL], mod[:, 2 * D_MODEL:]
    wdecp = jnp.pad(w_dec_full, ((0, LANES - GLA_RANK), (0, 0))).astype(BF)

    proj = _inproj_fwd(x2d, shift, sc1p, g_norm, w_t)
    og, o_gla, sprev = _gla_fwd(proj, wdecp, b_decay, g_gla_head)
    osw, o_swa, w_out_all = _swa_fwd(proj, cos, sin, sinks, half_out)
    w_out_all = w_out_all.reshape(D_MODEL, D_MODEL)
    dx2, dog, dos, dw_out, loss_p, dgf, dgate = _outproj(og, osw, w_out_all, x2d, target, gate, g_final.reshape(1, D_MODEL))
    dsq, dsz, dsk, dsv, dsinks, g_w_out = _swa_bwd(proj, dos, o_swa, cos, sin, sinks, dw_out.reshape(4, out_rows, D_MODEL))
    dqk, dv, dgz, dga, dwdp, dbd, dgg = _gla_bwd(proj, dog, o_gla, sprev, wdecp, b_decay, g_gla_head)
    pieces = (dqk, dv, dgz, dsq, dsz, dsk, dsv, dga)
    gx, dw_in_t, dshift, dscale, dgn = _inproj_bwd(x2d, shift, sc1p, g_norm, w_t, dx2, pieces)

    segs = [jnp.concatenate([dshift, dscale, dgate], axis=1), dgn, dgf, dwdp[:GLA_RANK], dbd, dgg, dsinks, loss_p]
    packed = [_rows8(a) for a in segs]
    offs = [0]
    for a in packed:
        offs.append(offs[-1] + a.shape[0])
    (g_window, small, g_w_ada, d_w_ada, nm_w_ada, nv_w_ada, d_w_out, nm_w_out, nv_w_out) = _epilogue(
        dw_in_t, jnp.concatenate(packed, axis=0), c_all, (w_ada2, m_w_ada[0], v_w_ada[0]),
        (w_out2, g_w_out, m_w_out[0], v_w_out[0]), offs[0])

    def seg(i, size):
        return small[:, offs[i]:offs[i + 1]].reshape(8, -1)[:, :size]

    dmod_all = seg(0, 3 * D_MODEL)
    dwd_all = lax.dynamic_slice(seg(3, GLA_RANK * 256).reshape(8, GLA_RANK, 256), (0, 0, chip * 64), (8, GLA_RANK, 64))
    parts = [dmod_all.reshape(8, 1, 3 * D_MODEL), seg(1, D_MODEL).reshape(8, 1, D_MODEL), dwd_all,
             seg(4, 256).reshape(8, 1, 256), seg(5, 512).reshape(8, 1, 512), seg(6, SWA_HEADS).reshape(8, 1, SWA_HEADS),
             seg(2, D_MODEL).reshape(8, 1, D_MODEL), seg(7, LANES).reshape(8, 1, LANES)]
    smalls = _small_update(
        parts,
        [b_ada, g_norm, w_dec2, b_decay, g_gla_head, sinks, g_final.reshape(1, D_MODEL)],
        [m_b_ada, m_g_norm, m_w_decay[0], m_b_decay, m_g_gla_head, m_sinks, m_g_final.reshape(1, D_MODEL)],
        [v_b_ada, v_g_norm, v_w_decay[0], v_b_decay, v_g_gla_head, v_sinks, v_g_final.reshape(1, D_MODEL)])
    (g_b_ada, d_b_ada, nm_b_ada, nv_b_ada, g_gn, d_gn, nm_gn, nv_gn, g_wd, d_wd, nm_wd, nv_wd,
     g_bd, d_bd, nm_bd, nv_bd, g_gg, d_gg, nm_gg, nv_gg, g_sk, d_sk, nm_sk, nv_sk,
     g_gf, d_gf, nm_gf, nv_gf, loss_row) = smalls
    loss = loss_row[0, 0]

    to3 = lambda a: jnp.transpose(a, (2, 0, 1))
    from3 = lambda a: jnp.transpose(a, (1, 2, 0))[0]
    d3, nm3, nv3, g3 = _adamw_t(to3(w_in), g_window, to3(m_w_in), to3(v_w_in), "adamw_w_in")
    g_w_in, d_w_in, nm_w_in, nv_w_in = from3(g3), from3(d3), from3(nm3), from3(nv3)

    flat = lambda a: a.reshape(D_MODEL)
    grads = [g_w_ada[None], g_b_ada, g_gn, g_w_in[None], g_wd[None], g_bd, g_gg, g_sk, g_w_out[None], flat(g_gf)]
    deltas = [d_w_ada[None], d_b_ada, d_gn, d_w_in[None], d_wd[None], d_bd, d_gg, d_sk, d_w_out[None], flat(d_gf)]
    new_m = [nm_w_ada[None], nm_b_ada, nm_gn, nm_w_in[None], nm_wd[None], nm_bd, nm_gg, nm_sk, nm_w_out[None], flat(nm_gf)]
    new_v = [nv_w_ada[None], nv_b_ada, nv_gn, nv_w_in[None], nv_wd[None], nv_bd, nv_gg, nv_sk, nv_w_out[None], flat(nv_gf)]
    return (loss, gx[None], *grads, *deltas, *new_m, *new_v)
```

```python
import jax
import jax.numpy as jnp
from jax import lax
from jax.experimental import pallas as pl
from jax.experimental.pallas import tpu as pltpu

F32 = jnp.float32
BF = jnp.bfloat16

D_MODEL = 1024
GLA_HEADS = 4
GLA_DK = 64
GLA_CHUNK = 64
GLA_RANK = 16
GLA_TAU = 16.0
GLA_SUB = 256
GLA_ROWS_FWD = 1024
GLA_ROWS_BWD = 512
SWA_HEADS = 8
SWA_BLOCK = 128
SWA_QBLOCKS_FWD = 8
SWA_QBLOCKS = 8
RMS_EPS = 1e-6
ROPE_THETA = 10000.0

OFF_QK, OFF_V, OFF_GZ, OFF_SQ, OFF_SZ, OFF_SK, OFF_SV, OFF_GA = 0, 512, 1024, 1536, 2048, 2560, 2688, 2816
D_PAD = 2944
D_IN = 2832
LANES = 128
VMEM_LIMIT = 56 * 1024 * 1024

ADAM_LR, ADAM_B1, ADAM_B2, ADAM_EPS, ADAM_WD, ADAM_STEP = 0.001, 0.9, 0.999, 1e-08, 0.01, 10

NT = (((1,), (1,)), ((), ()))
TN = (((0,), (0,)), ((), ()))
MESH = pl.DeviceIdType.MESH


def _dot(a, b, dims=None):
    if dims is None:
        return jnp.dot(a, b, preferred_element_type=F32)
    return lax.dot_general(a, b, dims, preferred_element_type=F32)


def _sigmoid(x):
    return 1.0 / (1.0 + jnp.exp(-x))


def _params(sem=None):
    return pltpu.CompilerParams(dimension_semantics=sem, vmem_limit_bytes=VMEM_LIMIT)


def _full(shape):
    return pl.BlockSpec(shape, lambda i: (0,) * len(shape))


def _subtiles(rows, size=256):
    size = min(size, rows)
    return [slice(k * size, (k + 1) * size) for k in range(rows // size)]


WEIGHT_CHUNKS = 4


def _gather_sems(chunks=1):
    return [pltpu.SemaphoreType.DMA((7 * chunks,)), pltpu.SemaphoreType.DMA((7 * chunks,)), pltpu.SemaphoreType.DMA]


_GATHER_SEMS = _gather_sems()


class _Gather:
    def __init__(self, x_ref, out_ref, send_sems, recv_sems, local_sem, slab=None, chunks=1):
        self.slab_of = slab
        self.chunks = chunks
        self.width = x_ref.shape[-1] // chunks
        x, y, c = lax.axis_index("x"), lax.axis_index("y"), lax.axis_index("c")
        self.me, self.sibling, self.c = (x, y, c), (x, y, 1 - c), c
        self.xn, self.yn, self.dg = (1 - x, y), (x, 1 - y), (1 - x, 1 - y)
        self.pass_from = (lax.rem(x + 1 - c, 2), lax.rem(y + c, 2))
        self.pass_to = (lax.rem(x + c, 2), lax.rem(y + 1 - c, 2))
        self.x_ref, self.out_ref, self.send_sems, self.recv_sems = x_ref, out_ref, send_sems, recv_sems
        self.mine = pltpu.make_async_copy(x_ref, self._slab(*self.me), local_sem)

    def _slab(self, px, py, pc):
        if self.slab_of is not None:
            return self.slab_of(self.out_ref, px, py, pc)
        return self.out_ref.at[4 * px + 2 * py + pc]

    def _part(self, ref, q):
        if self.chunks == 1:
            return ref
        lanes = slice(q * self.width, (q + 1) * self.width)
        return ref.at[(slice(None),) * (len(ref.shape) - 1) + (lanes,)]

    def _copy(self, k, q, blk, to, src=None):
        i = k * self.chunks + q
        return pltpu.make_async_remote_copy(
            src_ref=self._part(self._slab(*blk) if src is None else src, q), dst_ref=self._part(self._slab(*blk), q),
            send_sem=self.send_sems.at[i], recv_sem=self.recv_sems.at[i], device_id=to, device_id_type=MESH)

    def _sends(self, q):
        c = self.c
        return [self._copy(0, q, self.me, self.sibling, src=self.x_ref),
                self._copy(1, q, self.me, (*self.xn, c), src=self.x_ref),
                self._copy(2, q, self.me, (*self.yn, c), src=self.x_ref),
                self._copy(3, q, (*self.pass_from, c), (*self.pass_to, c)),
                self._copy(4, q, (*self.xn, c), self.sibling),
                self._copy(5, q, (*self.yn, c), self.sibling),
                self._copy(6, q, (*self.dg, c), self.sibling)]

    def start(self):
        self.mine.start()
        for q in range(self.chunks):
            sends = self._sends(q)
            for k in (1, 2, 0):
                sends[k].start()

    def pass_on(self, only=None):
        for q in range(self.chunks) if only is None else (only,):
            sends = self._sends(q)
            self._copy(1, q, (*self.xn, self.c), self.me).wait_recv()
            self._copy(2, q, (*self.yn, self.c), self.me).wait_recv()
            for k in (3, 4, 5):
                sends[k].start()

    def relay_diagonal(self, only=None):
        for q in range(self.chunks) if only is None else (only,):
            self._copy(3, q, (*self.dg, self.c), self.me).wait_recv()
            self._sends(q)[6].start()

    def relay(self):
        self.pass_on()
        self.relay_diagonal()

    def finish(self):
        c = self.c
        for q in range(self.chunks):
            self._copy(0, q, self.sibling, self.me).wait_recv()
            for k, chip in ((4, self.xn), (5, self.yn), (6, self.dg)):
                self._copy(k, q, (*chip, 1 - c), self.me).wait_recv()
            for cp in self._sends(q):
                cp.wait_send()
        self.mine.wait()


def _prologue(cw, w_ada, b_shard, win_window, win_edges, n_in, pos_rows, inv_freq):
    s = pos_rows.shape[0] * LANES
    rt = min(512, s)
    inner = win_window.shape[0] - 2 * W_TILE
    starts = [(n_in * j) // W_TILE * W_TILE for j in range(4)]
    edge_rows = starts + [starts[3] + inner + W_TILE]
    assert all(starts[j] + inner + W_TILE == edge_rows[j + 1] for j in range(4))

    def body(cw_ref, wada_hbm, b_ref, hin_ref, hedge_ref, pos_ref, f_ref,
             first_ref, mod_ref, win_ref, cos_hbm, sin_hbm,
             mod_blk, cos_ref, sin_ref, wada_ref, edge_ref, tile_ref, table_sems, local_sems, tile_sems, *sems):
        fetch_w = pltpu.make_async_copy(wada_hbm, wada_ref, local_sems.at[0])
        fetch_w.start()
        g_c = _Gather(cw_ref, first_ref, *sems[0:3])
        half_lanes = hin_ref.shape[1]

        def lanes_of(pc):
            return pl.ds(pl.multiple_of(pc * half_lanes, half_lanes), half_lanes)

        def inner_rows(px, py):
            return pl.ds(pl.multiple_of((n_in * (2 * px + py)) // W_TILE * W_TILE + W_TILE, W_TILE), inner)

        g_in = _Gather(hin_ref.at[pl.ds(W_TILE, inner), :], win_ref, *sems[3:6], chunks=WEIGHT_CHUNKS,
                       slab=lambda ref, px, py, pc: ref.at[inner_rows(px, py), lanes_of(pc)])
        g_mod = _Gather(mod_blk, mod_ref, *sems[6:9])
        g_edge = _Gather(hedge_ref, edge_ref, *sems[9:12],
                         slab=lambda ref, px, py, pc: ref.at[2 * px + py, :, :, lanes_of(pc)])
        g_c.start()
        g_edge.start()
        g_in.start()
        g_c.relay()
        g_edge.relay()
        g_c.finish()
        c_rows = [jnp.concatenate([first_ref[d, r:r + 1, :] for r in range(8)], axis=1) for d in range(8)]
        c_all = jnp.concatenate(c_rows, axis=0)
        sc = (c_all * _sigmoid(c_all)).astype(BF)
        fetch_w.wait()
        mod_blk[...] = _dot(sc, wada_ref[...].astype(BF)) + b_ref[...]
        g_mod.start()

        def rope_rows(i, carry):
            rows = pl.ds(pl.multiple_of(i * rt, rt), rt)
            cols = [jnp.transpose(jnp.broadcast_to(pos_ref[pl.ds(i * (rt // LANES) + b, 1), :].astype(F32), (LANES, LANES)))
                    for b in range(rt // LANES)]
            ang = jnp.concatenate(cols, axis=0) * f_ref[...]
            lane = lax.broadcasted_iota(jnp.int32, ang.shape, 1)
            cos_ref[rows, :] = jnp.cos(ang)
            sn = jnp.sin(ang)
            sin_ref[rows, :] = jnp.where((lane % 64) < 32, -sn, sn)
            pltpu.make_async_copy(cos_ref.at[rows, :], cos_hbm.at[rows, :], table_sems.at[0]).start()
            pltpu.make_async_copy(sin_ref.at[rows, :], sin_hbm.at[rows, :], table_sems.at[1]).start()
            return carry

        tiles = []

        def edge_tiles():
            g_edge.finish()
            row = lax.broadcasted_iota(jnp.int32, tile_ref.shape[1:], 0)
            for k, at in enumerate(edge_rows):
                last = edge_ref[max(k - 1, 0), 1].astype(F32)
                first = edge_ref[min(k, 3), 0].astype(F32)
                cut = W_TILE if k == 4 else (n_in * k) % W_TILE
                tile_ref[k] = jnp.where(row < cut, last, first).astype(tile_ref.dtype)
                tiles.append(pltpu.make_async_copy(tile_ref.at[k], win_ref.at[at:at + W_TILE, :], tile_sems.at[k]))
                tiles[-1].start()

        waits = ([lambda q=q: g_in.pass_on(q) for q in range(WEIGHT_CHUNKS)] + [edge_tiles]
                 + [lambda q=q: g_in.relay_diagonal(q) for q in range(WEIGHT_CHUNKS)] + [g_mod.relay])
        steps = s // rt
        lead = steps // 4
        per_wait = max((steps - lead) // len(waits), 1)
        lax.fori_loop(0, lead, rope_rows, 0)
        done = lead
        for wait in waits:
            wait()
            nxt = min(done + per_wait, steps)
            lax.fori_loop(done, nxt, rope_rows, 0)
            done = nxt
        lax.fori_loop(done, steps, rope_rows, 0)
        g_in.finish()
        g_mod.finish()
        for cp in tiles:
            cp.wait()
        pltpu.make_async_copy(cos_ref, cos_hbm, table_sems.at[0]).wait()
        pltpu.make_async_copy(sin_ref, sin_hbm, table_sems.at[1]).wait()

    vm = pl.BlockSpec(memory_space=pltpu.VMEM)
    hbm = pl.BlockSpec(memory_space=pl.ANY)
    half_lanes = win_window.shape[1]
    return pl.pallas_call(
        body, name="prologue",
        out_shape=[jax.ShapeDtypeStruct((8,) + cw.shape, F32), jax.ShapeDtypeStruct((8, 8, w_ada.shape[1]), F32),
                   jax.ShapeDtypeStruct((4 * n_in, 2 * half_lanes), win_window.dtype),
                   jax.ShapeDtypeStruct((s, LANES), F32), jax.ShapeDtypeStruct((s, LANES), F32)],
        in_specs=[vm, hbm, vm, hbm, vm, vm, vm], out_specs=[vm, vm, hbm, hbm, hbm],
        scratch_shapes=[pltpu.VMEM((8, w_ada.shape[1]), F32), pltpu.VMEM((s, LANES), F32), pltpu.VMEM((s, LANES), F32),
                        pltpu.VMEM(w_ada.shape, F32),
                        pltpu.VMEM((4, 2, W_TILE, 2 * half_lanes), win_window.dtype),
                        pltpu.VMEM((5, W_TILE, 2 * half_lanes), win_window.dtype),
                        pltpu.SemaphoreType.DMA((2,)), pltpu.SemaphoreType.DMA((1,)), pltpu.SemaphoreType.DMA((5,))]
        + _GATHER_SEMS + _gather_sems(WEIGHT_CHUNKS) + _GATHER_SEMS + _GATHER_SEMS,
        compiler_params=pltpu.CompilerParams(vmem_limit_bytes=VMEM_LIMIT),
    )(cw, w_ada, b_shard, win_window, win_edges, pos_rows, inv_freq)


def _reduce_scratch(rr, cc):
    c2 = cc // 2
    return [pltpu.VMEM((4, rr, c2), F32), pltpu.VMEM((4, rr, c2), F32), pltpu.VMEM((3, rr, c2), BF),
            pltpu.VMEM((2, rr, c2), BF), pltpu.VMEM((rr, c2), BF), pltpu.VMEM((rr, c2), F32),
            pltpu.SemaphoreType.DMA((8 + 3 * WEIGHT_CHUNKS,)), pltpu.SemaphoreType.DMA((8 + 3 * WEIGHT_CHUNKS,)),
            pltpu.SemaphoreType.DMA((5,))]


class _Reduce:
    def __init__(self, p_hbm, out_ref, acc_ref, own_ref, send_ref, land_ref, relay_ref, res_ref,
                 send_sems, recv_sems, local_sems, rows=None):
        x, y, c = lax.axis_index("x"), lax.axis_index("y"), lax.axis_index("c")
        part = (lambda j, ln: p_hbm.at[j, :, ln]) if rows is None else (lambda j, ln: p_hbm.at[rows(j), ln])
        c2 = out_ref.shape[1] // 2
        sibling = (x, y, 1 - c)
        first = (lax.rem(x + 1 - c, 2), lax.rem(y + c, 2))
        second = (lax.rem(x + c, 2), lax.rem(y + 1 - c, 2))
        shards = [2 * first[0] + first[1], 2 * second[0] + second[1], 2 * (1 - x) + (1 - y), 2 * x + y]
        sibling_slot = (1, 0, 2, 3)
        mine = pl.ds(pl.multiple_of(c * c2, c2), c2)
        other = pl.ds(pl.multiple_of((1 - c) * c2, c2), c2)
        self.acc_ref, self.own_ref, self.send_ref, self.land_ref = acc_ref, own_ref, send_ref, land_ref
        self.relay_ref, self.res_ref = relay_ref, res_ref
        self.own = [pltpu.make_async_copy(part(j, mine), own_ref.at[k], local_sems.at[k])
                    for k, j in enumerate(shards)]
        self.swap_out = [pltpu.make_async_remote_copy(
            src_ref=part(j, other), dst_ref=acc_ref.at[sibling_slot[k]], send_sem=send_sems.at[k],
            recv_sem=recv_sems.at[sibling_slot[k]], device_id=sibling, device_id_type=MESH) for k, j in enumerate(shards)]
        self.swap_in = [pltpu.make_async_remote_copy(
            src_ref=part(j, other), dst_ref=acc_ref.at[k], send_sem=send_sems.at[k], recv_sem=recv_sems.at[k],
            device_id=sibling, device_id_type=MESH) for k, j in enumerate(shards)]

        self.lanes = [slice(q * (c2 // WEIGHT_CHUNKS), (q + 1) * (c2 // WEIGHT_CHUNKS)) for q in range(WEIGHT_CHUNKS)]

        def message(m, src, dst, to):
            return [pltpu.make_async_remote_copy(
                src_ref=src.at[:, ln], dst_ref=dst.at[:, ln], send_sem=send_sems.at[8 + m * WEIGHT_CHUNKS + q],
                recv_sem=recv_sems.at[8 + m * WEIGHT_CHUNKS + q], device_id=(*to, c), device_id_type=MESH)
                for q, ln in enumerate(self.lanes)]

        self.direct = message(0, send_ref.at[0], land_ref.at[0], first)
        self.passed = message(1, send_ref.at[1], relay_ref, first)
        self.joint = message(2, send_ref.at[2], land_ref.at[1], second)
        self.put = pltpu.make_async_copy(res_ref, out_ref.at[:, mine], local_sems.at[4])
        self.share = pltpu.make_async_remote_copy(
            src_ref=res_ref, dst_ref=out_ref.at[:, mine], send_sem=send_sems.at[7],
            recv_sem=recv_sems.at[7], device_id=sibling, device_id_type=MESH)

    def start(self):
        for k in (2, 0, 1, 3):
            self.own[k].start()
            self.swap_out[k].start()

    def _combine(self, k):
        self.own[k].wait()
        self.swap_out[k].wait_send()
        self.swap_in[k].wait_recv()
        self.acc_ref[k] = self.acc_ref[k] + self.own_ref[k]

    def combine_and_send(self):
        dt = self.send_ref.dtype
        self._combine(2)
        self.send_ref[1] = self.acc_ref[2].astype(dt)
        for cp in self.passed:
            cp.start()
        self._combine(0)
        self.send_ref[0] = self.acc_ref[0].astype(dt)
        for cp in self.direct:
            cp.start()
        self._combine(1)
        self._combine(3)

    def send_joint(self):
        dt = self.send_ref.dtype
        for q, ln in enumerate(self.lanes):
            self.passed[q].wait_recv()
            self.send_ref[2, :, ln] = (self.acc_ref[1, :, ln] + self.relay_ref[:, ln].astype(F32)).astype(dt)
            self.joint[q].start()

    def total_and_share(self):
        for cp in self.direct + self.joint:
            cp.wait_recv()
        self.res_ref[...] = self.acc_ref[3] + self.land_ref[0].astype(F32) + self.land_ref[1].astype(F32)
        for cp in self.direct + self.passed + self.joint:
            cp.wait_send()
        self.put.start()
        self.share.start()

    def finish(self):
        self.put.wait()
        self.share.wait()


def _shard_window(n):
    return max(-(-(n * (j + 1)) // 8) * 8 - (n * j) // 8 * 8 for j in range(4))


class _LocalUpdate:
    def __init__(self, ins, in_vm, out_vm, outs, in_sems, out_sems):
        self.loads = [pltpu.make_async_copy(a, b, in_sems.at[k]) for k, (a, b) in enumerate(zip(ins, in_vm))]
        self.stores = [pltpu.make_async_copy(a, b, out_sems.at[k]) for k, (a, b) in enumerate(zip(out_vm, outs))]

    def start(self):
        for cp in self.loads:
            cp.start()

    def loaded(self):
        for cp in self.loads:
            cp.wait()

    def store(self):
        for cp in self.stores:
            cp.start()

    def finish(self):
        for cp in self.stores:
            cp.wait()


def _epilogue(dw_in_t, small, c_all, ada, out, dmod_row):
    cc = dw_in_t.shape[1]
    n = dw_in_t.shape[0] // 4
    r_in = _shard_window(n)
    n_red = len(_reduce_scratch(r_in, cc))
    ra, ca = ada[0].shape
    dm_rows = ca // LANES
    tr = min(512, ra)

    def body(pin_hbm, small_ref, c_ref, *rest):
        ada_hbm, out_hbm = rest[0:3], rest[3:7]
        gin_ref, small_all_ref = rest[7:9]
        ada_res, out_res = rest[9:13], rest[13:16]
        scratch = rest[16:]
        red_in = _Reduce(pin_hbm, gin_ref, *scratch[0:n_red],
                         rows=lambda j: pl.ds(pl.multiple_of((n * j) // 8 * 8, 8), r_in))
        gat = _Gather(small_ref, small_all_ref, *scratch[n_red:n_red + 3])
        local = scratch[n_red + 3:]
        ada_in, ada_out, out_in, out_out = local[0:3], local[3:7], local[7:11], local[11:14]
        upd_ada = _LocalUpdate(ada_hbm, ada_in, ada_out, ada_res, local[14], local[15])
        upd_out = _LocalUpdate(out_hbm, out_in, out_out, out_res, local[16], local[17])
        red_in.start()
        gat.start()
        upd_out.start()
        upd_ada.start()
        gat.relay()
        red_in.combine_and_send()
        gat.finish()
        red_in.send_joint()

        upd_out.loaded()
        out_out[0][...], out_out[1][...], out_out[2][...] = _adam(*[r[...] for r in out_in])
        upd_out.store()
        chip = 2 * lax.axis_index("x") + lax.axis_index("y")
        dm = jnp.concatenate(
            [jnp.concatenate([small_all_ref[d, pl.ds(dmod_row + dm_rows * chip + r, 1), :] for r in range(dm_rows)], axis=1)
             for d in range(8)], axis=0)
        cv = c_ref[...]
        sc = jnp.concatenate([cv * _sigmoid(cv), jnp.zeros_like(cv)], axis=0).astype(BF)
        dmb = jnp.concatenate([dm, jnp.zeros_like(dm)], axis=0).astype(BF)
        upd_ada.loaded()
        for r0 in range(0, ra, tr):
            rows = slice(r0, r0 + tr)
            g = _dot(sc[:, rows], dmb, TN)
            ada_out[0][rows, :] = g
            ada_out[1][rows, :], ada_out[2][rows, :], ada_out[3][rows, :] = _adam(
                ada_in[0][rows, :], g, ada_in[1][rows, :], ada_in[2][rows, :])
        upd_ada.store()

        red_in.total_and_share()
        red_in.finish()
        upd_out.finish()
        upd_ada.finish()

    vm = pl.BlockSpec(memory_space=pltpu.VMEM)
    anyspec = pl.BlockSpec(memory_space=pl.ANY)
    ada_buf, out_buf = pltpu.VMEM((ra, ca), F32), pltpu.VMEM(out[0].shape, F32)
    return pl.pallas_call(
        body, name="epilogue",
        out_shape=[jax.ShapeDtypeStruct((r_in, cc), F32), jax.ShapeDtypeStruct((8,) + small.shape, F32)]
        + [jax.ShapeDtypeStruct((ra, ca), F32)] * 4 + [jax.ShapeDtypeStruct(out[0].shape, F32)] * 3,
        in_specs=[anyspec, vm, vm] + [anyspec] * 7, out_specs=[anyspec, vm] + [anyspec] * 7,
        scratch_shapes=_reduce_scratch(r_in, cc) + _GATHER_SEMS + [ada_buf] * 7 + [out_buf] * 7
        + [pltpu.SemaphoreType.DMA((3,)), pltpu.SemaphoreType.DMA((4,)), pltpu.SemaphoreType.DMA((4,)), pltpu.SemaphoreType.DMA((3,))],
        compiler_params=pltpu.CompilerParams(vmem_limit_bytes=VMEM_LIMIT),
    )(dw_in_t, small, c_all, *ada, *out)


def _rope(t, cosb, sinb, first_half):
    partner = jnp.where(first_half, pltpu.roll(t, 96, 1), pltpu.roll(t, 32, 1))
    return t * cosb + partner * sinb


def _rope_t(g, cosb, sinb, first_half):
    gs = g * sinb
    partner = jnp.where(first_half, pltpu.roll(gs, 96, 1), pltpu.roll(gs, 32, 1))
    return g * cosb + partner


def _modnorm(x, g, sc1p, shift):
    r = lax.rsqrt(jnp.mean(x * x, axis=-1, keepdims=True) + RMS_EPS)
    xn = x * r
    return xn, r, (xn * g) * sc1p + shift


W_TILE = 16


def _w_window(n):
    return max(-(-(n * (j + 1)) // W_TILE) * W_TILE - (n * j) // W_TILE * W_TILE for j in range(4))


def _load_w_padded(w_hbm, w_vm, sems):
    copies = [pltpu.make_async_copy(w_hbm.at[ref:ref + n], w_vm.at[pad:pad + n], sems.at[k])
              for k, (pad, ref, n) in enumerate(_UNPAD_ROWS)]
    for cp in copies:
        cp.start()
    w_vm[OFF_GA + GLA_RANK:, :] = jnp.zeros((D_PAD - OFF_GA - GLA_RANK, D_MODEL), w_vm.dtype)
    return copies


def _inproj_fwd(x2d, shift, sc1p, g_norm, w_t):
    s = x2d.shape[0]
    tm = min(1024, s)

    def body(x_ref, sh_ref, sc_ref, g_ref, w_hbm, o_ref, w_vm, sems):
        @pl.when(pl.program_id(0) == 0)
        def _():
            for cp in _load_w_padded(w_hbm, w_vm, sems):
                cp.wait()

        subs = _subtiles(tm)
        hs = [_modnorm(x_ref[sl, :], g_ref[...], sc_ref[...], sh_ref[...])[2].astype(BF) for sl in subs]
        for sl, h in zip(subs, hs):
            o_ref[sl, :] = _dot(h, w_vm[...], NT)

    vec = _full((1, D_MODEL))
    return pl.pallas_call(
        body, name="inproj_fwd", grid=(s // tm,),
        in_specs=[pl.BlockSpec((tm, D_MODEL), lambda i: (i, 0)), vec, vec, vec, pl.BlockSpec(memory_space=pl.ANY)],
        out_specs=pl.BlockSpec((tm, D_PAD), lambda i: (i, 0)),
        out_shape=jax.ShapeDtypeStruct((s, D_PAD), F32),
        scratch_shapes=[pltpu.VMEM((D_PAD, D_MODEL), BF), pltpu.SemaphoreType.DMA((len(_UNPAD_ROWS),))],
        compiler_params=_params(("arbitrary",)),
    )(x2d, shift, sc1p, g_norm, w_t)


def _split3(a):
    hi = a.astype(BF)
    r1 = a - hi.astype(F32)
    mid = r1.astype(BF)
    lo = (r1 - mid.astype(F32)).astype(BF)
    return hi, mid, lo


def _tri_matmul(tri, a):
    hi, mid, lo = _split3(a)
    return _dot(tri, hi) + _dot(tri, mid) + _dot(tri, lo)


def _chunks(tb):
    return [slice(c * GLA_CHUNK, (c + 1) * GLA_CHUNK) for c in range(tb // GLA_CHUNK)]


def _per_chunk_rows(rows, width):
    return jnp.concatenate([jnp.broadcast_to(r, (GLA_CHUNK, width)) for r in rows], axis=0)


def _gla_triangle(tb):
    row = lax.broadcasted_iota(jnp.int32, (tb, tb), 0)
    col = lax.broadcasted_iota(jnp.int32, (tb, tb), 1)
    return (((row // GLA_CHUNK) == (col // GLA_CHUNK)) & (col <= row)).astype(F32)


def _lane_mean(x, ones_b):
    hi = x.astype(BF)
    lo = (x - hi.astype(F32)).astype(BF)
    return (_dot(hi, ones_b) + _dot(lo, ones_b)) * (1.0 / LANES)


def _head(t, h, lo_h):
    blk = t[:, LANES * (h // 2):LANES * (h // 2 + 1)]
    return jnp.where(lo_h, blk, 0.0) if h % 2 == 0 else jnp.where(lo_h, 0.0, blk)


def _gla_block_common(qk, ga, wd, bd, tril_b):
    tb = qk.shape[0]
    q, k = qk[:, :256], qk[:, 256:]
    z = _dot(ga.astype(BF), wd) + bd
    la = (jnp.minimum(z, 0.0) - jnp.log(1.0 + jnp.exp(-jnp.abs(z)))) * (1.0 / GLA_TAU)
    b = _tri_matmul(tril_b, la)
    bls = [b[rs.stop - 1:rs.stop, :] for rs in _chunks(tb)]
    eq = jnp.exp(b)
    ek = jnp.exp(-b)
    f = jnp.exp(_per_chunk_rows(bls, 256) - b)
    return z, eq, ek, f, q * (eq * GLA_DK ** -0.5), k * ek, k * f, bls


def _gla_units(s, rows):
    sub = min(GLA_SUB, s)
    tb = min(rows, s)
    subs = [slice(i * sub, (i + 1) * sub) for i in range(tb // sub)]
    units = [(i, h) for i in range(len(subs)) for h in range(GLA_HEADS)]
    return tb, sub, subs, units


def _gla_fwd(proj, wdecp, bdec, ggla):
    s = proj.shape[0]
    tb, sub, subs, units = _gla_units(s, GLA_ROWS_FWD)
    nch = sub // GLA_CHUNK

    def body(qk_ref, v_ref, gz_ref, ga_ref, wd_ref, bd_ref, gg_ref, tri_ref, og_ref, opre_ref, sprev_ref, st_ref):
        @pl.when(pl.program_id(0) == 0)
        def _():
            st_ref[...] = jnp.zeros_like(st_ref)

        lo_h = lax.broadcasted_iota(jnp.int32, (sub, LANES), 1) < GLA_DK
        tril = tri_ref[...] > 0.5
        tril_b = tri_ref[...].astype(BF)
        ones_b = jnp.ones((LANES, LANES), BF)
        gg, wd, bd = gg_ref[...], wd_ref[...], bd_ref[...]
        chunks = _chunks(sub)
        lanes = [slice(h * LANES, (h + 1) * LANES) for h in range(GLA_HEADS)]
        com = [_gla_block_common(qk_ref[sl, :], ga_ref[sl, :], wd, bd, tril_b) for sl in subs]
        decs = [[jnp.exp(bl) for bl in cm[7]] for cm in com]
        a = {(i, h): _head(com[i][4], h, lo_h).astype(BF) for i, h in units}
        bm = {(i, h): _head(com[i][5], h, lo_h).astype(BF) for i, h in units}
        ktl = {(i, h): _head(com[i][6], h, lo_h).astype(BF) for i, h in units}
        vh = {(i, h): v_ref[subs[i], lanes[h]].astype(BF) for i, h in units}
        sc = {u: _dot(a[u], bm[u], NT) for u in units}
        upd = {u: [_dot(vh[u][rs], ktl[u][rs], TN) for rs in chunks] for u in units}
        p = {u: jnp.where(tril, sc[u], 0.0).astype(BF) for u in units}
        o = {u: _dot(p[u], vh[u]) for u in units}
        states = {}
        for h in range(GLA_HEADS):
            st = st_ref[h]
            for i in range(len(subs)):
                entering = []
                for c in range(nch):
                    entering.append(st)
                    sprev_ref[i * nch + c, h] = st
                    st = st * decs[i][c][:, LANES * (h // 2):LANES * (h // 2 + 1)] + upd[(i, h)][c]
                states[(i, h)] = entering
            st_ref[h] = st
        inter = {u: [_dot(a[u][rs], states[u][c].astype(BF), NT) for c, rs in enumerate(chunks)] for u in units}
        o = {u: o[u] + jnp.concatenate(inter[u], axis=0) for u in units}
        ms = {u: _lane_mean(o[u] * o[u], ones_b) for u in units}
        for i, h in units:
            gzh = gz_ref[subs[i], lanes[h]]
            opre_ref[subs[i], lanes[h]] = o[(i, h)]
            og_ref[subs[i], lanes[h]] = (((o[(i, h)] * lax.rsqrt(ms[(i, h)] + RMS_EPS)) * gg[:, lanes[h]])
                                         * (gzh * _sigmoid(gzh))).astype(og_ref.dtype)

    def col(width, off):
        return pl.BlockSpec((tb, width), lambda i: (i, off // width))

    return pl.pallas_call(
        body, name="gla_fwd", grid=(s // tb,),
        in_specs=[col(512, OFF_QK), col(512, OFF_V), col(512, OFF_GZ), col(LANES, OFF_GA),
                  _full((LANES, 256)), _full((1, 256)), _full((1, 512)), _full((sub, sub))],
        out_specs=[pl.BlockSpec((tb, 512), lambda i: (i, 0)), pl.BlockSpec((tb, 512), lambda i: (i, 0)),
                   pl.BlockSpec((tb // GLA_CHUNK, GLA_HEADS, LANES, LANES), lambda i: (i, 0, 0, 0))],
        out_shape=[jax.ShapeDtypeStruct((s, 512), BF), jax.ShapeDtypeStruct((s, 512), F32),
                   jax.ShapeDtypeStruct((s // GLA_CHUNK, GLA_HEADS, LANES, LANES), F32)],
        scratch_shapes=[pltpu.VMEM((GLA_HEADS, LANES, LANES), F32)],
        compiler_params=_params(("arbitrary",)),
    )(proj, proj, proj, proj, wdecp, bdec, ggla, _gla_triangle(sub))


def _gla_bwd(proj, dog, opre, sprev, wdecp, bdec, ggla):
    s = proj.shape[0]
    tb, sub, subs, units = _gla_units(s, GLA_ROWS_BWD)
    nsub = len(subs)
    nch = sub // GLA_CHUNK
    nb = s // tb

    def body(qk_ref, v_ref, gz_ref, ga_ref, dog_ref, opre_ref, sprev_ref, wd_ref, bd_ref, gg_ref, tri_ref, triu_ref,
             dqk_ref, dv_ref, dgz_ref, dga_ref, dwd_ref, dbd_ref, dgg_ref, dst_ref):
        @pl.when(pl.program_id(0) == 0)
        def _():
            dst_ref[...] = jnp.zeros_like(dst_ref)
            dwd_ref[...] = jnp.zeros_like(dwd_ref)
            dbd_ref[...] = jnp.zeros_like(dbd_ref)
            dgg_ref[...] = jnp.zeros_like(dgg_ref)

        lo_h = lax.broadcasted_iota(jnp.int32, (sub, LANES), 1) < GLA_DK
        tril = tri_ref[...] > 0.5
        tril_b = tri_ref[...].astype(BF)
        triu_b = triu_ref[...].astype(BF)
        ones_b = jnp.ones((LANES, LANES), BF)
        last_row = (lax.broadcasted_iota(jnp.int32, (sub, LANES), 0) % GLA_CHUNK) == GLA_CHUNK - 1
        wd, gg, bd = wd_ref[...], gg_ref[...], bd_ref[...]
        chunks = _chunks(sub)
        lanes = [slice(h * LANES, (h + 1) * LANES) for h in range(GLA_HEADS)]
        blks = [slice(LANES * (h // 2), LANES * (h // 2 + 1)) for h in range(GLA_HEADS)]
        ga = [ga_ref[sl, :] for sl in subs]
        com = [_gla_block_common(qk_ref[sl, :], ga[i], wd, bd, tril_b) for i, sl in enumerate(subs)]
        decs = [[jnp.exp(bl) for bl in cm[7]] for cm in com]
        a = {(i, h): _head(com[i][4], h, lo_h).astype(BF) for i, h in units}
        bm = {(i, h): _head(com[i][5], h, lo_h).astype(BF) for i, h in units}
        ktl = {(i, h): _head(com[i][6], h, lo_h).astype(BF) for i, h in units}
        vh = {(i, h): v_ref[subs[i], lanes[h]].astype(BF) for i, h in units}
        sc = {u: _dot(a[u], bm[u], NT) for u in units}

        o = {(i, h): opre_ref[subs[i], lanes[h]] for i, h in units}
        ms = {u: _lane_mean(o[u] * o[u], ones_b) for u in units}
        gz = {(i, h): gz_ref[subs[i], lanes[h]] for i, h in units}
        dog = {(i, h): dog_ref[subs[i], lanes[h]] for i, h in units}
        sg = {u: _sigmoid(gz[u]) for u in units}
        r = {u: lax.rsqrt(ms[u] + RMS_EPS) for u in units}
        ohat = {u: o[u] * r[u] for u in units}
        sil = {u: gz[u] * sg[u] for u in units}
        for i, h in units:
            u = (i, h)
            dgz_ref[subs[i], lanes[h]] = (dog[u] * (ohat[u] * gg[:, lanes[h]])
                                          * (sg[u] * (1.0 + gz[u] * (1.0 - sg[u])))).astype(dgz_ref.dtype)
            dgg_ref[:, lanes[h]] += jnp.sum(dog[u] * sil[u] * ohat[u], axis=0, keepdims=True)
        dn = {(i, h): dog[(i, h)] * sil[(i, h)] * gg[:, lanes[h]] for i, h in units}
        mdn = {u: _lane_mean(dn[u] * ohat[u], ones_b) for u in units}
        do = {u: (r[u] * (dn[u] - ohat[u] * mdn[u])).astype(BF) for u in units}

        p = {u: jnp.where(tril, sc[u], 0.0).astype(BF) for u in units}
        dpr = {u: _dot(do[u], vh[u], NT) for u in units}
        incr = {u: [_dot(do[u][rs], a[u][rs], TN) for rs in chunks] for u in units}
        dv = {u: _dot(p[u], do[u], TN) for u in units}
        dp = {u: jnp.where(tril, dpr[u], 0.0).astype(BF) for u in units}
        dqd = {u: _dot(dp[u], bm[u]) for u in units}
        dkd = {u: _dot(dp[u], a[u], TN) for u in units}
        st = {(i, h): [sprev_ref[i * nch + c, h] for c in range(nch)] for i, h in units}
        leaving = {}
        for h in range(GLA_HEADS):
            d = dst_ref[h]
            for i in reversed(range(nsub)):
                out = [None] * nch
                for c in reversed(range(nch)):
                    out[c] = d
                    d = d * decs[i][c][:, blks[h]] + incr[(i, h)][c]
                leaving[(i, h)] = out
            dst_ref[h] = d
        lv_b = {u: [leaving[u][c].astype(BF) for c in range(nch)] for u in units}
        dv_s = {u: [_dot(ktl[u][rs], lv_b[u][c], NT) for c, rs in enumerate(chunks)] for u in units}
        dqd_s = {u: [_dot(do[u][rs], st[u][c].astype(BF)) for c, rs in enumerate(chunks)] for u in units}
        dkt_s = {u: [_dot(vh[u][rs], lv_b[u][c]) for c, rs in enumerate(chunks)] for u in units}
        ddec = {u: [jnp.sum(leaving[u][c] * st[u][c], axis=0, keepdims=True) for c in range(nch)] for u in units}
        for i, h in units:
            dv_ref[subs[i], lanes[h]] = (dv[(i, h)] + jnp.concatenate(dv_s[(i, h)], axis=0)).astype(dv_ref.dtype)
        dqd = {u: dqd[u] + jnp.concatenate(dqd_s[u], axis=0) for u in units}
        dkt = {u: jnp.concatenate(dkt_s[u], axis=0) for u in units}

        db = []
        for i, sl in enumerate(subs):
            _, eq, ek, f, qd, kd, kt, _ = com[i]
            parts = []
            for pair in range(GLA_HEADS // 2):
                blk, u0, u1 = blks[2 * pair], (i, 2 * pair), (i, 2 * pair + 1)
                dqd_b, dkd_b, dkt_b = dqd[u0] + dqd[u1], dkd[u0] + dkd[u1], dkt[u0] + dkt[u1]
                dqk_ref[sl, blk] = (dqd_b * (eq[:, blk] * GLA_DK ** -0.5)).astype(dqk_ref.dtype)
                dqk_ref[sl, 256 + LANES * pair:256 + LANES * (pair + 1)] = (dkd_b * ek[:, blk] + dkt_b * f[:, blk]).astype(dqk_ref.dtype)
                dkt_kt = dkt_b * kt[:, blk]
                dbp = dqd_b * qd[:, blk] - dkd_b * kd[:, blk] - dkt_kt
                dbl = [jnp.sum(dkt_kt[rs], axis=0, keepdims=True) + (ddec[u0][c] + ddec[u1][c]) * decs[i][c][:, blk]
                       for c, rs in enumerate(chunks)]
                parts.append(jnp.where(last_row, dbp + _per_chunk_rows(dbl, LANES), dbp))
            db.append(jnp.concatenate(parts, axis=1))
        dla = [_tri_matmul(triu_b, db[i]) for i in range(nsub)]
        dz32 = [dla[i] * (1.0 / GLA_TAU) * _sigmoid(-com[i][0]) for i in range(nsub)]
        dz = [t.astype(BF) for t in dz32]
        for i, sl in enumerate(subs):
            dga_ref[sl, :] = _dot(dz[i], wd, NT).astype(dga_ref.dtype)
            dwd_ref[...] += _dot(ga[i].astype(BF), dz[i], TN)
            dbd_ref[...] += jnp.sum(dz32[i], axis=0, keepdims=True)

    def col(width, off):
        return pl.BlockSpec((tb, width), lambda i: (nb - 1 - i, off // width))

    def rev(width):
        return pl.BlockSpec((tb, width), lambda i: (nb - 1 - i, 0))

    return pl.pallas_call(
        body, name="gla_bwd", grid=(nb,),
        in_specs=[col(512, OFF_QK), col(512, OFF_V), col(512, OFF_GZ), col(LANES, OFF_GA), rev(512), rev(512),
                  pl.BlockSpec((tb // GLA_CHUNK, GLA_HEADS, LANES, LANES), lambda i: (nb - 1 - i, 0, 0, 0)),
                  _full((LANES, 256)), _full((1, 256)), _full((1, 512)), _full((sub, sub)), _full((sub, sub))],
        out_specs=[rev(512), rev(512), rev(512), rev(LANES), _full((LANES, 256)), _full((1, 256)), _full((1, 512))],
        out_shape=[jax.ShapeDtypeStruct((s, 512), BF), jax.ShapeDtypeStruct((s, 512), BF),
                   jax.ShapeDtypeStruct((s, 512), BF), jax.ShapeDtypeStruct((s, LANES), BF),
                   jax.ShapeDtypeStruct((LANES, 256), F32), jax.ShapeDtypeStruct((1, 256), F32),
                   jax.ShapeDtypeStruct((1, 512), F32)],
        scratch_shapes=[pltpu.VMEM((GLA_HEADS, LANES, LANES), F32)],
        compiler_params=_params(("arbitrary",)),
    )(proj, proj, proj, proj, dog, opre, sprev, wdecp, bdec, ggla, _gla_triangle(sub), _gla_triangle(sub).T)


_SWA_COL_HEADS = (0, 2, 1, 3, 4, 6, 5, 7)
_SWA_COLS = SWA_HEADS * SWA_BLOCK


def _swa_masks():
    lo2 = lax.broadcasted_iota(jnp.int32, (2 * SWA_BLOCK, LANES), 1) < 64
    lane1 = lax.broadcasted_iota(jnp.int32, (SWA_BLOCK, LANES), 1)
    first_half = (lane1 % 64) < 32
    key = lax.broadcasted_iota(jnp.int32, (SWA_BLOCK, _SWA_COLS), 0)
    query = lax.broadcasted_iota(jnp.int32, (SWA_BLOCK, _SWA_COLS), 1) % SWA_BLOCK
    return lo2, lane1 < 64, first_half, key > query


def _merge_band(t, prev_mask, prev_bias=None):
    prev = t[:SWA_BLOCK] if prev_bias is None else t[:SWA_BLOCK] + prev_bias
    return jnp.where(prev_mask, prev, t[SWA_BLOCK:])


def _split_band(t, prev_mask_b):
    prev = t * prev_mask_b
    return jnp.concatenate([prev, t - prev], axis=0)


def _kv_variants(t, lo2):
    tr = pltpu.roll(t, 64, 1)
    lo_v = [jnp.where(lo2, t, 0.0).astype(BF), jnp.where(lo2, tr, 0.0).astype(BF)]
    hi_v = [jnp.where(lo2, 0.0, tr).astype(BF), jnp.where(lo2, 0.0, t).astype(BF)]
    return lo_v, hi_v


def _kv_variants_t(t):
    tt = t.T
    sw = jnp.concatenate([tt[64:], tt[:64]], axis=0)
    top = lax.broadcasted_iota(jnp.int32, tt.shape, 0) < 64
    lo_v = [jnp.where(top, tt, 0.0).astype(BF), jnp.where(top, sw, 0.0).astype(BF)]
    hi_v = [jnp.where(top, 0.0, sw).astype(BF), jnp.where(top, 0.0, tt).astype(BF)]
    return lo_v, hi_v


def _swa_scores(qg, k_lo, k_hi):
    return jnp.concatenate([_dot(k_lo[0], qg[0], NT), _dot(k_hi[0], qg[0], NT),
                            _dot(k_lo[1], qg[1], NT), _dot(k_hi[1], qg[1], NT)], axis=1)


def _sink_row(sinks_ref):
    return jnp.concatenate([jnp.full((1, SWA_BLOCK), sinks_ref[0, hd], F32) for hd in _SWA_COL_HEADS], axis=1)


def _swa_softmax(st, prev_mask, prev_bias, sink):
    st = _merge_band(st, prev_mask, prev_bias)
    m = jnp.maximum(jnp.max(st, axis=0, keepdims=True), sink)
    ex = jnp.exp(st - m)
    es = jnp.exp(sink - m)
    inv = 1.0 / (jnp.sum(ex, axis=0, keepdims=True) + es)
    return ex, es, inv


def _no_prev_bias(block_index):
    return jnp.where(block_index > 0, 0.0, -1e30).astype(F32)


def _swa_queries(sq_ref, rows, cosb, sinb, first_half):
    qs = [_rope(sq_ref[rows, p * LANES:(p + 1) * LANES], cosb, sinb, first_half) * 0.125 for p in range(4)]
    return [jnp.concatenate(qs[0:2], axis=0), jnp.concatenate(qs[2:4], axis=0)]


def _phase_steps(nsteps, phases):
    return [min(nsteps - 1, (k * nsteps) // phases) for k in range(phases - 1)] + [nsteps - 1]


def _swa_fwd(proj, cos, sin, sinks, half_out):
    s = proj.shape[0]
    nq = min(SWA_QBLOCKS_FWD, s // SWA_BLOCK)
    tq = nq * SWA_BLOCK
    steps = _phase_steps(s // tq, 4)

    def body(sq_ref, sz_ref, sk_ref, sv_ref, cos_ref, sin_ref, sinks_ref, hout_hbm, os_ref, opre_ref, wout_hbm,
             kprev, vprev, *gather_sems):
        n = pl.program_id(0)

        @pl.when(n == 0)
        def _():
            kprev[...] = jnp.zeros_like(kprev)
            vprev[...] = jnp.zeros_like(vprev)

        gather = _Gather(hout_hbm, wout_hbm, *gather_sems, chunks=WEIGHT_CHUNKS)
        for step, phase in zip(steps, (gather.start, gather.pass_on, gather.relay_diagonal, gather.finish)):
            pl.when(n == step)(phase)

        lo2, _, first_half, prev_mask = _swa_masks()
        prev_mask_b = jnp.where(prev_mask, 1.0, 0.0).astype(BF)
        sink = _sink_row(sinks_ref)
        blocks = range(nq)
        rows = [slice(j * SWA_BLOCK, (j + 1) * SWA_BLOCK) for j in blocks]
        cosb = [cos_ref[rows[j], :] for j in blocks]
        sinb = [sin_ref[rows[j], :] for j in blocks]
        kc = [_rope(sk_ref[rows[j], :], cosb[j], sinb[j], first_half) for j in blocks]
        vc = [sv_ref[rows[j], :] for j in blocks]
        kcat = [jnp.concatenate([kprev[...] if j == 0 else kc[j - 1], kc[j]], axis=0) for j in blocks]
        vcat = [jnp.concatenate([vprev[...] if j == 0 else vc[j - 1], vc[j]], axis=0) for j in blocks]
        kprev[...] = kc[-1]
        vprev[...] = vc[-1]
        kvar = [_kv_variants(kcat[j], lo2) for j in blocks]
        vtvar = [_kv_variants_t(vcat[j]) for j in blocks]
        qg = [[q.astype(BF) for q in _swa_queries(sq_ref, rows[j], cosb[j], sinb[j], first_half)] for j in blocks]
        st = [_swa_scores(qg[j], *kvar[j]) for j in blocks]
        soft = [_swa_softmax(st[j], prev_mask, _no_prev_bias(n) if j == 0 else None, sink) for j in blocks]
        pt = [_split_band(soft[j][0].astype(BF), prev_mask_b) for j in blocks]
        og = {}
        for j in blocks:
            inv = soft[j][2]
            for g in range(2):
                c0, c1, c2 = 512 * g, 512 * g + 256, 512 * g + 512
                ot = (_dot(vtvar[j][0][g], pt[j][:, c0:c1]) * inv[:, c0:c1]
                      + _dot(vtvar[j][1][g], pt[j][:, c1:c2]) * inv[:, c1:c2])
                og[(j, g)] = ot.T
        for j in blocks:
            for g in range(2):
                for i in range(2):
                    ls = slice((2 * g + i) * LANES, (2 * g + i + 1) * LANES)
                    o = og[(j, g)][i * SWA_BLOCK:(i + 1) * SWA_BLOCK]
                    sz = sz_ref[rows[j], ls]
                    opre_ref[rows[j], ls] = o
                    os_ref[rows[j], ls] = (o * (sz * _sigmoid(sz))).astype(os_ref.dtype)

    def col(width, off):
        return pl.BlockSpec((tq, width), lambda i: (i, off // width))

    row = pl.BlockSpec((tq, LANES), lambda i: (i, 0))
    return pl.pallas_call(
        body, name="swa_fwd", grid=(s // tq,),
        in_specs=[col(512, OFF_SQ), col(512, OFF_SZ), col(LANES, OFF_SK), col(LANES, OFF_SV), row, row,
                  pl.BlockSpec(memory_space=pltpu.SMEM), pl.BlockSpec(memory_space=pl.ANY)],
        out_specs=[pl.BlockSpec((tq, 512), lambda i: (i, 0))] * 2 + [pl.BlockSpec(memory_space=pl.ANY)],
        out_shape=[jax.ShapeDtypeStruct((s, 512), BF), jax.ShapeDtypeStruct((s, 512), F32),
                   jax.ShapeDtypeStruct((8,) + half_out.shape, half_out.dtype)],
        scratch_shapes=[pltpu.VMEM((SWA_BLOCK, LANES), F32)] * 2 + _gather_sems(WEIGHT_CHUNKS),
        compiler_params=_params(("arbitrary",)),
    )(proj, proj, proj, proj, cos, sin, sinks, half_out)


def _swa_bwd(proj, dos, opre, cos, sin, sinks, dw_out_parts):
    s = proj.shape[0]
    nq = min(SWA_QBLOCKS, s // SWA_BLOCK)
    tq = nq * SWA_BLOCK
    steps = _phase_steps(s // tq, 5)
    _, r_out, c_out = dw_out_parts.shape

    def body(sq_ref, sz_ref, sk_ref, sv_ref, dos_ref, opre_ref, cos_ref, sin_ref, sinks_ref, pout_hbm,
             dsq_ref, dsz_ref, dsk_ref, dsv_ref, dsink_ref, gout_hbm, kprev, vprev, cprev, sprev, *reduce_scratch):
        n = pl.program_id(0)

        @pl.when(n == 0)
        def _():
            kprev[...] = jnp.zeros_like(kprev)
            vprev[...] = jnp.zeros_like(vprev)
            cprev[...] = jnp.zeros_like(cprev)
            sprev[...] = jnp.zeros_like(sprev)
            for hd in range(SWA_HEADS):
                dsink_ref[0, hd] = 0.0

        reduce = _Reduce(pout_hbm, gout_hbm, *reduce_scratch)
        phases = (reduce.start, reduce.combine_and_send, reduce.send_joint, reduce.total_and_share, reduce.finish)
        for step, phase in zip(steps, phases):
            pl.when(n == step)(phase)

        lo2, lo1, first_half, prev_mask = _swa_masks()
        prev_mask_b = jnp.where(prev_mask, 1.0, 0.0).astype(BF)
        lo1s = jnp.concatenate([lo1, lo1], axis=0)
        sink = _sink_row(sinks_ref)

        def home(m0, m1):
            t0 = m0 + pltpu.roll(m0, 64, 1)
            t1 = m1 + pltpu.roll(m1, 64, 1)
            return jnp.where(lo2, t0, t1)

        kp, vp, cp_, sp_ = kprev[...], vprev[...], cprev[...], sprev[...]
        for j in range(nq):
            rows = slice(j * SWA_BLOCK, (j + 1) * SWA_BLOCK)
            blk = n * nq + j
            cosb, sinb = cos_ref[rows, :], sin_ref[rows, :]
            kc = _rope(sk_ref[rows, :], cosb, sinb, first_half)
            vc = sv_ref[rows, :]
            kcat = jnp.concatenate([kp, kc], axis=0)
            k_lo, k_hi = _kv_variants(kcat, lo2)
            kt_lo, kt_hi = _kv_variants_t(kcat)
            v_lo, v_hi = _kv_variants(jnp.concatenate([vp, vc], axis=0), lo2)
            qg32 = _swa_queries(sq_ref, rows, cosb, sinb, first_half)
            qg = [q.astype(BF) for q in qg32]
            ex, es, inv = _swa_softmax(_swa_scores(qg, k_lo, k_hi), prev_mask, _no_prev_bias(n) if j == 0 else None, sink)
            pr, ps = ex * inv, es * inv

            dog32 = []
            for g in range(2):
                parts = []
                for i in range(2):
                    ls = slice((2 * g + i) * LANES, (2 * g + i + 1) * LANES)
                    sz = sz_ref[rows, ls]
                    sg = _sigmoid(sz)
                    dos_p = dos_ref[rows, ls]
                    dsz_ref[rows, ls] = (dos_p * opre_ref[rows, ls] * (sg * (1.0 + sz * (1.0 - sg)))).astype(dsz_ref.dtype)
                    parts.append(dos_p * (sz * sg))
                dog32.append(jnp.concatenate(parts, axis=0))
            dog = [t.astype(BF) for t in dog32]
            dpr = _merge_band(jnp.concatenate([_dot(v_lo[0], dog[0], NT), _dot(v_hi[0], dog[0], NT),
                                               _dot(v_lo[1], dog[1], NT), _dot(v_hi[1], dog[1], NT)], axis=1), prev_mask)
            rd = jnp.sum(pr * dpr, axis=0, keepdims=True)
            ds = _split_band((pr * (dpr - rd)).astype(BF), prev_mask_b)
            prb = _split_band(pr.astype(BF), prev_mask_b)
            sink_term = ps * rd
            for r, hd in enumerate(_SWA_COL_HEADS):
                dsink_ref[0, hd] += -jnp.sum(sink_term[:, r * SWA_BLOCK:(r + 1) * SWA_BLOCK])

            dk_g, dv_g = [], []
            for g in range(2):
                c0, c1, c2 = 512 * g, 512 * g + 256, 512 * g + 512
                dq = (_dot(kt_lo[g], ds[:, c0:c1]) + _dot(kt_hi[g], ds[:, c1:c2])).T
                for i in range(2):
                    ls = slice((2 * g + i) * LANES, (2 * g + i + 1) * LANES)
                    dsq_ref[rows, ls] = _rope_t(dq[i * SWA_BLOCK:(i + 1) * SWA_BLOCK] * 0.125, cosb, sinb,
                                                first_half).astype(dsq_ref.dtype)
                q_split = jnp.concatenate([jnp.where(lo1s, qg32[g], 0.0), jnp.where(lo1s, 0.0, qg32[g])], axis=0).astype(BF)
                do_split = jnp.concatenate([jnp.where(lo1s, dog32[g], 0.0), jnp.where(lo1s, 0.0, dog32[g])], axis=0).astype(BF)
                dk_g.append(_dot(ds[:, c0:c2], q_split))
                dv_g.append(_dot(prb[:, c0:c2], do_split))
            dk = home(dk_g[0], dk_g[1])
            dv = home(dv_g[0], dv_g[1])
            cur = pl.ds(pl.multiple_of(blk * SWA_BLOCK, SWA_BLOCK), SWA_BLOCK)
            dsk_ref[cur, :] = _rope_t(dk[SWA_BLOCK:], cosb, sinb, first_half)
            dsv_ref[cur, :] = dv[SWA_BLOCK:]
            dk_prev = _rope_t(dk[:SWA_BLOCK], cp_, sp_, first_half)
            dv_prev = dv[:SWA_BLOCK]
            if j == 0:
                @pl.when(n > 0)
                def _():
                    prv = pl.ds(pl.multiple_of((blk - 1) * SWA_BLOCK, SWA_BLOCK), SWA_BLOCK)
                    dsk_ref[prv, :] += dk_prev
                    dsv_ref[prv, :] += dv_prev
            else:
                prv = pl.ds(pl.multiple_of((blk - 1) * SWA_BLOCK, SWA_BLOCK), SWA_BLOCK)
                dsk_ref[prv, :] += dk_prev
                dsv_ref[prv, :] += dv_prev
            kp, vp, cp_, sp_ = kc, vc, cosb, sinb
        kprev[...] = kp
        vprev[...] = vp
        cprev[...] = cp_
        sprev[...] = sp_

    def col(width, off):
        return pl.BlockSpec((tq, width), lambda i: (i, off // width))

    row = pl.BlockSpec((tq, LANES), lambda i: (i, 0))
    wide = pl.BlockSpec((tq, 512), lambda i: (i, 0))
    return pl.pallas_call(
        body, name="swa_bwd", grid=(s // tq,),
        in_specs=[col(512, OFF_SQ), col(512, OFF_SZ), col(LANES, OFF_SK), col(LANES, OFF_SV), wide, wide, row, row,
                  pl.BlockSpec(memory_space=pltpu.SMEM), pl.BlockSpec(memory_space=pl.ANY)],
        out_specs=[wide, wide, _full((s, LANES)), _full((s, LANES)), pl.BlockSpec(memory_space=pltpu.SMEM),
                   pl.BlockSpec(memory_space=pl.ANY)],
        out_shape=[jax.ShapeDtypeStruct((s, 512), BF), jax.ShapeDtypeStruct((s, 512), BF),
                   jax.ShapeDtypeStruct((s, LANES), F32), jax.ShapeDtypeStruct((s, LANES), F32),
                   jax.ShapeDtypeStruct((1, SWA_HEADS), F32), jax.ShapeDtypeStruct((r_out, c_out), F32)],
        scratch_shapes=[pltpu.VMEM((SWA_BLOCK, LANES), F32)] * 4 + _reduce_scratch(r_out, c_out),
        compiler_params=_params(("arbitrary",)),
    )(proj, proj, proj, proj, dos, opre, cos, sin, sinks, dw_out_parts)


def _outproj(og, osw, w_out, x2d, target, gate, g_final):
    s = x2d.shape[0]
    tm = min(512, s)

    def body(og_ref, os_ref, w_ref, x_ref, t_ref, gate_ref, gf_ref,
             dx2_ref, dog_ref, dos_ref, dw_ref, loss_ref, dgf_ref, dgate_ref):
        @pl.when(pl.program_id(0) == 0)
        def _():
            dw_ref[...] = jnp.zeros_like(dw_ref)
            loss_ref[...] = jnp.zeros_like(loss_ref)
            dgf_ref[...] = jnp.zeros_like(dgf_ref)
            dgate_ref[...] = jnp.zeros_like(dgate_ref)

        w = w_ref[...]
        gate, gf = gate_ref[...], gf_ref[...]
        subs = _subtiles(tm)
        ogv = [og_ref[sl, :] for sl in subs]
        osv = [os_ref[sl, :] for sl in subs]
        y = [_dot(ogv[k], w[:512]) + _dot(osv[k], w[512:]) for k in range(len(subs))]
        dys = []
        for k, sl in enumerate(subs):
            x2 = x_ref[sl, :] + gate * y[k]
            r = lax.rsqrt(jnp.mean(x2 * x2, axis=-1, keepdims=True) + RMS_EPS)
            xn = x2 * r
            err = xn * gf - t_ref[sl, :]
            loss_ref[...] += 0.5 * jnp.sum(jnp.mean(err * err, axis=-1, keepdims=True), axis=0, keepdims=True)
            dyf = err * (1.0 / D_MODEL)
            dgf_ref[...] += jnp.sum(dyf * xn, axis=0, keepdims=True)
            t = dyf * gf
            dx2 = r * (t - xn * jnp.mean(t * xn, axis=-1, keepdims=True))
            dx2_ref[sl, :] = dx2
            dgate_ref[...] += jnp.sum(dx2 * y[k], axis=0, keepdims=True)
            dys.append((dx2 * gate).astype(BF))
            dmix = _dot(dys[k], w, NT)
            dog_ref[sl, :] = dmix[:, :512]
            dos_ref[sl, :] = dmix[:, 512:]
        dy = jnp.concatenate(dys, axis=0)
        dw_ref[:512, :] += _dot(og_ref[...], dy, TN)
        dw_ref[512:, :] += _dot(os_ref[...], dy, TN)

    half = pl.BlockSpec((tm, 512), lambda i: (i, 0))
    rowb = pl.BlockSpec((tm, D_MODEL), lambda i: (i, 0))
    vec = _full((1, D_MODEL))
    return pl.pallas_call(
        body, name="outproj", grid=(s // tm,),
        in_specs=[half, half, _full((D_MODEL, D_MODEL)), rowb, rowb, vec, vec],
        out_specs=[rowb, half, half, _full((D_MODEL, D_MODEL)), _full((1, 1)), vec, vec],
        out_shape=[jax.ShapeDtypeStruct((s, D_MODEL), F32), jax.ShapeDtypeStruct((s, 512), F32),
                   jax.ShapeDtypeStruct((s, 512), F32), jax.ShapeDtypeStruct((D_MODEL, D_MODEL), F32),
                   jax.ShapeDtypeStruct((1, 1), F32), jax.ShapeDtypeStruct((1, D_MODEL), F32),
                   jax.ShapeDtypeStruct((1, D_MODEL), F32)],
        compiler_params=_params(("arbitrary",)),
    )(og, osw, w_out, x2d, target, gate, g_final)


_PIECES = ((OFF_QK, 512), (OFF_V, 512), (OFF_GZ, 512), (OFF_SQ, 512), (OFF_SZ, 512),
           (OFF_SK, LANES), (OFF_SV, LANES), (OFF_GA, LANES))

_UNPAD_ROWS = ((OFF_QK, 0, 1024),
               (OFF_GA, 1024, GLA_RANK),
               (OFF_GZ, 1040, 1024),
               (OFF_SK, 2064, 256),
               (OFF_SZ, 2320, 512))


def _inproj_bwd(x2d, shift, sc1p, g_norm, w_t, dx2, pieces):
    s = x2d.shape[0]
    tm = min(512, s)
    nsteps = s // tm

    def body(x_ref, sh_ref, sc_ref, g_ref, w_hbm, dx2_ref, *rest):
        piece_refs = rest[:len(_PIECES)]
        gx_ref, dw_hbm, dsh_ref, dsc_ref, dg_ref, w_vm, dw_vm, in_sems, out_sems = rest[len(_PIECES):]
        i = pl.program_id(0)

        @pl.when(i == 0)
        def _():
            loads = _load_w_padded(w_hbm, w_vm, in_sems)
            dw_vm[...] = jnp.zeros_like(dw_vm)
            dsh_ref[...] = jnp.zeros_like(dsh_ref)
            dsc_ref[...] = jnp.zeros_like(dsc_ref)
            dg_ref[...] = jnp.zeros_like(dg_ref)
            for cp in loads:
                cp.wait()

        g, sc1p_v, shift_v = g_ref[...], sc_ref[...], sh_ref[...]
        subs = _subtiles(tm)
        dhs = []
        for sl in subs:
            dh = None
            for (off, width), pr in zip(_PIECES, piece_refs):
                part = _dot(pr[sl, :].astype(BF), w_vm[off:off + width, :])
                dh = part if dh is None else dh + part
            dhs.append(dh)
        norm = [_modnorm(x_ref[sl, :], g, sc1p_v, shift_v) for sl in subs]
        hb = jnp.concatenate([h.astype(BF) for _, _, h in norm], axis=0)
        for (off, width), pr in zip(_PIECES, piece_refs):
            dw_vm[off:off + width, :] += _dot(pr[...].astype(BF), hb, TN)
        for sl, (xn, r, _), dh in zip(subs, norm, dhs):
            dsh_ref[...] += jnp.sum(dh, axis=0, keepdims=True)
            dsc_ref[...] += jnp.sum(dh * (xn * g), axis=0, keepdims=True)
            dg_ref[...] += jnp.sum(dh * xn * sc1p_v, axis=0, keepdims=True)
            dxn = dh * g * sc1p_v
            gx_ref[sl, :] = dx2_ref[sl, :] + r * (dxn - xn * jnp.mean(dxn * xn, axis=-1, keepdims=True))

        @pl.when(i == nsteps - 1)
        def _():
            copies = [pltpu.make_async_copy(dw_vm.at[src:src + n], dw_hbm.at[dst:dst + n], out_sems.at[k])
                      for k, (src, dst, n) in enumerate(_UNPAD_ROWS)]
            for cp in copies:
                cp.start()
            for cp in copies:
                cp.wait()

    rowb = pl.BlockSpec((tm, D_MODEL), lambda i: (i, 0))
    vec = _full((1, D_MODEL))
    anyspec = pl.BlockSpec(memory_space=pl.ANY)
    piece_specs = [pl.BlockSpec((tm, width), lambda i: (i, 0)) for _, width in _PIECES]
    return pl.pallas_call(
        body, name="inproj_bwd", grid=(nsteps,),
        in_specs=[rowb, vec, vec, vec, anyspec, rowb] + piece_specs,
        out_specs=[rowb, anyspec, vec, vec, vec],
        out_shape=[jax.ShapeDtypeStruct((s, D_MODEL), F32), jax.ShapeDtypeStruct((D_IN, D_MODEL), F32),
                   jax.ShapeDtypeStruct((1, D_MODEL), F32), jax.ShapeDtypeStruct((1, D_MODEL), F32),
                   jax.ShapeDtypeStruct((1, D_MODEL), F32)],
        scratch_shapes=[pltpu.VMEM((D_PAD, D_MODEL), BF), pltpu.VMEM((D_PAD, D_MODEL), F32),
                        pltpu.SemaphoreType.DMA((len(_UNPAD_ROWS),)), pltpu.SemaphoreType.DMA((len(_UNPAD_ROWS),))],
        compiler_params=_params(("arbitrary",)),
    )(x2d, shift, sc1p, g_norm, w_t, dx2, *pieces)


def _adam(w, g, m, v):
    m2 = ADAM_B1 * m + (1.0 - ADAM_B1) * g
    v2 = ADAM_B2 * v + (1.0 - ADAM_B2) * (g * g)
    m_hat = m2 / (1.0 - ADAM_B1 ** ADAM_STEP)
    v_hat = v2 / (1.0 - ADAM_B2 ** ADAM_STEP)
    delta = -ADAM_LR * (m_hat / (jnp.sqrt(v_hat) + ADAM_EPS) + ADAM_WD * w)
    return delta, m2, v2


def _adamw_t(w3, g_window, m3, v3, name):
    rr, _, cc = w3.shape
    parts = [slice(q * (cc // 4), (q + 1) * (cc // 4)) for q in range(4)]
    starts = sorted({(rr * j) % 8 for j in range(4)})

    def body(w_hbm, gw_hbm, m_hbm, v_hbm, d_hbm, m2_hbm, v2_hbm, g3_hbm,
             w_vm, m_vm, v_vm, gw_vm, d_vm, m2_vm, v2_vm, g_vm, in_sems, out_sems):
        start = lax.rem(rr * (2 * lax.axis_index("x") + lax.axis_index("y")), 8)
        ins = ((w_hbm, w_vm), (m_hbm, m_vm), (v_hbm, v_vm))
        outs = ((d_vm, d_hbm), (m2_vm, m2_hbm), (v2_vm, v2_hbm), (g_vm, g3_hbm))
        loads = [[pltpu.make_async_copy(src.at[:, 0, p], dst.at[:, p], in_sems.at[4 * q + k]) for k, (src, dst) in enumerate(ins)]
                 + [pltpu.make_async_copy(gw_hbm.at[:, p], gw_vm.at[:, p], in_sems.at[4 * q + 3])]
                 for q, p in enumerate(parts)]
        stores = [[pltpu.make_async_copy(src.at[:, p], dst.at[:, 0, p], out_sems.at[4 * q + k]) for k, (src, dst) in enumerate(outs)]
                  for q, p in enumerate(parts)]
        for group in loads:
            for cp in group:
                cp.start()
        for q, p in enumerate(parts):
            for cp in loads[q]:
                cp.wait()
            g = gw_vm[starts[0]:starts[0] + rr, p]
            for o in starts[1:]:
                g = jnp.where(start == o, gw_vm[o:o + rr, p], g)
            g_vm[:, p] = g
            d_vm[:, p], m2_vm[:, p], v2_vm[:, p] = _adam(w_vm[:, p], g, m_vm[:, p], v_vm[:, p])
            for cp in stores[q]:
                cp.start()
        for group in stores:
            for cp in group:
                cp.wait()

    hbm = pl.BlockSpec(memory_space=pl.ANY)
    return pl.pallas_call(
        body, name=name, grid=(1,), in_specs=[hbm] * 4,
        out_specs=[hbm] * 4, out_shape=[jax.ShapeDtypeStruct((rr, 1, cc), F32)] * 4,
        scratch_shapes=[pltpu.VMEM((rr, cc), F32)] * 3 + [pltpu.VMEM(g_window.shape, F32)] + [pltpu.VMEM((rr, cc), F32)] * 4
        + [pltpu.SemaphoreType.DMA((16,)), pltpu.SemaphoreType.DMA((16,))],
        compiler_params=_params(("arbitrary",)),
    )(w3, g_window, m3, v3)


def _small_update(parts, weights, moms, vels):
    n = len(weights)

    def body(*refs):
        p_refs, w_refs, m_refs, v_refs = refs[:n + 1], refs[n + 1:2 * n + 1], refs[2 * n + 1:3 * n + 1], refs[3 * n + 1:4 * n + 1]
        outs = refs[4 * n + 1:]
        for i in range(n):
            g = p_refs[i][0]
            for d in range(1, 8):
                g = g + p_refs[i][d]
            delta, m2, v2 = _adam(w_refs[i][...], g, m_refs[i][...], v_refs[i][...])
            outs[4 * i][...] = g
            outs[4 * i + 1][...] = delta
            outs[4 * i + 2][...] = m2
            outs[4 * i + 3][...] = v2
        tot = p_refs[n][0]
        for d in range(1, 8):
            tot = tot + p_refs[n][d]
        outs[4 * n][...] = tot

    out_shape = []
    for w in weights:
        out_shape += [jax.ShapeDtypeStruct(w.shape, F32)] * 4
    out_shape.append(jax.ShapeDtypeStruct(parts[n].shape[1:], F32))
    return pl.pallas_call(body, name="small_update", out_shape=out_shape, compiler_params=_params())(
        *parts, *weights, *moms, *vels)


def _rows8(a):
    flat = a.reshape(-1)
    rows = -(-flat.shape[0] // LANES)
    rows8 = -(-rows // 8) * 8
    flat = jnp.pad(flat, (0, rows8 * LANES - flat.shape[0]))
    return flat.reshape(rows8, LANES)


def kernel(x, c, positions, w_ada, b_ada, g_norm, w_in, w_decay, b_decay, g_gla_head, sinks, w_out, g_final, loss_target, m_w_ada, m_b_ada, m_g_norm, m_w_in, m_w_decay, m_b_decay, m_g_gla_head, m_sinks, m_w_out, m_g_final, v_w_ada, v_b_ada, v_g_norm, v_w_in, v_w_decay, v_b_decay, v_g_gla_head, v_sinks, v_w_out, v_g_final):
    ax, ay, ac = lax.axis_index("x"), lax.axis_index("y"), lax.axis_index("c")
    chip = 2 * ax + ay
    dev = 2 * chip + ac
    s = x.shape[1]
    x2d = x[0]
    target = loss_target[0]
    w_ada2, w_out2, w_dec2 = w_ada[0], w_out[0], w_decay[0]
    w_in_t = w_in[0].T
    ada_cols = w_ada2.shape[1]
    in_cols = w_in_t.shape[0]
    out_rows = w_out2.shape[0]
    half = D_MODEL // 2

    cw = jnp.concatenate([c.reshape(8, LANES), w_dec2.reshape(8, LANES)], axis=0)
    b_shard = lax.dynamic_slice(b_ada, (0, chip * ada_cols), (1, ada_cols))
    half_out = lax.dynamic_slice(w_out2, (ac * (out_rows // 2), 0), (out_rows // 2, D_MODEL)).astype(BF)
    inv_freq = 1.0 / (ROPE_THETA ** (jnp.arange(0, 64, 2, dtype=F32) / 64))
    room = _w_window(in_cols) - in_cols
    win_window = lax.dynamic_slice(jnp.pad(w_in_t, ((room, room), (0, 0))), (room - (in_cols * chip) % W_TILE, ac * half),
                                   (_w_window(in_cols), half)).astype(BF)
    win_edges = jnp.stack([win_window[:W_TILE], win_window[-W_TILE:]])
    first, mod_all, w_t, cos, sin = _prologue(
        cw, w_ada2, b_shard, win_window, win_edges, in_cols, positions.reshape(s // LANES, LANES), jnp.tile(inv_freq, 4).reshape(1, LANES))

    first = first.reshape(8, 2, 8, LANES)
    c_all = first[:, 0].reshape(8, D_MODEL)
    w_dec_full = first[0::2, 1].reshape(4, GLA_RANK, 64).transpose(1, 0, 2).reshape(GLA_RANK, 256)
    mod = mod_all.reshape(4, 2, 8, ada_cols)[:, 0]
    mod = lax.dynamic_slice(mod, (0, dev, 0), (4, 1, ada_cols)).reshape(1, 4 * ada_cols)
    shift, sc1p, gate = mod[:, :D_MODEL], 1.0 + mod[:, D_MODEL:2 * D_MODEL], mod[:, 2 * D_MODEL:]
    wdecp = jnp.pad(w_dec_full, ((0, LANES - GLA_RANK), (0, 0))).astype(BF)

    proj = _inproj_fwd(x2d, shift, sc1p, g_norm, w_t)
    og, o_gla, sprev = _gla_fwd(proj, wdecp, b_decay, g_gla_head)
    osw, o_swa, w_out_all = _swa_fwd(proj, cos, sin, sinks, half_out)
    w_out_all = w_out_all.reshape(D_MODEL, D_MODEL)
    dx2, dog, dos, dw_out, loss_p, dgf, dgate = _outproj(og, osw, w_out_all, x2d, target, gate, g_final.reshape(1, D_MODEL))
    dsq, dsz, dsk, dsv, dsinks, g_w_out = _swa_bwd(proj, dos, o_swa, cos, sin, sinks, dw_out.reshape(4, out_rows, D_MODEL))
    dqk, dv, dgz, dga, dwdp, dbd, dgg = _gla_bwd(proj, dog, o_gla, sprev, wdecp, b_decay, g_gla_head)
    pieces = (dqk, dv, dgz, dsq, dsz, dsk, dsv, dga)
    gx, dw_in_t, dshift, dscale, dgn = _inproj_bwd(x2d, shift, sc1p, g_norm, w_t, dx2, pieces)

    segs = [jnp.concatenate([dshift, dscale, dgate], axis=1), dgn, dgf, dwdp[:GLA_RANK], dbd, dgg, dsinks, loss_p]
    packed = [_rows8(a) for a in segs]
    offs = [0]
    for a in packed:
        offs.append(offs[-1] + a.shape[0])
    (g_window, small, g_w_ada, d_w_ada, nm_w_ada, nv_w_ada, d_w_out, nm_w_out, nv_w_out) = _epilogue(
        dw_in_t, jnp.concatenate(packed, axis=0), c_all, (w_ada2, m_w_ada[0], v_w_ada[0]),
        (w_out2, g_w_out, m_w_out[0], v_w_out[0]), offs[0])

    def seg(i, size):
        return small[:, offs[i]:offs[i + 1]].reshape(8, -1)[:, :size]

    dmod_all = seg(0, 3 * D_MODEL)
    dwd_all = lax.dynamic_slice(seg(3, GLA_RANK * 256).reshape(8, GLA_RANK, 256), (0, 0, chip * 64), (8, GLA_RANK, 64))
    parts = [dmod_all.reshape(8, 1, 3 * D_MODEL), seg(1, D_MODEL).reshape(8, 1, D_MODEL), dwd_all,
             seg(4, 256).reshape(8, 1, 256), seg(5, 512).reshape(8, 1, 512), seg(6, SWA_HEADS).reshape(8, 1, SWA_HEADS),
             seg(2, D_MODEL).reshape(8, 1, D_MODEL), seg(7, LANES).reshape(8, 1, LANES)]
    smalls = _small_update(
        parts,
        [b_ada, g_norm, w_dec2, b_decay, g_gla_head, sinks, g_final.reshape(1, D_MODEL)],
        [m_b_ada, m_g_norm, m_w_decay[0], m_b_decay, m_g_gla_head, m_sinks, m_g_final.reshape(1, D_MODEL)],
        [v_b_ada, v_g_norm, v_w_decay[0], v_b_decay, v_g_gla_head, v_sinks, v_g_final.reshape(1, D_MODEL)])
    (g_b_ada, d_b_ada, nm_b_ada, nv_b_ada, g_gn, d_gn, nm_gn, nv_gn, g_wd, d_wd, nm_wd, nv_wd,
     g_bd, d_bd, nm_bd, nv_bd, g_gg, d_gg, nm_gg, nv_gg, g_sk, d_sk, nm_sk, nv_sk,
     g_gf, d_gf, nm_gf, nv_gf, loss_row) = smalls
    loss = loss_row[0, 0]

    to3 = lambda a: jnp.transpose(a, (2, 0, 1))
    from3 = lambda a: jnp.transpose(a, (1, 2, 0))[0]
    d3, nm3, nv3, g3 = _adamw_t(to3(w_in), g_window, to3(m_w_in), to3(v_w_in), "adamw_w_in")
    g_w_in, d_w_in, nm_w_in, nv_w_in = from3(g3), from3(d3), from3(nm3), from3(nv3)

    flat = lambda a: a.reshape(D_MODEL)
    grads = [g_w_ada[None], g_b_ada, g_gn, g_w_in[None], g_wd[None], g_bd, g_gg, g_sk, g_w_out[None], flat(g_gf)]
    deltas = [d_w_ada[None], d_b_ada, d_gn, d_w_in[None], d_wd[None], d_bd, d_gg, d_sk, d_w_out[None], flat(d_gf)]
    new_m = [nm_w_ada[None], nm_b_ada, nm_gn, nm_w_in[None], nm_wd[None], nm_bd, nm_gg, nm_sk, nm_w_out[None], flat(nm_gf)]
    new_v = [nv_w_ada[None], nv_b_ada, nv_gn, nv_w_in[None], nv_wd[None], nv_bd, nv_gg, nv_sk, nv_w_out[None], flat(nv_gf)]
    return (loss, gx[None], *grads, *deltas, *new_m, *new_v)
```

```python
import jax
import jax.numpy as jnp
from jax import lax
from jax.experimental import pallas as pl
from jax.experimental.pallas import tpu as pltpu

F32 = jnp.float32
BF = jnp.bfloat16

D_MODEL = 1024
GLA_HEADS = 4
GLA_DK = 64
GLA_CHUNK = 64
GLA_RANK = 16
GLA_TAU = 16.0
GLA_SUB = 256
GLA_ROWS_FWD = 1024
GLA_ROWS_BWD = 512
SWA_HEADS = 8
SWA_BLOCK = 128
SWA_QBLOCKS_FWD = 8
SWA_QBLOCKS = 8
RMS_EPS = 1e-6
ROPE_THETA = 10000.0

OFF_QK, OFF_V, OFF_GZ, OFF_SQ, OFF_SZ, OFF_SK, OFF_SV, OFF_GA = 0, 512, 1024, 1536, 2048, 2560, 2688, 2816
D_PAD = 2944
D_IN = 2832
LANES = 128
VMEM_LIMIT = 56 * 1024 * 1024

ADAM_LR, ADAM_B1, ADAM_B2, ADAM_EPS, ADAM_WD, ADAM_STEP = 0.001, 0.9, 0.999, 1e-08, 0.01, 10

NT = (((1,), (1,)), ((), ()))
TN = (((0,), (0,)), ((), ()))
MESH = pl.DeviceIdType.MESH


def _dot(a, b, dims=None):
    if dims is None:
        return jnp.dot(a, b, preferred_element_type=F32)
    return lax.dot_general(a, b, dims, preferred_element_type=F32)


def _sigmoid(x):
    return 1.0 / (1.0 + jnp.exp(-x))


def _params(sem=None):
    return pltpu.CompilerParams(dimension_semantics=sem, vmem_limit_bytes=VMEM_LIMIT)


def _full(shape):
    return pl.BlockSpec(shape, lambda i: (0,) * len(shape))


def _subtiles(rows, size=256):
    size = min(size, rows)
    return [slice(k * size, (k + 1) * size) for k in range(rows // size)]


WEIGHT_CHUNKS = 4


def _gather_sems(chunks=1):
    return [pltpu.SemaphoreType.DMA((7 * chunks,)), pltpu.SemaphoreType.DMA((7 * chunks,)), pltpu.SemaphoreType.DMA]


_GATHER_SEMS = _gather_sems()


class _Gather:
    def __init__(self, x_ref, out_ref, send_sems, recv_sems, local_sem, slab=None, chunks=1):
        self.slab_of = slab
        self.chunks = chunks
        self.width = x_ref.shape[-1] // chunks
        x, y, c = lax.axis_index("x"), lax.axis_index("y"), lax.axis_index("c")
        self.me, self.sibling, self.c = (x, y, c), (x, y, 1 - c), c
        self.xn, self.yn, self.dg = (1 - x, y), (x, 1 - y), (1 - x, 1 - y)
        self.pass_from = (lax.rem(x + 1 - c, 2), lax.rem(y + c, 2))
        self.pass_to = (lax.rem(x + c, 2), lax.rem(y + 1 - c, 2))
        self.x_ref, self.out_ref, self.send_sems, self.recv_sems = x_ref, out_ref, send_sems, recv_sems
        self.mine = pltpu.make_async_copy(x_ref, self._slab(*self.me), local_sem)

    def _slab(self, px, py, pc):
        if self.slab_of is not None:
            return self.slab_of(self.out_ref, px, py, pc)
        return self.out_ref.at[4 * px + 2 * py + pc]

    def _part(self, ref, q):
        if self.chunks == 1:
            return ref
        lanes = slice(q * self.width, (q + 1) * self.width)
        return ref.at[(slice(None),) * (len(ref.shape) - 1) + (lanes,)]

    def _copy(self, k, q, blk, to, src=None):
        i = k * self.chunks + q
        return pltpu.make_async_remote_copy(
            src_ref=self._part(self._slab(*blk) if src is None else src, q), dst_ref=self._part(self._slab(*blk), q),
            send_sem=self.send_sems.at[i], recv_sem=self.recv_sems.at[i], device_id=to, device_id_type=MESH)

    def _sends(self, q):
        c = self.c
        return [self._copy(0, q, self.me, self.sibling, src=self.x_ref),
                self._copy(1, q, self.me, (*self.xn, c), src=self.x_ref),
                self._copy(2, q, self.me, (*self.yn, c), src=self.x_ref),
                self._copy(3, q, (*self.pass_from, c), (*self.pass_to, c)),
                self._copy(4, q, (*self.xn, c), self.sibling),
                self._copy(5, q, (*self.yn, c), self.sibling),
                self._copy(6, q, (*self.dg, c), self.sibling)]

    def start(self):
        self.mine.start()
        for q in range(self.chunks):
            sends = self._sends(q)
            for k in (1, 2, 0):
                sends[k].start()

    def pass_on(self, only=None):
        for q in range(self.chunks) if only is None else (only,):
            sends = self._sends(q)
            self._copy(1, q, (*self.xn, self.c), self.me).wait_recv()
            self._copy(2, q, (*self.yn, self.c), self.me).wait_recv()
            for k in (3, 4, 5):
                sends[k].start()

    def relay_diagonal(self, only=None):
        for q in range(self.chunks) if only is None else (only,):
            self._copy(3, q, (*self.dg, self.c), self.me).wait_recv()
            self._sends(q)[6].start()

    def relay(self):
        self.pass_on()
        self.relay_diagonal()

    def finish(self):
        c = self.c
        for q in range(self.chunks):
            self._copy(0, q, self.sibling, self.me).wait_recv()
            for k, chip in ((4, self.xn), (5, self.yn), (6, self.dg)):
                self._copy(k, q, (*chip, 1 - c), self.me).wait_recv()
            for cp in self._sends(q):
                cp.wait_send()
        self.mine.wait()


def _prologue(cw, w_ada, b_chips, win_window, win_edges, n_in, pos_rows, inv_freq):
    s = pos_rows.shape[0] * LANES
    rt = min(512, s)
    inner = win_window.shape[0] - 2 * W_TILE
    starts = [(n_in * j) // W_TILE * W_TILE for j in range(4)]
    edge_rows = starts + [starts[3] + inner + W_TILE]
    assert all(starts[j] + inner + W_TILE == edge_rows[j + 1] for j in range(4))

    def body(cw_ref, wada_hbm, b_ref, hin_ref, hedge_ref, pos_ref, f_ref,
             first_ref, mod_ref, win_ref, cos_hbm, sin_hbm,
             mod_blk, cos_ref, sin_ref, wada_ref, edge_ref, tile_ref, table_sems, local_sems, tile_sems, *sems):
        fetch_w = pltpu.make_async_copy(wada_hbm, wada_ref, local_sems.at[0])
        fetch_w.start()
        g_c = _Gather(cw_ref, first_ref, *sems[0:3])
        half_lanes = hin_ref.shape[1]

        def lanes_of(pc):
            return pl.ds(pl.multiple_of(pc * half_lanes, half_lanes), half_lanes)

        def inner_rows(px, py):
            return pl.ds(pl.multiple_of((n_in * (2 * px + py)) // W_TILE * W_TILE + W_TILE, W_TILE), inner)

        g_in = _Gather(hin_ref.at[pl.ds(W_TILE, inner), :], win_ref, *sems[3:6], chunks=WEIGHT_CHUNKS,
                       slab=lambda ref, px, py, pc: ref.at[inner_rows(px, py), lanes_of(pc)])
        g_mod = _Gather(mod_blk, mod_ref, *sems[6:9])
        g_edge = _Gather(hedge_ref, edge_ref, *sems[9:12],
                         slab=lambda ref, px, py, pc: ref.at[2 * px + py, :, :, lanes_of(pc)])
        g_c.start()
        g_edge.start()
        g_in.start()
        g_c.relay()
        g_edge.relay()
        g_c.finish()
        c_rows = [jnp.concatenate([first_ref[d, r:r + 1, :] for r in range(8)], axis=1) for d in range(8)]
        c_all = jnp.concatenate(c_rows, axis=0)
        sc = (c_all * _sigmoid(c_all)).astype(BF)
        fetch_w.wait()
        chip = 2 * lax.axis_index("x") + lax.axis_index("y")
        mod_blk[...] = _dot(sc, wada_ref[...].astype(BF)) + b_ref[pl.ds(chip, 1), :]
        g_mod.start()

        def rope_rows(i, carry):
            rows = pl.ds(pl.multiple_of(i * rt, rt), rt)
            cols = [jnp.transpose(jnp.broadcast_to(pos_ref[pl.ds(i * (rt // LANES) + b, 1), :].astype(F32), (LANES, LANES)))
                    for b in range(rt // LANES)]
            ang = jnp.concatenate(cols, axis=0) * f_ref[...]
            lane = lax.broadcasted_iota(jnp.int32, ang.shape, 1)
            cos_ref[rows, :] = jnp.cos(ang)
            sn = jnp.sin(ang)
            sin_ref[rows, :] = jnp.where((lane % 64) < 32, -sn, sn)
            pltpu.make_async_copy(cos_ref.at[rows, :], cos_hbm.at[rows, :], table_sems.at[0]).start()
            pltpu.make_async_copy(sin_ref.at[rows, :], sin_hbm.at[rows, :], table_sems.at[1]).start()
            return carry

        tiles = []

        def edge_tiles():
            g_edge.finish()
            row = lax.broadcasted_iota(jnp.int32, tile_ref.shape[1:], 0)
            for k, at in enumerate(edge_rows):
                last = edge_ref[max(k - 1, 0), 1].astype(F32)
                first = edge_ref[min(k, 3), 0].astype(F32)
                cut = W_TILE if k == 4 else (n_in * k) % W_TILE
                tile_ref[k] = jnp.where(row < cut, last, first).astype(tile_ref.dtype)
                tiles.append(pltpu.make_async_copy(tile_ref.at[k], win_ref.at[at:at + W_TILE, :], tile_sems.at[k]))
                tiles[-1].start()

        waits = ([lambda q=q: g_in.pass_on(q) for q in range(WEIGHT_CHUNKS)] + [edge_tiles]
                 + [lambda q=q: g_in.relay_diagonal(q) for q in range(WEIGHT_CHUNKS)] + [g_mod.relay])
        steps = s // rt
        lead = steps // 4
        per_wait = max((steps - lead) // len(waits), 1)
        lax.fori_loop(0, lead, rope_rows, 0)
        done = lead
        for wait in waits:
            wait()
            nxt = min(done + per_wait, steps)
            lax.fori_loop(done, nxt, rope_rows, 0)
            done = nxt
        lax.fori_loop(done, steps, rope_rows, 0)
        g_in.finish()
        g_mod.finish()
        for cp in tiles:
            cp.wait()
        pltpu.make_async_copy(cos_ref, cos_hbm, table_sems.at[0]).wait()
        pltpu.make_async_copy(sin_ref, sin_hbm, table_sems.at[1]).wait()

    vm = pl.BlockSpec(memory_space=pltpu.VMEM)
    hbm = pl.BlockSpec(memory_space=pl.ANY)
    half_lanes = win_window.shape[1]
    return pl.pallas_call(
        body, name="prologue",
        out_shape=[jax.ShapeDtypeStruct((8,) + cw.shape, F32), jax.ShapeDtypeStruct((8, 8, w_ada.shape[1]), F32),
                   jax.ShapeDtypeStruct((4 * n_in, 2 * half_lanes), win_window.dtype),
                   jax.ShapeDtypeStruct((s, LANES), F32), jax.ShapeDtypeStruct((s, LANES), F32)],
        in_specs=[vm, hbm, vm, hbm, vm, vm, vm], out_specs=[vm, vm, hbm, hbm, hbm],
        scratch_shapes=[pltpu.VMEM((8, w_ada.shape[1]), F32), pltpu.VMEM((s, LANES), F32), pltpu.VMEM((s, LANES), F32),
                        pltpu.VMEM(w_ada.shape, F32),
                        pltpu.VMEM((4, 2, W_TILE, 2 * half_lanes), win_window.dtype),
                        pltpu.VMEM((5, W_TILE, 2 * half_lanes), win_window.dtype),
                        pltpu.SemaphoreType.DMA((2,)), pltpu.SemaphoreType.DMA((1,)), pltpu.SemaphoreType.DMA((5,))]
        + _GATHER_SEMS + _gather_sems(WEIGHT_CHUNKS) + _GATHER_SEMS + _GATHER_SEMS,
        compiler_params=pltpu.CompilerParams(vmem_limit_bytes=VMEM_LIMIT),
    )(cw, w_ada, b_chips, win_window, win_edges, pos_rows, inv_freq)


def _reduce_scratch(rr, cc):
    c2 = cc // 2
    return [pltpu.VMEM((4, rr, c2), F32), pltpu.VMEM((4, rr, c2), F32), pltpu.VMEM((3, rr, c2), BF),
            pltpu.VMEM((2, rr, c2), BF), pltpu.VMEM((rr, c2), BF), pltpu.VMEM((rr, c2), F32),
            pltpu.SemaphoreType.DMA((8 + 3 * WEIGHT_CHUNKS,)), pltpu.SemaphoreType.DMA((8 + 3 * WEIGHT_CHUNKS,)),
            pltpu.SemaphoreType.DMA((5,))]


class _Reduce:
    def __init__(self, p_hbm, out_ref, acc_ref, own_ref, send_ref, land_ref, relay_ref, res_ref,
                 send_sems, recv_sems, local_sems, rows=None):
        x, y, c = lax.axis_index("x"), lax.axis_index("y"), lax.axis_index("c")
        part = (lambda j, ln: p_hbm.at[j, :, ln]) if rows is None else (lambda j, ln: p_hbm.at[rows(j), ln])
        c2 = out_ref.shape[1] // 2
        sibling = (x, y, 1 - c)
        first = (lax.rem(x + 1 - c, 2), lax.rem(y + c, 2))
        second = (lax.rem(x + c, 2), lax.rem(y + 1 - c, 2))
        shards = [2 * first[0] + first[1], 2 * second[0] + second[1], 2 * (1 - x) + (1 - y), 2 * x + y]
        sibling_slot = (1, 0, 2, 3)
        mine = pl.ds(pl.multiple_of(c * c2, c2), c2)
        other = pl.ds(pl.multiple_of((1 - c) * c2, c2), c2)
        self.acc_ref, self.own_ref, self.send_ref, self.land_ref = acc_ref, own_ref, send_ref, land_ref
        self.relay_ref, self.res_ref = relay_ref, res_ref
        self.own = [pltpu.make_async_copy(part(j, mine), own_ref.at[k], local_sems.at[k])
                    for k, j in enumerate(shards)]
        self.swap_out = [pltpu.make_async_remote_copy(
            src_ref=part(j, other), dst_ref=acc_ref.at[sibling_slot[k]], send_sem=send_sems.at[k],
            recv_sem=recv_sems.at[sibling_slot[k]], device_id=sibling, device_id_type=MESH) for k, j in enumerate(shards)]
        self.swap_in = [pltpu.make_async_remote_copy(
            src_ref=part(j, other), dst_ref=acc_ref.at[k], send_sem=send_sems.at[k], recv_sem=recv_sems.at[k],
            device_id=sibling, device_id_type=MESH) for k, j in enumerate(shards)]

        self.lanes = [slice(q * (c2 // WEIGHT_CHUNKS), (q + 1) * (c2 // WEIGHT_CHUNKS)) for q in range(WEIGHT_CHUNKS)]

        def message(m, src, dst, to):
            return [pltpu.make_async_remote_copy(
                src_ref=src.at[:, ln], dst_ref=dst.at[:, ln], send_sem=send_sems.at[8 + m * WEIGHT_CHUNKS + q],
                recv_sem=recv_sems.at[8 + m * WEIGHT_CHUNKS + q], device_id=(*to, c), device_id_type=MESH)
                for q, ln in enumerate(self.lanes)]

        self.direct = message(0, send_ref.at[0], land_ref.at[0], first)
        self.passed = message(1, send_ref.at[1], relay_ref, first)
        self.joint = message(2, send_ref.at[2], land_ref.at[1], second)
        self.put = pltpu.make_async_copy(res_ref, out_ref.at[:, mine], local_sems.at[4])
        self.share = pltpu.make_async_remote_copy(
            src_ref=res_ref, dst_ref=out_ref.at[:, mine], send_sem=send_sems.at[7],
            recv_sem=recv_sems.at[7], device_id=sibling, device_id_type=MESH)

    def start(self):
        for k in (2, 0, 1, 3):
            self.own[k].start()
            self.swap_out[k].start()

    def _combine(self, k):
        self.own[k].wait()
        self.swap_out[k].wait_send()
        self.swap_in[k].wait_recv()
        self.acc_ref[k] = self.acc_ref[k] + self.own_ref[k]

    def combine_and_send(self):
        dt = self.send_ref.dtype
        self._combine(2)
        self.send_ref[1] = self.acc_ref[2].astype(dt)
        for cp in self.passed:
            cp.start()
        self._combine(0)
        self.send_ref[0] = self.acc_ref[0].astype(dt)
        for cp in self.direct:
            cp.start()
        self._combine(1)
        self._combine(3)

    def send_joint(self):
        dt = self.send_ref.dtype
        for q, ln in enumerate(self.lanes):
            self.passed[q].wait_recv()
            self.send_ref[2, :, ln] = (self.acc_ref[1, :, ln] + self.relay_ref[:, ln].astype(F32)).astype(dt)
            self.joint[q].start()

    def total_and_share(self):
        for cp in self.direct + self.joint:
            cp.wait_recv()
        self.res_ref[...] = self.acc_ref[3] + self.land_ref[0].astype(F32) + self.land_ref[1].astype(F32)
        for cp in self.direct + self.passed + self.joint:
            cp.wait_send()
        self.put.start()
        self.share.start()

    def finish(self):
        self.put.wait()
        self.share.wait()


def _shard_window(n):
    return max(-(-(n * (j + 1)) // 8) * 8 - (n * j) // 8 * 8 for j in range(4))


class _LocalUpdate:
    def __init__(self, ins, in_vm, out_vm, outs, in_sems, out_sems):
        self.loads = [pltpu.make_async_copy(a, b, in_sems.at[k]) for k, (a, b) in enumerate(zip(ins, in_vm))]
        self.stores = [pltpu.make_async_copy(a, b, out_sems.at[k]) for k, (a, b) in enumerate(zip(out_vm, outs))]

    def start(self):
        for cp in self.loads:
            cp.start()

    def loaded(self):
        for cp in self.loads:
            cp.wait()

    def store(self):
        for cp in self.stores:
            cp.start()

    def finish(self):
        for cp in self.stores:
            cp.wait()


def _epilogue(dw_in_t, small, c_all, ada, out, dmod_row):
    cc = dw_in_t.shape[1]
    n = dw_in_t.shape[0] // 4
    r_in = _shard_window(n)
    n_red = len(_reduce_scratch(r_in, cc))
    ra, ca = ada[0].shape
    dm_rows = ca // LANES
    tr = min(512, ra)

    def body(pin_hbm, small_ref, c_ref, *rest):
        ada_hbm, out_hbm = rest[0:3], rest[3:7]
        gin_ref, small_all_ref = rest[7:9]
        ada_res, out_res = rest[9:13], rest[13:16]
        scratch = rest[16:]
        red_in = _Reduce(pin_hbm, gin_ref, *scratch[0:n_red],
                         rows=lambda j: pl.ds(pl.multiple_of((n * j) // 8 * 8, 8), r_in))
        gat = _Gather(small_ref, small_all_ref, *scratch[n_red:n_red + 3])
        local = scratch[n_red + 3:]
        ada_in, ada_out, out_in, out_out = local[0:3], local[3:7], local[7:11], local[11:14]
        upd_ada = _LocalUpdate(ada_hbm, ada_in, ada_out, ada_res, local[14], local[15])
        upd_out = _LocalUpdate(out_hbm, out_in, out_out, out_res, local[16], local[17])
        red_in.start()
        gat.start()
        upd_out.start()
        upd_ada.start()
        gat.relay()
        red_in.combine_and_send()
        gat.finish()
        red_in.send_joint()

        upd_out.loaded()
        out_out[0][...], out_out[1][...], out_out[2][...] = _adam(*[r[...] for r in out_in])
        upd_out.store()
        chip = 2 * lax.axis_index("x") + lax.axis_index("y")
        dm = jnp.concatenate(
            [jnp.concatenate([small_all_ref[d, pl.ds(dmod_row + dm_rows * chip + r, 1), :] for r in range(dm_rows)], axis=1)
             for d in range(8)], axis=0)
        cv = c_ref[...]
        sc = jnp.concatenate([cv * _sigmoid(cv), jnp.zeros_like(cv)], axis=0).astype(BF)
        dmb = jnp.concatenate([dm, jnp.zeros_like(dm)], axis=0).astype(BF)
        upd_ada.loaded()
        for r0 in range(0, ra, tr):
            rows = slice(r0, r0 + tr)
            g = _dot(sc[:, rows], dmb, TN)
            ada_out[0][rows, :] = g
            ada_out[1][rows, :], ada_out[2][rows, :], ada_out[3][rows, :] = _adam(
                ada_in[0][rows, :], g, ada_in[1][rows, :], ada_in[2][rows, :])
        upd_ada.store()

        red_in.total_and_share()
        red_in.finish()
        upd_out.finish()
        upd_ada.finish()

    vm = pl.BlockSpec(memory_space=pltpu.VMEM)
    anyspec = pl.BlockSpec(memory_space=pl.ANY)
    ada_buf, out_buf = pltpu.VMEM((ra, ca), F32), pltpu.VMEM(out[0].shape, F32)
    return pl.pallas_call(
        body, name="epilogue",
        out_shape=[jax.ShapeDtypeStruct((r_in, cc), F32), jax.ShapeDtypeStruct((8,) + small.shape, F32)]
        + [jax.ShapeDtypeStruct((ra, ca), F32)] * 4 + [jax.ShapeDtypeStruct(out[0].shape, F32)] * 3,
        in_specs=[anyspec, vm, vm] + [anyspec] * 7, out_specs=[anyspec, vm] + [anyspec] * 7,
        scratch_shapes=_reduce_scratch(r_in, cc) + _GATHER_SEMS + [ada_buf] * 7 + [out_buf] * 7
        + [pltpu.SemaphoreType.DMA((3,)), pltpu.SemaphoreType.DMA((4,)), pltpu.SemaphoreType.DMA((4,)), pltpu.SemaphoreType.DMA((3,))],
        compiler_params=pltpu.CompilerParams(vmem_limit_bytes=VMEM_LIMIT),
    )(dw_in_t, small, c_all, *ada, *out)


def _rope(t, cosb, sinb, first_half):
    partner = jnp.where(first_half, pltpu.roll(t, 96, 1), pltpu.roll(t, 32, 1))
    return t * cosb + partner * sinb


def _rope_t(g, cosb, sinb, first_half):
    gs = g * sinb
    partner = jnp.where(first_half, pltpu.roll(gs, 96, 1), pltpu.roll(gs, 32, 1))
    return g * cosb + partner


def _modnorm(x, g, sc1p, shift):
    r = lax.rsqrt(jnp.mean(x * x, axis=-1, keepdims=True) + RMS_EPS)
    xn = x * r
    return xn, r, (xn * g) * sc1p + shift


W_TILE = 16


def _w_window(n):
    return max(-(-(n * (j + 1)) // W_TILE) * W_TILE - (n * j) // W_TILE * W_TILE for j in range(4))


def _load_w_padded(w_hbm, w_vm, sems):
    copies = [pltpu.make_async_copy(w_hbm.at[ref:ref + n], w_vm.at[pad:pad + n], sems.at[k])
              for k, (pad, ref, n) in enumerate(_UNPAD_ROWS)]
    for cp in copies:
        cp.start()
    w_vm[OFF_GA + GLA_RANK:, :] = jnp.zeros((D_PAD - OFF_GA - GLA_RANK, D_MODEL), w_vm.dtype)
    return copies


def _inproj_fwd(x2d, shift, sc1p, g_norm, w_t):
    s = x2d.shape[0]
    tm = min(1024, s)

    def body(x_ref, sh_ref, sc_ref, g_ref, w_hbm, o_ref, w_vm, sems):
        @pl.when(pl.program_id(0) == 0)
        def _():
            for cp in _load_w_padded(w_hbm, w_vm, sems):
                cp.wait()

        subs = _subtiles(tm)
        hs = [_modnorm(x_ref[sl, :], g_ref[...], sc_ref[...], sh_ref[...])[2].astype(BF) for sl in subs]
        for sl, h in zip(subs, hs):
            o_ref[sl, :] = _dot(h, w_vm[...], NT)

    vec = _full((1, D_MODEL))
    return pl.pallas_call(
        body, name="inproj_fwd", grid=(s // tm,),
        in_specs=[pl.BlockSpec((tm, D_MODEL), lambda i: (i, 0)), vec, vec, vec, pl.BlockSpec(memory_space=pl.ANY)],
        out_specs=pl.BlockSpec((tm, D_PAD), lambda i: (i, 0)),
        out_shape=jax.ShapeDtypeStruct((s, D_PAD), F32),
        scratch_shapes=[pltpu.VMEM((D_PAD, D_MODEL), BF), pltpu.SemaphoreType.DMA((len(_UNPAD_ROWS),))],
        compiler_params=_params(("arbitrary",)),
    )(x2d, shift, sc1p, g_norm, w_t)


def _split3(a):
    hi = a.astype(BF)
    r1 = a - hi.astype(F32)
    mid = r1.astype(BF)
    lo = (r1 - mid.astype(F32)).astype(BF)
    return hi, mid, lo


def _tri_matmul(tri, a):
    hi, mid, lo = _split3(a)
    return _dot(tri, hi) + _dot(tri, mid) + _dot(tri, lo)


def _chunks(tb):
    return [slice(c * GLA_CHUNK, (c + 1) * GLA_CHUNK) for c in range(tb // GLA_CHUNK)]


def _per_chunk_rows(rows, width):
    return jnp.concatenate([jnp.broadcast_to(r, (GLA_CHUNK, width)) for r in rows], axis=0)


def _gla_triangle(tb):
    row = lax.broadcasted_iota(jnp.int32, (tb, tb), 0)
    col = lax.broadcasted_iota(jnp.int32, (tb, tb), 1)
    return (((row // GLA_CHUNK) == (col // GLA_CHUNK)) & (col <= row)).astype(F32)


def _lane_mean(x, ones_b):
    hi = x.astype(BF)
    lo = (x - hi.astype(F32)).astype(BF)
    return (_dot(hi, ones_b) + _dot(lo, ones_b)) * (1.0 / LANES)


def _head(t, h, lo_h):
    blk = t[:, LANES * (h // 2):LANES * (h // 2 + 1)]
    return jnp.where(lo_h, blk, 0.0) if h % 2 == 0 else jnp.where(lo_h, 0.0, blk)


def _gla_block_common(qk, ga, wd, bd, tril_b):
    tb = qk.shape[0]
    q, k = qk[:, :256], qk[:, 256:]
    z = _dot(ga.astype(BF), wd) + bd
    la = (jnp.minimum(z, 0.0) - jnp.log(1.0 + jnp.exp(-jnp.abs(z)))) * (1.0 / GLA_TAU)
    b = _tri_matmul(tril_b, la)
    bls = [b[rs.stop - 1:rs.stop, :] for rs in _chunks(tb)]
    eq = jnp.exp(b)
    ek = jnp.exp(-b)
    f = jnp.exp(_per_chunk_rows(bls, 256) - b)
    return z, eq, ek, f, q * (eq * GLA_DK ** -0.5), k * ek, k * f, bls


def _gla_units(s, rows):
    sub = min(GLA_SUB, s)
    tb = min(rows, s)
    subs = [slice(i * sub, (i + 1) * sub) for i in range(tb // sub)]
    units = [(i, h) for i in range(len(subs)) for h in range(GLA_HEADS)]
    return tb, sub, subs, units


def _gla_fwd(proj, wdecp, bdec, ggla):
    s = proj.shape[0]
    tb, sub, subs, units = _gla_units(s, GLA_ROWS_FWD)
    nch = sub // GLA_CHUNK

    def body(qk_ref, v_ref, gz_ref, ga_ref, wd_ref, bd_ref, gg_ref, tri_ref, og_ref, opre_ref, sprev_ref, st_ref):
        @pl.when(pl.program_id(0) == 0)
        def _():
            st_ref[...] = jnp.zeros_like(st_ref)

        lo_h = lax.broadcasted_iota(jnp.int32, (sub, LANES), 1) < GLA_DK
        tril = tri_ref[...] > 0.5
        tril_b = tri_ref[...].astype(BF)
        ones_b = jnp.ones((LANES, LANES), BF)
        gg, wd, bd = gg_ref[...], wd_ref[...], bd_ref[...]
        chunks = _chunks(sub)
        lanes = [slice(h * LANES, (h + 1) * LANES) for h in range(GLA_HEADS)]
        com = [_gla_block_common(qk_ref[sl, :], ga_ref[sl, :], wd, bd, tril_b) for sl in subs]
        decs = [[jnp.exp(bl) for bl in cm[7]] for cm in com]
        a = {(i, h): _head(com[i][4], h, lo_h).astype(BF) for i, h in units}
        bm = {(i, h): _head(com[i][5], h, lo_h).astype(BF) for i, h in units}
        ktl = {(i, h): _head(com[i][6], h, lo_h).astype(BF) for i, h in units}
        vh = {(i, h): v_ref[subs[i], lanes[h]].astype(BF) for i, h in units}
        sc = {u: _dot(a[u], bm[u], NT) for u in units}
        upd = {u: [_dot(vh[u][rs], ktl[u][rs], TN) for rs in chunks] for u in units}
        p = {u: jnp.where(tril, sc[u], 0.0).astype(BF) for u in units}
        o = {u: _dot(p[u], vh[u]) for u in units}
        states = {}
        for h in range(GLA_HEADS):
            st = st_ref[h]
            for i in range(len(subs)):
                entering = []
                for c in range(nch):
                    entering.append(st)
                    sprev_ref[i * nch + c, h] = st
                    st = st * decs[i][c][:, LANES * (h // 2):LANES * (h // 2 + 1)] + upd[(i, h)][c]
                states[(i, h)] = entering
            st_ref[h] = st
        inter = {u: [_dot(a[u][rs], states[u][c].astype(BF), NT) for c, rs in enumerate(chunks)] for u in units}
        o = {u: o[u] + jnp.concatenate(inter[u], axis=0) for u in units}
        ms = {u: _lane_mean(o[u] * o[u], ones_b) for u in units}
        for i, h in units:
            gzh = gz_ref[subs[i], lanes[h]]
            opre_ref[subs[i], lanes[h]] = o[(i, h)]
            og_ref[subs[i], lanes[h]] = (((o[(i, h)] * lax.rsqrt(ms[(i, h)] + RMS_EPS)) * gg[:, lanes[h]])
                                         * (gzh * _sigmoid(gzh))).astype(og_ref.dtype)

    def col(width, off):
        return pl.BlockSpec((tb, width), lambda i: (i, off // width))

    return pl.pallas_call(
        body, name="gla_fwd", grid=(s // tb,),
        in_specs=[col(512, OFF_QK), col(512, OFF_V), col(512, OFF_GZ), col(LANES, OFF_GA),
                  _full((LANES, 256)), _full((1, 256)), _full((1, 512)), _full((sub, sub))],
        out_specs=[pl.BlockSpec((tb, 512), lambda i: (i, 0)), pl.BlockSpec((tb, 512), lambda i: (i, 0)),
                   pl.BlockSpec((tb // GLA_CHUNK, GLA_HEADS, LANES, LANES), lambda i: (i, 0, 0, 0))],
        out_shape=[jax.ShapeDtypeStruct((s, 512), BF), jax.ShapeDtypeStruct((s, 512), F32),
                   jax.ShapeDtypeStruct((s // GLA_CHUNK, GLA_HEADS, LANES, LANES), F32)],
        scratch_shapes=[pltpu.VMEM((GLA_HEADS, LANES, LANES), F32)],
        compiler_params=_params(("arbitrary",)),
    )(proj, proj, proj, proj, wdecp, bdec, ggla, _gla_triangle(sub))


def _gla_bwd(proj, dog, opre, sprev, wdecp, bdec, ggla):
    s = proj.shape[0]
    tb, sub, subs, units = _gla_units(s, GLA_ROWS_BWD)
    nsub = len(subs)
    nch = sub // GLA_CHUNK
    nb = s // tb

    def body(qk_ref, v_ref, gz_ref, ga_ref, dog_ref, opre_ref, sprev_ref, wd_ref, bd_ref, gg_ref, tri_ref, triu_ref,
             dqk_ref, dv_ref, dgz_ref, dga_ref, dwd_ref, dbd_ref, dgg_ref, dst_ref):
        @pl.when(pl.program_id(0) == 0)
        def _():
            dst_ref[...] = jnp.zeros_like(dst_ref)
            dwd_ref[...] = jnp.zeros_like(dwd_ref)
            dbd_ref[...] = jnp.zeros_like(dbd_ref)
            dgg_ref[...] = jnp.zeros_like(dgg_ref)

        lo_h = lax.broadcasted_iota(jnp.int32, (sub, LANES), 1) < GLA_DK
        tril = tri_ref[...] > 0.5
        tril_b = tri_ref[...].astype(BF)
        triu_b = triu_ref[...].astype(BF)
        ones_b = jnp.ones((LANES, LANES), BF)
        last_row = (lax.broadcasted_iota(jnp.int32, (sub, LANES), 0) % GLA_CHUNK) == GLA_CHUNK - 1
        wd, gg, bd = wd_ref[...], gg_ref[...], bd_ref[...]
        chunks = _chunks(sub)
        lanes = [slice(h * LANES, (h + 1) * LANES) for h in range(GLA_HEADS)]
        blks = [slice(LANES * (h // 2), LANES * (h // 2 + 1)) for h in range(GLA_HEADS)]
        ga = [ga_ref[sl, :] for sl in subs]
        com = [_gla_block_common(qk_ref[sl, :], ga[i], wd, bd, tril_b) for i, sl in enumerate(subs)]
        decs = [[jnp.exp(bl) for bl in cm[7]] for cm in com]
        a = {(i, h): _head(com[i][4], h, lo_h).astype(BF) for i, h in units}
        bm = {(i, h): _head(com[i][5], h, lo_h).astype(BF) for i, h in units}
        ktl = {(i, h): _head(com[i][6], h, lo_h).astype(BF) for i, h in units}
        vh = {(i, h): v_ref[subs[i], lanes[h]].astype(BF) for i, h in units}
        sc = {u: _dot(a[u], bm[u], NT) for u in units}

        o = {(i, h): opre_ref[subs[i], lanes[h]] for i, h in units}
        ms = {u: _lane_mean(o[u] * o[u], ones_b) for u in units}
        gz = {(i, h): gz_ref[subs[i], lanes[h]] for i, h in units}
        dog = {(i, h): dog_ref[subs[i], lanes[h]] for i, h in units}
        sg = {u: _sigmoid(gz[u]) for u in units}
        r = {u: lax.rsqrt(ms[u] + RMS_EPS) for u in units}
        ohat = {u: o[u] * r[u] for u in units}
        sil = {u: gz[u] * sg[u] for u in units}
        for i, h in units:
            u = (i, h)
            dgz_ref[subs[i], lanes[h]] = (dog[u] * (ohat[u] * gg[:, lanes[h]])
                                          * (sg[u] * (1.0 + gz[u] * (1.0 - sg[u])))).astype(dgz_ref.dtype)
            dgg_ref[:, lanes[h]] += jnp.sum(dog[u] * sil[u] * ohat[u], axis=0, keepdims=True)
        dn = {(i, h): dog[(i, h)] * sil[(i, h)] * gg[:, lanes[h]] for i, h in units}
        mdn = {u: _lane_mean(dn[u] * ohat[u], ones_b) for u in units}
        do = {u: (r[u] * (dn[u] - ohat[u] * mdn[u])).astype(BF) for u in units}

        p = {u: jnp.where(tril, sc[u], 0.0).astype(BF) for u in units}
        dpr = {u: _dot(do[u], vh[u], NT) for u in units}
        incr = {u: [_dot(do[u][rs], a[u][rs], TN) for rs in chunks] for u in units}
        dv = {u: _dot(p[u], do[u], TN) for u in units}
        dp = {u: jnp.where(tril, dpr[u], 0.0).astype(BF) for u in units}
        dqd = {u: _dot(dp[u], bm[u]) for u in units}
        dkd = {u: _dot(dp[u], a[u], TN) for u in units}
        st = {(i, h): [sprev_ref[i * nch + c, h] for c in range(nch)] for i, h in units}
        leaving = {}
        for h in range(GLA_HEADS):
            d = dst_ref[h]
            for i in reversed(range(nsub)):
                out = [None] * nch
                for c in reversed(range(nch)):
                    out[c] = d
                    d = d * decs[i][c][:, blks[h]] + incr[(i, h)][c]
                leaving[(i, h)] = out
            dst_ref[h] = d
        lv_b = {u: [leaving[u][c].astype(BF) for c in range(nch)] for u in units}
        dv_s = {u: [_dot(ktl[u][rs], lv_b[u][c], NT) for c, rs in enumerate(chunks)] for u in units}
        dqd_s = {u: [_dot(do[u][rs], st[u][c].astype(BF)) for c, rs in enumerate(chunks)] for u in units}
        dkt_s = {u: [_dot(vh[u][rs], lv_b[u][c]) for c, rs in enumerate(chunks)] for u in units}
        ddec = {u: [jnp.sum(leaving[u][c] * st[u][c], axis=0, keepdims=True) for c in range(nch)] for u in units}
        for i, h in units:
            dv_ref[subs[i], lanes[h]] = (dv[(i, h)] + jnp.concatenate(dv_s[(i, h)], axis=0)).astype(dv_ref.dtype)
        dqd = {u: dqd[u] + jnp.concatenate(dqd_s[u], axis=0) for u in units}
        dkt = {u: jnp.concatenate(dkt_s[u], axis=0) for u in units}

        db = []
        for i, sl in enumerate(subs):
            _, eq, ek, f, qd, kd, kt, _ = com[i]
            parts = []
            for pair in range(GLA_HEADS // 2):
                blk, u0, u1 = blks[2 * pair], (i, 2 * pair), (i, 2 * pair + 1)
                dqd_b, dkd_b, dkt_b = dqd[u0] + dqd[u1], dkd[u0] + dkd[u1], dkt[u0] + dkt[u1]
                dqk_ref[sl, blk] = (dqd_b * (eq[:, blk] * GLA_DK ** -0.5)).astype(dqk_ref.dtype)
                dqk_ref[sl, 256 + LANES * pair:256 + LANES * (pair + 1)] = (dkd_b * ek[:, blk] + dkt_b * f[:, blk]).astype(dqk_ref.dtype)
                dkt_kt = dkt_b * kt[:, blk]
                dbp = dqd_b * qd[:, blk] - dkd_b * kd[:, blk] - dkt_kt
                dbl = [jnp.sum(dkt_kt[rs], axis=0, keepdims=True) + (ddec[u0][c] + ddec[u1][c]) * decs[i][c][:, blk]
                       for c, rs in enumerate(chunks)]
                parts.append(jnp.where(last_row, dbp + _per_chunk_rows(dbl, LANES), dbp))
            db.append(jnp.concatenate(parts, axis=1))
        dla = [_tri_matmul(triu_b, db[i]) for i in range(nsub)]
        dz32 = [dla[i] * (1.0 / GLA_TAU) * _sigmoid(-com[i][0]) for i in range(nsub)]
        dz = [t.astype(BF) for t in dz32]
        for i, sl in enumerate(subs):
            dga_ref[sl, :] = _dot(dz[i], wd, NT).astype(dga_ref.dtype)
            dwd_ref[...] += _dot(ga[i].astype(BF), dz[i], TN)
            dbd_ref[...] += jnp.sum(dz32[i], axis=0, keepdims=True)

    def col(width, off):
        return pl.BlockSpec((tb, width), lambda i: (nb - 1 - i, off // width))

    def rev(width):
        return pl.BlockSpec((tb, width), lambda i: (nb - 1 - i, 0))

    return pl.pallas_call(
        body, name="gla_bwd", grid=(nb,),
        in_specs=[col(512, OFF_QK), col(512, OFF_V), col(512, OFF_GZ), col(LANES, OFF_GA), rev(512), rev(512),
                  pl.BlockSpec((tb // GLA_CHUNK, GLA_HEADS, LANES, LANES), lambda i: (nb - 1 - i, 0, 0, 0)),
                  _full((LANES, 256)), _full((1, 256)), _full((1, 512)), _full((sub, sub)), _full((sub, sub))],
        out_specs=[rev(512), rev(512), rev(512), rev(LANES), _full((LANES, 256)), _full((1, 256)), _full((1, 512))],
        out_shape=[jax.ShapeDtypeStruct((s, 512), BF), jax.ShapeDtypeStruct((s, 512), BF),
                   jax.ShapeDtypeStruct((s, 512), BF), jax.ShapeDtypeStruct((s, LANES), BF),
                   jax.ShapeDtypeStruct((LANES, 256), F32), jax.ShapeDtypeStruct((1, 256), F32),
                   jax.ShapeDtypeStruct((1, 512), F32)],
        scratch_shapes=[pltpu.VMEM((GLA_HEADS, LANES, LANES), F32)],
        compiler_params=_params(("arbitrary",)),
    )(proj, proj, proj, proj, dog, opre, sprev, wdecp, bdec, ggla, _gla_triangle(sub), _gla_triangle(sub).T)


_SWA_COL_HEADS = (0, 2, 1, 3, 4, 6, 5, 7)
_SWA_COLS = SWA_HEADS * SWA_BLOCK


def _swa_masks():
    lo2 = lax.broadcasted_iota(jnp.int32, (2 * SWA_BLOCK, LANES), 1) < 64
    lane1 = lax.broadcasted_iota(jnp.int32, (SWA_BLOCK, LANES), 1)
    first_half = (lane1 % 64) < 32
    key = lax.broadcasted_iota(jnp.int32, (SWA_BLOCK, _SWA_COLS), 0)
    query = lax.broadcasted_iota(jnp.int32, (SWA_BLOCK, _SWA_COLS), 1) % SWA_BLOCK
    return lo2, lane1 < 64, first_half, key > query


def _merge_band(t, prev_mask, prev_bias=None):
    prev = t[:SWA_BLOCK] if prev_bias is None else t[:SWA_BLOCK] + prev_bias
    return jnp.where(prev_mask, prev, t[SWA_BLOCK:])


def _split_band(t, prev_mask_b):
    prev = t * prev_mask_b
    return jnp.concatenate([prev, t - prev], axis=0)


def _kv_variants(t, lo2):
    tr = pltpu.roll(t, 64, 1)
    lo_v = [jnp.where(lo2, t, 0.0).astype(BF), jnp.where(lo2, tr, 0.0).astype(BF)]
    hi_v = [jnp.where(lo2, 0.0, tr).astype(BF), jnp.where(lo2, 0.0, t).astype(BF)]
    return lo_v, hi_v


def _kv_variants_t(t):
    tt = t.T
    sw = jnp.concatenate([tt[64:], tt[:64]], axis=0)
    top = lax.broadcasted_iota(jnp.int32, tt.shape, 0) < 64
    lo_v = [jnp.where(top, tt, 0.0).astype(BF), jnp.where(top, sw, 0.0).astype(BF)]
    hi_v = [jnp.where(top, 0.0, sw).astype(BF), jnp.where(top, 0.0, tt).astype(BF)]
    return lo_v, hi_v


def _swa_scores(qg, k_lo, k_hi):
    return jnp.concatenate([_dot(k_lo[0], qg[0], NT), _dot(k_hi[0], qg[0], NT),
                            _dot(k_lo[1], qg[1], NT), _dot(k_hi[1], qg[1], NT)], axis=1)


def _sink_row(sinks_ref):
    return jnp.concatenate([jnp.full((1, SWA_BLOCK), sinks_ref[0, hd], F32) for hd in _SWA_COL_HEADS], axis=1)


def _swa_softmax(st, prev_mask, prev_bias, sink):
    st = _merge_band(st, prev_mask, prev_bias)
    m = jnp.maximum(jnp.max(st, axis=0, keepdims=True), sink)
    ex = jnp.exp(st - m)
    es = jnp.exp(sink - m)
    inv = 1.0 / (jnp.sum(ex, axis=0, keepdims=True) + es)
    return ex, es, inv


def _no_prev_bias(block_index):
    return jnp.where(block_index > 0, 0.0, -1e30).astype(F32)


def _swa_queries(sq_ref, rows, cosb, sinb, first_half):
    qs = [_rope(sq_ref[rows, p * LANES:(p + 1) * LANES], cosb, sinb, first_half) * 0.125 for p in range(4)]
    return [jnp.concatenate(qs[0:2], axis=0), jnp.concatenate(qs[2:4], axis=0)]


def _phase_steps(nsteps, phases):
    return [min(nsteps - 1, (k * nsteps) // phases) for k in range(phases - 1)] + [nsteps - 1]


def _swa_fwd(proj, cos, sin, sinks, w_out):
    s = proj.shape[0]
    nq = min(SWA_QBLOCKS_FWD, s // SWA_BLOCK)
    tq = nq * SWA_BLOCK
    steps = _phase_steps(s // tq, 4)
    half_rows = w_out.shape[0] // 2

    def body(sq_ref, sz_ref, sk_ref, sv_ref, cos_ref, sin_ref, sinks_ref, wshard_ref, os_ref, opre_ref, wout_hbm,
             kprev, vprev, half_ref, *gather_sems):
        n = pl.program_id(0)

        @pl.when(n == 0)
        def _():
            kprev[...] = jnp.zeros_like(kprev)
            vprev[...] = jnp.zeros_like(vprev)
            mine = pl.ds(pl.multiple_of(lax.axis_index("c") * half_rows, half_rows), half_rows)
            half_ref[...] = wshard_ref[mine, :].astype(half_ref.dtype)

        gather = _Gather(half_ref, wout_hbm, *gather_sems, chunks=WEIGHT_CHUNKS)
        for step, phase in zip(steps, (gather.start, gather.pass_on, gather.relay_diagonal, gather.finish)):
            pl.when(n == step)(phase)

        lo2, _, first_half, prev_mask = _swa_masks()
        prev_mask_b = jnp.where(prev_mask, 1.0, 0.0).astype(BF)
        sink = _sink_row(sinks_ref)
        blocks = range(nq)
        rows = [slice(j * SWA_BLOCK, (j + 1) * SWA_BLOCK) for j in blocks]
        cosb = [cos_ref[rows[j], :] for j in blocks]
        sinb = [sin_ref[rows[j], :] for j in blocks]
        kc = [_rope(sk_ref[rows[j], :], cosb[j], sinb[j], first_half) for j in blocks]
        vc = [sv_ref[rows[j], :] for j in blocks]
        kcat = [jnp.concatenate([kprev[...] if j == 0 else kc[j - 1], kc[j]], axis=0) for j in blocks]
        vcat = [jnp.concatenate([vprev[...] if j == 0 else vc[j - 1], vc[j]], axis=0) for j in blocks]
        kprev[...] = kc[-1]
        vprev[...] = vc[-1]
        kvar = [_kv_variants(kcat[j], lo2) for j in blocks]
        vtvar = [_kv_variants_t(vcat[j]) for j in blocks]
        qg = [[q.astype(BF) for q in _swa_queries(sq_ref, rows[j], cosb[j], sinb[j], first_half)] for j in blocks]
        st = [_swa_scores(qg[j], *kvar[j]) for j in blocks]
        soft = [_swa_softmax(st[j], prev_mask, _no_prev_bias(n) if j == 0 else None, sink) for j in blocks]
        pt = [_split_band(soft[j][0].astype(BF), prev_mask_b) for j in blocks]
        og = {}
        for j in blocks:
            inv = soft[j][2]
            for g in range(2):
                c0, c1, c2 = 512 * g, 512 * g + 256, 512 * g + 512
                ot = (_dot(vtvar[j][0][g], pt[j][:, c0:c1]) * inv[:, c0:c1]
                      + _dot(vtvar[j][1][g], pt[j][:, c1:c2]) * inv[:, c1:c2])
                og[(j, g)] = ot.T
        for j in blocks:
            for g in range(2):
                for i in range(2):
                    ls = slice((2 * g + i) * LANES, (2 * g + i + 1) * LANES)
                    o = og[(j, g)][i * SWA_BLOCK:(i + 1) * SWA_BLOCK]
                    sz = sz_ref[rows[j], ls]
                    opre_ref[rows[j], ls] = o
                    os_ref[rows[j], ls] = (o * (sz * _sigmoid(sz))).astype(os_ref.dtype)

    def col(width, off):
        return pl.BlockSpec((tq, width), lambda i: (i, off // width))

    row = pl.BlockSpec((tq, LANES), lambda i: (i, 0))
    return pl.pallas_call(
        body, name="swa_fwd", grid=(s // tq,),
        in_specs=[col(512, OFF_SQ), col(512, OFF_SZ), col(LANES, OFF_SK), col(LANES, OFF_SV), row, row,
                  pl.BlockSpec(memory_space=pltpu.SMEM), _full(w_out.shape)],
        out_specs=[pl.BlockSpec((tq, 512), lambda i: (i, 0))] * 2 + [pl.BlockSpec(memory_space=pl.ANY)],
        out_shape=[jax.ShapeDtypeStruct((s, 512), BF), jax.ShapeDtypeStruct((s, 512), F32),
                   jax.ShapeDtypeStruct((8, half_rows, w_out.shape[1]), BF)],
        scratch_shapes=[pltpu.VMEM((SWA_BLOCK, LANES), F32)] * 2 + [pltpu.VMEM((half_rows, w_out.shape[1]), BF)]
        + _gather_sems(WEIGHT_CHUNKS),
        compiler_params=_params(("arbitrary",)),
    )(proj, proj, proj, proj, cos, sin, sinks, w_out)


def _swa_bwd(proj, dos, opre, cos, sin, sinks, dw_out_parts):
    s = proj.shape[0]
    nq = min(SWA_QBLOCKS, s // SWA_BLOCK)
    tq = nq * SWA_BLOCK
    steps = _phase_steps(s // tq, 5)
    _, r_out, c_out = dw_out_parts.shape

    def body(sq_ref, sz_ref, sk_ref, sv_ref, dos_ref, opre_ref, cos_ref, sin_ref, sinks_ref, pout_hbm,
             dsq_ref, dsz_ref, dsk_ref, dsv_ref, dsink_ref, gout_hbm, kprev, vprev, cprev, sprev, *reduce_scratch):
        n = pl.program_id(0)

        @pl.when(n == 0)
        def _():
            kprev[...] = jnp.zeros_like(kprev)
            vprev[...] = jnp.zeros_like(vprev)
            cprev[...] = jnp.zeros_like(cprev)
            sprev[...] = jnp.zeros_like(sprev)
            for hd in range(SWA_HEADS):
                dsink_ref[0, hd] = 0.0

        reduce = _Reduce(pout_hbm, gout_hbm, *reduce_scratch)
        phases = (reduce.start, reduce.combine_and_send, reduce.send_joint, reduce.total_and_share, reduce.finish)
        for step, phase in zip(steps, phases):
            pl.when(n == step)(phase)

        lo2, lo1, first_half, prev_mask = _swa_masks()
        prev_mask_b = jnp.where(prev_mask, 1.0, 0.0).astype(BF)
        lo1s = jnp.concatenate([lo1, lo1], axis=0)
        sink = _sink_row(sinks_ref)

        def home(m0, m1):
            t0 = m0 + pltpu.roll(m0, 64, 1)
            t1 = m1 + pltpu.roll(m1, 64, 1)
            return jnp.where(lo2, t0, t1)

        kp, vp, cp_, sp_ = kprev[...], vprev[...], cprev[...], sprev[...]
        for j in range(nq):
            rows = slice(j * SWA_BLOCK, (j + 1) * SWA_BLOCK)
            blk = n * nq + j
            cosb, sinb = cos_ref[rows, :], sin_ref[rows, :]
            kc = _rope(sk_ref[rows, :], cosb, sinb, first_half)
            vc = sv_ref[rows, :]
            kcat = jnp.concatenate([kp, kc], axis=0)
            k_lo, k_hi = _kv_variants(kcat, lo2)
            kt_lo, kt_hi = _kv_variants_t(kcat)
            v_lo, v_hi = _kv_variants(jnp.concatenate([vp, vc], axis=0), lo2)
            qg32 = _swa_queries(sq_ref, rows, cosb, sinb, first_half)
            qg = [q.astype(BF) for q in qg32]
            ex, es, inv = _swa_softmax(_swa_scores(qg, k_lo, k_hi), prev_mask, _no_prev_bias(n) if j == 0 else None, sink)
            pr, ps = ex * inv, es * inv

            dog32 = []
            for g in range(2):
                parts = []
                for i in range(2):
                    ls = slice((2 * g + i) * LANES, (2 * g + i + 1) * LANES)
                    sz = sz_ref[rows, ls]
                    sg = _sigmoid(sz)
                    dos_p = dos_ref[rows, ls]
                    dsz_ref[rows, ls] = (dos_p * opre_ref[rows, ls] * (sg * (1.0 + sz * (1.0 - sg)))).astype(dsz_ref.dtype)
                    parts.append(dos_p * (sz * sg))
                dog32.append(jnp.concatenate(parts, axis=0))
            dog = [t.astype(BF) for t in dog32]
            dpr = _merge_band(jnp.concatenate([_dot(v_lo[0], dog[0], NT), _dot(v_hi[0], dog[0], NT),
                                               _dot(v_lo[1], dog[1], NT), _dot(v_hi[1], dog[1], NT)], axis=1), prev_mask)
            rd = jnp.sum(pr * dpr, axis=0, keepdims=True)
            ds = _split_band((pr * (dpr - rd)).astype(BF), prev_mask_b)
            prb = _split_band(pr.astype(BF), prev_mask_b)
            sink_term = ps * rd
            for r, hd in enumerate(_SWA_COL_HEADS):
                dsink_ref[0, hd] += -jnp.sum(sink_term[:, r * SWA_BLOCK:(r + 1) * SWA_BLOCK])

            dk_g, dv_g = [], []
            for g in range(2):
                c0, c1, c2 = 512 * g, 512 * g + 256, 512 * g + 512
                dq = (_dot(kt_lo[g], ds[:, c0:c1]) + _dot(kt_hi[g], ds[:, c1:c2])).T
                for i in range(2):
                    ls = slice((2 * g + i) * LANES, (2 * g + i + 1) * LANES)
                    dsq_ref[rows, ls] = _rope_t(dq[i * SWA_BLOCK:(i + 1) * SWA_BLOCK] * 0.125, cosb, sinb,
                                                first_half).astype(dsq_ref.dtype)
                q_split = jnp.concatenate([jnp.where(lo1s, qg32[g], 0.0), jnp.where(lo1s, 0.0, qg32[g])], axis=0).astype(BF)
                do_split = jnp.concatenate([jnp.where(lo1s, dog32[g], 0.0), jnp.where(lo1s, 0.0, dog32[g])], axis=0).astype(BF)
                dk_g.append(_dot(ds[:, c0:c2], q_split))
                dv_g.append(_dot(prb[:, c0:c2], do_split))
            dk = home(dk_g[0], dk_g[1])
            dv = home(dv_g[0], dv_g[1])
            cur = pl.ds(pl.multiple_of(blk * SWA_BLOCK, SWA_BLOCK), SWA_BLOCK)
            dsk_ref[cur, :] = _rope_t(dk[SWA_BLOCK:], cosb, sinb, first_half)
            dsv_ref[cur, :] = dv[SWA_BLOCK:]
            dk_prev = _rope_t(dk[:SWA_BLOCK], cp_, sp_, first_half)
            dv_prev = dv[:SWA_BLOCK]
            if j == 0:
                @pl.when(n > 0)
                def _():
                    prv = pl.ds(pl.multiple_of((blk - 1) * SWA_BLOCK, SWA_BLOCK), SWA_BLOCK)
                    dsk_ref[prv, :] += dk_prev
                    dsv_ref[prv, :] += dv_prev
            else:
                prv = pl.ds(pl.multiple_of((blk - 1) * SWA_BLOCK, SWA_BLOCK), SWA_BLOCK)
                dsk_ref[prv, :] += dk_prev
                dsv_ref[prv, :] += dv_prev
            kp, vp, cp_, sp_ = kc, vc, cosb, sinb
        kprev[...] = kp
        vprev[...] = vp
        cprev[...] = cp_
        sprev[...] = sp_

    def col(width, off):
        return pl.BlockSpec((tq, width), lambda i: (i, off // width))

    row = pl.BlockSpec((tq, LANES), lambda i: (i, 0))
    wide = pl.BlockSpec((tq, 512), lambda i: (i, 0))
    return pl.pallas_call(
        body, name="swa_bwd", grid=(s // tq,),
        in_specs=[col(512, OFF_SQ), col(512, OFF_SZ), col(LANES, OFF_SK), col(LANES, OFF_SV), wide, wide, row, row,
                  pl.BlockSpec(memory_space=pltpu.SMEM), pl.BlockSpec(memory_space=pl.ANY)],
        out_specs=[wide, wide, _full((s, LANES)), _full((s, LANES)), pl.BlockSpec(memory_space=pltpu.SMEM),
                   pl.BlockSpec(memory_space=pl.ANY)],
        out_shape=[jax.ShapeDtypeStruct((s, 512), BF), jax.ShapeDtypeStruct((s, 512), BF),
                   jax.ShapeDtypeStruct((s, LANES), F32), jax.ShapeDtypeStruct((s, LANES), F32),
                   jax.ShapeDtypeStruct((1, SWA_HEADS), F32), jax.ShapeDtypeStruct((r_out, c_out), F32)],
        scratch_shapes=[pltpu.VMEM((SWA_BLOCK, LANES), F32)] * 4 + _reduce_scratch(r_out, c_out),
        compiler_params=_params(("arbitrary",)),
    )(proj, proj, proj, proj, dos, opre, cos, sin, sinks, dw_out_parts)


def _outproj(og, osw, w_out, x2d, target, gate, g_final):
    s = x2d.shape[0]
    tm = min(512, s)

    def body(og_ref, os_ref, w_ref, x_ref, t_ref, gate_ref, gf_ref,
             dx2_ref, dog_ref, dos_ref, dw_ref, loss_ref, dgf_ref, dgate_ref):
        @pl.when(pl.program_id(0) == 0)
        def _():
            dw_ref[...] = jnp.zeros_like(dw_ref)
            loss_ref[...] = jnp.zeros_like(loss_ref)
            dgf_ref[...] = jnp.zeros_like(dgf_ref)
            dgate_ref[...] = jnp.zeros_like(dgate_ref)

        w = w_ref[...]
        gate, gf = gate_ref[...], gf_ref[...]
        subs = _subtiles(tm)
        ogv = [og_ref[sl, :] for sl in subs]
        osv = [os_ref[sl, :] for sl in subs]
        y = [_dot(ogv[k], w[:512]) + _dot(osv[k], w[512:]) for k in range(len(subs))]
        dys = []
        for k, sl in enumerate(subs):
            x2 = x_ref[sl, :] + gate * y[k]
            r = lax.rsqrt(jnp.mean(x2 * x2, axis=-1, keepdims=True) + RMS_EPS)
            xn = x2 * r
            err = xn * gf - t_ref[sl, :]
            loss_ref[...] += 0.5 * jnp.sum(jnp.mean(err * err, axis=-1, keepdims=True), axis=0, keepdims=True)
            dyf = err * (1.0 / D_MODEL)
            dgf_ref[...] += jnp.sum(dyf * xn, axis=0, keepdims=True)
            t = dyf * gf
            dx2 = r * (t - xn * jnp.mean(t * xn, axis=-1, keepdims=True))
            dx2_ref[sl, :] = dx2
            dgate_ref[...] += jnp.sum(dx2 * y[k], axis=0, keepdims=True)
            dys.append((dx2 * gate).astype(BF))
            dmix = _dot(dys[k], w, NT)
            dog_ref[sl, :] = dmix[:, :512]
            dos_ref[sl, :] = dmix[:, 512:]
        dy = jnp.concatenate(dys, axis=0)
        dw_ref[:512, :] += _dot(og_ref[...], dy, TN)
        dw_ref[512:, :] += _dot(os_ref[...], dy, TN)

    half = pl.BlockSpec((tm, 512), lambda i: (i, 0))
    rowb = pl.BlockSpec((tm, D_MODEL), lambda i: (i, 0))
    vec = _full((1, D_MODEL))
    return pl.pallas_call(
        body, name="outproj", grid=(s // tm,),
        in_specs=[half, half, _full((D_MODEL, D_MODEL)), rowb, rowb, vec, vec],
        out_specs=[rowb, half, half, _full((D_MODEL, D_MODEL)), _full((1, 1)), vec, vec],
        out_shape=[jax.ShapeDtypeStruct((s, D_MODEL), F32), jax.ShapeDtypeStruct((s, 512), F32),
                   jax.ShapeDtypeStruct((s, 512), F32), jax.ShapeDtypeStruct((D_MODEL, D_MODEL), F32),
                   jax.ShapeDtypeStruct((1, 1), F32), jax.ShapeDtypeStruct((1, D_MODEL), F32),
                   jax.ShapeDtypeStruct((1, D_MODEL), F32)],
        compiler_params=_params(("arbitrary",)),
    )(og, osw, w_out, x2d, target, gate, g_final)


_PIECES = ((OFF_QK, 512), (OFF_V, 512), (OFF_GZ, 512), (OFF_SQ, 512), (OFF_SZ, 512),
           (OFF_SK, LANES), (OFF_SV, LANES), (OFF_GA, LANES))

_UNPAD_ROWS = ((OFF_QK, 0, 1024),
               (OFF_GA, 1024, GLA_RANK),
               (OFF_GZ, 1040, 1024),
               (OFF_SK, 2064, 256),
               (OFF_SZ, 2320, 512))


def _inproj_bwd(x2d, shift, sc1p, g_norm, w_t, dx2, pieces):
    s = x2d.shape[0]
    tm = min(512, s)
    nsteps = s // tm

    def body(x_ref, sh_ref, sc_ref, g_ref, w_hbm, dx2_ref, *rest):
        piece_refs = rest[:len(_PIECES)]
        gx_ref, dw_hbm, dsh_ref, dsc_ref, dg_ref, w_vm, dw_vm, in_sems, out_sems = rest[len(_PIECES):]
        i = pl.program_id(0)

        @pl.when(i == 0)
        def _():
            loads = _load_w_padded(w_hbm, w_vm, in_sems)
            dw_vm[...] = jnp.zeros_like(dw_vm)
            dsh_ref[...] = jnp.zeros_like(dsh_ref)
            dsc_ref[...] = jnp.zeros_like(dsc_ref)
            dg_ref[...] = jnp.zeros_like(dg_ref)
            for cp in loads:
                cp.wait()

        g, sc1p_v, shift_v = g_ref[...], sc_ref[...], sh_ref[...]
        subs = _subtiles(tm)
        dhs = []
        for sl in subs:
            dh = None
            for (off, width), pr in zip(_PIECES, piece_refs):
                part = _dot(pr[sl, :].astype(BF), w_vm[off:off + width, :])
                dh = part if dh is None else dh + part
            dhs.append(dh)
        norm = [_modnorm(x_ref[sl, :], g, sc1p_v, shift_v) for sl in subs]
        hb = jnp.concatenate([h.astype(BF) for _, _, h in norm], axis=0)
        for (off, width), pr in zip(_PIECES, piece_refs):
            dw_vm[off:off + width, :] += _dot(pr[...].astype(BF), hb, TN)
        for sl, (xn, r, _), dh in zip(subs, norm, dhs):
            dsh_ref[...] += jnp.sum(dh, axis=0, keepdims=True)
            dsc_ref[...] += jnp.sum(dh * (xn * g), axis=0, keepdims=True)
            dg_ref[...] += jnp.sum(dh * xn * sc1p_v, axis=0, keepdims=True)
            dxn = dh * g * sc1p_v
            gx_ref[sl, :] = dx2_ref[sl, :] + r * (dxn - xn * jnp.mean(dxn * xn, axis=-1, keepdims=True))

        @pl.when(i == nsteps - 1)
        def _():
            copies = [pltpu.make_async_copy(dw_vm.at[src:src + n], dw_hbm.at[dst:dst + n], out_sems.at[k])
                      for k, (src, dst, n) in enumerate(_UNPAD_ROWS)]
            for cp in copies:
                cp.start()
            for cp in copies:
                cp.wait()

    rowb = pl.BlockSpec((tm, D_MODEL), lambda i: (i, 0))
    vec = _full((1, D_MODEL))
    anyspec = pl.BlockSpec(memory_space=pl.ANY)
    piece_specs = [pl.BlockSpec((tm, width), lambda i: (i, 0)) for _, width in _PIECES]
    return pl.pallas_call(
        body, name="inproj_bwd", grid=(nsteps,),
        in_specs=[rowb, vec, vec, vec, anyspec, rowb] + piece_specs,
        out_specs=[rowb, anyspec, vec, vec, vec],
        out_shape=[jax.ShapeDtypeStruct((s, D_MODEL), F32), jax.ShapeDtypeStruct((D_IN, D_MODEL), F32),
                   jax.ShapeDtypeStruct((1, D_MODEL), F32), jax.ShapeDtypeStruct((1, D_MODEL), F32),
                   jax.ShapeDtypeStruct((1, D_MODEL), F32)],
        scratch_shapes=[pltpu.VMEM((D_PAD, D_MODEL), BF), pltpu.VMEM((D_PAD, D_MODEL), F32),
                        pltpu.SemaphoreType.DMA((len(_UNPAD_ROWS),)), pltpu.SemaphoreType.DMA((len(_UNPAD_ROWS),))],
        compiler_params=_params(("arbitrary",)),
    )(x2d, shift, sc1p, g_norm, w_t, dx2, *pieces)


def _adam(w, g, m, v):
    m2 = ADAM_B1 * m + (1.0 - ADAM_B1) * g
    v2 = ADAM_B2 * v + (1.0 - ADAM_B2) * (g * g)
    m_hat = m2 / (1.0 - ADAM_B1 ** ADAM_STEP)
    v_hat = v2 / (1.0 - ADAM_B2 ** ADAM_STEP)
    delta = -ADAM_LR * (m_hat / (jnp.sqrt(v_hat) + ADAM_EPS) + ADAM_WD * w)
    return delta, m2, v2


def _adamw_t(w3, g_window, m3, v3, name):
    rr, _, cc = w3.shape
    parts = [slice(q * (cc // 4), (q + 1) * (cc // 4)) for q in range(4)]
    starts = sorted({(rr * j) % 8 for j in range(4)})

    def body(w_hbm, gw_hbm, m_hbm, v_hbm, d_hbm, m2_hbm, v2_hbm, g3_hbm,
             w_vm, m_vm, v_vm, gw_vm, d_vm, m2_vm, v2_vm, g_vm, in_sems, out_sems):
        start = lax.rem(rr * (2 * lax.axis_index("x") + lax.axis_index("y")), 8)
        ins = ((w_hbm, w_vm), (m_hbm, m_vm), (v_hbm, v_vm))
        outs = ((d_vm, d_hbm), (m2_vm, m2_hbm), (v2_vm, v2_hbm), (g_vm, g3_hbm))
        loads = [[pltpu.make_async_copy(src.at[:, 0, p], dst.at[:, p], in_sems.at[4 * q + k]) for k, (src, dst) in enumerate(ins)]
                 + [pltpu.make_async_copy(gw_hbm.at[:, p], gw_vm.at[:, p], in_sems.at[4 * q + 3])]
                 for q, p in enumerate(parts)]
        stores = [[pltpu.make_async_copy(src.at[:, p], dst.at[:, 0, p], out_sems.at[4 * q + k]) for k, (src, dst) in enumerate(outs)]
                  for q, p in enumerate(parts)]
        for group in loads:
            for cp in group:
                cp.start()
        for q, p in enumerate(parts):
            for cp in loads[q]:
                cp.wait()
            g = gw_vm[starts[0]:starts[0] + rr, p]
            for o in starts[1:]:
                g = jnp.where(start == o, gw_vm[o:o + rr, p], g)
            g_vm[:, p] = g
            d_vm[:, p], m2_vm[:, p], v2_vm[:, p] = _adam(w_vm[:, p], g, m_vm[:, p], v_vm[:, p])
            for cp in stores[q]:
                cp.start()
        for group in stores:
            for cp in group:
                cp.wait()

    hbm = pl.BlockSpec(memory_space=pl.ANY)
    return pl.pallas_call(
        body, name=name, grid=(1,), in_specs=[hbm] * 4,
        out_specs=[hbm] * 4, out_shape=[jax.ShapeDtypeStruct((rr, 1, cc), F32)] * 4,
        scratch_shapes=[pltpu.VMEM((rr, cc), F32)] * 3 + [pltpu.VMEM(g_window.shape, F32)] + [pltpu.VMEM((rr, cc), F32)] * 4
        + [pltpu.SemaphoreType.DMA((16,)), pltpu.SemaphoreType.DMA((16,))],
        compiler_params=_params(("arbitrary",)),
    )(w3, g_window, m3, v3)


def _small_update(parts, weights, moms, vels):
    n = len(weights)

    def body(*refs):
        p_refs, w_refs, m_refs, v_refs = refs[:n + 1], refs[n + 1:2 * n + 1], refs[2 * n + 1:3 * n + 1], refs[3 * n + 1:4 * n + 1]
        outs = refs[4 * n + 1:]
        for i in range(n):
            g = p_refs[i][0]
            for d in range(1, 8):
                g = g + p_refs[i][d]
            delta, m2, v2 = _adam(w_refs[i][...], g, m_refs[i][...], v_refs[i][...])
            outs[4 * i][...] = g
            outs[4 * i + 1][...] = delta
            outs[4 * i + 2][...] = m2
            outs[4 * i + 3][...] = v2
        tot = p_refs[n][0]
        for d in range(1, 8):
            tot = tot + p_refs[n][d]
        outs[4 * n][...] = tot

    out_shape = []
    for w in weights:
        out_shape += [jax.ShapeDtypeStruct(w.shape, F32)] * 4
    out_shape.append(jax.ShapeDtypeStruct(parts[n].shape[1:], F32))
    return pl.pallas_call(body, name="small_update", out_shape=out_shape, compiler_params=_params())(
        *parts, *weights, *moms, *vels)


def _rows8(a):
    flat = a.reshape(-1)
    rows = -(-flat.shape[0] // LANES)
    rows8 = -(-rows // 8) * 8
    flat = jnp.pad(flat, (0, rows8 * LANES - flat.shape[0]))
    return flat.reshape(rows8, LANES)


def kernel(x, c, positions, w_ada, b_ada, g_norm, w_in, w_decay, b_decay, g_gla_head, sinks, w_out, g_final, loss_target, m_w_ada, m_b_ada, m_g_norm, m_w_in, m_w_decay, m_b_decay, m_g_gla_head, m_sinks, m_w_out, m_g_final, v_w_ada, v_b_ada, v_g_norm, v_w_in, v_w_decay, v_b_decay, v_g_gla_head, v_sinks, v_w_out, v_g_final):
    ax, ay, ac = lax.axis_index("x"), lax.axis_index("y"), lax.axis_index("c")
    chip = 2 * ax + ay
    dev = 2 * chip + ac
    s = x.shape[1]
    x2d = x[0]
    target = loss_target[0]
    w_ada2, w_out2, w_dec2 = w_ada[0], w_out[0], w_decay[0]
    w_in_t = w_in[0].T
    ada_cols = w_ada2.shape[1]
    in_cols = w_in_t.shape[0]
    out_rows = w_out2.shape[0]
    half = D_MODEL // 2

    cw = jnp.concatenate([c.reshape(8, LANES), w_dec2.reshape(8, LANES)], axis=0)
    inv_freq = 1.0 / (ROPE_THETA ** (jnp.arange(0, 64, 2, dtype=F32) / 64))
    room = _w_window(in_cols) - in_cols
    win_window = lax.dynamic_slice(jnp.pad(w_in_t, ((room, room), (0, 0))), (room - (in_cols * chip) % W_TILE, ac * half),
                                   (_w_window(in_cols), half)).astype(BF)
    win_edges = jnp.stack([win_window[:W_TILE], win_window[-W_TILE:]])
    first, mod_all, w_t, cos, sin = _prologue(
        cw, w_ada2, b_ada.reshape(4, ada_cols), win_window, win_edges, in_cols, positions.reshape(s // LANES, LANES), jnp.tile(inv_freq, 4).reshape(1, LANES))

    first = first.reshape(8, 2, 8, LANES)
    c_all = first[:, 0].reshape(8, D_MODEL)
    w_dec_full = first[0::2, 1].reshape(4, GLA_RANK, 64).transpose(1, 0, 2).reshape(GLA_RANK, 256)
    mod = mod_all.reshape(4, 2, 8, ada_cols)[:, 0]
    mod = lax.dynamic_slice(mod, (0, dev, 0), (4, 1, ada_cols)).reshape(1, 4 * ada_cols)
    shift, sc1p, gate = mod[:, :D_MODEL], 1.0 + mod[:, D_MODEL:2 * D_MODEL], mod[:, 2 * D_MODEL:]
    wdecp = jnp.pad(w_dec_full, ((0, LANES - GLA_RANK), (0, 0))).astype(BF)

    proj = _inproj_fwd(x2d, shift, sc1p, g_norm, w_t)
    og, o_gla, sprev = _gla_fwd(proj, wdecp, b_decay, g_gla_head)
    osw, o_swa, w_out_all = _swa_fwd(proj, cos, sin, sinks, w_out2)
    w_out_all = w_out_all.reshape(D_MODEL, D_MODEL)
    dx2, dog, dos, dw_out, loss_p, dgf, dgate = _outproj(og, osw, w_out_all, x2d, target, gate, g_final.reshape(1, D_MODEL))
    dsq, dsz, dsk, dsv, dsinks, g_w_out = _swa_bwd(proj, dos, o_swa, cos, sin, sinks, dw_out.reshape(4, out_rows, D_MODEL))
    dqk, dv, dgz, dga, dwdp, dbd, dgg = _gla_bwd(proj, dog, o_gla, sprev, wdecp, b_decay, g_gla_head)
    pieces = (dqk, dv, dgz, dsq, dsz, dsk, dsv, dga)
    gx, dw_in_t, dshift, dscale, dgn = _inproj_bwd(x2d, shift, sc1p, g_norm, w_t, dx2, pieces)

    segs = [jnp.concatenate([dshift, dscale, dgate], axis=1), dgn, dgf, dwdp[:GLA_RANK], dbd, dgg, dsinks, loss_p]
    packed = [_rows8(a) for a in segs]
    offs = [0]
    for a in packed:
        offs.append(offs[-1] + a.shape[0])
    (g_window, small, g_w_ada, d_w_ada, nm_w_ada, nv_w_ada, d_w_out, nm_w_out, nv_w_out) = _epilogue(
        dw_in_t, jnp.concatenate(packed, axis=0), c_all, (w_ada2, m_w_ada[0], v_w_ada[0]),
        (w_out2, g_w_out, m_w_out[0], v_w_out[0]), offs[0])

    def seg(i, size):
        return small[:, offs[i]:offs[i + 1]].reshape(8, -1)[:, :size]

    dmod_all = seg(0, 3 * D_MODEL)
    dwd_all = lax.dynamic_slice(seg(3, GLA_RANK * 256).reshape(8, GLA_RANK, 256), (0, 0, chip * 64), (8, GLA_RANK, 64))
    parts = [dmod_all.reshape(8, 1, 3 * D_MODEL), seg(1, D_MODEL).reshape(8, 1, D_MODEL), dwd_all,
             seg(4, 256).reshape(8, 1, 256), seg(5, 512).reshape(8, 1, 512), seg(6, SWA_HEADS).reshape(8, 1, SWA_HEADS),
             seg(2, D_MODEL).reshape(8, 1, D_MODEL), seg(7, LANES).reshape(8, 1, LANES)]
    smalls = _small_update(
        parts,
        [b_ada, g_norm, w_dec2, b_decay, g_gla_head, sinks, g_final.reshape(1, D_MODEL)],
        [m_b_ada, m_g_norm, m_w_decay[0], m_b_decay, m_g_gla_head, m_sinks, m_g_final.reshape(1, D_MODEL)],
        [v_b_ada, v_g_norm, v_w_decay[0], v_b_decay, v_g_gla_head, v_sinks, v_g_final.reshape(1, D_MODEL)])
    (g_b_ada, d_b_ada, nm_b_ada, nv_b_ada, g_gn, d_gn, nm_gn, nv_gn, g_wd, d_wd, nm_wd, nv_wd,
     g_bd, d_bd, nm_bd, nv_bd, g_gg, d_gg, nm_gg, nv_gg, g_sk, d_sk, nm_sk, nv_sk,
     g_gf, d_gf, nm_gf, nv_gf, loss_row) = smalls
    loss = loss_row[0, 0]

    to3 = lambda a: jnp.transpose(a, (2, 0, 1))
    from3 = lambda a: jnp.transpose(a, (1, 2, 0))[0]
    d3, nm3, nv3, g3 = _adamw_t(to3(w_in), g_window, to3(m_w_in), to3(v_w_in), "adamw_w_in")
    g_w_in, d_w_in, nm_w_in, nv_w_in = from3(g3), from3(d3), from3(nm3), from3(nv3)

    flat = lambda a: a.reshape(D_MODEL)
    grads = [g_w_ada[None], g_b_ada, g_gn, g_w_in[None], g_wd[None], g_bd, g_gg, g_sk, g_w_out[None], flat(g_gf)]
    deltas = [d_w_ada[None], d_b_ada, d_gn, d_w_in[None], d_wd[None], d_bd, d_gg, d_sk, d_w_out[None], flat(d_gf)]
    new_m = [nm_w_ada[None], nm_b_ada, nm_gn, nm_w_in[None], nm_wd[None], nm_bd, nm_gg, nm_sk, nm_w_out[None], flat(nm_gf)]
    new_v = [nv_w_ada[None], nv_b_ada, nv_gn, nv_w_in[None], nv_wd[None], nv_bd, nv_gg, nv_sk, nv_w_out[None], flat(nv_gf)]
    return (loss, gx[None], *grads, *deltas, *new_m, *new_v)
```

```python
import jax
import jax.numpy as jnp
from jax import lax
from jax.experimental import pallas as pl
from jax.experimental.pallas import tpu as pltpu

F32 = jnp.float32
BF = jnp.bfloat16

D_MODEL = 1024
GLA_HEADS = 4
GLA_DK = 64
GLA_CHUNK = 64
GLA_RANK = 16
GLA_TAU = 16.0
GLA_SUB = 256
GLA_ROWS_FWD = 1024
GLA_ROWS_BWD = 512
SWA_HEADS = 8
SWA_BLOCK = 128
SWA_QBLOCKS_FWD = 8
SWA_QBLOCKS = 8
RMS_EPS = 1e-6
ROPE_THETA = 10000.0

OFF_QK, OFF_V, OFF_GZ, OFF_SQ, OFF_SZ, OFF_SK, OFF_SV, OFF_GA = 0, 512, 1024, 1536, 2048, 2560, 2688, 2816
D_PAD = 2944
D_IN = 2832
LANES = 128
VMEM_LIMIT = 56 * 1024 * 1024

ADAM_LR, ADAM_B1, ADAM_B2, ADAM_EPS, ADAM_WD, ADAM_STEP = 0.001, 0.9, 0.999, 1e-08, 0.01, 10

NT = (((1,), (1,)), ((), ()))
TN = (((0,), (0,)), ((), ()))
MESH = pl.DeviceIdType.MESH


def _dot(a, b, dims=None):
    if dims is None:
        return jnp.dot(a, b, preferred_element_type=F32)
    return lax.dot_general(a, b, dims, preferred_element_type=F32)


def _sigmoid(x):
    return 1.0 / (1.0 + jnp.exp(-x))


def _params(sem=None):
    return pltpu.CompilerParams(dimension_semantics=sem, vmem_limit_bytes=VMEM_LIMIT)


def _full(shape):
    return pl.BlockSpec(shape, lambda i: (0,) * len(shape))


def _subtiles(rows, size=256):
    size = min(size, rows)
    return [slice(k * size, (k + 1) * size) for k in range(rows // size)]


WEIGHT_CHUNKS = 4


def _gather_sems(chunks=1):
    return [pltpu.SemaphoreType.DMA((7 * chunks,)), pltpu.SemaphoreType.DMA((7 * chunks,)), pltpu.SemaphoreType.DMA]


_GATHER_SEMS = _gather_sems()


class _Gather:
    def __init__(self, x_ref, out_ref, send_sems, recv_sems, local_sem, slab=None, chunks=1):
        self.slab_of = slab
        self.chunks = chunks
        self.width = x_ref.shape[-1] // chunks
        x, y, c = lax.axis_index("x"), lax.axis_index("y"), lax.axis_index("c")
        self.me, self.sibling, self.c = (x, y, c), (x, y, 1 - c), c
        self.xn, self.yn, self.dg = (1 - x, y), (x, 1 - y), (1 - x, 1 - y)
        self.pass_from = (lax.rem(x + 1 - c, 2), lax.rem(y + c, 2))
        self.pass_to = (lax.rem(x + c, 2), lax.rem(y + 1 - c, 2))
        self.x_ref, self.out_ref, self.send_sems, self.recv_sems = x_ref, out_ref, send_sems, recv_sems
        self.mine = pltpu.make_async_copy(x_ref, self._slab(*self.me), local_sem)

    def _slab(self, px, py, pc):
        if self.slab_of is not None:
            return self.slab_of(self.out_ref, px, py, pc)
        return self.out_ref.at[4 * px + 2 * py + pc]

    def _part(self, ref, q):
        if self.chunks == 1:
            return ref
        lanes = slice(q * self.width, (q + 1) * self.width)
        return ref.at[(slice(None),) * (len(ref.shape) - 1) + (lanes,)]

    def _copy(self, k, q, blk, to, src=None):
        i = k * self.chunks + q
        return pltpu.make_async_remote_copy(
            src_ref=self._part(self._slab(*blk) if src is None else src, q), dst_ref=self._part(self._slab(*blk), q),
            send_sem=self.send_sems.at[i], recv_sem=self.recv_sems.at[i], device_id=to, device_id_type=MESH)

    def _sends(self, q):
        c = self.c
        return [self._copy(0, q, self.me, self.sibling, src=self.x_ref),
                self._copy(1, q, self.me, (*self.xn, c), src=self.x_ref),
                self._copy(2, q, self.me, (*self.yn, c), src=self.x_ref),
                self._copy(3, q, (*self.pass_from, c), (*self.pass_to, c)),
                self._copy(4, q, (*self.xn, c), self.sibling),
                self._copy(5, q, (*self.yn, c), self.sibling),
                self._copy(6, q, (*self.dg, c), self.sibling)]

    def start(self):
        self.mine.start()
        for q in range(self.chunks):
            sends = self._sends(q)
            for k in (1, 2, 0):
                sends[k].start()

    def pass_on(self, only=None):
        for q in range(self.chunks) if only is None else (only,):
            sends = self._sends(q)
            self._copy(1, q, (*self.xn, self.c), self.me).wait_recv()
            self._copy(2, q, (*self.yn, self.c), self.me).wait_recv()
            for k in (3, 4, 5):
                sends[k].start()

    def relay_diagonal(self, only=None):
        for q in range(self.chunks) if only is None else (only,):
            self._copy(3, q, (*self.dg, self.c), self.me).wait_recv()
            self._sends(q)[6].start()

    def relay(self):
        self.pass_on()
        self.relay_diagonal()

    def finish(self):
        c = self.c
        for q in range(self.chunks):
            self._copy(0, q, self.sibling, self.me).wait_recv()
            for k, chip in ((4, self.xn), (5, self.yn), (6, self.dg)):
                self._copy(k, q, (*chip, 1 - c), self.me).wait_recv()
            for cp in self._sends(q):
                cp.wait_send()
        self.mine.wait()


def _prologue(cw, w_ada, b_chips, win_window, win_edges, n_in, pos_rows, inv_freq):
    s = pos_rows.shape[0] * LANES
    rt = min(512, s)
    inner = win_window.shape[0] - 2 * W_TILE
    starts = [(n_in * j) // W_TILE * W_TILE for j in range(4)]
    edge_rows = starts + [starts[3] + inner + W_TILE]
    assert all(starts[j] + inner + W_TILE == edge_rows[j + 1] for j in range(4))

    def body(cw_ref, wada_hbm, b_ref, hin_ref, hedge_ref, pos_ref, f_ref,
             first_ref, mod_ref, win_ref, cos_hbm, sin_hbm,
             mod_blk, cos_ref, sin_ref, wada_ref, edge_ref, tile_ref, table_sems, local_sems, tile_sems, *sems):
        fetch_w = pltpu.make_async_copy(wada_hbm, wada_ref, local_sems.at[0])
        fetch_w.start()
        g_c = _Gather(cw_ref, first_ref, *sems[0:3])
        half_lanes = hin_ref.shape[1]

        def lanes_of(pc):
            return pl.ds(pl.multiple_of(pc * half_lanes, half_lanes), half_lanes)

        def inner_rows(px, py):
            return pl.ds(pl.multiple_of((n_in * (2 * px + py)) // W_TILE * W_TILE + W_TILE, W_TILE), inner)

        g_in = _Gather(hin_ref.at[pl.ds(W_TILE, inner), :], win_ref, *sems[3:6], chunks=WEIGHT_CHUNKS,
                       slab=lambda ref, px, py, pc: ref.at[inner_rows(px, py), lanes_of(pc)])
        g_mod = _Gather(mod_blk, mod_ref, *sems[6:9])
        g_edge = _Gather(hedge_ref, edge_ref, *sems[9:12],
                         slab=lambda ref, px, py, pc: ref.at[2 * px + py, :, :, lanes_of(pc)])
        g_c.start()
        g_edge.start()
        g_in.start()
        g_c.relay()
        g_edge.relay()
        g_c.finish()
        c_rows = [jnp.concatenate([first_ref[d, r:r + 1, :] for r in range(8)], axis=1) for d in range(8)]
        c_all = jnp.concatenate(c_rows, axis=0)
        sc = (c_all * _sigmoid(c_all)).astype(BF)
        fetch_w.wait()
        chip = 2 * lax.axis_index("x") + lax.axis_index("y")
        mod_blk[...] = _dot(sc, wada_ref[...].astype(BF)) + b_ref[pl.ds(chip, 1), :]
        g_mod.start()

        def rope_rows(i, carry):
            rows = pl.ds(pl.multiple_of(i * rt, rt), rt)
            cols = [jnp.transpose(jnp.broadcast_to(pos_ref[pl.ds(i * (rt // LANES) + b, 1), :].astype(F32), (LANES, LANES)))
                    for b in range(rt // LANES)]
            ang = jnp.concatenate(cols, axis=0) * f_ref[...]
            lane = lax.broadcasted_iota(jnp.int32, ang.shape, 1)
            cos_ref[rows, :] = jnp.cos(ang)
            sn = jnp.sin(ang)
            sin_ref[rows, :] = jnp.where((lane % 64) < 32, -sn, sn)
            pltpu.make_async_copy(cos_ref.at[rows, :], cos_hbm.at[rows, :], table_sems.at[0]).start()
            pltpu.make_async_copy(sin_ref.at[rows, :], sin_hbm.at[rows, :], table_sems.at[1]).start()
            return carry

        tiles = []

        def edge_tiles():
            g_edge.finish()
            row = lax.broadcasted_iota(jnp.int32, tile_ref.shape[1:], 0)
            for k, at in enumerate(edge_rows):
                last = edge_ref[max(k - 1, 0), 1].astype(F32)
                first = edge_ref[min(k, 3), 0].astype(F32)
                cut = W_TILE if k == 4 else (n_in * k) % W_TILE
                tile_ref[k] = jnp.where(row < cut, last, first).astype(tile_ref.dtype)
                tiles.append(pltpu.make_async_copy(tile_ref.at[k], win_ref.at[at:at + W_TILE, :], tile_sems.at[k]))
                tiles[-1].start()

        waits = ([lambda q=q: g_in.pass_on(q) for q in range(WEIGHT_CHUNKS)] + [edge_tiles]
                 + [lambda q=q: g_in.relay_diagonal(q) for q in range(WEIGHT_CHUNKS)] + [g_mod.relay])
        steps = s // rt
        lead = steps // 4
        per_wait = max((steps - lead) // len(waits), 1)
        lax.fori_loop(0, lead, rope_rows, 0)
        done = lead
        for wait in waits:
            wait()
            nxt = min(done + per_wait, steps)
            lax.fori_loop(done, nxt, rope_rows, 0)
            done = nxt
        lax.fori_loop(done, steps, rope_rows, 0)
        g_in.finish()
        g_mod.finish()
        for cp in tiles:
            cp.wait()
        pltpu.make_async_copy(cos_ref, cos_hbm, table_sems.at[0]).wait()
        pltpu.make_async_copy(sin_ref, sin_hbm, table_sems.at[1]).wait()

    vm = pl.BlockSpec(memory_space=pltpu.VMEM)
    hbm = pl.BlockSpec(memory_space=pl.ANY)
    half_lanes = win_window.shape[1]
    return pl.pallas_call(
        body, name="prologue",
        out_shape=[jax.ShapeDtypeStruct((8,) + cw.shape, F32), jax.ShapeDtypeStruct((8, 8, w_ada.shape[1]), F32),
                   jax.ShapeDtypeStruct((4 * n_in, 2 * half_lanes), win_window.dtype),
                   jax.ShapeDtypeStruct((s, LANES), F32), jax.ShapeDtypeStruct((s, LANES), F32)],
        in_specs=[vm, hbm, vm, hbm, vm, vm, vm], out_specs=[vm, vm, hbm, hbm, hbm],
        scratch_shapes=[pltpu.VMEM((8, w_ada.shape[1]), F32), pltpu.VMEM((s, LANES), F32), pltpu.VMEM((s, LANES), F32),
                        pltpu.VMEM(w_ada.shape, F32),
                        pltpu.VMEM((4, 2, W_TILE, 2 * half_lanes), win_window.dtype),
                        pltpu.VMEM((5, W_TILE, 2 * half_lanes), win_window.dtype),
                        pltpu.SemaphoreType.DMA((2,)), pltpu.SemaphoreType.DMA((1,)), pltpu.SemaphoreType.DMA((5,))]
        + _GATHER_SEMS + _gather_sems(WEIGHT_CHUNKS) + _GATHER_SEMS + _GATHER_SEMS,
        compiler_params=pltpu.CompilerParams(vmem_limit_bytes=VMEM_LIMIT),
    )(cw, w_ada, b_chips, win_window, win_edges, pos_rows, inv_freq)


def _reduce_scratch(rr, cc):
    c2 = cc // 2
    return [pltpu.VMEM((4, rr, c2), F32), pltpu.VMEM((4, rr, c2), F32), pltpu.VMEM((3, rr, c2), BF),
            pltpu.VMEM((2, rr, c2), BF), pltpu.VMEM((rr, c2), BF), pltpu.VMEM((rr, c2), F32),
            pltpu.SemaphoreType.DMA((8 + 3 * WEIGHT_CHUNKS,)), pltpu.SemaphoreType.DMA((8 + 3 * WEIGHT_CHUNKS,)),
            pltpu.SemaphoreType.DMA((5,))]


class _Reduce:
    def __init__(self, p_hbm, out_ref, acc_ref, own_ref, send_ref, land_ref, relay_ref, res_ref,
                 send_sems, recv_sems, local_sems, rows=None):
        x, y, c = lax.axis_index("x"), lax.axis_index("y"), lax.axis_index("c")
        part = (lambda j, ln: p_hbm.at[j, :, ln]) if rows is None else (lambda j, ln: p_hbm.at[rows(j), ln])
        c2 = out_ref.shape[1] // 2
        sibling = (x, y, 1 - c)
        first = (lax.rem(x + 1 - c, 2), lax.rem(y + c, 2))
        second = (lax.rem(x + c, 2), lax.rem(y + 1 - c, 2))
        shards = [2 * first[0] + first[1], 2 * second[0] + second[1], 2 * (1 - x) + (1 - y), 2 * x + y]
        sibling_slot = (1, 0, 2, 3)
        mine = pl.ds(pl.multiple_of(c * c2, c2), c2)
        other = pl.ds(pl.multiple_of((1 - c) * c2, c2), c2)
        self.acc_ref, self.own_ref, self.send_ref, self.land_ref = acc_ref, own_ref, send_ref, land_ref
        self.relay_ref, self.res_ref = relay_ref, res_ref
        self.own = [pltpu.make_async_copy(part(j, mine), own_ref.at[k], local_sems.at[k])
                    for k, j in enumerate(shards)]
        self.swap_out = [pltpu.make_async_remote_copy(
            src_ref=part(j, other), dst_ref=acc_ref.at[sibling_slot[k]], send_sem=send_sems.at[k],
            recv_sem=recv_sems.at[sibling_slot[k]], device_id=sibling, device_id_type=MESH) for k, j in enumerate(shards)]
        self.swap_in = [pltpu.make_async_remote_copy(
            src_ref=part(j, other), dst_ref=acc_ref.at[k], send_sem=send_sems.at[k], recv_sem=recv_sems.at[k],
            device_id=sibling, device_id_type=MESH) for k, j in enumerate(shards)]

        self.lanes = [slice(q * (c2 // WEIGHT_CHUNKS), (q + 1) * (c2 // WEIGHT_CHUNKS)) for q in range(WEIGHT_CHUNKS)]

        def message(m, src, dst, to):
            return [pltpu.make_async_remote_copy(
                src_ref=src.at[:, ln], dst_ref=dst.at[:, ln], send_sem=send_sems.at[8 + m * WEIGHT_CHUNKS + q],
                recv_sem=recv_sems.at[8 + m * WEIGHT_CHUNKS + q], device_id=(*to, c), device_id_type=MESH)
                for q, ln in enumerate(self.lanes)]

        self.direct = message(0, send_ref.at[0], land_ref.at[0], first)
        self.passed = message(1, send_ref.at[1], relay_ref, first)
        self.joint = message(2, send_ref.at[2], land_ref.at[1], second)
        self.put = pltpu.make_async_copy(res_ref, out_ref.at[:, mine], local_sems.at[4])
        self.share = pltpu.make_async_remote_copy(
            src_ref=res_ref, dst_ref=out_ref.at[:, mine], send_sem=send_sems.at[7],
            recv_sem=recv_sems.at[7], device_id=sibling, device_id_type=MESH)

    def start(self):
        for k in (2, 0, 1, 3):
            self.own[k].start()
            self.swap_out[k].start()

    def _combine(self, k):
        self.own[k].wait()
        self.swap_out[k].wait_send()
        self.swap_in[k].wait_recv()
        self.acc_ref[k] = self.acc_ref[k] + self.own_ref[k]

    def combine_and_send(self):
        dt = self.send_ref.dtype
        self._combine(2)
        self.send_ref[1] = self.acc_ref[2].astype(dt)
        for cp in self.passed:
            cp.start()
        self._combine(0)
        self.send_ref[0] = self.acc_ref[0].astype(dt)
        for cp in self.direct:
            cp.start()
        self._combine(1)
        self._combine(3)

    def send_joint(self):
        dt = self.send_ref.dtype
        for q, ln in enumerate(self.lanes):
            self.passed[q].wait_recv()
            self.send_ref[2, :, ln] = (self.acc_ref[1, :, ln] + self.relay_ref[:, ln].astype(F32)).astype(dt)
            self.joint[q].start()

    def total_and_share(self):
        for cp in self.direct + self.joint:
            cp.wait_recv()
        self.res_ref[...] = self.acc_ref[3] + self.land_ref[0].astype(F32) + self.land_ref[1].astype(F32)
        for cp in self.direct + self.passed + self.joint:
            cp.wait_send()
        self.put.start()
        self.share.start()

    def finish(self):
        self.put.wait()
        self.share.wait()


def _shard_window(n):
    return max(-(-(n * (j + 1)) // 8) * 8 - (n * j) // 8 * 8 for j in range(4))


class _LocalUpdate:
    def __init__(self, ins, in_vm, out_vm, outs, in_sems, out_sems):
        self.loads = [pltpu.make_async_copy(a, b, in_sems.at[k]) for k, (a, b) in enumerate(zip(ins, in_vm))]
        self.stores = [pltpu.make_async_copy(a, b, out_sems.at[k]) for k, (a, b) in enumerate(zip(out_vm, outs))]

    def start(self):
        for cp in self.loads:
            cp.start()

    def loaded(self):
        for cp in self.loads:
            cp.wait()

    def store(self):
        for cp in self.stores:
            cp.start()

    def finish(self):
        for cp in self.stores:
            cp.wait()


def _epilogue(dw_in_t, small, c_all, ada, out, dmod_row):
    cc = dw_in_t.shape[1]
    n = dw_in_t.shape[0] // 4
    r_in = _shard_window(n)
    n_red = len(_reduce_scratch(r_in, cc))
    ra, ca = ada[0].shape
    dm_rows = ca // LANES
    tr = min(512, ra)

    def body(pin_hbm, small_ref, c_ref, *rest):
        ada_hbm, out_hbm = rest[0:3], rest[3:7]
        gin_ref, small_all_ref = rest[7:9]
        ada_res, out_res = rest[9:13], rest[13:16]
        scratch = rest[16:]
        red_in = _Reduce(pin_hbm, gin_ref, *scratch[0:n_red],
                         rows=lambda j: pl.ds(pl.multiple_of((n * j) // 8 * 8, 8), r_in))
        gat = _Gather(small_ref, small_all_ref, *scratch[n_red:n_red + 3])
        local = scratch[n_red + 3:]
        ada_in, ada_out, out_in, out_out = local[0:3], local[3:7], local[7:11], local[11:14]
        upd_ada = _LocalUpdate(ada_hbm, ada_in, ada_out, ada_res, local[14], local[15])
        upd_out = _LocalUpdate(out_hbm, out_in, out_out, out_res, local[16], local[17])
        red_in.start()
        gat.start()
        upd_out.start()
        upd_ada.start()
        gat.relay()
        red_in.combine_and_send()
        gat.finish()
        red_in.send_joint()

        upd_out.loaded()
        out_out[0][...], out_out[1][...], out_out[2][...] = _adam(*[r[...] for r in out_in])
        upd_out.store()
        chip = 2 * lax.axis_index("x") + lax.axis_index("y")
        dm = jnp.concatenate(
            [jnp.concatenate([small_all_ref[d, pl.ds(dmod_row + dm_rows * chip + r, 1), :] for r in range(dm_rows)], axis=1)
             for d in range(8)], axis=0)
        cv = c_ref[...]
        sc = jnp.concatenate([cv * _sigmoid(cv), jnp.zeros_like(cv)], axis=0).astype(BF)
        dmb = jnp.concatenate([dm, jnp.zeros_like(dm)], axis=0).astype(BF)
        upd_ada.loaded()
        for r0 in range(0, ra, tr):
            rows = slice(r0, r0 + tr)
            g = _dot(sc[:, rows], dmb, TN)
            ada_out[0][rows, :] = g
            ada_out[1][rows, :], ada_out[2][rows, :], ada_out[3][rows, :] = _adam(
                ada_in[0][rows, :], g, ada_in[1][rows, :], ada_in[2][rows, :])
        upd_ada.store()

        red_in.total_and_share()
        red_in.finish()
        upd_out.finish()
        upd_ada.finish()

    vm = pl.BlockSpec(memory_space=pltpu.VMEM)
    anyspec = pl.BlockSpec(memory_space=pl.ANY)
    ada_buf, out_buf = pltpu.VMEM((ra, ca), F32), pltpu.VMEM(out[0].shape, F32)
    return pl.pallas_call(
        body, name="epilogue",
        out_shape=[jax.ShapeDtypeStruct((r_in, cc), F32), jax.ShapeDtypeStruct((8,) + small.shape, F32)]
        + [jax.ShapeDtypeStruct((ra, ca), F32)] * 4 + [jax.ShapeDtypeStruct(out[0].shape, F32)] * 3,
        in_specs=[anyspec, vm, vm] + [anyspec] * 7, out_specs=[anyspec, vm] + [anyspec] * 7,
        scratch_shapes=_reduce_scratch(r_in, cc) + _GATHER_SEMS + [ada_buf] * 7 + [out_buf] * 7
        + [pltpu.SemaphoreType.DMA((3,)), pltpu.SemaphoreType.DMA((4,)), pltpu.SemaphoreType.DMA((4,)), pltpu.SemaphoreType.DMA((3,))],
        compiler_params=pltpu.CompilerParams(vmem_limit_bytes=VMEM_LIMIT),
    )(dw_in_t, small, c_all, *ada, *out)


def _rope(t, cosb, sinb, first_half):
    partner = jnp.where(first_half, pltpu.roll(t, 96, 1), pltpu.roll(t, 32, 1))
    return t * cosb + partner * sinb


def _rope_t(g, cosb, sinb, first_half):
    gs = g * sinb
    partner = jnp.where(first_half, pltpu.roll(gs, 96, 1), pltpu.roll(gs, 32, 1))
    return g * cosb + partner


def _modnorm(x, g, sc1p, shift):
    r = lax.rsqrt(jnp.mean(x * x, axis=-1, keepdims=True) + RMS_EPS)
    xn = x * r
    return xn, r, (xn * g) * sc1p + shift


W_TILE = 16


def _w_window(n):
    return max(-(-(n * (j + 1)) // W_TILE) * W_TILE - (n * j) // W_TILE * W_TILE for j in range(4))


def _load_w_padded(w_hbm, w_vm, sems):
    copies = [pltpu.make_async_copy(w_hbm.at[ref:ref + n], w_vm.at[pad:pad + n], sems.at[k])
              for k, (pad, ref, n) in enumerate(_UNPAD_ROWS)]
    for cp in copies:
        cp.start()
    w_vm[OFF_GA + GLA_RANK:, :] = jnp.zeros((D_PAD - OFF_GA - GLA_RANK, D_MODEL), w_vm.dtype)
    return copies


def _inproj_fwd(x2d, shift, sc1p, g_norm, w_t):
    s = x2d.shape[0]
    tm = min(1024, s)
    nsteps = s // tm
    slots = 3

    def body(x_hbm, sh_ref, sc_ref, g_ref, w_hbm, o_ref, w_vm, x_buf, sems, x_sems):
        i = pl.program_id(0)

        def fetch(step):
            static = isinstance(step, int)
            slot = step % slots if static else lax.rem(step, slots)
            rows = pl.ds(step * tm if static else pl.multiple_of(step * tm, tm), tm)
            return pltpu.make_async_copy(x_hbm.at[rows, :], x_buf.at[slot], x_sems.at[slot])

        @pl.when(i == 0)
        def _():
            loads = _load_w_padded(w_hbm, w_vm, sems)
            for step in range(min(slots - 1, nsteps)):
                fetch(step).start()
            for cp in loads:
                cp.wait()

        @pl.when(i + (slots - 1) < nsteps)
        def _():
            fetch(i + (slots - 1)).start()

        fetch(i).wait()
        slot = lax.rem(i, slots)
        subs = _subtiles(tm)
        hs = [_modnorm(x_buf[slot, sl, :], g_ref[...], sc_ref[...], sh_ref[...])[2].astype(BF) for sl in subs]
        for sl, h in zip(subs, hs):
            o_ref[sl, :] = _dot(h, w_vm[...], NT)

    vec = _full((1, D_MODEL))
    return pl.pallas_call(
        body, name="inproj_fwd", grid=(nsteps,),
        in_specs=[pl.BlockSpec(memory_space=pl.ANY), vec, vec, vec, pl.BlockSpec(memory_space=pl.ANY)],
        out_specs=pl.BlockSpec((tm, D_PAD), lambda i: (i, 0)),
        out_shape=jax.ShapeDtypeStruct((s, D_PAD), F32),
        scratch_shapes=[pltpu.VMEM((D_PAD, D_MODEL), BF), pltpu.VMEM((slots, tm, D_MODEL), F32),
                        pltpu.SemaphoreType.DMA((len(_UNPAD_ROWS),)), pltpu.SemaphoreType.DMA((slots,))],
        compiler_params=_params(("arbitrary",)),
    )(x2d, shift, sc1p, g_norm, w_t)


def _split3(a):
    hi = a.astype(BF)
    r1 = a - hi.astype(F32)
    mid = r1.astype(BF)
    lo = (r1 - mid.astype(F32)).astype(BF)
    return hi, mid, lo


def _tri_matmul(tri, a):
    hi, mid, lo = _split3(a)
    return _dot(tri, hi) + _dot(tri, mid) + _dot(tri, lo)


def _chunks(tb):
    return [slice(c * GLA_CHUNK, (c + 1) * GLA_CHUNK) for c in range(tb // GLA_CHUNK)]


def _per_chunk_rows(rows, width):
    return jnp.concatenate([jnp.broadcast_to(r, (GLA_CHUNK, width)) for r in rows], axis=0)


def _gla_triangle(tb):
    row = lax.broadcasted_iota(jnp.int32, (tb, tb), 0)
    col = lax.broadcasted_iota(jnp.int32, (tb, tb), 1)
    return (((row // GLA_CHUNK) == (col // GLA_CHUNK)) & (col <= row)).astype(F32)


def _lane_mean(x, ones_b):
    hi = x.astype(BF)
    lo = (x - hi.astype(F32)).astype(BF)
    return (_dot(hi, ones_b) + _dot(lo, ones_b)) * (1.0 / LANES)


def _head(t, h, lo_h):
    blk = t[:, LANES * (h // 2):LANES * (h // 2 + 1)]
    return jnp.where(lo_h, blk, 0.0) if h % 2 == 0 else jnp.where(lo_h, 0.0, blk)


def _gla_block_common(qk, ga, wd, bd, tril_b):
    tb = qk.shape[0]
    q, k = qk[:, :256], qk[:, 256:]
    z = _dot(ga.astype(BF), wd) + bd
    la = (jnp.minimum(z, 0.0) - jnp.log(1.0 + jnp.exp(-jnp.abs(z)))) * (1.0 / GLA_TAU)
    b = _tri_matmul(tril_b, la)
    bls = [b[rs.stop - 1:rs.stop, :] for rs in _chunks(tb)]
    eq = jnp.exp(b)
    ek = jnp.exp(-b)
    f = jnp.exp(_per_chunk_rows(bls, 256) - b)
    return z, eq, ek, f, q * (eq * GLA_DK ** -0.5), k * ek, k * f, bls


def _gla_units(s, rows):
    sub = min(GLA_SUB, s)
    tb = min(rows, s)
    subs = [slice(i * sub, (i + 1) * sub) for i in range(tb // sub)]
    units = [(i, h) for i in range(len(subs)) for h in range(GLA_HEADS)]
    return tb, sub, subs, units


def _gla_fwd(proj, wdecp, bdec, ggla):
    s = proj.shape[0]
    tb, sub, subs, units = _gla_units(s, GLA_ROWS_FWD)
    nch = sub // GLA_CHUNK

    def body(qk_ref, v_ref, gz_ref, ga_ref, wd_ref, bd_ref, gg_ref, tri_ref, og_ref, opre_ref, sprev_ref, st_ref):
        @pl.when(pl.program_id(0) == 0)
        def _():
            st_ref[...] = jnp.zeros_like(st_ref)

        lo_h = lax.broadcasted_iota(jnp.int32, (sub, LANES), 1) < GLA_DK
        tril = tri_ref[...] > 0.5
        tril_b = tri_ref[...].astype(BF)
        ones_b = jnp.ones((LANES, LANES), BF)
        gg, wd, bd = gg_ref[...], wd_ref[...], bd_ref[...]
        chunks = _chunks(sub)
        lanes = [slice(h * LANES, (h + 1) * LANES) for h in range(GLA_HEADS)]
        com = [_gla_block_common(qk_ref[sl, :], ga_ref[sl, :], wd, bd, tril_b) for sl in subs]
        decs = [[jnp.exp(bl) for bl in cm[7]] for cm in com]
        a = {(i, h): _head(com[i][4], h, lo_h).astype(BF) for i, h in units}
        bm = {(i, h): _head(com[i][5], h, lo_h).astype(BF) for i, h in units}
        ktl = {(i, h): _head(com[i][6], h, lo_h).astype(BF) for i, h in units}
        vh = {(i, h): v_ref[subs[i], lanes[h]].astype(BF) for i, h in units}
        sc = {u: _dot(a[u], bm[u], NT) for u in units}
        upd = {u: [_dot(vh[u][rs], ktl[u][rs], TN) for rs in chunks] for u in units}
        p = {u: jnp.where(tril, sc[u], 0.0).astype(BF) for u in units}
        o = {u: _dot(p[u], vh[u]) for u in units}
        states = {}
        for h in range(GLA_HEADS):
            st = st_ref[h]
            for i in range(len(subs)):
                entering = []
                for c in range(nch):
                    entering.append(st)
                    sprev_ref[i * nch + c, h] = st
                    st = st * decs[i][c][:, LANES * (h // 2):LANES * (h // 2 + 1)] + upd[(i, h)][c]
                states[(i, h)] = entering
            st_ref[h] = st
        inter = {u: [_dot(a[u][rs], states[u][c].astype(BF), NT) for c, rs in enumerate(chunks)] for u in units}
        o = {u: o[u] + jnp.concatenate(inter[u], axis=0) for u in units}
        ms = {u: _lane_mean(o[u] * o[u], ones_b) for u in units}
        for i, h in units:
            gzh = gz_ref[subs[i], lanes[h]]
            opre_ref[subs[i], lanes[h]] = o[(i, h)]
            og_ref[subs[i], lanes[h]] = (((o[(i, h)] * lax.rsqrt(ms[(i, h)] + RMS_EPS)) * gg[:, lanes[h]])
                                         * (gzh * _sigmoid(gzh))).astype(og_ref.dtype)

    def col(width, off):
        return pl.BlockSpec((tb, width), lambda i: (i, off // width))

    return pl.pallas_call(
        body, name="gla_fwd", grid=(s // tb,),
        in_specs=[col(512, OFF_QK), col(512, OFF_V), col(512, OFF_GZ), col(LANES, OFF_GA),
                  _full((LANES, 256)), _full((1, 256)), _full((1, 512)), _full((sub, sub))],
        out_specs=[pl.BlockSpec((tb, 512), lambda i: (i, 0)), pl.BlockSpec((tb, 512), lambda i: (i, 0)),
                   pl.BlockSpec((tb // GLA_CHUNK, GLA_HEADS, LANES, LANES), lambda i: (i, 0, 0, 0))],
        out_shape=[jax.ShapeDtypeStruct((s, 512), BF), jax.ShapeDtypeStruct((s, 512), F32),
                   jax.ShapeDtypeStruct((s // GLA_CHUNK, GLA_HEADS, LANES, LANES), F32)],
        scratch_shapes=[pltpu.VMEM((GLA_HEADS, LANES, LANES), F32)],
        compiler_params=_params(("arbitrary",)),
    )(proj, proj, proj, proj, wdecp, bdec, ggla, _gla_triangle(sub))


def _gla_bwd(proj, dog, opre, sprev, wdecp, bdec, ggla):
    s = proj.shape[0]
    tb, sub, subs, units = _gla_units(s, GLA_ROWS_BWD)
    nsub = len(subs)
    nch = sub // GLA_CHUNK
    nb = s // tb

    def body(qk_ref, v_ref, gz_ref, ga_ref, dog_ref, opre_ref, sprev_ref, wd_ref, bd_ref, gg_ref, tri_ref, triu_ref,
             dqk_ref, dv_ref, dgz_ref, dga_ref, dwd_ref, dbd_ref, dgg_ref, dst_ref):
        @pl.when(pl.program_id(0) == 0)
        def _():
            dst_ref[...] = jnp.zeros_like(dst_ref)
            dwd_ref[...] = jnp.zeros_like(dwd_ref)
            dbd_ref[...] = jnp.zeros_like(dbd_ref)
            dgg_ref[...] = jnp.zeros_like(dgg_ref)

        lo_h = lax.broadcasted_iota(jnp.int32, (sub, LANES), 1) < GLA_DK
        tril = tri_ref[...] > 0.5
        tril_b = tri_ref[...].astype(BF)
        triu_b = triu_ref[...].astype(BF)
        ones_b = jnp.ones((LANES, LANES), BF)
        last_row = (lax.broadcasted_iota(jnp.int32, (sub, LANES), 0) % GLA_CHUNK) == GLA_CHUNK - 1
        wd, gg, bd = wd_ref[...], gg_ref[...], bd_ref[...]
        chunks = _chunks(sub)
        lanes = [slice(h * LANES, (h + 1) * LANES) for h in range(GLA_HEADS)]
        blks = [slice(LANES * (h // 2), LANES * (h // 2 + 1)) for h in range(GLA_HEADS)]
        ga = [ga_ref[sl, :] for sl in subs]
        com = [_gla_block_common(qk_ref[sl, :], ga[i], wd, bd, tril_b) for i, sl in enumerate(subs)]
        decs = [[jnp.exp(bl) for bl in cm[7]] for cm in com]
        a = {(i, h): _head(com[i][4], h, lo_h).astype(BF) for i, h in units}
        bm = {(i, h): _head(com[i][5], h, lo_h).astype(BF) for i, h in units}
        ktl = {(i, h): _head(com[i][6], h, lo_h).astype(BF) for i, h in units}
        vh = {(i, h): v_ref[subs[i], lanes[h]].astype(BF) for i, h in units}
        sc = {u: _dot(a[u], bm[u], NT) for u in units}

        o = {(i, h): opre_ref[subs[i], lanes[h]] for i, h in units}
        ms = {u: _lane_mean(o[u] * o[u], ones_b) for u in units}
        gz = {(i, h): gz_ref[subs[i], lanes[h]] for i, h in units}
        dog = {(i, h): dog_ref[subs[i], lanes[h]] for i, h in units}
        sg = {u: _sigmoid(gz[u]) for u in units}
        r = {u: lax.rsqrt(ms[u] + RMS_EPS) for u in units}
        ohat = {u: o[u] * r[u] for u in units}
        sil = {u: gz[u] * sg[u] for u in units}
        for i, h in units:
            u = (i, h)
            dgz_ref[subs[i], lanes[h]] = (dog[u] * (ohat[u] * gg[:, lanes[h]])
                                          * (sg[u] * (1.0 + gz[u] * (1.0 - sg[u])))).astype(dgz_ref.dtype)
            dgg_ref[:, lanes[h]] += jnp.sum(dog[u] * sil[u] * ohat[u], axis=0, keepdims=True)
        dn = {(i, h): dog[(i, h)] * sil[(i, h)] * gg[:, lanes[h]] for i, h in units}
        mdn = {u: _lane_mean(dn[u] * ohat[u], ones_b) for u in units}
        do = {u: (r[u] * (dn[u] - ohat[u] * mdn[u])).astype(BF) for u in units}

        p = {u: jnp.where(tril, sc[u], 0.0).astype(BF) for u in units}
        dpr = {u: _dot(do[u], vh[u], NT) for u in units}
        incr = {u: [_dot(do[u][rs], a[u][rs], TN) for rs in chunks] for u in units}
        dv = {u: _dot(p[u], do[u], TN) for u in units}
        dp = {u: jnp.where(tril, dpr[u], 0.0).astype(BF) for u in units}
        dqd = {u: _dot(dp[u], bm[u]) for u in units}
        dkd = {u: _dot(dp[u], a[u], TN) for u in units}
        st = {(i, h): [sprev_ref[i * nch + c, h] for c in range(nch)] for i, h in units}
        leaving = {}
        for h in range(GLA_HEADS):
            d = dst_ref[h]
            for i in reversed(range(nsub)):
                out = [None] * nch
                for c in reversed(range(nch)):
                    out[c] = d
                    d = d * decs[i][c][:, blks[h]] + incr[(i, h)][c]
                leaving[(i, h)] = out
            dst_ref[h] = d
        lv_b = {u: [leaving[u][c].astype(BF) for c in range(nch)] for u in units}
        dv_s = {u: [_dot(ktl[u][rs], lv_b[u][c], NT) for c, rs in enumerate(chunks)] for u in units}
        dqd_s = {u: [_dot(do[u][rs], st[u][c].astype(BF)) for c, rs in enumerate(chunks)] for u in units}
        dkt_s = {u: [_dot(vh[u][rs], lv_b[u][c]) for c, rs in enumerate(chunks)] for u in units}
        ddec = {u: [jnp.sum(leaving[u][c] * st[u][c], axis=0, keepdims=True) for c in range(nch)] for u in units}
        for i, h in units:
            dv_ref[subs[i], lanes[h]] = (dv[(i, h)] + jnp.concatenate(dv_s[(i, h)], axis=0)).astype(dv_ref.dtype)
        dqd = {u: dqd[u] + jnp.concatenate(dqd_s[u], axis=0) for u in units}
        dkt = {u: jnp.concatenate(dkt_s[u], axis=0) for u in units}

        db = []
        for i, sl in enumerate(subs):
            _, eq, ek, f, qd, kd, kt, _ = com[i]
            parts = []
            for pair in range(GLA_HEADS // 2):
                blk, u0, u1 = blks[2 * pair], (i, 2 * pair), (i, 2 * pair + 1)
                dqd_b, dkd_b, dkt_b = dqd[u0] + dqd[u1], dkd[u0] + dkd[u1], dkt[u0] + dkt[u1]
                dqk_ref[sl, blk] = (dqd_b * (eq[:, blk] * GLA_DK ** -0.5)).astype(dqk_ref.dtype)
                dqk_ref[sl, 256 + LANES * pair:256 + LANES * (pair + 1)] = (dkd_b * ek[:, blk] + dkt_b * f[:, blk]).astype(dqk_ref.dtype)
                dkt_kt = dkt_b * kt[:, blk]
                dbp = dqd_b * qd[:, blk] - dkd_b * kd[:, blk] - dkt_kt
                dbl = [jnp.sum(dkt_kt[rs], axis=0, keepdims=True) + (ddec[u0][c] + ddec[u1][c]) * decs[i][c][:, blk]
                       for c, rs in enumerate(chunks)]
                parts.append(jnp.where(last_row, dbp + _per_chunk_rows(dbl, LANES), dbp))
            db.append(jnp.concatenate(parts, axis=1))
        dla = [_tri_matmul(triu_b, db[i]) for i in range(nsub)]
        dz32 = [dla[i] * (1.0 / GLA_TAU) * _sigmoid(-com[i][0]) for i in range(nsub)]
        dz = [t.astype(BF) for t in dz32]
        for i, sl in enumerate(subs):
            dga_ref[sl, :] = _dot(dz[i], wd, NT).astype(dga_ref.dtype)
            dwd_ref[...] += _dot(ga[i].astype(BF), dz[i], TN)
            dbd_ref[...] += jnp.sum(dz32[i], axis=0, keepdims=True)

    def col(width, off):
        return pl.BlockSpec((tb, width), lambda i: (nb - 1 - i, off // width))

    def rev(width):
        return pl.BlockSpec((tb, width), lambda i: (nb - 1 - i, 0))

    return pl.pallas_call(
        body, name="gla_bwd", grid=(nb,),
        in_specs=[col(512, OFF_QK), col(512, OFF_V), col(512, OFF_GZ), col(LANES, OFF_GA), rev(512), rev(512),
                  pl.BlockSpec((tb // GLA_CHUNK, GLA_HEADS, LANES, LANES), lambda i: (nb - 1 - i, 0, 0, 0)),
                  _full((LANES, 256)), _full((1, 256)), _full((1, 512)), _full((sub, sub)), _full((sub, sub))],
        out_specs=[rev(512), rev(512), rev(512), rev(LANES), _full((LANES, 256)), _full((1, 256)), _full((1, 512))],
        out_shape=[jax.ShapeDtypeStruct((s, 512), BF), jax.ShapeDtypeStruct((s, 512), BF),
                   jax.ShapeDtypeStruct((s, 512), BF), jax.ShapeDtypeStruct((s, LANES), BF),
                   jax.ShapeDtypeStruct((LANES, 256), F32), jax.ShapeDtypeStruct((1, 256), F32),
                   jax.ShapeDtypeStruct((1, 512), F32)],
        scratch_shapes=[pltpu.VMEM((GLA_HEADS, LANES, LANES), F32)],
        compiler_params=_params(("arbitrary",)),
    )(proj, proj, proj, proj, dog, opre, sprev, wdecp, bdec, ggla, _gla_triangle(sub), _gla_triangle(sub).T)


_SWA_COL_HEADS = (0, 2, 1, 3, 4, 6, 5, 7)
_SWA_COLS = SWA_HEADS * SWA_BLOCK


def _swa_masks():
    lo2 = lax.broadcasted_iota(jnp.int32, (2 * SWA_BLOCK, LANES), 1) < 64
    lane1 = lax.broadcasted_iota(jnp.int32, (SWA_BLOCK, LANES), 1)
    first_half = (lane1 % 64) < 32
    key = lax.broadcasted_iota(jnp.int32, (SWA_BLOCK, _SWA_COLS), 0)
    query = lax.broadcasted_iota(jnp.int32, (SWA_BLOCK, _SWA_COLS), 1) % SWA_BLOCK
    return lo2, lane1 < 64, first_half, key > query


def _merge_band(t, prev_mask, prev_bias=None):
    prev = t[:SWA_BLOCK] if prev_bias is None else t[:SWA_BLOCK] + prev_bias
    return jnp.where(prev_mask, prev, t[SWA_BLOCK:])


def _split_band(t, prev_mask_b):
    prev = t * prev_mask_b
    return jnp.concatenate([prev, t - prev], axis=0)


def _kv_variants(t, lo2):
    tr = pltpu.roll(t, 64, 1)
    lo_v = [jnp.where(lo2, t, 0.0).astype(BF), jnp.where(lo2, tr, 0.0).astype(BF)]
    hi_v = [jnp.where(lo2, 0.0, tr).astype(BF), jnp.where(lo2, 0.0, t).astype(BF)]
    return lo_v, hi_v


def _kv_variants_t(t):
    tt = t.T
    sw = jnp.concatenate([tt[64:], tt[:64]], axis=0)
    top = lax.broadcasted_iota(jnp.int32, tt.shape, 0) < 64
    lo_v = [jnp.where(top, tt, 0.0).astype(BF), jnp.where(top, sw, 0.0).astype(BF)]
    hi_v = [jnp.where(top, 0.0, sw).astype(BF), jnp.where(top, 0.0, tt).astype(BF)]
    return lo_v, hi_v


def _swa_scores(qg, k_lo, k_hi):
    return jnp.concatenate([_dot(k_lo[0], qg[0], NT), _dot(k_hi[0], qg[0], NT),
                            _dot(k_lo[1], qg[1], NT), _dot(k_hi[1], qg[1], NT)], axis=1)


def _sink_row(sinks_ref):
    return jnp.concatenate([jnp.full((1, SWA_BLOCK), sinks_ref[0, hd], F32) for hd in _SWA_COL_HEADS], axis=1)


def _swa_softmax(st, prev_mask, prev_bias, sink):
    st = _merge_band(st, prev_mask, prev_bias)
    m = jnp.maximum(jnp.max(st, axis=0, keepdims=True), sink)
    ex = jnp.exp(st - m)
    es = jnp.exp(sink - m)
    inv = 1.0 / (jnp.sum(ex, axis=0, keepdims=True) + es)
    return ex, es, inv


def _no_prev_bias(block_index):
    return jnp.where(block_index > 0, 0.0, -1e30).astype(F32)


def _swa_queries(sq_ref, rows, cosb, sinb, first_half):
    qs = [_rope(sq_ref[rows, p * LANES:(p + 1) * LANES], cosb, sinb, first_half) * 0.125 for p in range(4)]
    return [jnp.concatenate(qs[0:2], axis=0), jnp.concatenate(qs[2:4], axis=0)]


def _phase_steps(nsteps, phases):
    return [min(nsteps - 1, (k * nsteps) // phases) for k in range(phases - 1)] + [nsteps - 1]


def _swa_fwd(proj, cos, sin, sinks, w_out):
    s = proj.shape[0]
    nq = min(SWA_QBLOCKS_FWD, s // SWA_BLOCK)
    tq = nq * SWA_BLOCK
    steps = _phase_steps(s // tq, 4)
    half_rows = w_out.shape[0] // 2

    def body(sq_ref, sz_ref, sk_ref, sv_ref, cos_ref, sin_ref, sinks_ref, wshard_ref, os_ref, opre_ref, wout_hbm,
             kprev, vprev, half_ref, *gather_sems):
        n = pl.program_id(0)

        @pl.when(n == 0)
        def _():
            kprev[...] = jnp.zeros_like(kprev)
            vprev[...] = jnp.zeros_like(vprev)
            mine = pl.ds(pl.multiple_of(lax.axis_index("c") * half_rows, half_rows), half_rows)
            half_ref[...] = wshard_ref[mine, :].astype(half_ref.dtype)

        gather = _Gather(half_ref, wout_hbm, *gather_sems, chunks=WEIGHT_CHUNKS)
        for step, phase in zip(steps, (gather.start, gather.pass_on, gather.relay_diagonal, gather.finish)):
            pl.when(n == step)(phase)

        lo2, _, first_half, prev_mask = _swa_masks()
        prev_mask_b = jnp.where(prev_mask, 1.0, 0.0).astype(BF)
        sink = _sink_row(sinks_ref)
        blocks = range(nq)
        rows = [slice(j * SWA_BLOCK, (j + 1) * SWA_BLOCK) for j in blocks]
        cosb = [cos_ref[rows[j], :] for j in blocks]
        sinb = [sin_ref[rows[j], :] for j in blocks]
        kc = [_rope(sk_ref[rows[j], :], cosb[j], sinb[j], first_half) for j in blocks]
        vc = [sv_ref[rows[j], :] for j in blocks]
        kcat = [jnp.concatenate([kprev[...] if j == 0 else kc[j - 1], kc[j]], axis=0) for j in blocks]
        vcat = [jnp.concatenate([vprev[...] if j == 0 else vc[j - 1], vc[j]], axis=0) for j in blocks]
        kprev[...] = kc[-1]
        vprev[...] = vc[-1]
        kvar = [_kv_variants(kcat[j], lo2) for j in blocks]
        vtvar = [_kv_variants_t(vcat[j]) for j in blocks]
        qg = [[q.astype(BF) for q in _swa_queries(sq_ref, rows[j], cosb[j], sinb[j], first_half)] for j in blocks]
        st = [_swa_scores(qg[j], *kvar[j]) for j in blocks]
        soft = [_swa_softmax(st[j], prev_mask, _no_prev_bias(n) if j == 0 else None, sink) for j in blocks]
        pt = [_split_band(soft[j][0].astype(BF), prev_mask_b) for j in blocks]
        og = {}
        for j in blocks:
            inv = soft[j][2]
            for g in range(2):
                c0, c1, c2 = 512 * g, 512 * g + 256, 512 * g + 512
                ot = (_dot(vtvar[j][0][g], pt[j][:, c0:c1]) * inv[:, c0:c1]
                      + _dot(vtvar[j][1][g], pt[j][:, c1:c2]) * inv[:, c1:c2])
                og[(j, g)] = ot.T
        for j in blocks:
            for g in range(2):
                for i in range(2):
                    ls = slice((2 * g + i) * LANES, (2 * g + i + 1) * LANES)
                    o = og[(j, g)][i * SWA_BLOCK:(i + 1) * SWA_BLOCK]
                    sz = sz_ref[rows[j], ls]
                    opre_ref[rows[j], ls] = o
                    os_ref[rows[j], ls] = (o * (sz * _sigmoid(sz))).astype(os_ref.dtype)

    def col(width, off):
        return pl.BlockSpec((tq, width), lambda i: (i, off // width))

    row = pl.BlockSpec((tq, LANES), lambda i: (i, 0))
    return pl.pallas_call(
        body, name="swa_fwd", grid=(s // tq,),
        in_specs=[col(512, OFF_SQ), col(512, OFF_SZ), col(LANES, OFF_SK), col(LANES, OFF_SV), row, row,
                  pl.BlockSpec(memory_space=pltpu.SMEM), _full(w_out.shape)],
        out_specs=[pl.BlockSpec((tq, 512), lambda i: (i, 0))] * 2 + [pl.BlockSpec(memory_space=pl.ANY)],
        out_shape=[jax.ShapeDtypeStruct((s, 512), BF), jax.ShapeDtypeStruct((s, 512), F32),
                   jax.ShapeDtypeStruct((8, half_rows, w_out.shape[1]), BF)],
        scratch_shapes=[pltpu.VMEM((SWA_BLOCK, LANES), F32)] * 2 + [pltpu.VMEM((half_rows, w_out.shape[1]), BF)]
        + _gather_sems(WEIGHT_CHUNKS),
        compiler_params=_params(("arbitrary",)),
    )(proj, proj, proj, proj, cos, sin, sinks, w_out)


def _swa_bwd(proj, dos, opre, cos, sin, sinks, dw_out_parts):
    s = proj.shape[0]
    nq = min(SWA_QBLOCKS, s // SWA_BLOCK)
    tq = nq * SWA_BLOCK
    steps = _phase_steps(s // tq, 5)
    _, r_out, c_out = dw_out_parts.shape

    def body(sq_ref, sz_ref, sk_ref, sv_ref, dos_ref, opre_ref, cos_ref, sin_ref, sinks_ref, pout_hbm,
             dsq_ref, dsz_ref, dsk_ref, dsv_ref, dsink_ref, gout_hbm, kprev, vprev, cprev, sprev, *reduce_scratch):
        n = pl.program_id(0)

        @pl.when(n == 0)
        def _():
            kprev[...] = jnp.zeros_like(kprev)
            vprev[...] = jnp.zeros_like(vprev)
            cprev[...] = jnp.zeros_like(cprev)
            sprev[...] = jnp.zeros_like(sprev)
            for hd in range(SWA_HEADS):
                dsink_ref[0, hd] = 0.0

        reduce = _Reduce(pout_hbm, gout_hbm, *reduce_scratch)
        phases = (reduce.start, reduce.combine_and_send, reduce.send_joint, reduce.total_and_share, reduce.finish)
        for step, phase in zip(steps, phases):
            pl.when(n == step)(phase)

        lo2, lo1, first_half, prev_mask = _swa_masks()
        prev_mask_b = jnp.where(prev_mask, 1.0, 0.0).astype(BF)
        lo1s = jnp.concatenate([lo1, lo1], axis=0)
        sink = _sink_row(sinks_ref)

        def home(m0, m1):
            t0 = m0 + pltpu.roll(m0, 64, 1)
            t1 = m1 + pltpu.roll(m1, 64, 1)
            return jnp.where(lo2, t0, t1)

        kp, vp, cp_, sp_ = kprev[...], vprev[...], cprev[...], sprev[...]
        for j in range(nq):
            rows = slice(j * SWA_BLOCK, (j + 1) * SWA_BLOCK)
            blk = n * nq + j
            cosb, sinb = cos_ref[rows, :], sin_ref[rows, :]
            kc = _rope(sk_ref[rows, :], cosb, sinb, first_half)
            vc = sv_ref[rows, :]
            kcat = jnp.concatenate([kp, kc], axis=0)
            k_lo, k_hi = _kv_variants(kcat, lo2)
            kt_lo, kt_hi = _kv_variants_t(kcat)
            v_lo, v_hi = _kv_variants(jnp.concatenate([vp, vc], axis=0), lo2)
            qg32 = _swa_queries(sq_ref, rows, cosb, sinb, first_half)
            qg = [q.astype(BF) for q in qg32]
            ex, es, inv = _swa_softmax(_swa_scores(qg, k_lo, k_hi), prev_mask, _no_prev_bias(n) if j == 0 else None, sink)
            pr, ps = ex * inv, es * inv

            dog32 = []
            for g in range(2):
                parts = []
                for i in range(2):
                    ls = slice((2 * g + i) * LANES, (2 * g + i + 1) * LANES)
                    sz = sz_ref[rows, ls]
                    sg = _sigmoid(sz)
                    dos_p = dos_ref[rows, ls]
                    dsz_ref[rows, ls] = (dos_p * opre_ref[rows, ls] * (sg * (1.0 + sz * (1.0 - sg)))).astype(dsz_ref.dtype)
                    parts.append(dos_p * (sz * sg))
                dog32.append(jnp.concatenate(parts, axis=0))
            dog = [t.astype(BF) for t in dog32]
            dpr = _merge_band(jnp.concatenate([_dot(v_lo[0], dog[0], NT), _dot(v_hi[0], dog[0], NT),
                                               _dot(v_lo[1], dog[1], NT), _dot(v_hi[1], dog[1], NT)], axis=1), prev_mask)
            rd = jnp.sum(pr * dpr, axis=0, keepdims=True)
            ds = _split_band((pr * (dpr - rd)).astype(BF), prev_mask_b)
            prb = _split_band(pr.astype(BF), prev_mask_b)
            sink_term = ps * rd
            for r, hd in enumerate(_SWA_COL_HEADS):
                dsink_ref[0, hd] += -jnp.sum(sink_term[:, r * SWA_BLOCK:(r + 1) * SWA_BLOCK])

            dk_g, dv_g = [], []
            for g in range(2):
                c0, c1, c2 = 512 * g, 512 * g + 256, 512 * g + 512
                dq = (_dot(kt_lo[g], ds[:, c0:c1]) + _dot(kt_hi[g], ds[:, c1:c2])).T
                for i in range(2):
                    ls = slice((2 * g + i) * LANES, (2 * g + i + 1) * LANES)
                    dsq_ref[rows, ls] = _rope_t(dq[i * SWA_BLOCK:(i + 1) * SWA_BLOCK] * 0.125, cosb, sinb,
                                                first_half).astype(dsq_ref.dtype)
                q_split = jnp.concatenate([jnp.where(lo1s, qg32[g], 0.0), jnp.where(lo1s, 0.0, qg32[g])], axis=0).astype(BF)
                do_split = jnp.concatenate([jnp.where(lo1s, dog32[g], 0.0), jnp.where(lo1s, 0.0, dog32[g])], axis=0).astype(BF)
                dk_g.append(_dot(ds[:, c0:c2], q_split))
                dv_g.append(_dot(prb[:, c0:c2], do_split))
            dk = home(dk_g[0], dk_g[1])
            dv = home(dv_g[0], dv_g[1])
            cur = pl.ds(pl.multiple_of(blk * SWA_BLOCK, SWA_BLOCK), SWA_BLOCK)
            dsk_ref[cur, :] = _rope_t(dk[SWA_BLOCK:], cosb, sinb, first_half)
            dsv_ref[cur, :] = dv[SWA_BLOCK:]
            dk_prev = _rope_t(dk[:SWA_BLOCK], cp_, sp_, first_half)
            dv_prev = dv[:SWA_BLOCK]
            if j == 0:
                @pl.when(n > 0)
                def _():
                    prv = pl.ds(pl.multiple_of((blk - 1) * SWA_BLOCK, SWA_BLOCK), SWA_BLOCK)
                    dsk_ref[prv, :] += dk_prev
                    dsv_ref[prv, :] += dv_prev
            else:
                prv = pl.ds(pl.multiple_of((blk - 1) * SWA_BLOCK, SWA_BLOCK), SWA_BLOCK)
                dsk_ref[prv, :] += dk_prev
                dsv_ref[prv, :] += dv_prev
            kp, vp, cp_, sp_ = kc, vc, cosb, sinb
        kprev[...] = kp
        vprev[...] = vp
        cprev[...] = cp_
        sprev[...] = sp_

    def col(width, off):
        return pl.BlockSpec((tq, width), lambda i: (i, off // width))

    row = pl.BlockSpec((tq, LANES), lambda i: (i, 0))
    wide = pl.BlockSpec((tq, 512), lambda i: (i, 0))
    return pl.pallas_call(
        body, name="swa_bwd", grid=(s // tq,),
        in_specs=[col(512, OFF_SQ), col(512, OFF_SZ), col(LANES, OFF_SK), col(LANES, OFF_SV), wide, wide, row, row,
                  pl.BlockSpec(memory_space=pltpu.SMEM), pl.BlockSpec(memory_space=pl.ANY)],
        out_specs=[wide, wide, _full((s, LANES)), _full((s, LANES)), pl.BlockSpec(memory_space=pltpu.SMEM),
                   pl.BlockSpec(memory_space=pl.ANY)],
        out_shape=[jax.ShapeDtypeStruct((s, 512), BF), jax.ShapeDtypeStruct((s, 512), BF),
                   jax.ShapeDtypeStruct((s, LANES), F32), jax.ShapeDtypeStruct((s, LANES), F32),
                   jax.ShapeDtypeStruct((1, SWA_HEADS), F32), jax.ShapeDtypeStruct((r_out, c_out), F32)],
        scratch_shapes=[pltpu.VMEM((SWA_BLOCK, LANES), F32)] * 4 + _reduce_scratch(r_out, c_out),
        compiler_params=_params(("arbitrary",)),
    )(proj, proj, proj, proj, dos, opre, cos, sin, sinks, dw_out_parts)


def _outproj(og, osw, w_out, x2d, target, gate, g_final):
    s = x2d.shape[0]
    tm = min(512, s)

    def body(og_ref, os_ref, w_ref, x_ref, t_ref, gate_ref, gf_ref,
             dx2_ref, dog_ref, dos_ref, dw_ref, loss_ref, dgf_ref, dgate_ref):
        @pl.when(pl.program_id(0) == 0)
        def _():
            dw_ref[...] = jnp.zeros_like(dw_ref)
            loss_ref[...] = jnp.zeros_like(loss_ref)
            dgf_ref[...] = jnp.zeros_like(dgf_ref)
            dgate_ref[...] = jnp.zeros_like(dgate_ref)

        w = w_ref[...]
        gate, gf = gate_ref[...], gf_ref[...]
        subs = _subtiles(tm)
        ogv = [og_ref[sl, :] for sl in subs]
        osv = [os_ref[sl, :] for sl in subs]
        y = [_dot(ogv[k], w[:512]) + _dot(osv[k], w[512:]) for k in range(len(subs))]
        dys = []
        for k, sl in enumerate(subs):
            x2 = x_ref[sl, :] + gate * y[k]
            r = lax.rsqrt(jnp.mean(x2 * x2, axis=-1, keepdims=True) + RMS_EPS)
            xn = x2 * r
            err = xn * gf - t_ref[sl, :]
            loss_ref[...] += 0.5 * jnp.sum(jnp.mean(err * err, axis=-1, keepdims=True), axis=0, keepdims=True)
            dyf = err * (1.0 / D_MODEL)
            dgf_ref[...] += jnp.sum(dyf * xn, axis=0, keepdims=True)
            t = dyf * gf
            dx2 = r * (t - xn * jnp.mean(t * xn, axis=-1, keepdims=True))
            dx2_ref[sl, :] = dx2
            dgate_ref[...] += jnp.sum(dx2 * y[k], axis=0, keepdims=True)
            dys.append((dx2 * gate).astype(BF))
            dmix = _dot(dys[k], w, NT)
            dog_ref[sl, :] = dmix[:, :512]
            dos_ref[sl, :] = dmix[:, 512:]
        dy = jnp.concatenate(dys, axis=0)
        dw_ref[:512, :] += _dot(og_ref[...], dy, TN)
        dw_ref[512:, :] += _dot(os_ref[...], dy, TN)

    half = pl.BlockSpec((tm, 512), lambda i: (i, 0))
    rowb = pl.BlockSpec((tm, D_MODEL), lambda i: (i, 0))
    vec = _full((1, D_MODEL))
    return pl.pallas_call(
        body, name="outproj", grid=(s // tm,),
        in_specs=[half, half, _full((D_MODEL, D_MODEL)), rowb, rowb, vec, vec],
        out_specs=[rowb, half, half, _full((D_MODEL, D_MODEL)), _full((1, 1)), vec, vec],
        out_shape=[jax.ShapeDtypeStruct((s, D_MODEL), F32), jax.ShapeDtypeStruct((s, 512), F32),
                   jax.ShapeDtypeStruct((s, 512), F32), jax.ShapeDtypeStruct((D_MODEL, D_MODEL), F32),
                   jax.ShapeDtypeStruct((1, 1), F32), jax.ShapeDtypeStruct((1, D_MODEL), F32),
                   jax.ShapeDtypeStruct((1, D_MODEL), F32)],
        compiler_params=_params(("arbitrary",)),
    )(og, osw, w_out, x2d, target, gate, g_final)


_PIECES = ((OFF_QK, 512), (OFF_V, 512), (OFF_GZ, 512), (OFF_SQ, 512), (OFF_SZ, 512),
           (OFF_SK, LANES), (OFF_SV, LANES), (OFF_GA, LANES))

_UNPAD_ROWS = ((OFF_QK, 0, 1024),
               (OFF_GA, 1024, GLA_RANK),
               (OFF_GZ, 1040, 1024),
               (OFF_SK, 2064, 256),
               (OFF_SZ, 2320, 512))


def _inproj_bwd(x2d, shift, sc1p, g_norm, w_t, dx2, pieces):
    s = x2d.shape[0]
    tm = min(512, s)
    nsteps = s // tm

    def body(x_ref, sh_ref, sc_ref, g_ref, w_hbm, dx2_ref, *rest):
        piece_refs = rest[:len(_PIECES)]
        gx_ref, dw_hbm, dsh_ref, dsc_ref, dg_ref, w_vm, dw_vm, in_sems, out_sems = rest[len(_PIECES):]
        i = pl.program_id(0)

        @pl.when(i == 0)
        def _():
            loads = _load_w_padded(w_hbm, w_vm, in_sems)
            dw_vm[...] = jnp.zeros_like(dw_vm)
            dsh_ref[...] = jnp.zeros_like(dsh_ref)
            dsc_ref[...] = jnp.zeros_like(dsc_ref)
            dg_ref[...] = jnp.zeros_like(dg_ref)
            for cp in loads:
                cp.wait()

        g, sc1p_v, shift_v = g_ref[...], sc_ref[...], sh_ref[...]
        subs = _subtiles(tm)
        dhs = []
        for sl in subs:
            dh = None
            for (off, width), pr in zip(_PIECES, piece_refs):
                part = _dot(pr[sl, :].astype(BF), w_vm[off:off + width, :])
                dh = part if dh is None else dh + part
            dhs.append(dh)
        norm = [_modnorm(x_ref[sl, :], g, sc1p_v, shift_v) for sl in subs]
        hb = jnp.concatenate([h.astype(BF) for _, _, h in norm], axis=0)
        for (off, width), pr in zip(_PIECES, piece_refs):
            dw_vm[off:off + width, :] += _dot(pr[...].astype(BF), hb, TN)
        for sl, (xn, r, _), dh in zip(subs, norm, dhs):
            dsh_ref[...] += jnp.sum(dh, axis=0, keepdims=True)
            dsc_ref[...] += jnp.sum(dh * (xn * g), axis=0, keepdims=True)
            dg_ref[...] += jnp.sum(dh * xn * sc1p_v, axis=0, keepdims=True)
            dxn = dh * g * sc1p_v
            gx_ref[sl, :] = dx2_ref[sl, :] + r * (dxn - xn * jnp.mean(dxn * xn, axis=-1, keepdims=True))

        @pl.when(i == nsteps - 1)
        def _():
            copies = [pltpu.make_async_copy(dw_vm.at[src:src + n], dw_hbm.at[dst:dst + n], out_sems.at[k])
                      for k, (src, dst, n) in enumerate(_UNPAD_ROWS)]
            for cp in copies:
                cp.start()
            for cp in copies:
                cp.wait()

    rowb = pl.BlockSpec((tm, D_MODEL), lambda i: (i, 0))
    vec = _full((1, D_MODEL))
    anyspec = pl.BlockSpec(memory_space=pl.ANY)
    piece_specs = [pl.BlockSpec((tm, width), lambda i: (i, 0)) for _, width in _PIECES]
    return pl.pallas_call(
        body, name="inproj_bwd", grid=(nsteps,),
        in_specs=[rowb, vec, vec, vec, anyspec, rowb] + piece_specs,
        out_specs=[rowb, anyspec, vec, vec, vec],
        out_shape=[jax.ShapeDtypeStruct((s, D_MODEL), F32), jax.ShapeDtypeStruct((D_IN, D_MODEL), F32),
                   jax.ShapeDtypeStruct((1, D_MODEL), F32), jax.ShapeDtypeStruct((1, D_MODEL), F32),
                   jax.ShapeDtypeStruct((1, D_MODEL), F32)],
        scratch_shapes=[pltpu.VMEM((D_PAD, D_MODEL), BF), pltpu.VMEM((D_PAD, D_MODEL), F32),
                        pltpu.SemaphoreType.DMA((len(_UNPAD_ROWS),)), pltpu.SemaphoreType.DMA((len(_UNPAD_ROWS),))],
        compiler_params=_params(("arbitrary",)),
    )(x2d, shift, sc1p, g_norm, w_t, dx2, *pieces)


def _adam(w, g, m, v):
    m2 = ADAM_B1 * m + (1.0 - ADAM_B1) * g
    v2 = ADAM_B2 * v + (1.0 - ADAM_B2) * (g * g)
    m_hat = m2 / (1.0 - ADAM_B1 ** ADAM_STEP)
    v_hat = v2 / (1.0 - ADAM_B2 ** ADAM_STEP)
    delta = -ADAM_LR * (m_hat / (jnp.sqrt(v_hat) + ADAM_EPS) + ADAM_WD * w)
    return delta, m2, v2


def _adamw_t(w3, g_window, m3, v3, name):
    rr, _, cc = w3.shape
    parts = [slice(q * (cc // 4), (q + 1) * (cc // 4)) for q in range(4)]
    starts = sorted({(rr * j) % 8 for j in range(4)})

    def body(w_hbm, gw_hbm, m_hbm, v_hbm, d_hbm, m2_hbm, v2_hbm, g3_hbm,
             w_vm, m_vm, v_vm, gw_vm, d_vm, m2_vm, v2_vm, g_vm, in_sems, out_sems):
        start = lax.rem(rr * (2 * lax.axis_index("x") + lax.axis_index("y")), 8)
        ins = ((w_hbm, w_vm), (m_hbm, m_vm), (v_hbm, v_vm))
        outs = ((d_vm, d_hbm), (m2_vm, m2_hbm), (v2_vm, v2_hbm), (g_vm, g3_hbm))
        loads = [[pltpu.make_async_copy(src.at[:, 0, p], dst.at[:, p], in_sems.at[4 * q + k]) for k, (src, dst) in enumerate(ins)]
                 + [pltpu.make_async_copy(gw_hbm.at[:, p], gw_vm.at[:, p], in_sems.at[4 * q + 3])]
                 for q, p in enumerate(parts)]
        stores = [[pltpu.make_async_copy(src.at[:, p], dst.at[:, 0, p], out_sems.at[4 * q + k]) for k, (src, dst) in enumerate(outs)]
                  for q, p in enumerate(parts)]
        for group in loads:
            for cp in group:
                cp.start()
        for q, p in enumerate(parts):
            for cp in loads[q]:
                cp.wait()
            g = gw_vm[starts[0]:starts[0] + rr, p]
            for o in starts[1:]:
                g = jnp.where(start == o, gw_vm[o:o + rr, p], g)
            g_vm[:, p] = g
            d_vm[:, p], m2_vm[:, p], v2_vm[:, p] = _adam(w_vm[:, p], g, m_vm[:, p], v_vm[:, p])
            for cp in stores[q]:
                cp.start()
        for group in stores:
            for cp in group:
                cp.wait()

    hbm = pl.BlockSpec(memory_space=pl.ANY)
    return pl.pallas_call(
        body, name=name, grid=(1,), in_specs=[hbm] * 4,
        out_specs=[hbm] * 4, out_shape=[jax.ShapeDtypeStruct((rr, 1, cc), F32)] * 4,
        scratch_shapes=[pltpu.VMEM((rr, cc), F32)] * 3 + [pltpu.VMEM(g_window.shape, F32)] + [pltpu.VMEM((rr, cc), F32)] * 4
        + [pltpu.SemaphoreType.DMA((16,)), pltpu.SemaphoreType.DMA((16,))],
        compiler_params=_params(("arbitrary",)),
    )(w3, g_window, m3, v3)


def _small_update(parts, weights, moms, vels):
    n = len(weights)

    def body(*refs):
        p_refs, w_refs, m_refs, v_refs = refs[:n + 1], refs[n + 1:2 * n + 1], refs[2 * n + 1:3 * n + 1], refs[3 * n + 1:4 * n + 1]
        outs = refs[4 * n + 1:]
        for i in range(n):
            g = p_refs[i][0]
            for d in range(1, 8):
                g = g + p_refs[i][d]
            delta, m2, v2 = _adam(w_refs[i][...], g, m_refs[i][...], v_refs[i][...])
            outs[4 * i][...] = g
            outs[4 * i + 1][...] = delta
            outs[4 * i + 2][...] = m2
            outs[4 * i + 3][...] = v2
        tot = p_refs[n][0]
        for d in range(1, 8):
            tot = tot + p_refs[n][d]
        outs[4 * n][...] = tot

    out_shape = []
    for w in weights:
        out_shape += [jax.ShapeDtypeStruct(w.shape, F32)] * 4
    out_shape.append(jax.ShapeDtypeStruct(parts[n].shape[1:], F32))
    return pl.pallas_call(body, name="small_update", out_shape=out_shape, compiler_params=_params())(
        *parts, *weights, *moms, *vels)


def _rows8(a):
    flat = a.reshape(-1)
    rows = -(-flat.shape[0] // LANES)
    rows8 = -(-rows // 8) * 8
    flat = jnp.pad(flat, (0, rows8 * LANES - flat.shape[0]))
    return flat.reshape(rows8, LANES)


def kernel(x, c, positions, w_ada, b_ada, g_norm, w_in, w_decay, b_decay, g_gla_head, sinks, w_out, g_final, loss_target, m_w_ada, m_b_ada, m_g_norm, m_w_in, m_w_decay, m_b_decay, m_g_gla_head, m_sinks, m_w_out, m_g_final, v_w_ada, v_b_ada, v_g_norm, v_w_in, v_w_decay, v_b_decay, v_g_gla_head, v_sinks, v_w_out, v_g_final):
    ax, ay, ac = lax.axis_index("x"), lax.axis_index("y"), lax.axis_index("c")
    chip = 2 * ax + ay
    dev = 2 * chip + ac
    s = x.shape[1]
    x2d = x[0]
    target = loss_target[0]
    w_ada2, w_out2, w_dec2 = w_ada[0], w_out[0], w_decay[0]
    w_in_t = w_in[0].T
    ada_cols = w_ada2.shape[1]
    in_cols = w_in_t.shape[0]
    out_rows = w_out2.shape[0]
    half = D_MODEL // 2

    cw = jnp.concatenate([c.reshape(8, LANES), w_dec2.reshape(8, LANES)], axis=0)
    inv_freq = 1.0 / (ROPE_THETA ** (jnp.arange(0, 64, 2, dtype=F32) / 64))
    room = _w_window(in_cols) - in_cols
    win_window = lax.dynamic_slice(jnp.pad(w_in_t, ((room, room), (0, 0))), (room - (in_cols * chip) % W_TILE, ac * half),
                                   (_w_window(in_cols), half)).astype(BF)
    win_edges = jnp.stack([win_window[:W_TILE], win_window[-W_TILE:]])
    first, mod_all, w_t, cos, sin = _prologue(
        cw, w_ada2, b_ada.reshape(4, ada_cols), win_window, win_edges, in_cols, positions.reshape(s // LANES, LANES), jnp.tile(inv_freq, 4).reshape(1, LANES))

    first = first.reshape(8, 2, 8, LANES)
    c_all = first[:, 0].reshape(8, D_MODEL)
    w_dec_full = first[0::2, 1].reshape(4, GLA_RANK, 64).transpose(1, 0, 2).reshape(GLA_RANK, 256)
    mod = mod_all.reshape(4, 2, 8, ada_cols)[:, 0]
    mod = lax.dynamic_slice(mod, (0, dev, 0), (4, 1, ada_cols)).reshape(1, 4 * ada_cols)
    shift, sc1p, gate = mod[:, :D_MODEL], 1.0 + mod[:, D_MODEL:2 * D_MODEL], mod[:, 2 * D_MODEL:]
    wdecp = jnp.pad(w_dec_full, ((0, LANES - GLA_RANK), (0, 0))).astype(BF)

    proj = _inproj_fwd(x2d, shift, sc1p, g_norm, w_t)
    og, o_gla, sprev = _gla_fwd(proj, wdecp, b_decay, g_gla_head)
    osw, o_swa, w_out_all = _swa_fwd(proj, cos, sin, sinks, w_out2)
    w_out_all = w_out_all.reshape(D_MODEL, D_MODEL)
    dx2, dog, dos, dw_out, loss_p, dgf, dgate = _outproj(og, osw, w_out_all, x2d, target, gate, g_final.reshape(1, D_MODEL))
    dsq, dsz, dsk, dsv, dsinks, g_w_out = _swa_bwd(proj, dos, o_swa, cos, sin, sinks, dw_out.reshape(4, out_rows, D_MODEL))
    dqk, dv, dgz, dga, dwdp, dbd, dgg = _gla_bwd(proj, dog, o_gla, sprev, wdecp, b_decay, g_gla_head)
    pieces = (dqk, dv, dgz, dsq, dsz, dsk, dsv, dga)
    gx, dw_in_t, dshift, dscale, dgn = _inproj_bwd(x2d, shift, sc1p, g_norm, w_t, dx2, pieces)

    segs = [jnp.concatenate([dshift, dscale, dgate], axis=1), dgn, dgf, dwdp[:GLA_RANK], dbd, dgg, dsinks, loss_p]
    packed = [_rows8(a) for a in segs]
    offs = [0]
    for a in packed:
        offs.append(offs[-1] + a.shape[0])
    (g_window, small, g_w_ada, d_w_ada, nm_w_ada, nv_w_ada, d_w_out, nm_w_out, nv_w_out) = _epilogue(
        dw_in_t, jnp.concatenate(packed, axis=0), c_all, (w_ada2, m_w_ada[0], v_w_ada[0]),
        (w_out2, g_w_out, m_w_out[0], v_w_out[0]), offs[0])

    def seg(i, size):
        return small[:, offs[i]:offs[i + 1]].reshape(8, -1)[:, :size]

    dmod_all = seg(0, 3 * D_MODEL)
    dwd_all = lax.dynamic_slice(seg(3, GLA_RANK * 256).reshape(8, GLA_RANK, 256), (0, 0, chip * 64), (8, GLA_RANK, 64))
    parts = [dmod_all.reshape(8, 1, 3 * D_MODEL), seg(1, D_MODEL).reshape(8, 1, D_MODEL), dwd_all,
             seg(4, 256).reshape(8, 1, 256), seg(5, 512).reshape(8, 1, 512), seg(6, SWA_HEADS).reshape(8, 1, SWA_HEADS),
             seg(2, D_MODEL).reshape(8, 1, D_MODEL), seg(7, LANES).reshape(8, 1, LANES)]
    smalls = _small_update(
        parts,
        [b_ada, g_norm, w_dec2, b_decay, g_gla_head, sinks, g_final.reshape(1, D_MODEL)],
        [m_b_ada, m_g_norm, m_w_decay[0], m_b_decay, m_g_gla_head, m_sinks, m_g_final.reshape(1, D_MODEL)],
        [v_b_ada, v_g_norm, v_w_decay[0], v_b_decay, v_g_gla_head, v_sinks, v_g_final.reshape(1, D_MODEL)])
    (g_b_ada, d_b_ada, nm_b_ada, nv_b_ada, g_gn, d_gn, nm_gn, nv_gn, g_wd, d_wd, nm_wd, nv_wd,
     g_bd, d_bd, nm_bd, nv_bd, g_gg, d_gg, nm_gg, nv_gg, g_sk, d_sk, nm_sk, nv_sk,
     g_gf, d_gf, nm_gf, nv_gf, loss_row) = smalls
    loss = loss_row[0, 0]

    to3 = lambda a: jnp.transpose(a, (2, 0, 1))
    from3 = lambda a: jnp.transpose(a, (1, 2, 0))[0]
    d3, nm3, nv3, g3 = _adamw_t(to3(w_in), g_window, to3(m_w_in), to3(v_w_in), "adamw_w_in")
    g_w_in, d_w_in, nm_w_in, nv_w_in = from3(g3), from3(d3), from3(nm3), from3(nv3)

    flat = lambda a: a.reshape(D_MODEL)
    grads = [g_w_ada[None], g_b_ada, g_gn, g_w_in[None], g_wd[None], g_bd, g_gg, g_sk, g_w_out[None], flat(g_gf)]
    deltas = [d_w_ada[None], d_b_ada, d_gn, d_w_in[None], d_wd[None], d_bd, d_gg, d_sk, d_w_out[None], flat(d_gf)]
    new_m = [nm_w_ada[None], nm_b_ada, nm_gn, nm_w_in[None], nm_wd[None], nm_bd, nm_gg, nm_sk, nm_w_out[None], flat(nm_gf)]
    new_v = [nv_w_ada[None], nv_b_ada, nv_gn, nv_w_in[None], nv_wd[None], nv_bd, nv_gg, nv_sk, nv_w_out[None], flat(nv_gf)]
    return (loss, gx[None], *grads, *deltas, *new_m, *new_v)
```

```python
import jax
import jax.numpy as jnp
from jax import lax
from jax.experimental import pallas as pl
from jax.experimental.pallas import tpu as pltpu

F32 = jnp.float32
BF = jnp.bfloat16

D_MODEL = 1024
GLA_HEADS = 4
GLA_DK = 64
GLA_CHUNK = 64
GLA_RANK = 16
GLA_TAU = 16.0
GLA_SUB = 256
GLA_ROWS_FWD = 1024
GLA_ROWS_BWD = 512
SWA_HEADS = 8
SWA_BLOCK = 128
SWA_QBLOCKS_FWD = 8
SWA_QBLOCKS = 8
RMS_EPS = 1e-6
ROPE_THETA = 10000.0

OFF_QK, OFF_V, OFF_GZ, OFF_SQ, OFF_SZ, OFF_SK, OFF_SV, OFF_GA = 0, 512, 1024, 1536, 2048, 2560, 2688, 2816
D_PAD = 2944
D_IN = 2832
LANES = 128
VMEM_LIMIT = 56 * 1024 * 1024

ADAM_LR, ADAM_B1, ADAM_B2, ADAM_EPS, ADAM_WD, ADAM_STEP = 0.001, 0.9, 0.999, 1e-08, 0.01, 10

NT = (((1,), (1,)), ((), ()))
TN = (((0,), (0,)), ((), ()))
MESH = pl.DeviceIdType.MESH


def _dot(a, b, dims=None):
    if dims is None:
        return jnp.dot(a, b, preferred_element_type=F32)
    return lax.dot_general(a, b, dims, preferred_element_type=F32)


def _sigmoid(x):
    return 1.0 / (1.0 + jnp.exp(-x))


def _params(sem=None):
    return pltpu.CompilerParams(dimension_semantics=sem, vmem_limit_bytes=VMEM_LIMIT)


def _full(shape):
    return pl.BlockSpec(shape, lambda i: (0,) * len(shape))


def _subtiles(rows, size=256):
    size = min(size, rows)
    return [slice(k * size, (k + 1) * size) for k in range(rows // size)]


WEIGHT_CHUNKS = 4


def _gather_sems(chunks=1):
    return [pltpu.SemaphoreType.DMA((7 * chunks,)), pltpu.SemaphoreType.DMA((7 * chunks,)), pltpu.SemaphoreType.DMA]


_GATHER_SEMS = _gather_sems()


class _Gather:
    def __init__(self, x_ref, out_ref, send_sems, recv_sems, local_sem, slab=None, chunks=1):
        self.slab_of = slab
        self.chunks = chunks
        self.width = x_ref.shape[-1] // chunks
        x, y, c = lax.axis_index("x"), lax.axis_index("y"), lax.axis_index("c")
        self.me, self.sibling, self.c = (x, y, c), (x, y, 1 - c), c
        self.xn, self.yn, self.dg = (1 - x, y), (x, 1 - y), (1 - x, 1 - y)
        self.pass_from = (lax.rem(x + 1 - c, 2), lax.rem(y + c, 2))
        self.pass_to = (lax.rem(x + c, 2), lax.rem(y + 1 - c, 2))
        self.x_ref, self.out_ref, self.send_sems, self.recv_sems = x_ref, out_ref, send_sems, recv_sems
        self.mine = pltpu.make_async_copy(x_ref, self._slab(*self.me), local_sem)

    def _slab(self, px, py, pc):
        if self.slab_of is not None:
            return self.slab_of(self.out_ref, px, py, pc)
        return self.out_ref.at[4 * px + 2 * py + pc]

    def _part(self, ref, q):
        if self.chunks == 1:
            return ref
        lanes = slice(q * self.width, (q + 1) * self.width)
        return ref.at[(slice(None),) * (len(ref.shape) - 1) + (lanes,)]

    def _copy(self, k, q, blk, to, src=None):
        i = k * self.chunks + q
        return pltpu.make_async_remote_copy(
            src_ref=self._part(self._slab(*blk) if src is None else src, q), dst_ref=self._part(self._slab(*blk), q),
            send_sem=self.send_sems.at[i], recv_sem=self.recv_sems.at[i], device_id=to, device_id_type=MESH)

    def _sends(self, q):
        c = self.c
        return [self._copy(0, q, self.me, self.sibling, src=self.x_ref),
                self._copy(1, q, self.me, (*self.xn, c), src=self.x_ref),
                self._copy(2, q, self.me, (*self.yn, c), src=self.x_ref),
                self._copy(3, q, (*self.pass_from, c), (*self.pass_to, c)),
                self._copy(4, q, (*self.xn, c), self.sibling),
                self._copy(5, q, (*self.yn, c), self.sibling),
                self._copy(6, q, (*self.dg, c), self.sibling)]

    def start(self):
        self.mine.start()
        for q in range(self.chunks):
            sends = self._sends(q)
            for k in (1, 2, 0):
                sends[k].start()

    def pass_on(self, only=None):
        for q in range(self.chunks) if only is None else (only,):
            sends = self._sends(q)
            self._copy(1, q, (*self.xn, self.c), self.me).wait_recv()
            self._copy(2, q, (*self.yn, self.c), self.me).wait_recv()
            for k in (3, 4, 5):
                sends[k].start()

    def relay_diagonal(self, only=None):
        for q in range(self.chunks) if only is None else (only,):
            self._copy(3, q, (*self.dg, self.c), self.me).wait_recv()
            self._sends(q)[6].start()

    def relay(self):
        self.pass_on()
        self.relay_diagonal()

    def finish(self):
        c = self.c
        for q in range(self.chunks):
            self._copy(0, q, self.sibling, self.me).wait_recv()
            for k, chip in ((4, self.xn), (5, self.yn), (6, self.dg)):
                self._copy(k, q, (*chip, 1 - c), self.me).wait_recv()
            for cp in self._sends(q):
                cp.wait_send()
        self.mine.wait()


def _prologue(cw, w_ada, b_chips, win_window, win_edges, n_in, pos_rows, inv_freq):
    s = pos_rows.shape[0] * LANES
    rt = min(512, s)
    inner = win_window.shape[0] - 2 * W_TILE
    starts = [(n_in * j) // W_TILE * W_TILE for j in range(4)]
    edge_rows = starts + [starts[3] + inner + W_TILE]
    assert all(starts[j] + inner + W_TILE == edge_rows[j + 1] for j in range(4))

    def body(cw_ref, wada_hbm, b_ref, hin_ref, hedge_ref, pos_ref, f_ref,
             first_ref, mod_ref, win_ref, cos_hbm, sin_hbm,
             mod_blk, cos_ref, sin_ref, wada_ref, edge_ref, tile_ref, table_sems, local_sems, tile_sems, *sems):
        fetch_w = pltpu.make_async_copy(wada_hbm, wada_ref, local_sems.at[0])
        fetch_w.start()
        g_c = _Gather(cw_ref, first_ref, *sems[0:3])
        half_lanes = hin_ref.shape[1]

        def lanes_of(pc):
            return pl.ds(pl.multiple_of(pc * half_lanes, half_lanes), half_lanes)

        def inner_rows(px, py):
            return pl.ds(pl.multiple_of((n_in * (2 * px + py)) // W_TILE * W_TILE + W_TILE, W_TILE), inner)

        g_in = _Gather(hin_ref.at[pl.ds(W_TILE, inner), :], win_ref, *sems[3:6], chunks=WEIGHT_CHUNKS,
                       slab=lambda ref, px, py, pc: ref.at[inner_rows(px, py), lanes_of(pc)])
        g_mod = _Gather(mod_blk, mod_ref, *sems[6:9])
        g_edge = _Gather(hedge_ref, edge_ref, *sems[9:12],
                         slab=lambda ref, px, py, pc: ref.at[2 * px + py, :, :, lanes_of(pc)])
        g_c.start()
        g_edge.start()
        g_in.start()
        g_c.relay()
        g_edge.relay()
        g_c.finish()
        c_rows = [jnp.concatenate([first_ref[d, r:r + 1, :] for r in range(8)], axis=1) for d in range(8)]
        c_all = jnp.concatenate(c_rows, axis=0)
        sc = (c_all * _sigmoid(c_all)).astype(BF)
        fetch_w.wait()
        chip = 2 * lax.axis_index("x") + lax.axis_index("y")
        mod_blk[...] = _dot(sc, wada_ref[...].astype(BF)) + b_ref[pl.ds(chip, 1), :]
        g_mod.start()

        def rope_rows(i, carry):
            rows = pl.ds(pl.multiple_of(i * rt, rt), rt)
            cols = [jnp.transpose(jnp.broadcast_to(pos_ref[pl.ds(i * (rt // LANES) + b, 1), :].astype(F32), (LANES, LANES)))
                    for b in range(rt // LANES)]
            ang = jnp.concatenate(cols, axis=0) * f_ref[...]
            lane = lax.broadcasted_iota(jnp.int32, ang.shape, 1)
            cos_ref[rows, :] = jnp.cos(ang)
            sn = jnp.sin(ang)
            sin_ref[rows, :] = jnp.where((lane % 64) < 32, -sn, sn)
            pltpu.make_async_copy(cos_ref.at[rows, :], cos_hbm.at[rows, :], table_sems.at[0]).start()
            pltpu.make_async_copy(sin_ref.at[rows, :], sin_hbm.at[rows, :], table_sems.at[1]).start()
            return carry

        tiles = []

        def edge_tiles():
            g_edge.finish()
            row = lax.broadcasted_iota(jnp.int32, tile_ref.shape[1:], 0)
            for k, at in enumerate(edge_rows):
                last = edge_ref[max(k - 1, 0), 1].astype(F32)
                first = edge_ref[min(k, 3), 0].astype(F32)
                cut = W_TILE if k == 4 else (n_in * k) % W_TILE
                tile_ref[k] = jnp.where(row < cut, last, first).astype(tile_ref.dtype)
                tiles.append(pltpu.make_async_copy(tile_ref.at[k], win_ref.at[at:at + W_TILE, :], tile_sems.at[k]))
                tiles[-1].start()

        waits = ([lambda q=q: g_in.pass_on(q) for q in range(WEIGHT_CHUNKS)] + [edge_tiles]
                 + [lambda q=q: g_in.relay_diagonal(q) for q in range(WEIGHT_CHUNKS)] + [g_mod.relay])
        steps = s // rt
        lead = steps // 4
        per_wait = max((steps - lead) // len(waits), 1)
        lax.fori_loop(0, lead, rope_rows, 0)
        done = lead
        for wait in waits:
            wait()
            nxt = min(done + per_wait, steps)
            lax.fori_loop(done, nxt, rope_rows, 0)
            done = nxt
        lax.fori_loop(done, steps, rope_rows, 0)
        g_in.finish()
        g_mod.finish()
        for cp in tiles:
            cp.wait()
        pltpu.make_async_copy(cos_ref, cos_hbm, table_sems.at[0]).wait()
        pltpu.make_async_copy(sin_ref, sin_hbm, table_sems.at[1]).wait()

    vm = pl.BlockSpec(memory_space=pltpu.VMEM)
    hbm = pl.BlockSpec(memory_space=pl.ANY)
    half_lanes = win_window.shape[1]
    return pl.pallas_call(
        body, name="prologue",
        out_shape=[jax.ShapeDtypeStruct((8,) + cw.shape, F32), jax.ShapeDtypeStruct((8, 8, w_ada.shape[1]), F32),
                   jax.ShapeDtypeStruct((4 * n_in, 2 * half_lanes), win_window.dtype),
                   jax.ShapeDtypeStruct((s, LANES), F32), jax.ShapeDtypeStruct((s, LANES), F32)],
        in_specs=[vm, hbm, vm, hbm, vm, vm, vm], out_specs=[vm, vm, hbm, hbm, hbm],
        scratch_shapes=[pltpu.VMEM((8, w_ada.shape[1]), F32), pltpu.VMEM((s, LANES), F32), pltpu.VMEM((s, LANES), F32),
                        pltpu.VMEM(w_ada.shape, F32),
                        pltpu.VMEM((4, 2, W_TILE, 2 * half_lanes), win_window.dtype),
                        pltpu.VMEM((5, W_TILE, 2 * half_lanes), win_window.dtype),
                        pltpu.SemaphoreType.DMA((2,)), pltpu.SemaphoreType.DMA((1,)), pltpu.SemaphoreType.DMA((5,))]
        + _GATHER_SEMS + _gather_sems(WEIGHT_CHUNKS) + _GATHER_SEMS + _GATHER_SEMS,
        compiler_params=pltpu.CompilerParams(vmem_limit_bytes=VMEM_LIMIT),
    )(cw, w_ada, b_chips, win_window, win_edges, pos_rows, inv_freq)


def _reduce_scratch(rr, cc):
    c2 = cc // 2
    return [pltpu.VMEM((4, rr, c2), F32), pltpu.VMEM((4, rr, c2), F32), pltpu.VMEM((3, rr, c2), BF),
            pltpu.VMEM((2, rr, c2), BF), pltpu.VMEM((rr, c2), BF), pltpu.VMEM((rr, c2), F32),
            pltpu.SemaphoreType.DMA((8 + 3 * WEIGHT_CHUNKS,)), pltpu.SemaphoreType.DMA((8 + 3 * WEIGHT_CHUNKS,)),
            pltpu.SemaphoreType.DMA((5,))]


class _Reduce:
    def __init__(self, p_hbm, out_ref, acc_ref, own_ref, send_ref, land_ref, relay_ref, res_ref,
                 send_sems, recv_sems, local_sems, rows=None):
        x, y, c = lax.axis_index("x"), lax.axis_index("y"), lax.axis_index("c")
        part = (lambda j, ln: p_hbm.at[j, :, ln]) if rows is None else (lambda j, ln: p_hbm.at[rows(j), ln])
        c2 = out_ref.shape[1] // 2
        sibling = (x, y, 1 - c)
        first = (lax.rem(x + 1 - c, 2), lax.rem(y + c, 2))
        second = (lax.rem(x + c, 2), lax.rem(y + 1 - c, 2))
        shards = [2 * first[0] + first[1], 2 * second[0] + second[1], 2 * (1 - x) + (1 - y), 2 * x + y]
        sibling_slot = (1, 0, 2, 3)
        mine = pl.ds(pl.multiple_of(c * c2, c2), c2)
        other = pl.ds(pl.multiple_of((1 - c) * c2, c2), c2)
        self.acc_ref, self.own_ref, self.send_ref, self.land_ref = acc_ref, own_ref, send_ref, land_ref
        self.relay_ref, self.res_ref = relay_ref, res_ref
        self.own = [pltpu.make_async_copy(part(j, mine), own_ref.at[k], local_sems.at[k])
                    for k, j in enumerate(shards)]
        self.swap_out = [pltpu.make_async_remote_copy(
            src_ref=part(j, other), dst_ref=acc_ref.at[sibling_slot[k]], send_sem=send_sems.at[k],
            recv_sem=recv_sems.at[sibling_slot[k]], device_id=sibling, device_id_type=MESH) for k, j in enumerate(shards)]
        self.swap_in = [pltpu.make_async_remote_copy(
            src_ref=part(j, other), dst_ref=acc_ref.at[k], send_sem=send_sems.at[k], recv_sem=recv_sems.at[k],
            device_id=sibling, device_id_type=MESH) for k, j in enumerate(shards)]

        self.lanes = [slice(q * (c2 // WEIGHT_CHUNKS), (q + 1) * (c2 // WEIGHT_CHUNKS)) for q in range(WEIGHT_CHUNKS)]

        def message(m, src, dst, to):
            return [pltpu.make_async_remote_copy(
                src_ref=src.at[:, ln], dst_ref=dst.at[:, ln], send_sem=send_sems.at[8 + m * WEIGHT_CHUNKS + q],
                recv_sem=recv_sems.at[8 + m * WEIGHT_CHUNKS + q], device_id=(*to, c), device_id_type=MESH)
                for q, ln in enumerate(self.lanes)]

        self.direct = message(0, send_ref.at[0], land_ref.at[0], first)
        self.passed = message(1, send_ref.at[1], relay_ref, first)
        self.joint = message(2, send_ref.at[2], land_ref.at[1], second)
        self.put = pltpu.make_async_copy(res_ref, out_ref.at[:, mine], local_sems.at[4])
        self.share = pltpu.make_async_remote_copy(
            src_ref=res_ref, dst_ref=out_ref.at[:, mine], send_sem=send_sems.at[7],
            recv_sem=recv_sems.at[7], device_id=sibling, device_id_type=MESH)

    def start(self):
        for k in (2, 0, 1, 3):
            self.own[k].start()
            self.swap_out[k].start()

    def _combine(self, k):
        self.own[k].wait()
        self.swap_out[k].wait_send()
        self.swap_in[k].wait_recv()
        self.acc_ref[k] = self.acc_ref[k] + self.own_ref[k]

    def combine_and_send(self):
        dt = self.send_ref.dtype
        self._combine(2)
        self.send_ref[1] = self.acc_ref[2].astype(dt)
        for cp in self.passed:
            cp.start()
        self._combine(0)
        self.send_ref[0] = self.acc_ref[0].astype(dt)
        for cp in self.direct:
            cp.start()
        self._combine(1)
        self._combine(3)

    def send_joint(self):
        dt = self.send_ref.dtype
        for q, ln in enumerate(self.lanes):
            self.passed[q].wait_recv()
            self.send_ref[2, :, ln] = (self.acc_ref[1, :, ln] + self.relay_ref[:, ln].astype(F32)).astype(dt)
            self.joint[q].start()

    def total_and_share(self):
        for cp in self.direct + self.joint:
            cp.wait_recv()
        self.res_ref[...] = self.acc_ref[3] + self.land_ref[0].astype(F32) + self.land_ref[1].astype(F32)
        for cp in self.direct + self.passed + self.joint:
            cp.wait_send()
        self.put.start()
        self.share.start()

    def finish(self):
        self.put.wait()
        self.share.wait()


def _shard_window(n):
    return max(-(-(n * (j + 1)) // 8) * 8 - (n * j) // 8 * 8 for j in range(4))


class _LocalUpdate:
    def __init__(self, ins, in_vm, out_vm, outs, in_sems, out_sems):
        self.loads = [pltpu.make_async_copy(a, b, in_sems.at[k]) for k, (a, b) in enumerate(zip(ins, in_vm))]
        self.stores = [pltpu.make_async_copy(a, b, out_sems.at[k]) for k, (a, b) in enumerate(zip(out_vm, outs))]

    def start(self):
        for cp in self.loads:
            cp.start()

    def loaded(self):
        for cp in self.loads:
            cp.wait()

    def store(self):
        for cp in self.stores:
            cp.start()

    def finish(self):
        for cp in self.stores:
            cp.wait()


def _epilogue(dw_in_t, small, c_all, ada, out, dmod_row):
    cc = dw_in_t.shape[1]
    n = dw_in_t.shape[0] // 4
    r_in = _shard_window(n)
    n_red = len(_reduce_scratch(r_in, cc))
    ra, ca = ada[0].shape
    dm_rows = ca // LANES
    tr = min(512, ra)

    def body(pin_hbm, small_ref, c_ref, *rest):
        ada_hbm, out_hbm = rest[0:3], rest[3:7]
        gin_ref, small_all_ref = rest[7:9]
        ada_res, out_res = rest[9:13], rest[13:16]
        scratch = rest[16:]
        red_in = _Reduce(pin_hbm, gin_ref, *scratch[0:n_red],
                         rows=lambda j: pl.ds(pl.multiple_of((n * j) // 8 * 8, 8), r_in))
        gat = _Gather(small_ref, small_all_ref, *scratch[n_red:n_red + 3])
        local = scratch[n_red + 3:]
        ada_in, ada_out, out_in, out_out = local[0:3], local[3:7], local[7:11], local[11:14]
        upd_ada = _LocalUpdate(ada_hbm, ada_in, ada_out, ada_res, local[14], local[15])
        upd_out = _LocalUpdate(out_hbm, out_in, out_out, out_res, local[16], local[17])
        red_in.start()
        gat.start()
        upd_out.start()
        upd_ada.start()
        gat.relay()
        red_in.combine_and_send()
        gat.finish()
        red_in.send_joint()

        upd_out.loaded()
        out_out[0][...], out_out[1][...], out_out[2][...] = _adam(*[r[...] for r in out_in])
        upd_out.store()
        chip = 2 * lax.axis_index("x") + lax.axis_index("y")
        dm = jnp.concatenate(
            [jnp.concatenate([small_all_ref[d, pl.ds(dmod_row + dm_rows * chip + r, 1), :] for r in range(dm_rows)], axis=1)
             for d in range(8)], axis=0)
        cv = c_ref[...]
        sc = jnp.concatenate([cv * _sigmoid(cv), jnp.zeros_like(cv)], axis=0).astype(BF)
        dmb = jnp.concatenate([dm, jnp.zeros_like(dm)], axis=0).astype(BF)
        upd_ada.loaded()
        for r0 in range(0, ra, tr):
            rows = slice(r0, r0 + tr)
            g = _dot(sc[:, rows], dmb, TN)
            ada_out[0][rows, :] = g
            ada_out[1][rows, :], ada_out[2][rows, :], ada_out[3][rows, :] = _adam(
                ada_in[0][rows, :], g, ada_in[1][rows, :], ada_in[2][rows, :])
        upd_ada.store()

        red_in.total_and_share()
        red_in.finish()
        upd_out.finish()
        upd_ada.finish()

    vm = pl.BlockSpec(memory_space=pltpu.VMEM)
    anyspec = pl.BlockSpec(memory_space=pl.ANY)
    ada_buf, out_buf = pltpu.VMEM((ra, ca), F32), pltpu.VMEM(out[0].shape, F32)
    return pl.pallas_call(
        body, name="epilogue",
        out_shape=[jax.ShapeDtypeStruct((r_in, cc), F32), jax.ShapeDtypeStruct((8,) + small.shape, F32)]
        + [jax.ShapeDtypeStruct((ra, ca), F32)] * 4 + [jax.ShapeDtypeStruct(out[0].shape, F32)] * 3,
        in_specs=[anyspec, vm, vm] + [anyspec] * 7, out_specs=[anyspec, vm] + [anyspec] * 7,
        scratch_shapes=_reduce_scratch(r_in, cc) + _GATHER_SEMS + [ada_buf] * 7 + [out_buf] * 7
        + [pltpu.SemaphoreType.DMA((3,)), pltpu.SemaphoreType.DMA((4,)), pltpu.SemaphoreType.DMA((4,)), pltpu.SemaphoreType.DMA((3,))],
        compiler_params=pltpu.CompilerParams(vmem_limit_bytes=VMEM_LIMIT),
    )(dw_in_t, small, c_all, *ada, *out)


def _rope(t, cosb, sinb, first_half):
    partner = jnp.where(first_half, pltpu.roll(t, 96, 1), pltpu.roll(t, 32, 1))
    return t * cosb + partner * sinb


def _rope_t(g, cosb, sinb, first_half):
    gs = g * sinb
    partner = jnp.where(first_half, pltpu.roll(gs, 96, 1), pltpu.roll(gs, 32, 1))
    return g * cosb + partner


def _modnorm(x, g, sc1p, shift):
    r = lax.rsqrt(jnp.mean(x * x, axis=-1, keepdims=True) + RMS_EPS)
    xn = x * r
    return xn, r, (xn * g) * sc1p + shift


W_TILE = 16


def _w_window(n):
    return max(-(-(n * (j + 1)) // W_TILE) * W_TILE - (n * j) // W_TILE * W_TILE for j in range(4))


def _load_w_padded(w_hbm, w_vm, sems):
    copies = [pltpu.make_async_copy(w_hbm.at[ref:ref + n], w_vm.at[pad:pad + n], sems.at[k])
              for k, (pad, ref, n) in enumerate(_UNPAD_ROWS)]
    for cp in copies:
        cp.start()
    w_vm[OFF_GA + GLA_RANK:, :] = jnp.zeros((D_PAD - OFF_GA - GLA_RANK, D_MODEL), w_vm.dtype)
    return copies


def _inproj_fwd(x2d, shift, sc1p, g_norm, w_t):
    s = x2d.shape[0]
    tm = min(1024, s)
    nsteps = s // tm
    slots = 3

    def body(x_hbm, sh_ref, sc_ref, g_ref, w_hbm, o_ref, w_vm, x_buf, sems, x_sems):
        i = pl.program_id(0)

        def fetch(step):
            static = isinstance(step, int)
            slot = step % slots if static else lax.rem(step, slots)
            rows = pl.ds(step * tm if static else pl.multiple_of(step * tm, tm), tm)
            return pltpu.make_async_copy(x_hbm.at[rows, :], x_buf.at[slot], x_sems.at[slot])

        @pl.when(i == 0)
        def _():
            loads = _load_w_padded(w_hbm, w_vm, sems)
            for step in range(min(slots - 1, nsteps)):
                fetch(step).start()
            for cp in loads:
                cp.wait()

        @pl.when(i + (slots - 1) < nsteps)
        def _():
            fetch(i + (slots - 1)).start()

        fetch(i).wait()
        slot = lax.rem(i, slots)
        subs = _subtiles(tm)
        hs = [_modnorm(x_buf[slot, sl, :], g_ref[...], sc_ref[...], sh_ref[...])[2].astype(BF) for sl in subs]
        for sl, h in zip(subs, hs):
            o_ref[sl, :] = _dot(h, w_vm[...], NT)

    vec = _full((1, D_MODEL))
    return pl.pallas_call(
        body, name="inproj_fwd", grid=(nsteps,),
        in_specs=[pl.BlockSpec(memory_space=pl.ANY), vec, vec, vec, pl.BlockSpec(memory_space=pl.ANY)],
        out_specs=pl.BlockSpec((tm, D_PAD), lambda i: (i, 0)),
        out_shape=jax.ShapeDtypeStruct((s, D_PAD), F32),
        scratch_shapes=[pltpu.VMEM((D_PAD, D_MODEL), BF), pltpu.VMEM((slots, tm, D_MODEL), F32),
                        pltpu.SemaphoreType.DMA((len(_UNPAD_ROWS),)), pltpu.SemaphoreType.DMA((slots,))],
        compiler_params=_params(("arbitrary",)),
    )(x2d, shift, sc1p, g_norm, w_t)


def _split3(a):
    hi = a.astype(BF)
    r1 = a - hi.astype(F32)
    mid = r1.astype(BF)
    lo = (r1 - mid.astype(F32)).astype(BF)
    return hi, mid, lo


def _tri_matmul(tri, a):
    hi, mid, lo = _split3(a)
    return _dot(tri, hi) + _dot(tri, mid) + _dot(tri, lo)


def _chunks(tb):
    return [slice(c * GLA_CHUNK, (c + 1) * GLA_CHUNK) for c in range(tb // GLA_CHUNK)]


def _per_chunk_rows(rows, width):
    return jnp.concatenate([jnp.broadcast_to(r, (GLA_CHUNK, width)) for r in rows], axis=0)


def _gla_triangle(tb):
    row = lax.broadcasted_iota(jnp.int32, (tb, tb), 0)
    col = lax.broadcasted_iota(jnp.int32, (tb, tb), 1)
    return (((row // GLA_CHUNK) == (col // GLA_CHUNK)) & (col <= row)).astype(F32)


def _lane_mean(x, ones_b):
    hi = x.astype(BF)
    lo = (x - hi.astype(F32)).astype(BF)
    return (_dot(hi, ones_b) + _dot(lo, ones_b)) * (1.0 / LANES)


def _head(t, h, lo_h):
    blk = t[:, LANES * (h // 2):LANES * (h // 2 + 1)]
    return jnp.where(lo_h, blk, 0.0) if h % 2 == 0 else jnp.where(lo_h, 0.0, blk)


def _gla_block_common(qk, ga, wd, bd, tril_b):
    tb = qk.shape[0]
    q, k = qk[:, :256], qk[:, 256:]
    z = _dot(ga.astype(BF), wd) + bd
    la = (jnp.minimum(z, 0.0) - jnp.log(1.0 + jnp.exp(-jnp.abs(z)))) * (1.0 / GLA_TAU)
    b = _tri_matmul(tril_b, la)
    bls = [b[rs.stop - 1:rs.stop, :] for rs in _chunks(tb)]
    eq = jnp.exp(b)
    ek = jnp.exp(-b)
    f = jnp.exp(_per_chunk_rows(bls, 256) - b)
    return z, eq, ek, f, q * (eq * GLA_DK ** -0.5), k * ek, k * f, bls


def _gla_units(s, rows):
    sub = min(GLA_SUB, s)
    tb = min(rows, s)
    subs = [slice(i * sub, (i + 1) * sub) for i in range(tb // sub)]
    units = [(i, h) for i in range(len(subs)) for h in range(GLA_HEADS)]
    return tb, sub, subs, units


def _gla_fwd(proj, wdecp, bdec, ggla):
    s = proj.shape[0]
    tb, sub, subs, units = _gla_units(s, GLA_ROWS_FWD)
    nch = sub // GLA_CHUNK

    def body(qk_ref, v_ref, gz_ref, ga_ref, wd_ref, bd_ref, gg_ref, tri_ref, og_ref, opre_ref, sprev_ref, st_ref):
        @pl.when(pl.program_id(0) == 0)
        def _():
            st_ref[...] = jnp.zeros_like(st_ref)

        lo_h = lax.broadcasted_iota(jnp.int32, (sub, LANES), 1) < GLA_DK
        tril = tri_ref[...] > 0.5
        tril_b = tri_ref[...].astype(BF)
        ones_b = jnp.ones((LANES, LANES), BF)
        gg, wd, bd = gg_ref[...], wd_ref[...], bd_ref[...]
        chunks = _chunks(sub)
        lanes = [slice(h * LANES, (h + 1) * LANES) for h in range(GLA_HEADS)]
        com = [_gla_block_common(qk_ref[sl, :], ga_ref[sl, :], wd, bd, tril_b) for sl in subs]
        decs = [[jnp.exp(bl) for bl in cm[7]] for cm in com]
        a = {(i, h): _head(com[i][4], h, lo_h).astype(BF) for i, h in units}
        bm = {(i, h): _head(com[i][5], h, lo_h).astype(BF) for i, h in units}
        ktl = {(i, h): _head(com[i][6], h, lo_h).astype(BF) for i, h in units}
        vh = {(i, h): v_ref[subs[i], lanes[h]].astype(BF) for i, h in units}
        sc = {u: _dot(a[u], bm[u], NT) for u in units}
        upd = {u: [_dot(vh[u][rs], ktl[u][rs], TN) for rs in chunks] for u in units}
        p = {u: jnp.where(tril, sc[u], 0.0).astype(BF) for u in units}
        o = {u: _dot(p[u], vh[u]) for u in units}
        states = {}
        for h in range(GLA_HEADS):
            st = st_ref[h]
            for i in range(len(subs)):
                entering = []
                for c in range(nch):
                    entering.append(st)
                    sprev_ref[i * nch + c, h] = st
                    st = st * decs[i][c][:, LANES * (h // 2):LANES * (h // 2 + 1)] + upd[(i, h)][c]
                states[(i, h)] = entering
            st_ref[h] = st
        inter = {u: [_dot(a[u][rs], states[u][c].astype(BF), NT) for c, rs in enumerate(chunks)] for u in units}
        o = {u: o[u] + jnp.concatenate(inter[u], axis=0) for u in units}
        ms = {u: _lane_mean(o[u] * o[u], ones_b) for u in units}
        for i, h in units:
            gzh = gz_ref[subs[i], lanes[h]]
            opre_ref[subs[i], lanes[h]] = o[(i, h)]
            og_ref[subs[i], lanes[h]] = (((o[(i, h)] * lax.rsqrt(ms[(i, h)] + RMS_EPS)) * gg[:, lanes[h]])
                                         * (gzh * _sigmoid(gzh))).astype(og_ref.dtype)

    def col(width, off):
        return pl.BlockSpec((tb, width), lambda i: (i, off // width))

    return pl.pallas_call(
        body, name="gla_fwd", grid=(s // tb,),
        in_specs=[col(512, OFF_QK), col(512, OFF_V), col(512, OFF_GZ), col(LANES, OFF_GA),
                  _full((LANES, 256)), _full((1, 256)), _full((1, 512)), _full((sub, sub))],
        out_specs=[pl.BlockSpec((tb, 512), lambda i: (i, 0)), pl.BlockSpec((tb, 512), lambda i: (i, 0)),
                   pl.BlockSpec((tb // GLA_CHUNK, GLA_HEADS, LANES, LANES), lambda i: (i, 0, 0, 0))],
        out_shape=[jax.ShapeDtypeStruct((s, 512), BF), jax.ShapeDtypeStruct((s, 512), F32),
                   jax.ShapeDtypeStruct((s // GLA_CHUNK, GLA_HEADS, LANES, LANES), F32)],
        scratch_shapes=[pltpu.VMEM((GLA_HEADS, LANES, LANES), F32)],
        compiler_params=_params(("arbitrary",)),
    )(proj, proj, proj, proj, wdecp, bdec, ggla, _gla_triangle(sub))


def _gla_bwd(proj, dog, opre, sprev, wdecp, bdec, ggla):
    s = proj.shape[0]
    tb, sub, subs, units = _gla_units(s, GLA_ROWS_BWD)
    nsub = len(subs)
    nch = sub // GLA_CHUNK
    nb = s // tb

    def body(qk_ref, v_ref, gz_ref, ga_ref, dog_ref, opre_ref, sprev_ref, wd_ref, bd_ref, gg_ref, tri_ref, triu_ref,
             dqk_ref, dv_ref, dgz_ref, dga_ref, dwd_ref, dbd_ref, dgg_ref, dst_ref):
        @pl.when(pl.program_id(0) == 0)
        def _():
            dst_ref[...] = jnp.zeros_like(dst_ref)
            dwd_ref[...] = jnp.zeros_like(dwd_ref)
            dbd_ref[...] = jnp.zeros_like(dbd_ref)
            dgg_ref[...] = jnp.zeros_like(dgg_ref)

        lo_h = lax.broadcasted_iota(jnp.int32, (sub, LANES), 1) < GLA_DK
        tril = tri_ref[...] > 0.5
        tril_b = tri_ref[...].astype(BF)
        triu_b = triu_ref[...].astype(BF)
        ones_b = jnp.ones((LANES, LANES), BF)
        last_row = (lax.broadcasted_iota(jnp.int32, (sub, LANES), 0) % GLA_CHUNK) == GLA_CHUNK - 1
        wd, gg, bd = wd_ref[...], gg_ref[...], bd_ref[...]
        chunks = _chunks(sub)
        lanes = [slice(h * LANES, (h + 1) * LANES) for h in range(GLA_HEADS)]
        blks = [slice(LANES * (h // 2), LANES * (h // 2 + 1)) for h in range(GLA_HEADS)]
        ga = [ga_ref[sl, :] for sl in subs]
        com = [_gla_block_common(qk_ref[sl, :], ga[i], wd, bd, tril_b) for i, sl in enumerate(subs)]
        decs = [[jnp.exp(bl) for bl in cm[7]] for cm in com]
        a = {(i, h): _head(com[i][4], h, lo_h).astype(BF) for i, h in units}
        bm = {(i, h): _head(com[i][5], h, lo_h).astype(BF) for i, h in units}
        ktl = {(i, h): _head(com[i][6], h, lo_h).astype(BF) for i, h in units}
        vh = {(i, h): v_ref[subs[i], lanes[h]].astype(BF) for i, h in units}
        sc = {u: _dot(a[u], bm[u], NT) for u in units}

        o = {(i, h): opre_ref[subs[i], lanes[h]] for i, h in units}
        ms = {u: _lane_mean(o[u] * o[u], ones_b) for u in units}
        gz = {(i, h): gz_ref[subs[i], lanes[h]] for i, h in units}
        dog = {(i, h): dog_ref[subs[i], lanes[h]] for i, h in units}
        sg = {u: _sigmoid(gz[u]) for u in units}
        r = {u: lax.rsqrt(ms[u] + RMS_EPS) for u in units}
        ohat = {u: o[u] * r[u] for u in units}
        sil = {u: gz[u] * sg[u] for u in units}
        for i, h in units:
            u = (i, h)
            dgz_ref[subs[i], lanes[h]] = (dog[u] * (ohat[u] * gg[:, lanes[h]])
                                          * (sg[u] * (1.0 + gz[u] * (1.0 - sg[u])))).astype(dgz_ref.dtype)
            dgg_ref[:, lanes[h]] += jnp.sum(dog[u] * sil[u] * ohat[u], axis=0, keepdims=True)
        dn = {(i, h): dog[(i, h)] * sil[(i, h)] * gg[:, lanes[h]] for i, h in units}
        mdn = {u: _lane_mean(dn[u] * ohat[u], ones_b) for u in units}
        do = {u: (r[u] * (dn[u] - ohat[u] * mdn[u])).astype(BF) for u in units}

        p = {u: jnp.where(tril, sc[u], 0.0).astype(BF) for u in units}
        dpr = {u: _dot(do[u], vh[u], NT) for u in units}
        incr = {u: [_dot(do[u][rs], a[u][rs], TN) for rs in chunks] for u in units}
        dv = {u: _dot(p[u], do[u], TN) for u in units}
        dp = {u: jnp.where(tril, dpr[u], 0.0).astype(BF) for u in units}
        dqd = {u: _dot(dp[u], bm[u]) for u in units}
        dkd = {u: _dot(dp[u], a[u], TN) for u in units}
        st = {(i, h): [sprev_ref[i * nch + c, h] for c in range(nch)] for i, h in units}
        leaving = {}
        for h in range(GLA_HEADS):
            d = dst_ref[h]
            for i in reversed(range(nsub)):
                out = [None] * nch
                for c in reversed(range(nch)):
                    out[c] = d
                    d = d * decs[i][c][:, blks[h]] + incr[(i, h)][c]
                leaving[(i, h)] = out
            dst_ref[h] = d
        lv_b = {u: [leaving[u][c].astype(BF) for c in range(nch)] for u in units}
        dv_s = {u: [_dot(ktl[u][rs], lv_b[u][c], NT) for c, rs in enumerate(chunks)] for u in units}
        dqd_s = {u: [_dot(do[u][rs], st[u][c].astype(BF)) for c, rs in enumerate(chunks)] for u in units}
        dkt_s = {u: [_dot(vh[u][rs], lv_b[u][c]) for c, rs in enumerate(chunks)] for u in units}
        ddec = {u: [jnp.sum(leaving[u][c] * st[u][c], axis=0, keepdims=True) for c in range(nch)] for u in units}
        for i, h in units:
            dv_ref[subs[i], lanes[h]] = (dv[(i, h)] + jnp.concatenate(dv_s[(i, h)], axis=0)).astype(dv_ref.dtype)
        dqd = {u: dqd[u] + jnp.concatenate(dqd_s[u], axis=0) for u in units}
        dkt = {u: jnp.concatenate(dkt_s[u], axis=0) for u in units}

        db = []
        for i, sl in enumerate(subs):
            _, eq, ek, f, qd, kd, kt, _ = com[i]
            parts = []
            for pair in range(GLA_HEADS // 2):
                blk, u0, u1 = blks[2 * pair], (i, 2 * pair), (i, 2 * pair + 1)
                dqd_b, dkd_b, dkt_b = dqd[u0] + dqd[u1], dkd[u0] + dkd[u1], dkt[u0] + dkt[u1]
                dqk_ref[sl, blk] = (dqd_b * (eq[:, blk] * GLA_DK ** -0.5)).astype(dqk_ref.dtype)
                dqk_ref[sl, 256 + LANES * pair:256 + LANES * (pair + 1)] = (dkd_b * ek[:, blk] + dkt_b * f[:, blk]).astype(dqk_ref.dtype)
                dkt_kt = dkt_b * kt[:, blk]
                dbp = dqd_b * qd[:, blk] - dkd_b * kd[:, blk] - dkt_kt
                dbl = [jnp.sum(dkt_kt[rs], axis=0, keepdims=True) + (ddec[u0][c] + ddec[u1][c]) * decs[i][c][:, blk]
                       for c, rs in enumerate(chunks)]
                parts.append(jnp.where(last_row, dbp + _per_chunk_rows(dbl, LANES), dbp))
            db.append(jnp.concatenate(parts, axis=1))
        dla = [_tri_matmul(triu_b, db[i]) for i in range(nsub)]
        dz32 = [dla[i] * (1.0 / GLA_TAU) * _sigmoid(-com[i][0]) for i in range(nsub)]
        dz = [t.astype(BF) for t in dz32]
        for i, sl in enumerate(subs):
            dga_ref[sl, :] = _dot(dz[i], wd, NT).astype(dga_ref.dtype)
            dwd_ref[...] += _dot(ga[i].astype(BF), dz[i], TN)
            dbd_ref[...] += jnp.sum(dz32[i], axis=0, keepdims=True)

    def col(width, off):
        return pl.BlockSpec((tb, width), lambda i: (nb - 1 - i, off // width))

    def rev(width):
        return pl.BlockSpec((tb, width), lambda i: (nb - 1 - i, 0))

    return pl.pallas_call(
        body, name="gla_bwd", grid=(nb,),
        in_specs=[col(512, OFF_QK), col(512, OFF_V), col(512, OFF_GZ), col(LANES, OFF_GA), rev(512), rev(512),
                  pl.BlockSpec((tb // GLA_CHUNK, GLA_HEADS, LANES, LANES), lambda i: (nb - 1 - i, 0, 0, 0)),
                  _full((LANES, 256)), _full((1, 256)), _full((1, 512)), _full((sub, sub)), _full((sub, sub))],
        out_specs=[rev(512), rev(512), rev(512), rev(LANES), _full((LANES, 256)), _full((1, 256)), _full((1, 512))],
        out_shape=[jax.ShapeDtypeStruct((s, 512), BF), jax.ShapeDtypeStruct((s, 512), BF),
                   jax.ShapeDtypeStruct((s, 512), BF), jax.ShapeDtypeStruct((s, LANES), BF),
                   jax.ShapeDtypeStruct((LANES, 256), F32), jax.ShapeDtypeStruct((1, 256), F32),
                   jax.ShapeDtypeStruct((1, 512), F32)],
        scratch_shapes=[pltpu.VMEM((GLA_HEADS, LANES, LANES), F32)],
        compiler_params=_params(("arbitrary",)),
    )(proj, proj, proj, proj, dog, opre, sprev, wdecp, bdec, ggla, _gla_triangle(sub), _gla_triangle(sub).T)


_SWA_COL_HEADS = (0, 2, 1, 3, 4, 6, 5, 7)
_SWA_COLS = SWA_HEADS * SWA_BLOCK


def _swa_masks():
    lo2 = lax.broadcasted_iota(jnp.int32, (2 * SWA_BLOCK, LANES), 1) < 64
    lane1 = lax.broadcasted_iota(jnp.int32, (SWA_BLOCK, LANES), 1)
    first_half = (lane1 % 64) < 32
    key = lax.broadcasted_iota(jnp.int32, (SWA_BLOCK, _SWA_COLS), 0)
    query = lax.broadcasted_iota(jnp.int32, (SWA_BLOCK, _SWA_COLS), 1) % SWA_BLOCK
    return lo2, lane1 < 64, first_half, key > query


def _merge_band(t, prev_mask, prev_bias=None):
    prev = t[:SWA_BLOCK] if prev_bias is None else t[:SWA_BLOCK] + prev_bias
    return jnp.where(prev_mask, prev, t[SWA_BLOCK:])


def _split_band(t, prev_mask_b):
    prev = t * prev_mask_b
    return jnp.concatenate([prev, t - prev], axis=0)


def _kv_variants(t, lo2):
    tr = pltpu.roll(t, 64, 1)
    lo_v = [jnp.where(lo2, t, 0.0).astype(BF), jnp.where(lo2, tr, 0.0).astype(BF)]
    hi_v = [jnp.where(lo2, 0.0, tr).astype(BF), jnp.where(lo2, 0.0, t).astype(BF)]
    return lo_v, hi_v


def _kv_variants_t(t):
    tt = t.T
    sw = jnp.concatenate([tt[64:], tt[:64]], axis=0)
    top = lax.broadcasted_iota(jnp.int32, tt.shape, 0) < 64
    lo_v = [jnp.where(top, tt, 0.0).astype(BF), jnp.where(top, sw, 0.0).astype(BF)]
    hi_v = [jnp.where(top, 0.0, sw).astype(BF), jnp.where(top, 0.0, tt).astype(BF)]
    return lo_v, hi_v


def _swa_scores(qg, k_lo, k_hi):
    return jnp.concatenate([_dot(k_lo[0], qg[0], NT), _dot(k_hi[0], qg[0], NT),
                            _dot(k_lo[1], qg[1], NT), _dot(k_hi[1], qg[1], NT)], axis=1)


def _sink_row(sinks_ref):
    return jnp.concatenate([jnp.full((1, SWA_BLOCK), sinks_ref[0, hd], F32) for hd in _SWA_COL_HEADS], axis=1)


def _swa_softmax(st, prev_mask, prev_bias, sink):
    st = _merge_band(st, prev_mask, prev_bias)
    m = jnp.maximum(jnp.max(st, axis=0, keepdims=True), sink)
    ex = jnp.exp(st - m)
    es = jnp.exp(sink - m)
    inv = 1.0 / (jnp.sum(ex, axis=0, keepdims=True) + es)
    return ex, es, inv


def _no_prev_bias(block_index):
    return jnp.where(block_index > 0, 0.0, -1e30).astype(F32)


def _swa_queries(sq_ref, rows, cosb, sinb, first_half):
    qs = [_rope(sq_ref[rows, p * LANES:(p + 1) * LANES], cosb, sinb, first_half) * 0.125 for p in range(4)]
    return [jnp.concatenate(qs[0:2], axis=0), jnp.concatenate(qs[2:4], axis=0)]


def _phase_steps(nsteps, phases):
    return [min(nsteps - 1, (k * nsteps) // phases) for k in range(phases - 1)] + [nsteps - 1]


def _swa_fwd(proj, cos, sin, sinks, w_out):
    s = proj.shape[0]
    nq = min(SWA_QBLOCKS_FWD, s // SWA_BLOCK)
    tq = nq * SWA_BLOCK
    steps = _phase_steps(s // tq, 4)
    half_rows = w_out.shape[0] // 2

    def body(sq_ref, sz_ref, sk_ref, sv_ref, cos_ref, sin_ref, sinks_ref, wshard_ref, os_ref, opre_ref, wout_hbm,
             kprev, vprev, half_ref, *gather_sems):
        n = pl.program_id(0)

        @pl.when(n == 0)
        def _():
            kprev[...] = jnp.zeros_like(kprev)
            vprev[...] = jnp.zeros_like(vprev)
            mine = pl.ds(pl.multiple_of(lax.axis_index("c") * half_rows, half_rows), half_rows)
            half_ref[...] = wshard_ref[mine, :].astype(half_ref.dtype)

        gather = _Gather(half_ref, wout_hbm, *gather_sems, chunks=WEIGHT_CHUNKS)
        for step, phase in zip(steps, (gather.start, gather.pass_on, gather.relay_diagonal, gather.finish)):
            pl.when(n == step)(phase)

        lo2, _, first_half, prev_mask = _swa_masks()
        prev_mask_b = jnp.where(prev_mask, 1.0, 0.0).astype(BF)
        sink = _sink_row(sinks_ref)
        blocks = range(nq)
        rows = [slice(j * SWA_BLOCK, (j + 1) * SWA_BLOCK) for j in blocks]
        cosb = [cos_ref[rows[j], :] for j in blocks]
        sinb = [sin_ref[rows[j], :] for j in blocks]
        kc = [_rope(sk_ref[rows[j], :], cosb[j], sinb[j], first_half) for j in blocks]
        vc = [sv_ref[rows[j], :] for j in blocks]
        kcat = [jnp.concatenate([kprev[...] if j == 0 else kc[j - 1], kc[j]], axis=0) for j in blocks]
        vcat = [jnp.concatenate([vprev[...] if j == 0 else vc[j - 1], vc[j]], axis=0) for j in blocks]
        kprev[...] = kc[-1]
        vprev[...] = vc[-1]
        kvar = [_kv_variants(kcat[j], lo2) for j in blocks]
        vtvar = [_kv_variants_t(vcat[j]) for j in blocks]
        qg = [[q.astype(BF) for q in _swa_queries(sq_ref, rows[j], cosb[j], sinb[j], first_half)] for j in blocks]
        st = [_swa_scores(qg[j], *kvar[j]) for j in blocks]
        soft = [_swa_softmax(st[j], prev_mask, _no_prev_bias(n) if j == 0 else None, sink) for j in blocks]
        pt = [_split_band(soft[j][0].astype(BF), prev_mask_b) for j in blocks]
        og = {}
        for j in blocks:
            inv = soft[j][2]
            for g in range(2):
                c0, c1, c2 = 512 * g, 512 * g + 256, 512 * g + 512
                ot = (_dot(vtvar[j][0][g], pt[j][:, c0:c1]) * inv[:, c0:c1]
                      + _dot(vtvar[j][1][g], pt[j][:, c1:c2]) * inv[:, c1:c2])
                og[(j, g)] = ot.T
        for j in blocks:
            for g in range(2):
                for i in range(2):
                    ls = slice((2 * g + i) * LANES, (2 * g + i + 1) * LANES)
                    o = og[(j, g)][i * SWA_BLOCK:(i + 1) * SWA_BLOCK]
                    sz = sz_ref[rows[j], ls]
                    opre_ref[rows[j], ls] = o
                    os_ref[rows[j], ls] = (o * (sz * _sigmoid(sz))).astype(os_ref.dtype)

    def col(width, off):
        return pl.BlockSpec((tq, width), lambda i: (i, off // width))

    row = pl.BlockSpec((tq, LANES), lambda i: (i, 0))
    return pl.pallas_call(
        body, name="swa_fwd", grid=(s // tq,),
        in_specs=[col(512, OFF_SQ), col(512, OFF_SZ), col(LANES, OFF_SK), col(LANES, OFF_SV), row, row,
                  pl.BlockSpec(memory_space=pltpu.SMEM), _full(w_out.shape)],
        out_specs=[pl.BlockSpec((tq, 512), lambda i: (i, 0))] * 2 + [pl.BlockSpec(memory_space=pl.ANY)],
        out_shape=[jax.ShapeDtypeStruct((s, 512), BF), jax.ShapeDtypeStruct((s, 512), F32),
                   jax.ShapeDtypeStruct((8, half_rows, w_out.shape[1]), BF)],
        scratch_shapes=[pltpu.VMEM((SWA_BLOCK, LANES), F32)] * 2 + [pltpu.VMEM((half_rows, w_out.shape[1]), BF)]
        + _gather_sems(WEIGHT_CHUNKS),
        compiler_params=_params(("arbitrary",)),
    )(proj, proj, proj, proj, cos, sin, sinks, w_out)


def _swa_bwd(proj, dos, opre, cos, sin, sinks, dw_out_parts):
    s = proj.shape[0]
    nq = min(SWA_QBLOCKS, s // SWA_BLOCK)
    tq = nq * SWA_BLOCK
    steps = _phase_steps(s // tq, 5)
    _, r_out, c_out = dw_out_parts.shape

    def body(sq_ref, sz_ref, sk_ref, sv_ref, dos_ref, opre_ref, cos_ref, sin_ref, sinks_ref, pout_hbm,
             dsq_ref, dsz_ref, dsk_ref, dsv_ref, dsink_ref, gout_hbm, kprev, vprev, cprev, sprev, *reduce_scratch):
        n = pl.program_id(0)

        @pl.when(n == 0)
        def _():
            kprev[...] = jnp.zeros_like(kprev)
            vprev[...] = jnp.zeros_like(vprev)
            cprev[...] = jnp.zeros_like(cprev)
            sprev[...] = jnp.zeros_like(sprev)
            for hd in range(SWA_HEADS):
                dsink_ref[0, hd] = 0.0

        reduce = _Reduce(pout_hbm, gout_hbm, *reduce_scratch)
        phases = (reduce.start, reduce.combine_and_send, reduce.send_joint, reduce.total_and_share, reduce.finish)
        for step, phase in zip(steps, phases):
            pl.when(n == step)(phase)

        lo2, lo1, first_half, prev_mask = _swa_masks()
        prev_mask_b = jnp.where(prev_mask, 1.0, 0.0).astype(BF)
        lo1s = jnp.concatenate([lo1, lo1], axis=0)
        sink = _sink_row(sinks_ref)

        def home(m0, m1):
            t0 = m0 + pltpu.roll(m0, 64, 1)
            t1 = m1 + pltpu.roll(m1, 64, 1)
            return jnp.where(lo2, t0, t1)

        kp, vp, cp_, sp_ = kprev[...], vprev[...], cprev[...], sprev[...]
        for j in range(nq):
            rows = slice(j * SWA_BLOCK, (j + 1) * SWA_BLOCK)
            blk = n * nq + j
            cosb, sinb = cos_ref[rows, :], sin_ref[rows, :]
            kc = _rope(sk_ref[rows, :], cosb, sinb, first_half)
            vc = sv_ref[rows, :]
            kcat = jnp.concatenate([kp, kc], axis=0)
            k_lo, k_hi = _kv_variants(kcat, lo2)
            kt_lo, kt_hi = _kv_variants_t(kcat)
            v_lo, v_hi = _kv_variants(jnp.concatenate([vp, vc], axis=0), lo2)
            qg32 = _swa_queries(sq_ref, rows, cosb, sinb, first_half)
            qg = [q.astype(BF) for q in qg32]
            ex, es, inv = _swa_softmax(_swa_scores(qg, k_lo, k_hi), prev_mask, _no_prev_bias(n) if j == 0 else None, sink)
            pr, ps = ex * inv, es * inv

            dog32 = []
            for g in range(2):
                parts = []
                for i in range(2):
                    ls = slice((2 * g + i) * LANES, (2 * g + i + 1) * LANES)
                    sz = sz_ref[rows, ls]
                    sg = _sigmoid(sz)
                    dos_p = dos_ref[rows, ls]
                    dsz_ref[rows, ls] = (dos_p * opre_ref[rows, ls] * (sg * (1.0 + sz * (1.0 - sg)))).astype(dsz_ref.dtype)
                    parts.append(dos_p * (sz * sg))
                dog32.append(jnp.concatenate(parts, axis=0))
            dog = [t.astype(BF) for t in dog32]
            dpr = _merge_band(jnp.concatenate([_dot(v_lo[0], dog[0], NT), _dot(v_hi[0], dog[0], NT),
                                               _dot(v_lo[1], dog[1], NT), _dot(v_hi[1], dog[1], NT)], axis=1), prev_mask)
            rd = jnp.sum(pr * dpr, axis=0, keepdims=True)
            ds = _split_band((pr * (dpr - rd)).astype(BF), prev_mask_b)
            prb = _split_band(pr.astype(BF), prev_mask_b)
            sink_term = ps * rd
            for r, hd in enumerate(_SWA_COL_HEADS):
                dsink_ref[0, hd] += -jnp.sum(sink_term[:, r * SWA_BLOCK:(r + 1) * SWA_BLOCK])

            dk_g, dv_g = [], []
            for g in range(2):
                c0, c1, c2 = 512 * g, 512 * g + 256, 512 * g + 512
                dq = (_dot(kt_lo[g], ds[:, c0:c1]) + _dot(kt_hi[g], ds[:, c1:c2])).T
                for i in range(2):
                    ls = slice((2 * g + i) * LANES, (2 * g + i + 1) * LANES)
                    dsq_ref[rows, ls] = _rope_t(dq[i * SWA_BLOCK:(i + 1) * SWA_BLOCK] * 0.125, cosb, sinb,
                                                first_half).astype(dsq_ref.dtype)
                q_split = jnp.concatenate([jnp.where(lo1s, qg32[g], 0.0), jnp.where(lo1s, 0.0, qg32[g])], axis=0).astype(BF)
                do_split = jnp.concatenate([jnp.where(lo1s, dog32[g], 0.0), jnp.where(lo1s, 0.0, dog32[g])], axis=0).astype(BF)
                dk_g.append(_dot(ds[:, c0:c2], q_split))
                dv_g.append(_dot(prb[:, c0:c2], do_split))
            dk = home(dk_g[0], dk_g[1])
            dv = home(dv_g[0], dv_g[1])
            cur = pl.ds(pl.multiple_of(blk * SWA_BLOCK, SWA_BLOCK), SWA_BLOCK)
            dsk_ref[cur, :] = _rope_t(dk[SWA_BLOCK:], cosb, sinb, first_half)
            dsv_ref[cur, :] = dv[SWA_BLOCK:]
            dk_prev = _rope_t(dk[:SWA_BLOCK], cp_, sp_, first_half)
            dv_prev = dv[:SWA_BLOCK]
            if j == 0:
                @pl.when(n > 0)
                def _():
                    prv = pl.ds(pl.multiple_of((blk - 1) * SWA_BLOCK, SWA_BLOCK), SWA_BLOCK)
                    dsk_ref[prv, :] += dk_prev
                    dsv_ref[prv, :] += dv_prev
            else:
                prv = pl.ds(pl.multiple_of((blk - 1) * SWA_BLOCK, SWA_BLOCK), SWA_BLOCK)
                dsk_ref[prv, :] += dk_prev
                dsv_ref[prv, :] += dv_prev
            kp, vp, cp_, sp_ = kc, vc, cosb, sinb
        kprev[...] = kp
        vprev[...] = vp
        cprev[...] = cp_
        sprev[...] = sp_

    def col(width, off):
        return pl.BlockSpec((tq, width), lambda i: (i, off // width))

    row = pl.BlockSpec((tq, LANES), lambda i: (i, 0))
    wide = pl.BlockSpec((tq, 512), lambda i: (i, 0))
    return pl.pallas_call(
        body, name="swa_bwd", grid=(s // tq,),
        in_specs=[col(512, OFF_SQ), col(512, OFF_SZ), col(LANES, OFF_SK), col(LANES, OFF_SV), wide, wide, row, row,
                  pl.BlockSpec(memory_space=pltpu.SMEM), pl.BlockSpec(memory_space=pl.ANY)],
        out_specs=[wide, wide, _full((s, LANES)), _full((s, LANES)), pl.BlockSpec(memory_space=pltpu.SMEM),
                   pl.BlockSpec(memory_space=pl.ANY)],
        out_shape=[jax.ShapeDtypeStruct((s, 512), BF), jax.ShapeDtypeStruct((s, 512), BF),
                   jax.ShapeDtypeStruct((s, LANES), F32), jax.ShapeDtypeStruct((s, LANES), F32),
                   jax.ShapeDtypeStruct((1, SWA_HEADS), F32), jax.ShapeDtypeStruct((r_out, c_out), F32)],
        scratch_shapes=[pltpu.VMEM((SWA_BLOCK, LANES), F32)] * 4 + _reduce_scratch(r_out, c_out),
        compiler_params=_params(("arbitrary",)),
    )(proj, proj, proj, proj, dos, opre, cos, sin, sinks, dw_out_parts)


RING_SLOTS = 3


class _Ring:
    def __init__(self, hbm, buf, sems, nsteps):
        self.hbm, self.buf, self.sems, self.nsteps, self.tm = hbm, buf, sems, nsteps, buf.shape[1]

    def _fetch(self, step):
        static = isinstance(step, int)
        slot = step % RING_SLOTS if static else lax.rem(step, RING_SLOTS)
        rows = pl.ds(step * self.tm if static else pl.multiple_of(step * self.tm, self.tm), self.tm)
        return pltpu.make_async_copy(self.hbm.at[rows, :], self.buf.at[slot], self.sems.at[slot])

    def prefill(self):
        for step in range(min(RING_SLOTS - 1, self.nsteps)):
            self._fetch(step).start()

    def tile(self, i):
        ahead = i + (RING_SLOTS - 1)
        pl.when(ahead < self.nsteps)(lambda: self._fetch(ahead).start())
        self._fetch(i).wait()
        return lax.rem(i, RING_SLOTS)


def _outproj(og, osw, w_out, x2d, target, gate, g_final):
    s = x2d.shape[0]
    tm = min(512, s)
    nsteps = s // tm

    def body(og_ref, os_ref, w_ref, x_hbm, t_hbm, gate_ref, gf_ref,
             dx2_ref, dog_ref, dos_ref, dw_ref, loss_ref, dgf_ref, dgate_ref, x_buf, t_buf, x_sems, t_sems):
        i = pl.program_id(0)
        x_ring, t_ring = _Ring(x_hbm, x_buf, x_sems, nsteps), _Ring(t_hbm, t_buf, t_sems, nsteps)

        @pl.when(i == 0)
        def _():
            x_ring.prefill()
            t_ring.prefill()
            dw_ref[...] = jnp.zeros_like(dw_ref)
            loss_ref[...] = jnp.zeros_like(loss_ref)
            dgf_ref[...] = jnp.zeros_like(dgf_ref)
            dgate_ref[...] = jnp.zeros_like(dgate_ref)

        x_ref, t_ref = x_buf.at[x_ring.tile(i)], t_buf.at[t_ring.tile(i)]
        w = w_ref[...]
        gate, gf = gate_ref[...], gf_ref[...]
        subs = _subtiles(tm)
        ogv = [og_ref[sl, :] for sl in subs]
        osv = [os_ref[sl, :] for sl in subs]
        y = [_dot(ogv[k], w[:512]) + _dot(osv[k], w[512:]) for k in range(len(subs))]
        dys = []
        for k, sl in enumerate(subs):
            x2 = x_ref[sl, :] + gate * y[k]
            r = lax.rsqrt(jnp.mean(x2 * x2, axis=-1, keepdims=True) + RMS_EPS)
            xn = x2 * r
            err = xn * gf - t_ref[sl, :]
            loss_ref[...] += 0.5 * jnp.sum(jnp.mean(err * err, axis=-1, keepdims=True), axis=0, keepdims=True)
            dyf = err * (1.0 / D_MODEL)
            dgf_ref[...] += jnp.sum(dyf * xn, axis=0, keepdims=True)
            t = dyf * gf
            dx2 = r * (t - xn * jnp.mean(t * xn, axis=-1, keepdims=True))
            dx2_ref[sl, :] = dx2
            dgate_ref[...] += jnp.sum(dx2 * y[k], axis=0, keepdims=True)
            dys.append((dx2 * gate).astype(BF))
            dmix = _dot(dys[k], w, NT)
            dog_ref[sl, :] = dmix[:, :512]
            dos_ref[sl, :] = dmix[:, 512:]
        dy = jnp.concatenate(dys, axis=0)
        dw_ref[:512, :] += _dot(og_ref[...], dy, TN)
        dw_ref[512:, :] += _dot(os_ref[...], dy, TN)

    half = pl.BlockSpec((tm, 512), lambda i: (i, 0))
    rowb = pl.BlockSpec((tm, D_MODEL), lambda i: (i, 0))
    vec = _full((1, D_MODEL))
    anyspec = pl.BlockSpec(memory_space=pl.ANY)
    return pl.pallas_call(
        body, name="outproj", grid=(nsteps,),
        in_specs=[half, half, _full((D_MODEL, D_MODEL)), anyspec, anyspec, vec, vec],
        out_specs=[rowb, half, half, _full((D_MODEL, D_MODEL)), _full((1, 1)), vec, vec],
        out_shape=[jax.ShapeDtypeStruct((s, D_MODEL), F32), jax.ShapeDtypeStruct((s, 512), F32),
                   jax.ShapeDtypeStruct((s, 512), F32), jax.ShapeDtypeStruct((D_MODEL, D_MODEL), F32),
                   jax.ShapeDtypeStruct((1, 1), F32), jax.ShapeDtypeStruct((1, D_MODEL), F32),
                   jax.ShapeDtypeStruct((1, D_MODEL), F32)],
        scratch_shapes=[pltpu.VMEM((RING_SLOTS, tm, D_MODEL), F32)] * 2 + [pltpu.SemaphoreType.DMA((RING_SLOTS,))] * 2,
        compiler_params=_params(("arbitrary",)),
    )(og, osw, w_out, x2d, target, gate, g_final)


_PIECES = ((OFF_QK, 512), (OFF_V, 512), (OFF_GZ, 512), (OFF_SQ, 512), (OFF_SZ, 512),
           (OFF_SK, LANES), (OFF_SV, LANES), (OFF_GA, LANES))

_UNPAD_ROWS = ((OFF_QK, 0, 1024),
               (OFF_GA, 1024, GLA_RANK),
               (OFF_GZ, 1040, 1024),
               (OFF_SK, 2064, 256),
               (OFF_SZ, 2320, 512))


def _inproj_bwd(x2d, shift, sc1p, g_norm, w_t, dx2, pieces):
    s = x2d.shape[0]
    tm = min(512, s)
    nsteps = s // tm

    def body(x_ref, sh_ref, sc_ref, g_ref, w_hbm, dx2_ref, *rest):
        piece_refs = rest[:len(_PIECES)]
        gx_ref, dw_hbm, dsh_ref, dsc_ref, dg_ref, w_vm, dw_vm, in_sems, out_sems = rest[len(_PIECES):]
        i = pl.program_id(0)

        @pl.when(i == 0)
        def _():
            loads = _load_w_padded(w_hbm, w_vm, in_sems)
            dw_vm[...] = jnp.zeros_like(dw_vm)
            dsh_ref[...] = jnp.zeros_like(dsh_ref)
            dsc_ref[...] = jnp.zeros_like(dsc_ref)
            dg_ref[...] = jnp.zeros_like(dg_ref)
            for cp in loads:
                cp.wait()

        g, sc1p_v, shift_v = g_ref[...], sc_ref[...], sh_ref[...]
        subs = _subtiles(tm)
        dhs = []
        for sl in subs:
            dh = None
            for (off, width), pr in zip(_PIECES, piece_refs):
                part = _dot(pr[sl, :].astype(BF), w_vm[off:off + width, :])
                dh = part if dh is None else dh + part
            dhs.append(dh)
        norm = [_modnorm(x_ref[sl, :], g, sc1p_v, shift_v) for sl in subs]
        hb = jnp.concatenate([h.astype(BF) for _, _, h in norm], axis=0)
        for (off, width), pr in zip(_PIECES, piece_refs):
            dw_vm[off:off + width, :] += _dot(pr[...].astype(BF), hb, TN)
        for sl, (xn, r, _), dh in zip(subs, norm, dhs):
            dsh_ref[...] += jnp.sum(dh, axis=0, keepdims=True)
            dsc_ref[...] += jnp.sum(dh * (xn * g), axis=0, keepdims=True)
            dg_ref[...] += jnp.sum(dh * xn * sc1p_v, axis=0, keepdims=True)
            dxn = dh * g * sc1p_v
            gx_ref[sl, :] = dx2_ref[sl, :] + r * (dxn - xn * jnp.mean(dxn * xn, axis=-1, keepdims=True))

        @pl.when(i == nsteps - 1)
        def _():
            copies = [pltpu.make_async_copy(dw_vm.at[src:src + n], dw_hbm.at[dst:dst + n], out_sems.at[k])
                      for k, (src, dst, n) in enumerate(_UNPAD_ROWS)]
            for cp in copies:
                cp.start()
            for cp in copies:
                cp.wait()

    rowb = pl.BlockSpec((tm, D_MODEL), lambda i: (i, 0))
    vec = _full((1, D_MODEL))
    anyspec = pl.BlockSpec(memory_space=pl.ANY)
    piece_specs = [pl.BlockSpec((tm, width), lambda i: (i, 0)) for _, width in _PIECES]
    return pl.pallas_call(
        body, name="inproj_bwd", grid=(nsteps,),
        in_specs=[rowb, vec, vec, vec, anyspec, rowb] + piece_specs,
        out_specs=[rowb, anyspec, vec, vec, vec],
        out_shape=[jax.ShapeDtypeStruct((s, D_MODEL), F32), jax.ShapeDtypeStruct((D_IN, D_MODEL), F32),
                   jax.ShapeDtypeStruct((1, D_MODEL), F32), jax.ShapeDtypeStruct((1, D_MODEL), F32),
                   jax.ShapeDtypeStruct((1, D_MODEL), F32)],
        scratch_shapes=[pltpu.VMEM((D_PAD, D_MODEL), BF), pltpu.VMEM((D_PAD, D_MODEL), F32),
                        pltpu.SemaphoreType.DMA((len(_UNPAD_ROWS),)), pltpu.SemaphoreType.DMA((len(_UNPAD_ROWS),))],
        compiler_params=_params(("arbitrary",)),
    )(x2d, shift, sc1p, g_norm, w_t, dx2, *pieces)


def _adam(w, g, m, v):
    m2 = ADAM_B1 * m + (1.0 - ADAM_B1) * g
    v2 = ADAM_B2 * v + (1.0 - ADAM_B2) * (g * g)
    m_hat = m2 / (1.0 - ADAM_B1 ** ADAM_STEP)
    v_hat = v2 / (1.0 - ADAM_B2 ** ADAM_STEP)
    delta = -ADAM_LR * (m_hat / (jnp.sqrt(v_hat) + ADAM_EPS) + ADAM_WD * w)
    return delta, m2, v2


def _adamw_t(w3, g_window, m3, v3, name):
    rr, _, cc = w3.shape
    parts = [slice(q * (cc // 4), (q + 1) * (cc // 4)) for q in range(4)]
    starts = sorted({(rr * j) % 8 for j in range(4)})

    def body(w_hbm, gw_hbm, m_hbm, v_hbm, d_hbm, m2_hbm, v2_hbm, g3_hbm,
             w_vm, m_vm, v_vm, gw_vm, d_vm, m2_vm, v2_vm, g_vm, in_sems, out_sems):
        start = lax.rem(rr * (2 * lax.axis_index("x") + lax.axis_index("y")), 8)
        ins = ((w_hbm, w_vm), (m_hbm, m_vm), (v_hbm, v_vm))
        outs = ((d_vm, d_hbm), (m2_vm, m2_hbm), (v2_vm, v2_hbm), (g_vm, g3_hbm))
        loads = [[pltpu.make_async_copy(src.at[:, 0, p], dst.at[:, p], in_sems.at[4 * q + k]) for k, (src, dst) in enumerate(ins)]
                 + [pltpu.make_async_copy(gw_hbm.at[:, p], gw_vm.at[:, p], in_sems.at[4 * q + 3])]
                 for q, p in enumerate(parts)]
        stores = [[pltpu.make_async_copy(src.at[:, p], dst.at[:, 0, p], out_sems.at[4 * q + k]) for k, (src, dst) in enumerate(outs)]
                  for q, p in enumerate(parts)]
        for group in loads:
            for cp in group:
                cp.start()
        for q, p in enumerate(parts):
            for cp in loads[q]:
                cp.wait()
            g = gw_vm[starts[0]:starts[0] + rr, p]
            for o in starts[1:]:
                g = jnp.where(start == o, gw_vm[o:o + rr, p], g)
            g_vm[:, p] = g
            d_vm[:, p], m2_vm[:, p], v2_vm[:, p] = _adam(w_vm[:, p], g, m_vm[:, p], v_vm[:, p])
            for cp in stores[q]:
                cp.start()
        for group in stores:
            for cp in group:
                cp.wait()

    hbm = pl.BlockSpec(memory_space=pl.ANY)
    return pl.pallas_call(
        body, name=name, grid=(1,), in_specs=[hbm] * 4,
        out_specs=[hbm] * 4, out_shape=[jax.ShapeDtypeStruct((rr, 1, cc), F32)] * 4,
        scratch_shapes=[pltpu.VMEM((rr, cc), F32)] * 3 + [pltpu.VMEM(g_window.shape, F32)] + [pltpu.VMEM((rr, cc), F32)] * 4
        + [pltpu.SemaphoreType.DMA((16,)), pltpu.SemaphoreType.DMA((16,))],
        compiler_params=_params(("arbitrary",)),
    )(w3, g_window, m3, v3)


def _small_update(parts, weights, moms, vels):
    n = len(weights)

    def body(*refs):
        p_refs, w_refs, m_refs, v_refs = refs[:n + 1], refs[n + 1:2 * n + 1], refs[2 * n + 1:3 * n + 1], refs[3 * n + 1:4 * n + 1]
        outs = refs[4 * n + 1:]
        for i in range(n):
            g = p_refs[i][0]
            for d in range(1, 8):
                g = g + p_refs[i][d]
            delta, m2, v2 = _adam(w_refs[i][...], g, m_refs[i][...], v_refs[i][...])
            outs[4 * i][...] = g
            outs[4 * i + 1][...] = delta
            outs[4 * i + 2][...] = m2
            outs[4 * i + 3][...] = v2
        tot = p_refs[n][0]
        for d in range(1, 8):
            tot = tot + p_refs[n][d]
        outs[4 * n][...] = tot

    out_shape = []
    for w in weights:
        out_shape += [jax.ShapeDtypeStruct(w.shape, F32)] * 4
    out_shape.append(jax.ShapeDtypeStruct(parts[n].shape[1:], F32))
    return pl.pallas_call(body, name="small_update", out_shape=out_shape, compiler_params=_params())(
        *parts, *weights, *moms, *vels)


def _rows8(a):
    flat = a.reshape(-1)
    rows = -(-flat.shape[0] // LANES)
    rows8 = -(-rows // 8) * 8
    flat = jnp.pad(flat, (0, rows8 * LANES - flat.shape[0]))
    return flat.reshape(rows8, LANES)


def kernel(x, c, positions, w_ada, b_ada, g_norm, w_in, w_decay, b_decay, g_gla_head, sinks, w_out, g_final, loss_target, m_w_ada, m_b_ada, m_g_norm, m_w_in, m_w_decay, m_b_decay, m_g_gla_head, m_sinks, m_w_out, m_g_final, v_w_ada, v_b_ada, v_g_norm, v_w_in, v_w_decay, v_b_decay, v_g_gla_head, v_sinks, v_w_out, v_g_final):
    ax, ay, ac = lax.axis_index("x"), lax.axis_index("y"), lax.axis_index("c")
    chip = 2 * ax + ay
    dev = 2 * chip + ac
    s = x.shape[1]
    x2d = x[0]
    target = loss_target[0]
    w_ada2, w_out2, w_dec2 = w_ada[0], w_out[0], w_decay[0]
    w_in_t = w_in[0].T
    ada_cols = w_ada2.shape[1]
    in_cols = w_in_t.shape[0]
    out_rows = w_out2.shape[0]
    half = D_MODEL // 2

    cw = jnp.concatenate([c.reshape(8, LANES), w_dec2.reshape(8, LANES)], axis=0)
    inv_freq = 1.0 / (ROPE_THETA ** (jnp.arange(0, 64, 2, dtype=F32) / 64))
    room = _w_window(in_cols) - in_cols
    win_window = lax.dynamic_slice(jnp.pad(w_in_t, ((room, room), (0, 0))), (room - (in_cols * chip) % W_TILE, ac * half),
                                   (_w_window(in_cols), half)).astype(BF)
    win_edges = jnp.stack([win_window[:W_TILE], win_window[-W_TILE:]])
    first, mod_all, w_t, cos, sin = _prologue(
        cw, w_ada2, b_ada.reshape(4, ada_cols), win_window, win_edges, in_cols, positions.reshape(s // LANES, LANES), jnp.tile(inv_freq, 4).reshape(1, LANES))

    first = first.reshape(8, 2, 8, LANES)
    c_all = first[:, 0].reshape(8, D_MODEL)
    w_dec_full = first[0::2, 1].reshape(4, GLA_RANK, 64).transpose(1, 0, 2).reshape(GLA_RANK, 256)
    mod = mod_all.reshape(4, 2, 8, ada_cols)[:, 0]
    mod = lax.dynamic_slice(mod, (0, dev, 0), (4, 1, ada_cols)).reshape(1, 4 * ada_cols)
    shift, sc1p, gate = mod[:, :D_MODEL], 1.0 + mod[:, D_MODEL:2 * D_MODEL], mod[:, 2 * D_MODEL:]
    wdecp = jnp.pad(w_dec_full, ((0, LANES - GLA_RANK), (0, 0))).astype(BF)

    proj = _inproj_fwd(x2d, shift, sc1p, g_norm, w_t)
    og, o_gla, sprev = _gla_fwd(proj, wdecp, b_decay, g_gla_head)
    osw, o_swa, w_out_all = _swa_fwd(proj, cos, sin, sinks, w_out2)
    w_out_all = w_out_all.reshape(D_MODEL, D_MODEL)
    dx2, dog, dos, dw_out, loss_p, dgf, dgate = _outproj(og, osw, w_out_all, x2d, target, gate, g_final.reshape(1, D_MODEL))
    dsq, dsz, dsk, dsv, dsinks, g_w_out = _swa_bwd(proj, dos, o_swa, cos, sin, sinks, dw_out.reshape(4, out_rows, D_MODEL))
    dqk, dv, dgz, dga, dwdp, dbd, dgg = _gla_bwd(proj, dog, o_gla, sprev, wdecp, b_decay, g_gla_head)
    pieces = (dqk, dv, dgz, dsq, dsz, dsk, dsv, dga)
    gx, dw_in_t, dshift, dscale, dgn = _inproj_bwd(x2d, shift, sc1p, g_norm, w_t, dx2, pieces)

    segs = [jnp.concatenate([dshift, dscale, dgate], axis=1), dgn, dgf, dwdp[:GLA_RANK], dbd, dgg, dsinks, loss_p]
    packed = [_rows8(a) for a in segs]
    offs = [0]
    for a in packed:
        offs.append(offs[-1] + a.shape[0])
    (g_window, small, g_w_ada, d_w_ada, nm_w_ada, nv_w_ada, d_w_out, nm_w_out, nv_w_out) = _epilogue(
        dw_in_t, jnp.concatenate(packed, axis=0), c_all, (w_ada2, m_w_ada[0], v_w_ada[0]),
        (w_out2, g_w_out, m_w_out[0], v_w_out[0]), offs[0])

    def seg(i, size):
        return small[:, offs[i]:offs[i + 1]].reshape(8, -1)[:, :size]

    dmod_all = seg(0, 3 * D_MODEL)
    dwd_all = lax.dynamic_slice(seg(3, GLA_RANK * 256).reshape(8, GLA_RANK, 256), (0, 0, chip * 64), (8, GLA_RANK, 64))
    parts = [dmod_all.reshape(8, 1, 3 * D_MODEL), seg(1, D_MODEL).reshape(8, 1, D_MODEL), dwd_all,
             seg(4, 256).reshape(8, 1, 256), seg(5, 512).reshape(8, 1, 512), seg(6, SWA_HEADS).reshape(8, 1, SWA_HEADS),
             seg(2, D_MODEL).reshape(8, 1, D_MODEL), seg(7, LANES).reshape(8, 1, LANES)]
    smalls = _small_update(
        parts,
        [b_ada, g_norm, w_dec2, b_decay, g_gla_head, sinks, g_final.reshape(1, D_MODEL)],
        [m_b_ada, m_g_norm, m_w_decay[0], m_b_decay, m_g_gla_head, m_sinks, m_g_final.reshape(1, D_MODEL)],
        [v_b_ada, v_g_norm, v_w_decay[0], v_b_decay, v_g_gla_head, v_sinks, v_g_final.reshape(1, D_MODEL)])
    (g_b_ada, d_b_ada, nm_b_ada, nv_b_ada, g_gn, d_gn, nm_gn, nv_gn, g_wd, d_wd, nm_wd, nv_wd,
     g_bd, d_bd, nm_bd, nv_bd, g_gg, d_gg, nm_gg, nv_gg, g_sk, d_sk, nm_sk, nv_sk,
     g_gf, d_gf, nm_gf, nv_gf, loss_row) = smalls
    loss = loss_row[0, 0]

    to3 = lambda a: jnp.transpose(a, (2, 0, 1))
    from3 = lambda a: jnp.transpose(a, (1, 2, 0))[0]
    d3, nm3, nv3, g3 = _adamw_t(to3(w_in), g_window, to3(m_w_in), to3(v_w_in), "adamw_w_in")
    g_w_in, d_w_in, nm_w_in, nv_w_in = from3(g3), from3(d3), from3(nm3), from3(nv3)

    flat = lambda a: a.reshape(D_MODEL)
    grads = [g_w_ada[None], g_b_ada, g_gn, g_w_in[None], g_wd[None], g_bd, g_gg, g_sk, g_w_out[None], flat(g_gf)]
    deltas = [d_w_ada[None], d_b_ada, d_gn, d_w_in[None], d_wd[None], d_bd, d_gg, d_sk, d_w_out[None], flat(d_gf)]
    new_m = [nm_w_ada[None], nm_b_ada, nm_gn, nm_w_in[None], nm_wd[None], nm_bd, nm_gg, nm_sk, nm_w_out[None], flat(nm_gf)]
    new_v = [nv_w_ada[None], nv_b_ada, nv_gn, nv_w_in[None], nv_wd[None], nv_bd, nv_gg, nv_sk, nv_w_out[None], flat(nv_gf)]
    return (loss, gx[None], *grads, *deltas, *new_m, *new_v)
```
